```python
import jax, jax.numpy as jnp
from jax import lax
import numpy as np

D_MODEL = 1024
BATCH = 16
SEQ = 2048
DEPTH = 2

A_GROUPS = 8
A_GROUP_DIM = D_MODEL // 16
A_DIM = A_GROUPS * A_GROUP_DIM
B_GROUPS = 8
B_GROUP_DIM = D_MODEL // 16
B_DIM = B_GROUPS * B_GROUP_DIM
MIX_DIM = A_DIM + B_DIM
IN_EVEN = 2 * A_DIM + 3 * B_DIM
A_CONV_WIDTH = 31
B_CONV_WIDTH = 3
CHUNK = 128
C_GROUPS = 8
C_GROUP_DIM = D_MODEL // 8
C_DIM = C_GROUPS * C_GROUP_DIM
D_FF = 4 * D_MODEL
N_EVEN = (DEPTH + 1) // 2
N_ODD = DEPTH // 2
RMS_EPS = 1e-6
LN_EPS = 1e-5

kernel_name = "hybrid_conformer_shortconv_gmlp_trunk"


def rms_norm(x, g):
    xf = x.astype(jnp.float32)
    y = xf * lax.rsqrt(jnp.mean(xf * xf, axis=-1, keepdims=True) + RMS_EPS)
    return (y * g.astype(jnp.float32)).astype(x.dtype)


def layer_norm(x, g, b):
    xf = x.astype(jnp.float32)
    mu = jnp.mean(xf, axis=-1, keepdims=True)
    xc = xf - mu
    var = jnp.mean(xc * xc, axis=-1, keepdims=True)
    y = xc * lax.rsqrt(var + LN_EPS) * g.astype(jnp.float32) + b.astype(jnp.float32)
    return y.astype(x.dtype)


def causal_depthwise_conv(x, w):
    k = w.shape[0]
    return lax.conv_general_dilated(
        x, w[:, None, :].astype(x.dtype), window_strides=(1,), padding=[(k - 1, 0)],
        dimension_numbers=("NWC", "WIO", "NWC"), feature_group_count=x.shape[-1])


def conv_mixers(h, w_in, conv_a_w, conv_a_b, ln_a_g, ln_a_b, conv_b_w, w_out):
    z = h @ w_in
    a_val, a_gate, b_gate, c_gate, b_val = jnp.split(
        z, [A_DIM, 2 * A_DIM, 2 * A_DIM + B_DIM, 2 * A_DIM + 2 * B_DIM], axis=-1)
    a = a_val * jax.nn.sigmoid(a_gate)
    a = causal_depthwise_conv(a, conv_a_w) + conv_a_b
    a = jax.nn.silu(layer_norm(a, ln_a_g, ln_a_b))
    bo = b_gate * causal_depthwise_conv(c_gate * b_val, conv_b_w)
    return jnp.concatenate([a, bo], axis=-1) @ w_out


def chunked_spatial_gating(h, w_in, b_in, ln_v_g, ln_v_b, w_s, b_s, w_out):
    z = jax.nn.gelu(h @ w_in + b_in, approximate=False)
    u, v = jnp.split(z, 2, axis=-1)
    v = layer_norm(v, ln_v_g, ln_v_b)
    bsz, s, _ = v.shape
    vc = v.reshape(bsz, s // CHUNK, CHUNK, C_GROUPS, C_GROUP_DIM)
    mask = jnp.tril(jnp.ones((CHUNK, CHUNK), dtype=bool))
    ws = jnp.where(mask[None], w_s, 0.0).astype(v.dtype)
    sv = jnp.einsum("gts,bnsgc->bntgc", ws, vc) + b_s.T[None, None, :, :, None].astype(v.dtype)
    y = u * sv.reshape(bsz, s, C_DIM)
    return y @ w_out


def squared_relu_mlp(h, w1, w2):
    a = jax.nn.relu(h @ w1)
    return (a * a) @ w2


def _fwd_setup_inputs(seed: int = 0) -> dict:
    key = jax.random.key(seed)
    ks = iter(jax.random.split(key, 32))

    def nrm(shape, scale):
        return jax.random.normal(next(ks), shape, jnp.float32) * scale

    def gain(shape):
        return 1.0 + nrm(shape, 0.02)

    d = D_MODEL
    return {
        "x": nrm((BATCH, SEQ, d), 1.0),
        "ev_norm_g": gain((N_EVEN, d)),
        "ev_w_in": nrm((N_EVEN, d, IN_EVEN), d ** -0.5),
        "ev_conv_a_w": nrm((N_EVEN, A_CONV_WIDTH, A_DIM), A_CONV_WIDTH ** -0.5),
        "ev_conv_a_b": nrm((N_EVEN, A_DIM), 0.02),
        "ev_ln_a_g": gain((N_EVEN, A_DIM)),
        "ev_ln_a_b": nrm((N_EVEN, A_DIM), 0.02),
        "ev_conv_b_w": nrm((N_EVEN, B_CONV_WIDTH, B_DIM), B_CONV_WIDTH ** -0.5),
        "ev_w_out": nrm((N_EVEN, MIX_DIM, d), MIX_DIM ** -0.5),
        "od_norm_g": gain((N_ODD, d)),
        "od_w_in": nrm((N_ODD, d, 2 * C_DIM), d ** -0.5),
        "od_b_in": nrm((N_ODD, 2 * C_DIM), 0.02),
        "od_ln_v_g": gain((N_ODD, C_DIM)),
        "od_ln_v_b": nrm((N_ODD, C_DIM), 0.02),
        "od_w_s": nrm((N_ODD, C_GROUPS, CHUNK, CHUNK), CHUNK ** -0.5),
        "od_b_s": gain((N_ODD, C_GROUPS, CHUNK)),
        "od_w_out": nrm((N_ODD, C_DIM, d), C_DIM ** -0.5),
        "mlp_norm_g": gain((DEPTH, d)),
        "mlp_w1": nrm((DEPTH, d, D_FF), d ** -0.5),
        "mlp_w2": nrm((DEPTH, D_FF, d), D_FF ** -0.5),
        "final_norm_g": gain((d,)),
    }


def _fwd_reference(x, ev_norm_g, ev_w_in, ev_conv_a_w, ev_conv_a_b, ev_ln_a_g, ev_ln_a_b,
              ev_conv_b_w, ev_w_out, od_norm_g, od_w_in, od_b_in, od_ln_v_g, od_ln_v_b,
              od_w_s, od_b_s, od_w_out, mlp_norm_g, mlp_w1, mlp_w2, final_norm_g):
    h = x
    for i in range(DEPTH):
        j = i // 2
        if i % 2 == 0:
            h = h + conv_mixers(rms_norm(h, ev_norm_g[j]), ev_w_in[j], ev_conv_a_w[j],
                                ev_conv_a_b[j], ev_ln_a_g[j], ev_ln_a_b[j],
                                ev_conv_b_w[j], ev_w_out[j])
        else:
            h = h + chunked_spatial_gating(rms_norm(h, od_norm_g[j]), od_w_in[j], od_b_in[j],
                                           od_ln_v_g[j], od_ln_v_b[j], od_w_s[j],
                                           od_b_s[j], od_w_out[j])
        h = h + squared_relu_mlp(rms_norm(h, mlp_norm_g[i]), mlp_w1[i], mlp_w2[i])
    return rms_norm(h, final_norm_g)


import jax as _jax
import jax.numpy as _jnp

TWIN_FORMAT = 'train_step'
FWD_PARAMS = ['x', 'ev_norm_g', 'ev_w_in', 'ev_conv_a_w', 'ev_conv_a_b', 'ev_ln_a_g', 'ev_ln_a_b', 'ev_conv_b_w', 'ev_w_out', 'od_norm_g', 'od_w_in', 'od_b_in', 'od_ln_v_g', 'od_ln_v_b', 'od_w_s', 'od_b_s', 'od_w_out', 'mlp_norm_g', 'mlp_w1', 'mlp_w2', 'final_norm_g']
TWIN_WEIGHTS = ['ev_norm_g', 'ev_w_in', 'ev_conv_a_w', 'ev_conv_a_b', 'ev_ln_a_g', 'ev_ln_a_b', 'ev_conv_b_w', 'ev_w_out', 'od_norm_g', 'od_w_in', 'od_b_in', 'od_ln_v_g', 'od_ln_v_b', 'od_w_s', 'od_b_s', 'od_w_out', 'mlp_norm_g', 'mlp_w1', 'mlp_w2', 'final_norm_g']
TWIN_DIFF_INPUT = 'x'
TWIN_INPUTS = ['x', 'ev_norm_g', 'ev_w_in', 'ev_conv_a_w', 'ev_conv_a_b', 'ev_ln_a_g', 'ev_ln_a_b', 'ev_conv_b_w', 'ev_w_out', 'od_norm_g', 'od_w_in', 'od_b_in', 'od_ln_v_g', 'od_ln_v_b', 'od_w_s', 'od_b_s', 'od_w_out', 'mlp_norm_g', 'mlp_w1', 'mlp_w2', 'final_norm_g', 'loss_target', 'm_ev_norm_g', 'm_ev_w_in', 'm_ev_conv_a_w', 'm_ev_conv_a_b', 'm_ev_ln_a_g', 'm_ev_ln_a_b', 'm_ev_conv_b_w', 'm_ev_w_out', 'm_od_norm_g', 'm_od_w_in', 'm_od_b_in', 'm_od_ln_v_g', 'm_od_ln_v_b', 'm_od_w_s', 'm_od_b_s', 'm_od_w_out', 'm_mlp_norm_g', 'm_mlp_w1', 'm_mlp_w2', 'm_final_norm_g', 'v_ev_norm_g', 'v_ev_w_in', 'v_ev_conv_a_w', 'v_ev_conv_a_b', 'v_ev_ln_a_g', 'v_ev_ln_a_b', 'v_ev_conv_b_w', 'v_ev_w_out', 'v_od_norm_g', 'v_od_w_in', 'v_od_b_in', 'v_od_ln_v_g', 'v_od_ln_v_b', 'v_od_w_s', 'v_od_b_s', 'v_od_w_out', 'v_mlp_norm_g', 'v_mlp_w1', 'v_mlp_w2', 'v_final_norm_g']
TWIN_OUTPUTS = ['loss', 'grad_x', 'grad_ev_norm_g', 'grad_ev_w_in', 'grad_ev_conv_a_w', 'grad_ev_conv_a_b', 'grad_ev_ln_a_g', 'grad_ev_ln_a_b', 'grad_ev_conv_b_w', 'grad_ev_w_out', 'grad_od_norm_g', 'grad_od_w_in', 'grad_od_b_in', 'grad_od_ln_v_g', 'grad_od_ln_v_b', 'grad_od_w_s', 'grad_od_b_s', 'grad_od_w_out', 'grad_mlp_norm_g', 'grad_mlp_w1', 'grad_mlp_w2', 'grad_final_norm_g', 'delta_ev_norm_g', 'delta_ev_w_in', 'delta_ev_conv_a_w', 'delta_ev_conv_a_b', 'delta_ev_ln_a_g', 'delta_ev_ln_a_b', 'delta_ev_conv_b_w', 'delta_ev_w_out', 'delta_od_norm_g', 'delta_od_w_in', 'delta_od_b_in', 'delta_od_ln_v_g', 'delta_od_ln_v_b', 'delta_od_w_s', 'delta_od_b_s', 'delta_od_w_out', 'delta_mlp_norm_g', 'delta_mlp_w1', 'delta_mlp_w2', 'delta_final_norm_g', 'new_m_ev_norm_g', 'new_m_ev_w_in', 'new_m_ev_conv_a_w', 'new_m_ev_conv_a_b', 'new_m_ev_ln_a_g', 'new_m_ev_ln_a_b', 'new_m_ev_conv_b_w', 'new_m_ev_w_out', 'new_m_od_norm_g', 'new_m_od_w_in', 'new_m_od_b_in', 'new_m_od_ln_v_g', 'new_m_od_ln_v_b', 'new_m_od_w_s', 'new_m_od_b_s', 'new_m_od_w_out', 'new_m_mlp_norm_g', 'new_m_mlp_w1', 'new_m_mlp_w2', 'new_m_final_norm_g', 'new_v_ev_norm_g', 'new_v_ev_w_in', 'new_v_ev_conv_a_w', 'new_v_ev_conv_a_b', 'new_v_ev_ln_a_g', 'new_v_ev_ln_a_b', 'new_v_ev_conv_b_w', 'new_v_ev_w_out', 'new_v_od_norm_g', 'new_v_od_w_in', 'new_v_od_b_in', 'new_v_od_ln_v_g', 'new_v_od_ln_v_b', 'new_v_od_w_s', 'new_v_od_b_s', 'new_v_od_w_out', 'new_v_mlp_norm_g', 'new_v_mlp_w1', 'new_v_mlp_w2', 'new_v_final_norm_g']
TWIN_LEAF_KINDS = {'loss': 'loss', 'grad_x': 'grad_x', 'grad_ev_norm_g': 'grad_w', 'grad_ev_w_in': 'grad_w', 'grad_ev_conv_a_w': 'grad_w', 'grad_ev_conv_a_b': 'grad_w', 'grad_ev_ln_a_g': 'grad_w', 'grad_ev_ln_a_b': 'grad_w', 'grad_ev_conv_b_w': 'grad_w', 'grad_ev_w_out': 'grad_w', 'grad_od_norm_g': 'grad_w', 'grad_od_w_in': 'grad_w', 'grad_od_b_in': 'grad_w', 'grad_od_ln_v_g': 'grad_w', 'grad_od_ln_v_b': 'grad_w', 'grad_od_w_s': 'grad_w', 'grad_od_b_s': 'grad_w', 'grad_od_w_out': 'grad_w', 'grad_mlp_norm_g': 'grad_w', 'grad_mlp_w1': 'grad_w', 'grad_mlp_w2': 'grad_w', 'grad_final_norm_g': 'grad_w', 'delta_ev_norm_g': 'delta_w', 'delta_ev_w_in': 'delta_w', 'delta_ev_conv_a_w': 'delta_w', 'delta_ev_conv_a_b': 'delta_w', 'delta_ev_ln_a_g': 'delta_w', 'delta_ev_ln_a_b': 'delta_w', 'delta_ev_conv_b_w': 'delta_w', 'delta_ev_w_out': 'delta_w', 'delta_od_norm_g': 'delta_w', 'delta_od_w_in': 'delta_w', 'delta_od_b_in': 'delta_w', 'delta_od_ln_v_g': 'delta_w', 'delta_od_ln_v_b': 'delta_w', 'delta_od_w_s': 'delta_w', 'delta_od_b_s': 'delta_w', 'delta_od_w_out': 'delta_w', 'delta_mlp_norm_g': 'delta_w', 'delta_mlp_w1': 'delta_w', 'delta_mlp_w2': 'delta_w', 'delta_final_norm_g': 'delta_w', 'new_m_ev_norm_g': 'new_m', 'new_m_ev_w_in': 'new_m', 'new_m_ev_conv_a_w': 'new_m', 'new_m_ev_conv_a_b': 'new_m', 'new_m_ev_ln_a_g': 'new_m', 'new_m_ev_ln_a_b': 'new_m', 'new_m_ev_conv_b_w': 'new_m', 'new_m_ev_w_out': 'new_m', 'new_m_od_norm_g': 'new_m', 'new_m_od_w_in': 'new_m', 'new_m_od_b_in': 'new_m', 'new_m_od_ln_v_g': 'new_m', 'new_m_od_ln_v_b': 'new_m', 'new_m_od_w_s': 'new_m', 'new_m_od_b_s': 'new_m', 'new_m_od_w_out': 'new_m', 'new_m_mlp_norm_g': 'new_m', 'new_m_mlp_w1': 'new_m', 'new_m_mlp_w2': 'new_m', 'new_m_final_norm_g': 'new_m', 'new_v_ev_norm_g': 'new_v', 'new_v_ev_w_in': 'new_v', 'new_v_ev_conv_a_w': 'new_v', 'new_v_ev_conv_a_b': 'new_v', 'new_v_ev_ln_a_g': 'new_v', 'new_v_ev_ln_a_b': 'new_v', 'new_v_ev_conv_b_w': 'new_v', 'new_v_ev_w_out': 'new_v', 'new_v_od_norm_g': 'new_v', 'new_v_od_w_in': 'new_v', 'new_v_od_b_in': 'new_v', 'new_v_od_ln_v_g': 'new_v', 'new_v_od_ln_v_b': 'new_v', 'new_v_od_w_s': 'new_v', 'new_v_od_b_s': 'new_v', 'new_v_od_w_out': 'new_v', 'new_v_mlp_norm_g': 'new_v', 'new_v_mlp_w1': 'new_v', 'new_v_mlp_w2': 'new_v', 'new_v_final_norm_g': 'new_v'}


def _forward(args):
    return _fwd_reference(*[args[k] for k in FWD_PARAMS])


def _output_shape():
    out = _jax.eval_shape(lambda: _forward(_fwd_setup_inputs(0)))
    return out.shape, out.dtype

N_MICROBATCH = 1
ADAM_LR = 0.001
ADAM_B1 = 0.9
ADAM_B2 = 0.999
ADAM_EPS = 1e-08
ADAM_WD = 0.01
ADAM_STEP = 10
PER_EXAMPLE_BATCH_AXIS = {'x': 0, 'loss_target': 0}
SHARED_INPUTS = []
_WEIGHT_DTYPES = {'ev_norm_g': _jnp.float32, 'ev_w_in': _jnp.float32, 'ev_conv_a_w': _jnp.float32, 'ev_conv_a_b': _jnp.float32, 'ev_ln_a_g': _jnp.float32, 'ev_ln_a_b': _jnp.float32, 'ev_conv_b_w': _jnp.float32, 'ev_w_out': _jnp.float32, 'od_norm_g': _jnp.float32, 'od_w_in': _jnp.float32, 'od_b_in': _jnp.float32, 'od_ln_v_g': _jnp.float32, 'od_ln_v_b': _jnp.float32, 'od_w_s': _jnp.float32, 'od_b_s': _jnp.float32, 'od_w_out': _jnp.float32, 'mlp_norm_g': _jnp.float32, 'mlp_w1': _jnp.float32, 'mlp_w2': _jnp.float32, 'final_norm_g': _jnp.float32}
MOMENT_SCALE = {'ev_norm_g': 2.176896e-01, 'ev_w_in': 1.331852e-01, 'ev_conv_a_w': 9.980075e-02, 'ev_conv_a_b': 2.202138e-01, 'ev_ln_a_g': 1.210284e-01, 'ev_ln_a_b': 1.351660e-01, 'ev_conv_b_w': 1.660617e-01, 'ev_w_out': 1.331298e-01, 'od_norm_g': 9.603014e-02, 'od_w_in': 6.690391e-02, 'od_b_in': 7.304222e-02, 'od_ln_v_g': 4.769631e-02, 'od_ln_v_b': 4.360570e-02, 'od_w_s': 4.419256e-02, 'od_b_s': 6.133714e-02, 'od_w_out': 8.343056e-02, 'mlp_norm_g': 1.390665e-01, 'mlp_w1': 6.784063e-02, 'mlp_w2': 1.277110e-01, 'final_norm_g': 3.247968e+01}


def _to_microbatches(a, axis):
    t = _jnp.moveaxis(a, axis, 0)
    t = t.reshape((N_MICROBATCH, t.shape[0] // N_MICROBATCH) + t.shape[1:])
    return _jnp.moveaxis(t, 1, axis + 1)


def setup_inputs(seed: int = 0) -> dict:
    inp = _fwd_setup_inputs(seed)
    key = _jax.random.fold_in(_jax.random.key(seed), 7919)
    shape, _ = _output_shape()
    out = dict(inp)
    out["loss_target"] = _jax.random.normal(_jax.random.fold_in(key, 0), shape, _jnp.float32)
    for i, name in enumerate(TWIN_WEIGHTS):
        w = inp[name].astype(_jnp.float32)
        if MOMENT_SCALE is None:
            s = _jnp.sqrt(_jnp.mean(_jnp.square(w)) + 1e-30)
        else:
            s = MOMENT_SCALE[name]
        km, kv = _jax.random.split(_jax.random.fold_in(key, i + 1))
        out[name] = w
        out["m_" + name] = s * _jax.random.normal(km, w.shape, _jnp.float32)
        out["v_" + name] = (s * s) * _jax.random.uniform(kv, w.shape, _jnp.float32, 0.5, 1.5)
    if N_MICROBATCH > 1:
        for name, axis in PER_EXAMPLE_BATCH_AXIS.items():
            out[name] = _to_microbatches(out[name], axis)
    return {'x': out['x'], 'ev_norm_g': out['ev_norm_g'], 'ev_w_in': out['ev_w_in'], 'ev_conv_a_w': out['ev_conv_a_w'], 'ev_conv_a_b': out['ev_conv_a_b'], 'ev_ln_a_g': out['ev_ln_a_g'], 'ev_ln_a_b': out['ev_ln_a_b'], 'ev_conv_b_w': out['ev_conv_b_w'], 'ev_w_out': out['ev_w_out'], 'od_norm_g': out['od_norm_g'], 'od_w_in': out['od_w_in'], 'od_b_in': out['od_b_in'], 'od_ln_v_g': out['od_ln_v_g'], 'od_ln_v_b': out['od_ln_v_b'], 'od_w_s': out['od_w_s'], 'od_b_s': out['od_b_s'], 'od_w_out': out['od_w_out'], 'mlp_norm_g': out['mlp_norm_g'], 'mlp_w1': out['mlp_w1'], 'mlp_w2': out['mlp_w2'], 'final_norm_g': out['final_norm_g'], 'loss_target': out['loss_target'], 'm_ev_norm_g': out['m_ev_norm_g'], 'm_ev_w_in': out['m_ev_w_in'], 'm_ev_conv_a_w': out['m_ev_conv_a_w'], 'm_ev_conv_a_b': out['m_ev_conv_a_b'], 'm_ev_ln_a_g': out['m_ev_ln_a_g'], 'm_ev_ln_a_b': out['m_ev_ln_a_b'], 'm_ev_conv_b_w': out['m_ev_conv_b_w'], 'm_ev_w_out': out['m_ev_w_out'], 'm_od_norm_g': out['m_od_norm_g'], 'm_od_w_in': out['m_od_w_in'], 'm_od_b_in': out['m_od_b_in'], 'm_od_ln_v_g': out['m_od_ln_v_g'], 'm_od_ln_v_b': out['m_od_ln_v_b'], 'm_od_w_s': out['m_od_w_s'], 'm_od_b_s': out['m_od_b_s'], 'm_od_w_out': out['m_od_w_out'], 'm_mlp_norm_g': out['m_mlp_norm_g'], 'm_mlp_w1': out['m_mlp_w1'], 'm_mlp_w2': out['m_mlp_w2'], 'm_final_norm_g': out['m_final_norm_g'], 'v_ev_norm_g': out['v_ev_norm_g'], 'v_ev_w_in': out['v_ev_w_in'], 'v_ev_conv_a_w': out['v_ev_conv_a_w'], 'v_ev_conv_a_b': out['v_ev_conv_a_b'], 'v_ev_ln_a_g': out['v_ev_ln_a_g'], 'v_ev_ln_a_b': out['v_ev_ln_a_b'], 'v_ev_conv_b_w': out['v_ev_conv_b_w'], 'v_ev_w_out': out['v_ev_w_out'], 'v_od_norm_g': out['v_od_norm_g'], 'v_od_w_in': out['v_od_w_in'], 'v_od_b_in': out['v_od_b_in'], 'v_od_ln_v_g': out['v_od_ln_v_g'], 'v_od_ln_v_b': out['v_od_ln_v_b'], 'v_od_w_s': out['v_od_w_s'], 'v_od_b_s': out['v_od_b_s'], 'v_od_w_out': out['v_od_w_out'], 'v_mlp_norm_g': out['v_mlp_norm_g'], 'v_mlp_w1': out['v_mlp_w1'], 'v_mlp_w2': out['v_mlp_w2'], 'v_final_norm_g': out['v_final_norm_g']}


def _loss(weights, diff, rest, loss_target):
    with _jax.named_scope("forward"):
        args = {**rest, TWIN_DIFF_INPUT: diff, **{k: w.astype(_WEIGHT_DTYPES[k]) for k, w in weights.items()}}
        y = _forward(args)
    with _jax.named_scope("loss_head"):
        err = _jnp.square(y.astype(_jnp.float32) - loss_target)
        return 0.5 * _jnp.sum(_jnp.mean(err, axis=-1)) if err.ndim else 0.5 * err


def _adamw(w, g, m, v):
    m = ADAM_B1 * m + (1.0 - ADAM_B1) * g
    v = ADAM_B2 * v + (1.0 - ADAM_B2) * _jnp.square(g)
    m_hat = m / (1.0 - ADAM_B1 ** ADAM_STEP)
    v_hat = v / (1.0 - ADAM_B2 ** ADAM_STEP)
    delta = -ADAM_LR * (m_hat / (_jnp.sqrt(v_hat) + ADAM_EPS) + ADAM_WD * w)
    return delta, m, v


def reference(x, ev_norm_g, ev_w_in, ev_conv_a_w, ev_conv_a_b, ev_ln_a_g, ev_ln_a_b, ev_conv_b_w, ev_w_out, od_norm_g, od_w_in, od_b_in, od_ln_v_g, od_ln_v_b, od_w_s, od_b_s, od_w_out, mlp_norm_g, mlp_w1, mlp_w2, final_norm_g, loss_target, m_ev_norm_g, m_ev_w_in, m_ev_conv_a_w, m_ev_conv_a_b, m_ev_ln_a_g, m_ev_ln_a_b, m_ev_conv_b_w, m_ev_w_out, m_od_norm_g, m_od_w_in, m_od_b_in, m_od_ln_v_g, m_od_ln_v_b, m_od_w_s, m_od_b_s, m_od_w_out, m_mlp_norm_g, m_mlp_w1, m_mlp_w2, m_final_norm_g, v_ev_norm_g, v_ev_w_in, v_ev_conv_a_w, v_ev_conv_a_b, v_ev_ln_a_g, v_ev_ln_a_b, v_ev_conv_b_w, v_ev_w_out, v_od_norm_g, v_od_w_in, v_od_b_in, v_od_ln_v_g, v_od_ln_v_b, v_od_w_s, v_od_b_s, v_od_w_out, v_mlp_norm_g, v_mlp_w1, v_mlp_w2, v_final_norm_g):
    given = dict(x=x, ev_norm_g=ev_norm_g, ev_w_in=ev_w_in, ev_conv_a_w=ev_conv_a_w, ev_conv_a_b=ev_conv_a_b, ev_ln_a_g=ev_ln_a_g, ev_ln_a_b=ev_ln_a_b, ev_conv_b_w=ev_conv_b_w, ev_w_out=ev_w_out, od_norm_g=od_norm_g, od_w_in=od_w_in, od_b_in=od_b_in, od_ln_v_g=od_ln_v_g, od_ln_v_b=od_ln_v_b, od_w_s=od_w_s, od_b_s=od_b_s, od_w_out=od_w_out, mlp_norm_g=mlp_norm_g, mlp_w1=mlp_w1, mlp_w2=mlp_w2, final_norm_g=final_norm_g, loss_target=loss_target, m_ev_norm_g=m_ev_norm_g, m_ev_w_in=m_ev_w_in, m_ev_conv_a_w=m_ev_conv_a_w, m_ev_conv_a_b=m_ev_conv_a_b, m_ev_ln_a_g=m_ev_ln_a_g, m_ev_ln_a_b=m_ev_ln_a_b, m_ev_conv_b_w=m_ev_conv_b_w, m_ev_w_out=m_ev_w_out, m_od_norm_g=m_od_norm_g, m_od_w_in=m_od_w_in, m_od_b_in=m_od_b_in, m_od_ln_v_g=m_od_ln_v_g, m_od_ln_v_b=m_od_ln_v_b, m_od_w_s=m_od_w_s, m_od_b_s=m_od_b_s, m_od_w_out=m_od_w_out, m_mlp_norm_g=m_mlp_norm_g, m_mlp_w1=m_mlp_w1, m_mlp_w2=m_mlp_w2, m_final_norm_g=m_final_norm_g, v_ev_norm_g=v_ev_norm_g, v_ev_w_in=v_ev_w_in, v_ev_conv_a_w=v_ev_conv_a_w, v_ev_conv_a_b=v_ev_conv_a_b, v_ev_ln_a_g=v_ev_ln_a_g, v_ev_ln_a_b=v_ev_ln_a_b, v_ev_conv_b_w=v_ev_conv_b_w, v_ev_w_out=v_ev_w_out, v_od_norm_g=v_od_norm_g, v_od_w_in=v_od_w_in, v_od_b_in=v_od_b_in, v_od_ln_v_g=v_od_ln_v_g, v_od_ln_v_b=v_od_ln_v_b, v_od_w_s=v_od_w_s, v_od_b_s=v_od_b_s, v_od_w_out=v_od_w_out, v_mlp_norm_g=v_mlp_norm_g, v_mlp_w1=v_mlp_w1, v_mlp_w2=v_mlp_w2, v_final_norm_g=v_final_norm_g)
    weights = {n: given[n] for n in TWIN_WEIGHTS}
    shared = {n: given[n] for n in SHARED_INPUTS}
    per_example = {n: given[n] for n in ['x']}
    grad_fn = _jax.value_and_grad(_loss, argnums=(0, 1))

    def one_microbatch(ex, loss_target):
        ex = dict(ex)
        diff = ex.pop(TWIN_DIFF_INPUT)
        return grad_fn(weights, diff, {**shared, **ex}, loss_target)

    if N_MICROBATCH == 1:
        loss, (grad_w, grad_x) = one_microbatch(per_example, given["loss_target"])
    else:
        def body(carry, xs):
            loss_sum, grad_sum = carry
            l_k, (gw_k, gx_k) = one_microbatch(xs[0], xs[1])
            with _jax.named_scope("update"):
                return (loss_sum + l_k, _jax.tree.map(_jnp.add, grad_sum, gw_k)), gx_k

        init = (_jnp.zeros((), _jnp.float32), _jax.tree.map(_jnp.zeros_like, weights))
        (loss, grad_w), grad_x = _jax.lax.scan(body, init, (per_example, given["loss_target"]))
    with _jax.named_scope("update"):
        delta_w, new_m, new_v = {}, {}, {}
        for n in TWIN_WEIGHTS:
            delta_w[n], new_m[n], new_v[n] = _adamw(weights[n], grad_w[n], given["m_" + n], given["v_" + n])
    return (loss, grad_x, *[grad_w[n] for n in TWIN_WEIGHTS], *[delta_w[n] for n in TWIN_WEIGHTS],
            *[new_m[n] for n in TWIN_WEIGHTS], *[new_v[n] for n in TWIN_WEIGHTS])
```

```python
import functools

import jax
import jax.numpy as jnp
from jax import lax
from jax.experimental import pallas as pl
from jax.experimental.pallas import tpu as pltpu

F32 = jnp.float32
BF16 = jnp.bfloat16

D_MODEL = 1024
SEQ = 2048
A_DIM = 512
B_DIM = 512
IN_EVEN = 2 * A_DIM + 3 * B_DIM
A_CONV_WIDTH = 31
B_CONV_WIDTH = 3
CHUNK = 128
C_GROUPS = 8
C_DIM = 1024
D_FF = 4096
RMS_EPS = 1e-6
LN_EPS = 1e-5
ADAM_LR = 0.001
ADAM_B1 = 0.9
ADAM_B2 = 0.999
ADAM_EPS = 1e-08
ADAM_WD = 0.01
ADAM_STEP = 10

N_CHIPS = 4
N_DEV = 8
TOKEN_TILE = 512
A_HALO = 32
B_HALO = 8
CONV_ROWS = 16
PAIR = 2 * CHUNK
LANES = 128
MIB = 1024 * 1024
MESH = pl.DeviceIdType.MESH


def _dot(a, b):
    return lax.dot_general(a, b, (((1,), (0,)), ((), ())), preferred_element_type=F32)


def _dot_nt(a, b):
    return lax.dot_general(a, b, (((1,), (1,)), ((), ())), preferred_element_type=F32)


def _dot_tn(a, b):
    return lax.dot_general(a, b, (((0,), (0,)), ((), ())), preferred_element_type=F32)


def _params(vmem_mib, semantics=("arbitrary",)):
    return pltpu.CompilerParams(dimension_semantics=semantics, vmem_limit_bytes=vmem_mib * MIB)


def _row_spec(tm, cols, rev_nt=None):
    if rev_nt is None:
        return pl.BlockSpec((tm, cols), lambda i: (i, 0))
    return pl.BlockSpec((tm, cols), lambda i: (rev_nt - 1 - i, 0))


def _full_spec(shape):
    nd = len(shape)
    return pl.BlockSpec(shape, lambda i: (0,) * nd)


ANY = pl.BlockSpec(memory_space=pl.ANY)


def _block_rows(rows, cap=512):
    best = 8
    for br in range(8, min(rows, cap) + 1, 8):
        if rows % br == 0:
            best = br
    return best


def _load(src, dst, sem):
    cp = pltpu.make_async_copy(src, dst, sem)
    cp.start()
    cp.wait()


def _rms_fwd(x, g):
    rstd = lax.rsqrt(jnp.mean(x * x, axis=-1, keepdims=True) + RMS_EPS)
    return x * rstd * g, rstd


def _rms_bwd(dn, x, rstd, g):
    a = dn * g
    xh = x * rstd
    dx = rstd * (a - xh * jnp.mean(a * xh, axis=-1, keepdims=True))
    dg = jnp.sum(dn * xh, axis=0, keepdims=True)
    return dx, dg


def _ln_stats(v):
    mu = jnp.mean(v, axis=-1, keepdims=True)
    xc = v - mu
    rs = lax.rsqrt(jnp.mean(xc * xc, axis=-1, keepdims=True) + LN_EPS)
    return xc * rs, rs


def _ln_bwd(dy, xhat, rs, g):
    dxh = dy * g
    dv = rs * (dxh - jnp.mean(dxh, axis=-1, keepdims=True) - xhat * jnp.mean(dxh * xhat, axis=-1, keepdims=True))
    return dv, jnp.sum(dy * xhat, axis=0, keepdims=True), jnp.sum(dy, axis=0, keepdims=True)


def _gelu_parts(s):
    cdf = 0.5 * (1.0 + lax.erf(s * 0.7071067811865476))
    return cdf, jnp.exp(-0.5 * s * s) * 0.3989422804014327


def _cast_bf16(x, name):
    rows, cols = x.shape
    br = min(rows, 512)

    def body(x_ref, o_ref):
        o_ref[...] = x_ref[...].astype(BF16)

    return pl.pallas_call(
        body, name=name, grid=(rows // br,),
        in_specs=[pl.BlockSpec((br, cols), lambda i: (i, 0))],
        out_specs=pl.BlockSpec((br, cols), lambda i: (i, 0)),
        out_shape=jax.ShapeDtypeStruct((rows, cols), BF16),
        compiler_params=_params(16, ("parallel",)),
    )(x)


def _mesh_pos():
    return lax.axis_index("x"), lax.axis_index("y"), lax.axis_index("c")


def _remote(src, dst, send_sem, recv_sem, to):
    return pltpu.make_async_remote_copy(src_ref=src, dst_ref=dst, send_sem=send_sem, recv_sem=recv_sem,
                                        device_id=to, device_id_type=MESH)


def _all_gather_shards(shards):
    n = len(shards)

    def body(*refs):
        ins, outs = refs[:n], refs[n:2 * n]
        send_sems, recv_sems, local_sems = refs[2 * n:]
        x, y, c = _mesh_pos()
        me, sibling = (x, y, c), (x, y, 1 - c)
        chips = [(1 - x, y), (x, 1 - y), (1 - x, 1 - y)]
        mine = 2 * x + y
        local, sent = [], []
        for t in range(n):
            cp = pltpu.make_async_copy(ins[t], outs[t].at[mine], local_sems.at[t])
            cp.start()
            local.append(cp)
            for k, chip in enumerate(chips):
                cp = _remote(ins[t].at[c], outs[t].at[mine, c], send_sems.at[t, k], recv_sems.at[t, k], (*chip, c))
                cp.start()
                sent.append(cp)
        for t in range(n):
            for k, chip in enumerate(chips):
                blk = outs[t].at[2 * chip[0] + chip[1], c]
                _remote(blk, blk, send_sems.at[t, k], recv_sems.at[t, k], me).wait_recv()
                cp = _remote(blk, blk, send_sems.at[t, 3 + k], recv_sems.at[t, 3 + k], sibling)
                cp.start()
                sent.append(cp)
        for t in range(n):
            for k, chip in enumerate(chips):
                blk = outs[t].at[2 * chip[0] + chip[1], 1 - c]
                _remote(blk, blk, send_sems.at[t, 3 + k], recv_sems.at[t, 3 + k], me).wait_recv()
        for cp in sent:
            cp.wait_send()
        for cp in local:
            cp.wait()

    return pl.pallas_call(
        body, name="all_gather_weights",
        in_specs=[ANY] * n, out_specs=[ANY] * n,
        out_shape=[jax.ShapeDtypeStruct((N_CHIPS,) + s.shape, s.dtype) for s in shards],
        scratch_shapes=[pltpu.SemaphoreType.DMA((n, 6)), pltpu.SemaphoreType.DMA((n, 6)), pltpu.SemaphoreType.DMA((n,))],
        compiler_params=pltpu.CompilerParams(has_side_effects=True),
    )(*shards)


def _pair_exchange_halves(grads):
    n = len(grads)

    def body(*refs):
        ins, outs = refs[:n], refs[n:2 * n]
        send_sems, recv_sems = refs[2 * n:]
        x, y, c = _mesh_pos()
        sibling = (x, y, 1 - c)
        sent = []
        for t in range(n):
            cp = _remote(ins[t].at[:, 1 - c], outs[t], send_sems.at[t], recv_sems.at[t], sibling)
            cp.start()
            sent.append(cp)
        for cp in sent:
            cp.wait()

    return pl.pallas_call(
        body, name="grad_pair_exchange",
        in_specs=[ANY] * n, out_specs=[ANY] * n,
        out_shape=[jax.ShapeDtypeStruct((g.shape[0],) + g.shape[2:], g.dtype) for g in grads],
        scratch_shapes=[pltpu.SemaphoreType.DMA((n,)), pltpu.SemaphoreType.DMA((n,))],
        compiler_params=pltpu.CompilerParams(has_side_effects=True),
    )(*grads)


def _chip_exchange(parts):
    n = len(parts)

    def body(*refs):
        ins, outs = refs[:n], refs[n:2 * n]
        send_sems, recv_sems = refs[2 * n:]
        x, y, c = _mesh_pos()
        chips = [(1 - x, y), (x, 1 - y), (1 - x, 1 - y)]
        sent = []
        for t in range(n):
            for k, chip in enumerate(chips):
                cp = _remote(ins[t].at[2 * chip[0] + chip[1]], outs[t].at[k], send_sems.at[t, k], recv_sems.at[t, k], (*chip, c))
                cp.start()
                sent.append(cp)
        for cp in sent:
            cp.wait()

    return pl.pallas_call(
        body, name="grad_chip_exchange",
        in_specs=[ANY] * n, out_specs=[ANY] * n,
        out_shape=[jax.ShapeDtypeStruct((3,) + p.shape[1:], p.dtype) for p in parts],
        scratch_shapes=[pltpu.SemaphoreType.DMA((n, 3)), pltpu.SemaphoreType.DMA((n, 3))],
        compiler_params=pltpu.CompilerParams(has_side_effects=True),
    )(*parts)


def _pair_assemble(halves):
    n = len(halves)

    def body(*refs):
        ins, outs = refs[:n], refs[n:2 * n]
        send_sems, recv_sems, local_sems = refs[2 * n:]
        x, y, c = _mesh_pos()
        sibling = (x, y, 1 - c)
        cps = []
        for t in range(n):
            lc = pltpu.make_async_copy(ins[t], outs[t].at[c], local_sems.at[t])
            lc.start()
            cp = _remote(ins[t], outs[t].at[c], send_sems.at[t], recv_sems.at[t], sibling)
            cp.start()
            cps.append((lc, cp))
        for lc, cp in cps:
            cp.wait()
            lc.wait()

    return pl.pallas_call(
        body, name="grad_pair_assemble",
        in_specs=[ANY] * n, out_specs=[ANY] * n,
        out_shape=[jax.ShapeDtypeStruct((2,) + h.shape, h.dtype) for h in halves],
        scratch_shapes=[pltpu.SemaphoreType.DMA((n,)), pltpu.SemaphoreType.DMA((n,)), pltpu.SemaphoreType.DMA((n,))],
        compiler_params=pltpu.CompilerParams(has_side_effects=True),
    )(*halves)


def _all_to_all_small(v):
    def body(v_ref, o_ref, send_sems, recv_sems, local_sem):
        x, y, c = _mesh_pos()
        mine = 4 * x + 2 * y + c
        lc = pltpu.make_async_copy(v_ref, o_ref.at[mine], local_sem)
        lc.start()
        sent = []
        for r in range(1, N_DEV):
            fx, fy, fc = (r >> 2) & 1, (r >> 1) & 1, r & 1
            peer = (x ^ fx, y ^ fy, c ^ fc)
            cp = _remote(v_ref, o_ref.at[mine], send_sems.at[r - 1], recv_sems.at[r - 1], peer)
            cp.start()
            sent.append(cp)
        for r in range(1, N_DEV):
            fx, fy, fc = (r >> 2) & 1, (r >> 1) & 1, r & 1
            src = 4 * (x ^ fx) + 2 * (y ^ fy) + (c ^ fc)
            blk = o_ref.at[src]
            _remote(blk, blk, send_sems.at[r - 1], recv_sems.at[r - 1], (x, y, c)).wait_recv()
        for cp in sent:
            cp.wait_send()
        lc.wait()

    return pl.pallas_call(
        body, name="small_grad_exchange",
        in_specs=[ANY], out_specs=ANY,
        out_shape=jax.ShapeDtypeStruct((N_DEV,) + v.shape, v.dtype),
        scratch_shapes=[pltpu.SemaphoreType.DMA((N_DEV - 1,)), pltpu.SemaphoreType.DMA((N_DEV - 1,)), pltpu.SemaphoreType.DMA],
        compiler_params=pltpu.CompilerParams(has_side_effects=True),
    )(v)


def _add_pair(g, recv, name):
    _, _, r, cdim = g.shape
    br = min(r, 256)
    c = lax.axis_index("c")

    def body(c_ref, g_ref, r_ref, o_ref):
        o_ref[...] = (g_ref[...] + r_ref[...]).astype(BF16)

    return pl.pallas_call(
        body, name=name,
        grid_spec=pltpu.PrefetchScalarGridSpec(
            num_scalar_prefetch=1, grid=(N_CHIPS, r // br),
            in_specs=[pl.BlockSpec((None, None, br, cdim), lambda q, i, c_ref: (q, c_ref[0], i, 0)),
                      pl.BlockSpec((None, br, cdim), lambda q, i, c_ref: (q, i, 0))],
            out_specs=pl.BlockSpec((None, br, cdim), lambda q, i, c_ref: (q, i, 0))),
        out_shape=jax.ShapeDtypeStruct((N_CHIPS, r, cdim), BF16),
        compiler_params=_params(16, ("parallel", "parallel")),
    )(jnp.reshape(c, (1,)).astype(jnp.int32), g, recv)


def _add_chips(own, recv, name):
    _, r, cdim = own.shape
    br = min(r, 256)
    mine = 2 * lax.axis_index("x") + lax.axis_index("y")

    def body(q_ref, o_ref_in, r_ref, o_ref):
        acc = o_ref_in[...].astype(F32)
        for k in range(3):
            acc = acc + r_ref[k].astype(F32)
        o_ref[...] = acc

    return pl.pallas_call(
        body, name=name,
        grid_spec=pltpu.PrefetchScalarGridSpec(
            num_scalar_prefetch=1, grid=(r // br,),
            in_specs=[pl.BlockSpec((None, br, cdim), lambda i, q_ref: (q_ref[0], i, 0)),
                      pl.BlockSpec((3, br, cdim), lambda i, q_ref: (0, i, 0))],
            out_specs=pl.BlockSpec((br, cdim), lambda i, q_ref: (i, 0))),
        out_shape=jax.ShapeDtypeStruct((r, cdim), F32),
        compiler_params=_params(16, ("parallel",)),
    )(jnp.reshape(mine, (1,)).astype(jnp.int32), own, recv)


def _sum_devices(v8):
    _, r, cdim = v8.shape

    def body(v_ref, o_ref):
        acc = v_ref[0]
        for d in range(1, N_DEV):
            acc = acc + v_ref[d]
        o_ref[...] = acc

    return pl.pallas_call(
        body, name="small_grad_sum",
        in_specs=[pl.BlockSpec(memory_space=pltpu.VMEM)], out_specs=pl.BlockSpec(memory_space=pltpu.VMEM),
        out_shape=jax.ShapeDtypeStruct((r, cdim), F32),
        compiler_params=pltpu.CompilerParams(vmem_limit_bytes=32 * MIB),
    )(v8)


def _adamw(w, m, v, grads, name):
    layers, r, cdim = w.shape
    br = _block_rows(r, 256 if cdim > LANES else 1024)
    c1 = 1.0 / (1.0 - ADAM_B1 ** ADAM_STEP)
    c2 = 1.0 / (1.0 - ADAM_B2 ** ADAM_STEP)

    def body(*refs):
        w_ref, m_ref, v_ref = refs[:3]
        g_refs = refs[3:3 + layers]
        go_ref, d_ref, mo_ref, vo_ref = refs[3 + layers:]
        layer = pl.program_id(0)
        for l in range(layers):
            @pl.when(layer == l)
            def _(l=l):
                g = g_refs[l][...]
                m_new = ADAM_B1 * m_ref[...] + (1.0 - ADAM_B1) * g
                v_new = ADAM_B2 * v_ref[...] + (1.0 - ADAM_B2) * (g * g)
                go_ref[...] = g
                mo_ref[...] = m_new
                vo_ref[...] = v_new
                d_ref[...] = -ADAM_LR * ((m_new * c1) / (jnp.sqrt(v_new * c2) + ADAM_EPS) + ADAM_WD * w_ref[...])

    spec3 = pl.BlockSpec((None, br, cdim), lambda l, i: (l, i, 0))
    spec2 = pl.BlockSpec((br, cdim), lambda l, i: (i, 0))
    out = jax.ShapeDtypeStruct((layers, r, cdim), F32)
    return pl.pallas_call(
        body, name=name, grid=(layers, r // br),
        in_specs=[spec3, spec3, spec3] + [spec2] * layers,
        out_specs=[spec3, spec3, spec3, spec3],
        out_shape=[out, out, out, out],
        compiler_params=_params(32, ("parallel", "parallel")),
    )(w, m, v, *grads)


def _conv31(src, w_ref, rows, base, init):
    acc = init
    for k in range(A_CONV_WIDTH):
        acc = acc + w_ref[k:k + 1, :] * src[base + k + rows.start:base + k + rows.stop, :]
    return acc


def _fwd_even(x, norm_g, w_in, conv_a_w, conv_a_b, ln_g, ln_b, conv_b_w, w_out, *, tm, seq):
    tokens = x.shape[0]
    nt, tps = tokens // tm, seq // tm

    def body(x_ref, g_ref, win_hbm, caw_ref, cab_ref, lng_ref, lnb_ref, cbw_ref, wout_hbm,
             h_ref, n_ref, z_ref, a2_ref, cv_ref, mix_ref, win_v, wout_v, pa, pb, sem):
        i = pl.program_id(0)

        @pl.when(i == 0)
        def _():
            _load(win_hbm, win_v, sem)
            _load(wout_hbm, wout_v, sem)

        xv = x_ref[...]
        nf, _ = _rms_fwd(xv, g_ref[...])
        n = nf.astype(BF16)
        n_ref[...] = n
        z = jnp.concatenate([_dot(n, win_v[j]) for j in range(N_CHIPS)], axis=1)
        z_ref[...] = z.astype(BF16)
        a_val, a_gate = z[:, 0:A_DIM], z[:, A_DIM:2 * A_DIM]
        b_gate, c_gate, b_val = z[:, 1024:1536], z[:, 1536:2048], z[:, 2048:2560]

        first = (i % tps) == 0

        @pl.when(first)
        def _():
            pa[0:A_HALO, :] = jnp.zeros((A_HALO, A_DIM), F32)
            pb[0:B_HALO, :] = jnp.zeros((B_HALO, B_DIM), F32)

        @pl.when(jnp.logical_not(first))
        def _():
            pa[0:A_HALO, :] = pa[tm:tm + A_HALO, :]
            pb[0:B_HALO, :] = pb[tm:tm + B_HALO, :]

        pa[A_HALO:A_HALO + tm, :] = a_val * jax.nn.sigmoid(a_gate)
        pb[B_HALO:B_HALO + tm, :] = c_gate * b_val
        bias = jnp.broadcast_to(cab_ref[...], (CONV_ROWS, A_DIM))
        for r0 in range(0, tm, CONV_ROWS):
            rows = slice(r0, r0 + CONV_ROWS)
            a2_ref[rows, :] = _conv31(pa, caw_ref, rows, A_HALO - (A_CONV_WIDTH - 1), bias)
        xhat, _ = _ln_stats(a2_ref[...])
        a3 = xhat * lng_ref[...] + lnb_ref[...]
        a4 = a3 * jax.nn.sigmoid(a3)
        cv = cbw_ref[0:1, :] * pb[B_HALO - 2:B_HALO - 2 + tm, :]
        cv = cv + cbw_ref[1:2, :] * pb[B_HALO - 1:B_HALO - 1 + tm, :]
        cv = cv + cbw_ref[2:3, :] * pb[B_HALO:B_HALO + tm, :]
        cv_ref[...] = cv.astype(BF16)
        mix = jnp.concatenate([a4, b_gate * cv], axis=1).astype(BF16)
        mix_ref[...] = mix
        h_ref[...] = xv + _dot(mix, wout_v[...])

    shp = lambda cols, dt: jax.ShapeDtypeStruct((tokens, cols), dt)
    return pl.pallas_call(
        body, name="fwd_even", grid=(nt,),
        in_specs=[_row_spec(tm, D_MODEL), _full_spec((1, D_MODEL)), ANY, _full_spec((A_CONV_WIDTH, A_DIM)),
                  _full_spec((1, A_DIM)), _full_spec((1, A_DIM)), _full_spec((1, A_DIM)),
                  _full_spec((B_CONV_WIDTH, B_DIM)), ANY],
        out_specs=[_row_spec(tm, D_MODEL), _row_spec(tm, D_MODEL), _row_spec(tm, IN_EVEN), _row_spec(tm, A_DIM),
                   _row_spec(tm, B_DIM), _row_spec(tm, D_MODEL)],
        out_shape=[shp(D_MODEL, F32), shp(D_MODEL, BF16), shp(IN_EVEN, BF16), shp(A_DIM, F32), shp(B_DIM, BF16),
                   shp(D_MODEL, BF16)],
        scratch_shapes=[pltpu.VMEM((N_CHIPS, D_MODEL, IN_EVEN // N_CHIPS), BF16), pltpu.VMEM((D_MODEL, D_MODEL), BF16),
                        pltpu.VMEM((A_HALO + tm, A_DIM), F32), pltpu.VMEM((B_HALO + tm, B_DIM), F32),
                        pltpu.SemaphoreType.DMA],
        compiler_params=_params(56),
    )(x, norm_g, w_in, conv_a_w, conv_a_b, ln_g, ln_b, conv_b_w, w_out)


def _fwd_mlp(h, norm_g, w1_all, w2_all, layer, *, tm):
    tokens = h.shape[0]
    nt = tokens // tm
    fs = D_FF // N_CHIPS

    def body(h_ref, g_ref, w1_hbm, w2_hbm, ho_ref, n_ref, p_ref, q_ref, w1_v, w2_v, sem):
        @pl.when(pl.program_id(0) == 0)
        def _():
            for j in range(N_CHIPS):
                _load(w1_hbm.at[j, layer], w1_v.at[j], sem)
                _load(w2_hbm.at[j, layer], w2_v.at[j], sem)

        xv = h_ref[...]
        nf, _ = _rms_fwd(xv, g_ref[...])
        n = nf.astype(BF16)
        n_ref[...] = n
        acc = xv
        for j in range(N_CHIPS):
            p = _dot(n, w1_v[j])
            p_ref[:, j * fs:(j + 1) * fs] = p.astype(BF16)
            r = jnp.maximum(p, 0.0)
            q = (r * r).astype(BF16)
            q_ref[:, j * fs:(j + 1) * fs] = q
            acc = acc + _dot(q, w2_v[j])
        ho_ref[...] = acc

    shp = lambda cols, dt: jax.ShapeDtypeStruct((tokens, cols), dt)
    return pl.pallas_call(
        body, name=f"fwd_mlp{layer}", grid=(nt,),
        in_specs=[_row_spec(tm, D_MODEL), _full_spec((1, D_MODEL)), ANY, ANY],
        out_specs=[_row_spec(tm, D_MODEL), _row_spec(tm, D_MODEL), _row_spec(tm, D_FF), _row_spec(tm, D_FF)],
        out_shape=[shp(D_MODEL, F32), shp(D_MODEL, BF16), shp(D_FF, BF16), shp(D_FF, BF16)],
        scratch_shapes=[pltpu.VMEM((N_CHIPS, D_MODEL, fs), BF16), pltpu.VMEM((N_CHIPS, fs, D_MODEL), BF16),
                        pltpu.SemaphoreType.DMA],
        compiler_params=_params(56),
    )(h, norm_g, w1_all, w2_all)


def _tril_mask():
    row = lax.broadcasted_iota(jnp.int32, (CHUNK, CHUNK), 0)
    col = lax.broadcasted_iota(jnp.int32, (CHUNK, CHUNK), 1)
    return row >= col


def _triu_mask():
    row = lax.broadcasted_iota(jnp.int32, (CHUNK, CHUNK), 0)
    col = lax.broadcasted_iota(jnp.int32, (CHUNK, CHUNK), 1)
    return row <= col


def _fwd_odd(h, norm_g, w_in, b_in, ln_g, ln_b, w_s, b_s_rows, w_out, *, tm):
    tokens = h.shape[0]
    nt = tokens // tm
    cs = 2 * C_DIM // N_CHIPS

    def body(h_ref, g_ref, win_hbm, bin_ref, lng_ref, lnb_ref, ws_ref, bs_ref, wout_hbm,
             ho_ref, n_ref, s_ref, sv_ref, y_ref, win_v, wout_v, bd, sem):
        @pl.when(pl.program_id(0) == 0)
        def _():
            _load(win_hbm, win_v, sem)
            _load(wout_hbm, wout_v, sem)
            mask = _tril_mask()
            bd[...] = jnp.zeros(bd.shape, BF16)
            for g in range(C_GROUPS):
                w = jnp.where(mask, ws_ref[g], 0.0).astype(BF16)
                bd[g, 0:CHUNK, 0:CHUNK] = w
                bd[g, CHUNK:PAIR, CHUNK:PAIR] = w

        xv = h_ref[...]
        nf, _ = _rms_fwd(xv, g_ref[...])
        n = nf.astype(BF16)
        n_ref[...] = n
        s = jnp.concatenate([_dot(n, win_v[j]) for j in range(N_CHIPS)], axis=1) + bin_ref[...]
        s_ref[...] = s.astype(BF16)
        cdf, _ = _gelu_parts(s)
        zz = s * cdf
        u, v = zz[:, 0:C_DIM], zz[:, C_DIM:2 * C_DIM]
        xhat, _ = _ln_stats(v)
        vn = (xhat * lng_ref[...] + lnb_ref[...]).astype(BF16)
        for g in range(C_GROUPS):
            cols = slice(g * CHUNK, (g + 1) * CHUNK)
            bias = jnp.concatenate([bs_ref[g], bs_ref[g]], axis=0)
            for r0 in range(0, tm, PAIR):
                sv = _dot(bd[g], vn[r0:r0 + PAIR, cols]) + bias
                sv_ref[r0:r0 + PAIR, cols] = sv.astype(BF16)
                y_ref[r0:r0 + PAIR, cols] = (u[r0:r0 + PAIR, cols] * sv).astype(BF16)
        ho_ref[...] = xv + _dot(y_ref[...], wout_v[...])

    shp = lambda cols, dt: jax.ShapeDtypeStruct((tokens, cols), dt)
    return pl.pallas_call(
        body, name="fwd_odd", grid=(nt,),
        in_specs=[_row_spec(tm, D_MODEL), _full_spec((1, D_MODEL)), ANY, _full_spec((1, 2 * C_DIM)),
                  _full_spec((1, C_DIM)), _full_spec((1, C_DIM)), _full_spec((C_GROUPS, CHUNK, CHUNK)),
                  _full_spec((C_GROUPS, CHUNK, CHUNK)), ANY],
        out_specs=[_row_spec(tm, D_MODEL), _row_spec(tm, D_MODEL), _row_spec(tm, 2 * C_DIM), _row_spec(tm, C_DIM),
                   _row_spec(tm, C_DIM)],
        out_shape=[shp(D_MODEL, F32), shp(D_MODEL, BF16), shp(2 * C_DIM, BF16), shp(C_DIM, BF16), shp(C_DIM, BF16)],
        scratch_shapes=[pltpu.VMEM((N_CHIPS, D_MODEL, cs), BF16), pltpu.VMEM((C_DIM, D_MODEL), BF16),
                        pltpu.VMEM((C_GROUPS, PAIR, PAIR), BF16), pltpu.SemaphoreType.DMA],
        compiler_params=_params(56),
    )(h, norm_g, w_in, b_in, ln_g, ln_b, w_s, b_s_rows, w_out)


def _loss_head(h, norm_g, target, *, tm):
    tokens = h.shape[0]
    nt = tokens // tm

    def body(h_ref, g_ref, t_ref, loss_ref, dh_ref, dg_ref):
        @pl.when(pl.program_id(0) == 0)
        def _():
            loss_ref[...] = jnp.zeros((1, 1), F32)
            dg_ref[...] = jnp.zeros((1, D_MODEL), F32)

        xv = h_ref[...]
        g = g_ref[...]
        out, rstd = _rms_fwd(xv, g)
        err = out - t_ref[...]
        per_token = jnp.sum(err * err, axis=1, keepdims=True) * (1.0 / D_MODEL)
        loss_ref[...] += 0.5 * jnp.sum(per_token, axis=0, keepdims=True)
        dx, dg = _rms_bwd(err * (1.0 / D_MODEL), xv, rstd, g)
        dh_ref[...] = dx
        dg_ref[...] += dg

    return pl.pallas_call(
        body, name="loss_head", grid=(nt,),
        in_specs=[_row_spec(tm, D_MODEL), _full_spec((1, D_MODEL)), _row_spec(tm, D_MODEL)],
        out_specs=[_full_spec((1, 1)), _row_spec(tm, D_MODEL), _full_spec((1, D_MODEL))],
        out_shape=[jax.ShapeDtypeStruct((1, 1), F32), jax.ShapeDtypeStruct((tokens, D_MODEL), F32),
                   jax.ShapeDtypeStruct((1, D_MODEL), F32)],
        compiler_params=_params(32),
    )(h, norm_g, target)


def _bwd_mlp(dh, h, norm_g, p, w1_all, w2_all, layer, *, tm):
    tokens = h.shape[0]
    nt = tokens // tm
    fs = D_FF // N_CHIPS

    def body(dh_ref, h_ref, g_ref, p_ref, w1_hbm, w2_hbm, dx_ref, dp_ref, dg_ref, w1_v, w2_v, sem):
        @pl.when(pl.program_id(0) == 0)
        def _():
            for j in range(N_CHIPS):
                _load(w1_hbm.at[j, layer], w1_v.at[j], sem)
                _load(w2_hbm.at[j, layer], w2_v.at[j], sem)
            dg_ref[...] = jnp.zeros((1, D_MODEL), F32)

        dhv = dh_ref[...]
        dhb = dhv.astype(BF16)
        dn = jnp.zeros((tm, D_MODEL), F32)
        for j in range(N_CHIPS):
            dq = _dot_nt(dhb, w2_v[j])
            r = jnp.maximum(p_ref[:, j * fs:(j + 1) * fs].astype(F32), 0.0)
            dp = ((2.0 * r) * dq).astype(BF16)
            dp_ref[:, j * fs:(j + 1) * fs] = dp
            dn = dn + _dot_nt(dp, w1_v[j])
        xv = h_ref[...]
        g = g_ref[...]
        _, rstd = _rms_fwd(xv, g)
        dx, dg = _rms_bwd(dn, xv, rstd, g)
        dx_ref[...] = dhv + dx
        dg_ref[...] += dg

    return pl.pallas_call(
        body, name=f"bwd_mlp{layer}", grid=(nt,),
        in_specs=[_row_spec(tm, D_MODEL), _row_spec(tm, D_MODEL), _full_spec((1, D_MODEL)), _row_spec(tm, D_FF), ANY, ANY],
        out_specs=[_row_spec(tm, D_MODEL), _row_spec(tm, D_FF), _full_spec((1, D_MODEL))],
        out_shape=[jax.ShapeDtypeStruct((tokens, D_MODEL), F32), jax.ShapeDtypeStruct((tokens, D_FF), BF16),
                   jax.ShapeDtypeStruct((1, D_MODEL), F32)],
        scratch_shapes=[pltpu.VMEM((N_CHIPS, D_MODEL, fs), BF16), pltpu.VMEM((N_CHIPS, fs, D_MODEL), BF16),
                        pltpu.SemaphoreType.DMA],
        compiler_params=_params(56),
    )(dh, h, norm_g, p, w1_all, w2_all)


def _bwd_odd(dh, h, norm_g, s, sv, w_in, ln_g, ln_b, w_s, w_out, *, tm):
    tokens = h.shape[0]
    nt = tokens // tm
    cs = 2 * C_DIM // N_CHIPS

    def body(dh_ref, h_ref, g_ref, s_ref, sv_ref, win_hbm, lng_ref, lnb_ref, ws_ref, wout_hbm,
             dx_ref, ds_ref, dg_ref, dbin_ref, dlng_ref, dlnb_ref, dws_ref, dbs_ref,
             win_v, wout_v, bdt, dws_acc, dbs_acc, dvn, sem):
        i = pl.program_id(0)

        @pl.when(i == 0)
        def _():
            _load(win_hbm, win_v, sem)
            _load(wout_hbm, wout_v, sem)
            mask_t = _triu_mask()
            bdt[...] = jnp.zeros(bdt.shape, BF16)
            for g in range(C_GROUPS):
                wt = jnp.where(mask_t, ws_ref[g].T, 0.0).astype(BF16)
                bdt[g, 0:CHUNK, 0:CHUNK] = wt
                bdt[g, CHUNK:PAIR, CHUNK:PAIR] = wt
            dws_acc[...] = jnp.zeros(dws_acc.shape, F32)
            dbs_acc[...] = jnp.zeros(dbs_acc.shape, F32)
            dg_ref[...] = jnp.zeros(dg_ref.shape, F32)
            dbin_ref[...] = jnp.zeros(dbin_ref.shape, F32)
            dlng_ref[...] = jnp.zeros(dlng_ref.shape, F32)
            dlnb_ref[...] = jnp.zeros(dlnb_ref.shape, F32)

        dhv = dh_ref[...]
        dy = _dot_nt(dhv.astype(BF16), wout_v[...])
        sf = s_ref[...].astype(F32)
        cdf, pdf = _gelu_parts(sf)
        zz = sf * cdf
        dgelu = cdf + sf * pdf
        u, v = zz[:, 0:C_DIM], zz[:, C_DIM:2 * C_DIM]
        xhat, rs = _ln_stats(v)
        lng = lng_ref[...]
        vn = (xhat * lng + lnb_ref[...]).astype(BF16)
        du = dy * sv_ref[...].astype(F32)
        dsv = dy * u
        dsvb = dsv.astype(BF16)
        for g in range(C_GROUPS):
            cols = slice(g * CHUNK, (g + 1) * CHUNK)
            for r0 in range(0, tm, PAIR):
                blk = dsvb[r0:r0 + PAIR, cols]
                dvn[r0:r0 + PAIR, cols] = _dot(bdt[g], blk)
                dws_acc[g] += _dot_nt(blk, vn[r0:r0 + PAIR, cols])
                dbs_acc[g] += dsv[r0:r0 + CHUNK, cols] + dsv[r0 + CHUNK:r0 + PAIR, cols]
        dv, dlng, dlnb = _ln_bwd(dvn[...], xhat, rs, lng)
        dlng_ref[...] += dlng
        dlnb_ref[...] += dlnb
        ds = jnp.concatenate([du, dv], axis=1) * dgelu
        dbin_ref[...] += jnp.sum(ds, axis=0, keepdims=True)
        dsb = ds.astype(BF16)
        ds_ref[...] = dsb
        dn = jnp.zeros((tm, D_MODEL), F32)
        for j in range(N_CHIPS):
            dn = dn + _dot_nt(dsb[:, j * cs:(j + 1) * cs], win_v[j])
        xv = h_ref[...]
        g = g_ref[...]
        _, rstd = _rms_fwd(xv, g)
        dx, dg = _rms_bwd(dn, xv, rstd, g)
        dx_ref[...] = dhv + dx
        dg_ref[...] += dg

        @pl.when(i == nt - 1)
        def _():
            mask = _tril_mask()
            for g in range(C_GROUPS):
                full = dws_acc[g]
                dws_ref[g] = jnp.where(mask, full[0:CHUNK, 0:CHUNK] + full[CHUNK:PAIR, CHUNK:PAIR], 0.0)
                dbs_ref[g] = jnp.sum(dbs_acc[g], axis=1, keepdims=True)

    row = lambda cols: jax.ShapeDtypeStruct((1, cols), F32)
    return pl.pallas_call(
        body, name="bwd_odd", grid=(nt,),
        in_specs=[_row_spec(tm, D_MODEL), _row_spec(tm, D_MODEL), _full_spec((1, D_MODEL)), _row_spec(tm, 2 * C_DIM),
                  _row_spec(tm, C_DIM), ANY, _full_spec((1, C_DIM)), _full_spec((1, C_DIM)),
                  _full_spec((C_GROUPS, CHUNK, CHUNK)), ANY],
        out_specs=[_row_spec(tm, D_MODEL), _row_spec(tm, 2 * C_DIM), _full_spec((1, D_MODEL)), _full_spec((1, 2 * C_DIM)),
                   _full_spec((1, C_DIM)), _full_spec((1, C_DIM)), _full_spec((C_GROUPS, CHUNK, CHUNK)),
                   _full_spec((C_GROUPS, CHUNK, 1))],
        out_shape=[jax.ShapeDtypeStruct((tokens, D_MODEL), F32), jax.ShapeDtypeStruct((tokens, 2 * C_DIM), BF16),
                   row(D_MODEL), row(2 * C_DIM), row(C_DIM), row(C_DIM),
                   jax.ShapeDtypeStruct((C_GROUPS, CHUNK, CHUNK), F32), jax.ShapeDtypeStruct((C_GROUPS, CHUNK, 1), F32)],
        scratch_shapes=[pltpu.VMEM((N_CHIPS, D_MODEL, cs), BF16), pltpu.VMEM((C_DIM, D_MODEL), BF16),
                        pltpu.VMEM((C_GROUPS, PAIR, PAIR), BF16), pltpu.VMEM((C_GROUPS, PAIR, PAIR), F32),
                        pltpu.VMEM((C_GROUPS, CHUNK, CHUNK), F32), pltpu.VMEM((tm, C_DIM), F32),
                        pltpu.SemaphoreType.DMA],
        compiler_params=_params(56),
    )(dh, h, norm_g, s, sv, w_in, ln_g, ln_b, w_s, w_out)


def _bwd_even(dh, x, norm_g, z, a2, cv, w_in, conv_a_w, ln_g, ln_b, conv_b_w, w_out, *, tm, seq):
    tokens = x.shape[0]
    nt, tps = tokens // tm, seq // tm
    ws = IN_EVEN // N_CHIPS

    def body(dh_ref, x_ref, g_ref, z_ref, a2_ref, cv_ref, win_hbm, caw_ref, lng_ref, lnb_ref, cbw_ref, wout_hbm,
             dx_ref, dz_ref, dg_ref, dcaw_ref, dcab_ref, dlng_ref, dlnb_ref, dcbw_ref,
             win_v, wout_v, ea, eb, a1s, da1s, dw_acc, sem):
        i = pl.program_id(0)

        @pl.when(i == 0)
        def _():
            _load(win_hbm, win_v, sem)
            _load(wout_hbm, wout_v, sem)
            dw_acc[...] = jnp.zeros(dw_acc.shape, F32)
            for ref in (dg_ref, dcab_ref, dlng_ref, dlnb_ref, dcbw_ref):
                ref[...] = jnp.zeros(ref.shape, F32)

        dhv = dh_ref[...]
        dmix = _dot_nt(dhv.astype(BF16), wout_v[...])
        da4, dbo = dmix[:, 0:A_DIM], dmix[:, A_DIM:A_DIM + B_DIM]
        zf = z_ref[...].astype(F32)
        a_val, a_gate = zf[:, 0:A_DIM], zf[:, A_DIM:2 * A_DIM]
        b_gate, c_gate, b_val = zf[:, 1024:1536], zf[:, 1536:2048], zf[:, 2048:2560]

        xhat, rs = _ln_stats(a2_ref[...])
        lng = lng_ref[...]
        a3 = xhat * lng + lnb_ref[...]
        sg = jax.nn.sigmoid(a3)
        da3 = da4 * (sg * (1.0 + a3 * (1.0 - sg)))
        da2, dlng, dlnb = _ln_bwd(da3, xhat, rs, lng)
        dlng_ref[...] += dlng
        dlnb_ref[...] += dlnb
        dcab_ref[...] += jnp.sum(da2, axis=0, keepdims=True)

        last = ((nt - 1 - i) % tps) == tps - 1
        dcv = dbo * b_gate

        @pl.when(last)
        def _():
            ea[tm:tm + A_HALO, :] = jnp.zeros((A_HALO, A_DIM), F32)
            eb[tm:tm + B_HALO, :] = jnp.zeros((B_HALO, B_DIM), F32)

        @pl.when(jnp.logical_not(last))
        def _():
            ea[tm:tm + A_HALO, :] = ea[0:A_HALO, :]
            eb[tm:tm + B_HALO, :] = eb[0:B_HALO, :]

        ea[0:tm, :] = da2
        eb[0:tm, :] = dcv
        sig = jax.nn.sigmoid(a_gate)
        a1s[...] = a_val * sig
        for r0 in range(0, tm, CONV_ROWS):
            a1c = a1s[r0:r0 + CONV_ROWS, :]
            acc = jnp.zeros((CONV_ROWS, A_DIM), F32)
            for j in range(A_CONV_WIDTH):
                k = A_CONV_WIDTH - 1 - j
                sl = ea[r0 + j:r0 + j + CONV_ROWS, :]
                acc = acc + caw_ref[k:k + 1, :] * sl
                dw_acc[k] += sl * a1c
            da1s[r0:r0 + CONV_ROWS, :] = acc
        da1 = da1s[...]
        da_val = da1 * sig
        da_gate = da1 * a_val * (sig * (1.0 - sig))

        db_gate = dbo * cv_ref[...].astype(F32)
        cb = c_gate * b_val
        dcb = jnp.zeros((tm, B_DIM), F32)
        for j in range(B_CONV_WIDTH):
            k = B_CONV_WIDTH - 1 - j
            sl = eb[j:j + tm, :]
            dcb = dcb + cbw_ref[k:k + 1, :] * sl
            dcbw_ref[k:k + 1, :] += jnp.sum(sl * cb, axis=0, keepdims=True)
        dz = jnp.concatenate([da_val, da_gate, db_gate, dcb * b_val, dcb * c_gate], axis=1).astype(BF16)
        dz_ref[...] = dz
        dn = jnp.zeros((tm, D_MODEL), F32)
        for j in range(N_CHIPS):
            dn = dn + _dot_nt(dz[:, j * ws:(j + 1) * ws], win_v[j])
        xv = x_ref[...]
        g = g_ref[...]
        _, rstd = _rms_fwd(xv, g)
        dx, dg = _rms_bwd(dn, xv, rstd, g)
        dx_ref[...] = dhv + dx
        dg_ref[...] += dg

        @pl.when(i == nt - 1)
        def _():
            for k in range(A_CONV_WIDTH):
                dcaw_ref[k:k + 1, :] = jnp.sum(dw_acc[k], axis=0, keepdims=True)

    row = lambda cols: jax.ShapeDtypeStruct((1, cols), F32)
    rs_ = functools.partial(_row_spec, rev_nt=nt)
    return pl.pallas_call(
        body, name="bwd_even", grid=(nt,),
        in_specs=[rs_(tm, D_MODEL), rs_(tm, D_MODEL), _full_spec((1, D_MODEL)), rs_(tm, IN_EVEN), rs_(tm, A_DIM),
                  rs_(tm, B_DIM), ANY, _full_spec((A_CONV_WIDTH, A_DIM)), _full_spec((1, A_DIM)), _full_spec((1, A_DIM)),
                  _full_spec((B_CONV_WIDTH, B_DIM)), ANY],
        out_specs=[rs_(tm, D_MODEL), rs_(tm, IN_EVEN), _full_spec((1, D_MODEL)), _full_spec((A_CONV_WIDTH, A_DIM)),
                   _full_spec((1, A_DIM)), _full_spec((1, A_DIM)), _full_spec((1, A_DIM)), _full_spec((B_CONV_WIDTH, B_DIM))],
        out_shape=[jax.ShapeDtypeStruct((tokens, D_MODEL), F32), jax.ShapeDtypeStruct((tokens, IN_EVEN), BF16),
                   row(D_MODEL), jax.ShapeDtypeStruct((A_CONV_WIDTH, A_DIM), F32), row(A_DIM), row(A_DIM), row(A_DIM),
                   jax.ShapeDtypeStruct((B_CONV_WIDTH, B_DIM), F32)],
        scratch_shapes=[pltpu.VMEM((N_CHIPS, D_MODEL, ws), BF16), pltpu.VMEM((D_MODEL, D_MODEL), BF16),
                        pltpu.VMEM((tm + A_HALO, A_DIM), F32), pltpu.VMEM((tm + B_HALO, B_DIM), F32),
                        pltpu.VMEM((tm, A_DIM), F32), pltpu.VMEM((tm, A_DIM), F32),
                        pltpu.VMEM((A_CONV_WIDTH, CONV_ROWS, A_DIM), F32), pltpu.SemaphoreType.DMA],
        compiler_params=_params(56),
    )(dh, x, norm_g, z, a2, cv, w_in, conv_a_w, ln_g, ln_b, conv_b_w, w_out)


def _wgrad(a, b, name, *, col_shards):
    tokens, m = a.shape
    n = b.shape[1]
    kc = 512
    if col_shards:
        bm, bn = m // 2, n // N_CHIPS
        grid = (2, N_CHIPS)
        out_shape = (N_CHIPS, 2, bm, bn)
        out_spec = pl.BlockSpec((None, None, bm, bn), lambda i, j: (j, i, 0, 0))
    else:
        bm, bn = m // 8, n
        grid = (8, 1)
        out_shape = (N_CHIPS, 2, bm, bn)
        out_spec = pl.BlockSpec((None, None, bm, bn), lambda i, j: (i // 2, i % 2, 0, 0))

    def body(a_ref, b_ref, o_ref):
        acc = jnp.zeros((bm, bn), F32)
        for k0 in range(0, tokens, kc):
            acc = acc + _dot_tn(a_ref[k0:k0 + kc, :].astype(BF16), b_ref[k0:k0 + kc, :].astype(BF16))
        o_ref[...] = acc

    return pl.pallas_call(
        body, name=name, grid=grid,
        in_specs=[pl.BlockSpec((tokens, bm), lambda i, j: (0, i)), pl.BlockSpec((tokens, bn), lambda i, j: (0, j))],
        out_specs=out_spec,
        out_shape=jax.ShapeDtypeStruct(out_shape, F32),
        compiler_params=_params(56, ("parallel", "parallel")),
    )(a, b)


def _forward_backward(x2, tgt2, w_ev_in, w_ev_out, w_od_in, w_od_out, g_w1, g_w2, conv_a_w, conv_b_w, od_norm, od_bias,
                      od_lng, od_lnb, ev_norm_g, ev_conv_a_b, ev_ln_a_g, ev_ln_a_b, od_w_s, od_b_s, mlp_norm_g, final_norm_g,
                      *, tm, seq):
    d = x2.shape[1]
    b_s_rows = jnp.broadcast_to(od_b_s[0][:, :, None], (C_GROUPS, CHUNK, CHUNK))
    h1, n0, z, a2, cv, mix = _fwd_even(x2, ev_norm_g, w_ev_in, conv_a_w, ev_conv_a_b, ev_ln_a_g, ev_ln_a_b, conv_b_w,
                                       w_ev_out, tm=tm, seq=seq)
    h2, n1, p0, q0 = _fwd_mlp(h1, mlp_norm_g[0:1], g_w1, g_w2, 0, tm=tm)
    h3, n2, s, sv, y = _fwd_odd(h2, od_norm, w_od_in, od_bias, od_lng, od_lnb, od_w_s[0], b_s_rows, w_od_out, tm=tm)
    h4, n3, p1, q1 = _fwd_mlp(h3, mlp_norm_g[1:2], g_w1, g_w2, 1, tm=tm)
    loss_part, dh4, d_final_g = _loss_head(h4, jnp.reshape(final_norm_g, (1, d)), tgt2, tm=tm)

    dh3, dp1, d_mlp_g1 = _bwd_mlp(dh4, h3, mlp_norm_g[1:2], p1, g_w1, g_w2, 1, tm=tm)
    gw2_1 = _wgrad(q1, dh4, "wgrad_w2_1", col_shards=False)
    gw1_1 = _wgrad(n3, dp1, "wgrad_w1_1", col_shards=True)
    dh2, ds, d_od_norm, d_od_bin, d_od_lng, d_od_lnb, d_ws, d_bs = _bwd_odd(
        dh3, h2, od_norm, s, sv, w_od_in, od_lng, od_lnb, od_w_s[0], w_od_out, tm=tm)
    g_od_out_w = _wgrad(y, dh3, "wgrad_od_out", col_shards=False)
    g_od_in_w = _wgrad(n2, ds, "wgrad_od_in", col_shards=True)
    dh1, dp0, d_mlp_g0 = _bwd_mlp(dh2, h1, mlp_norm_g[0:1], p0, g_w1, g_w2, 0, tm=tm)
    gw2_0 = _wgrad(q0, dh2, "wgrad_w2_0", col_shards=False)
    gw1_0 = _wgrad(n1, dp0, "wgrad_w1_0", col_shards=True)
    dx, dz, d_ev_norm, d_caw, d_cab, d_ev_lng, d_ev_lnb, d_cbw = _bwd_even(
        dh1, x2, ev_norm_g, z, a2, cv, w_ev_in, conv_a_w, ev_ln_a_g, ev_ln_a_b, conv_b_w, w_ev_out, tm=tm, seq=seq)
    g_ev_out_w = _wgrad(mix, dh1, "wgrad_ev_out", col_shards=False)
    g_ev_in_w = _wgrad(n0, dz, "wgrad_ev_in", col_shards=True)

    big = [g_ev_in_w, g_ev_out_w, g_od_in_w, g_od_out_w, gw1_0, gw1_1, gw2_0, gw2_1]
    rep_grads = [d_ev_norm, d_cab, d_ev_lng, d_ev_lnb, d_ws, d_bs, d_mlp_g0, d_mlp_g1, d_final_g]
    shard_grads_full = [d_caw, d_cbw, d_od_norm, d_od_bin, d_od_lng, d_od_lnb]
    return loss_part, dx, big, rep_grads, shard_grads_full


def _rows128(a):
    return jnp.reshape(a, (-1, LANES))


def _pack(arrays, pad_to=8):
    rows = [_rows128(a) for a in arrays]
    total = sum(r.shape[0] for r in rows)
    pad = (-total) % pad_to
    if pad:
        rows.append(jnp.zeros((pad, LANES), rows[0].dtype))
    return jnp.concatenate(rows, axis=0)


def _unpack(buf, shapes):
    out, r0 = [], 0
    for shp in shapes:
        size = 1
        for d in shp:
            size *= d
        nr = size // LANES
        out.append(jnp.reshape(buf[r0:r0 + nr], shp))
        r0 += nr
    return out


def _halves(w2d):
    r, c = w2d.shape
    return jnp.reshape(w2d, (2, r // 2, c))


def kernel(x, ev_norm_g, ev_w_in, ev_conv_a_w, ev_conv_a_b, ev_ln_a_g, ev_ln_a_b, ev_conv_b_w, ev_w_out, od_norm_g, od_w_in, od_b_in, od_ln_v_g, od_ln_v_b, od_w_s, od_b_s, od_w_out, mlp_norm_g, mlp_w1, mlp_w2, final_norm_g, loss_target, m_ev_norm_g, m_ev_w_in, m_ev_conv_a_w, m_ev_conv_a_b, m_ev_ln_a_g, m_ev_ln_a_b, m_ev_conv_b_w, m_ev_w_out, m_od_norm_g, m_od_w_in, m_od_b_in, m_od_ln_v_g, m_od_ln_v_b, m_od_w_s, m_od_b_s, m_od_w_out, m_mlp_norm_g, m_mlp_w1, m_mlp_w2, m_final_norm_g, v_ev_norm_g, v_ev_w_in, v_ev_conv_a_w, v_ev_conv_a_b, v_ev_ln_a_g, v_ev_ln_a_b, v_ev_conv_b_w, v_ev_w_out, v_od_norm_g, v_od_w_in, v_od_b_in, v_od_ln_v_g, v_od_ln_v_b, v_od_w_s, v_od_b_s, v_od_w_out, v_mlp_norm_g, v_mlp_w1, v_mlp_w2, v_final_norm_g):
    tm = TOKEN_TILE
    batch, seq, d = x.shape
    tokens = batch * seq
    x2 = jnp.reshape(x, (tokens, d))
    tgt2 = jnp.reshape(loss_target, (tokens, d))
    chip = 2 * lax.axis_index("x") + lax.axis_index("y")

    small_shapes = [(A_CONV_WIDTH, LANES), (B_CONV_WIDTH, LANES), (256,), (512,), (256,), (256,)]
    small_shard = _pack([ev_conv_a_w[0], ev_conv_b_w[0], od_norm_g[0], od_b_in[0], od_ln_v_g[0], od_ln_v_b[0]], pad_to=16)
    shards = [
        _halves(_cast_bf16(ev_w_in[0], "cast_ev_w_in")),
        _halves(_cast_bf16(ev_w_out[0], "cast_ev_w_out")),
        _halves(_cast_bf16(od_w_in[0], "cast_od_w_in")),
        _halves(_cast_bf16(od_w_out[0], "cast_od_w_out")),
        _cast_bf16(jnp.reshape(mlp_w1, (2 * D_MODEL, D_FF // N_CHIPS)), "cast_mlp_w1").reshape(2, D_MODEL, D_FF // N_CHIPS),
        _cast_bf16(jnp.reshape(mlp_w2, (2 * D_FF // N_CHIPS, D_MODEL)), "cast_mlp_w2").reshape(2, D_FF // N_CHIPS, D_MODEL),
        _halves(small_shard),
    ]
    g_ev_in, g_ev_out, g_od_in, g_od_out, g_w1, g_w2, g_small = _all_gather_shards(shards)
    w_ev_in = jnp.reshape(g_ev_in, (N_CHIPS, D_MODEL, IN_EVEN // N_CHIPS))
    w_ev_out = jnp.reshape(g_ev_out, (D_MODEL, D_MODEL))
    w_od_in = jnp.reshape(g_od_in, (N_CHIPS, D_MODEL, 2 * C_DIM // N_CHIPS))
    w_od_out = jnp.reshape(g_od_out, (C_DIM, D_MODEL))
    small_all = jnp.reshape(g_small, (N_CHIPS, -1, LANES))
    per_chip = [_unpack(small_all[q], small_shapes) for q in range(N_CHIPS)]
    conv_a_w = jnp.concatenate([pc[0] for pc in per_chip], axis=1)
    conv_b_w = jnp.concatenate([pc[1] for pc in per_chip], axis=1)
    od_norm = jnp.concatenate([pc[2] for pc in per_chip])[None, :]
    od_bias = jnp.concatenate([pc[3] for pc in per_chip])[None, :]
    od_lng = jnp.concatenate([pc[4] for pc in per_chip])[None, :]
    od_lnb = jnp.concatenate([pc[5] for pc in per_chip])[None, :]

    loss_part, dx, big, rep_grads, shard_grads_full = _forward_backward(
        x2, tgt2, w_ev_in, w_ev_out, w_od_in, w_od_out, g_w1, g_w2, conv_a_w, conv_b_w, od_norm, od_bias, od_lng, od_lnb,
        ev_norm_g, ev_conv_a_b, ev_ln_a_g, ev_ln_a_b, od_w_s, od_b_s, mlp_norm_g, final_norm_g, tm=tm, seq=seq)
    loss = lax.psum(loss_part[0, 0], ("x", "y", "c"))

    names =["ev_in", "ev_out", "od_in", "od_out", "w1_0", "w1_1", "w2_0", "w2_1"]
    from_sibling = _pair_exchange_halves(big)
    chip_sums = [_add_pair(g, r, f"pair_sum_{nm}") for g, r, nm in zip(big, from_sibling, names)]
    from_chips = _chip_exchange(chip_sums)
    halves = [_add_chips(o, r, f"chip_sum_{nm}") for o, r, nm in zip(chip_sums, from_chips, names)]
    full = _pair_assemble(halves)
    r_ev_in, r_ev_out, r_od_in, r_od_out, r_w1_0, r_w1_1, r_w2_0, r_w2_1 = [
        jnp.reshape(f, (2 * f.shape[1], f.shape[2])) for f in full]

    n_rep_rows = sum(g.size for g in rep_grads) // LANES
    small_sum = _sum_devices(_all_to_all_small(_pack(rep_grads + shard_grads_full)))
    rep_shapes = [(1, D_MODEL), (1, A_DIM), (1, A_DIM), (1, A_DIM), (1, C_GROUPS, CHUNK, CHUNK), (1, C_GROUPS, CHUNK),
                  (1, D_MODEL), (1, D_MODEL), (D_MODEL,)]
    full_shapes = [(A_CONV_WIDTH, A_DIM), (B_CONV_WIDTH, B_DIM), (4, 256), (4, 512), (4, 256), (4, 256)]
    rep_red = _unpack(small_sum[:n_rep_rows], rep_shapes)
    full_red = _unpack(small_sum[n_rep_rows:], full_shapes)
    shard_red = [
        lax.dynamic_slice_in_dim(full_red[0], chip * LANES, LANES, axis=1)[None],
        lax.dynamic_slice_in_dim(full_red[1], chip * LANES, LANES, axis=1)[None],
        lax.dynamic_index_in_dim(full_red[2], chip, axis=0, keepdims=True),
        lax.dynamic_index_in_dim(full_red[3], chip, axis=0, keepdims=True),
        lax.dynamic_index_in_dim(full_red[4], chip, axis=0, keepdims=True),
        lax.dynamic_index_in_dim(full_red[5], chip, axis=0, keepdims=True),
    ]
    rep_red[6] = jnp.concatenate([rep_red[6], rep_red[7]], axis=0)
    del rep_red[7]

    def big_update(w, m, v, grads, name):
        layers = len(grads)
        shp3 = (layers,) + grads[0].shape
        outs = _adamw(jnp.reshape(w, shp3), jnp.reshape(m, shp3), jnp.reshape(v, shp3), grads, name)
        return [jnp.reshape(o, w.shape) for o in outs]

    upd = {
        "ev_w_in": big_update(ev_w_in, m_ev_w_in, v_ev_w_in, [r_ev_in], "adamw_ev_w_in"),
        "ev_w_out": big_update(ev_w_out, m_ev_w_out, v_ev_w_out, [r_ev_out], "adamw_ev_w_out"),
        "od_w_in": big_update(od_w_in, m_od_w_in, v_od_w_in, [r_od_in], "adamw_od_w_in"),
        "od_w_out": big_update(od_w_out, m_od_w_out, v_od_w_out, [r_od_out], "adamw_od_w_out"),
        "mlp_w1": big_update(mlp_w1, m_mlp_w1, v_mlp_w1, [r_w1_0, r_w1_1], "adamw_mlp_w1"),
        "mlp_w2": big_update(mlp_w2, m_mlp_w2, v_mlp_w2, [r_w2_0, r_w2_1], "adamw_mlp_w2"),
    }
    small_names = ["ev_norm_g", "ev_conv_a_b", "ev_ln_a_g", "ev_ln_a_b", "od_w_s", "od_b_s", "mlp_norm_g", "final_norm_g",
                   "ev_conv_a_w", "ev_conv_b_w", "od_norm_g", "od_b_in", "od_ln_v_g", "od_ln_v_b"]
    small_w = [ev_norm_g, ev_conv_a_b, ev_ln_a_g, ev_ln_a_b, od_w_s, od_b_s, mlp_norm_g, final_norm_g,
               ev_conv_a_w, ev_conv_b_w, od_norm_g, od_b_in, od_ln_v_g, od_ln_v_b]
    small_m = [m_ev_norm_g, m_ev_conv_a_b, m_ev_ln_a_g, m_ev_ln_a_b, m_od_w_s, m_od_b_s, m_mlp_norm_g, m_final_norm_g,
               m_ev_conv_a_w, m_ev_conv_b_w, m_od_norm_g, m_od_b_in, m_od_ln_v_g, m_od_ln_v_b]
    small_v = [v_ev_norm_g, v_ev_conv_a_b, v_ev_ln_a_g, v_ev_ln_a_b, v_od_w_s, v_od_b_s, v_mlp_norm_g, v_final_norm_g,
               v_ev_conv_a_w, v_ev_conv_b_w, v_od_norm_g, v_od_b_in, v_od_ln_v_g, v_od_ln_v_b]
    small_g = rep_red + shard_red
    packed = [_pack(group)[None] for group in (small_w, small_m, small_v)]
    outs = _adamw(packed[0], packed[1], packed[2], [_pack(small_g)], "adamw_small")
    small_out_shapes = [w.shape for w in small_w]
    small_outs = [_unpack(o[0], small_out_shapes) for o in outs]
    for idx, nm in enumerate(small_names):
        upd[nm] = [small_outs[kind][idx] for kind in range(4)]

    order = ["ev_norm_g", "ev_w_in", "ev_conv_a_w", "ev_conv_a_b", "ev_ln_a_g", "ev_ln_a_b", "ev_conv_b_w", "ev_w_out",
             "od_norm_g", "od_w_in", "od_b_in", "od_ln_v_g", "od_ln_v_b", "od_w_s", "od_b_s", "od_w_out", "mlp_norm_g",
             "mlp_w1", "mlp_w2", "final_norm_g"]
    grad_x = jnp.reshape(dx, x.shape)
    return (loss, grad_x, *[upd[nm][0] for nm in order], *[upd[nm][1] for nm in order],
            *[upd[nm][2] for nm in order], *[upd[nm][3] for nm in order])
```

```python
import functools

import jax
import jax.numpy as jnp
from jax import lax
from jax.experimental import pallas as pl
from jax.experimental.pallas import tpu as pltpu

F32 = jnp.float32
BF16 = jnp.bfloat16

D_MODEL = 1024
A_DIM = 512
B_DIM = 512
IN_EVEN = 2 * A_DIM + 3 * B_DIM
A_CONV_WIDTH = 31
B_CONV_WIDTH = 3
CHUNK = 128
C_GROUPS = 8
C_DIM = 1024
D_FF = 4096
RMS_EPS = 1e-6
LN_EPS = 1e-5
ADAM_LR = 0.001
ADAM_B1 = 0.9
ADAM_B2 = 0.999
ADAM_EPS = 1e-08
ADAM_WD = 0.01
ADAM_STEP = 10

N_CHIPS = 4
N_DEV = 8
TOKEN_TILE = 512
A_HALO = 32
B_HALO = 8
CONV_ROWS = 16
PAIR = 2 * CHUNK
LANES = 128
SUBLANES = 8
MIB = 1024 * 1024
MESH = pl.DeviceIdType.MESH
ANY = pl.BlockSpec(memory_space=pl.ANY)


def _dot(a, b):
    return lax.dot_general(a, b, (((1,), (0,)), ((), ())), preferred_element_type=F32)


def _dot_nt(a, b):
    return lax.dot_general(a, b, (((1,), (1,)), ((), ())), preferred_element_type=F32)


def _dot_tn(a, b):
    return lax.dot_general(a, b, (((0,), (0,)), ((), ())), preferred_element_type=F32)


def _params(vmem_mib, n_axes=1):
    return pltpu.CompilerParams(dimension_semantics=("arbitrary",) * n_axes, vmem_limit_bytes=vmem_mib * MIB)


def _row_spec(tm, cols, rev_nt=None):
    if rev_nt is None:
        return pl.BlockSpec((tm, cols), lambda i: (i, 0))
    return pl.BlockSpec((tm, cols), lambda i: (rev_nt - 1 - i, 0))


def _full_spec(shape):
    nd = len(shape)
    return pl.BlockSpec(shape, lambda i: (0,) * nd)


def _block_rows(rows, cap=512):
    best = SUBLANES
    for br in range(SUBLANES, min(rows, cap) + 1, SUBLANES):
        if rows % br == 0:
            best = br
    return best


def _load(src, dst, sem):
    cp = pltpu.make_async_copy(src, dst, sem)
    cp.start()
    cp.wait()


def _rms_fwd(x, g):
    rstd = lax.rsqrt(jnp.mean(x * x, axis=-1, keepdims=True) + RMS_EPS)
    return x * rstd * g, rstd


def _rms_bwd(dn, x, rstd, g):
    a = dn * g
    xh = x * rstd
    dx = rstd * (a - xh * jnp.mean(a * xh, axis=-1, keepdims=True))
    dg = jnp.sum(dn * xh, axis=0, keepdims=True)
    return dx, dg


def _ln_stats(v):
    mu = jnp.mean(v, axis=-1, keepdims=True)
    xc = v - mu
    rs = lax.rsqrt(jnp.mean(xc * xc, axis=-1, keepdims=True) + LN_EPS)
    return xc * rs, rs


def _ln_bwd(dy, xhat, rs, g):
    dxh = dy * g
    dv = rs * (dxh - jnp.mean(dxh, axis=-1, keepdims=True) - xhat * jnp.mean(dxh * xhat, axis=-1, keepdims=True))
    return dv, jnp.sum(dy * xhat, axis=0, keepdims=True), jnp.sum(dy, axis=0, keepdims=True)


def _gelu_parts(s):
    cdf = 0.5 * (1.0 + lax.erf(s * 0.7071067811865476))
    return cdf, jnp.exp(-0.5 * s * s) * 0.3989422804014327


def _mesh_pos():
    return lax.axis_index("x"), lax.axis_index("y"), lax.axis_index("c")


def _other_chips(x, y):
    return [(1 - x, y), (x, 1 - y), (1 - x, 1 - y)]


def _remote(src, dst, send_sem, recv_sem, to):
    return pltpu.make_async_remote_copy(src_ref=src, dst_ref=dst, send_sem=send_sem, recv_sem=recv_sem,
                                        device_id=to, device_id_type=MESH)


def _like(arrays):
    return [jax.ShapeDtypeStruct(a.shape, a.dtype) for a in arrays]


class _Gather:
    def __init__(self, bufs):
        self.ins = list(bufs)
        self.out_shapes = _like(bufs)
        self.aliases = {t: t for t in range(len(bufs))}
        self.n_sems = 6 * len(bufs)

    def _ici(self, ins, outs, send, recv, t, k, chip, mine, c):
        return _remote(ins[t].at[mine, c], outs[t].at[mine, c], send.at[6 * t + k], recv.at[6 * t + k], (*chip, c))

    def start(self, ins, outs, send, recv):
        x, y, c = _mesh_pos()
        for t in range(len(ins)):
            for k, chip in enumerate(_other_chips(x, y)):
                self._ici(ins, outs, send, recv, t, k, chip, 2 * x + y, c).start()

    def finish(self, ins, outs, send, recv):
        x, y, c = _mesh_pos()
        me, sibling = (x, y, c), (x, y, 1 - c)
        chips = _other_chips(x, y)
        passed = []
        for t in range(len(ins)):
            for k, chip in enumerate(chips):
                blk = outs[t].at[2 * chip[0] + chip[1], c]
                _remote(blk, blk, send.at[6 * t + k], recv.at[6 * t + k], me).wait_recv()
                cp = _remote(blk, blk, send.at[6 * t + 3 + k], recv.at[6 * t + 3 + k], sibling)
                cp.start()
                passed.append(cp)
        for t in range(len(ins)):
            for k, chip in enumerate(chips):
                blk = outs[t].at[2 * chip[0] + chip[1], 1 - c]
                _remote(blk, blk, send.at[6 * t + 3 + k], recv.at[6 * t + 3 + k], me).wait_recv()
        for t in range(len(ins)):
            for k, chip in enumerate(chips):
                self._ici(ins, outs, send, recv, t, k, chip, 2 * x + y, c).wait_send()
        for cp in passed:
            cp.wait_send()


class _PairSwap:
    def __init__(self, grads):
        self.ins = list(grads)
        self.out_shapes = [jax.ShapeDtypeStruct((g.shape[0],) + g.shape[2:], g.dtype) for g in grads]
        self.aliases = {}
        self.n_sems = len(grads)

    def _copies(self, ins, outs, send, recv):
        x, y, c = _mesh_pos()
        return [_remote(ins[t].at[:, 1 - c], outs[t], send.at[t], recv.at[t], (x, y, 1 - c)) for t in range(len(ins))]

    def start(self, ins, outs, send, recv):
        for cp in self._copies(ins, outs, send, recv):
            cp.start()

    def finish(self, ins, outs, send, recv):
        for cp in self._copies(ins, outs, send, recv):
            cp.wait()


class _ChipSwap:
    def __init__(self, parts):
        self.ins = list(parts)
        self.out_shapes = [jax.ShapeDtypeStruct((3,) + p.shape[1:], p.dtype) for p in parts]
        self.aliases = {}
        self.n_sems = 3 * len(parts)

    def _copies(self, ins, outs, send, recv):
        x, y, c = _mesh_pos()
        return [_remote(ins[t].at[2 * chip[0] + chip[1]], outs[t].at[k], send.at[3 * t + k], recv.at[3 * t + k], (*chip, c))
                for t in range(len(ins)) for k, chip in enumerate(_other_chips(x, y))]

    def start(self, ins, outs, send, recv):
        for cp in self._copies(ins, outs, send, recv):
            cp.start()

    def finish(self, ins, outs, send, recv):
        for cp in self._copies(ins, outs, send, recv):
            cp.wait()


class _PairShare:
    def __init__(self, fulls):
        self.ins = list(fulls)
        self.out_shapes = _like(fulls)
        self.aliases = {t: t for t in range(len(fulls))}
        self.n_sems = len(fulls)

    def _copies(self, ins, outs, send, recv):
        x, y, c = _mesh_pos()
        return [_remote(ins[t].at[c], outs[t].at[c], send.at[t], recv.at[t], (x, y, 1 - c)) for t in range(len(ins))]

    def start(self, ins, outs, send, recv):
        for cp in self._copies(ins, outs, send, recv):
            cp.start()

    def finish(self, ins, outs, send, recv):
        for cp in self._copies(ins, outs, send, recv):
            cp.wait()


class _ShareAll:
    def __init__(self, buf):
        self.ins = [buf]
        self.out_shapes = _like([buf])
        self.aliases = {0: 0}
        self.n_sems = N_DEV - 1

    def _peers(self):
        x, y, c = _mesh_pos()
        flips = [((r >> 2) & 1, (r >> 1) & 1, r & 1) for r in range(1, N_DEV)]
        return (x, y, c), [(x ^ fx, y ^ fy, c ^ fc) for fx, fy, fc in flips]

    def start(self, ins, outs, send, recv):
        (x, y, c), peers = self._peers()
        mine = 4 * x + 2 * y + c
        for r, peer in enumerate(peers):
            _remote(ins[0].at[mine], outs[0].at[mine], send.at[r], recv.at[r], peer).start()

    def finish(self, ins, outs, send, recv):
        (x, y, c), peers = self._peers()
        mine = 4 * x + 2 * y + c
        for r, (px, py, pc) in enumerate(peers):
            blk = outs[0].at[4 * px + 2 * py + pc]
            _remote(blk, blk, send.at[r], recv.at[r], (x, y, c)).wait_recv()
        for r, peer in enumerate(peers):
            _remote(ins[0].at[mine], outs[0].at[mine], send.at[r], recv.at[r], peer).wait_send()


def _pallas(body, operands, *, name, grid, in_specs, out_specs, out_shape, scratch_shapes=(), vmem_mib=32, riders=()):
    in_specs, out_specs, out_shape, scratch_shapes = list(in_specs), list(out_specs), list(out_shape), list(scratch_shapes)
    if not riders:
        outs = pl.pallas_call(body, name=name, grid=grid, in_specs=in_specs, out_specs=out_specs, out_shape=out_shape,
                              scratch_shapes=scratch_shapes, compiler_params=_params(vmem_mib, len(grid)))(*operands)
        return list(outs), []
    n_in, n_out, n_scr = len(in_specs), len(out_specs), len(scratch_shapes)
    r_in = [len(r.ins) for r in riders]
    r_out = [len(r.out_shapes) for r in riders]
    steps = 1
    for g in grid:
        steps *= g

    def wrapped(*refs):
        refs = list(refs)
        ins, refs = refs[:n_in], refs[n_in:]
        rins = []
        for k in r_in:
            rins.append(refs[:k])
            refs = refs[k:]
        outs, refs = refs[:n_out], refs[n_out:]
        routs = []
        for k in r_out:
            routs.append(refs[:k])
            refs = refs[k:]
        scr, sems = refs[:n_scr], refs[n_scr:]
        step = 0
        for ax, g in enumerate(grid):
            step = step * g + pl.program_id(ax)

        def each(what):
            for j, r in enumerate(riders):
                getattr(r, what)(rins[j], routs[j], sems[2 * j], sems[2 * j + 1])

        if grid:
            pl.when(step == 0)(lambda: each("start"))
        else:
            each("start")
        body(*ins, *outs, *scr)
        if grid:
            pl.when(step == steps - 1)(lambda: each("finish"))
        else:
            each("finish")

    aliases, off_in, off_out = {}, n_in, n_out
    for r, ki, ko in zip(riders, r_in, r_out):
        for i, o in r.aliases.items():
            aliases[off_in + i] = off_out + o
        off_in, off_out = off_in + ki, off_out + ko
    sems = []
    for r in riders:
        sems += [pltpu.SemaphoreType.DMA((r.n_sems,)), pltpu.SemaphoreType.DMA((r.n_sems,))]
    res = pl.pallas_call(
        wrapped, name=name, grid=grid,
        in_specs=in_specs + [ANY] * sum(r_in), out_specs=out_specs + [ANY] * sum(r_out),
        out_shape=out_shape + [s for r in riders for s in r.out_shapes],
        scratch_shapes=scratch_shapes + sems, input_output_aliases=aliases,
        compiler_params=pltpu.CompilerParams(dimension_semantics=("arbitrary",) * len(grid),
                                             vmem_limit_bytes=vmem_mib * MIB, has_side_effects=True),
    )(*operands, *[a for r in riders for a in r.ins])
    res = list(res)
    outs, res = res[:n_out], res[n_out:]
    routs = []
    for k in r_out:
        routs.append(res[:k])
        res = res[k:]
    return outs, routs


def _exchange(riders, name):
    return _pallas(lambda: None, [], name=name, grid=(), in_specs=[], out_specs=[], out_shape=[], riders=riders)[1]


def _place_shard(w, layer, dtype, name):
    _, rows, cols = w.shape
    half = rows // 2
    br = _block_rows(half)
    nb = half // br
    mine = 2 * lax.axis_index("x") + lax.axis_index("y")

    def body(q_ref, w_ref, o_ref):
        o_ref[...] = w_ref[...].astype(dtype)

    return pl.pallas_call(
        body, name=name,
        grid_spec=pltpu.PrefetchScalarGridSpec(
            num_scalar_prefetch=1, grid=(2, nb),
            in_specs=[pl.BlockSpec((None, br, cols), lambda h, i, q: (layer, h * nb + i, 0))],
            out_specs=pl.BlockSpec((None, None, br, cols), lambda h, i, q: (q[0], h, i, 0))),
        out_shape=jax.ShapeDtypeStruct((N_CHIPS, 2, half, cols), dtype),
        compiler_params=_params(16, 2),
    )(jnp.reshape(mine, (1,)).astype(jnp.int32), w)


def _place_slab(v, name):
    r, cdim = v.shape
    x, y, c = _mesh_pos()

    def body(d_ref, v_ref, o_ref):
        o_ref[...] = v_ref[...]

    return pl.pallas_call(
        body, name=name,
        grid_spec=pltpu.PrefetchScalarGridSpec(
            num_scalar_prefetch=1, grid=(1,),
            in_specs=[pl.BlockSpec((r, cdim), lambda i, d: (0, 0))],
            out_specs=pl.BlockSpec((None, r, cdim), lambda i, d: (d[0], 0, 0))),
        out_shape=jax.ShapeDtypeStruct((N_DEV, r, cdim), v.dtype),
        compiler_params=_params(16, 1),
    )(jnp.reshape(4 * x + 2 * y + c, (1,)).astype(jnp.int32), v)


def _add_pair(g, recv, name):
    _, _, r, cdim = g.shape
    br = _block_rows(r, 256)
    c = lax.axis_index("c")

    def body(c_ref, g_ref, r_ref, o_ref):
        o_ref[...] = (g_ref[...] + r_ref[...]).astype(BF16)

    return pl.pallas_call(
        body, name=name,
        grid_spec=pltpu.PrefetchScalarGridSpec(
            num_scalar_prefetch=1, grid=(N_CHIPS, r // br),
            in_specs=[pl.BlockSpec((None, None, br, cdim), lambda q, i, c_ref: (q, c_ref[0], i, 0)),
                      pl.BlockSpec((None, br, cdim), lambda q, i, c_ref: (q, i, 0))],
            out_specs=pl.BlockSpec((None, br, cdim), lambda q, i, c_ref: (q, i, 0))),
        out_shape=jax.ShapeDtypeStruct((N_CHIPS, r, cdim), BF16),
        compiler_params=_params(16, 2),
    )(jnp.reshape(c, (1,)).astype(jnp.int32), g, recv)


def _add_chips(own, recv, name):
    _, r, cdim = own.shape
    br = _block_rows(r, 256)
    x, y, c = _mesh_pos()

    def body(pos_ref, own_ref, r_ref, o_ref):
        acc = own_ref[...].astype(F32)
        for k in range(3):
            acc = acc + r_ref[k].astype(F32)
        o_ref[...] = acc

    return pl.pallas_call(
        body, name=name,
        grid_spec=pltpu.PrefetchScalarGridSpec(
            num_scalar_prefetch=1, grid=(r // br,),
            in_specs=[pl.BlockSpec((None, br, cdim), lambda i, pos: (pos[0], i, 0)),
                      pl.BlockSpec((3, br, cdim), lambda i, pos: (0, i, 0))],
            out_specs=pl.BlockSpec((None, br, cdim), lambda i, pos: (pos[1], i, 0))),
        out_shape=jax.ShapeDtypeStruct((2, r, cdim), F32),
        compiler_params=_params(16, 1),
    )(jnp.stack([2 * x + y, c]).astype(jnp.int32), own, recv)


def _sum_devices(v8):
    _, r, cdim = v8.shape

    def body(v_ref, o_ref):
        acc = v_ref[0]
        for d in range(1, N_DEV):
            acc = acc + v_ref[d]
        o_ref[...] = acc

    return pl.pallas_call(
        body, name="small_grad_sum",
        in_specs=[pl.BlockSpec(memory_space=pltpu.VMEM)], out_specs=pl.BlockSpec(memory_space=pltpu.VMEM),
        out_shape=jax.ShapeDtypeStruct((r, cdim), F32),
        compiler_params=pltpu.CompilerParams(vmem_limit_bytes=32 * MIB),
    )(v8)


def _adamw(w, m, v, grads, name, riders=()):
    layers, r, cdim = w.shape
    br = _block_rows(r, 256 if cdim > LANES else 1024)
    c1 = 1.0 / (1.0 - ADAM_B1 ** ADAM_STEP)
    c2 = 1.0 / (1.0 - ADAM_B2 ** ADAM_STEP)

    def body(*refs):
        w_ref, m_ref, v_ref = refs[:3]
        g_refs = refs[3:3 + layers]
        go_ref, d_ref, mo_ref, vo_ref = refs[3 + layers:]
        layer = pl.program_id(0)
        for l in range(layers):
            @pl.when(layer == l)
            def _(l=l):
                g = g_refs[l][...]
                m_new = ADAM_B1 * m_ref[...] + (1.0 - ADAM_B1) * g
                v_new = ADAM_B2 * v_ref[...] + (1.0 - ADAM_B2) * (g * g)
                go_ref[...] = g
                mo_ref[...] = m_new
                vo_ref[...] = v_new
                d_ref[...] = -ADAM_LR * ((m_new * c1) / (jnp.sqrt(v_new * c2) + ADAM_EPS) + ADAM_WD * w_ref[...])

    spec3 = pl.BlockSpec((None, br, cdim), lambda l, i: (l, i, 0))
    spec2 = pl.BlockSpec((br, cdim), lambda l, i: (i, 0))
    out = jax.ShapeDtypeStruct((layers, r, cdim), F32)
    return _pallas(body, [w, m, v, *grads], name=name, grid=(layers, r // br),
                   in_specs=[spec3, spec3, spec3] + [spec2] * layers, out_specs=[spec3] * 4, out_shape=[out] * 4,
                   vmem_mib=32, riders=riders)


def _conv31(src, w_ref, rows, base, init):
    acc = init
    for k in range(A_CONV_WIDTH):
        acc = acc + w_ref[k:k + 1, :] * src[base + k + rows.start:base + k + rows.stop, :]
    return acc


def _fwd_even(x, norm_g, w_in, conv_a_w, conv_a_b, ln_g, ln_b, conv_b_w, w_out, *, tm, seq, riders=()):
    tokens = x.shape[0]
    nt, tps = tokens // tm, seq // tm

    def body(x_ref, g_ref, win_hbm, caw_ref, cab_ref, lng_ref, lnb_ref, cbw_ref, wout_hbm,
             h_ref, n_ref, z_ref, a2_ref, cv_ref, mix_ref, win_v, wout_v, pa, pb, sem):
        i = pl.program_id(0)

        @pl.when(i == 0)
        def _():
            _load(win_hbm, win_v, sem)
            _load(wout_hbm, wout_v, sem)

        xv = x_ref[...]
        nf, _ = _rms_fwd(xv, g_ref[...])
        n = nf.astype(BF16)
        n_ref[...] = n
        z = jnp.concatenate([_dot(n, win_v[j]) for j in range(N_CHIPS)], axis=1)
        z_ref[...] = z.astype(BF16)
        a_val, a_gate = z[:, 0:A_DIM], z[:, A_DIM:2 * A_DIM]
        b_gate, c_gate, b_val = z[:, 1024:1536], z[:, 1536:2048], z[:, 2048:2560]

        first = (i % tps) == 0

        @pl.when(first)
        def _():
            pa[0:A_HALO, :] = jnp.zeros((A_HALO, A_DIM), F32)
            pb[0:B_HALO, :] = jnp.zeros((B_HALO, B_DIM), F32)

        @pl.when(jnp.logical_not(first))
        def _():
            pa[0:A_HALO, :] = pa[tm:tm + A_HALO, :]
            pb[0:B_HALO, :] = pb[tm:tm + B_HALO, :]

        pa[A_HALO:A_HALO + tm, :] = a_val * jax.nn.sigmoid(a_gate)
        pb[B_HALO:B_HALO + tm, :] = c_gate * b_val
        bias = jnp.broadcast_to(cab_ref[...], (CONV_ROWS, A_DIM))
        for r0 in range(0, tm, CONV_ROWS):
            rows = slice(r0, r0 + CONV_ROWS)
            a2_ref[rows, :] = _conv31(pa, caw_ref, rows, A_HALO - (A_CONV_WIDTH - 1), bias)
        xhat, _ = _ln_stats(a2_ref[...])
        a3 = xhat * lng_ref[...] + lnb_ref[...]
        a4 = a3 * jax.nn.sigmoid(a3)
        cv = cbw_ref[0:1, :] * pb[B_HALO - 2:B_HALO - 2 + tm, :]
        cv = cv + cbw_ref[1:2, :] * pb[B_HALO - 1:B_HALO - 1 + tm, :]
        cv = cv + cbw_ref[2:3, :] * pb[B_HALO:B_HALO + tm, :]
        cv_ref[...] = cv.astype(BF16)
        mix = jnp.concatenate([a4, b_gate * cv], axis=1).astype(BF16)
        mix_ref[...] = mix
        h_ref[...] = xv + _dot(mix, wout_v[...])

    shp = lambda cols, dt: jax.ShapeDtypeStruct((tokens, cols), dt)
    return _pallas(
        body, [x, norm_g, w_in, conv_a_w, conv_a_b, ln_g, ln_b, conv_b_w, w_out], name="fwd_even", grid=(nt,),
        in_specs=[_row_spec(tm, D_MODEL), _full_spec((1, D_MODEL)), ANY, _full_spec((A_CONV_WIDTH, A_DIM)),
                  _full_spec((1, A_DIM)), _full_spec((1, A_DIM)), _full_spec((1, A_DIM)),
                  _full_spec((B_CONV_WIDTH, B_DIM)), ANY],
        out_specs=[_row_spec(tm, D_MODEL), _row_spec(tm, D_MODEL), _row_spec(tm, IN_EVEN), _row_spec(tm, A_DIM),
                   _row_spec(tm, B_DIM), _row_spec(tm, D_MODEL)],
        out_shape=[shp(D_MODEL, F32), shp(D_MODEL, BF16), shp(IN_EVEN, BF16), shp(A_DIM, F32), shp(B_DIM, BF16),
                   shp(D_MODEL, BF16)],
        scratch_shapes=[pltpu.VMEM((N_CHIPS, D_MODEL, IN_EVEN // N_CHIPS), BF16), pltpu.VMEM((D_MODEL, D_MODEL), BF16),
                        pltpu.VMEM((A_HALO + tm, A_DIM), F32), pltpu.VMEM((B_HALO + tm, B_DIM), F32),
                        pltpu.SemaphoreType.DMA],
        vmem_mib=56, riders=riders)


def _fwd_mlp(h, norm_g, w1, w2, layer, *, tm, riders=()):
    tokens = h.shape[0]
    nt = tokens // tm
    fs = D_FF // N_CHIPS

    def body(h_ref, g_ref, w1_hbm, w2_hbm, ho_ref, n_ref, p_ref, q_ref, w1_v, w2_v, sem):
        @pl.when(pl.program_id(0) == 0)
        def _():
            _load(w1_hbm, w1_v, sem)
            _load(w2_hbm, w2_v, sem)

        xv = h_ref[...]
        nf, _ = _rms_fwd(xv, g_ref[...])
        n = nf.astype(BF16)
        n_ref[...] = n
        acc = xv
        for j in range(N_CHIPS):
            p = _dot(n, w1_v[j])
            p_ref[:, j * fs:(j + 1) * fs] = p.astype(BF16)
            r = jnp.maximum(p, 0.0)
            q = (r * r).astype(BF16)
            q_ref[:, j * fs:(j + 1) * fs] = q
            acc = acc + _dot(q, w2_v[j])
        ho_ref[...] = acc

    shp = lambda cols, dt: jax.ShapeDtypeStruct((tokens, cols), dt)
    return _pallas(
        body, [h, norm_g, w1, w2], name=f"fwd_mlp{layer}", grid=(nt,),
        in_specs=[_row_spec(tm, D_MODEL), _full_spec((1, D_MODEL)), ANY, ANY],
        out_specs=[_row_spec(tm, D_MODEL), _row_spec(tm, D_MODEL), _row_spec(tm, D_FF), _row_spec(tm, D_FF)],
        out_shape=[shp(D_MODEL, F32), shp(D_MODEL, BF16), shp(D_FF, BF16), shp(D_FF, BF16)],
        scratch_shapes=[pltpu.VMEM((N_CHIPS, D_MODEL, fs), BF16), pltpu.VMEM((N_CHIPS, fs, D_MODEL), BF16),
                        pltpu.SemaphoreType.DMA],
        vmem_mib=56, riders=riders)


def _tril_mask():
    row = lax.broadcasted_iota(jnp.int32, (CHUNK, CHUNK), 0)
    col = lax.broadcasted_iota(jnp.int32, (CHUNK, CHUNK), 1)
    return row >= col


def _triu_mask():
    row = lax.broadcasted_iota(jnp.int32, (CHUNK, CHUNK), 0)
    col = lax.broadcasted_iota(jnp.int32, (CHUNK, CHUNK), 1)
    return row <= col


def _fwd_odd(h, norm_g, w_in, b_in, ln_g, ln_b, w_s, b_s_rows, w_out, *, tm, riders=()):
    tokens = h.shape[0]
    nt = tokens // tm
    cs = 2 * C_DIM // N_CHIPS

    def body(h_ref, g_ref, win_hbm, bin_ref, lng_ref, lnb_ref, ws_ref, bs_ref, wout_hbm,
             ho_ref, n_ref, s_ref, sv_ref, y_ref, win_v, wout_v, bd, sem):
        @pl.when(pl.program_id(0) == 0)
        def _():
            _load(win_hbm, win_v, sem)
            _load(wout_hbm, wout_v, sem)
            mask = _tril_mask()
            bd[...] = jnp.zeros(bd.shape, BF16)
            for g in range(C_GROUPS):
                w = jnp.where(mask, ws_ref[g], 0.0).astype(BF16)
                bd[g, 0:CHUNK, 0:CHUNK] = w
                bd[g, CHUNK:PAIR, CHUNK:PAIR] = w

        xv = h_ref[...]
        nf, _ = _rms_fwd(xv, g_ref[...])
        n = nf.astype(BF16)
        n_ref[...] = n
        s = jnp.concatenate([_dot(n, win_v[j]) for j in range(N_CHIPS)], axis=1) + bin_ref[...]
        s_ref[...] = s.astype(BF16)
        cdf, _ = _gelu_parts(s)
        zz = s * cdf
        u, v = zz[:, 0:C_DIM], zz[:, C_DIM:2 * C_DIM]
        xhat, _ = _ln_stats(v)
        vn = (xhat * lng_ref[...] + lnb_ref[...]).astype(BF16)
        for g in range(C_GROUPS):
            cols = slice(g * CHUNK, (g + 1) * CHUNK)
            bias = jnp.concatenate([bs_ref[g], bs_ref[g]], axis=0)
            for r0 in range(0, tm, PAIR):
                sv = _dot(bd[g], vn[r0:r0 + PAIR, cols]) + bias
                sv_ref[r0:r0 + PAIR, cols] = sv.astype(BF16)
                y_ref[r0:r0 + PAIR, cols] = (u[r0:r0 + PAIR, cols] * sv).astype(BF16)
        ho_ref[...] = xv + _dot(y_ref[...], wout_v[...])

    shp = lambda cols, dt: jax.ShapeDtypeStruct((tokens, cols), dt)
    return _pallas(
        body, [h, norm_g, w_in, b_in, ln_g, ln_b, w_s, b_s_rows, w_out], name="fwd_odd", grid=(nt,),
        in_specs=[_row_spec(tm, D_MODEL), _full_spec((1, D_MODEL)), ANY, _full_spec((1, 2 * C_DIM)),
                  _full_spec((1, C_DIM)), _full_spec((1, C_DIM)), _full_spec((C_GROUPS, CHUNK, CHUNK)),
                  _full_spec((C_GROUPS, CHUNK, CHUNK)), ANY],
        out_specs=[_row_spec(tm, D_MODEL), _row_spec(tm, D_MODEL), _row_spec(tm, 2 * C_DIM), _row_spec(tm, C_DIM),
                   _row_spec(tm, C_DIM)],
        out_shape=[shp(D_MODEL, F32), shp(D_MODEL, BF16), shp(2 * C_DIM, BF16), shp(C_DIM, BF16), shp(C_DIM, BF16)],
        scratch_shapes=[pltpu.VMEM((N_CHIPS, D_MODEL, cs), BF16), pltpu.VMEM((C_DIM, D_MODEL), BF16),
                        pltpu.VMEM((C_GROUPS, PAIR, PAIR), BF16), pltpu.SemaphoreType.DMA],
        vmem_mib=56, riders=riders)


def _loss_head(h, norm_g, target, *, tm):
    tokens = h.shape[0]
    nt = tokens // tm

    def body(h_ref, g_ref, t_ref, loss_ref, dh_ref, dg_ref):
        @pl.when(pl.program_id(0) == 0)
        def _():
            loss_ref[...] = jnp.zeros((1, 1), F32)
            dg_ref[...] = jnp.zeros((1, D_MODEL), F32)

        xv = h_ref[...]
        g = g_ref[...]
        out, rstd = _rms_fwd(xv, g)
        err = out - t_ref[...]
        per_token = jnp.sum(err * err, axis=1, keepdims=True) * (1.0 / D_MODEL)
        loss_ref[...] += 0.5 * jnp.sum(per_token, axis=0, keepdims=True)
        dx, dg = _rms_bwd(err * (1.0 / D_MODEL), xv, rstd, g)
        dh_ref[...] = dx
        dg_ref[...] += dg

    return _pallas(
        body, [h, norm_g, target], name="loss_head", grid=(nt,),
        in_specs=[_row_spec(tm, D_MODEL), _full_spec((1, D_MODEL)), _row_spec(tm, D_MODEL)],
        out_specs=[_full_spec((1, 1)), _row_spec(tm, D_MODEL), _full_spec((1, D_MODEL))],
        out_shape=[jax.ShapeDtypeStruct((1, 1), F32), jax.ShapeDtypeStruct((tokens, D_MODEL), F32),
                   jax.ShapeDtypeStruct((1, D_MODEL), F32)],
        vmem_mib=32)[0]


def _bwd_mlp(dh, h, norm_g, p, w1, w2, layer, *, tm, riders=()):
    tokens = h.shape[0]
    nt = tokens // tm
    fs = D_FF // N_CHIPS

    def body(dh_ref, h_ref, g_ref, p_ref, w1_hbm, w2_hbm, dx_ref, dp_ref, dg_ref, w1_v, w2_v, sem):
        @pl.when(pl.program_id(0) == 0)
        def _():
            _load(w1_hbm, w1_v, sem)
            _load(w2_hbm, w2_v, sem)
            dg_ref[...] = jnp.zeros((1, D_MODEL), F32)

        dhv = dh_ref[...]
        dhb = dhv.astype(BF16)
        dn = jnp.zeros((tm, D_MODEL), F32)
        for j in range(N_CHIPS):
            dq = _dot_nt(dhb, w2_v[j])
            r = jnp.maximum(p_ref[:, j * fs:(j + 1) * fs].astype(F32), 0.0)
            dp = ((2.0 * r) * dq).astype(BF16)
            dp_ref[:, j * fs:(j + 1) * fs] = dp
            dn = dn + _dot_nt(dp, w1_v[j])
        xv = h_ref[...]
        g = g_ref[...]
        _, rstd = _rms_fwd(xv, g)
        dx, dg = _rms_bwd(dn, xv, rstd, g)
        dx_ref[...] = dhv + dx
        dg_ref[...] += dg

    return _pallas(
        body, [dh, h, norm_g, p, w1, w2], name=f"bwd_mlp{layer}", grid=(nt,),
        in_specs=[_row_spec(tm, D_MODEL), _row_spec(tm, D_MODEL), _full_spec((1, D_MODEL)), _row_spec(tm, D_FF), ANY, ANY],
        out_specs=[_row_spec(tm, D_MODEL), _row_spec(tm, D_FF), _full_spec((1, D_MODEL))],
        out_shape=[jax.ShapeDtypeStruct((tokens, D_MODEL), F32), jax.ShapeDtypeStruct((tokens, D_FF), BF16),
                   jax.ShapeDtypeStruct((1, D_MODEL), F32)],
        scratch_shapes=[pltpu.VMEM((N_CHIPS, D_MODEL, fs), BF16), pltpu.VMEM((N_CHIPS, fs, D_MODEL), BF16),
                        pltpu.SemaphoreType.DMA],
        vmem_mib=56, riders=riders)


def _bwd_odd(dh, h, norm_g, s, sv, w_in, ln_g, ln_b, w_s, w_out, *, tm, riders=()):
    tokens = h.shape[0]
    nt = tokens // tm
    cs = 2 * C_DIM // N_CHIPS

    def body(dh_ref, h_ref, g_ref, s_ref, sv_ref, win_hbm, lng_ref, lnb_ref, ws_ref, wout_hbm,
             dx_ref, ds_ref, dg_ref, dbin_ref, dlng_ref, dlnb_ref, dws_ref, dbs_ref,
             win_v, wout_v, bdt, dws_acc, dbs_acc, dvn, sem):
        i = pl.program_id(0)

        @pl.when(i == 0)
        def _():
            _load(win_hbm, win_v, sem)
            _load(wout_hbm, wout_v, sem)
            mask_t = _triu_mask()
            bdt[...] = jnp.zeros(bdt.shape, BF16)
            for g in range(C_GROUPS):
                wt = jnp.where(mask_t, ws_ref[g].T, 0.0).astype(BF16)
                bdt[g, 0:CHUNK, 0:CHUNK] = wt
                bdt[g, CHUNK:PAIR, CHUNK:PAIR] = wt
            dws_acc[...] = jnp.zeros(dws_acc.shape, F32)
            dbs_acc[...] = jnp.zeros(dbs_acc.shape, F32)
            dg_ref[...] = jnp.zeros(dg_ref.shape, F32)
            dbin_ref[...] = jnp.zeros(dbin_ref.shape, F32)
            dlng_ref[...] = jnp.zeros(dlng_ref.shape, F32)
            dlnb_ref[...] = jnp.zeros(dlnb_ref.shape, F32)

        dhv = dh_ref[...]
        dy = _dot_nt(dhv.astype(BF16), wout_v[...])
        sf = s_ref[...].astype(F32)
        cdf, pdf = _gelu_parts(sf)
        zz = sf * cdf
        dgelu = cdf + sf * pdf
        u, v = zz[:, 0:C_DIM], zz[:, C_DIM:2 * C_DIM]
        xhat, rs = _ln_stats(v)
        lng = lng_ref[...]
        vn = (xhat * lng + lnb_ref[...]).astype(BF16)
        du = dy * sv_ref[...].astype(F32)
        dsv = dy * u
        dsvb = dsv.astype(BF16)
        for g in range(C_GROUPS):
            cols = slice(g * CHUNK, (g + 1) * CHUNK)
            for r0 in range(0, tm, PAIR):
                blk = dsvb[r0:r0 + PAIR, cols]
                dvn[r0:r0 + PAIR, cols] = _dot(bdt[g], blk)
                dws_acc[g] += _dot_nt(blk, vn[r0:r0 + PAIR, cols])
                dbs_acc[g] += dsv[r0:r0 + CHUNK, cols] + dsv[r0 + CHUNK:r0 + PAIR, cols]
        dv, dlng, dlnb = _ln_bwd(dvn[...], xhat, rs, lng)
        dlng_ref[...] += dlng
        dlnb_ref[...] += dlnb
        ds = jnp.concatenate([du, dv], axis=1) * dgelu
        dbin_ref[...] += jnp.sum(ds, axis=0, keepdims=True)
        dsb = ds.astype(BF16)
        ds_ref[...] = dsb
        dn = jnp.zeros((tm, D_MODEL), F32)
        for j in range(N_CHIPS):
            dn = dn + _dot_nt(dsb[:, j * cs:(j + 1) * cs], win_v[j])
        xv = h_ref[...]
        g = g_ref[...]
        _, rstd = _rms_fwd(xv, g)
        dx, dg = _rms_bwd(dn, xv, rstd, g)
        dx_ref[...] = dhv + dx
        dg_ref[...] += dg

        @pl.when(i == nt - 1)
        def _():
            mask = _tril_mask()
            for g in range(C_GROUPS):
                full = dws_acc[g]
                dws_ref[g] = jnp.where(mask, full[0:CHUNK, 0:CHUNK] + full[CHUNK:PAIR, CHUNK:PAIR], 0.0)
                dbs_ref[g] = jnp.sum(dbs_acc[g], axis=1, keepdims=True)

    row = lambda cols: jax.ShapeDtypeStruct((1, cols), F32)
    return _pallas(
        body, [dh, h, norm_g, s, sv, w_in, ln_g, ln_b, w_s, w_out], name="bwd_odd", grid=(nt,),
        in_specs=[_row_spec(tm, D_MODEL), _row_spec(tm, D_MODEL), _full_spec((1, D_MODEL)), _row_spec(tm, 2 * C_DIM),
                  _row_spec(tm, C_DIM), ANY, _full_spec((1, C_DIM)), _full_spec((1, C_DIM)),
                  _full_spec((C_GROUPS, CHUNK, CHUNK)), ANY],
        out_specs=[_row_spec(tm, D_MODEL), _row_spec(tm, 2 * C_DIM), _full_spec((1, D_MODEL)), _full_spec((1, 2 * C_DIM)),
                   _full_spec((1, C_DIM)), _full_spec((1, C_DIM)), _full_spec((C_GROUPS, CHUNK, CHUNK)),
                   _full_spec((C_GROUPS, CHUNK, 1))],
        out_shape=[jax.ShapeDtypeStruct((tokens, D_MODEL), F32), jax.ShapeDtypeStruct((tokens, 2 * C_DIM), BF16),
                   row(D_MODEL), row(2 * C_DIM), row(C_DIM), row(C_DIM),
                   jax.ShapeDtypeStruct((C_GROUPS, CHUNK, CHUNK), F32), jax.ShapeDtypeStruct((C_GROUPS, CHUNK, 1), F32)],
        scratch_shapes=[pltpu.VMEM((N_CHIPS, D_MODEL, cs), BF16), pltpu.VMEM((C_DIM, D_MODEL), BF16),
                        pltpu.VMEM((C_GROUPS, PAIR, PAIR), BF16), pltpu.VMEM((C_GROUPS, PAIR, PAIR), F32),
                        pltpu.VMEM((C_GROUPS, CHUNK, CHUNK), F32), pltpu.VMEM((tm, C_DIM), F32),
                        pltpu.SemaphoreType.DMA],
        vmem_mib=56, riders=riders)


def _bwd_even(dh, x, norm_g, z, a2, cv, w_in, conv_a_w, ln_g, ln_b, conv_b_w, w_out, *, tm, seq, riders=()):
    tokens = x.shape[0]
    nt, tps = tokens // tm, seq // tm
    ws = IN_EVEN // N_CHIPS

    def body(dh_ref, x_ref, g_ref, z_ref, a2_ref, cv_ref, win_hbm, caw_ref, lng_ref, lnb_ref, cbw_ref, wout_hbm,
             dx_ref, dz_ref, dg_ref, dcaw_ref, dcab_ref, dlng_ref, dlnb_ref, dcbw_ref,
             win_v, wout_v, ea, eb, a1s, da1s, dw_acc, sem):
        i = pl.program_id(0)

        @pl.when(i == 0)
        def _():
            _load(win_hbm, win_v, sem)
            _load(wout_hbm, wout_v, sem)
            dw_acc[...] = jnp.zeros(dw_acc.shape, F32)
            for ref in (dg_ref, dcab_ref, dlng_ref, dlnb_ref, dcbw_ref):
                ref[...] = jnp.zeros(ref.shape, F32)

        dhv = dh_ref[...]
        dmix = _dot_nt(dhv.astype(BF16), wout_v[...])
        da4, dbo = dmix[:, 0:A_DIM], dmix[:, A_DIM:A_DIM + B_DIM]
        zf = z_ref[...].astype(F32)
        a_val, a_gate = zf[:, 0:A_DIM], zf[:, A_DIM:2 * A_DIM]
        b_gate, c_gate, b_val = zf[:, 1024:1536], zf[:, 1536:2048], zf[:, 2048:2560]

        xhat, rs = _ln_stats(a2_ref[...])
        lng = lng_ref[...]
        a3 = xhat * lng + lnb_ref[...]
        sg = jax.nn.sigmoid(a3)
        da3 = da4 * (sg * (1.0 + a3 * (1.0 - sg)))
        da2, dlng, dlnb = _ln_bwd(da3, xhat, rs, lng)
        dlng_ref[...] += dlng
        dlnb_ref[...] += dlnb
        dcab_ref[...] += jnp.sum(da2, axis=0, keepdims=True)

        last = ((nt - 1 - i) % tps) == tps - 1
        dcv = dbo * b_gate

        @pl.when(last)
        def _():
            ea[tm:tm + A_HALO, :] = jnp.zeros((A_HALO, A_DIM), F32)
            eb[tm:tm + B_HALO, :] = jnp.zeros((B_HALO, B_DIM), F32)

        @pl.when(jnp.logical_not(last))
        def _():
            ea[tm:tm + A_HALO, :] = ea[0:A_HALO, :]
            eb[tm:tm + B_HALO, :] = eb[0:B_HALO, :]

        ea[0:tm, :] = da2
        eb[0:tm, :] = dcv
        sig = jax.nn.sigmoid(a_gate)
        a1s[...] = a_val * sig
        for r0 in range(0, tm, CONV_ROWS):
            a1c = a1s[r0:r0 + CONV_ROWS, :]
            acc = jnp.zeros((CONV_ROWS, A_DIM), F32)
            for j in range(A_CONV_WIDTH):
                k = A_CONV_WIDTH - 1 - j
                sl = ea[r0 + j:r0 + j + CONV_ROWS, :]
                acc = acc + caw_ref[k:k + 1, :] * sl
                dw_acc[k] += sl * a1c
            da1s[r0:r0 + CONV_ROWS, :] = acc
        da1 = da1s[...]
        da_val = da1 * sig
        da_gate = da1 * a_val * (sig * (1.0 - sig))

        db_gate = dbo * cv_ref[...].astype(F32)
        cb = c_gate * b_val
        dcb = jnp.zeros((tm, B_DIM), F32)
        for j in range(B_CONV_WIDTH):
            k = B_CONV_WIDTH - 1 - j
            sl = eb[j:j + tm, :]
            dcb = dcb + cbw_ref[k:k + 1, :] * sl
            dcbw_ref[k:k + 1, :] += jnp.sum(sl * cb, axis=0, keepdims=True)
        dz = jnp.concatenate([da_val, da_gate, db_gate, dcb * b_val, dcb * c_gate], axis=1).astype(BF16)
        dz_ref[...] = dz
        dn = jnp.zeros((tm, D_MODEL), F32)
        for j in range(N_CHIPS):
            dn = dn + _dot_nt(dz[:, j * ws:(j + 1) * ws], win_v[j])
        xv = x_ref[...]
        g = g_ref[...]
        _, rstd = _rms_fwd(xv, g)
        dx, dg = _rms_bwd(dn, xv, rstd, g)
        dx_ref[...] = dhv + dx
        dg_ref[...] += dg

        @pl.when(i == nt - 1)
        def _():
            for k in range(A_CONV_WIDTH):
                dcaw_ref[k:k + 1, :] = jnp.sum(dw_acc[k], axis=0, keepdims=True)

    row = lambda cols: jax.ShapeDtypeStruct((1, cols), F32)
    rs_ = functools.partial(_row_spec, rev_nt=nt)
    return _pallas(
        body, [dh, x, norm_g, z, a2, cv, w_in, conv_a_w, ln_g, ln_b, conv_b_w, w_out], name="bwd_even", grid=(nt,),
        in_specs=[rs_(tm, D_MODEL), rs_(tm, D_MODEL), _full_spec((1, D_MODEL)), rs_(tm, IN_EVEN), rs_(tm, A_DIM),
                  rs_(tm, B_DIM), ANY, _full_spec((A_CONV_WIDTH, A_DIM)), _full_spec((1, A_DIM)), _full_spec((1, A_DIM)),
                  _full_spec((B_CONV_WIDTH, B_DIM)), ANY],
        out_specs=[rs_(tm, D_MODEL), rs_(tm, IN_EVEN), _full_spec((1, D_MODEL)), _full_spec((A_CONV_WIDTH, A_DIM)),
                   _full_spec((1, A_DIM)), _full_spec((1, A_DIM)), _full_spec((1, A_DIM)), _full_spec((B_CONV_WIDTH, B_DIM))],
        out_shape=[jax.ShapeDtypeStruct((tokens, D_MODEL), F32), jax.ShapeDtypeStruct((tokens, IN_EVEN), BF16),
                   row(D_MODEL), jax.ShapeDtypeStruct((A_CONV_WIDTH, A_DIM), F32), row(A_DIM), row(A_DIM), row(A_DIM),
                   jax.ShapeDtypeStruct((B_CONV_WIDTH, B_DIM), F32)],
        scratch_shapes=[pltpu.VMEM((N_CHIPS, D_MODEL, ws), BF16), pltpu.VMEM((D_MODEL, D_MODEL), BF16),
                        pltpu.VMEM((tm + A_HALO, A_DIM), F32), pltpu.VMEM((tm + B_HALO, B_DIM), F32),
                        pltpu.VMEM((tm, A_DIM), F32), pltpu.VMEM((tm, A_DIM), F32),
                        pltpu.VMEM((A_CONV_WIDTH, CONV_ROWS, A_DIM), F32), pltpu.SemaphoreType.DMA],
        vmem_mib=56, riders=riders)


def _wgrad(a, b, name, *, col_shards, riders=()):
    tokens, m = a.shape
    n = b.shape[1]
    kc = 512
    if col_shards:
        bm, bn = m // 2, n // N_CHIPS
        grid = (2, N_CHIPS)
        out_spec = pl.BlockSpec((None, None, bm, bn), lambda i, j: (j, i, 0, 0))
    else:
        bm, bn = m // 8, n
        grid = (8, 1)
        out_spec = pl.BlockSpec((None, None, bm, bn), lambda i, j: (i // 2, i % 2, 0, 0))

    def body(a_ref, b_ref, o_ref):
        acc = jnp.zeros((bm, bn), F32)
        for k0 in range(0, tokens, kc):
            acc = acc + _dot_tn(a_ref[k0:k0 + kc, :].astype(BF16), b_ref[k0:k0 + kc, :].astype(BF16))
        o_ref[...] = acc

    outs, routs = _pallas(
        body, [a, b], name=name, grid=grid,
        in_specs=[pl.BlockSpec((tokens, bm), lambda i, j: (0, i)), pl.BlockSpec((tokens, bn), lambda i, j: (0, j))],
        out_specs=[out_spec], out_shape=[jax.ShapeDtypeStruct((N_CHIPS, 2, bm, bn), F32)],
        vmem_mib=56, riders=riders)
    return outs[0], routs


class _GradReduce:
    def __init__(self, name, grad):
        self.name, self.grad = name, grad
        self.from_sibling = self.chip_sum = self.from_chips = self.full = None

    def pair_swap(self):
        return _PairSwap([self.grad])

    def took_pair(self, outs):
        self.chip_sum = _add_pair(self.grad, outs[0], f"pair_sum_{self.name}")

    def chip_swap(self):
        return _ChipSwap([self.chip_sum])

    def took_chips(self, outs):
        self.full = _add_chips(self.chip_sum, outs[0], f"chip_sum_{self.name}")

    def pair_share(self):
        return _PairShare([self.full])

    def took_share(self, outs):
        self.full = outs[0]

    def reduced(self):
        return jnp.reshape(self.full, (2 * self.full.shape[1], self.full.shape[2]))


def _forward_backward(x2, tgt2, gathered, staged, conv_a_w, conv_b_w, od_norm, od_bias, od_lng, od_lnb,
                      ev_norm_g, ev_conv_a_b, ev_ln_a_g, ev_ln_a_b, od_w_s, od_b_s, mlp_norm_g, final_norm_g,
                      *, tm, seq, distributed=True):
    d = x2.shape[1]
    w = dict(gathered)
    b_s_rows = jnp.broadcast_to(od_b_s[0][:, :, None], (C_GROUPS, CHUNK, CHUNK))

    def ride(*names):
        return [_Gather([staged[nm] for nm in names])] if distributed else []

    def land(routs, *names):
        if distributed:
            for nm, buf in zip(names, routs[0]):
                w[nm] = buf

    def as_cols(buf):
        return jnp.reshape(buf, (N_CHIPS, 2 * buf.shape[2], buf.shape[3]))

    def as_rows(buf):
        return jnp.reshape(buf, (8 * buf.shape[2], buf.shape[3]))

    (h1, n0, z, a2, cv, mix), routs = _fwd_even(
        x2, ev_norm_g, as_cols(w["ev_in"]), conv_a_w, ev_conv_a_b, ev_ln_a_g, ev_ln_a_b, conv_b_w, as_rows(w["ev_out"]),
        tm=tm, seq=seq, riders=ride("w1_0", "w2_0"))
    land(routs, "w1_0", "w2_0")
    (h2, n1, p0, q0), routs = _fwd_mlp(h1, mlp_norm_g[0:1], as_cols(w["w1_0"]), as_cols(w["w2_0"]), 0, tm=tm,
                                       riders=ride("od_in", "od_out", "w1_1"))
    land(routs, "od_in", "od_out", "w1_1")
    (h3, n2, s, sv, y), routs = _fwd_odd(h2, od_norm, as_cols(w["od_in"]), od_bias, od_lng, od_lnb, od_w_s[0], b_s_rows,
                                         as_rows(w["od_out"]), tm=tm, riders=ride("w2_1"))
    land(routs, "w2_1")
    (h4, n3, p1, q1), _ = _fwd_mlp(h3, mlp_norm_g[1:2], as_cols(w["w1_1"]), as_cols(w["w2_1"]), 1, tm=tm)
    loss_part, dh4, d_final_g = _loss_head(h4, jnp.reshape(final_norm_g, (1, d)), tgt2, tm=tm)

    red = {}

    def swap(*names):
        return [red[nm].pair_swap() for nm in names] if distributed else []

    def chips(*names):
        return [red[nm].chip_swap() for nm in names] if distributed else []

    def share(*names):
        return [red[nm].pair_share() for nm in names] if distributed else []

    def took(routs, *steps):
        if distributed:
            for (nm, what), outs in zip(steps, routs):
                getattr(red[nm], what)(outs)

    g, _ = _wgrad(q1, dh4, "wgrad_w2_1", col_shards=False)
    red["w2_1"] = _GradReduce("w2_1", g)
    (dh3, dp1, d_mlp_g1), routs = _bwd_mlp(dh4, h3, mlp_norm_g[1:2], p1, as_cols(w["w1_1"]), as_cols(w["w2_1"]), 1, tm=tm,
                                           riders=swap("w2_1"))
    took(routs, ("w2_1", "took_pair"))
    g, routs = _wgrad(n3, dp1, "wgrad_w1_1", col_shards=True, riders=chips("w2_1"))
    red["w1_1"] = _GradReduce("w1_1", g)
    took(routs, ("w2_1", "took_chips"))
    g, routs = _wgrad(y, dh3, "wgrad_od_out", col_shards=False, riders=swap("w1_1"))
    red["od_out"] = _GradReduce("od_out", g)
    took(routs, ("w1_1", "took_pair"))
    (dh2, ds, d_od_norm, d_od_bin, d_od_lng, d_od_lnb, d_ws, d_bs), routs = _bwd_odd(
        dh3, h2, od_norm, s, sv, as_cols(w["od_in"]), od_lng, od_lnb, od_w_s[0], as_rows(w["od_out"]), tm=tm,
        riders=chips("w1_1") + swap("od_out") + share("w2_1"))
    took(routs, ("w1_1", "took_chips"), ("od_out", "took_pair"), ("w2_1", "took_share"))
    g, routs = _wgrad(n2, ds, "wgrad_od_in", col_shards=True, riders=chips("od_out") + share("w1_1"))
    red["od_in"] = _GradReduce("od_in", g)
    took(routs, ("od_out", "took_chips"), ("w1_1", "took_share"))
    g, routs = _wgrad(q0, dh2, "wgrad_w2_0", col_shards=False, riders=swap("od_in") + share("od_out"))
    red["w2_0"] = _GradReduce("w2_0", g)
    took(routs, ("od_in", "took_pair"), ("od_out", "took_share"))
    (dh1, dp0, d_mlp_g0), routs = _bwd_mlp(dh2, h1, mlp_norm_g[0:1], p0, as_cols(w["w1_0"]), as_cols(w["w2_0"]), 0, tm=tm,
                                           riders=chips("od_in") + swap("w2_0"))
    took(routs, ("od_in", "took_chips"), ("w2_0", "took_pair"))
    g, routs = _wgrad(n1, dp0, "wgrad_w1_0", col_shards=True, riders=chips("w2_0") + share("od_in"))
    red["w1_0"] = _GradReduce("w1_0", g)
    took(routs, ("w2_0", "took_chips"), ("od_in", "took_share"))
    g, routs = _wgrad(mix, dh1, "wgrad_ev_out", col_shards=False, riders=swap("w1_0") + share("w2_0"))
    red["ev_out"] = _GradReduce("ev_out", g)
    took(routs, ("w1_0", "took_pair"), ("w2_0", "took_share"))
    (dx, dz, d_ev_norm, d_caw, d_cab, d_ev_lng, d_ev_lnb, d_cbw), _ = _bwd_even(
        dh1, x2, ev_norm_g, z, a2, cv, as_cols(w["ev_in"]), conv_a_w, ev_ln_a_g, ev_ln_a_b, conv_b_w, as_rows(w["ev_out"]),
        tm=tm, seq=seq)
    g, routs = _wgrad(n0, dz, "wgrad_ev_in", col_shards=True, riders=chips("w1_0") + swap("ev_out"))
    red["ev_in"] = _GradReduce("ev_in", g)
    took(routs, ("w1_0", "took_chips"), ("ev_out", "took_pair"))

    rep_grads = [d_ev_norm, d_cab, d_ev_lng, d_ev_lnb, d_ws, d_bs, d_mlp_g0, d_mlp_g1, d_final_g]
    shard_grads_full = [d_caw, d_cbw, d_od_norm, d_od_bin, d_od_lng, d_od_lnb]
    return loss_part, dx, red, rep_grads, shard_grads_full


def _rows128(a):
    rows = jnp.reshape(a, (-1, LANES))
    pad = (-rows.shape[0]) % SUBLANES
    return jnp.pad(rows, ((0, pad), (0, 0))) if pad else rows


def _pack(arrays):
    return jnp.concatenate([_rows128(a) for a in arrays], axis=0)


def _unpack(buf, shapes):
    out, r0 = [], 0
    for shp in shapes:
        size = 1
        for dim in shp:
            size *= dim
        nr = size // LANES
        out.append(jnp.reshape(buf[r0:r0 + nr], shp))
        r0 += nr + (-nr) % SUBLANES
    return out


def _packed_rows(shapes):
    total = 0
    for shp in shapes:
        size = 1
        for dim in shp:
            size *= dim
        nr = size // LANES
        total += nr + (-nr) % SUBLANES
    return total


def kernel(x, ev_norm_g, ev_w_in, ev_conv_a_w, ev_conv_a_b, ev_ln_a_g, ev_ln_a_b, ev_conv_b_w, ev_w_out, od_norm_g, od_w_in, od_b_in, od_ln_v_g, od_ln_v_b, od_w_s, od_b_s, od_w_out, mlp_norm_g, mlp_w1, mlp_w2, final_norm_g, loss_target, m_ev_norm_g, m_ev_w_in, m_ev_conv_a_w, m_ev_conv_a_b, m_ev_ln_a_g, m_ev_ln_a_b, m_ev_conv_b_w, m_ev_w_out, m_od_norm_g, m_od_w_in, m_od_b_in, m_od_ln_v_g, m_od_ln_v_b, m_od_w_s, m_od_b_s, m_od_w_out, m_mlp_norm_g, m_mlp_w1, m_mlp_w2, m_final_norm_g, v_ev_norm_g, v_ev_w_in, v_ev_conv_a_w, v_ev_conv_a_b, v_ev_ln_a_g, v_ev_ln_a_b, v_ev_conv_b_w, v_ev_w_out, v_od_norm_g, v_od_w_in, v_od_b_in, v_od_ln_v_g, v_od_ln_v_b, v_od_w_s, v_od_b_s, v_od_w_out, v_mlp_norm_g, v_mlp_w1, v_mlp_w2, v_final_norm_g):
    tm = TOKEN_TILE
    batch, seq, d = x.shape
    tokens = batch * seq
    x2 = jnp.reshape(x, (tokens, d))
    tgt2 = jnp.reshape(loss_target, (tokens, d))
    chip = 2 * lax.axis_index("x") + lax.axis_index("y")

    small_shapes = [(A_CONV_WIDTH, LANES), (B_CONV_WIDTH, LANES), (256,), (512,), (256,), (256,)]
    small_shard = _pack([ev_conv_a_w[0], ev_conv_b_w[0], od_norm_g[0], od_b_in[0], od_ln_v_g[0], od_ln_v_b[0]])
    small_shard = jnp.pad(small_shard, ((0, (-small_shard.shape[0]) % (2 * SUBLANES)), (0, 0)))
    first = [_place_shard(ev_w_in, 0, BF16, "place_ev_w_in"), _place_shard(ev_w_out, 0, BF16, "place_ev_w_out"),
             _place_shard(small_shard[None], 0, F32, "place_small")]
    staged = {
        "w1_0": _place_shard(mlp_w1, 0, BF16, "place_w1_0"), "w2_0": _place_shard(mlp_w2, 0, BF16, "place_w2_0"),
        "od_in": _place_shard(od_w_in, 0, BF16, "place_od_w_in"), "od_out": _place_shard(od_w_out, 0, BF16, "place_od_w_out"),
        "w1_1": _place_shard(mlp_w1, 1, BF16, "place_w1_1"), "w2_1": _place_shard(mlp_w2, 1, BF16, "place_w2_1"),
    }
    (g_ev_in, g_ev_out, g_small), = _exchange([_Gather(first)], "gather_first")
    small_all = jnp.reshape(g_small, (N_CHIPS, -1, LANES))
    per_chip = [_unpack(small_all[q], small_shapes) for q in range(N_CHIPS)]
    conv_a_w = jnp.concatenate([pc[0] for pc in per_chip], axis=1)
    conv_b_w = jnp.concatenate([pc[1] for pc in per_chip], axis=1)
    od_norm = jnp.concatenate([pc[2] for pc in per_chip])[None, :]
    od_bias = jnp.concatenate([pc[3] for pc in per_chip])[None, :]
    od_lng = jnp.concatenate([pc[4] for pc in per_chip])[None, :]
    od_lnb = jnp.concatenate([pc[5] for pc in per_chip])[None, :]

    loss_part, dx, red, rep_grads, shard_grads_full = _forward_backward(
        x2, tgt2, {"ev_in": g_ev_in, "ev_out": g_ev_out}, staged, conv_a_w, conv_b_w, od_norm, od_bias, od_lng, od_lnb,
        ev_norm_g, ev_conv_a_b, ev_ln_a_g, ev_ln_a_b, od_w_s, od_b_s, mlp_norm_g, final_norm_g, tm=tm, seq=seq)
    loss = lax.psum(loss_part[0, 0], ("x", "y", "c"))

    rep_shapes = [(1, D_MODEL), (1, A_DIM), (1, A_DIM), (1, A_DIM), (1, C_GROUPS, CHUNK, CHUNK), (1, C_GROUPS, CHUNK),
                  (1, D_MODEL), (1, D_MODEL), (D_MODEL,)]
    full_shapes = [(A_CONV_WIDTH, A_DIM), (B_CONV_WIDTH, B_DIM), (4, 256), (4, 512), (4, 256), (4, 256)]
    (small8,), = _exchange([_ShareAll(_place_slab(_pack(rep_grads + shard_grads_full), "place_small_grads"))], "small_grad_exchange")
    small_sum = _sum_devices(small8)
    n_rep_rows = _packed_rows(rep_shapes)
    rep_red = _unpack(small_sum[:n_rep_rows], rep_shapes)
    full_red = _unpack(small_sum[n_rep_rows:], full_shapes)
    shard_red = [
        lax.dynamic_slice_in_dim(full_red[0], chip * LANES, LANES, axis=1)[None],
        lax.dynamic_slice_in_dim(full_red[1], chip * LANES, LANES, axis=1)[None],
        lax.dynamic_index_in_dim(full_red[2], chip, axis=0, keepdims=True),
        lax.dynamic_index_in_dim(full_red[3], chip, axis=0, keepdims=True),
        lax.dynamic_index_in_dim(full_red[4], chip, axis=0, keepdims=True),
        lax.dynamic_index_in_dim(full_red[5], chip, axis=0, keepdims=True),
    ]
    rep_red[6] = jnp.concatenate([rep_red[6], rep_red[7]], axis=0)
    del rep_red[7]

    def big_update(wt, m, v, names, call, riders=()):
        grads = [red[nm].reduced() for nm in names]
        shp3 = (len(grads),) + grads[0].shape
        outs, routs = _adamw(jnp.reshape(wt, shp3), jnp.reshape(m, shp3), jnp.reshape(v, shp3), grads, call, riders=riders)
        return [jnp.reshape(o, wt.shape) for o in outs], routs

    upd = {}
    routs =_exchange([red["ev_out"].chip_swap(), red["ev_in"].pair_swap(), red["w1_0"].pair_share()], "reduce_tail_1")
    red["ev_out"].took_chips(routs[0])
    red["ev_in"].took_pair(routs[1])
    red["w1_0"].took_share(routs[2])
    upd["mlp_w2"], routs = big_update(mlp_w2, m_mlp_w2, v_mlp_w2, ["w2_0", "w2_1"], "adamw_mlp_w2",
                                      riders=[red["ev_in"].chip_swap(), red["ev_out"].pair_share()])
    red["ev_in"].took_chips(routs[0])
    red["ev_out"].took_share(routs[1])
    upd["mlp_w1"], routs = big_update(mlp_w1, m_mlp_w1, v_mlp_w1, ["w1_0", "w1_1"], "adamw_mlp_w1",
                                      riders=[red["ev_in"].pair_share()])
    red["ev_in"].took_share(routs[0])
    upd["ev_w_in"], _ = big_update(ev_w_in, m_ev_w_in, v_ev_w_in, ["ev_in"], "adamw_ev_w_in")
    upd["ev_w_out"], _ = big_update(ev_w_out, m_ev_w_out, v_ev_w_out, ["ev_out"], "adamw_ev_w_out")
    upd["od_w_in"], _ = big_update(od_w_in, m_od_w_in, v_od_w_in, ["od_in"], "adamw_od_w_in")
    upd["od_w_out"], _ = big_update(od_w_out, m_od_w_out, v_od_w_out, ["od_out"], "adamw_od_w_out")

    small_names = ["ev_norm_g", "ev_conv_a_b", "ev_ln_a_g", "ev_ln_a_b", "od_w_s", "od_b_s", "mlp_norm_g", "final_norm_g",
                   "ev_conv_a_w", "ev_conv_b_w", "od_norm_g", "od_b_in", "od_ln_v_g", "od_ln_v_b"]
    small_w = [ev_norm_g, ev_conv_a_b, ev_ln_a_g, ev_ln_a_b, od_w_s, od_b_s, mlp_norm_g, final_norm_g,
               ev_conv_a_w, ev_conv_b_w, od_norm_g, od_b_in, od_ln_v_g, od_ln_v_b]
    small_m = [m_ev_norm_g, m_ev_conv_a_b, m_ev_ln_a_g, m_ev_ln_a_b, m_od_w_s, m_od_b_s, m_mlp_norm_g, m_final_norm_g,
               m_ev_conv_a_w, m_ev_conv_b_w, m_od_norm_g, m_od_b_in, m_od_ln_v_g, m_od_ln_v_b]
    small_v = [v_ev_norm_g, v_ev_conv_a_b, v_ev_ln_a_g, v_ev_ln_a_b, v_od_w_s, v_od_b_s, v_mlp_norm_g, v_final_norm_g,
               v_ev_conv_a_w, v_ev_conv_b_w, v_od_norm_g, v_od_b_in, v_od_ln_v_g, v_od_ln_v_b]
    small_g = rep_red + shard_red
    packed = [_pack(group)[None] for group in (small_w, small_m, small_v)]
    outs, _ = _adamw(packed[0], packed[1], packed[2], [_pack(small_g)], "adamw_small")
    small_out_shapes = [wt.shape for wt in small_w]
    small_outs = [_unpack(o[0], small_out_shapes) for o in outs]
    for idx, nm in enumerate(small_names):
        upd[nm] = [small_outs[kind][idx] for kind in range(4)]

    order = ["ev_norm_g", "ev_w_in", "ev_conv_a_w", "ev_conv_a_b", "ev_ln_a_g", "ev_ln_a_b", "ev_conv_b_w", "ev_w_out",
             "od_norm_g", "od_w_in", "od_b_in", "od_ln_v_g", "od_ln_v_b", "od_w_s", "od_b_s", "od_w_out", "mlp_norm_g",
             "mlp_w1", "mlp_w2", "final_norm_g"]
    grad_x = jnp.reshape(dx, x.shape)
    return (loss, grad_x, *[upd[nm][0] for nm in order], *[upd[nm][1] for nm in order],
            *[upd[nm][2] for nm in order], *[upd[nm][3] for nm in order])
```

```python
import functools

import jax
import jax.numpy as jnp
from jax import lax
from jax.experimental import pallas as pl
from jax.experimental.pallas import tpu as pltpu

F32 = jnp.float32
BF16 = jnp.bfloat16

D_MODEL = 1024
A_DIM = 512
B_DIM = 512
IN_EVEN = 2 * A_DIM + 3 * B_DIM
A_CONV_WIDTH = 31
B_CONV_WIDTH = 3
CHUNK = 128
C_GROUPS = 8
C_DIM = 1024
D_FF = 4096
RMS_EPS = 1e-6
LN_EPS = 1e-5
ADAM_LR = 0.001
ADAM_B1 = 0.9
ADAM_B2 = 0.999
ADAM_EPS = 1e-08
ADAM_WD = 0.01
ADAM_STEP = 10

N_CHIPS = 4
N_DEV = 8
TOKEN_TILE = 512
A_HALO = 32
B_HALO = 8
CONV_ROWS = 16
PAIR = 2 * CHUNK
LANES = 128
SUBLANES = 8
MIB = 1024 * 1024
MESH = pl.DeviceIdType.MESH
ANY = pl.BlockSpec(memory_space=pl.ANY)


def _dot(a, b):
    return lax.dot_general(a, b, (((1,), (0,)), ((), ())), preferred_element_type=F32)


def _dot_nt(a, b):
    return lax.dot_general(a, b, (((1,), (1,)), ((), ())), preferred_element_type=F32)


def _dot_tn(a, b):
    return lax.dot_general(a, b, (((0,), (0,)), ((), ())), preferred_element_type=F32)


def _params(vmem_mib, n_axes=1):
    return pltpu.CompilerParams(dimension_semantics=("arbitrary",) * n_axes, vmem_limit_bytes=vmem_mib * MIB)


def _row_spec(tm, cols, rev_nt=None):
    if rev_nt is None:
        return pl.BlockSpec((tm, cols), lambda i: (i, 0))
    return pl.BlockSpec((tm, cols), lambda i: (rev_nt - 1 - i, 0))


def _full_spec(shape):
    nd = len(shape)
    return pl.BlockSpec(shape, lambda i: (0,) * nd)


def _block_rows(rows, cap=512):
    best = SUBLANES
    for br in range(SUBLANES, min(rows, cap) + 1, SUBLANES):
        if rows % br == 0:
            best = br
    return best


def _load(src, dst, sem):
    cp = pltpu.make_async_copy(src, dst, sem)
    cp.start()
    cp.wait()


def _rms_fwd(x, g):
    rstd = lax.rsqrt(jnp.mean(x * x, axis=-1, keepdims=True) + RMS_EPS)
    return x * rstd * g, rstd


def _rms_bwd(dn, x, rstd, g):
    a = dn * g
    xh = x * rstd
    dx = rstd * (a - xh * jnp.mean(a * xh, axis=-1, keepdims=True))
    dg = jnp.sum(dn * xh, axis=0, keepdims=True)
    return dx, dg


def _ln_stats(v):
    mu = jnp.mean(v, axis=-1, keepdims=True)
    xc = v - mu
    rs = lax.rsqrt(jnp.mean(xc * xc, axis=-1, keepdims=True) + LN_EPS)
    return xc * rs, rs


def _ln_bwd(dy, xhat, rs, g):
    dxh = dy * g
    dv = rs * (dxh - jnp.mean(dxh, axis=-1, keepdims=True) - xhat * jnp.mean(dxh * xhat, axis=-1, keepdims=True))
    return dv, jnp.sum(dy * xhat, axis=0, keepdims=True), jnp.sum(dy, axis=0, keepdims=True)


def _gelu_parts(s):
    cdf = 0.5 * (1.0 + lax.erf(s * 0.7071067811865476))
    return cdf, jnp.exp(-0.5 * s * s) * 0.3989422804014327


def _mesh_pos():
    return lax.axis_index("x"), lax.axis_index("y"), lax.axis_index("c")


def _other_chips(x, y):
    return [(1 - x, y), (x, 1 - y), (1 - x, 1 - y)]


def _remote(src, dst, send_sem, recv_sem, to):
    return pltpu.make_async_remote_copy(src_ref=src, dst_ref=dst, send_sem=send_sem, recv_sem=recv_sem,
                                        device_id=to, device_id_type=MESH)


def _like(arrays):
    return [jax.ShapeDtypeStruct(a.shape, a.dtype) for a in arrays]


class _Gather:
    def __init__(self, bufs):
        self.ins = list(bufs)
        self.out_shapes = _like(bufs)
        self.aliases = {t: t for t in range(len(bufs))}
        self.n_sems = 6 * len(bufs)

    def _ici(self, ins, outs, send, recv, t, k, chip, mine, c):
        return _remote(ins[t].at[mine, c], outs[t].at[mine, c], send.at[6 * t + k], recv.at[6 * t + k], (*chip, c))

    def start(self, ins, outs, send, recv):
        x, y, c = _mesh_pos()
        for t in range(len(ins)):
            for k, chip in enumerate(_other_chips(x, y)):
                self._ici(ins, outs, send, recv, t, k, chip, 2 * x + y, c).start()

    def finish(self, ins, outs, send, recv):
        x, y, c = _mesh_pos()
        me, sibling = (x, y, c), (x, y, 1 - c)
        chips = _other_chips(x, y)
        passed = []
        for t in range(len(ins)):
            for k, chip in enumerate(chips):
                blk = outs[t].at[2 * chip[0] + chip[1], c]
                _remote(blk, blk, send.at[6 * t + k], recv.at[6 * t + k], me).wait_recv()
                cp = _remote(blk, blk, send.at[6 * t + 3 + k], recv.at[6 * t + 3 + k], sibling)
                cp.start()
                passed.append(cp)
        for t in range(len(ins)):
            for k, chip in enumerate(chips):
                blk = outs[t].at[2 * chip[0] + chip[1], 1 - c]
                _remote(blk, blk, send.at[6 * t + 3 + k], recv.at[6 * t + 3 + k], me).wait_recv()
        for t in range(len(ins)):
            for k, chip in enumerate(chips):
                self._ici(ins, outs, send, recv, t, k, chip, 2 * x + y, c).wait_send()
        for cp in passed:
            cp.wait_send()


class _PairSwap:
    def __init__(self, grads):
        self.ins = list(grads)
        self.out_shapes = [jax.ShapeDtypeStruct((g.shape[0],) + g.shape[2:], g.dtype) for g in grads]
        self.aliases = {}
        self.n_sems = len(grads)

    def _copies(self, ins, outs, send, recv):
        x, y, c = _mesh_pos()
        return [_remote(ins[t].at[:, 1 - c], outs[t], send.at[t], recv.at[t], (x, y, 1 - c)) for t in range(len(ins))]

    def start(self, ins, outs, send, recv):
        for cp in self._copies(ins, outs, send, recv):
            cp.start()

    def finish(self, ins, outs, send, recv):
        for cp in self._copies(ins, outs, send, recv):
            cp.wait()


class _ChipSwap:
    def __init__(self, parts):
        self.ins = list(parts)
        self.out_shapes = [jax.ShapeDtypeStruct((3,) + p.shape[1:], p.dtype) for p in parts]
        self.aliases = {}
        self.n_sems = 3 * len(parts)

    def _copies(self, ins, outs, send, recv):
        x, y, c = _mesh_pos()
        return [_remote(ins[t].at[2 * chip[0] + chip[1]], outs[t].at[k], send.at[3 * t + k], recv.at[3 * t + k], (*chip, c))
                for t in range(len(ins)) for k, chip in enumerate(_other_chips(x, y))]

    def start(self, ins, outs, send, recv):
        for cp in self._copies(ins, outs, send, recv):
            cp.start()

    def finish(self, ins, outs, send, recv):
        for cp in self._copies(ins, outs, send, recv):
            cp.wait()


class _PairShare:
    def __init__(self, fulls):
        self.ins = list(fulls)
        self.out_shapes = _like(fulls)
        self.aliases = {t: t for t in range(len(fulls))}
        self.n_sems = len(fulls)

    def _copies(self, ins, outs, send, recv):
        x, y, c = _mesh_pos()
        return [_remote(ins[t].at[c], outs[t].at[c], send.at[t], recv.at[t], (x, y, 1 - c)) for t in range(len(ins))]

    def start(self, ins, outs, send, recv):
        for cp in self._copies(ins, outs, send, recv):
            cp.start()

    def finish(self, ins, outs, send, recv):
        for cp in self._copies(ins, outs, send, recv):
            cp.wait()


class _ShareAll:
    def __init__(self, buf):
        self.ins = [buf]
        self.out_shapes = _like([buf])
        self.aliases = {0: 0}
        self.n_sems = N_DEV - 1

    def _peers(self):
        x, y, c = _mesh_pos()
        flips = [((r >> 2) & 1, (r >> 1) & 1, r & 1) for r in range(1, N_DEV)]
        return (x, y, c), [(x ^ fx, y ^ fy, c ^ fc) for fx, fy, fc in flips]

    def start(self, ins, outs, send, recv):
        (x, y, c), peers = self._peers()
        mine = 4 * x + 2 * y + c
        for r, peer in enumerate(peers):
            _remote(ins[0].at[mine], outs[0].at[mine], send.at[r], recv.at[r], peer).start()

    def finish(self, ins, outs, send, recv):
        (x, y, c), peers = self._peers()
        mine = 4 * x + 2 * y + c
        for r, (px, py, pc) in enumerate(peers):
            blk = outs[0].at[4 * px + 2 * py + pc]
            _remote(blk, blk, send.at[r], recv.at[r], (x, y, c)).wait_recv()
        for r, peer in enumerate(peers):
            _remote(ins[0].at[mine], outs[0].at[mine], send.at[r], recv.at[r], peer).wait_send()


def _pallas(body, operands, *, name, grid, in_specs, out_specs, out_shape, scratch_shapes=(), vmem_mib=32, riders=()):
    in_specs, out_specs, out_shape, scratch_shapes = list(in_specs), list(out_specs), list(out_shape), list(scratch_shapes)
    if not riders:
        outs = pl.pallas_call(body, name=name, grid=grid, in_specs=in_specs, out_specs=out_specs, out_shape=out_shape,
                              scratch_shapes=scratch_shapes, compiler_params=_params(vmem_mib, len(grid)))(*operands)
        return list(outs), []
    n_in, n_out, n_scr = len(in_specs), len(out_specs), len(scratch_shapes)
    r_in = [len(r.ins) for r in riders]
    r_out = [len(r.out_shapes) for r in riders]
    steps = 1
    for g in grid:
        steps *= g

    def wrapped(*refs):
        refs = list(refs)
        ins, refs = refs[:n_in], refs[n_in:]
        rins = []
        for k in r_in:
            rins.append(refs[:k])
            refs = refs[k:]
        outs, refs = refs[:n_out], refs[n_out:]
        routs = []
        for k in r_out:
            routs.append(refs[:k])
            refs = refs[k:]
        scr, sems = refs[:n_scr], refs[n_scr:]
        step = 0
        for ax, g in enumerate(grid):
            step = step * g + pl.program_id(ax)

        def each(what):
            for j, r in enumerate(riders):
                getattr(r, what)(rins[j], routs[j], sems[2 * j], sems[2 * j + 1])

        if grid:
            pl.when(step == 0)(lambda: each("start"))
        else:
            each("start")
        body(*ins, *outs, *scr)
        if grid:
            pl.when(step == steps - 1)(lambda: each("finish"))
        else:
            each("finish")

    aliases, off_in, off_out = {}, n_in, n_out
    for r, ki, ko in zip(riders, r_in, r_out):
        for i, o in r.aliases.items():
            aliases[off_in + i] = off_out + o
        off_in, off_out = off_in + ki, off_out + ko
    sems = []
    for r in riders:
        sems += [pltpu.SemaphoreType.DMA((r.n_sems,)), pltpu.SemaphoreType.DMA((r.n_sems,))]
    res = pl.pallas_call(
        wrapped, name=name, grid=grid,
        in_specs=in_specs + [ANY] * sum(r_in), out_specs=out_specs + [ANY] * sum(r_out),
        out_shape=out_shape + [s for r in riders for s in r.out_shapes],
        scratch_shapes=scratch_shapes + sems, input_output_aliases=aliases,
        compiler_params=pltpu.CompilerParams(dimension_semantics=("arbitrary",) * len(grid),
                                             vmem_limit_bytes=vmem_mib * MIB, has_side_effects=True),
    )(*operands, *[a for r in riders for a in r.ins])
    res = list(res)
    outs, res = res[:n_out], res[n_out:]
    routs = []
    for k in r_out:
        routs.append(res[:k])
        res = res[k:]
    return outs, routs


def _exchange(riders, name):
    return _pallas(lambda: None, [], name=name, grid=(), in_specs=[], out_specs=[], out_shape=[], riders=riders)[1]


def _place_shard(w, layer, dtype, name):
    _, rows, cols = w.shape
    half = rows // 2
    br = _block_rows(half)
    nb = half // br
    mine = 2 * lax.axis_index("x") + lax.axis_index("y")

    def body(q_ref, w_ref, o_ref):
        o_ref[...] = w_ref[...].astype(dtype)

    return pl.pallas_call(
        body, name=name,
        grid_spec=pltpu.PrefetchScalarGridSpec(
            num_scalar_prefetch=1, grid=(2, nb),
            in_specs=[pl.BlockSpec((None, br, cols), lambda h, i, q: (layer, h * nb + i, 0))],
            out_specs=pl.BlockSpec((None, None, br, cols), lambda h, i, q: (q[0], h, i, 0))),
        out_shape=jax.ShapeDtypeStruct((N_CHIPS, 2, half, cols), dtype),
        compiler_params=_params(16, 2),
    )(jnp.reshape(mine, (1,)).astype(jnp.int32), w)


def _place_slab(v, name):
    r, cdim = v.shape
    x, y, c = _mesh_pos()

    def body(d_ref, v_ref, o_ref):
        o_ref[...] = v_ref[...]

    return pl.pallas_call(
        body, name=name,
        grid_spec=pltpu.PrefetchScalarGridSpec(
            num_scalar_prefetch=1, grid=(1,),
            in_specs=[pl.BlockSpec((r, cdim), lambda i, d: (0, 0))],
            out_specs=pl.BlockSpec((None, r, cdim), lambda i, d: (d[0], 0, 0))),
        out_shape=jax.ShapeDtypeStruct((N_DEV, r, cdim), v.dtype),
        compiler_params=_params(16, 1),
    )(jnp.reshape(4 * x + 2 * y + c, (1,)).astype(jnp.int32), v)


def _add_pair(g, recv, name):
    _, _, r, cdim = g.shape
    br = _block_rows(r, 256)
    c = lax.axis_index("c")

    def body(c_ref, g_ref, r_ref, o_ref):
        o_ref[...] = (g_ref[...] + r_ref[...]).astype(BF16)

    return pl.pallas_call(
        body, name=name,
        grid_spec=pltpu.PrefetchScalarGridSpec(
            num_scalar_prefetch=1, grid=(N_CHIPS, r // br),
            in_specs=[pl.BlockSpec((None, None, br, cdim), lambda q, i, c_ref: (q, c_ref[0], i, 0)),
                      pl.BlockSpec((None, br, cdim), lambda q, i, c_ref: (q, i, 0))],
            out_specs=pl.BlockSpec((None, br, cdim), lambda q, i, c_ref: (q, i, 0))),
        out_shape=jax.ShapeDtypeStruct((N_CHIPS, r, cdim), BF16),
        compiler_params=_params(16, 2),
    )(jnp.reshape(c, (1,)).astype(jnp.int32), g, recv)


def _add_chips(own, recv, name):
    _, r, cdim = own.shape
    br = _block_rows(r, 256)
    x, y, c = _mesh_pos()

    def body(pos_ref, own_ref, r_ref, o_ref):
        acc = own_ref[...].astype(F32)
        for k in range(3):
            acc = acc + r_ref[k].astype(F32)
        o_ref[...] = acc

    return pl.pallas_call(
        body, name=name,
        grid_spec=pltpu.PrefetchScalarGridSpec(
            num_scalar_prefetch=1, grid=(r // br,),
            in_specs=[pl.BlockSpec((None, br, cdim), lambda i, pos: (pos[0], i, 0)),
                      pl.BlockSpec((3, br, cdim), lambda i, pos: (0, i, 0))],
            out_specs=pl.BlockSpec((None, br, cdim), lambda i, pos: (pos[1], i, 0))),
        out_shape=jax.ShapeDtypeStruct((2, r, cdim), F32),
        compiler_params=_params(16, 1),
    )(jnp.stack([2 * x + y, c]).astype(jnp.int32), own, recv)


def _sum_devices(v8):
    _, r, cdim = v8.shape

    def body(v_ref, o_ref):
        acc = v_ref[0]
        for d in range(1, N_DEV):
            acc = acc + v_ref[d]
        o_ref[...] = acc

    return pl.pallas_call(
        body, name="small_grad_sum",
        in_specs=[pl.BlockSpec(memory_space=pltpu.VMEM)], out_specs=pl.BlockSpec(memory_space=pltpu.VMEM),
        out_shape=jax.ShapeDtypeStruct((r, cdim), F32),
        compiler_params=pltpu.CompilerParams(vmem_limit_bytes=32 * MIB),
    )(v8)


def _adamw(w, m, v, grads, name, riders=()):
    layers, r, cdim = w.shape
    br = _block_rows(r, 256 if cdim > LANES else 1024)
    c1 = 1.0 / (1.0 - ADAM_B1 ** ADAM_STEP)
    c2 = 1.0 / (1.0 - ADAM_B2 ** ADAM_STEP)

    def body(*refs):
        w_ref, m_ref, v_ref = refs[:3]
        g_refs = refs[3:3 + layers]
        go_ref, d_ref, mo_ref, vo_ref = refs[3 + layers:]
        layer = pl.program_id(0)
        for l in range(layers):
            @pl.when(layer == l)
            def _(l=l):
                g = g_refs[l][...]
                m_new = ADAM_B1 * m_ref[...] + (1.0 - ADAM_B1) * g
                v_new = ADAM_B2 * v_ref[...] + (1.0 - ADAM_B2) * (g * g)
                go_ref[...] = g
                mo_ref[...] = m_new
                vo_ref[...] = v_new
                d_ref[...] = -ADAM_LR * ((m_new * c1) / (jnp.sqrt(v_new * c2) + ADAM_EPS) + ADAM_WD * w_ref[...])

    spec3 = pl.BlockSpec((None, br, cdim), lambda l, i: (l, i, 0))
    spec2 = pl.BlockSpec((br, cdim), lambda l, i: (i, 0))
    out = jax.ShapeDtypeStruct((layers, r, cdim), F32)
    return _pallas(body, [w, m, v, *grads], name=name, grid=(layers, r // br),
                   in_specs=[spec3, spec3, spec3] + [spec2] * layers, out_specs=[spec3] * 4, out_shape=[out] * 4,
                   vmem_mib=32, riders=riders)


def _conv31(src, w_ref, rows, base, init):
    acc = init
    for k in range(A_CONV_WIDTH):
        acc = acc + w_ref[k:k + 1, :] * src[base + k + rows.start:base + k + rows.stop, :]
    return acc


def _fwd_even(x, norm_g, w_in, conv_a_w, conv_a_b, ln_g, ln_b, conv_b_w, w_out, *, tm, seq, riders=()):
    tokens = x.shape[0]
    nt, tps = tokens // tm, seq // tm

    def body(x_ref, g_ref, win_hbm, caw_ref, cab_ref, lng_ref, lnb_ref, cbw_ref, wout_hbm,
             h_ref, n_ref, z_ref, a2_ref, cv_ref, mix_ref, win_v, wout_v, pa, pb, sem):
        i = pl.program_id(0)

        @pl.when(i == 0)
        def _():
            _load(win_hbm, win_v, sem)
            _load(wout_hbm, wout_v, sem)

        xv = x_ref[...]
        nf, _ = _rms_fwd(xv, g_ref[...])
        n = nf.astype(BF16)
        n_ref[...] = n
        z = jnp.concatenate([_dot(n, win_v[j]) for j in range(N_CHIPS)], axis=1)
        z_ref[...] = z.astype(BF16)
        a_val, a_gate = z[:, 0:A_DIM], z[:, A_DIM:2 * A_DIM]
        b_gate, c_gate, b_val = z[:, 1024:1536], z[:, 1536:2048], z[:, 2048:2560]

        first = (i % tps) == 0

        @pl.when(first)
        def _():
            pa[0:A_HALO, :] = jnp.zeros((A_HALO, A_DIM), F32)
            pb[0:B_HALO, :] = jnp.zeros((B_HALO, B_DIM), F32)

        @pl.when(jnp.logical_not(first))
        def _():
            pa[0:A_HALO, :] = pa[tm:tm + A_HALO, :]
            pb[0:B_HALO, :] = pb[tm:tm + B_HALO, :]

        pa[A_HALO:A_HALO + tm, :] = a_val * jax.nn.sigmoid(a_gate)
        pb[B_HALO:B_HALO + tm, :] = c_gate * b_val
        bias = jnp.broadcast_to(cab_ref[...], (CONV_ROWS, A_DIM))
        for r0 in range(0, tm, CONV_ROWS):
            rows = slice(r0, r0 + CONV_ROWS)
            a2_ref[rows, :] = _conv31(pa, caw_ref, rows, A_HALO - (A_CONV_WIDTH - 1), bias)
        xhat, _ = _ln_stats(a2_ref[...])
        a3 = xhat * lng_ref[...] + lnb_ref[...]
        a4 = a3 * jax.nn.sigmoid(a3)
        cv = cbw_ref[0:1, :] * pb[B_HALO - 2:B_HALO - 2 + tm, :]
        cv = cv + cbw_ref[1:2, :] * pb[B_HALO - 1:B_HALO - 1 + tm, :]
        cv = cv + cbw_ref[2:3, :] * pb[B_HALO:B_HALO + tm, :]
        cv_ref[...] = cv.astype(BF16)
        mix = jnp.concatenate([a4, b_gate * cv], axis=1).astype(BF16)
        mix_ref[...] = mix
        h_ref[...] = xv + _dot(mix, wout_v[...])

    shp = lambda cols, dt: jax.ShapeDtypeStruct((tokens, cols), dt)
    return _pallas(
        body, [x, norm_g, w_in, conv_a_w, conv_a_b, ln_g, ln_b, conv_b_w, w_out], name="fwd_even", grid=(nt,),
        in_specs=[_row_spec(tm, D_MODEL), _full_spec((1, D_MODEL)), ANY, _full_spec((A_CONV_WIDTH, A_DIM)),
                  _full_spec((1, A_DIM)), _full_spec((1, A_DIM)), _full_spec((1, A_DIM)),
                  _full_spec((B_CONV_WIDTH, B_DIM)), ANY],
        out_specs=[_row_spec(tm, D_MODEL), _row_spec(tm, D_MODEL), _row_spec(tm, IN_EVEN), _row_spec(tm, A_DIM),
                   _row_spec(tm, B_DIM), _row_spec(tm, D_MODEL)],
        out_shape=[shp(D_MODEL, F32), shp(D_MODEL, BF16), shp(IN_EVEN, BF16), shp(A_DIM, F32), shp(B_DIM, BF16),
                   shp(D_MODEL, BF16)],
        scratch_shapes=[pltpu.VMEM((N_CHIPS, D_MODEL, IN_EVEN // N_CHIPS), BF16), pltpu.VMEM((D_MODEL, D_MODEL), BF16),
                        pltpu.VMEM((A_HALO + tm, A_DIM), F32), pltpu.VMEM((B_HALO + tm, B_DIM), F32),
                        pltpu.SemaphoreType.DMA],
        vmem_mib=56, riders=riders)


def _fwd_mlp(h, norm_g, w1, w2, layer, *, tm, riders=()):
    tokens = h.shape[0]
    nt = tokens // tm
    fs = D_FF // N_CHIPS

    def body(h_ref, g_ref, w1_hbm, w2_hbm, ho_ref, n_ref, p_ref, q_ref, w1_v, w2_v, sem):
        @pl.when(pl.program_id(0) == 0)
        def _():
            _load(w1_hbm, w1_v, sem)
            _load(w2_hbm, w2_v, sem)

        xv = h_ref[...]
        nf, _ = _rms_fwd(xv, g_ref[...])
        n = nf.astype(BF16)
        n_ref[...] = n
        acc = xv
        for j in range(N_CHIPS):
            p = _dot(n, w1_v[j])
            p_ref[:, j * fs:(j + 1) * fs] = p.astype(BF16)
            r = jnp.maximum(p, 0.0)
            q = (r * r).astype(BF16)
            q_ref[:, j * fs:(j + 1) * fs] = q
            acc = acc + _dot(q, w2_v[j])
        ho_ref[...] = acc

    shp = lambda cols, dt: jax.ShapeDtypeStruct((tokens, cols), dt)
    return _pallas(
        body, [h, norm_g, w1, w2], name=f"fwd_mlp{layer}", grid=(nt,),
        in_specs=[_row_spec(tm, D_MODEL), _full_spec((1, D_MODEL)), ANY, ANY],
        out_specs=[_row_spec(tm, D_MODEL), _row_spec(tm, D_MODEL), _row_spec(tm, D_FF), _row_spec(tm, D_FF)],
        out_shape=[shp(D_MODEL, F32), shp(D_MODEL, BF16), shp(D_FF, BF16), shp(D_FF, BF16)],
        scratch_shapes=[pltpu.VMEM((N_CHIPS, D_MODEL, fs), BF16), pltpu.VMEM((N_CHIPS, fs, D_MODEL), BF16),
                        pltpu.SemaphoreType.DMA],
        vmem_mib=56, riders=riders)


def _tril_mask():
    row = lax.broadcasted_iota(jnp.int32, (CHUNK, CHUNK), 0)
    col = lax.broadcasted_iota(jnp.int32, (CHUNK, CHUNK), 1)
    return row >= col


def _triu_mask():
    row = lax.broadcasted_iota(jnp.int32, (CHUNK, CHUNK), 0)
    col = lax.broadcasted_iota(jnp.int32, (CHUNK, CHUNK), 1)
    return row <= col


def _fwd_odd(h, norm_g, w_in, b_in, ln_g, ln_b, w_s, b_s_rows, w_out, *, tm, riders=()):
    tokens = h.shape[0]
    nt = tokens // tm
    cs = 2 * C_DIM // N_CHIPS

    def body(h_ref, g_ref, win_hbm, bin_ref, lng_ref, lnb_ref, ws_ref, bs_ref, wout_hbm,
             ho_ref, n_ref, s_ref, sv_ref, y_ref, win_v, wout_v, bd, sem):
        @pl.when(pl.program_id(0) == 0)
        def _():
            _load(win_hbm, win_v, sem)
            _load(wout_hbm, wout_v, sem)
            mask = _tril_mask()
            bd[...] = jnp.zeros(bd.shape, BF16)
            for g in range(C_GROUPS):
                w = jnp.where(mask, ws_ref[g], 0.0).astype(BF16)
                bd[g, 0:CHUNK, 0:CHUNK] = w
                bd[g, CHUNK:PAIR, CHUNK:PAIR] = w

        xv = h_ref[...]
        nf, _ = _rms_fwd(xv, g_ref[...])
        n = nf.astype(BF16)
        n_ref[...] = n
        s = jnp.concatenate([_dot(n, win_v[j]) for j in range(N_CHIPS)], axis=1) + bin_ref[...]
        s_ref[...] = s.astype(BF16)
        cdf, _ = _gelu_parts(s)
        zz = s * cdf
        u, v = zz[:, 0:C_DIM], zz[:, C_DIM:2 * C_DIM]
        xhat, _ = _ln_stats(v)
        vn = (xhat * lng_ref[...] + lnb_ref[...]).astype(BF16)
        for g in range(C_GROUPS):
            cols = slice(g * CHUNK, (g + 1) * CHUNK)
            bias = jnp.concatenate([bs_ref[g], bs_ref[g]], axis=0)
            for r0 in range(0, tm, PAIR):
                sv = _dot(bd[g], vn[r0:r0 + PAIR, cols]) + bias
                sv_ref[r0:r0 + PAIR, cols] = sv.astype(BF16)
                y_ref[r0:r0 + PAIR, cols] = (u[r0:r0 + PAIR, cols] * sv).astype(BF16)
        ho_ref[...] = xv + _dot(y_ref[...], wout_v[...])

    shp = lambda cols, dt: jax.ShapeDtypeStruct((tokens, cols), dt)
    return _pallas(
        body, [h, norm_g, w_in, b_in, ln_g, ln_b, w_s, b_s_rows, w_out], name="fwd_odd", grid=(nt,),
        in_specs=[_row_spec(tm, D_MODEL), _full_spec((1, D_MODEL)), ANY, _full_spec((1, 2 * C_DIM)),
                  _full_spec((1, C_DIM)), _full_spec((1, C_DIM)), _full_spec((C_GROUPS, CHUNK, CHUNK)),
                  _full_spec((C_GROUPS, CHUNK, CHUNK)), ANY],
        out_specs=[_row_spec(tm, D_MODEL), _row_spec(tm, D_MODEL), _row_spec(tm, 2 * C_DIM), _row_spec(tm, C_DIM),
                   _row_spec(tm, C_DIM)],
        out_shape=[shp(D_MODEL, F32), shp(D_MODEL, BF16), shp(2 * C_DIM, BF16), shp(C_DIM, BF16), shp(C_DIM, BF16)],
        scratch_shapes=[pltpu.VMEM((N_CHIPS, D_MODEL, cs), BF16), pltpu.VMEM((C_DIM, D_MODEL), BF16),
                        pltpu.VMEM((C_GROUPS, PAIR, PAIR), BF16), pltpu.SemaphoreType.DMA],
        vmem_mib=56, riders=riders)


def _loss_head(h, norm_g, target, *, tm):
    tokens = h.shape[0]
    nt = tokens // tm

    def body(h_ref, g_ref, t_ref, loss_ref, dh_ref, dg_ref):
        @pl.when(pl.program_id(0) == 0)
        def _():
            loss_ref[...] = jnp.zeros((1, 1), F32)
            dg_ref[...] = jnp.zeros((1, D_MODEL), F32)

        xv = h_ref[...]
        g = g_ref[...]
        out, rstd = _rms_fwd(xv, g)
        err = out - t_ref[...]
        per_token = jnp.sum(err * err, axis=1, keepdims=True) * (1.0 / D_MODEL)
        loss_ref[...] += 0.5 * jnp.sum(per_token, axis=0, keepdims=True)
        dx, dg = _rms_bwd(err * (1.0 / D_MODEL), xv, rstd, g)
        dh_ref[...] = dx
        dg_ref[...] += dg

    return _pallas(
        body, [h, norm_g, target], name="loss_head", grid=(nt,),
        in_specs=[_row_spec(tm, D_MODEL), _full_spec((1, D_MODEL)), _row_spec(tm, D_MODEL)],
        out_specs=[_full_spec((1, 1)), _row_spec(tm, D_MODEL), _full_spec((1, D_MODEL))],
        out_shape=[jax.ShapeDtypeStruct((1, 1), F32), jax.ShapeDtypeStruct((tokens, D_MODEL), F32),
                   jax.ShapeDtypeStruct((1, D_MODEL), F32)],
        vmem_mib=32)[0]


def _bwd_mlp(dh, h, norm_g, p, w1, w2, layer, *, tm, riders=()):
    tokens = h.shape[0]
    nt = tokens // tm
    fs = D_FF // N_CHIPS

    def body(dh_ref, h_ref, g_ref, p_ref, w1_hbm, w2_hbm, dx_ref, dp_ref, dg_ref, w1_v, w2_v, sem):
        @pl.when(pl.program_id(0) == 0)
        def _():
            _load(w1_hbm, w1_v, sem)
            _load(w2_hbm, w2_v, sem)
            dg_ref[...] = jnp.zeros((1, D_MODEL), F32)

        dhv = dh_ref[...]
        dhb = dhv.astype(BF16)
        dn = jnp.zeros((tm, D_MODEL), F32)
        for j in range(N_CHIPS):
            dq = _dot_nt(dhb, w2_v[j])
            r = jnp.maximum(p_ref[:, j * fs:(j + 1) * fs].astype(F32), 0.0)
            dp = ((2.0 * r) * dq).astype(BF16)
            dp_ref[:, j * fs:(j + 1) * fs] = dp
            dn = dn + _dot_nt(dp, w1_v[j])
        xv = h_ref[...]
        g = g_ref[...]
        _, rstd = _rms_fwd(xv, g)
        dx, dg = _rms_bwd(dn, xv, rstd, g)
        dx_ref[...] = dhv + dx
        dg_ref[...] += dg

    return _pallas(
        body, [dh, h, norm_g, p, w1, w2], name=f"bwd_mlp{layer}", grid=(nt,),
        in_specs=[_row_spec(tm, D_MODEL), _row_spec(tm, D_MODEL), _full_spec((1, D_MODEL)), _row_spec(tm, D_FF), ANY, ANY],
        out_specs=[_row_spec(tm, D_MODEL), _row_spec(tm, D_FF), _full_spec((1, D_MODEL))],
        out_shape=[jax.ShapeDtypeStruct((tokens, D_MODEL), F32), jax.ShapeDtypeStruct((tokens, D_FF), BF16),
                   jax.ShapeDtypeStruct((1, D_MODEL), F32)],
        scratch_shapes=[pltpu.VMEM((N_CHIPS, D_MODEL, fs), BF16), pltpu.VMEM((N_CHIPS, fs, D_MODEL), BF16),
                        pltpu.SemaphoreType.DMA],
        vmem_mib=56, riders=riders)


def _bwd_odd(dh, h, norm_g, s, sv, w_in, ln_g, ln_b, w_s, w_out, *, tm, riders=()):
    tokens = h.shape[0]
    nt = tokens // tm
    cs = 2 * C_DIM // N_CHIPS

    def body(dh_ref, h_ref, g_ref, s_ref, sv_ref, win_hbm, lng_ref, lnb_ref, ws_ref, wout_hbm,
             dx_ref, ds_ref, dg_ref, dbin_ref, dlng_ref, dlnb_ref, dws_ref, dbs_ref,
             win_v, wout_v, bdt, dws_acc, dbs_acc, dvn, sem):
        i = pl.program_id(0)

        @pl.when(i == 0)
        def _():
            _load(win_hbm, win_v, sem)
            _load(wout_hbm, wout_v, sem)
            mask_t = _triu_mask()
            bdt[...] = jnp.zeros(bdt.shape, BF16)
            for g in range(C_GROUPS):
                wt = jnp.where(mask_t, ws_ref[g].T, 0.0).astype(BF16)
                bdt[g, 0:CHUNK, 0:CHUNK] = wt
                bdt[g, CHUNK:PAIR, CHUNK:PAIR] = wt
            dws_acc[...] = jnp.zeros(dws_acc.shape, F32)
            dbs_acc[...] = jnp.zeros(dbs_acc.shape, F32)
            dg_ref[...] = jnp.zeros(dg_ref.shape, F32)
            dbin_ref[...] = jnp.zeros(dbin_ref.shape, F32)
            dlng_ref[...] = jnp.zeros(dlng_ref.shape, F32)
            dlnb_ref[...] = jnp.zeros(dlnb_ref.shape, F32)

        dhv = dh_ref[...]
        dy = _dot_nt(dhv.astype(BF16), wout_v[...])
        sf = s_ref[...].astype(F32)
        cdf, pdf = _gelu_parts(sf)
        zz = sf * cdf
        dgelu = cdf + sf * pdf
        u, v = zz[:, 0:C_DIM], zz[:, C_DIM:2 * C_DIM]
        xhat, rs = _ln_stats(v)
        lng = lng_ref[...]
        vn = (xhat * lng + lnb_ref[...]).astype(BF16)
        du = dy * sv_ref[...].astype(F32)
        dsv = dy * u
        dsvb = dsv.astype(BF16)
        for g in range(C_GROUPS):
            cols = slice(g * CHUNK, (g + 1) * CHUNK)
            for r0 in range(0, tm, PAIR):
                blk = dsvb[r0:r0 + PAIR, cols]
                dvn[r0:r0 + PAIR, cols] = _dot(bdt[g], blk)
                dws_acc[g] += _dot_nt(blk, vn[r0:r0 + PAIR, cols])
                dbs_acc[g] += dsv[r0:r0 + CHUNK, cols] + dsv[r0 + CHUNK:r0 + PAIR, cols]
        dv, dlng, dlnb = _ln_bwd(dvn[...], xhat, rs, lng)
        dlng_ref[...] += dlng
        dlnb_ref[...] += dlnb
        ds = jnp.concatenate([du, dv], axis=1) * dgelu
        dbin_ref[...] += jnp.sum(ds, axis=0, keepdims=True)
        dsb = ds.astype(BF16)
        ds_ref[...] = dsb
        dn = jnp.zeros((tm, D_MODEL), F32)
        for j in range(N_CHIPS):
            dn = dn + _dot_nt(dsb[:, j * cs:(j + 1) * cs], win_v[j])
        xv = h_ref[...]
        g = g_ref[...]
        _, rstd = _rms_fwd(xv, g)
        dx, dg = _rms_bwd(dn, xv, rstd, g)
        dx_ref[...] = dhv + dx
        dg_ref[...] += dg

        @pl.when(i == nt - 1)
        def _():
            mask = _tril_mask()
            for g in range(C_GROUPS):
                full = dws_acc[g]
                dws_ref[g] = jnp.where(mask, full[0:CHUNK, 0:CHUNK] + full[CHUNK:PAIR, CHUNK:PAIR], 0.0)
                dbs_ref[g] = jnp.sum(dbs_acc[g], axis=1, keepdims=True)

    row = lambda cols: jax.ShapeDtypeStruct((1, cols), F32)
    return _pallas(
        body, [dh, h, norm_g, s, sv, w_in, ln_g, ln_b, w_s, w_out], name="bwd_odd", grid=(nt,),
        in_specs=[_row_spec(tm, D_MODEL), _row_spec(tm, D_MODEL), _full_spec((1, D_MODEL)), _row_spec(tm, 2 * C_DIM),
                  _row_spec(tm, C_DIM), ANY, _full_spec((1, C_DIM)), _full_spec((1, C_DIM)),
                  _full_spec((C_GROUPS, CHUNK, CHUNK)), ANY],
        out_specs=[_row_spec(tm, D_MODEL), _row_spec(tm, 2 * C_DIM), _full_spec((1, D_MODEL)), _full_spec((1, 2 * C_DIM)),
                   _full_spec((1, C_DIM)), _full_spec((1, C_DIM)), _full_spec((C_GROUPS, CHUNK, CHUNK)),
                   _full_spec((C_GROUPS, CHUNK, 1))],
        out_shape=[jax.ShapeDtypeStruct((tokens, D_MODEL), F32), jax.ShapeDtypeStruct((tokens, 2 * C_DIM), BF16),
                   row(D_MODEL), row(2 * C_DIM), row(C_DIM), row(C_DIM),
                   jax.ShapeDtypeStruct((C_GROUPS, CHUNK, CHUNK), F32), jax.ShapeDtypeStruct((C_GROUPS, CHUNK, 1), F32)],
        scratch_shapes=[pltpu.VMEM((N_CHIPS, D_MODEL, cs), BF16), pltpu.VMEM((C_DIM, D_MODEL), BF16),
                        pltpu.VMEM((C_GROUPS, PAIR, PAIR), BF16), pltpu.VMEM((C_GROUPS, PAIR, PAIR), F32),
                        pltpu.VMEM((C_GROUPS, CHUNK, CHUNK), F32), pltpu.VMEM((tm, C_DIM), F32),
                        pltpu.SemaphoreType.DMA],
        vmem_mib=56, riders=riders)


def _bwd_even(dh, x, norm_g, z, a2, cv, w_in, conv_a_w, ln_g, ln_b, conv_b_w, w_out, *, tm, seq, riders=()):
    tokens = x.shape[0]
    nt, tps = tokens // tm, seq // tm
    ws = IN_EVEN // N_CHIPS

    def body(dh_ref, x_ref, g_ref, z_ref, a2_ref, cv_ref, win_hbm, caw_ref, lng_ref, lnb_ref, cbw_ref, wout_hbm,
             dx_ref, dz_ref, dg_ref, dcaw_ref, dcab_ref, dlng_ref, dlnb_ref, dcbw_ref,
             win_v, wout_v, ea, eb, a1s, da1s, dw_acc, sem):
        i = pl.program_id(0)

        @pl.when(i == 0)
        def _():
            _load(win_hbm, win_v, sem)
            _load(wout_hbm, wout_v, sem)
            dw_acc[...] = jnp.zeros(dw_acc.shape, F32)
            for ref in (dg_ref, dcab_ref, dlng_ref, dlnb_ref, dcbw_ref):
                ref[...] = jnp.zeros(ref.shape, F32)

        dhv = dh_ref[...]
        dmix = _dot_nt(dhv.astype(BF16), wout_v[...])
        da4, dbo = dmix[:, 0:A_DIM], dmix[:, A_DIM:A_DIM + B_DIM]
        zf = z_ref[...].astype(F32)
        a_val, a_gate = zf[:, 0:A_DIM], zf[:, A_DIM:2 * A_DIM]
        b_gate, c_gate, b_val = zf[:, 1024:1536], zf[:, 1536:2048], zf[:, 2048:2560]

        xhat, rs = _ln_stats(a2_ref[...])
        lng = lng_ref[...]
        a3 = xhat * lng + lnb_ref[...]
        sg = jax.nn.sigmoid(a3)
        da3 = da4 * (sg * (1.0 + a3 * (1.0 - sg)))
        da2, dlng, dlnb = _ln_bwd(da3, xhat, rs, lng)
        dlng_ref[...] += dlng
        dlnb_ref[...] += dlnb
        dcab_ref[...] += jnp.sum(da2, axis=0, keepdims=True)

        last = ((nt - 1 - i) % tps) == tps - 1
        dcv = dbo * b_gate

        @pl.when(last)
        def _():
            ea[tm:tm + A_HALO, :] = jnp.zeros((A_HALO, A_DIM), F32)
            eb[tm:tm + B_HALO, :] = jnp.zeros((B_HALO, B_DIM), F32)

        @pl.when(jnp.logical_not(last))
        def _():
            ea[tm:tm + A_HALO, :] = ea[0:A_HALO, :]
            eb[tm:tm + B_HALO, :] = eb[0:B_HALO, :]

        ea[0:tm, :] = da2
        eb[0:tm, :] = dcv
        sig = jax.nn.sigmoid(a_gate)
        a1s[...] = a_val * sig
        for r0 in range(0, tm, CONV_ROWS):
            a1c = a1s[r0:r0 + CONV_ROWS, :]
            acc = jnp.zeros((CONV_ROWS, A_DIM), F32)
            for j in range(A_CONV_WIDTH):
                k = A_CONV_WIDTH - 1 - j
                sl = ea[r0 + j:r0 + j + CONV_ROWS, :]
                acc = acc + caw_ref[k:k + 1, :] * sl
                dw_acc[k] += sl * a1c
            da1s[r0:r0 + CONV_ROWS, :] = acc
        da1 = da1s[...]
        da_val = da1 * sig
        da_gate = da1 * a_val * (sig * (1.0 - sig))

        db_gate = dbo * cv_ref[...].astype(F32)
        cb = c_gate * b_val
        dcb = jnp.zeros((tm, B_DIM), F32)
        for j in range(B_CONV_WIDTH):
            k = B_CONV_WIDTH - 1 - j
            sl = eb[j:j + tm, :]
            dcb = dcb + cbw_ref[k:k + 1, :] * sl
            dcbw_ref[k:k + 1, :] += jnp.sum(sl * cb, axis=0, keepdims=True)
        dz = jnp.concatenate([da_val, da_gate, db_gate, dcb * b_val, dcb * c_gate], axis=1).astype(BF16)
        dz_ref[...] = dz
        dn = jnp.zeros((tm, D_MODEL), F32)
        for j in range(N_CHIPS):
            dn = dn + _dot_nt(dz[:, j * ws:(j + 1) * ws], win_v[j])
        xv = x_ref[...]
        g = g_ref[...]
        _, rstd = _rms_fwd(xv, g)
        dx, dg = _rms_bwd(dn, xv, rstd, g)
        dx_ref[...] = dhv + dx
        dg_ref[...] += dg

        @pl.when(i == nt - 1)
        def _():
            for k in range(A_CONV_WIDTH):
                dcaw_ref[k:k + 1, :] = jnp.sum(dw_acc[k], axis=0, keepdims=True)

    row = lambda cols: jax.ShapeDtypeStruct((1, cols), F32)
    rs_ = functools.partial(_row_spec, rev_nt=nt)
    return _pallas(
        body, [dh, x, norm_g, z, a2, cv, w_in, conv_a_w, ln_g, ln_b, conv_b_w, w_out], name="bwd_even", grid=(nt,),
        in_specs=[rs_(tm, D_MODEL), rs_(tm, D_MODEL), _full_spec((1, D_MODEL)), rs_(tm, IN_EVEN), rs_(tm, A_DIM),
                  rs_(tm, B_DIM), ANY, _full_spec((A_CONV_WIDTH, A_DIM)), _full_spec((1, A_DIM)), _full_spec((1, A_DIM)),
                  _full_spec((B_CONV_WIDTH, B_DIM)), ANY],
        out_specs=[rs_(tm, D_MODEL), rs_(tm, IN_EVEN), _full_spec((1, D_MODEL)), _full_spec((A_CONV_WIDTH, A_DIM)),
                   _full_spec((1, A_DIM)), _full_spec((1, A_DIM)), _full_spec((1, A_DIM)), _full_spec((B_CONV_WIDTH, B_DIM))],
        out_shape=[jax.ShapeDtypeStruct((tokens, D_MODEL), F32), jax.ShapeDtypeStruct((tokens, IN_EVEN), BF16),
                   row(D_MODEL), jax.ShapeDtypeStruct((A_CONV_WIDTH, A_DIM), F32), row(A_DIM), row(A_DIM), row(A_DIM),
                   jax.ShapeDtypeStruct((B_CONV_WIDTH, B_DIM), F32)],
        scratch_shapes=[pltpu.VMEM((N_CHIPS, D_MODEL, ws), BF16), pltpu.VMEM((D_MODEL, D_MODEL), BF16),
                        pltpu.VMEM((tm + A_HALO, A_DIM), F32), pltpu.VMEM((tm + B_HALO, B_DIM), F32),
                        pltpu.VMEM((tm, A_DIM), F32), pltpu.VMEM((tm, A_DIM), F32),
                        pltpu.VMEM((A_CONV_WIDTH, CONV_ROWS, A_DIM), F32), pltpu.SemaphoreType.DMA],
        vmem_mib=56, riders=riders)


def _wgrad(a, b, name, *, col_shards, riders=()):
    tokens, m = a.shape
    n = b.shape[1]
    kc = 512
    if col_shards:
        bm, bn = m // 2, n // N_CHIPS
        grid = (2, N_CHIPS)
        out_spec = pl.BlockSpec((None, None, bm, bn), lambda i, j: (j, i, 0, 0))
    else:
        bm, bn = m // 8, n
        grid = (8, 1)
        out_spec = pl.BlockSpec((None, None, bm, bn), lambda i, j: (i // 2, i % 2, 0, 0))

    def body(a_ref, b_ref, o_ref):
        acc = jnp.zeros((bm, bn), F32)
        for k0 in range(0, tokens, kc):
            acc = acc + _dot_tn(a_ref[k0:k0 + kc, :].astype(BF16), b_ref[k0:k0 + kc, :].astype(BF16))
        o_ref[...] = acc

    outs, routs = _pallas(
        body, [a, b], name=name, grid=grid,
        in_specs=[pl.BlockSpec((tokens, bm), lambda i, j: (0, i)), pl.BlockSpec((tokens, bn), lambda i, j: (0, j))],
        out_specs=[out_spec], out_shape=[jax.ShapeDtypeStruct((N_CHIPS, 2, bm, bn), F32)],
        vmem_mib=56, riders=riders)
    return outs[0], routs


class _GradReduce:
    def __init__(self, name, grad):
        self.name, self.grad = name, grad
        self.from_sibling = self.chip_sum = self.from_chips = self.full = None

    def pair_swap(self):
        return _PairSwap([self.grad])

    def took_pair(self, outs):
        self.chip_sum = _add_pair(self.grad, outs[0], f"pair_sum_{self.name}")

    def chip_swap(self):
        return _ChipSwap([self.chip_sum])

    def took_chips(self, outs):
        self.full = _add_chips(self.chip_sum, outs[0], f"chip_sum_{self.name}")

    def pair_share(self):
        return _PairShare([self.full])

    def took_share(self, outs):
        self.full = outs[0]

    def reduced(self):
        return jnp.reshape(self.full, (2 * self.full.shape[1], self.full.shape[2]))


def _forward_backward(x2, tgt2, gathered, staged, conv_a_w, conv_b_w, od_norm, od_bias, od_lng, od_lnb,
                      ev_norm_g, ev_conv_a_b, ev_ln_a_g, ev_ln_a_b, od_w_s, od_b_s, mlp_norm_g, final_norm_g,
                      *, tm, seq, distributed=True):
    d = x2.shape[1]
    w = dict(gathered)
    b_s_rows = jnp.broadcast_to(od_b_s[0][:, :, None], (C_GROUPS, CHUNK, CHUNK))

    def ride(*names):
        return [_Gather([staged[nm] for nm in names])] if distributed else []

    def land(routs, *names):
        if distributed:
            for nm, buf in zip(names, routs[0]):
                w[nm] = buf

    def as_cols(buf):
        return jnp.reshape(buf, (N_CHIPS, 2 * buf.shape[2], buf.shape[3]))

    def as_rows(buf):
        return jnp.reshape(buf, (8 * buf.shape[2], buf.shape[3]))

    (h1, n0, z, a2, cv, mix), routs = _fwd_even(
        x2, ev_norm_g, as_cols(w["ev_in"]), conv_a_w, ev_conv_a_b, ev_ln_a_g, ev_ln_a_b, conv_b_w, as_rows(w["ev_out"]),
        tm=tm, seq=seq, riders=ride("w1_0", "w2_0"))
    land(routs, "w1_0", "w2_0")
    (h2, n1, p0, q0), routs = _fwd_mlp(h1, mlp_norm_g[0:1], as_cols(w["w1_0"]), as_cols(w["w2_0"]), 0, tm=tm,
                                       riders=ride("od_in", "od_out", "w1_1"))
    land(routs, "od_in", "od_out", "w1_1")
    (h3, n2, s, sv, y), routs = _fwd_odd(h2, od_norm, as_cols(w["od_in"]), od_bias, od_lng, od_lnb, od_w_s[0], b_s_rows,
                                         as_rows(w["od_out"]), tm=tm, riders=ride("w2_1"))
    land(routs, "w2_1")
    (h4, n3, p1, q1), _ = _fwd_mlp(h3, mlp_norm_g[1:2], as_cols(w["w1_1"]), as_cols(w["w2_1"]), 1, tm=tm)
    loss_part, dh4, d_final_g = _loss_head(h4, jnp.reshape(final_norm_g, (1, d)), tgt2, tm=tm)

    red = {}

    def swap(*names):
        return [red[nm].pair_swap() for nm in names] if distributed else []

    def chips(*names):
        return [red[nm].chip_swap() for nm in names] if distributed else []

    def share(*names):
        return [red[nm].pair_share() for nm in names] if distributed else []

    def took(routs, *steps):
        if distributed:
            for (nm, what), outs in zip(steps, routs):
                getattr(red[nm], what)(outs)

    g, _ = _wgrad(q1, dh4, "wgrad_w2_1", col_shards=False)
    red["w2_1"] = _GradReduce("w2_1", g)
    (dh3, dp1, d_mlp_g1), routs = _bwd_mlp(dh4, h3, mlp_norm_g[1:2], p1, as_cols(w["w1_1"]), as_cols(w["w2_1"]), 1, tm=tm,
                                           riders=swap("w2_1"))
    took(routs, ("w2_1", "took_pair"))
    g, _ = _wgrad(n3, dp1, "wgrad_w1_1", col_shards=True)
    red["w1_1"] = _GradReduce("w1_1", g)
    g, routs = _wgrad(y, dh3, "wgrad_od_out", col_shards=False, riders=swap("w1_1"))
    red["od_out"] = _GradReduce("od_out", g)
    took(routs, ("w1_1", "took_pair"))
    (dh2, ds, d_od_norm, d_od_bin, d_od_lng, d_od_lnb, d_ws, d_bs), routs = _bwd_odd(
        dh3, h2, od_norm, s, sv, as_cols(w["od_in"]), od_lng, od_lnb, od_w_s[0], as_rows(w["od_out"]), tm=tm,
        riders=chips("w2_1") + swap("od_out"))
    took(routs, ("w2_1", "took_chips"), ("od_out", "took_pair"))
    g, routs = _wgrad(n2, ds, "wgrad_od_in", col_shards=True, riders=share("w2_1"))
    red["od_in"] = _GradReduce("od_in", g)
    took(routs, ("w2_1", "took_share"))
    g, routs = _wgrad(q0, dh2, "wgrad_w2_0", col_shards=False, riders=swap("od_in"))
    red["w2_0"] = _GradReduce("w2_0", g)
    took(routs, ("od_in", "took_pair"))
    (dh1, dp0, d_mlp_g0), routs = _bwd_mlp(dh2, h1, mlp_norm_g[0:1], p0, as_cols(w["w1_0"]), as_cols(w["w2_0"]), 0, tm=tm,
                                           riders=chips("w1_1") + chips("od_out") + chips("od_in") + swap("w2_0"))
    took(routs, ("w1_1", "took_chips"), ("od_out", "took_chips"), ("od_in", "took_chips"), ("w2_0", "took_pair"))
    g, routs = _wgrad(n1, dp0, "wgrad_w1_0", col_shards=True, riders=share("w1_1") + share("od_out") + share("od_in"))
    red["w1_0"] = _GradReduce("w1_0", g)
    took(routs, ("w1_1", "took_share"), ("od_out", "took_share"), ("od_in", "took_share"))
    g, routs = _wgrad(mix, dh1, "wgrad_ev_out", col_shards=False, riders=swap("w1_0"))
    red["ev_out"] = _GradReduce("ev_out", g)
    took(routs, ("w1_0", "took_pair"))

    early = {"od_w_s": d_ws, "od_b_s": d_bs, "mlp_norm_g0": d_mlp_g0, "mlp_norm_g1": d_mlp_g1, "final_norm_g": d_final_g,
             "od_norm_g": d_od_norm, "od_b_in": d_od_bin, "od_ln_v_g": d_od_lng, "od_ln_v_b": d_od_lnb}
    share_early = [_ShareAll(_place_slab(_pack(list(early.values())), "place_small_early"))] if distributed else []
    (dx, dz, d_ev_norm, d_caw, d_cab, d_ev_lng, d_ev_lnb, d_cbw), routs = _bwd_even(
        dh1, x2, ev_norm_g, z, a2, cv, as_cols(w["ev_in"]), conv_a_w, ev_ln_a_g, ev_ln_a_b, conv_b_w, as_rows(w["ev_out"]),
        tm=tm, seq=seq, riders=chips("w2_0") + chips("w1_0") + swap("ev_out") + share_early)
    took(routs, ("w2_0", "took_chips"), ("w1_0", "took_chips"), ("ev_out", "took_pair"))
    late = {"ev_norm_g": d_ev_norm, "ev_conv_a_b": d_cab, "ev_ln_a_g": d_ev_lng, "ev_ln_a_b": d_ev_lnb,
            "ev_conv_a_w": d_caw, "ev_conv_b_w": d_cbw}
    share_late = [_ShareAll(_place_slab(_pack(list(late.values())), "place_small_late"))] if distributed else []
    g, routs2 = _wgrad(n0, dz, "wgrad_ev_in", col_shards=True,
                       riders=chips("ev_out") + share("w2_0") + share("w1_0") + share_late)
    red["ev_in"] = _GradReduce("ev_in", g)
    took(routs2, ("ev_out", "took_chips"), ("w2_0", "took_share"), ("w1_0", "took_share"))
    small_all = (routs[3][0], routs2[3][0]) if distributed else None
    return loss_part, dx, red, early, late, small_all


def _rows128(a):
    rows = jnp.reshape(a, (-1, LANES))
    pad = (-rows.shape[0]) % SUBLANES
    return jnp.pad(rows, ((0, pad), (0, 0))) if pad else rows


def _pack(arrays):
    return jnp.concatenate([_rows128(a) for a in arrays], axis=0)


def _unpack(buf, shapes):
    out, r0 = [], 0
    for shp in shapes:
        size = 1
        for dim in shp:
            size *= dim
        nr = size // LANES
        out.append(jnp.reshape(buf[r0:r0 + nr], shp))
        r0 += nr + (-nr) % SUBLANES
    return out


def kernel(x, ev_norm_g, ev_w_in, ev_conv_a_w, ev_conv_a_b, ev_ln_a_g, ev_ln_a_b, ev_conv_b_w, ev_w_out, od_norm_g, od_w_in, od_b_in, od_ln_v_g, od_ln_v_b, od_w_s, od_b_s, od_w_out, mlp_norm_g, mlp_w1, mlp_w2, final_norm_g, loss_target, m_ev_norm_g, m_ev_w_in, m_ev_conv_a_w, m_ev_conv_a_b, m_ev_ln_a_g, m_ev_ln_a_b, m_ev_conv_b_w, m_ev_w_out, m_od_norm_g, m_od_w_in, m_od_b_in, m_od_ln_v_g, m_od_ln_v_b, m_od_w_s, m_od_b_s, m_od_w_out, m_mlp_norm_g, m_mlp_w1, m_mlp_w2, m_final_norm_g, v_ev_norm_g, v_ev_w_in, v_ev_conv_a_w, v_ev_conv_a_b, v_ev_ln_a_g, v_ev_ln_a_b, v_ev_conv_b_w, v_ev_w_out, v_od_norm_g, v_od_w_in, v_od_b_in, v_od_ln_v_g, v_od_ln_v_b, v_od_w_s, v_od_b_s, v_od_w_out, v_mlp_norm_g, v_mlp_w1, v_mlp_w2, v_final_norm_g):
    tm = TOKEN_TILE
    batch, seq, d = x.shape
    tokens = batch * seq
    x2 = jnp.reshape(x, (tokens, d))
    tgt2 = jnp.reshape(loss_target, (tokens, d))
    chip = 2 * lax.axis_index("x") + lax.axis_index("y")

    small_shapes = [(A_CONV_WIDTH, LANES), (B_CONV_WIDTH, LANES), (256,), (512,), (256,), (256,)]
    small_shard = _pack([ev_conv_a_w[0], ev_conv_b_w[0], od_norm_g[0], od_b_in[0], od_ln_v_g[0], od_ln_v_b[0]])
    small_shard = jnp.pad(small_shard, ((0, (-small_shard.shape[0]) % (2 * SUBLANES)), (0, 0)))
    first = [_place_shard(ev_w_in, 0, BF16, "place_ev_w_in"), _place_shard(ev_w_out, 0, BF16, "place_ev_w_out"),
             _place_shard(small_shard[None], 0, F32, "place_small")]
    staged = {
        "w1_0": _place_shard(mlp_w1, 0, BF16, "place_w1_0"), "w2_0": _place_shard(mlp_w2, 0, BF16, "place_w2_0"),
        "od_in": _place_shard(od_w_in, 0, BF16, "place_od_w_in"), "od_out": _place_shard(od_w_out, 0, BF16, "place_od_w_out"),
        "w1_1": _place_shard(mlp_w1, 1, BF16, "place_w1_1"), "w2_1": _place_shard(mlp_w2, 1, BF16, "place_w2_1"),
    }
    (g_ev_in, g_ev_out, g_small), = _exchange([_Gather(first)], "gather_first")
    small_all = jnp.reshape(g_small, (N_CHIPS, -1, LANES))
    per_chip = [_unpack(small_all[q], small_shapes) for q in range(N_CHIPS)]
    conv_a_w = jnp.concatenate([pc[0] for pc in per_chip], axis=1)
    conv_b_w = jnp.concatenate([pc[1] for pc in per_chip], axis=1)
    od_norm = jnp.concatenate([pc[2] for pc in per_chip])[None, :]
    od_bias = jnp.concatenate([pc[3] for pc in per_chip])[None, :]
    od_lng = jnp.concatenate([pc[4] for pc in per_chip])[None, :]
    od_lnb = jnp.concatenate([pc[5] for pc in per_chip])[None, :]

    loss_part, dx, red, early, late, small_all = _forward_backward(
        x2, tgt2, {"ev_in": g_ev_in, "ev_out": g_ev_out}, staged, conv_a_w, conv_b_w, od_norm, od_bias, od_lng, od_lnb,
        ev_norm_g, ev_conv_a_b, ev_ln_a_g, ev_ln_a_b, od_w_s, od_b_s, mlp_norm_g, final_norm_g, tm=tm, seq=seq)
    loss = lax.psum(loss_part[0, 0], ("x", "y", "c"))

    routs = _exchange([red["ev_in"].pair_swap(), red["ev_out"].pair_share()], "reduce_tail_1")
    red["ev_in"].took_pair(routs[0])
    red["ev_out"].took_share(routs[1])
    routs = _exchange([red["ev_in"].chip_swap()], "reduce_tail_2")
    red["ev_in"].took_chips(routs[0])
    routs = _exchange([red["ev_in"].pair_share()], "reduce_tail_3")
    red["ev_in"].took_share(routs[0])

    small_red = {}
    for group, all8 in zip((early, late), small_all):
        shapes = [g.shape for g in group.values()]
        small_red.update(zip(group.keys(), _unpack(_sum_devices(all8), shapes)))
    by_chip = lambda g, width: lax.dynamic_slice_in_dim(jnp.reshape(g, (-1, N_CHIPS * width)), chip * width, width, axis=1)
    small_g = [
        small_red["ev_norm_g"], small_red["ev_conv_a_b"], small_red["ev_ln_a_g"], small_red["ev_ln_a_b"],
        small_red["od_w_s"], small_red["od_b_s"], jnp.concatenate([small_red["mlp_norm_g0"], small_red["mlp_norm_g1"]], axis=0),
        small_red["final_norm_g"], by_chip(small_red["ev_conv_a_w"], LANES), by_chip(small_red["ev_conv_b_w"], LANES),
        by_chip(small_red["od_norm_g"], 256), by_chip(small_red["od_b_in"], 512), by_chip(small_red["od_ln_v_g"], 256),
        by_chip(small_red["od_ln_v_b"], 256)]

    def big_update(wt, m, v, names, call):
        grads = [red[nm].reduced() for nm in names]
        shp3 = (len(grads),) + grads[0].shape
        outs, _ = _adamw(jnp.reshape(wt, shp3), jnp.reshape(m, shp3), jnp.reshape(v, shp3), grads, call)
        return [jnp.reshape(o, wt.shape) for o in outs], None

    upd = {}
    upd["mlp_w2"], _ = big_update(mlp_w2, m_mlp_w2, v_mlp_w2, ["w2_0", "w2_1"], "adamw_mlp_w2")
    upd["mlp_w1"], _ = big_update(mlp_w1, m_mlp_w1, v_mlp_w1, ["w1_0", "w1_1"], "adamw_mlp_w1")
    upd["ev_w_in"], _ = big_update(ev_w_in, m_ev_w_in, v_ev_w_in, ["ev_in"], "adamw_ev_w_in")
    upd["ev_w_out"], _ = big_update(ev_w_out, m_ev_w_out, v_ev_w_out, ["ev_out"], "adamw_ev_w_out")
    upd["od_w_in"], _ = big_update(od_w_in, m_od_w_in, v_od_w_in, ["od_in"], "adamw_od_w_in")
    upd["od_w_out"], _ = big_update(od_w_out, m_od_w_out, v_od_w_out, ["od_out"], "adamw_od_w_out")

    small_names = ["ev_norm_g", "ev_conv_a_b", "ev_ln_a_g", "ev_ln_a_b", "od_w_s", "od_b_s", "mlp_norm_g", "final_norm_g",
                   "ev_conv_a_w", "ev_conv_b_w", "od_norm_g", "od_b_in", "od_ln_v_g", "od_ln_v_b"]
    small_w = [ev_norm_g, ev_conv_a_b, ev_ln_a_g, ev_ln_a_b, od_w_s, od_b_s, mlp_norm_g, final_norm_g,
               ev_conv_a_w, ev_conv_b_w, od_norm_g, od_b_in, od_ln_v_g, od_ln_v_b]
    small_m = [m_ev_norm_g, m_ev_conv_a_b, m_ev_ln_a_g, m_ev_ln_a_b, m_od_w_s, m_od_b_s, m_mlp_norm_g, m_final_norm_g,
               m_ev_conv_a_w, m_ev_conv_b_w, m_od_norm_g, m_od_b_in, m_od_ln_v_g, m_od_ln_v_b]
    small_v = [v_ev_norm_g, v_ev_conv_a_b, v_ev_ln_a_g, v_ev_ln_a_b, v_od_w_s, v_od_b_s, v_mlp_norm_g, v_final_norm_g,
               v_ev_conv_a_w, v_ev_conv_b_w, v_od_norm_g, v_od_b_in, v_od_ln_v_g, v_od_ln_v_b]
    packed =[_pack(group)[None] for group in (small_w, small_m, small_v)]
    outs, _ = _adamw(packed[0], packed[1], packed[2], [_pack(small_g)], "adamw_small")
    small_out_shapes = [wt.shape for wt in small_w]
    small_outs = [_unpack(o[0], small_out_shapes) for o in outs]
    for idx, nm in enumerate(small_names):
        upd[nm] = [small_outs[kind][idx] for kind in range(4)]

    order = ["ev_norm_g", "ev_w_in", "ev_conv_a_w", "ev_conv_a_b", "ev_ln_a_g", "ev_ln_a_b", "ev_conv_b_w", "ev_w_out",
             "od_norm_g", "od_w_in", "od_b_in", "od_ln_v_g", "od_ln_v_b", "od_w_s", "od_b_s", "od_w_out", "mlp_norm_g",
             "mlp_w1", "mlp_w2", "final_norm_g"]
    grad_x = jnp.reshape(dx, x.shape)
    return (loss, grad_x, *[upd[nm][0] for nm in order], *[upd[nm][1] for nm in order],
            *[upd[nm][2] for nm in order], *[upd[nm][3] for nm in order])
```

```python
import functools

import jax
import jax.numpy as jnp
from jax import lax
from jax.experimental import pallas as pl
from jax.experimental.pallas import tpu as pltpu

F32 = jnp.float32
BF16 = jnp.bfloat16

D_MODEL = 1024
A_DIM = 512
B_DIM = 512
IN_EVEN = 2 * A_DIM + 3 * B_DIM
A_CONV_WIDTH = 31
B_CONV_WIDTH = 3
CHUNK = 128
C_GROUPS = 8
C_DIM = 1024
D_FF = 4096
RMS_EPS = 1e-6
LN_EPS = 1e-5
ADAM_LR = 0.001
ADAM_B1 = 0.9
ADAM_B2 = 0.999
ADAM_EPS = 1e-08
ADAM_WD = 0.01
ADAM_STEP = 10

N_CHIPS = 4
N_DEV = 8
TOKEN_TILE = 512
A_HALO = 32
B_HALO = 8
CONV_ROWS = 16
PAIR = 2 * CHUNK
LANES = 128
SUBLANES = 8
MIB = 1024 * 1024
MESH = pl.DeviceIdType.MESH
ANY = pl.BlockSpec(memory_space=pl.ANY)


def _dot(a, b):
    return lax.dot_general(a, b, (((1,), (0,)), ((), ())), preferred_element_type=F32)


def _dot_nt(a, b):
    return lax.dot_general(a, b, (((1,), (1,)), ((), ())), preferred_element_type=F32)


def _dot_tn(a, b):
    return lax.dot_general(a, b, (((0,), (0,)), ((), ())), preferred_element_type=F32)


def _params(vmem_mib, n_axes=1):
    return pltpu.CompilerParams(dimension_semantics=("arbitrary",) * n_axes, vmem_limit_bytes=vmem_mib * MIB)


def _row_spec(tm, cols, rev_nt=None):
    if rev_nt is None:
        return pl.BlockSpec((tm, cols), lambda i: (i, 0))
    return pl.BlockSpec((tm, cols), lambda i: (rev_nt - 1 - i, 0))


def _full_spec(shape):
    nd = len(shape)
    return pl.BlockSpec(shape, lambda i: (0,) * nd)


def _block_rows(rows, cap=512):
    best = SUBLANES
    for br in range(SUBLANES, min(rows, cap) + 1, SUBLANES):
        if rows % br == 0:
            best = br
    return best


def _load(src, dst, sem):
    cp = pltpu.make_async_copy(src, dst, sem)
    cp.start()
    cp.wait()


def _rms_fwd(x, g):
    rstd = lax.rsqrt(jnp.mean(x * x, axis=-1, keepdims=True) + RMS_EPS)
    return x * rstd * g, rstd


def _rms_bwd(dn, x, rstd, g):
    a = dn * g
    xh = x * rstd
    dx = rstd * (a - xh * jnp.mean(a * xh, axis=-1, keepdims=True))
    dg = jnp.sum(dn * xh, axis=0, keepdims=True)
    return dx, dg


def _ln_stats(v):
    mu = jnp.mean(v, axis=-1, keepdims=True)
    xc = v - mu
    rs = lax.rsqrt(jnp.mean(xc * xc, axis=-1, keepdims=True) + LN_EPS)
    return xc * rs, rs


def _ln_bwd(dy, xhat, rs, g):
    dxh = dy * g
    dv = rs * (dxh - jnp.mean(dxh, axis=-1, keepdims=True) - xhat * jnp.mean(dxh * xhat, axis=-1, keepdims=True))
    return dv, jnp.sum(dy * xhat, axis=0, keepdims=True), jnp.sum(dy, axis=0, keepdims=True)


def _gelu_parts(s):
    cdf = 0.5 * (1.0 + lax.erf(s * 0.7071067811865476))
    return cdf, jnp.exp(-0.5 * s * s) * 0.3989422804014327


def _mesh_pos():
    return lax.axis_index("x"), lax.axis_index("y"), lax.axis_index("c")


def _other_chips(x, y):
    return [(1 - x, y), (x, 1 - y), (1 - x, 1 - y)]


def _remote(src, dst, send_sem, recv_sem, to):
    return pltpu.make_async_remote_copy(src_ref=src, dst_ref=dst, send_sem=send_sem, recv_sem=recv_sem,
                                        device_id=to, device_id_type=MESH)


def _like(arrays):
    return [jax.ShapeDtypeStruct(a.shape, a.dtype) for a in arrays]


class _Gather:
    def __init__(self, bufs):
        self.ins = list(bufs)
        self.out_shapes = _like(bufs)
        self.aliases = {t: t for t in range(len(bufs))}
        self.n_sems = 6 * len(bufs)

    def _ici(self, ins, outs, send, recv, t, k, chip, mine, c):
        return _remote(ins[t].at[mine, c], outs[t].at[mine, c], send.at[6 * t + k], recv.at[6 * t + k], (*chip, c))

    def start(self, ins, outs, send, recv):
        x, y, c = _mesh_pos()
        for t in range(len(ins)):
            for k, chip in enumerate(_other_chips(x, y)):
                self._ici(ins, outs, send, recv, t, k, chip, 2 * x + y, c).start()

    def finish(self, ins, outs, send, recv):
        x, y, c = _mesh_pos()
        me, sibling = (x, y, c), (x, y, 1 - c)
        chips = _other_chips(x, y)
        passed = []
        for t in range(len(ins)):
            for k, chip in enumerate(chips):
                blk = outs[t].at[2 * chip[0] + chip[1], c]
                _remote(blk, blk, send.at[6 * t + k], recv.at[6 * t + k], me).wait_recv()
                cp = _remote(blk, blk, send.at[6 * t + 3 + k], recv.at[6 * t + 3 + k], sibling)
                cp.start()
                passed.append(cp)
        for t in range(len(ins)):
            for k, chip in enumerate(chips):
                blk = outs[t].at[2 * chip[0] + chip[1], 1 - c]
                _remote(blk, blk, send.at[6 * t + 3 + k], recv.at[6 * t + 3 + k], me).wait_recv()
        for t in range(len(ins)):
            for k, chip in enumerate(chips):
                self._ici(ins, outs, send, recv, t, k, chip, 2 * x + y, c).wait_send()
        for cp in passed:
            cp.wait_send()


class _PairSwap:
    def __init__(self, grads):
        self.ins = list(grads)
        self.out_shapes = [jax.ShapeDtypeStruct((g.shape[0],) + g.shape[2:], g.dtype) for g in grads]
        self.aliases = {}
        self.n_sems = len(grads)

    def _copies(self, ins, outs, send, recv):
        x, y, c = _mesh_pos()
        return [_remote(ins[t].at[:, 1 - c], outs[t], send.at[t], recv.at[t], (x, y, 1 - c)) for t in range(len(ins))]

    def start(self, ins, outs, send, recv):
        for cp in self._copies(ins, outs, send, recv):
            cp.start()

    def finish(self, ins, outs, send, recv):
        for cp in self._copies(ins, outs, send, recv):
            cp.wait()


class _ChipSwap:
    def __init__(self, parts):
        self.ins = list(parts)
        self.out_shapes = [jax.ShapeDtypeStruct((3,) + p.shape[1:], p.dtype) for p in parts]
        self.aliases = {}
        self.n_sems = 3 * len(parts)

    def _copies(self, ins, outs, send, recv):
        x, y, c = _mesh_pos()
        return [_remote(ins[t].at[2 * chip[0] + chip[1]], outs[t].at[k], send.at[3 * t + k], recv.at[3 * t + k], (*chip, c))
                for t in range(len(ins)) for k, chip in enumerate(_other_chips(x, y))]

    def start(self, ins, outs, send, recv):
        for cp in self._copies(ins, outs, send, recv):
            cp.start()

    def finish(self, ins, outs, send, recv):
        for cp in self._copies(ins, outs, send, recv):
            cp.wait()


class _PairShare:
    def __init__(self, fulls):
        self.ins = list(fulls)
        self.out_shapes = _like(fulls)
        self.aliases = {t: t for t in range(len(fulls))}
        self.n_sems = len(fulls)

    def _copies(self, ins, outs, send, recv):
        x, y, c = _mesh_pos()
        return [_remote(ins[t].at[c], outs[t].at[c], send.at[t], recv.at[t], (x, y, 1 - c)) for t in range(len(ins))]

    def start(self, ins, outs, send, recv):
        for cp in self._copies(ins, outs, send, recv):
            cp.start()

    def finish(self, ins, outs, send, recv):
        for cp in self._copies(ins, outs, send, recv):
            cp.wait()


class _ShareAll:
    def __init__(self, arrays):
        self.ins = list(arrays)
        self.out_shapes = [jax.ShapeDtypeStruct((N_DEV,) + a.shape, a.dtype) for a in arrays]
        self.aliases = {}
        self.n_sems = (N_DEV - 1) * len(arrays)

    def _peers(self):
        x, y, c = _mesh_pos()
        flips = [((r >> 2) & 1, (r >> 1) & 1, r & 1) for r in range(1, N_DEV)]
        return (x, y, c), [(x ^ fx, y ^ fy, c ^ fc) for fx, fy, fc in flips]

    def _sends(self, ins, outs, send, recv):
        (x, y, c), peers = self._peers()
        mine = 4 * x + 2 * y + c
        return [_remote(ins[a], outs[a].at[mine], send.at[7 * a + r], recv.at[7 * a + r], peer)
                for a in range(len(ins)) for r, peer in enumerate(peers)]

    def start(self, ins, outs, send, recv):
        for cp in self._sends(ins, outs, send, recv):
            cp.start()

    def finish(self, ins, outs, send, recv):
        (x, y, c), peers = self._peers()
        for a in range(len(ins)):
            for r, (px, py, pc) in enumerate(peers):
                blk = outs[a].at[4 * px + 2 * py + pc]
                _remote(blk, blk, send.at[7 * a + r], recv.at[7 * a + r], (x, y, c)).wait_recv()
        for cp in self._sends(ins, outs, send, recv):
            cp.wait_send()


def _pallas(body, operands, *, name, grid, in_specs, out_specs, out_shape, scratch_shapes=(), vmem_mib=32, riders=()):
    in_specs, out_specs, out_shape, scratch_shapes = list(in_specs), list(out_specs), list(out_shape), list(scratch_shapes)
    if not riders:
        outs = pl.pallas_call(body, name=name, grid=grid, in_specs=in_specs, out_specs=out_specs, out_shape=out_shape,
                              scratch_shapes=scratch_shapes, compiler_params=_params(vmem_mib, len(grid)))(*operands)
        return list(outs), []
    n_in, n_out, n_scr = len(in_specs), len(out_specs), len(scratch_shapes)
    r_in = [len(r.ins) for r in riders]
    r_out = [len(r.out_shapes) for r in riders]
    steps = 1
    for g in grid:
        steps *= g

    def wrapped(*refs):
        refs = list(refs)
        ins, refs = refs[:n_in], refs[n_in:]
        rins = []
        for k in r_in:
            rins.append(refs[:k])
            refs = refs[k:]
        outs, refs = refs[:n_out], refs[n_out:]
        routs = []
        for k in r_out:
            routs.append(refs[:k])
            refs = refs[k:]
        scr, sems = refs[:n_scr], refs[n_scr:]
        step = 0
        for ax, g in enumerate(grid):
            step = step * g + pl.program_id(ax)

        def each(what):
            for j, r in enumerate(riders):
                getattr(r, what)(rins[j], routs[j], sems[2 * j], sems[2 * j + 1])

        if grid:
            pl.when(step == 0)(lambda: each("start"))
        else:
            each("start")
        body(*ins, *outs, *scr)
        if grid:
            pl.when(step == steps - 1)(lambda: each("finish"))
        else:
            each("finish")

    aliases, off_in, off_out = {}, n_in, n_out
    for r, ki, ko in zip(riders, r_in, r_out):
        for i, o in r.aliases.items():
            aliases[off_in + i] = off_out + o
        off_in, off_out = off_in + ki, off_out + ko
    sems = []
    for r in riders:
        sems += [pltpu.SemaphoreType.DMA((r.n_sems,)), pltpu.SemaphoreType.DMA((r.n_sems,))]
    res = pl.pallas_call(
        wrapped, name=name, grid=grid,
        in_specs=in_specs + [ANY] * sum(r_in), out_specs=out_specs + [ANY] * sum(r_out),
        out_shape=out_shape + [s for r in riders for s in r.out_shapes],
        scratch_shapes=scratch_shapes + sems, input_output_aliases=aliases,
        compiler_params=pltpu.CompilerParams(dimension_semantics=("arbitrary",) * len(grid),
                                             vmem_limit_bytes=vmem_mib * MIB, has_side_effects=True),
    )(*operands, *[a for r in riders for a in r.ins])
    res = list(res)
    outs, res = res[:n_out], res[n_out:]
    routs = []
    for k in r_out:
        routs.append(res[:k])
        res = res[k:]
    return outs, routs


def _exchange(riders, name):
    return _pallas(lambda: None, [], name=name, grid=(), in_specs=[], out_specs=[], out_shape=[], riders=riders)[1]


def _in_hbm(a):
    return pltpu.with_memory_space_constraint(a, pltpu.HBM)


def _place_shard(w, layer, dtype, name):
    _, rows, cols = w.shape
    half = rows // 2
    br = _block_rows(half)
    nb = half // br
    mine = 2 * lax.axis_index("x") + lax.axis_index("y")

    def body(q_ref, w_ref, o_ref):
        o_ref[...] = w_ref[...].astype(dtype)

    return pl.pallas_call(
        body, name=name,
        grid_spec=pltpu.PrefetchScalarGridSpec(
            num_scalar_prefetch=1, grid=(2, nb),
            in_specs=[pl.BlockSpec((None, br, cols), lambda h, i, q: (layer, h * nb + i, 0))],
            out_specs=pl.BlockSpec((None, None, br, cols), lambda h, i, q: (q[0], h, i, 0))),
        out_shape=jax.ShapeDtypeStruct((N_CHIPS, 2, half, cols), dtype),
        compiler_params=_params(16, 2),
    )(jnp.reshape(mine, (1,)).astype(jnp.int32), w)


def _add_pair(g, recv, name):
    _, _, r, cdim = g.shape
    br = _block_rows(r, 256)
    c = lax.axis_index("c")

    def body(c_ref, g_ref, r_ref, o_ref):
        o_ref[...] = (g_ref[...] + r_ref[...]).astype(BF16)

    return pl.pallas_call(
        body, name=name,
        grid_spec=pltpu.PrefetchScalarGridSpec(
            num_scalar_prefetch=1, grid=(N_CHIPS, r // br),
            in_specs=[pl.BlockSpec((None, None, br, cdim), lambda q, i, c_ref: (q, c_ref[0], i, 0)),
                      pl.BlockSpec((None, br, cdim), lambda q, i, c_ref: (q, i, 0))],
            out_specs=pl.BlockSpec((None, br, cdim), lambda q, i, c_ref: (q, i, 0))),
        out_shape=jax.ShapeDtypeStruct((N_CHIPS, r, cdim), BF16),
        compiler_params=_params(16, 2),
    )(jnp.reshape(c, (1,)).astype(jnp.int32), _in_hbm(g), _in_hbm(recv))


def _add_chips(own, recv, name):
    _, r, cdim = own.shape
    br = _block_rows(r, 256)
    x, y, c = _mesh_pos()

    def body(pos_ref, own_ref, r_ref, o_ref):
        acc = own_ref[...].astype(F32)
        for k in range(3):
            acc = acc + r_ref[k].astype(F32)
        o_ref[...] = acc

    return pl.pallas_call(
        body, name=name,
        grid_spec=pltpu.PrefetchScalarGridSpec(
            num_scalar_prefetch=1, grid=(r // br,),
            in_specs=[pl.BlockSpec((None, br, cdim), lambda i, pos: (pos[0], i, 0)),
                      pl.BlockSpec((3, br, cdim), lambda i, pos: (0, i, 0))],
            out_specs=pl.BlockSpec((None, br, cdim), lambda i, pos: (pos[1], i, 0))),
        out_shape=jax.ShapeDtypeStruct((2, r, cdim), F32),
        compiler_params=_params(16, 1),
    )(jnp.stack([2 * x + y, c]).astype(jnp.int32), _in_hbm(own), _in_hbm(recv))


def _adam_math(w, m, v, g):
    c1 = 1.0 / (1.0 - ADAM_B1 ** ADAM_STEP)
    c2 = 1.0 / (1.0 - ADAM_B2 ** ADAM_STEP)
    m_new = ADAM_B1 * m + (1.0 - ADAM_B1) * g
    v_new = ADAM_B2 * v + (1.0 - ADAM_B2) * (g * g)
    return -ADAM_LR * ((m_new * c1) / (jnp.sqrt(v_new * c2) + ADAM_EPS) + ADAM_WD * w), m_new, v_new


SMALL_WEIGHTS = [
    ("ev_norm_g", (1, D_MODEL), ["ev_norm_g"], None), ("ev_conv_a_b", (1, A_DIM), ["ev_conv_a_b"], None),
    ("ev_ln_a_g", (1, A_DIM), ["ev_ln_a_g"], None), ("ev_ln_a_b", (1, A_DIM), ["ev_ln_a_b"], None),
    ("od_w_s", (C_GROUPS, CHUNK, CHUNK), ["od_w_s"], None), ("od_b_s", (C_GROUPS, CHUNK), ["od_b_s"], None),
    ("mlp_norm_g", (2, D_MODEL), ["mlp_norm_g0", "mlp_norm_g1"], None), ("final_norm_g", (1, D_MODEL), ["final_norm_g"], None),
    ("ev_conv_a_w", (A_CONV_WIDTH, A_DIM // N_CHIPS), ["ev_conv_a_w"], A_DIM // N_CHIPS),
    ("ev_conv_b_w", (B_CONV_WIDTH, B_DIM // N_CHIPS), ["ev_conv_b_w"], B_DIM // N_CHIPS),
    ("od_norm_g", (1, D_MODEL // N_CHIPS), ["od_norm_g"], D_MODEL // N_CHIPS),
    ("od_b_in", (1, 2 * C_DIM // N_CHIPS), ["od_b_in"], 2 * C_DIM // N_CHIPS),
    ("od_ln_v_g", (1, C_DIM // N_CHIPS), ["od_ln_v_g"], C_DIM // N_CHIPS),
    ("od_ln_v_b", (1, C_DIM // N_CHIPS), ["od_ln_v_b"], C_DIM // N_CHIPS),
]


def _small_update(own, landed, weights):
    names = list(own.keys())
    n_g, n_w = len(names), len(SMALL_WEIGHTS)

    def body(*refs):
        refs = list(refs)
        own_refs = dict(zip(names, refs[:n_g]))
        land_refs = dict(zip(names, refs[n_g:2 * n_g]))
        wmv = [refs[2 * n_g + 3 * i:2 * n_g + 3 * i + 3] for i in range(n_w)]
        o0 = 2 * n_g + 3 * n_w
        loss_ref = refs[o0]
        outs = [refs[o0 + 1 + 4 * i:o0 + 5 + 4 * i] for i in range(n_w)]
        acc = dict(zip(names, refs[o0 + 1 + 4 * n_w:]))
        x, y, c = _mesh_pos()
        mine, chip = 4 * x + 2 * y + c, 2 * x + y

        for nm in names:
            for d in range(N_DEV):
                def add(term, nm=nm, d=d):
                    acc[nm][...] = term if d == 0 else acc[nm][...] + term
                pl.when(mine == d)(lambda nm=nm, add=add: add(own_refs[nm][...]))
                pl.when(mine != d)(lambda nm=nm, d=d, add=add: add(land_refs[nm][d]))
        loss_ref[...] = acc["loss"][...]

        def update(i, rows, g):
            w_ref, m_ref, v_ref = wmv[i]
            delta, m_new, v_new = _adam_math(w_ref[rows], m_ref[rows], v_ref[rows], g)
            for ref, val in zip(outs[i], (g, delta, m_new, v_new)):
                ref[rows] = val

        for i, (_, shape, grads, per_chip) in enumerate(SMALL_WEIGHTS):
            for row, gname in enumerate(grads):
                rows = slice(row, row + 1) if len(grads) > 1 else slice(None)
                if per_chip is None:
                    update(i, rows, acc[gname][...])
                else:
                    for q in range(N_CHIPS):
                        pl.when(chip == q)(lambda i=i, rows=rows, gname=gname, q=q, per_chip=per_chip:
                                           update(i, rows, acc[gname][:, q * per_chip:(q + 1) * per_chip]))

    vmem = pl.BlockSpec(memory_space=pltpu.VMEM)
    operands = [own[nm] for nm in names] + [landed[nm] for nm in names]
    for nm, _, _, _ in SMALL_WEIGHTS:
        operands += list(weights[nm])
    out_shape = [jax.ShapeDtypeStruct((1, 1), F32)]
    for _, shape, _, _ in SMALL_WEIGHTS:
        out_shape += [jax.ShapeDtypeStruct(shape, F32)] * 4
    res = pl.pallas_call(
        body, name="small_update", in_specs=[vmem] * len(operands), out_specs=[vmem] * len(out_shape), out_shape=out_shape,
        scratch_shapes=[pltpu.VMEM(own[nm].shape, F32) for nm in names],
        compiler_params=pltpu.CompilerParams(vmem_limit_bytes=32 * MIB),
    )(*operands)
    return res[0], {nm: res[1 + 4 * i:5 + 4 * i] for i, (nm, _, _, _) in enumerate(SMALL_WEIGHTS)}


def _adamw(w, m, v, grads, name, riders=()):
    layers, r, cdim = w.shape
    br = _block_rows(r, 256 if cdim > LANES else 1024)

    def body(*refs):
        w_ref, m_ref, v_ref = refs[:3]
        g_refs = refs[3:3 + layers]
        go_ref, d_ref, mo_ref, vo_ref = refs[3 + layers:]
        layer = pl.program_id(0)
        for l in range(layers):
            @pl.when(layer == l)
            def _(l=l):
                g = g_refs[l][...]
                go_ref[...] = g
                d_ref[...], mo_ref[...], vo_ref[...] = _adam_math(w_ref[...], m_ref[...], v_ref[...], g)

    spec3 = pl.BlockSpec((None, br, cdim), lambda l, i: (l, i, 0))
    spec2 = pl.BlockSpec((br, cdim), lambda l, i: (i, 0))
    out = jax.ShapeDtypeStruct((layers, r, cdim), F32)
    return _pallas(body, [w, m, v, *grads], name=name, grid=(layers, r // br),
                   in_specs=[spec3, spec3, spec3] + [spec2] * layers, out_specs=[spec3] * 4, out_shape=[out] * 4,
                   vmem_mib=32, riders=riders)


def _conv31(src, w_ref, rows, base, init):
    acc = init
    for k in range(A_CONV_WIDTH):
        acc = acc + w_ref[k:k + 1, :] * src[base + k + rows.start:base + k + rows.stop, :]
    return acc


def _fwd_even(x, norm_g, w_in, conv_a_w, conv_a_b, ln_g, ln_b, conv_b_w, w_out, *, tm, seq, riders=()):
    tokens = x.shape[0]
    nt, tps = tokens // tm, seq // tm

    def body(x_ref, g_ref, win_hbm, caw_ref, cab_ref, lng_ref, lnb_ref, cbw_ref, wout_hbm,
             h_ref, n_ref, z_ref, a2_ref, cv_ref, mix_ref, win_v, wout_v, pa, pb, sem):
        i = pl.program_id(0)

        @pl.when(i == 0)
        def _():
            _load(win_hbm, win_v, sem)
            _load(wout_hbm, wout_v, sem)

        xv = x_ref[...]
        nf, _ = _rms_fwd(xv, g_ref[...])
        n = nf.astype(BF16)
        n_ref[...] = n
        z = jnp.concatenate([_dot(n, win_v[j]) for j in range(N_CHIPS)], axis=1)
        z_ref[...] = z.astype(BF16)
        a_val, a_gate = z[:, 0:A_DIM], z[:, A_DIM:2 * A_DIM]
        b_gate, c_gate, b_val = z[:, 1024:1536], z[:, 1536:2048], z[:, 2048:2560]

        first = (i % tps) == 0

        @pl.when(first)
        def _():
            pa[0:A_HALO, :] = jnp.zeros((A_HALO, A_DIM), F32)
            pb[0:B_HALO, :] = jnp.zeros((B_HALO, B_DIM), F32)

        @pl.when(jnp.logical_not(first))
        def _():
            pa[0:A_HALO, :] = pa[tm:tm + A_HALO, :]
            pb[0:B_HALO, :] = pb[tm:tm + B_HALO, :]

        pa[A_HALO:A_HALO + tm, :] = a_val * jax.nn.sigmoid(a_gate)
        pb[B_HALO:B_HALO + tm, :] = c_gate * b_val
        bias = jnp.broadcast_to(cab_ref[...], (CONV_ROWS, A_DIM))
        for r0 in range(0, tm, CONV_ROWS):
            rows = slice(r0, r0 + CONV_ROWS)
            a2_ref[rows, :] = _conv31(pa, caw_ref, rows, A_HALO - (A_CONV_WIDTH - 1), bias)
        xhat, _ = _ln_stats(a2_ref[...])
        a3 = xhat * lng_ref[...] + lnb_ref[...]
        a4 = a3 * jax.nn.sigmoid(a3)
        cv = cbw_ref[0:1, :] * pb[B_HALO - 2:B_HALO - 2 + tm, :]
        cv = cv + cbw_ref[1:2, :] * pb[B_HALO - 1:B_HALO - 1 + tm, :]
        cv = cv + cbw_ref[2:3, :] * pb[B_HALO:B_HALO + tm, :]
        cv_ref[...] = cv.astype(BF16)
        mix = jnp.concatenate([a4, b_gate * cv], axis=1).astype(BF16)
        mix_ref[...] = mix
        h_ref[...] = xv + _dot(mix, wout_v[...])

    shp = lambda cols, dt: jax.ShapeDtypeStruct((tokens, cols), dt)
    return _pallas(
        body, [x, norm_g, w_in, conv_a_w, conv_a_b, ln_g, ln_b, conv_b_w, w_out], name="fwd_even", grid=(nt,),
        in_specs=[_row_spec(tm, D_MODEL), _full_spec((1, D_MODEL)), ANY, _full_spec((A_CONV_WIDTH, A_DIM)),
                  _full_spec((1, A_DIM)), _full_spec((1, A_DIM)), _full_spec((1, A_DIM)),
                  _full_spec((B_CONV_WIDTH, B_DIM)), ANY],
        out_specs=[_row_spec(tm, D_MODEL), _row_spec(tm, D_MODEL), _row_spec(tm, IN_EVEN), _row_spec(tm, A_DIM),
                   _row_spec(tm, B_DIM), _row_spec(tm, D_MODEL)],
        out_shape=[shp(D_MODEL, F32), shp(D_MODEL, BF16), shp(IN_EVEN, BF16), shp(A_DIM, F32), shp(B_DIM, BF16),
                   shp(D_MODEL, BF16)],
        scratch_shapes=[pltpu.VMEM((N_CHIPS, D_MODEL, IN_EVEN // N_CHIPS), BF16), pltpu.VMEM((D_MODEL, D_MODEL), BF16),
                        pltpu.VMEM((A_HALO + tm, A_DIM), F32), pltpu.VMEM((B_HALO + tm, B_DIM), F32),
                        pltpu.SemaphoreType.DMA],
        vmem_mib=56, riders=riders)


def _fwd_mlp(h, norm_g, w1, w2, layer, *, tm, riders=()):
    tokens = h.shape[0]
    nt = tokens // tm
    fs = D_FF // N_CHIPS

    def body(h_ref, g_ref, w1_hbm, w2_hbm, ho_ref, n_ref, p_ref, q_ref, w1_v, w2_v, sem):
        @pl.when(pl.program_id(0) == 0)
        def _():
            _load(w1_hbm, w1_v, sem)
            _load(w2_hbm, w2_v, sem)

        xv = h_ref[...]
        nf, _ = _rms_fwd(xv, g_ref[...])
        n = nf.astype(BF16)
        n_ref[...] = n
        acc = xv
        for j in range(N_CHIPS):
            p = _dot(n, w1_v[j])
            p_ref[:, j * fs:(j + 1) * fs] = p.astype(BF16)
            r = jnp.maximum(p, 0.0)
            q = (r * r).astype(BF16)
            q_ref[:, j * fs:(j + 1) * fs] = q
            acc = acc + _dot(q, w2_v[j])
        ho_ref[...] = acc

    shp = lambda cols, dt: jax.ShapeDtypeStruct((tokens, cols), dt)
    return _pallas(
        body, [h, norm_g, w1, w2], name=f"fwd_mlp{layer}", grid=(nt,),
        in_specs=[_row_spec(tm, D_MODEL), _full_spec((1, D_MODEL)), ANY, ANY],
        out_specs=[_row_spec(tm, D_MODEL), _row_spec(tm, D_MODEL), _row_spec(tm, D_FF), _row_spec(tm, D_FF)],
        out_shape=[shp(D_MODEL, F32), shp(D_MODEL, BF16), shp(D_FF, BF16), shp(D_FF, BF16)],
        scratch_shapes=[pltpu.VMEM((N_CHIPS, D_MODEL, fs), BF16), pltpu.VMEM((N_CHIPS, fs, D_MODEL), BF16),
                        pltpu.SemaphoreType.DMA],
        vmem_mib=56, riders=riders)


def _tril_mask():
    row = lax.broadcasted_iota(jnp.int32, (CHUNK, CHUNK), 0)
    col = lax.broadcasted_iota(jnp.int32, (CHUNK, CHUNK), 1)
    return row >= col


def _triu_mask():
    row = lax.broadcasted_iota(jnp.int32, (CHUNK, CHUNK), 0)
    col = lax.broadcasted_iota(jnp.int32, (CHUNK, CHUNK), 1)
    return row <= col


def _fwd_odd(h, norm_g, w_in, b_in, ln_g, ln_b, w_s, b_s_rows, w_out, *, tm, riders=()):
    tokens = h.shape[0]
    nt = tokens // tm
    cs = 2 * C_DIM // N_CHIPS

    def body(h_ref, g_ref, win_hbm, bin_ref, lng_ref, lnb_ref, ws_ref, bs_ref, wout_hbm,
             ho_ref, n_ref, s_ref, sv_ref, y_ref, win_v, wout_v, bd, sem):
        @pl.when(pl.program_id(0) == 0)
        def _():
            _load(win_hbm, win_v, sem)
            _load(wout_hbm, wout_v, sem)
            mask = _tril_mask()
            bd[...] = jnp.zeros(bd.shape, BF16)
            for g in range(C_GROUPS):
                w = jnp.where(mask, ws_ref[g], 0.0).astype(BF16)
                bd[g, 0:CHUNK, 0:CHUNK] = w
                bd[g, CHUNK:PAIR, CHUNK:PAIR] = w

        xv = h_ref[...]
        nf, _ = _rms_fwd(xv, g_ref[...])
        n = nf.astype(BF16)
        n_ref[...] = n
        s = jnp.concatenate([_dot(n, win_v[j]) for j in range(N_CHIPS)], axis=1) + bin_ref[...]
        s_ref[...] = s.astype(BF16)
        cdf, _ = _gelu_parts(s)
        zz = s * cdf
        u, v = zz[:, 0:C_DIM], zz[:, C_DIM:2 * C_DIM]
        xhat, _ = _ln_stats(v)
        vn = (xhat * lng_ref[...] + lnb_ref[...]).astype(BF16)
        for g in range(C_GROUPS):
            cols = slice(g * CHUNK, (g + 1) * CHUNK)
            bias = jnp.concatenate([bs_ref[g], bs_ref[g]], axis=0)
            for r0 in range(0, tm, PAIR):
                sv = _dot(bd[g], vn[r0:r0 + PAIR, cols]) + bias
                sv_ref[r0:r0 + PAIR, cols] = sv.astype(BF16)
                y_ref[r0:r0 + PAIR, cols] = (u[r0:r0 + PAIR, cols] * sv).astype(BF16)
        ho_ref[...] = xv + _dot(y_ref[...], wout_v[...])

    shp = lambda cols, dt: jax.ShapeDtypeStruct((tokens, cols), dt)
    return _pallas(
        body, [h, norm_g, w_in, b_in, ln_g, ln_b, w_s, b_s_rows, w_out], name="fwd_odd", grid=(nt,),
        in_specs=[_row_spec(tm, D_MODEL), _full_spec((1, D_MODEL)), ANY, _full_spec((1, 2 * C_DIM)),
                  _full_spec((1, C_DIM)), _full_spec((1, C_DIM)), _full_spec((C_GROUPS, CHUNK, CHUNK)),
                  _full_spec((C_GROUPS, CHUNK, CHUNK)), ANY],
        out_specs=[_row_spec(tm, D_MODEL), _row_spec(tm, D_MODEL), _row_spec(tm, 2 * C_DIM), _row_spec(tm, C_DIM),
                   _row_spec(tm, C_DIM)],
        out_shape=[shp(D_MODEL, F32), shp(D_MODEL, BF16), shp(2 * C_DIM, BF16), shp(C_DIM, BF16), shp(C_DIM, BF16)],
        scratch_shapes=[pltpu.VMEM((N_CHIPS, D_MODEL, cs), BF16), pltpu.VMEM((C_DIM, D_MODEL), BF16),
                        pltpu.VMEM((C_GROUPS, PAIR, PAIR), BF16), pltpu.SemaphoreType.DMA],
        vmem_mib=56, riders=riders)


def _loss_head(h, norm_g, target, *, tm):
    tokens = h.shape[0]
    nt = tokens // tm

    def body(h_ref, g_ref, t_ref, loss_ref, dh_ref, dg_ref):
        @pl.when(pl.program_id(0) == 0)
        def _():
            loss_ref[...] = jnp.zeros((1, 1), F32)
            dg_ref[...] = jnp.zeros((1, D_MODEL), F32)

        xv = h_ref[...]
        g = g_ref[...]
        out, rstd = _rms_fwd(xv, g)
        err = out - t_ref[...]
        per_token = jnp.sum(err * err, axis=1, keepdims=True) * (1.0 / D_MODEL)
        loss_ref[...] += 0.5 * jnp.sum(per_token, axis=0, keepdims=True)
        dx, dg = _rms_bwd(err * (1.0 / D_MODEL), xv, rstd, g)
        dh_ref[...] = dx
        dg_ref[...] += dg

    return _pallas(
        body, [h, norm_g, target], name="loss_head", grid=(nt,),
        in_specs=[_row_spec(tm, D_MODEL), _full_spec((1, D_MODEL)), _row_spec(tm, D_MODEL)],
        out_specs=[_full_spec((1, 1)), _row_spec(tm, D_MODEL), _full_spec((1, D_MODEL))],
        out_shape=[jax.ShapeDtypeStruct((1, 1), F32), jax.ShapeDtypeStruct((tokens, D_MODEL), F32),
                   jax.ShapeDtypeStruct((1, D_MODEL), F32)],
        vmem_mib=32)[0]


def _bwd_mlp(dh, h, norm_g, p, w1, w2, layer, *, tm, riders=()):
    tokens = h.shape[0]
    nt = tokens // tm
    fs = D_FF // N_CHIPS

    def body(dh_ref, h_ref, g_ref, p_ref, w1_hbm, w2_hbm, dx_ref, dp_ref, dg_ref, w1_v, w2_v, sem):
        @pl.when(pl.program_id(0) == 0)
        def _():
            _load(w1_hbm, w1_v, sem)
            _load(w2_hbm, w2_v, sem)
            dg_ref[...] = jnp.zeros((1, D_MODEL), F32)

        dhv = dh_ref[...]
        dhb = dhv.astype(BF16)
        dn = jnp.zeros((tm, D_MODEL), F32)
        for j in range(N_CHIPS):
            dq = _dot_nt(dhb, w2_v[j])
            r = jnp.maximum(p_ref[:, j * fs:(j + 1) * fs].astype(F32), 0.0)
            dp = ((2.0 * r) * dq).astype(BF16)
            dp_ref[:, j * fs:(j + 1) * fs] = dp
            dn = dn + _dot_nt(dp, w1_v[j])
        xv = h_ref[...]
        g = g_ref[...]
        _, rstd = _rms_fwd(xv, g)
        dx, dg = _rms_bwd(dn, xv, rstd, g)
        dx_ref[...] = dhv + dx
        dg_ref[...] += dg

    return _pallas(
        body, [dh, h, norm_g, p, w1, w2], name=f"bwd_mlp{layer}", grid=(nt,),
        in_specs=[_row_spec(tm, D_MODEL), _row_spec(tm, D_MODEL), _full_spec((1, D_MODEL)), _row_spec(tm, D_FF), ANY, ANY],
        out_specs=[_row_spec(tm, D_MODEL), _row_spec(tm, D_FF), _full_spec((1, D_MODEL))],
        out_shape=[jax.ShapeDtypeStruct((tokens, D_MODEL), F32), jax.ShapeDtypeStruct((tokens, D_FF), BF16),
                   jax.ShapeDtypeStruct((1, D_MODEL), F32)],
        scratch_shapes=[pltpu.VMEM((N_CHIPS, D_MODEL, fs), BF16), pltpu.VMEM((N_CHIPS, fs, D_MODEL), BF16),
                        pltpu.SemaphoreType.DMA],
        vmem_mib=56, riders=riders)


def _bwd_odd(dh, h, norm_g, s, sv, w_in, ln_g, ln_b, w_s, w_out, *, tm, riders=()):
    tokens = h.shape[0]
    nt = tokens // tm
    cs = 2 * C_DIM // N_CHIPS

    def body(dh_ref, h_ref, g_ref, s_ref, sv_ref, win_hbm, lng_ref, lnb_ref, ws_ref, wout_hbm,
             dx_ref, ds_ref, dg_ref, dbin_ref, dlng_ref, dlnb_ref, dws_ref, dbs_ref,
             win_v, wout_v, bdt, dws_acc, dbs_acc, dvn, sem):
        i = pl.program_id(0)

        @pl.when(i == 0)
        def _():
            _load(win_hbm, win_v, sem)
            _load(wout_hbm, wout_v, sem)
            mask_t = _triu_mask()
            bdt[...] = jnp.zeros(bdt.shape, BF16)
            for g in range(C_GROUPS):
                wt = jnp.where(mask_t, ws_ref[g].T, 0.0).astype(BF16)
                bdt[g, 0:CHUNK, 0:CHUNK] = wt
                bdt[g, CHUNK:PAIR, CHUNK:PAIR] = wt
            dws_acc[...] = jnp.zeros(dws_acc.shape, F32)
            dbs_acc[...] = jnp.zeros(dbs_acc.shape, F32)
            dg_ref[...] = jnp.zeros(dg_ref.shape, F32)
            dbin_ref[...] = jnp.zeros(dbin_ref.shape, F32)
            dlng_ref[...] = jnp.zeros(dlng_ref.shape, F32)
            dlnb_ref[...] = jnp.zeros(dlnb_ref.shape, F32)

        dhv = dh_ref[...]
        dy = _dot_nt(dhv.astype(BF16), wout_v[...])
        sf = s_ref[...].astype(F32)
        cdf, pdf = _gelu_parts(sf)
        zz = sf * cdf
        dgelu = cdf + sf * pdf
        u, v = zz[:, 0:C_DIM], zz[:, C_DIM:2 * C_DIM]
        xhat, rs = _ln_stats(v)
        lng = lng_ref[...]
        vn = (xhat * lng + lnb_ref[...]).astype(BF16)
        du = dy * sv_ref[...].astype(F32)
        dsv = dy * u
        dsvb = dsv.astype(BF16)
        for g in range(C_GROUPS):
            cols = slice(g * CHUNK, (g + 1) * CHUNK)
            for r0 in range(0, tm, PAIR):
                blk = dsvb[r0:r0 + PAIR, cols]
                dvn[r0:r0 + PAIR, cols] = _dot(bdt[g], blk)
                dws_acc[g] += _dot_nt(blk, vn[r0:r0 + PAIR, cols])
                dbs_acc[g] += dsv[r0:r0 + CHUNK, cols] + dsv[r0 + CHUNK:r0 + PAIR, cols]
        dv, dlng, dlnb = _ln_bwd(dvn[...], xhat, rs, lng)
        dlng_ref[...] += dlng
        dlnb_ref[...] += dlnb
        ds = jnp.concatenate([du, dv], axis=1) * dgelu
        dbin_ref[...] += jnp.sum(ds, axis=0, keepdims=True)
        dsb = ds.astype(BF16)
        ds_ref[...] = dsb
        dn = jnp.zeros((tm, D_MODEL), F32)
        for j in range(N_CHIPS):
            dn = dn + _dot_nt(dsb[:, j * cs:(j + 1) * cs], win_v[j])
        xv = h_ref[...]
        g = g_ref[...]
        _, rstd = _rms_fwd(xv, g)
        dx, dg = _rms_bwd(dn, xv, rstd, g)
        dx_ref[...] = dhv + dx
        dg_ref[...] += dg

        @pl.when(i == nt - 1)
        def _():
            mask = _tril_mask()
            for g in range(C_GROUPS):
                full = dws_acc[g]
                dws_ref[g] = jnp.where(mask, full[0:CHUNK, 0:CHUNK] + full[CHUNK:PAIR, CHUNK:PAIR], 0.0)
                dbs_ref[g:g + 1, :] = jnp.sum(dbs_acc[g].T, axis=0, keepdims=True)

    row = lambda cols: jax.ShapeDtypeStruct((1, cols), F32)
    return _pallas(
        body, [dh, h, norm_g, s, sv, w_in, ln_g, ln_b, w_s, w_out], name="bwd_odd", grid=(nt,),
        in_specs=[_row_spec(tm, D_MODEL), _row_spec(tm, D_MODEL), _full_spec((1, D_MODEL)), _row_spec(tm, 2 * C_DIM),
                  _row_spec(tm, C_DIM), ANY, _full_spec((1, C_DIM)), _full_spec((1, C_DIM)),
                  _full_spec((C_GROUPS, CHUNK, CHUNK)), ANY],
        out_specs=[_row_spec(tm, D_MODEL), _row_spec(tm, 2 * C_DIM), _full_spec((1, D_MODEL)), _full_spec((1, 2 * C_DIM)),
                   _full_spec((1, C_DIM)), _full_spec((1, C_DIM)), _full_spec((C_GROUPS, CHUNK, CHUNK)),
                   _full_spec((C_GROUPS, CHUNK))],
        out_shape=[jax.ShapeDtypeStruct((tokens, D_MODEL), F32), jax.ShapeDtypeStruct((tokens, 2 * C_DIM), BF16),
                   row(D_MODEL), row(2 * C_DIM), row(C_DIM), row(C_DIM),
                   jax.ShapeDtypeStruct((C_GROUPS, CHUNK, CHUNK), F32), jax.ShapeDtypeStruct((C_GROUPS, CHUNK), F32)],
        scratch_shapes=[pltpu.VMEM((N_CHIPS, D_MODEL, cs), BF16), pltpu.VMEM((C_DIM, D_MODEL), BF16),
                        pltpu.VMEM((C_GROUPS, PAIR, PAIR), BF16), pltpu.VMEM((C_GROUPS, PAIR, PAIR), F32),
                        pltpu.VMEM((C_GROUPS, CHUNK, CHUNK), F32), pltpu.VMEM((tm, C_DIM), F32),
                        pltpu.SemaphoreType.DMA],
        vmem_mib=56, riders=riders)


def _bwd_even(dh, x, norm_g, z, a2, cv, w_in, conv_a_w, ln_g, ln_b, conv_b_w, w_out, *, tm, seq, riders=()):
    tokens = x.shape[0]
    nt, tps = tokens // tm, seq // tm
    ws = IN_EVEN // N_CHIPS

    def body(dh_ref, x_ref, g_ref, z_ref, a2_ref, cv_ref, win_hbm, caw_ref, lng_ref, lnb_ref, cbw_ref, wout_hbm,
             dx_ref, dz_ref, dg_ref, dcaw_ref, dcab_ref, dlng_ref, dlnb_ref, dcbw_ref,
             win_v, wout_v, ea, eb, a1s, da1s, dw_acc, sem):
        i = pl.program_id(0)

        @pl.when(i == 0)
        def _():
            _load(win_hbm, win_v, sem)
            _load(wout_hbm, wout_v, sem)
            dw_acc[...] = jnp.zeros(dw_acc.shape, F32)
            for ref in (dg_ref, dcab_ref, dlng_ref, dlnb_ref, dcbw_ref):
                ref[...] = jnp.zeros(ref.shape, F32)

        dhv = dh_ref[...]
        dmix = _dot_nt(dhv.astype(BF16), wout_v[...])
        da4, dbo = dmix[:, 0:A_DIM], dmix[:, A_DIM:A_DIM + B_DIM]
        zf = z_ref[...].astype(F32)
        a_val, a_gate = zf[:, 0:A_DIM], zf[:, A_DIM:2 * A_DIM]
        b_gate, c_gate, b_val = zf[:, 1024:1536], zf[:, 1536:2048], zf[:, 2048:2560]

        xhat, rs = _ln_stats(a2_ref[...])
        lng = lng_ref[...]
        a3 = xhat * lng + lnb_ref[...]
        sg = jax.nn.sigmoid(a3)
        da3 = da4 * (sg * (1.0 + a3 * (1.0 - sg)))
        da2, dlng, dlnb = _ln_bwd(da3, xhat, rs, lng)
        dlng_ref[...] += dlng
        dlnb_ref[...] += dlnb
        dcab_ref[...] += jnp.sum(da2, axis=0, keepdims=True)

        last = ((nt - 1 - i) % tps) == tps - 1
        dcv = dbo * b_gate

        @pl.when(last)
        def _():
            ea[tm:tm + A_HALO, :] = jnp.zeros((A_HALO, A_DIM), F32)
            eb[tm:tm + B_HALO, :] = jnp.zeros((B_HALO, B_DIM), F32)

        @pl.when(jnp.logical_not(last))
        def _():
            ea[tm:tm + A_HALO, :] = ea[0:A_HALO, :]
            eb[tm:tm + B_HALO, :] = eb[0:B_HALO, :]

        ea[0:tm, :] = da2
        eb[0:tm, :] = dcv
        sig = jax.nn.sigmoid(a_gate)
        a1s[...] = a_val * sig
        for r0 in range(0, tm, CONV_ROWS):
            a1c = a1s[r0:r0 + CONV_ROWS, :]
            acc = jnp.zeros((CONV_ROWS, A_DIM), F32)
            for j in range(A_CONV_WIDTH):
                k = A_CONV_WIDTH - 1 - j
                sl = ea[r0 + j:r0 + j + CONV_ROWS, :]
                acc = acc + caw_ref[k:k + 1, :] * sl
                dw_acc[k] += sl * a1c
            da1s[r0:r0 + CONV_ROWS, :] = acc
        da1 = da1s[...]
        da_val = da1 * sig
        da_gate = da1 * a_val * (sig * (1.0 - sig))

        db_gate = dbo * cv_ref[...].astype(F32)
        cb = c_gate * b_val
        dcb = jnp.zeros((tm, B_DIM), F32)
        for j in range(B_CONV_WIDTH):
            k = B_CONV_WIDTH - 1 - j
            sl = eb[j:j + tm, :]
            dcb = dcb + cbw_ref[k:k + 1, :] * sl
            dcbw_ref[k:k + 1, :] += jnp.sum(sl * cb, axis=0, keepdims=True)
        dz = jnp.concatenate([da_val, da_gate, db_gate, dcb * b_val, dcb * c_gate], axis=1).astype(BF16)
        dz_ref[...] = dz
        dn = jnp.zeros((tm, D_MODEL), F32)
        for j in range(N_CHIPS):
            dn = dn + _dot_nt(dz[:, j * ws:(j + 1) * ws], win_v[j])
        xv = x_ref[...]
        g = g_ref[...]
        _, rstd = _rms_fwd(xv, g)
        dx, dg = _rms_bwd(dn, xv, rstd, g)
        dx_ref[...] = dhv + dx
        dg_ref[...] += dg

        @pl.when(i == nt - 1)
        def _():
            for k in range(A_CONV_WIDTH):
                dcaw_ref[k:k + 1, :] = jnp.sum(dw_acc[k], axis=0, keepdims=True)

    row = lambda cols: jax.ShapeDtypeStruct((1, cols), F32)
    rs_ = functools.partial(_row_spec, rev_nt=nt)
    return _pallas(
        body, [dh, x, norm_g, z, a2, cv, w_in, conv_a_w, ln_g, ln_b, conv_b_w, w_out], name="bwd_even", grid=(nt,),
        in_specs=[rs_(tm, D_MODEL), rs_(tm, D_MODEL), _full_spec((1, D_MODEL)), rs_(tm, IN_EVEN), rs_(tm, A_DIM),
                  rs_(tm, B_DIM), ANY, _full_spec((A_CONV_WIDTH, A_DIM)), _full_spec((1, A_DIM)), _full_spec((1, A_DIM)),
                  _full_spec((B_CONV_WIDTH, B_DIM)), ANY],
        out_specs=[rs_(tm, D_MODEL), rs_(tm, IN_EVEN), _full_spec((1, D_MODEL)), _full_spec((A_CONV_WIDTH, A_DIM)),
                   _full_spec((1, A_DIM)), _full_spec((1, A_DIM)), _full_spec((1, A_DIM)), _full_spec((B_CONV_WIDTH, B_DIM))],
        out_shape=[jax.ShapeDtypeStruct((tokens, D_MODEL), F32), jax.ShapeDtypeStruct((tokens, IN_EVEN), BF16),
                   row(D_MODEL), jax.ShapeDtypeStruct((A_CONV_WIDTH, A_DIM), F32), row(A_DIM), row(A_DIM), row(A_DIM),
                   jax.ShapeDtypeStruct((B_CONV_WIDTH, B_DIM), F32)],
        scratch_shapes=[pltpu.VMEM((N_CHIPS, D_MODEL, ws), BF16), pltpu.VMEM((D_MODEL, D_MODEL), BF16),
                        pltpu.VMEM((tm + A_HALO, A_DIM), F32), pltpu.VMEM((tm + B_HALO, B_DIM), F32),
                        pltpu.VMEM((tm, A_DIM), F32), pltpu.VMEM((tm, A_DIM), F32),
                        pltpu.VMEM((A_CONV_WIDTH, CONV_ROWS, A_DIM), F32), pltpu.SemaphoreType.DMA],
        vmem_mib=56, riders=riders)


def _wgrad(a, b, name, *, col_shards, riders=()):
    tokens, m = a.shape
    n = b.shape[1]
    kc = 512
    if col_shards:
        bm, bn = m // 2, n // N_CHIPS
        grid = (2, N_CHIPS)
        out_spec = pl.BlockSpec((None, None, bm, bn), lambda i, j: (j, i, 0, 0))
    else:
        bm, bn = m // 8, n
        grid = (8, 1)
        out_spec = pl.BlockSpec((None, None, bm, bn), lambda i, j: (i // 2, i % 2, 0, 0))

    def body(a_ref, b_ref, o_ref):
        acc = jnp.zeros((bm, bn), F32)
        for k0 in range(0, tokens, kc):
            acc = acc + _dot_tn(a_ref[k0:k0 + kc, :].astype(BF16), b_ref[k0:k0 + kc, :].astype(BF16))
        o_ref[...] = acc

    outs, routs = _pallas(
        body, [a, b], name=name, grid=grid,
        in_specs=[pl.BlockSpec((tokens, bm), lambda i, j: (0, i)), pl.BlockSpec((tokens, bn), lambda i, j: (0, j))],
        out_specs=[out_spec], out_shape=[jax.ShapeDtypeStruct((N_CHIPS, 2, bm, bn), F32)],
        vmem_mib=56, riders=riders)
    return outs[0], routs


class _GradReduce:
    def __init__(self, name, grad):
        self.name, self.grad = name, grad
        self.from_sibling = self.chip_sum = self.from_chips = self.full = None

    def pair_swap(self):
        return _PairSwap([self.grad])

    def took_pair(self, outs):
        self.chip_sum = _add_pair(self.grad, outs[0], f"pair_sum_{self.name}")

    def chip_swap(self):
        return _ChipSwap([self.chip_sum])

    def took_chips(self, outs):
        self.full = _add_chips(self.chip_sum, outs[0], f"chip_sum_{self.name}")

    def pair_share(self):
        return _PairShare([self.full])

    def took_share(self, outs):
        self.full = outs[0]

    def reduced(self):
        return jnp.reshape(self.full, (2 * self.full.shape[1], self.full.shape[2]))


def _forward_backward(x2, tgt2, gathered, staged, conv_a_w, conv_b_w, od_norm, od_bias, od_lng, od_lnb,
                      ev_norm_g, ev_conv_a_b, ev_ln_a_g, ev_ln_a_b, od_w_s, od_b_s, mlp_norm_g, final_norm_g,
                      *, tm, seq, distributed=True):
    d = x2.shape[1]
    w = dict(gathered)
    b_s_rows = jnp.broadcast_to(od_b_s[0][:, :, None], (C_GROUPS, CHUNK, CHUNK))

    def ride(*names):
        return [_Gather([staged[nm] for nm in names])] if distributed else []

    def land(routs, *names):
        if distributed:
            for nm, buf in zip(names, routs[0]):
                w[nm] = buf

    def as_cols(buf):
        return jnp.reshape(buf, (N_CHIPS, 2 * buf.shape[2], buf.shape[3]))

    def as_rows(buf):
        return jnp.reshape(buf, (8 * buf.shape[2], buf.shape[3]))

    (h1, n0, z, a2, cv, mix), routs = _fwd_even(
        x2, ev_norm_g, as_cols(w["ev_in"]), conv_a_w, ev_conv_a_b, ev_ln_a_g, ev_ln_a_b, conv_b_w, as_rows(w["ev_out"]),
        tm=tm, seq=seq, riders=ride("w1_0", "w2_0"))
    land(routs, "w1_0", "w2_0")
    (h2, n1, p0, q0), routs = _fwd_mlp(h1, mlp_norm_g[0:1], as_cols(w["w1_0"]), as_cols(w["w2_0"]), 0, tm=tm,
                                       riders=ride("od_in", "od_out", "w1_1"))
    land(routs, "od_in", "od_out", "w1_1")
    (h3, n2, s, sv, y), routs = _fwd_odd(h2, od_norm, as_cols(w["od_in"]), od_bias, od_lng, od_lnb, od_w_s[0], b_s_rows,
                                         as_rows(w["od_out"]), tm=tm, riders=ride("w2_1"))
    land(routs, "w2_1")
    (h4, n3, p1, q1), _ = _fwd_mlp(h3, mlp_norm_g[1:2], as_cols(w["w1_1"]), as_cols(w["w2_1"]), 1, tm=tm)
    loss_part, dh4, d_final_g = _loss_head(h4, jnp.reshape(final_norm_g, (1, d)), tgt2, tm=tm)

    red = {}

    def swap(*names):
        return [red[nm].pair_swap() for nm in names] if distributed else []

    def chips(*names):
        return [red[nm].chip_swap() for nm in names] if distributed else []

    def share(*names):
        return [red[nm].pair_share() for nm in names] if distributed else []

    def took(routs, *steps):
        if distributed:
            for (nm, what), outs in zip(steps, routs):
                getattr(red[nm], what)(outs)

    g, _ = _wgrad(q1, dh4, "wgrad_w2_1", col_shards=False)
    red["w2_1"] = _GradReduce("w2_1", g)
    (dh3, dp1, d_mlp_g1), routs = _bwd_mlp(dh4, h3, mlp_norm_g[1:2], p1, as_cols(w["w1_1"]), as_cols(w["w2_1"]), 1, tm=tm,
                                           riders=swap("w2_1"))
    took(routs, ("w2_1", "took_pair"))
    g, _ = _wgrad(n3, dp1, "wgrad_w1_1", col_shards=True)
    red["w1_1"] = _GradReduce("w1_1", g)
    g, routs = _wgrad(y, dh3, "wgrad_od_out", col_shards=False, riders=swap("w1_1"))
    red["od_out"] = _GradReduce("od_out", g)
    took(routs, ("w1_1", "took_pair"))
    (dh2, ds, d_od_norm, d_od_bin, d_od_lng, d_od_lnb, d_ws, d_bs), routs = _bwd_odd(
        dh3, h2, od_norm, s, sv, as_cols(w["od_in"]), od_lng, od_lnb, od_w_s[0], as_rows(w["od_out"]), tm=tm,
        riders=chips("w2_1") + swap("od_out"))
    took(routs, ("w2_1", "took_chips"), ("od_out", "took_pair"))
    g, routs = _wgrad(n2, ds, "wgrad_od_in", col_shards=True, riders=share("w2_1"))
    red["od_in"] = _GradReduce("od_in", g)
    took(routs, ("w2_1", "took_share"))
    g, routs = _wgrad(q0, dh2, "wgrad_w2_0", col_shards=False, riders=swap("od_in"))
    red["w2_0"] = _GradReduce("w2_0", g)
    took(routs, ("od_in", "took_pair"))
    (dh1, dp0, d_mlp_g0), routs = _bwd_mlp(dh2, h1, mlp_norm_g[0:1], p0, as_cols(w["w1_0"]), as_cols(w["w2_0"]), 0, tm=tm,
                                           riders=chips("w1_1") + chips("od_out") + chips("od_in") + swap("w2_0"))
    took(routs, ("w1_1", "took_chips"), ("od_out", "took_chips"), ("od_in", "took_chips"), ("w2_0", "took_pair"))
    g, routs = _wgrad(n1, dp0, "wgrad_w1_0", col_shards=True, riders=share("w1_1") + share("od_out") + share("od_in"))
    red["w1_0"] = _GradReduce("w1_0", g)
    took(routs, ("w1_1", "took_share"), ("od_out", "took_share"), ("od_in", "took_share"))
    g, routs = _wgrad(mix, dh1, "wgrad_ev_out", col_shards=False, riders=swap("w1_0"))
    red["ev_out"] = _GradReduce("ev_out", g)
    took(routs, ("w1_0", "took_pair"))

    early = {"loss": loss_part, "od_w_s": d_ws, "od_b_s": d_bs, "mlp_norm_g0": d_mlp_g0, "mlp_norm_g1": d_mlp_g1,
             "final_norm_g": d_final_g, "od_norm_g": d_od_norm, "od_b_in": d_od_bin, "od_ln_v_g": d_od_lng,
             "od_ln_v_b": d_od_lnb}
    share_early = [_ShareAll(list(early.values()))] if distributed else []
    (dx, dz, d_ev_norm, d_caw, d_cab, d_ev_lng, d_ev_lnb, d_cbw), routs = _bwd_even(
        dh1, x2, ev_norm_g, z, a2, cv, as_cols(w["ev_in"]), conv_a_w, ev_ln_a_g, ev_ln_a_b, conv_b_w, as_rows(w["ev_out"]),
        tm=tm, seq=seq, riders=chips("w2_0") + chips("w1_0") + swap("ev_out") + share_early)
    took(routs, ("w2_0", "took_chips"), ("w1_0", "took_chips"), ("ev_out", "took_pair"))
    late = {"ev_norm_g": d_ev_norm, "ev_conv_a_b": d_cab, "ev_ln_a_g": d_ev_lng, "ev_ln_a_b": d_ev_lnb,
            "ev_conv_a_w": d_caw, "ev_conv_b_w": d_cbw}
    share_late = [_ShareAll(list(late.values()))] if distributed else []
    g, routs2 = _wgrad(n0, dz, "wgrad_ev_in", col_shards=True,
                       riders=chips("ev_out") + share("w2_0") + share("w1_0") + share_late)
    red["ev_in"] = _GradReduce("ev_in", g)
    took(routs2, ("ev_out", "took_chips"), ("w2_0", "took_share"), ("w1_0", "took_share"))
    own = {**early, **late}
    landed = dict(zip(own.keys(), routs[3] + routs2[3])) if distributed else None
    return dx, red, own, landed


def _rows128(a):
    rows = jnp.reshape(a, (-1, LANES))
    pad = (-rows.shape[0]) % SUBLANES
    return jnp.pad(rows, ((0, pad), (0, 0))) if pad else rows


def _pack(arrays):
    return jnp.concatenate([_rows128(a) for a in arrays], axis=0)


def _unpack(buf, shapes):
    out, r0 = [], 0
    for shp in shapes:
        size = 1
        for dim in shp:
            size *= dim
        nr = size // LANES
        out.append(jnp.reshape(buf[r0:r0 + nr], shp))
        r0 += nr + (-nr) % SUBLANES
    return out


def kernel(x, ev_norm_g, ev_w_in, ev_conv_a_w, ev_conv_a_b, ev_ln_a_g, ev_ln_a_b, ev_conv_b_w, ev_w_out, od_norm_g, od_w_in, od_b_in, od_ln_v_g, od_ln_v_b, od_w_s, od_b_s, od_w_out, mlp_norm_g, mlp_w1, mlp_w2, final_norm_g, loss_target, m_ev_norm_g, m_ev_w_in, m_ev_conv_a_w, m_ev_conv_a_b, m_ev_ln_a_g, m_ev_ln_a_b, m_ev_conv_b_w, m_ev_w_out, m_od_norm_g, m_od_w_in, m_od_b_in, m_od_ln_v_g, m_od_ln_v_b, m_od_w_s, m_od_b_s, m_od_w_out, m_mlp_norm_g, m_mlp_w1, m_mlp_w2, m_final_norm_g, v_ev_norm_g, v_ev_w_in, v_ev_conv_a_w, v_ev_conv_a_b, v_ev_ln_a_g, v_ev_ln_a_b, v_ev_conv_b_w, v_ev_w_out, v_od_norm_g, v_od_w_in, v_od_b_in, v_od_ln_v_g, v_od_ln_v_b, v_od_w_s, v_od_b_s, v_od_w_out, v_mlp_norm_g, v_mlp_w1, v_mlp_w2, v_final_norm_g):
    tm = TOKEN_TILE
    batch, seq, d = x.shape
    tokens = batch * seq
    x2 = jnp.reshape(x, (tokens, d))
    tgt2 = jnp.reshape(loss_target, (tokens, d))
    chip = 2 * lax.axis_index("x") + lax.axis_index("y")

    small_shapes = [(A_CONV_WIDTH, LANES), (B_CONV_WIDTH, LANES), (256,), (512,), (256,), (256,)]
    small_shard = _pack([ev_conv_a_w[0], ev_conv_b_w[0], od_norm_g[0], od_b_in[0], od_ln_v_g[0], od_ln_v_b[0]])
    small_shard = jnp.pad(small_shard, ((0, (-small_shard.shape[0]) % (2 * SUBLANES)), (0, 0)))
    first = [_place_shard(ev_w_in, 0, BF16, "place_ev_w_in"), _place_shard(ev_w_out, 0, BF16, "place_ev_w_out"),
             _place_shard(small_shard[None], 0, F32, "place_small")]
    staged = {
        "w1_0": _place_shard(mlp_w1, 0, BF16, "place_w1_0"), "w2_0": _place_shard(mlp_w2, 0, BF16, "place_w2_0"),
        "od_in": _place_shard(od_w_in, 0, BF16, "place_od_w_in"), "od_out": _place_shard(od_w_out, 0, BF16, "place_od_w_out"),
        "w1_1": _place_shard(mlp_w1, 1, BF16, "place_w1_1"), "w2_1": _place_shard(mlp_w2, 1, BF16, "place_w2_1"),
    }
    (g_ev_in, g_ev_out, g_small), = _exchange([_Gather(first)], "gather_first")
    small_all = jnp.reshape(g_small, (N_CHIPS, -1, LANES))
    per_chip = [_unpack(small_all[q], small_shapes) for q in range(N_CHIPS)]
    conv_a_w = jnp.concatenate([pc[0] for pc in per_chip], axis=1)
    conv_b_w = jnp.concatenate([pc[1] for pc in per_chip], axis=1)
    od_norm = jnp.concatenate([pc[2] for pc in per_chip])[None, :]
    od_bias = jnp.concatenate([pc[3] for pc in per_chip])[None, :]
    od_lng = jnp.concatenate([pc[4] for pc in per_chip])[None, :]
    od_lnb = jnp.concatenate([pc[5] for pc in per_chip])[None, :]

    dx, red, own, landed = _forward_backward(
        x2, tgt2, {"ev_in": g_ev_in, "ev_out": g_ev_out}, staged, conv_a_w, conv_b_w, od_norm, od_bias, od_lng, od_lnb,
        ev_norm_g, ev_conv_a_b, ev_ln_a_g, ev_ln_a_b, od_w_s, od_b_s, mlp_norm_g, final_norm_g, tm=tm, seq=seq)

    routs = _exchange([red["ev_in"].pair_swap(), red["ev_out"].pair_share()], "reduce_tail_1")
    red["ev_in"].took_pair(routs[0])
    red["ev_out"].took_share(routs[1])
    routs = _exchange([red["ev_in"].chip_swap()], "reduce_tail_2")
    red["ev_in"].took_chips(routs[0])
    routs = _exchange([red["ev_in"].pair_share()], "reduce_tail_3")
    red["ev_in"].took_share(routs[0])

    given = {"ev_norm_g": (ev_norm_g, m_ev_norm_g, v_ev_norm_g), "ev_conv_a_b": (ev_conv_a_b, m_ev_conv_a_b, v_ev_conv_a_b),
             "ev_ln_a_g": (ev_ln_a_g, m_ev_ln_a_g, v_ev_ln_a_g), "ev_ln_a_b": (ev_ln_a_b, m_ev_ln_a_b, v_ev_ln_a_b),
             "od_w_s": (od_w_s, m_od_w_s, v_od_w_s), "od_b_s": (od_b_s, m_od_b_s, v_od_b_s),
             "mlp_norm_g": (mlp_norm_g, m_mlp_norm_g, v_mlp_norm_g), "final_norm_g": (final_norm_g, m_final_norm_g, v_final_norm_g),
             "ev_conv_a_w": (ev_conv_a_w, m_ev_conv_a_w, v_ev_conv_a_w), "ev_conv_b_w": (ev_conv_b_w, m_ev_conv_b_w, v_ev_conv_b_w),
             "od_norm_g": (od_norm_g, m_od_norm_g, v_od_norm_g), "od_b_in": (od_b_in, m_od_b_in, v_od_b_in),
             "od_ln_v_g": (od_ln_v_g, m_od_ln_v_g, v_od_ln_v_g), "od_ln_v_b": (od_ln_v_b, m_od_ln_v_b, v_od_ln_v_b)}
    shaped = {nm: tuple(jnp.reshape(a, shape) for a in given[nm]) for nm, shape, _, _ in SMALL_WEIGHTS}
    loss11, small_upd = _small_update(own, landed, shaped)
    loss = loss11[0, 0]
    upd = {nm: [jnp.reshape(o, given[nm][0].shape) for o in outs] for nm, outs in small_upd.items()}

    def big_update(wt, m, v, names, call):
        grads = [red[nm].reduced() for nm in names]
        shp3 = (len(grads),) + grads[0].shape
        outs, _ = _adamw(jnp.reshape(wt, shp3), jnp.reshape(m, shp3), jnp.reshape(v, shp3), grads, call)
        return [jnp.reshape(o, wt.shape) for o in outs], None

    upd["mlp_w2"], _ = big_update(mlp_w2, m_mlp_w2, v_mlp_w2, ["w2_0", "w2_1"], "adamw_mlp_w2")
    upd["mlp_w1"], _ = big_update(mlp_w1, m_mlp_w1, v_mlp_w1, ["w1_0", "w1_1"], "adamw_mlp_w1")
    upd["ev_w_in"], _ = big_update(ev_w_in, m_ev_w_in, v_ev_w_in, ["ev_in"], "adamw_ev_w_in")
    upd["ev_w_out"], _ = big_update(ev_w_out, m_ev_w_out, v_ev_w_out, ["ev_out"], "adamw_ev_w_out")
    upd["od_w_in"], _ = big_update(od_w_in, m_od_w_in, v_od_w_in, ["od_in"], "adamw_od_w_in")
    upd["od_w_out"], _ = big_update(od_w_out, m_od_w_out, v_od_w_out, ["od_out"], "adamw_od_w_out")

    order = ["ev_norm_g", "ev_w_in", "ev_conv_a_w", "ev_conv_a_b", "ev_ln_a_g", "ev_ln_a_b", "ev_conv_b_w", "ev_w_out",
             "od_norm_g", "od_w_in", "od_b_in", "od_ln_v_g", "od_ln_v_b", "od_w_s", "od_b_s", "od_w_out", "mlp_norm_g",
             "mlp_w1", "mlp_w2", "final_norm_g"]
    grad_x = jnp.reshape(dx, x.shape)
    return (loss, grad_x, *[upd[nm][0] for nm in order], *[upd[nm][1] for nm in order],
            *[upd[nm][2] for nm in order], *[upd[nm][3] for nm in order])
```

```python
import functools

import jax
import jax.numpy as jnp
from jax import lax
from jax.experimental import pallas as pl
from jax.experimental.pallas import tpu as pltpu

F32 = jnp.float32
BF16 = jnp.bfloat16

D_MODEL = 1024
A_DIM = 512
B_DIM = 512
IN_EVEN = 2 * A_DIM + 3 * B_DIM
A_CONV_WIDTH = 31
B_CONV_WIDTH = 3
CHUNK = 128
C_GROUPS = 8
C_DIM = 1024
D_FF = 4096
RMS_EPS = 1e-6
LN_EPS = 1e-5
ADAM_LR = 0.001
ADAM_B1 = 0.9
ADAM_B2 = 0.999
ADAM_EPS = 1e-08
ADAM_WD = 0.01
ADAM_STEP = 10

N_CHIPS = 4
N_DEV = 8
TOKEN_TILE = 512
A_HALO = 32
B_HALO = 8
CONV_ROWS = 16
PAIR = 2 * CHUNK
LANES = 128
SUBLANES = 8
MXU_ROWS = 256
MIB = 1024 * 1024
MESH = pl.DeviceIdType.MESH
ANY = pl.BlockSpec(memory_space=pl.ANY)


def _dot(a, b):
    return lax.dot_general(a, b, (((1,), (0,)), ((), ())), preferred_element_type=F32)


def _dot_nt(a, b):
    return lax.dot_general(a, b, (((1,), (1,)), ((), ())), preferred_element_type=F32)


def _dot_tn(a, b):
    return lax.dot_general(a, b, (((0,), (0,)), ((), ())), preferred_element_type=F32)


def _params(vmem_mib, n_axes=1):
    return pltpu.CompilerParams(dimension_semantics=("arbitrary",) * n_axes, vmem_limit_bytes=vmem_mib * MIB)


def _row_spec(tm, cols, rev_nt=None):
    if rev_nt is None:
        return pl.BlockSpec((tm, cols), lambda i: (i, 0))
    return pl.BlockSpec((tm, cols), lambda i: (rev_nt - 1 - i, 0))


def _full_spec(shape):
    nd = len(shape)
    return pl.BlockSpec(shape, lambda i: (0,) * nd)


def _block_rows(rows, cap=512):
    best = SUBLANES
    for br in range(SUBLANES, min(rows, cap) + 1, SUBLANES):
        if rows % br == 0:
            best = br
    return best


def _load(src, dst, sem):
    cp = pltpu.make_async_copy(src, dst, sem)
    cp.start()
    cp.wait()


def _rms_fwd(x, g):
    rstd = lax.rsqrt(jnp.mean(x * x, axis=-1, keepdims=True) + RMS_EPS)
    return x * rstd * g, rstd


def _rms_bwd(dn, x, rstd, g):
    a = dn * g
    xh = x * rstd
    dx = rstd * (a - xh * jnp.mean(a * xh, axis=-1, keepdims=True))
    dg = jnp.sum(dn * xh, axis=0, keepdims=True)
    return dx, dg


def _ln_stats(v):
    mu = jnp.mean(v, axis=-1, keepdims=True)
    xc = v - mu
    rs = lax.rsqrt(jnp.mean(xc * xc, axis=-1, keepdims=True) + LN_EPS)
    return xc * rs, rs


def _ln_bwd(dy, xhat, rs, g):
    dxh = dy * g
    dv = rs * (dxh - jnp.mean(dxh, axis=-1, keepdims=True) - xhat * jnp.mean(dxh * xhat, axis=-1, keepdims=True))
    return dv, jnp.sum(dy * xhat, axis=0, keepdims=True), jnp.sum(dy, axis=0, keepdims=True)


def _gelu_cdf(s):
    return 0.5 * (1.0 + lax.erf(s * 0.7071067811865476))


def _mesh_pos():
    return lax.axis_index("x"), lax.axis_index("y"), lax.axis_index("c")


def _other_chips(x, y):
    return [(1 - x, y), (x, 1 - y), (1 - x, 1 - y)]


def _remote(src, dst, send_sem, recv_sem, to):
    return pltpu.make_async_remote_copy(src_ref=src, dst_ref=dst, send_sem=send_sem, recv_sem=recv_sem,
                                        device_id=to, device_id_type=MESH)


def _like(arrays):
    return [jax.ShapeDtypeStruct(a.shape, a.dtype) for a in arrays]


class _Gather:
    def __init__(self, bufs):
        self.ins = list(bufs)
        self.out_shapes = _like(bufs)
        self.aliases = {t: t for t in range(len(bufs))}
        self.n_sems = 6 * len(bufs)

    def _ici(self, ins, outs, send, recv, t, k, chip, mine, c):
        return _remote(ins[t].at[mine, c], outs[t].at[mine, c], send.at[6 * t + k], recv.at[6 * t + k], (*chip, c))

    def start(self, ins, outs, send, recv):
        x, y, c = _mesh_pos()
        for t in range(len(ins)):
            for k, chip in enumerate(_other_chips(x, y)):
                self._ici(ins, outs, send, recv, t, k, chip, 2 * x + y, c).start()

    def finish(self, ins, outs, send, recv):
        x, y, c = _mesh_pos()
        me, sibling = (x, y, c), (x, y, 1 - c)
        chips = _other_chips(x, y)
        passed = []
        for t in range(len(ins)):
            for k, chip in enumerate(chips):
                blk = outs[t].at[2 * chip[0] + chip[1], c]
                _remote(blk, blk, send.at[6 * t + k], recv.at[6 * t + k], me).wait_recv()
                cp = _remote(blk, blk, send.at[6 * t + 3 + k], recv.at[6 * t + 3 + k], sibling)
                cp.start()
                passed.append(cp)
        for t in range(len(ins)):
            for k, chip in enumerate(chips):
                blk = outs[t].at[2 * chip[0] + chip[1], 1 - c]
                _remote(blk, blk, send.at[6 * t + 3 + k], recv.at[6 * t + 3 + k], me).wait_recv()
        for t in range(len(ins)):
            for k, chip in enumerate(chips):
                self._ici(ins, outs, send, recv, t, k, chip, 2 * x + y, c).wait_send()
        for cp in passed:
            cp.wait_send()


class _PairSwap:
    def __init__(self, grads):
        self.ins = list(grads)
        self.out_shapes = [jax.ShapeDtypeStruct((g.shape[0],) + g.shape[2:], g.dtype) for g in grads]
        self.aliases = {}
        self.n_sems = len(grads)

    def _copies(self, ins, outs, send, recv):
        x, y, c = _mesh_pos()
        return [_remote(ins[t].at[:, 1 - c], outs[t], send.at[t], recv.at[t], (x, y, 1 - c)) for t in range(len(ins))]

    def start(self, ins, outs, send, recv):
        for cp in self._copies(ins, outs, send, recv):
            cp.start()

    def finish(self, ins, outs, send, recv):
        for cp in self._copies(ins, outs, send, recv):
            cp.wait()


class _ChipSwap:
    def __init__(self, parts):
        self.ins = list(parts)
        self.out_shapes = [jax.ShapeDtypeStruct((3,) + p.shape[1:], p.dtype) for p in parts]
        self.aliases = {}
        self.n_sems = 3 * len(parts)

    def _copies(self, ins, outs, send, recv):
        x, y, c = _mesh_pos()
        return [_remote(ins[t].at[2 * chip[0] + chip[1]], outs[t].at[k], send.at[3 * t + k], recv.at[3 * t + k], (*chip, c))
                for t in range(len(ins)) for k, chip in enumerate(_other_chips(x, y))]

    def start(self, ins, outs, send, recv):
        for cp in self._copies(ins, outs, send, recv):
            cp.start()

    def finish(self, ins, outs, send, recv):
        for cp in self._copies(ins, outs, send, recv):
            cp.wait()


class _PairShare:
    def __init__(self, fulls):
        self.ins = list(fulls)
        self.out_shapes = _like(fulls)
        self.aliases = {t: t for t in range(len(fulls))}
        self.n_sems = len(fulls)

    def _copies(self, ins, outs, send, recv):
        x, y, c = _mesh_pos()
        return [_remote(ins[t].at[c], outs[t].at[c], send.at[t], recv.at[t], (x, y, 1 - c)) for t in range(len(ins))]

    def start(self, ins, outs, send, recv):
        for cp in self._copies(ins, outs, send, recv):
            cp.start()

    def finish(self, ins, outs, send, recv):
        for cp in self._copies(ins, outs, send, recv):
            cp.wait()


class _ShareAll:
    def __init__(self, arrays):
        self.ins = list(arrays)
        self.out_shapes = [jax.ShapeDtypeStruct((N_DEV,) + a.shape, a.dtype) for a in arrays]
        self.aliases = {}
        self.n_sems = (N_DEV - 1) * len(arrays)

    def _peers(self):
        x, y, c = _mesh_pos()
        flips = [((r >> 2) & 1, (r >> 1) & 1, r & 1) for r in range(1, N_DEV)]
        return (x, y, c), [(x ^ fx, y ^ fy, c ^ fc) for fx, fy, fc in flips]

    def _sends(self, ins, outs, send, recv):
        (x, y, c), peers = self._peers()
        mine = 4 * x + 2 * y + c
        return [_remote(ins[a], outs[a].at[mine], send.at[7 * a + r], recv.at[7 * a + r], peer)
                for a in range(len(ins)) for r, peer in enumerate(peers)]

    def start(self, ins, outs, send, recv):
        for cp in self._sends(ins, outs, send, recv):
            cp.start()

    def finish(self, ins, outs, send, recv):
        (x, y, c), peers = self._peers()
        for a in range(len(ins)):
            for r, (px, py, pc) in enumerate(peers):
                blk = outs[a].at[4 * px + 2 * py + pc]
                _remote(blk, blk, send.at[7 * a + r], recv.at[7 * a + r], (x, y, c)).wait_recv()
        for cp in self._sends(ins, outs, send, recv):
            cp.wait_send()


def _pallas(body, operands, *, name, grid, in_specs, out_specs, out_shape, scratch_shapes=(), vmem_mib=32, riders=()):
    in_specs, out_specs, out_shape, scratch_shapes = list(in_specs), list(out_specs), list(out_shape), list(scratch_shapes)
    if not riders:
        outs = pl.pallas_call(body, name=name, grid=grid, in_specs=in_specs, out_specs=out_specs, out_shape=out_shape,
                              scratch_shapes=scratch_shapes, compiler_params=_params(vmem_mib, len(grid)))(*operands)
        return list(outs), []
    n_in, n_out, n_scr = len(in_specs), len(out_specs), len(scratch_shapes)
    r_in = [len(r.ins) for r in riders]
    r_out = [len(r.out_shapes) for r in riders]
    steps = 1
    for g in grid:
        steps *= g

    def wrapped(*refs):
        refs = list(refs)
        ins, refs = refs[:n_in], refs[n_in:]
        rins = []
        for k in r_in:
            rins.append(refs[:k])
            refs = refs[k:]
        outs, refs = refs[:n_out], refs[n_out:]
        routs = []
        for k in r_out:
            routs.append(refs[:k])
            refs = refs[k:]
        scr, sems = refs[:n_scr], refs[n_scr:]
        step = 0
        for ax, g in enumerate(grid):
            step = step * g + pl.program_id(ax)

        def each(what):
            for j, r in enumerate(riders):
                getattr(r, what)(rins[j], routs[j], sems[2 * j], sems[2 * j + 1])

        if grid:
            pl.when(step == 0)(lambda: each("start"))
        else:
            each("start")
        body(*ins, *outs, *scr)
        if grid:
            pl.when(step == steps - 1)(lambda: each("finish"))
        else:
            each("finish")

    aliases, off_in, off_out = {}, n_in, n_out
    for r, ki, ko in zip(riders, r_in, r_out):
        for i, o in r.aliases.items():
            aliases[off_in + i] = off_out + o
        off_in, off_out = off_in + ki, off_out + ko
    sems = []
    for r in riders:
        sems += [pltpu.SemaphoreType.DMA((r.n_sems,)), pltpu.SemaphoreType.DMA((r.n_sems,))]
    res = pl.pallas_call(
        wrapped, name=name, grid=grid,
        in_specs=in_specs + [ANY] * sum(r_in), out_specs=out_specs + [ANY] * sum(r_out),
        out_shape=out_shape + [s for r in riders for s in r.out_shapes],
        scratch_shapes=scratch_shapes + sems, input_output_aliases=aliases,
        compiler_params=pltpu.CompilerParams(dimension_semantics=("arbitrary",) * len(grid),
                                             vmem_limit_bytes=vmem_mib * MIB, has_side_effects=True),
    )(*operands, *[a for r in riders for a in r.ins])
    res = list(res)
    outs, res = res[:n_out], res[n_out:]
    routs = []
    for k in r_out:
        routs.append(res[:k])
        res = res[k:]
    return outs, routs


def _exchange(riders, name):
    return _pallas(lambda: None, [], name=name, grid=(), in_specs=[], out_specs=[], out_shape=[], riders=riders)[1]


def _in_hbm(a):
    return pltpu.with_memory_space_constraint(a, pltpu.HBM)


def _place_shard(w, layer, dtype, name):
    _, rows, cols = w.shape
    half = rows // 2
    br = _block_rows(half)
    nb = half // br
    mine = 2 * lax.axis_index("x") + lax.axis_index("y")

    def body(q_ref, w_ref, o_ref):
        o_ref[...] = w_ref[...].astype(dtype)

    return pl.pallas_call(
        body, name=name,
        grid_spec=pltpu.PrefetchScalarGridSpec(
            num_scalar_prefetch=1, grid=(2, nb),
            in_specs=[pl.BlockSpec((None, br, cols), lambda h, i, q: (layer, h * nb + i, 0))],
            out_specs=pl.BlockSpec((None, None, br, cols), lambda h, i, q: (q[0], h, i, 0))),
        out_shape=jax.ShapeDtypeStruct((N_CHIPS, 2, half, cols), dtype),
        compiler_params=_params(16, 2),
    )(jnp.reshape(mine, (1,)).astype(jnp.int32), w)


def _add_pair(g, recv, name):
    _, _, r, cdim = g.shape
    br = _block_rows(r, 256)
    c = lax.axis_index("c")

    def body(c_ref, g_ref, r_ref, o_ref):
        o_ref[...] = (g_ref[...] + r_ref[...]).astype(BF16)

    return pl.pallas_call(
        body, name=name,
        grid_spec=pltpu.PrefetchScalarGridSpec(
            num_scalar_prefetch=1, grid=(N_CHIPS, r // br),
            in_specs=[pl.BlockSpec((None, None, br, cdim), lambda q, i, c_ref: (q, c_ref[0], i, 0)),
                      pl.BlockSpec((None, br, cdim), lambda q, i, c_ref: (q, i, 0))],
            out_specs=pl.BlockSpec((None, br, cdim), lambda q, i, c_ref: (q, i, 0))),
        out_shape=jax.ShapeDtypeStruct((N_CHIPS, r, cdim), BF16),
        compiler_params=_params(16, 2),
    )(jnp.reshape(c, (1,)).astype(jnp.int32), _in_hbm(g), _in_hbm(recv))


def _add_chips(own, recv, name):
    _, r, cdim = own.shape
    br = _block_rows(r, 256)
    x, y, c = _mesh_pos()

    def body(pos_ref, own_ref, r_ref, o_ref):
        acc = own_ref[...].astype(F32)
        for k in range(3):
            acc = acc + r_ref[k].astype(F32)
        o_ref[...] = acc

    return pl.pallas_call(
        body, name=name,
        grid_spec=pltpu.PrefetchScalarGridSpec(
            num_scalar_prefetch=1, grid=(r // br,),
            in_specs=[pl.BlockSpec((None, br, cdim), lambda i, pos: (pos[0], i, 0)),
                      pl.BlockSpec((3, br, cdim), lambda i, pos: (0, i, 0))],
            out_specs=pl.BlockSpec((None, br, cdim), lambda i, pos: (pos[1], i, 0))),
        out_shape=jax.ShapeDtypeStruct((2, r, cdim), F32),
        compiler_params=_params(16, 1),
    )(jnp.stack([2 * x + y, c]).astype(jnp.int32), _in_hbm(own), _in_hbm(recv))


def _adam_math(w, m, v, g):
    c1 = 1.0 / (1.0 - ADAM_B1 ** ADAM_STEP)
    c2 = 1.0 / (1.0 - ADAM_B2 ** ADAM_STEP)
    m_new = ADAM_B1 * m + (1.0 - ADAM_B1) * g
    v_new = ADAM_B2 * v + (1.0 - ADAM_B2) * (g * g)
    return -ADAM_LR * ((m_new * c1) / (jnp.sqrt(v_new * c2) + ADAM_EPS) + ADAM_WD * w), m_new, v_new


SMALL_WEIGHTS = [
    ("ev_norm_g", (1, D_MODEL), ["ev_norm_g"], None), ("ev_conv_a_b", (1, A_DIM), ["ev_conv_a_b"], None),
    ("ev_ln_a_g", (1, A_DIM), ["ev_ln_a_g"], None), ("ev_ln_a_b", (1, A_DIM), ["ev_ln_a_b"], None),
    ("od_w_s", (C_GROUPS, CHUNK, CHUNK), ["od_w_s"], None), ("od_b_s", (C_GROUPS, CHUNK), ["od_b_s"], None),
    ("mlp_norm_g", (2, D_MODEL), ["mlp_norm_g0", "mlp_norm_g1"], None), ("final_norm_g", (1, D_MODEL), ["final_norm_g"], None),
    ("ev_conv_a_w", (A_CONV_WIDTH, A_DIM // N_CHIPS), ["ev_conv_a_w"], A_DIM // N_CHIPS),
    ("ev_conv_b_w", (B_CONV_WIDTH, B_DIM // N_CHIPS), ["ev_conv_b_w"], B_DIM // N_CHIPS),
    ("od_norm_g", (1, D_MODEL // N_CHIPS), ["od_norm_g"], D_MODEL // N_CHIPS),
    ("od_b_in", (1, 2 * C_DIM // N_CHIPS), ["od_b_in"], 2 * C_DIM // N_CHIPS),
    ("od_ln_v_g", (1, C_DIM // N_CHIPS), ["od_ln_v_g"], C_DIM // N_CHIPS),
    ("od_ln_v_b", (1, C_DIM // N_CHIPS), ["od_ln_v_b"], C_DIM // N_CHIPS),
]


def _small_update(own, landed, weights):
    names = list(own.keys())
    n_g, n_w = len(names), len(SMALL_WEIGHTS)

    def body(*refs):
        refs = list(refs)
        own_refs = dict(zip(names, refs[:n_g]))
        land_refs = dict(zip(names, refs[n_g:2 * n_g]))
        wmv = [refs[2 * n_g + 3 * i:2 * n_g + 3 * i + 3] for i in range(n_w)]
        o0 = 2 * n_g + 3 * n_w
        loss_ref = refs[o0]
        outs = [refs[o0 + 1 + 4 * i:o0 + 5 + 4 * i] for i in range(n_w)]
        acc = dict(zip(names, refs[o0 + 1 + 4 * n_w:]))
        x, y, c = _mesh_pos()
        mine, chip = 4 * x + 2 * y + c, 2 * x + y

        for nm in names:
            for d in range(N_DEV):
                def add(term, nm=nm, d=d):
                    acc[nm][...] = term if d == 0 else acc[nm][...] + term
                pl.when(mine == d)(lambda nm=nm, add=add: add(own_refs[nm][...]))
                pl.when(mine != d)(lambda nm=nm, d=d, add=add: add(land_refs[nm][d]))
        loss_ref[...] = acc["loss"][...]

        def update(i, rows, g):
            w_ref, m_ref, v_ref = wmv[i]
            delta, m_new, v_new = _adam_math(w_ref[rows], m_ref[rows], v_ref[rows], g)
            for ref, val in zip(outs[i], (g, delta, m_new, v_new)):
                ref[rows] = val

        for i, (_, shape, grads, per_chip) in enumerate(SMALL_WEIGHTS):
            for row, gname in enumerate(grads):
                rows = slice(row, row + 1) if len(grads) > 1 else slice(None)
                if per_chip is None:
                    update(i, rows, acc[gname][...])
                else:
                    for q in range(N_CHIPS):
                        pl.when(chip == q)(lambda i=i, rows=rows, gname=gname, q=q, per_chip=per_chip:
                                           update(i, rows, acc[gname][:, q * per_chip:(q + 1) * per_chip]))

    operands = [own[nm] for nm in names] + [landed[nm] for nm in names]
    for nm, _, _, _ in SMALL_WEIGHTS:
        operands += list(weights[nm])
    out_shape = [jax.ShapeDtypeStruct((1, 1), F32)]
    for _, shape, _, _ in SMALL_WEIGHTS:
        out_shape += [jax.ShapeDtypeStruct(shape, F32)] * 4
    res = pl.pallas_call(
        body, name="small_update", grid=(1,),
        in_specs=[_full_spec(a.shape) for a in operands], out_specs=[_full_spec(s.shape) for s in out_shape],
        out_shape=out_shape, scratch_shapes=[pltpu.VMEM(own[nm].shape, F32) for nm in names],
        compiler_params=_params(32, 1),
    )(*operands)
    return res[0], {nm: res[1 + 4 * i:5 + 4 * i] for i, (nm, _, _, _) in enumerate(SMALL_WEIGHTS)}


def _adamw(w, m, v, grads, name, riders=()):
    layers, r, cdim = w.shape
    br = _block_rows(r, 256 if cdim > LANES else 1024)

    def body(*refs):
        w_ref, m_ref, v_ref = refs[:3]
        g_refs = refs[3:3 + layers]
        go_ref, d_ref, mo_ref, vo_ref = refs[3 + layers:]
        layer = pl.program_id(0)
        for l in range(layers):
            @pl.when(layer == l)
            def _(l=l):
                g = g_refs[l][...]
                go_ref[...] = g
                d_ref[...], mo_ref[...], vo_ref[...] = _adam_math(w_ref[...], m_ref[...], v_ref[...], g)

    spec3 = pl.BlockSpec((None, br, cdim), lambda l, i: (l, i, 0))
    spec2 = pl.BlockSpec((br, cdim), lambda l, i: (i, 0))
    out = jax.ShapeDtypeStruct((layers, r, cdim), F32)
    return _pallas(body, [w, m, v, *grads], name=name, grid=(layers, r // br),
                   in_specs=[spec3, spec3, spec3] + [spec2] * layers, out_specs=[spec3] * 4, out_shape=[out] * 4,
                   vmem_mib=32, riders=riders)


def _fill_shifted(buf, rows):
    for b in range(1, SUBLANES):
        buf[b, 0:rows - SUBLANES, :] = buf[0, b:b + rows - SUBLANES, :]


def _window(buf, start, size):
    return buf[start % SUBLANES, start - start % SUBLANES:start - start % SUBLANES + size, :]


def _conv31(src, w_ref, r0, base, init):
    acc = init
    for k in range(A_CONV_WIDTH):
        acc = acc + w_ref[k:k + 1, :] * _window(src, base + k + r0, CONV_ROWS)
    return acc


def _fwd_even(x, norm_g, w_in, conv_a_w, conv_a_b, ln_g, ln_b, conv_b_w, w_out, *, tm, seq, riders=()):
    tokens = x.shape[0]
    nt, tps = tokens // tm, seq // tm

    def body(x_ref, g_ref, win_hbm, caw_ref, cab_ref, lng_ref, lnb_ref, cbw_ref, wout_hbm,
             h_ref, n_ref, z_ref, a2_ref, cv_ref, mix_ref, win_v, wout_v, pa, pb, sem):
        i = pl.program_id(0)

        @pl.when(i == 0)
        def _():
            _load(win_hbm, win_v, sem)
            _load(wout_hbm, wout_v, sem)

        xv = x_ref[...]
        nf, _ = _rms_fwd(xv, g_ref[...])
        n = nf.astype(BF16)
        n_ref[...] = n
        z = jnp.concatenate([_dot(n, win_v[j]) for j in range(N_CHIPS)], axis=1)
        z_ref[...] = z.astype(BF16)
        a_val, a_gate = z[:, 0:A_DIM], z[:, A_DIM:2 * A_DIM]
        b_gate, c_gate, b_val = z[:, 1024:1536], z[:, 1536:2048], z[:, 2048:2560]

        first = (i % tps) == 0

        @pl.when(first)
        def _():
            pa[0, 0:A_HALO, :] = jnp.zeros((A_HALO, A_DIM), F32)
            pb[0:B_HALO, :] = jnp.zeros((B_HALO, B_DIM), F32)

        @pl.when(jnp.logical_not(first))
        def _():
            pa[0, 0:A_HALO, :] = pa[0, tm:tm + A_HALO, :]
            pb[0:B_HALO, :] = pb[tm:tm + B_HALO, :]

        pa[0, A_HALO:A_HALO + tm, :] = a_val * jax.nn.sigmoid(a_gate)
        pb[B_HALO:B_HALO + tm, :] = c_gate * b_val
        _fill_shifted(pa, A_HALO + tm)
        bias = jnp.broadcast_to(cab_ref[...], (CONV_ROWS, A_DIM))
        for r0 in range(0, tm, CONV_ROWS):
            a2_ref[r0:r0 + CONV_ROWS, :] = _conv31(pa, caw_ref, r0, A_HALO - (A_CONV_WIDTH - 1), bias)
        xhat, _ = _ln_stats(a2_ref[...])
        a3 = xhat * lng_ref[...] + lnb_ref[...]
        a4 = a3 * jax.nn.sigmoid(a3)
        cv = cbw_ref[0:1, :] * pb[B_HALO - 2:B_HALO - 2 + tm, :]
        cv = cv + cbw_ref[1:2, :] * pb[B_HALO - 1:B_HALO - 1 + tm, :]
        cv = cv + cbw_ref[2:3, :] * pb[B_HALO:B_HALO + tm, :]
        cv_ref[...] = cv.astype(BF16)
        mix = jnp.concatenate([a4, b_gate * cv], axis=1).astype(BF16)
        mix_ref[...] = mix
        h_ref[...] = xv + _dot(mix, wout_v[...])

    shp = lambda cols, dt: jax.ShapeDtypeStruct((tokens, cols), dt)
    return _pallas(
        body, [x, norm_g, w_in, conv_a_w, conv_a_b, ln_g, ln_b, conv_b_w, w_out], name="fwd_even", grid=(nt,),
        in_specs=[_row_spec(tm, D_MODEL), _full_spec((1, D_MODEL)), ANY, _full_spec((A_CONV_WIDTH, A_DIM)),
                  _full_spec((1, A_DIM)), _full_spec((1, A_DIM)), _full_spec((1, A_DIM)),
                  _full_spec((B_CONV_WIDTH, B_DIM)), ANY],
        out_specs=[_row_spec(tm, D_MODEL), _row_spec(tm, D_MODEL), _row_spec(tm, IN_EVEN), _row_spec(tm, A_DIM),
                   _row_spec(tm, B_DIM), _row_spec(tm, D_MODEL)],
        out_shape=[shp(D_MODEL, F32), shp(D_MODEL, BF16), shp(IN_EVEN, BF16), shp(A_DIM, F32), shp(B_DIM, BF16),
                   shp(D_MODEL, BF16)],
        scratch_shapes=[pltpu.VMEM((N_CHIPS, D_MODEL, IN_EVEN // N_CHIPS), BF16), pltpu.VMEM((D_MODEL, D_MODEL), BF16),
                        pltpu.VMEM((SUBLANES, A_HALO + tm, A_DIM), F32), pltpu.VMEM((B_HALO + tm, B_DIM), F32),
                        pltpu.SemaphoreType.DMA],
        vmem_mib=56, riders=riders)


def _fwd_mlp(h, norm_g, w1, w2, layer, *, tm, riders=()):
    tokens = h.shape[0]
    nt = tokens // tm
    fs = D_FF // N_CHIPS

    def body(h_ref, g_ref, w1_hbm, w2_hbm, ho_ref, n_ref, p_ref, q_ref, w1_v, w2_v, sem):
        @pl.when(pl.program_id(0) == 0)
        def _():
            _load(w1_hbm, w1_v, sem)
            _load(w2_hbm, w2_v, sem)

        xv = h_ref[...]
        nf, _ = _rms_fwd(xv, g_ref[...])
        n = nf.astype(BF16)
        n_ref[...] = n
        acc = xv
        for j in range(N_CHIPS):
            p = _dot(n, w1_v[j])
            p_ref[:, j * fs:(j + 1) * fs] = p.astype(BF16)
            r = jnp.maximum(p, 0.0)
            q = (r * r).astype(BF16)
            q_ref[:, j * fs:(j + 1) * fs] = q
            acc = acc + _dot(q, w2_v[j])
        ho_ref[...] = acc

    shp = lambda cols, dt: jax.ShapeDtypeStruct((tokens, cols), dt)
    return _pallas(
        body, [h, norm_g, w1, w2], name=f"fwd_mlp{layer}", grid=(nt,),
        in_specs=[_row_spec(tm, D_MODEL), _full_spec((1, D_MODEL)), ANY, ANY],
        out_specs=[_row_spec(tm, D_MODEL), _row_spec(tm, D_MODEL), _row_spec(tm, D_FF), _row_spec(tm, D_FF)],
        out_shape=[shp(D_MODEL, F32), shp(D_MODEL, BF16), shp(D_FF, BF16), shp(D_FF, BF16)],
        scratch_shapes=[pltpu.VMEM((N_CHIPS, D_MODEL, fs), BF16), pltpu.VMEM((N_CHIPS, fs, D_MODEL), BF16),
                        pltpu.SemaphoreType.DMA],
        vmem_mib=56, riders=riders)


def _tril_mask():
    row = lax.broadcasted_iota(jnp.int32, (CHUNK, CHUNK), 0)
    col = lax.broadcasted_iota(jnp.int32, (CHUNK, CHUNK), 1)
    return row >= col


def _triu_mask():
    row = lax.broadcasted_iota(jnp.int32, (CHUNK, CHUNK), 0)
    col = lax.broadcasted_iota(jnp.int32, (CHUNK, CHUNK), 1)
    return row <= col


def _fwd_odd(h, norm_g, w_in, b_in, ln_g, ln_b, w_s, b_s_rows, w_out, *, tm, riders=()):
    tokens = h.shape[0]
    nt = tokens // tm
    cs = 2 * C_DIM // N_CHIPS

    def body(h_ref, g_ref, win_hbm, bin_ref, lng_ref, lnb_ref, ws_ref, bs_ref, wout_hbm,
             ho_ref, n_ref, s_ref, cdf_ref, sv_ref, y_ref, win_v, wout_v, bd, sem):
        @pl.when(pl.program_id(0) == 0)
        def _():
            _load(win_hbm, win_v, sem)
            _load(wout_hbm, wout_v, sem)
            mask = _tril_mask()
            bd[...] = jnp.zeros(bd.shape, BF16)
            for g in range(C_GROUPS):
                w = jnp.where(mask, ws_ref[g], 0.0).astype(BF16)
                bd[g, 0:CHUNK, 0:CHUNK] = w
                bd[g, CHUNK:PAIR, CHUNK:PAIR] = w

        xv = h_ref[...]
        nf, _ = _rms_fwd(xv, g_ref[...])
        n = nf.astype(BF16)
        n_ref[...] = n
        s = jnp.concatenate([_dot(n, win_v[j]) for j in range(N_CHIPS)], axis=1) + bin_ref[...]
        s_ref[...] = s.astype(BF16)
        cdf = _gelu_cdf(s)
        cdf_ref[...] = cdf.astype(BF16)
        zz = s * cdf
        u, v = zz[:, 0:C_DIM], zz[:, C_DIM:2 * C_DIM]
        xhat, _ = _ln_stats(v)
        vn = (xhat * lng_ref[...] + lnb_ref[...]).astype(BF16)
        for g in range(C_GROUPS):
            cols = slice(g * CHUNK, (g + 1) * CHUNK)
            bias = jnp.concatenate([bs_ref[g], bs_ref[g]], axis=0)
            for r0 in range(0, tm, PAIR):
                sv = _dot(bd[g], vn[r0:r0 + PAIR, cols]) + bias
                sv_ref[r0:r0 + PAIR, cols] = sv.astype(BF16)
                y_ref[r0:r0 + PAIR, cols] = (u[r0:r0 + PAIR, cols] * sv).astype(BF16)
        ho_ref[...] = xv + _dot(y_ref[...], wout_v[...])

    shp = lambda cols, dt: jax.ShapeDtypeStruct((tokens, cols), dt)
    return _pallas(
        body, [h, norm_g, w_in, b_in, ln_g, ln_b, w_s, b_s_rows, w_out], name="fwd_odd", grid=(nt,),
        in_specs=[_row_spec(tm, D_MODEL), _full_spec((1, D_MODEL)), ANY, _full_spec((1, 2 * C_DIM)),
                  _full_spec((1, C_DIM)), _full_spec((1, C_DIM)), _full_spec((C_GROUPS, CHUNK, CHUNK)),
                  _full_spec((C_GROUPS, CHUNK, CHUNK)), ANY],
        out_specs=[_row_spec(tm, D_MODEL), _row_spec(tm, D_MODEL), _row_spec(tm, 2 * C_DIM), _row_spec(tm, 2 * C_DIM),
                   _row_spec(tm, C_DIM), _row_spec(tm, C_DIM)],
        out_shape=[shp(D_MODEL, F32), shp(D_MODEL, BF16), shp(2 * C_DIM, BF16), shp(2 * C_DIM, BF16), shp(C_DIM, BF16),
                   shp(C_DIM, BF16)],
        scratch_shapes=[pltpu.VMEM((N_CHIPS, D_MODEL, cs), BF16), pltpu.VMEM((C_DIM, D_MODEL), BF16),
                        pltpu.VMEM((C_GROUPS, PAIR, PAIR), BF16), pltpu.SemaphoreType.DMA],
        vmem_mib=56, riders=riders)


def _loss_head(h, norm_g, target, *, tm):
    tokens = h.shape[0]
    nt = tokens // tm

    def body(h_ref, g_ref, t_ref, loss_ref, dh_ref, dhb_ref, dg_ref):
        @pl.when(pl.program_id(0) == 0)
        def _():
            loss_ref[...] = jnp.zeros((1, 1), F32)
            dg_ref[...] = jnp.zeros((1, D_MODEL), F32)

        xv = h_ref[...]
        g = g_ref[...]
        out, rstd = _rms_fwd(xv, g)
        err = out - t_ref[...]
        per_token = jnp.sum(err * err, axis=1, keepdims=True) * (1.0 / D_MODEL)
        loss_ref[...] += 0.5 * jnp.sum(per_token, axis=0, keepdims=True)
        dx, dg = _rms_bwd(err * (1.0 / D_MODEL), xv, rstd, g)
        dh_ref[...] = dx
        dhb_ref[...] = dx.astype(BF16)
        dg_ref[...] += dg

    return _pallas(
        body, [h, norm_g, target], name="loss_head", grid=(nt,),
        in_specs=[_row_spec(tm, D_MODEL), _full_spec((1, D_MODEL)), _row_spec(tm, D_MODEL)],
        out_specs=[_full_spec((1, 1)), _row_spec(tm, D_MODEL), _row_spec(tm, D_MODEL), _full_spec((1, D_MODEL))],
        out_shape=[jax.ShapeDtypeStruct((1, 1), F32), jax.ShapeDtypeStruct((tokens, D_MODEL), F32),
                   jax.ShapeDtypeStruct((tokens, D_MODEL), BF16), jax.ShapeDtypeStruct((1, D_MODEL), F32)],
        vmem_mib=32)[0]


def _bwd_mlp(dh, h, norm_g, p, w1, w2, layer, *, tm, riders=()):
    tokens = h.shape[0]
    nt = tokens // tm
    fs = D_FF // N_CHIPS

    def body(dh_ref, h_ref, g_ref, p_ref, w1_hbm, w2_hbm, dx_ref, dxb_ref, dp_ref, dg_ref, w1_v, w2_v, sem):
        @pl.when(pl.program_id(0) == 0)
        def _():
            _load(w1_hbm, w1_v, sem)
            _load(w2_hbm, w2_v, sem)
            dg_ref[...] = jnp.zeros((1, D_MODEL), F32)

        dhv = dh_ref[...]
        dhb = dhv.astype(BF16)
        dn = jnp.zeros((tm, D_MODEL), F32)
        for j in range(N_CHIPS):
            dq = _dot_nt(dhb, w2_v[j])
            r = jnp.maximum(p_ref[:, j * fs:(j + 1) * fs].astype(F32), 0.0)
            dp = ((2.0 * r) * dq).astype(BF16)
            dp_ref[:, j * fs:(j + 1) * fs] = dp
            dn = dn + _dot_nt(dp, w1_v[j])
        xv = h_ref[...]
        g = g_ref[...]
        _, rstd = _rms_fwd(xv, g)
        dx, dg = _rms_bwd(dn, xv, rstd, g)
        dx_ref[...] = dhv + dx
        dxb_ref[...] = (dhv + dx).astype(BF16)
        dg_ref[...] += dg

    return _pallas(
        body, [dh, h, norm_g, p, w1, w2], name=f"bwd_mlp{layer}", grid=(nt,),
        in_specs=[_row_spec(tm, D_MODEL), _row_spec(tm, D_MODEL), _full_spec((1, D_MODEL)), _row_spec(tm, D_FF), ANY, ANY],
        out_specs=[_row_spec(tm, D_MODEL), _row_spec(tm, D_MODEL), _row_spec(tm, D_FF), _full_spec((1, D_MODEL))],
        out_shape=[jax.ShapeDtypeStruct((tokens, D_MODEL), F32), jax.ShapeDtypeStruct((tokens, D_MODEL), BF16),
                   jax.ShapeDtypeStruct((tokens, D_FF), BF16), jax.ShapeDtypeStruct((1, D_MODEL), F32)],
        scratch_shapes=[pltpu.VMEM((N_CHIPS, D_MODEL, fs), BF16), pltpu.VMEM((N_CHIPS, fs, D_MODEL), BF16),
                        pltpu.SemaphoreType.DMA],
        vmem_mib=56, riders=riders)


def _bwd_odd(dh, h, norm_g, s, cdf, sv, w_in, ln_g, ln_b, w_s, w_out, *, tm, riders=()):
    tokens = h.shape[0]
    nt = tokens // tm
    cs = 2 * C_DIM // N_CHIPS

    def body(dh_ref, h_ref, g_ref, s_ref, cdf_ref, sv_ref, win_hbm, lng_ref, lnb_ref, ws_ref, wout_hbm,
             dx_ref, dxb_ref, ds_ref, dg_ref, dbin_ref, dlng_ref, dlnb_ref, dws_ref, dbs_ref,
             win_v, wout_v, bdt, dws_acc, dbs_acc, dvn, sem):
        i = pl.program_id(0)

        @pl.when(i == 0)
        def _():
            _load(win_hbm, win_v, sem)
            _load(wout_hbm, wout_v, sem)
            mask_t = _triu_mask()
            bdt[...] = jnp.zeros(bdt.shape, BF16)
            for g in range(C_GROUPS):
                wt = jnp.where(mask_t, ws_ref[g].T, 0.0).astype(BF16)
                bdt[g, 0:CHUNK, 0:CHUNK] = wt
                bdt[g, CHUNK:PAIR, CHUNK:PAIR] = wt
            dws_acc[...] = jnp.zeros(dws_acc.shape, F32)
            dbs_acc[...] = jnp.zeros(dbs_acc.shape, F32)
            dg_ref[...] = jnp.zeros(dg_ref.shape, F32)
            dbin_ref[...] = jnp.zeros(dbin_ref.shape, F32)
            dlng_ref[...] = jnp.zeros(dlng_ref.shape, F32)
            dlnb_ref[...] = jnp.zeros(dlnb_ref.shape, F32)

        dhv = dh_ref[...]
        dy = _dot_nt(dhv.astype(BF16), wout_v[...])
        sf = s_ref[...].astype(F32)
        cdf = cdf_ref[...].astype(F32)
        pdf = jnp.exp(-0.5 * sf * sf) * 0.3989422804014327
        zz = sf * cdf
        dgelu = cdf + sf * pdf
        u, v = zz[:, 0:C_DIM], zz[:, C_DIM:2 * C_DIM]
        xhat, rs = _ln_stats(v)
        lng = lng_ref[...]
        vn = (xhat * lng + lnb_ref[...]).astype(BF16)
        du = dy * sv_ref[...].astype(F32)
        dsv = dy * u
        dsvb = dsv.astype(BF16)
        for g in range(C_GROUPS):
            cols = slice(g * CHUNK, (g + 1) * CHUNK)
            for r0 in range(0, tm, PAIR):
                blk = dsvb[r0:r0 + PAIR, cols]
                dvn[r0:r0 + PAIR, cols] = _dot(bdt[g], blk)
                dws_acc[g] += _dot_nt(blk, vn[r0:r0 + PAIR, cols])
                dbs_acc[g] += dsv[r0:r0 + CHUNK, cols] + dsv[r0 + CHUNK:r0 + PAIR, cols]
        dv, dlng, dlnb = _ln_bwd(dvn[...], xhat, rs, lng)
        dlng_ref[...] += dlng
        dlnb_ref[...] += dlnb
        ds = jnp.concatenate([du, dv], axis=1) * dgelu
        dbin_ref[...] += jnp.sum(ds, axis=0, keepdims=True)
        dsb = ds.astype(BF16)
        ds_ref[...] = dsb
        dn = jnp.zeros((tm, D_MODEL), F32)
        for j in range(N_CHIPS):
            dn = dn + _dot_nt(dsb[:, j * cs:(j + 1) * cs], win_v[j])
        xv = h_ref[...]
        g = g_ref[...]
        _, rstd = _rms_fwd(xv, g)
        dx, dg = _rms_bwd(dn, xv, rstd, g)
        dx_ref[...] = dhv + dx
        dxb_ref[...] = (dhv + dx).astype(BF16)
        dg_ref[...] += dg

        @pl.when(i == nt - 1)
        def _():
            mask = _tril_mask()
            for g in range(C_GROUPS):
                full = dws_acc[g]
                dws_ref[g] = jnp.where(mask, full[0:CHUNK, 0:CHUNK] + full[CHUNK:PAIR, CHUNK:PAIR], 0.0)
                dbs_ref[g:g + 1, :] = jnp.sum(dbs_acc[g].T, axis=0, keepdims=True)

    row = lambda cols: jax.ShapeDtypeStruct((1, cols), F32)
    return _pallas(
        body, [dh, h, norm_g, s, cdf, sv, w_in, ln_g, ln_b, w_s, w_out], name="bwd_odd", grid=(nt,),
        in_specs=[_row_spec(tm, D_MODEL), _row_spec(tm, D_MODEL), _full_spec((1, D_MODEL)), _row_spec(tm, 2 * C_DIM),
                  _row_spec(tm, 2 * C_DIM), _row_spec(tm, C_DIM), ANY, _full_spec((1, C_DIM)), _full_spec((1, C_DIM)),
                  _full_spec((C_GROUPS, CHUNK, CHUNK)), ANY],
        out_specs=[_row_spec(tm, D_MODEL), _row_spec(tm, D_MODEL), _row_spec(tm, 2 * C_DIM), _full_spec((1, D_MODEL)),
                   _full_spec((1, 2 * C_DIM)),
                   _full_spec((1, C_DIM)), _full_spec((1, C_DIM)), _full_spec((C_GROUPS, CHUNK, CHUNK)),
                   _full_spec((C_GROUPS, CHUNK))],
        out_shape=[jax.ShapeDtypeStruct((tokens, D_MODEL), F32), jax.ShapeDtypeStruct((tokens, D_MODEL), BF16),
                   jax.ShapeDtypeStruct((tokens, 2 * C_DIM), BF16),
                   row(D_MODEL), row(2 * C_DIM), row(C_DIM), row(C_DIM),
                   jax.ShapeDtypeStruct((C_GROUPS, CHUNK, CHUNK), F32), jax.ShapeDtypeStruct((C_GROUPS, CHUNK), F32)],
        scratch_shapes=[pltpu.VMEM((N_CHIPS, D_MODEL, cs), BF16), pltpu.VMEM((C_DIM, D_MODEL), BF16),
                        pltpu.VMEM((C_GROUPS, PAIR, PAIR), BF16), pltpu.VMEM((C_GROUPS, PAIR, PAIR), F32),
                        pltpu.VMEM((C_GROUPS, CHUNK, CHUNK), F32), pltpu.VMEM((tm, C_DIM), F32),
                        pltpu.SemaphoreType.DMA],
        vmem_mib=56, riders=riders)


def _bwd_even(dh, x, norm_g, z, a2, cv, w_in, conv_a_w, ln_g, ln_b, conv_b_w, w_out, *, tm, seq, riders=()):
    tokens = x.shape[0]
    nt, tps = tokens // tm, seq // tm
    ws = IN_EVEN // N_CHIPS

    def body(dh_ref, x_ref, g_ref, z_ref, a2_ref, cv_ref, win_hbm, caw_ref, lng_ref, lnb_ref, cbw_ref, wout_hbm,
             dx_ref, dz_ref, dg_ref, dcaw_ref, dcab_ref, dlng_ref, dlnb_ref, dcbw_ref,
             win_v, wout_v, ea, eb, a1s, da1s, dw_acc, sem):
        i = pl.program_id(0)

        @pl.when(i == 0)
        def _():
            _load(win_hbm, win_v, sem)
            _load(wout_hbm, wout_v, sem)
            dw_acc[...] = jnp.zeros(dw_acc.shape, F32)
            for ref in (dg_ref, dcab_ref, dlng_ref, dlnb_ref, dcbw_ref):
                ref[...] = jnp.zeros(ref.shape, F32)

        dhv = dh_ref[...]
        dmix = _dot_nt(dhv.astype(BF16), wout_v[...])
        da4, dbo = dmix[:, 0:A_DIM], dmix[:, A_DIM:A_DIM + B_DIM]
        zf = z_ref[...].astype(F32)
        a_val, a_gate = zf[:, 0:A_DIM], zf[:, A_DIM:2 * A_DIM]
        b_gate, c_gate, b_val = zf[:, 1024:1536], zf[:, 1536:2048], zf[:, 2048:2560]

        xhat, rs = _ln_stats(a2_ref[...])
        lng = lng_ref[...]
        a3 = xhat * lng + lnb_ref[...]
        sg = jax.nn.sigmoid(a3)
        da3 = da4 * (sg * (1.0 + a3 * (1.0 - sg)))
        da2, dlng, dlnb = _ln_bwd(da3, xhat, rs, lng)
        dlng_ref[...] += dlng
        dlnb_ref[...] += dlnb
        dcab_ref[...] += jnp.sum(da2, axis=0, keepdims=True)

        last = ((nt - 1 - i) % tps) == tps - 1
        dcv = dbo * b_gate

        @pl.when(last)
        def _():
            ea[0, tm:tm + A_HALO, :] = jnp.zeros((A_HALO, A_DIM), F32)
            eb[tm:tm + B_HALO, :] = jnp.zeros((B_HALO, B_DIM), F32)

        @pl.when(jnp.logical_not(last))
        def _():
            ea[0, tm:tm + A_HALO, :] = ea[0, 0:A_HALO, :]
            eb[tm:tm + B_HALO, :] = eb[0:B_HALO, :]

        ea[0, 0:tm, :] = da2
        eb[0:tm, :] = dcv
        _fill_shifted(ea, tm + A_HALO)
        sig = jax.nn.sigmoid(a_gate)
        a1s[...] = a_val * sig
        for r0 in range(0, tm, CONV_ROWS):
            a1c = a1s[r0:r0 + CONV_ROWS, :]
            acc = jnp.zeros((CONV_ROWS, A_DIM), F32)
            for j in range(A_CONV_WIDTH):
                k = A_CONV_WIDTH - 1 - j
                sl = _window(ea, r0 + j, CONV_ROWS)
                acc = acc + caw_ref[k:k + 1, :] * sl
                dw_acc[k] += sl * a1c
            da1s[r0:r0 + CONV_ROWS, :] = acc
        da1 = da1s[...]
        da_val = da1 * sig
        da_gate = da1 * a_val * (sig * (1.0 - sig))

        db_gate = dbo * cv_ref[...].astype(F32)
        cb = c_gate * b_val
        dcb = jnp.zeros((tm, B_DIM), F32)
        for j in range(B_CONV_WIDTH):
            k = B_CONV_WIDTH - 1 - j
            sl = eb[j:j + tm, :]
            dcb = dcb + cbw_ref[k:k + 1, :] * sl
            dcbw_ref[k:k + 1, :] += jnp.sum(sl * cb, axis=0, keepdims=True)
        dz = jnp.concatenate([da_val, da_gate, db_gate, dcb * b_val, dcb * c_gate], axis=1).astype(BF16)
        dz_ref[...] = dz
        dn = jnp.zeros((tm, D_MODEL), F32)
        for j in range(N_CHIPS):
            dn = dn + _dot_nt(dz[:, j * ws:(j + 1) * ws], win_v[j])
        xv = x_ref[...]
        g = g_ref[...]
        _, rstd = _rms_fwd(xv, g)
        dx, dg = _rms_bwd(dn, xv, rstd, g)
        dx_ref[...] = dhv + dx
        dg_ref[...] += dg

        @pl.when(i == nt - 1)
        def _():
            for k in range(A_CONV_WIDTH):
                dcaw_ref[k:k + 1, :] = jnp.sum(dw_acc[k], axis=0, keepdims=True)

    row = lambda cols: jax.ShapeDtypeStruct((1, cols), F32)
    rs_ = functools.partial(_row_spec, rev_nt=nt)
    return _pallas(
        body, [dh, x, norm_g, z, a2, cv, w_in, conv_a_w, ln_g, ln_b, conv_b_w, w_out], name="bwd_even", grid=(nt,),
        in_specs=[rs_(tm, D_MODEL), rs_(tm, D_MODEL), _full_spec((1, D_MODEL)), rs_(tm, IN_EVEN), rs_(tm, A_DIM),
                  rs_(tm, B_DIM), ANY, _full_spec((A_CONV_WIDTH, A_DIM)), _full_spec((1, A_DIM)), _full_spec((1, A_DIM)),
                  _full_spec((B_CONV_WIDTH, B_DIM)), ANY],
        out_specs=[rs_(tm, D_MODEL), rs_(tm, IN_EVEN), _full_spec((1, D_MODEL)), _full_spec((A_CONV_WIDTH, A_DIM)),
                   _full_spec((1, A_DIM)), _full_spec((1, A_DIM)), _full_spec((1, A_DIM)), _full_spec((B_CONV_WIDTH, B_DIM))],
        out_shape=[jax.ShapeDtypeStruct((tokens, D_MODEL), F32), jax.ShapeDtypeStruct((tokens, IN_EVEN), BF16),
                   row(D_MODEL), jax.ShapeDtypeStruct((A_CONV_WIDTH, A_DIM), F32), row(A_DIM), row(A_DIM), row(A_DIM),
                   jax.ShapeDtypeStruct((B_CONV_WIDTH, B_DIM), F32)],
        scratch_shapes=[pltpu.VMEM((N_CHIPS, D_MODEL, ws), BF16), pltpu.VMEM((D_MODEL, D_MODEL), BF16),
                        pltpu.VMEM((SUBLANES, tm + A_HALO, A_DIM), F32), pltpu.VMEM((tm + B_HALO, B_DIM), F32),
                        pltpu.VMEM((tm, A_DIM), F32), pltpu.VMEM((tm, A_DIM), F32),
                        pltpu.VMEM((A_CONV_WIDTH, CONV_ROWS, A_DIM), F32), pltpu.SemaphoreType.DMA],
        vmem_mib=56, riders=riders)


def _wgrad(a, b, name, *, col_shards, riders=()):
    tokens, m = a.shape
    n = b.shape[1]
    kc = 512
    if col_shards:
        bm, bn = m // 2, n // N_CHIPS
        grid = (2, N_CHIPS)
        out_spec = pl.BlockSpec((None, None, bm, bn), lambda i, j: (j, i, 0, 0))
    elif m // 8 >= MXU_ROWS:
        bm, bn = m // 8, n
        grid = (8, 1)
        out_spec = pl.BlockSpec((None, None, bm, bn), lambda i, j: (i // 2, i % 2, 0, 0))
    else:
        bm, bn = m // N_CHIPS, n
        grid = (N_CHIPS, 1)
        out_spec = pl.BlockSpec((None, 2, bm // 2, bn), lambda i, j: (i, 0, 0, 0))

    def body(a_ref, b_ref, o_ref):
        acc = jnp.zeros((bm, bn), F32)
        for k0 in range(0, tokens, kc):
            acc = acc + _dot_tn(a_ref[k0:k0 + kc, :].astype(BF16), b_ref[k0:k0 + kc, :].astype(BF16))
        if len(o_ref.shape) == 3:
            o_ref[0] = acc[0:bm // 2]
            o_ref[1] = acc[bm // 2:bm]
        else:
            o_ref[...] = acc

    out_rows = m // 2 if col_shards else m // 8
    outs, routs = _pallas(
        body, [a, b], name=name, grid=grid,
        in_specs=[pl.BlockSpec((tokens, bm), lambda i, j: (0, i)), pl.BlockSpec((tokens, bn), lambda i, j: (0, j))],
        out_specs=[out_spec], out_shape=[jax.ShapeDtypeStruct((N_CHIPS, 2, out_rows, bn), F32)],
        vmem_mib=56, riders=riders)
    return outs[0], routs


class _GradReduce:
    def __init__(self, name, grad):
        self.name, self.grad = name, grad
        self.from_sibling = self.chip_sum = self.from_chips = self.full = None

    def pair_swap(self):
        return _PairSwap([self.grad])

    def took_pair(self, outs):
        self.chip_sum = _add_pair(self.grad, outs[0], f"pair_sum_{self.name}")

    def chip_swap(self):
        return _ChipSwap([self.chip_sum])

    def took_chips(self, outs):
        self.full = _add_chips(self.chip_sum, outs[0], f"chip_sum_{self.name}")

    def pair_share(self):
        return _PairShare([self.full])

    def took_share(self, outs):
        self.full = outs[0]

    def reduced(self):
        return jnp.reshape(self.full, (2 * self.full.shape[1], self.full.shape[2]))


def _forward_backward(x2, tgt2, gathered, staged, conv_a_w, conv_b_w, od_norm, od_bias, od_lng, od_lnb,
                      ev_norm_g, ev_conv_a_b, ev_ln_a_g, ev_ln_a_b, od_w_s, od_b_s, mlp_norm_g, final_norm_g,
                      *, tm, seq, distributed=True):
    d = x2.shape[1]
    w = dict(gathered)
    b_s_rows = jnp.broadcast_to(od_b_s[0][:, :, None], (C_GROUPS, CHUNK, CHUNK))

    def ride(*names):
        return [_Gather([staged[nm] for nm in names])] if distributed else []

    def land(routs, *names):
        if distributed:
            for nm, buf in zip(names, routs[0]):
                w[nm] = buf

    def as_cols(buf):
        return jnp.reshape(buf, (N_CHIPS, 2 * buf.shape[2], buf.shape[3]))

    def as_rows(buf):
        return jnp.reshape(buf, (8 * buf.shape[2], buf.shape[3]))

    (h1, n0, z, a2, cv, mix), routs = _fwd_even(
        x2, ev_norm_g, as_cols(w["ev_in"]), conv_a_w, ev_conv_a_b, ev_ln_a_g, ev_ln_a_b, conv_b_w, as_rows(w["ev_out"]),
        tm=tm, seq=seq, riders=ride("w1_0", "w2_0"))
    land(routs, "w1_0", "w2_0")
    (h2, n1, p0, q0), routs = _fwd_mlp(h1, mlp_norm_g[0:1], as_cols(w["w1_0"]), as_cols(w["w2_0"]), 0, tm=tm,
                                       riders=ride("od_in", "od_out", "w1_1"))
    land(routs, "od_in", "od_out", "w1_1")
    (h3, n2, s, cdf, sv, y), routs = _fwd_odd(h2, od_norm, as_cols(w["od_in"]), od_bias, od_lng, od_lnb, od_w_s[0], b_s_rows,
                                         as_rows(w["od_out"]), tm=tm, riders=ride("w2_1"))
    land(routs, "w2_1")
    (h4, n3, p1, q1), _ = _fwd_mlp(h3, mlp_norm_g[1:2], as_cols(w["w1_1"]), as_cols(w["w2_1"]), 1, tm=tm)
    loss_part, dh4, dh4b, d_final_g = _loss_head(h4, jnp.reshape(final_norm_g, (1, d)), tgt2, tm=tm)

    red = {}

    def swap(*names):
        return [red[nm].pair_swap() for nm in names] if distributed else []

    def chips(*names):
        return [red[nm].chip_swap() for nm in names] if distributed else []

    def share(*names):
        return [red[nm].pair_share() for nm in names] if distributed else []

    def took(routs, *steps):
        if distributed:
            for (nm, what), outs in zip(steps, routs):
                getattr(red[nm], what)(outs)

    g, _ = _wgrad(q1, dh4b, "wgrad_w2_1", col_shards=False)
    red["w2_1"] = _GradReduce("w2_1", g)
    (dh3, dh3b, dp1, d_mlp_g1), routs = _bwd_mlp(dh4, h3, mlp_norm_g[1:2], p1, as_cols(w["w1_1"]), as_cols(w["w2_1"]), 1, tm=tm,
                                           riders=swap("w2_1"))
    took(routs, ("w2_1", "took_pair"))
    g, _ = _wgrad(n3, dp1, "wgrad_w1_1", col_shards=True)
    red["w1_1"] = _GradReduce("w1_1", g)
    g, routs = _wgrad(y, dh3b, "wgrad_od_out", col_shards=False, riders=swap("w1_1"))
    red["od_out"] = _GradReduce("od_out", g)
    took(routs, ("w1_1", "took_pair"))
    (dh2, dh2b, ds, d_od_norm, d_od_bin, d_od_lng, d_od_lnb, d_ws, d_bs), routs = _bwd_odd(
        dh3, h2, od_norm, s, cdf, sv, as_cols(w["od_in"]), od_lng, od_lnb, od_w_s[0], as_rows(w["od_out"]), tm=tm,
        riders=chips("w2_1") + swap("od_out"))
    took(routs, ("w2_1", "took_chips"), ("od_out", "took_pair"))
    g, routs = _wgrad(n2, ds, "wgrad_od_in", col_shards=True, riders=share("w2_1"))
    red["od_in"] = _GradReduce("od_in", g)
    took(routs, ("w2_1", "took_share"))
    g, routs = _wgrad(q0, dh2b, "wgrad_w2_0", col_shards=False, riders=swap("od_in"))
    red["w2_0"] = _GradReduce("w2_0", g)
    took(routs, ("od_in", "took_pair"))
    (dh1, dh1b, dp0, d_mlp_g0), routs = _bwd_mlp(dh2, h1, mlp_norm_g[0:1], p0, as_cols(w["w1_0"]), as_cols(w["w2_0"]), 0, tm=tm,
                                           riders=chips("w1_1") + chips("od_out") + chips("od_in") + swap("w2_0"))
    took(routs, ("w1_1", "took_chips"), ("od_out", "took_chips"), ("od_in", "took_chips"), ("w2_0", "took_pair"))
    g, routs = _wgrad(n1, dp0, "wgrad_w1_0", col_shards=True, riders=share("w1_1") + share("od_out") + share("od_in"))
    red["w1_0"] = _GradReduce("w1_0", g)
    took(routs, ("w1_1", "took_share"), ("od_out", "took_share"), ("od_in", "took_share"))
    g, routs = _wgrad(mix, dh1b, "wgrad_ev_out", col_shards=False, riders=swap("w1_0"))
    red["ev_out"] = _GradReduce("ev_out", g)
    took(routs, ("w1_0", "took_pair"))

    early = {"loss": loss_part, "od_w_s": d_ws, "od_b_s": d_bs, "mlp_norm_g0": d_mlp_g0, "mlp_norm_g1": d_mlp_g1,
             "final_norm_g": d_final_g, "od_norm_g": d_od_norm, "od_b_in": d_od_bin, "od_ln_v_g": d_od_lng,
             "od_ln_v_b": d_od_lnb}
    share_early = [_ShareAll(list(early.values()))] if distributed else []
    (dx, dz, d_ev_norm, d_caw, d_cab, d_ev_lng, d_ev_lnb, d_cbw), routs = _bwd_even(
        dh1, x2, ev_norm_g, z, a2, cv, as_cols(w["ev_in"]), conv_a_w, ev_ln_a_g, ev_ln_a_b, conv_b_w, as_rows(w["ev_out"]),
        tm=tm, seq=seq, riders=chips("w2_0") + chips("w1_0") + swap("ev_out") + share_early)
    took(routs, ("w2_0", "took_chips"), ("w1_0", "took_chips"), ("ev_out", "took_pair"))
    late = {"ev_norm_g": d_ev_norm, "ev_conv_a_b": d_cab, "ev_ln_a_g": d_ev_lng, "ev_ln_a_b": d_ev_lnb,
            "ev_conv_a_w": d_caw, "ev_conv_b_w": d_cbw}
    share_late = [_ShareAll(list(late.values()))] if distributed else []
    g, routs2 = _wgrad(n0, dz, "wgrad_ev_in", col_shards=True,
                       riders=chips("ev_out") + share("w2_0") + share("w1_0") + share_late)
    red["ev_in"] = _GradReduce("ev_in", g)
    took(routs2, ("ev_out", "took_chips"), ("w2_0", "took_share"), ("w1_0", "took_share"))
    own = {**early, **late}
    landed = dict(zip(own.keys(), routs[3] + routs2[3])) if distributed else None
    return dx, red, own, landed


def _rows128(a):
    rows = jnp.reshape(a, (-1, LANES))
    pad = (-rows.shape[0]) % SUBLANES
    return jnp.pad(rows, ((0, pad), (0, 0))) if pad else rows


def _pack(arrays):
    return jnp.concatenate([_rows128(a) for a in arrays], axis=0)


def _unpack(buf, shapes):
    out, r0 = [], 0
    for shp in shapes:
        size = 1
        for dim in shp:
            size *= dim
        nr = size // LANES
        out.append(jnp.reshape(buf[r0:r0 + nr], shp))
        r0 += nr + (-nr) % SUBLANES
    return out


def kernel(x, ev_norm_g, ev_w_in, ev_conv_a_w, ev_conv_a_b, ev_ln_a_g, ev_ln_a_b, ev_conv_b_w, ev_w_out, od_norm_g, od_w_in, od_b_in, od_ln_v_g, od_ln_v_b, od_w_s, od_b_s, od_w_out, mlp_norm_g, mlp_w1, mlp_w2, final_norm_g, loss_target, m_ev_norm_g, m_ev_w_in, m_ev_conv_a_w, m_ev_conv_a_b, m_ev_ln_a_g, m_ev_ln_a_b, m_ev_conv_b_w, m_ev_w_out, m_od_norm_g, m_od_w_in, m_od_b_in, m_od_ln_v_g, m_od_ln_v_b, m_od_w_s, m_od_b_s, m_od_w_out, m_mlp_norm_g, m_mlp_w1, m_mlp_w2, m_final_norm_g, v_ev_norm_g, v_ev_w_in, v_ev_conv_a_w, v_ev_conv_a_b, v_ev_ln_a_g, v_ev_ln_a_b, v_ev_conv_b_w, v_ev_w_out, v_od_norm_g, v_od_w_in, v_od_b_in, v_od_ln_v_g, v_od_ln_v_b, v_od_w_s, v_od_b_s, v_od_w_out, v_mlp_norm_g, v_mlp_w1, v_mlp_w2, v_final_norm_g):
    tm = TOKEN_TILE
    batch, seq, d = x.shape
    tokens = batch * seq
    x2 = jnp.reshape(x, (tokens, d))
    tgt2 = jnp.reshape(loss_target, (tokens, d))
    chip = 2 * lax.axis_index("x") + lax.axis_index("y")

    small_shapes = [(A_CONV_WIDTH, LANES), (B_CONV_WIDTH, LANES), (256,), (512,), (256,), (256,)]
    small_shard = _pack([ev_conv_a_w[0], ev_conv_b_w[0], od_norm_g[0], od_b_in[0], od_ln_v_g[0], od_ln_v_b[0]])
    small_shard = jnp.pad(small_shard, ((0, (-small_shard.shape[0]) % (2 * SUBLANES)), (0, 0)))
    first = [_place_shard(ev_w_in, 0, BF16, "place_ev_w_in"), _place_shard(ev_w_out, 0, BF16, "place_ev_w_out"),
             _place_shard(small_shard[None], 0, F32, "place_small")]
    staged = {
        "w1_0": _place_shard(mlp_w1, 0, BF16, "place_w1_0"), "w2_0": _place_shard(mlp_w2, 0, BF16, "place_w2_0"),
        "od_in": _place_shard(od_w_in, 0, BF16, "place_od_w_in"), "od_out": _place_shard(od_w_out, 0, BF16, "place_od_w_out"),
        "w1_1": _place_shard(mlp_w1, 1, BF16, "place_w1_1"), "w2_1": _place_shard(mlp_w2, 1, BF16, "place_w2_1"),
    }
    (g_ev_in, g_ev_out, g_small), = _exchange([_Gather(first)], "gather_first")
    small_all = jnp.reshape(g_small, (N_CHIPS, -1, LANES))
    per_chip = [_unpack(small_all[q], small_shapes) for q in range(N_CHIPS)]
    conv_a_w = jnp.concatenate([pc[0] for pc in per_chip], axis=1)
    conv_b_w = jnp.concatenate([pc[1] for pc in per_chip], axis=1)
    od_norm = jnp.concatenate([pc[2] for pc in per_chip])[None, :]
    od_bias = jnp.concatenate([pc[3] for pc in per_chip])[None, :]
    od_lng = jnp.concatenate([pc[4] for pc in per_chip])[None, :]
    od_lnb = jnp.concatenate([pc[5] for pc in per_chip])[None, :]

    dx, red, own, landed = _forward_backward(
        x2, tgt2, {"ev_in": g_ev_in, "ev_out": g_ev_out}, staged, conv_a_w, conv_b_w, od_norm, od_bias, od_lng, od_lnb,
        ev_norm_g, ev_conv_a_b, ev_ln_a_g, ev_ln_a_b, od_w_s, od_b_s, mlp_norm_g, final_norm_g, tm=tm, seq=seq)

    routs = _exchange([red["ev_in"].pair_swap(), red["ev_out"].pair_share()], "reduce_tail_1")
    red["ev_in"].took_pair(routs[0])
    red["ev_out"].took_share(routs[1])
    routs = _exchange([red["ev_in"].chip_swap()], "reduce_tail_2")
    red["ev_in"].took_chips(routs[0])
    routs = _exchange([red["ev_in"].pair_share()], "reduce_tail_3")
    red["ev_in"].took_share(routs[0])

    given = {"ev_norm_g": (ev_norm_g, m_ev_norm_g, v_ev_norm_g), "ev_conv_a_b": (ev_conv_a_b, m_ev_conv_a_b, v_ev_conv_a_b),
             "ev_ln_a_g": (ev_ln_a_g, m_ev_ln_a_g, v_ev_ln_a_g), "ev_ln_a_b": (ev_ln_a_b, m_ev_ln_a_b, v_ev_ln_a_b),
             "od_w_s": (od_w_s, m_od_w_s, v_od_w_s), "od_b_s": (od_b_s, m_od_b_s, v_od_b_s),
             "mlp_norm_g": (mlp_norm_g, m_mlp_norm_g, v_mlp_norm_g), "final_norm_g": (final_norm_g, m_final_norm_g, v_final_norm_g),
             "ev_conv_a_w": (ev_conv_a_w, m_ev_conv_a_w, v_ev_conv_a_w), "ev_conv_b_w": (ev_conv_b_w, m_ev_conv_b_w, v_ev_conv_b_w),
             "od_norm_g": (od_norm_g, m_od_norm_g, v_od_norm_g), "od_b_in": (od_b_in, m_od_b_in, v_od_b_in),
             "od_ln_v_g": (od_ln_v_g, m_od_ln_v_g, v_od_ln_v_g), "od_ln_v_b": (od_ln_v_b, m_od_ln_v_b, v_od_ln_v_b)}
    shaped = {nm: tuple(jnp.reshape(a, shape) for a in given[nm]) for nm, shape, _, _ in SMALL_WEIGHTS}
    loss11, small_upd = _small_update(own, landed, shaped)
    loss = loss11[0, 0]
    upd = {nm: [jnp.reshape(o, given[nm][0].shape) for o in outs] for nm, outs in small_upd.items()}

    def big_update(wt, m, v, names, call):
        grads = [red[nm].reduced() for nm in names]
        shp3 = (len(grads),) + grads[0].shape
        outs, _ = _adamw(jnp.reshape(wt, shp3), jnp.reshape(m, shp3), jnp.reshape(v, shp3), grads, call)
        return [jnp.reshape(o, wt.shape) for o in outs], None

    upd["mlp_w2"], _ = big_update(mlp_w2, m_mlp_w2, v_mlp_w2, ["w2_0", "w2_1"], "adamw_mlp_w2")
    upd["mlp_w1"], _ = big_update(mlp_w1, m_mlp_w1, v_mlp_w1, ["w1_0", "w1_1"], "adamw_mlp_w1")
    upd["ev_w_in"], _ = big_update(ev_w_in, m_ev_w_in, v_ev_w_in, ["ev_in"], "adamw_ev_w_in")
    upd["ev_w_out"], _ = big_update(ev_w_out, m_ev_w_out, v_ev_w_out, ["ev_out"], "adamw_ev_w_out")
    upd["od_w_in"], _ = big_update(od_w_in, m_od_w_in, v_od_w_in, ["od_in"], "adamw_od_w_in")
    upd["od_w_out"], _ = big_update(od_w_out, m_od_w_out, v_od_w_out, ["od_out"], "adamw_od_w_out")

    order = ["ev_norm_g", "ev_w_in", "ev_conv_a_w", "ev_conv_a_b", "ev_ln_a_g", "ev_ln_a_b", "ev_conv_b_w", "ev_w_out",
             "od_norm_g", "od_w_in", "od_b_in", "od_ln_v_g", "od_ln_v_b", "od_w_s", "od_b_s", "od_w_out", "mlp_norm_g",
             "mlp_w1", "mlp_w2", "final_norm_g"]
    grad_x = jnp.reshape(dx, x.shape)
    return (loss, grad_x, *[upd[nm][0] for nm in order], *[upd[nm][1] for nm in order],
            *[upd[nm][2] for nm in order], *[upd[nm][3] for nm in order])
```

```python
import functools

import jax
import jax.numpy as jnp
from jax import lax
from jax.experimental import pallas as pl
from jax.experimental.pallas import tpu as pltpu

F32 = jnp.float32
BF16 = jnp.bfloat16

D_MODEL = 1024
A_DIM = 512
B_DIM = 512
IN_EVEN = 2 * A_DIM + 3 * B_DIM
A_CONV_WIDTH = 31
B_CONV_WIDTH = 3
CHUNK = 128
C_GROUPS = 8
C_DIM = 1024
D_FF = 4096
RMS_EPS = 1e-6
LN_EPS = 1e-5
ADAM_LR = 0.001
ADAM_B1 = 0.9
ADAM_B2 = 0.999
ADAM_EPS = 1e-08
ADAM_WD = 0.01
ADAM_STEP = 10

N_CHIPS = 4
N_DEV = 8
TOKEN_TILE = 512
A_HALO = 32
B_HALO = 8
CONV_ROWS = 16
PAIR = 2 * CHUNK
LANES = 128
SUBLANES = 8
MXU_ROWS = 256
MIB = 1024 * 1024
MESH = pl.DeviceIdType.MESH
ANY = pl.BlockSpec(memory_space=pl.ANY)


def _dot(a, b):
    return lax.dot_general(a, b, (((1,), (0,)), ((), ())), preferred_element_type=F32)


def _dot_nt(a, b):
    return lax.dot_general(a, b, (((1,), (1,)), ((), ())), preferred_element_type=F32)


def _dot_tn(a, b):
    return lax.dot_general(a, b, (((0,), (0,)), ((), ())), preferred_element_type=F32)


def _params(vmem_mib, n_axes=1):
    return pltpu.CompilerParams(dimension_semantics=("arbitrary",) * n_axes, vmem_limit_bytes=vmem_mib * MIB)


def _row_spec(tm, cols, rev_nt=None):
    if rev_nt is None:
        return pl.BlockSpec((tm, cols), lambda i: (i, 0))
    return pl.BlockSpec((tm, cols), lambda i: (rev_nt - 1 - i, 0))


def _full_spec(shape):
    nd = len(shape)
    return pl.BlockSpec(shape, lambda i: (0,) * nd)


def _block_rows(rows, cap=512):
    best = SUBLANES
    for br in range(SUBLANES, min(rows, cap) + 1, SUBLANES):
        if rows % br == 0:
            best = br
    return best


N_LOADS = 2 * N_CHIPS


class _Weights:
    def __init__(self, sems):
        self.sems, self.copies = sems, []

    def start(self, src, dst):
        cp = pltpu.make_async_copy(src, dst, self.sems.at[len(self.copies)])
        self.copies.append(cp)
        pl.when(pl.program_id(0) == 0)(cp.start)
        return len(self.copies) - 1

    def ready(self, k):
        pl.when(pl.program_id(0) == 0)(self.copies[k].wait)


def _rms_fwd(x, g):
    rstd = lax.rsqrt(jnp.mean(x * x, axis=-1, keepdims=True) + RMS_EPS)
    return x * rstd * g, rstd


def _rms_bwd(dn, x, rstd, g):
    a = dn * g
    xh = x * rstd
    dx = rstd * (a - xh * jnp.mean(a * xh, axis=-1, keepdims=True))
    dg = jnp.sum(dn * xh, axis=0, keepdims=True)
    return dx, dg


def _ln_stats(v):
    mu = jnp.mean(v, axis=-1, keepdims=True)
    xc = v - mu
    rs = lax.rsqrt(jnp.mean(xc * xc, axis=-1, keepdims=True) + LN_EPS)
    return xc * rs, rs


def _ln_bwd(dy, xhat, rs, g):
    dxh = dy * g
    dv = rs * (dxh - jnp.mean(dxh, axis=-1, keepdims=True) - xhat * jnp.mean(dxh * xhat, axis=-1, keepdims=True))
    return dv, jnp.sum(dy * xhat, axis=0, keepdims=True), jnp.sum(dy, axis=0, keepdims=True)


def _gelu_cdf(s):
    return 0.5 * (1.0 + lax.erf(s * 0.7071067811865476))


def _mesh_pos():
    return lax.axis_index("x"), lax.axis_index("y"), lax.axis_index("c")


def _other_chips(x, y):
    return [(1 - x, y), (x, 1 - y), (1 - x, 1 - y)]


def _remote(src, dst, send_sem, recv_sem, to):
    return pltpu.make_async_remote_copy(src_ref=src, dst_ref=dst, send_sem=send_sem, recv_sem=recv_sem,
                                        device_id=to, device_id_type=MESH)


def _like(arrays):
    return [jax.ShapeDtypeStruct(a.shape, a.dtype) for a in arrays]


class _Gather:
    def __init__(self, bufs):
        self.ins = list(bufs)
        self.out_shapes = _like(bufs)
        self.aliases = {t: t for t in range(len(bufs))}
        self.n_sems = 6 * len(bufs)

    def _ici(self, ins, outs, send, recv, t, k, chip, mine, c):
        return _remote(ins[t].at[mine, c], outs[t].at[mine, c], send.at[6 * t + k], recv.at[6 * t + k], (*chip, c))

    def start(self, ins, outs, send, recv):
        x, y, c = _mesh_pos()
        for t in range(len(ins)):
            for k, chip in enumerate(_other_chips(x, y)):
                self._ici(ins, outs, send, recv, t, k, chip, 2 * x + y, c).start()

    def finish(self, ins, outs, send, recv):
        x, y, c = _mesh_pos()
        me, sibling = (x, y, c), (x, y, 1 - c)
        chips = _other_chips(x, y)
        passed = []
        for t in range(len(ins)):
            for k, chip in enumerate(chips):
                blk = outs[t].at[2 * chip[0] + chip[1], c]
                _remote(blk, blk, send.at[6 * t + k], recv.at[6 * t + k], me).wait_recv()
                cp = _remote(blk, blk, send.at[6 * t + 3 + k], recv.at[6 * t + 3 + k], sibling)
                cp.start()
                passed.append(cp)
        for t in range(len(ins)):
            for k, chip in enumerate(chips):
                blk = outs[t].at[2 * chip[0] + chip[1], 1 - c]
                _remote(blk, blk, send.at[6 * t + 3 + k], recv.at[6 * t + 3 + k], me).wait_recv()
        for t in range(len(ins)):
            for k, chip in enumerate(chips):
                self._ici(ins, outs, send, recv, t, k, chip, 2 * x + y, c).wait_send()
        for cp in passed:
            cp.wait_send()


class _PairSwap:
    def __init__(self, grads):
        self.ins = list(grads)
        self.out_shapes = [jax.ShapeDtypeStruct((g.shape[0],) + g.shape[2:], g.dtype) for g in grads]
        self.aliases = {}
        self.n_sems = len(grads)

    def _copies(self, ins, outs, send, recv):
        x, y, c = _mesh_pos()
        return [_remote(ins[t].at[:, 1 - c], outs[t], send.at[t], recv.at[t], (x, y, 1 - c)) for t in range(len(ins))]

    def start(self, ins, outs, send, recv):
        for cp in self._copies(ins, outs, send, recv):
            cp.start()

    def finish(self, ins, outs, send, recv):
        for cp in self._copies(ins, outs, send, recv):
            cp.wait()


class _ChipSwap:
    def __init__(self, parts):
        self.ins = list(parts)
        self.out_shapes = [jax.ShapeDtypeStruct((3,) + p.shape[1:], p.dtype) for p in parts]
        self.aliases = {}
        self.n_sems = 3 * len(parts)

    def _copies(self, ins, outs, send, recv):
        x, y, c = _mesh_pos()
        return [_remote(ins[t].at[2 * chip[0] + chip[1]], outs[t].at[k], send.at[3 * t + k], recv.at[3 * t + k], (*chip, c))
                for t in range(len(ins)) for k, chip in enumerate(_other_chips(x, y))]

    def start(self, ins, outs, send, recv):
        for cp in self._copies(ins, outs, send, recv):
            cp.start()

    def finish(self, ins, outs, send, recv):
        for cp in self._copies(ins, outs, send, recv):
            cp.wait()


class _PairShare:
    def __init__(self, fulls):
        self.ins = list(fulls)
        self.out_shapes = _like(fulls)
        self.aliases = {t: t for t in range(len(fulls))}
        self.n_sems = len(fulls)

    def _copies(self, ins, outs, send, recv):
        x, y, c = _mesh_pos()
        return [_remote(ins[t].at[c], outs[t].at[c], send.at[t], recv.at[t], (x, y, 1 - c)) for t in range(len(ins))]

    def start(self, ins, outs, send, recv):
        for cp in self._copies(ins, outs, send, recv):
            cp.start()

    def finish(self, ins, outs, send, recv):
        for cp in self._copies(ins, outs, send, recv):
            cp.wait()


class _ShareAll:
    def __init__(self, arrays):
        self.ins = list(arrays)
        self.out_shapes = [jax.ShapeDtypeStruct((N_DEV,) + a.shape, a.dtype) for a in arrays]
        self.aliases = {}
        self.n_sems = (N_DEV - 1) * len(arrays)

    def _peers(self):
        x, y, c = _mesh_pos()
        flips = [((r >> 2) & 1, (r >> 1) & 1, r & 1) for r in range(1, N_DEV)]
        return (x, y, c), [(x ^ fx, y ^ fy, c ^ fc) for fx, fy, fc in flips]

    def _sends(self, ins, outs, send, recv):
        (x, y, c), peers = self._peers()
        mine = 4 * x + 2 * y + c
        return [_remote(ins[a], outs[a].at[mine], send.at[7 * a + r], recv.at[7 * a + r], peer)
                for a in range(len(ins)) for r, peer in enumerate(peers)]

    def start(self, ins, outs, send, recv):
        for cp in self._sends(ins, outs, send, recv):
            cp.start()

    def finish(self, ins, outs, send, recv):
        (x, y, c), peers = self._peers()
        for a in range(len(ins)):
            for r, (px, py, pc) in enumerate(peers):
                blk = outs[a].at[4 * px + 2 * py + pc]
                _remote(blk, blk, send.at[7 * a + r], recv.at[7 * a + r], (x, y, c)).wait_recv()
        for cp in self._sends(ins, outs, send, recv):
            cp.wait_send()


def _pallas(body, operands, *, name, grid, in_specs, out_specs, out_shape, scratch_shapes=(), vmem_mib=32, riders=()):
    in_specs, out_specs, out_shape, scratch_shapes = list(in_specs), list(out_specs), list(out_shape), list(scratch_shapes)
    if not riders:
        outs = pl.pallas_call(body, name=name, grid=grid, in_specs=in_specs, out_specs=out_specs, out_shape=out_shape,
                              scratch_shapes=scratch_shapes, compiler_params=_params(vmem_mib, len(grid)))(*operands)
        return list(outs), []
    n_in, n_out, n_scr = len(in_specs), len(out_specs), len(scratch_shapes)
    r_in = [len(r.ins) for r in riders]
    r_out = [len(r.out_shapes) for r in riders]
    steps = 1
    for g in grid:
        steps *= g

    def wrapped(*refs):
        refs = list(refs)
        ins, refs = refs[:n_in], refs[n_in:]
        rins = []
        for k in r_in:
            rins.append(refs[:k])
            refs = refs[k:]
        outs, refs = refs[:n_out], refs[n_out:]
        routs = []
        for k in r_out:
            routs.append(refs[:k])
            refs = refs[k:]
        scr, sems = refs[:n_scr], refs[n_scr:]
        step = 0
        for ax, g in enumerate(grid):
            step = step * g + pl.program_id(ax)

        def each(what):
            for j, r in enumerate(riders):
                getattr(r, what)(rins[j], routs[j], sems[2 * j], sems[2 * j + 1])

        if grid:
            pl.when(step == 0)(lambda: each("start"))
        else:
            each("start")
        body(*ins, *outs, *scr)
        if grid:
            pl.when(step == steps - 1)(lambda: each("finish"))
        else:
            each("finish")

    aliases, off_in, off_out = {}, n_in, n_out
    for r, ki, ko in zip(riders, r_in, r_out):
        for i, o in r.aliases.items():
            aliases[off_in + i] = off_out + o
        off_in, off_out = off_in + ki, off_out + ko
    sems = []
    for r in riders:
        sems += [pltpu.SemaphoreType.DMA((r.n_sems,)), pltpu.SemaphoreType.DMA((r.n_sems,))]
    res = pl.pallas_call(
        wrapped, name=name, grid=grid,
        in_specs=in_specs + [ANY] * sum(r_in), out_specs=out_specs + [ANY] * sum(r_out),
        out_shape=out_shape + [s for r in riders for s in r.out_shapes],
        scratch_shapes=scratch_shapes + sems, input_output_aliases=aliases,
        compiler_params=pltpu.CompilerParams(dimension_semantics=("arbitrary",) * len(grid),
                                             vmem_limit_bytes=vmem_mib * MIB, has_side_effects=True),
    )(*operands, *[a for r in riders for a in r.ins])
    res = list(res)
    outs, res = res[:n_out], res[n_out:]
    routs = []
    for k in r_out:
        routs.append(res[:k])
        res = res[k:]
    return outs, routs


def _exchange(riders, name):
    return _pallas(lambda: None, [], name=name, grid=(), in_specs=[], out_specs=[], out_shape=[], riders=riders)[1]


def _in_hbm(a):
    return pltpu.with_memory_space_constraint(a, pltpu.HBM)


def _place_shard(w, layer, dtype, name):
    _, rows, cols = w.shape
    half = rows // 2
    br = _block_rows(half)
    nb = half // br
    mine = 2 * lax.axis_index("x") + lax.axis_index("y")

    def body(q_ref, w_ref, o_ref):
        o_ref[...] = w_ref[...].astype(dtype)

    return pl.pallas_call(
        body, name=name,
        grid_spec=pltpu.PrefetchScalarGridSpec(
            num_scalar_prefetch=1, grid=(2, nb),
            in_specs=[pl.BlockSpec((None, br, cols), lambda h, i, q: (layer, h * nb + i, 0))],
            out_specs=pl.BlockSpec((None, None, br, cols), lambda h, i, q: (q[0], h, i, 0))),
        out_shape=jax.ShapeDtypeStruct((N_CHIPS, 2, half, cols), dtype),
        compiler_params=_params(16, 2),
    )(jnp.reshape(mine, (1,)).astype(jnp.int32), w)


def _add_pair(g, recv, name):
    _, _, r, cdim = g.shape
    br = _block_rows(r, 256)
    c = lax.axis_index("c")

    def body(c_ref, g_ref, r_ref, o_ref):
        o_ref[...] = (g_ref[...] + r_ref[...]).astype(BF16)

    return pl.pallas_call(
        body, name=name,
        grid_spec=pltpu.PrefetchScalarGridSpec(
            num_scalar_prefetch=1, grid=(N_CHIPS, r // br),
            in_specs=[pl.BlockSpec((None, None, br, cdim), lambda q, i, c_ref: (q, c_ref[0], i, 0)),
                      pl.BlockSpec((None, br, cdim), lambda q, i, c_ref: (q, i, 0))],
            out_specs=pl.BlockSpec((None, br, cdim), lambda q, i, c_ref: (q, i, 0))),
        out_shape=jax.ShapeDtypeStruct((N_CHIPS, r, cdim), BF16),
        compiler_params=_params(16, 2),
    )(jnp.reshape(c, (1,)).astype(jnp.int32), _in_hbm(g), _in_hbm(recv))


def _add_chips(own, recv, name):
    _, r, cdim = own.shape
    br = _block_rows(r, 256)
    x, y, c = _mesh_pos()

    def body(pos_ref, own_ref, r_ref, o_ref):
        acc = own_ref[...].astype(F32)
        for k in range(3):
            acc = acc + r_ref[k].astype(F32)
        o_ref[...] = acc

    return pl.pallas_call(
        body, name=name,
        grid_spec=pltpu.PrefetchScalarGridSpec(
            num_scalar_prefetch=1, grid=(r // br,),
            in_specs=[pl.BlockSpec((None, br, cdim), lambda i, pos: (pos[0], i, 0)),
                      pl.BlockSpec((3, br, cdim), lambda i, pos: (0, i, 0))],
            out_specs=pl.BlockSpec((None, br, cdim), lambda i, pos: (pos[1], i, 0))),
        out_shape=jax.ShapeDtypeStruct((2, r, cdim), F32),
        compiler_params=_params(16, 1),
    )(jnp.stack([2 * x + y, c]).astype(jnp.int32), _in_hbm(own), _in_hbm(recv))


def _adam_math(w, m, v, g):
    c1 = 1.0 / (1.0 - ADAM_B1 ** ADAM_STEP)
    c2 = 1.0 / (1.0 - ADAM_B2 ** ADAM_STEP)
    m_new = ADAM_B1 * m + (1.0 - ADAM_B1) * g
    v_new = ADAM_B2 * v + (1.0 - ADAM_B2) * (g * g)
    return -ADAM_LR * ((m_new * c1) / (jnp.sqrt(v_new * c2) + ADAM_EPS) + ADAM_WD * w), m_new, v_new


SMALL_WEIGHTS = [
    ("ev_norm_g", (1, D_MODEL), ["ev_norm_g"], None), ("ev_conv_a_b", (1, A_DIM), ["ev_conv_a_b"], None),
    ("ev_ln_a_g", (1, A_DIM), ["ev_ln_a_g"], None), ("ev_ln_a_b", (1, A_DIM), ["ev_ln_a_b"], None),
    ("od_w_s", (C_GROUPS, CHUNK, CHUNK), ["od_w_s"], None), ("od_b_s", (C_GROUPS, CHUNK), ["od_b_s"], None),
    ("mlp_norm_g", (2, D_MODEL), ["mlp_norm_g0", "mlp_norm_g1"], None), ("final_norm_g", (1, D_MODEL), ["final_norm_g"], None),
    ("ev_conv_a_w", (A_CONV_WIDTH, A_DIM // N_CHIPS), ["ev_conv_a_w"], A_DIM // N_CHIPS),
    ("ev_conv_b_w", (B_CONV_WIDTH, B_DIM // N_CHIPS), ["ev_conv_b_w"], B_DIM // N_CHIPS),
    ("od_norm_g", (1, D_MODEL // N_CHIPS), ["od_norm_g"], D_MODEL // N_CHIPS),
    ("od_b_in", (1, 2 * C_DIM // N_CHIPS), ["od_b_in"], 2 * C_DIM // N_CHIPS),
    ("od_ln_v_g", (1, C_DIM // N_CHIPS), ["od_ln_v_g"], C_DIM // N_CHIPS),
    ("od_ln_v_b", (1, C_DIM // N_CHIPS), ["od_ln_v_b"], C_DIM // N_CHIPS),
]


def _small_update(own, landed, weights):
    names = list(own.keys())
    n_g, n_w = len(names), len(SMALL_WEIGHTS)

    def body(*refs):
        refs = list(refs)
        own_refs = dict(zip(names, refs[:n_g]))
        land_refs = dict(zip(names, refs[n_g:2 * n_g]))
        wmv = [refs[2 * n_g + 3 * i:2 * n_g + 3 * i + 3] for i in range(n_w)]
        o0 = 2 * n_g + 3 * n_w
        loss_ref = refs[o0]
        outs = [refs[o0 + 1 + 4 * i:o0 + 5 + 4 * i] for i in range(n_w)]
        acc = dict(zip(names, refs[o0 + 1 + 4 * n_w:]))
        x, y, c = _mesh_pos()
        mine, chip = 4 * x + 2 * y + c, 2 * x + y

        for nm in names:
            for d in range(N_DEV):
                def add(term, nm=nm, d=d):
                    acc[nm][...] = term if d == 0 else acc[nm][...] + term
                pl.when(mine == d)(lambda nm=nm, add=add: add(own_refs[nm][...]))
                pl.when(mine != d)(lambda nm=nm, d=d, add=add: add(land_refs[nm][d]))
        loss_ref[...] = acc["loss"][...]

        def update(i, rows, g):
            w_ref, m_ref, v_ref = wmv[i]
            delta, m_new, v_new = _adam_math(w_ref[rows], m_ref[rows], v_ref[rows], g)
            for ref, val in zip(outs[i], (g, delta, m_new, v_new)):
                ref[rows] = val

        for i, (_, shape, grads, per_chip) in enumerate(SMALL_WEIGHTS):
            for row, gname in enumerate(grads):
                rows = slice(row, row + 1) if len(grads) > 1 else slice(None)
                if per_chip is None:
                    update(i, rows, acc[gname][...])
                else:
                    for q in range(N_CHIPS):
                        pl.when(chip == q)(lambda i=i, rows=rows, gname=gname, q=q, per_chip=per_chip:
                                           update(i, rows, acc[gname][:, q * per_chip:(q + 1) * per_chip]))

    operands = [own[nm] for nm in names] + [landed[nm] for nm in names]
    for nm, _, _, _ in SMALL_WEIGHTS:
        operands += list(weights[nm])
    out_shape = [jax.ShapeDtypeStruct((1, 1), F32)]
    for _, shape, _, _ in SMALL_WEIGHTS:
        out_shape += [jax.ShapeDtypeStruct(shape, F32)] * 4
    res = pl.pallas_call(
        body, name="small_update", grid=(1,),
        in_specs=[_full_spec(a.shape) for a in operands], out_specs=[_full_spec(s.shape) for s in out_shape],
        out_shape=out_shape, scratch_shapes=[pltpu.VMEM(own[nm].shape, F32) for nm in names],
        compiler_params=_params(32, 1),
    )(*[_in_hbm(a) for a in operands])
    return res[0], {nm: res[1 + 4 * i:5 + 4 * i] for i, (nm, _, _, _) in enumerate(SMALL_WEIGHTS)}


def _adamw(w, m, v, grads, name, riders=()):
    layers, r, cdim = w.shape
    br = _block_rows(r, 256 if cdim > LANES else 1024)

    def body(*refs):
        w_ref, m_ref, v_ref = refs[:3]
        g_refs = refs[3:3 + layers]
        go_ref, d_ref, mo_ref, vo_ref = refs[3 + layers:]
        layer = pl.program_id(0)
        for l in range(layers):
            @pl.when(layer == l)
            def _(l=l):
                g = g_refs[l][...]
                go_ref[...] = g
                d_ref[...], mo_ref[...], vo_ref[...] = _adam_math(w_ref[...], m_ref[...], v_ref[...], g)

    spec3 = pl.BlockSpec((None, br, cdim), lambda l, i: (l, i, 0))
    spec2 = pl.BlockSpec((br, cdim), lambda l, i: (i, 0))
    out = jax.ShapeDtypeStruct((layers, r, cdim), F32)
    return _pallas(body, [w, m, v, *[_in_hbm(g) for g in grads]], name=name, grid=(layers, r // br),
                   in_specs=[spec3, spec3, spec3] + [spec2] * layers, out_specs=[spec3] * 4, out_shape=[out] * 4,
                   vmem_mib=32, riders=riders)


def _fill_shifted(buf, rows):
    for b in range(1, SUBLANES):
        buf[b, 0:rows - SUBLANES, :] = buf[0, b:b + rows - SUBLANES, :]


def _window(buf, start, size):
    return buf[start % SUBLANES, start - start % SUBLANES:start - start % SUBLANES + size, :]


def _conv31(src, w_ref, r0, base, init):
    acc = init
    for k in range(A_CONV_WIDTH):
        acc = acc + w_ref[k:k + 1, :] * _window(src, base + k + r0, CONV_ROWS)
    return acc


def _fwd_even(x, norm_g, w_in, conv_a_w, conv_a_b, ln_g, ln_b, conv_b_w, w_out, *, tm, seq, riders=()):
    tokens = x.shape[0]
    nt, tps = tokens // tm, seq // tm

    def body(x_ref, g_ref, win_hbm, caw_ref, cab_ref, lng_ref, lnb_ref, cbw_ref, wout_hbm,
             h_ref, n_ref, z_ref, a2_ref, cv_ref, mix_ref, win_v, wout_v, pa, pb, sem):
        i = pl.program_id(0)

        wt = _Weights(sem)
        k_in = [wt.start(win_hbm.at[j], win_v.at[j]) for j in range(N_CHIPS)]
        k_out = wt.start(wout_hbm, wout_v)

        xv = x_ref[...]
        nf, _ = _rms_fwd(xv, g_ref[...])
        n = nf.astype(BF16)
        n_ref[...] = n
        zs = []
        for j in range(N_CHIPS):
            wt.ready(k_in[j])
            zs.append(_dot(n, win_v[j]))
        z = jnp.concatenate(zs, axis=1)
        z_ref[...] = z.astype(BF16)
        a_val, a_gate = z[:, 0:A_DIM], z[:, A_DIM:2 * A_DIM]
        b_gate, c_gate, b_val = z[:, 1024:1536], z[:, 1536:2048], z[:, 2048:2560]

        first = (i % tps) == 0

        @pl.when(first)
        def _():
            pa[0, 0:A_HALO, :] = jnp.zeros((A_HALO, A_DIM), F32)
            pb[0:B_HALO, :] = jnp.zeros((B_HALO, B_DIM), F32)

        @pl.when(jnp.logical_not(first))
        def _():
            pa[0, 0:A_HALO, :] = pa[0, tm:tm + A_HALO, :]
            pb[0:B_HALO, :] = pb[tm:tm + B_HALO, :]

        pa[0, A_HALO:A_HALO + tm, :] = a_val * jax.nn.sigmoid(a_gate)
        pb[B_HALO:B_HALO + tm, :] = c_gate * b_val
        _fill_shifted(pa, A_HALO + tm)
        bias = jnp.broadcast_to(cab_ref[...], (CONV_ROWS, A_DIM))
        for r0 in range(0, tm, CONV_ROWS):
            a2_ref[r0:r0 + CONV_ROWS, :] = _conv31(pa, caw_ref, r0, A_HALO - (A_CONV_WIDTH - 1), bias)
        xhat, _ = _ln_stats(a2_ref[...])
        a3 = xhat * lng_ref[...] + lnb_ref[...]
        a4 = a3 * jax.nn.sigmoid(a3)
        cv = cbw_ref[0:1, :] * pb[B_HALO - 2:B_HALO - 2 + tm, :]
        cv = cv + cbw_ref[1:2, :] * pb[B_HALO - 1:B_HALO - 1 + tm, :]
        cv = cv + cbw_ref[2:3, :] * pb[B_HALO:B_HALO + tm, :]
        cv_ref[...] = cv.astype(BF16)
        mix = jnp.concatenate([a4, b_gate * cv], axis=1).astype(BF16)
        mix_ref[...] = mix
        wt.ready(k_out)
        h_ref[...] = xv + _dot(mix, wout_v[...])

    shp = lambda cols, dt: jax.ShapeDtypeStruct((tokens, cols), dt)
    return _pallas(
        body, [x, norm_g, w_in, conv_a_w, conv_a_b, ln_g, ln_b, conv_b_w, w_out], name="fwd_even", grid=(nt,),
        in_specs=[_row_spec(tm, D_MODEL), _full_spec((1, D_MODEL)), ANY, _full_spec((A_CONV_WIDTH, A_DIM)),
                  _full_spec((1, A_DIM)), _full_spec((1, A_DIM)), _full_spec((1, A_DIM)),
                  _full_spec((B_CONV_WIDTH, B_DIM)), ANY],
        out_specs=[_row_spec(tm, D_MODEL), _row_spec(tm, D_MODEL), _row_spec(tm, IN_EVEN), _row_spec(tm, A_DIM),
                   _row_spec(tm, B_DIM), _row_spec(tm, D_MODEL)],
        out_shape=[shp(D_MODEL, F32), shp(D_MODEL, BF16), shp(IN_EVEN, BF16), shp(A_DIM, F32), shp(B_DIM, BF16),
                   shp(D_MODEL, BF16)],
        scratch_shapes=[pltpu.VMEM((N_CHIPS, D_MODEL, IN_EVEN // N_CHIPS), BF16), pltpu.VMEM((D_MODEL, D_MODEL), BF16),
                        pltpu.VMEM((SUBLANES, A_HALO + tm, A_DIM), F32), pltpu.VMEM((B_HALO + tm, B_DIM), F32),
                        pltpu.SemaphoreType.DMA((N_LOADS,))],
        vmem_mib=56, riders=riders)


def _fwd_mlp(h, norm_g, w1, w2, layer, *, tm, riders=()):
    tokens = h.shape[0]
    nt = tokens // tm
    fs = D_FF // N_CHIPS

    def body(h_ref, g_ref, w1_hbm, w2_hbm, ho_ref, n_ref, p_ref, q_ref, w1_v, w2_v, sem):
        wt = _Weights(sem)
        k1, k2 = [], []
        for j in range(N_CHIPS):
            k1.append(wt.start(w1_hbm.at[j], w1_v.at[j]))
            k2.append(wt.start(w2_hbm.at[j], w2_v.at[j]))

        xv = h_ref[...]
        nf, _ = _rms_fwd(xv, g_ref[...])
        n = nf.astype(BF16)
        n_ref[...] = n
        acc = xv
        for j in range(N_CHIPS):
            wt.ready(k1[j])
            p = _dot(n, w1_v[j])
            p_ref[:, j * fs:(j + 1) * fs] = p.astype(BF16)
            r = jnp.maximum(p, 0.0)
            q = (r * r).astype(BF16)
            q_ref[:, j * fs:(j + 1) * fs] = q
            wt.ready(k2[j])
            acc = acc + _dot(q, w2_v[j])
        ho_ref[...] = acc

    shp = lambda cols, dt: jax.ShapeDtypeStruct((tokens, cols), dt)
    return _pallas(
        body, [h, norm_g, w1, w2], name=f"fwd_mlp{layer}", grid=(nt,),
        in_specs=[_row_spec(tm, D_MODEL), _full_spec((1, D_MODEL)), ANY, ANY],
        out_specs=[_row_spec(tm, D_MODEL), _row_spec(tm, D_MODEL), _row_spec(tm, D_FF), _row_spec(tm, D_FF)],
        out_shape=[shp(D_MODEL, F32), shp(D_MODEL, BF16), shp(D_FF, BF16), shp(D_FF, BF16)],
        scratch_shapes=[pltpu.VMEM((N_CHIPS, D_MODEL, fs), BF16), pltpu.VMEM((N_CHIPS, fs, D_MODEL), BF16),
                        pltpu.SemaphoreType.DMA((N_LOADS,))],
        vmem_mib=56, riders=riders)


def _tril_mask():
    row = lax.broadcasted_iota(jnp.int32, (CHUNK, CHUNK), 0)
    col = lax.broadcasted_iota(jnp.int32, (CHUNK, CHUNK), 1)
    return row >= col


def _triu_mask():
    row = lax.broadcasted_iota(jnp.int32, (CHUNK, CHUNK), 0)
    col = lax.broadcasted_iota(jnp.int32, (CHUNK, CHUNK), 1)
    return row <= col


def _fwd_odd(h, norm_g, w_in, b_in, ln_g, ln_b, w_s, b_s_rows, w_out, *, tm, riders=()):
    tokens = h.shape[0]
    nt = tokens // tm
    cs = 2 * C_DIM // N_CHIPS

    def body(h_ref, g_ref, win_hbm, bin_ref, lng_ref, lnb_ref, ws_ref, bs_ref, wout_hbm,
             ho_ref, n_ref, s_ref, cdf_ref, sv_ref, y_ref, win_v, wout_v, bd, sem):
        wt = _Weights(sem)
        k_in = [wt.start(win_hbm.at[j], win_v.at[j]) for j in range(N_CHIPS)]
        k_out = wt.start(wout_hbm, wout_v)

        @pl.when(pl.program_id(0) == 0)
        def _():
            mask = _tril_mask()
            bd[...] = jnp.zeros(bd.shape, BF16)
            for g in range(C_GROUPS):
                w = jnp.where(mask, ws_ref[g], 0.0).astype(BF16)
                bd[g, 0:CHUNK, 0:CHUNK] = w
                bd[g, CHUNK:PAIR, CHUNK:PAIR] = w

        xv = h_ref[...]
        nf, _ = _rms_fwd(xv, g_ref[...])
        n = nf.astype(BF16)
        n_ref[...] = n
        ss = []
        for j in range(N_CHIPS):
            wt.ready(k_in[j])
            ss.append(_dot(n, win_v[j]))
        s = jnp.concatenate(ss, axis=1) + bin_ref[...]
        s_ref[...] = s.astype(BF16)
        cdf = _gelu_cdf(s)
        cdf_ref[...] = cdf.astype(BF16)
        zz = s * cdf
        u, v = zz[:, 0:C_DIM], zz[:, C_DIM:2 * C_DIM]
        xhat, _ = _ln_stats(v)
        vn = (xhat * lng_ref[...] + lnb_ref[...]).astype(BF16)
        for g in range(C_GROUPS):
            cols = slice(g * CHUNK, (g + 1) * CHUNK)
            bias = jnp.concatenate([bs_ref[g], bs_ref[g]], axis=0)
            for r0 in range(0, tm, PAIR):
                sv = _dot(bd[g], vn[r0:r0 + PAIR, cols]) + bias
                sv_ref[r0:r0 + PAIR, cols] = sv.astype(BF16)
                y_ref[r0:r0 + PAIR, cols] = (u[r0:r0 + PAIR, cols] * sv).astype(BF16)
        wt.ready(k_out)
        ho_ref[...] = xv + _dot(y_ref[...], wout_v[...])

    shp = lambda cols, dt: jax.ShapeDtypeStruct((tokens, cols), dt)
    return _pallas(
        body, [h, norm_g, w_in, b_in, ln_g, ln_b, w_s, b_s_rows, w_out], name="fwd_odd", grid=(nt,),
        in_specs=[_row_spec(tm, D_MODEL), _full_spec((1, D_MODEL)), ANY, _full_spec((1, 2 * C_DIM)),
                  _full_spec((1, C_DIM)), _full_spec((1, C_DIM)), _full_spec((C_GROUPS, CHUNK, CHUNK)),
                  _full_spec((C_GROUPS, CHUNK, CHUNK)), ANY],
        out_specs=[_row_spec(tm, D_MODEL), _row_spec(tm, D_MODEL), _row_spec(tm, 2 * C_DIM), _row_spec(tm, 2 * C_DIM),
                   _row_spec(tm, C_DIM), _row_spec(tm, C_DIM)],
        out_shape=[shp(D_MODEL, F32), shp(D_MODEL, BF16), shp(2 * C_DIM, BF16), shp(2 * C_DIM, BF16), shp(C_DIM, BF16),
                   shp(C_DIM, BF16)],
        scratch_shapes=[pltpu.VMEM((N_CHIPS, D_MODEL, cs), BF16), pltpu.VMEM((C_DIM, D_MODEL), BF16),
                        pltpu.VMEM((C_GROUPS, PAIR, PAIR), BF16), pltpu.SemaphoreType.DMA((N_LOADS,))],
        vmem_mib=56, riders=riders)


def _loss_head(h, norm_g, target, *, tm):
    tokens = h.shape[0]
    nt = tokens // tm

    def body(h_ref, g_ref, t_ref, loss_ref, dh_ref, dhb_ref, dg_ref):
        @pl.when(pl.program_id(0) == 0)
        def _():
            loss_ref[...] = jnp.zeros((1, 1), F32)
            dg_ref[...] = jnp.zeros((1, D_MODEL), F32)

        xv = h_ref[...]
        g = g_ref[...]
        out, rstd = _rms_fwd(xv, g)
        err = out - t_ref[...]
        per_token = jnp.sum(err * err, axis=1, keepdims=True) * (1.0 / D_MODEL)
        loss_ref[...] += 0.5 * jnp.sum(per_token, axis=0, keepdims=True)
        dx, dg = _rms_bwd(err * (1.0 / D_MODEL), xv, rstd, g)
        dh_ref[...] = dx
        dhb_ref[...] = dx.astype(BF16)
        dg_ref[...] += dg

    return _pallas(
        body, [h, norm_g, target], name="loss_head", grid=(nt,),
        in_specs=[_row_spec(tm, D_MODEL), _full_spec((1, D_MODEL)), _row_spec(tm, D_MODEL)],
        out_specs=[_full_spec((1, 1)), _row_spec(tm, D_MODEL), _row_spec(tm, D_MODEL), _full_spec((1, D_MODEL))],
        out_shape=[jax.ShapeDtypeStruct((1, 1), F32), jax.ShapeDtypeStruct((tokens, D_MODEL), F32),
                   jax.ShapeDtypeStruct((tokens, D_MODEL), BF16), jax.ShapeDtypeStruct((1, D_MODEL), F32)],
        vmem_mib=32)[0]


def _bwd_mlp(dh, h, norm_g, p, w1, w2, layer, *, tm, riders=()):
    tokens = h.shape[0]
    nt = tokens // tm
    fs = D_FF // N_CHIPS

    def body(dh_ref, h_ref, g_ref, p_ref, w1_hbm, w2_hbm, dx_ref, dxb_ref, dp_ref, dg_ref, w1_v, w2_v, sem):
        @pl.when(pl.program_id(0) == 0)
        def _():
            dg_ref[...] = jnp.zeros((1, D_MODEL), F32)

        wt = _Weights(sem)
        k1, k2 = [], []
        for j in range(N_CHIPS):
            k2.append(wt.start(w2_hbm.at[j], w2_v.at[j]))
            k1.append(wt.start(w1_hbm.at[j], w1_v.at[j]))

        dhv = dh_ref[...]
        dhb = dhv.astype(BF16)
        dn = jnp.zeros((tm, D_MODEL), F32)
        for j in range(N_CHIPS):
            wt.ready(k2[j])
            dq = _dot_nt(dhb, w2_v[j])
            r = jnp.maximum(p_ref[:, j * fs:(j + 1) * fs].astype(F32), 0.0)
            dp = ((2.0 * r) * dq).astype(BF16)
            dp_ref[:, j * fs:(j + 1) * fs] = dp
            wt.ready(k1[j])
            dn = dn + _dot_nt(dp, w1_v[j])
        xv = h_ref[...]
        g = g_ref[...]
        _, rstd = _rms_fwd(xv, g)
        dx, dg = _rms_bwd(dn, xv, rstd, g)
        dx_ref[...] = dhv + dx
        dxb_ref[...] = (dhv + dx).astype(BF16)
        dg_ref[...] += dg

    return _pallas(
        body, [dh, h, norm_g, p, w1, w2], name=f"bwd_mlp{layer}", grid=(nt,),
        in_specs=[_row_spec(tm, D_MODEL), _row_spec(tm, D_MODEL), _full_spec((1, D_MODEL)), _row_spec(tm, D_FF), ANY, ANY],
        out_specs=[_row_spec(tm, D_MODEL), _row_spec(tm, D_MODEL), _row_spec(tm, D_FF), _full_spec((1, D_MODEL))],
        out_shape=[jax.ShapeDtypeStruct((tokens, D_MODEL), F32), jax.ShapeDtypeStruct((tokens, D_MODEL), BF16),
                   jax.ShapeDtypeStruct((tokens, D_FF), BF16), jax.ShapeDtypeStruct((1, D_MODEL), F32)],
        scratch_shapes=[pltpu.VMEM((N_CHIPS, D_MODEL, fs), BF16), pltpu.VMEM((N_CHIPS, fs, D_MODEL), BF16),
                        pltpu.SemaphoreType.DMA((N_LOADS,))],
        vmem_mib=56, riders=riders)


def _bwd_odd(dh, h, norm_g, s, cdf, sv, w_in, ln_g, ln_b, w_s, w_out, *, tm, riders=()):
    tokens = h.shape[0]
    nt = tokens // tm
    cs = 2 * C_DIM // N_CHIPS

    def body(dh_ref, h_ref, g_ref, s_ref, cdf_ref, sv_ref, win_hbm, lng_ref, lnb_ref, ws_ref, wout_hbm,
             dx_ref, dxb_ref, ds_ref, dg_ref, dbin_ref, dlng_ref, dlnb_ref, dws_ref, dbs_ref,
             win_v, wout_v, bdt, dws_acc, dbs_acc, dvn, sem):
        i = pl.program_id(0)

        loads = _Weights(sem)
        k_out = loads.start(wout_hbm, wout_v)
        k_in = [loads.start(win_hbm.at[j], win_v.at[j]) for j in range(N_CHIPS)]

        @pl.when(i == 0)
        def _():
            mask_t = _triu_mask()
            bdt[...] = jnp.zeros(bdt.shape, BF16)
            for g in range(C_GROUPS):
                wt = jnp.where(mask_t, ws_ref[g].T, 0.0).astype(BF16)
                bdt[g, 0:CHUNK, 0:CHUNK] = wt
                bdt[g, CHUNK:PAIR, CHUNK:PAIR] = wt
            dws_acc[...] = jnp.zeros(dws_acc.shape, F32)
            dbs_acc[...] = jnp.zeros(dbs_acc.shape, F32)
            dg_ref[...] = jnp.zeros(dg_ref.shape, F32)
            dbin_ref[...] = jnp.zeros(dbin_ref.shape, F32)
            dlng_ref[...] = jnp.zeros(dlng_ref.shape, F32)
            dlnb_ref[...] = jnp.zeros(dlnb_ref.shape, F32)

        dhv = dh_ref[...]
        loads.ready(k_out)
        dy = _dot_nt(dhv.astype(BF16), wout_v[...])
        sf = s_ref[...].astype(F32)
        cdf = cdf_ref[...].astype(F32)
        pdf = jnp.exp(-0.5 * sf * sf) * 0.3989422804014327
        zz = sf * cdf
        dgelu = cdf + sf * pdf
        u, v = zz[:, 0:C_DIM], zz[:, C_DIM:2 * C_DIM]
        xhat, rs = _ln_stats(v)
        lng = lng_ref[...]
        vn = (xhat * lng + lnb_ref[...]).astype(BF16)
        du = dy * sv_ref[...].astype(F32)
        dsv = dy * u
        dsvb = dsv.astype(BF16)
        for g in range(C_GROUPS):
            cols = slice(g * CHUNK, (g + 1) * CHUNK)
            for r0 in range(0, tm, PAIR):
                blk = dsvb[r0:r0 + PAIR, cols]
                dvn[r0:r0 + PAIR, cols] = _dot(bdt[g], blk)
                dws_acc[g] += _dot_nt(blk, vn[r0:r0 + PAIR, cols])
                dbs_acc[g] += dsv[r0:r0 + CHUNK, cols] + dsv[r0 + CHUNK:r0 + PAIR, cols]
        dv, dlng, dlnb = _ln_bwd(dvn[...], xhat, rs, lng)
        dlng_ref[...] += dlng
        dlnb_ref[...] += dlnb
        ds = jnp.concatenate([du, dv], axis=1) * dgelu
        dbin_ref[...] += jnp.sum(ds, axis=0, keepdims=True)
        dsb = ds.astype(BF16)
        ds_ref[...] = dsb
        dn = jnp.zeros((tm, D_MODEL), F32)
        for j in range(N_CHIPS):
            loads.ready(k_in[j])
            dn = dn + _dot_nt(dsb[:, j * cs:(j + 1) * cs], win_v[j])
        xv = h_ref[...]
        g = g_ref[...]
        _, rstd = _rms_fwd(xv, g)
        dx, dg = _rms_bwd(dn, xv, rstd, g)
        dx_ref[...] = dhv + dx
        dxb_ref[...] = (dhv + dx).astype(BF16)
        dg_ref[...] += dg

        @pl.when(i == nt - 1)
        def _():
            mask = _tril_mask()
            for g in range(C_GROUPS):
                full = dws_acc[g]
                dws_ref[g] = jnp.where(mask, full[0:CHUNK, 0:CHUNK] + full[CHUNK:PAIR, CHUNK:PAIR], 0.0)
                dbs_ref[g:g + 1, :] = jnp.sum(dbs_acc[g].T, axis=0, keepdims=True)

    row = lambda cols: jax.ShapeDtypeStruct((1, cols), F32)
    return _pallas(
        body, [dh, h, norm_g, s, cdf, sv, w_in, ln_g, ln_b, w_s, w_out], name="bwd_odd", grid=(nt,),
        in_specs=[_row_spec(tm, D_MODEL), _row_spec(tm, D_MODEL), _full_spec((1, D_MODEL)), _row_spec(tm, 2 * C_DIM),
                  _row_spec(tm, 2 * C_DIM), _row_spec(tm, C_DIM), ANY, _full_spec((1, C_DIM)), _full_spec((1, C_DIM)),
                  _full_spec((C_GROUPS, CHUNK, CHUNK)), ANY],
        out_specs=[_row_spec(tm, D_MODEL), _row_spec(tm, D_MODEL), _row_spec(tm, 2 * C_DIM), _full_spec((1, D_MODEL)),
                   _full_spec((1, 2 * C_DIM)),
                   _full_spec((1, C_DIM)), _full_spec((1, C_DIM)), _full_spec((C_GROUPS, CHUNK, CHUNK)),
                   _full_spec((C_GROUPS, CHUNK))],
        out_shape=[jax.ShapeDtypeStruct((tokens, D_MODEL), F32), jax.ShapeDtypeStruct((tokens, D_MODEL), BF16),
                   jax.ShapeDtypeStruct((tokens, 2 * C_DIM), BF16),
                   row(D_MODEL), row(2 * C_DIM), row(C_DIM), row(C_DIM),
                   jax.ShapeDtypeStruct((C_GROUPS, CHUNK, CHUNK), F32), jax.ShapeDtypeStruct((C_GROUPS, CHUNK), F32)],
        scratch_shapes=[pltpu.VMEM((N_CHIPS, D_MODEL, cs), BF16), pltpu.VMEM((C_DIM, D_MODEL), BF16),
                        pltpu.VMEM((C_GROUPS, PAIR, PAIR), BF16), pltpu.VMEM((C_GROUPS, PAIR, PAIR), F32),
                        pltpu.VMEM((C_GROUPS, CHUNK, CHUNK), F32), pltpu.VMEM((tm, C_DIM), F32),
                        pltpu.SemaphoreType.DMA((N_LOADS,))],
        vmem_mib=56, riders=riders)


def _bwd_even(dh, x, norm_g, z, a2, cv, w_in, conv_a_w, ln_g, ln_b, conv_b_w, w_out, *, tm, seq, riders=()):
    tokens = x.shape[0]
    nt, tps = tokens // tm, seq // tm
    ws = IN_EVEN // N_CHIPS

    def body(dh_ref, x_ref, g_ref, z_ref, a2_ref, cv_ref, win_hbm, caw_ref, lng_ref, lnb_ref, cbw_ref, wout_hbm,
             dx_ref, dz_ref, dg_ref, dcaw_ref, dcab_ref, dlng_ref, dlnb_ref, dcbw_ref,
             win_v, wout_v, ea, eb, a1s, da1s, dw_acc, sem):
        i = pl.program_id(0)

        loads = _Weights(sem)
        k_out = loads.start(wout_hbm, wout_v)
        k_in = [loads.start(win_hbm.at[j], win_v.at[j]) for j in range(N_CHIPS)]

        @pl.when(i == 0)
        def _():
            dw_acc[...] = jnp.zeros(dw_acc.shape, F32)
            for ref in (dg_ref, dcab_ref, dlng_ref, dlnb_ref, dcbw_ref):
                ref[...] = jnp.zeros(ref.shape, F32)

        dhv = dh_ref[...]
        loads.ready(k_out)
        dmix = _dot_nt(dhv.astype(BF16), wout_v[...])
        da4, dbo = dmix[:, 0:A_DIM], dmix[:, A_DIM:A_DIM + B_DIM]
        zf = z_ref[...].astype(F32)
        a_val, a_gate = zf[:, 0:A_DIM], zf[:, A_DIM:2 * A_DIM]
        b_gate, c_gate, b_val = zf[:, 1024:1536], zf[:, 1536:2048], zf[:, 2048:2560]

        xhat, rs = _ln_stats(a2_ref[...])
        lng = lng_ref[...]
        a3 = xhat * lng + lnb_ref[...]
        sg = jax.nn.sigmoid(a3)
        da3 = da4 * (sg * (1.0 + a3 * (1.0 - sg)))
        da2, dlng, dlnb = _ln_bwd(da3, xhat, rs, lng)
        dlng_ref[...] += dlng
        dlnb_ref[...] += dlnb
        dcab_ref[...] += jnp.sum(da2, axis=0, keepdims=True)

        last = ((nt - 1 - i) % tps) == tps - 1
        dcv = dbo * b_gate

        @pl.when(last)
        def _():
            ea[0, tm:tm + A_HALO, :] = jnp.zeros((A_HALO, A_DIM), F32)
            eb[tm:tm + B_HALO, :] = jnp.zeros((B_HALO, B_DIM), F32)

        @pl.when(jnp.logical_not(last))
        def _():
            ea[0, tm:tm + A_HALO, :] = ea[0, 0:A_HALO, :]
            eb[tm:tm + B_HALO, :] = eb[0:B_HALO, :]

        ea[0, 0:tm, :] = da2
        eb[0:tm, :] = dcv
        _fill_shifted(ea, tm + A_HALO)
        sig = jax.nn.sigmoid(a_gate)
        a1s[...] = a_val * sig
        for r0 in range(0, tm, CONV_ROWS):
            a1c = a1s[r0:r0 + CONV_ROWS, :]
            acc = jnp.zeros((CONV_ROWS, A_DIM), F32)
            for j in range(A_CONV_WIDTH):
                k = A_CONV_WIDTH - 1 - j
                sl = _window(ea, r0 + j, CONV_ROWS)
                acc = acc + caw_ref[k:k + 1, :] * sl
                dw_acc[k] += sl * a1c
            da1s[r0:r0 + CONV_ROWS, :] = acc
        da1 = da1s[...]
        da_val = da1 * sig
        da_gate = da1 * a_val * (sig * (1.0 - sig))

        db_gate = dbo * cv_ref[...].astype(F32)
        cb = c_gate * b_val
        dcb = jnp.zeros((tm, B_DIM), F32)
        for j in range(B_CONV_WIDTH):
            k = B_CONV_WIDTH - 1 - j
            sl = eb[j:j + tm, :]
            dcb = dcb + cbw_ref[k:k + 1, :] * sl
            dcbw_ref[k:k + 1, :] += jnp.sum(sl * cb, axis=0, keepdims=True)
        dz = jnp.concatenate([da_val, da_gate, db_gate, dcb * b_val, dcb * c_gate], axis=1).astype(BF16)
        dz_ref[...] = dz
        dn = jnp.zeros((tm, D_MODEL), F32)
        for j in range(N_CHIPS):
            loads.ready(k_in[j])
            dn = dn + _dot_nt(dz[:, j * ws:(j + 1) * ws], win_v[j])
        xv = x_ref[...]
        g = g_ref[...]
        _, rstd = _rms_fwd(xv, g)
        dx, dg = _rms_bwd(dn, xv, rstd, g)
        dx_ref[...] = dhv + dx
        dg_ref[...] += dg

        @pl.when(i == nt - 1)
        def _():
            for k in range(A_CONV_WIDTH):
                dcaw_ref[k:k + 1, :] = jnp.sum(dw_acc[k], axis=0, keepdims=True)

    row = lambda cols: jax.ShapeDtypeStruct((1, cols), F32)
    rs_ = functools.partial(_row_spec, rev_nt=nt)
    return _pallas(
        body, [dh, x, norm_g, z, a2, cv, w_in, conv_a_w, ln_g, ln_b, conv_b_w, w_out], name="bwd_even", grid=(nt,),
        in_specs=[rs_(tm, D_MODEL), rs_(tm, D_MODEL), _full_spec((1, D_MODEL)), rs_(tm, IN_EVEN), rs_(tm, A_DIM),
                  rs_(tm, B_DIM), ANY, _full_spec((A_CONV_WIDTH, A_DIM)), _full_spec((1, A_DIM)), _full_spec((1, A_DIM)),
                  _full_spec((B_CONV_WIDTH, B_DIM)), ANY],
        out_specs=[rs_(tm, D_MODEL), rs_(tm, IN_EVEN), _full_spec((1, D_MODEL)), _full_spec((A_CONV_WIDTH, A_DIM)),
                   _full_spec((1, A_DIM)), _full_spec((1, A_DIM)), _full_spec((1, A_DIM)), _full_spec((B_CONV_WIDTH, B_DIM))],
        out_shape=[jax.ShapeDtypeStruct((tokens, D_MODEL), F32), jax.ShapeDtypeStruct((tokens, IN_EVEN), BF16),
                   row(D_MODEL), jax.ShapeDtypeStruct((A_CONV_WIDTH, A_DIM), F32), row(A_DIM), row(A_DIM), row(A_DIM),
                   jax.ShapeDtypeStruct((B_CONV_WIDTH, B_DIM), F32)],
        scratch_shapes=[pltpu.VMEM((N_CHIPS, D_MODEL, ws), BF16), pltpu.VMEM((D_MODEL, D_MODEL), BF16),
                        pltpu.VMEM((SUBLANES, tm + A_HALO, A_DIM), F32), pltpu.VMEM((tm + B_HALO, B_DIM), F32),
                        pltpu.VMEM((tm, A_DIM), F32), pltpu.VMEM((tm, A_DIM), F32),
                        pltpu.VMEM((A_CONV_WIDTH, CONV_ROWS, A_DIM), F32), pltpu.SemaphoreType.DMA((N_LOADS,))],
        vmem_mib=56, riders=riders)


def _wgrad(a, b, name, *, col_shards, riders=()):
    tokens, m = a.shape
    n = b.shape[1]
    kc = 512
    if col_shards:
        bm, bn = m // 2, n // N_CHIPS
        grid = (2, N_CHIPS)
        out_spec = pl.BlockSpec((None, None, bm, bn), lambda i, j: (j, i, 0, 0))
    elif m // 8 >= MXU_ROWS:
        bm, bn = m // 8, n
        grid = (8, 1)
        out_spec = pl.BlockSpec((None, None, bm, bn), lambda i, j: (i // 2, i % 2, 0, 0))
    else:
        bm, bn = m // N_CHIPS, n
        grid = (N_CHIPS, 1)
        out_spec = pl.BlockSpec((None, 2, bm // 2, bn), lambda i, j: (i, 0, 0, 0))

    def body(a_ref, b_ref, o_ref):
        acc = jnp.zeros((bm, bn), F32)
        for k0 in range(0, tokens, kc):
            acc = acc + _dot_tn(a_ref[k0:k0 + kc, :].astype(BF16), b_ref[k0:k0 + kc, :].astype(BF16))
        if len(o_ref.shape) == 3:
            o_ref[0] = acc[0:bm // 2]
            o_ref[1] = acc[bm // 2:bm]
        else:
            o_ref[...] = acc

    out_rows = m // 2 if col_shards else m // 8
    outs, routs = _pallas(
        body, [a, b], name=name, grid=grid,
        in_specs=[pl.BlockSpec((tokens, bm), lambda i, j: (0, i)), pl.BlockSpec((tokens, bn), lambda i, j: (0, j))],
        out_specs=[out_spec], out_shape=[jax.ShapeDtypeStruct((N_CHIPS, 2, out_rows, bn), F32)],
        vmem_mib=56, riders=riders)
    return outs[0], routs


class _GradReduce:
    def __init__(self, name, grad):
        self.name, self.grad = name, grad
        self.from_sibling = self.chip_sum = self.from_chips = self.full = None

    def pair_swap(self):
        return _PairSwap([self.grad])

    def took_pair(self, outs):
        self.chip_sum = _add_pair(self.grad, outs[0], f"pair_sum_{self.name}")

    def chip_swap(self):
        return _ChipSwap([self.chip_sum])

    def took_chips(self, outs):
        self.full = _add_chips(self.chip_sum, outs[0], f"chip_sum_{self.name}")

    def pair_share(self):
        return _PairShare([self.full])

    def took_share(self, outs):
        self.full = outs[0]

    def reduced(self):
        return jnp.reshape(self.full, (2 * self.full.shape[1], self.full.shape[2]))


def _forward_backward(x2, tgt2, gathered, staged, conv_a_w, conv_b_w, od_norm, od_bias, od_lng, od_lnb,
                      ev_norm_g, ev_conv_a_b, ev_ln_a_g, ev_ln_a_b, od_w_s, od_b_s, mlp_norm_g, final_norm_g,
                      *, tm, seq, distributed=True):
    d = x2.shape[1]
    w = dict(gathered)
    b_s_rows = jnp.broadcast_to(od_b_s[0][:, :, None], (C_GROUPS, CHUNK, CHUNK))

    def ride(*names):
        return [_Gather([staged[nm] for nm in names])] if distributed else []

    def land(routs, *names):
        if distributed:
            for nm, buf in zip(names, routs[0]):
                w[nm] = buf

    def as_cols(buf):
        return jnp.reshape(buf, (N_CHIPS, 2 * buf.shape[2], buf.shape[3]))

    def as_rows(buf):
        return jnp.reshape(buf, (8 * buf.shape[2], buf.shape[3]))

    (h1, n0, z, a2, cv, mix), routs = _fwd_even(
        x2, ev_norm_g, as_cols(w["ev_in"]), conv_a_w, ev_conv_a_b, ev_ln_a_g, ev_ln_a_b, conv_b_w, as_rows(w["ev_out"]),
        tm=tm, seq=seq, riders=ride("w1_0", "w2_0"))
    land(routs, "w1_0", "w2_0")
    (h2, n1, p0, q0), routs = _fwd_mlp(h1, mlp_norm_g[0:1], as_cols(w["w1_0"]), as_cols(w["w2_0"]), 0, tm=tm,
                                       riders=ride("od_in", "od_out", "w1_1"))
    land(routs, "od_in", "od_out", "w1_1")
    (h3, n2, s, cdf, sv, y), routs = _fwd_odd(h2, od_norm, as_cols(w["od_in"]), od_bias, od_lng, od_lnb, od_w_s[0], b_s_rows,
                                         as_rows(w["od_out"]), tm=tm, riders=ride("w2_1"))
    land(routs, "w2_1")
    (h4, n3, p1, q1), _ = _fwd_mlp(h3, mlp_norm_g[1:2], as_cols(w["w1_1"]), as_cols(w["w2_1"]), 1, tm=tm)
    loss_part, dh4, dh4b, d_final_g = _loss_head(h4, jnp.reshape(final_norm_g, (1, d)), tgt2, tm=tm)

    red = {}

    def swap(*names):
        return [red[nm].pair_swap() for nm in names] if distributed else []

    def chips(*names):
        return [red[nm].chip_swap() for nm in names] if distributed else []

    def share(*names):
        return [red[nm].pair_share() for nm in names] if distributed else []

    def took(routs, *steps):
        if distributed:
            for (nm, what), outs in zip(steps, routs):
                getattr(red[nm], what)(outs)

    g, _ = _wgrad(q1, dh4b, "wgrad_w2_1", col_shards=False)
    red["w2_1"] = _GradReduce("w2_1", g)
    (dh3, dh3b, dp1, d_mlp_g1), routs = _bwd_mlp(dh4, h3, mlp_norm_g[1:2], p1, as_cols(w["w1_1"]), as_cols(w["w2_1"]), 1, tm=tm,
                                           riders=swap("w2_1"))
    took(routs, ("w2_1", "took_pair"))
    g, _ = _wgrad(n3, dp1, "wgrad_w1_1", col_shards=True)
    red["w1_1"] = _GradReduce("w1_1", g)
    g, routs = _wgrad(y, dh3b, "wgrad_od_out", col_shards=False, riders=swap("w1_1"))
    red["od_out"] = _GradReduce("od_out", g)
    took(routs, ("w1_1", "took_pair"))
    (dh2, dh2b, ds, d_od_norm, d_od_bin, d_od_lng, d_od_lnb, d_ws, d_bs), routs = _bwd_odd(
        dh3, h2, od_norm, s, cdf, sv, as_cols(w["od_in"]), od_lng, od_lnb, od_w_s[0], as_rows(w["od_out"]), tm=tm,
        riders=chips("w2_1") + swap("od_out"))
    took(routs, ("w2_1", "took_chips"), ("od_out", "took_pair"))
    g, routs = _wgrad(n2, ds, "wgrad_od_in", col_shards=True, riders=share("w2_1"))
    red["od_in"] = _GradReduce("od_in", g)
    took(routs, ("w2_1", "took_share"))
    early = {"loss": loss_part, "od_w_s": d_ws, "od_b_s": d_bs, "mlp_norm_g1": d_mlp_g1, "final_norm_g": d_final_g,
             "od_norm_g": d_od_norm, "od_b_in": d_od_bin, "od_ln_v_g": d_od_lng, "od_ln_v_b": d_od_lnb}
    share_early = [_ShareAll(list(early.values()))] if distributed else []
    g, routs = _wgrad(q0, dh2b, "wgrad_w2_0", col_shards=False, riders=swap("od_in") + share_early)
    red["w2_0"] = _GradReduce("w2_0", g)
    took(routs, ("od_in", "took_pair"))
    landed_early = routs[1] if distributed else []
    (dh1, dh1b, dp0, d_mlp_g0), routs = _bwd_mlp(dh2, h1, mlp_norm_g[0:1], p0, as_cols(w["w1_0"]), as_cols(w["w2_0"]), 0, tm=tm,
                                           riders=chips("w1_1") + chips("od_out") + chips("od_in") + swap("w2_0"))
    took(routs, ("w1_1", "took_chips"), ("od_out", "took_chips"), ("od_in", "took_chips"), ("w2_0", "took_pair"))
    g, routs = _wgrad(n1, dp0, "wgrad_w1_0", col_shards=True, riders=share("w1_1") + share("od_out") + share("od_in"))
    red["w1_0"] = _GradReduce("w1_0", g)
    took(routs, ("w1_1", "took_share"), ("od_out", "took_share"), ("od_in", "took_share"))
    g, routs = _wgrad(mix, dh1b, "wgrad_ev_out", col_shards=False, riders=swap("w1_0"))
    red["ev_out"] = _GradReduce("ev_out", g)
    took(routs, ("w1_0", "took_pair"))

    (dx, dz, d_ev_norm, d_caw, d_cab, d_ev_lng, d_ev_lnb, d_cbw), routs = _bwd_even(
        dh1, x2, ev_norm_g, z, a2, cv, as_cols(w["ev_in"]), conv_a_w, ev_ln_a_g, ev_ln_a_b, conv_b_w, as_rows(w["ev_out"]),
        tm=tm, seq=seq, riders=chips("w2_0") + chips("w1_0") + swap("ev_out"))
    took(routs, ("w2_0", "took_chips"), ("w1_0", "took_chips"), ("ev_out", "took_pair"))
    late = {"mlp_norm_g0": d_mlp_g0, "ev_norm_g": d_ev_norm, "ev_conv_a_b": d_cab, "ev_ln_a_g": d_ev_lng,
            "ev_ln_a_b": d_ev_lnb, "ev_conv_a_w": d_caw, "ev_conv_b_w": d_cbw}
    share_late = [_ShareAll(list(late.values()))] if distributed else []
    g, routs2 = _wgrad(n0, dz, "wgrad_ev_in", col_shards=True,
                       riders=chips("ev_out") + share("w2_0") + share("w1_0") + share_late)
    red["ev_in"] = _GradReduce("ev_in", g)
    took(routs2, ("ev_out", "took_chips"), ("w2_0", "took_share"), ("w1_0", "took_share"))
    own = {**early, **late}
    landed = dict(zip(own.keys(), landed_early + routs2[3])) if distributed else None
    return dx, red, own, landed


def _rows128(a):
    rows = jnp.reshape(a, (-1, LANES))
    pad = (-rows.shape[0]) % SUBLANES
    return jnp.pad(rows, ((0, pad), (0, 0))) if pad else rows


def _pack(arrays):
    return jnp.concatenate([_rows128(a) for a in arrays], axis=0)


def _unpack(buf, shapes):
    out, r0 = [], 0
    for shp in shapes:
        size = 1
        for dim in shp:
            size *= dim
        nr = size // LANES
        out.append(jnp.reshape(buf[r0:r0 + nr], shp))
        r0 += nr + (-nr) % SUBLANES
    return out


def kernel(x, ev_norm_g, ev_w_in, ev_conv_a_w, ev_conv_a_b, ev_ln_a_g, ev_ln_a_b, ev_conv_b_w, ev_w_out, od_norm_g, od_w_in, od_b_in, od_ln_v_g, od_ln_v_b, od_w_s, od_b_s, od_w_out, mlp_norm_g, mlp_w1, mlp_w2, final_norm_g, loss_target, m_ev_norm_g, m_ev_w_in, m_ev_conv_a_w, m_ev_conv_a_b, m_ev_ln_a_g, m_ev_ln_a_b, m_ev_conv_b_w, m_ev_w_out, m_od_norm_g, m_od_w_in, m_od_b_in, m_od_ln_v_g, m_od_ln_v_b, m_od_w_s, m_od_b_s, m_od_w_out, m_mlp_norm_g, m_mlp_w1, m_mlp_w2, m_final_norm_g, v_ev_norm_g, v_ev_w_in, v_ev_conv_a_w, v_ev_conv_a_b, v_ev_ln_a_g, v_ev_ln_a_b, v_ev_conv_b_w, v_ev_w_out, v_od_norm_g, v_od_w_in, v_od_b_in, v_od_ln_v_g, v_od_ln_v_b, v_od_w_s, v_od_b_s, v_od_w_out, v_mlp_norm_g, v_mlp_w1, v_mlp_w2, v_final_norm_g):
    tm = TOKEN_TILE
    batch, seq, d = x.shape
    tokens = batch * seq
    x2 = jnp.reshape(x, (tokens, d))
    tgt2 = jnp.reshape(loss_target, (tokens, d))
    chip = 2 * lax.axis_index("x") + lax.axis_index("y")

    small_shapes = [(A_CONV_WIDTH, LANES), (B_CONV_WIDTH, LANES), (256,), (512,), (256,), (256,)]
    small_shard = _pack([ev_conv_a_w[0], ev_conv_b_w[0], od_norm_g[0], od_b_in[0], od_ln_v_g[0], od_ln_v_b[0]])
    small_shard = jnp.pad(small_shard, ((0, (-small_shard.shape[0]) % (2 * SUBLANES)), (0, 0)))
    first = [_place_shard(ev_w_in, 0, BF16, "place_ev_w_in"), _place_shard(ev_w_out, 0, BF16, "place_ev_w_out"),
             _place_shard(small_shard[None], 0, F32, "place_small")]
    staged = {
        "w1_0": _place_shard(mlp_w1, 0, BF16, "place_w1_0"), "w2_0": _place_shard(mlp_w2, 0, BF16, "place_w2_0"),
        "od_in": _place_shard(od_w_in, 0, BF16, "place_od_w_in"), "od_out": _place_shard(od_w_out, 0, BF16, "place_od_w_out"),
        "w1_1": _place_shard(mlp_w1, 1, BF16, "place_w1_1"), "w2_1": _place_shard(mlp_w2, 1, BF16, "place_w2_1"),
    }
    (g_ev_in, g_ev_out, g_small), = _exchange([_Gather(first)], "gather_first")
    small_all = jnp.reshape(g_small, (N_CHIPS, -1, LANES))
    per_chip = [_unpack(small_all[q], small_shapes) for q in range(N_CHIPS)]
    conv_a_w = jnp.concatenate([pc[0] for pc in per_chip], axis=1)
    conv_b_w = jnp.concatenate([pc[1] for pc in per_chip], axis=1)
    od_norm = jnp.concatenate([pc[2] for pc in per_chip])[None, :]
    od_bias = jnp.concatenate([pc[3] for pc in per_chip])[None, :]
    od_lng = jnp.concatenate([pc[4] for pc in per_chip])[None, :]
    od_lnb = jnp.concatenate([pc[5] for pc in per_chip])[None, :]

    dx, red, own, landed = _forward_backward(
        x2, tgt2, {"ev_in": g_ev_in, "ev_out": g_ev_out}, staged, conv_a_w, conv_b_w, od_norm, od_bias, od_lng, od_lnb,
        ev_norm_g, ev_conv_a_b, ev_ln_a_g, ev_ln_a_b, od_w_s, od_b_s, mlp_norm_g, final_norm_g, tm=tm, seq=seq)

    routs = _exchange([red["ev_in"].pair_swap(), red["ev_out"].pair_share()], "reduce_tail_1")
    red["ev_in"].took_pair(routs[0])
    red["ev_out"].took_share(routs[1])
    routs = _exchange([red["ev_in"].chip_swap()], "reduce_tail_2")
    red["ev_in"].took_chips(routs[0])
    routs = _exchange([red["ev_in"].pair_share()], "reduce_tail_3")
    red["ev_in"].took_share(routs[0])

    given = {"ev_norm_g": (ev_norm_g, m_ev_norm_g, v_ev_norm_g), "ev_conv_a_b": (ev_conv_a_b, m_ev_conv_a_b, v_ev_conv_a_b),
             "ev_ln_a_g": (ev_ln_a_g, m_ev_ln_a_g, v_ev_ln_a_g), "ev_ln_a_b": (ev_ln_a_b, m_ev_ln_a_b, v_ev_ln_a_b),
             "od_w_s": (od_w_s, m_od_w_s, v_od_w_s), "od_b_s": (od_b_s, m_od_b_s, v_od_b_s),
             "mlp_norm_g": (mlp_norm_g, m_mlp_norm_g, v_mlp_norm_g), "final_norm_g": (final_norm_g, m_final_norm_g, v_final_norm_g),
             "ev_conv_a_w": (ev_conv_a_w, m_ev_conv_a_w, v_ev_conv_a_w), "ev_conv_b_w": (ev_conv_b_w, m_ev_conv_b_w, v_ev_conv_b_w),
             "od_norm_g": (od_norm_g, m_od_norm_g, v_od_norm_g), "od_b_in": (od_b_in, m_od_b_in, v_od_b_in),
             "od_ln_v_g": (od_ln_v_g, m_od_ln_v_g, v_od_ln_v_g), "od_ln_v_b": (od_ln_v_b, m_od_ln_v_b, v_od_ln_v_b)}
    shaped = {nm: tuple(jnp.reshape(a, shape) for a in given[nm]) for nm, shape, _, _ in SMALL_WEIGHTS}
    loss11, small_upd = _small_update(own, landed, shaped)
    loss = loss11[0, 0]
    upd = {nm: [jnp.reshape(o, given[nm][0].shape) for o in outs] for nm, outs in small_upd.items()}

    def big_update(wt, m, v, names, call):
        grads = [red[nm].reduced() for nm in names]
        shp3 = (len(grads),) + grads[0].shape
        outs, _ = _adamw(jnp.reshape(wt, shp3), jnp.reshape(m, shp3), jnp.reshape(v, shp3), grads, call)
        return [jnp.reshape(o, wt.shape) for o in outs], None

    upd["mlp_w2"], _ = big_update(mlp_w2, m_mlp_w2, v_mlp_w2, ["w2_0", "w2_1"], "adamw_mlp_w2")
    upd["mlp_w1"], _ = big_update(mlp_w1, m_mlp_w1, v_mlp_w1, ["w1_0", "w1_1"], "adamw_mlp_w1")
    upd["ev_w_in"], _ = big_update(ev_w_in, m_ev_w_in, v_ev_w_in, ["ev_in"], "adamw_ev_w_in")
    upd["ev_w_out"], _ = big_update(ev_w_out, m_ev_w_out, v_ev_w_out, ["ev_out"], "adamw_ev_w_out")
    upd["od_w_in"], _ = big_update(od_w_in, m_od_w_in, v_od_w_in, ["od_in"], "adamw_od_w_in")
    upd["od_w_out"], _ = big_update(od_w_out, m_od_w_out, v_od_w_out, ["od_out"], "adamw_od_w_out")

    order = ["ev_norm_g", "ev_w_in", "ev_conv_a_w", "ev_conv_a_b", "ev_ln_a_g", "ev_ln_a_b", "ev_conv_b_w", "ev_w_out",
             "od_norm_g", "od_w_in", "od_b_in", "od_ln_v_g", "od_ln_v_b", "od_w_s", "od_b_s", "od_w_out", "mlp_norm_g",
             "mlp_w1", "mlp_w2", "final_norm_g"]
    grad_x = jnp.reshape(dx, x.shape)
    return (loss, grad_x, *[upd[nm][0] for nm in order], *[upd[nm][1] for nm in order],
            *[upd[nm][2] for nm in order], *[upd[nm][3] for nm in order])
```

```python
import functools

import jax
import jax.numpy as jnp
from jax import lax
from jax.experimental import pallas as pl
from jax.experimental.pallas import tpu as pltpu

F32 = jnp.float32
BF16 = jnp.bfloat16

D_MODEL = 1024
A_DIM = 512
B_DIM = 512
IN_EVEN = 2 * A_DIM + 3 * B_DIM
A_CONV_WIDTH = 31
B_CONV_WIDTH = 3
CHUNK = 128
C_GROUPS = 8
C_DIM = 1024
D_FF = 4096
RMS_EPS = 1e-6
LN_EPS = 1e-5
ADAM_LR = 0.001
ADAM_B1 = 0.9
ADAM_B2 = 0.999
ADAM_EPS = 1e-08
ADAM_WD = 0.01
ADAM_STEP = 10

N_CHIPS = 4
N_DEV = 8
TOKEN_TILE = 512
A_HALO = 32
B_HALO = 8
CONV_ROWS = 16
PAIR = 2 * CHUNK
LANES = 128
SUBLANES = 8
MXU_ROWS = 256
MIB = 1024 * 1024
MESH = pl.DeviceIdType.MESH
ANY = pl.BlockSpec(memory_space=pl.ANY)


def _dot(a, b):
    return lax.dot_general(a, b, (((1,), (0,)), ((), ())), preferred_element_type=F32)


def _dot_nt(a, b):
    return lax.dot_general(a, b, (((1,), (1,)), ((), ())), preferred_element_type=F32)


def _dot_tn(a, b):
    return lax.dot_general(a, b, (((0,), (0,)), ((), ())), preferred_element_type=F32)


def _params(vmem_mib, n_axes=1):
    return pltpu.CompilerParams(dimension_semantics=("arbitrary",) * n_axes, vmem_limit_bytes=vmem_mib * MIB)


def _row_spec(tm, cols, rev_nt=None):
    if rev_nt is None:
        return pl.BlockSpec((tm, cols), lambda i: (i, 0))
    return pl.BlockSpec((tm, cols), lambda i: (rev_nt - 1 - i, 0))


def _full_spec(shape):
    nd = len(shape)
    return pl.BlockSpec(shape, lambda i: (0,) * nd)


def _block_rows(rows, cap=512):
    best = SUBLANES
    for br in range(SUBLANES, min(rows, cap) + 1, SUBLANES):
        if rows % br == 0:
            best = br
    return best


N_LOADS = 2


def _load_weights(pairs, sems):
    @pl.when(pl.program_id(0) == 0)
    def _():
        copies = [pltpu.make_async_copy(src, dst, sems.at[k]) for k, (src, dst) in enumerate(pairs)]
        for cp in copies:
            cp.start()
        for cp in copies:
            cp.wait()


def _rms_fwd(x, g):
    rstd = lax.rsqrt(jnp.mean(x * x, axis=-1, keepdims=True) + RMS_EPS)
    return x * rstd * g, rstd


def _rms_bwd(dn, x, rstd, g):
    a = dn * g
    xh = x * rstd
    dx = rstd * (a - xh * jnp.mean(a * xh, axis=-1, keepdims=True))
    dg = jnp.sum(dn * xh, axis=0, keepdims=True)
    return dx, dg


def _ln_stats(v):
    mu = jnp.mean(v, axis=-1, keepdims=True)
    xc = v - mu
    rs = lax.rsqrt(jnp.mean(xc * xc, axis=-1, keepdims=True) + LN_EPS)
    return xc * rs, rs


def _ln_bwd(dy, xhat, rs, g):
    dxh = dy * g
    dv = rs * (dxh - jnp.mean(dxh, axis=-1, keepdims=True) - xhat * jnp.mean(dxh * xhat, axis=-1, keepdims=True))
    return dv, jnp.sum(dy * xhat, axis=0, keepdims=True), jnp.sum(dy, axis=0, keepdims=True)


def _gelu_cdf(s):
    return 0.5 * (1.0 + lax.erf(s * 0.7071067811865476))


def _mesh_pos():
    return lax.axis_index("x"), lax.axis_index("y"), lax.axis_index("c")


def _other_chips(x, y):
    return [(1 - x, y), (x, 1 - y), (1 - x, 1 - y)]


def _remote(src, dst, send_sem, recv_sem, to):
    return pltpu.make_async_remote_copy(src_ref=src, dst_ref=dst, send_sem=send_sem, recv_sem=recv_sem,
                                        device_id=to, device_id_type=MESH)


def _like(arrays):
    return [jax.ShapeDtypeStruct(a.shape, a.dtype) for a in arrays]


class _Gather:
    def __init__(self, bufs):
        self.ins = list(bufs)
        self.out_shapes = _like(bufs)
        self.aliases = {t: t for t in range(len(bufs))}
        self.n_sems = 6 * len(bufs)

    def _ici(self, ins, outs, send, recv, t, k, chip, mine, c):
        return _remote(ins[t].at[mine, c], outs[t].at[mine, c], send.at[6 * t + k], recv.at[6 * t + k], (*chip, c))

    def start(self, ins, outs, send, recv):
        x, y, c = _mesh_pos()
        for t in range(len(ins)):
            for k, chip in enumerate(_other_chips(x, y)):
                self._ici(ins, outs, send, recv, t, k, chip, 2 * x + y, c).start()

    def finish(self, ins, outs, send, recv):
        x, y, c = _mesh_pos()
        me, sibling = (x, y, c), (x, y, 1 - c)
        chips = _other_chips(x, y)
        passed = []
        for t in range(len(ins)):
            for k, chip in enumerate(chips):
                blk = outs[t].at[2 * chip[0] + chip[1], c]
                _remote(blk, blk, send.at[6 * t + k], recv.at[6 * t + k], me).wait_recv()
                cp = _remote(blk, blk, send.at[6 * t + 3 + k], recv.at[6 * t + 3 + k], sibling)
                cp.start()
                passed.append(cp)
        for t in range(len(ins)):
            for k, chip in enumerate(chips):
                blk = outs[t].at[2 * chip[0] + chip[1], 1 - c]
                _remote(blk, blk, send.at[6 * t + 3 + k], recv.at[6 * t + 3 + k], me).wait_recv()
        for t in range(len(ins)):
            for k, chip in enumerate(chips):
                self._ici(ins, outs, send, recv, t, k, chip, 2 * x + y, c).wait_send()
        for cp in passed:
            cp.wait_send()


class _PairSwap:
    def __init__(self, grads):
        self.ins = list(grads)
        self.out_shapes = [jax.ShapeDtypeStruct((g.shape[0],) + g.shape[2:], g.dtype) for g in grads]
        self.aliases = {}
        self.n_sems = len(grads)

    def _copies(self, ins, outs, send, recv):
        x, y, c = _mesh_pos()
        return [_remote(ins[t].at[:, 1 - c], outs[t], send.at[t], recv.at[t], (x, y, 1 - c)) for t in range(len(ins))]

    def start(self, ins, outs, send, recv):
        for cp in self._copies(ins, outs, send, recv):
            cp.start()

    def finish(self, ins, outs, send, recv):
        for cp in self._copies(ins, outs, send, recv):
            cp.wait()


class _ChipSwap:
    def __init__(self, parts):
        self.ins = list(parts)
        self.out_shapes = [jax.ShapeDtypeStruct((3,) + p.shape[1:], p.dtype) for p in parts]
        self.aliases = {}
        self.n_sems = 3 * len(parts)

    def _copies(self, ins, outs, send, recv):
        x, y, c = _mesh_pos()
        return [_remote(ins[t].at[2 * chip[0] + chip[1]], outs[t].at[k], send.at[3 * t + k], recv.at[3 * t + k], (*chip, c))
                for t in range(len(ins)) for k, chip in enumerate(_other_chips(x, y))]

    def start(self, ins, outs, send, recv):
        for cp in self._copies(ins, outs, send, recv):
            cp.start()

    def finish(self, ins, outs, send, recv):
        for cp in self._copies(ins, outs, send, recv):
            cp.wait()


class _PairShare:
    def __init__(self, fulls):
        self.ins = list(fulls)
        self.out_shapes = _like(fulls)
        self.aliases = {t: t for t in range(len(fulls))}
        self.n_sems = len(fulls)

    def _copies(self, ins, outs, send, recv):
        x, y, c = _mesh_pos()
        return [_remote(ins[t].at[c], outs[t].at[c], send.at[t], recv.at[t], (x, y, 1 - c)) for t in range(len(ins))]

    def start(self, ins, outs, send, recv):
        for cp in self._copies(ins, outs, send, recv):
            cp.start()

    def finish(self, ins, outs, send, recv):
        for cp in self._copies(ins, outs, send, recv):
            cp.wait()


class _ShareAll:
    def __init__(self, arrays):
        self.ins = list(arrays)
        self.out_shapes = [jax.ShapeDtypeStruct((N_DEV,) + a.shape, a.dtype) for a in arrays]
        self.aliases = {}
        self.n_sems = (N_DEV - 1) * len(arrays)

    def _peers(self):
        x, y, c = _mesh_pos()
        flips = [((r >> 2) & 1, (r >> 1) & 1, r & 1) for r in range(1, N_DEV)]
        return (x, y, c), [(x ^ fx, y ^ fy, c ^ fc) for fx, fy, fc in flips]

    def _sends(self, ins, outs, send, recv):
        (x, y, c), peers = self._peers()
        mine = 4 * x + 2 * y + c
        return [_remote(ins[a], outs[a].at[mine], send.at[7 * a + r], recv.at[7 * a + r], peer)
                for a in range(len(ins)) for r, peer in enumerate(peers)]

    def start(self, ins, outs, send, recv):
        for cp in self._sends(ins, outs, send, recv):
            cp.start()

    def finish(self, ins, outs, send, recv):
        (x, y, c), peers = self._peers()
        for a in range(len(ins)):
            for r, (px, py, pc) in enumerate(peers):
                blk = outs[a].at[4 * px + 2 * py + pc]
                _remote(blk, blk, send.at[7 * a + r], recv.at[7 * a + r], (x, y, c)).wait_recv()
        for cp in self._sends(ins, outs, send, recv):
            cp.wait_send()


def _pallas(body, operands, *, name, grid, in_specs, out_specs, out_shape, scratch_shapes=(), vmem_mib=32, riders=()):
    in_specs, out_specs, out_shape, scratch_shapes = list(in_specs), list(out_specs), list(out_shape), list(scratch_shapes)
    if not riders:
        outs = pl.pallas_call(body, name=name, grid=grid, in_specs=in_specs, out_specs=out_specs, out_shape=out_shape,
                              scratch_shapes=scratch_shapes, compiler_params=_params(vmem_mib, len(grid)))(*operands)
        return list(outs), []
    n_in, n_out, n_scr = len(in_specs), len(out_specs), len(scratch_shapes)
    r_in = [len(r.ins) for r in riders]
    r_out = [len(r.out_shapes) for r in riders]
    steps = 1
    for g in grid:
        steps *= g

    def wrapped(*refs):
        refs = list(refs)
        ins, refs = refs[:n_in], refs[n_in:]
        rins = []
        for k in r_in:
            rins.append(refs[:k])
            refs = refs[k:]
        outs, refs = refs[:n_out], refs[n_out:]
        routs = []
        for k in r_out:
            routs.append(refs[:k])
            refs = refs[k:]
        scr, sems = refs[:n_scr], refs[n_scr:]
        step = 0
        for ax, g in enumerate(grid):
            step = step * g + pl.program_id(ax)

        def each(what):
            for j, r in enumerate(riders):
                getattr(r, what)(rins[j], routs[j], sems[2 * j], sems[2 * j + 1])

        if grid:
            pl.when(step == 0)(lambda: each("start"))
        else:
            each("start")
        body(*ins, *outs, *scr)
        if grid:
            pl.when(step == steps - 1)(lambda: each("finish"))
        else:
            each("finish")

    aliases, off_in, off_out = {}, n_in, n_out
    for r, ki, ko in zip(riders, r_in, r_out):
        for i, o in r.aliases.items():
            aliases[off_in + i] = off_out + o
        off_in, off_out = off_in + ki, off_out + ko
    sems = []
    for r in riders:
        sems += [pltpu.SemaphoreType.DMA((r.n_sems,)), pltpu.SemaphoreType.DMA((r.n_sems,))]
    res = pl.pallas_call(
        wrapped, name=name, grid=grid,
        in_specs=in_specs + [ANY] * sum(r_in), out_specs=out_specs + [ANY] * sum(r_out),
        out_shape=out_shape + [s for r in riders for s in r.out_shapes],
        scratch_shapes=scratch_shapes + sems, input_output_aliases=aliases,
        compiler_params=pltpu.CompilerParams(dimension_semantics=("arbitrary",) * len(grid),
                                             vmem_limit_bytes=vmem_mib * MIB, has_side_effects=True),
    )(*operands, *[a for r in riders for a in r.ins])
    res = list(res)
    outs, res = res[:n_out], res[n_out:]
    routs = []
    for k in r_out:
        routs.append(res[:k])
        res = res[k:]
    return outs, routs


def _exchange(riders, name):
    return _pallas(lambda: None, [], name=name, grid=(), in_specs=[], out_specs=[], out_shape=[], riders=riders)[1]


def _in_hbm(a):
    return pltpu.with_memory_space_constraint(a, pltpu.HBM)


def _place_shard(w, layer, dtype, name):
    _, rows, cols = w.shape
    half = rows // 2
    br = _block_rows(half)
    nb = half // br
    mine = 2 * lax.axis_index("x") + lax.axis_index("y")

    def body(q_ref, w_ref, o_ref):
        o_ref[...] = w_ref[...].astype(dtype)

    return pl.pallas_call(
        body, name=name,
        grid_spec=pltpu.PrefetchScalarGridSpec(
            num_scalar_prefetch=1, grid=(2, nb),
            in_specs=[pl.BlockSpec((None, br, cols), lambda h, i, q: (layer, h * nb + i, 0))],
            out_specs=pl.BlockSpec((None, None, br, cols), lambda h, i, q: (q[0], h, i, 0))),
        out_shape=jax.ShapeDtypeStruct((N_CHIPS, 2, half, cols), dtype),
        compiler_params=_params(16, 2),
    )(jnp.reshape(mine, (1,)).astype(jnp.int32), w)


def _add_pair(g, recv, name):
    _, _, r, cdim = g.shape
    br = _block_rows(r, 256)
    c = lax.axis_index("c")

    def body(c_ref, g_ref, r_ref, o_ref):
        o_ref[...] = (g_ref[...] + r_ref[...]).astype(BF16)

    return pl.pallas_call(
        body, name=name,
        grid_spec=pltpu.PrefetchScalarGridSpec(
            num_scalar_prefetch=1, grid=(N_CHIPS, r // br),
            in_specs=[pl.BlockSpec((None, None, br, cdim), lambda q, i, c_ref: (q, c_ref[0], i, 0)),
                      pl.BlockSpec((None, br, cdim), lambda q, i, c_ref: (q, i, 0))],
            out_specs=pl.BlockSpec((None, br, cdim), lambda q, i, c_ref: (q, i, 0))),
        out_shape=jax.ShapeDtypeStruct((N_CHIPS, r, cdim), BF16),
        compiler_params=_params(16, 2),
    )(jnp.reshape(c, (1,)).astype(jnp.int32), _in_hbm(g), _in_hbm(recv))


def _add_chips(own, recv, name):
    _, r, cdim = own.shape
    br = _block_rows(r, 256)
    x, y, c = _mesh_pos()

    def body(pos_ref, own_ref, r_ref, o_ref):
        acc = own_ref[...].astype(F32)
        for k in range(3):
            acc = acc + r_ref[k].astype(F32)
        o_ref[...] = acc

    return pl.pallas_call(
        body, name=name,
        grid_spec=pltpu.PrefetchScalarGridSpec(
            num_scalar_prefetch=1, grid=(r // br,),
            in_specs=[pl.BlockSpec((None, br, cdim), lambda i, pos: (pos[0], i, 0)),
                      pl.BlockSpec((3, br, cdim), lambda i, pos: (0, i, 0))],
            out_specs=pl.BlockSpec((None, br, cdim), lambda i, pos: (pos[1], i, 0))),
        out_shape=jax.ShapeDtypeStruct((2, r, cdim), F32),
        compiler_params=_params(16, 1),
    )(jnp.stack([2 * x + y, c]).astype(jnp.int32), _in_hbm(own), _in_hbm(recv))


def _adam_math(w, m, v, g):
    c1 = 1.0 / (1.0 - ADAM_B1 ** ADAM_STEP)
    c2 = 1.0 / (1.0 - ADAM_B2 ** ADAM_STEP)
    m_new = ADAM_B1 * m + (1.0 - ADAM_B1) * g
    v_new = ADAM_B2 * v + (1.0 - ADAM_B2) * (g * g)
    return -ADAM_LR * ((m_new * c1) / (jnp.sqrt(v_new * c2) + ADAM_EPS) + ADAM_WD * w), m_new, v_new


SMALL_WEIGHTS = [
    ("ev_norm_g", (1, D_MODEL), ["ev_norm_g"], None), ("ev_conv_a_b", (1, A_DIM), ["ev_conv_a_b"], None),
    ("ev_ln_a_g", (1, A_DIM), ["ev_ln_a_g"], None), ("ev_ln_a_b", (1, A_DIM), ["ev_ln_a_b"], None),
    ("od_w_s", (C_GROUPS, CHUNK, CHUNK), ["od_w_s_lo", "od_w_s_hi"], None), ("od_b_s", (C_GROUPS, CHUNK), ["od_b_s"], None),
    ("mlp_norm_g", (2, D_MODEL), ["mlp_norm_g0", "mlp_norm_g1"], None), ("final_norm_g", (1, D_MODEL), ["final_norm_g"], None),
    ("ev_conv_a_w", (A_CONV_WIDTH, A_DIM // N_CHIPS), ["ev_conv_a_w"], A_DIM // N_CHIPS),
    ("ev_conv_b_w", (B_CONV_WIDTH, B_DIM // N_CHIPS), ["ev_conv_b_w"], B_DIM // N_CHIPS),
    ("od_norm_g", (1, D_MODEL // N_CHIPS), ["od_norm_g"], D_MODEL // N_CHIPS),
    ("od_b_in", (1, 2 * C_DIM // N_CHIPS), ["od_b_in"], 2 * C_DIM // N_CHIPS),
    ("od_ln_v_g", (1, C_DIM // N_CHIPS), ["od_ln_v_g"], C_DIM // N_CHIPS),
    ("od_ln_v_b", (1, C_DIM // N_CHIPS), ["od_ln_v_b"], C_DIM // N_CHIPS),
]


def _small_update(own, landed, weights):
    names = list(own.keys())
    n_g, n_w = len(names), len(SMALL_WEIGHTS)

    def body(*refs):
        refs = list(refs)
        own_refs = dict(zip(names, refs[:n_g]))
        land_refs = dict(zip(names, refs[n_g:2 * n_g]))
        wmv = [refs[2 * n_g + 3 * i:2 * n_g + 3 * i + 3] for i in range(n_w)]
        o0 = 2 * n_g + 3 * n_w
        loss_ref = refs[o0]
        outs = [refs[o0 + 1 + 4 * i:o0 + 5 + 4 * i] for i in range(n_w)]
        acc = dict(zip(names, refs[o0 + 1 + 4 * n_w:]))
        x, y, c = _mesh_pos()
        mine, chip = 4 * x + 2 * y + c, 2 * x + y

        for nm in names:
            for d in range(N_DEV):
                def add(term, nm=nm, d=d):
                    acc[nm][...] = term if d == 0 else acc[nm][...] + term
                pl.when(mine == d)(lambda nm=nm, add=add: add(own_refs[nm][...]))
                pl.when(mine != d)(lambda nm=nm, d=d, add=add: add(land_refs[nm][d]))
        loss_ref[...] = acc["loss"][...]

        def update(i, rows, g):
            w_ref, m_ref, v_ref = wmv[i]
            delta, m_new, v_new = _adam_math(w_ref[rows], m_ref[rows], v_ref[rows], g)
            for ref, val in zip(outs[i], (g, delta, m_new, v_new)):
                ref[rows] = val

        for i, (_, shape, grads, per_chip) in enumerate(SMALL_WEIGHTS):
            for row, gname in enumerate(grads):
                per_grad = shape[0] // len(grads)
                rows = slice(row * per_grad, (row + 1) * per_grad)
                if per_chip is None:
                    update(i, rows, acc[gname][...])
                else:
                    for q in range(N_CHIPS):
                        pl.when(chip == q)(lambda i=i, rows=rows, gname=gname, q=q, per_chip=per_chip:
                                           update(i, rows, acc[gname][:, q * per_chip:(q + 1) * per_chip]))

    operands = [own[nm] for nm in names] + [landed[nm] for nm in names]
    for nm, _, _, _ in SMALL_WEIGHTS:
        operands += list(weights[nm])
    out_shape = [jax.ShapeDtypeStruct((1, 1), F32)]
    for _, shape, _, _ in SMALL_WEIGHTS:
        out_shape += [jax.ShapeDtypeStruct(shape, F32)] * 4
    res = pl.pallas_call(
        body, name="small_update", grid=(1,),
        in_specs=[_full_spec(a.shape) for a in operands], out_specs=[_full_spec(s.shape) for s in out_shape],
        out_shape=out_shape, scratch_shapes=[pltpu.VMEM(own[nm].shape, F32) for nm in names],
        compiler_params=_params(32, 1),
    )(*[_in_hbm(a) for a in operands])
    return res[0], {nm: res[1 + 4 * i:5 + 4 * i] for i, (nm, _, _, _) in enumerate(SMALL_WEIGHTS)}


def _adamw(w, m, v, grads, name, riders=()):
    layers, r, cdim = w.shape
    br = _block_rows(r, 256 if cdim > LANES else 1024)

    def body(*refs):
        w_ref, m_ref, v_ref = refs[:3]
        g_refs = refs[3:3 + layers]
        go_ref, d_ref, mo_ref, vo_ref = refs[3 + layers:]
        layer = pl.program_id(0)
        for l in range(layers):
            @pl.when(layer == l)
            def _(l=l):
                g = g_refs[l][...]
                go_ref[...] = g
                d_ref[...], mo_ref[...], vo_ref[...] = _adam_math(w_ref[...], m_ref[...], v_ref[...], g)

    spec3 = pl.BlockSpec((None, br, cdim), lambda l, i: (l, i, 0))
    spec2 = pl.BlockSpec((br, cdim), lambda l, i: (i, 0))
    out = jax.ShapeDtypeStruct((layers, r, cdim), F32)
    return _pallas(body, [w, m, v, *[_in_hbm(g) for g in grads]], name=name, grid=(layers, r // br),
                   in_specs=[spec3, spec3, spec3] + [spec2] * layers, out_specs=[spec3] * 4, out_shape=[out] * 4,
                   vmem_mib=32, riders=riders)


def _fill_shifted(buf, rows):
    for b in range(1, SUBLANES):
        buf[b, 0:rows - SUBLANES, :] = buf[0, b:b + rows - SUBLANES, :]


def _window(buf, start, size):
    return buf[start % SUBLANES, start - start % SUBLANES:start - start % SUBLANES + size, :]


def _conv31(src, w_ref, r0, base, init):
    acc = init
    for k in range(A_CONV_WIDTH):
        acc = acc + w_ref[k:k + 1, :] * _window(src, base + k + r0, CONV_ROWS)
    return acc


def _fwd_even(x, norm_g, w_in, conv_a_w, conv_a_b, ln_g, ln_b, conv_b_w, w_out, *, tm, seq, riders=()):
    tokens = x.shape[0]
    nt, tps = tokens // tm, seq // tm

    def body(x_ref, g_ref, win_hbm, caw_ref, cab_ref, lng_ref, lnb_ref, cbw_ref, wout_hbm,
             h_ref, n_ref, z_ref, a2_ref, cv_ref, mix_ref, win_v, wout_v, pa, pb, sem):
        i = pl.program_id(0)

        _load_weights([(win_hbm, win_v), (wout_hbm, wout_v)], sem)

        xv = x_ref[...]
        nf, _ = _rms_fwd(xv, g_ref[...])
        n = nf.astype(BF16)
        n_ref[...] = n
        z = jnp.concatenate([_dot(n, win_v[j]) for j in range(N_CHIPS)], axis=1)
        z_ref[...] = z.astype(BF16)
        a_val, a_gate = z[:, 0:A_DIM], z[:, A_DIM:2 * A_DIM]
        b_gate, c_gate, b_val = z[:, 1024:1536], z[:, 1536:2048], z[:, 2048:2560]

        first = (i % tps) == 0

        @pl.when(first)
        def _():
            pa[0, 0:A_HALO, :] = jnp.zeros((A_HALO, A_DIM), F32)
            pb[0:B_HALO, :] = jnp.zeros((B_HALO, B_DIM), F32)

        @pl.when(jnp.logical_not(first))
        def _():
            pa[0, 0:A_HALO, :] = pa[0, tm:tm + A_HALO, :]
            pb[0:B_HALO, :] = pb[tm:tm + B_HALO, :]

        pa[0, A_HALO:A_HALO + tm, :] = a_val * jax.nn.sigmoid(a_gate)
        pb[B_HALO:B_HALO + tm, :] = c_gate * b_val
        _fill_shifted(pa, A_HALO + tm)
        bias = jnp.broadcast_to(cab_ref[...], (CONV_ROWS, A_DIM))
        for r0 in range(0, tm, CONV_ROWS):
            a2_ref[r0:r0 + CONV_ROWS, :] = _conv31(pa, caw_ref, r0, A_HALO - (A_CONV_WIDTH - 1), bias)
        xhat, _ = _ln_stats(a2_ref[...])
        a3 = xhat * lng_ref[...] + lnb_ref[...]
        a4 = a3 * jax.nn.sigmoid(a3)
        cv = cbw_ref[0:1, :] * pb[B_HALO - 2:B_HALO - 2 + tm, :]
        cv = cv + cbw_ref[1:2, :] * pb[B_HALO - 1:B_HALO - 1 + tm, :]
        cv = cv + cbw_ref[2:3, :] * pb[B_HALO:B_HALO + tm, :]
        cv_ref[...] = cv.astype(BF16)
        mix = jnp.concatenate([a4, b_gate * cv], axis=1).astype(BF16)
        mix_ref[...] = mix
        h_ref[...] = xv + _dot(mix, wout_v[...])

    shp = lambda cols, dt: jax.ShapeDtypeStruct((tokens, cols), dt)
    return _pallas(
        body, [x, norm_g, w_in, conv_a_w, conv_a_b, ln_g, ln_b, conv_b_w, w_out], name="fwd_even", grid=(nt,),
        in_specs=[_row_spec(tm, D_MODEL), _full_spec((1, D_MODEL)), ANY, _full_spec((A_CONV_WIDTH, A_DIM)),
                  _full_spec((1, A_DIM)), _full_spec((1, A_DIM)), _full_spec((1, A_DIM)),
                  _full_spec((B_CONV_WIDTH, B_DIM)), ANY],
        out_specs=[_row_spec(tm, D_MODEL), _row_spec(tm, D_MODEL), _row_spec(tm, IN_EVEN), _row_spec(tm, A_DIM),
                   _row_spec(tm, B_DIM), _row_spec(tm, D_MODEL)],
        out_shape=[shp(D_MODEL, F32), shp(D_MODEL, BF16), shp(IN_EVEN, BF16), shp(A_DIM, F32), shp(B_DIM, BF16),
                   shp(D_MODEL, BF16)],
        scratch_shapes=[pltpu.VMEM((N_CHIPS, D_MODEL, IN_EVEN // N_CHIPS), BF16), pltpu.VMEM((D_MODEL, D_MODEL), BF16),
                        pltpu.VMEM((SUBLANES, A_HALO + tm, A_DIM), F32), pltpu.VMEM((B_HALO + tm, B_DIM), F32),
                        pltpu.SemaphoreType.DMA((N_LOADS,))],
        vmem_mib=56, riders=riders)


def _fwd_mlp(h, norm_g, w1, w2, layer, *, tm, riders=()):
    tokens = h.shape[0]
    nt = tokens // tm
    fs = D_FF // N_CHIPS

    def body(h_ref, g_ref, w1_hbm, w2_hbm, ho_ref, n_ref, p_ref, q_ref, w1_v, w2_v, sem):
        _load_weights([(w1_hbm, w1_v), (w2_hbm, w2_v)], sem)

        xv = h_ref[...]
        nf, _ = _rms_fwd(xv, g_ref[...])
        n = nf.astype(BF16)
        n_ref[...] = n
        acc = xv
        for j in range(N_CHIPS):
            p = _dot(n, w1_v[j])
            p_ref[:, j * fs:(j + 1) * fs] = p.astype(BF16)
            r = jnp.maximum(p, 0.0)
            q = (r * r).astype(BF16)
            q_ref[:, j * fs:(j + 1) * fs] = q
            acc = acc + _dot(q, w2_v[j])
        ho_ref[...] = acc

    shp = lambda cols, dt: jax.ShapeDtypeStruct((tokens, cols), dt)
    return _pallas(
        body, [h, norm_g, w1, w2], name=f"fwd_mlp{layer}", grid=(nt,),
        in_specs=[_row_spec(tm, D_MODEL), _full_spec((1, D_MODEL)), ANY, ANY],
        out_specs=[_row_spec(tm, D_MODEL), _row_spec(tm, D_MODEL), _row_spec(tm, D_FF), _row_spec(tm, D_FF)],
        out_shape=[shp(D_MODEL, F32), shp(D_MODEL, BF16), shp(D_FF, BF16), shp(D_FF, BF16)],
        scratch_shapes=[pltpu.VMEM((N_CHIPS, D_MODEL, fs), BF16), pltpu.VMEM((N_CHIPS, fs, D_MODEL), BF16),
                        pltpu.SemaphoreType.DMA((N_LOADS,))],
        vmem_mib=56, riders=riders)


def _tril_mask():
    row = lax.broadcasted_iota(jnp.int32, (CHUNK, CHUNK), 0)
    col = lax.broadcasted_iota(jnp.int32, (CHUNK, CHUNK), 1)
    return row >= col


def _triu_mask():
    row = lax.broadcasted_iota(jnp.int32, (CHUNK, CHUNK), 0)
    col = lax.broadcasted_iota(jnp.int32, (CHUNK, CHUNK), 1)
    return row <= col


def _fwd_odd(h, norm_g, w_in, b_in, ln_g, ln_b, w_s, b_s_rows, w_out, *, tm, riders=()):
    tokens = h.shape[0]
    nt = tokens // tm
    cs = 2 * C_DIM // N_CHIPS

    def body(h_ref, g_ref, win_hbm, bin_ref, lng_ref, lnb_ref, ws_ref, bs_ref, wout_hbm,
             ho_ref, n_ref, s_ref, cdf_ref, sv_ref, y_ref, win_v, wout_v, bd, sem):
        _load_weights([(win_hbm, win_v), (wout_hbm, wout_v)], sem)

        @pl.when(pl.program_id(0) == 0)
        def _():
            mask = _tril_mask()
            bd[...] = jnp.zeros(bd.shape, BF16)
            for g in range(C_GROUPS):
                w = jnp.where(mask, ws_ref[g], 0.0).astype(BF16)
                bd[g, 0:CHUNK, 0:CHUNK] = w
                bd[g, CHUNK:PAIR, CHUNK:PAIR] = w

        xv = h_ref[...]
        nf, _ = _rms_fwd(xv, g_ref[...])
        n = nf.astype(BF16)
        n_ref[...] = n
        s = jnp.concatenate([_dot(n, win_v[j]) for j in range(N_CHIPS)], axis=1) + bin_ref[...]
        s_ref[...] = s.astype(BF16)
        cdf = _gelu_cdf(s)
        cdf_ref[...] = cdf.astype(BF16)
        zz = s * cdf
        u, v = zz[:, 0:C_DIM], zz[:, C_DIM:2 * C_DIM]
        xhat, _ = _ln_stats(v)
        vn = (xhat * lng_ref[...] + lnb_ref[...]).astype(BF16)
        for g in range(C_GROUPS):
            cols = slice(g * CHUNK, (g + 1) * CHUNK)
            bias = jnp.concatenate([bs_ref[g], bs_ref[g]], axis=0)
            for r0 in range(0, tm, PAIR):
                sv = _dot(bd[g], vn[r0:r0 + PAIR, cols]) + bias
                sv_ref[r0:r0 + PAIR, cols] = sv.astype(BF16)
                y_ref[r0:r0 + PAIR, cols] = (u[r0:r0 + PAIR, cols] * sv).astype(BF16)
        ho_ref[...] = xv + _dot(y_ref[...], wout_v[...])

    shp = lambda cols, dt: jax.ShapeDtypeStruct((tokens, cols), dt)
    return _pallas(
        body, [h, norm_g, w_in, b_in, ln_g, ln_b, w_s, b_s_rows, w_out], name="fwd_odd", grid=(nt,),
        in_specs=[_row_spec(tm, D_MODEL), _full_spec((1, D_MODEL)), ANY, _full_spec((1, 2 * C_DIM)),
                  _full_spec((1, C_DIM)), _full_spec((1, C_DIM)), _full_spec((C_GROUPS, CHUNK, CHUNK)),
                  _full_spec((C_GROUPS, CHUNK, CHUNK)), ANY],
        out_specs=[_row_spec(tm, D_MODEL), _row_spec(tm, D_MODEL), _row_spec(tm, 2 * C_DIM), _row_spec(tm, 2 * C_DIM),
                   _row_spec(tm, C_DIM), _row_spec(tm, C_DIM)],
        out_shape=[shp(D_MODEL, F32), shp(D_MODEL, BF16), shp(2 * C_DIM, BF16), shp(2 * C_DIM, BF16), shp(C_DIM, BF16),
                   shp(C_DIM, BF16)],
        scratch_shapes=[pltpu.VMEM((N_CHIPS, D_MODEL, cs), BF16), pltpu.VMEM((C_DIM, D_MODEL), BF16),
                        pltpu.VMEM((C_GROUPS, PAIR, PAIR), BF16), pltpu.SemaphoreType.DMA((N_LOADS,))],
        vmem_mib=56, riders=riders)


def _loss_head(h, norm_g, target, *, tm):
    tokens = h.shape[0]
    nt = tokens // tm

    def body(h_ref, g_ref, t_ref, loss_ref, dh_ref, dhb_ref, dg_ref):
        @pl.when(pl.program_id(0) == 0)
        def _():
            loss_ref[...] = jnp.zeros((1, 1), F32)
            dg_ref[...] = jnp.zeros((1, D_MODEL), F32)

        xv = h_ref[...]
        g = g_ref[...]
        out, rstd = _rms_fwd(xv, g)
        err = out - t_ref[...]
        per_token = jnp.sum(err * err, axis=1, keepdims=True) * (1.0 / D_MODEL)
        loss_ref[...] += 0.5 * jnp.sum(per_token, axis=0, keepdims=True)
        dx, dg = _rms_bwd(err * (1.0 / D_MODEL), xv, rstd, g)
        dh_ref[...] = dx
        dhb_ref[...] = dx.astype(BF16)
        dg_ref[...] += dg

    return _pallas(
        body, [h, norm_g, target], name="loss_head", grid=(nt,),
        in_specs=[_row_spec(tm, D_MODEL), _full_spec((1, D_MODEL)), _row_spec(tm, D_MODEL)],
        out_specs=[_full_spec((1, 1)), _row_spec(tm, D_MODEL), _row_spec(tm, D_MODEL), _full_spec((1, D_MODEL))],
        out_shape=[jax.ShapeDtypeStruct((1, 1), F32), jax.ShapeDtypeStruct((tokens, D_MODEL), F32),
                   jax.ShapeDtypeStruct((tokens, D_MODEL), BF16), jax.ShapeDtypeStruct((1, D_MODEL), F32)],
        vmem_mib=32)[0]


def _bwd_mlp(dh, h, norm_g, p, w1, w2, layer, *, tm, riders=()):
    tokens = h.shape[0]
    nt = tokens // tm
    fs = D_FF // N_CHIPS

    def body(dh_ref, h_ref, g_ref, p_ref, w1_hbm, w2_hbm, dx_ref, dxb_ref, dp_ref, dg_ref, w1_v, w2_v, sem):
        @pl.when(pl.program_id(0) == 0)
        def _():
            dg_ref[...] = jnp.zeros((1, D_MODEL), F32)

        _load_weights([(w1_hbm, w1_v), (w2_hbm, w2_v)], sem)

        dhv = dh_ref[...]
        dhb = dhv.astype(BF16)
        dn = jnp.zeros((tm, D_MODEL), F32)
        for j in range(N_CHIPS):
            dq = _dot_nt(dhb, w2_v[j])
            r = jnp.maximum(p_ref[:, j * fs:(j + 1) * fs].astype(F32), 0.0)
            dp = ((2.0 * r) * dq).astype(BF16)
            dp_ref[:, j * fs:(j + 1) * fs] = dp
            dn = dn + _dot_nt(dp, w1_v[j])
        xv = h_ref[...]
        g = g_ref[...]
        _, rstd = _rms_fwd(xv, g)
        dx, dg = _rms_bwd(dn, xv, rstd, g)
        dx_ref[...] = dhv + dx
        dxb_ref[...] = (dhv + dx).astype(BF16)
        dg_ref[...] += dg

    return _pallas(
        body, [dh, h, norm_g, p, w1, w2], name=f"bwd_mlp{layer}", grid=(nt,),
        in_specs=[_row_spec(tm, D_MODEL), _row_spec(tm, D_MODEL), _full_spec((1, D_MODEL)), _row_spec(tm, D_FF), ANY, ANY],
        out_specs=[_row_spec(tm, D_MODEL), _row_spec(tm, D_MODEL), _row_spec(tm, D_FF), _full_spec((1, D_MODEL))],
        out_shape=[jax.ShapeDtypeStruct((tokens, D_MODEL), F32), jax.ShapeDtypeStruct((tokens, D_MODEL), BF16),
                   jax.ShapeDtypeStruct((tokens, D_FF), BF16), jax.ShapeDtypeStruct((1, D_MODEL), F32)],
        scratch_shapes=[pltpu.VMEM((N_CHIPS, D_MODEL, fs), BF16), pltpu.VMEM((N_CHIPS, fs, D_MODEL), BF16),
                        pltpu.SemaphoreType.DMA((N_LOADS,))],
        vmem_mib=56, riders=riders)


def _bwd_odd(dh, h, norm_g, s, cdf, sv, w_in, ln_g, ln_b, w_s, w_out, *, tm, riders=()):
    tokens = h.shape[0]
    nt = tokens // tm
    cs = 2 * C_DIM // N_CHIPS

    def body(dh_ref, h_ref, g_ref, s_ref, cdf_ref, sv_ref, win_hbm, lng_ref, lnb_ref, ws_ref, wout_hbm,
             dx_ref, dxb_ref, ds_ref, dg_ref, dbin_ref, dlng_ref, dlnb_ref, dws_ref, dbs_ref,
             win_v, wout_v, bdt, dws_acc, dbs_acc, dvn, sem):
        i = pl.program_id(0)

        _load_weights([(win_hbm, win_v), (wout_hbm, wout_v)], sem)

        @pl.when(i == 0)
        def _():
            mask_t = _triu_mask()
            bdt[...] = jnp.zeros(bdt.shape, BF16)
            for g in range(C_GROUPS):
                wt = jnp.where(mask_t, ws_ref[g].T, 0.0).astype(BF16)
                bdt[g, 0:CHUNK, 0:CHUNK] = wt
                bdt[g, CHUNK:PAIR, CHUNK:PAIR] = wt
            dws_acc[...] = jnp.zeros(dws_acc.shape, F32)
            dbs_acc[...] = jnp.zeros(dbs_acc.shape, F32)
            dg_ref[...] = jnp.zeros(dg_ref.shape, F32)
            dbin_ref[...] = jnp.zeros(dbin_ref.shape, F32)
            dlng_ref[...] = jnp.zeros(dlng_ref.shape, F32)
            dlnb_ref[...] = jnp.zeros(dlnb_ref.shape, F32)

        dhv = dh_ref[...]
        dy = _dot_nt(dhv.astype(BF16), wout_v[...])
        sf = s_ref[...].astype(F32)
        cdf = cdf_ref[...].astype(F32)
        pdf = jnp.exp(-0.5 * sf * sf) * 0.3989422804014327
        zz = sf * cdf
        dgelu = cdf + sf * pdf
        u, v = zz[:, 0:C_DIM], zz[:, C_DIM:2 * C_DIM]
        xhat, rs = _ln_stats(v)
        lng = lng_ref[...]
        vn = (xhat * lng + lnb_ref[...]).astype(BF16)
        du = dy * sv_ref[...].astype(F32)
        dsv = dy * u
        dsvb = dsv.astype(BF16)
        for g in range(C_GROUPS):
            cols = slice(g * CHUNK, (g + 1) * CHUNK)
            for r0 in range(0, tm, PAIR):
                blk = dsvb[r0:r0 + PAIR, cols]
                dvn[r0:r0 + PAIR, cols] = _dot(bdt[g], blk)
                dws_acc[g] += _dot_nt(blk, vn[r0:r0 + PAIR, cols])
                dbs_acc[g] += dsv[r0:r0 + CHUNK, cols] + dsv[r0 + CHUNK:r0 + PAIR, cols]
        dv, dlng, dlnb = _ln_bwd(dvn[...], xhat, rs, lng)
        dlng_ref[...] += dlng
        dlnb_ref[...] += dlnb
        ds = jnp.concatenate([du, dv], axis=1) * dgelu
        dbin_ref[...] += jnp.sum(ds, axis=0, keepdims=True)
        dsb = ds.astype(BF16)
        ds_ref[...] = dsb
        dn = jnp.zeros((tm, D_MODEL), F32)
        for j in range(N_CHIPS):
            dn = dn + _dot_nt(dsb[:, j * cs:(j + 1) * cs], win_v[j])
        xv = h_ref[...]
        g = g_ref[...]
        _, rstd = _rms_fwd(xv, g)
        dx, dg = _rms_bwd(dn, xv, rstd, g)
        dx_ref[...] = dhv + dx
        dxb_ref[...] = (dhv + dx).astype(BF16)
        dg_ref[...] += dg

        @pl.when(i == nt - 1)
        def _():
            mask = _tril_mask()
            for g in range(C_GROUPS):
                full = dws_acc[g]
                dws_ref[g] = jnp.where(mask, full[0:CHUNK, 0:CHUNK] + full[CHUNK:PAIR, CHUNK:PAIR], 0.0)
                dbs_ref[g:g + 1, :] = jnp.sum(dbs_acc[g].T, axis=0, keepdims=True)

    row = lambda cols: jax.ShapeDtypeStruct((1, cols), F32)
    return _pallas(
        body, [dh, h, norm_g, s, cdf, sv, w_in, ln_g, ln_b, w_s, w_out], name="bwd_odd", grid=(nt,),
        in_specs=[_row_spec(tm, D_MODEL), _row_spec(tm, D_MODEL), _full_spec((1, D_MODEL)), _row_spec(tm, 2 * C_DIM),
                  _row_spec(tm, 2 * C_DIM), _row_spec(tm, C_DIM), ANY, _full_spec((1, C_DIM)), _full_spec((1, C_DIM)),
                  _full_spec((C_GROUPS, CHUNK, CHUNK)), ANY],
        out_specs=[_row_spec(tm, D_MODEL), _row_spec(tm, D_MODEL), _row_spec(tm, 2 * C_DIM), _full_spec((1, D_MODEL)),
                   _full_spec((1, 2 * C_DIM)),
                   _full_spec((1, C_DIM)), _full_spec((1, C_DIM)), _full_spec((C_GROUPS, CHUNK, CHUNK)),
                   _full_spec((C_GROUPS, CHUNK))],
        out_shape=[jax.ShapeDtypeStruct((tokens, D_MODEL), F32), jax.ShapeDtypeStruct((tokens, D_MODEL), BF16),
                   jax.ShapeDtypeStruct((tokens, 2 * C_DIM), BF16),
                   row(D_MODEL), row(2 * C_DIM), row(C_DIM), row(C_DIM),
                   jax.ShapeDtypeStruct((C_GROUPS, CHUNK, CHUNK), F32), jax.ShapeDtypeStruct((C_GROUPS, CHUNK), F32)],
        scratch_shapes=[pltpu.VMEM((N_CHIPS, D_MODEL, cs), BF16), pltpu.VMEM((C_DIM, D_MODEL), BF16),
                        pltpu.VMEM((C_GROUPS, PAIR, PAIR), BF16), pltpu.VMEM((C_GROUPS, PAIR, PAIR), F32),
                        pltpu.VMEM((C_GROUPS, CHUNK, CHUNK), F32), pltpu.VMEM((tm, C_DIM), F32),
                        pltpu.SemaphoreType.DMA((N_LOADS,))],
        vmem_mib=56, riders=riders)


def _bwd_even(dh, x, norm_g, z, a2, cv, w_in, conv_a_w, ln_g, ln_b, conv_b_w, w_out, *, tm, seq, riders=()):
    tokens = x.shape[0]
    nt, tps = tokens // tm, seq // tm
    ws = IN_EVEN // N_CHIPS

    def body(dh_ref, x_ref, g_ref, z_ref, a2_ref, cv_ref, win_hbm, caw_ref, lng_ref, lnb_ref, cbw_ref, wout_hbm,
             dx_ref, dz_ref, dg_ref, dcaw_ref, dcab_ref, dlng_ref, dlnb_ref, dcbw_ref,
             win_v, wout_v, ea, eb, a1s, da1s, dw_acc, sem):
        i = pl.program_id(0)

        _load_weights([(win_hbm, win_v), (wout_hbm, wout_v)], sem)

        @pl.when(i == 0)
        def _():
            dw_acc[...] = jnp.zeros(dw_acc.shape, F32)
            for ref in (dg_ref, dcab_ref, dlng_ref, dlnb_ref, dcbw_ref):
                ref[...] = jnp.zeros(ref.shape, F32)

        dhv = dh_ref[...]
        dmix = _dot_nt(dhv.astype(BF16), wout_v[...])
        da4, dbo = dmix[:, 0:A_DIM], dmix[:, A_DIM:A_DIM + B_DIM]
        zf = z_ref[...].astype(F32)
        a_val, a_gate = zf[:, 0:A_DIM], zf[:, A_DIM:2 * A_DIM]
        b_gate, c_gate, b_val = zf[:, 1024:1536], zf[:, 1536:2048], zf[:, 2048:2560]

        xhat, rs = _ln_stats(a2_ref[...])
        lng = lng_ref[...]
        a3 = xhat * lng + lnb_ref[...]
        sg = jax.nn.sigmoid(a3)
        da3 = da4 * (sg * (1.0 + a3 * (1.0 - sg)))
        da2, dlng, dlnb = _ln_bwd(da3, xhat, rs, lng)
        dlng_ref[...] += dlng
        dlnb_ref[...] += dlnb
        dcab_ref[...] += jnp.sum(da2, axis=0, keepdims=True)

        last = ((nt - 1 - i) % tps) == tps - 1
        dcv = dbo * b_gate

        @pl.when(last)
        def _():
            ea[0, tm:tm + A_HALO, :] = jnp.zeros((A_HALO, A_DIM), F32)
            eb[tm:tm + B_HALO, :] = jnp.zeros((B_HALO, B_DIM), F32)

        @pl.when(jnp.logical_not(last))
        def _():
            ea[0, tm:tm + A_HALO, :] = ea[0, 0:A_HALO, :]
            eb[tm:tm + B_HALO, :] = eb[0:B_HALO, :]

        ea[0, 0:tm, :] = da2
        eb[0:tm, :] = dcv
        _fill_shifted(ea, tm + A_HALO)
        sig = jax.nn.sigmoid(a_gate)
        a1s[...] = a_val * sig
        for r0 in range(0, tm, CONV_ROWS):
            a1c = a1s[r0:r0 + CONV_ROWS, :]
            acc = jnp.zeros((CONV_ROWS, A_DIM), F32)
            for j in range(A_CONV_WIDTH):
                k = A_CONV_WIDTH - 1 - j
                sl = _window(ea, r0 + j, CONV_ROWS)
                acc = acc + caw_ref[k:k + 1, :] * sl
                dw_acc[k] += sl * a1c
            da1s[r0:r0 + CONV_ROWS, :] = acc
        da1 = da1s[...]
        da_val = da1 * sig
        da_gate = da1 * a_val * (sig * (1.0 - sig))

        db_gate = dbo * cv_ref[...].astype(F32)
        cb = c_gate * b_val
        dcb = jnp.zeros((tm, B_DIM), F32)
        for j in range(B_CONV_WIDTH):
            k = B_CONV_WIDTH - 1 - j
            sl = eb[j:j + tm, :]
            dcb = dcb + cbw_ref[k:k + 1, :] * sl
            dcbw_ref[k:k + 1, :] += jnp.sum(sl * cb, axis=0, keepdims=True)
        dz = jnp.concatenate([da_val, da_gate, db_gate, dcb * b_val, dcb * c_gate], axis=1).astype(BF16)
        dz_ref[...] = dz
        dn = jnp.zeros((tm, D_MODEL), F32)
        for j in range(N_CHIPS):
            dn = dn + _dot_nt(dz[:, j * ws:(j + 1) * ws], win_v[j])
        xv = x_ref[...]
        g = g_ref[...]
        _, rstd = _rms_fwd(xv, g)
        dx, dg = _rms_bwd(dn, xv, rstd, g)
        dx_ref[...] = dhv + dx
        dg_ref[...] += dg

        @pl.when(i == nt - 1)
        def _():
            for k in range(A_CONV_WIDTH):
                dcaw_ref[k:k + 1, :] = jnp.sum(dw_acc[k], axis=0, keepdims=True)

    row = lambda cols: jax.ShapeDtypeStruct((1, cols), F32)
    rs_ = functools.partial(_row_spec, rev_nt=nt)
    return _pallas(
        body, [dh, x, norm_g, z, a2, cv, w_in, conv_a_w, ln_g, ln_b, conv_b_w, w_out], name="bwd_even", grid=(nt,),
        in_specs=[rs_(tm, D_MODEL), rs_(tm, D_MODEL), _full_spec((1, D_MODEL)), rs_(tm, IN_EVEN), rs_(tm, A_DIM),
                  rs_(tm, B_DIM), ANY, _full_spec((A_CONV_WIDTH, A_DIM)), _full_spec((1, A_DIM)), _full_spec((1, A_DIM)),
                  _full_spec((B_CONV_WIDTH, B_DIM)), ANY],
        out_specs=[rs_(tm, D_MODEL), rs_(tm, IN_EVEN), _full_spec((1, D_MODEL)), _full_spec((A_CONV_WIDTH, A_DIM)),
                   _full_spec((1, A_DIM)), _full_spec((1, A_DIM)), _full_spec((1, A_DIM)), _full_spec((B_CONV_WIDTH, B_DIM))],
        out_shape=[jax.ShapeDtypeStruct((tokens, D_MODEL), F32), jax.ShapeDtypeStruct((tokens, IN_EVEN), BF16),
                   row(D_MODEL), jax.ShapeDtypeStruct((A_CONV_WIDTH, A_DIM), F32), row(A_DIM), row(A_DIM), row(A_DIM),
                   jax.ShapeDtypeStruct((B_CONV_WIDTH, B_DIM), F32)],
        scratch_shapes=[pltpu.VMEM((N_CHIPS, D_MODEL, ws), BF16), pltpu.VMEM((D_MODEL, D_MODEL), BF16),
                        pltpu.VMEM((SUBLANES, tm + A_HALO, A_DIM), F32), pltpu.VMEM((tm + B_HALO, B_DIM), F32),
                        pltpu.VMEM((tm, A_DIM), F32), pltpu.VMEM((tm, A_DIM), F32),
                        pltpu.VMEM((A_CONV_WIDTH, CONV_ROWS, A_DIM), F32), pltpu.SemaphoreType.DMA((N_LOADS,))],
        vmem_mib=56, riders=riders)


def _wgrad(a, b, name, *, col_shards, riders=()):
    tokens, m = a.shape
    n = b.shape[1]
    kc = 512
    if col_shards:
        bm, bn = m // 2, n // N_CHIPS
        grid = (2, N_CHIPS)
        out_spec = pl.BlockSpec((None, None, bm, bn), lambda i, j: (j, i, 0, 0))
    elif m // 8 >= MXU_ROWS:
        bm, bn = m // 8, n
        grid = (8, 1)
        out_spec = pl.BlockSpec((None, None, bm, bn), lambda i, j: (i // 2, i % 2, 0, 0))
    else:
        bm, bn = m // N_CHIPS, n
        grid = (N_CHIPS, 1)
        out_spec = pl.BlockSpec((None, 2, bm // 2, bn), lambda i, j: (i, 0, 0, 0))

    def body(a_ref, b_ref, o_ref):
        acc = jnp.zeros((bm, bn), F32)
        for k0 in range(0, tokens, kc):
            acc = acc + _dot_tn(a_ref[k0:k0 + kc, :].astype(BF16), b_ref[k0:k0 + kc, :].astype(BF16))
        if len(o_ref.shape) == 3:
            o_ref[0] = acc[0:bm // 2]
            o_ref[1] = acc[bm // 2:bm]
        else:
            o_ref[...] = acc

    out_rows = m // 2 if col_shards else m // 8
    outs, routs = _pallas(
        body, [a, b], name=name, grid=grid,
        in_specs=[pl.BlockSpec((tokens, bm), lambda i, j: (0, i)), pl.BlockSpec((tokens, bn), lambda i, j: (0, j))],
        out_specs=[out_spec], out_shape=[jax.ShapeDtypeStruct((N_CHIPS, 2, out_rows, bn), F32)],
        vmem_mib=56, riders=riders)
    return outs[0], routs


class _GradReduce:
    def __init__(self, name, grad):
        self.name, self.grad = name, grad
        self.from_sibling = self.chip_sum = self.from_chips = self.full = None

    def pair_swap(self):
        return _PairSwap([self.grad])

    def took_pair(self, outs):
        self.chip_sum = _add_pair(self.grad, outs[0], f"pair_sum_{self.name}")

    def chip_swap(self):
        return _ChipSwap([self.chip_sum])

    def took_chips(self, outs):
        self.full = _add_chips(self.chip_sum, outs[0], f"chip_sum_{self.name}")

    def pair_share(self):
        return _PairShare([self.full])

    def took_share(self, outs):
        self.full = outs[0]

    def reduced(self):
        return jnp.reshape(self.full, (2 * self.full.shape[1], self.full.shape[2]))


def _forward_backward(x2, tgt2, gathered, staged, conv_a_w, conv_b_w, od_norm, od_bias, od_lng, od_lnb,
                      ev_norm_g, ev_conv_a_b, ev_ln_a_g, ev_ln_a_b, od_w_s, od_b_s, mlp_norm_g, final_norm_g,
                      *, tm, seq, distributed=True):
    d = x2.shape[1]
    w = dict(gathered)
    b_s_rows = jnp.broadcast_to(od_b_s[0][:, :, None], (C_GROUPS, CHUNK, CHUNK))

    def ride(*names):
        return [_Gather([staged[nm] for nm in names])] if distributed else []

    def land(routs, *names):
        if distributed:
            for nm, buf in zip(names, routs[0]):
                w[nm] = buf

    def as_cols(buf):
        return jnp.reshape(buf, (N_CHIPS, 2 * buf.shape[2], buf.shape[3]))

    def as_rows(buf):
        return jnp.reshape(buf, (8 * buf.shape[2], buf.shape[3]))

    (h1, n0, z, a2, cv, mix), routs = _fwd_even(
        x2, ev_norm_g, as_cols(w["ev_in"]), conv_a_w, ev_conv_a_b, ev_ln_a_g, ev_ln_a_b, conv_b_w, as_rows(w["ev_out"]),
        tm=tm, seq=seq, riders=ride("w1_0", "w2_0"))
    land(routs, "w1_0", "w2_0")
    (h2, n1, p0, q0), routs = _fwd_mlp(h1, mlp_norm_g[0:1], as_cols(w["w1_0"]), as_cols(w["w2_0"]), 0, tm=tm,
                                       riders=ride("od_in", "od_out", "w1_1"))
    land(routs, "od_in", "od_out", "w1_1")
    (h3, n2, s, cdf, sv, y), routs = _fwd_odd(h2, od_norm, as_cols(w["od_in"]), od_bias, od_lng, od_lnb, od_w_s[0], b_s_rows,
                                         as_rows(w["od_out"]), tm=tm, riders=ride("w2_1"))
    land(routs, "w2_1")
    (h4, n3, p1, q1), _ = _fwd_mlp(h3, mlp_norm_g[1:2], as_cols(w["w1_1"]), as_cols(w["w2_1"]), 1, tm=tm)
    loss_part, dh4, dh4b, d_final_g = _loss_head(h4, jnp.reshape(final_norm_g, (1, d)), tgt2, tm=tm)

    red = {}

    def swap(*names):
        return [red[nm].pair_swap() for nm in names] if distributed else []

    def chips(*names):
        return [red[nm].chip_swap() for nm in names] if distributed else []

    def share(*names):
        return [red[nm].pair_share() for nm in names] if distributed else []

    def took(routs, *steps):
        if distributed:
            for (nm, what), outs in zip(steps, routs):
                getattr(red[nm], what)(outs)

    g, _ = _wgrad(q1, dh4b, "wgrad_w2_1", col_shards=False)
    red["w2_1"] = _GradReduce("w2_1", g)
    (dh3, dh3b, dp1, d_mlp_g1), routs = _bwd_mlp(dh4, h3, mlp_norm_g[1:2], p1, as_cols(w["w1_1"]), as_cols(w["w2_1"]), 1, tm=tm,
                                           riders=swap("w2_1"))
    took(routs, ("w2_1", "took_pair"))
    g, _ = _wgrad(n3, dp1, "wgrad_w1_1", col_shards=True)
    red["w1_1"] = _GradReduce("w1_1", g)
    g, routs = _wgrad(y, dh3b, "wgrad_od_out", col_shards=False, riders=swap("w1_1"))
    red["od_out"] = _GradReduce("od_out", g)
    took(routs, ("w1_1", "took_pair"))
    (dh2, dh2b, ds, d_od_norm, d_od_bin, d_od_lng, d_od_lnb, d_ws, d_bs), routs = _bwd_odd(
        dh3, h2, od_norm, s, cdf, sv, as_cols(w["od_in"]), od_lng, od_lnb, od_w_s[0], as_rows(w["od_out"]), tm=tm,
        riders=chips("w2_1") + swap("od_out"))
    took(routs, ("w2_1", "took_chips"), ("od_out", "took_pair"))
    g, routs = _wgrad(n2, ds, "wgrad_od_in", col_shards=True, riders=share("w2_1"))
    red["od_in"] = _GradReduce("od_in", g)
    took(routs, ("w2_1", "took_share"))
    half_groups = C_GROUPS // 2
    early = {"loss": loss_part, "od_w_s_lo": d_ws[:half_groups], "od_b_s": d_bs, "mlp_norm_g1": d_mlp_g1, "final_norm_g": d_final_g,
             "od_norm_g": d_od_norm, "od_b_in": d_od_bin, "od_ln_v_g": d_od_lng, "od_ln_v_b": d_od_lnb}
    share_early = [_ShareAll(list(early.values()))] if distributed else []
    g, routs = _wgrad(q0, dh2b, "wgrad_w2_0", col_shards=False, riders=swap("od_in") + share_early)
    red["w2_0"] = _GradReduce("w2_0", g)
    took(routs, ("od_in", "took_pair"))
    landed_early = routs[1] if distributed else []
    (dh1, dh1b, dp0, d_mlp_g0), routs = _bwd_mlp(dh2, h1, mlp_norm_g[0:1], p0, as_cols(w["w1_0"]), as_cols(w["w2_0"]), 0, tm=tm,
                                           riders=chips("w1_1") + chips("od_out") + chips("od_in") + swap("w2_0"))
    took(routs, ("w1_1", "took_chips"), ("od_out", "took_chips"), ("od_in", "took_chips"), ("w2_0", "took_pair"))
    middle = {"od_w_s_hi": d_ws[half_groups:]}
    share_middle = [_ShareAll(list(middle.values()))] if distributed else []
    g, routs = _wgrad(n1, dp0, "wgrad_w1_0", col_shards=True,
                      riders=share("w1_1") + share("od_out") + share("od_in") + share_middle)
    red["w1_0"] = _GradReduce("w1_0", g)
    took(routs, ("w1_1", "took_share"), ("od_out", "took_share"), ("od_in", "took_share"))
    landed_middle = routs[3] if distributed else []
    g, routs = _wgrad(mix, dh1b, "wgrad_ev_out", col_shards=False, riders=swap("w1_0"))
    red["ev_out"] = _GradReduce("ev_out", g)
    took(routs, ("w1_0", "took_pair"))

    (dx, dz, d_ev_norm, d_caw, d_cab, d_ev_lng, d_ev_lnb, d_cbw), routs = _bwd_even(
        dh1, x2, ev_norm_g, z, a2, cv, as_cols(w["ev_in"]), conv_a_w, ev_ln_a_g, ev_ln_a_b, conv_b_w, as_rows(w["ev_out"]),
        tm=tm, seq=seq, riders=chips("w2_0") + chips("w1_0") + swap("ev_out"))
    took(routs, ("w2_0", "took_chips"), ("w1_0", "took_chips"), ("ev_out", "took_pair"))
    late = {"mlp_norm_g0": d_mlp_g0, "ev_norm_g": d_ev_norm, "ev_conv_a_b": d_cab, "ev_ln_a_g": d_ev_lng,
            "ev_ln_a_b": d_ev_lnb, "ev_conv_a_w": d_caw, "ev_conv_b_w": d_cbw}
    share_late = [_ShareAll(list(late.values()))] if distributed else []
    g, routs2 = _wgrad(n0, dz, "wgrad_ev_in", col_shards=True,
                       riders=chips("ev_out") + share("w2_0") + share("w1_0") + share_late)
    red["ev_in"] = _GradReduce("ev_in", g)
    took(routs2, ("ev_out", "took_chips"), ("w2_0", "took_share"), ("w1_0", "took_share"))
    own = {**early, **middle, **late}
    landed = dict(zip(own.keys(), landed_early + landed_middle + routs2[3])) if distributed else None
    return dx, red, own, landed


def _rows128(a):
    rows = jnp.reshape(a, (-1, LANES))
    pad = (-rows.shape[0]) % SUBLANES
    return jnp.pad(rows, ((0, pad), (0, 0))) if pad else rows


def _pack(arrays):
    return jnp.concatenate([_rows128(a) for a in arrays], axis=0)


def _unpack(buf, shapes):
    out, r0 = [], 0
    for shp in shapes:
        size = 1
        for dim in shp:
            size *= dim
        nr = size // LANES
        out.append(jnp.reshape(buf[r0:r0 + nr], shp))
        r0 += nr + (-nr) % SUBLANES
    return out


def kernel(x, ev_norm_g, ev_w_in, ev_conv_a_w, ev_conv_a_b, ev_ln_a_g, ev_ln_a_b, ev_conv_b_w, ev_w_out, od_norm_g, od_w_in, od_b_in, od_ln_v_g, od_ln_v_b, od_w_s, od_b_s, od_w_out, mlp_norm_g, mlp_w1, mlp_w2, final_norm_g, loss_target, m_ev_norm_g, m_ev_w_in, m_ev_conv_a_w, m_ev_conv_a_b, m_ev_ln_a_g, m_ev_ln_a_b, m_ev_conv_b_w, m_ev_w_out, m_od_norm_g, m_od_w_in, m_od_b_in, m_od_ln_v_g, m_od_ln_v_b, m_od_w_s, m_od_b_s, m_od_w_out, m_mlp_norm_g, m_mlp_w1, m_mlp_w2, m_final_norm_g, v_ev_norm_g, v_ev_w_in, v_ev_conv_a_w, v_ev_conv_a_b, v_ev_ln_a_g, v_ev_ln_a_b, v_ev_conv_b_w, v_ev_w_out, v_od_norm_g, v_od_w_in, v_od_b_in, v_od_ln_v_g, v_od_ln_v_b, v_od_w_s, v_od_b_s, v_od_w_out, v_mlp_norm_g, v_mlp_w1, v_mlp_w2, v_final_norm_g):
    tm = TOKEN_TILE
    batch, seq, d = x.shape
    tokens = batch * seq
    x2 = jnp.reshape(x, (tokens, d))
    tgt2 = jnp.reshape(loss_target, (tokens, d))
    chip = 2 * lax.axis_index("x") + lax.axis_index("y")

    small_shapes = [(A_CONV_WIDTH, LANES), (B_CONV_WIDTH, LANES), (256,), (512,), (256,), (256,)]
    small_shard = _pack([ev_conv_a_w[0], ev_conv_b_w[0], od_norm_g[0], od_b_in[0], od_ln_v_g[0], od_ln_v_b[0]])
    small_shard = jnp.pad(small_shard, ((0, (-small_shard.shape[0]) % (2 * SUBLANES)), (0, 0)))
    first = [_place_shard(ev_w_in, 0, BF16, "place_ev_w_in"), _place_shard(ev_w_out, 0, BF16, "place_ev_w_out"),
             _place_shard(small_shard[None], 0, F32, "place_small")]
    staged = {
        "w1_0": _place_shard(mlp_w1, 0, BF16, "place_w1_0"), "w2_0": _place_shard(mlp_w2, 0, BF16, "place_w2_0"),
        "od_in": _place_shard(od_w_in, 0, BF16, "place_od_w_in"), "od_out": _place_shard(od_w_out, 0, BF16, "place_od_w_out"),
        "w1_1": _place_shard(mlp_w1, 1, BF16, "place_w1_1"), "w2_1": _place_shard(mlp_w2, 1, BF16, "place_w2_1"),
    }
    (g_ev_in, g_ev_out, g_small), = _exchange([_Gather(first)], "gather_first")
    small_all = jnp.reshape(g_small, (N_CHIPS, -1, LANES))
    per_chip = [_unpack(small_all[q], small_shapes) for q in range(N_CHIPS)]
    conv_a_w = jnp.concatenate([pc[0] for pc in per_chip], axis=1)
    conv_b_w = jnp.concatenate([pc[1] for pc in per_chip], axis=1)
    od_norm = jnp.concatenate([pc[2] for pc in per_chip])[None, :]
    od_bias = jnp.concatenate([pc[3] for pc in per_chip])[None, :]
    od_lng = jnp.concatenate([pc[4] for pc in per_chip])[None, :]
    od_lnb = jnp.concatenate([pc[5] for pc in per_chip])[None, :]

    dx, red, own, landed = _forward_backward(
        x2, tgt2, {"ev_in": g_ev_in, "ev_out": g_ev_out}, staged, conv_a_w, conv_b_w, od_norm, od_bias, od_lng, od_lnb,
        ev_norm_g, ev_conv_a_b, ev_ln_a_g, ev_ln_a_b, od_w_s, od_b_s, mlp_norm_g, final_norm_g, tm=tm, seq=seq)

    routs = _exchange([red["ev_in"].pair_swap(), red["ev_out"].pair_share()], "reduce_tail_1")
    red["ev_in"].took_pair(routs[0])
    red["ev_out"].took_share(routs[1])
    routs = _exchange([red["ev_in"].chip_swap()], "reduce_tail_2")
    red["ev_in"].took_chips(routs[0])
    routs = _exchange([red["ev_in"].pair_share()], "reduce_tail_3")
    red["ev_in"].took_share(routs[0])

    given = {"ev_norm_g": (ev_norm_g, m_ev_norm_g, v_ev_norm_g), "ev_conv_a_b": (ev_conv_a_b, m_ev_conv_a_b, v_ev_conv_a_b),
             "ev_ln_a_g": (ev_ln_a_g, m_ev_ln_a_g, v_ev_ln_a_g), "ev_ln_a_b": (ev_ln_a_b, m_ev_ln_a_b, v_ev_ln_a_b),
             "od_w_s": (od_w_s, m_od_w_s, v_od_w_s), "od_b_s": (od_b_s, m_od_b_s, v_od_b_s),
             "mlp_norm_g": (mlp_norm_g, m_mlp_norm_g, v_mlp_norm_g), "final_norm_g": (final_norm_g, m_final_norm_g, v_final_norm_g),
             "ev_conv_a_w": (ev_conv_a_w, m_ev_conv_a_w, v_ev_conv_a_w), "ev_conv_b_w": (ev_conv_b_w, m_ev_conv_b_w, v_ev_conv_b_w),
             "od_norm_g": (od_norm_g, m_od_norm_g, v_od_norm_g), "od_b_in": (od_b_in, m_od_b_in, v_od_b_in),
             "od_ln_v_g": (od_ln_v_g, m_od_ln_v_g, v_od_ln_v_g), "od_ln_v_b": (od_ln_v_b, m_od_ln_v_b, v_od_ln_v_b)}
    shaped = {nm: tuple(jnp.reshape(a, shape) for a in given[nm]) for nm, shape, _, _ in SMALL_WEIGHTS}
    loss11, small_upd = _small_update(own, landed, shaped)
    loss = loss11[0, 0]
    upd = {nm: [jnp.reshape(o, given[nm][0].shape) for o in outs] for nm, outs in small_upd.items()}

    def big_update(wt, m, v, names, call):
        grads = [red[nm].reduced() for nm in names]
        shp3 = (len(grads),) + grads[0].shape
        outs, _ = _adamw(jnp.reshape(wt, shp3), jnp.reshape(m, shp3), jnp.reshape(v, shp3), grads, call)
        return [jnp.reshape(o, wt.shape) for o in outs], None

    upd["mlp_w2"], _ = big_update(mlp_w2, m_mlp_w2, v_mlp_w2, ["w2_0", "w2_1"], "adamw_mlp_w2")
    upd["mlp_w1"], _ = big_update(mlp_w1, m_mlp_w1, v_mlp_w1, ["w1_0", "w1_1"], "adamw_mlp_w1")
    upd["ev_w_in"], _ = big_update(ev_w_in, m_ev_w_in, v_ev_w_in, ["ev_in"], "adamw_ev_w_in")
    upd["ev_w_out"], _ = big_update(ev_w_out, m_ev_w_out, v_ev_w_out, ["ev_out"], "adamw_ev_w_out")
    upd["od_w_in"], _ = big_update(od_w_in, m_od_w_in, v_od_w_in, ["od_in"], "adamw_od_w_in")
    upd["od_w_out"], _ = big_update(od_w_out, m_od_w_out, v_od_w_out, ["od_out"], "adamw_od_w_out")

    order = ["ev_norm_g", "ev_w_in", "ev_conv_a_w", "ev_conv_a_b", "ev_ln_a_g", "ev_ln_a_b", "ev_conv_b_w", "ev_w_out",
             "od_norm_g", "od_w_in", "od_b_in", "od_ln_v_g", "od_ln_v_b", "od_w_s", "od_b_s", "od_w_out", "mlp_norm_g",
             "mlp_w1", "mlp_w2", "final_norm_g"]
    grad_x = jnp.reshape(dx, x.shape)
    return (loss, grad_x, *[upd[nm][0] for nm in order], *[upd[nm][1] for nm in order],
            *[upd[nm][2] for nm in order], *[upd[nm][3] for nm in order])
```

```python
import functools

import jax
import jax.numpy as jnp
from jax import lax
from jax.experimental import pallas as pl
from jax.experimental.pallas import tpu as pltpu

F32 = jnp.float32
BF16 = jnp.bfloat16

D_MODEL = 1024
A_DIM = 512
B_DIM = 512
IN_EVEN = 2 * A_DIM + 3 * B_DIM
A_CONV_WIDTH = 31
B_CONV_WIDTH = 3
CHUNK = 128
C_GROUPS = 8
C_DIM = 1024
D_FF = 4096
RMS_EPS = 1e-6
LN_EPS = 1e-5
ADAM_LR = 0.001
ADAM_B1 = 0.9
ADAM_B2 = 0.999
ADAM_EPS = 1e-08
ADAM_WD = 0.01
ADAM_STEP = 10

N_CHIPS = 4
N_DEV = 8
TOKEN_TILE = 512
A_HALO = 32
B_HALO = 8
CONV_ROWS = 16
PAIR = 2 * CHUNK
LANES = 128
SUBLANES = 8
MXU_ROWS = 256
MIB = 1024 * 1024
MESH = pl.DeviceIdType.MESH
ANY = pl.BlockSpec(memory_space=pl.ANY)


def _dot(a, b):
    return lax.dot_general(a, b, (((1,), (0,)), ((), ())), preferred_element_type=F32)


def _dot_nt(a, b):
    return lax.dot_general(a, b, (((1,), (1,)), ((), ())), preferred_element_type=F32)


def _dot_tn(a, b):
    return lax.dot_general(a, b, (((0,), (0,)), ((), ())), preferred_element_type=F32)


def _params(vmem_mib, n_axes=1):
    return pltpu.CompilerParams(dimension_semantics=("arbitrary",) * n_axes, vmem_limit_bytes=vmem_mib * MIB)


def _row_spec(tm, cols, rev_nt=None):
    if rev_nt is None:
        return pl.BlockSpec((tm, cols), lambda i: (i, 0))
    return pl.BlockSpec((tm, cols), lambda i: (rev_nt - 1 - i, 0))


def _full_spec(shape):
    nd = len(shape)
    return pl.BlockSpec(shape, lambda i: (0,) * nd)


def _block_rows(rows, cap=512):
    best = SUBLANES
    for br in range(SUBLANES, min(rows, cap) + 1, SUBLANES):
        if rows % br == 0:
            best = br
    return best


N_LOADS = 2


def _load_weights(pairs, sems):
    @pl.when(pl.program_id(0) == 0)
    def _():
        copies = [pltpu.make_async_copy(src, dst, sems.at[k]) for k, (src, dst) in enumerate(pairs)]
        for cp in copies:
            cp.start()
        for cp in copies:
            cp.wait()


def _rms_fwd(x, g):
    rstd = lax.rsqrt(jnp.mean(x * x, axis=-1, keepdims=True) + RMS_EPS)
    return x * rstd * g, rstd


def _rms_bwd(dn, x, rstd, g):
    a = dn * g
    xh = x * rstd
    dx = rstd * (a - xh * jnp.mean(a * xh, axis=-1, keepdims=True))
    dg = jnp.sum(dn * xh, axis=0, keepdims=True)
    return dx, dg


def _ln_stats(v):
    mu = jnp.mean(v, axis=-1, keepdims=True)
    xc = v - mu
    rs = lax.rsqrt(jnp.mean(xc * xc, axis=-1, keepdims=True) + LN_EPS)
    return xc * rs, rs


def _ln_bwd(dy, xhat, rs, g):
    dxh = dy * g
    dv = rs * (dxh - jnp.mean(dxh, axis=-1, keepdims=True) - xhat * jnp.mean(dxh * xhat, axis=-1, keepdims=True))
    return dv, jnp.sum(dy * xhat, axis=0, keepdims=True), jnp.sum(dy, axis=0, keepdims=True)


def _gelu_cdf(s):
    return 0.5 * (1.0 + lax.erf(s * 0.7071067811865476))


def _mesh_pos():
    return lax.axis_index("x"), lax.axis_index("y"), lax.axis_index("c")


def _other_chips(x, y):
    return [(1 - x, y), (x, 1 - y), (1 - x, 1 - y)]


def _remote(src, dst, send_sem, recv_sem, to):
    return pltpu.make_async_remote_copy(src_ref=src, dst_ref=dst, send_sem=send_sem, recv_sem=recv_sem,
                                        device_id=to, device_id_type=MESH)


def _like(arrays):
    return [jax.ShapeDtypeStruct(a.shape, a.dtype) for a in arrays]


class _Gather:
    def __init__(self, bufs):
        self.ins = list(bufs)
        self.out_shapes = _like(bufs)
        self.aliases = {t: t for t in range(len(bufs))}
        self.n_sems = 6 * len(bufs)

    def _ici(self, ins, outs, send, recv, t, k, chip, mine, c):
        return _remote(ins[t].at[mine, c], outs[t].at[mine, c], send.at[6 * t + k], recv.at[6 * t + k], (*chip, c))

    def start(self, ins, outs, send, recv):
        x, y, c = _mesh_pos()
        for t in range(len(ins)):
            for k, chip in enumerate(_other_chips(x, y)):
                self._ici(ins, outs, send, recv, t, k, chip, 2 * x + y, c).start()

    def finish(self, ins, outs, send, recv):
        x, y, c = _mesh_pos()
        me, sibling = (x, y, c), (x, y, 1 - c)
        chips = _other_chips(x, y)
        passed = []
        for t in range(len(ins)):
            for k, chip in enumerate(chips):
                blk = outs[t].at[2 * chip[0] + chip[1], c]
                _remote(blk, blk, send.at[6 * t + k], recv.at[6 * t + k], me).wait_recv()
                cp = _remote(blk, blk, send.at[6 * t + 3 + k], recv.at[6 * t + 3 + k], sibling)
                cp.start()
                passed.append(cp)
        for t in range(len(ins)):
            for k, chip in enumerate(chips):
                blk = outs[t].at[2 * chip[0] + chip[1], 1 - c]
                _remote(blk, blk, send.at[6 * t + 3 + k], recv.at[6 * t + 3 + k], me).wait_recv()
        for t in range(len(ins)):
            for k, chip in enumerate(chips):
                self._ici(ins, outs, send, recv, t, k, chip, 2 * x + y, c).wait_send()
        for cp in passed:
            cp.wait_send()


class _PairSwap:
    def __init__(self, grads):
        self.ins = list(grads)
        self.out_shapes = [jax.ShapeDtypeStruct((g.shape[0],) + g.shape[2:], g.dtype) for g in grads]
        self.aliases = {}
        self.n_sems = len(grads)

    def _copies(self, ins, outs, send, recv):
        x, y, c = _mesh_pos()
        return [_remote(ins[t].at[:, 1 - c], outs[t], send.at[t], recv.at[t], (x, y, 1 - c)) for t in range(len(ins))]

    def start(self, ins, outs, send, recv):
        for cp in self._copies(ins, outs, send, recv):
            cp.start()

    def finish(self, ins, outs, send, recv):
        for cp in self._copies(ins, outs, send, recv):
            cp.wait()


class _ChipSwap:
    def __init__(self, parts):
        self.ins = list(parts)
        self.out_shapes = [jax.ShapeDtypeStruct((3,) + p.shape[1:], p.dtype) for p in parts]
        self.aliases = {}
        self.n_sems = 3 * len(parts)

    def _copies(self, ins, outs, send, recv):
        x, y, c = _mesh_pos()
        return [_remote(ins[t].at[2 * chip[0] + chip[1]], outs[t].at[k], send.at[3 * t + k], recv.at[3 * t + k], (*chip, c))
                for t in range(len(ins)) for k, chip in enumerate(_other_chips(x, y))]

    def start(self, ins, outs, send, recv):
        for cp in self._copies(ins, outs, send, recv):
            cp.start()

    def finish(self, ins, outs, send, recv):
        for cp in self._copies(ins, outs, send, recv):
            cp.wait()


class _PairShare:
    def __init__(self, fulls):
        self.ins = list(fulls)
        self.out_shapes = _like(fulls)
        self.aliases = {t: t for t in range(len(fulls))}
        self.n_sems = len(fulls)

    def _copies(self, ins, outs, send, recv):
        x, y, c = _mesh_pos()
        return [_remote(ins[t].at[c], outs[t].at[c], send.at[t], recv.at[t], (x, y, 1 - c)) for t in range(len(ins))]

    def start(self, ins, outs, send, recv):
        for cp in self._copies(ins, outs, send, recv):
            cp.start()

    def finish(self, ins, outs, send, recv):
        for cp in self._copies(ins, outs, send, recv):
            cp.wait()


class _ShareAll:
    def __init__(self, arrays):
        self.ins = list(arrays)
        self.out_shapes = [jax.ShapeDtypeStruct((N_DEV,) + a.shape, a.dtype) for a in arrays]
        self.aliases = {}
        self.n_sems = (N_DEV - 1) * len(arrays)

    def _peers(self):
        x, y, c = _mesh_pos()
        flips = [((r >> 2) & 1, (r >> 1) & 1, r & 1) for r in range(1, N_DEV)]
        return (x, y, c), [(x ^ fx, y ^ fy, c ^ fc) for fx, fy, fc in flips]

    def _sends(self, ins, outs, send, recv):
        (x, y, c), peers = self._peers()
        mine = 4 * x + 2 * y + c
        return [_remote(ins[a], outs[a].at[mine], send.at[7 * a + r], recv.at[7 * a + r], peer)
                for a in range(len(ins)) for r, peer in enumerate(peers)]

    def start(self, ins, outs, send, recv):
        for cp in self._sends(ins, outs, send, recv):
            cp.start()

    def finish(self, ins, outs, send, recv):
        (x, y, c), peers = self._peers()
        for a in range(len(ins)):
            for r, (px, py, pc) in enumerate(peers):
                blk = outs[a].at[4 * px + 2 * py + pc]
                _remote(blk, blk, send.at[7 * a + r], recv.at[7 * a + r], (x, y, c)).wait_recv()
        for cp in self._sends(ins, outs, send, recv):
            cp.wait_send()


def _pallas(body, operands, *, name, grid, in_specs, out_specs, out_shape, scratch_shapes=(), vmem_mib=32, riders=()):
    in_specs, out_specs, out_shape, scratch_shapes = list(in_specs), list(out_specs), list(out_shape), list(scratch_shapes)
    if not riders:
        outs = pl.pallas_call(body, name=name, grid=grid, in_specs=in_specs, out_specs=out_specs, out_shape=out_shape,
                              scratch_shapes=scratch_shapes, compiler_params=_params(vmem_mib, len(grid)))(*operands)
        return list(outs), []
    n_in, n_out, n_scr = len(in_specs), len(out_specs), len(scratch_shapes)
    r_in = [len(r.ins) for r in riders]
    r_out = [len(r.out_shapes) for r in riders]
    steps = 1
    for g in grid:
        steps *= g

    def wrapped(*refs):
        refs = list(refs)
        ins, refs = refs[:n_in], refs[n_in:]
        rins = []
        for k in r_in:
            rins.append(refs[:k])
            refs = refs[k:]
        outs, refs = refs[:n_out], refs[n_out:]
        routs = []
        for k in r_out:
            routs.append(refs[:k])
            refs = refs[k:]
        scr, sems = refs[:n_scr], refs[n_scr:]
        step = 0
        for ax, g in enumerate(grid):
            step = step * g + pl.program_id(ax)

        def each(what):
            for j, r in enumerate(riders):
                getattr(r, what)(rins[j], routs[j], sems[2 * j], sems[2 * j + 1])

        if grid:
            pl.when(step == 0)(lambda: each("start"))
        else:
            each("start")
        body(*ins, *outs, *scr)
        if grid:
            pl.when(step == steps - 1)(lambda: each("finish"))
        else:
            each("finish")

    aliases, off_in, off_out = {}, n_in, n_out
    for r, ki, ko in zip(riders, r_in, r_out):
        for i, o in r.aliases.items():
            aliases[off_in + i] = off_out + o
        off_in, off_out = off_in + ki, off_out + ko
    sems = []
    for r in riders:
        sems += [pltpu.SemaphoreType.DMA((r.n_sems,)), pltpu.SemaphoreType.DMA((r.n_sems,))]
    res = pl.pallas_call(
        wrapped, name=name, grid=grid,
        in_specs=in_specs + [ANY] * sum(r_in), out_specs=out_specs + [ANY] * sum(r_out),
        out_shape=out_shape + [s for r in riders for s in r.out_shapes],
        scratch_shapes=scratch_shapes + sems, input_output_aliases=aliases,
        compiler_params=pltpu.CompilerParams(dimension_semantics=("arbitrary",) * len(grid),
                                             vmem_limit_bytes=vmem_mib * MIB, has_side_effects=True),
    )(*operands, *[a for r in riders for a in r.ins])
    res = list(res)
    outs, res = res[:n_out], res[n_out:]
    routs = []
    for k in r_out:
        routs.append(res[:k])
        res = res[k:]
    return outs, routs


def _exchange(riders, name):
    return _pallas(lambda: None, [], name=name, grid=(), in_specs=[], out_specs=[], out_shape=[], riders=riders)[1]


def _in_hbm(a):
    return pltpu.with_memory_space_constraint(a, pltpu.HBM)


def _place_shard(w, layer, dtype, name):
    _, rows, cols = w.shape
    half = rows // 2
    br = _block_rows(half)
    nb = half // br
    mine = 2 * lax.axis_index("x") + lax.axis_index("y")

    def body(q_ref, w_ref, o_ref):
        o_ref[...] = w_ref[...].astype(dtype)

    return pl.pallas_call(
        body, name=name,
        grid_spec=pltpu.PrefetchScalarGridSpec(
            num_scalar_prefetch=1, grid=(2, nb),
            in_specs=[pl.BlockSpec((None, br, cols), lambda h, i, q: (layer, h * nb + i, 0))],
            out_specs=pl.BlockSpec((None, None, br, cols), lambda h, i, q: (q[0], h, i, 0))),
        out_shape=jax.ShapeDtypeStruct((N_CHIPS, 2, half, cols), dtype),
        compiler_params=_params(16, 2),
    )(jnp.reshape(mine, (1,)).astype(jnp.int32), w)


def _add_pair(g, recv, name):
    _, _, r, cdim = g.shape
    br = _block_rows(r, 256)
    c = lax.axis_index("c")

    def body(c_ref, g_ref, r_ref, o_ref):
        o_ref[...] = (g_ref[...] + r_ref[...]).astype(BF16)

    return pl.pallas_call(
        body, name=name,
        grid_spec=pltpu.PrefetchScalarGridSpec(
            num_scalar_prefetch=1, grid=(N_CHIPS, r // br),
            in_specs=[pl.BlockSpec((None, None, br, cdim), lambda q, i, c_ref: (q, c_ref[0], i, 0)),
                      pl.BlockSpec((None, br, cdim), lambda q, i, c_ref: (q, i, 0))],
            out_specs=pl.BlockSpec((None, br, cdim), lambda q, i, c_ref: (q, i, 0))),
        out_shape=jax.ShapeDtypeStruct((N_CHIPS, r, cdim), BF16),
        compiler_params=_params(16, 2),
    )(jnp.reshape(c, (1,)).astype(jnp.int32), _in_hbm(g), _in_hbm(recv))


def _add_chips(own, recv, name):
    _, r, cdim = own.shape
    br = _block_rows(r, 256)
    x, y, c = _mesh_pos()

    def body(pos_ref, own_ref, r_ref, o_ref):
        acc = own_ref[...].astype(F32)
        for k in range(3):
            acc = acc + r_ref[k].astype(F32)
        o_ref[...] = acc

    return pl.pallas_call(
        body, name=name,
        grid_spec=pltpu.PrefetchScalarGridSpec(
            num_scalar_prefetch=1, grid=(r // br,),
            in_specs=[pl.BlockSpec((None, br, cdim), lambda i, pos: (pos[0], i, 0)),
                      pl.BlockSpec((3, br, cdim), lambda i, pos: (0, i, 0))],
            out_specs=pl.BlockSpec((None, br, cdim), lambda i, pos: (pos[1], i, 0))),
        out_shape=jax.ShapeDtypeStruct((2, r, cdim), F32),
        compiler_params=_params(16, 1),
    )(jnp.stack([2 * x + y, c]).astype(jnp.int32), _in_hbm(own), _in_hbm(recv))


def _adam_math(w, m, v, g):
    c1 = 1.0 / (1.0 - ADAM_B1 ** ADAM_STEP)
    c2 = 1.0 / (1.0 - ADAM_B2 ** ADAM_STEP)
    m_new = ADAM_B1 * m + (1.0 - ADAM_B1) * g
    v_new = ADAM_B2 * v + (1.0 - ADAM_B2) * (g * g)
    return -ADAM_LR * ((m_new * c1) / (jnp.sqrt(v_new * c2) + ADAM_EPS) + ADAM_WD * w), m_new, v_new


SMALL_WEIGHTS = [
    ("ev_norm_g", (1, D_MODEL), ["ev_norm_g"], None), ("ev_conv_a_b", (1, A_DIM), ["ev_conv_a_b"], None),
    ("ev_ln_a_g", (1, A_DIM), ["ev_ln_a_g"], None), ("ev_ln_a_b", (1, A_DIM), ["ev_ln_a_b"], None),
    ("od_w_s", (C_GROUPS, CHUNK, CHUNK), ["od_w_s_lo", "od_w_s_hi"], None), ("od_b_s", (C_GROUPS, CHUNK), ["od_b_s"], None),
    ("mlp_norm_g", (2, D_MODEL), ["mlp_norm_g0", "mlp_norm_g1"], None), ("final_norm_g", (1, D_MODEL), ["final_norm_g"], None),
    ("ev_conv_a_w", (A_CONV_WIDTH, A_DIM // N_CHIPS), ["ev_conv_a_w"], A_DIM // N_CHIPS),
    ("ev_conv_b_w", (B_CONV_WIDTH, B_DIM // N_CHIPS), ["ev_conv_b_w"], B_DIM // N_CHIPS),
    ("od_norm_g", (1, D_MODEL // N_CHIPS), ["od_norm_g"], D_MODEL // N_CHIPS),
    ("od_b_in", (1, 2 * C_DIM // N_CHIPS), ["od_b_in"], 2 * C_DIM // N_CHIPS),
    ("od_ln_v_g", (1, C_DIM // N_CHIPS), ["od_ln_v_g"], C_DIM // N_CHIPS),
    ("od_ln_v_b", (1, C_DIM // N_CHIPS), ["od_ln_v_b"], C_DIM // N_CHIPS),
]


def _small_update(own, landed, weights):
    names = list(own.keys())
    n_g, n_w = len(names), len(SMALL_WEIGHTS)

    def body(*refs):
        refs = list(refs)
        own_refs = dict(zip(names, refs[:n_g]))
        land_refs = dict(zip(names, refs[n_g:2 * n_g]))
        wmv = [refs[2 * n_g + 3 * i:2 * n_g + 3 * i + 3] for i in range(n_w)]
        o0 = 2 * n_g + 3 * n_w
        loss_ref = refs[o0]
        outs = [refs[o0 + 1 + 4 * i:o0 + 5 + 4 * i] for i in range(n_w)]
        acc = dict(zip(names, refs[o0 + 1 + 4 * n_w:]))
        x, y, c = _mesh_pos()
        mine, chip = 4 * x + 2 * y + c, 2 * x + y

        for nm in names:
            for d in range(N_DEV):
                def add(term, nm=nm, d=d):
                    acc[nm][...] = term if d == 0 else acc[nm][...] + term
                pl.when(mine == d)(lambda nm=nm, add=add: add(own_refs[nm][...]))
                pl.when(mine != d)(lambda nm=nm, d=d, add=add: add(land_refs[nm][d]))
        loss_ref[...] = acc["loss"][...]

        def update(i, rows, g):
            w_ref, m_ref, v_ref = wmv[i]
            delta, m_new, v_new = _adam_math(w_ref[rows], m_ref[rows], v_ref[rows], g)
            for ref, val in zip(outs[i], (g, delta, m_new, v_new)):
                ref[rows] = val

        for i, (_, shape, grads, per_chip) in enumerate(SMALL_WEIGHTS):
            for row, gname in enumerate(grads):
                per_grad = shape[0] // len(grads)
                rows = slice(row * per_grad, (row + 1) * per_grad)
                if per_chip is None:
                    update(i, rows, acc[gname][...])
                else:
                    for q in range(N_CHIPS):
                        pl.when(chip == q)(lambda i=i, rows=rows, gname=gname, q=q, per_chip=per_chip:
                                           update(i, rows, acc[gname][:, q * per_chip:(q + 1) * per_chip]))

    operands = [own[nm] for nm in names] + [landed[nm] for nm in names]
    for nm, _, _, _ in SMALL_WEIGHTS:
        operands += list(weights[nm])
    out_shape = [jax.ShapeDtypeStruct((1, 1), F32)]
    for _, shape, _, _ in SMALL_WEIGHTS:
        out_shape += [jax.ShapeDtypeStruct(shape, F32)] * 4
    res = pl.pallas_call(
        body, name="small_update", grid=(1,),
        in_specs=[_full_spec(a.shape) for a in operands], out_specs=[_full_spec(s.shape) for s in out_shape],
        out_shape=out_shape, scratch_shapes=[pltpu.VMEM(own[nm].shape, F32) for nm in names],
        compiler_params=_params(32, 1),
    )(*[_in_hbm(a) for a in operands])
    return res[0], {nm: res[1 + 4 * i:5 + 4 * i] for i, (nm, _, _, _) in enumerate(SMALL_WEIGHTS)}


def _adamw(w, m, v, grads, name, riders=()):
    layers, r, cdim = w.shape
    br = _block_rows(r, 256 if cdim > LANES else 1024)

    def body(*refs):
        w_ref, m_ref, v_ref = refs[:3]
        g_refs = refs[3:3 + layers]
        go_ref, d_ref, mo_ref, vo_ref = refs[3 + layers:]
        layer = pl.program_id(0)
        for l in range(layers):
            @pl.when(layer == l)
            def _(l=l):
                g = g_refs[l][...]
                go_ref[...] = g
                d_ref[...], mo_ref[...], vo_ref[...] = _adam_math(w_ref[...], m_ref[...], v_ref[...], g)

    spec3 = pl.BlockSpec((None, br, cdim), lambda l, i: (l, i, 0))
    spec2 = pl.BlockSpec((br, cdim), lambda l, i: (i, 0))
    out = jax.ShapeDtypeStruct((layers, r, cdim), F32)
    return _pallas(body, [w, m, v, *[_in_hbm(g) for g in grads]], name=name, grid=(layers, r // br),
                   in_specs=[spec3, spec3, spec3] + [spec2] * layers, out_specs=[spec3] * 4, out_shape=[out] * 4,
                   vmem_mib=32, riders=riders)


def _fill_shifted(buf, rows):
    for b in range(1, SUBLANES):
        buf[b, 0:rows - SUBLANES, :] = buf[0, b:b + rows - SUBLANES, :]


def _window(buf, start, size):
    return buf[start % SUBLANES, start - start % SUBLANES:start - start % SUBLANES + size, :]


def _conv31(src, w_ref, r0, base, init):
    acc = init
    for k in range(A_CONV_WIDTH):
        acc = acc + w_ref[k:k + 1, :] * _window(src, base + k + r0, CONV_ROWS)
    return acc


def _fwd_even(x, norm_g, w_in, conv_a_w, conv_a_b, ln_g, ln_b, conv_b_w, w_out, *, tm, seq, riders=()):
    tokens = x.shape[0]
    nt, tps = tokens // tm, seq // tm

    def body(x_ref, g_ref, win_hbm, caw_ref, cab_ref, lng_ref, lnb_ref, cbw_ref, wout_hbm,
             h_ref, n_ref, z_ref, a2_ref, cv_ref, mix_ref, win_v, wout_v, pa, pb, sem):
        i = pl.program_id(0)

        _load_weights([(win_hbm, win_v), (wout_hbm, wout_v)], sem)

        xv = x_ref[...]
        nf, _ = _rms_fwd(xv, g_ref[...])
        n = nf.astype(BF16)
        n_ref[...] = n
        z = jnp.concatenate([_dot(n, win_v[j]) for j in range(N_CHIPS)], axis=1)
        z_ref[...] = z.astype(BF16)
        a_val, a_gate = z[:, 0:A_DIM], z[:, A_DIM:2 * A_DIM]
        b_gate, c_gate, b_val = z[:, 1024:1536], z[:, 1536:2048], z[:, 2048:2560]

        first = (i % tps) == 0

        @pl.when(first)
        def _():
            pa[0, 0:A_HALO, :] = jnp.zeros((A_HALO, A_DIM), F32)
            pb[0:B_HALO, :] = jnp.zeros((B_HALO, B_DIM), F32)

        @pl.when(jnp.logical_not(first))
        def _():
            pa[0, 0:A_HALO, :] = pa[0, tm:tm + A_HALO, :]
            pb[0:B_HALO, :] = pb[tm:tm + B_HALO, :]

        pa[0, A_HALO:A_HALO + tm, :] = a_val * jax.nn.sigmoid(a_gate)
        pb[B_HALO:B_HALO + tm, :] = c_gate * b_val
        _fill_shifted(pa, A_HALO + tm)
        bias = jnp.broadcast_to(cab_ref[...], (CONV_ROWS, A_DIM))
        for r0 in range(0, tm, CONV_ROWS):
            a2_ref[r0:r0 + CONV_ROWS, :] = _conv31(pa, caw_ref, r0, A_HALO - (A_CONV_WIDTH - 1), bias)
        xhat, _ = _ln_stats(a2_ref[...])
        a3 = xhat * lng_ref[...] + lnb_ref[...]
        a4 = a3 * jax.nn.sigmoid(a3)
        cv = cbw_ref[0:1, :] * pb[B_HALO - 2:B_HALO - 2 + tm, :]
        cv = cv + cbw_ref[1:2, :] * pb[B_HALO - 1:B_HALO - 1 + tm, :]
        cv = cv + cbw_ref[2:3, :] * pb[B_HALO:B_HALO + tm, :]
        cv_ref[...] = cv.astype(BF16)
        mix = jnp.concatenate([a4, b_gate * cv], axis=1).astype(BF16)
        mix_ref[...] = mix
        h_ref[...] = xv + _dot(mix, wout_v[...])

    shp = lambda cols, dt: jax.ShapeDtypeStruct((tokens, cols), dt)
    return _pallas(
        body, [x, norm_g, w_in, conv_a_w, conv_a_b, ln_g, ln_b, conv_b_w, w_out], name="fwd_even", grid=(nt,),
        in_specs=[_row_spec(tm, D_MODEL), _full_spec((1, D_MODEL)), ANY, _full_spec((A_CONV_WIDTH, A_DIM)),
                  _full_spec((1, A_DIM)), _full_spec((1, A_DIM)), _full_spec((1, A_DIM)),
                  _full_spec((B_CONV_WIDTH, B_DIM)), ANY],
        out_specs=[_row_spec(tm, D_MODEL), _row_spec(tm, D_MODEL), _row_spec(tm, IN_EVEN), _row_spec(tm, A_DIM),
                   _row_spec(tm, B_DIM), _row_spec(tm, D_MODEL)],
        out_shape=[shp(D_MODEL, F32), shp(D_MODEL, BF16), shp(IN_EVEN, BF16), shp(A_DIM, F32), shp(B_DIM, BF16),
                   shp(D_MODEL, BF16)],
        scratch_shapes=[pltpu.VMEM((N_CHIPS, D_MODEL, IN_EVEN // N_CHIPS), BF16), pltpu.VMEM((D_MODEL, D_MODEL), BF16),
                        pltpu.VMEM((SUBLANES, A_HALO + tm, A_DIM), F32), pltpu.VMEM((B_HALO + tm, B_DIM), F32),
                        pltpu.SemaphoreType.DMA((N_LOADS,))],
        vmem_mib=56, riders=riders)


def _loss_tail(xv, g, target, loss_ref, dh_ref, dhb_ref, dg_ref):
    @pl.when(pl.program_id(0) == 0)
    def _():
        loss_ref[...] = jnp.zeros((1, 1), F32)
        dg_ref[...] = jnp.zeros((1, D_MODEL), F32)

    out, rstd = _rms_fwd(xv, g)
    err = out - target
    per_token = jnp.sum(err * err, axis=1, keepdims=True) * (1.0 / D_MODEL)
    loss_ref[...] += 0.5 * jnp.sum(per_token, axis=0, keepdims=True)
    dx, dg = _rms_bwd(err * (1.0 / D_MODEL), xv, rstd, g)
    dh_ref[...] = dx
    dhb_ref[...] = dx.astype(BF16)
    dg_ref[...] += dg


def _fwd_mlp(h, norm_g, w1, w2, layer, *, tm, riders=(), head=None):
    tokens = h.shape[0]
    nt = tokens // tm
    fs = D_FF // N_CHIPS
    n_in = 4 if head is None else 6

    def body(*refs):
        h_ref, g_ref, w1_hbm, w2_hbm = refs[:4]
        w1_v, w2_v, sem = refs[-3:]
        outs = refs[n_in:-3]
        n_ref, p_ref, q_ref = outs[1:4] if head is None else outs[0:3]
        _load_weights([(w1_hbm, w1_v), (w2_hbm, w2_v)], sem)

        xv = h_ref[...]
        nf, _ = _rms_fwd(xv, g_ref[...])
        n = nf.astype(BF16)
        n_ref[...] = n
        acc = xv
        for j in range(N_CHIPS):
            p = _dot(n, w1_v[j])
            p_ref[:, j * fs:(j + 1) * fs] = p.astype(BF16)
            r = jnp.maximum(p, 0.0)
            q = (r * r).astype(BF16)
            q_ref[:, j * fs:(j + 1) * fs] = q
            acc = acc + _dot(q, w2_v[j])
        if head is None:
            outs[0][...] = acc
        else:
            _loss_tail(acc, refs[4][...], refs[5][...], *outs[3:7])

    shp = lambda cols, dt: jax.ShapeDtypeStruct((tokens, cols), dt)
    saved_specs = [_row_spec(tm, D_MODEL), _row_spec(tm, D_FF), _row_spec(tm, D_FF)]
    saved_shapes = [shp(D_MODEL, BF16), shp(D_FF, BF16), shp(D_FF, BF16)]
    if head is None:
        operands, in_specs = [h, norm_g, w1, w2], [_row_spec(tm, D_MODEL), _full_spec((1, D_MODEL)), ANY, ANY]
        out_specs, out_shape = [_row_spec(tm, D_MODEL)] + saved_specs, [shp(D_MODEL, F32)] + saved_shapes
    else:
        operands = [h, norm_g, w1, w2, *head]
        in_specs = [_row_spec(tm, D_MODEL), _full_spec((1, D_MODEL)), ANY, ANY, _full_spec((1, D_MODEL)), _row_spec(tm, D_MODEL)]
        out_specs = saved_specs + [_full_spec((1, 1)), _row_spec(tm, D_MODEL), _row_spec(tm, D_MODEL), _full_spec((1, D_MODEL))]
        out_shape = saved_shapes + [jax.ShapeDtypeStruct((1, 1), F32), shp(D_MODEL, F32), shp(D_MODEL, BF16),
                                    jax.ShapeDtypeStruct((1, D_MODEL), F32)]
    return _pallas(
        body, operands, name=f"fwd_mlp{layer}", grid=(nt,), in_specs=in_specs, out_specs=out_specs, out_shape=out_shape,
        scratch_shapes=[pltpu.VMEM((N_CHIPS, D_MODEL, fs), BF16), pltpu.VMEM((N_CHIPS, fs, D_MODEL), BF16),
                        pltpu.SemaphoreType.DMA((N_LOADS,))],
        vmem_mib=56, riders=riders)


def _tril_mask():
    row = lax.broadcasted_iota(jnp.int32, (CHUNK, CHUNK), 0)
    col = lax.broadcasted_iota(jnp.int32, (CHUNK, CHUNK), 1)
    return row >= col


def _triu_mask():
    row = lax.broadcasted_iota(jnp.int32, (CHUNK, CHUNK), 0)
    col = lax.broadcasted_iota(jnp.int32, (CHUNK, CHUNK), 1)
    return row <= col


def _fwd_odd(h, norm_g, w_in, b_in, ln_g, ln_b, w_s, b_s_rows, w_out, *, tm, riders=()):
    tokens = h.shape[0]
    nt = tokens // tm
    cs = 2 * C_DIM // N_CHIPS

    def body(h_ref, g_ref, win_hbm, bin_ref, lng_ref, lnb_ref, ws_ref, bs_ref, wout_hbm,
             ho_ref, n_ref, s_ref, cdf_ref, sv_ref, y_ref, win_v, wout_v, bd, sem):
        _load_weights([(win_hbm, win_v), (wout_hbm, wout_v)], sem)

        @pl.when(pl.program_id(0) == 0)
        def _():
            mask = _tril_mask()
            bd[...] = jnp.zeros(bd.shape, BF16)
            for g in range(C_GROUPS):
                w = jnp.where(mask, ws_ref[g], 0.0).astype(BF16)
                bd[g, 0:CHUNK, 0:CHUNK] = w
                bd[g, CHUNK:PAIR, CHUNK:PAIR] = w

        xv = h_ref[...]
        nf, _ = _rms_fwd(xv, g_ref[...])
        n = nf.astype(BF16)
        n_ref[...] = n
        s = jnp.concatenate([_dot(n, win_v[j]) for j in range(N_CHIPS)], axis=1) + bin_ref[...]
        s_ref[...] = s.astype(BF16)
        cdf = _gelu_cdf(s)
        cdf_ref[...] = cdf.astype(BF16)
        zz = s * cdf
        u, v = zz[:, 0:C_DIM], zz[:, C_DIM:2 * C_DIM]
        xhat, _ = _ln_stats(v)
        vn = (xhat * lng_ref[...] + lnb_ref[...]).astype(BF16)
        for g in range(C_GROUPS):
            cols = slice(g * CHUNK, (g + 1) * CHUNK)
            bias = jnp.concatenate([bs_ref[g], bs_ref[g]], axis=0)
            for r0 in range(0, tm, PAIR):
                sv = _dot(bd[g], vn[r0:r0 + PAIR, cols]) + bias
                sv_ref[r0:r0 + PAIR, cols] = sv.astype(BF16)
                y_ref[r0:r0 + PAIR, cols] = (u[r0:r0 + PAIR, cols] * sv).astype(BF16)
        ho_ref[...] = xv + _dot(y_ref[...], wout_v[...])

    shp = lambda cols, dt: jax.ShapeDtypeStruct((tokens, cols), dt)
    return _pallas(
        body, [h, norm_g, w_in, b_in, ln_g, ln_b, w_s, b_s_rows, w_out], name="fwd_odd", grid=(nt,),
        in_specs=[_row_spec(tm, D_MODEL), _full_spec((1, D_MODEL)), ANY, _full_spec((1, 2 * C_DIM)),
                  _full_spec((1, C_DIM)), _full_spec((1, C_DIM)), _full_spec((C_GROUPS, CHUNK, CHUNK)),
                  _full_spec((C_GROUPS, CHUNK, CHUNK)), ANY],
        out_specs=[_row_spec(tm, D_MODEL), _row_spec(tm, D_MODEL), _row_spec(tm, 2 * C_DIM), _row_spec(tm, 2 * C_DIM),
                   _row_spec(tm, C_DIM), _row_spec(tm, C_DIM)],
        out_shape=[shp(D_MODEL, F32), shp(D_MODEL, BF16), shp(2 * C_DIM, BF16), shp(2 * C_DIM, BF16), shp(C_DIM, BF16),
                   shp(C_DIM, BF16)],
        scratch_shapes=[pltpu.VMEM((N_CHIPS, D_MODEL, cs), BF16), pltpu.VMEM((C_DIM, D_MODEL), BF16),
                        pltpu.VMEM((C_GROUPS, PAIR, PAIR), BF16), pltpu.SemaphoreType.DMA((N_LOADS,))],
        vmem_mib=56, riders=riders)


def _bwd_mlp(dh, h, norm_g, p, w1, w2, layer, *, tm, riders=()):
    tokens = h.shape[0]
    nt = tokens // tm
    fs = D_FF // N_CHIPS

    def body(dh_ref, h_ref, g_ref, p_ref, w1_hbm, w2_hbm, dx_ref, dxb_ref, dp_ref, dg_ref, w1_v, w2_v, sem):
        @pl.when(pl.program_id(0) == 0)
        def _():
            dg_ref[...] = jnp.zeros((1, D_MODEL), F32)

        _load_weights([(w1_hbm, w1_v), (w2_hbm, w2_v)], sem)

        dhv = dh_ref[...]
        dhb = dhv.astype(BF16)
        dn = jnp.zeros((tm, D_MODEL), F32)
        for j in range(N_CHIPS):
            dq = _dot_nt(dhb, w2_v[j])
            r = jnp.maximum(p_ref[:, j * fs:(j + 1) * fs].astype(F32), 0.0)
            dp = ((2.0 * r) * dq).astype(BF16)
            dp_ref[:, j * fs:(j + 1) * fs] = dp
            dn = dn + _dot_nt(dp, w1_v[j])
        xv = h_ref[...]
        g = g_ref[...]
        _, rstd = _rms_fwd(xv, g)
        dx, dg = _rms_bwd(dn, xv, rstd, g)
        dx_ref[...] = dhv + dx
        dxb_ref[...] = (dhv + dx).astype(BF16)
        dg_ref[...] += dg

    return _pallas(
        body, [dh, h, norm_g, p, w1, w2], name=f"bwd_mlp{layer}", grid=(nt,),
        in_specs=[_row_spec(tm, D_MODEL), _row_spec(tm, D_MODEL), _full_spec((1, D_MODEL)), _row_spec(tm, D_FF), ANY, ANY],
        out_specs=[_row_spec(tm, D_MODEL), _row_spec(tm, D_MODEL), _row_spec(tm, D_FF), _full_spec((1, D_MODEL))],
        out_shape=[jax.ShapeDtypeStruct((tokens, D_MODEL), F32), jax.ShapeDtypeStruct((tokens, D_MODEL), BF16),
                   jax.ShapeDtypeStruct((tokens, D_FF), BF16), jax.ShapeDtypeStruct((1, D_MODEL), F32)],
        scratch_shapes=[pltpu.VMEM((N_CHIPS, D_MODEL, fs), BF16), pltpu.VMEM((N_CHIPS, fs, D_MODEL), BF16),
                        pltpu.SemaphoreType.DMA((N_LOADS,))],
        vmem_mib=56, riders=riders)


def _bwd_odd(dh, h, norm_g, s, cdf, sv, w_in, ln_g, ln_b, w_s, w_out, *, tm, riders=()):
    tokens = h.shape[0]
    nt = tokens // tm
    cs = 2 * C_DIM // N_CHIPS

    def body(dh_ref, h_ref, g_ref, s_ref, cdf_ref, sv_ref, win_hbm, lng_ref, lnb_ref, ws_ref, wout_hbm,
             dx_ref, dxb_ref, ds_ref, dg_ref, dbin_ref, dlng_ref, dlnb_ref, dws_ref, dbs_ref,
             win_v, wout_v, bdt, dws_acc, dbs_acc, dvn, sem):
        i = pl.program_id(0)

        _load_weights([(win_hbm, win_v), (wout_hbm, wout_v)], sem)

        @pl.when(i == 0)
        def _():
            mask_t = _triu_mask()
            bdt[...] = jnp.zeros(bdt.shape, BF16)
            for g in range(C_GROUPS):
                wt = jnp.where(mask_t, ws_ref[g].T, 0.0).astype(BF16)
                bdt[g, 0:CHUNK, 0:CHUNK] = wt
                bdt[g, CHUNK:PAIR, CHUNK:PAIR] = wt
            dws_acc[...] = jnp.zeros(dws_acc.shape, F32)
            dbs_acc[...] = jnp.zeros(dbs_acc.shape, F32)
            dg_ref[...] = jnp.zeros(dg_ref.shape, F32)
            dbin_ref[...] = jnp.zeros(dbin_ref.shape, F32)
            dlng_ref[...] = jnp.zeros(dlng_ref.shape, F32)
            dlnb_ref[...] = jnp.zeros(dlnb_ref.shape, F32)

        dhv = dh_ref[...]
        dy = _dot_nt(dhv.astype(BF16), wout_v[...])
        sf = s_ref[...].astype(F32)
        cdf = cdf_ref[...].astype(F32)
        pdf = jnp.exp(-0.5 * sf * sf) * 0.3989422804014327
        zz = sf * cdf
        dgelu = cdf + sf * pdf
        u, v = zz[:, 0:C_DIM], zz[:, C_DIM:2 * C_DIM]
        xhat, rs = _ln_stats(v)
        lng = lng_ref[...]
        vn = (xhat * lng + lnb_ref[...]).astype(BF16)
        du = dy * sv_ref[...].astype(F32)
        dsv = dy * u
        dsvb = dsv.astype(BF16)
        for g in range(C_GROUPS):
            cols = slice(g * CHUNK, (g + 1) * CHUNK)
            for r0 in range(0, tm, PAIR):
                blk = dsvb[r0:r0 + PAIR, cols]
                dvn[r0:r0 + PAIR, cols] = _dot(bdt[g], blk)
                dws_acc[g] += _dot_nt(blk, vn[r0:r0 + PAIR, cols])
                dbs_acc[g] += dsv[r0:r0 + CHUNK, cols] + dsv[r0 + CHUNK:r0 + PAIR, cols]
        dv, dlng, dlnb = _ln_bwd(dvn[...], xhat, rs, lng)
        dlng_ref[...] += dlng
        dlnb_ref[...] += dlnb
        ds = jnp.concatenate([du, dv], axis=1) * dgelu
        dbin_ref[...] += jnp.sum(ds, axis=0, keepdims=True)
        dsb = ds.astype(BF16)
        ds_ref[...] = dsb
        dn = jnp.zeros((tm, D_MODEL), F32)
        for j in range(N_CHIPS):
            dn = dn + _dot_nt(dsb[:, j * cs:(j + 1) * cs], win_v[j])
        xv = h_ref[...]
        g = g_ref[...]
        _, rstd = _rms_fwd(xv, g)
        dx, dg = _rms_bwd(dn, xv, rstd, g)
        dx_ref[...] = dhv + dx
        dxb_ref[...] = (dhv + dx).astype(BF16)
        dg_ref[...] += dg

        @pl.when(i == nt - 1)
        def _():
            mask = _tril_mask()
            for g in range(C_GROUPS):
                full = dws_acc[g]
                dws_ref[g] = jnp.where(mask, full[0:CHUNK, 0:CHUNK] + full[CHUNK:PAIR, CHUNK:PAIR], 0.0)
                dbs_ref[g:g + 1, :] = jnp.sum(dbs_acc[g].T, axis=0, keepdims=True)

    row = lambda cols: jax.ShapeDtypeStruct((1, cols), F32)
    return _pallas(
        body, [dh, h, norm_g, s, cdf, sv, w_in, ln_g, ln_b, w_s, w_out], name="bwd_odd", grid=(nt,),
        in_specs=[_row_spec(tm, D_MODEL), _row_spec(tm, D_MODEL), _full_spec((1, D_MODEL)), _row_spec(tm, 2 * C_DIM),
                  _row_spec(tm, 2 * C_DIM), _row_spec(tm, C_DIM), ANY, _full_spec((1, C_DIM)), _full_spec((1, C_DIM)),
                  _full_spec((C_GROUPS, CHUNK, CHUNK)), ANY],
        out_specs=[_row_spec(tm, D_MODEL), _row_spec(tm, D_MODEL), _row_spec(tm, 2 * C_DIM), _full_spec((1, D_MODEL)),
                   _full_spec((1, 2 * C_DIM)),
                   _full_spec((1, C_DIM)), _full_spec((1, C_DIM)), _full_spec((C_GROUPS, CHUNK, CHUNK)),
                   _full_spec((C_GROUPS, CHUNK))],
        out_shape=[jax.ShapeDtypeStruct((tokens, D_MODEL), F32), jax.ShapeDtypeStruct((tokens, D_MODEL), BF16),
                   jax.ShapeDtypeStruct((tokens, 2 * C_DIM), BF16),
                   row(D_MODEL), row(2 * C_DIM), row(C_DIM), row(C_DIM),
                   jax.ShapeDtypeStruct((C_GROUPS, CHUNK, CHUNK), F32), jax.ShapeDtypeStruct((C_GROUPS, CHUNK), F32)],
        scratch_shapes=[pltpu.VMEM((N_CHIPS, D_MODEL, cs), BF16), pltpu.VMEM((C_DIM, D_MODEL), BF16),
                        pltpu.VMEM((C_GROUPS, PAIR, PAIR), BF16), pltpu.VMEM((C_GROUPS, PAIR, PAIR), F32),
                        pltpu.VMEM((C_GROUPS, CHUNK, CHUNK), F32), pltpu.VMEM((tm, C_DIM), F32),
                        pltpu.SemaphoreType.DMA((N_LOADS,))],
        vmem_mib=56, riders=riders)


def _bwd_even(dh, x, norm_g, z, a2, cv, w_in, conv_a_w, ln_g, ln_b, conv_b_w, w_out, *, tm, seq, riders=()):
    tokens = x.shape[0]
    nt, tps = tokens // tm, seq // tm
    ws = IN_EVEN // N_CHIPS

    def body(dh_ref, x_ref, g_ref, z_ref, a2_ref, cv_ref, win_hbm, caw_ref, lng_ref, lnb_ref, cbw_ref, wout_hbm,
             dx_ref, dz_ref, dg_ref, dcaw_ref, dcab_ref, dlng_ref, dlnb_ref, dcbw_ref,
             win_v, wout_v, ea, eb, a1s, da1s, dw_acc, sem):
        i = pl.program_id(0)

        _load_weights([(win_hbm, win_v), (wout_hbm, wout_v)], sem)

        @pl.when(i == 0)
        def _():
            dw_acc[...] = jnp.zeros(dw_acc.shape, F32)
            for ref in (dg_ref, dcab_ref, dlng_ref, dlnb_ref, dcbw_ref):
                ref[...] = jnp.zeros(ref.shape, F32)

        dhv = dh_ref[...]
        dmix = _dot_nt(dhv.astype(BF16), wout_v[...])
        da4, dbo = dmix[:, 0:A_DIM], dmix[:, A_DIM:A_DIM + B_DIM]
        zf = z_ref[...].astype(F32)
        a_val, a_gate = zf[:, 0:A_DIM], zf[:, A_DIM:2 * A_DIM]
        b_gate, c_gate, b_val = zf[:, 1024:1536], zf[:, 1536:2048], zf[:, 2048:2560]

        xhat, rs = _ln_stats(a2_ref[...])
        lng = lng_ref[...]
        a3 = xhat * lng + lnb_ref[...]
        sg = jax.nn.sigmoid(a3)
        da3 = da4 * (sg * (1.0 + a3 * (1.0 - sg)))
        da2, dlng, dlnb = _ln_bwd(da3, xhat, rs, lng)
        dlng_ref[...] += dlng
        dlnb_ref[...] += dlnb
        dcab_ref[...] += jnp.sum(da2, axis=0, keepdims=True)

        last = ((nt - 1 - i) % tps) == tps - 1
        dcv = dbo * b_gate

        @pl.when(last)
        def _():
            ea[0, tm:tm + A_HALO, :] = jnp.zeros((A_HALO, A_DIM), F32)
            eb[tm:tm + B_HALO, :] = jnp.zeros((B_HALO, B_DIM), F32)

        @pl.when(jnp.logical_not(last))
        def _():
            ea[0, tm:tm + A_HALO, :] = ea[0, 0:A_HALO, :]
            eb[tm:tm + B_HALO, :] = eb[0:B_HALO, :]

        ea[0, 0:tm, :] = da2
        eb[0:tm, :] = dcv
        _fill_shifted(ea, tm + A_HALO)
        sig = jax.nn.sigmoid(a_gate)
        a1s[...] = a_val * sig
        for r0 in range(0, tm, CONV_ROWS):
            a1c = a1s[r0:r0 + CONV_ROWS, :]
            acc = jnp.zeros((CONV_ROWS, A_DIM), F32)
            for j in range(A_CONV_WIDTH):
                k = A_CONV_WIDTH - 1 - j
                sl = _window(ea, r0 + j, CONV_ROWS)
                acc = acc + caw_ref[k:k + 1, :] * sl
                dw_acc[k] += sl * a1c
            da1s[r0:r0 + CONV_ROWS, :] = acc
        da1 = da1s[...]
        da_val = da1 * sig
        da_gate = da1 * a_val * (sig * (1.0 - sig))

        db_gate = dbo * cv_ref[...].astype(F32)
        cb = c_gate * b_val
        dcb = jnp.zeros((tm, B_DIM), F32)
        for j in range(B_CONV_WIDTH):
            k = B_CONV_WIDTH - 1 - j
            sl = eb[j:j + tm, :]
            dcb = dcb + cbw_ref[k:k + 1, :] * sl
            dcbw_ref[k:k + 1, :] += jnp.sum(sl * cb, axis=0, keepdims=True)
        dz = jnp.concatenate([da_val, da_gate, db_gate, dcb * b_val, dcb * c_gate], axis=1).astype(BF16)
        dz_ref[...] = dz
        dn = jnp.zeros((tm, D_MODEL), F32)
        for j in range(N_CHIPS):
            dn = dn + _dot_nt(dz[:, j * ws:(j + 1) * ws], win_v[j])
        xv = x_ref[...]
        g = g_ref[...]
        _, rstd = _rms_fwd(xv, g)
        dx, dg = _rms_bwd(dn, xv, rstd, g)
        dx_ref[...] = dhv + dx
        dg_ref[...] += dg

        @pl.when(i == nt - 1)
        def _():
            for k in range(A_CONV_WIDTH):
                dcaw_ref[k:k + 1, :] = jnp.sum(dw_acc[k], axis=0, keepdims=True)

    row = lambda cols: jax.ShapeDtypeStruct((1, cols), F32)
    rs_ = functools.partial(_row_spec, rev_nt=nt)
    return _pallas(
        body, [dh, x, norm_g, z, a2, cv, w_in, conv_a_w, ln_g, ln_b, conv_b_w, w_out], name="bwd_even", grid=(nt,),
        in_specs=[rs_(tm, D_MODEL), rs_(tm, D_MODEL), _full_spec((1, D_MODEL)), rs_(tm, IN_EVEN), rs_(tm, A_DIM),
                  rs_(tm, B_DIM), ANY, _full_spec((A_CONV_WIDTH, A_DIM)), _full_spec((1, A_DIM)), _full_spec((1, A_DIM)),
                  _full_spec((B_CONV_WIDTH, B_DIM)), ANY],
        out_specs=[rs_(tm, D_MODEL), rs_(tm, IN_EVEN), _full_spec((1, D_MODEL)), _full_spec((A_CONV_WIDTH, A_DIM)),
                   _full_spec((1, A_DIM)), _full_spec((1, A_DIM)), _full_spec((1, A_DIM)), _full_spec((B_CONV_WIDTH, B_DIM))],
        out_shape=[jax.ShapeDtypeStruct((tokens, D_MODEL), F32), jax.ShapeDtypeStruct((tokens, IN_EVEN), BF16),
                   row(D_MODEL), jax.ShapeDtypeStruct((A_CONV_WIDTH, A_DIM), F32), row(A_DIM), row(A_DIM), row(A_DIM),
                   jax.ShapeDtypeStruct((B_CONV_WIDTH, B_DIM), F32)],
        scratch_shapes=[pltpu.VMEM((N_CHIPS, D_MODEL, ws), BF16), pltpu.VMEM((D_MODEL, D_MODEL), BF16),
                        pltpu.VMEM((SUBLANES, tm + A_HALO, A_DIM), F32), pltpu.VMEM((tm + B_HALO, B_DIM), F32),
                        pltpu.VMEM((tm, A_DIM), F32), pltpu.VMEM((tm, A_DIM), F32),
                        pltpu.VMEM((A_CONV_WIDTH, CONV_ROWS, A_DIM), F32), pltpu.SemaphoreType.DMA((N_LOADS,))],
        vmem_mib=56, riders=riders)


def _wgrad(a, b, name, *, col_shards, riders=()):
    tokens, m = a.shape
    n = b.shape[1]
    kc = 512
    if col_shards:
        bm, bn = m // 2, n // N_CHIPS
        grid = (2, N_CHIPS)
        out_spec = pl.BlockSpec((None, None, bm, bn), lambda i, j: (j, i, 0, 0))
    elif m // 8 >= MXU_ROWS:
        bm, bn = m // 8, n
        grid = (8, 1)
        out_spec = pl.BlockSpec((None, None, bm, bn), lambda i, j: (i // 2, i % 2, 0, 0))
    else:
        bm, bn = m // N_CHIPS, n
        grid = (N_CHIPS, 1)
        out_spec = pl.BlockSpec((None, 2, bm // 2, bn), lambda i, j: (i, 0, 0, 0))

    def body(a_ref, b_ref, o_ref):
        acc = jnp.zeros((bm, bn), F32)
        for k0 in range(0, tokens, kc):
            acc = acc + _dot_tn(a_ref[k0:k0 + kc, :].astype(BF16), b_ref[k0:k0 + kc, :].astype(BF16))
        if len(o_ref.shape) == 3:
            o_ref[0] = acc[0:bm // 2]
            o_ref[1] = acc[bm // 2:bm]
        else:
            o_ref[...] = acc

    out_rows = m // 2 if col_shards else m // 8
    outs, routs = _pallas(
        body, [a, b], name=name, grid=grid,
        in_specs=[pl.BlockSpec((tokens, bm), lambda i, j: (0, i)), pl.BlockSpec((tokens, bn), lambda i, j: (0, j))],
        out_specs=[out_spec], out_shape=[jax.ShapeDtypeStruct((N_CHIPS, 2, out_rows, bn), F32)],
        vmem_mib=56, riders=riders)
    return outs[0], routs


class _GradReduce:
    def __init__(self, name, grad):
        self.name, self.grad = name, grad
        self.from_sibling = self.chip_sum = self.from_chips = self.full = None

    def pair_swap(self):
        return _PairSwap([self.grad])

    def took_pair(self, outs):
        self.chip_sum = _in_hbm(_add_pair(self.grad, outs[0], f"pair_sum_{self.name}"))

    def chip_swap(self):
        return _ChipSwap([self.chip_sum])

    def took_chips(self, outs):
        self.full = _in_hbm(_add_chips(self.chip_sum, outs[0], f"chip_sum_{self.name}"))

    def pair_share(self):
        return _PairShare([self.full])

    def took_share(self, outs):
        self.full = outs[0]

    def reduced(self):
        return jnp.reshape(self.full, (2 * self.full.shape[1], self.full.shape[2]))


def _forward_backward(x2, tgt2, gathered, staged, conv_a_w, conv_b_w, od_norm, od_bias, od_lng, od_lnb,
                      ev_norm_g, ev_conv_a_b, ev_ln_a_g, ev_ln_a_b, od_w_s, od_b_s, mlp_norm_g, final_norm_g,
                      *, tm, seq, distributed=True):
    d = x2.shape[1]
    w = dict(gathered)
    b_s_rows = jnp.broadcast_to(od_b_s[0][:, :, None], (C_GROUPS, CHUNK, CHUNK))

    def ride(*names):
        return [_Gather([staged[nm] for nm in names])] if distributed else []

    def land(routs, *names):
        if distributed:
            for nm, buf in zip(names, routs[0]):
                w[nm] = buf

    def as_cols(buf):
        return jnp.reshape(buf, (N_CHIPS, 2 * buf.shape[2], buf.shape[3]))

    def as_rows(buf):
        return jnp.reshape(buf, (8 * buf.shape[2], buf.shape[3]))

    (h1, n0, z, a2, cv, mix), routs = _fwd_even(
        x2, ev_norm_g, as_cols(w["ev_in"]), conv_a_w, ev_conv_a_b, ev_ln_a_g, ev_ln_a_b, conv_b_w, as_rows(w["ev_out"]),
        tm=tm, seq=seq, riders=ride("w1_0", "w2_0"))
    land(routs, "w1_0", "w2_0")
    (h2, n1, p0, q0), routs = _fwd_mlp(h1, mlp_norm_g[0:1], as_cols(w["w1_0"]), as_cols(w["w2_0"]), 0, tm=tm,
                                       riders=ride("od_in", "od_out", "w1_1"))
    land(routs, "od_in", "od_out", "w1_1")
    (h3, n2, s, cdf, sv, y), routs = _fwd_odd(h2, od_norm, as_cols(w["od_in"]), od_bias, od_lng, od_lnb, od_w_s[0], b_s_rows,
                                         as_rows(w["od_out"]), tm=tm, riders=ride("w2_1"))
    land(routs, "w2_1")
    (n3, p1, q1, loss_part, dh4, dh4b, d_final_g), _ = _fwd_mlp(
        h3, mlp_norm_g[1:2], as_cols(w["w1_1"]), as_cols(w["w2_1"]), 1, tm=tm,
        head=(jnp.reshape(final_norm_g, (1, d)), tgt2))

    red = {}

    def swap(*names):
        return [red[nm].pair_swap() for nm in names] if distributed else []

    def chips(*names):
        return [red[nm].chip_swap() for nm in names] if distributed else []

    def share(*names):
        return [red[nm].pair_share() for nm in names] if distributed else []

    def took(routs, *steps):
        if distributed:
            for (nm, what), outs in zip(steps, routs):
                getattr(red[nm], what)(outs)

    g, _ = _wgrad(q1, dh4b, "wgrad_w2_1", col_shards=False)
    red["w2_1"] = _GradReduce("w2_1", g)
    (dh3, dh3b, dp1, d_mlp_g1), routs = _bwd_mlp(dh4, h3, mlp_norm_g[1:2], p1, as_cols(w["w1_1"]), as_cols(w["w2_1"]), 1, tm=tm,
                                           riders=swap("w2_1"))
    took(routs, ("w2_1", "took_pair"))
    g, _ = _wgrad(n3, dp1, "wgrad_w1_1", col_shards=True)
    red["w1_1"] = _GradReduce("w1_1", g)
    g, routs = _wgrad(y, dh3b, "wgrad_od_out", col_shards=False, riders=swap("w1_1"))
    red["od_out"] = _GradReduce("od_out", g)
    took(routs, ("w1_1", "took_pair"))
    (dh2, dh2b, ds, d_od_norm, d_od_bin, d_od_lng, d_od_lnb, d_ws, d_bs), routs = _bwd_odd(
        dh3, h2, od_norm, s, cdf, sv, as_cols(w["od_in"]), od_lng, od_lnb, od_w_s[0], as_rows(w["od_out"]), tm=tm,
        riders=chips("w2_1") + swap("od_out"))
    took(routs, ("w2_1", "took_chips"), ("od_out", "took_pair"))
    g, routs = _wgrad(n2, ds, "wgrad_od_in", col_shards=True, riders=share("w2_1"))
    red["od_in"] = _GradReduce("od_in", g)
    took(routs, ("w2_1", "took_share"))
    half_groups = C_GROUPS // 2
    early = {"loss": loss_part, "od_w_s_lo": d_ws[:half_groups], "od_b_s": d_bs, "mlp_norm_g1": d_mlp_g1, "final_norm_g": d_final_g,
             "od_norm_g": d_od_norm, "od_b_in": d_od_bin, "od_ln_v_g": d_od_lng, "od_ln_v_b": d_od_lnb}
    share_early = [_ShareAll(list(early.values()))] if distributed else []
    g, routs = _wgrad(q0, dh2b, "wgrad_w2_0", col_shards=False, riders=swap("od_in") + share_early)
    red["w2_0"] = _GradReduce("w2_0", g)
    took(routs, ("od_in", "took_pair"))
    landed_early = routs[1] if distributed else []
    (dh1, dh1b, dp0, d_mlp_g0), routs = _bwd_mlp(dh2, h1, mlp_norm_g[0:1], p0, as_cols(w["w1_0"]), as_cols(w["w2_0"]), 0, tm=tm,
                                           riders=chips("w1_1") + chips("od_out") + chips("od_in") + swap("w2_0"))
    took(routs, ("w1_1", "took_chips"), ("od_out", "took_chips"), ("od_in", "took_chips"), ("w2_0", "took_pair"))
    middle = {"od_w_s_hi": d_ws[half_groups:]}
    share_middle = [_ShareAll(list(middle.values()))] if distributed else []
    g, routs = _wgrad(n1, dp0, "wgrad_w1_0", col_shards=True,
                      riders=share("w1_1") + share("od_out") + share("od_in") + share_middle)
    red["w1_0"] = _GradReduce("w1_0", g)
    took(routs, ("w1_1", "took_share"), ("od_out", "took_share"), ("od_in", "took_share"))
    landed_middle = routs[3] if distributed else []
    g, routs = _wgrad(mix, dh1b, "wgrad_ev_out", col_shards=False, riders=swap("w1_0"))
    red["ev_out"] = _GradReduce("ev_out", g)
    took(routs, ("w1_0", "took_pair"))

    (dx, dz, d_ev_norm, d_caw, d_cab, d_ev_lng, d_ev_lnb, d_cbw), routs = _bwd_even(
        dh1, x2, ev_norm_g, z, a2, cv, as_cols(w["ev_in"]), conv_a_w, ev_ln_a_g, ev_ln_a_b, conv_b_w, as_rows(w["ev_out"]),
        tm=tm, seq=seq, riders=chips("w2_0") + chips("w1_0") + swap("ev_out"))
    took(routs, ("w2_0", "took_chips"), ("w1_0", "took_chips"), ("ev_out", "took_pair"))
    late = {"mlp_norm_g0": d_mlp_g0, "ev_norm_g": d_ev_norm, "ev_conv_a_b": d_cab, "ev_ln_a_g": d_ev_lng,
            "ev_ln_a_b": d_ev_lnb, "ev_conv_a_w": d_caw, "ev_conv_b_w": d_cbw}
    share_late = [_ShareAll(list(late.values()))] if distributed else []
    g, routs2 = _wgrad(n0, dz, "wgrad_ev_in", col_shards=True,
                       riders=chips("ev_out") + share("w2_0") + share("w1_0") + share_late)
    red["ev_in"] = _GradReduce("ev_in", g)
    took(routs2, ("ev_out", "took_chips"), ("w2_0", "took_share"), ("w1_0", "took_share"))
    own = {**early, **middle, **late}
    landed = dict(zip(own.keys(), landed_early + landed_middle + routs2[3])) if distributed else None
    return dx, red, own, landed


def _rows128(a):
    rows = jnp.reshape(a, (-1, LANES))
    pad = (-rows.shape[0]) % SUBLANES
    return jnp.pad(rows, ((0, pad), (0, 0))) if pad else rows


def _pack(arrays):
    return jnp.concatenate([_rows128(a) for a in arrays], axis=0)


def _unpack(buf, shapes):
    out, r0 = [], 0
    for shp in shapes:
        size = 1
        for dim in shp:
            size *= dim
        nr = size // LANES
        out.append(jnp.reshape(buf[r0:r0 + nr], shp))
        r0 += nr + (-nr) % SUBLANES
    return out


def kernel(x, ev_norm_g, ev_w_in, ev_conv_a_w, ev_conv_a_b, ev_ln_a_g, ev_ln_a_b, ev_conv_b_w, ev_w_out, od_norm_g, od_w_in, od_b_in, od_ln_v_g, od_ln_v_b, od_w_s, od_b_s, od_w_out, mlp_norm_g, mlp_w1, mlp_w2, final_norm_g, loss_target, m_ev_norm_g, m_ev_w_in, m_ev_conv_a_w, m_ev_conv_a_b, m_ev_ln_a_g, m_ev_ln_a_b, m_ev_conv_b_w, m_ev_w_out, m_od_norm_g, m_od_w_in, m_od_b_in, m_od_ln_v_g, m_od_ln_v_b, m_od_w_s, m_od_b_s, m_od_w_out, m_mlp_norm_g, m_mlp_w1, m_mlp_w2, m_final_norm_g, v_ev_norm_g, v_ev_w_in, v_ev_conv_a_w, v_ev_conv_a_b, v_ev_ln_a_g, v_ev_ln_a_b, v_ev_conv_b_w, v_ev_w_out, v_od_norm_g, v_od_w_in, v_od_b_in, v_od_ln_v_g, v_od_ln_v_b, v_od_w_s, v_od_b_s, v_od_w_out, v_mlp_norm_g, v_mlp_w1, v_mlp_w2, v_final_norm_g):
    tm = TOKEN_TILE
    batch, seq, d = x.shape
    tokens = batch * seq
    x2 = jnp.reshape(x, (tokens, d))
    tgt2 = jnp.reshape(loss_target, (tokens, d))
    chip = 2 * lax.axis_index("x") + lax.axis_index("y")

    small_shapes = [(A_CONV_WIDTH, LANES), (B_CONV_WIDTH, LANES), (256,), (512,), (256,), (256,)]
    small_shard = _pack([ev_conv_a_w[0], ev_conv_b_w[0], od_norm_g[0], od_b_in[0], od_ln_v_g[0], od_ln_v_b[0]])
    small_shard = jnp.pad(small_shard, ((0, (-small_shard.shape[0]) % (2 * SUBLANES)), (0, 0)))
    first = [_place_shard(ev_w_in, 0, BF16, "place_ev_w_in"), _place_shard(ev_w_out, 0, BF16, "place_ev_w_out"),
             _place_shard(small_shard[None], 0, F32, "place_small")]
    staged = {
        "w1_0": _place_shard(mlp_w1, 0, BF16, "place_w1_0"), "w2_0": _place_shard(mlp_w2, 0, BF16, "place_w2_0"),
        "od_in": _place_shard(od_w_in, 0, BF16, "place_od_w_in"), "od_out": _place_shard(od_w_out, 0, BF16, "place_od_w_out"),
        "w1_1": _place_shard(mlp_w1, 1, BF16, "place_w1_1"), "w2_1": _place_shard(mlp_w2, 1, BF16, "place_w2_1"),
    }
    first = [_in_hbm(a) for a in first]
    staged = {nm: _in_hbm(a) for nm, a in staged.items()}
    (g_ev_in, g_ev_out, g_small), = _exchange([_Gather(first)], "gather_first")
    small_all = jnp.reshape(g_small, (N_CHIPS, -1, LANES))
    per_chip = [_unpack(small_all[q], small_shapes) for q in range(N_CHIPS)]
    conv_a_w = jnp.concatenate([pc[0] for pc in per_chip], axis=1)
    conv_b_w = jnp.concatenate([pc[1] for pc in per_chip], axis=1)
    od_norm = jnp.concatenate([pc[2] for pc in per_chip])[None, :]
    od_bias = jnp.concatenate([pc[3] for pc in per_chip])[None, :]
    od_lng = jnp.concatenate([pc[4] for pc in per_chip])[None, :]
    od_lnb = jnp.concatenate([pc[5] for pc in per_chip])[None, :]

    dx, red, own, landed = _forward_backward(
        x2, tgt2, {"ev_in": g_ev_in, "ev_out": g_ev_out}, staged, conv_a_w, conv_b_w, od_norm, od_bias, od_lng, od_lnb,
        ev_norm_g, ev_conv_a_b, ev_ln_a_g, ev_ln_a_b, od_w_s, od_b_s, mlp_norm_g, final_norm_g, tm=tm, seq=seq)

    routs = _exchange([red["ev_in"].pair_swap(), red["ev_out"].pair_share()], "reduce_tail_1")
    red["ev_in"].took_pair(routs[0])
    red["ev_out"].took_share(routs[1])
    routs = _exchange([red["ev_in"].chip_swap()], "reduce_tail_2")
    red["ev_in"].took_chips(routs[0])
    routs = _exchange([red["ev_in"].pair_share()], "reduce_tail_3")
    red["ev_in"].took_share(routs[0])

    given = {"ev_norm_g": (ev_norm_g, m_ev_norm_g, v_ev_norm_g), "ev_conv_a_b": (ev_conv_a_b, m_ev_conv_a_b, v_ev_conv_a_b),
             "ev_ln_a_g": (ev_ln_a_g, m_ev_ln_a_g, v_ev_ln_a_g), "ev_ln_a_b": (ev_ln_a_b, m_ev_ln_a_b, v_ev_ln_a_b),
             "od_w_s": (od_w_s, m_od_w_s, v_od_w_s), "od_b_s": (od_b_s, m_od_b_s, v_od_b_s),
             "mlp_norm_g": (mlp_norm_g, m_mlp_norm_g, v_mlp_norm_g), "final_norm_g": (final_norm_g, m_final_norm_g, v_final_norm_g),
             "ev_conv_a_w": (ev_conv_a_w, m_ev_conv_a_w, v_ev_conv_a_w), "ev_conv_b_w": (ev_conv_b_w, m_ev_conv_b_w, v_ev_conv_b_w),
             "od_norm_g": (od_norm_g, m_od_norm_g, v_od_norm_g), "od_b_in": (od_b_in, m_od_b_in, v_od_b_in),
             "od_ln_v_g": (od_ln_v_g, m_od_ln_v_g, v_od_ln_v_g), "od_ln_v_b": (od_ln_v_b, m_od_ln_v_b, v_od_ln_v_b)}
    shaped = {nm: tuple(jnp.reshape(a, shape) for a in given[nm]) for nm, shape, _, _ in SMALL_WEIGHTS}
    loss11, small_upd = _small_update(own, landed, shaped)
    loss = loss11[0, 0]
    upd = {nm: [jnp.reshape(o, given[nm][0].shape) for o in outs] for nm, outs in small_upd.items()}

    def big_update(wt, m, v, names, call):
        grads = [red[nm].reduced() for nm in names]
        shp3 = (len(grads),) + grads[0].shape
        outs, _ = _adamw(jnp.reshape(wt, shp3), jnp.reshape(m, shp3), jnp.reshape(v, shp3), grads, call)
        return [jnp.reshape(o, wt.shape) for o in outs], None

    upd["mlp_w2"], _ = big_update(mlp_w2, m_mlp_w2, v_mlp_w2, ["w2_0", "w2_1"], "adamw_mlp_w2")
    upd["mlp_w1"], _ = big_update(mlp_w1, m_mlp_w1, v_mlp_w1, ["w1_0", "w1_1"], "adamw_mlp_w1")
    upd["ev_w_in"], _ = big_update(ev_w_in, m_ev_w_in, v_ev_w_in, ["ev_in"], "adamw_ev_w_in")
    upd["ev_w_out"], _ = big_update(ev_w_out, m_ev_w_out, v_ev_w_out, ["ev_out"], "adamw_ev_w_out")
    upd["od_w_in"], _ = big_update(od_w_in, m_od_w_in, v_od_w_in, ["od_in"], "adamw_od_w_in")
    upd["od_w_out"], _ = big_update(od_w_out, m_od_w_out, v_od_w_out, ["od_out"], "adamw_od_w_out")

    order = ["ev_norm_g", "ev_w_in", "ev_conv_a_w", "ev_conv_a_b", "ev_ln_a_g", "ev_ln_a_b", "ev_conv_b_w", "ev_w_out",
             "od_norm_g", "od_w_in", "od_b_in", "od_ln_v_g", "od_ln_v_b", "od_w_s", "od_b_s", "od_w_out", "mlp_norm_g",
             "mlp_w1", "mlp_w2", "final_norm_g"]
    grad_x = jnp.reshape(dx, x.shape)
    return (loss, grad_x, *[upd[nm][0] for nm in order], *[upd[nm][1] for nm in order],
            *[upd[nm][2] for nm in order], *[upd[nm][3] for nm in order])
```

```python
import functools

import jax
import jax.numpy as jnp
from jax import lax
from jax.experimental import pallas as pl
from jax.experimental.pallas import tpu as pltpu

F32 = jnp.float32
BF16 = jnp.bfloat16

D_MODEL = 1024
A_DIM = 512
B_DIM = 512
IN_EVEN = 2 * A_DIM + 3 * B_DIM
A_CONV_WIDTH = 31
B_CONV_WIDTH = 3
CHUNK = 128
C_GROUPS = 8
C_DIM = 1024
D_FF = 4096
RMS_EPS = 1e-6
LN_EPS = 1e-5
ADAM_LR = 0.001
ADAM_B1 = 0.9
ADAM_B2 = 0.999
ADAM_EPS = 1e-08
ADAM_WD = 0.01
ADAM_STEP = 10

N_CHIPS = 4
N_DEV = 8
TOKEN_TILE = 512
A_HALO = 32
B_HALO = 8
CONV_ROWS = 16
PAIR = 2 * CHUNK
LANES = 128
SUBLANES = 8
MXU_ROWS = 256
MIB = 1024 * 1024
MESH = pl.DeviceIdType.MESH
ANY = pl.BlockSpec(memory_space=pl.ANY)


def _dot(a, b):
    return lax.dot_general(a, b, (((1,), (0,)), ((), ())), preferred_element_type=F32)


def _dot_nt(a, b):
    return lax.dot_general(a, b, (((1,), (1,)), ((), ())), preferred_element_type=F32)


def _dot_tn(a, b):
    return lax.dot_general(a, b, (((0,), (0,)), ((), ())), preferred_element_type=F32)


def _params(vmem_mib, n_axes=1):
    return pltpu.CompilerParams(dimension_semantics=("arbitrary",) * n_axes, vmem_limit_bytes=vmem_mib * MIB)


def _row_spec(tm, cols, rev_nt=None):
    if rev_nt is None:
        return pl.BlockSpec((tm, cols), lambda i: (i, 0))
    return pl.BlockSpec((tm, cols), lambda i: (rev_nt - 1 - i, 0))


def _full_spec(shape):
    nd = len(shape)
    return pl.BlockSpec(shape, lambda i: (0,) * nd)


def _block_rows(rows, cap=512):
    best = SUBLANES
    for br in range(SUBLANES, min(rows, cap) + 1, SUBLANES):
        if rows % br == 0:
            best = br
    return best


N_LOADS = 2


def _load_weights(pairs, sems):
    @pl.when(pl.program_id(0) == 0)
    def _():
        copies = [pltpu.make_async_copy(src, dst, sems.at[k]) for k, (src, dst) in enumerate(pairs)]
        for cp in copies:
            cp.start()
        for cp in copies:
            cp.wait()


def _rms_fwd(x, g):
    rstd = lax.rsqrt(jnp.mean(x * x, axis=-1, keepdims=True) + RMS_EPS)
    return x * rstd * g, rstd


def _rms_bwd(dn, x, rstd, g):
    a = dn * g
    xh = x * rstd
    dx = rstd * (a - xh * jnp.mean(a * xh, axis=-1, keepdims=True))
    dg = jnp.sum(dn * xh, axis=0, keepdims=True)
    return dx, dg


def _ln_stats(v):
    mu = jnp.mean(v, axis=-1, keepdims=True)
    xc = v - mu
    rs = lax.rsqrt(jnp.mean(xc * xc, axis=-1, keepdims=True) + LN_EPS)
    return xc * rs, rs


def _ln_bwd(dy, xhat, rs, g):
    dxh = dy * g
    dv = rs * (dxh - jnp.mean(dxh, axis=-1, keepdims=True) - xhat * jnp.mean(dxh * xhat, axis=-1, keepdims=True))
    return dv, jnp.sum(dy * xhat, axis=0, keepdims=True), jnp.sum(dy, axis=0, keepdims=True)


def _gelu_cdf(s):
    return 0.5 * (1.0 + lax.erf(s * 0.7071067811865476))


def _mesh_pos():
    return lax.axis_index("x"), lax.axis_index("y"), lax.axis_index("c")


def _other_chips(x, y):
    return [(1 - x, y), (x, 1 - y), (1 - x, 1 - y)]


def _remote(src, dst, send_sem, recv_sem, to):
    return pltpu.make_async_remote_copy(src_ref=src, dst_ref=dst, send_sem=send_sem, recv_sem=recv_sem,
                                        device_id=to, device_id_type=MESH)


def _like(arrays):
    return [jax.ShapeDtypeStruct(a.shape, a.dtype) for a in arrays]


class _Gather:
    def __init__(self, bufs):
        self.ins = list(bufs)
        self.out_shapes = _like(bufs)
        self.aliases = {t: t for t in range(len(bufs))}
        self.n_sems = 6 * len(bufs)

    def _ici(self, ins, outs, send, recv, t, k, chip, mine, c):
        return _remote(ins[t].at[mine, c], outs[t].at[mine, c], send.at[6 * t + k], recv.at[6 * t + k], (*chip, c))

    def start(self, ins, outs, send, recv):
        x, y, c = _mesh_pos()
        for t in range(len(ins)):
            for k, chip in enumerate(_other_chips(x, y)):
                self._ici(ins, outs, send, recv, t, k, chip, 2 * x + y, c).start()

    def finish(self, ins, outs, send, recv):
        x, y, c = _mesh_pos()
        me, sibling = (x, y, c), (x, y, 1 - c)
        chips = _other_chips(x, y)
        passed = []
        for t in range(len(ins)):
            for k, chip in enumerate(chips):
                blk = outs[t].at[2 * chip[0] + chip[1], c]
                _remote(blk, blk, send.at[6 * t + k], recv.at[6 * t + k], me).wait_recv()
                cp = _remote(blk, blk, send.at[6 * t + 3 + k], recv.at[6 * t + 3 + k], sibling)
                cp.start()
                passed.append(cp)
        for t in range(len(ins)):
            for k, chip in enumerate(chips):
                blk = outs[t].at[2 * chip[0] + chip[1], 1 - c]
                _remote(blk, blk, send.at[6 * t + 3 + k], recv.at[6 * t + 3 + k], me).wait_recv()
        for t in range(len(ins)):
            for k, chip in enumerate(chips):
                self._ici(ins, outs, send, recv, t, k, chip, 2 * x + y, c).wait_send()
        for cp in passed:
            cp.wait_send()


class _PairSwap:
    def __init__(self, grads):
        self.ins = list(grads)
        self.out_shapes = [jax.ShapeDtypeStruct((g.shape[0],) + g.shape[2:], g.dtype) for g in grads]
        self.aliases = {}
        self.n_sems = len(grads)

    def _copies(self, ins, outs, send, recv):
        x, y, c = _mesh_pos()
        return [_remote(ins[t].at[:, 1 - c], outs[t], send.at[t], recv.at[t], (x, y, 1 - c)) for t in range(len(ins))]

    def start(self, ins, outs, send, recv):
        for cp in self._copies(ins, outs, send, recv):
            cp.start()

    def finish(self, ins, outs, send, recv):
        for cp in self._copies(ins, outs, send, recv):
            cp.wait()


class _ChipSwap:
    def __init__(self, parts):
        self.ins = list(parts)
        self.out_shapes = [jax.ShapeDtypeStruct((3,) + p.shape[1:], p.dtype) for p in parts]
        self.aliases = {}
        self.n_sems = 3 * len(parts)

    def _copies(self, ins, outs, send, recv):
        x, y, c = _mesh_pos()
        return [_remote(ins[t].at[2 * chip[0] + chip[1]], outs[t].at[k], send.at[3 * t + k], recv.at[3 * t + k], (*chip, c))
                for t in range(len(ins)) for k, chip in enumerate(_other_chips(x, y))]

    def start(self, ins, outs, send, recv):
        for cp in self._copies(ins, outs, send, recv):
            cp.start()

    def finish(self, ins, outs, send, recv):
        for cp in self._copies(ins, outs, send, recv):
            cp.wait()


class _PairShare:
    def __init__(self, fulls):
        self.ins = list(fulls)
        self.out_shapes = _like(fulls)
        self.aliases = {t: t for t in range(len(fulls))}
        self.n_sems = len(fulls)

    def _copies(self, ins, outs, send, recv):
        x, y, c = _mesh_pos()
        return [_remote(ins[t].at[c], outs[t].at[c], send.at[t], recv.at[t], (x, y, 1 - c)) for t in range(len(ins))]

    def start(self, ins, outs, send, recv):
        for cp in self._copies(ins, outs, send, recv):
            cp.start()

    def finish(self, ins, outs, send, recv):
        for cp in self._copies(ins, outs, send, recv):
            cp.wait()


class _ShareAll:
    def __init__(self, arrays):
        self.ins = list(arrays)
        self.out_shapes = [jax.ShapeDtypeStruct((N_DEV,) + a.shape, a.dtype) for a in arrays]
        self.aliases = {}
        self.n_sems = (N_DEV - 1) * len(arrays)

    def _peers(self):
        x, y, c = _mesh_pos()
        flips = [((r >> 2) & 1, (r >> 1) & 1, r & 1) for r in range(1, N_DEV)]
        return (x, y, c), [(x ^ fx, y ^ fy, c ^ fc) for fx, fy, fc in flips]

    def _sends(self, ins, outs, send, recv):
        (x, y, c), peers = self._peers()
        mine = 4 * x + 2 * y + c
        return [_remote(ins[a], outs[a].at[mine], send.at[7 * a + r], recv.at[7 * a + r], peer)
                for a in range(len(ins)) for r, peer in enumerate(peers)]

    def start(self, ins, outs, send, recv):
        for cp in self._sends(ins, outs, send, recv):
            cp.start()

    def finish(self, ins, outs, send, recv):
        (x, y, c), peers = self._peers()
        for a in range(len(ins)):
            for r, (px, py, pc) in enumerate(peers):
                blk = outs[a].at[4 * px + 2 * py + pc]
                _remote(blk, blk, send.at[7 * a + r], recv.at[7 * a + r], (x, y, c)).wait_recv()
        for cp in self._sends(ins, outs, send, recv):
            cp.wait_send()


def _pallas(body, operands, *, name, grid, in_specs, out_specs, out_shape, scratch_shapes=(), vmem_mib=32, riders=(),
            prefetch=None):
    in_specs, out_specs, out_shape, scratch_shapes = list(in_specs), list(out_specs), list(out_shape), list(scratch_shapes)
    if not riders and prefetch is None:
        outs = pl.pallas_call(body, name=name, grid=grid, in_specs=in_specs, out_specs=out_specs, out_shape=out_shape,
                              scratch_shapes=scratch_shapes, compiler_params=_params(vmem_mib, len(grid)))(*operands)
        return list(outs), []
    n_in, n_out, n_scr = len(in_specs), len(out_specs), len(scratch_shapes)
    r_in = [len(r.ins) for r in riders]
    r_out = [len(r.out_shapes) for r in riders]
    steps = 1
    for g in grid:
        steps *= g

    n_pre = 0 if prefetch is None else 1

    def wrapped(*refs):
        refs = list(refs)
        pre, refs = refs[:n_pre], refs[n_pre:]
        ins, refs = refs[:n_in], refs[n_in:]
        rins = []
        for k in r_in:
            rins.append(refs[:k])
            refs = refs[k:]
        outs, refs = refs[:n_out], refs[n_out:]
        routs = []
        for k in r_out:
            routs.append(refs[:k])
            refs = refs[k:]
        scr, sems = refs[:n_scr], refs[n_scr:]
        step = 0
        for ax, g in enumerate(grid):
            step = step * g + pl.program_id(ax)

        def each(what):
            for j, r in enumerate(riders):
                getattr(r, what)(rins[j], routs[j], sems[2 * j], sems[2 * j + 1])

        if grid:
            pl.when(step == 0)(lambda: each("start"))
        else:
            each("start")
        body(*pre, *ins, *outs, *scr)
        if grid:
            pl.when(step == steps - 1)(lambda: each("finish"))
        else:
            each("finish")

    aliases, off_in, off_out = {}, n_pre + n_in, n_out
    for r, ki, ko in zip(riders, r_in, r_out):
        for i, o in r.aliases.items():
            aliases[off_in + i] = off_out + o
        off_in, off_out = off_in + ki, off_out + ko
    sems = []
    for r in riders:
        sems += [pltpu.SemaphoreType.DMA((r.n_sems,)), pltpu.SemaphoreType.DMA((r.n_sems,))]
    layout = dict(grid=grid, in_specs=in_specs + [ANY] * sum(r_in), out_specs=out_specs + [ANY] * sum(r_out),
                  scratch_shapes=scratch_shapes + sems)
    if prefetch is not None:
        layout = dict(grid_spec=pltpu.PrefetchScalarGridSpec(num_scalar_prefetch=1, **layout))
    res = pl.pallas_call(
        wrapped, name=name, **layout,
        out_shape=out_shape + [s for r in riders for s in r.out_shapes], input_output_aliases=aliases,
        compiler_params=pltpu.CompilerParams(dimension_semantics=("arbitrary",) * len(grid),
                                             vmem_limit_bytes=vmem_mib * MIB, has_side_effects=True),
    )(*([] if prefetch is None else [prefetch]), *operands, *[a for r in riders for a in r.ins])
    res = list(res)
    outs, res = res[:n_out], res[n_out:]
    routs = []
    for k in r_out:
        routs.append(res[:k])
        res = res[k:]
    return outs, routs


def _exchange(riders, name):
    return _pallas(lambda: None, [], name=name, grid=(), in_specs=[], out_specs=[], out_shape=[], riders=riders)[1]


def _in_hbm(a):
    return pltpu.with_memory_space_constraint(a, pltpu.HBM)


def _place_shard(w, layer, dtype, name):
    _, rows, cols = w.shape
    half = rows // 2
    br = _block_rows(half)
    nb = half // br
    mine = 2 * lax.axis_index("x") + lax.axis_index("y")

    def body(q_ref, w_ref, o_ref):
        o_ref[...] = w_ref[...].astype(dtype)

    return pl.pallas_call(
        body, name=name,
        grid_spec=pltpu.PrefetchScalarGridSpec(
            num_scalar_prefetch=1, grid=(2, nb),
            in_specs=[pl.BlockSpec((None, br, cols), lambda h, i, q: (layer, h * nb + i, 0))],
            out_specs=pl.BlockSpec((None, None, br, cols), lambda h, i, q: (q[0], h, i, 0))),
        out_shape=jax.ShapeDtypeStruct((N_CHIPS, 2, half, cols), dtype),
        compiler_params=_params(16, 2),
    )(jnp.reshape(mine, (1,)).astype(jnp.int32), w)


def _add_pair(g, recv, name):
    _, _, r, cdim = g.shape
    br = _block_rows(r, 256)
    c = lax.axis_index("c")

    def body(c_ref, g_ref, r_ref, o_ref):
        o_ref[...] = (g_ref[...] + r_ref[...]).astype(BF16)

    return pl.pallas_call(
        body, name=name,
        grid_spec=pltpu.PrefetchScalarGridSpec(
            num_scalar_prefetch=1, grid=(N_CHIPS, r // br),
            in_specs=[pl.BlockSpec((None, None, br, cdim), lambda q, i, c_ref: (q, c_ref[0], i, 0)),
                      pl.BlockSpec((None, br, cdim), lambda q, i, c_ref: (q, i, 0))],
            out_specs=pl.BlockSpec((None, br, cdim), lambda q, i, c_ref: (q, i, 0))),
        out_shape=jax.ShapeDtypeStruct((N_CHIPS, r, cdim), BF16),
        compiler_params=_params(16, 2),
    )(jnp.reshape(c, (1,)).astype(jnp.int32), _in_hbm(g), _in_hbm(recv))


def _add_chips(own, recv, name):
    _, r, cdim = own.shape
    br = _block_rows(r, 256)
    x, y, c = _mesh_pos()

    def body(pos_ref, own_ref, r_ref, o_ref):
        acc = own_ref[...].astype(F32)
        for k in range(3):
            acc = acc + r_ref[k].astype(F32)
        o_ref[...] = acc

    return pl.pallas_call(
        body, name=name,
        grid_spec=pltpu.PrefetchScalarGridSpec(
            num_scalar_prefetch=1, grid=(r // br,),
            in_specs=[pl.BlockSpec((None, br, cdim), lambda i, pos: (pos[0], i, 0)),
                      pl.BlockSpec((3, br, cdim), lambda i, pos: (0, i, 0))],
            out_specs=pl.BlockSpec((None, br, cdim), lambda i, pos: (pos[1], i, 0))),
        out_shape=jax.ShapeDtypeStruct((2, r, cdim), F32),
        compiler_params=_params(16, 1),
    )(jnp.stack([2 * x + y, c]).astype(jnp.int32), _in_hbm(own), _in_hbm(recv))


def _adam_math(w, m, v, g):
    c1 = 1.0 / (1.0 - ADAM_B1 ** ADAM_STEP)
    c2 = 1.0 / (1.0 - ADAM_B2 ** ADAM_STEP)
    m_new = ADAM_B1 * m + (1.0 - ADAM_B1) * g
    v_new = ADAM_B2 * v + (1.0 - ADAM_B2) * (g * g)
    return -ADAM_LR * ((m_new * c1) / (jnp.sqrt(v_new * c2) + ADAM_EPS) + ADAM_WD * w), m_new, v_new


SMALL_WEIGHTS = [
    ("ev_norm_g", (1, D_MODEL), ["ev_norm_g"], None), ("ev_conv_a_b", (1, A_DIM), ["ev_conv_a_b"], None),
    ("ev_ln_a_g", (1, A_DIM), ["ev_ln_a_g"], None), ("ev_ln_a_b", (1, A_DIM), ["ev_ln_a_b"], None),
    ("od_w_s", (C_GROUPS, CHUNK, CHUNK), ["od_w_s_lo", "od_w_s_hi"], None), ("od_b_s", (C_GROUPS, CHUNK), ["od_b_s"], None),
    ("mlp_norm_g", (2, D_MODEL), ["mlp_norm_g0", "mlp_norm_g1"], None), ("final_norm_g", (1, D_MODEL), ["final_norm_g"], None),
    ("ev_conv_a_w", (A_CONV_WIDTH, A_DIM // N_CHIPS), ["ev_conv_a_w"], A_DIM // N_CHIPS),
    ("ev_conv_b_w", (B_CONV_WIDTH, B_DIM // N_CHIPS), ["ev_conv_b_w"], B_DIM // N_CHIPS),
    ("od_norm_g", (1, D_MODEL // N_CHIPS), ["od_norm_g"], D_MODEL // N_CHIPS),
    ("od_b_in", (1, 2 * C_DIM // N_CHIPS), ["od_b_in"], 2 * C_DIM // N_CHIPS),
    ("od_ln_v_g", (1, C_DIM // N_CHIPS), ["od_ln_v_g"], C_DIM // N_CHIPS),
    ("od_ln_v_b", (1, C_DIM // N_CHIPS), ["od_ln_v_b"], C_DIM // N_CHIPS),
]


def _small_update(own, landed, weights):
    names = list(own.keys())
    n_g, n_w = len(names), len(SMALL_WEIGHTS)

    def body(*refs):
        refs = list(refs)
        own_refs = dict(zip(names, refs[:n_g]))
        land_refs = dict(zip(names, refs[n_g:2 * n_g]))
        wmv = [refs[2 * n_g + 3 * i:2 * n_g + 3 * i + 3] for i in range(n_w)]
        o0 = 2 * n_g + 3 * n_w
        loss_ref = refs[o0]
        outs = [refs[o0 + 1 + 4 * i:o0 + 5 + 4 * i] for i in range(n_w)]
        acc = dict(zip(names, refs[o0 + 1 + 4 * n_w:]))
        x, y, c = _mesh_pos()
        mine, chip = 4 * x + 2 * y + c, 2 * x + y

        for nm in names:
            for d in range(N_DEV):
                def add(term, nm=nm, d=d):
                    acc[nm][...] = term if d == 0 else acc[nm][...] + term
                pl.when(mine == d)(lambda nm=nm, add=add: add(own_refs[nm][...]))
                pl.when(mine != d)(lambda nm=nm, d=d, add=add: add(land_refs[nm][d]))
        loss_ref[...] = acc["loss"][...]

        def update(i, rows, g):
            w_ref, m_ref, v_ref = wmv[i]
            delta, m_new, v_new = _adam_math(w_ref[rows], m_ref[rows], v_ref[rows], g)
            for ref, val in zip(outs[i], (g, delta, m_new, v_new)):
                ref[rows] = val

        for i, (_, shape, grads, per_chip) in enumerate(SMALL_WEIGHTS):
            for row, gname in enumerate(grads):
                per_grad = shape[0] // len(grads)
                rows = slice(row * per_grad, (row + 1) * per_grad)
                if per_chip is None:
                    update(i, rows, acc[gname][...])
                else:
                    for q in range(N_CHIPS):
                        pl.when(chip == q)(lambda i=i, rows=rows, gname=gname, q=q, per_chip=per_chip:
                                           update(i, rows, acc[gname][:, q * per_chip:(q + 1) * per_chip]))

    operands = [own[nm] for nm in names] + [landed[nm] for nm in names]
    for nm, _, _, _ in SMALL_WEIGHTS:
        operands += list(weights[nm])
    out_shape = [jax.ShapeDtypeStruct((1, 1), F32)]
    for _, shape, _, _ in SMALL_WEIGHTS:
        out_shape += [jax.ShapeDtypeStruct(shape, F32)] * 4
    res = pl.pallas_call(
        body, name="small_update", grid=(1,),
        in_specs=[_full_spec(a.shape) for a in operands], out_specs=[_full_spec(s.shape) for s in out_shape],
        out_shape=out_shape, scratch_shapes=[pltpu.VMEM(own[nm].shape, F32) for nm in names],
        compiler_params=_params(32, 1),
    )(*[_in_hbm(a) for a in operands])
    return res[0], {nm: res[1 + 4 * i:5 + 4 * i] for i, (nm, _, _, _) in enumerate(SMALL_WEIGHTS)}


def _adamw(w, m, v, grads, name, riders=()):
    layers, r, cdim = w.shape
    br = _block_rows(r, 256 if cdim > LANES else 1024)

    def body(*refs):
        w_ref, m_ref, v_ref = refs[:3]
        g_refs = refs[3:3 + layers]
        go_ref, d_ref, mo_ref, vo_ref = refs[3 + layers:]
        layer = pl.program_id(0)
        for l in range(layers):
            @pl.when(layer == l)
            def _(l=l):
                g = g_refs[l][...]
                go_ref[...] = g
                d_ref[...], mo_ref[...], vo_ref[...] = _adam_math(w_ref[...], m_ref[...], v_ref[...], g)

    spec3 = pl.BlockSpec((None, br, cdim), lambda l, i: (l, i, 0))
    spec2 = pl.BlockSpec((br, cdim), lambda l, i: (i, 0))
    out = jax.ShapeDtypeStruct((layers, r, cdim), F32)
    return _pallas(body, [w, m, v, *[_in_hbm(g) for g in grads]], name=name, grid=(layers, r // br),
                   in_specs=[spec3, spec3, spec3] + [spec2] * layers, out_specs=[spec3] * 4, out_shape=[out] * 4,
                   vmem_mib=32, riders=riders)


def _fill_shifted(buf, rows):
    for b in range(1, SUBLANES):
        buf[b, 0:rows - SUBLANES, :] = buf[0, b:b + rows - SUBLANES, :]


def _window(buf, start, size):
    return buf[start % SUBLANES, start - start % SUBLANES:start - start % SUBLANES + size, :]


def _conv31(src, w_ref, r0, base, init):
    acc = init
    for k in range(A_CONV_WIDTH):
        acc = acc + w_ref[k:k + 1, :] * _window(src, base + k + r0, CONV_ROWS)
    return acc


def _fwd_even(x, norm_g, w_in, conv_a_w, conv_a_b, ln_g, ln_b, conv_b_w, w_out, *, tm, seq, riders=()):
    tokens = x.shape[0]
    nt, tps = tokens // tm, seq // tm

    def body(x_ref, g_ref, win_hbm, caw_ref, cab_ref, lng_ref, lnb_ref, cbw_ref, wout_hbm,
             h_ref, n_ref, z_ref, a2_ref, cv_ref, mix_ref, win_v, wout_v, pa, pb, sem):
        i = pl.program_id(0)

        _load_weights([(win_hbm, win_v), (wout_hbm, wout_v)], sem)

        xv = x_ref[...]
        nf, _ = _rms_fwd(xv, g_ref[...])
        n = nf.astype(BF16)
        n_ref[...] = n
        z = jnp.concatenate([_dot(n, win_v[j]) for j in range(N_CHIPS)], axis=1)
        z_ref[...] = z.astype(BF16)
        a_val, a_gate = z[:, 0:A_DIM], z[:, A_DIM:2 * A_DIM]
        b_gate, c_gate, b_val = z[:, 1024:1536], z[:, 1536:2048], z[:, 2048:2560]

        first = (i % tps) == 0

        @pl.when(first)
        def _():
            pa[0, 0:A_HALO, :] = jnp.zeros((A_HALO, A_DIM), F32)
            pb[0:B_HALO, :] = jnp.zeros((B_HALO, B_DIM), F32)

        @pl.when(jnp.logical_not(first))
        def _():
            pa[0, 0:A_HALO, :] = pa[0, tm:tm + A_HALO, :]
            pb[0:B_HALO, :] = pb[tm:tm + B_HALO, :]

        pa[0, A_HALO:A_HALO + tm, :] = a_val * jax.nn.sigmoid(a_gate)
        pb[B_HALO:B_HALO + tm, :] = c_gate * b_val
        _fill_shifted(pa, A_HALO + tm)
        bias = jnp.broadcast_to(cab_ref[...], (CONV_ROWS, A_DIM))
        for r0 in range(0, tm, CONV_ROWS):
            a2_ref[r0:r0 + CONV_ROWS, :] = _conv31(pa, caw_ref, r0, A_HALO - (A_CONV_WIDTH - 1), bias)
        xhat, _ = _ln_stats(a2_ref[...])
        a3 = xhat * lng_ref[...] + lnb_ref[...]
        a4 = a3 * jax.nn.sigmoid(a3)
        cv = cbw_ref[0:1, :] * pb[B_HALO - 2:B_HALO - 2 + tm, :]
        cv = cv + cbw_ref[1:2, :] * pb[B_HALO - 1:B_HALO - 1 + tm, :]
        cv = cv + cbw_ref[2:3, :] * pb[B_HALO:B_HALO + tm, :]
        cv_ref[...] = cv.astype(BF16)
        mix = jnp.concatenate([a4, b_gate * cv], axis=1).astype(BF16)
        mix_ref[...] = mix
        h_ref[...] = xv + _dot(mix, wout_v[...])

    shp = lambda cols, dt: jax.ShapeDtypeStruct((tokens, cols), dt)
    return _pallas(
        body, [x, norm_g, w_in, conv_a_w, conv_a_b, ln_g, ln_b, conv_b_w, w_out], name="fwd_even", grid=(nt,),
        in_specs=[_row_spec(tm, D_MODEL), _full_spec((1, D_MODEL)), ANY, _full_spec((A_CONV_WIDTH, A_DIM)),
                  _full_spec((1, A_DIM)), _full_spec((1, A_DIM)), _full_spec((1, A_DIM)),
                  _full_spec((B_CONV_WIDTH, B_DIM)), ANY],
        out_specs=[_row_spec(tm, D_MODEL), _row_spec(tm, D_MODEL), _row_spec(tm, IN_EVEN), _row_spec(tm, A_DIM),
                   _row_spec(tm, B_DIM), _row_spec(tm, D_MODEL)],
        out_shape=[shp(D_MODEL, F32), shp(D_MODEL, BF16), shp(IN_EVEN, BF16), shp(A_DIM, F32), shp(B_DIM, BF16),
                   shp(D_MODEL, BF16)],
        scratch_shapes=[pltpu.VMEM((N_CHIPS, D_MODEL, IN_EVEN // N_CHIPS), BF16), pltpu.VMEM((D_MODEL, D_MODEL), BF16),
                        pltpu.VMEM((SUBLANES, A_HALO + tm, A_DIM), F32), pltpu.VMEM((B_HALO + tm, B_DIM), F32),
                        pltpu.SemaphoreType.DMA((N_LOADS,))],
        vmem_mib=56, riders=riders)


def _loss_tail(xv, g, target, loss_ref, dh_ref, dhb_ref, dg_ref):
    @pl.when(pl.program_id(0) == 0)
    def _():
        loss_ref[...] = jnp.zeros((1, 1), F32)
        dg_ref[...] = jnp.zeros((1, D_MODEL), F32)

    out, rstd = _rms_fwd(xv, g)
    err = out - target
    per_token = jnp.sum(err * err, axis=1, keepdims=True) * (1.0 / D_MODEL)
    loss_ref[...] += 0.5 * jnp.sum(per_token, axis=0, keepdims=True)
    dx, dg = _rms_bwd(err * (1.0 / D_MODEL), xv, rstd, g)
    dh_ref[...] = dx
    dhb_ref[...] = dx.astype(BF16)
    dg_ref[...] += dg


def _fwd_mlp(h, norm_g, w1, w2, layer, *, tm, riders=(), head=None):
    tokens = h.shape[0]
    nt = tokens // tm
    fs = D_FF // N_CHIPS
    n_in = 4 if head is None else 6

    def body(*refs):
        h_ref, g_ref, w1_hbm, w2_hbm = refs[:4]
        w1_v, w2_v, sem = refs[-3:]
        outs = refs[n_in:-3]
        n_ref, p_ref, q_ref = outs[1:4] if head is None else outs[0:3]
        _load_weights([(w1_hbm, w1_v), (w2_hbm, w2_v)], sem)

        xv = h_ref[...]
        nf, _ = _rms_fwd(xv, g_ref[...])
        n = nf.astype(BF16)
        n_ref[...] = n
        acc = xv
        for j in range(N_CHIPS):
            p = _dot(n, w1_v[j])
            p_ref[:, j * fs:(j + 1) * fs] = p.astype(BF16)
            r = jnp.maximum(p, 0.0)
            q = (r * r).astype(BF16)
            q_ref[:, j * fs:(j + 1) * fs] = q
            acc = acc + _dot(q, w2_v[j])
        if head is None:
            outs[0][...] = acc
        else:
            _loss_tail(acc, refs[4][...], refs[5][...], *outs[3:7])

    shp = lambda cols, dt: jax.ShapeDtypeStruct((tokens, cols), dt)
    saved_specs = [_row_spec(tm, D_MODEL), _row_spec(tm, D_FF), _row_spec(tm, D_FF)]
    saved_shapes = [shp(D_MODEL, BF16), shp(D_FF, BF16), shp(D_FF, BF16)]
    if head is None:
        operands, in_specs = [h, norm_g, w1, w2], [_row_spec(tm, D_MODEL), _full_spec((1, D_MODEL)), ANY, ANY]
        out_specs, out_shape = [_row_spec(tm, D_MODEL)] + saved_specs, [shp(D_MODEL, F32)] + saved_shapes
    else:
        operands = [h, norm_g, w1, w2, *head]
        in_specs = [_row_spec(tm, D_MODEL), _full_spec((1, D_MODEL)), ANY, ANY, _full_spec((1, D_MODEL)), _row_spec(tm, D_MODEL)]
        out_specs = saved_specs + [_full_spec((1, 1)), _row_spec(tm, D_MODEL), _row_spec(tm, D_MODEL), _full_spec((1, D_MODEL))]
        out_shape = saved_shapes + [jax.ShapeDtypeStruct((1, 1), F32), shp(D_MODEL, F32), shp(D_MODEL, BF16),
                                    jax.ShapeDtypeStruct((1, D_MODEL), F32)]
    return _pallas(
        body, operands, name=f"fwd_mlp{layer}", grid=(nt,), in_specs=in_specs, out_specs=out_specs, out_shape=out_shape,
        scratch_shapes=[pltpu.VMEM((N_CHIPS, D_MODEL, fs), BF16), pltpu.VMEM((N_CHIPS, fs, D_MODEL), BF16),
                        pltpu.SemaphoreType.DMA((N_LOADS,))],
        vmem_mib=56, riders=riders)


def _tril_mask():
    row = lax.broadcasted_iota(jnp.int32, (CHUNK, CHUNK), 0)
    col = lax.broadcasted_iota(jnp.int32, (CHUNK, CHUNK), 1)
    return row >= col


def _triu_mask():
    row = lax.broadcasted_iota(jnp.int32, (CHUNK, CHUNK), 0)
    col = lax.broadcasted_iota(jnp.int32, (CHUNK, CHUNK), 1)
    return row <= col


def _fwd_odd(h, norm_g, w_in, b_in, ln_g, ln_b, w_s, b_s_rows, w_out, *, tm, riders=()):
    tokens = h.shape[0]
    nt = tokens // tm
    cs = 2 * C_DIM // N_CHIPS

    def body(h_ref, g_ref, win_hbm, bin_ref, lng_ref, lnb_ref, ws_ref, bs_ref, wout_hbm,
             ho_ref, n_ref, s_ref, cdf_ref, sv_ref, y_ref, win_v, wout_v, bd, sem):
        _load_weights([(win_hbm, win_v), (wout_hbm, wout_v)], sem)

        @pl.when(pl.program_id(0) == 0)
        def _():
            mask = _tril_mask()
            bd[...] = jnp.zeros(bd.shape, BF16)
            for g in range(C_GROUPS):
                w = jnp.where(mask, ws_ref[g], 0.0).astype(BF16)
                bd[g, 0:CHUNK, 0:CHUNK] = w
                bd[g, CHUNK:PAIR, CHUNK:PAIR] = w

        xv = h_ref[...]
        nf, _ = _rms_fwd(xv, g_ref[...])
        n = nf.astype(BF16)
        n_ref[...] = n
        s = jnp.concatenate([_dot(n, win_v[j]) for j in range(N_CHIPS)], axis=1) + bin_ref[...]
        s_ref[...] = s.astype(BF16)
        cdf = _gelu_cdf(s)
        cdf_ref[...] = cdf.astype(BF16)
        zz = s * cdf
        u, v = zz[:, 0:C_DIM], zz[:, C_DIM:2 * C_DIM]
        xhat, _ = _ln_stats(v)
        vn = (xhat * lng_ref[...] + lnb_ref[...]).astype(BF16)
        for g in range(C_GROUPS):
            cols = slice(g * CHUNK, (g + 1) * CHUNK)
            bias = jnp.concatenate([bs_ref[g], bs_ref[g]], axis=0)
            for r0 in range(0, tm, PAIR):
                sv = _dot(bd[g], vn[r0:r0 + PAIR, cols]) + bias
                sv_ref[r0:r0 + PAIR, cols] = sv.astype(BF16)
                y_ref[r0:r0 + PAIR, cols] = (u[r0:r0 + PAIR, cols] * sv).astype(BF16)
        ho_ref[...] = xv + _dot(y_ref[...], wout_v[...])

    shp = lambda cols, dt: jax.ShapeDtypeStruct((tokens, cols), dt)
    return _pallas(
        body, [h, norm_g, w_in, b_in, ln_g, ln_b, w_s, b_s_rows, w_out], name="fwd_odd", grid=(nt,),
        in_specs=[_row_spec(tm, D_MODEL), _full_spec((1, D_MODEL)), ANY, _full_spec((1, 2 * C_DIM)),
                  _full_spec((1, C_DIM)), _full_spec((1, C_DIM)), _full_spec((C_GROUPS, CHUNK, CHUNK)),
                  _full_spec((C_GROUPS, CHUNK, CHUNK)), ANY],
        out_specs=[_row_spec(tm, D_MODEL), _row_spec(tm, D_MODEL), _row_spec(tm, 2 * C_DIM), _row_spec(tm, 2 * C_DIM),
                   _row_spec(tm, C_DIM), _row_spec(tm, C_DIM)],
        out_shape=[shp(D_MODEL, F32), shp(D_MODEL, BF16), shp(2 * C_DIM, BF16), shp(2 * C_DIM, BF16), shp(C_DIM, BF16),
                   shp(C_DIM, BF16)],
        scratch_shapes=[pltpu.VMEM((N_CHIPS, D_MODEL, cs), BF16), pltpu.VMEM((C_DIM, D_MODEL), BF16),
                        pltpu.VMEM((C_GROUPS, PAIR, PAIR), BF16), pltpu.SemaphoreType.DMA((N_LOADS,))],
        vmem_mib=56, riders=riders)


def _bwd_mlp(dh, h, norm_g, p, w1, w2, layer, *, tm, riders=()):
    tokens = h.shape[0]
    nt = tokens // tm
    fs = D_FF // N_CHIPS

    def body(dh_ref, h_ref, g_ref, p_ref, w1_hbm, w2_hbm, dx_ref, dxb_ref, dp_ref, dg_ref, w1_v, w2_v, sem):
        @pl.when(pl.program_id(0) == 0)
        def _():
            dg_ref[...] = jnp.zeros((1, D_MODEL), F32)

        _load_weights([(w1_hbm, w1_v), (w2_hbm, w2_v)], sem)

        dhv = dh_ref[...]
        dhb = dhv.astype(BF16)
        dn = jnp.zeros((tm, D_MODEL), F32)
        for j in range(N_CHIPS):
            dq = _dot_nt(dhb, w2_v[j])
            r = jnp.maximum(p_ref[:, j * fs:(j + 1) * fs].astype(F32), 0.0)
            dp = ((2.0 * r) * dq).astype(BF16)
            dp_ref[:, j * fs:(j + 1) * fs] = dp
            dn = dn + _dot_nt(dp, w1_v[j])
        xv = h_ref[...]
        g = g_ref[...]
        _, rstd = _rms_fwd(xv, g)
        dx, dg = _rms_bwd(dn, xv, rstd, g)
        dx_ref[...] = dhv + dx
        dxb_ref[...] = (dhv + dx).astype(BF16)
        dg_ref[...] += dg

    return _pallas(
        body, [dh, h, norm_g, p, w1, w2], name=f"bwd_mlp{layer}", grid=(nt,),
        in_specs=[_row_spec(tm, D_MODEL), _row_spec(tm, D_MODEL), _full_spec((1, D_MODEL)), _row_spec(tm, D_FF), ANY, ANY],
        out_specs=[_row_spec(tm, D_MODEL), _row_spec(tm, D_MODEL), _row_spec(tm, D_FF), _full_spec((1, D_MODEL))],
        out_shape=[jax.ShapeDtypeStruct((tokens, D_MODEL), F32), jax.ShapeDtypeStruct((tokens, D_MODEL), BF16),
                   jax.ShapeDtypeStruct((tokens, D_FF), BF16), jax.ShapeDtypeStruct((1, D_MODEL), F32)],
        scratch_shapes=[pltpu.VMEM((N_CHIPS, D_MODEL, fs), BF16), pltpu.VMEM((N_CHIPS, fs, D_MODEL), BF16),
                        pltpu.SemaphoreType.DMA((N_LOADS,))],
        vmem_mib=56, riders=riders)


def _bwd_odd(dh, h, norm_g, s, cdf, sv, w_in, ln_g, ln_b, w_s, w_out, *, tm, riders=()):
    tokens = h.shape[0]
    nt = tokens // tm
    cs = 2 * C_DIM // N_CHIPS

    def body(dh_ref, h_ref, g_ref, s_ref, cdf_ref, sv_ref, win_hbm, lng_ref, lnb_ref, ws_ref, wout_hbm,
             dx_ref, dxb_ref, ds_ref, dg_ref, dbin_ref, dlng_ref, dlnb_ref, dws_ref, dbs_ref,
             win_v, wout_v, bdt, dws_acc, dbs_acc, dvn, sem):
        i = pl.program_id(0)

        _load_weights([(win_hbm, win_v), (wout_hbm, wout_v)], sem)

        @pl.when(i == 0)
        def _():
            mask_t = _triu_mask()
            bdt[...] = jnp.zeros(bdt.shape, BF16)
            for g in range(C_GROUPS):
                wt = jnp.where(mask_t, ws_ref[g].T, 0.0).astype(BF16)
                bdt[g, 0:CHUNK, 0:CHUNK] = wt
                bdt[g, CHUNK:PAIR, CHUNK:PAIR] = wt
            dws_acc[...] = jnp.zeros(dws_acc.shape, F32)
            dbs_acc[...] = jnp.zeros(dbs_acc.shape, F32)
            dg_ref[...] = jnp.zeros(dg_ref.shape, F32)
            dbin_ref[...] = jnp.zeros(dbin_ref.shape, F32)
            dlng_ref[...] = jnp.zeros(dlng_ref.shape, F32)
            dlnb_ref[...] = jnp.zeros(dlnb_ref.shape, F32)

        dhv = dh_ref[...]
        dy = _dot_nt(dhv.astype(BF16), wout_v[...])
        sf = s_ref[...].astype(F32)
        cdf = cdf_ref[...].astype(F32)
        pdf = jnp.exp(-0.5 * sf * sf) * 0.3989422804014327
        zz = sf * cdf
        dgelu = cdf + sf * pdf
        u, v = zz[:, 0:C_DIM], zz[:, C_DIM:2 * C_DIM]
        xhat, rs = _ln_stats(v)
        lng = lng_ref[...]
        vn = (xhat * lng + lnb_ref[...]).astype(BF16)
        du = dy * sv_ref[...].astype(F32)
        dsv = dy * u
        dsvb = dsv.astype(BF16)
        for g in range(C_GROUPS):
            cols = slice(g * CHUNK, (g + 1) * CHUNK)
            for r0 in range(0, tm, PAIR):
                blk = dsvb[r0:r0 + PAIR, cols]
                dvn[r0:r0 + PAIR, cols] = _dot(bdt[g], blk)
                dws_acc[g] += _dot_nt(blk, vn[r0:r0 + PAIR, cols])
                dbs_acc[g] += dsv[r0:r0 + CHUNK, cols] + dsv[r0 + CHUNK:r0 + PAIR, cols]
        dv, dlng, dlnb = _ln_bwd(dvn[...], xhat, rs, lng)
        dlng_ref[...] += dlng
        dlnb_ref[...] += dlnb
        ds = jnp.concatenate([du, dv], axis=1) * dgelu
        dbin_ref[...] += jnp.sum(ds, axis=0, keepdims=True)
        dsb = ds.astype(BF16)
        ds_ref[...] = dsb
        dn = jnp.zeros((tm, D_MODEL), F32)
        for j in range(N_CHIPS):
            dn = dn + _dot_nt(dsb[:, j * cs:(j + 1) * cs], win_v[j])
        xv = h_ref[...]
        g = g_ref[...]
        _, rstd = _rms_fwd(xv, g)
        dx, dg = _rms_bwd(dn, xv, rstd, g)
        dx_ref[...] = dhv + dx
        dxb_ref[...] = (dhv + dx).astype(BF16)
        dg_ref[...] += dg

        @pl.when(i == nt - 1)
        def _():
            mask = _tril_mask()
            for g in range(C_GROUPS):
                full = dws_acc[g]
                dws_ref[g] = jnp.where(mask, full[0:CHUNK, 0:CHUNK] + full[CHUNK:PAIR, CHUNK:PAIR], 0.0)
                dbs_ref[g:g + 1, :] = jnp.sum(dbs_acc[g].T, axis=0, keepdims=True)

    row = lambda cols: jax.ShapeDtypeStruct((1, cols), F32)
    return _pallas(
        body, [dh, h, norm_g, s, cdf, sv, w_in, ln_g, ln_b, w_s, w_out], name="bwd_odd", grid=(nt,),
        in_specs=[_row_spec(tm, D_MODEL), _row_spec(tm, D_MODEL), _full_spec((1, D_MODEL)), _row_spec(tm, 2 * C_DIM),
                  _row_spec(tm, 2 * C_DIM), _row_spec(tm, C_DIM), ANY, _full_spec((1, C_DIM)), _full_spec((1, C_DIM)),
                  _full_spec((C_GROUPS, CHUNK, CHUNK)), ANY],
        out_specs=[_row_spec(tm, D_MODEL), _row_spec(tm, D_MODEL), _row_spec(tm, 2 * C_DIM), _full_spec((1, D_MODEL)),
                   _full_spec((1, 2 * C_DIM)),
                   _full_spec((1, C_DIM)), _full_spec((1, C_DIM)), _full_spec((C_GROUPS, CHUNK, CHUNK)),
                   _full_spec((C_GROUPS, CHUNK))],
        out_shape=[jax.ShapeDtypeStruct((tokens, D_MODEL), F32), jax.ShapeDtypeStruct((tokens, D_MODEL), BF16),
                   jax.ShapeDtypeStruct((tokens, 2 * C_DIM), BF16),
                   row(D_MODEL), row(2 * C_DIM), row(C_DIM), row(C_DIM),
                   jax.ShapeDtypeStruct((C_GROUPS, CHUNK, CHUNK), F32), jax.ShapeDtypeStruct((C_GROUPS, CHUNK), F32)],
        scratch_shapes=[pltpu.VMEM((N_CHIPS, D_MODEL, cs), BF16), pltpu.VMEM((C_DIM, D_MODEL), BF16),
                        pltpu.VMEM((C_GROUPS, PAIR, PAIR), BF16), pltpu.VMEM((C_GROUPS, PAIR, PAIR), F32),
                        pltpu.VMEM((C_GROUPS, CHUNK, CHUNK), F32), pltpu.VMEM((tm, C_DIM), F32),
                        pltpu.SemaphoreType.DMA((N_LOADS,))],
        vmem_mib=56, riders=riders)


def _bwd_even(dh, x, norm_g, z, a2, cv, w_in, conv_a_w, ln_g, ln_b, conv_b_w, w_out, *, tm, seq, riders=()):
    tokens = x.shape[0]
    nt, tps = tokens // tm, seq // tm
    ws = IN_EVEN // N_CHIPS

    def body(dh_ref, x_ref, g_ref, z_ref, a2_ref, cv_ref, win_hbm, caw_ref, lng_ref, lnb_ref, cbw_ref, wout_hbm,
             dx_ref, dz_ref, dg_ref, dcaw_ref, dcab_ref, dlng_ref, dlnb_ref, dcbw_ref,
             win_v, wout_v, ea, eb, a1s, da1s, dw_acc, sem):
        i = pl.program_id(0)

        _load_weights([(win_hbm, win_v), (wout_hbm, wout_v)], sem)

        @pl.when(i == 0)
        def _():
            dw_acc[...] = jnp.zeros(dw_acc.shape, F32)
            for ref in (dg_ref, dcab_ref, dlng_ref, dlnb_ref, dcbw_ref):
                ref[...] = jnp.zeros(ref.shape, F32)

        dhv = dh_ref[...]
        dmix = _dot_nt(dhv.astype(BF16), wout_v[...])
        da4, dbo = dmix[:, 0:A_DIM], dmix[:, A_DIM:A_DIM + B_DIM]
        zf = z_ref[...].astype(F32)
        a_val, a_gate = zf[:, 0:A_DIM], zf[:, A_DIM:2 * A_DIM]
        b_gate, c_gate, b_val = zf[:, 1024:1536], zf[:, 1536:2048], zf[:, 2048:2560]

        xhat, rs = _ln_stats(a2_ref[...])
        lng = lng_ref[...]
        a3 = xhat * lng + lnb_ref[...]
        sg = jax.nn.sigmoid(a3)
        da3 = da4 * (sg * (1.0 + a3 * (1.0 - sg)))
        da2, dlng, dlnb = _ln_bwd(da3, xhat, rs, lng)
        dlng_ref[...] += dlng
        dlnb_ref[...] += dlnb
        dcab_ref[...] += jnp.sum(da2, axis=0, keepdims=True)

        last = ((nt - 1 - i) % tps) == tps - 1
        dcv = dbo * b_gate

        @pl.when(last)
        def _():
            ea[0, tm:tm + A_HALO, :] = jnp.zeros((A_HALO, A_DIM), F32)
            eb[tm:tm + B_HALO, :] = jnp.zeros((B_HALO, B_DIM), F32)

        @pl.when(jnp.logical_not(last))
        def _():
            ea[0, tm:tm + A_HALO, :] = ea[0, 0:A_HALO, :]
            eb[tm:tm + B_HALO, :] = eb[0:B_HALO, :]

        ea[0, 0:tm, :] = da2
        eb[0:tm, :] = dcv
        _fill_shifted(ea, tm + A_HALO)
        sig = jax.nn.sigmoid(a_gate)
        a1s[...] = a_val * sig
        for r0 in range(0, tm, CONV_ROWS):
            a1c = a1s[r0:r0 + CONV_ROWS, :]
            acc = jnp.zeros((CONV_ROWS, A_DIM), F32)
            for j in range(A_CONV_WIDTH):
                k = A_CONV_WIDTH - 1 - j
                sl = _window(ea, r0 + j, CONV_ROWS)
                acc = acc + caw_ref[k:k + 1, :] * sl
                dw_acc[k] += sl * a1c
            da1s[r0:r0 + CONV_ROWS, :] = acc
        da1 = da1s[...]
        da_val = da1 * sig
        da_gate = da1 * a_val * (sig * (1.0 - sig))

        db_gate = dbo * cv_ref[...].astype(F32)
        cb = c_gate * b_val
        dcb = jnp.zeros((tm, B_DIM), F32)
        for j in range(B_CONV_WIDTH):
            k = B_CONV_WIDTH - 1 - j
            sl = eb[j:j + tm, :]
            dcb = dcb + cbw_ref[k:k + 1, :] * sl
            dcbw_ref[k:k + 1, :] += jnp.sum(sl * cb, axis=0, keepdims=True)
        dz = jnp.concatenate([da_val, da_gate, db_gate, dcb * b_val, dcb * c_gate], axis=1).astype(BF16)
        dz_ref[...] = dz
        dn = jnp.zeros((tm, D_MODEL), F32)
        for j in range(N_CHIPS):
            dn = dn + _dot_nt(dz[:, j * ws:(j + 1) * ws], win_v[j])
        xv = x_ref[...]
        g = g_ref[...]
        _, rstd = _rms_fwd(xv, g)
        dx, dg = _rms_bwd(dn, xv, rstd, g)
        dx_ref[...] = dhv + dx
        dg_ref[...] += dg

        @pl.when(i == nt - 1)
        def _():
            for k in range(A_CONV_WIDTH):
                dcaw_ref[k:k + 1, :] = jnp.sum(dw_acc[k], axis=0, keepdims=True)

    row = lambda cols: jax.ShapeDtypeStruct((1, cols), F32)
    rs_ = functools.partial(_row_spec, rev_nt=nt)
    return _pallas(
        body, [dh, x, norm_g, z, a2, cv, w_in, conv_a_w, ln_g, ln_b, conv_b_w, w_out], name="bwd_even", grid=(nt,),
        in_specs=[rs_(tm, D_MODEL), rs_(tm, D_MODEL), _full_spec((1, D_MODEL)), rs_(tm, IN_EVEN), rs_(tm, A_DIM),
                  rs_(tm, B_DIM), ANY, _full_spec((A_CONV_WIDTH, A_DIM)), _full_spec((1, A_DIM)), _full_spec((1, A_DIM)),
                  _full_spec((B_CONV_WIDTH, B_DIM)), ANY],
        out_specs=[rs_(tm, D_MODEL), rs_(tm, IN_EVEN), _full_spec((1, D_MODEL)), _full_spec((A_CONV_WIDTH, A_DIM)),
                   _full_spec((1, A_DIM)), _full_spec((1, A_DIM)), _full_spec((1, A_DIM)), _full_spec((B_CONV_WIDTH, B_DIM))],
        out_shape=[jax.ShapeDtypeStruct((tokens, D_MODEL), F32), jax.ShapeDtypeStruct((tokens, IN_EVEN), BF16),
                   row(D_MODEL), jax.ShapeDtypeStruct((A_CONV_WIDTH, A_DIM), F32), row(A_DIM), row(A_DIM), row(A_DIM),
                   jax.ShapeDtypeStruct((B_CONV_WIDTH, B_DIM), F32)],
        scratch_shapes=[pltpu.VMEM((N_CHIPS, D_MODEL, ws), BF16), pltpu.VMEM((D_MODEL, D_MODEL), BF16),
                        pltpu.VMEM((SUBLANES, tm + A_HALO, A_DIM), F32), pltpu.VMEM((tm + B_HALO, B_DIM), F32),
                        pltpu.VMEM((tm, A_DIM), F32), pltpu.VMEM((tm, A_DIM), F32),
                        pltpu.VMEM((A_CONV_WIDTH, CONV_ROWS, A_DIM), F32), pltpu.SemaphoreType.DMA((N_LOADS,))],
        vmem_mib=56, riders=riders)


def _wgrad(a, b, name, *, col_shards, riders=()):
    tokens, m = a.shape
    n = b.shape[1]
    kc = 512
    if col_shards:
        bm, bn = m // 2, n // N_CHIPS
        grid = (2, N_CHIPS)
        out_spec = pl.BlockSpec((None, None, bm, bn), lambda i, j: (j, i, 0, 0))
    elif m // 8 >= MXU_ROWS:
        bm, bn = m // 8, n
        grid = (8, 1)
        out_spec = pl.BlockSpec((None, None, bm, bn), lambda i, j: (i // 2, i % 2, 0, 0))
    else:
        bm, bn = m // N_CHIPS, n
        grid = (N_CHIPS, 1)
        out_spec = pl.BlockSpec((None, 2, bm // 2, bn), lambda i, j: (i, 0, 0, 0))

    def body(a_ref, b_ref, o_ref):
        acc = jnp.zeros((bm, bn), F32)
        for k0 in range(0, tokens, kc):
            acc = acc + _dot_tn(a_ref[k0:k0 + kc, :].astype(BF16), b_ref[k0:k0 + kc, :].astype(BF16))
        if len(o_ref.shape) == 3:
            o_ref[0] = acc[0:bm // 2]
            o_ref[1] = acc[bm // 2:bm]
        else:
            o_ref[...] = acc

    out_rows = m // 2 if col_shards else m // 8
    outs, routs = _pallas(
        body, [a, b], name=name, grid=grid,
        in_specs=[pl.BlockSpec((tokens, bm), lambda i, j: (0, i)), pl.BlockSpec((tokens, bn), lambda i, j: (0, j))],
        out_specs=[out_spec], out_shape=[jax.ShapeDtypeStruct((N_CHIPS, 2, out_rows, bn), F32)],
        vmem_mib=56, riders=riders)
    return outs[0], routs


def _wgrad_pair(a, b, name, *, col_shards, riders=()):
    tokens, m = a.shape
    n = b.shape[1]
    kc = 512
    c = lax.axis_index("c")
    if col_shards:
        bm, bn = m // 2, n // N_CHIPS
        a_spec = pl.BlockSpec((tokens, bm), lambda ph, q, cr: (0, (ph + 1 + cr[0]) % 2))
        b_spec = pl.BlockSpec((tokens, bn), lambda ph, q, cr: (0, q))
    else:
        bm, bn = m // 8, n
        a_spec = pl.BlockSpec((tokens, bm), lambda ph, q, cr: (0, 2 * q + (ph + 1 + cr[0]) % 2))
        b_spec = pl.BlockSpec((tokens, bn), lambda ph, q, cr: (0, 0))

    def body(c_ref, a_ref, b_ref, o_ref, give, got, send_sems, recv_sems):
        ph, q = pl.program_id(0), pl.program_id(1)
        acc = jnp.zeros((bm, bn), F32)
        for k0 in range(0, tokens, kc):
            acc = acc + _dot_tn(a_ref[k0:k0 + kc, :].astype(BF16), b_ref[k0:k0 + kc, :].astype(BF16))
        x, y, cc = _mesh_pos()

        def tile(t):
            return _remote(give.at[t], got.at[t], send_sems.at[t], recv_sems.at[t], (x, y, 1 - cc))

        @pl.when(ph == 0)
        def _():
            give[q] = acc
            tile(q).start()

        @pl.when(ph == 1)
        def _():
            tile(q).wait_recv()
            o_ref[...] = (acc + got[q]).astype(BF16)

        @pl.when((ph == 1) & (q == N_CHIPS - 1))
        def _():
            for t in range(N_CHIPS):
                tile(t).wait_send()

    outs, routs = _pallas(
        body, [a, b], name=name, grid=(2, N_CHIPS), in_specs=[a_spec, b_spec],
        out_specs=[pl.BlockSpec((None, bm, bn), lambda ph, q, cr: (ph * q, 0, 0))],
        out_shape=[jax.ShapeDtypeStruct((N_CHIPS, bm, bn), BF16)],
        scratch_shapes=[pltpu.VMEM((N_CHIPS, bm, bn), F32), pltpu.VMEM((N_CHIPS, bm, bn), F32),
                        pltpu.SemaphoreType.DMA((N_CHIPS,)), pltpu.SemaphoreType.DMA((N_CHIPS,))],
        vmem_mib=56, riders=riders, prefetch=jnp.reshape(c, (1,)).astype(jnp.int32))
    return outs[0], routs


class _GradReduce:
    def __init__(self, name, grad=None, chip_sum=None):
        self.name, self.grad, self.chip_sum = name, grad, chip_sum
        self.full = None

    def pair_swap(self):
        return _PairSwap([self.grad])

    def took_pair(self, outs):
        self.chip_sum = _in_hbm(_add_pair(self.grad, outs[0], f"pair_sum_{self.name}"))

    def chip_swap(self):
        return _ChipSwap([self.chip_sum])

    def took_chips(self, outs):
        self.full = _in_hbm(_add_chips(self.chip_sum, outs[0], f"chip_sum_{self.name}"))

    def pair_share(self):
        return _PairShare([self.full])

    def took_share(self, outs):
        self.full = outs[0]

    def reduced(self):
        return jnp.reshape(self.full, (2 * self.full.shape[1], self.full.shape[2]))


def _forward_backward(x2, tgt2, gathered, staged, conv_a_w, conv_b_w, od_norm, od_bias, od_lng, od_lnb,
                      ev_norm_g, ev_conv_a_b, ev_ln_a_g, ev_ln_a_b, od_w_s, od_b_s, mlp_norm_g, final_norm_g,
                      *, tm, seq, distributed=True):
    d = x2.shape[1]
    w = dict(gathered)
    b_s_rows = jnp.broadcast_to(od_b_s[0][:, :, None], (C_GROUPS, CHUNK, CHUNK))

    def ride(*names):
        return [_Gather([staged[nm] for nm in names])] if distributed else []

    def land(routs, *names):
        if distributed:
            for nm, buf in zip(names, routs[0]):
                w[nm] = buf

    def as_cols(buf):
        return jnp.reshape(buf, (N_CHIPS, 2 * buf.shape[2], buf.shape[3]))

    def as_rows(buf):
        return jnp.reshape(buf, (8 * buf.shape[2], buf.shape[3]))

    (h1, n0, z, a2, cv, mix), routs = _fwd_even(
        x2, ev_norm_g, as_cols(w["ev_in"]), conv_a_w, ev_conv_a_b, ev_ln_a_g, ev_ln_a_b, conv_b_w, as_rows(w["ev_out"]),
        tm=tm, seq=seq, riders=ride("w1_0", "w2_0"))
    land(routs, "w1_0", "w2_0")
    (h2, n1, p0, q0), routs = _fwd_mlp(h1, mlp_norm_g[0:1], as_cols(w["w1_0"]), as_cols(w["w2_0"]), 0, tm=tm,
                                       riders=ride("od_in", "od_out", "w1_1"))
    land(routs, "od_in", "od_out", "w1_1")
    (h3, n2, s, cdf, sv, y), routs = _fwd_odd(h2, od_norm, as_cols(w["od_in"]), od_bias, od_lng, od_lnb, od_w_s[0], b_s_rows,
                                         as_rows(w["od_out"]), tm=tm, riders=ride("w2_1"))
    land(routs, "w2_1")
    (n3, p1, q1, loss_part, dh4, dh4b, d_final_g), _ = _fwd_mlp(
        h3, mlp_norm_g[1:2], as_cols(w["w1_1"]), as_cols(w["w2_1"]), 1, tm=tm,
        head=(jnp.reshape(final_norm_g, (1, d)), tgt2))

    red = {}

    def swap(*names):
        return [red[nm].pair_swap() for nm in names] if distributed else []

    def chips(*names):
        return [red[nm].chip_swap() for nm in names] if distributed else []

    def share(*names):
        return [red[nm].pair_share() for nm in names] if distributed else []

    def took(routs, *steps):
        if distributed:
            for (nm, what), outs in zip(steps, routs):
                getattr(red[nm], what)(outs)

    def big(lhs, rhs, name, col_shards, riders=()):
        if distributed:
            chip_sum, routs = _wgrad_pair(lhs, rhs, f"wgrad_{name}", col_shards=col_shards, riders=riders)
            red[name] = _GradReduce(name, chip_sum=_in_hbm(chip_sum))
        else:
            g, routs = _wgrad(lhs, rhs, f"wgrad_{name}", col_shards=col_shards)
            red[name] = _GradReduce(name, grad=g)
        return routs

    big(q1, dh4b, "w2_1", False)
    (dh3, dh3b, dp1, d_mlp_g1), routs = _bwd_mlp(dh4, h3, mlp_norm_g[1:2], p1, as_cols(w["w1_1"]), as_cols(w["w2_1"]), 1, tm=tm,
                                           riders=chips("w2_1"))
    took(routs, ("w2_1", "took_chips"))
    big(n3, dp1, "w1_1", True)
    g, routs = _wgrad(y, dh3b, "wgrad_od_out", col_shards=False, riders=share("w2_1"))
    red["od_out"] = _GradReduce("od_out", grad=g)
    took(routs, ("w2_1", "took_share"))
    (dh2, dh2b, ds, d_od_norm, d_od_bin, d_od_lng, d_od_lnb, d_ws, d_bs), routs = _bwd_odd(
        dh3, h2, od_norm, s, cdf, sv, as_cols(w["od_in"]), od_lng, od_lnb, od_w_s[0], as_rows(w["od_out"]), tm=tm,
        riders=chips("w1_1") + swap("od_out"))
    took(routs, ("w1_1", "took_chips"), ("od_out", "took_pair"))
    routs = big(n2, ds, "od_in", True, riders=share("w1_1"))
    took(routs, ("w1_1", "took_share"))
    half_groups = C_GROUPS // 2
    early = {"loss": loss_part, "od_w_s_lo": d_ws[:half_groups], "od_b_s": d_bs, "mlp_norm_g1": d_mlp_g1, "final_norm_g": d_final_g,
             "od_norm_g": d_od_norm, "od_b_in": d_od_bin, "od_ln_v_g": d_od_lng, "od_ln_v_b": d_od_lnb}
    share_early = [_ShareAll(list(early.values()))] if distributed else []
    routs = big(q0, dh2b, "w2_0", False, riders=share_early)
    landed_early = routs[0] if distributed else []
    (dh1, dh1b, dp0, d_mlp_g0), routs = _bwd_mlp(dh2, h1, mlp_norm_g[0:1], p0, as_cols(w["w1_0"]), as_cols(w["w2_0"]), 0, tm=tm,
                                           riders=chips("od_out") + chips("od_in") + chips("w2_0"))
    took(routs, ("od_out", "took_chips"), ("od_in", "took_chips"), ("w2_0", "took_chips"))
    middle = {"od_w_s_hi": d_ws[half_groups:]}
    share_middle = [_ShareAll(list(middle.values()))] if distributed else []
    routs = big(n1, dp0, "w1_0", True, riders=share("od_out") + share("od_in") + share("w2_0") + share_middle)
    took(routs, ("od_out", "took_share"), ("od_in", "took_share"), ("w2_0", "took_share"))
    landed_middle = routs[3] if distributed else []
    g, _ = _wgrad(mix, dh1b, "wgrad_ev_out", col_shards=False)
    red["ev_out"] = _GradReduce("ev_out", grad=g)

    (dx, dz, d_ev_norm, d_caw, d_cab, d_ev_lng, d_ev_lnb, d_cbw), routs = _bwd_even(
        dh1, x2, ev_norm_g, z, a2, cv, as_cols(w["ev_in"]), conv_a_w, ev_ln_a_g, ev_ln_a_b, conv_b_w, as_rows(w["ev_out"]),
        tm=tm, seq=seq, riders=chips("w1_0") + swap("ev_out"))
    took(routs, ("w1_0", "took_chips"), ("ev_out", "took_pair"))
    late = {"mlp_norm_g0": d_mlp_g0, "ev_norm_g": d_ev_norm, "ev_conv_a_b": d_cab, "ev_ln_a_g": d_ev_lng,
            "ev_ln_a_b": d_ev_lnb, "ev_conv_a_w": d_caw, "ev_conv_b_w": d_cbw}
    share_late = [_ShareAll(list(late.values()))] if distributed else []
    routs2 = big(n0, dz, "ev_in", True, riders=chips("ev_out") + share("w1_0") + share_late)
    took(routs2, ("ev_out", "took_chips"), ("w1_0", "took_share"))
    own = {**early, **middle, **late}
    landed = dict(zip(own.keys(), landed_early + landed_middle + routs2[2])) if distributed else None
    return dx, red, own, landed


def _rows128(a):
    rows = jnp.reshape(a, (-1, LANES))
    pad = (-rows.shape[0]) % SUBLANES
    return jnp.pad(rows, ((0, pad), (0, 0))) if pad else rows


def _pack(arrays):
    return jnp.concatenate([_rows128(a) for a in arrays], axis=0)


def _unpack(buf, shapes):
    out, r0 = [], 0
    for shp in shapes:
        size = 1
        for dim in shp:
            size *= dim
        nr = size // LANES
        out.append(jnp.reshape(buf[r0:r0 + nr], shp))
        r0 += nr + (-nr) % SUBLANES
    return out


def kernel(x, ev_norm_g, ev_w_in, ev_conv_a_w, ev_conv_a_b, ev_ln_a_g, ev_ln_a_b, ev_conv_b_w, ev_w_out, od_norm_g, od_w_in, od_b_in, od_ln_v_g, od_ln_v_b, od_w_s, od_b_s, od_w_out, mlp_norm_g, mlp_w1, mlp_w2, final_norm_g, loss_target, m_ev_norm_g, m_ev_w_in, m_ev_conv_a_w, m_ev_conv_a_b, m_ev_ln_a_g, m_ev_ln_a_b, m_ev_conv_b_w, m_ev_w_out, m_od_norm_g, m_od_w_in, m_od_b_in, m_od_ln_v_g, m_od_ln_v_b, m_od_w_s, m_od_b_s, m_od_w_out, m_mlp_norm_g, m_mlp_w1, m_mlp_w2, m_final_norm_g, v_ev_norm_g, v_ev_w_in, v_ev_conv_a_w, v_ev_conv_a_b, v_ev_ln_a_g, v_ev_ln_a_b, v_ev_conv_b_w, v_ev_w_out, v_od_norm_g, v_od_w_in, v_od_b_in, v_od_ln_v_g, v_od_ln_v_b, v_od_w_s, v_od_b_s, v_od_w_out, v_mlp_norm_g, v_mlp_w1, v_mlp_w2, v_final_norm_g):
    tm = TOKEN_TILE
    batch, seq, d = x.shape
    tokens = batch * seq
    x2 = jnp.reshape(x, (tokens, d))
    tgt2 = jnp.reshape(loss_target, (tokens, d))
    chip = 2 * lax.axis_index("x") + lax.axis_index("y")

    small_shapes = [(A_CONV_WIDTH, LANES), (B_CONV_WIDTH, LANES), (256,), (512,), (256,), (256,)]
    small_shard = _pack([ev_conv_a_w[0], ev_conv_b_w[0], od_norm_g[0], od_b_in[0], od_ln_v_g[0], od_ln_v_b[0]])
    small_shard = jnp.pad(small_shard, ((0, (-small_shard.shape[0]) % (2 * SUBLANES)), (0, 0)))
    first = [_place_shard(ev_w_in, 0, BF16, "place_ev_w_in"), _place_shard(ev_w_out, 0, BF16, "place_ev_w_out"),
             _place_shard(small_shard[None], 0, F32, "place_small")]
    staged = {
        "w1_0": _place_shard(mlp_w1, 0, BF16, "place_w1_0"), "w2_0": _place_shard(mlp_w2, 0, BF16, "place_w2_0"),
        "od_in": _place_shard(od_w_in, 0, BF16, "place_od_w_in"), "od_out": _place_shard(od_w_out, 0, BF16, "place_od_w_out"),
        "w1_1": _place_shard(mlp_w1, 1, BF16, "place_w1_1"), "w2_1": _place_shard(mlp_w2, 1, BF16, "place_w2_1"),
    }
    first = [_in_hbm(a) for a in first]
    staged = {nm: _in_hbm(a) for nm, a in staged.items()}
    (g_ev_in, g_ev_out, g_small), = _exchange([_Gather(first)], "gather_first")
    small_all = jnp.reshape(g_small, (N_CHIPS, -1, LANES))
    per_chip = [_unpack(small_all[q], small_shapes) for q in range(N_CHIPS)]
    conv_a_w = jnp.concatenate([pc[0] for pc in per_chip], axis=1)
    conv_b_w = jnp.concatenate([pc[1] for pc in per_chip], axis=1)
    od_norm = jnp.concatenate([pc[2] for pc in per_chip])[None, :]
    od_bias = jnp.concatenate([pc[3] for pc in per_chip])[None, :]
    od_lng = jnp.concatenate([pc[4] for pc in per_chip])[None, :]
    od_lnb = jnp.concatenate([pc[5] for pc in per_chip])[None, :]

    dx, red, own, landed = _forward_backward(
        x2, tgt2, {"ev_in": g_ev_in, "ev_out": g_ev_out}, staged, conv_a_w, conv_b_w, od_norm, od_bias, od_lng, od_lnb,
        ev_norm_g, ev_conv_a_b, ev_ln_a_g, ev_ln_a_b, od_w_s, od_b_s, mlp_norm_g, final_norm_g, tm=tm, seq=seq)

    routs = _exchange([red["ev_in"].chip_swap(), red["ev_out"].pair_share()], "reduce_tail_1")
    red["ev_in"].took_chips(routs[0])
    red["ev_out"].took_share(routs[1])
    routs = _exchange([red["ev_in"].pair_share()], "reduce_tail_2")
    red["ev_in"].took_share(routs[0])

    given = {"ev_norm_g": (ev_norm_g, m_ev_norm_g, v_ev_norm_g), "ev_conv_a_b": (ev_conv_a_b, m_ev_conv_a_b, v_ev_conv_a_b),
             "ev_ln_a_g": (ev_ln_a_g, m_ev_ln_a_g, v_ev_ln_a_g), "ev_ln_a_b": (ev_ln_a_b, m_ev_ln_a_b, v_ev_ln_a_b),
             "od_w_s": (od_w_s, m_od_w_s, v_od_w_s), "od_b_s": (od_b_s, m_od_b_s, v_od_b_s),
             "mlp_norm_g": (mlp_norm_g, m_mlp_norm_g, v_mlp_norm_g), "final_norm_g": (final_norm_g, m_final_norm_g, v_final_norm_g),
             "ev_conv_a_w": (ev_conv_a_w, m_ev_conv_a_w, v_ev_conv_a_w), "ev_conv_b_w": (ev_conv_b_w, m_ev_conv_b_w, v_ev_conv_b_w),
             "od_norm_g": (od_norm_g, m_od_norm_g, v_od_norm_g), "od_b_in": (od_b_in, m_od_b_in, v_od_b_in),
             "od_ln_v_g": (od_ln_v_g, m_od_ln_v_g, v_od_ln_v_g), "od_ln_v_b": (od_ln_v_b, m_od_ln_v_b, v_od_ln_v_b)}
    shaped = {nm: tuple(jnp.reshape(a, shape) for a in given[nm]) for nm, shape, _, _ in SMALL_WEIGHTS}
    loss11, small_upd = _small_update(own, landed, shaped)
    loss = loss11[0, 0]
    upd = {nm: [jnp.reshape(o, given[nm][0].shape) for o in outs] for nm, outs in small_upd.items()}

    def big_update(wt, m, v, names, call):
        grads = [red[nm].reduced() for nm in names]
        shp3 = (len(grads),) + grads[0].shape
        outs, _ = _adamw(jnp.reshape(wt, shp3), jnp.reshape(m, shp3), jnp.reshape(v, shp3), grads, call)
        return [jnp.reshape(o, wt.shape) for o in outs], None

    upd["mlp_w2"], _ = big_update(mlp_w2, m_mlp_w2, v_mlp_w2, ["w2_0", "w2_1"], "adamw_mlp_w2")
    upd["mlp_w1"], _ = big_update(mlp_w1, m_mlp_w1, v_mlp_w1, ["w1_0", "w1_1"], "adamw_mlp_w1")
    upd["ev_w_in"], _ = big_update(ev_w_in, m_ev_w_in, v_ev_w_in, ["ev_in"], "adamw_ev_w_in")
    upd["ev_w_out"], _ = big_update(ev_w_out, m_ev_w_out, v_ev_w_out, ["ev_out"], "adamw_ev_w_out")
    upd["od_w_in"], _ = big_update(od_w_in, m_od_w_in, v_od_w_in, ["od_in"], "adamw_od_w_in")
    upd["od_w_out"], _ = big_update(od_w_out, m_od_w_out, v_od_w_out, ["od_out"], "adamw_od_w_out")

    order = ["ev_norm_g", "ev_w_in", "ev_conv_a_w", "ev_conv_a_b", "ev_ln_a_g", "ev_ln_a_b", "ev_conv_b_w", "ev_w_out",
             "od_norm_g", "od_w_in", "od_b_in", "od_ln_v_g", "od_ln_v_b", "od_w_s", "od_b_s", "od_w_out", "mlp_norm_g",
             "mlp_w1", "mlp_w2", "final_norm_g"]
    grad_x = jnp.reshape(dx, x.shape)
    return (loss, grad_x, *[upd[nm][0] for nm in order], *[upd[nm][1] for nm in order],
            *[upd[nm][2] for nm in order], *[upd[nm][3] for nm in order])
```

```python
import functools

import jax
import jax.numpy as jnp
from jax import lax
from jax.experimental import pallas as pl
from jax.experimental.pallas import tpu as pltpu

F32 = jnp.float32
BF16 = jnp.bfloat16

D_MODEL = 1024
A_DIM = 512
B_DIM = 512
IN_EVEN = 2 * A_DIM + 3 * B_DIM
A_CONV_WIDTH = 31
B_CONV_WIDTH = 3
CHUNK = 128
C_GROUPS = 8
C_DIM = 1024
D_FF = 4096
RMS_EPS = 1e-6
LN_EPS = 1e-5
ADAM_LR = 0.001
ADAM_B1 = 0.9
ADAM_B2 = 0.999
ADAM_EPS = 1e-08
ADAM_WD = 0.01
ADAM_STEP = 10

N_CHIPS = 4
N_DEV = 8
TOKEN_TILE = 512
A_HALO = 32
B_HALO = 8
CONV_ROWS = 16
PAIR = 2 * CHUNK
LANES = 128
SUBLANES = 8
MXU_ROWS = 256
MIB = 1024 * 1024
MESH = pl.DeviceIdType.MESH
ANY = pl.BlockSpec(memory_space=pl.ANY)


def _dot(a, b):
    return lax.dot_general(a, b, (((1,), (0,)), ((), ())), preferred_element_type=F32)


def _dot_nt(a, b):
    return lax.dot_general(a, b, (((1,), (1,)), ((), ())), preferred_element_type=F32)


def _dot_tn(a, b):
    return lax.dot_general(a, b, (((0,), (0,)), ((), ())), preferred_element_type=F32)


def _params(vmem_mib, n_axes=1):
    return pltpu.CompilerParams(dimension_semantics=("arbitrary",) * n_axes, vmem_limit_bytes=vmem_mib * MIB)


def _row_spec(tm, cols, rev_nt=None):
    if rev_nt is None:
        return pl.BlockSpec((tm, cols), lambda i: (i, 0))
    return pl.BlockSpec((tm, cols), lambda i: (rev_nt - 1 - i, 0))


def _full_spec(shape):
    nd = len(shape)
    return pl.BlockSpec(shape, lambda i: (0,) * nd)


def _block_rows(rows, cap=512):
    best = SUBLANES
    for br in range(SUBLANES, min(rows, cap) + 1, SUBLANES):
        if rows % br == 0:
            best = br
    return best


N_LOADS = 2


def _load_weights(pairs, sems):
    @pl.when(pl.program_id(0) == 0)
    def _():
        copies = [pltpu.make_async_copy(src, dst, sems.at[k]) for k, (src, dst) in enumerate(pairs)]
        for cp in copies:
            cp.start()
        for cp in copies:
            cp.wait()


def _rms_fwd(x, g):
    rstd = lax.rsqrt(jnp.mean(x * x, axis=-1, keepdims=True) + RMS_EPS)
    return x * rstd * g, rstd


def _rms_bwd(dn, x, rstd, g):
    a = dn * g
    xh = x * rstd
    dx = rstd * (a - xh * jnp.mean(a * xh, axis=-1, keepdims=True))
    dg = jnp.sum(dn * xh, axis=0, keepdims=True)
    return dx, dg


def _ln_stats(v):
    mu = jnp.mean(v, axis=-1, keepdims=True)
    xc = v - mu
    rs = lax.rsqrt(jnp.mean(xc * xc, axis=-1, keepdims=True) + LN_EPS)
    return xc * rs, rs


def _ln_bwd(dy, xhat, rs, g):
    dxh = dy * g
    dv = rs * (dxh - jnp.mean(dxh, axis=-1, keepdims=True) - xhat * jnp.mean(dxh * xhat, axis=-1, keepdims=True))
    return dv, jnp.sum(dy * xhat, axis=0, keepdims=True), jnp.sum(dy, axis=0, keepdims=True)


def _gelu_cdf(s):
    return 0.5 * (1.0 + lax.erf(s * 0.7071067811865476))


def _mesh_pos():
    return lax.axis_index("x"), lax.axis_index("y"), lax.axis_index("c")


def _other_chips(x, y):
    return [(1 - x, y), (x, 1 - y), (1 - x, 1 - y)]


def _remote(src, dst, send_sem, recv_sem, to):
    return pltpu.make_async_remote_copy(src_ref=src, dst_ref=dst, send_sem=send_sem, recv_sem=recv_sem,
                                        device_id=to, device_id_type=MESH)


def _like(arrays):
    return [jax.ShapeDtypeStruct(a.shape, a.dtype) for a in arrays]


class _Gather:
    def __init__(self, bufs):
        self.ins = list(bufs)
        self.out_shapes = _like(bufs)
        self.aliases = {t: t for t in range(len(bufs))}
        self.n_sems = 6 * len(bufs)

    def _ici(self, ins, outs, send, recv, t, k, chip, mine, c):
        return _remote(ins[t].at[mine, c], outs[t].at[mine, c], send.at[6 * t + k], recv.at[6 * t + k], (*chip, c))

    def start(self, ins, outs, send, recv):
        x, y, c = _mesh_pos()
        for t in range(len(ins)):
            for k, chip in enumerate(_other_chips(x, y)):
                self._ici(ins, outs, send, recv, t, k, chip, 2 * x + y, c).start()

    def _pass_on(self, outs, send, recv, t, k, chip, c, to):
        blk = outs[t].at[2 * chip[0] + chip[1], c]
        return _remote(blk, blk, send.at[6 * t + 3 + k], recv.at[6 * t + 3 + k], to)

    def near_end(self, ins, outs, send, recv):
        x, y, c = _mesh_pos()
        for t in range(len(ins)):
            for k, chip in enumerate(_other_chips(x, y)):
                blk = outs[t].at[2 * chip[0] + chip[1], c]
                _remote(blk, blk, send.at[6 * t + k], recv.at[6 * t + k], (x, y, c)).wait_recv()
                self._pass_on(outs, send, recv, t, k, chip, c, (x, y, 1 - c)).start()

    def finish(self, ins, outs, send, recv):
        x, y, c = _mesh_pos()
        chips = _other_chips(x, y)
        for t in range(len(ins)):
            for k, chip in enumerate(chips):
                self._pass_on(outs, send, recv, t, k, chip, 1 - c, (x, y, c)).wait_recv()
        for t in range(len(ins)):
            for k, chip in enumerate(chips):
                self._ici(ins, outs, send, recv, t, k, chip, 2 * x + y, c).wait_send()
                self._pass_on(outs, send, recv, t, k, chip, c, (x, y, 1 - c)).wait_send()


class _PairSwap:
    def __init__(self, grads):
        self.ins = list(grads)
        self.out_shapes = [jax.ShapeDtypeStruct((g.shape[0],) + g.shape[2:], g.dtype) for g in grads]
        self.aliases = {}
        self.n_sems = len(grads)

    def _copies(self, ins, outs, send, recv):
        x, y, c = _mesh_pos()
        return [_remote(ins[t].at[:, 1 - c], outs[t], send.at[t], recv.at[t], (x, y, 1 - c)) for t in range(len(ins))]

    def start(self, ins, outs, send, recv):
        for cp in self._copies(ins, outs, send, recv):
            cp.start()

    def finish(self, ins, outs, send, recv):
        for cp in self._copies(ins, outs, send, recv):
            cp.wait()


class _ChipSwap:
    def __init__(self, parts):
        self.ins = list(parts)
        self.out_shapes = [jax.ShapeDtypeStruct((3,) + p.shape[1:], p.dtype) for p in parts]
        self.aliases = {}
        self.n_sems = 3 * len(parts)

    def _copies(self, ins, outs, send, recv):
        x, y, c = _mesh_pos()
        return [_remote(ins[t].at[2 * chip[0] + chip[1]], outs[t].at[k], send.at[3 * t + k], recv.at[3 * t + k], (*chip, c))
                for t in range(len(ins)) for k, chip in enumerate(_other_chips(x, y))]

    def start(self, ins, outs, send, recv):
        for cp in self._copies(ins, outs, send, recv):
            cp.start()

    def finish(self, ins, outs, send, recv):
        for cp in self._copies(ins, outs, send, recv):
            cp.wait()


class _PairShare:
    def __init__(self, fulls):
        self.ins = list(fulls)
        self.out_shapes = _like(fulls)
        self.aliases = {t: t for t in range(len(fulls))}
        self.n_sems = len(fulls)

    def _copies(self, ins, outs, send, recv):
        x, y, c = _mesh_pos()
        return [_remote(ins[t].at[c], outs[t].at[c], send.at[t], recv.at[t], (x, y, 1 - c)) for t in range(len(ins))]

    def start(self, ins, outs, send, recv):
        for cp in self._copies(ins, outs, send, recv):
            cp.start()

    def finish(self, ins, outs, send, recv):
        for cp in self._copies(ins, outs, send, recv):
            cp.wait()


class _ShareAll:
    def __init__(self, arrays):
        self.ins = list(arrays)
        self.out_shapes = [jax.ShapeDtypeStruct((N_DEV,) + a.shape, a.dtype) for a in arrays]
        self.aliases = {}
        self.n_sems = (N_DEV - 1) * len(arrays)

    def _peers(self):
        x, y, c = _mesh_pos()
        flips = [((r >> 2) & 1, (r >> 1) & 1, r & 1) for r in range(1, N_DEV)]
        return (x, y, c), [(x ^ fx, y ^ fy, c ^ fc) for fx, fy, fc in flips]

    def _sends(self, ins, outs, send, recv):
        (x, y, c), peers = self._peers()
        mine = 4 * x + 2 * y + c
        return [_remote(ins[a], outs[a].at[mine], send.at[7 * a + r], recv.at[7 * a + r], peer)
                for a in range(len(ins)) for r, peer in enumerate(peers)]

    def start(self, ins, outs, send, recv):
        for cp in self._sends(ins, outs, send, recv):
            cp.start()

    def finish(self, ins, outs, send, recv):
        (x, y, c), peers = self._peers()
        for a in range(len(ins)):
            for r, (px, py, pc) in enumerate(peers):
                blk = outs[a].at[4 * px + 2 * py + pc]
                _remote(blk, blk, send.at[7 * a + r], recv.at[7 * a + r], (x, y, c)).wait_recv()
        for cp in self._sends(ins, outs, send, recv):
            cp.wait_send()


def _pallas(body, operands, *, name, grid, in_specs, out_specs, out_shape, scratch_shapes=(), vmem_mib=32, riders=(),
            prefetch=None):
    in_specs, out_specs, out_shape, scratch_shapes = list(in_specs), list(out_specs), list(out_shape), list(scratch_shapes)
    if not riders and prefetch is None:
        outs = pl.pallas_call(body, name=name, grid=grid, in_specs=in_specs, out_specs=out_specs, out_shape=out_shape,
                              scratch_shapes=scratch_shapes, compiler_params=_params(vmem_mib, len(grid)))(*operands)
        return list(outs), []
    n_in, n_out, n_scr = len(in_specs), len(out_specs), len(scratch_shapes)
    r_in = [len(r.ins) for r in riders]
    r_out = [len(r.out_shapes) for r in riders]
    steps = 1
    for g in grid:
        steps *= g

    n_pre = 0 if prefetch is None else 1

    def wrapped(*refs):
        refs = list(refs)
        pre, refs = refs[:n_pre], refs[n_pre:]
        ins, refs = refs[:n_in], refs[n_in:]
        rins = []
        for k in r_in:
            rins.append(refs[:k])
            refs = refs[k:]
        outs, refs = refs[:n_out], refs[n_out:]
        routs = []
        for k in r_out:
            routs.append(refs[:k])
            refs = refs[k:]
        scr, sems = refs[:n_scr], refs[n_scr:]
        step = 0
        for ax, g in enumerate(grid):
            step = step * g + pl.program_id(ax)

        def each(what):
            for j, r in enumerate(riders):
                if hasattr(r, what):
                    getattr(r, what)(rins[j], routs[j], sems[2 * j], sems[2 * j + 1])

        if grid:
            pl.when(step == 0)(lambda: each("start"))
        else:
            each("start")
        body(*pre, *ins, *outs, *scr)
        if grid:
            pl.when(step == max(steps - 2, 0))(lambda: each("near_end"))
            pl.when(step == steps - 1)(lambda: each("finish"))
        else:
            each("near_end")
            each("finish")

    aliases, off_in, off_out = {}, n_pre + n_in, n_out
    for r, ki, ko in zip(riders, r_in, r_out):
        for i, o in r.aliases.items():
            aliases[off_in + i] = off_out + o
        off_in, off_out = off_in + ki, off_out + ko
    sems = []
    for r in riders:
        sems += [pltpu.SemaphoreType.DMA((r.n_sems,)), pltpu.SemaphoreType.DMA((r.n_sems,))]
    layout = dict(grid=grid, in_specs=in_specs + [ANY] * sum(r_in), out_specs=out_specs + [ANY] * sum(r_out),
                  scratch_shapes=scratch_shapes + sems)
    if prefetch is not None:
        layout = dict(grid_spec=pltpu.PrefetchScalarGridSpec(num_scalar_prefetch=1, **layout))
    res = pl.pallas_call(
        wrapped, name=name, **layout,
        out_shape=out_shape + [s for r in riders for s in r.out_shapes], input_output_aliases=aliases,
        compiler_params=pltpu.CompilerParams(dimension_semantics=("arbitrary",) * len(grid),
                                             vmem_limit_bytes=vmem_mib * MIB, has_side_effects=True),
    )(*([] if prefetch is None else [prefetch]), *operands, *[a for r in riders for a in r.ins])
    res = list(res)
    outs, res = res[:n_out], res[n_out:]
    routs = []
    for k in r_out:
        routs.append(res[:k])
        res = res[k:]
    return outs, routs


def _exchange(riders, name):
    return _pallas(lambda: None, [], name=name, grid=(), in_specs=[], out_specs=[], out_shape=[], riders=riders)[1]


def _in_hbm(a):
    return pltpu.with_memory_space_constraint(a, pltpu.HBM)


def _place_shard(w, layer, dtype, name):
    _, rows, cols = w.shape
    half = rows // 2
    br = _block_rows(half)
    nb = half // br
    mine = 2 * lax.axis_index("x") + lax.axis_index("y")

    def body(q_ref, w_ref, o_ref):
        o_ref[...] = w_ref[...].astype(dtype)

    return pl.pallas_call(
        body, name=name,
        grid_spec=pltpu.PrefetchScalarGridSpec(
            num_scalar_prefetch=1, grid=(2, nb),
            in_specs=[pl.BlockSpec((None, br, cols), lambda h, i, q: (layer, h * nb + i, 0))],
            out_specs=pl.BlockSpec((None, None, br, cols), lambda h, i, q: (q[0], h, i, 0))),
        out_shape=pltpu.HBM((N_CHIPS, 2, half, cols), dtype),
        compiler_params=_params(16, 2),
    )(jnp.reshape(mine, (1,)).astype(jnp.int32), w)


def _add_pair(g, recv, name):
    _, _, r, cdim = g.shape
    br = _block_rows(r, 256)
    c = lax.axis_index("c")

    def body(c_ref, g_ref, r_ref, o_ref):
        o_ref[...] = (g_ref[...] + r_ref[...]).astype(BF16)

    return pl.pallas_call(
        body, name=name,
        grid_spec=pltpu.PrefetchScalarGridSpec(
            num_scalar_prefetch=1, grid=(N_CHIPS, r // br),
            in_specs=[pl.BlockSpec((None, None, br, cdim), lambda q, i, c_ref: (q, c_ref[0], i, 0)),
                      pl.BlockSpec((None, br, cdim), lambda q, i, c_ref: (q, i, 0))],
            out_specs=pl.BlockSpec((None, br, cdim), lambda q, i, c_ref: (q, i, 0))),
        out_shape=pltpu.HBM((N_CHIPS, r, cdim), BF16),
        compiler_params=_params(16, 2),
    )(jnp.reshape(c, (1,)).astype(jnp.int32), _in_hbm(g), _in_hbm(recv))


def _add_chips(own, recv, name):
    _, r, cdim = own.shape
    br = _block_rows(r, 256)
    x, y, c = _mesh_pos()

    def body(pos_ref, own_ref, r_ref, o_ref):
        acc = own_ref[...].astype(F32)
        for k in range(3):
            acc = acc + r_ref[k].astype(F32)
        o_ref[...] = acc

    return pl.pallas_call(
        body, name=name,
        grid_spec=pltpu.PrefetchScalarGridSpec(
            num_scalar_prefetch=1, grid=(r // br,),
            in_specs=[pl.BlockSpec((None, br, cdim), lambda i, pos: (pos[0], i, 0)),
                      pl.BlockSpec((3, br, cdim), lambda i, pos: (0, i, 0))],
            out_specs=pl.BlockSpec((None, br, cdim), lambda i, pos: (pos[1], i, 0))),
        out_shape=pltpu.HBM((2, r, cdim), F32),
        compiler_params=_params(16, 1),
    )(jnp.stack([2 * x + y, c]).astype(jnp.int32), _in_hbm(own), _in_hbm(recv))


def _adam_math(w, m, v, g):
    c1 = 1.0 / (1.0 - ADAM_B1 ** ADAM_STEP)
    c2 = 1.0 / (1.0 - ADAM_B2 ** ADAM_STEP)
    m_new = ADAM_B1 * m + (1.0 - ADAM_B1) * g
    v_new = ADAM_B2 * v + (1.0 - ADAM_B2) * (g * g)
    return -ADAM_LR * ((m_new * c1) / (jnp.sqrt(v_new * c2) + ADAM_EPS) + ADAM_WD * w), m_new, v_new


SMALL_WEIGHTS = [
    ("ev_norm_g", (1, D_MODEL), ["ev_norm_g"], None), ("ev_conv_a_b", (1, A_DIM), ["ev_conv_a_b"], None),
    ("ev_ln_a_g", (1, A_DIM), ["ev_ln_a_g"], None), ("ev_ln_a_b", (1, A_DIM), ["ev_ln_a_b"], None),
    ("od_w_s", (C_GROUPS, CHUNK, CHUNK), ["od_w_s_lo", "od_w_s_hi"], None), ("od_b_s", (C_GROUPS, CHUNK), ["od_b_s"], None),
    ("mlp_norm_g", (2, D_MODEL), ["mlp_norm_g0", "mlp_norm_g1"], None), ("final_norm_g", (1, D_MODEL), ["final_norm_g"], None),
    ("ev_conv_a_w", (A_CONV_WIDTH, A_DIM // N_CHIPS), ["ev_conv_a_w"], A_DIM // N_CHIPS),
    ("ev_conv_b_w", (B_CONV_WIDTH, B_DIM // N_CHIPS), ["ev_conv_b_w"], B_DIM // N_CHIPS),
    ("od_norm_g", (1, D_MODEL // N_CHIPS), ["od_norm_g"], D_MODEL // N_CHIPS),
    ("od_b_in", (1, 2 * C_DIM // N_CHIPS), ["od_b_in"], 2 * C_DIM // N_CHIPS),
    ("od_ln_v_g", (1, C_DIM // N_CHIPS), ["od_ln_v_g"], C_DIM // N_CHIPS),
    ("od_ln_v_b", (1, C_DIM // N_CHIPS), ["od_ln_v_b"], C_DIM // N_CHIPS),
]


def _small_update(own, landed, weights):
    names = list(own.keys())
    n_g, n_w = len(names), len(SMALL_WEIGHTS)

    def body(*refs):
        refs = list(refs)
        own_refs = dict(zip(names, refs[:n_g]))
        land_refs = dict(zip(names, refs[n_g:2 * n_g]))
        wmv = [refs[2 * n_g + 3 * i:2 * n_g + 3 * i + 3] for i in range(n_w)]
        o0 = 2 * n_g + 3 * n_w
        loss_ref = refs[o0]
        outs = [refs[o0 + 1 + 4 * i:o0 + 5 + 4 * i] for i in range(n_w)]
        acc = dict(zip(names, refs[o0 + 1 + 4 * n_w:]))
        x, y, c = _mesh_pos()
        mine, chip = 4 * x + 2 * y + c, 2 * x + y

        for nm in names:
            for d in range(N_DEV):
                def add(term, nm=nm, d=d):
                    acc[nm][...] = term if d == 0 else acc[nm][...] + term
                pl.when(mine == d)(lambda nm=nm, add=add: add(own_refs[nm][...]))
                pl.when(mine != d)(lambda nm=nm, d=d, add=add: add(land_refs[nm][d]))
        loss_ref[...] = acc["loss"][...]

        def update(i, rows, g):
            w_ref, m_ref, v_ref = wmv[i]
            delta, m_new, v_new = _adam_math(w_ref[rows], m_ref[rows], v_ref[rows], g)
            for ref, val in zip(outs[i], (g, delta, m_new, v_new)):
                ref[rows] = val

        for i, (_, shape, grads, per_chip) in enumerate(SMALL_WEIGHTS):
            for row, gname in enumerate(grads):
                per_grad = shape[0] // len(grads)
                rows = slice(row * per_grad, (row + 1) * per_grad)
                if per_chip is None:
                    update(i, rows, acc[gname][...])
                else:
                    for q in range(N_CHIPS):
                        pl.when(chip == q)(lambda i=i, rows=rows, gname=gname, q=q, per_chip=per_chip:
                                           update(i, rows, acc[gname][:, q * per_chip:(q + 1) * per_chip]))

    operands = [own[nm] for nm in names] + [landed[nm] for nm in names]
    for nm, _, _, _ in SMALL_WEIGHTS:
        operands += list(weights[nm])
    out_shape = [jax.ShapeDtypeStruct((1, 1), F32)]
    for _, shape, _, _ in SMALL_WEIGHTS:
        out_shape += [jax.ShapeDtypeStruct(shape, F32)] * 4
    res = pl.pallas_call(
        body, name="small_update", grid=(1,),
        in_specs=[_full_spec(a.shape) for a in operands], out_specs=[_full_spec(s.shape) for s in out_shape],
        out_shape=out_shape, scratch_shapes=[pltpu.VMEM(own[nm].shape, F32) for nm in names],
        compiler_params=_params(32, 1),
    )(*[_in_hbm(a) for a in operands])
    return res[0], {nm: res[1 + 4 * i:5 + 4 * i] for i, (nm, _, _, _) in enumerate(SMALL_WEIGHTS)}


def _adamw(w, m, v, grads, name, riders=()):
    layers, r, cdim = w.shape
    br = _block_rows(r, 256 if cdim > LANES else 1024)

    def body(*refs):
        w_ref, m_ref, v_ref = refs[:3]
        g_refs = refs[3:3 + layers]
        go_ref, d_ref, mo_ref, vo_ref = refs[3 + layers:]
        layer = pl.program_id(0)
        for l in range(layers):
            @pl.when(layer == l)
            def _(l=l):
                g = g_refs[l][...]
                go_ref[...] = g
                d_ref[...], mo_ref[...], vo_ref[...] = _adam_math(w_ref[...], m_ref[...], v_ref[...], g)

    spec3 = pl.BlockSpec((None, br, cdim), lambda l, i: (l, i, 0))
    spec2 = pl.BlockSpec((br, cdim), lambda l, i: (i, 0))
    out = jax.ShapeDtypeStruct((layers, r, cdim), F32)
    return _pallas(body, [w, m, v, *[_in_hbm(g) for g in grads]], name=name, grid=(layers, r // br),
                   in_specs=[spec3, spec3, spec3] + [spec2] * layers, out_specs=[spec3] * 4, out_shape=[out] * 4,
                   vmem_mib=32, riders=riders)


def _fill_shifted(buf, rows):
    for b in range(1, SUBLANES):
        buf[b, 0:rows - SUBLANES, :] = buf[0, b:b + rows - SUBLANES, :]


def _window(buf, start, size):
    return buf[start % SUBLANES, start - start % SUBLANES:start - start % SUBLANES + size, :]


def _conv31(src, w_ref, r0, base, init):
    acc = init
    for k in range(A_CONV_WIDTH):
        acc = acc + w_ref[k:k + 1, :] * _window(src, base + k + r0, CONV_ROWS)
    return acc


def _fwd_even(x, norm_g, w_in, conv_a_w, conv_a_b, ln_g, ln_b, conv_b_w, w_out, *, tm, seq, riders=()):
    tokens = x.shape[0]
    nt, tps = tokens // tm, seq // tm

    def body(x_ref, g_ref, win_hbm, caw_ref, cab_ref, lng_ref, lnb_ref, cbw_ref, wout_hbm,
             h_ref, n_ref, z_ref, a2_ref, cv_ref, mix_ref, win_v, wout_v, pa, pb, sem):
        i = pl.program_id(0)

        _load_weights([(win_hbm, win_v), (wout_hbm, wout_v)], sem)

        xv = x_ref[...]
        nf, _ = _rms_fwd(xv, g_ref[...])
        n = nf.astype(BF16)
        n_ref[...] = n
        z = jnp.concatenate([_dot(n, win_v[j]) for j in range(N_CHIPS)], axis=1)
        z_ref[...] = z.astype(BF16)
        a_val, a_gate = z[:, 0:A_DIM], z[:, A_DIM:2 * A_DIM]
        b_gate, c_gate, b_val = z[:, 1024:1536], z[:, 1536:2048], z[:, 2048:2560]

        first = (i % tps) == 0

        @pl.when(first)
        def _():
            pa[0, 0:A_HALO, :] = jnp.zeros((A_HALO, A_DIM), F32)
            pb[0:B_HALO, :] = jnp.zeros((B_HALO, B_DIM), F32)

        @pl.when(jnp.logical_not(first))
        def _():
            pa[0, 0:A_HALO, :] = pa[0, tm:tm + A_HALO, :]
            pb[0:B_HALO, :] = pb[tm:tm + B_HALO, :]

        pa[0, A_HALO:A_HALO + tm, :] = a_val * jax.nn.sigmoid(a_gate)
        pb[B_HALO:B_HALO + tm, :] = c_gate * b_val
        _fill_shifted(pa, A_HALO + tm)
        bias = jnp.broadcast_to(cab_ref[...], (CONV_ROWS, A_DIM))
        for r0 in range(0, tm, CONV_ROWS):
            a2_ref[r0:r0 + CONV_ROWS, :] = _conv31(pa, caw_ref, r0, A_HALO - (A_CONV_WIDTH - 1), bias)
        xhat, _ = _ln_stats(a2_ref[...])
        a3 = xhat * lng_ref[...] + lnb_ref[...]
        a4 = a3 * jax.nn.sigmoid(a3)
        cv = cbw_ref[0:1, :] * pb[B_HALO - 2:B_HALO - 2 + tm, :]
        cv = cv + cbw_ref[1:2, :] * pb[B_HALO - 1:B_HALO - 1 + tm, :]
        cv = cv + cbw_ref[2:3, :] * pb[B_HALO:B_HALO + tm, :]
        cv_ref[...] = cv.astype(BF16)
        mix = jnp.concatenate([a4, b_gate * cv], axis=1).astype(BF16)
        mix_ref[...] = mix
        h_ref[...] = xv + _dot(mix, wout_v[...])

    shp = lambda cols, dt: jax.ShapeDtypeStruct((tokens, cols), dt)
    return _pallas(
        body, [x, norm_g, w_in, conv_a_w, conv_a_b, ln_g, ln_b, conv_b_w, w_out], name="fwd_even", grid=(nt,),
        in_specs=[_row_spec(tm, D_MODEL), _full_spec((1, D_MODEL)), ANY, _full_spec((A_CONV_WIDTH, A_DIM)),
                  _full_spec((1, A_DIM)), _full_spec((1, A_DIM)), _full_spec((1, A_DIM)),
                  _full_spec((B_CONV_WIDTH, B_DIM)), ANY],
        out_specs=[_row_spec(tm, D_MODEL), _row_spec(tm, D_MODEL), _row_spec(tm, IN_EVEN), _row_spec(tm, A_DIM),
                   _row_spec(tm, B_DIM), _row_spec(tm, D_MODEL)],
        out_shape=[shp(D_MODEL, F32), shp(D_MODEL, BF16), shp(IN_EVEN, BF16), shp(A_DIM, F32), shp(B_DIM, BF16),
                   shp(D_MODEL, BF16)],
        scratch_shapes=[pltpu.VMEM((N_CHIPS, D_MODEL, IN_EVEN // N_CHIPS), BF16), pltpu.VMEM((D_MODEL, D_MODEL), BF16),
                        pltpu.VMEM((SUBLANES, A_HALO + tm, A_DIM), F32), pltpu.VMEM((B_HALO + tm, B_DIM), F32),
                        pltpu.SemaphoreType.DMA((N_LOADS,))],
        vmem_mib=56, riders=riders)


def _loss_tail(xv, g, target, loss_ref, dh_ref, dhb_ref, dg_ref):
    @pl.when(pl.program_id(0) == 0)
    def _():
        loss_ref[...] = jnp.zeros((1, 1), F32)
        dg_ref[...] = jnp.zeros((1, D_MODEL), F32)

    out, rstd = _rms_fwd(xv, g)
    err = out - target
    per_token = jnp.sum(err * err, axis=1, keepdims=True) * (1.0 / D_MODEL)
    loss_ref[...] += 0.5 * jnp.sum(per_token, axis=0, keepdims=True)
    dx, dg = _rms_bwd(err * (1.0 / D_MODEL), xv, rstd, g)
    dh_ref[...] = dx
    dhb_ref[...] = dx.astype(BF16)
    dg_ref[...] += dg


def _fwd_mlp(h, norm_g, w1, w2, layer, *, tm, riders=(), head=None):
    tokens = h.shape[0]
    nt = tokens // tm
    fs = D_FF // N_CHIPS
    n_in = 4 if head is None else 6

    def body(*refs):
        h_ref, g_ref, w1_hbm, w2_hbm = refs[:4]
        w1_v, w2_v, sem = refs[-3:]
        outs = refs[n_in:-3]
        n_ref, p_ref, q_ref = outs[1:4] if head is None else outs[0:3]
        _load_weights([(w1_hbm, w1_v), (w2_hbm, w2_v)], sem)

        xv = h_ref[...]
        nf, _ = _rms_fwd(xv, g_ref[...])
        n = nf.astype(BF16)
        n_ref[...] = n
        acc = xv
        for j in range(N_CHIPS):
            p = _dot(n, w1_v[j])
            p_ref[:, j * fs:(j + 1) * fs] = p.astype(BF16)
            r = jnp.maximum(p, 0.0)
            q = (r * r).astype(BF16)
            q_ref[:, j * fs:(j + 1) * fs] = q
            acc = acc + _dot(q, w2_v[j])
        if head is None:
            outs[0][...] = acc
        else:
            _loss_tail(acc, refs[4][...], refs[5][...], *outs[3:7])

    shp = lambda cols, dt: jax.ShapeDtypeStruct((tokens, cols), dt)
    saved_specs = [_row_spec(tm, D_MODEL), _row_spec(tm, D_FF), _row_spec(tm, D_FF)]
    saved_shapes = [shp(D_MODEL, BF16), shp(D_FF, BF16), shp(D_FF, BF16)]
    if head is None:
        operands, in_specs = [h, norm_g, w1, w2], [_row_spec(tm, D_MODEL), _full_spec((1, D_MODEL)), ANY, ANY]
        out_specs, out_shape = [_row_spec(tm, D_MODEL)] + saved_specs, [shp(D_MODEL, F32)] + saved_shapes
    else:
        operands = [h, norm_g, w1, w2, *head]
        in_specs = [_row_spec(tm, D_MODEL), _full_spec((1, D_MODEL)), ANY, ANY, _full_spec((1, D_MODEL)), _row_spec(tm, D_MODEL)]
        out_specs = saved_specs + [_full_spec((1, 1)), _row_spec(tm, D_MODEL), _row_spec(tm, D_MODEL), _full_spec((1, D_MODEL))]
        out_shape = saved_shapes + [jax.ShapeDtypeStruct((1, 1), F32), shp(D_MODEL, F32), shp(D_MODEL, BF16),
                                    jax.ShapeDtypeStruct((1, D_MODEL), F32)]
    return _pallas(
        body, operands, name=f"fwd_mlp{layer}", grid=(nt,), in_specs=in_specs, out_specs=out_specs, out_shape=out_shape,
        scratch_shapes=[pltpu.VMEM((N_CHIPS, D_MODEL, fs), BF16), pltpu.VMEM((N_CHIPS, fs, D_MODEL), BF16),
                        pltpu.SemaphoreType.DMA((N_LOADS,))],
        vmem_mib=56, riders=riders)


def _tril_mask():
    row = lax.broadcasted_iota(jnp.int32, (CHUNK, CHUNK), 0)
    col = lax.broadcasted_iota(jnp.int32, (CHUNK, CHUNK), 1)
    return row >= col


def _triu_mask():
    row = lax.broadcasted_iota(jnp.int32, (CHUNK, CHUNK), 0)
    col = lax.broadcasted_iota(jnp.int32, (CHUNK, CHUNK), 1)
    return row <= col


def _fwd_odd(h, norm_g, w_in, b_in, ln_g, ln_b, w_s, b_s_rows, w_out, *, tm, riders=()):
    tokens = h.shape[0]
    nt = tokens // tm
    cs = 2 * C_DIM // N_CHIPS

    def body(h_ref, g_ref, win_hbm, bin_ref, lng_ref, lnb_ref, ws_ref, bs_ref, wout_hbm,
             ho_ref, n_ref, s_ref, cdf_ref, sv_ref, y_ref, win_v, wout_v, bd, sem):
        _load_weights([(win_hbm, win_v), (wout_hbm, wout_v)], sem)

        @pl.when(pl.program_id(0) == 0)
        def _():
            mask = _tril_mask()
            bd[...] = jnp.zeros(bd.shape, BF16)
            for g in range(C_GROUPS):
                w = jnp.where(mask, ws_ref[g], 0.0).astype(BF16)
                bd[g, 0:CHUNK, 0:CHUNK] = w
                bd[g, CHUNK:PAIR, CHUNK:PAIR] = w

        xv = h_ref[...]
        nf, _ = _rms_fwd(xv, g_ref[...])
        n = nf.astype(BF16)
        n_ref[...] = n
        s = jnp.concatenate([_dot(n, win_v[j]) for j in range(N_CHIPS)], axis=1) + bin_ref[...]
        s_ref[...] = s.astype(BF16)
        cdf = _gelu_cdf(s)
        cdf_ref[...] = cdf.astype(BF16)
        zz = s * cdf
        u, v = zz[:, 0:C_DIM], zz[:, C_DIM:2 * C_DIM]
        xhat, _ = _ln_stats(v)
        vn = (xhat * lng_ref[...] + lnb_ref[...]).astype(BF16)
        for g in range(C_GROUPS):
            cols = slice(g * CHUNK, (g + 1) * CHUNK)
            bias = jnp.concatenate([bs_ref[g], bs_ref[g]], axis=0)
            for r0 in range(0, tm, PAIR):
                sv = _dot(bd[g], vn[r0:r0 + PAIR, cols]) + bias
                sv_ref[r0:r0 + PAIR, cols] = sv.astype(BF16)
                y_ref[r0:r0 + PAIR, cols] = (u[r0:r0 + PAIR, cols] * sv).astype(BF16)
        ho_ref[...] = xv + _dot(y_ref[...], wout_v[...])

    shp = lambda cols, dt: jax.ShapeDtypeStruct((tokens, cols), dt)
    return _pallas(
        body, [h, norm_g, w_in, b_in, ln_g, ln_b, w_s, b_s_rows, w_out], name="fwd_odd", grid=(nt,),
        in_specs=[_row_spec(tm, D_MODEL), _full_spec((1, D_MODEL)), ANY, _full_spec((1, 2 * C_DIM)),
                  _full_spec((1, C_DIM)), _full_spec((1, C_DIM)), _full_spec((C_GROUPS, CHUNK, CHUNK)),
                  _full_spec((C_GROUPS, CHUNK, CHUNK)), ANY],
        out_specs=[_row_spec(tm, D_MODEL), _row_spec(tm, D_MODEL), _row_spec(tm, 2 * C_DIM), _row_spec(tm, 2 * C_DIM),
                   _row_spec(tm, C_DIM), _row_spec(tm, C_DIM)],
        out_shape=[shp(D_MODEL, F32), shp(D_MODEL, BF16), shp(2 * C_DIM, BF16), shp(2 * C_DIM, BF16), shp(C_DIM, BF16),
                   shp(C_DIM, BF16)],
        scratch_shapes=[pltpu.VMEM((N_CHIPS, D_MODEL, cs), BF16), pltpu.VMEM((C_DIM, D_MODEL), BF16),
                        pltpu.VMEM((C_GROUPS, PAIR, PAIR), BF16), pltpu.SemaphoreType.DMA((N_LOADS,))],
        vmem_mib=56, riders=riders)


def _bwd_mlp(dh, h, norm_g, p, w1, w2, layer, *, tm, riders=()):
    tokens = h.shape[0]
    nt = tokens // tm
    fs = D_FF // N_CHIPS

    def body(dh_ref, h_ref, g_ref, p_ref, w1_hbm, w2_hbm, dx_ref, dxb_ref, dp_ref, dg_ref, w1_v, w2_v, sem):
        @pl.when(pl.program_id(0) == 0)
        def _():
            dg_ref[...] = jnp.zeros((1, D_MODEL), F32)

        _load_weights([(w1_hbm, w1_v), (w2_hbm, w2_v)], sem)

        dhv = dh_ref[...]
        dhb = dhv.astype(BF16)
        dn = jnp.zeros((tm, D_MODEL), F32)
        for j in range(N_CHIPS):
            dq = _dot_nt(dhb, w2_v[j])
            r = jnp.maximum(p_ref[:, j * fs:(j + 1) * fs].astype(F32), 0.0)
            dp = ((2.0 * r) * dq).astype(BF16)
            dp_ref[:, j * fs:(j + 1) * fs] = dp
            dn = dn + _dot_nt(dp, w1_v[j])
        xv = h_ref[...]
        g = g_ref[...]
        _, rstd = _rms_fwd(xv, g)
        dx, dg = _rms_bwd(dn, xv, rstd, g)
        dx_ref[...] = dhv + dx
        dxb_ref[...] = (dhv + dx).astype(BF16)
        dg_ref[...] += dg

    return _pallas(
        body, [dh, h, norm_g, p, w1, w2], name=f"bwd_mlp{layer}", grid=(nt,),
        in_specs=[_row_spec(tm, D_MODEL), _row_spec(tm, D_MODEL), _full_spec((1, D_MODEL)), _row_spec(tm, D_FF), ANY, ANY],
        out_specs=[_row_spec(tm, D_MODEL), _row_spec(tm, D_MODEL), _row_spec(tm, D_FF), _full_spec((1, D_MODEL))],
        out_shape=[jax.ShapeDtypeStruct((tokens, D_MODEL), F32), jax.ShapeDtypeStruct((tokens, D_MODEL), BF16),
                   jax.ShapeDtypeStruct((tokens, D_FF), BF16), jax.ShapeDtypeStruct((1, D_MODEL), F32)],
        scratch_shapes=[pltpu.VMEM((N_CHIPS, D_MODEL, fs), BF16), pltpu.VMEM((N_CHIPS, fs, D_MODEL), BF16),
                        pltpu.SemaphoreType.DMA((N_LOADS,))],
        vmem_mib=56, riders=riders)


def _bwd_odd(dh, h, norm_g, s, cdf, sv, w_in, ln_g, ln_b, w_s, w_out, *, tm, riders=()):
    tokens = h.shape[0]
    nt = tokens // tm
    cs = 2 * C_DIM // N_CHIPS

    def body(dh_ref, h_ref, g_ref, s_ref, cdf_ref, sv_ref, win_hbm, lng_ref, lnb_ref, ws_ref, wout_hbm,
             dx_ref, dxb_ref, ds_ref, dg_ref, dbin_ref, dlng_ref, dlnb_ref, dws_ref, dbs_ref,
             win_v, wout_v, bdt, dws_acc, dbs_acc, dvn, sem):
        i = pl.program_id(0)

        _load_weights([(win_hbm, win_v), (wout_hbm, wout_v)], sem)

        @pl.when(i == 0)
        def _():
            mask_t = _triu_mask()
            bdt[...] = jnp.zeros(bdt.shape, BF16)
            for g in range(C_GROUPS):
                wt = jnp.where(mask_t, ws_ref[g].T, 0.0).astype(BF16)
                bdt[g, 0:CHUNK, 0:CHUNK] = wt
                bdt[g, CHUNK:PAIR, CHUNK:PAIR] = wt
            dws_acc[...] = jnp.zeros(dws_acc.shape, F32)
            dbs_acc[...] = jnp.zeros(dbs_acc.shape, F32)
            dg_ref[...] = jnp.zeros(dg_ref.shape, F32)
            dbin_ref[...] = jnp.zeros(dbin_ref.shape, F32)
            dlng_ref[...] = jnp.zeros(dlng_ref.shape, F32)
            dlnb_ref[...] = jnp.zeros(dlnb_ref.shape, F32)

        dhv = dh_ref[...]
        dy = _dot_nt(dhv.astype(BF16), wout_v[...])
        sf = s_ref[...].astype(F32)
        cdf = cdf_ref[...].astype(F32)
        pdf = jnp.exp(-0.5 * sf * sf) * 0.3989422804014327
        zz = sf * cdf
        dgelu = cdf + sf * pdf
        u, v = zz[:, 0:C_DIM], zz[:, C_DIM:2 * C_DIM]
        xhat, rs = _ln_stats(v)
        lng = lng_ref[...]
        vn = (xhat * lng + lnb_ref[...]).astype(BF16)
        du = dy * sv_ref[...].astype(F32)
        dsv = dy * u
        dsvb = dsv.astype(BF16)
        for g in range(C_GROUPS):
            cols = slice(g * CHUNK, (g + 1) * CHUNK)
            for r0 in range(0, tm, PAIR):
                blk = dsvb[r0:r0 + PAIR, cols]
                dvn[r0:r0 + PAIR, cols] = _dot(bdt[g], blk)
                dws_acc[g] += _dot_nt(blk, vn[r0:r0 + PAIR, cols])
                dbs_acc[g] += dsv[r0:r0 + CHUNK, cols] + dsv[r0 + CHUNK:r0 + PAIR, cols]
        dv, dlng, dlnb = _ln_bwd(dvn[...], xhat, rs, lng)
        dlng_ref[...] += dlng
        dlnb_ref[...] += dlnb
        ds = jnp.concatenate([du, dv], axis=1) * dgelu
        dbin_ref[...] += jnp.sum(ds, axis=0, keepdims=True)
        dsb = ds.astype(BF16)
        ds_ref[...] = dsb
        dn = jnp.zeros((tm, D_MODEL), F32)
        for j in range(N_CHIPS):
            dn = dn + _dot_nt(dsb[:, j * cs:(j + 1) * cs], win_v[j])
        xv = h_ref[...]
        g = g_ref[...]
        _, rstd = _rms_fwd(xv, g)
        dx, dg = _rms_bwd(dn, xv, rstd, g)
        dx_ref[...] = dhv + dx
        dxb_ref[...] = (dhv + dx).astype(BF16)
        dg_ref[...] += dg

        @pl.when(i == nt - 1)
        def _():
            mask = _tril_mask()
            for g in range(C_GROUPS):
                full = dws_acc[g]
                dws_ref[g] = jnp.where(mask, full[0:CHUNK, 0:CHUNK] + full[CHUNK:PAIR, CHUNK:PAIR], 0.0)
                dbs_ref[g:g + 1, :] = jnp.sum(dbs_acc[g].T, axis=0, keepdims=True)

    row = lambda cols: jax.ShapeDtypeStruct((1, cols), F32)
    return _pallas(
        body, [dh, h, norm_g, s, cdf, sv, w_in, ln_g, ln_b, w_s, w_out], name="bwd_odd", grid=(nt,),
        in_specs=[_row_spec(tm, D_MODEL), _row_spec(tm, D_MODEL), _full_spec((1, D_MODEL)), _row_spec(tm, 2 * C_DIM),
                  _row_spec(tm, 2 * C_DIM), _row_spec(tm, C_DIM), ANY, _full_spec((1, C_DIM)), _full_spec((1, C_DIM)),
                  _full_spec((C_GROUPS, CHUNK, CHUNK)), ANY],
        out_specs=[_row_spec(tm, D_MODEL), _row_spec(tm, D_MODEL), _row_spec(tm, 2 * C_DIM), _full_spec((1, D_MODEL)),
                   _full_spec((1, 2 * C_DIM)),
                   _full_spec((1, C_DIM)), _full_spec((1, C_DIM)), _full_spec((C_GROUPS, CHUNK, CHUNK)),
                   _full_spec((C_GROUPS, CHUNK))],
        out_shape=[jax.ShapeDtypeStruct((tokens, D_MODEL), F32), jax.ShapeDtypeStruct((tokens, D_MODEL), BF16),
                   jax.ShapeDtypeStruct((tokens, 2 * C_DIM), BF16),
                   row(D_MODEL), row(2 * C_DIM), row(C_DIM), row(C_DIM),
                   jax.ShapeDtypeStruct((C_GROUPS, CHUNK, CHUNK), F32), jax.ShapeDtypeStruct((C_GROUPS, CHUNK), F32)],
        scratch_shapes=[pltpu.VMEM((N_CHIPS, D_MODEL, cs), BF16), pltpu.VMEM((C_DIM, D_MODEL), BF16),
                        pltpu.VMEM((C_GROUPS, PAIR, PAIR), BF16), pltpu.VMEM((C_GROUPS, PAIR, PAIR), F32),
                        pltpu.VMEM((C_GROUPS, CHUNK, CHUNK), F32), pltpu.VMEM((tm, C_DIM), F32),
                        pltpu.SemaphoreType.DMA((N_LOADS,))],
        vmem_mib=56, riders=riders)


def _bwd_even(dh, x, norm_g, z, a2, cv, w_in, conv_a_w, ln_g, ln_b, conv_b_w, w_out, *, tm, seq, riders=()):
    tokens = x.shape[0]
    nt, tps = tokens // tm, seq // tm
    ws = IN_EVEN // N_CHIPS

    def body(dh_ref, x_ref, g_ref, z_ref, a2_ref, cv_ref, win_hbm, caw_ref, lng_ref, lnb_ref, cbw_ref, wout_hbm,
             dx_ref, dz_ref, dg_ref, dcaw_ref, dcab_ref, dlng_ref, dlnb_ref, dcbw_ref,
             win_v, wout_v, ea, eb, a1s, da1s, dw_acc, sem):
        i = pl.program_id(0)

        _load_weights([(win_hbm, win_v), (wout_hbm, wout_v)], sem)

        @pl.when(i == 0)
        def _():
            dw_acc[...] = jnp.zeros(dw_acc.shape, F32)
            for ref in (dg_ref, dcab_ref, dlng_ref, dlnb_ref, dcbw_ref):
                ref[...] = jnp.zeros(ref.shape, F32)

        dhv = dh_ref[...]
        dmix = _dot_nt(dhv.astype(BF16), wout_v[...])
        da4, dbo = dmix[:, 0:A_DIM], dmix[:, A_DIM:A_DIM + B_DIM]
        zf = z_ref[...].astype(F32)
        a_val, a_gate = zf[:, 0:A_DIM], zf[:, A_DIM:2 * A_DIM]
        b_gate, c_gate, b_val = zf[:, 1024:1536], zf[:, 1536:2048], zf[:, 2048:2560]

        xhat, rs = _ln_stats(a2_ref[...])
        lng = lng_ref[...]
        a3 = xhat * lng + lnb_ref[...]
        sg = jax.nn.sigmoid(a3)
        da3 = da4 * (sg * (1.0 + a3 * (1.0 - sg)))
        da2, dlng, dlnb = _ln_bwd(da3, xhat, rs, lng)
        dlng_ref[...] += dlng
        dlnb_ref[...] += dlnb
        dcab_ref[...] += jnp.sum(da2, axis=0, keepdims=True)

        last = ((nt - 1 - i) % tps) == tps - 1
        dcv = dbo * b_gate

        @pl.when(last)
        def _():
            ea[0, tm:tm + A_HALO, :] = jnp.zeros((A_HALO, A_DIM), F32)
            eb[tm:tm + B_HALO, :] = jnp.zeros((B_HALO, B_DIM), F32)

        @pl.when(jnp.logical_not(last))
        def _():
            ea[0, tm:tm + A_HALO, :] = ea[0, 0:A_HALO, :]
            eb[tm:tm + B_HALO, :] = eb[0:B_HALO, :]

        ea[0, 0:tm, :] = da2
        eb[0:tm, :] = dcv
        _fill_shifted(ea, tm + A_HALO)
        sig = jax.nn.sigmoid(a_gate)
        a1s[...] = a_val * sig
        for r0 in range(0, tm, CONV_ROWS):
            a1c = a1s[r0:r0 + CONV_ROWS, :]
            acc = jnp.zeros((CONV_ROWS, A_DIM), F32)
            for j in range(A_CONV_WIDTH):
                k = A_CONV_WIDTH - 1 - j
                sl = _window(ea, r0 + j, CONV_ROWS)
                acc = acc + caw_ref[k:k + 1, :] * sl
                dw_acc[k] += sl * a1c
            da1s[r0:r0 + CONV_ROWS, :] = acc
        da1 = da1s[...]
        da_val = da1 * sig
        da_gate = da1 * a_val * (sig * (1.0 - sig))

        db_gate = dbo * cv_ref[...].astype(F32)
        cb = c_gate * b_val
        dcb = jnp.zeros((tm, B_DIM), F32)
        for j in range(B_CONV_WIDTH):
            k = B_CONV_WIDTH - 1 - j
            sl = eb[j:j + tm, :]
            dcb = dcb + cbw_ref[k:k + 1, :] * sl
            dcbw_ref[k:k + 1, :] += jnp.sum(sl * cb, axis=0, keepdims=True)
        dz = jnp.concatenate([da_val, da_gate, db_gate, dcb * b_val, dcb * c_gate], axis=1).astype(BF16)
        dz_ref[...] = dz
        dn = jnp.zeros((tm, D_MODEL), F32)
        for j in range(N_CHIPS):
            dn = dn + _dot_nt(dz[:, j * ws:(j + 1) * ws], win_v[j])
        xv = x_ref[...]
        g = g_ref[...]
        _, rstd = _rms_fwd(xv, g)
        dx, dg = _rms_bwd(dn, xv, rstd, g)
        dx_ref[...] = dhv + dx
        dg_ref[...] += dg

        @pl.when(i == nt - 1)
        def _():
            for k in range(A_CONV_WIDTH):
                dcaw_ref[k:k + 1, :] = jnp.sum(dw_acc[k], axis=0, keepdims=True)

    row = lambda cols: jax.ShapeDtypeStruct((1, cols), F32)
    rs_ = functools.partial(_row_spec, rev_nt=nt)
    return _pallas(
        body, [dh, x, norm_g, z, a2, cv, w_in, conv_a_w, ln_g, ln_b, conv_b_w, w_out], name="bwd_even", grid=(nt,),
        in_specs=[rs_(tm, D_MODEL), rs_(tm, D_MODEL), _full_spec((1, D_MODEL)), rs_(tm, IN_EVEN), rs_(tm, A_DIM),
                  rs_(tm, B_DIM), ANY, _full_spec((A_CONV_WIDTH, A_DIM)), _full_spec((1, A_DIM)), _full_spec((1, A_DIM)),
                  _full_spec((B_CONV_WIDTH, B_DIM)), ANY],
        out_specs=[rs_(tm, D_MODEL), rs_(tm, IN_EVEN), _full_spec((1, D_MODEL)), _full_spec((A_CONV_WIDTH, A_DIM)),
                   _full_spec((1, A_DIM)), _full_spec((1, A_DIM)), _full_spec((1, A_DIM)), _full_spec((B_CONV_WIDTH, B_DIM))],
        out_shape=[jax.ShapeDtypeStruct((tokens, D_MODEL), F32), jax.ShapeDtypeStruct((tokens, IN_EVEN), BF16),
                   row(D_MODEL), jax.ShapeDtypeStruct((A_CONV_WIDTH, A_DIM), F32), row(A_DIM), row(A_DIM), row(A_DIM),
                   jax.ShapeDtypeStruct((B_CONV_WIDTH, B_DIM), F32)],
        scratch_shapes=[pltpu.VMEM((N_CHIPS, D_MODEL, ws), BF16), pltpu.VMEM((D_MODEL, D_MODEL), BF16),
                        pltpu.VMEM((SUBLANES, tm + A_HALO, A_DIM), F32), pltpu.VMEM((tm + B_HALO, B_DIM), F32),
                        pltpu.VMEM((tm, A_DIM), F32), pltpu.VMEM((tm, A_DIM), F32),
                        pltpu.VMEM((A_CONV_WIDTH, CONV_ROWS, A_DIM), F32), pltpu.SemaphoreType.DMA((N_LOADS,))],
        vmem_mib=56, riders=riders)


def _wgrad(a, b, name, *, col_shards, riders=()):
    tokens, m = a.shape
    n = b.shape[1]
    kc = 512
    if col_shards:
        bm, bn = m // 2, n // N_CHIPS
        grid = (2, N_CHIPS)
        out_spec = pl.BlockSpec((None, None, bm, bn), lambda i, j: (j, i, 0, 0))
    elif m // 8 >= MXU_ROWS:
        bm, bn = m // 8, n
        grid = (8, 1)
        out_spec = pl.BlockSpec((None, None, bm, bn), lambda i, j: (i // 2, i % 2, 0, 0))
    else:
        bm, bn = m // N_CHIPS, n
        grid = (N_CHIPS, 1)
        out_spec = pl.BlockSpec((None, 2, bm // 2, bn), lambda i, j: (i, 0, 0, 0))

    def body(a_ref, b_ref, o_ref):
        acc = jnp.zeros((bm, bn), F32)
        for k0 in range(0, tokens, kc):
            acc = acc + _dot_tn(a_ref[k0:k0 + kc, :].astype(BF16), b_ref[k0:k0 + kc, :].astype(BF16))
        if len(o_ref.shape) == 3:
            o_ref[0] = acc[0:bm // 2]
            o_ref[1] = acc[bm // 2:bm]
        else:
            o_ref[...] = acc

    out_rows = m // 2 if col_shards else m // 8
    outs, routs = _pallas(
        body, [a, b], name=name, grid=grid,
        in_specs=[pl.BlockSpec((tokens, bm), lambda i, j: (0, i)), pl.BlockSpec((tokens, bn), lambda i, j: (0, j))],
        out_specs=[out_spec], out_shape=[jax.ShapeDtypeStruct((N_CHIPS, 2, out_rows, bn), F32)],
        vmem_mib=56, riders=riders)
    return outs[0], routs


def _wgrad_pair(a, b, name, *, col_shards, riders=()):
    tokens, m = a.shape
    n = b.shape[1]
    kc = 512
    c = lax.axis_index("c")
    if col_shards:
        bm, bn = m // 2, n // N_CHIPS
        a_spec = pl.BlockSpec((tokens, bm), lambda ph, q, cr: (0, (ph + 1 + cr[0]) % 2))
        b_spec = pl.BlockSpec((tokens, bn), lambda ph, q, cr: (0, q))
    else:
        bm, bn = m // 8, n
        a_spec = pl.BlockSpec((tokens, bm), lambda ph, q, cr: (0, 2 * q + (ph + 1 + cr[0]) % 2))
        b_spec = pl.BlockSpec((tokens, bn), lambda ph, q, cr: (0, 0))

    def body(c_ref, a_ref, b_ref, o_ref, give, got, send_sems, recv_sems):
        ph, q = pl.program_id(0), pl.program_id(1)
        acc = jnp.zeros((bm, bn), F32)
        for k0 in range(0, tokens, kc):
            acc = acc + _dot_tn(a_ref[k0:k0 + kc, :].astype(BF16), b_ref[k0:k0 + kc, :].astype(BF16))
        x, y, cc = _mesh_pos()

        def tile(t):
            return _remote(give.at[t], got.at[t], send_sems.at[t], recv_sems.at[t], (x, y, 1 - cc))

        @pl.when(ph == 0)
        def _():
            give[q] = acc
            tile(q).start()

        @pl.when(ph == 1)
        def _():
            tile(q).wait_recv()
            o_ref[...] = (acc + got[q]).astype(BF16)

        @pl.when((ph == 1) & (q == N_CHIPS - 1))
        def _():
            for t in range(N_CHIPS):
                tile(t).wait_send()

    outs, routs = _pallas(
        body, [a, b], name=name, grid=(2, N_CHIPS), in_specs=[a_spec, b_spec],
        out_specs=[pl.BlockSpec((None, bm, bn), lambda ph, q, cr: (ph * q, 0, 0))],
        out_shape=[jax.ShapeDtypeStruct((N_CHIPS, bm, bn), BF16)],
        scratch_shapes=[pltpu.VMEM((N_CHIPS, bm, bn), F32), pltpu.VMEM((N_CHIPS, bm, bn), F32),
                        pltpu.SemaphoreType.DMA((N_CHIPS,)), pltpu.SemaphoreType.DMA((N_CHIPS,))],
        vmem_mib=56, riders=riders, prefetch=jnp.reshape(c, (1,)).astype(jnp.int32))
    return outs[0], routs


class _GradReduce:
    def __init__(self, name, grad=None, chip_sum=None):
        self.name, self.grad, self.chip_sum = name, grad, chip_sum
        self.full = None

    def pair_swap(self):
        return _PairSwap([self.grad])

    def took_pair(self, outs):
        self.chip_sum = _in_hbm(_add_pair(self.grad, outs[0], f"pair_sum_{self.name}"))

    def chip_swap(self):
        return _ChipSwap([self.chip_sum])

    def took_chips(self, outs):
        self.full = _in_hbm(_add_chips(self.chip_sum, outs[0], f"chip_sum_{self.name}"))

    def pair_share(self):
        return _PairShare([self.full])

    def took_share(self, outs):
        self.full = outs[0]

    def reduced(self):
        return jnp.reshape(self.full, (2 * self.full.shape[1], self.full.shape[2]))


def _forward_backward(x2, tgt2, gathered, staged, conv_a_w, conv_b_w, od_norm, od_bias, od_lng, od_lnb,
                      ev_norm_g, ev_conv_a_b, ev_ln_a_g, ev_ln_a_b, od_w_s, od_b_s, mlp_norm_g, final_norm_g,
                      *, tm, seq, distributed=True):
    d = x2.shape[1]
    w = dict(gathered)
    b_s_rows = jnp.broadcast_to(od_b_s[0][:, :, None], (C_GROUPS, CHUNK, CHUNK))

    def ride(*names):
        return [_Gather([staged[nm] for nm in names])] if distributed else []

    def land(routs, *names):
        if distributed:
            for nm, buf in zip(names, routs[0]):
                w[nm] = buf

    def as_cols(buf):
        return jnp.reshape(buf, (N_CHIPS, 2 * buf.shape[2], buf.shape[3]))

    def as_rows(buf):
        return jnp.reshape(buf, (8 * buf.shape[2], buf.shape[3]))

    (h1, n0, z, a2, cv, mix), routs = _fwd_even(
        x2, ev_norm_g, as_cols(w["ev_in"]), conv_a_w, ev_conv_a_b, ev_ln_a_g, ev_ln_a_b, conv_b_w, as_rows(w["ev_out"]),
        tm=tm, seq=seq, riders=ride("w1_0", "w2_0"))
    land(routs, "w1_0", "w2_0")
    (h2, n1, p0, q0), routs = _fwd_mlp(h1, mlp_norm_g[0:1], as_cols(w["w1_0"]), as_cols(w["w2_0"]), 0, tm=tm,
                                       riders=ride("od_in", "od_out", "w1_1"))
    land(routs, "od_in", "od_out", "w1_1")
    (h3, n2, s, cdf, sv, y), routs = _fwd_odd(h2, od_norm, as_cols(w["od_in"]), od_bias, od_lng, od_lnb, od_w_s[0], b_s_rows,
                                         as_rows(w["od_out"]), tm=tm, riders=ride("w2_1"))
    land(routs, "w2_1")
    (n3, p1, q1, loss_part, dh4, dh4b, d_final_g), _ = _fwd_mlp(
        h3, mlp_norm_g[1:2], as_cols(w["w1_1"]), as_cols(w["w2_1"]), 1, tm=tm,
        head=(jnp.reshape(final_norm_g, (1, d)), tgt2))

    red = {}

    def swap(*names):
        return [red[nm].pair_swap() for nm in names] if distributed else []

    def chips(*names):
        return [red[nm].chip_swap() for nm in names] if distributed else []

    def share(*names):
        return [red[nm].pair_share() for nm in names] if distributed else []

    def took(routs, *steps):
        if distributed:
            for (nm, what), outs in zip(steps, routs):
                getattr(red[nm], what)(outs)

    def big(lhs, rhs, name, col_shards, riders=()):
        if distributed:
            chip_sum, routs = _wgrad_pair(lhs, rhs, f"wgrad_{name}", col_shards=col_shards, riders=riders)
            red[name] = _GradReduce(name, chip_sum=_in_hbm(chip_sum))
        else:
            g, routs = _wgrad(lhs, rhs, f"wgrad_{name}", col_shards=col_shards)
            red[name] = _GradReduce(name, grad=g)
        return routs

    big(q1, dh4b, "w2_1", False)
    (dh3, dh3b, dp1, d_mlp_g1), routs = _bwd_mlp(dh4, h3, mlp_norm_g[1:2], p1, as_cols(w["w1_1"]), as_cols(w["w2_1"]), 1, tm=tm,
                                           riders=chips("w2_1"))
    took(routs, ("w2_1", "took_chips"))
    big(n3, dp1, "w1_1", True)
    g, routs = _wgrad(y, dh3b, "wgrad_od_out", col_shards=False, riders=share("w2_1"))
    red["od_out"] = _GradReduce("od_out", grad=g)
    took(routs, ("w2_1", "took_share"))
    (dh2, dh2b, ds, d_od_norm, d_od_bin, d_od_lng, d_od_lnb, d_ws, d_bs), routs = _bwd_odd(
        dh3, h2, od_norm, s, cdf, sv, as_cols(w["od_in"]), od_lng, od_lnb, od_w_s[0], as_rows(w["od_out"]), tm=tm,
        riders=chips("w1_1") + swap("od_out"))
    took(routs, ("w1_1", "took_chips"), ("od_out", "took_pair"))
    routs = big(n2, ds, "od_in", True, riders=share("w1_1"))
    took(routs, ("w1_1", "took_share"))
    half_groups = C_GROUPS // 2
    early = {"loss": loss_part, "od_w_s_lo": d_ws[:half_groups], "od_b_s": d_bs, "mlp_norm_g1": d_mlp_g1, "final_norm_g": d_final_g,
             "od_norm_g": d_od_norm, "od_b_in": d_od_bin, "od_ln_v_g": d_od_lng, "od_ln_v_b": d_od_lnb}
    share_early = [_ShareAll(list(early.values()))] if distributed else []
    routs = big(q0, dh2b, "w2_0", False, riders=share_early)
    landed_early = routs[0] if distributed else []
    (dh1, dh1b, dp0, d_mlp_g0), routs = _bwd_mlp(dh2, h1, mlp_norm_g[0:1], p0, as_cols(w["w1_0"]), as_cols(w["w2_0"]), 0, tm=tm,
                                           riders=chips("od_out") + chips("od_in") + chips("w2_0"))
    took(routs, ("od_out", "took_chips"), ("od_in", "took_chips"), ("w2_0", "took_chips"))
    middle = {"od_w_s_hi": d_ws[half_groups:]}
    share_middle = [_ShareAll(list(middle.values()))] if distributed else []
    routs = big(n1, dp0, "w1_0", True, riders=share("od_out") + share("od_in") + share("w2_0") + share_middle)
    took(routs, ("od_out", "took_share"), ("od_in", "took_share"), ("w2_0", "took_share"))
    landed_middle = routs[3] if distributed else []
    g, _ = _wgrad(mix, dh1b, "wgrad_ev_out", col_shards=False)
    red["ev_out"] = _GradReduce("ev_out", grad=g)

    (dx, dz, d_ev_norm, d_caw, d_cab, d_ev_lng, d_ev_lnb, d_cbw), routs = _bwd_even(
        dh1, x2, ev_norm_g, z, a2, cv, as_cols(w["ev_in"]), conv_a_w, ev_ln_a_g, ev_ln_a_b, conv_b_w, as_rows(w["ev_out"]),
        tm=tm, seq=seq, riders=chips("w1_0") + swap("ev_out"))
    took(routs, ("w1_0", "took_chips"), ("ev_out", "took_pair"))
    late = {"mlp_norm_g0": d_mlp_g0, "ev_norm_g": d_ev_norm, "ev_conv_a_b": d_cab, "ev_ln_a_g": d_ev_lng,
            "ev_ln_a_b": d_ev_lnb, "ev_conv_a_w": d_caw, "ev_conv_b_w": d_cbw}
    share_late = [_ShareAll(list(late.values()))] if distributed else []
    routs2 = big(n0, dz, "ev_in", True, riders=chips("ev_out") + share("w1_0") + share_late)
    took(routs2, ("ev_out", "took_chips"), ("w1_0", "took_share"))
    own = {**early, **middle, **late}
    landed = dict(zip(own.keys(), landed_early + landed_middle + routs2[2])) if distributed else None
    return dx, red, own, landed


def _rows128(a):
    rows = jnp.reshape(a, (-1, LANES))
    pad = (-rows.shape[0]) % SUBLANES
    return jnp.pad(rows, ((0, pad), (0, 0))) if pad else rows


def _pack(arrays):
    return jnp.concatenate([_rows128(a) for a in arrays], axis=0)


def _unpack(buf, shapes):
    out, r0 = [], 0
    for shp in shapes:
        size = 1
        for dim in shp:
            size *= dim
        nr = size // LANES
        out.append(jnp.reshape(buf[r0:r0 + nr], shp))
        r0 += nr + (-nr) % SUBLANES
    return out


def kernel(x, ev_norm_g, ev_w_in, ev_conv_a_w, ev_conv_a_b, ev_ln_a_g, ev_ln_a_b, ev_conv_b_w, ev_w_out, od_norm_g, od_w_in, od_b_in, od_ln_v_g, od_ln_v_b, od_w_s, od_b_s, od_w_out, mlp_norm_g, mlp_w1, mlp_w2, final_norm_g, loss_target, m_ev_norm_g, m_ev_w_in, m_ev_conv_a_w, m_ev_conv_a_b, m_ev_ln_a_g, m_ev_ln_a_b, m_ev_conv_b_w, m_ev_w_out, m_od_norm_g, m_od_w_in, m_od_b_in, m_od_ln_v_g, m_od_ln_v_b, m_od_w_s, m_od_b_s, m_od_w_out, m_mlp_norm_g, m_mlp_w1, m_mlp_w2, m_final_norm_g, v_ev_norm_g, v_ev_w_in, v_ev_conv_a_w, v_ev_conv_a_b, v_ev_ln_a_g, v_ev_ln_a_b, v_ev_conv_b_w, v_ev_w_out, v_od_norm_g, v_od_w_in, v_od_b_in, v_od_ln_v_g, v_od_ln_v_b, v_od_w_s, v_od_b_s, v_od_w_out, v_mlp_norm_g, v_mlp_w1, v_mlp_w2, v_final_norm_g):
    tm = TOKEN_TILE
    batch, seq, d = x.shape
    tokens = batch * seq
    x2 = jnp.reshape(x, (tokens, d))
    tgt2 = jnp.reshape(loss_target, (tokens, d))
    chip = 2 * lax.axis_index("x") + lax.axis_index("y")

    small_shapes = [(A_CONV_WIDTH, LANES), (B_CONV_WIDTH, LANES), (256,), (512,), (256,), (256,)]
    small_shard = _pack([ev_conv_a_w[0], ev_conv_b_w[0], od_norm_g[0], od_b_in[0], od_ln_v_g[0], od_ln_v_b[0]])
    small_shard = jnp.pad(small_shard, ((0, (-small_shard.shape[0]) % (2 * SUBLANES)), (0, 0)))
    first = [_place_shard(ev_w_in, 0, BF16, "place_ev_w_in"), _place_shard(ev_w_out, 0, BF16, "place_ev_w_out"),
             _place_shard(small_shard[None], 0, F32, "place_small")]
    staged = {
        "w1_0": _place_shard(mlp_w1, 0, BF16, "place_w1_0"), "w2_0": _place_shard(mlp_w2, 0, BF16, "place_w2_0"),
        "od_in": _place_shard(od_w_in, 0, BF16, "place_od_w_in"), "od_out": _place_shard(od_w_out, 0, BF16, "place_od_w_out"),
        "w1_1": _place_shard(mlp_w1, 1, BF16, "place_w1_1"), "w2_1": _place_shard(mlp_w2, 1, BF16, "place_w2_1"),
    }
    first = [_in_hbm(a) for a in first]
    staged = {nm: _in_hbm(a) for nm, a in staged.items()}
    (g_ev_in, g_ev_out, g_small), = _exchange([_Gather(first)], "gather_first")
    small_all = jnp.reshape(g_small, (N_CHIPS, -1, LANES))
    per_chip = [_unpack(small_all[q], small_shapes) for q in range(N_CHIPS)]
    conv_a_w = jnp.concatenate([pc[0] for pc in per_chip], axis=1)
    conv_b_w = jnp.concatenate([pc[1] for pc in per_chip], axis=1)
    od_norm = jnp.concatenate([pc[2] for pc in per_chip])[None, :]
    od_bias = jnp.concatenate([pc[3] for pc in per_chip])[None, :]
    od_lng = jnp.concatenate([pc[4] for pc in per_chip])[None, :]
    od_lnb = jnp.concatenate([pc[5] for pc in per_chip])[None, :]

    dx, red, own, landed = _forward_backward(
        x2, tgt2, {"ev_in": g_ev_in, "ev_out": g_ev_out}, staged, conv_a_w, conv_b_w, od_norm, od_bias, od_lng, od_lnb,
        ev_norm_g, ev_conv_a_b, ev_ln_a_g, ev_ln_a_b, od_w_s, od_b_s, mlp_norm_g, final_norm_g, tm=tm, seq=seq)

    routs = _exchange([red["ev_in"].chip_swap(), red["ev_out"].pair_share()], "reduce_tail_1")
    red["ev_in"].took_chips(routs[0])
    red["ev_out"].took_share(routs[1])
    routs = _exchange([red["ev_in"].pair_share()], "reduce_tail_2")
    red["ev_in"].took_share(routs[0])

    given = {"ev_norm_g": (ev_norm_g, m_ev_norm_g, v_ev_norm_g), "ev_conv_a_b": (ev_conv_a_b, m_ev_conv_a_b, v_ev_conv_a_b),
             "ev_ln_a_g": (ev_ln_a_g, m_ev_ln_a_g, v_ev_ln_a_g), "ev_ln_a_b": (ev_ln_a_b, m_ev_ln_a_b, v_ev_ln_a_b),
             "od_w_s": (od_w_s, m_od_w_s, v_od_w_s), "od_b_s": (od_b_s, m_od_b_s, v_od_b_s),
             "mlp_norm_g": (mlp_norm_g, m_mlp_norm_g, v_mlp_norm_g), "final_norm_g": (final_norm_g, m_final_norm_g, v_final_norm_g),
             "ev_conv_a_w": (ev_conv_a_w, m_ev_conv_a_w, v_ev_conv_a_w), "ev_conv_b_w": (ev_conv_b_w, m_ev_conv_b_w, v_ev_conv_b_w),
             "od_norm_g": (od_norm_g, m_od_norm_g, v_od_norm_g), "od_b_in": (od_b_in, m_od_b_in, v_od_b_in),
             "od_ln_v_g": (od_ln_v_g, m_od_ln_v_g, v_od_ln_v_g), "od_ln_v_b": (od_ln_v_b, m_od_ln_v_b, v_od_ln_v_b)}
    shaped = {nm: tuple(jnp.reshape(a, shape) for a in given[nm]) for nm, shape, _, _ in SMALL_WEIGHTS}
    loss11, small_upd = _small_update(own, landed, shaped)
    loss = loss11[0, 0]
    upd = {nm: [jnp.reshape(o, given[nm][0].shape) for o in outs] for nm, outs in small_upd.items()}

    def big_update(wt, m, v, names, call):
        grads = [red[nm].reduced() for nm in names]
        shp3 = (len(grads),) + grads[0].shape
        outs, _ = _adamw(jnp.reshape(wt, shp3), jnp.reshape(m, shp3), jnp.reshape(v, shp3), grads, call)
        return [jnp.reshape(o, wt.shape) for o in outs], None

    upd["mlp_w2"], _ = big_update(mlp_w2, m_mlp_w2, v_mlp_w2, ["w2_0", "w2_1"], "adamw_mlp_w2")
    upd["mlp_w1"], _ = big_update(mlp_w1, m_mlp_w1, v_mlp_w1, ["w1_0", "w1_1"], "adamw_mlp_w1")
    upd["ev_w_in"], _ = big_update(ev_w_in, m_ev_w_in, v_ev_w_in, ["ev_in"], "adamw_ev_w_in")
    upd["ev_w_out"], _ = big_update(ev_w_out, m_ev_w_out, v_ev_w_out, ["ev_out"], "adamw_ev_w_out")
    upd["od_w_in"], _ = big_update(od_w_in, m_od_w_in, v_od_w_in, ["od_in"], "adamw_od_w_in")
    upd["od_w_out"], _ = big_update(od_w_out, m_od_w_out, v_od_w_out, ["od_out"], "adamw_od_w_out")

    order = ["ev_norm_g", "ev_w_in", "ev_conv_a_w", "ev_conv_a_b", "ev_ln_a_g", "ev_ln_a_b", "ev_conv_b_w", "ev_w_out",
             "od_norm_g", "od_w_in", "od_b_in", "od_ln_v_g", "od_ln_v_b", "od_w_s", "od_b_s", "od_w_out", "mlp_norm_g",
             "mlp_w1", "mlp_w2", "final_norm_g"]
    grad_x = jnp.reshape(dx, x.shape)
    return (loss, grad_x, *[upd[nm][0] for nm in order], *[upd[nm][1] for nm in order],
            *[upd[nm][2] for nm in order], *[upd[nm][3] for nm in order])
```

```python
import functools

import jax
import jax.numpy as jnp
from jax import lax
from jax.experimental import pallas as pl
from jax.experimental.pallas import tpu as pltpu

F32 = jnp.float32
BF16 = jnp.bfloat16

D_MODEL = 1024
A_DIM = 512
B_DIM = 512
IN_EVEN = 2 * A_DIM + 3 * B_DIM
A_CONV_WIDTH = 31
B_CONV_WIDTH = 3
CHUNK = 128
C_GROUPS = 8
C_DIM = 1024
D_FF = 4096
RMS_EPS = 1e-6
LN_EPS = 1e-5
ADAM_LR = 0.001
ADAM_B1 = 0.9
ADAM_B2 = 0.999
ADAM_EPS = 1e-08
ADAM_WD = 0.01
ADAM_STEP = 10

N_CHIPS = 4
N_DEV = 8
TOKEN_TILE = 512
A_HALO = 32
B_HALO = 8
CONV_ROWS = 16
DW_TAPS = 4
PAIR = 2 * CHUNK
LANES = 128
SUBLANES = 8
MXU_ROWS = 256
MIB = 1024 * 1024
MESH = pl.DeviceIdType.MESH
ANY = pl.BlockSpec(memory_space=pl.ANY)


def _dot(a, b):
    return lax.dot_general(a, b, (((1,), (0,)), ((), ())), preferred_element_type=F32)


def _dot_nt(a, b):
    return lax.dot_general(a, b, (((1,), (1,)), ((), ())), preferred_element_type=F32)


def _dot_tn(a, b):
    return lax.dot_general(a, b, (((0,), (0,)), ((), ())), preferred_element_type=F32)


def _params(vmem_mib, n_axes=1):
    return pltpu.CompilerParams(dimension_semantics=("arbitrary",) * n_axes, vmem_limit_bytes=vmem_mib * MIB)


def _row_spec(tm, cols, rev_nt=None):
    if rev_nt is None:
        return pl.BlockSpec((tm, cols), lambda i: (i, 0))
    return pl.BlockSpec((tm, cols), lambda i: (rev_nt - 1 - i, 0))


def _full_spec(shape):
    nd = len(shape)
    return pl.BlockSpec(shape, lambda i: (0,) * nd)


def _block_rows(rows, cap=512):
    best = SUBLANES
    for br in range(SUBLANES, min(rows, cap) + 1, SUBLANES):
        if rows % br == 0:
            best = br
    return best


N_LOADS = 2


def _load_weights(pairs, sems):
    @pl.when(pl.program_id(0) == 0)
    def _():
        copies = [pltpu.make_async_copy(src, dst, sems.at[k]) for k, (src, dst) in enumerate(pairs)]
        for cp in copies:
            cp.start()
        for cp in copies:
            cp.wait()


def _rms_fwd(x, g):
    rstd = lax.rsqrt(jnp.mean(x * x, axis=-1, keepdims=True) + RMS_EPS)
    return x * rstd * g, rstd


def _rms_bwd(dn, x, rstd, g):
    a = dn * g
    xh = x * rstd
    dx = rstd * (a - xh * jnp.mean(a * xh, axis=-1, keepdims=True))
    dg = jnp.sum(dn * xh, axis=0, keepdims=True)
    return dx, dg


def _ln_stats(v):
    mu = jnp.mean(v, axis=-1, keepdims=True)
    xc = v - mu
    rs = lax.rsqrt(jnp.mean(xc * xc, axis=-1, keepdims=True) + LN_EPS)
    return xc * rs, rs


def _ln_bwd(dy, xhat, rs, g):
    dxh = dy * g
    dv = rs * (dxh - jnp.mean(dxh, axis=-1, keepdims=True) - xhat * jnp.mean(dxh * xhat, axis=-1, keepdims=True))
    return dv, jnp.sum(dy * xhat, axis=0, keepdims=True), jnp.sum(dy, axis=0, keepdims=True)


def _gelu_cdf(s):
    return 0.5 * (1.0 + lax.erf(s * 0.7071067811865476))


def _mesh_pos():
    return lax.axis_index("x"), lax.axis_index("y"), lax.axis_index("c")


def _other_chips(x, y):
    return [(1 - x, y), (x, 1 - y), (1 - x, 1 - y)]


def _remote(src, dst, send_sem, recv_sem, to):
    return pltpu.make_async_remote_copy(src_ref=src, dst_ref=dst, send_sem=send_sem, recv_sem=recv_sem,
                                        device_id=to, device_id_type=MESH)


def _like(arrays):
    return [jax.ShapeDtypeStruct(a.shape, a.dtype) for a in arrays]


class _Gather:
    def __init__(self, bufs):
        self.ins = list(bufs)
        self.out_shapes = _like(bufs)
        self.aliases = {t: t for t in range(len(bufs))}
        self.n_sems = 6 * len(bufs)

    def _ici(self, ins, outs, send, recv, t, k, chip, mine, c):
        return _remote(ins[t].at[mine, c], outs[t].at[mine, c], send.at[6 * t + k], recv.at[6 * t + k], (*chip, c))

    def start(self, ins, outs, send, recv):
        x, y, c = _mesh_pos()
        for t in range(len(ins)):
            for k, chip in enumerate(_other_chips(x, y)):
                self._ici(ins, outs, send, recv, t, k, chip, 2 * x + y, c).start()

    def _pass_on(self, outs, send, recv, t, k, chip, c, to):
        blk = outs[t].at[2 * chip[0] + chip[1], c]
        return _remote(blk, blk, send.at[6 * t + 3 + k], recv.at[6 * t + 3 + k], to)

    def near_end(self, ins, outs, send, recv):
        x, y, c = _mesh_pos()
        for t in range(len(ins)):
            for k, chip in enumerate(_other_chips(x, y)):
                blk = outs[t].at[2 * chip[0] + chip[1], c]
                _remote(blk, blk, send.at[6 * t + k], recv.at[6 * t + k], (x, y, c)).wait_recv()
                self._pass_on(outs, send, recv, t, k, chip, c, (x, y, 1 - c)).start()

    def finish(self, ins, outs, send, recv):
        x, y, c = _mesh_pos()
        chips = _other_chips(x, y)
        for t in range(len(ins)):
            for k, chip in enumerate(chips):
                self._pass_on(outs, send, recv, t, k, chip, 1 - c, (x, y, c)).wait_recv()
        for t in range(len(ins)):
            for k, chip in enumerate(chips):
                self._ici(ins, outs, send, recv, t, k, chip, 2 * x + y, c).wait_send()
                self._pass_on(outs, send, recv, t, k, chip, c, (x, y, 1 - c)).wait_send()


class _PairSwap:
    def __init__(self, grads):
        self.ins = list(grads)
        self.out_shapes = [jax.ShapeDtypeStruct((g.shape[0],) + g.shape[2:], g.dtype) for g in grads]
        self.aliases = {}
        self.n_sems = len(grads)

    def _copies(self, ins, outs, send, recv):
        x, y, c = _mesh_pos()
        return [_remote(ins[t].at[:, 1 - c], outs[t], send.at[t], recv.at[t], (x, y, 1 - c)) for t in range(len(ins))]

    def start(self, ins, outs, send, recv):
        for cp in self._copies(ins, outs, send, recv):
            cp.start()

    def finish(self, ins, outs, send, recv):
        for cp in self._copies(ins, outs, send, recv):
            cp.wait()


class _ChipSwap:
    def __init__(self, parts):
        self.ins = list(parts)
        self.out_shapes = [jax.ShapeDtypeStruct((3,) + p.shape[1:], p.dtype) for p in parts]
        self.aliases = {}
        self.n_sems = 3 * len(parts)

    def _copies(self, ins, outs, send, recv):
        x, y, c = _mesh_pos()
        return [_remote(ins[t].at[2 * chip[0] + chip[1]], outs[t].at[k], send.at[3 * t + k], recv.at[3 * t + k], (*chip, c))
                for t in range(len(ins)) for k, chip in enumerate(_other_chips(x, y))]

    def start(self, ins, outs, send, recv):
        for cp in self._copies(ins, outs, send, recv):
            cp.start()

    def finish(self, ins, outs, send, recv):
        for cp in self._copies(ins, outs, send, recv):
            cp.wait()


class _PairShare:
    def __init__(self, fulls):
        self.ins = list(fulls)
        self.out_shapes = _like(fulls)
        self.aliases = {t: t for t in range(len(fulls))}
        self.n_sems = len(fulls)

    def _copies(self, ins, outs, send, recv):
        x, y, c = _mesh_pos()
        return [_remote(ins[t].at[c], outs[t].at[c], send.at[t], recv.at[t], (x, y, 1 - c)) for t in range(len(ins))]

    def start(self, ins, outs, send, recv):
        for cp in self._copies(ins, outs, send, recv):
            cp.start()

    def finish(self, ins, outs, send, recv):
        for cp in self._copies(ins, outs, send, recv):
            cp.wait()


class _ShareAll:
    def __init__(self, arrays):
        self.ins = list(arrays)
        self.out_shapes = [jax.ShapeDtypeStruct((N_DEV,) + a.shape, a.dtype) for a in arrays]
        self.aliases = {}
        self.n_sems = (N_DEV - 1) * len(arrays)

    def _peers(self):
        x, y, c = _mesh_pos()
        flips = [((r >> 2) & 1, (r >> 1) & 1, r & 1) for r in range(1, N_DEV)]
        return (x, y, c), [(x ^ fx, y ^ fy, c ^ fc) for fx, fy, fc in flips]

    def _sends(self, ins, outs, send, recv):
        (x, y, c), peers = self._peers()
        mine = 4 * x + 2 * y + c
        return [_remote(ins[a], outs[a].at[mine], send.at[7 * a + r], recv.at[7 * a + r], peer)
                for a in range(len(ins)) for r, peer in enumerate(peers)]

    def start(self, ins, outs, send, recv):
        for cp in self._sends(ins, outs, send, recv):
            cp.start()

    def finish(self, ins, outs, send, recv):
        (x, y, c), peers = self._peers()
        for a in range(len(ins)):
            for r, (px, py, pc) in enumerate(peers):
                blk = outs[a].at[4 * px + 2 * py + pc]
                _remote(blk, blk, send.at[7 * a + r], recv.at[7 * a + r], (x, y, c)).wait_recv()
        for cp in self._sends(ins, outs, send, recv):
            cp.wait_send()


def _pallas(body, operands, *, name, grid, in_specs, out_specs, out_shape, scratch_shapes=(), vmem_mib=32, riders=(),
            prefetch=None):
    in_specs, out_specs, out_shape, scratch_shapes = list(in_specs), list(out_specs), list(out_shape), list(scratch_shapes)
    if not riders and prefetch is None:
        outs = pl.pallas_call(body, name=name, grid=grid, in_specs=in_specs, out_specs=out_specs, out_shape=out_shape,
                              scratch_shapes=scratch_shapes, compiler_params=_params(vmem_mib, len(grid)))(*operands)
        return list(outs), []
    n_in, n_out, n_scr = len(in_specs), len(out_specs), len(scratch_shapes)
    r_in = [len(r.ins) for r in riders]
    r_out = [len(r.out_shapes) for r in riders]
    steps = 1
    for g in grid:
        steps *= g

    n_pre = 0 if prefetch is None else 1

    def wrapped(*refs):
        refs = list(refs)
        pre, refs = refs[:n_pre], refs[n_pre:]
        ins, refs = refs[:n_in], refs[n_in:]
        rins = []
        for k in r_in:
            rins.append(refs[:k])
            refs = refs[k:]
        outs, refs = refs[:n_out], refs[n_out:]
        routs = []
        for k in r_out:
            routs.append(refs[:k])
            refs = refs[k:]
        scr, sems = refs[:n_scr], refs[n_scr:]
        step = 0
        for ax, g in enumerate(grid):
            step = step * g + pl.program_id(ax)

        def each(what):
            for j, r in enumerate(riders):
                if hasattr(r, what):
                    getattr(r, what)(rins[j], routs[j], sems[2 * j], sems[2 * j + 1])

        if grid:
            pl.when(step == 0)(lambda: each("start"))
        else:
            each("start")
        body(*pre, *ins, *outs, *scr)
        if grid:
            @pl.when(step == steps - 1)
            def _():
                each("near_end")
                each("finish")
        else:
            each("near_end")
            each("finish")

    aliases, off_in, off_out = {}, n_pre + n_in, n_out
    for r, ki, ko in zip(riders, r_in, r_out):
        for i, o in r.aliases.items():
            aliases[off_in + i] = off_out + o
        off_in, off_out = off_in + ki, off_out + ko
    sems = []
    for r in riders:
        sems += [pltpu.SemaphoreType.DMA((r.n_sems,)), pltpu.SemaphoreType.DMA((r.n_sems,))]
    layout = dict(grid=grid, in_specs=in_specs + [ANY] * sum(r_in), out_specs=out_specs + [ANY] * sum(r_out),
                  scratch_shapes=scratch_shapes + sems)
    if prefetch is not None:
        layout = dict(grid_spec=pltpu.PrefetchScalarGridSpec(num_scalar_prefetch=1, **layout))
    res = pl.pallas_call(
        wrapped, name=name, **layout,
        out_shape=out_shape + [s for r in riders for s in r.out_shapes], input_output_aliases=aliases,
        compiler_params=pltpu.CompilerParams(dimension_semantics=("arbitrary",) * len(grid),
                                             vmem_limit_bytes=vmem_mib * MIB, has_side_effects=True),
    )(*([] if prefetch is None else [prefetch]), *operands, *[a for r in riders for a in r.ins])
    res = list(res)
    outs, res = res[:n_out], res[n_out:]
    routs = []
    for k in r_out:
        routs.append(res[:k])
        res = res[k:]
    return outs, routs


def _exchange(riders, name):
    return _pallas(lambda: None, [], name=name, grid=(), in_specs=[], out_specs=[], out_shape=[], riders=riders)[1]


def _in_hbm(a):
    return pltpu.with_memory_space_constraint(a, pltpu.HBM)


def _place_shard(w, layer, dtype, name):
    _, rows, cols = w.shape
    half = rows // 2
    br = _block_rows(half)
    nb = half // br
    mine = 2 * lax.axis_index("x") + lax.axis_index("y")

    def body(q_ref, w_ref, o_ref):
        o_ref[...] = w_ref[...].astype(dtype)

    return pl.pallas_call(
        body, name=name,
        grid_spec=pltpu.PrefetchScalarGridSpec(
            num_scalar_prefetch=1, grid=(2, nb),
            in_specs=[pl.BlockSpec((None, br, cols), lambda h, i, q: (layer, h * nb + i, 0))],
            out_specs=pl.BlockSpec((None, None, br, cols), lambda h, i, q: (q[0], h, i, 0))),
        out_shape=pltpu.HBM((N_CHIPS, 2, half, cols), dtype),
        compiler_params=_params(16, 2),
    )(jnp.reshape(mine, (1,)).astype(jnp.int32), w)


def _add_pair(g, recv, name):
    _, _, r, cdim = g.shape
    br = _block_rows(r, 256)
    c = lax.axis_index("c")

    def body(c_ref, g_ref, r_ref, o_ref):
        o_ref[...] = (g_ref[...] + r_ref[...]).astype(BF16)

    return pl.pallas_call(
        body, name=name,
        grid_spec=pltpu.PrefetchScalarGridSpec(
            num_scalar_prefetch=1, grid=(N_CHIPS, r // br),
            in_specs=[pl.BlockSpec((None, None, br, cdim), lambda q, i, c_ref: (q, c_ref[0], i, 0)),
                      pl.BlockSpec((None, br, cdim), lambda q, i, c_ref: (q, i, 0))],
            out_specs=pl.BlockSpec((None, br, cdim), lambda q, i, c_ref: (q, i, 0))),
        out_shape=pltpu.HBM((N_CHIPS, r, cdim), BF16),
        compiler_params=_params(16, 2),
    )(jnp.reshape(c, (1,)).astype(jnp.int32), _in_hbm(g), _in_hbm(recv))


def _add_chips(own, recv, name):
    _, r, cdim = own.shape
    br = _block_rows(r, 256)
    x, y, c = _mesh_pos()

    def body(pos_ref, own_ref, r_ref, o_ref):
        acc = own_ref[...].astype(F32)
        for k in range(3):
            acc = acc + r_ref[k].astype(F32)
        o_ref[...] = acc

    return pl.pallas_call(
        body, name=name,
        grid_spec=pltpu.PrefetchScalarGridSpec(
            num_scalar_prefetch=1, grid=(r // br,),
            in_specs=[pl.BlockSpec((None, br, cdim), lambda i, pos: (pos[0], i, 0)),
                      pl.BlockSpec((3, br, cdim), lambda i, pos: (0, i, 0))],
            out_specs=pl.BlockSpec((None, br, cdim), lambda i, pos: (pos[1], i, 0))),
        out_shape=pltpu.HBM((2, r, cdim), F32),
        compiler_params=_params(16, 1),
    )(jnp.stack([2 * x + y, c]).astype(jnp.int32), _in_hbm(own), _in_hbm(recv))


def _adam_math(w, m, v, g):
    c1 = 1.0 / (1.0 - ADAM_B1 ** ADAM_STEP)
    c2 = 1.0 / (1.0 - ADAM_B2 ** ADAM_STEP)
    m_new = ADAM_B1 * m + (1.0 - ADAM_B1) * g
    v_new = ADAM_B2 * v + (1.0 - ADAM_B2) * (g * g)
    return -ADAM_LR * ((m_new * c1) / (jnp.sqrt(v_new * c2) + ADAM_EPS) + ADAM_WD * w), m_new, v_new


SMALL_WEIGHTS = [
    ("ev_norm_g", (1, D_MODEL), ["ev_norm_g"], None), ("ev_conv_a_b", (1, A_DIM), ["ev_conv_a_b"], None),
    ("ev_ln_a_g", (1, A_DIM), ["ev_ln_a_g"], None), ("ev_ln_a_b", (1, A_DIM), ["ev_ln_a_b"], None),
    ("od_w_s", (C_GROUPS, CHUNK, CHUNK), ["od_w_s_lo", "od_w_s_hi"], None), ("od_b_s", (C_GROUPS, CHUNK), ["od_b_s"], None),
    ("mlp_norm_g", (2, D_MODEL), ["mlp_norm_g0", "mlp_norm_g1"], None), ("final_norm_g", (1, D_MODEL), ["final_norm_g"], None),
    ("ev_conv_a_w", (A_CONV_WIDTH, A_DIM // N_CHIPS), ["ev_conv_a_w"], A_DIM // N_CHIPS),
    ("ev_conv_b_w", (B_CONV_WIDTH, B_DIM // N_CHIPS), ["ev_conv_b_w"], B_DIM // N_CHIPS),
    ("od_norm_g", (1, D_MODEL // N_CHIPS), ["od_norm_g"], D_MODEL // N_CHIPS),
    ("od_b_in", (1, 2 * C_DIM // N_CHIPS), ["od_b_in"], 2 * C_DIM // N_CHIPS),
    ("od_ln_v_g", (1, C_DIM // N_CHIPS), ["od_ln_v_g"], C_DIM // N_CHIPS),
    ("od_ln_v_b", (1, C_DIM // N_CHIPS), ["od_ln_v_b"], C_DIM // N_CHIPS),
]


def _small_update(own, landed, weights):
    names = list(own.keys())
    n_g, n_w = len(names), len(SMALL_WEIGHTS)

    def body(*refs):
        refs = list(refs)
        own_refs = dict(zip(names, refs[:n_g]))
        land_refs = dict(zip(names, refs[n_g:2 * n_g]))
        wmv = [refs[2 * n_g + 3 * i:2 * n_g + 3 * i + 3] for i in range(n_w)]
        o0 = 2 * n_g + 3 * n_w
        loss_ref = refs[o0]
        outs = [refs[o0 + 1 + 4 * i:o0 + 5 + 4 * i] for i in range(n_w)]
        acc = dict(zip(names, refs[o0 + 1 + 4 * n_w:]))
        x, y, c = _mesh_pos()
        mine, chip = 4 * x + 2 * y + c, 2 * x + y

        for nm in names:
            for d in range(N_DEV):
                def add(term, nm=nm, d=d):
                    acc[nm][...] = term if d == 0 else acc[nm][...] + term
                pl.when(mine == d)(lambda nm=nm, add=add: add(own_refs[nm][...]))
                pl.when(mine != d)(lambda nm=nm, d=d, add=add: add(land_refs[nm][d]))
        loss_ref[...] = acc["loss"][...]

        def update(i, rows, g):
            w_ref, m_ref, v_ref = wmv[i]
            delta, m_new, v_new = _adam_math(w_ref[rows], m_ref[rows], v_ref[rows], g)
            for ref, val in zip(outs[i], (g, delta, m_new, v_new)):
                ref[rows] = val

        for i, (_, shape, grads, per_chip) in enumerate(SMALL_WEIGHTS):
            for row, gname in enumerate(grads):
                per_grad = shape[0] // len(grads)
                rows = slice(row * per_grad, (row + 1) * per_grad)
                if per_chip is None:
                    update(i, rows, acc[gname][...])
                else:
                    for q in range(N_CHIPS):
                        pl.when(chip == q)(lambda i=i, rows=rows, gname=gname, q=q, per_chip=per_chip:
                                           update(i, rows, acc[gname][:, q * per_chip:(q + 1) * per_chip]))

    operands = [own[nm] for nm in names] + [landed[nm] for nm in names]
    for nm, _, _, _ in SMALL_WEIGHTS:
        operands += list(weights[nm])
    out_shape = [jax.ShapeDtypeStruct((1, 1), F32)]
    for _, shape, _, _ in SMALL_WEIGHTS:
        out_shape += [jax.ShapeDtypeStruct(shape, F32)] * 4
    res = pl.pallas_call(
        body, name="small_update", grid=(1,),
        in_specs=[_full_spec(a.shape) for a in operands], out_specs=[_full_spec(s.shape) for s in out_shape],
        out_shape=out_shape, scratch_shapes=[pltpu.VMEM(own[nm].shape, F32) for nm in names],
        compiler_params=_params(32, 1),
    )(*[_in_hbm(a) for a in operands])
    return res[0], {nm: res[1 + 4 * i:5 + 4 * i] for i, (nm, _, _, _) in enumerate(SMALL_WEIGHTS)}


def _adamw(w, m, v, grads, name, riders=()):
    layers, r, cdim = w.shape
    br = _block_rows(r, 256 if cdim > LANES else 1024)

    def body(*refs):
        w_ref, m_ref, v_ref = refs[:3]
        g_refs = refs[3:3 + layers]
        go_ref, d_ref, mo_ref, vo_ref = refs[3 + layers:]
        layer = pl.program_id(0)
        for l in range(layers):
            @pl.when(layer == l)
            def _(l=l):
                g = g_refs[l][...]
                go_ref[...] = g
                d_ref[...], mo_ref[...], vo_ref[...] = _adam_math(w_ref[...], m_ref[...], v_ref[...], g)

    spec3 = pl.BlockSpec((None, br, cdim), lambda l, i: (l, i, 0))
    spec2 = pl.BlockSpec((br, cdim), lambda l, i: (i, 0))
    out = jax.ShapeDtypeStruct((layers, r, cdim), F32)
    return _pallas(body, [w, m, v, *[_in_hbm(g) for g in grads]], name=name, grid=(layers, r // br),
                   in_specs=[spec3, spec3, spec3] + [spec2] * layers, out_specs=[spec3] * 4, out_shape=[out] * 4,
                   vmem_mib=32, riders=riders)


def _fill_shifted(buf, rows):
    for b in range(1, SUBLANES):
        buf[b, 0:rows - SUBLANES, :] = buf[0, b:b + rows - SUBLANES, :]


def _window(buf, start, size):
    return buf[start % SUBLANES, start - start % SUBLANES:start - start % SUBLANES + size, :]


def _conv31(src, w_ref, r0, base, init):
    acc = init
    for k in range(A_CONV_WIDTH):
        acc = acc + w_ref[k:k + 1, :] * _window(src, base + k + r0, CONV_ROWS)
    return acc


def _fwd_even(x, norm_g, w_in, conv_a_w, conv_a_b, ln_g, ln_b, conv_b_w, w_out, *, tm, seq, riders=()):
    tokens = x.shape[0]
    nt, tps = tokens // tm, seq // tm

    def body(x_ref, g_ref, win_hbm, caw_ref, cab_ref, lng_ref, lnb_ref, cbw_ref, wout_hbm,
             h_ref, n_ref, z_ref, a2_ref, cv_ref, mix_ref, win_v, wout_v, pa, pb, sem):
        i = pl.program_id(0)

        _load_weights([(win_hbm, win_v), (wout_hbm, wout_v)], sem)

        xv = x_ref[...]
        nf, _ = _rms_fwd(xv, g_ref[...])
        n = nf.astype(BF16)
        n_ref[...] = n
        z = jnp.concatenate([_dot(n, win_v[j]) for j in range(N_CHIPS)], axis=1)
        z_ref[...] = z.astype(BF16)
        a_val, a_gate = z[:, 0:A_DIM], z[:, A_DIM:2 * A_DIM]
        b_gate, c_gate, b_val = z[:, 1024:1536], z[:, 1536:2048], z[:, 2048:2560]

        first = (i % tps) == 0

        @pl.when(first)
        def _():
            pa[0, 0:A_HALO, :] = jnp.zeros((A_HALO, A_DIM), F32)
            pb[0:B_HALO, :] = jnp.zeros((B_HALO, B_DIM), F32)

        @pl.when(jnp.logical_not(first))
        def _():
            pa[0, 0:A_HALO, :] = pa[0, tm:tm + A_HALO, :]
            pb[0:B_HALO, :] = pb[tm:tm + B_HALO, :]

        pa[0, A_HALO:A_HALO + tm, :] = a_val * jax.nn.sigmoid(a_gate)
        pb[B_HALO:B_HALO + tm, :] = c_gate * b_val
        _fill_shifted(pa, A_HALO + tm)
        bias = jnp.broadcast_to(cab_ref[...], (CONV_ROWS, A_DIM))
        for r0 in range(0, tm, CONV_ROWS):
            a2_ref[r0:r0 + CONV_ROWS, :] = _conv31(pa, caw_ref, r0, A_HALO - (A_CONV_WIDTH - 1), bias)
        xhat, _ = _ln_stats(a2_ref[...])
        a3 = xhat * lng_ref[...] + lnb_ref[...]
        a4 = a3 * jax.nn.sigmoid(a3)
        cv = cbw_ref[0:1, :] * pb[B_HALO - 2:B_HALO - 2 + tm, :]
        cv = cv + cbw_ref[1:2, :] * pb[B_HALO - 1:B_HALO - 1 + tm, :]
        cv = cv + cbw_ref[2:3, :] * pb[B_HALO:B_HALO + tm, :]
        cv_ref[...] = cv.astype(BF16)
        mix = jnp.concatenate([a4, b_gate * cv], axis=1).astype(BF16)
        mix_ref[...] = mix
        h_ref[...] = xv + _dot(mix, wout_v[...])

    shp = lambda cols, dt: jax.ShapeDtypeStruct((tokens, cols), dt)
    return _pallas(
        body, [x, norm_g, w_in, conv_a_w, conv_a_b, ln_g, ln_b, conv_b_w, w_out], name="fwd_even", grid=(nt,),
        in_specs=[_row_spec(tm, D_MODEL), _full_spec((1, D_MODEL)), ANY, _full_spec((A_CONV_WIDTH, A_DIM)),
                  _full_spec((1, A_DIM)), _full_spec((1, A_DIM)), _full_spec((1, A_DIM)),
                  _full_spec((B_CONV_WIDTH, B_DIM)), ANY],
        out_specs=[_row_spec(tm, D_MODEL), _row_spec(tm, D_MODEL), _row_spec(tm, IN_EVEN), _row_spec(tm, A_DIM),
                   _row_spec(tm, B_DIM), _row_spec(tm, D_MODEL)],
        out_shape=[shp(D_MODEL, F32), shp(D_MODEL, BF16), shp(IN_EVEN, BF16), shp(A_DIM, F32), shp(B_DIM, BF16),
                   shp(D_MODEL, BF16)],
        scratch_shapes=[pltpu.VMEM((N_CHIPS, D_MODEL, IN_EVEN // N_CHIPS), BF16), pltpu.VMEM((D_MODEL, D_MODEL), BF16),
                        pltpu.VMEM((SUBLANES, A_HALO + tm, A_DIM), F32), pltpu.VMEM((B_HALO + tm, B_DIM), F32),
                        pltpu.SemaphoreType.DMA((N_LOADS,))],
        vmem_mib=56, riders=riders)


def _loss_tail(xv, g, target, loss_ref, dh_ref, dhb_ref, dg_ref):
    @pl.when(pl.program_id(0) == 0)
    def _():
        loss_ref[...] = jnp.zeros((1, 1), F32)
        dg_ref[...] = jnp.zeros((1, D_MODEL), F32)

    out, rstd = _rms_fwd(xv, g)
    err = out - target
    per_token = jnp.sum(err * err, axis=1, keepdims=True) * (1.0 / D_MODEL)
    loss_ref[...] += 0.5 * jnp.sum(per_token, axis=0, keepdims=True)
    dx, dg = _rms_bwd(err * (1.0 / D_MODEL), xv, rstd, g)
    dh_ref[...] = dx
    dhb_ref[...] = dx.astype(BF16)
    dg_ref[...] += dg


def _fwd_mlp(h, norm_g, w1, w2, layer, *, tm, riders=(), head=None):
    tokens = h.shape[0]
    nt = tokens // tm
    fs = D_FF // N_CHIPS
    n_in = 4 if head is None else 6

    def body(*refs):
        h_ref, g_ref, w1_hbm, w2_hbm = refs[:4]
        w1_v, w2_v, sem = refs[-3:]
        outs = refs[n_in:-3]
        n_ref, p_ref, q_ref = outs[1:4] if head is None else outs[0:3]
        _load_weights([(w1_hbm, w1_v), (w2_hbm, w2_v)], sem)

        xv = h_ref[...]
        nf, _ = _rms_fwd(xv, g_ref[...])
        n = nf.astype(BF16)
        n_ref[...] = n
        acc = xv
        for j in range(N_CHIPS):
            p = _dot(n, w1_v[j])
            p_ref[:, j * fs:(j + 1) * fs] = p.astype(BF16)
            r = jnp.maximum(p, 0.0)
            q = (r * r).astype(BF16)
            q_ref[:, j * fs:(j + 1) * fs] = q
            acc = acc + _dot(q, w2_v[j])
        if head is None:
            outs[0][...] = acc
        else:
            _loss_tail(acc, refs[4][...], refs[5][...], *outs[3:7])

    shp = lambda cols, dt: jax.ShapeDtypeStruct((tokens, cols), dt)
    saved_specs = [_row_spec(tm, D_MODEL), _row_spec(tm, D_FF), _row_spec(tm, D_FF)]
    saved_shapes = [shp(D_MODEL, BF16), shp(D_FF, BF16), shp(D_FF, BF16)]
    if head is None:
        operands, in_specs = [h, norm_g, w1, w2], [_row_spec(tm, D_MODEL), _full_spec((1, D_MODEL)), ANY, ANY]
        out_specs, out_shape = [_row_spec(tm, D_MODEL)] + saved_specs, [shp(D_MODEL, F32)] + saved_shapes
    else:
        operands = [h, norm_g, w1, w2, *head]
        in_specs = [_row_spec(tm, D_MODEL), _full_spec((1, D_MODEL)), ANY, ANY, _full_spec((1, D_MODEL)), _row_spec(tm, D_MODEL)]
        out_specs = saved_specs + [_full_spec((1, 1)), _row_spec(tm, D_MODEL), _row_spec(tm, D_MODEL), _full_spec((1, D_MODEL))]
        out_shape = saved_shapes + [jax.ShapeDtypeStruct((1, 1), F32), shp(D_MODEL, F32), shp(D_MODEL, BF16),
                                    jax.ShapeDtypeStruct((1, D_MODEL), F32)]
    return _pallas(
        body, operands, name=f"fwd_mlp{layer}", grid=(nt,), in_specs=in_specs, out_specs=out_specs, out_shape=out_shape,
        scratch_shapes=[pltpu.VMEM((N_CHIPS, D_MODEL, fs), BF16), pltpu.VMEM((N_CHIPS, fs, D_MODEL), BF16),
                        pltpu.SemaphoreType.DMA((N_LOADS,))],
        vmem_mib=56, riders=riders)


def _tril_mask():
    row = lax.broadcasted_iota(jnp.int32, (CHUNK, CHUNK), 0)
    col = lax.broadcasted_iota(jnp.int32, (CHUNK, CHUNK), 1)
    return row >= col


def _triu_mask():
    row = lax.broadcasted_iota(jnp.int32, (CHUNK, CHUNK), 0)
    col = lax.broadcasted_iota(jnp.int32, (CHUNK, CHUNK), 1)
    return row <= col


def _fwd_odd(h, norm_g, w_in, b_in, ln_g, ln_b, w_s, b_s_rows, w_out, *, tm, riders=()):
    tokens = h.shape[0]
    nt = tokens // tm
    cs = 2 * C_DIM // N_CHIPS

    def body(h_ref, g_ref, win_hbm, bin_ref, lng_ref, lnb_ref, ws_ref, bs_ref, wout_hbm,
             ho_ref, n_ref, s_ref, cdf_ref, sv_ref, y_ref, win_v, wout_v, bd, sem):
        _load_weights([(win_hbm, win_v), (wout_hbm, wout_v)], sem)

        @pl.when(pl.program_id(0) == 0)
        def _():
            mask = _tril_mask()
            bd[...] = jnp.zeros(bd.shape, BF16)
            for g in range(C_GROUPS):
                w = jnp.where(mask, ws_ref[g], 0.0).astype(BF16)
                bd[g, 0:CHUNK, 0:CHUNK] = w
                bd[g, CHUNK:PAIR, CHUNK:PAIR] = w

        xv = h_ref[...]
        nf, _ = _rms_fwd(xv, g_ref[...])
        n = nf.astype(BF16)
        n_ref[...] = n
        s = jnp.concatenate([_dot(n, win_v[j]) for j in range(N_CHIPS)], axis=1) + bin_ref[...]
        s_ref[...] = s.astype(BF16)
        cdf = _gelu_cdf(s)
        cdf_ref[...] = cdf.astype(BF16)
        zz = s * cdf
        u, v = zz[:, 0:C_DIM], zz[:, C_DIM:2 * C_DIM]
        xhat, _ = _ln_stats(v)
        vn = (xhat * lng_ref[...] + lnb_ref[...]).astype(BF16)
        for g in range(C_GROUPS):
            cols = slice(g * CHUNK, (g + 1) * CHUNK)
            bias = jnp.concatenate([bs_ref[g], bs_ref[g]], axis=0)
            for r0 in range(0, tm, PAIR):
                sv = _dot(bd[g], vn[r0:r0 + PAIR, cols]) + bias
                sv_ref[r0:r0 + PAIR, cols] = sv.astype(BF16)
                y_ref[r0:r0 + PAIR, cols] = (u[r0:r0 + PAIR, cols] * sv).astype(BF16)
        ho_ref[...] = xv + _dot(y_ref[...], wout_v[...])

    shp = lambda cols, dt: jax.ShapeDtypeStruct((tokens, cols), dt)
    return _pallas(
        body, [h, norm_g, w_in, b_in, ln_g, ln_b, w_s, b_s_rows, w_out], name="fwd_odd", grid=(nt,),
        in_specs=[_row_spec(tm, D_MODEL), _full_spec((1, D_MODEL)), ANY, _full_spec((1, 2 * C_DIM)),
                  _full_spec((1, C_DIM)), _full_spec((1, C_DIM)), _full_spec((C_GROUPS, CHUNK, CHUNK)),
                  _full_spec((C_GROUPS, CHUNK, CHUNK)), ANY],
        out_specs=[_row_spec(tm, D_MODEL), _row_spec(tm, D_MODEL), _row_spec(tm, 2 * C_DIM), _row_spec(tm, 2 * C_DIM),
                   _row_spec(tm, C_DIM), _row_spec(tm, C_DIM)],
        out_shape=[shp(D_MODEL, F32), shp(D_MODEL, BF16), shp(2 * C_DIM, BF16), shp(2 * C_DIM, BF16), shp(C_DIM, BF16),
                   shp(C_DIM, BF16)],
        scratch_shapes=[pltpu.VMEM((N_CHIPS, D_MODEL, cs), BF16), pltpu.VMEM((C_DIM, D_MODEL), BF16),
                        pltpu.VMEM((C_GROUPS, PAIR, PAIR), BF16), pltpu.SemaphoreType.DMA((N_LOADS,))],
        vmem_mib=56, riders=riders)


def _bwd_mlp(dh, h, norm_g, p, w1, w2, layer, *, tm, riders=()):
    tokens = h.shape[0]
    nt = tokens // tm
    fs = D_FF // N_CHIPS

    def body(dh_ref, h_ref, g_ref, p_ref, w1_hbm, w2_hbm, dx_ref, dxb_ref, dp_ref, dg_ref, w1_v, w2_v, sem):
        @pl.when(pl.program_id(0) == 0)
        def _():
            dg_ref[...] = jnp.zeros((1, D_MODEL), F32)

        _load_weights([(w1_hbm, w1_v), (w2_hbm, w2_v)], sem)

        dhv = dh_ref[...]
        dhb = dhv.astype(BF16)
        dn = jnp.zeros((tm, D_MODEL), F32)
        for j in range(N_CHIPS):
            dq = _dot_nt(dhb, w2_v[j])
            r = jnp.maximum(p_ref[:, j * fs:(j + 1) * fs].astype(F32), 0.0)
            dp = ((2.0 * r) * dq).astype(BF16)
            dp_ref[:, j * fs:(j + 1) * fs] = dp
            dn = dn + _dot_nt(dp, w1_v[j])
        xv = h_ref[...]
        g = g_ref[...]
        _, rstd = _rms_fwd(xv, g)
        dx, dg = _rms_bwd(dn, xv, rstd, g)
        dx_ref[...] = dhv + dx
        dxb_ref[...] = (dhv + dx).astype(BF16)
        dg_ref[...] += dg

    return _pallas(
        body, [dh, h, norm_g, p, w1, w2], name=f"bwd_mlp{layer}", grid=(nt,),
        in_specs=[_row_spec(tm, D_MODEL), _row_spec(tm, D_MODEL), _full_spec((1, D_MODEL)), _row_spec(tm, D_FF), ANY, ANY],
        out_specs=[_row_spec(tm, D_MODEL), _row_spec(tm, D_MODEL), _row_spec(tm, D_FF), _full_spec((1, D_MODEL))],
        out_shape=[jax.ShapeDtypeStruct((tokens, D_MODEL), F32), jax.ShapeDtypeStruct((tokens, D_MODEL), BF16),
                   jax.ShapeDtypeStruct((tokens, D_FF), BF16), jax.ShapeDtypeStruct((1, D_MODEL), F32)],
        scratch_shapes=[pltpu.VMEM((N_CHIPS, D_MODEL, fs), BF16), pltpu.VMEM((N_CHIPS, fs, D_MODEL), BF16),
                        pltpu.SemaphoreType.DMA((N_LOADS,))],
        vmem_mib=56, riders=riders)


def _bwd_odd(dh, h, norm_g, s, cdf, sv, w_in, ln_g, ln_b, w_s, w_out, *, tm, riders=()):
    tokens = h.shape[0]
    nt = tokens // tm
    cs = 2 * C_DIM // N_CHIPS

    def body(dh_ref, h_ref, g_ref, s_ref, cdf_ref, sv_ref, win_hbm, lng_ref, lnb_ref, ws_ref, wout_hbm,
             dx_ref, dxb_ref, ds_ref, dg_ref, dbin_ref, dlng_ref, dlnb_ref, dws_ref, dbs_ref,
             win_v, wout_v, bdt, dws_acc, dbs_acc, dvn, sem):
        i = pl.program_id(0)

        _load_weights([(win_hbm, win_v), (wout_hbm, wout_v)], sem)

        @pl.when(i == 0)
        def _():
            mask_t = _triu_mask()
            bdt[...] = jnp.zeros(bdt.shape, BF16)
            for g in range(C_GROUPS):
                wt = jnp.where(mask_t, ws_ref[g].T, 0.0).astype(BF16)
                bdt[g, 0:CHUNK, 0:CHUNK] = wt
                bdt[g, CHUNK:PAIR, CHUNK:PAIR] = wt
            dws_acc[...] = jnp.zeros(dws_acc.shape, F32)
            dbs_acc[...] = jnp.zeros(dbs_acc.shape, F32)
            dg_ref[...] = jnp.zeros(dg_ref.shape, F32)
            dbin_ref[...] = jnp.zeros(dbin_ref.shape, F32)
            dlng_ref[...] = jnp.zeros(dlng_ref.shape, F32)
            dlnb_ref[...] = jnp.zeros(dlnb_ref.shape, F32)

        dhv = dh_ref[...]
        dy = _dot_nt(dhv.astype(BF16), wout_v[...])
        sf = s_ref[...].astype(F32)
        cdf = cdf_ref[...].astype(F32)
        pdf = jnp.exp(-0.5 * sf * sf) * 0.3989422804014327
        zz = sf * cdf
        dgelu = cdf + sf * pdf
        u, v = zz[:, 0:C_DIM], zz[:, C_DIM:2 * C_DIM]
        xhat, rs = _ln_stats(v)
        lng = lng_ref[...]
        vn = (xhat * lng + lnb_ref[...]).astype(BF16)
        du = dy * sv_ref[...].astype(F32)
        dsv = dy * u
        dsvb = dsv.astype(BF16)
        for g in range(C_GROUPS):
            cols = slice(g * CHUNK, (g + 1) * CHUNK)
            for r0 in range(0, tm, PAIR):
                blk = dsvb[r0:r0 + PAIR, cols]
                dvn[r0:r0 + PAIR, cols] = _dot(bdt[g], blk)
                dws_acc[g] += _dot_nt(blk, vn[r0:r0 + PAIR, cols])
                dbs_acc[g] += dsv[r0:r0 + CHUNK, cols] + dsv[r0 + CHUNK:r0 + PAIR, cols]
        dv, dlng, dlnb = _ln_bwd(dvn[...], xhat, rs, lng)
        dlng_ref[...] += dlng
        dlnb_ref[...] += dlnb
        ds = jnp.concatenate([du, dv], axis=1) * dgelu
        dbin_ref[...] += jnp.sum(ds, axis=0, keepdims=True)
        dsb = ds.astype(BF16)
        ds_ref[...] = dsb
        dn = jnp.zeros((tm, D_MODEL), F32)
        for j in range(N_CHIPS):
            dn = dn + _dot_nt(dsb[:, j * cs:(j + 1) * cs], win_v[j])
        xv = h_ref[...]
        g = g_ref[...]
        _, rstd = _rms_fwd(xv, g)
        dx, dg = _rms_bwd(dn, xv, rstd, g)
        dx_ref[...] = dhv + dx
        dxb_ref[...] = (dhv + dx).astype(BF16)
        dg_ref[...] += dg

        @pl.when(i == nt - 1)
        def _():
            mask = _tril_mask()
            for g in range(C_GROUPS):
                full = dws_acc[g]
                dws_ref[g] = jnp.where(mask, full[0:CHUNK, 0:CHUNK] + full[CHUNK:PAIR, CHUNK:PAIR], 0.0)
                dbs_ref[g:g + 1, :] = jnp.sum(dbs_acc[g].T, axis=0, keepdims=True)

    row = lambda cols: jax.ShapeDtypeStruct((1, cols), F32)
    return _pallas(
        body, [dh, h, norm_g, s, cdf, sv, w_in, ln_g, ln_b, w_s, w_out], name="bwd_odd", grid=(nt,),
        in_specs=[_row_spec(tm, D_MODEL), _row_spec(tm, D_MODEL), _full_spec((1, D_MODEL)), _row_spec(tm, 2 * C_DIM),
                  _row_spec(tm, 2 * C_DIM), _row_spec(tm, C_DIM), ANY, _full_spec((1, C_DIM)), _full_spec((1, C_DIM)),
                  _full_spec((C_GROUPS, CHUNK, CHUNK)), ANY],
        out_specs=[_row_spec(tm, D_MODEL), _row_spec(tm, D_MODEL), _row_spec(tm, 2 * C_DIM), _full_spec((1, D_MODEL)),
                   _full_spec((1, 2 * C_DIM)),
                   _full_spec((1, C_DIM)), _full_spec((1, C_DIM)), _full_spec((C_GROUPS, CHUNK, CHUNK)),
                   _full_spec((C_GROUPS, CHUNK))],
        out_shape=[jax.ShapeDtypeStruct((tokens, D_MODEL), F32), jax.ShapeDtypeStruct((tokens, D_MODEL), BF16),
                   jax.ShapeDtypeStruct((tokens, 2 * C_DIM), BF16),
                   row(D_MODEL), row(2 * C_DIM), row(C_DIM), row(C_DIM),
                   jax.ShapeDtypeStruct((C_GROUPS, CHUNK, CHUNK), F32), jax.ShapeDtypeStruct((C_GROUPS, CHUNK), F32)],
        scratch_shapes=[pltpu.VMEM((N_CHIPS, D_MODEL, cs), BF16), pltpu.VMEM((C_DIM, D_MODEL), BF16),
                        pltpu.VMEM((C_GROUPS, PAIR, PAIR), BF16), pltpu.VMEM((C_GROUPS, PAIR, PAIR), F32),
                        pltpu.VMEM((C_GROUPS, CHUNK, CHUNK), F32), pltpu.VMEM((tm, C_DIM), F32),
                        pltpu.SemaphoreType.DMA((N_LOADS,))],
        vmem_mib=56, riders=riders)


def _bwd_even(dh, x, norm_g, z, a2, cv, w_in, conv_a_w, ln_g, ln_b, conv_b_w, w_out, *, tm, seq, riders=()):
    tokens = x.shape[0]
    nt, tps = tokens // tm, seq // tm
    ws = IN_EVEN // N_CHIPS

    def body(dh_ref, x_ref, g_ref, z_ref, a2_ref, cv_ref, win_hbm, caw_ref, lng_ref, lnb_ref, cbw_ref, wout_hbm,
             dx_ref, dz_ref, dg_ref, dcaw_ref, dcab_ref, dlng_ref, dlnb_ref, dcbw_ref,
             win_v, wout_v, ea, eb, a1s, da1s, dw_acc, sem):
        i = pl.program_id(0)

        _load_weights([(win_hbm, win_v), (wout_hbm, wout_v)], sem)

        @pl.when(i == 0)
        def _():
            dw_acc[...] = jnp.zeros(dw_acc.shape, F32)
            for ref in (dg_ref, dcab_ref, dlng_ref, dlnb_ref, dcbw_ref):
                ref[...] = jnp.zeros(ref.shape, F32)

        dhv = dh_ref[...]
        dmix = _dot_nt(dhv.astype(BF16), wout_v[...])
        da4, dbo = dmix[:, 0:A_DIM], dmix[:, A_DIM:A_DIM + B_DIM]
        zf = z_ref[...].astype(F32)
        a_val, a_gate = zf[:, 0:A_DIM], zf[:, A_DIM:2 * A_DIM]
        b_gate, c_gate, b_val = zf[:, 1024:1536], zf[:, 1536:2048], zf[:, 2048:2560]

        xhat, rs = _ln_stats(a2_ref[...])
        lng = lng_ref[...]
        a3 = xhat * lng + lnb_ref[...]
        sg = jax.nn.sigmoid(a3)
        da3 = da4 * (sg * (1.0 + a3 * (1.0 - sg)))
        da2, dlng, dlnb = _ln_bwd(da3, xhat, rs, lng)
        dlng_ref[...] += dlng
        dlnb_ref[...] += dlnb
        dcab_ref[...] += jnp.sum(da2, axis=0, keepdims=True)

        last = ((nt - 1 - i) % tps) == tps - 1
        dcv = dbo * b_gate

        @pl.when(last)
        def _():
            ea[0, tm:tm + A_HALO, :] = jnp.zeros((A_HALO, A_DIM), F32)
            eb[tm:tm + B_HALO, :] = jnp.zeros((B_HALO, B_DIM), F32)

        @pl.when(jnp.logical_not(last))
        def _():
            ea[0, tm:tm + A_HALO, :] = ea[0, 0:A_HALO, :]
            eb[tm:tm + B_HALO, :] = eb[0:B_HALO, :]

        ea[0, 0:tm, :] = da2
        eb[0:tm, :] = dcv
        _fill_shifted(ea, tm + A_HALO)
        sig = jax.nn.sigmoid(a_gate)
        a1s[...] = a_val * sig
        for r0 in range(0, tm, CONV_ROWS):
            acc = jnp.zeros((CONV_ROWS, A_DIM), F32)
            for j in range(A_CONV_WIDTH):
                acc = acc + caw_ref[A_CONV_WIDTH - 1 - j:A_CONV_WIDTH - j, :] * _window(ea, r0 + j, CONV_ROWS)
            da1s[r0:r0 + CONV_ROWS, :] = acc
        for j0 in range(0, A_CONV_WIDTH, DW_TAPS):
            taps = range(j0, min(j0 + DW_TAPS, A_CONV_WIDTH))
            part = [jnp.zeros((CONV_ROWS, A_DIM), F32) for _ in taps]
            for r0 in range(0, tm, CONV_ROWS):
                a1c = a1s[r0:r0 + CONV_ROWS, :]
                for u, j in enumerate(taps):
                    part[u] = part[u] + _window(ea, r0 + j, CONV_ROWS) * a1c
            for u, j in enumerate(taps):
                dw_acc[A_CONV_WIDTH - 1 - j] += part[u]
        da1 = da1s[...]
        da_val = da1 * sig
        da_gate = da1 * a_val * (sig * (1.0 - sig))

        db_gate = dbo * cv_ref[...].astype(F32)
        cb = c_gate * b_val
        dcb = jnp.zeros((tm, B_DIM), F32)
        for j in range(B_CONV_WIDTH):
            k = B_CONV_WIDTH - 1 - j
            sl = eb[j:j + tm, :]
            dcb = dcb + cbw_ref[k:k + 1, :] * sl
            dcbw_ref[k:k + 1, :] += jnp.sum(sl * cb, axis=0, keepdims=True)
        dz = jnp.concatenate([da_val, da_gate, db_gate, dcb * b_val, dcb * c_gate], axis=1).astype(BF16)
        dz_ref[...] = dz
        dn = jnp.zeros((tm, D_MODEL), F32)
        for j in range(N_CHIPS):
            dn = dn + _dot_nt(dz[:, j * ws:(j + 1) * ws], win_v[j])
        xv = x_ref[...]
        g = g_ref[...]
        _, rstd = _rms_fwd(xv, g)
        dx, dg = _rms_bwd(dn, xv, rstd, g)
        dx_ref[...] = dhv + dx
        dg_ref[...] += dg

        @pl.when(i == nt - 1)
        def _():
            for k in range(A_CONV_WIDTH):
                dcaw_ref[k:k + 1, :] = jnp.sum(dw_acc[k], axis=0, keepdims=True)

    row = lambda cols: jax.ShapeDtypeStruct((1, cols), F32)
    rs_ = functools.partial(_row_spec, rev_nt=nt)
    return _pallas(
        body, [dh, x, norm_g, z, a2, cv, w_in, conv_a_w, ln_g, ln_b, conv_b_w, w_out], name="bwd_even", grid=(nt,),
        in_specs=[rs_(tm, D_MODEL), rs_(tm, D_MODEL), _full_spec((1, D_MODEL)), rs_(tm, IN_EVEN), rs_(tm, A_DIM),
                  rs_(tm, B_DIM), ANY, _full_spec((A_CONV_WIDTH, A_DIM)), _full_spec((1, A_DIM)), _full_spec((1, A_DIM)),
                  _full_spec((B_CONV_WIDTH, B_DIM)), ANY],
        out_specs=[rs_(tm, D_MODEL), rs_(tm, IN_EVEN), _full_spec((1, D_MODEL)), _full_spec((A_CONV_WIDTH, A_DIM)),
                   _full_spec((1, A_DIM)), _full_spec((1, A_DIM)), _full_spec((1, A_DIM)), _full_spec((B_CONV_WIDTH, B_DIM))],
        out_shape=[jax.ShapeDtypeStruct((tokens, D_MODEL), F32), jax.ShapeDtypeStruct((tokens, IN_EVEN), BF16),
                   row(D_MODEL), jax.ShapeDtypeStruct((A_CONV_WIDTH, A_DIM), F32), row(A_DIM), row(A_DIM), row(A_DIM),
                   jax.ShapeDtypeStruct((B_CONV_WIDTH, B_DIM), F32)],
        scratch_shapes=[pltpu.VMEM((N_CHIPS, D_MODEL, ws), BF16), pltpu.VMEM((D_MODEL, D_MODEL), BF16),
                        pltpu.VMEM((SUBLANES, tm + A_HALO, A_DIM), F32), pltpu.VMEM((tm + B_HALO, B_DIM), F32),
                        pltpu.VMEM((tm, A_DIM), F32), pltpu.VMEM((tm, A_DIM), F32),
                        pltpu.VMEM((A_CONV_WIDTH, CONV_ROWS, A_DIM), F32), pltpu.SemaphoreType.DMA((N_LOADS,))],
        vmem_mib=56, riders=riders)


def _wgrad(a, b, name, *, col_shards, riders=()):
    tokens, m = a.shape
    n = b.shape[1]
    kc = 512
    if col_shards:
        bm, bn = m // 2, n // N_CHIPS
        grid = (2, N_CHIPS)
        out_spec = pl.BlockSpec((None, None, bm, bn), lambda i, j: (j, i, 0, 0))
    elif m // 8 >= MXU_ROWS:
        bm, bn = m // 8, n
        grid = (8, 1)
        out_spec = pl.BlockSpec((None, None, bm, bn), lambda i, j: (i // 2, i % 2, 0, 0))
    else:
        bm, bn = m // N_CHIPS, n
        grid = (N_CHIPS, 1)
        out_spec = pl.BlockSpec((None, 2, bm // 2, bn), lambda i, j: (i, 0, 0, 0))

    def body(a_ref, b_ref, o_ref):
        acc = jnp.zeros((bm, bn), F32)
        for k0 in range(0, tokens, kc):
            acc = acc + _dot_tn(a_ref[k0:k0 + kc, :].astype(BF16), b_ref[k0:k0 + kc, :].astype(BF16))
        if len(o_ref.shape) == 3:
            o_ref[0] = acc[0:bm // 2]
            o_ref[1] = acc[bm // 2:bm]
        else:
            o_ref[...] = acc

    out_rows = m // 2 if col_shards else m // 8
    outs, routs = _pallas(
        body, [a, b], name=name, grid=grid,
        in_specs=[pl.BlockSpec((tokens, bm), lambda i, j: (0, i)), pl.BlockSpec((tokens, bn), lambda i, j: (0, j))],
        out_specs=[out_spec], out_shape=[jax.ShapeDtypeStruct((N_CHIPS, 2, out_rows, bn), F32)],
        vmem_mib=56, riders=riders)
    return outs[0], routs


def _wgrad_pair(a, b, name, *, col_shards, riders=(), to_chips=False):
    tokens, m = a.shape
    n = b.shape[1]
    kc = 512
    x0, y0, c0 = _mesh_pos()
    rot = 1 if to_chips else 0

    def slab(q, pre):
        return (q + rot * (1 + pre[1])) % N_CHIPS

    if col_shards:
        bm, bn = m // 2, n // N_CHIPS
        a_spec = pl.BlockSpec((tokens, bm), lambda ph, q, pre: (0, (ph + 1 + pre[0]) % 2))
        b_spec = pl.BlockSpec((tokens, bn), lambda ph, q, pre: (0, slab(q, pre)))
    else:
        bm, bn = m // 8, n
        a_spec = pl.BlockSpec((tokens, bm), lambda ph, q, pre: (0, 2 * slab(q, pre) + (ph + 1 + pre[0]) % 2))
        b_spec = pl.BlockSpec((tokens, bn), lambda ph, q, pre: (0, 0))

    def body(pre_ref, a_ref, b_ref, o_ref, *rest):
        if to_chips:
            land, give, got, mine, send_sems, recv_sems, chip_send, chip_recv = rest
        else:
            give, got, send_sems, recv_sems = rest
        ph, q = pl.program_id(0), pl.program_id(1)
        acc = jnp.zeros((bm, bn), F32)
        for k0 in range(0, tokens, kc):
            acc = acc + _dot_tn(a_ref[k0:k0 + kc, :].astype(BF16), b_ref[k0:k0 + kc, :].astype(BF16))
        x, y, cc = _mesh_pos()

        def tile(t):
            return _remote(give.at[t], got.at[t], send_sems.at[t], recv_sems.at[t], (x, y, 1 - cc))

        def to_chip(s):
            t = (s + 1 + 2 * x + y) % N_CHIPS
            tx, ty = t // 2, t % 2
            k = 2 * (ty ^ y) + (tx ^ x) - 1
            return _remote(mine.at[s], land.at[k], chip_send.at[k], chip_recv.at[k], (tx, ty, cc))

        @pl.when(ph == 0)
        def _():
            give[q] = acc
            tile(q).start()

        @pl.when(ph == 1)
        def _():
            tile(q).wait_recv()
            total = (acc + got[q]).astype(BF16)
            o_ref[...] = total
            if to_chips:
                for s in range(N_CHIPS - 1):
                    @pl.when(q == s)
                    def _(s=s):
                        mine[s] = total
                        to_chip(s).start()

        @pl.when((ph == 1) & (q == N_CHIPS - 1))
        def _():
            for t in range(N_CHIPS):
                tile(t).wait_send()
            if to_chips:
                for s in range(N_CHIPS - 1):
                    to_chip(s).wait()

    prefetch = jnp.stack([c0, 2 * x0 + y0]).astype(jnp.int32)
    out_specs = [pl.BlockSpec((None, bm, bn), lambda ph, q, pre: (slab(ph * q, pre), 0, 0))]
    out_shape = [jax.ShapeDtypeStruct((N_CHIPS, bm, bn), BF16)]
    scratch = [pltpu.VMEM((N_CHIPS, bm, bn), F32), pltpu.VMEM((N_CHIPS, bm, bn), F32)]
    sems = [pltpu.SemaphoreType.DMA((N_CHIPS,)), pltpu.SemaphoreType.DMA((N_CHIPS,))]
    if to_chips:
        out_specs.append(ANY)
        out_shape.append(jax.ShapeDtypeStruct((N_CHIPS - 1, bm, bn), BF16))
        scratch.append(pltpu.VMEM((N_CHIPS - 1, bm, bn), BF16))
        sems += [pltpu.SemaphoreType.DMA((N_CHIPS - 1,)), pltpu.SemaphoreType.DMA((N_CHIPS - 1,))]
    outs, routs = _pallas(
        body, [a, b], name=name, grid=(2, N_CHIPS), in_specs=[a_spec, b_spec], out_specs=out_specs, out_shape=out_shape,
        scratch_shapes=scratch + sems, vmem_mib=56, riders=riders, prefetch=prefetch)
    return (outs if to_chips else outs[0]), routs


class _GradReduce:
    def __init__(self, name, grad=None, chip_sum=None):
        self.name, self.grad, self.chip_sum = name, grad, chip_sum
        self.full = None

    def pair_swap(self):
        return _PairSwap([self.grad])

    def took_pair(self, outs):
        self.chip_sum = _in_hbm(_add_pair(self.grad, outs[0], f"pair_sum_{self.name}"))

    def chip_swap(self):
        return _ChipSwap([self.chip_sum])

    def took_chips(self, outs):
        self.full = _in_hbm(_add_chips(self.chip_sum, outs[0], f"chip_sum_{self.name}"))

    def pair_share(self):
        return _PairShare([self.full])

    def took_share(self, outs):
        self.full = outs[0]

    def reduced(self):
        return jnp.reshape(self.full, (2 * self.full.shape[1], self.full.shape[2]))


def _forward_backward(x2, tgt2, gathered, staged, conv_a_w, conv_b_w, od_norm, od_bias, od_lng, od_lnb,
                      ev_norm_g, ev_conv_a_b, ev_ln_a_g, ev_ln_a_b, od_w_s, od_b_s, mlp_norm_g, final_norm_g,
                      *, tm, seq, distributed=True):
    d = x2.shape[1]
    w = dict(gathered)
    b_s_rows = jnp.broadcast_to(od_b_s[0][:, :, None], (C_GROUPS, CHUNK, CHUNK))

    def ride(*names):
        return [_Gather([staged[nm] for nm in names])] if distributed else []

    def land(routs, *names):
        if distributed:
            for nm, buf in zip(names, routs[0]):
                w[nm] = buf

    def as_cols(buf):
        return jnp.reshape(buf, (N_CHIPS, 2 * buf.shape[2], buf.shape[3]))

    def as_rows(buf):
        return jnp.reshape(buf, (8 * buf.shape[2], buf.shape[3]))

    (h1, n0, z, a2, cv, mix), routs = _fwd_even(
        x2, ev_norm_g, as_cols(w["ev_in"]), conv_a_w, ev_conv_a_b, ev_ln_a_g, ev_ln_a_b, conv_b_w, as_rows(w["ev_out"]),
        tm=tm, seq=seq, riders=ride("w1_0", "w2_0"))
    land(routs, "w1_0", "w2_0")
    (h2, n1, p0, q0), routs = _fwd_mlp(h1, mlp_norm_g[0:1], as_cols(w["w1_0"]), as_cols(w["w2_0"]), 0, tm=tm,
                                       riders=ride("od_in", "od_out", "w1_1"))
    land(routs, "od_in", "od_out", "w1_1")
    (h3, n2, s, cdf, sv, y), routs = _fwd_odd(h2, od_norm, as_cols(w["od_in"]), od_bias, od_lng, od_lnb, od_w_s[0], b_s_rows,
                                         as_rows(w["od_out"]), tm=tm, riders=ride("w2_1"))
    land(routs, "w2_1")
    (n3, p1, q1, loss_part, dh4, dh4b, d_final_g), _ = _fwd_mlp(
        h3, mlp_norm_g[1:2], as_cols(w["w1_1"]), as_cols(w["w2_1"]), 1, tm=tm,
        head=(jnp.reshape(final_norm_g, (1, d)), tgt2))

    red = {}

    def swap(*names):
        return [red[nm].pair_swap() for nm in names] if distributed else []

    def chips(*names):
        return [red[nm].chip_swap() for nm in names] if distributed else []

    def share(*names):
        return [red[nm].pair_share() for nm in names] if distributed else []

    def took(routs, *steps):
        if distributed:
            for (nm, what), outs in zip(steps, routs):
                getattr(red[nm], what)(outs)

    def big(lhs, rhs, name, col_shards, riders=(), to_chips=False):
        if distributed and to_chips:
            (chip_sum, from_chips), routs = _wgrad_pair(lhs, rhs, f"wgrad_{name}", col_shards=col_shards, riders=riders,
                                                        to_chips=True)
            red[name] = _GradReduce(name, chip_sum=_in_hbm(chip_sum))
            red[name].took_chips([_in_hbm(from_chips)])
        elif distributed:
            chip_sum, routs = _wgrad_pair(lhs, rhs, f"wgrad_{name}", col_shards=col_shards, riders=riders)
            red[name] = _GradReduce(name, chip_sum=_in_hbm(chip_sum))
        else:
            g, routs = _wgrad(lhs, rhs, f"wgrad_{name}", col_shards=col_shards)
            red[name] = _GradReduce(name, grad=g)
        return routs

    big(q1, dh4b, "w2_1", False)
    (dh3, dh3b, dp1, d_mlp_g1), routs = _bwd_mlp(dh4, h3, mlp_norm_g[1:2], p1, as_cols(w["w1_1"]), as_cols(w["w2_1"]), 1, tm=tm,
                                           riders=chips("w2_1"))
    took(routs, ("w2_1", "took_chips"))
    big(n3, dp1, "w1_1", True)
    g, routs = _wgrad(y, dh3b, "wgrad_od_out", col_shards=False, riders=share("w2_1"))
    red["od_out"] = _GradReduce("od_out", grad=g)
    took(routs, ("w2_1", "took_share"))
    (dh2, dh2b, ds, d_od_norm, d_od_bin, d_od_lng, d_od_lnb, d_ws, d_bs), routs = _bwd_odd(
        dh3, h2, od_norm, s, cdf, sv, as_cols(w["od_in"]), od_lng, od_lnb, od_w_s[0], as_rows(w["od_out"]), tm=tm,
        riders=chips("w1_1") + swap("od_out"))
    took(routs, ("w1_1", "took_chips"), ("od_out", "took_pair"))
    routs = big(n2, ds, "od_in", True, riders=share("w1_1"))
    took(routs, ("w1_1", "took_share"))
    half_groups = C_GROUPS // 2
    early = {"loss": loss_part, "od_w_s_lo": d_ws[:half_groups], "od_b_s": d_bs, "mlp_norm_g1": d_mlp_g1, "final_norm_g": d_final_g,
             "od_norm_g": d_od_norm, "od_b_in": d_od_bin, "od_ln_v_g": d_od_lng, "od_ln_v_b": d_od_lnb}
    share_early = [_ShareAll(list(early.values()))] if distributed else []
    routs = big(q0, dh2b, "w2_0", False, riders=share_early)
    landed_early = routs[0] if distributed else []
    (dh1, dh1b, dp0, d_mlp_g0), routs = _bwd_mlp(dh2, h1, mlp_norm_g[0:1], p0, as_cols(w["w1_0"]), as_cols(w["w2_0"]), 0, tm=tm,
                                           riders=chips("od_out") + chips("od_in") + chips("w2_0"))
    took(routs, ("od_out", "took_chips"), ("od_in", "took_chips"), ("w2_0", "took_chips"))
    middle = {"od_w_s_hi": d_ws[half_groups:]}
    share_middle = [_ShareAll(list(middle.values()))] if distributed else []
    routs = big(n1, dp0, "w1_0", True, riders=share("od_out") + share("od_in") + share("w2_0") + share_middle)
    took(routs, ("od_out", "took_share"), ("od_in", "took_share"), ("w2_0", "took_share"))
    landed_middle = routs[3] if distributed else []
    g, _ = _wgrad(mix, dh1b, "wgrad_ev_out", col_shards=False)
    red["ev_out"] = _GradReduce("ev_out", grad=g)

    (dx, dz, d_ev_norm, d_caw, d_cab, d_ev_lng, d_ev_lnb, d_cbw), routs = _bwd_even(
        dh1, x2, ev_norm_g, z, a2, cv, as_cols(w["ev_in"]), conv_a_w, ev_ln_a_g, ev_ln_a_b, conv_b_w, as_rows(w["ev_out"]),
        tm=tm, seq=seq, riders=chips("w1_0") + swap("ev_out"))
    took(routs, ("w1_0", "took_chips"), ("ev_out", "took_pair"))
    late = {"mlp_norm_g0": d_mlp_g0, "ev_norm_g": d_ev_norm, "ev_conv_a_b": d_cab, "ev_ln_a_g": d_ev_lng,
            "ev_ln_a_b": d_ev_lnb, "ev_conv_a_w": d_caw, "ev_conv_b_w": d_cbw}
    share_late = [_ShareAll(list(late.values()))] if distributed else []
    routs2 = big(n0, dz, "ev_in", True, riders=chips("ev_out") + share("w1_0") + share_late, to_chips=True)
    took(routs2, ("ev_out", "took_chips"), ("w1_0", "took_share"))
    own = {**early, **middle, **late}
    landed = dict(zip(own.keys(), landed_early + landed_middle + routs2[2])) if distributed else None
    return dx, red, own, landed


def _rows128(a):
    rows = jnp.reshape(a, (-1, LANES))
    pad = (-rows.shape[0]) % SUBLANES
    return jnp.pad(rows, ((0, pad), (0, 0))) if pad else rows


def _pack(arrays):
    return jnp.concatenate([_rows128(a) for a in arrays], axis=0)


def _unpack(buf, shapes):
    out, r0 = [], 0
    for shp in shapes:
        size = 1
        for dim in shp:
            size *= dim
        nr = size // LANES
        out.append(jnp.reshape(buf[r0:r0 + nr], shp))
        r0 += nr + (-nr) % SUBLANES
    return out


def kernel(x, ev_norm_g, ev_w_in, ev_conv_a_w, ev_conv_a_b, ev_ln_a_g, ev_ln_a_b, ev_conv_b_w, ev_w_out, od_norm_g, od_w_in, od_b_in, od_ln_v_g, od_ln_v_b, od_w_s, od_b_s, od_w_out, mlp_norm_g, mlp_w1, mlp_w2, final_norm_g, loss_target, m_ev_norm_g, m_ev_w_in, m_ev_conv_a_w, m_ev_conv_a_b, m_ev_ln_a_g, m_ev_ln_a_b, m_ev_conv_b_w, m_ev_w_out, m_od_norm_g, m_od_w_in, m_od_b_in, m_od_ln_v_g, m_od_ln_v_b, m_od_w_s, m_od_b_s, m_od_w_out, m_mlp_norm_g, m_mlp_w1, m_mlp_w2, m_final_norm_g, v_ev_norm_g, v_ev_w_in, v_ev_conv_a_w, v_ev_conv_a_b, v_ev_ln_a_g, v_ev_ln_a_b, v_ev_conv_b_w, v_ev_w_out, v_od_norm_g, v_od_w_in, v_od_b_in, v_od_ln_v_g, v_od_ln_v_b, v_od_w_s, v_od_b_s, v_od_w_out, v_mlp_norm_g, v_mlp_w1, v_mlp_w2, v_final_norm_g):
    tm = TOKEN_TILE
    batch, seq, d = x.shape
    tokens = batch * seq
    x2 = jnp.reshape(x, (tokens, d))
    tgt2 = jnp.reshape(loss_target, (tokens, d))
    chip = 2 * lax.axis_index("x") + lax.axis_index("y")

    small_shapes = [(A_CONV_WIDTH, LANES), (B_CONV_WIDTH, LANES), (256,), (512,), (256,), (256,)]
    small_shard = _pack([ev_conv_a_w[0], ev_conv_b_w[0], od_norm_g[0], od_b_in[0], od_ln_v_g[0], od_ln_v_b[0]])
    small_shard = jnp.pad(small_shard, ((0, (-small_shard.shape[0]) % (2 * SUBLANES)), (0, 0)))
    first = [_place_shard(ev_w_in, 0, BF16, "place_ev_w_in"), _place_shard(ev_w_out, 0, BF16, "place_ev_w_out"),
             _place_shard(small_shard[None], 0, F32, "place_small")]
    staged = {
        "w1_0": _place_shard(mlp_w1, 0, BF16, "place_w1_0"), "w2_0": _place_shard(mlp_w2, 0, BF16, "place_w2_0"),
        "od_in": _place_shard(od_w_in, 0, BF16, "place_od_w_in"), "od_out": _place_shard(od_w_out, 0, BF16, "place_od_w_out"),
        "w1_1": _place_shard(mlp_w1, 1, BF16, "place_w1_1"), "w2_1": _place_shard(mlp_w2, 1, BF16, "place_w2_1"),
    }
    first = [_in_hbm(a) for a in first]
    staged = {nm: _in_hbm(a) for nm, a in staged.items()}
    (g_ev_in, g_ev_out, g_small), = _exchange([_Gather(first)], "gather_first")
    small_all = jnp.reshape(g_small, (N_CHIPS, -1, LANES))
    per_chip = [_unpack(small_all[q], small_shapes) for q in range(N_CHIPS)]
    conv_a_w = jnp.concatenate([pc[0] for pc in per_chip], axis=1)
    conv_b_w = jnp.concatenate([pc[1] for pc in per_chip], axis=1)
    od_norm = jnp.concatenate([pc[2] for pc in per_chip])[None, :]
    od_bias = jnp.concatenate([pc[3] for pc in per_chip])[None, :]
    od_lng = jnp.concatenate([pc[4] for pc in per_chip])[None, :]
    od_lnb = jnp.concatenate([pc[5] for pc in per_chip])[None, :]

    dx, red, own, landed = _forward_backward(
        x2, tgt2, {"ev_in": g_ev_in, "ev_out": g_ev_out}, staged, conv_a_w, conv_b_w, od_norm, od_bias, od_lng, od_lnb,
        ev_norm_g, ev_conv_a_b, ev_ln_a_g, ev_ln_a_b, od_w_s, od_b_s, mlp_norm_g, final_norm_g, tm=tm, seq=seq)

    routs = _exchange([red["ev_in"].pair_share(), red["ev_out"].pair_share()], "reduce_tail")
    red["ev_in"].took_share(routs[0])
    red["ev_out"].took_share(routs[1])

    given = {"ev_norm_g": (ev_norm_g, m_ev_norm_g, v_ev_norm_g), "ev_conv_a_b": (ev_conv_a_b, m_ev_conv_a_b, v_ev_conv_a_b),
             "ev_ln_a_g": (ev_ln_a_g, m_ev_ln_a_g, v_ev_ln_a_g), "ev_ln_a_b": (ev_ln_a_b, m_ev_ln_a_b, v_ev_ln_a_b),
             "od_w_s": (od_w_s, m_od_w_s, v_od_w_s), "od_b_s": (od_b_s, m_od_b_s, v_od_b_s),
             "mlp_norm_g": (mlp_norm_g, m_mlp_norm_g, v_mlp_norm_g), "final_norm_g": (final_norm_g, m_final_norm_g, v_final_norm_g),
             "ev_conv_a_w": (ev_conv_a_w, m_ev_conv_a_w, v_ev_conv_a_w), "ev_conv_b_w": (ev_conv_b_w, m_ev_conv_b_w, v_ev_conv_b_w),
             "od_norm_g": (od_norm_g, m_od_norm_g, v_od_norm_g), "od_b_in": (od_b_in, m_od_b_in, v_od_b_in),
             "od_ln_v_g": (od_ln_v_g, m_od_ln_v_g, v_od_ln_v_g), "od_ln_v_b": (od_ln_v_b, m_od_ln_v_b, v_od_ln_v_b)}
    shaped = {nm: tuple(jnp.reshape(a, shape) for a in given[nm]) for nm, shape, _, _ in SMALL_WEIGHTS}
    loss11, small_upd = _small_update(own, landed, shaped)
    loss = loss11[0, 0]
    upd = {nm: [jnp.reshape(o, given[nm][0].shape) for o in outs] for nm, outs in small_upd.items()}

    def big_update(wt, m, v, names, call):
        grads = [red[nm].reduced() for nm in names]
        shp3 = (len(grads),) + grads[0].shape
        outs, _ = _adamw(jnp.reshape(wt, shp3), jnp.reshape(m, shp3), jnp.reshape(v, shp3), grads, call)
        return [jnp.reshape(o, wt.shape) for o in outs], None

    upd["mlp_w2"], _ = big_update(mlp_w2, m_mlp_w2, v_mlp_w2, ["w2_0", "w2_1"], "adamw_mlp_w2")
    upd["mlp_w1"], _ = big_update(mlp_w1, m_mlp_w1, v_mlp_w1, ["w1_0", "w1_1"], "adamw_mlp_w1")
    upd["ev_w_in"], _ = big_update(ev_w_in, m_ev_w_in, v_ev_w_in, ["ev_in"], "adamw_ev_w_in")
    upd["ev_w_out"], _ = big_update(ev_w_out, m_ev_w_out, v_ev_w_out, ["ev_out"], "adamw_ev_w_out")
    upd["od_w_in"], _ = big_update(od_w_in, m_od_w_in, v_od_w_in, ["od_in"], "adamw_od_w_in")
    upd["od_w_out"], _ = big_update(od_w_out, m_od_w_out, v_od_w_out, ["od_out"], "adamw_od_w_out")

    order = ["ev_norm_g", "ev_w_in", "ev_conv_a_w", "ev_conv_a_b", "ev_ln_a_g", "ev_ln_a_b", "ev_conv_b_w", "ev_w_out",
             "od_norm_g", "od_w_in", "od_b_in", "od_ln_v_g", "od_ln_v_b", "od_w_s", "od_b_s", "od_w_out", "mlp_norm_g",
             "mlp_w1", "mlp_w2", "final_norm_g"]
    grad_x = jnp.reshape(dx, x.shape)
    return (loss, grad_x, *[upd[nm][0] for nm in order], *[upd[nm][1] for nm in order],
            *[upd[nm][2] for nm in order], *[upd[nm][3] for nm in order])
```

```python
import functools

import jax
import jax.numpy as jnp
from jax import lax
from jax.experimental import pallas as pl
from jax.experimental.pallas import tpu as pltpu

F32 = jnp.float32
BF16 = jnp.bfloat16

D_MODEL = 1024
A_DIM = 512
B_DIM = 512
IN_EVEN = 2 * A_DIM + 3 * B_DIM
A_CONV_WIDTH = 31
B_CONV_WIDTH = 3
CHUNK = 128
C_GROUPS = 8
C_DIM = 1024
D_FF = 4096
RMS_EPS = 1e-6
LN_EPS = 1e-5
ADAM_LR = 0.001
ADAM_B1 = 0.9
ADAM_B2 = 0.999
ADAM_EPS = 1e-08
ADAM_WD = 0.01
ADAM_STEP = 10

N_CHIPS = 4
N_DEV = 8
TOKEN_TILE = 512
A_HALO = 32
B_HALO = 8
CONV_ROWS = 16
DW_TAPS = 4
PAIR = 2 * CHUNK
LANES = 128
SUBLANES = 8
MXU_ROWS = 256
MIB = 1024 * 1024
MESH = pl.DeviceIdType.MESH
ANY = pl.BlockSpec(memory_space=pl.ANY)


def _dot(a, b):
    return lax.dot_general(a, b, (((1,), (0,)), ((), ())), preferred_element_type=F32)


def _dot_nt(a, b):
    return lax.dot_general(a, b, (((1,), (1,)), ((), ())), preferred_element_type=F32)


def _dot_tn(a, b):
    return lax.dot_general(a, b, (((0,), (0,)), ((), ())), preferred_element_type=F32)


def _params(vmem_mib, n_axes=1):
    return pltpu.CompilerParams(dimension_semantics=("arbitrary",) * n_axes, vmem_limit_bytes=vmem_mib * MIB)


def _row_spec(tm, cols, rev_nt=None):
    if rev_nt is None:
        return pl.BlockSpec((tm, cols), lambda i: (i, 0))
    return pl.BlockSpec((tm, cols), lambda i: (rev_nt - 1 - i, 0))


def _full_spec(shape):
    nd = len(shape)
    return pl.BlockSpec(shape, lambda i: (0,) * nd)


def _block_rows(rows, cap=512):
    best = SUBLANES
    for br in range(SUBLANES, min(rows, cap) + 1, SUBLANES):
        if rows % br == 0:
            best = br
    return best


N_LOADS = 2


def _load_weights(pairs, sems):
    @pl.when(pl.program_id(0) == 0)
    def _():
        copies = [pltpu.make_async_copy(src, dst, sems.at[k]) for k, (src, dst) in enumerate(pairs)]
        for cp in copies:
            cp.start()
        for cp in copies:
            cp.wait()


def _rms_fwd(x, g):
    rstd = lax.rsqrt(jnp.mean(x * x, axis=-1, keepdims=True) + RMS_EPS)
    return x * rstd * g, rstd


def _rms_bwd(dn, x, rstd, g):
    a = dn * g
    xh = x * rstd
    dx = rstd * (a - xh * jnp.mean(a * xh, axis=-1, keepdims=True))
    dg = jnp.sum(dn * xh, axis=0, keepdims=True)
    return dx, dg


def _ln_stats(v):
    mu = jnp.mean(v, axis=-1, keepdims=True)
    xc = v - mu
    rs = lax.rsqrt(jnp.mean(xc * xc, axis=-1, keepdims=True) + LN_EPS)
    return xc * rs, rs


def _ln_bwd(dy, xhat, rs, g):
    dxh = dy * g
    dv = rs * (dxh - jnp.mean(dxh, axis=-1, keepdims=True) - xhat * jnp.mean(dxh * xhat, axis=-1, keepdims=True))
    return dv, jnp.sum(dy * xhat, axis=0, keepdims=True), jnp.sum(dy, axis=0, keepdims=True)


def _gelu_cdf(s):
    return 0.5 * (1.0 + lax.erf(s * 0.7071067811865476))


def _mesh_pos():
    return lax.axis_index("x"), lax.axis_index("y"), lax.axis_index("c")


def _other_chips(x, y):
    return [(1 - x, y), (x, 1 - y), (1 - x, 1 - y)]


def _remote(src, dst, send_sem, recv_sem, to):
    return pltpu.make_async_remote_copy(src_ref=src, dst_ref=dst, send_sem=send_sem, recv_sem=recv_sem,
                                        device_id=to, device_id_type=MESH)


def _like(arrays):
    return [jax.ShapeDtypeStruct(a.shape, a.dtype) for a in arrays]


class _Gather:
    def __init__(self, bufs):
        self.ins = list(bufs)
        self.out_shapes = _like(bufs)
        self.aliases = {t: t for t in range(len(bufs))}
        self.n_sems = 6 * len(bufs)

    def _ici(self, ins, outs, send, recv, t, k, chip, mine, c):
        return _remote(ins[t].at[mine, c], outs[t].at[mine, c], send.at[6 * t + k], recv.at[6 * t + k], (*chip, c))

    def start(self, ins, outs, send, recv):
        x, y, c = _mesh_pos()
        for t in range(len(ins)):
            for k, chip in enumerate(_other_chips(x, y)):
                self._ici(ins, outs, send, recv, t, k, chip, 2 * x + y, c).start()

    def _pass_on(self, outs, send, recv, t, k, chip, c, to):
        blk = outs[t].at[2 * chip[0] + chip[1], c]
        return _remote(blk, blk, send.at[6 * t + 3 + k], recv.at[6 * t + 3 + k], to)

    def near_end(self, ins, outs, send, recv):
        x, y, c = _mesh_pos()
        for t in range(len(ins)):
            for k, chip in enumerate(_other_chips(x, y)):
                blk = outs[t].at[2 * chip[0] + chip[1], c]
                _remote(blk, blk, send.at[6 * t + k], recv.at[6 * t + k], (x, y, c)).wait_recv()
                self._pass_on(outs, send, recv, t, k, chip, c, (x, y, 1 - c)).start()

    def finish(self, ins, outs, send, recv):
        x, y, c = _mesh_pos()
        chips = _other_chips(x, y)
        for t in range(len(ins)):
            for k, chip in enumerate(chips):
                self._pass_on(outs, send, recv, t, k, chip, 1 - c, (x, y, c)).wait_recv()
        for t in range(len(ins)):
            for k, chip in enumerate(chips):
                self._ici(ins, outs, send, recv, t, k, chip, 2 * x + y, c).wait_send()
                self._pass_on(outs, send, recv, t, k, chip, c, (x, y, 1 - c)).wait_send()


class _PairSwap:
    def __init__(self, grads):
        self.ins = list(grads)
        self.out_shapes = [jax.ShapeDtypeStruct((g.shape[0],) + g.shape[2:], g.dtype) for g in grads]
        self.aliases = {}
        self.n_sems = len(grads)

    def _copies(self, ins, outs, send, recv):
        x, y, c = _mesh_pos()
        return [_remote(ins[t].at[:, 1 - c], outs[t], send.at[t], recv.at[t], (x, y, 1 - c)) for t in range(len(ins))]

    def start(self, ins, outs, send, recv):
        for cp in self._copies(ins, outs, send, recv):
            cp.start()

    def finish(self, ins, outs, send, recv):
        for cp in self._copies(ins, outs, send, recv):
            cp.wait()


class _ChipSwap:
    def __init__(self, parts):
        self.ins = list(parts)
        self.out_shapes = [jax.ShapeDtypeStruct((3,) + p.shape[1:], p.dtype) for p in parts]
        self.aliases = {}
        self.n_sems = 3 * len(parts)

    def _copies(self, ins, outs, send, recv):
        x, y, c = _mesh_pos()
        return [_remote(ins[t].at[2 * chip[0] + chip[1]], outs[t].at[k], send.at[3 * t + k], recv.at[3 * t + k], (*chip, c))
                for t in range(len(ins)) for k, chip in enumerate(_other_chips(x, y))]

    def start(self, ins, outs, send, recv):
        for cp in self._copies(ins, outs, send, recv):
            cp.start()

    def finish(self, ins, outs, send, recv):
        for cp in self._copies(ins, outs, send, recv):
            cp.wait()


class _PairShare:
    def __init__(self, fulls):
        self.ins = list(fulls)
        self.out_shapes = _like(fulls)
        self.aliases = {t: t for t in range(len(fulls))}
        self.n_sems = len(fulls)

    def _copies(self, ins, outs, send, recv):
        x, y, c = _mesh_pos()
        return [_remote(ins[t].at[c], outs[t].at[c], send.at[t], recv.at[t], (x, y, 1 - c)) for t in range(len(ins))]

    def start(self, ins, outs, send, recv):
        for cp in self._copies(ins, outs, send, recv):
            cp.start()

    def finish(self, ins, outs, send, recv):
        for cp in self._copies(ins, outs, send, recv):
            cp.wait()


class _ShareAll:
    def __init__(self, arrays):
        self.ins = list(arrays)
        self.out_shapes = [jax.ShapeDtypeStruct((N_DEV,) + a.shape, a.dtype) for a in arrays]
        self.aliases = {}
        self.n_sems = (N_DEV - 1) * len(arrays)

    def _peers(self):
        x, y, c = _mesh_pos()
        flips = [((r >> 2) & 1, (r >> 1) & 1, r & 1) for r in range(1, N_DEV)]
        return (x, y, c), [(x ^ fx, y ^ fy, c ^ fc) for fx, fy, fc in flips]

    def _sends(self, ins, outs, send, recv):
        (x, y, c), peers = self._peers()
        mine = 4 * x + 2 * y + c
        return [_remote(ins[a], outs[a].at[mine], send.at[7 * a + r], recv.at[7 * a + r], peer)
                for a in range(len(ins)) for r, peer in enumerate(peers)]

    def start(self, ins, outs, send, recv):
        for cp in self._sends(ins, outs, send, recv):
            cp.start()

    def finish(self, ins, outs, send, recv):
        (x, y, c), peers = self._peers()
        for a in range(len(ins)):
            for r, (px, py, pc) in enumerate(peers):
                blk = outs[a].at[4 * px + 2 * py + pc]
                _remote(blk, blk, send.at[7 * a + r], recv.at[7 * a + r], (x, y, c)).wait_recv()
        for cp in self._sends(ins, outs, send, recv):
            cp.wait_send()


def _pallas(body, operands, *, name, grid, in_specs, out_specs, out_shape, scratch_shapes=(), vmem_mib=32, riders=(),
            prefetch=None):
    in_specs, out_specs, out_shape, scratch_shapes = list(in_specs), list(out_specs), list(out_shape), list(scratch_shapes)
    if not riders and prefetch is None:
        outs = pl.pallas_call(body, name=name, grid=grid, in_specs=in_specs, out_specs=out_specs, out_shape=out_shape,
                              scratch_shapes=scratch_shapes, compiler_params=_params(vmem_mib, len(grid)))(*operands)
        return list(outs), []
    n_in, n_out, n_scr = len(in_specs), len(out_specs), len(scratch_shapes)
    r_in = [len(r.ins) for r in riders]
    r_out = [len(r.out_shapes) for r in riders]
    steps = 1
    for g in grid:
        steps *= g

    n_pre = 0 if prefetch is None else 1

    def wrapped(*refs):
        refs = list(refs)
        pre, refs = refs[:n_pre], refs[n_pre:]
        ins, refs = refs[:n_in], refs[n_in:]
        rins = []
        for k in r_in:
            rins.append(refs[:k])
            refs = refs[k:]
        outs, refs = refs[:n_out], refs[n_out:]
        routs = []
        for k in r_out:
            routs.append(refs[:k])
            refs = refs[k:]
        scr, sems = refs[:n_scr], refs[n_scr:]
        step = 0
        for ax, g in enumerate(grid):
            step = step * g + pl.program_id(ax)

        def each(what):
            for j, r in enumerate(riders):
                if hasattr(r, what):
                    getattr(r, what)(rins[j], routs[j], sems[2 * j], sems[2 * j + 1])

        if grid:
            pl.when(step == 0)(lambda: each("start"))
        else:
            each("start")
        body(*pre, *ins, *outs, *scr)
        if grid:
            @pl.when(step == steps - 1)
            def _():
                each("near_end")
                each("finish")
        else:
            each("near_end")
            each("finish")

    aliases, off_in, off_out = {}, n_pre + n_in, n_out
    for r, ki, ko in zip(riders, r_in, r_out):
        for i, o in r.aliases.items():
            aliases[off_in + i] = off_out + o
        off_in, off_out = off_in + ki, off_out + ko
    sems = []
    for r in riders:
        sems += [pltpu.SemaphoreType.DMA((r.n_sems,)), pltpu.SemaphoreType.DMA((r.n_sems,))]
    layout = dict(grid=grid, in_specs=in_specs + [ANY] * sum(r_in), out_specs=out_specs + [ANY] * sum(r_out),
                  scratch_shapes=scratch_shapes + sems)
    if prefetch is not None:
        layout = dict(grid_spec=pltpu.PrefetchScalarGridSpec(num_scalar_prefetch=1, **layout))
    res = pl.pallas_call(
        wrapped, name=name, **layout,
        out_shape=out_shape + [s for r in riders for s in r.out_shapes], input_output_aliases=aliases,
        compiler_params=pltpu.CompilerParams(dimension_semantics=("arbitrary",) * len(grid),
                                             vmem_limit_bytes=vmem_mib * MIB, has_side_effects=True),
    )(*([] if prefetch is None else [prefetch]), *operands, *[a for r in riders for a in r.ins])
    res = list(res)
    outs, res = res[:n_out], res[n_out:]
    routs = []
    for k in r_out:
        routs.append(res[:k])
        res = res[k:]
    return outs, routs


def _exchange(riders, name):
    return _pallas(lambda: None, [], name=name, grid=(), in_specs=[], out_specs=[], out_shape=[], riders=riders)[1]


def _in_hbm(a):
    return pltpu.with_memory_space_constraint(a, pltpu.HBM)


def _place_shard(w, layer, dtype, name):
    _, rows, cols = w.shape
    half = rows // 2
    br = _block_rows(half)
    nb = half // br
    mine = 2 * lax.axis_index("x") + lax.axis_index("y")

    def body(q_ref, w_ref, o_ref):
        o_ref[...] = w_ref[...].astype(dtype)

    return pl.pallas_call(
        body, name=name,
        grid_spec=pltpu.PrefetchScalarGridSpec(
            num_scalar_prefetch=1, grid=(2, nb),
            in_specs=[pl.BlockSpec((None, br, cols), lambda h, i, q: (layer, h * nb + i, 0))],
            out_specs=pl.BlockSpec((None, None, br, cols), lambda h, i, q: (q[0], h, i, 0))),
        out_shape=pltpu.HBM((N_CHIPS, 2, half, cols), dtype),
        compiler_params=_params(16, 2),
    )(jnp.reshape(mine, (1,)).astype(jnp.int32), w)


def _add_pair(g, recv, name):
    _, _, r, cdim = g.shape
    br = _block_rows(r, 256)
    c = lax.axis_index("c")

    def body(c_ref, g_ref, r_ref, o_ref):
        o_ref[...] = (g_ref[...] + r_ref[...]).astype(BF16)

    return pl.pallas_call(
        body, name=name,
        grid_spec=pltpu.PrefetchScalarGridSpec(
            num_scalar_prefetch=1, grid=(N_CHIPS, r // br),
            in_specs=[pl.BlockSpec((None, None, br, cdim), lambda q, i, c_ref: (q, c_ref[0], i, 0)),
                      pl.BlockSpec((None, br, cdim), lambda q, i, c_ref: (q, i, 0))],
            out_specs=pl.BlockSpec((None, br, cdim), lambda q, i, c_ref: (q, i, 0))),
        out_shape=pltpu.HBM((N_CHIPS, r, cdim), BF16),
        compiler_params=_params(16, 2),
    )(jnp.reshape(c, (1,)).astype(jnp.int32), _in_hbm(g), _in_hbm(recv))


def _add_chips(own, recv, name):
    _, r, cdim = own.shape
    br = _block_rows(r, 256)
    x, y, c = _mesh_pos()

    def body(pos_ref, own_ref, r_ref, o_ref):
        acc = own_ref[...].astype(F32)
        for k in range(3):
            acc = acc + r_ref[k].astype(F32)
        o_ref[...] = acc

    return pl.pallas_call(
        body, name=name,
        grid_spec=pltpu.PrefetchScalarGridSpec(
            num_scalar_prefetch=1, grid=(r // br,),
            in_specs=[pl.BlockSpec((None, br, cdim), lambda i, pos: (pos[0], i, 0)),
                      pl.BlockSpec((3, br, cdim), lambda i, pos: (0, i, 0))],
            out_specs=pl.BlockSpec((None, br, cdim), lambda i, pos: (pos[1], i, 0))),
        out_shape=pltpu.HBM((2, r, cdim), F32),
        compiler_params=_params(16, 1),
    )(jnp.stack([2 * x + y, c]).astype(jnp.int32), _in_hbm(own), _in_hbm(recv))


def _adam_math(w, m, v, g):
    c1 = 1.0 / (1.0 - ADAM_B1 ** ADAM_STEP)
    c2 = 1.0 / (1.0 - ADAM_B2 ** ADAM_STEP)
    m_new = ADAM_B1 * m + (1.0 - ADAM_B1) * g
    v_new = ADAM_B2 * v + (1.0 - ADAM_B2) * (g * g)
    return -ADAM_LR * ((m_new * c1) / (jnp.sqrt(v_new * c2) + ADAM_EPS) + ADAM_WD * w), m_new, v_new


SMALL_WEIGHTS = [
    ("ev_norm_g", (1, D_MODEL), ["ev_norm_g"], None), ("ev_conv_a_b", (1, A_DIM), ["ev_conv_a_b"], None),
    ("ev_ln_a_g", (1, A_DIM), ["ev_ln_a_g"], None), ("ev_ln_a_b", (1, A_DIM), ["ev_ln_a_b"], None),
    ("od_w_s", (C_GROUPS, CHUNK, CHUNK), ["od_w_s_lo", "od_w_s_hi"], None), ("od_b_s", (C_GROUPS, CHUNK), ["od_b_s"], None),
    ("mlp_norm_g", (2, D_MODEL), ["mlp_norm_g0", "mlp_norm_g1"], None), ("final_norm_g", (1, D_MODEL), ["final_norm_g"], None),
    ("ev_conv_a_w", (A_CONV_WIDTH, A_DIM // N_CHIPS), ["ev_conv_a_w"], A_DIM // N_CHIPS),
    ("ev_conv_b_w", (B_CONV_WIDTH, B_DIM // N_CHIPS), ["ev_conv_b_w"], B_DIM // N_CHIPS),
    ("od_norm_g", (1, D_MODEL // N_CHIPS), ["od_norm_g"], D_MODEL // N_CHIPS),
    ("od_b_in", (1, 2 * C_DIM // N_CHIPS), ["od_b_in"], 2 * C_DIM // N_CHIPS),
    ("od_ln_v_g", (1, C_DIM // N_CHIPS), ["od_ln_v_g"], C_DIM // N_CHIPS),
    ("od_ln_v_b", (1, C_DIM // N_CHIPS), ["od_ln_v_b"], C_DIM // N_CHIPS),
]


def _small_update(own, landed, weights):
    names = list(own.keys())
    n_g, n_w = len(names), len(SMALL_WEIGHTS)

    def body(*refs):
        refs = list(refs)
        own_refs = dict(zip(names, refs[:n_g]))
        land_refs = dict(zip(names, refs[n_g:2 * n_g]))
        wmv = [refs[2 * n_g + 3 * i:2 * n_g + 3 * i + 3] for i in range(n_w)]
        o0 = 2 * n_g + 3 * n_w
        loss_ref = refs[o0]
        outs = [refs[o0 + 1 + 4 * i:o0 + 5 + 4 * i] for i in range(n_w)]
        acc = dict(zip(names, refs[o0 + 1 + 4 * n_w:]))
        x, y, c = _mesh_pos()
        mine, chip = 4 * x + 2 * y + c, 2 * x + y

        for nm in names:
            for d in range(N_DEV):
                def add(term, nm=nm, d=d):
                    acc[nm][...] = term if d == 0 else acc[nm][...] + term
                pl.when(mine == d)(lambda nm=nm, add=add: add(own_refs[nm][...]))
                pl.when(mine != d)(lambda nm=nm, d=d, add=add: add(land_refs[nm][d]))
        loss_ref[...] = acc["loss"][...]

        def update(i, rows, g):
            w_ref, m_ref, v_ref = wmv[i]
            delta, m_new, v_new = _adam_math(w_ref[rows], m_ref[rows], v_ref[rows], g)
            for ref, val in zip(outs[i], (g, delta, m_new, v_new)):
                ref[rows] = val

        for i, (_, shape, grads, per_chip) in enumerate(SMALL_WEIGHTS):
            for row, gname in enumerate(grads):
                per_grad = shape[0] // len(grads)
                rows = slice(row * per_grad, (row + 1) * per_grad)
                if per_chip is None:
                    update(i, rows, acc[gname][...])
                else:
                    for q in range(N_CHIPS):
                        pl.when(chip == q)(lambda i=i, rows=rows, gname=gname, q=q, per_chip=per_chip:
                                           update(i, rows, acc[gname][:, q * per_chip:(q + 1) * per_chip]))

    operands = [own[nm] for nm in names] + [landed[nm] for nm in names]
    for nm, _, _, _ in SMALL_WEIGHTS:
        operands += list(weights[nm])
    out_shape = [jax.ShapeDtypeStruct((1, 1), F32)]
    for _, shape, _, _ in SMALL_WEIGHTS:
        out_shape += [jax.ShapeDtypeStruct(shape, F32)] * 4
    res = pl.pallas_call(
        body, name="small_update", grid=(1,),
        in_specs=[_full_spec(a.shape) for a in operands], out_specs=[_full_spec(s.shape) for s in out_shape],
        out_shape=out_shape, scratch_shapes=[pltpu.VMEM(own[nm].shape, F32) for nm in names],
        compiler_params=_params(32, 1),
    )(*[_in_hbm(a) for a in operands])
    return res[0], {nm: res[1 + 4 * i:5 + 4 * i] for i, (nm, _, _, _) in enumerate(SMALL_WEIGHTS)}


def _adamw(w, m, v, grads, name, riders=()):
    layers, r, cdim = w.shape
    br = _block_rows(r, 256 if cdim > LANES else 1024)

    def body(*refs):
        w_ref, m_ref, v_ref = refs[:3]
        g_refs = refs[3:3 + layers]
        go_ref, d_ref, mo_ref, vo_ref = refs[3 + layers:]
        layer = pl.program_id(0)
        for l in range(layers):
            @pl.when(layer == l)
            def _(l=l):
                g = g_refs[l][...]
                go_ref[...] = g
                d_ref[...], mo_ref[...], vo_ref[...] = _adam_math(w_ref[...], m_ref[...], v_ref[...], g)

    spec3 = pl.BlockSpec((None, br, cdim), lambda l, i: (l, i, 0))
    spec2 = pl.BlockSpec((br, cdim), lambda l, i: (i, 0))
    out = jax.ShapeDtypeStruct((layers, r, cdim), F32)
    return _pallas(body, [w, m, v, *[_in_hbm(g) for g in grads]], name=name, grid=(layers, r // br),
                   in_specs=[spec3, spec3, spec3] + [spec2] * layers, out_specs=[spec3] * 4, out_shape=[out] * 4,
                   vmem_mib=32, riders=riders)


def _fill_shifted(buf, rows):
    for b in range(1, SUBLANES):
        buf[b, 0:rows - SUBLANES, :] = buf[0, b:b + rows - SUBLANES, :]


def _window(buf, start, size):
    return buf[start % SUBLANES, start - start % SUBLANES:start - start % SUBLANES + size, :]


def _conv31(src, w_ref, r0, base, init):
    acc = init
    for k in range(A_CONV_WIDTH):
        acc = acc + w_ref[k:k + 1, :] * _window(src, base + k + r0, CONV_ROWS)
    return acc


def _fwd_even(x, norm_g, w_in, conv_a_w, conv_a_b, ln_g, ln_b, conv_b_w, w_out, *, tm, seq, riders=()):
    tokens = x.shape[0]
    nt, tps = tokens // tm, seq // tm

    def body(x_ref, g_ref, win_hbm, caw_ref, cab_ref, lng_ref, lnb_ref, cbw_ref, wout_hbm,
             h_ref, n_ref, z_ref, a2_ref, cv_ref, mix_ref, win_v, wout_v, pa, pb, sem):
        i = pl.program_id(0)

        _load_weights([(win_hbm, win_v), (wout_hbm, wout_v)], sem)

        xv = x_ref[...]
        nf, _ = _rms_fwd(xv, g_ref[...])
        n = nf.astype(BF16)
        n_ref[...] = n
        z = jnp.concatenate([_dot(n, win_v[j]) for j in range(N_CHIPS)], axis=1)
        z_ref[...] = z.astype(BF16)
        a_val, a_gate = z[:, 0:A_DIM], z[:, A_DIM:2 * A_DIM]
        b_gate, c_gate, b_val = z[:, 1024:1536], z[:, 1536:2048], z[:, 2048:2560]

        first = (i % tps) == 0

        @pl.when(first)
        def _():
            pa[0, 0:A_HALO, :] = jnp.zeros((A_HALO, A_DIM), F32)
            pb[0:B_HALO, :] = jnp.zeros((B_HALO, B_DIM), F32)

        @pl.when(jnp.logical_not(first))
        def _():
            pa[0, 0:A_HALO, :] = pa[0, tm:tm + A_HALO, :]
            pb[0:B_HALO, :] = pb[tm:tm + B_HALO, :]

        pa[0, A_HALO:A_HALO + tm, :] = a_val * jax.nn.sigmoid(a_gate)
        pb[B_HALO:B_HALO + tm, :] = c_gate * b_val
        _fill_shifted(pa, A_HALO + tm)
        bias = jnp.broadcast_to(cab_ref[...], (CONV_ROWS, A_DIM))
        for r0 in range(0, tm, CONV_ROWS):
            a2_ref[r0:r0 + CONV_ROWS, :] = _conv31(pa, caw_ref, r0, A_HALO - (A_CONV_WIDTH - 1), bias)
        xhat, _ = _ln_stats(a2_ref[...])
        a3 = xhat * lng_ref[...] + lnb_ref[...]
        a4 = a3 * jax.nn.sigmoid(a3)
        cv = cbw_ref[0:1, :] * pb[B_HALO - 2:B_HALO - 2 + tm, :]
        cv = cv + cbw_ref[1:2, :] * pb[B_HALO - 1:B_HALO - 1 + tm, :]
        cv = cv + cbw_ref[2:3, :] * pb[B_HALO:B_HALO + tm, :]
        cv_ref[...] = cv.astype(BF16)
        mix = jnp.concatenate([a4, b_gate * cv], axis=1).astype(BF16)
        mix_ref[...] = mix
        h_ref[...] = xv + _dot(mix, wout_v[...])

    shp = lambda cols, dt: jax.ShapeDtypeStruct((tokens, cols), dt)
    return _pallas(
        body, [x, norm_g, w_in, conv_a_w, conv_a_b, ln_g, ln_b, conv_b_w, w_out], name="fwd_even", grid=(nt,),
        in_specs=[_row_spec(tm, D_MODEL), _full_spec((1, D_MODEL)), ANY, _full_spec((A_CONV_WIDTH, A_DIM)),
                  _full_spec((1, A_DIM)), _full_spec((1, A_DIM)), _full_spec((1, A_DIM)),
                  _full_spec((B_CONV_WIDTH, B_DIM)), ANY],
        out_specs=[_row_spec(tm, D_MODEL), _row_spec(tm, D_MODEL), _row_spec(tm, IN_EVEN), _row_spec(tm, A_DIM),
                   _row_spec(tm, B_DIM), _row_spec(tm, D_MODEL)],
        out_shape=[shp(D_MODEL, F32), shp(D_MODEL, BF16), shp(IN_EVEN, BF16), shp(A_DIM, F32), shp(B_DIM, BF16),
                   shp(D_MODEL, BF16)],
        scratch_shapes=[pltpu.VMEM((N_CHIPS, D_MODEL, IN_EVEN // N_CHIPS), BF16), pltpu.VMEM((D_MODEL, D_MODEL), BF16),
                        pltpu.VMEM((SUBLANES, A_HALO + tm, A_DIM), F32), pltpu.VMEM((B_HALO + tm, B_DIM), F32),
                        pltpu.SemaphoreType.DMA((N_LOADS,))],
        vmem_mib=56, riders=riders)


def _loss_tail(xv, g, target, loss_ref, dh_ref, dhb_ref, dg_ref):
    @pl.when(pl.program_id(0) == 0)
    def _():
        loss_ref[...] = jnp.zeros((1, 1), F32)
        dg_ref[...] = jnp.zeros((1, D_MODEL), F32)

    out, rstd = _rms_fwd(xv, g)
    err = out - target
    per_token = jnp.sum(err * err, axis=1, keepdims=True) * (1.0 / D_MODEL)
    loss_ref[...] += 0.5 * jnp.sum(per_token, axis=0, keepdims=True)
    dx, dg = _rms_bwd(err * (1.0 / D_MODEL), xv, rstd, g)
    dh_ref[...] = dx
    dhb_ref[...] = dx.astype(BF16)
    dg_ref[...] += dg


def _fwd_mlp(h, norm_g, w1, w2, layer, *, tm, riders=(), head=None):
    tokens = h.shape[0]
    nt = tokens // tm
    fs = D_FF // N_CHIPS
    n_in = 4 if head is None else 6

    def body(*refs):
        h_ref, g_ref, w1_hbm, w2_hbm = refs[:4]
        w1_v, w2_v, sem = refs[-3:]
        outs = refs[n_in:-3]
        n_ref, p_ref, q_ref = outs[1:4] if head is None else outs[0:3]
        _load_weights([(w1_hbm, w1_v), (w2_hbm, w2_v)], sem)

        xv = h_ref[...]
        nf, _ = _rms_fwd(xv, g_ref[...])
        n = nf.astype(BF16)
        n_ref[...] = n
        acc = xv
        for j in range(N_CHIPS):
            p = _dot(n, w1_v[j])
            p_ref[:, j * fs:(j + 1) * fs] = p.astype(BF16)
            r = jnp.maximum(p, 0.0)
            q = (r * r).astype(BF16)
            q_ref[:, j * fs:(j + 1) * fs] = q
            acc = acc + _dot(q, w2_v[j])
        if head is None:
            outs[0][...] = acc
        else:
            _loss_tail(acc, refs[4][...], refs[5][...], *outs[3:7])

    shp = lambda cols, dt: jax.ShapeDtypeStruct((tokens, cols), dt)
    saved_specs = [_row_spec(tm, D_MODEL), _row_spec(tm, D_FF), _row_spec(tm, D_FF)]
    saved_shapes = [shp(D_MODEL, BF16), shp(D_FF, BF16), shp(D_FF, BF16)]
    if head is None:
        operands, in_specs = [h, norm_g, w1, w2], [_row_spec(tm, D_MODEL), _full_spec((1, D_MODEL)), ANY, ANY]
        out_specs, out_shape = [_row_spec(tm, D_MODEL)] + saved_specs, [shp(D_MODEL, F32)] + saved_shapes
    else:
        operands = [h, norm_g, w1, w2, *head]
        in_specs = [_row_spec(tm, D_MODEL), _full_spec((1, D_MODEL)), ANY, ANY, _full_spec((1, D_MODEL)), _row_spec(tm, D_MODEL)]
        out_specs = saved_specs + [_full_spec((1, 1)), _row_spec(tm, D_MODEL), _row_spec(tm, D_MODEL), _full_spec((1, D_MODEL))]
        out_shape = saved_shapes + [jax.ShapeDtypeStruct((1, 1), F32), shp(D_MODEL, F32), shp(D_MODEL, BF16),
                                    jax.ShapeDtypeStruct((1, D_MODEL), F32)]
    return _pallas(
        body, operands, name=f"fwd_mlp{layer}", grid=(nt,), in_specs=in_specs, out_specs=out_specs, out_shape=out_shape,
        scratch_shapes=[pltpu.VMEM((N_CHIPS, D_MODEL, fs), BF16), pltpu.VMEM((N_CHIPS, fs, D_MODEL), BF16),
                        pltpu.SemaphoreType.DMA((N_LOADS,))],
        vmem_mib=56, riders=riders)


def _tril_mask():
    row = lax.broadcasted_iota(jnp.int32, (CHUNK, CHUNK), 0)
    col = lax.broadcasted_iota(jnp.int32, (CHUNK, CHUNK), 1)
    return row >= col


def _triu_mask():
    row = lax.broadcasted_iota(jnp.int32, (CHUNK, CHUNK), 0)
    col = lax.broadcasted_iota(jnp.int32, (CHUNK, CHUNK), 1)
    return row <= col


def _fwd_odd(h, norm_g, w_in, b_in, ln_g, ln_b, w_s, b_s_rows, w_out, *, tm, riders=()):
    tokens = h.shape[0]
    nt = tokens // tm
    cs = 2 * C_DIM // N_CHIPS

    def body(h_ref, g_ref, win_hbm, bin_ref, lng_ref, lnb_ref, ws_ref, bs_ref, wout_hbm,
             ho_ref, n_ref, s_ref, cdf_ref, sv_ref, y_ref, win_v, wout_v, bd, sem):
        _load_weights([(win_hbm, win_v), (wout_hbm, wout_v)], sem)

        @pl.when(pl.program_id(0) == 0)
        def _():
            mask = _tril_mask()
            bd[...] = jnp.zeros(bd.shape, BF16)
            for g in range(C_GROUPS):
                w = jnp.where(mask, ws_ref[g], 0.0).astype(BF16)
                bd[g, 0:CHUNK, 0:CHUNK] = w
                bd[g, CHUNK:PAIR, CHUNK:PAIR] = w

        xv = h_ref[...]
        nf, _ = _rms_fwd(xv, g_ref[...])
        n = nf.astype(BF16)
        n_ref[...] = n
        s = jnp.concatenate([_dot(n, win_v[j]) for j in range(N_CHIPS)], axis=1) + bin_ref[...]
        s_ref[...] = s.astype(BF16)
        cdf = _gelu_cdf(s)
        cdf_ref[...] = cdf.astype(BF16)
        zz = s * cdf
        u, v = zz[:, 0:C_DIM], zz[:, C_DIM:2 * C_DIM]
        xhat, _ = _ln_stats(v)
        vn = (xhat * lng_ref[...] + lnb_ref[...]).astype(BF16)
        for g in range(C_GROUPS):
            cols = slice(g * CHUNK, (g + 1) * CHUNK)
            bias = jnp.concatenate([bs_ref[g], bs_ref[g]], axis=0)
            for r0 in range(0, tm, PAIR):
                sv = _dot(bd[g], vn[r0:r0 + PAIR, cols]) + bias
                sv_ref[r0:r0 + PAIR, cols] = sv.astype(BF16)
                y_ref[r0:r0 + PAIR, cols] = (u[r0:r0 + PAIR, cols] * sv).astype(BF16)
        ho_ref[...] = xv + _dot(y_ref[...], wout_v[...])

    shp = lambda cols, dt: jax.ShapeDtypeStruct((tokens, cols), dt)
    return _pallas(
        body, [h, norm_g, w_in, b_in, ln_g, ln_b, w_s, b_s_rows, w_out], name="fwd_odd", grid=(nt,),
        in_specs=[_row_spec(tm, D_MODEL), _full_spec((1, D_MODEL)), ANY, _full_spec((1, 2 * C_DIM)),
                  _full_spec((1, C_DIM)), _full_spec((1, C_DIM)), _full_spec((C_GROUPS, CHUNK, CHUNK)),
                  _full_spec((C_GROUPS, CHUNK, CHUNK)), ANY],
        out_specs=[_row_spec(tm, D_MODEL), _row_spec(tm, D_MODEL), _row_spec(tm, 2 * C_DIM), _row_spec(tm, 2 * C_DIM),
                   _row_spec(tm, C_DIM), _row_spec(tm, C_DIM)],
        out_shape=[shp(D_MODEL, F32), shp(D_MODEL, BF16), shp(2 * C_DIM, BF16), shp(2 * C_DIM, BF16), shp(C_DIM, BF16),
                   shp(C_DIM, BF16)],
        scratch_shapes=[pltpu.VMEM((N_CHIPS, D_MODEL, cs), BF16), pltpu.VMEM((C_DIM, D_MODEL), BF16),
                        pltpu.VMEM((C_GROUPS, PAIR, PAIR), BF16), pltpu.SemaphoreType.DMA((N_LOADS,))],
        vmem_mib=56, riders=riders)


def _bwd_mlp(dh, h, norm_g, p, w1, w2, layer, *, tm, riders=()):
    tokens = h.shape[0]
    nt = tokens // tm
    fs = D_FF // N_CHIPS

    def body(dh_ref, h_ref, g_ref, p_ref, w1_hbm, w2_hbm, dx_ref, dxb_ref, dp_ref, dg_ref, w1_v, w2_v, sem):
        @pl.when(pl.program_id(0) == 0)
        def _():
            dg_ref[...] = jnp.zeros((1, D_MODEL), F32)

        _load_weights([(w1_hbm, w1_v), (w2_hbm, w2_v)], sem)

        dhv = dh_ref[...]
        dhb = dhv.astype(BF16)
        dn = jnp.zeros((tm, D_MODEL), F32)
        for j in range(N_CHIPS):
            dq = _dot_nt(dhb, w2_v[j])
            r = jnp.maximum(p_ref[:, j * fs:(j + 1) * fs].astype(F32), 0.0)
            dp = ((2.0 * r) * dq).astype(BF16)
            dp_ref[:, j * fs:(j + 1) * fs] = dp
            dn = dn + _dot_nt(dp, w1_v[j])
        xv = h_ref[...]
        g = g_ref[...]
        _, rstd = _rms_fwd(xv, g)
        dx, dg = _rms_bwd(dn, xv, rstd, g)
        dx_ref[...] = dhv + dx
        dxb_ref[...] = (dhv + dx).astype(BF16)
        dg_ref[...] += dg

    return _pallas(
        body, [dh, h, norm_g, p, w1, w2], name=f"bwd_mlp{layer}", grid=(nt,),
        in_specs=[_row_spec(tm, D_MODEL), _row_spec(tm, D_MODEL), _full_spec((1, D_MODEL)), _row_spec(tm, D_FF), ANY, ANY],
        out_specs=[_row_spec(tm, D_MODEL), _row_spec(tm, D_MODEL), _row_spec(tm, D_FF), _full_spec((1, D_MODEL))],
        out_shape=[jax.ShapeDtypeStruct((tokens, D_MODEL), F32), jax.ShapeDtypeStruct((tokens, D_MODEL), BF16),
                   jax.ShapeDtypeStruct((tokens, D_FF), BF16), jax.ShapeDtypeStruct((1, D_MODEL), F32)],
        scratch_shapes=[pltpu.VMEM((N_CHIPS, D_MODEL, fs), BF16), pltpu.VMEM((N_CHIPS, fs, D_MODEL), BF16),
                        pltpu.SemaphoreType.DMA((N_LOADS,))],
        vmem_mib=56, riders=riders)


def _bwd_odd(dh, h, norm_g, s, cdf, sv, w_in, ln_g, ln_b, w_s, w_out, *, tm, riders=()):
    tokens = h.shape[0]
    nt = tokens // tm
    cs = 2 * C_DIM // N_CHIPS

    def body(dh_ref, h_ref, g_ref, s_ref, cdf_ref, sv_ref, win_hbm, lng_ref, lnb_ref, ws_ref, wout_hbm,
             dx_ref, dxb_ref, ds_ref, dg_ref, dbin_ref, dlng_ref, dlnb_ref, dws_ref, dbs_ref,
             win_v, wout_v, bdt, dws_acc, dbs_acc, dvn, sem):
        i = pl.program_id(0)

        _load_weights([(win_hbm, win_v), (wout_hbm, wout_v)], sem)

        @pl.when(i == 0)
        def _():
            mask_t = _triu_mask()
            bdt[...] = jnp.zeros(bdt.shape, BF16)
            for g in range(C_GROUPS):
                wt = jnp.where(mask_t, ws_ref[g].T, 0.0).astype(BF16)
                bdt[g, 0:CHUNK, 0:CHUNK] = wt
                bdt[g, CHUNK:PAIR, CHUNK:PAIR] = wt
            dws_acc[...] = jnp.zeros(dws_acc.shape, F32)
            dbs_acc[...] = jnp.zeros(dbs_acc.shape, F32)
            dg_ref[...] = jnp.zeros(dg_ref.shape, F32)
            dbin_ref[...] = jnp.zeros(dbin_ref.shape, F32)
            dlng_ref[...] = jnp.zeros(dlng_ref.shape, F32)
            dlnb_ref[...] = jnp.zeros(dlnb_ref.shape, F32)

        dhv = dh_ref[...]
        dy = _dot_nt(dhv.astype(BF16), wout_v[...])
        sf = s_ref[...].astype(F32)
        cdf = cdf_ref[...].astype(F32)
        pdf = jnp.exp(-0.5 * sf * sf) * 0.3989422804014327
        zz = sf * cdf
        dgelu = cdf + sf * pdf
        u, v = zz[:, 0:C_DIM], zz[:, C_DIM:2 * C_DIM]
        xhat, rs = _ln_stats(v)
        lng = lng_ref[...]
        vn = (xhat * lng + lnb_ref[...]).astype(BF16)
        du = dy * sv_ref[...].astype(F32)
        dsv = dy * u
        dsvb = dsv.astype(BF16)
        for g in range(C_GROUPS):
            cols = slice(g * CHUNK, (g + 1) * CHUNK)
            for r0 in range(0, tm, PAIR):
                blk = dsvb[r0:r0 + PAIR, cols]
                dvn[r0:r0 + PAIR, cols] = _dot(bdt[g], blk)
                dws_acc[g] += _dot_nt(blk, vn[r0:r0 + PAIR, cols])
                dbs_acc[g] += dsv[r0:r0 + CHUNK, cols] + dsv[r0 + CHUNK:r0 + PAIR, cols]
        dv, dlng, dlnb = _ln_bwd(dvn[...], xhat, rs, lng)
        dlng_ref[...] += dlng
        dlnb_ref[...] += dlnb
        ds = jnp.concatenate([du, dv], axis=1) * dgelu
        dbin_ref[...] += jnp.sum(ds, axis=0, keepdims=True)
        dsb = ds.astype(BF16)
        ds_ref[...] = dsb
        dn = jnp.zeros((tm, D_MODEL), F32)
        for j in range(N_CHIPS):
            dn = dn + _dot_nt(dsb[:, j * cs:(j + 1) * cs], win_v[j])
        xv = h_ref[...]
        g = g_ref[...]
        _, rstd = _rms_fwd(xv, g)
        dx, dg = _rms_bwd(dn, xv, rstd, g)
        dx_ref[...] = dhv + dx
        dxb_ref[...] = (dhv + dx).astype(BF16)
        dg_ref[...] += dg

        @pl.when(i == nt - 1)
        def _():
            mask = _tril_mask()
            for g in range(C_GROUPS):
                full = dws_acc[g]
                dws_ref[g] = jnp.where(mask, full[0:CHUNK, 0:CHUNK] + full[CHUNK:PAIR, CHUNK:PAIR], 0.0)
                dbs_ref[g:g + 1, :] = jnp.sum(dbs_acc[g].T, axis=0, keepdims=True)

    row = lambda cols: jax.ShapeDtypeStruct((1, cols), F32)
    return _pallas(
        body, [dh, h, norm_g, s, cdf, sv, w_in, ln_g, ln_b, w_s, w_out], name="bwd_odd", grid=(nt,),
        in_specs=[_row_spec(tm, D_MODEL), _row_spec(tm, D_MODEL), _full_spec((1, D_MODEL)), _row_spec(tm, 2 * C_DIM),
                  _row_spec(tm, 2 * C_DIM), _row_spec(tm, C_DIM), ANY, _full_spec((1, C_DIM)), _full_spec((1, C_DIM)),
                  _full_spec((C_GROUPS, CHUNK, CHUNK)), ANY],
        out_specs=[_row_spec(tm, D_MODEL), _row_spec(tm, D_MODEL), _row_spec(tm, 2 * C_DIM), _full_spec((1, D_MODEL)),
                   _full_spec((1, 2 * C_DIM)),
                   _full_spec((1, C_DIM)), _full_spec((1, C_DIM)), _full_spec((C_GROUPS, CHUNK, CHUNK)),
                   _full_spec((C_GROUPS, CHUNK))],
        out_shape=[jax.ShapeDtypeStruct((tokens, D_MODEL), F32), jax.ShapeDtypeStruct((tokens, D_MODEL), BF16),
                   jax.ShapeDtypeStruct((tokens, 2 * C_DIM), BF16),
                   row(D_MODEL), row(2 * C_DIM), row(C_DIM), row(C_DIM),
                   jax.ShapeDtypeStruct((C_GROUPS, CHUNK, CHUNK), F32), jax.ShapeDtypeStruct((C_GROUPS, CHUNK), F32)],
        scratch_shapes=[pltpu.VMEM((N_CHIPS, D_MODEL, cs), BF16), pltpu.VMEM((C_DIM, D_MODEL), BF16),
                        pltpu.VMEM((C_GROUPS, PAIR, PAIR), BF16), pltpu.VMEM((C_GROUPS, PAIR, PAIR), F32),
                        pltpu.VMEM((C_GROUPS, CHUNK, CHUNK), F32), pltpu.VMEM((tm, C_DIM), F32),
                        pltpu.SemaphoreType.DMA((N_LOADS,))],
        vmem_mib=56, riders=riders)


def _bwd_even(dh, x, norm_g, z, a2, cv, w_in, conv_a_w, ln_g, ln_b, conv_b_w, w_out, *, tm, seq, riders=()):
    tokens = x.shape[0]
    nt, tps = tokens // tm, seq // tm
    ws = IN_EVEN // N_CHIPS

    def body(dh_ref, x_ref, g_ref, z_ref, a2_ref, cv_ref, win_hbm, caw_ref, lng_ref, lnb_ref, cbw_ref, wout_hbm,
             dx_ref, dz_ref, dg_ref, dcaw_ref, dcab_ref, dlng_ref, dlnb_ref, dcbw_ref,
             win_v, wout_v, ea, eb, a1s, da1s, dw_acc, sem):
        i = pl.program_id(0)

        _load_weights([(win_hbm, win_v), (wout_hbm, wout_v)], sem)

        @pl.when(i == 0)
        def _():
            dw_acc[...] = jnp.zeros(dw_acc.shape, F32)
            for ref in (dg_ref, dcab_ref, dlng_ref, dlnb_ref, dcbw_ref):
                ref[...] = jnp.zeros(ref.shape, F32)

        dhv = dh_ref[...]
        dmix = _dot_nt(dhv.astype(BF16), wout_v[...])
        da4, dbo = dmix[:, 0:A_DIM], dmix[:, A_DIM:A_DIM + B_DIM]
        zf = z_ref[...].astype(F32)
        a_val, a_gate = zf[:, 0:A_DIM], zf[:, A_DIM:2 * A_DIM]
        b_gate, c_gate, b_val = zf[:, 1024:1536], zf[:, 1536:2048], zf[:, 2048:2560]

        xhat, rs = _ln_stats(a2_ref[...])
        lng = lng_ref[...]
        a3 = xhat * lng + lnb_ref[...]
        sg = jax.nn.sigmoid(a3)
        da3 = da4 * (sg * (1.0 + a3 * (1.0 - sg)))
        da2, dlng, dlnb = _ln_bwd(da3, xhat, rs, lng)
        dlng_ref[...] += dlng
        dlnb_ref[...] += dlnb
        dcab_ref[...] += jnp.sum(da2, axis=0, keepdims=True)

        last = ((nt - 1 - i) % tps) == tps - 1
        dcv = dbo * b_gate

        @pl.when(last)
        def _():
            ea[0, tm:tm + A_HALO, :] = jnp.zeros((A_HALO, A_DIM), F32)
            eb[tm:tm + B_HALO, :] = jnp.zeros((B_HALO, B_DIM), F32)

        @pl.when(jnp.logical_not(last))
        def _():
            ea[0, tm:tm + A_HALO, :] = ea[0, 0:A_HALO, :]
            eb[tm:tm + B_HALO, :] = eb[0:B_HALO, :]

        ea[0, 0:tm, :] = da2
        eb[0:tm, :] = dcv
        _fill_shifted(ea, tm + A_HALO)
        sig = jax.nn.sigmoid(a_gate)
        a1s[...] = a_val * sig
        for r0 in range(0, tm, CONV_ROWS):
            acc = jnp.zeros((CONV_ROWS, A_DIM), F32)
            for j in range(A_CONV_WIDTH):
                acc = acc + caw_ref[A_CONV_WIDTH - 1 - j:A_CONV_WIDTH - j, :] * _window(ea, r0 + j, CONV_ROWS)
            da1s[r0:r0 + CONV_ROWS, :] = acc
        for j0 in range(0, A_CONV_WIDTH, DW_TAPS):
            taps = range(j0, min(j0 + DW_TAPS, A_CONV_WIDTH))
            part = [jnp.zeros((CONV_ROWS, A_DIM), F32) for _ in taps]
            for r0 in range(0, tm, CONV_ROWS):
                a1c = a1s[r0:r0 + CONV_ROWS, :]
                for u, j in enumerate(taps):
                    part[u] = part[u] + _window(ea, r0 + j, CONV_ROWS) * a1c
            for u, j in enumerate(taps):
                dw_acc[A_CONV_WIDTH - 1 - j] += part[u]
        da1 = da1s[...]
        da_val = da1 * sig
        da_gate = da1 * a_val * (sig * (1.0 - sig))

        db_gate = dbo * cv_ref[...].astype(F32)
        cb = c_gate * b_val
        dcb = jnp.zeros((tm, B_DIM), F32)
        for j in range(B_CONV_WIDTH):
            k = B_CONV_WIDTH - 1 - j
            sl = eb[j:j + tm, :]
            dcb = dcb + cbw_ref[k:k + 1, :] * sl
            dcbw_ref[k:k + 1, :] += jnp.sum(sl * cb, axis=0, keepdims=True)
        dz = jnp.concatenate([da_val, da_gate, db_gate, dcb * b_val, dcb * c_gate], axis=1).astype(BF16)
        dz_ref[...] = dz
        dn = jnp.zeros((tm, D_MODEL), F32)
        for j in range(N_CHIPS):
            dn = dn + _dot_nt(dz[:, j * ws:(j + 1) * ws], win_v[j])
        xv = x_ref[...]
        g = g_ref[...]
        _, rstd = _rms_fwd(xv, g)
        dx, dg = _rms_bwd(dn, xv, rstd, g)
        dx_ref[...] = dhv + dx
        dg_ref[...] += dg

        @pl.when(i == nt - 1)
        def _():
            for k in range(A_CONV_WIDTH):
                dcaw_ref[k:k + 1, :] = jnp.sum(dw_acc[k], axis=0, keepdims=True)

    row = lambda cols: jax.ShapeDtypeStruct((1, cols), F32)
    rs_ = functools.partial(_row_spec, rev_nt=nt)
    return _pallas(
        body, [dh, x, norm_g, z, a2, cv, w_in, conv_a_w, ln_g, ln_b, conv_b_w, w_out], name="bwd_even", grid=(nt,),
        in_specs=[rs_(tm, D_MODEL), rs_(tm, D_MODEL), _full_spec((1, D_MODEL)), rs_(tm, IN_EVEN), rs_(tm, A_DIM),
                  rs_(tm, B_DIM), ANY, _full_spec((A_CONV_WIDTH, A_DIM)), _full_spec((1, A_DIM)), _full_spec((1, A_DIM)),
                  _full_spec((B_CONV_WIDTH, B_DIM)), ANY],
        out_specs=[rs_(tm, D_MODEL), rs_(tm, IN_EVEN), _full_spec((1, D_MODEL)), _full_spec((A_CONV_WIDTH, A_DIM)),
                   _full_spec((1, A_DIM)), _full_spec((1, A_DIM)), _full_spec((1, A_DIM)), _full_spec((B_CONV_WIDTH, B_DIM))],
        out_shape=[jax.ShapeDtypeStruct((tokens, D_MODEL), F32), jax.ShapeDtypeStruct((tokens, IN_EVEN), BF16),
                   row(D_MODEL), jax.ShapeDtypeStruct((A_CONV_WIDTH, A_DIM), F32), row(A_DIM), row(A_DIM), row(A_DIM),
                   jax.ShapeDtypeStruct((B_CONV_WIDTH, B_DIM), F32)],
        scratch_shapes=[pltpu.VMEM((N_CHIPS, D_MODEL, ws), BF16), pltpu.VMEM((D_MODEL, D_MODEL), BF16),
                        pltpu.VMEM((SUBLANES, tm + A_HALO, A_DIM), F32), pltpu.VMEM((tm + B_HALO, B_DIM), F32),
                        pltpu.VMEM((tm, A_DIM), F32), pltpu.VMEM((tm, A_DIM), F32),
                        pltpu.VMEM((A_CONV_WIDTH, CONV_ROWS, A_DIM), F32), pltpu.SemaphoreType.DMA((N_LOADS,))],
        vmem_mib=56, riders=riders)


def _wgrad(a, b, name, *, col_shards, riders=()):
    tokens, m = a.shape
    n = b.shape[1]
    kc = 512
    if col_shards:
        bm, bn = m // 2, n // N_CHIPS
        grid = (2, N_CHIPS)
        out_spec = pl.BlockSpec((None, None, bm, bn), lambda i, j: (j, i, 0, 0))
    elif m // 8 >= MXU_ROWS:
        bm, bn = m // 8, n
        grid = (8, 1)
        out_spec = pl.BlockSpec((None, None, bm, bn), lambda i, j: (i // 2, i % 2, 0, 0))
    else:
        bm, bn = m // N_CHIPS, n
        grid = (N_CHIPS, 1)
        out_spec = pl.BlockSpec((None, 2, bm // 2, bn), lambda i, j: (i, 0, 0, 0))

    def body(a_ref, b_ref, o_ref):
        acc = jnp.zeros((bm, bn), F32)
        for k0 in range(0, tokens, kc):
            acc = acc + _dot_tn(a_ref[k0:k0 + kc, :].astype(BF16), b_ref[k0:k0 + kc, :].astype(BF16))
        if len(o_ref.shape) == 3:
            o_ref[0] = acc[0:bm // 2]
            o_ref[1] = acc[bm // 2:bm]
        else:
            o_ref[...] = acc

    out_rows = m // 2 if col_shards else m // 8
    outs, routs = _pallas(
        body, [a, b], name=name, grid=grid,
        in_specs=[pl.BlockSpec((tokens, bm), lambda i, j: (0, i)), pl.BlockSpec((tokens, bn), lambda i, j: (0, j))],
        out_specs=[out_spec], out_shape=[jax.ShapeDtypeStruct((N_CHIPS, 2, out_rows, bn), F32)],
        vmem_mib=56, riders=riders)
    return outs[0], routs


def _wgrad_pair(a, b, name, *, col_shards, riders=(), to_chips=False):
    tokens, m = a.shape
    n = b.shape[1]
    kc = 512
    x0, y0, c0 = _mesh_pos()
    rot = 1 if to_chips else 0

    def slab(q, pre):
        return (q + rot * (1 + pre[1])) % N_CHIPS

    if col_shards:
        bm, bn = m // 2, n // N_CHIPS
        a_spec = pl.BlockSpec((tokens, bm), lambda ph, q, pre: (0, (ph + 1 + pre[0]) % 2))
        b_spec = pl.BlockSpec((tokens, bn), lambda ph, q, pre: (0, slab(q, pre)))
    else:
        bm, bn = m // 8, n
        a_spec = pl.BlockSpec((tokens, bm), lambda ph, q, pre: (0, 2 * slab(q, pre) + (ph + 1 + pre[0]) % 2))
        b_spec = pl.BlockSpec((tokens, bn), lambda ph, q, pre: (0, 0))

    def body(pre_ref, a_ref, b_ref, o_ref, *rest):
        if to_chips:
            land, give, got, mine, send_sems, recv_sems, chip_send, chip_recv = rest
        else:
            give, got, send_sems, recv_sems = rest
        ph, q = pl.program_id(0), pl.program_id(1)
        acc = jnp.zeros((bm, bn), F32)
        for k0 in range(0, tokens, kc):
            acc = acc + _dot_tn(a_ref[k0:k0 + kc, :].astype(BF16), b_ref[k0:k0 + kc, :].astype(BF16))
        x, y, cc = _mesh_pos()

        def tile(t):
            return _remote(give.at[t], got.at[t], send_sems.at[t], recv_sems.at[t], (x, y, 1 - cc))

        def to_chip(s):
            t = (s + 1 + 2 * x + y) % N_CHIPS
            tx, ty = t // 2, t % 2
            k = 2 * (ty ^ y) + (tx ^ x) - 1
            return _remote(mine.at[s], land.at[k], chip_send.at[k], chip_recv.at[k], (tx, ty, cc))

        @pl.when(ph == 0)
        def _():
            give[q] = acc
            tile(q).start()

        @pl.when(ph == 1)
        def _():
            tile(q).wait_recv()
            total = (acc + got[q]).astype(BF16)
            o_ref[...] = total
            if to_chips:
                for s in range(N_CHIPS - 1):
                    @pl.when(q == s)
                    def _(s=s):
                        mine[s] = total
                        to_chip(s).start()

        @pl.when((ph == 1) & (q == N_CHIPS - 1))
        def _():
            for t in range(N_CHIPS):
                tile(t).wait_send()
            if to_chips:
                for s in range(N_CHIPS - 1):
                    to_chip(s).wait()

    prefetch = jnp.stack([c0, 2 * x0 + y0]).astype(jnp.int32)
    out_specs = [pl.BlockSpec((None, bm, bn), lambda ph, q, pre: (slab(ph * q, pre), 0, 0))]
    out_shape = [jax.ShapeDtypeStruct((N_CHIPS, bm, bn), BF16)]
    scratch = [pltpu.VMEM((N_CHIPS, bm, bn), F32), pltpu.VMEM((N_CHIPS, bm, bn), F32)]
    sems = [pltpu.SemaphoreType.DMA((N_CHIPS,)), pltpu.SemaphoreType.DMA((N_CHIPS,))]
    if to_chips:
        out_specs.append(ANY)
        out_shape.append(jax.ShapeDtypeStruct((N_CHIPS - 1, bm, bn), BF16))
        scratch.append(pltpu.VMEM((N_CHIPS - 1, bm, bn), BF16))
        sems += [pltpu.SemaphoreType.DMA((N_CHIPS - 1,)), pltpu.SemaphoreType.DMA((N_CHIPS - 1,))]
    outs, routs = _pallas(
        body, [a, b], name=name, grid=(2, N_CHIPS), in_specs=[a_spec, b_spec], out_specs=out_specs, out_shape=out_shape,
        scratch_shapes=scratch + sems, vmem_mib=56, riders=riders, prefetch=prefetch)
    return (outs if to_chips else outs[0]), routs


class _GradReduce:
    def __init__(self, name, grad=None, chip_sum=None):
        self.name, self.grad, self.chip_sum = name, grad, chip_sum
        self.full = None

    def pair_swap(self):
        return _PairSwap([self.grad])

    def took_pair(self, outs):
        self.chip_sum = _in_hbm(_add_pair(self.grad, outs[0], f"pair_sum_{self.name}"))

    def chip_swap(self):
        return _ChipSwap([self.chip_sum])

    def took_chips(self, outs):
        self.full = _in_hbm(_add_chips(self.chip_sum, outs[0], f"chip_sum_{self.name}"))

    def pair_share(self):
        return _PairShare([self.full])

    def took_share(self, outs):
        self.full = outs[0]

    def reduced(self):
        return jnp.reshape(self.full, (2 * self.full.shape[1], self.full.shape[2]))


def _forward_backward(x2, tgt2, gathered, staged, conv_a_w, conv_b_w, od_norm, od_bias, od_lng, od_lnb,
                      ev_norm_g, ev_conv_a_b, ev_ln_a_g, ev_ln_a_b, od_w_s, od_b_s, mlp_norm_g, final_norm_g,
                      *, tm, seq, distributed=True):
    d = x2.shape[1]
    w = dict(gathered)
    b_s_rows = jnp.broadcast_to(od_b_s[0][:, :, None], (C_GROUPS, CHUNK, CHUNK))

    def ride(*names):
        return [_Gather([staged[nm] for nm in names])] if distributed else []

    def land(routs, *names):
        if distributed:
            for nm, buf in zip(names, routs[0]):
                w[nm] = buf

    def as_cols(buf):
        return jnp.reshape(buf, (N_CHIPS, 2 * buf.shape[2], buf.shape[3]))

    def as_rows(buf):
        return jnp.reshape(buf, (8 * buf.shape[2], buf.shape[3]))

    (h1, n0, z, a2, cv, mix), routs = _fwd_even(
        x2, ev_norm_g, as_cols(w["ev_in"]), conv_a_w, ev_conv_a_b, ev_ln_a_g, ev_ln_a_b, conv_b_w, as_rows(w["ev_out"]),
        tm=tm, seq=seq, riders=ride("w1_0", "w2_0"))
    land(routs, "w1_0", "w2_0")
    (h2, n1, p0, q0), routs = _fwd_mlp(h1, mlp_norm_g[0:1], as_cols(w["w1_0"]), as_cols(w["w2_0"]), 0, tm=tm,
                                       riders=ride("od_in", "od_out", "w1_1"))
    land(routs, "od_in", "od_out", "w1_1")
    (h3, n2, s, cdf, sv, y), routs = _fwd_odd(h2, od_norm, as_cols(w["od_in"]), od_bias, od_lng, od_lnb, od_w_s[0], b_s_rows,
                                         as_rows(w["od_out"]), tm=tm, riders=ride("w2_1"))
    land(routs, "w2_1")
    (n3, p1, q1, loss_part, dh4, dh4b, d_final_g), _ = _fwd_mlp(
        h3, mlp_norm_g[1:2], as_cols(w["w1_1"]), as_cols(w["w2_1"]), 1, tm=tm,
        head=(jnp.reshape(final_norm_g, (1, d)), tgt2))

    red = {}

    def swap(*names):
        return [red[nm].pair_swap() for nm in names] if distributed else []

    def chips(*names):
        return [red[nm].chip_swap() for nm in names] if distributed else []

    def share(*names):
        return [red[nm].pair_share() for nm in names] if distributed else []

    def took(routs, *steps):
        if distributed:
            for (nm, what), outs in zip(steps, routs):
                getattr(red[nm], what)(outs)

    def big(lhs, rhs, name, col_shards, riders=(), to_chips=False):
        if distributed and to_chips:
            (chip_sum, from_chips), routs = _wgrad_pair(lhs, rhs, f"wgrad_{name}", col_shards=col_shards, riders=riders,
                                                        to_chips=True)
            red[name] = _GradReduce(name, chip_sum=_in_hbm(chip_sum))
            red[name].took_chips([_in_hbm(from_chips)])
        elif distributed:
            chip_sum, routs = _wgrad_pair(lhs, rhs, f"wgrad_{name}", col_shards=col_shards, riders=riders)
            red[name] = _GradReduce(name, chip_sum=_in_hbm(chip_sum))
        else:
            g, routs = _wgrad(lhs, rhs, f"wgrad_{name}", col_shards=col_shards)
            red[name] = _GradReduce(name, grad=g)
        return routs

    big(q1, dh4b, "w2_1", False)
    (dh3, dh3b, dp1, d_mlp_g1), routs = _bwd_mlp(dh4, h3, mlp_norm_g[1:2], p1, as_cols(w["w1_1"]), as_cols(w["w2_1"]), 1, tm=tm,
                                           riders=chips("w2_1"))
    took(routs, ("w2_1", "took_chips"))
    big(n3, dp1, "w1_1", True)
    g, routs = _wgrad(y, dh3b, "wgrad_od_out", col_shards=False, riders=share("w2_1"))
    red["od_out"] = _GradReduce("od_out", grad=g)
    took(routs, ("w2_1", "took_share"))
    (dh2, dh2b, ds, d_od_norm, d_od_bin, d_od_lng, d_od_lnb, d_ws, d_bs), routs = _bwd_odd(
        dh3, h2, od_norm, s, cdf, sv, as_cols(w["od_in"]), od_lng, od_lnb, od_w_s[0], as_rows(w["od_out"]), tm=tm,
        riders=chips("w1_1") + swap("od_out"))
    took(routs, ("w1_1", "took_chips"), ("od_out", "took_pair"))
    routs = big(n2, ds, "od_in", True, riders=share("w1_1"))
    took(routs, ("w1_1", "took_share"))
    half_groups = C_GROUPS // 2
    early = {"loss": loss_part, "od_w_s_lo": d_ws[:half_groups], "od_b_s": d_bs, "mlp_norm_g1": d_mlp_g1, "final_norm_g": d_final_g,
             "od_norm_g": d_od_norm, "od_b_in": d_od_bin, "od_ln_v_g": d_od_lng, "od_ln_v_b": d_od_lnb}
    share_early = [_ShareAll(list(early.values()))] if distributed else []
    routs = big(q0, dh2b, "w2_0", False, riders=share_early)
    landed_early = routs[0] if distributed else []
    (dh1, dh1b, dp0, d_mlp_g0), routs = _bwd_mlp(dh2, h1, mlp_norm_g[0:1], p0, as_cols(w["w1_0"]), as_cols(w["w2_0"]), 0, tm=tm,
                                           riders=chips("od_out") + chips("od_in") + chips("w2_0"))
    took(routs, ("od_out", "took_chips"), ("od_in", "took_chips"), ("w2_0", "took_chips"))
    middle = {"od_w_s_hi": d_ws[half_groups:]}
    share_middle = [_ShareAll(list(middle.values()))] if distributed else []
    g, _ = _wgrad(mix, dh1b, "wgrad_ev_out", col_shards=False)
    red["ev_out"] = _GradReduce("ev_out", grad=g)
    routs = big(n1, dp0, "w1_0", True,
                riders=share("od_out") + share("od_in") + share("w2_0") + share_middle + swap("ev_out"))
    took(routs, ("od_out", "took_share"), ("od_in", "took_share"), ("w2_0", "took_share"))
    landed_middle = routs[3] if distributed else []
    if distributed:
        red["ev_out"].took_pair(routs[4])

    (dx, dz, d_ev_norm, d_caw, d_cab, d_ev_lng, d_ev_lnb, d_cbw), routs = _bwd_even(
        dh1, x2, ev_norm_g, z, a2, cv, as_cols(w["ev_in"]), conv_a_w, ev_ln_a_g, ev_ln_a_b, conv_b_w, as_rows(w["ev_out"]),
        tm=tm, seq=seq, riders=chips("w1_0") + chips("ev_out"))
    took(routs, ("w1_0", "took_chips"), ("ev_out", "took_chips"))
    late = {"mlp_norm_g0": d_mlp_g0, "ev_norm_g": d_ev_norm, "ev_conv_a_b": d_cab, "ev_ln_a_g": d_ev_lng,
            "ev_ln_a_b": d_ev_lnb, "ev_conv_a_w": d_caw, "ev_conv_b_w": d_cbw}
    share_late = [_ShareAll(list(late.values()))] if distributed else []
    routs2 = big(n0, dz, "ev_in", True, riders=share("ev_out") + share("w1_0") + share_late, to_chips=True)
    took(routs2, ("ev_out", "took_share"), ("w1_0", "took_share"))
    own = {**early, **middle, **late}
    landed = dict(zip(own.keys(), landed_early + landed_middle + routs2[2])) if distributed else None
    return dx, red, own, landed


def _rows128(a):
    rows = jnp.reshape(a, (-1, LANES))
    pad = (-rows.shape[0]) % SUBLANES
    return jnp.pad(rows, ((0, pad), (0, 0))) if pad else rows


def _pack(arrays):
    return jnp.concatenate([_rows128(a) for a in arrays], axis=0)


def _unpack(buf, shapes):
    out, r0 = [], 0
    for shp in shapes:
        size = 1
        for dim in shp:
            size *= dim
        nr = size // LANES
        out.append(jnp.reshape(buf[r0:r0 + nr], shp))
        r0 += nr + (-nr) % SUBLANES
    return out


def kernel(x, ev_norm_g, ev_w_in, ev_conv_a_w, ev_conv_a_b, ev_ln_a_g, ev_ln_a_b, ev_conv_b_w, ev_w_out, od_norm_g, od_w_in, od_b_in, od_ln_v_g, od_ln_v_b, od_w_s, od_b_s, od_w_out, mlp_norm_g, mlp_w1, mlp_w2, final_norm_g, loss_target, m_ev_norm_g, m_ev_w_in, m_ev_conv_a_w, m_ev_conv_a_b, m_ev_ln_a_g, m_ev_ln_a_b, m_ev_conv_b_w, m_ev_w_out, m_od_norm_g, m_od_w_in, m_od_b_in, m_od_ln_v_g, m_od_ln_v_b, m_od_w_s, m_od_b_s, m_od_w_out, m_mlp_norm_g, m_mlp_w1, m_mlp_w2, m_final_norm_g, v_ev_norm_g, v_ev_w_in, v_ev_conv_a_w, v_ev_conv_a_b, v_ev_ln_a_g, v_ev_ln_a_b, v_ev_conv_b_w, v_ev_w_out, v_od_norm_g, v_od_w_in, v_od_b_in, v_od_ln_v_g, v_od_ln_v_b, v_od_w_s, v_od_b_s, v_od_w_out, v_mlp_norm_g, v_mlp_w1, v_mlp_w2, v_final_norm_g):
    tm = TOKEN_TILE
    batch, seq, d = x.shape
    tokens = batch * seq
    x2 = jnp.reshape(x, (tokens, d))
    tgt2 = jnp.reshape(loss_target, (tokens, d))
    chip = 2 * lax.axis_index("x") + lax.axis_index("y")

    small_shapes = [(A_CONV_WIDTH, LANES), (B_CONV_WIDTH, LANES), (256,), (512,), (256,), (256,)]
    small_shard = _pack([ev_conv_a_w[0], ev_conv_b_w[0], od_norm_g[0], od_b_in[0], od_ln_v_g[0], od_ln_v_b[0]])
    small_shard = jnp.pad(small_shard, ((0, (-small_shard.shape[0]) % (2 * SUBLANES)), (0, 0)))
    first = [_place_shard(ev_w_in, 0, BF16, "place_ev_w_in"), _place_shard(ev_w_out, 0, BF16, "place_ev_w_out"),
             _place_shard(small_shard[None], 0, F32, "place_small")]
    staged = {
        "w1_0": _place_shard(mlp_w1, 0, BF16, "place_w1_0"), "w2_0": _place_shard(mlp_w2, 0, BF16, "place_w2_0"),
        "od_in": _place_shard(od_w_in, 0, BF16, "place_od_w_in"), "od_out": _place_shard(od_w_out, 0, BF16, "place_od_w_out"),
        "w1_1": _place_shard(mlp_w1, 1, BF16, "place_w1_1"), "w2_1": _place_shard(mlp_w2, 1, BF16, "place_w2_1"),
    }
    first = [_in_hbm(a) for a in first]
    staged = {nm: _in_hbm(a) for nm, a in staged.items()}
    (g_ev_in, g_ev_out, g_small), = _exchange([_Gather(first)], "gather_first")
    small_all = jnp.reshape(g_small, (N_CHIPS, -1, LANES))
    per_chip = [_unpack(small_all[q], small_shapes) for q in range(N_CHIPS)]
    conv_a_w = jnp.concatenate([pc[0] for pc in per_chip], axis=1)
    conv_b_w = jnp.concatenate([pc[1] for pc in per_chip], axis=1)
    od_norm = jnp.concatenate([pc[2] for pc in per_chip])[None, :]
    od_bias = jnp.concatenate([pc[3] for pc in per_chip])[None, :]
    od_lng = jnp.concatenate([pc[4] for pc in per_chip])[None, :]
    od_lnb = jnp.concatenate([pc[5] for pc in per_chip])[None, :]

    dx, red, own, landed = _forward_backward(
        x2, tgt2, {"ev_in": g_ev_in, "ev_out": g_ev_out}, staged, conv_a_w, conv_b_w, od_norm, od_bias, od_lng, od_lnb,
        ev_norm_g, ev_conv_a_b, ev_ln_a_g, ev_ln_a_b, od_w_s, od_b_s, mlp_norm_g, final_norm_g, tm=tm, seq=seq)

    routs = _exchange([red["ev_in"].pair_share()], "reduce_tail")
    red["ev_in"].took_share(routs[0])

    given = {"ev_norm_g": (ev_norm_g, m_ev_norm_g, v_ev_norm_g), "ev_conv_a_b": (ev_conv_a_b, m_ev_conv_a_b, v_ev_conv_a_b),
             "ev_ln_a_g": (ev_ln_a_g, m_ev_ln_a_g, v_ev_ln_a_g), "ev_ln_a_b": (ev_ln_a_b, m_ev_ln_a_b, v_ev_ln_a_b),
             "od_w_s": (od_w_s, m_od_w_s, v_od_w_s), "od_b_s": (od_b_s, m_od_b_s, v_od_b_s),
             "mlp_norm_g": (mlp_norm_g, m_mlp_norm_g, v_mlp_norm_g), "final_norm_g": (final_norm_g, m_final_norm_g, v_final_norm_g),
             "ev_conv_a_w": (ev_conv_a_w, m_ev_conv_a_w, v_ev_conv_a_w), "ev_conv_b_w": (ev_conv_b_w, m_ev_conv_b_w, v_ev_conv_b_w),
             "od_norm_g": (od_norm_g, m_od_norm_g, v_od_norm_g), "od_b_in": (od_b_in, m_od_b_in, v_od_b_in),
             "od_ln_v_g": (od_ln_v_g, m_od_ln_v_g, v_od_ln_v_g), "od_ln_v_b": (od_ln_v_b, m_od_ln_v_b, v_od_ln_v_b)}
    shaped = {nm: tuple(jnp.reshape(a, shape) for a in given[nm]) for nm, shape, _, _ in SMALL_WEIGHTS}
    loss11, small_upd = _small_update(own, landed, shaped)
    loss = loss11[0, 0]
    upd = {nm: [jnp.reshape(o, given[nm][0].shape) for o in outs] for nm, outs in small_upd.items()}

    def big_update(wt, m, v, names, call):
        grads = [red[nm].reduced() for nm in names]
        shp3 = (len(grads),) + grads[0].shape
        outs, _ = _adamw(jnp.reshape(wt, shp3), jnp.reshape(m, shp3), jnp.reshape(v, shp3), grads, call)
        return [jnp.reshape(o, wt.shape) for o in outs], None

    upd["mlp_w2"], _ = big_update(mlp_w2, m_mlp_w2, v_mlp_w2, ["w2_0", "w2_1"], "adamw_mlp_w2")
    upd["mlp_w1"], _ = big_update(mlp_w1, m_mlp_w1, v_mlp_w1, ["w1_0", "w1_1"], "adamw_mlp_w1")
    upd["ev_w_in"], _ = big_update(ev_w_in, m_ev_w_in, v_ev_w_in, ["ev_in"], "adamw_ev_w_in")
    upd["ev_w_out"], _ = big_update(ev_w_out, m_ev_w_out, v_ev_w_out, ["ev_out"], "adamw_ev_w_out")
    upd["od_w_in"], _ = big_update(od_w_in, m_od_w_in, v_od_w_in, ["od_in"], "adamw_od_w_in")
    upd["od_w_out"], _ = big_update(od_w_out, m_od_w_out, v_od_w_out, ["od_out"], "adamw_od_w_out")

    order = ["ev_norm_g", "ev_w_in", "ev_conv_a_w", "ev_conv_a_b", "ev_ln_a_g", "ev_ln_a_b", "ev_conv_b_w", "ev_w_out",
             "od_norm_g", "od_w_in", "od_b_in", "od_ln_v_g", "od_ln_v_b", "od_w_s", "od_b_s", "od_w_out", "mlp_norm_g",
             "mlp_w1", "mlp_w2", "final_norm_g"]
    grad_x = jnp.reshape(dx, x.shape)
    return (loss, grad_x, *[upd[nm][0] for nm in order], *[upd[nm][1] for nm in order],
            *[upd[nm][2] for nm in order], *[upd[nm][3] for nm in order])
```

```python
import functools

import jax
import jax.numpy as jnp
from jax import lax
from jax.experimental import pallas as pl
from jax.experimental.pallas import tpu as pltpu

F32 = jnp.float32
BF16 = jnp.bfloat16

D_MODEL = 1024
A_DIM = 512
B_DIM = 512
IN_EVEN = 2 * A_DIM + 3 * B_DIM
A_CONV_WIDTH = 31
B_CONV_WIDTH = 3
CHUNK = 128
C_GROUPS = 8
C_DIM = 1024
D_FF = 4096
RMS_EPS = 1e-6
LN_EPS = 1e-5
ADAM_LR = 0.001
ADAM_B1 = 0.9
ADAM_B2 = 0.999
ADAM_EPS = 1e-08
ADAM_WD = 0.01
ADAM_STEP = 10

N_CHIPS = 4
N_DEV = 8
TOKEN_TILE = 512
A_HALO = 32
B_HALO = 8
CONV_ROWS = 16
DW_TAPS = 4
ELEM_ROWS = 16
PAIR = 2 * CHUNK
LANES = 128
SUBLANES = 8
MXU_ROWS = 256
MIB = 1024 * 1024
MESH = pl.DeviceIdType.MESH
ANY = pl.BlockSpec(memory_space=pl.ANY)


def _dot(a, b):
    return lax.dot_general(a, b, (((1,), (0,)), ((), ())), preferred_element_type=F32)


def _dot_nt(a, b):
    return lax.dot_general(a, b, (((1,), (1,)), ((), ())), preferred_element_type=F32)


def _dot_tn(a, b):
    return lax.dot_general(a, b, (((0,), (0,)), ((), ())), preferred_element_type=F32)


def _params(vmem_mib, n_axes=1):
    return pltpu.CompilerParams(dimension_semantics=("arbitrary",) * n_axes, vmem_limit_bytes=vmem_mib * MIB)


def _row_spec(tm, cols, rev_nt=None):
    if rev_nt is None:
        return pl.BlockSpec((tm, cols), lambda i: (i, 0))
    return pl.BlockSpec((tm, cols), lambda i: (rev_nt - 1 - i, 0))


def _full_spec(shape):
    nd = len(shape)
    return pl.BlockSpec(shape, lambda i: (0,) * nd)


def _block_rows(rows, cap=512):
    best = SUBLANES
    for br in range(SUBLANES, min(rows, cap) + 1, SUBLANES):
        if rows % br == 0:
            best = br
    return best


N_LOADS = 2


def _load_weights(pairs, sems):
    @pl.when(pl.program_id(0) == 0)
    def _():
        copies = [pltpu.make_async_copy(src, dst, sems.at[k]) for k, (src, dst) in enumerate(pairs)]
        for cp in copies:
            cp.start()
        for cp in copies:
            cp.wait()


def _rms_fwd(x, g):
    rstd = lax.rsqrt(jnp.mean(x * x, axis=-1, keepdims=True) + RMS_EPS)
    return x * rstd * g, rstd


def _rms_bwd(dn, x, rstd, g):
    a = dn * g
    xh = x * rstd
    dx = rstd * (a - xh * jnp.mean(a * xh, axis=-1, keepdims=True))
    dg = jnp.sum(dn * xh, axis=0, keepdims=True)
    return dx, dg


def _ln_stats(v):
    mu = jnp.mean(v, axis=-1, keepdims=True)
    xc = v - mu
    rs = lax.rsqrt(jnp.mean(xc * xc, axis=-1, keepdims=True) + LN_EPS)
    return xc * rs, rs


def _ln_bwd(dy, xhat, rs, g):
    dxh = dy * g
    dv = rs * (dxh - jnp.mean(dxh, axis=-1, keepdims=True) - xhat * jnp.mean(dxh * xhat, axis=-1, keepdims=True))
    return dv, jnp.sum(dy * xhat, axis=0, keepdims=True), jnp.sum(dy, axis=0, keepdims=True)


def _gelu_cdf(s):
    return 0.5 * (1.0 + lax.erf(s * 0.7071067811865476))


def _mesh_pos():
    return lax.axis_index("x"), lax.axis_index("y"), lax.axis_index("c")


def _other_chips(x, y):
    return [(1 - x, y), (x, 1 - y), (1 - x, 1 - y)]


def _remote(src, dst, send_sem, recv_sem, to):
    return pltpu.make_async_remote_copy(src_ref=src, dst_ref=dst, send_sem=send_sem, recv_sem=recv_sem,
                                        device_id=to, device_id_type=MESH)


def _like(arrays):
    return [jax.ShapeDtypeStruct(a.shape, a.dtype) for a in arrays]


class _Gather:
    def __init__(self, bufs):
        self.ins = list(bufs)
        self.out_shapes = _like(bufs)
        self.aliases = {t: t for t in range(len(bufs))}
        self.n_sems = 6 * len(bufs)

    def _ici(self, ins, outs, send, recv, t, k, chip, mine, c):
        return _remote(ins[t].at[mine, c], outs[t].at[mine, c], send.at[6 * t + k], recv.at[6 * t + k], (*chip, c))

    def start(self, ins, outs, send, recv):
        x, y, c = _mesh_pos()
        for t in range(len(ins)):
            for k, chip in enumerate(_other_chips(x, y)):
                self._ici(ins, outs, send, recv, t, k, chip, 2 * x + y, c).start()

    def _pass_on(self, outs, send, recv, t, k, chip, c, to):
        blk = outs[t].at[2 * chip[0] + chip[1], c]
        return _remote(blk, blk, send.at[6 * t + 3 + k], recv.at[6 * t + 3 + k], to)

    def near_end(self, ins, outs, send, recv):
        x, y, c = _mesh_pos()
        for t in range(len(ins)):
            for k, chip in enumerate(_other_chips(x, y)):
                blk = outs[t].at[2 * chip[0] + chip[1], c]
                _remote(blk, blk, send.at[6 * t + k], recv.at[6 * t + k], (x, y, c)).wait_recv()
                self._pass_on(outs, send, recv, t, k, chip, c, (x, y, 1 - c)).start()

    def finish(self, ins, outs, send, recv):
        x, y, c = _mesh_pos()
        chips = _other_chips(x, y)
        for t in range(len(ins)):
            for k, chip in enumerate(chips):
                self._pass_on(outs, send, recv, t, k, chip, 1 - c, (x, y, c)).wait_recv()
        for t in range(len(ins)):
            for k, chip in enumerate(chips):
                self._ici(ins, outs, send, recv, t, k, chip, 2 * x + y, c).wait_send()
                self._pass_on(outs, send, recv, t, k, chip, c, (x, y, 1 - c)).wait_send()


class _PairSwap:
    def __init__(self, grads):
        self.ins = list(grads)
        self.out_shapes = [jax.ShapeDtypeStruct((g.shape[0],) + g.shape[2:], g.dtype) for g in grads]
        self.aliases = {}
        self.n_sems = len(grads)

    def _copies(self, ins, outs, send, recv):
        x, y, c = _mesh_pos()
        return [_remote(ins[t].at[:, 1 - c], outs[t], send.at[t], recv.at[t], (x, y, 1 - c)) for t in range(len(ins))]

    def start(self, ins, outs, send, recv):
        for cp in self._copies(ins, outs, send, recv):
            cp.start()

    def finish(self, ins, outs, send, recv):
        for cp in self._copies(ins, outs, send, recv):
            cp.wait()


class _ChipSwap:
    def __init__(self, parts):
        self.ins = list(parts)
        self.out_shapes = [jax.ShapeDtypeStruct((3,) + p.shape[1:], p.dtype) for p in parts]
        self.aliases = {}
        self.n_sems = 3 * len(parts)

    def _copies(self, ins, outs, send, recv):
        x, y, c = _mesh_pos()
        return [_remote(ins[t].at[2 * chip[0] + chip[1]], outs[t].at[k], send.at[3 * t + k], recv.at[3 * t + k], (*chip, c))
                for t in range(len(ins)) for k, chip in enumerate(_other_chips(x, y))]

    def start(self, ins, outs, send, recv):
        for cp in self._copies(ins, outs, send, recv):
            cp.start()

    def finish(self, ins, outs, send, recv):
        for cp in self._copies(ins, outs, send, recv):
            cp.wait()


class _PairShare:
    def __init__(self, fulls):
        self.ins = list(fulls)
        self.out_shapes = _like(fulls)
        self.aliases = {t: t for t in range(len(fulls))}
        self.n_sems = len(fulls)

    def _copies(self, ins, outs, send, recv):
        x, y, c = _mesh_pos()
        return [_remote(ins[t].at[c], outs[t].at[c], send.at[t], recv.at[t], (x, y, 1 - c)) for t in range(len(ins))]

    def start(self, ins, outs, send, recv):
        for cp in self._copies(ins, outs, send, recv):
            cp.start()

    def finish(self, ins, outs, send, recv):
        for cp in self._copies(ins, outs, send, recv):
            cp.wait()


class _ShareAll:
    def __init__(self, arrays):
        self.ins = list(arrays)
        self.out_shapes = [jax.ShapeDtypeStruct((N_DEV,) + a.shape, a.dtype) for a in arrays]
        self.aliases = {}
        self.n_sems = (N_DEV - 1) * len(arrays)

    def _peers(self):
        x, y, c = _mesh_pos()
        flips = [((r >> 2) & 1, (r >> 1) & 1, r & 1) for r in range(1, N_DEV)]
        return (x, y, c), [(x ^ fx, y ^ fy, c ^ fc) for fx, fy, fc in flips]

    def _sends(self, ins, outs, send, recv):
        (x, y, c), peers = self._peers()
        mine = 4 * x + 2 * y + c
        return [_remote(ins[a], outs[a].at[mine], send.at[7 * a + r], recv.at[7 * a + r], peer)
                for a in range(len(ins)) for r, peer in enumerate(peers)]

    def start(self, ins, outs, send, recv):
        for cp in self._sends(ins, outs, send, recv):
            cp.start()

    def finish(self, ins, outs, send, recv):
        (x, y, c), peers = self._peers()
        for a in range(len(ins)):
            for r, (px, py, pc) in enumerate(peers):
                blk = outs[a].at[4 * px + 2 * py + pc]
                _remote(blk, blk, send.at[7 * a + r], recv.at[7 * a + r], (x, y, c)).wait_recv()
        for cp in self._sends(ins, outs, send, recv):
            cp.wait_send()


def _pallas(body, operands, *, name, grid, in_specs, out_specs, out_shape, scratch_shapes=(), vmem_mib=32, riders=(),
            prefetch=None):
    in_specs, out_specs, out_shape, scratch_shapes = list(in_specs), list(out_specs), list(out_shape), list(scratch_shapes)
    if not riders and prefetch is None:
        outs = pl.pallas_call(body, name=name, grid=grid, in_specs=in_specs, out_specs=out_specs, out_shape=out_shape,
                              scratch_shapes=scratch_shapes, compiler_params=_params(vmem_mib, len(grid)))(*operands)
        return list(outs), []
    n_in, n_out, n_scr = len(in_specs), len(out_specs), len(scratch_shapes)
    r_in = [len(r.ins) for r in riders]
    r_out = [len(r.out_shapes) for r in riders]
    steps = 1
    for g in grid:
        steps *= g

    n_pre = 0 if prefetch is None else 1

    def wrapped(*refs):
        refs = list(refs)
        pre, refs = refs[:n_pre], refs[n_pre:]
        ins, refs = refs[:n_in], refs[n_in:]
        rins = []
        for k in r_in:
            rins.append(refs[:k])
            refs = refs[k:]
        outs, refs = refs[:n_out], refs[n_out:]
        routs = []
        for k in r_out:
            routs.append(refs[:k])
            refs = refs[k:]
        scr, sems = refs[:n_scr], refs[n_scr:]
        step = 0
        for ax, g in enumerate(grid):
            step = step * g + pl.program_id(ax)

        def each(what):
            for j, r in enumerate(riders):
                if hasattr(r, what):
                    getattr(r, what)(rins[j], routs[j], sems[2 * j], sems[2 * j + 1])

        if grid:
            pl.when(step == 0)(lambda: each("start"))
        else:
            each("start")
        body(*pre, *ins, *outs, *scr)
        if grid:
            @pl.when(step == steps - 1)
            def _():
                each("near_end")
                each("finish")
        else:
            each("near_end")
            each("finish")

    aliases, off_in, off_out = {}, n_pre + n_in, n_out
    for r, ki, ko in zip(riders, r_in, r_out):
        for i, o in r.aliases.items():
            aliases[off_in + i] = off_out + o
        off_in, off_out = off_in + ki, off_out + ko
    sems = []
    for r in riders:
        sems += [pltpu.SemaphoreType.DMA((r.n_sems,)), pltpu.SemaphoreType.DMA((r.n_sems,))]
    layout = dict(grid=grid, in_specs=in_specs + [ANY] * sum(r_in), out_specs=out_specs + [ANY] * sum(r_out),
                  scratch_shapes=scratch_shapes + sems)
    if prefetch is not None:
        layout = dict(grid_spec=pltpu.PrefetchScalarGridSpec(num_scalar_prefetch=1, **layout))
    res = pl.pallas_call(
        wrapped, name=name, **layout,
        out_shape=out_shape + [s for r in riders for s in r.out_shapes], input_output_aliases=aliases,
        compiler_params=pltpu.CompilerParams(dimension_semantics=("arbitrary",) * len(grid),
                                             vmem_limit_bytes=vmem_mib * MIB, has_side_effects=True),
    )(*([] if prefetch is None else [prefetch]), *operands, *[a for r in riders for a in r.ins])
    res = list(res)
    outs, res = res[:n_out], res[n_out:]
    routs = []
    for k in r_out:
        routs.append(res[:k])
        res = res[k:]
    return outs, routs


def _exchange(riders, name):
    return _pallas(lambda: None, [], name=name, grid=(), in_specs=[], out_specs=[], out_shape=[], riders=riders)[1]


def _in_hbm(a):
    return pltpu.with_memory_space_constraint(a, pltpu.HBM)


def _place_shard(w, layer, dtype, name):
    _, rows, cols = w.shape
    half = rows // 2
    br = _block_rows(half)
    nb = half // br
    mine = 2 * lax.axis_index("x") + lax.axis_index("y")

    def body(q_ref, w_ref, o_ref):
        o_ref[...] = w_ref[...].astype(dtype)

    return pl.pallas_call(
        body, name=name,
        grid_spec=pltpu.PrefetchScalarGridSpec(
            num_scalar_prefetch=1, grid=(2, nb),
            in_specs=[pl.BlockSpec((None, br, cols), lambda h, i, q: (layer, h * nb + i, 0))],
            out_specs=pl.BlockSpec((None, None, br, cols), lambda h, i, q: (q[0], h, i, 0))),
        out_shape=pltpu.HBM((N_CHIPS, 2, half, cols), dtype),
        compiler_params=_params(16, 2),
    )(jnp.reshape(mine, (1,)).astype(jnp.int32), w)


def _add_pair(g, recv, name):
    _, _, r, cdim = g.shape
    br = _block_rows(r, 256)
    c = lax.axis_index("c")

    def body(c_ref, g_ref, r_ref, o_ref):
        o_ref[...] = (g_ref[...] + r_ref[...]).astype(BF16)

    return pl.pallas_call(
        body, name=name,
        grid_spec=pltpu.PrefetchScalarGridSpec(
            num_scalar_prefetch=1, grid=(N_CHIPS, r // br),
            in_specs=[pl.BlockSpec((None, None, br, cdim), lambda q, i, c_ref: (q, c_ref[0], i, 0)),
                      pl.BlockSpec((None, br, cdim), lambda q, i, c_ref: (q, i, 0))],
            out_specs=pl.BlockSpec((None, br, cdim), lambda q, i, c_ref: (q, i, 0))),
        out_shape=pltpu.HBM((N_CHIPS, r, cdim), BF16),
        compiler_params=_params(16, 2),
    )(jnp.reshape(c, (1,)).astype(jnp.int32), _in_hbm(g), _in_hbm(recv))


def _add_chips(own, recv, name):
    _, r, cdim = own.shape
    br = _block_rows(r, 256)
    x, y, c = _mesh_pos()

    def body(pos_ref, own_ref, r_ref, o_ref):
        acc = own_ref[...].astype(F32)
        for k in range(3):
            acc = acc + r_ref[k].astype(F32)
        o_ref[...] = acc

    return pl.pallas_call(
        body, name=name,
        grid_spec=pltpu.PrefetchScalarGridSpec(
            num_scalar_prefetch=1, grid=(r // br,),
            in_specs=[pl.BlockSpec((None, br, cdim), lambda i, pos: (pos[0], i, 0)),
                      pl.BlockSpec((3, br, cdim), lambda i, pos: (0, i, 0))],
            out_specs=pl.BlockSpec((None, br, cdim), lambda i, pos: (pos[1], i, 0))),
        out_shape=pltpu.HBM((2, r, cdim), F32),
        compiler_params=_params(16, 1),
    )(jnp.stack([2 * x + y, c]).astype(jnp.int32), _in_hbm(own), _in_hbm(recv))


def _adam_math(w, m, v, g):
    c1 = 1.0 / (1.0 - ADAM_B1 ** ADAM_STEP)
    c2 = 1.0 / (1.0 - ADAM_B2 ** ADAM_STEP)
    m_new = ADAM_B1 * m + (1.0 - ADAM_B1) * g
    v_new = ADAM_B2 * v + (1.0 - ADAM_B2) * (g * g)
    return -ADAM_LR * ((m_new * c1) / (jnp.sqrt(v_new * c2) + ADAM_EPS) + ADAM_WD * w), m_new, v_new


SMALL_WEIGHTS = [
    ("ev_norm_g", (1, D_MODEL), ["ev_norm_g"], None), ("ev_conv_a_b", (1, A_DIM), ["ev_conv_a_b"], None),
    ("ev_ln_a_g", (1, A_DIM), ["ev_ln_a_g"], None), ("ev_ln_a_b", (1, A_DIM), ["ev_ln_a_b"], None),
    ("od_w_s", (C_GROUPS, CHUNK, CHUNK), ["od_w_s_lo", "od_w_s_hi"], None), ("od_b_s", (C_GROUPS, CHUNK), ["od_b_s"], None),
    ("mlp_norm_g", (2, D_MODEL), ["mlp_norm_g0", "mlp_norm_g1"], None), ("final_norm_g", (1, D_MODEL), ["final_norm_g"], None),
    ("ev_conv_a_w", (A_CONV_WIDTH, A_DIM // N_CHIPS), ["ev_conv_a_w"], A_DIM // N_CHIPS),
    ("ev_conv_b_w", (B_CONV_WIDTH, B_DIM // N_CHIPS), ["ev_conv_b_w"], B_DIM // N_CHIPS),
    ("od_norm_g", (1, D_MODEL // N_CHIPS), ["od_norm_g"], D_MODEL // N_CHIPS),
    ("od_b_in", (1, 2 * C_DIM // N_CHIPS), ["od_b_in"], 2 * C_DIM // N_CHIPS),
    ("od_ln_v_g", (1, C_DIM // N_CHIPS), ["od_ln_v_g"], C_DIM // N_CHIPS),
    ("od_ln_v_b", (1, C_DIM // N_CHIPS), ["od_ln_v_b"], C_DIM // N_CHIPS),
]


def _small_update(own, landed, weights):
    names = list(own.keys())
    n_g, n_w = len(names), len(SMALL_WEIGHTS)

    def body(*refs):
        refs = list(refs)
        own_refs = dict(zip(names, refs[:n_g]))
        land_refs = dict(zip(names, refs[n_g:2 * n_g]))
        wmv = [refs[2 * n_g + 3 * i:2 * n_g + 3 * i + 3] for i in range(n_w)]
        o0 = 2 * n_g + 3 * n_w
        loss_ref = refs[o0]
        outs = [refs[o0 + 1 + 4 * i:o0 + 5 + 4 * i] for i in range(n_w)]
        acc = dict(zip(names, refs[o0 + 1 + 4 * n_w:]))
        x, y, c = _mesh_pos()
        mine, chip = 4 * x + 2 * y + c, 2 * x + y

        for nm in names:
            for d in range(N_DEV):
                def add(term, nm=nm, d=d):
                    acc[nm][...] = term if d == 0 else acc[nm][...] + term
                pl.when(mine == d)(lambda nm=nm, add=add: add(own_refs[nm][...]))
                pl.when(mine != d)(lambda nm=nm, d=d, add=add: add(land_refs[nm][d]))
        loss_ref[...] = acc["loss"][...]

        def update(i, rows, g):
            w_ref, m_ref, v_ref = wmv[i]
            delta, m_new, v_new = _adam_math(w_ref[rows], m_ref[rows], v_ref[rows], g)
            for ref, val in zip(outs[i], (g, delta, m_new, v_new)):
                ref[rows] = val

        for i, (_, shape, grads, per_chip) in enumerate(SMALL_WEIGHTS):
            for row, gname in enumerate(grads):
                per_grad = shape[0] // len(grads)
                rows = slice(row * per_grad, (row + 1) * per_grad)
                if per_chip is None:
                    update(i, rows, acc[gname][...])
                else:
                    for q in range(N_CHIPS):
                        pl.when(chip == q)(lambda i=i, rows=rows, gname=gname, q=q, per_chip=per_chip:
                                           update(i, rows, acc[gname][:, q * per_chip:(q + 1) * per_chip]))

    operands = [own[nm] for nm in names] + [landed[nm] for nm in names]
    for nm, _, _, _ in SMALL_WEIGHTS:
        operands += list(weights[nm])
    out_shape = [jax.ShapeDtypeStruct((1, 1), F32)]
    for _, shape, _, _ in SMALL_WEIGHTS:
        out_shape += [jax.ShapeDtypeStruct(shape, F32)] * 4
    res = pl.pallas_call(
        body, name="small_update", grid=(1,),
        in_specs=[_full_spec(a.shape) for a in operands], out_specs=[_full_spec(s.shape) for s in out_shape],
        out_shape=out_shape, scratch_shapes=[pltpu.VMEM(own[nm].shape, F32) for nm in names],
        compiler_params=_params(32, 1),
    )(*[_in_hbm(a) for a in operands])
    return res[0], {nm: res[1 + 4 * i:5 + 4 * i] for i, (nm, _, _, _) in enumerate(SMALL_WEIGHTS)}


def _adamw(w, m, v, grads, name, riders=()):
    layers, r, cdim = w.shape
    br = _block_rows(r, 256 if cdim > LANES else 1024)

    def body(*refs):
        w_ref, m_ref, v_ref = refs[:3]
        g_refs = refs[3:3 + layers]
        go_ref, d_ref, mo_ref, vo_ref = refs[3 + layers:]
        layer = pl.program_id(0)
        for l in range(layers):
            @pl.when(layer == l)
            def _(l=l):
                g = g_refs[l][...]
                go_ref[...] = g
                d_ref[...], mo_ref[...], vo_ref[...] = _adam_math(w_ref[...], m_ref[...], v_ref[...], g)

    spec3 = pl.BlockSpec((None, br, cdim), lambda l, i: (l, i, 0))
    spec2 = pl.BlockSpec((br, cdim), lambda l, i: (i, 0))
    out = jax.ShapeDtypeStruct((layers, r, cdim), F32)
    return _pallas(body, [w, m, v, *[_in_hbm(g) for g in grads]], name=name, grid=(layers, r // br),
                   in_specs=[spec3, spec3, spec3] + [spec2] * layers, out_specs=[spec3] * 4, out_shape=[out] * 4,
                   vmem_mib=32, riders=riders)


def _fill_shifted(buf, rows):
    for b in range(1, SUBLANES):
        buf[b, 0:rows - SUBLANES, :] = buf[0, b:b + rows - SUBLANES, :]


def _window(buf, start, size):
    return buf[start % SUBLANES, start - start % SUBLANES:start - start % SUBLANES + size, :]


def _conv31(src, w_ref, r0, base, init):
    acc = init
    for k in range(A_CONV_WIDTH):
        acc = acc + w_ref[k:k + 1, :] * _window(src, base + k + r0, CONV_ROWS)
    return acc


def _fwd_even(x, norm_g, w_in, conv_a_w, conv_a_b, ln_g, ln_b, conv_b_w, w_out, *, tm, seq, riders=()):
    tokens = x.shape[0]
    nt, tps = tokens // tm, seq // tm

    def body(x_ref, g_ref, win_hbm, caw_ref, cab_ref, lng_ref, lnb_ref, cbw_ref, wout_hbm,
             h_ref, n_ref, z_ref, a2_ref, cv_ref, mix_ref, win_v, wout_v, pa, pb, sem):
        i = pl.program_id(0)

        _load_weights([(win_hbm, win_v), (wout_hbm, wout_v)], sem)

        xv = x_ref[...]
        nf, _ = _rms_fwd(xv, g_ref[...])
        n = nf.astype(BF16)
        n_ref[...] = n
        z = jnp.concatenate([_dot(n, win_v[j]) for j in range(N_CHIPS)], axis=1)
        z_ref[...] = z.astype(BF16)
        a_val, a_gate = z[:, 0:A_DIM], z[:, A_DIM:2 * A_DIM]
        b_gate, c_gate, b_val = z[:, 1024:1536], z[:, 1536:2048], z[:, 2048:2560]

        first = (i % tps) == 0

        @pl.when(first)
        def _():
            pa[0, 0:A_HALO, :] = jnp.zeros((A_HALO, A_DIM), F32)
            pb[0:B_HALO, :] = jnp.zeros((B_HALO, B_DIM), F32)

        @pl.when(jnp.logical_not(first))
        def _():
            pa[0, 0:A_HALO, :] = pa[0, tm:tm + A_HALO, :]
            pb[0:B_HALO, :] = pb[tm:tm + B_HALO, :]

        pa[0, A_HALO:A_HALO + tm, :] = a_val * jax.nn.sigmoid(a_gate)
        pb[B_HALO:B_HALO + tm, :] = c_gate * b_val
        _fill_shifted(pa, A_HALO + tm)
        bias = jnp.broadcast_to(cab_ref[...], (CONV_ROWS, A_DIM))
        for r0 in range(0, tm, CONV_ROWS):
            a2_ref[r0:r0 + CONV_ROWS, :] = _conv31(pa, caw_ref, r0, A_HALO - (A_CONV_WIDTH - 1), bias)
        xhat, _ = _ln_stats(a2_ref[...])
        a3 = xhat * lng_ref[...] + lnb_ref[...]
        a4 = a3 * jax.nn.sigmoid(a3)
        cv = cbw_ref[0:1, :] * pb[B_HALO - 2:B_HALO - 2 + tm, :]
        cv = cv + cbw_ref[1:2, :] * pb[B_HALO - 1:B_HALO - 1 + tm, :]
        cv = cv + cbw_ref[2:3, :] * pb[B_HALO:B_HALO + tm, :]
        cv_ref[...] = cv.astype(BF16)
        mix = jnp.concatenate([a4, b_gate * cv], axis=1).astype(BF16)
        mix_ref[...] = mix
        h_ref[...] = xv + _dot(mix, wout_v[...])

    shp = lambda cols, dt: jax.ShapeDtypeStruct((tokens, cols), dt)
    return _pallas(
        body, [x, norm_g, w_in, conv_a_w, conv_a_b, ln_g, ln_b, conv_b_w, w_out], name="fwd_even", grid=(nt,),
        in_specs=[_row_spec(tm, D_MODEL), _full_spec((1, D_MODEL)), ANY, _full_spec((A_CONV_WIDTH, A_DIM)),
                  _full_spec((1, A_DIM)), _full_spec((1, A_DIM)), _full_spec((1, A_DIM)),
                  _full_spec((B_CONV_WIDTH, B_DIM)), ANY],
        out_specs=[_row_spec(tm, D_MODEL), _row_spec(tm, D_MODEL), _row_spec(tm, IN_EVEN), _row_spec(tm, A_DIM),
                   _row_spec(tm, B_DIM), _row_spec(tm, D_MODEL)],
        out_shape=[shp(D_MODEL, F32), shp(D_MODEL, BF16), shp(IN_EVEN, BF16), shp(A_DIM, F32), shp(B_DIM, BF16),
                   shp(D_MODEL, BF16)],
        scratch_shapes=[pltpu.VMEM((N_CHIPS, D_MODEL, IN_EVEN // N_CHIPS), BF16), pltpu.VMEM((D_MODEL, D_MODEL), BF16),
                        pltpu.VMEM((SUBLANES, A_HALO + tm, A_DIM), F32), pltpu.VMEM((B_HALO + tm, B_DIM), F32),
                        pltpu.SemaphoreType.DMA((N_LOADS,))],
        vmem_mib=56, riders=riders)


def _loss_tail(xv, g, target, loss_ref, dh_ref, dhb_ref, dg_ref):
    @pl.when(pl.program_id(0) == 0)
    def _():
        loss_ref[...] = jnp.zeros((1, 1), F32)
        dg_ref[...] = jnp.zeros((1, D_MODEL), F32)

    out, rstd = _rms_fwd(xv, g)
    err = out - target
    per_token = jnp.sum(err * err, axis=1, keepdims=True) * (1.0 / D_MODEL)
    loss_ref[...] += 0.5 * jnp.sum(per_token, axis=0, keepdims=True)
    dx, dg = _rms_bwd(err * (1.0 / D_MODEL), xv, rstd, g)
    dh_ref[...] = dx
    dhb_ref[...] = dx.astype(BF16)
    dg_ref[...] += dg


def _fwd_mlp(h, norm_g, w1, w2, layer, *, tm, riders=(), head=None):
    tokens = h.shape[0]
    nt = tokens // tm
    fs = D_FF // N_CHIPS
    n_in = 4 if head is None else 6

    def body(*refs):
        h_ref, g_ref, w1_hbm, w2_hbm = refs[:4]
        w1_v, w2_v, sem = refs[-3:]
        outs = refs[n_in:-3]
        n_ref, p_ref, q_ref = outs[1:4] if head is None else outs[0:3]
        _load_weights([(w1_hbm, w1_v), (w2_hbm, w2_v)], sem)

        xv = h_ref[...]
        nf, _ = _rms_fwd(xv, g_ref[...])
        n = nf.astype(BF16)
        n_ref[...] = n
        acc = xv
        for j in range(N_CHIPS):
            p = _dot(n, w1_v[j])
            p_ref[:, j * fs:(j + 1) * fs] = p.astype(BF16)
            r = jnp.maximum(p, 0.0)
            q = (r * r).astype(BF16)
            q_ref[:, j * fs:(j + 1) * fs] = q
            acc = acc + _dot(q, w2_v[j])
        if head is None:
            outs[0][...] = acc
        else:
            _loss_tail(acc, refs[4][...], refs[5][...], *outs[3:7])

    shp = lambda cols, dt: jax.ShapeDtypeStruct((tokens, cols), dt)
    saved_specs = [_row_spec(tm, D_MODEL), _row_spec(tm, D_FF), _row_spec(tm, D_FF)]
    saved_shapes = [shp(D_MODEL, BF16), shp(D_FF, BF16), shp(D_FF, BF16)]
    if head is None:
        operands, in_specs = [h, norm_g, w1, w2], [_row_spec(tm, D_MODEL), _full_spec((1, D_MODEL)), ANY, ANY]
        out_specs, out_shape = [_row_spec(tm, D_MODEL)] + saved_specs, [shp(D_MODEL, F32)] + saved_shapes
    else:
        operands = [h, norm_g, w1, w2, *head]
        in_specs = [_row_spec(tm, D_MODEL), _full_spec((1, D_MODEL)), ANY, ANY, _full_spec((1, D_MODEL)), _row_spec(tm, D_MODEL)]
        out_specs = saved_specs + [_full_spec((1, 1)), _row_spec(tm, D_MODEL), _row_spec(tm, D_MODEL), _full_spec((1, D_MODEL))]
        out_shape = saved_shapes + [jax.ShapeDtypeStruct((1, 1), F32), shp(D_MODEL, F32), shp(D_MODEL, BF16),
                                    jax.ShapeDtypeStruct((1, D_MODEL), F32)]
    return _pallas(
        body, operands, name=f"fwd_mlp{layer}", grid=(nt,), in_specs=in_specs, out_specs=out_specs, out_shape=out_shape,
        scratch_shapes=[pltpu.VMEM((N_CHIPS, D_MODEL, fs), BF16), pltpu.VMEM((N_CHIPS, fs, D_MODEL), BF16),
                        pltpu.SemaphoreType.DMA((N_LOADS,))],
        vmem_mib=56, riders=riders)


def _tril_mask():
    row = lax.broadcasted_iota(jnp.int32, (CHUNK, CHUNK), 0)
    col = lax.broadcasted_iota(jnp.int32, (CHUNK, CHUNK), 1)
    return row >= col


def _triu_mask():
    row = lax.broadcasted_iota(jnp.int32, (CHUNK, CHUNK), 0)
    col = lax.broadcasted_iota(jnp.int32, (CHUNK, CHUNK), 1)
    return row <= col


def _fwd_odd(h, norm_g, w_in, b_in, ln_g, ln_b, w_s, b_s_rows, w_out, *, tm, riders=()):
    tokens = h.shape[0]
    nt = tokens // tm
    cs = 2 * C_DIM // N_CHIPS

    def body(h_ref, g_ref, win_hbm, bin_ref, lng_ref, lnb_ref, ws_ref, bs_ref, wout_hbm,
             ho_ref, n_ref, s_ref, cdf_ref, sv_ref, y_ref, win_v, wout_v, bd, sem):
        _load_weights([(win_hbm, win_v), (wout_hbm, wout_v)], sem)

        @pl.when(pl.program_id(0) == 0)
        def _():
            mask = _tril_mask()
            bd[...] = jnp.zeros(bd.shape, BF16)
            for g in range(C_GROUPS):
                w = jnp.where(mask, ws_ref[g], 0.0).astype(BF16)
                bd[g, 0:CHUNK, 0:CHUNK] = w
                bd[g, CHUNK:PAIR, CHUNK:PAIR] = w

        xv = h_ref[...]
        nf, _ = _rms_fwd(xv, g_ref[...])
        n = nf.astype(BF16)
        n_ref[...] = n
        s = jnp.concatenate([_dot(n, win_v[j]) for j in range(N_CHIPS)], axis=1) + bin_ref[...]
        s_ref[...] = s.astype(BF16)
        cdf = _gelu_cdf(s)
        cdf_ref[...] = cdf.astype(BF16)
        zz = s * cdf
        u, v = zz[:, 0:C_DIM], zz[:, C_DIM:2 * C_DIM]
        xhat, _ = _ln_stats(v)
        vn = (xhat * lng_ref[...] + lnb_ref[...]).astype(BF16)
        for g in range(C_GROUPS):
            cols = slice(g * CHUNK, (g + 1) * CHUNK)
            bias = jnp.concatenate([bs_ref[g], bs_ref[g]], axis=0)
            for r0 in range(0, tm, PAIR):
                sv = _dot(bd[g], vn[r0:r0 + PAIR, cols]) + bias
                sv_ref[r0:r0 + PAIR, cols] = sv.astype(BF16)
                y_ref[r0:r0 + PAIR, cols] = (u[r0:r0 + PAIR, cols] * sv).astype(BF16)
        ho_ref[...] = xv + _dot(y_ref[...], wout_v[...])

    shp = lambda cols, dt: jax.ShapeDtypeStruct((tokens, cols), dt)
    return _pallas(
        body, [h, norm_g, w_in, b_in, ln_g, ln_b, w_s, b_s_rows, w_out], name="fwd_odd", grid=(nt,),
        in_specs=[_row_spec(tm, D_MODEL), _full_spec((1, D_MODEL)), ANY, _full_spec((1, 2 * C_DIM)),
                  _full_spec((1, C_DIM)), _full_spec((1, C_DIM)), _full_spec((C_GROUPS, CHUNK, CHUNK)),
                  _full_spec((C_GROUPS, CHUNK, CHUNK)), ANY],
        out_specs=[_row_spec(tm, D_MODEL), _row_spec(tm, D_MODEL), _row_spec(tm, 2 * C_DIM), _row_spec(tm, 2 * C_DIM),
                   _row_spec(tm, C_DIM), _row_spec(tm, C_DIM)],
        out_shape=[shp(D_MODEL, F32), shp(D_MODEL, BF16), shp(2 * C_DIM, BF16), shp(2 * C_DIM, BF16), shp(C_DIM, BF16),
                   shp(C_DIM, BF16)],
        scratch_shapes=[pltpu.VMEM((N_CHIPS, D_MODEL, cs), BF16), pltpu.VMEM((C_DIM, D_MODEL), BF16),
                        pltpu.VMEM((C_GROUPS, PAIR, PAIR), BF16), pltpu.SemaphoreType.DMA((N_LOADS,))],
        vmem_mib=56, riders=riders)


def _bwd_mlp(dh, h, norm_g, p, w1, w2, layer, *, tm, riders=()):
    tokens = h.shape[0]
    nt = tokens // tm
    fs = D_FF // N_CHIPS

    def body(dh_ref, h_ref, g_ref, p_ref, w1_hbm, w2_hbm, dx_ref, dxb_ref, dp_ref, dg_ref, w1_v, w2_v, sem):
        @pl.when(pl.program_id(0) == 0)
        def _():
            dg_ref[...] = jnp.zeros((1, D_MODEL), F32)

        _load_weights([(w1_hbm, w1_v), (w2_hbm, w2_v)], sem)

        dhv = dh_ref[...]
        dhb = dhv.astype(BF16)
        dn = jnp.zeros((tm, D_MODEL), F32)
        for j in range(N_CHIPS):
            dq = _dot_nt(dhb, w2_v[j])
            r = jnp.maximum(p_ref[:, j * fs:(j + 1) * fs].astype(F32), 0.0)
            dp = ((2.0 * r) * dq).astype(BF16)
            dp_ref[:, j * fs:(j + 1) * fs] = dp
            dn = dn + _dot_nt(dp, w1_v[j])
        xv = h_ref[...]
        g = g_ref[...]
        _, rstd = _rms_fwd(xv, g)
        dx, dg = _rms_bwd(dn, xv, rstd, g)
        dx_ref[...] = dhv + dx
        dxb_ref[...] = (dhv + dx).astype(BF16)
        dg_ref[...] += dg

    return _pallas(
        body, [dh, h, norm_g, p, w1, w2], name=f"bwd_mlp{layer}", grid=(nt,),
        in_specs=[_row_spec(tm, D_MODEL), _row_spec(tm, D_MODEL), _full_spec((1, D_MODEL)), _row_spec(tm, D_FF), ANY, ANY],
        out_specs=[_row_spec(tm, D_MODEL), _row_spec(tm, D_MODEL), _row_spec(tm, D_FF), _full_spec((1, D_MODEL))],
        out_shape=[jax.ShapeDtypeStruct((tokens, D_MODEL), F32), jax.ShapeDtypeStruct((tokens, D_MODEL), BF16),
                   jax.ShapeDtypeStruct((tokens, D_FF), BF16), jax.ShapeDtypeStruct((1, D_MODEL), F32)],
        scratch_shapes=[pltpu.VMEM((N_CHIPS, D_MODEL, fs), BF16), pltpu.VMEM((N_CHIPS, fs, D_MODEL), BF16),
                        pltpu.SemaphoreType.DMA((N_LOADS,))],
        vmem_mib=56, riders=riders)


def _bwd_odd(dh, h, norm_g, s, cdf, sv, w_in, ln_g, ln_b, w_s, w_out, *, tm, riders=()):
    tokens = h.shape[0]
    nt = tokens // tm
    cs = 2 * C_DIM // N_CHIPS

    def body(dh_ref, h_ref, g_ref, s_ref, cdf_ref, sv_ref, win_hbm, lng_ref, lnb_ref, ws_ref, wout_hbm,
             dx_ref, dxb_ref, ds_ref, dg_ref, dbin_ref, dlng_ref, dlnb_ref, dws_ref, dbs_ref,
             win_v, wout_v, bdt, dws_acc, dbs_acc, dvn, sem):
        i = pl.program_id(0)

        _load_weights([(win_hbm, win_v), (wout_hbm, wout_v)], sem)

        @pl.when(i == 0)
        def _():
            mask_t = _triu_mask()
            bdt[...] = jnp.zeros(bdt.shape, BF16)
            for g in range(C_GROUPS):
                wt = jnp.where(mask_t, ws_ref[g].T, 0.0).astype(BF16)
                bdt[g, 0:CHUNK, 0:CHUNK] = wt
                bdt[g, CHUNK:PAIR, CHUNK:PAIR] = wt
            dws_acc[...] = jnp.zeros(dws_acc.shape, F32)
            dbs_acc[...] = jnp.zeros(dbs_acc.shape, F32)
            dg_ref[...] = jnp.zeros(dg_ref.shape, F32)
            dbin_ref[...] = jnp.zeros(dbin_ref.shape, F32)
            dlng_ref[...] = jnp.zeros(dlng_ref.shape, F32)
            dlnb_ref[...] = jnp.zeros(dlnb_ref.shape, F32)

        dhv = dh_ref[...]
        dy = _dot_nt(dhv.astype(BF16), wout_v[...])
        sf = s_ref[...].astype(F32)
        cdf = cdf_ref[...].astype(F32)
        pdf = jnp.exp(-0.5 * sf * sf) * 0.3989422804014327
        zz = sf * cdf
        dgelu = cdf + sf * pdf
        u, v = zz[:, 0:C_DIM], zz[:, C_DIM:2 * C_DIM]
        xhat, rs = _ln_stats(v)
        lng = lng_ref[...]
        vn = (xhat * lng + lnb_ref[...]).astype(BF16)
        du = dy * sv_ref[...].astype(F32)
        dsv = dy * u
        dsvb = dsv.astype(BF16)
        for g in range(C_GROUPS):
            cols = slice(g * CHUNK, (g + 1) * CHUNK)
            for r0 in range(0, tm, PAIR):
                blk = dsvb[r0:r0 + PAIR, cols]
                dvn[r0:r0 + PAIR, cols] = _dot(bdt[g], blk)
                dws_acc[g] += _dot_nt(blk, vn[r0:r0 + PAIR, cols])
                dbs_acc[g] += dsv[r0:r0 + CHUNK, cols] + dsv[r0 + CHUNK:r0 + PAIR, cols]
        dv, dlng, dlnb = _ln_bwd(dvn[...], xhat, rs, lng)
        dlng_ref[...] += dlng
        dlnb_ref[...] += dlnb
        ds = jnp.concatenate([du, dv], axis=1) * dgelu
        dbin_ref[...] += jnp.sum(ds, axis=0, keepdims=True)
        dsb = ds.astype(BF16)
        ds_ref[...] = dsb
        dn = jnp.zeros((tm, D_MODEL), F32)
        for j in range(N_CHIPS):
            dn = dn + _dot_nt(dsb[:, j * cs:(j + 1) * cs], win_v[j])
        xv = h_ref[...]
        g = g_ref[...]
        _, rstd = _rms_fwd(xv, g)
        dx, dg = _rms_bwd(dn, xv, rstd, g)
        dx_ref[...] = dhv + dx
        dxb_ref[...] = (dhv + dx).astype(BF16)
        dg_ref[...] += dg

        @pl.when(i == nt - 1)
        def _():
            mask = _tril_mask()
            for g in range(C_GROUPS):
                full = dws_acc[g]
                dws_ref[g] = jnp.where(mask, full[0:CHUNK, 0:CHUNK] + full[CHUNK:PAIR, CHUNK:PAIR], 0.0)
                dbs_ref[g:g + 1, :] = jnp.sum(dbs_acc[g].T, axis=0, keepdims=True)

    row = lambda cols: jax.ShapeDtypeStruct((1, cols), F32)
    return _pallas(
        body, [dh, h, norm_g, s, cdf, sv, w_in, ln_g, ln_b, w_s, w_out], name="bwd_odd", grid=(nt,),
        in_specs=[_row_spec(tm, D_MODEL), _row_spec(tm, D_MODEL), _full_spec((1, D_MODEL)), _row_spec(tm, 2 * C_DIM),
                  _row_spec(tm, 2 * C_DIM), _row_spec(tm, C_DIM), ANY, _full_spec((1, C_DIM)), _full_spec((1, C_DIM)),
                  _full_spec((C_GROUPS, CHUNK, CHUNK)), ANY],
        out_specs=[_row_spec(tm, D_MODEL), _row_spec(tm, D_MODEL), _row_spec(tm, 2 * C_DIM), _full_spec((1, D_MODEL)),
                   _full_spec((1, 2 * C_DIM)),
                   _full_spec((1, C_DIM)), _full_spec((1, C_DIM)), _full_spec((C_GROUPS, CHUNK, CHUNK)),
                   _full_spec((C_GROUPS, CHUNK))],
        out_shape=[jax.ShapeDtypeStruct((tokens, D_MODEL), F32), jax.ShapeDtypeStruct((tokens, D_MODEL), BF16),
                   jax.ShapeDtypeStruct((tokens, 2 * C_DIM), BF16),
                   row(D_MODEL), row(2 * C_DIM), row(C_DIM), row(C_DIM),
                   jax.ShapeDtypeStruct((C_GROUPS, CHUNK, CHUNK), F32), jax.ShapeDtypeStruct((C_GROUPS, CHUNK), F32)],
        scratch_shapes=[pltpu.VMEM((N_CHIPS, D_MODEL, cs), BF16), pltpu.VMEM((C_DIM, D_MODEL), BF16),
                        pltpu.VMEM((C_GROUPS, PAIR, PAIR), BF16), pltpu.VMEM((C_GROUPS, PAIR, PAIR), F32),
                        pltpu.VMEM((C_GROUPS, CHUNK, CHUNK), F32), pltpu.VMEM((tm, C_DIM), F32),
                        pltpu.SemaphoreType.DMA((N_LOADS,))],
        vmem_mib=56, riders=riders)


def _bwd_even(dh, x, norm_g, z, a2, cv, w_in, conv_a_w, ln_g, ln_b, conv_b_w, w_out, *, tm, seq, riders=()):
    tokens = x.shape[0]
    nt, tps = tokens // tm, seq // tm
    ws = IN_EVEN // N_CHIPS

    def body(dh_ref, x_ref, g_ref, z_ref, a2_ref, cv_ref, win_hbm, caw_ref, lng_ref, lnb_ref, cbw_ref, wout_hbm,
             dx_ref, dz_ref, dg_ref, dcaw_ref, dcab_ref, dlng_ref, dlnb_ref, dcbw_ref,
             win_v, wout_v, ea, eb, a1s, da1s, sigs, wide, dw_acc, sem):
        i = pl.program_id(0)

        _load_weights([(win_hbm, win_v), (wout_hbm, wout_v)], sem)

        @pl.when(i == 0)
        def _():
            dw_acc[...] = jnp.zeros(dw_acc.shape, F32)
            for ref in (dg_ref, dcab_ref, dlng_ref, dlnb_ref, dcbw_ref):
                ref[...] = jnp.zeros(ref.shape, F32)

        last = ((nt - 1 - i) % tps) == tps - 1

        @pl.when(last)
        def _():
            ea[0, tm:tm + A_HALO, :] = jnp.zeros((A_HALO, A_DIM), F32)
            eb[tm:tm + B_HALO, :] = jnp.zeros((B_HALO, B_DIM), F32)

        @pl.when(jnp.logical_not(last))
        def _():
            ea[0, tm:tm + A_HALO, :] = ea[0, 0:A_HALO, :]
            eb[tm:tm + B_HALO, :] = eb[0:B_HALO, :]

        wide[...] = _dot_nt(dh_ref[...].astype(BF16), wout_v[...])
        lng, lnb = lng_ref[...], lnb_ref[...]
        zero_row = jnp.zeros((1, A_DIM), F32)
        dlng, dlnb, dcab = zero_row, zero_row, zero_row
        for r0 in range(0, tm, ELEM_ROWS):
            rows = slice(r0, r0 + ELEM_ROWS)
            a_val, a_gate = z_ref[rows, 0:A_DIM].astype(F32), z_ref[rows, A_DIM:2 * A_DIM].astype(F32)
            xhat, rs = _ln_stats(a2_ref[rows, :])
            a3 = xhat * lng + lnb
            sg = jax.nn.sigmoid(a3)
            da3 = wide[rows, 0:A_DIM] * (sg * (1.0 + a3 * (1.0 - sg)))
            da2, g_part, b_part = _ln_bwd(da3, xhat, rs, lng)
            dlng, dlnb, dcab = dlng + g_part, dlnb + b_part, dcab + jnp.sum(da2, axis=0, keepdims=True)
            ea[0, rows, :] = da2
            eb[rows, :] = wide[rows, A_DIM:A_DIM + B_DIM] * z_ref[rows, 1024:1536].astype(F32)
            sig = jax.nn.sigmoid(a_gate)
            sigs[rows, :] = sig
            a1s[rows, :] = a_val * sig
        dlng_ref[...] += dlng
        dlnb_ref[...] += dlnb
        dcab_ref[...] += dcab
        _fill_shifted(ea, tm + A_HALO)
        for r0 in range(0, tm, CONV_ROWS):
            acc = jnp.zeros((CONV_ROWS, A_DIM), F32)
            for j in range(A_CONV_WIDTH):
                acc = acc + caw_ref[A_CONV_WIDTH - 1 - j:A_CONV_WIDTH - j, :] * _window(ea, r0 + j, CONV_ROWS)
            da1s[r0:r0 + CONV_ROWS, :] = acc
        for j0 in range(0, A_CONV_WIDTH, DW_TAPS):
            taps = range(j0, min(j0 + DW_TAPS, A_CONV_WIDTH))
            part = [jnp.zeros((CONV_ROWS, A_DIM), F32) for _ in taps]
            for r0 in range(0, tm, CONV_ROWS):
                a1c = a1s[r0:r0 + CONV_ROWS, :]
                for u, j in enumerate(taps):
                    part[u] = part[u] + _window(ea, r0 + j, CONV_ROWS) * a1c
            for u, j in enumerate(taps):
                dw_acc[A_CONV_WIDTH - 1 - j] += part[u]
        dcbw = [jnp.zeros((1, B_DIM), F32) for _ in range(B_CONV_WIDTH)]
        for r0 in range(0, tm, ELEM_ROWS):
            rows = slice(r0, r0 + ELEM_ROWS)
            da1, sig = da1s[rows, :], sigs[rows, :]
            dz_ref[rows, 0:A_DIM] = (da1 * sig).astype(BF16)
            dz_ref[rows, A_DIM:2 * A_DIM] = (da1 * z_ref[rows, 0:A_DIM].astype(F32) * (sig * (1.0 - sig))).astype(BF16)
            c_gate, b_val = z_ref[rows, 1536:2048].astype(F32), z_ref[rows, 2048:2560].astype(F32)
            dz_ref[rows, 1024:1536] = (wide[rows, A_DIM:A_DIM + B_DIM] * cv_ref[rows, :].astype(F32)).astype(BF16)
            cb = c_gate * b_val
            dcb = jnp.zeros((ELEM_ROWS, B_DIM), F32)
            for j in range(B_CONV_WIDTH):
                k = B_CONV_WIDTH - 1 - j
                sl = eb[r0 + j:r0 + j + ELEM_ROWS, :]
                dcb = dcb + cbw_ref[k:k + 1, :] * sl
                dcbw[k] = dcbw[k] + jnp.sum(sl * cb, axis=0, keepdims=True)
            dz_ref[rows, 1536:2048] = (dcb * b_val).astype(BF16)
            dz_ref[rows, 2048:2560] = (dcb * c_gate).astype(BF16)
        for k in range(B_CONV_WIDTH):
            dcbw_ref[k:k + 1, :] += dcbw[k]
        dn = jnp.zeros((tm, D_MODEL), F32)
        for j in range(N_CHIPS):
            dn = dn + _dot_nt(dz_ref[:, j * ws:(j + 1) * ws], win_v[j])
        wide[...] = dn
        g = g_ref[...]
        dg = jnp.zeros((1, D_MODEL), F32)
        for r0 in range(0, tm, ELEM_ROWS):
            rows = slice(r0, r0 + ELEM_ROWS)
            xv = x_ref[rows, :]
            _, rstd = _rms_fwd(xv, g)
            dx, dg_part = _rms_bwd(wide[rows, :], xv, rstd, g)
            dx_ref[rows, :] = dh_ref[rows, :] + dx
            dg = dg + dg_part
        dg_ref[...] += dg

        @pl.when(i == nt - 1)
        def _():
            for k in range(A_CONV_WIDTH):
                dcaw_ref[k:k + 1, :] = jnp.sum(dw_acc[k], axis=0, keepdims=True)

    row = lambda cols: jax.ShapeDtypeStruct((1, cols), F32)
    rs_ = functools.partial(_row_spec, rev_nt=nt)
    return _pallas(
        body, [dh, x, norm_g, z, a2, cv, w_in, conv_a_w, ln_g, ln_b, conv_b_w, w_out], name="bwd_even", grid=(nt,),
        in_specs=[rs_(tm, D_MODEL), rs_(tm, D_MODEL), _full_spec((1, D_MODEL)), rs_(tm, IN_EVEN), rs_(tm, A_DIM),
                  rs_(tm, B_DIM), ANY, _full_spec((A_CONV_WIDTH, A_DIM)), _full_spec((1, A_DIM)), _full_spec((1, A_DIM)),
                  _full_spec((B_CONV_WIDTH, B_DIM)), ANY],
        out_specs=[rs_(tm, D_MODEL), rs_(tm, IN_EVEN), _full_spec((1, D_MODEL)), _full_spec((A_CONV_WIDTH, A_DIM)),
                   _full_spec((1, A_DIM)), _full_spec((1, A_DIM)), _full_spec((1, A_DIM)), _full_spec((B_CONV_WIDTH, B_DIM))],
        out_shape=[jax.ShapeDtypeStruct((tokens, D_MODEL), F32), jax.ShapeDtypeStruct((tokens, IN_EVEN), BF16),
                   row(D_MODEL), jax.ShapeDtypeStruct((A_CONV_WIDTH, A_DIM), F32), row(A_DIM), row(A_DIM), row(A_DIM),
                   jax.ShapeDtypeStruct((B_CONV_WIDTH, B_DIM), F32)],
        scratch_shapes=[pltpu.VMEM((N_CHIPS, D_MODEL, ws), BF16), pltpu.VMEM((D_MODEL, D_MODEL), BF16),
                        pltpu.VMEM((SUBLANES, tm + A_HALO, A_DIM), F32), pltpu.VMEM((tm + B_HALO, B_DIM), F32),
                        pltpu.VMEM((tm, A_DIM), F32), pltpu.VMEM((tm, A_DIM), F32), pltpu.VMEM((tm, A_DIM), F32),
                        pltpu.VMEM((tm, D_MODEL), F32),
                        pltpu.VMEM((A_CONV_WIDTH, CONV_ROWS, A_DIM), F32), pltpu.SemaphoreType.DMA((N_LOADS,))],
        vmem_mib=56, riders=riders)


def _wgrad(a, b, name, *, col_shards, riders=()):
    tokens, m = a.shape
    n = b.shape[1]
    kc = 512
    if col_shards:
        bm, bn = m // 2, n // N_CHIPS
        grid = (2, N_CHIPS)
        out_spec = pl.BlockSpec((None, None, bm, bn), lambda i, j: (j, i, 0, 0))
    elif m // 8 >= MXU_ROWS:
        bm, bn = m // 8, n
        grid = (8, 1)
        out_spec = pl.BlockSpec((None, None, bm, bn), lambda i, j: (i // 2, i % 2, 0, 0))
    else:
        bm, bn = m // N_CHIPS, n
        grid = (N_CHIPS, 1)
        out_spec = pl.BlockSpec((None, 2, bm // 2, bn), lambda i, j: (i, 0, 0, 0))

    def body(a_ref, b_ref, o_ref):
        acc = jnp.zeros((bm, bn), F32)
        for k0 in range(0, tokens, kc):
            acc = acc + _dot_tn(a_ref[k0:k0 + kc, :].astype(BF16), b_ref[k0:k0 + kc, :].astype(BF16))
        if len(o_ref.shape) == 3:
            o_ref[0] = acc[0:bm // 2]
            o_ref[1] = acc[bm // 2:bm]
        else:
            o_ref[...] = acc

    out_rows = m // 2 if col_shards else m // 8
    outs, routs = _pallas(
        body, [a, b], name=name, grid=grid,
        in_specs=[pl.BlockSpec((tokens, bm), lambda i, j: (0, i)), pl.BlockSpec((tokens, bn), lambda i, j: (0, j))],
        out_specs=[out_spec], out_shape=[jax.ShapeDtypeStruct((N_CHIPS, 2, out_rows, bn), F32)],
        vmem_mib=56, riders=riders)
    return outs[0], routs


def _wgrad_pair(a, b, name, *, col_shards, riders=(), to_chips=False):
    tokens, m = a.shape
    n = b.shape[1]
    kc = 512
    x0, y0, c0 = _mesh_pos()
    rot = 1 if to_chips else 0

    def slab(q, pre):
        return (q + rot * (1 + pre[1])) % N_CHIPS

    if col_shards:
        bm, bn = m // 2, n // N_CHIPS
        a_spec = pl.BlockSpec((tokens, bm), lambda ph, q, pre: (0, (ph + 1 + pre[0]) % 2))
        b_spec = pl.BlockSpec((tokens, bn), lambda ph, q, pre: (0, slab(q, pre)))
    else:
        bm, bn = m // 8, n
        a_spec = pl.BlockSpec((tokens, bm), lambda ph, q, pre: (0, 2 * slab(q, pre) + (ph + 1 + pre[0]) % 2))
        b_spec = pl.BlockSpec((tokens, bn), lambda ph, q, pre: (0, 0))

    def body(pre_ref, a_ref, b_ref, o_ref, *rest):
        if to_chips:
            land, give, got, mine, send_sems, recv_sems, chip_send, chip_recv = rest
        else:
            give, got, send_sems, recv_sems = rest
        ph, q = pl.program_id(0), pl.program_id(1)
        acc = jnp.zeros((bm, bn), F32)
        for k0 in range(0, tokens, kc):
            acc = acc + _dot_tn(a_ref[k0:k0 + kc, :].astype(BF16), b_ref[k0:k0 + kc, :].astype(BF16))
        x, y, cc = _mesh_pos()

        def tile(t):
            return _remote(give.at[t], got.at[t], send_sems.at[t], recv_sems.at[t], (x, y, 1 - cc))

        def to_chip(s):
            t = (s + 1 + 2 * x + y) % N_CHIPS
            tx, ty = t // 2, t % 2
            k = 2 * (ty ^ y) + (tx ^ x) - 1
            return _remote(mine.at[s], land.at[k], chip_send.at[k], chip_recv.at[k], (tx, ty, cc))

        @pl.when(ph == 0)
        def _():
            give[q] = acc
            tile(q).start()

        @pl.when(ph == 1)
        def _():
            tile(q).wait_recv()
            total = (acc + got[q]).astype(BF16)
            o_ref[...] = total
            if to_chips:
                for s in range(N_CHIPS - 1):
                    @pl.when(q == s)
                    def _(s=s):
                        mine[s] = total
                        to_chip(s).start()

        @pl.when((ph == 1) & (q == N_CHIPS - 1))
        def _():
            for t in range(N_CHIPS):
                tile(t).wait_send()
            if to_chips:
                for s in range(N_CHIPS - 1):
                    to_chip(s).wait()

    prefetch = jnp.stack([c0, 2 * x0 + y0]).astype(jnp.int32)
    out_specs = [pl.BlockSpec((None, bm, bn), lambda ph, q, pre: (slab(ph * q, pre), 0, 0))]
    out_shape = [jax.ShapeDtypeStruct((N_CHIPS, bm, bn), BF16)]
    scratch = [pltpu.VMEM((N_CHIPS, bm, bn), F32), pltpu.VMEM((N_CHIPS, bm, bn), F32)]
    sems = [pltpu.SemaphoreType.DMA((N_CHIPS,)), pltpu.SemaphoreType.DMA((N_CHIPS,))]
    if to_chips:
        out_specs.append(ANY)
        out_shape.append(jax.ShapeDtypeStruct((N_CHIPS - 1, bm, bn), BF16))
        scratch.append(pltpu.VMEM((N_CHIPS - 1, bm, bn), BF16))
        sems += [pltpu.SemaphoreType.DMA((N_CHIPS - 1,)), pltpu.SemaphoreType.DMA((N_CHIPS - 1,))]
    outs, routs = _pallas(
        body, [a, b], name=name, grid=(2, N_CHIPS), in_specs=[a_spec, b_spec], out_specs=out_specs, out_shape=out_shape,
        scratch_shapes=scratch + sems, vmem_mib=56, riders=riders, prefetch=prefetch)
    return (outs if to_chips else outs[0]), routs


class _GradReduce:
    def __init__(self, name, grad=None, chip_sum=None):
        self.name, self.grad, self.chip_sum = name, grad, chip_sum
        self.full = None

    def pair_swap(self):
        return _PairSwap([self.grad])

    def took_pair(self, outs):
        self.chip_sum = _in_hbm(_add_pair(self.grad, outs[0], f"pair_sum_{self.name}"))

    def chip_swap(self):
        return _ChipSwap([self.chip_sum])

    def took_chips(self, outs):
        self.full = _in_hbm(_add_chips(self.chip_sum, outs[0], f"chip_sum_{self.name}"))

    def pair_share(self):
        return _PairShare([self.full])

    def took_share(self, outs):
        self.full = outs[0]

    def reduced(self):
        return jnp.reshape(self.full, (2 * self.full.shape[1], self.full.shape[2]))


def _forward_backward(x2, tgt2, gathered, staged, conv_a_w, conv_b_w, od_norm, od_bias, od_lng, od_lnb,
                      ev_norm_g, ev_conv_a_b, ev_ln_a_g, ev_ln_a_b, od_w_s, od_b_s, mlp_norm_g, final_norm_g,
                      *, tm, seq, distributed=True):
    d = x2.shape[1]
    w = dict(gathered)
    b_s_rows = jnp.broadcast_to(od_b_s[0][:, :, None], (C_GROUPS, CHUNK, CHUNK))

    def ride(*names):
        return [_Gather([staged[nm] for nm in names])] if distributed else []

    def land(routs, *names):
        if distributed:
            for nm, buf in zip(names, routs[0]):
                w[nm] = buf

    def as_cols(buf):
        return jnp.reshape(buf, (N_CHIPS, 2 * buf.shape[2], buf.shape[3]))

    def as_rows(buf):
        return jnp.reshape(buf, (8 * buf.shape[2], buf.shape[3]))

    (h1, n0, z, a2, cv, mix), routs = _fwd_even(
        x2, ev_norm_g, as_cols(w["ev_in"]), conv_a_w, ev_conv_a_b, ev_ln_a_g, ev_ln_a_b, conv_b_w, as_rows(w["ev_out"]),
        tm=tm, seq=seq, riders=ride("w1_0", "w2_0"))
    land(routs, "w1_0", "w2_0")
    (h2, n1, p0, q0), routs = _fwd_mlp(h1, mlp_norm_g[0:1], as_cols(w["w1_0"]), as_cols(w["w2_0"]), 0, tm=tm,
                                       riders=ride("od_in", "od_out", "w1_1"))
    land(routs, "od_in", "od_out", "w1_1")
    (h3, n2, s, cdf, sv, y), routs = _fwd_odd(h2, od_norm, as_cols(w["od_in"]), od_bias, od_lng, od_lnb, od_w_s[0], b_s_rows,
                                         as_rows(w["od_out"]), tm=tm, riders=ride("w2_1"))
    land(routs, "w2_1")
    (n3, p1, q1, loss_part, dh4, dh4b, d_final_g), _ = _fwd_mlp(
        h3, mlp_norm_g[1:2], as_cols(w["w1_1"]), as_cols(w["w2_1"]), 1, tm=tm,
        head=(jnp.reshape(final_norm_g, (1, d)), tgt2))

    red = {}

    def swap(*names):
        return [red[nm].pair_swap() for nm in names] if distributed else []

    def chips(*names):
        return [red[nm].chip_swap() for nm in names] if distributed else []

    def share(*names):
        return [red[nm].pair_share() for nm in names] if distributed else []

    def took(routs, *steps):
        if distributed:
            for (nm, what), outs in zip(steps, routs):
                getattr(red[nm], what)(outs)

    def big(lhs, rhs, name, col_shards, riders=(), to_chips=False):
        if distributed and to_chips:
            (chip_sum, from_chips), routs = _wgrad_pair(lhs, rhs, f"wgrad_{name}", col_shards=col_shards, riders=riders,
                                                        to_chips=True)
            red[name] = _GradReduce(name, chip_sum=_in_hbm(chip_sum))
            red[name].took_chips([_in_hbm(from_chips)])
        elif distributed:
            chip_sum, routs = _wgrad_pair(lhs, rhs, f"wgrad_{name}", col_shards=col_shards, riders=riders)
            red[name] = _GradReduce(name, chip_sum=_in_hbm(chip_sum))
        else:
            g, routs = _wgrad(lhs, rhs, f"wgrad_{name}", col_shards=col_shards)
            red[name] = _GradReduce(name, grad=g)
        return routs

    big(q1, dh4b, "w2_1", False)
    (dh3, dh3b, dp1, d_mlp_g1), routs = _bwd_mlp(dh4, h3, mlp_norm_g[1:2], p1, as_cols(w["w1_1"]), as_cols(w["w2_1"]), 1, tm=tm,
                                           riders=chips("w2_1"))
    took(routs, ("w2_1", "took_chips"))
    big(n3, dp1, "w1_1", True)
    g, routs = _wgrad(y, dh3b, "wgrad_od_out", col_shards=False, riders=share("w2_1"))
    red["od_out"] = _GradReduce("od_out", grad=g)
    took(routs, ("w2_1", "took_share"))
    (dh2, dh2b, ds, d_od_norm, d_od_bin, d_od_lng, d_od_lnb, d_ws, d_bs), routs = _bwd_odd(
        dh3, h2, od_norm, s, cdf, sv, as_cols(w["od_in"]), od_lng, od_lnb, od_w_s[0], as_rows(w["od_out"]), tm=tm,
        riders=chips("w1_1") + swap("od_out"))
    took(routs, ("w1_1", "took_chips"), ("od_out", "took_pair"))
    routs = big(n2, ds, "od_in", True, riders=share("w1_1"))
    took(routs, ("w1_1", "took_share"))
    half_groups = C_GROUPS // 2
    early = {"loss": loss_part, "od_w_s_lo": d_ws[:half_groups], "od_b_s": d_bs, "mlp_norm_g1": d_mlp_g1, "final_norm_g": d_final_g,
             "od_norm_g": d_od_norm, "od_b_in": d_od_bin, "od_ln_v_g": d_od_lng, "od_ln_v_b": d_od_lnb}
    share_early = [_ShareAll(list(early.values()))] if distributed else []
    routs = big(q0, dh2b, "w2_0", False, riders=share_early)
    landed_early = routs[0] if distributed else []
    (dh1, dh1b, dp0, d_mlp_g0), routs = _bwd_mlp(dh2, h1, mlp_norm_g[0:1], p0, as_cols(w["w1_0"]), as_cols(w["w2_0"]), 0, tm=tm,
                                           riders=chips("od_out") + chips("od_in") + chips("w2_0"))
    took(routs, ("od_out", "took_chips"), ("od_in", "took_chips"), ("w2_0", "took_chips"))
    middle = {"od_w_s_hi": d_ws[half_groups:]}
    share_middle = [_ShareAll(list(middle.values()))] if distributed else []
    g, _ = _wgrad(mix, dh1b, "wgrad_ev_out", col_shards=False)
    red["ev_out"] = _GradReduce("ev_out", grad=g)
    routs = big(n1, dp0, "w1_0", True,
                riders=share("od_out") + share("od_in") + share("w2_0") + share_middle + swap("ev_out"))
    took(routs, ("od_out", "took_share"), ("od_in", "took_share"), ("w2_0", "took_share"))
    landed_middle = routs[3] if distributed else []
    if distributed:
        red["ev_out"].took_pair(routs[4])

    (dx, dz, d_ev_norm, d_caw, d_cab, d_ev_lng, d_ev_lnb, d_cbw), routs = _bwd_even(
        dh1, x2, ev_norm_g, z, a2, cv, as_cols(w["ev_in"]), conv_a_w, ev_ln_a_g, ev_ln_a_b, conv_b_w, as_rows(w["ev_out"]),
        tm=tm, seq=seq, riders=chips("w1_0") + chips("ev_out"))
    took(routs, ("w1_0", "took_chips"), ("ev_out", "took_chips"))
    late = {"mlp_norm_g0": d_mlp_g0, "ev_norm_g": d_ev_norm, "ev_conv_a_b": d_cab, "ev_ln_a_g": d_ev_lng,
            "ev_ln_a_b": d_ev_lnb, "ev_conv_a_w": d_caw, "ev_conv_b_w": d_cbw}
    share_late = [_ShareAll(list(late.values()))] if distributed else []
    routs2 = big(n0, dz, "ev_in", True, riders=share("ev_out") + share("w1_0") + share_late, to_chips=True)
    took(routs2, ("ev_out", "took_share"), ("w1_0", "took_share"))
    own = {**early, **middle, **late}
    landed = dict(zip(own.keys(), landed_early + landed_middle + routs2[2])) if distributed else None
    return dx, red, own, landed


def _rows128(a):
    rows = jnp.reshape(a, (-1, LANES))
    pad = (-rows.shape[0]) % SUBLANES
    return jnp.pad(rows, ((0, pad), (0, 0))) if pad else rows


def _pack(arrays):
    return jnp.concatenate([_rows128(a) for a in arrays], axis=0)


def _unpack(buf, shapes):
    out, r0 = [], 0
    for shp in shapes:
        size = 1
        for dim in shp:
            size *= dim
        nr = size // LANES
        out.append(jnp.reshape(buf[r0:r0 + nr], shp))
        r0 += nr + (-nr) % SUBLANES
    return out


def kernel(x, ev_norm_g, ev_w_in, ev_conv_a_w, ev_conv_a_b, ev_ln_a_g, ev_ln_a_b, ev_conv_b_w, ev_w_out, od_norm_g, od_w_in, od_b_in, od_ln_v_g, od_ln_v_b, od_w_s, od_b_s, od_w_out, mlp_norm_g, mlp_w1, mlp_w2, final_norm_g, loss_target, m_ev_norm_g, m_ev_w_in, m_ev_conv_a_w, m_ev_conv_a_b, m_ev_ln_a_g, m_ev_ln_a_b, m_ev_conv_b_w, m_ev_w_out, m_od_norm_g, m_od_w_in, m_od_b_in, m_od_ln_v_g, m_od_ln_v_b, m_od_w_s, m_od_b_s, m_od_w_out, m_mlp_norm_g, m_mlp_w1, m_mlp_w2, m_final_norm_g, v_ev_norm_g, v_ev_w_in, v_ev_conv_a_w, v_ev_conv_a_b, v_ev_ln_a_g, v_ev_ln_a_b, v_ev_conv_b_w, v_ev_w_out, v_od_norm_g, v_od_w_in, v_od_b_in, v_od_ln_v_g, v_od_ln_v_b, v_od_w_s, v_od_b_s, v_od_w_out, v_mlp_norm_g, v_mlp_w1, v_mlp_w2, v_final_norm_g):
    tm = TOKEN_TILE
    batch, seq, d = x.shape
    tokens = batch * seq
    x2 = jnp.reshape(x, (tokens, d))
    tgt2 = jnp.reshape(loss_target, (tokens, d))
    chip = 2 * lax.axis_index("x") + lax.axis_index("y")

    small_shapes = [(A_CONV_WIDTH, LANES), (B_CONV_WIDTH, LANES), (256,), (512,), (256,), (256,)]
    small_shard = _pack([ev_conv_a_w[0], ev_conv_b_w[0], od_norm_g[0], od_b_in[0], od_ln_v_g[0], od_ln_v_b[0]])
    small_shard = jnp.pad(small_shard, ((0, (-small_shard.shape[0]) % (2 * SUBLANES)), (0, 0)))
    first = [_place_shard(ev_w_in, 0, BF16, "place_ev_w_in"), _place_shard(ev_w_out, 0, BF16, "place_ev_w_out"),
             _place_shard(small_shard[None], 0, F32, "place_small")]
    staged = {
        "w1_0": _place_shard(mlp_w1, 0, BF16, "place_w1_0"), "w2_0": _place_shard(mlp_w2, 0, BF16, "place_w2_0"),
        "od_in": _place_shard(od_w_in, 0, BF16, "place_od_w_in"), "od_out": _place_shard(od_w_out, 0, BF16, "place_od_w_out"),
        "w1_1": _place_shard(mlp_w1, 1, BF16, "place_w1_1"), "w2_1": _place_shard(mlp_w2, 1, BF16, "place_w2_1"),
    }
    first = [_in_hbm(a) for a in first]
    staged = {nm: _in_hbm(a) for nm, a in staged.items()}
    (g_ev_in, g_ev_out, g_small), = _exchange([_Gather(first)], "gather_first")
    small_all = jnp.reshape(g_small, (N_CHIPS, -1, LANES))
    per_chip = [_unpack(small_all[q], small_shapes) for q in range(N_CHIPS)]
    conv_a_w = jnp.concatenate([pc[0] for pc in per_chip], axis=1)
    conv_b_w = jnp.concatenate([pc[1] for pc in per_chip], axis=1)
    od_norm = jnp.concatenate([pc[2] for pc in per_chip])[None, :]
    od_bias = jnp.concatenate([pc[3] for pc in per_chip])[None, :]
    od_lng = jnp.concatenate([pc[4] for pc in per_chip])[None, :]
    od_lnb = jnp.concatenate([pc[5] for pc in per_chip])[None, :]

    dx, red, own, landed = _forward_backward(
        x2, tgt2, {"ev_in": g_ev_in, "ev_out": g_ev_out}, staged, conv_a_w, conv_b_w, od_norm, od_bias, od_lng, od_lnb,
        ev_norm_g, ev_conv_a_b, ev_ln_a_g, ev_ln_a_b, od_w_s, od_b_s, mlp_norm_g, final_norm_g, tm=tm, seq=seq)

    routs = _exchange([red["ev_in"].pair_share()], "reduce_tail")
    red["ev_in"].took_share(routs[0])

    given = {"ev_norm_g": (ev_norm_g, m_ev_norm_g, v_ev_norm_g), "ev_conv_a_b": (ev_conv_a_b, m_ev_conv_a_b, v_ev_conv_a_b),
             "ev_ln_a_g": (ev_ln_a_g, m_ev_ln_a_g, v_ev_ln_a_g), "ev_ln_a_b": (ev_ln_a_b, m_ev_ln_a_b, v_ev_ln_a_b),
             "od_w_s": (od_w_s, m_od_w_s, v_od_w_s), "od_b_s": (od_b_s, m_od_b_s, v_od_b_s),
             "mlp_norm_g": (mlp_norm_g, m_mlp_norm_g, v_mlp_norm_g), "final_norm_g": (final_norm_g, m_final_norm_g, v_final_norm_g),
             "ev_conv_a_w": (ev_conv_a_w, m_ev_conv_a_w, v_ev_conv_a_w), "ev_conv_b_w": (ev_conv_b_w, m_ev_conv_b_w, v_ev_conv_b_w),
             "od_norm_g": (od_norm_g, m_od_norm_g, v_od_norm_g), "od_b_in": (od_b_in, m_od_b_in, v_od_b_in),
             "od_ln_v_g": (od_ln_v_g, m_od_ln_v_g, v_od_ln_v_g), "od_ln_v_b": (od_ln_v_b, m_od_ln_v_b, v_od_ln_v_b)}
    shaped = {nm: tuple(jnp.reshape(a, shape) for a in given[nm]) for nm, shape, _, _ in SMALL_WEIGHTS}
    loss11, small_upd = _small_update(own, landed, shaped)
    loss = loss11[0, 0]
    upd = {nm: [jnp.reshape(o, given[nm][0].shape) for o in outs] for nm, outs in small_upd.items()}

    def big_update(wt, m, v, names, call):
        grads = [red[nm].reduced() for nm in names]
        shp3 = (len(grads),) + grads[0].shape
        outs, _ = _adamw(jnp.reshape(wt, shp3), jnp.reshape(m, shp3), jnp.reshape(v, shp3), grads, call)
        return [jnp.reshape(o, wt.shape) for o in outs], None

    upd["mlp_w2"], _ = big_update(mlp_w2, m_mlp_w2, v_mlp_w2, ["w2_0", "w2_1"], "adamw_mlp_w2")
    upd["mlp_w1"], _ = big_update(mlp_w1, m_mlp_w1, v_mlp_w1, ["w1_0", "w1_1"], "adamw_mlp_w1")
    upd["ev_w_in"], _ = big_update(ev_w_in, m_ev_w_in, v_ev_w_in, ["ev_in"], "adamw_ev_w_in")
    upd["ev_w_out"], _ = big_update(ev_w_out, m_ev_w_out, v_ev_w_out, ["ev_out"], "adamw_ev_w_out")
    upd["od_w_in"], _ = big_update(od_w_in, m_od_w_in, v_od_w_in, ["od_in"], "adamw_od_w_in")
    upd["od_w_out"], _ = big_update(od_w_out, m_od_w_out, v_od_w_out, ["od_out"], "adamw_od_w_out")

    order = ["ev_norm_g", "ev_w_in", "ev_conv_a_w", "ev_conv_a_b", "ev_ln_a_g", "ev_ln_a_b", "ev_conv_b_w", "ev_w_out",
             "od_norm_g", "od_w_in", "od_b_in", "od_ln_v_g", "od_ln_v_b", "od_w_s", "od_b_s", "od_w_out", "mlp_norm_g",
             "mlp_w1", "mlp_w2", "final_norm_g"]
    grad_x = jnp.reshape(dx, x.shape)
    return (loss, grad_x, *[upd[nm][0] for nm in order], *[upd[nm][1] for nm in order],
            *[upd[nm][2] for nm in order], *[upd[nm][3] for nm in order])
```

```python
import functools

import jax
import jax.numpy as jnp
from jax import lax
from jax.experimental import pallas as pl
from jax.experimental.pallas import tpu as pltpu

F32 = jnp.float32
BF16 = jnp.bfloat16

D_MODEL = 1024
A_DIM = 512
B_DIM = 512
IN_EVEN = 2 * A_DIM + 3 * B_DIM
A_CONV_WIDTH = 31
B_CONV_WIDTH = 3
CHUNK = 128
C_GROUPS = 8
C_DIM = 1024
D_FF = 4096
RMS_EPS = 1e-6
LN_EPS = 1e-5
ADAM_LR = 0.001
ADAM_B1 = 0.9
ADAM_B2 = 0.999
ADAM_EPS = 1e-08
ADAM_WD = 0.01
ADAM_STEP = 10

N_CHIPS = 4
N_DEV = 8
TOKEN_TILE = 512
A_HALO = 32
B_HALO = 8
CONV_ROWS = 16
DW_TAPS = 4
ELEM_ROWS = 16
PAIR = 2 * CHUNK
LANES = 128
SUBLANES = 8
MXU_ROWS = 256
MIB = 1024 * 1024
MESH = pl.DeviceIdType.MESH
ANY = pl.BlockSpec(memory_space=pl.ANY)


def _dot(a, b):
    return lax.dot_general(a, b, (((1,), (0,)), ((), ())), preferred_element_type=F32)


def _dot_nt(a, b):
    return lax.dot_general(a, b, (((1,), (1,)), ((), ())), preferred_element_type=F32)


def _dot_tn(a, b):
    return lax.dot_general(a, b, (((0,), (0,)), ((), ())), preferred_element_type=F32)


def _params(vmem_mib, n_axes=1):
    return pltpu.CompilerParams(dimension_semantics=("arbitrary",) * n_axes, vmem_limit_bytes=vmem_mib * MIB)


def _row_spec(tm, cols, rev_nt=None):
    if rev_nt is None:
        return pl.BlockSpec((tm, cols), lambda i: (i, 0))
    return pl.BlockSpec((tm, cols), lambda i: (rev_nt - 1 - i, 0))


def _full_spec(shape):
    nd = len(shape)
    return pl.BlockSpec(shape, lambda i: (0,) * nd)


def _block_rows(rows, cap=512):
    best = SUBLANES
    for br in range(SUBLANES, min(rows, cap) + 1, SUBLANES):
        if rows % br == 0:
            best = br
    return best


N_LOADS = 2


def _load_weights(pairs, sems):
    @pl.when(pl.program_id(0) == 0)
    def _():
        copies = [pltpu.make_async_copy(src, dst, sems.at[k]) for k, (src, dst) in enumerate(pairs)]
        for cp in copies:
            cp.start()
        for cp in copies:
            cp.wait()


def _rms_fwd(x, g):
    rstd = lax.rsqrt(jnp.mean(x * x, axis=-1, keepdims=True) + RMS_EPS)
    return x * rstd * g, rstd


def _rms_bwd(dn, x, rstd, g):
    a = dn * g
    xh = x * rstd
    dx = rstd * (a - xh * jnp.mean(a * xh, axis=-1, keepdims=True))
    dg = jnp.sum(dn * xh, axis=0, keepdims=True)
    return dx, dg


def _ln_stats(v):
    mu = jnp.mean(v, axis=-1, keepdims=True)
    xc = v - mu
    rs = lax.rsqrt(jnp.mean(xc * xc, axis=-1, keepdims=True) + LN_EPS)
    return xc * rs, rs


def _ln_bwd(dy, xhat, rs, g):
    dxh = dy * g
    dv = rs * (dxh - jnp.mean(dxh, axis=-1, keepdims=True) - xhat * jnp.mean(dxh * xhat, axis=-1, keepdims=True))
    return dv, jnp.sum(dy * xhat, axis=0, keepdims=True), jnp.sum(dy, axis=0, keepdims=True)


def _gelu_cdf(s):
    return 0.5 * (1.0 + lax.erf(s * 0.7071067811865476))


def _mesh_pos():
    return lax.axis_index("x"), lax.axis_index("y"), lax.axis_index("c")


def _other_chips(x, y):
    return [(1 - x, y), (x, 1 - y), (1 - x, 1 - y)]


def _remote(src, dst, send_sem, recv_sem, to):
    return pltpu.make_async_remote_copy(src_ref=src, dst_ref=dst, send_sem=send_sem, recv_sem=recv_sem,
                                        device_id=to, device_id_type=MESH)


def _like(arrays):
    return [jax.ShapeDtypeStruct(a.shape, a.dtype) for a in arrays]


class _Gather:
    def __init__(self, bufs):
        self.ins = list(bufs)
        self.out_shapes = _like(bufs)
        self.aliases = {t: t for t in range(len(bufs))}
        self.n_sems = 6 * len(bufs)

    def _ici(self, ins, outs, send, recv, t, k, chip, mine, c):
        return _remote(ins[t].at[mine, c], outs[t].at[mine, c], send.at[6 * t + k], recv.at[6 * t + k], (*chip, c))

    def start(self, ins, outs, send, recv):
        x, y, c = _mesh_pos()
        for t in range(len(ins)):
            for k, chip in enumerate(_other_chips(x, y)):
                self._ici(ins, outs, send, recv, t, k, chip, 2 * x + y, c).start()

    def _pass_on(self, outs, send, recv, t, k, chip, c, to):
        blk = outs[t].at[2 * chip[0] + chip[1], c]
        return _remote(blk, blk, send.at[6 * t + 3 + k], recv.at[6 * t + 3 + k], to)

    def near_end(self, ins, outs, send, recv):
        x, y, c = _mesh_pos()
        for t in range(len(ins)):
            for k, chip in enumerate(_other_chips(x, y)):
                blk = outs[t].at[2 * chip[0] + chip[1], c]
                _remote(blk, blk, send.at[6 * t + k], recv.at[6 * t + k], (x, y, c)).wait_recv()
                self._pass_on(outs, send, recv, t, k, chip, c, (x, y, 1 - c)).start()

    def finish(self, ins, outs, send, recv):
        x, y, c = _mesh_pos()
        chips = _other_chips(x, y)
        for t in range(len(ins)):
            for k, chip in enumerate(chips):
                self._pass_on(outs, send, recv, t, k, chip, 1 - c, (x, y, c)).wait_recv()
        for t in range(len(ins)):
            for k, chip in enumerate(chips):
                self._ici(ins, outs, send, recv, t, k, chip, 2 * x + y, c).wait_send()
                self._pass_on(outs, send, recv, t, k, chip, c, (x, y, 1 - c)).wait_send()


class _PairSwap:
    def __init__(self, grads):
        self.ins = list(grads)
        self.out_shapes = [jax.ShapeDtypeStruct((g.shape[0],) + g.shape[2:], g.dtype) for g in grads]
        self.aliases = {}
        self.n_sems = len(grads)

    def _copies(self, ins, outs, send, recv):
        x, y, c = _mesh_pos()
        return [_remote(ins[t].at[:, 1 - c], outs[t], send.at[t], recv.at[t], (x, y, 1 - c)) for t in range(len(ins))]

    def start(self, ins, outs, send, recv):
        for cp in self._copies(ins, outs, send, recv):
            cp.start()

    def finish(self, ins, outs, send, recv):
        for cp in self._copies(ins, outs, send, recv):
            cp.wait()


class _ChipSwap:
    def __init__(self, parts):
        self.ins = list(parts)
        self.out_shapes = [jax.ShapeDtypeStruct((3,) + p.shape[1:], p.dtype) for p in parts]
        self.aliases = {}
        self.n_sems = 3 * len(parts)

    def _copies(self, ins, outs, send, recv):
        x, y, c = _mesh_pos()
        return [_remote(ins[t].at[2 * chip[0] + chip[1]], outs[t].at[k], send.at[3 * t + k], recv.at[3 * t + k], (*chip, c))
                for t in range(len(ins)) for k, chip in enumerate(_other_chips(x, y))]

    def start(self, ins, outs, send, recv):
        for cp in self._copies(ins, outs, send, recv):
            cp.start()

    def finish(self, ins, outs, send, recv):
        for cp in self._copies(ins, outs, send, recv):
            cp.wait()


class _PairShare:
    def __init__(self, fulls):
        self.ins = list(fulls)
        self.out_shapes = _like(fulls)
        self.aliases = {t: t for t in range(len(fulls))}
        self.n_sems = len(fulls)

    def _copies(self, ins, outs, send, recv):
        x, y, c = _mesh_pos()
        return [_remote(ins[t].at[c], outs[t].at[c], send.at[t], recv.at[t], (x, y, 1 - c)) for t in range(len(ins))]

    def start(self, ins, outs, send, recv):
        for cp in self._copies(ins, outs, send, recv):
            cp.start()

    def finish(self, ins, outs, send, recv):
        for cp in self._copies(ins, outs, send, recv):
            cp.wait()


class _ShareAll:
    def __init__(self, arrays):
        self.ins = list(arrays)
        self.out_shapes = [jax.ShapeDtypeStruct((N_DEV,) + a.shape, a.dtype) for a in arrays]
        self.aliases = {}
        self.n_sems = (N_DEV - 1) * len(arrays)

    def _peers(self):
        x, y, c = _mesh_pos()
        flips = [((r >> 2) & 1, (r >> 1) & 1, r & 1) for r in range(1, N_DEV)]
        return (x, y, c), [(x ^ fx, y ^ fy, c ^ fc) for fx, fy, fc in flips]

    def _sends(self, ins, outs, send, recv):
        (x, y, c), peers = self._peers()
        mine = 4 * x + 2 * y + c
        return [_remote(ins[a], outs[a].at[mine], send.at[7 * a + r], recv.at[7 * a + r], peer)
                for a in range(len(ins)) for r, peer in enumerate(peers)]

    def start(self, ins, outs, send, recv):
        for cp in self._sends(ins, outs, send, recv):
            cp.start()

    def finish(self, ins, outs, send, recv):
        (x, y, c), peers = self._peers()
        for a in range(len(ins)):
            for r, (px, py, pc) in enumerate(peers):
                blk = outs[a].at[4 * px + 2 * py + pc]
                _remote(blk, blk, send.at[7 * a + r], recv.at[7 * a + r], (x, y, c)).wait_recv()
        for cp in self._sends(ins, outs, send, recv):
            cp.wait_send()


def _pallas(body, operands, *, name, grid, in_specs, out_specs, out_shape, scratch_shapes=(), vmem_mib=32, riders=(),
            prefetch=None):
    in_specs, out_specs, out_shape, scratch_shapes = list(in_specs), list(out_specs), list(out_shape), list(scratch_shapes)
    if not riders and prefetch is None:
        outs = pl.pallas_call(body, name=name, grid=grid, in_specs=in_specs, out_specs=out_specs, out_shape=out_shape,
                              scratch_shapes=scratch_shapes, compiler_params=_params(vmem_mib, len(grid)))(*operands)
        return list(outs), []
    n_in, n_out, n_scr = len(in_specs), len(out_specs), len(scratch_shapes)
    r_in = [len(r.ins) for r in riders]
    r_out = [len(r.out_shapes) for r in riders]
    steps = 1
    for g in grid:
        steps *= g

    n_pre = 0 if prefetch is None else 1

    def wrapped(*refs):
        refs = list(refs)
        pre, refs = refs[:n_pre], refs[n_pre:]
        ins, refs = refs[:n_in], refs[n_in:]
        rins = []
        for k in r_in:
            rins.append(refs[:k])
            refs = refs[k:]
        outs, refs = refs[:n_out], refs[n_out:]
        routs = []
        for k in r_out:
            routs.append(refs[:k])
            refs = refs[k:]
        scr, sems = refs[:n_scr], refs[n_scr:]
        step = 0
        for ax, g in enumerate(grid):
            step = step * g + pl.program_id(ax)

        def each(what):
            for j, r in enumerate(riders):
                if hasattr(r, what):
                    getattr(r, what)(rins[j], routs[j], sems[2 * j], sems[2 * j + 1])

        if grid:
            pl.when(step == 0)(lambda: each("start"))
        else:
            each("start")
        body(*pre, *ins, *outs, *scr)
        if grid:
            @pl.when(step == steps - 1)
            def _():
                each("near_end")
                each("finish")
        else:
            each("near_end")
            each("finish")

    aliases, off_in, off_out = {}, n_pre + n_in, n_out
    for r, ki, ko in zip(riders, r_in, r_out):
        for i, o in r.aliases.items():
            aliases[off_in + i] = off_out + o
        off_in, off_out = off_in + ki, off_out + ko
    sems = []
    for r in riders:
        sems += [pltpu.SemaphoreType.DMA((r.n_sems,)), pltpu.SemaphoreType.DMA((r.n_sems,))]
    layout = dict(grid=grid, in_specs=in_specs + [ANY] * sum(r_in), out_specs=out_specs + [ANY] * sum(r_out),
                  scratch_shapes=scratch_shapes + sems)
    if prefetch is not None:
        layout = dict(grid_spec=pltpu.PrefetchScalarGridSpec(num_scalar_prefetch=1, **layout))
    res = pl.pallas_call(
        wrapped, name=name, **layout,
        out_shape=out_shape + [s for r in riders for s in r.out_shapes], input_output_aliases=aliases,
        compiler_params=pltpu.CompilerParams(dimension_semantics=("arbitrary",) * len(grid),
                                             vmem_limit_bytes=vmem_mib * MIB, has_side_effects=True),
    )(*([] if prefetch is None else [prefetch]), *operands, *[a for r in riders for a in r.ins])
    res = list(res)
    outs, res = res[:n_out], res[n_out:]
    routs = []
    for k in r_out:
        routs.append(res[:k])
        res = res[k:]
    return outs, routs


def _exchange(riders, name):
    return _pallas(lambda: None, [], name=name, grid=(), in_specs=[], out_specs=[], out_shape=[], riders=riders)[1]


def _in_hbm(a):
    return pltpu.with_memory_space_constraint(a, pltpu.HBM)


def _place_shard(w, layer, dtype, name):
    _, rows, cols = w.shape
    half = rows // 2
    br = _block_rows(half)
    nb = half // br
    mine = 2 * lax.axis_index("x") + lax.axis_index("y")

    def body(q_ref, w_ref, o_ref):
        o_ref[...] = w_ref[...].astype(dtype)

    return pl.pallas_call(
        body, name=name,
        grid_spec=pltpu.PrefetchScalarGridSpec(
            num_scalar_prefetch=1, grid=(2, nb),
            in_specs=[pl.BlockSpec((None, br, cols), lambda h, i, q: (layer, h * nb + i, 0))],
            out_specs=pl.BlockSpec((None, None, br, cols), lambda h, i, q: (q[0], h, i, 0))),
        out_shape=pltpu.HBM((N_CHIPS, 2, half, cols), dtype),
        compiler_params=_params(16, 2),
    )(jnp.reshape(mine, (1,)).astype(jnp.int32), w)


def _add_pair(g, recv, name):
    _, _, r, cdim = g.shape
    br = _block_rows(r, 256)
    c = lax.axis_index("c")

    def body(c_ref, g_ref, r_ref, o_ref):
        o_ref[...] = (g_ref[...] + r_ref[...]).astype(BF16)

    return pl.pallas_call(
        body, name=name,
        grid_spec=pltpu.PrefetchScalarGridSpec(
            num_scalar_prefetch=1, grid=(N_CHIPS, r // br),
            in_specs=[pl.BlockSpec((None, None, br, cdim), lambda q, i, c_ref: (q, c_ref[0], i, 0)),
                      pl.BlockSpec((None, br, cdim), lambda q, i, c_ref: (q, i, 0))],
            out_specs=pl.BlockSpec((None, br, cdim), lambda q, i, c_ref: (q, i, 0))),
        out_shape=pltpu.HBM((N_CHIPS, r, cdim), BF16),
        compiler_params=_params(16, 2),
    )(jnp.reshape(c, (1,)).astype(jnp.int32), _in_hbm(g), _in_hbm(recv))


def _add_chips(own, recv, name):
    _, r, cdim = own.shape
    br = _block_rows(r, 256)
    x, y, c = _mesh_pos()

    def body(pos_ref, own_ref, r_ref, o_ref):
        acc = own_ref[...].astype(F32)
        for k in range(3):
            acc = acc + r_ref[k].astype(F32)
        o_ref[...] = acc

    return pl.pallas_call(
        body, name=name,
        grid_spec=pltpu.PrefetchScalarGridSpec(
            num_scalar_prefetch=1, grid=(r // br,),
            in_specs=[pl.BlockSpec((None, br, cdim), lambda i, pos: (pos[0], i, 0)),
                      pl.BlockSpec((3, br, cdim), lambda i, pos: (0, i, 0))],
            out_specs=pl.BlockSpec((None, br, cdim), lambda i, pos: (pos[1], i, 0))),
        out_shape=pltpu.HBM((2, r, cdim), F32),
        compiler_params=_params(16, 1),
    )(jnp.stack([2 * x + y, c]).astype(jnp.int32), _in_hbm(own), _in_hbm(recv))


def _adam_math(w, m, v, g):
    c1 = 1.0 / (1.0 - ADAM_B1 ** ADAM_STEP)
    c2 = 1.0 / (1.0 - ADAM_B2 ** ADAM_STEP)
    m_new = ADAM_B1 * m + (1.0 - ADAM_B1) * g
    v_new = ADAM_B2 * v + (1.0 - ADAM_B2) * (g * g)
    return -ADAM_LR * ((m_new * c1) / (jnp.sqrt(v_new * c2) + ADAM_EPS) + ADAM_WD * w), m_new, v_new


SMALL_WEIGHTS = [
    ("ev_norm_g", (1, D_MODEL), ["ev_norm_g"], None), ("ev_conv_a_b", (1, A_DIM), ["ev_conv_a_b"], None),
    ("ev_ln_a_g", (1, A_DIM), ["ev_ln_a_g"], None), ("ev_ln_a_b", (1, A_DIM), ["ev_ln_a_b"], None),
    ("od_w_s", (C_GROUPS, CHUNK, CHUNK), ["od_w_s_lo", "od_w_s_hi"], None), ("od_b_s", (C_GROUPS, CHUNK), ["od_b_s"], None),
    ("mlp_norm_g", (2, D_MODEL), ["mlp_norm_g0", "mlp_norm_g1"], None), ("final_norm_g", (1, D_MODEL), ["final_norm_g"], None),
    ("ev_conv_a_w", (A_CONV_WIDTH, A_DIM // N_CHIPS), ["ev_conv_a_w"], A_DIM // N_CHIPS),
    ("ev_conv_b_w", (B_CONV_WIDTH, B_DIM // N_CHIPS), ["ev_conv_b_w"], B_DIM // N_CHIPS),
    ("od_norm_g", (1, D_MODEL // N_CHIPS), ["od_norm_g"], D_MODEL // N_CHIPS),
    ("od_b_in", (1, 2 * C_DIM // N_CHIPS), ["od_b_in"], 2 * C_DIM // N_CHIPS),
    ("od_ln_v_g", (1, C_DIM // N_CHIPS), ["od_ln_v_g"], C_DIM // N_CHIPS),
    ("od_ln_v_b", (1, C_DIM // N_CHIPS), ["od_ln_v_b"], C_DIM // N_CHIPS),
]


def _small_update(own, landed, weights):
    names = list(own.keys())
    n_g, n_w = len(names), len(SMALL_WEIGHTS)

    def body(*refs):
        refs = list(refs)
        own_refs = dict(zip(names, refs[:n_g]))
        land_refs = dict(zip(names, refs[n_g:2 * n_g]))
        wmv = [refs[2 * n_g + 3 * i:2 * n_g + 3 * i + 3] for i in range(n_w)]
        o0 = 2 * n_g + 3 * n_w
        loss_ref = refs[o0]
        outs = [refs[o0 + 1 + 4 * i:o0 + 5 + 4 * i] for i in range(n_w)]
        acc = dict(zip(names, refs[o0 + 1 + 4 * n_w:]))
        x, y, c = _mesh_pos()
        mine, chip = 4 * x + 2 * y + c, 2 * x + y

        for nm in names:
            for d in range(N_DEV):
                def add(term, nm=nm, d=d):
                    acc[nm][...] = term if d == 0 else acc[nm][...] + term
                pl.when(mine == d)(lambda nm=nm, add=add: add(own_refs[nm][...]))
                pl.when(mine != d)(lambda nm=nm, d=d, add=add: add(land_refs[nm][d]))
        loss_ref[...] = acc["loss"][...]

        def update(i, rows, g):
            w_ref, m_ref, v_ref = wmv[i]
            delta, m_new, v_new = _adam_math(w_ref[rows], m_ref[rows], v_ref[rows], g)
            for ref, val in zip(outs[i], (g, delta, m_new, v_new)):
                ref[rows] = val

        for i, (_, shape, grads, per_chip) in enumerate(SMALL_WEIGHTS):
            for row, gname in enumerate(grads):
                per_grad = shape[0] // len(grads)
                rows = slice(row * per_grad, (row + 1) * per_grad)
                if per_chip is None:
                    update(i, rows, acc[gname][...])
                else:
                    for q in range(N_CHIPS):
                        pl.when(chip == q)(lambda i=i, rows=rows, gname=gname, q=q, per_chip=per_chip:
                                           update(i, rows, acc[gname][:, q * per_chip:(q + 1) * per_chip]))

    operands = [own[nm] for nm in names] + [landed[nm] for nm in names]
    for nm, _, _, _ in SMALL_WEIGHTS:
        operands += list(weights[nm])
    out_shape = [jax.ShapeDtypeStruct((1, 1), F32)]
    for _, shape, _, _ in SMALL_WEIGHTS:
        out_shape += [jax.ShapeDtypeStruct(shape, F32)] * 4
    res = pl.pallas_call(
        body, name="small_update", grid=(1,),
        in_specs=[_full_spec(a.shape) for a in operands], out_specs=[_full_spec(s.shape) for s in out_shape],
        out_shape=out_shape, scratch_shapes=[pltpu.VMEM(own[nm].shape, F32) for nm in names],
        compiler_params=_params(32, 1),
    )(*[_in_hbm(a) for a in operands])
    return res[0], {nm: res[1 + 4 * i:5 + 4 * i] for i, (nm, _, _, _) in enumerate(SMALL_WEIGHTS)}


def _adamw(w, m, v, grads, name, riders=()):
    layers, r, cdim = w.shape
    br = _block_rows(r, 256 if cdim > LANES else 1024)

    def body(*refs):
        w_ref, m_ref, v_ref = refs[:3]
        g_refs = refs[3:3 + layers]
        go_ref, d_ref, mo_ref, vo_ref = refs[3 + layers:]
        layer = pl.program_id(0)
        for l in range(layers):
            @pl.when(layer == l)
            def _(l=l):
                g = g_refs[l][...]
                go_ref[...] = g
                d_ref[...], mo_ref[...], vo_ref[...] = _adam_math(w_ref[...], m_ref[...], v_ref[...], g)

    spec3 = pl.BlockSpec((None, br, cdim), lambda l, i: (l, i, 0))
    spec2 = pl.BlockSpec((br, cdim), lambda l, i: (i, 0))
    out = jax.ShapeDtypeStruct((layers, r, cdim), F32)
    return _pallas(body, [w, m, v, *[_in_hbm(g) for g in grads]], name=name, grid=(layers, r // br),
                   in_specs=[spec3, spec3, spec3] + [spec2] * layers, out_specs=[spec3] * 4, out_shape=[out] * 4,
                   vmem_mib=32, riders=riders)


def _fill_shifted(buf, rows):
    for b in range(1, SUBLANES):
        buf[b, 0:rows - SUBLANES, :] = buf[0, b:b + rows - SUBLANES, :]


def _window(buf, start, size):
    return buf[start % SUBLANES, start - start % SUBLANES:start - start % SUBLANES + size, :]


def _conv31(src, w_ref, r0, base, init):
    acc = init
    for k in range(A_CONV_WIDTH):
        acc = acc + w_ref[k:k + 1, :] * _window(src, base + k + r0, CONV_ROWS)
    return acc


def _fwd_even(x, norm_g, w_in, conv_a_w, conv_a_b, ln_g, ln_b, conv_b_w, w_out, *, tm, seq, riders=()):
    tokens = x.shape[0]
    nt, tps = tokens // tm, seq // tm

    def body(x_ref, g_ref, win_hbm, caw_ref, cab_ref, lng_ref, lnb_ref, cbw_ref, wout_hbm,
             h_ref, n_ref, z_ref, a2_ref, cv_ref, mix_ref, win_v, wout_v, pa, pb, sem):
        i = pl.program_id(0)

        _load_weights([(win_hbm, win_v), (wout_hbm, wout_v)], sem)

        xv = x_ref[...]
        nf, _ = _rms_fwd(xv, g_ref[...])
        n = nf.astype(BF16)
        n_ref[...] = n
        z = jnp.concatenate([_dot(n, win_v[j]) for j in range(N_CHIPS)], axis=1)
        z_ref[...] = z.astype(BF16)
        a_val, a_gate = z[:, 0:A_DIM], z[:, A_DIM:2 * A_DIM]
        b_gate, c_gate, b_val = z[:, 1024:1536], z[:, 1536:2048], z[:, 2048:2560]

        first = (i % tps) == 0

        @pl.when(first)
        def _():
            pa[0, 0:A_HALO, :] = jnp.zeros((A_HALO, A_DIM), F32)
            pb[0:B_HALO, :] = jnp.zeros((B_HALO, B_DIM), F32)

        @pl.when(jnp.logical_not(first))
        def _():
            pa[0, 0:A_HALO, :] = pa[0, tm:tm + A_HALO, :]
            pb[0:B_HALO, :] = pb[tm:tm + B_HALO, :]

        pa[0, A_HALO:A_HALO + tm, :] = a_val * jax.nn.sigmoid(a_gate)
        pb[B_HALO:B_HALO + tm, :] = c_gate * b_val
        _fill_shifted(pa, A_HALO + tm)
        bias = jnp.broadcast_to(cab_ref[...], (CONV_ROWS, A_DIM))
        for r0 in range(0, tm, CONV_ROWS):
            a2_ref[r0:r0 + CONV_ROWS, :] = _conv31(pa, caw_ref, r0, A_HALO - (A_CONV_WIDTH - 1), bias)
        xhat, _ = _ln_stats(a2_ref[...])
        a3 = xhat * lng_ref[...] + lnb_ref[...]
        a4 = a3 * jax.nn.sigmoid(a3)
        cv = cbw_ref[0:1, :] * pb[B_HALO - 2:B_HALO - 2 + tm, :]
        cv = cv + cbw_ref[1:2, :] * pb[B_HALO - 1:B_HALO - 1 + tm, :]
        cv = cv + cbw_ref[2:3, :] * pb[B_HALO:B_HALO + tm, :]
        cv_ref[...] = cv.astype(BF16)
        mix = jnp.concatenate([a4, b_gate * cv], axis=1).astype(BF16)
        mix_ref[...] = mix
        h_ref[...] = xv + _dot(mix, wout_v[...])

    shp = lambda cols, dt: jax.ShapeDtypeStruct((tokens, cols), dt)
    return _pallas(
        body, [x, norm_g, w_in, conv_a_w, conv_a_b, ln_g, ln_b, conv_b_w, w_out], name="fwd_even", grid=(nt,),
        in_specs=[_row_spec(tm, D_MODEL), _full_spec((1, D_MODEL)), ANY, _full_spec((A_CONV_WIDTH, A_DIM)),
                  _full_spec((1, A_DIM)), _full_spec((1, A_DIM)), _full_spec((1, A_DIM)),
                  _full_spec((B_CONV_WIDTH, B_DIM)), ANY],
        out_specs=[_row_spec(tm, D_MODEL), _row_spec(tm, D_MODEL), _row_spec(tm, IN_EVEN), _row_spec(tm, A_DIM),
                   _row_spec(tm, B_DIM), _row_spec(tm, D_MODEL)],
        out_shape=[shp(D_MODEL, F32), shp(D_MODEL, BF16), shp(IN_EVEN, BF16), shp(A_DIM, F32), shp(B_DIM, BF16),
                   shp(D_MODEL, BF16)],
        scratch_shapes=[pltpu.VMEM((N_CHIPS, D_MODEL, IN_EVEN // N_CHIPS), BF16), pltpu.VMEM((D_MODEL, D_MODEL), BF16),
                        pltpu.VMEM((SUBLANES, A_HALO + tm, A_DIM), F32), pltpu.VMEM((B_HALO + tm, B_DIM), F32),
                        pltpu.SemaphoreType.DMA((N_LOADS,))],
        vmem_mib=56, riders=riders)


def _loss_tail(xv, g, target, loss_ref, dh_ref, dhb_ref, dg_ref):
    @pl.when(pl.program_id(0) == 0)
    def _():
        loss_ref[...] = jnp.zeros((1, 1), F32)
        dg_ref[...] = jnp.zeros((1, D_MODEL), F32)

    out, rstd = _rms_fwd(xv, g)
    err = out - target
    per_token = jnp.sum(err * err, axis=1, keepdims=True) * (1.0 / D_MODEL)
    loss_ref[...] += 0.5 * jnp.sum(per_token, axis=0, keepdims=True)
    dx, dg = _rms_bwd(err * (1.0 / D_MODEL), xv, rstd, g)
    dh_ref[...] = dx
    dhb_ref[...] = dx.astype(BF16)
    dg_ref[...] += dg


def _fwd_mlp(h, norm_g, w1, w2, layer, *, tm, riders=(), head=None):
    tokens = h.shape[0]
    nt = tokens // tm
    fs = D_FF // N_CHIPS
    n_in = 4 if head is None else 6

    def body(*refs):
        h_ref, g_ref, w1_hbm, w2_hbm = refs[:4]
        w1_v, w2_v, sem = refs[-3:]
        outs = refs[n_in:-3]
        n_ref, p_ref, q_ref = outs[1:4] if head is None else outs[0:3]
        _load_weights([(w1_hbm, w1_v), (w2_hbm, w2_v)], sem)

        xv = h_ref[...]
        nf, _ = _rms_fwd(xv, g_ref[...])
        n = nf.astype(BF16)
        n_ref[...] = n
        acc = xv
        for j in range(N_CHIPS):
            p = _dot(n, w1_v[j])
            p_ref[:, j * fs:(j + 1) * fs] = p.astype(BF16)
            r = jnp.maximum(p, 0.0)
            q = (r * r).astype(BF16)
            q_ref[:, j * fs:(j + 1) * fs] = q
            acc = acc + _dot(q, w2_v[j])
        if head is None:
            outs[0][...] = acc
        else:
            _loss_tail(acc, refs[4][...], refs[5][...], *outs[3:7])

    shp = lambda cols, dt: jax.ShapeDtypeStruct((tokens, cols), dt)
    saved_specs = [_row_spec(tm, D_MODEL), _row_spec(tm, D_FF), _row_spec(tm, D_FF)]
    saved_shapes = [shp(D_MODEL, BF16), shp(D_FF, BF16), shp(D_FF, BF16)]
    if head is None:
        operands, in_specs = [h, norm_g, w1, w2], [_row_spec(tm, D_MODEL), _full_spec((1, D_MODEL)), ANY, ANY]
        out_specs, out_shape = [_row_spec(tm, D_MODEL)] + saved_specs, [shp(D_MODEL, F32)] + saved_shapes
    else:
        operands = [h, norm_g, w1, w2, *head]
        in_specs = [_row_spec(tm, D_MODEL), _full_spec((1, D_MODEL)), ANY, ANY, _full_spec((1, D_MODEL)), _row_spec(tm, D_MODEL)]
        out_specs = saved_specs + [_full_spec((1, 1)), _row_spec(tm, D_MODEL), _row_spec(tm, D_MODEL), _full_spec((1, D_MODEL))]
        out_shape = saved_shapes + [jax.ShapeDtypeStruct((1, 1), F32), shp(D_MODEL, F32), shp(D_MODEL, BF16),
                                    jax.ShapeDtypeStruct((1, D_MODEL), F32)]
    return _pallas(
        body, operands, name=f"fwd_mlp{layer}", grid=(nt,), in_specs=in_specs, out_specs=out_specs, out_shape=out_shape,
        scratch_shapes=[pltpu.VMEM((N_CHIPS, D_MODEL, fs), BF16), pltpu.VMEM((N_CHIPS, fs, D_MODEL), BF16),
                        pltpu.SemaphoreType.DMA((N_LOADS,))],
        vmem_mib=56, riders=riders)


def _tril_mask():
    row = lax.broadcasted_iota(jnp.int32, (CHUNK, CHUNK), 0)
    col = lax.broadcasted_iota(jnp.int32, (CHUNK, CHUNK), 1)
    return row >= col


def _triu_mask():
    row = lax.broadcasted_iota(jnp.int32, (CHUNK, CHUNK), 0)
    col = lax.broadcasted_iota(jnp.int32, (CHUNK, CHUNK), 1)
    return row <= col


def _fwd_odd(h, norm_g, w_in, b_in, ln_g, ln_b, w_s, b_s_rows, w_out, *, tm, riders=()):
    tokens = h.shape[0]
    nt = tokens // tm
    cs = 2 * C_DIM // N_CHIPS

    def body(h_ref, g_ref, win_hbm, bin_ref, lng_ref, lnb_ref, ws_ref, bs_ref, wout_hbm,
             ho_ref, n_ref, s_ref, cdf_ref, sv_ref, y_ref, win_v, wout_v, bd, sem):
        _load_weights([(win_hbm, win_v), (wout_hbm, wout_v)], sem)

        @pl.when(pl.program_id(0) == 0)
        def _():
            mask = _tril_mask()
            bd[...] = jnp.zeros(bd.shape, BF16)
            for g in range(C_GROUPS):
                w = jnp.where(mask, ws_ref[g], 0.0).astype(BF16)
                bd[g, 0:CHUNK, 0:CHUNK] = w
                bd[g, CHUNK:PAIR, CHUNK:PAIR] = w

        xv = h_ref[...]
        nf, _ = _rms_fwd(xv, g_ref[...])
        n = nf.astype(BF16)
        n_ref[...] = n
        s = jnp.concatenate([_dot(n, win_v[j]) for j in range(N_CHIPS)], axis=1) + bin_ref[...]
        s_ref[...] = s.astype(BF16)
        cdf = _gelu_cdf(s)
        cdf_ref[...] = cdf.astype(BF16)
        zz = s * cdf
        u, v = zz[:, 0:C_DIM], zz[:, C_DIM:2 * C_DIM]
        xhat, _ = _ln_stats(v)
        vn = (xhat * lng_ref[...] + lnb_ref[...]).astype(BF16)
        for g in range(C_GROUPS):
            cols = slice(g * CHUNK, (g + 1) * CHUNK)
            bias = jnp.concatenate([bs_ref[g], bs_ref[g]], axis=0)
            for r0 in range(0, tm, PAIR):
                sv = _dot(bd[g], vn[r0:r0 + PAIR, cols]) + bias
                sv_ref[r0:r0 + PAIR, cols] = sv.astype(BF16)
                y_ref[r0:r0 + PAIR, cols] = (u[r0:r0 + PAIR, cols] * sv).astype(BF16)
        ho_ref[...] = xv + _dot(y_ref[...], wout_v[...])

    shp = lambda cols, dt: jax.ShapeDtypeStruct((tokens, cols), dt)
    return _pallas(
        body, [h, norm_g, w_in, b_in, ln_g, ln_b, w_s, b_s_rows, w_out], name="fwd_odd", grid=(nt,),
        in_specs=[_row_spec(tm, D_MODEL), _full_spec((1, D_MODEL)), ANY, _full_spec((1, 2 * C_DIM)),
                  _full_spec((1, C_DIM)), _full_spec((1, C_DIM)), _full_spec((C_GROUPS, CHUNK, CHUNK)),
                  _full_spec((C_GROUPS, CHUNK, CHUNK)), ANY],
        out_specs=[_row_spec(tm, D_MODEL), _row_spec(tm, D_MODEL), _row_spec(tm, 2 * C_DIM), _row_spec(tm, 2 * C_DIM),
                   _row_spec(tm, C_DIM), _row_spec(tm, C_DIM)],
        out_shape=[shp(D_MODEL, F32), shp(D_MODEL, BF16), shp(2 * C_DIM, BF16), shp(2 * C_DIM, BF16), shp(C_DIM, BF16),
                   shp(C_DIM, BF16)],
        scratch_shapes=[pltpu.VMEM((N_CHIPS, D_MODEL, cs), BF16), pltpu.VMEM((C_DIM, D_MODEL), BF16),
                        pltpu.VMEM((C_GROUPS, PAIR, PAIR), BF16), pltpu.SemaphoreType.DMA((N_LOADS,))],
        vmem_mib=56, riders=riders)


def _bwd_mlp(dh, h, norm_g, p, w1, w2, layer, *, tm, riders=()):
    tokens = h.shape[0]
    nt = tokens // tm
    fs = D_FF // N_CHIPS

    def body(dh_ref, h_ref, g_ref, p_ref, w1_hbm, w2_hbm, dx_ref, dxb_ref, dp_ref, dg_ref, w1_v, w2_v, sem):
        @pl.when(pl.program_id(0) == 0)
        def _():
            dg_ref[...] = jnp.zeros((1, D_MODEL), F32)

        _load_weights([(w1_hbm, w1_v), (w2_hbm, w2_v)], sem)

        dhv = dh_ref[...]
        dhb = dhv.astype(BF16)
        dn = jnp.zeros((tm, D_MODEL), F32)
        for j in range(N_CHIPS):
            dq = _dot_nt(dhb, w2_v[j])
            r = jnp.maximum(p_ref[:, j * fs:(j + 1) * fs].astype(F32), 0.0)
            dp = ((2.0 * r) * dq).astype(BF16)
            dp_ref[:, j * fs:(j + 1) * fs] = dp
            dn = dn + _dot_nt(dp, w1_v[j])
        xv = h_ref[...]
        g = g_ref[...]
        _, rstd = _rms_fwd(xv, g)
        dx, dg = _rms_bwd(dn, xv, rstd, g)
        dx_ref[...] = dhv + dx
        dxb_ref[...] = (dhv + dx).astype(BF16)
        dg_ref[...] += dg

    return _pallas(
        body, [dh, h, norm_g, p, w1, w2], name=f"bwd_mlp{layer}", grid=(nt,),
        in_specs=[_row_spec(tm, D_MODEL), _row_spec(tm, D_MODEL), _full_spec((1, D_MODEL)), _row_spec(tm, D_FF), ANY, ANY],
        out_specs=[_row_spec(tm, D_MODEL), _row_spec(tm, D_MODEL), _row_spec(tm, D_FF), _full_spec((1, D_MODEL))],
        out_shape=[jax.ShapeDtypeStruct((tokens, D_MODEL), F32), jax.ShapeDtypeStruct((tokens, D_MODEL), BF16),
                   jax.ShapeDtypeStruct((tokens, D_FF), BF16), jax.ShapeDtypeStruct((1, D_MODEL), F32)],
        scratch_shapes=[pltpu.VMEM((N_CHIPS, D_MODEL, fs), BF16), pltpu.VMEM((N_CHIPS, fs, D_MODEL), BF16),
                        pltpu.SemaphoreType.DMA((N_LOADS,))],
        vmem_mib=56, riders=riders)


def _bwd_odd(dh, h, norm_g, s, cdf, sv, w_in, ln_g, ln_b, w_s, w_out, *, tm, riders=()):
    tokens = h.shape[0]
    nt = tokens // tm
    cs = 2 * C_DIM // N_CHIPS

    def body(dh_ref, h_ref, g_ref, s_ref, cdf_ref, sv_ref, win_hbm, lng_ref, lnb_ref, ws_ref, wout_hbm,
             dx_ref, dxb_ref, ds_ref, dg_ref, dbin_ref, dlng_ref, dlnb_ref, dws_ref, dbs_ref,
             win_v, wout_v, bdt, dws_acc, dbs_acc, dvn, sem):
        i = pl.program_id(0)

        _load_weights([(win_hbm, win_v), (wout_hbm, wout_v)], sem)

        @pl.when(i == 0)
        def _():
            mask_t = _triu_mask()
            bdt[...] = jnp.zeros(bdt.shape, BF16)
            for g in range(C_GROUPS):
                wt = jnp.where(mask_t, ws_ref[g].T, 0.0).astype(BF16)
                bdt[g, 0:CHUNK, 0:CHUNK] = wt
                bdt[g, CHUNK:PAIR, CHUNK:PAIR] = wt
            dws_acc[...] = jnp.zeros(dws_acc.shape, F32)
            dbs_acc[...] = jnp.zeros(dbs_acc.shape, F32)
            dg_ref[...] = jnp.zeros(dg_ref.shape, F32)
            dbin_ref[...] = jnp.zeros(dbin_ref.shape, F32)
            dlng_ref[...] = jnp.zeros(dlng_ref.shape, F32)
            dlnb_ref[...] = jnp.zeros(dlnb_ref.shape, F32)

        dhv = dh_ref[...]
        dy = _dot_nt(dhv.astype(BF16), wout_v[...])
        sf = s_ref[...].astype(F32)
        cdf = cdf_ref[...].astype(F32)
        pdf = jnp.exp(-0.5 * sf * sf) * 0.3989422804014327
        zz = sf * cdf
        dgelu = cdf + sf * pdf
        u, v = zz[:, 0:C_DIM], zz[:, C_DIM:2 * C_DIM]
        xhat, rs = _ln_stats(v)
        lng = lng_ref[...]
        vn = (xhat * lng + lnb_ref[...]).astype(BF16)
        du = dy * sv_ref[...].astype(F32)
        dsv = dy * u
        dsvb = dsv.astype(BF16)
        for g in range(C_GROUPS):
            cols = slice(g * CHUNK, (g + 1) * CHUNK)
            for r0 in range(0, tm, PAIR):
                blk = dsvb[r0:r0 + PAIR, cols]
                dvn[r0:r0 + PAIR, cols] = _dot(bdt[g], blk)
                dws_acc[g] += _dot_nt(blk, vn[r0:r0 + PAIR, cols])
                dbs_acc[g] += dsv[r0:r0 + CHUNK, cols] + dsv[r0 + CHUNK:r0 + PAIR, cols]
        dv, dlng, dlnb = _ln_bwd(dvn[...], xhat, rs, lng)
        dlng_ref[...] += dlng
        dlnb_ref[...] += dlnb
        ds = jnp.concatenate([du, dv], axis=1) * dgelu
        dbin_ref[...] += jnp.sum(ds, axis=0, keepdims=True)
        dsb = ds.astype(BF16)
        ds_ref[...] = dsb
        dn = jnp.zeros((tm, D_MODEL), F32)
        for j in range(N_CHIPS):
            dn = dn + _dot_nt(dsb[:, j * cs:(j + 1) * cs], win_v[j])
        xv = h_ref[...]
        g = g_ref[...]
        _, rstd = _rms_fwd(xv, g)
        dx, dg = _rms_bwd(dn, xv, rstd, g)
        dx_ref[...] = dhv + dx
        dxb_ref[...] = (dhv + dx).astype(BF16)
        dg_ref[...] += dg

        @pl.when(i == nt - 1)
        def _():
            mask = _tril_mask()
            for g in range(C_GROUPS):
                full = dws_acc[g]
                dws_ref[g] = jnp.where(mask, full[0:CHUNK, 0:CHUNK] + full[CHUNK:PAIR, CHUNK:PAIR], 0.0)
                dbs_ref[g:g + 1, :] = jnp.sum(dbs_acc[g].T, axis=0, keepdims=True)

    row = lambda cols: jax.ShapeDtypeStruct((1, cols), F32)
    return _pallas(
        body, [dh, h, norm_g, s, cdf, sv, w_in, ln_g, ln_b, w_s, w_out], name="bwd_odd", grid=(nt,),
        in_specs=[_row_spec(tm, D_MODEL), _row_spec(tm, D_MODEL), _full_spec((1, D_MODEL)), _row_spec(tm, 2 * C_DIM),
                  _row_spec(tm, 2 * C_DIM), _row_spec(tm, C_DIM), ANY, _full_spec((1, C_DIM)), _full_spec((1, C_DIM)),
                  _full_spec((C_GROUPS, CHUNK, CHUNK)), ANY],
        out_specs=[_row_spec(tm, D_MODEL), _row_spec(tm, D_MODEL), _row_spec(tm, 2 * C_DIM), _full_spec((1, D_MODEL)),
                   _full_spec((1, 2 * C_DIM)),
                   _full_spec((1, C_DIM)), _full_spec((1, C_DIM)), _full_spec((C_GROUPS, CHUNK, CHUNK)),
                   _full_spec((C_GROUPS, CHUNK))],
        out_shape=[jax.ShapeDtypeStruct((tokens, D_MODEL), F32), jax.ShapeDtypeStruct((tokens, D_MODEL), BF16),
                   jax.ShapeDtypeStruct((tokens, 2 * C_DIM), BF16),
                   row(D_MODEL), row(2 * C_DIM), row(C_DIM), row(C_DIM),
                   jax.ShapeDtypeStruct((C_GROUPS, CHUNK, CHUNK), F32), jax.ShapeDtypeStruct((C_GROUPS, CHUNK), F32)],
        scratch_shapes=[pltpu.VMEM((N_CHIPS, D_MODEL, cs), BF16), pltpu.VMEM((C_DIM, D_MODEL), BF16),
                        pltpu.VMEM((C_GROUPS, PAIR, PAIR), BF16), pltpu.VMEM((C_GROUPS, PAIR, PAIR), F32),
                        pltpu.VMEM((C_GROUPS, CHUNK, CHUNK), F32), pltpu.VMEM((tm, C_DIM), F32),
                        pltpu.SemaphoreType.DMA((N_LOADS,))],
        vmem_mib=56, riders=riders)


def _bwd_even(dh, x, norm_g, z, a2, cv, w_in, conv_a_w, ln_g, ln_b, conv_b_w, w_out, *, tm, seq, riders=()):
    tokens = x.shape[0]
    nt, tps = tokens // tm, seq // tm
    ws = IN_EVEN // N_CHIPS

    def body(dh_ref, x_ref, g_ref, z_ref, a2_ref, cv_ref, win_hbm, caw_ref, lng_ref, lnb_ref, cbw_ref, wout_hbm,
             dx_ref, dz_ref, dg_ref, dcaw_ref, dcab_ref, dlng_ref, dlnb_ref, dcbw_ref,
             win_v, wout_v, ea, eb, a1s, da1s, sigs, wide, dw_acc, sem):
        i = pl.program_id(0)

        _load_weights([(win_hbm, win_v), (wout_hbm, wout_v)], sem)

        @pl.when(i == 0)
        def _():
            dw_acc[...] = jnp.zeros(dw_acc.shape, F32)
            for ref in (dg_ref, dcab_ref, dlng_ref, dlnb_ref, dcbw_ref):
                ref[...] = jnp.zeros(ref.shape, F32)

        last = ((nt - 1 - i) % tps) == tps - 1

        @pl.when(last)
        def _():
            ea[0, tm:tm + A_HALO, :] = jnp.zeros((A_HALO, A_DIM), F32)
            eb[tm:tm + B_HALO, :] = jnp.zeros((B_HALO, B_DIM), F32)

        @pl.when(jnp.logical_not(last))
        def _():
            ea[0, tm:tm + A_HALO, :] = ea[0, 0:A_HALO, :]
            eb[tm:tm + B_HALO, :] = eb[0:B_HALO, :]

        wide[...] = _dot_nt(dh_ref[...].astype(BF16), wout_v[...])
        lng, lnb = lng_ref[...], lnb_ref[...]
        zero_row = jnp.zeros((1, A_DIM), F32)
        dlng, dlnb, dcab = zero_row, zero_row, zero_row
        for r0 in range(0, tm, ELEM_ROWS):
            rows = slice(r0, r0 + ELEM_ROWS)
            a_val, a_gate = z_ref[rows, 0:A_DIM].astype(F32), z_ref[rows, A_DIM:2 * A_DIM].astype(F32)
            xhat, rs = _ln_stats(a2_ref[rows, :])
            a3 = xhat * lng + lnb
            sg = jax.nn.sigmoid(a3)
            da3 = wide[rows, 0:A_DIM] * (sg * (1.0 + a3 * (1.0 - sg)))
            da2, g_part, b_part = _ln_bwd(da3, xhat, rs, lng)
            dlng, dlnb, dcab = dlng + g_part, dlnb + b_part, dcab + jnp.sum(da2, axis=0, keepdims=True)
            ea[0, rows, :] = da2
            eb[rows, :] = wide[rows, A_DIM:A_DIM + B_DIM] * z_ref[rows, 1024:1536].astype(F32)
            sig = jax.nn.sigmoid(a_gate)
            sigs[rows, :] = sig
            a1s[rows, :] = a_val * sig
        dlng_ref[...] += dlng
        dlnb_ref[...] += dlnb
        dcab_ref[...] += dcab
        _fill_shifted(ea, tm + A_HALO)
        for r0 in range(0, tm, CONV_ROWS):
            acc = jnp.zeros((CONV_ROWS, A_DIM), F32)
            for j in range(A_CONV_WIDTH):
                acc = acc + caw_ref[A_CONV_WIDTH - 1 - j:A_CONV_WIDTH - j, :] * _window(ea, r0 + j, CONV_ROWS)
            da1s[r0:r0 + CONV_ROWS, :] = acc
        for j0 in range(0, A_CONV_WIDTH, DW_TAPS):
            taps = range(j0, min(j0 + DW_TAPS, A_CONV_WIDTH))
            part = [jnp.zeros((CONV_ROWS, A_DIM), F32) for _ in taps]
            for r0 in range(0, tm, CONV_ROWS):
                a1c = a1s[r0:r0 + CONV_ROWS, :]
                for u, j in enumerate(taps):
                    part[u] = part[u] + _window(ea, r0 + j, CONV_ROWS) * a1c
            for u, j in enumerate(taps):
                dw_acc[A_CONV_WIDTH - 1 - j] += part[u]
        dcbw = [jnp.zeros((1, B_DIM), F32) for _ in range(B_CONV_WIDTH)]
        for r0 in range(0, tm, ELEM_ROWS):
            rows = slice(r0, r0 + ELEM_ROWS)
            da1, sig = da1s[rows, :], sigs[rows, :]
            dz_ref[rows, 0:A_DIM] = (da1 * sig).astype(BF16)
            dz_ref[rows, A_DIM:2 * A_DIM] = (da1 * z_ref[rows, 0:A_DIM].astype(F32) * (sig * (1.0 - sig))).astype(BF16)
            c_gate, b_val = z_ref[rows, 1536:2048].astype(F32), z_ref[rows, 2048:2560].astype(F32)
            dz_ref[rows, 1024:1536] = (wide[rows, A_DIM:A_DIM + B_DIM] * cv_ref[rows, :].astype(F32)).astype(BF16)
            cb = c_gate * b_val
            dcb = jnp.zeros((ELEM_ROWS, B_DIM), F32)
            for j in range(B_CONV_WIDTH):
                k = B_CONV_WIDTH - 1 - j
                sl = eb[r0 + j:r0 + j + ELEM_ROWS, :]
                dcb = dcb + cbw_ref[k:k + 1, :] * sl
                dcbw[k] = dcbw[k] + jnp.sum(sl * cb, axis=0, keepdims=True)
            dz_ref[rows, 1536:2048] = (dcb * b_val).astype(BF16)
            dz_ref[rows, 2048:2560] = (dcb * c_gate).astype(BF16)
        for k in range(B_CONV_WIDTH):
            dcbw_ref[k:k + 1, :] += dcbw[k]
        dn = jnp.zeros((tm, D_MODEL), F32)
        for j in range(N_CHIPS):
            dn = dn + _dot_nt(dz_ref[:, j * ws:(j + 1) * ws], win_v[j])
        wide[...] = dn
        g = g_ref[...]
        dg = jnp.zeros((1, D_MODEL), F32)
        for r0 in range(0, tm, ELEM_ROWS):
            rows = slice(r0, r0 + ELEM_ROWS)
            xv = x_ref[rows, :]
            _, rstd = _rms_fwd(xv, g)
            dx, dg_part = _rms_bwd(wide[rows, :], xv, rstd, g)
            dx_ref[rows, :] = dh_ref[rows, :] + dx
            dg = dg + dg_part
        dg_ref[...] += dg

        @pl.when(i == nt - 1)
        def _():
            for k in range(A_CONV_WIDTH):
                dcaw_ref[k:k + 1, :] = jnp.sum(dw_acc[k], axis=0, keepdims=True)

    row = lambda cols: jax.ShapeDtypeStruct((1, cols), F32)
    rs_ = functools.partial(_row_spec, rev_nt=nt)
    return _pallas(
        body, [dh, x, norm_g, z, a2, cv, w_in, conv_a_w, ln_g, ln_b, conv_b_w, w_out], name="bwd_even", grid=(nt,),
        in_specs=[rs_(tm, D_MODEL), rs_(tm, D_MODEL), _full_spec((1, D_MODEL)), rs_(tm, IN_EVEN), rs_(tm, A_DIM),
                  rs_(tm, B_DIM), ANY, _full_spec((A_CONV_WIDTH, A_DIM)), _full_spec((1, A_DIM)), _full_spec((1, A_DIM)),
                  _full_spec((B_CONV_WIDTH, B_DIM)), ANY],
        out_specs=[rs_(tm, D_MODEL), rs_(tm, IN_EVEN), _full_spec((1, D_MODEL)), _full_spec((A_CONV_WIDTH, A_DIM)),
                   _full_spec((1, A_DIM)), _full_spec((1, A_DIM)), _full_spec((1, A_DIM)), _full_spec((B_CONV_WIDTH, B_DIM))],
        out_shape=[jax.ShapeDtypeStruct((tokens, D_MODEL), F32), jax.ShapeDtypeStruct((tokens, IN_EVEN), BF16),
                   row(D_MODEL), jax.ShapeDtypeStruct((A_CONV_WIDTH, A_DIM), F32), row(A_DIM), row(A_DIM), row(A_DIM),
                   jax.ShapeDtypeStruct((B_CONV_WIDTH, B_DIM), F32)],
        scratch_shapes=[pltpu.VMEM((N_CHIPS, D_MODEL, ws), BF16), pltpu.VMEM((D_MODEL, D_MODEL), BF16),
                        pltpu.VMEM((SUBLANES, tm + A_HALO, A_DIM), F32), pltpu.VMEM((tm + B_HALO, B_DIM), F32),
                        pltpu.VMEM((tm, A_DIM), F32), pltpu.VMEM((tm, A_DIM), F32), pltpu.VMEM((tm, A_DIM), F32),
                        pltpu.VMEM((tm, D_MODEL), F32),
                        pltpu.VMEM((A_CONV_WIDTH, CONV_ROWS, A_DIM), F32), pltpu.SemaphoreType.DMA((N_LOADS,))],
        vmem_mib=56, riders=riders)


def _wgrad(a, b, name, *, col_shards, riders=()):
    tokens, m = a.shape
    n = b.shape[1]
    kc = 512
    if col_shards:
        bm, bn = m // 2, n // N_CHIPS
        grid = (2, N_CHIPS)
        out_spec = pl.BlockSpec((None, None, bm, bn), lambda i, j: (j, i, 0, 0))
    elif m // 8 >= MXU_ROWS:
        bm, bn = m // 8, n
        grid = (8, 1)
        out_spec = pl.BlockSpec((None, None, bm, bn), lambda i, j: (i // 2, i % 2, 0, 0))
    else:
        bm, bn = m // N_CHIPS, n
        grid = (N_CHIPS, 1)
        out_spec = pl.BlockSpec((None, 2, bm // 2, bn), lambda i, j: (i, 0, 0, 0))

    def body(a_ref, b_ref, o_ref):
        acc = jnp.zeros((bm, bn), F32)
        for k0 in range(0, tokens, kc):
            acc = acc + _dot_tn(a_ref[k0:k0 + kc, :].astype(BF16), b_ref[k0:k0 + kc, :].astype(BF16))
        if len(o_ref.shape) == 3:
            o_ref[0] = acc[0:bm // 2]
            o_ref[1] = acc[bm // 2:bm]
        else:
            o_ref[...] = acc

    out_rows = m // 2 if col_shards else m // 8
    outs, routs = _pallas(
        body, [a, b], name=name, grid=grid,
        in_specs=[pl.BlockSpec((tokens, bm), lambda i, j: (0, i)), pl.BlockSpec((tokens, bn), lambda i, j: (0, j))],
        out_specs=[out_spec], out_shape=[jax.ShapeDtypeStruct((N_CHIPS, 2, out_rows, bn), F32)],
        vmem_mib=56, riders=riders)
    return outs[0], routs


def _wgrad_pair(a, b, name, *, col_shards, riders=(), to_chips=False):
    tokens, m = a.shape
    n = b.shape[1]
    kc = 512
    x0, y0, c0 = _mesh_pos()
    rot = 1 if to_chips else 0
    phases = [0, 0, 1, 0, 1, 0, 1, 1]
    tiles = [0, 1, 0, 2, 1, 3, 2, 3]
    out_tiles = [0, 0, 0, 0, 1, 1, 2, 3]
    steps = len(phases)
    P0, T0, O0 = 2, 2 + steps, 2 + 2 * steps

    def slab(t, pre):
        return (t + rot * (1 + pre[1])) % N_CHIPS

    def half(s, pre):
        return (pre[P0 + s] + 1 + pre[0]) % 2

    if col_shards:
        bm, bn = m // 2, n // N_CHIPS
        a_spec = pl.BlockSpec((tokens, bm), lambda s, pre: (0, half(s, pre)))
        b_spec = pl.BlockSpec((tokens, bn), lambda s, pre: (0, slab(pre[T0 + s], pre)))
    else:
        bm, bn = m // 8, n
        a_spec = pl.BlockSpec((tokens, bm), lambda s, pre: (0, 2 * slab(pre[T0 + s], pre) + half(s, pre)))
        b_spec = pl.BlockSpec((tokens, bn), lambda s, pre: (0, 0))

    def body(pre_ref, a_ref, b_ref, o_ref, *rest):
        if to_chips:
            land, give, got, mine, send_sems, recv_sems, chip_send, chip_recv = rest
        else:
            give, got, send_sems, recv_sems = rest
        step = pl.program_id(0)
        ph, q = pre_ref[P0 + step], pre_ref[T0 + step]
        acc = jnp.zeros((bm, bn), F32)
        for k0 in range(0, tokens, kc):
            acc = acc + _dot_tn(a_ref[k0:k0 + kc, :].astype(BF16), b_ref[k0:k0 + kc, :].astype(BF16))
        x, y, cc = _mesh_pos()

        def tile(t):
            return _remote(give.at[t], got.at[t], send_sems.at[t], recv_sems.at[t], (x, y, 1 - cc))

        def to_chip(s):
            t = (s + 1 + 2 * x + y) % N_CHIPS
            tx, ty = t // 2, t % 2
            k = 2 * (ty ^ y) + (tx ^ x) - 1
            return _remote(mine.at[s], land.at[k], chip_send.at[k], chip_recv.at[k], (tx, ty, cc))

        @pl.when(ph == 0)
        def _():
            give[q] = acc
            tile(q).start()

        @pl.when(ph == 1)
        def _():
            tile(q).wait_recv()
            total = (acc + got[q]).astype(BF16)
            o_ref[...] = total
            if to_chips:
                for s in range(N_CHIPS - 1):
                    @pl.when(q == s)
                    def _(s=s):
                        mine[s] = total
                        to_chip(s).start()

        @pl.when(step == steps - 1)
        def _():
            for t in range(N_CHIPS):
                tile(t).wait_send()
            if to_chips:
                for s in range(N_CHIPS - 1):
                    to_chip(s).wait()

    prefetch = jnp.concatenate([jnp.stack([c0, 2 * x0 + y0]).astype(jnp.int32),
                                jnp.asarray(phases + tiles + out_tiles, jnp.int32)])
    out_specs = [pl.BlockSpec((None, bm, bn), lambda s, pre: (slab(pre[O0 + s], pre), 0, 0))]
    out_shape = [jax.ShapeDtypeStruct((N_CHIPS, bm, bn), BF16)]
    scratch = [pltpu.VMEM((N_CHIPS, bm, bn), F32), pltpu.VMEM((N_CHIPS, bm, bn), F32)]
    sems = [pltpu.SemaphoreType.DMA((N_CHIPS,)), pltpu.SemaphoreType.DMA((N_CHIPS,))]
    if to_chips:
        out_specs.append(ANY)
        out_shape.append(jax.ShapeDtypeStruct((N_CHIPS - 1, bm, bn), BF16))
        scratch.append(pltpu.VMEM((N_CHIPS - 1, bm, bn), BF16))
        sems += [pltpu.SemaphoreType.DMA((N_CHIPS - 1,)), pltpu.SemaphoreType.DMA((N_CHIPS - 1,))]
    outs, routs = _pallas(
        body, [a, b], name=name, grid=(steps,), in_specs=[a_spec, b_spec], out_specs=out_specs, out_shape=out_shape,
        scratch_shapes=scratch + sems, vmem_mib=56, riders=riders, prefetch=prefetch)
    return (outs if to_chips else outs[0]), routs


class _GradReduce:
    def __init__(self, name, grad=None, chip_sum=None):
        self.name, self.grad, self.chip_sum = name, grad, chip_sum
        self.full = None

    def pair_swap(self):
        return _PairSwap([self.grad])

    def took_pair(self, outs):
        self.chip_sum = _in_hbm(_add_pair(self.grad, outs[0], f"pair_sum_{self.name}"))

    def chip_swap(self):
        return _ChipSwap([self.chip_sum])

    def took_chips(self, outs):
        self.full = _in_hbm(_add_chips(self.chip_sum, outs[0], f"chip_sum_{self.name}"))

    def pair_share(self):
        return _PairShare([self.full])

    def took_share(self, outs):
        self.full = outs[0]

    def reduced(self):
        return jnp.reshape(self.full, (2 * self.full.shape[1], self.full.shape[2]))


def _forward_backward(x2, tgt2, gathered, staged, conv_a_w, conv_b_w, od_norm, od_bias, od_lng, od_lnb,
                      ev_norm_g, ev_conv_a_b, ev_ln_a_g, ev_ln_a_b, od_w_s, od_b_s, mlp_norm_g, final_norm_g,
                      *, tm, seq, distributed=True):
    d = x2.shape[1]
    w = dict(gathered)
    b_s_rows = jnp.broadcast_to(od_b_s[0][:, :, None], (C_GROUPS, CHUNK, CHUNK))

    def ride(*names):
        return [_Gather([staged[nm] for nm in names])] if distributed else []

    def land(routs, *names):
        if distributed:
            for nm, buf in zip(names, routs[0]):
                w[nm] = buf

    def as_cols(buf):
        return jnp.reshape(buf, (N_CHIPS, 2 * buf.shape[2], buf.shape[3]))

    def as_rows(buf):
        return jnp.reshape(buf, (8 * buf.shape[2], buf.shape[3]))

    (h1, n0, z, a2, cv, mix), routs = _fwd_even(
        x2, ev_norm_g, as_cols(w["ev_in"]), conv_a_w, ev_conv_a_b, ev_ln_a_g, ev_ln_a_b, conv_b_w, as_rows(w["ev_out"]),
        tm=tm, seq=seq, riders=ride("w1_0", "w2_0"))
    land(routs, "w1_0", "w2_0")
    (h2, n1, p0, q0), routs = _fwd_mlp(h1, mlp_norm_g[0:1], as_cols(w["w1_0"]), as_cols(w["w2_0"]), 0, tm=tm,
                                       riders=ride("od_in", "od_out", "w1_1"))
    land(routs, "od_in", "od_out", "w1_1")
    (h3, n2, s, cdf, sv, y), routs = _fwd_odd(h2, od_norm, as_cols(w["od_in"]), od_bias, od_lng, od_lnb, od_w_s[0], b_s_rows,
                                         as_rows(w["od_out"]), tm=tm, riders=ride("w2_1"))
    land(routs, "w2_1")
    (n3, p1, q1, loss_part, dh4, dh4b, d_final_g), _ = _fwd_mlp(
        h3, mlp_norm_g[1:2], as_cols(w["w1_1"]), as_cols(w["w2_1"]), 1, tm=tm,
        head=(jnp.reshape(final_norm_g, (1, d)), tgt2))

    red = {}

    def swap(*names):
        return [red[nm].pair_swap() for nm in names] if distributed else []

    def chips(*names):
        return [red[nm].chip_swap() for nm in names] if distributed else []

    def share(*names):
        return [red[nm].pair_share() for nm in names] if distributed else []

    def took(routs, *steps):
        if distributed:
            for (nm, what), outs in zip(steps, routs):
                getattr(red[nm], what)(outs)

    def big(lhs, rhs, name, col_shards, riders=(), to_chips=False):
        if distributed and to_chips:
            (chip_sum, from_chips), routs = _wgrad_pair(lhs, rhs, f"wgrad_{name}", col_shards=col_shards, riders=riders,
                                                        to_chips=True)
            red[name] = _GradReduce(name, chip_sum=_in_hbm(chip_sum))
            red[name].took_chips([_in_hbm(from_chips)])
        elif distributed:
            chip_sum, routs = _wgrad_pair(lhs, rhs, f"wgrad_{name}", col_shards=col_shards, riders=riders)
            red[name] = _GradReduce(name, chip_sum=_in_hbm(chip_sum))
        else:
            g, routs = _wgrad(lhs, rhs, f"wgrad_{name}", col_shards=col_shards)
            red[name] = _GradReduce(name, grad=g)
        return routs

    big(q1, dh4b, "w2_1", False)
    (dh3, dh3b, dp1, d_mlp_g1), routs = _bwd_mlp(dh4, h3, mlp_norm_g[1:2], p1, as_cols(w["w1_1"]), as_cols(w["w2_1"]), 1, tm=tm,
                                           riders=chips("w2_1"))
    took(routs, ("w2_1", "took_chips"))
    big(n3, dp1, "w1_1", True)
    g, routs = _wgrad(y, dh3b, "wgrad_od_out", col_shards=False, riders=share("w2_1"))
    red["od_out"] = _GradReduce("od_out", grad=g)
    took(routs, ("w2_1", "took_share"))
    (dh2, dh2b, ds, d_od_norm, d_od_bin, d_od_lng, d_od_lnb, d_ws, d_bs), routs = _bwd_odd(
        dh3, h2, od_norm, s, cdf, sv, as_cols(w["od_in"]), od_lng, od_lnb, od_w_s[0], as_rows(w["od_out"]), tm=tm,
        riders=chips("w1_1") + swap("od_out"))
    took(routs, ("w1_1", "took_chips"), ("od_out", "took_pair"))
    routs = big(n2, ds, "od_in", True, riders=share("w1_1"))
    took(routs, ("w1_1", "took_share"))
    half_groups = C_GROUPS // 2
    early = {"loss": loss_part, "od_w_s_lo": d_ws[:half_groups], "od_b_s": d_bs, "mlp_norm_g1": d_mlp_g1, "final_norm_g": d_final_g,
             "od_norm_g": d_od_norm, "od_b_in": d_od_bin, "od_ln_v_g": d_od_lng, "od_ln_v_b": d_od_lnb}
    share_early = [_ShareAll(list(early.values()))] if distributed else []
    routs = big(q0, dh2b, "w2_0", False, riders=share_early)
    landed_early = routs[0] if distributed else []
    (dh1, dh1b, dp0, d_mlp_g0), routs = _bwd_mlp(dh2, h1, mlp_norm_g[0:1], p0, as_cols(w["w1_0"]), as_cols(w["w2_0"]), 0, tm=tm,
                                           riders=chips("od_out") + chips("od_in") + chips("w2_0"))
    took(routs, ("od_out", "took_chips"), ("od_in", "took_chips"), ("w2_0", "took_chips"))
    middle = {"od_w_s_hi": d_ws[half_groups:]}
    share_middle = [_ShareAll(list(middle.values()))] if distributed else []
    g, _ = _wgrad(mix, dh1b, "wgrad_ev_out", col_shards=False)
    red["ev_out"] = _GradReduce("ev_out", grad=g)
    routs = big(n1, dp0, "w1_0", True,
                riders=share("od_out") + share("od_in") + share("w2_0") + share_middle + swap("ev_out"))
    took(routs, ("od_out", "took_share"), ("od_in", "took_share"), ("w2_0", "took_share"))
    landed_middle = routs[3] if distributed else []
    if distributed:
        red["ev_out"].took_pair(routs[4])

    (dx, dz, d_ev_norm, d_caw, d_cab, d_ev_lng, d_ev_lnb, d_cbw), routs = _bwd_even(
        dh1, x2, ev_norm_g, z, a2, cv, as_cols(w["ev_in"]), conv_a_w, ev_ln_a_g, ev_ln_a_b, conv_b_w, as_rows(w["ev_out"]),
        tm=tm, seq=seq, riders=chips("w1_0") + chips("ev_out"))
    took(routs, ("w1_0", "took_chips"), ("ev_out", "took_chips"))
    late = {"mlp_norm_g0": d_mlp_g0, "ev_norm_g": d_ev_norm, "ev_conv_a_b": d_cab, "ev_ln_a_g": d_ev_lng,
            "ev_ln_a_b": d_ev_lnb, "ev_conv_a_w": d_caw, "ev_conv_b_w": d_cbw}
    share_late = [_ShareAll(list(late.values()))] if distributed else []
    routs2 = big(n0, dz, "ev_in", True, riders=share("ev_out") + share("w1_0") + share_late, to_chips=True)
    took(routs2, ("ev_out", "took_share"), ("w1_0", "took_share"))
    own = {**early, **middle, **late}
    landed = dict(zip(own.keys(), landed_early + landed_middle + routs2[2])) if distributed else None
    return dx, red, own, landed


def _rows128(a):
    rows = jnp.reshape(a, (-1, LANES))
    pad = (-rows.shape[0]) % SUBLANES
    return jnp.pad(rows, ((0, pad), (0, 0))) if pad else rows


def _pack(arrays):
    return jnp.concatenate([_rows128(a) for a in arrays], axis=0)


def _unpack(buf, shapes):
    out, r0 = [], 0
    for shp in shapes:
        size = 1
        for dim in shp:
            size *= dim
        nr = size // LANES
        out.append(jnp.reshape(buf[r0:r0 + nr], shp))
        r0 += nr + (-nr) % SUBLANES
    return out


def kernel(x, ev_norm_g, ev_w_in, ev_conv_a_w, ev_conv_a_b, ev_ln_a_g, ev_ln_a_b, ev_conv_b_w, ev_w_out, od_norm_g, od_w_in, od_b_in, od_ln_v_g, od_ln_v_b, od_w_s, od_b_s, od_w_out, mlp_norm_g, mlp_w1, mlp_w2, final_norm_g, loss_target, m_ev_norm_g, m_ev_w_in, m_ev_conv_a_w, m_ev_conv_a_b, m_ev_ln_a_g, m_ev_ln_a_b, m_ev_conv_b_w, m_ev_w_out, m_od_norm_g, m_od_w_in, m_od_b_in, m_od_ln_v_g, m_od_ln_v_b, m_od_w_s, m_od_b_s, m_od_w_out, m_mlp_norm_g, m_mlp_w1, m_mlp_w2, m_final_norm_g, v_ev_norm_g, v_ev_w_in, v_ev_conv_a_w, v_ev_conv_a_b, v_ev_ln_a_g, v_ev_ln_a_b, v_ev_conv_b_w, v_ev_w_out, v_od_norm_g, v_od_w_in, v_od_b_in, v_od_ln_v_g, v_od_ln_v_b, v_od_w_s, v_od_b_s, v_od_w_out, v_mlp_norm_g, v_mlp_w1, v_mlp_w2, v_final_norm_g):
    tm = TOKEN_TILE
    batch, seq, d = x.shape
    tokens = batch * seq
    x2 = jnp.reshape(x, (tokens, d))
    tgt2 = jnp.reshape(loss_target, (tokens, d))
    chip = 2 * lax.axis_index("x") + lax.axis_index("y")

    small_shapes = [(A_CONV_WIDTH, LANES), (B_CONV_WIDTH, LANES), (256,), (512,), (256,), (256,)]
    small_shard = _pack([ev_conv_a_w[0], ev_conv_b_w[0], od_norm_g[0], od_b_in[0], od_ln_v_g[0], od_ln_v_b[0]])
    small_shard = jnp.pad(small_shard, ((0, (-small_shard.shape[0]) % (2 * SUBLANES)), (0, 0)))
    first = [_place_shard(ev_w_in, 0, BF16, "place_ev_w_in"), _place_shard(ev_w_out, 0, BF16, "place_ev_w_out"),
             _place_shard(small_shard[None], 0, F32, "place_small")]
    staged = {
        "w1_0": _place_shard(mlp_w1, 0, BF16, "place_w1_0"), "w2_0": _place_shard(mlp_w2, 0, BF16, "place_w2_0"),
        "od_in": _place_shard(od_w_in, 0, BF16, "place_od_w_in"), "od_out": _place_shard(od_w_out, 0, BF16, "place_od_w_out"),
        "w1_1": _place_shard(mlp_w1, 1, BF16, "place_w1_1"), "w2_1": _place_shard(mlp_w2, 1, BF16, "place_w2_1"),
    }
    first = [_in_hbm(a) for a in first]
    staged = {nm: _in_hbm(a) for nm, a in staged.items()}
    (g_ev_in, g_ev_out, g_small), = _exchange([_Gather(first)], "gather_first")
    small_all = jnp.reshape(g_small, (N_CHIPS, -1, LANES))
    per_chip = [_unpack(small_all[q], small_shapes) for q in range(N_CHIPS)]
    conv_a_w = jnp.concatenate([pc[0] for pc in per_chip], axis=1)
    conv_b_w = jnp.concatenate([pc[1] for pc in per_chip], axis=1)
    od_norm = jnp.concatenate([pc[2] for pc in per_chip])[None, :]
    od_bias = jnp.concatenate([pc[3] for pc in per_chip])[None, :]
    od_lng = jnp.concatenate([pc[4] for pc in per_chip])[None, :]
    od_lnb = jnp.concatenate([pc[5] for pc in per_chip])[None, :]

    dx, red, own, landed = _forward_backward(
        x2, tgt2, {"ev_in": g_ev_in, "ev_out": g_ev_out}, staged, conv_a_w, conv_b_w, od_norm, od_bias, od_lng, od_lnb,
        ev_norm_g, ev_conv_a_b, ev_ln_a_g, ev_ln_a_b, od_w_s, od_b_s, mlp_norm_g, final_norm_g, tm=tm, seq=seq)

    routs = _exchange([red["ev_in"].pair_share()], "reduce_tail")
    red["ev_in"].took_share(routs[0])

    given = {"ev_norm_g": (ev_norm_g, m_ev_norm_g, v_ev_norm_g), "ev_conv_a_b": (ev_conv_a_b, m_ev_conv_a_b, v_ev_conv_a_b),
             "ev_ln_a_g": (ev_ln_a_g, m_ev_ln_a_g, v_ev_ln_a_g), "ev_ln_a_b": (ev_ln_a_b, m_ev_ln_a_b, v_ev_ln_a_b),
             "od_w_s": (od_w_s, m_od_w_s, v_od_w_s), "od_b_s": (od_b_s, m_od_b_s, v_od_b_s),
             "mlp_norm_g": (mlp_norm_g, m_mlp_norm_g, v_mlp_norm_g), "final_norm_g": (final_norm_g, m_final_norm_g, v_final_norm_g),
             "ev_conv_a_w": (ev_conv_a_w, m_ev_conv_a_w, v_ev_conv_a_w), "ev_conv_b_w": (ev_conv_b_w, m_ev_conv_b_w, v_ev_conv_b_w),
             "od_norm_g": (od_norm_g, m_od_norm_g, v_od_norm_g), "od_b_in": (od_b_in, m_od_b_in, v_od_b_in),
             "od_ln_v_g": (od_ln_v_g, m_od_ln_v_g, v_od_ln_v_g), "od_ln_v_b": (od_ln_v_b, m_od_ln_v_b, v_od_ln_v_b)}
    shaped = {nm: tuple(jnp.reshape(a, shape) for a in given[nm]) for nm, shape, _, _ in SMALL_WEIGHTS}
    loss11, small_upd = _small_update(own, landed, shaped)
    loss = loss11[0, 0]
    upd = {nm: [jnp.reshape(o, given[nm][0].shape) for o in outs] for nm, outs in small_upd.items()}

    def big_update(wt, m, v, names, call):
        grads = [red[nm].reduced() for nm in names]
        shp3 = (len(grads),) + grads[0].shape
        outs, _ = _adamw(jnp.reshape(wt, shp3), jnp.reshape(m, shp3), jnp.reshape(v, shp3), grads, call)
        return [jnp.reshape(o, wt.shape) for o in outs], None

    upd["mlp_w2"], _ = big_update(mlp_w2, m_mlp_w2, v_mlp_w2, ["w2_0", "w2_1"], "adamw_mlp_w2")
    upd["mlp_w1"], _ = big_update(mlp_w1, m_mlp_w1, v_mlp_w1, ["w1_0", "w1_1"], "adamw_mlp_w1")
    upd["ev_w_in"], _ = big_update(ev_w_in, m_ev_w_in, v_ev_w_in, ["ev_in"], "adamw_ev_w_in")
    upd["ev_w_out"], _ = big_update(ev_w_out, m_ev_w_out, v_ev_w_out, ["ev_out"], "adamw_ev_w_out")
    upd["od_w_in"], _ = big_update(od_w_in, m_od_w_in, v_od_w_in, ["od_in"], "adamw_od_w_in")
    upd["od_w_out"], _ = big_update(od_w_out, m_od_w_out, v_od_w_out, ["od_out"], "adamw_od_w_out")

    order = ["ev_norm_g", "ev_w_in", "ev_conv_a_w", "ev_conv_a_b", "ev_ln_a_g", "ev_ln_a_b", "ev_conv_b_w", "ev_w_out",
             "od_norm_g", "od_w_in", "od_b_in", "od_ln_v_g", "od_ln_v_b", "od_w_s", "od_b_s", "od_w_out", "mlp_norm_g",
             "mlp_w1", "mlp_w2", "final_norm_g"]
    grad_x = jnp.reshape(dx, x.shape)
    return (loss, grad_x, *[upd[nm][0] for nm in order], *[upd[nm][1] for nm in order],
            *[upd[nm][2] for nm in order], *[upd[nm][3] for nm in order])
```

```python
import functools

import jax
import jax.numpy as jnp
from jax import lax
from jax.experimental import pallas as pl
from jax.experimental.pallas import tpu as pltpu

F32 = jnp.float32
BF16 = jnp.bfloat16

D_MODEL = 1024
A_DIM = 512
B_DIM = 512
IN_EVEN = 2 * A_DIM + 3 * B_DIM
A_CONV_WIDTH = 31
B_CONV_WIDTH = 3
CHUNK = 128
C_GROUPS = 8
C_DIM = 1024
D_FF = 4096
RMS_EPS = 1e-6
LN_EPS = 1e-5
ADAM_LR = 0.001
ADAM_B1 = 0.9
ADAM_B2 = 0.999
ADAM_EPS = 1e-08
ADAM_WD = 0.01
ADAM_STEP = 10

N_CHIPS = 4
N_DEV = 8
TOKEN_TILE = 512
A_HALO = 32
B_HALO = 8
CONV_ROWS = 16
DW_TAPS = 4
ELEM_ROWS = 16
PAIR = 2 * CHUNK
LANES = 128
SUBLANES = 8
MXU_ROWS = 256
MIB = 1024 * 1024
MESH = pl.DeviceIdType.MESH
ANY = pl.BlockSpec(memory_space=pl.ANY)


def _dot(a, b):
    return lax.dot_general(a, b, (((1,), (0,)), ((), ())), preferred_element_type=F32)


def _dot_nt(a, b):
    return lax.dot_general(a, b, (((1,), (1,)), ((), ())), preferred_element_type=F32)


def _dot_tn(a, b):
    return lax.dot_general(a, b, (((0,), (0,)), ((), ())), preferred_element_type=F32)


def _params(vmem_mib, n_axes=1):
    return pltpu.CompilerParams(dimension_semantics=("arbitrary",) * n_axes, vmem_limit_bytes=vmem_mib * MIB)


def _row_spec(tm, cols, rev_nt=None):
    if rev_nt is None:
        return pl.BlockSpec((tm, cols), lambda i: (i, 0))
    return pl.BlockSpec((tm, cols), lambda i: (rev_nt - 1 - i, 0))


def _full_spec(shape):
    nd = len(shape)
    return pl.BlockSpec(shape, lambda i: (0,) * nd)


def _block_rows(rows, cap=512):
    best = SUBLANES
    for br in range(SUBLANES, min(rows, cap) + 1, SUBLANES):
        if rows % br == 0:
            best = br
    return best


N_LOADS = 2


def _load_weights(pairs, sems):
    @pl.when(pl.program_id(0) == 0)
    def _():
        copies = [pltpu.make_async_copy(src, dst, sems.at[k]) for k, (src, dst) in enumerate(pairs)]
        for cp in copies:
            cp.start()
        for cp in copies:
            cp.wait()


def _rms_fwd(x, g):
    rstd = lax.rsqrt(jnp.mean(x * x, axis=-1, keepdims=True) + RMS_EPS)
    return x * rstd * g, rstd


def _rms_bwd(dn, x, rstd, g):
    a = dn * g
    xh = x * rstd
    dx = rstd * (a - xh * jnp.mean(a * xh, axis=-1, keepdims=True))
    dg = jnp.sum(dn * xh, axis=0, keepdims=True)
    return dx, dg


def _ln_stats(v):
    mu = jnp.mean(v, axis=-1, keepdims=True)
    xc = v - mu
    rs = lax.rsqrt(jnp.mean(xc * xc, axis=-1, keepdims=True) + LN_EPS)
    return xc * rs, rs


def _ln_bwd(dy, xhat, rs, g):
    dxh = dy * g
    dv = rs * (dxh - jnp.mean(dxh, axis=-1, keepdims=True) - xhat * jnp.mean(dxh * xhat, axis=-1, keepdims=True))
    return dv, jnp.sum(dy * xhat, axis=0, keepdims=True), jnp.sum(dy, axis=0, keepdims=True)


def _gelu_cdf(s):
    return 0.5 * (1.0 + lax.erf(s * 0.7071067811865476))


def _mesh_pos():
    return lax.axis_index("x"), lax.axis_index("y"), lax.axis_index("c")


def _other_chips(x, y):
    return [(1 - x, y), (x, 1 - y), (1 - x, 1 - y)]


def _remote(src, dst, send_sem, recv_sem, to):
    return pltpu.make_async_remote_copy(src_ref=src, dst_ref=dst, send_sem=send_sem, recv_sem=recv_sem,
                                        device_id=to, device_id_type=MESH)


def _like(arrays):
    return [jax.ShapeDtypeStruct(a.shape, a.dtype) for a in arrays]


class _Gather:
    def __init__(self, bufs):
        self.ins = list(bufs)
        self.out_shapes = _like(bufs)
        self.aliases = {t: t for t in range(len(bufs))}
        self.n_sems = 6 * len(bufs)

    def _ici(self, ins, outs, send, recv, t, k, chip, mine, c):
        return _remote(ins[t].at[mine, c], outs[t].at[mine, c], send.at[6 * t + k], recv.at[6 * t + k], (*chip, c))

    def start(self, ins, outs, send, recv):
        x, y, c = _mesh_pos()
        for t in range(len(ins)):
            for k, chip in enumerate(_other_chips(x, y)):
                self._ici(ins, outs, send, recv, t, k, chip, 2 * x + y, c).start()

    def _pass_on(self, outs, send, recv, t, k, chip, c, to):
        blk = outs[t].at[2 * chip[0] + chip[1], c]
        return _remote(blk, blk, send.at[6 * t + 3 + k], recv.at[6 * t + 3 + k], to)

    def near_end(self, ins, outs, send, recv):
        x, y, c = _mesh_pos()
        for t in range(len(ins)):
            for k, chip in enumerate(_other_chips(x, y)):
                blk = outs[t].at[2 * chip[0] + chip[1], c]
                _remote(blk, blk, send.at[6 * t + k], recv.at[6 * t + k], (x, y, c)).wait_recv()
                self._pass_on(outs, send, recv, t, k, chip, c, (x, y, 1 - c)).start()

    def finish(self, ins, outs, send, recv):
        x, y, c = _mesh_pos()
        chips = _other_chips(x, y)
        for t in range(len(ins)):
            for k, chip in enumerate(chips):
                self._pass_on(outs, send, recv, t, k, chip, 1 - c, (x, y, c)).wait_recv()
        for t in range(len(ins)):
            for k, chip in enumerate(chips):
                self._ici(ins, outs, send, recv, t, k, chip, 2 * x + y, c).wait_send()
                self._pass_on(outs, send, recv, t, k, chip, c, (x, y, 1 - c)).wait_send()


class _PairSwap:
    def __init__(self, grads):
        self.ins = list(grads)
        self.out_shapes = [jax.ShapeDtypeStruct((g.shape[0],) + g.shape[2:], g.dtype) for g in grads]
        self.aliases = {}
        self.n_sems = len(grads)

    def _copies(self, ins, outs, send, recv):
        x, y, c = _mesh_pos()
        return [_remote(ins[t].at[:, 1 - c], outs[t], send.at[t], recv.at[t], (x, y, 1 - c)) for t in range(len(ins))]

    def start(self, ins, outs, send, recv):
        for cp in self._copies(ins, outs, send, recv):
            cp.start()

    def finish(self, ins, outs, send, recv):
        for cp in self._copies(ins, outs, send, recv):
            cp.wait()


class _ChipSwap:
    def __init__(self, parts):
        self.ins = list(parts)
        self.out_shapes = [jax.ShapeDtypeStruct((3,) + p.shape[1:], p.dtype) for p in parts]
        self.aliases = {}
        self.n_sems = 3 * len(parts)

    def _copies(self, ins, outs, send, recv):
        x, y, c = _mesh_pos()
        return [_remote(ins[t].at[2 * chip[0] + chip[1]], outs[t].at[k], send.at[3 * t + k], recv.at[3 * t + k], (*chip, c))
                for t in range(len(ins)) for k, chip in enumerate(_other_chips(x, y))]

    def start(self, ins, outs, send, recv):
        for cp in self._copies(ins, outs, send, recv):
            cp.start()

    def finish(self, ins, outs, send, recv):
        for cp in self._copies(ins, outs, send, recv):
            cp.wait()


class _PairShare:
    def __init__(self, fulls):
        self.ins = list(fulls)
        self.out_shapes = _like(fulls)
        self.aliases = {t: t for t in range(len(fulls))}
        self.n_sems = len(fulls)

    def _copies(self, ins, outs, send, recv):
        x, y, c = _mesh_pos()
        return [_remote(ins[t].at[c], outs[t].at[c], send.at[t], recv.at[t], (x, y, 1 - c)) for t in range(len(ins))]

    def start(self, ins, outs, send, recv):
        for cp in self._copies(ins, outs, send, recv):
            cp.start()

    def finish(self, ins, outs, send, recv):
        for cp in self._copies(ins, outs, send, recv):
            cp.wait()


class _ShareAll:
    def __init__(self, arrays):
        self.ins = list(arrays)
        self.out_shapes = [jax.ShapeDtypeStruct((N_DEV,) + a.shape, a.dtype) for a in arrays]
        self.aliases = {}
        self.n_sems = (N_DEV - 1) * len(arrays)

    def _peers(self):
        x, y, c = _mesh_pos()
        flips = [((r >> 2) & 1, (r >> 1) & 1, r & 1) for r in range(1, N_DEV)]
        return (x, y, c), [(x ^ fx, y ^ fy, c ^ fc) for fx, fy, fc in flips]

    def _sends(self, ins, outs, send, recv):
        (x, y, c), peers = self._peers()
        mine = 4 * x + 2 * y + c
        return [_remote(ins[a], outs[a].at[mine], send.at[7 * a + r], recv.at[7 * a + r], peer)
                for a in range(len(ins)) for r, peer in enumerate(peers)]

    def start(self, ins, outs, send, recv):
        for cp in self._sends(ins, outs, send, recv):
            cp.start()

    def finish(self, ins, outs, send, recv):
        (x, y, c), peers = self._peers()
        for a in range(len(ins)):
            for r, (px, py, pc) in enumerate(peers):
                blk = outs[a].at[4 * px + 2 * py + pc]
                _remote(blk, blk, send.at[7 * a + r], recv.at[7 * a + r], (x, y, c)).wait_recv()
        for cp in self._sends(ins, outs, send, recv):
            cp.wait_send()


def _pallas(body, operands, *, name, grid, in_specs, out_specs, out_shape, scratch_shapes=(), vmem_mib=32, riders=(),
            prefetch=None):
    in_specs, out_specs, out_shape, scratch_shapes = list(in_specs), list(out_specs), list(out_shape), list(scratch_shapes)
    if not riders and prefetch is None:
        outs = pl.pallas_call(body, name=name, grid=grid, in_specs=in_specs, out_specs=out_specs, out_shape=out_shape,
                              scratch_shapes=scratch_shapes, compiler_params=_params(vmem_mib, len(grid)))(*operands)
        return list(outs), []
    n_in, n_out, n_scr = len(in_specs), len(out_specs), len(scratch_shapes)
    r_in = [len(r.ins) for r in riders]
    r_out = [len(r.out_shapes) for r in riders]
    steps = 1
    for g in grid:
        steps *= g

    n_pre = 0 if prefetch is None else 1

    def wrapped(*refs):
        refs = list(refs)
        pre, refs = refs[:n_pre], refs[n_pre:]
        ins, refs = refs[:n_in], refs[n_in:]
        rins = []
        for k in r_in:
            rins.append(refs[:k])
            refs = refs[k:]
        outs, refs = refs[:n_out], refs[n_out:]
        routs = []
        for k in r_out:
            routs.append(refs[:k])
            refs = refs[k:]
        scr, sems = refs[:n_scr], refs[n_scr:]
        step = 0
        for ax, g in enumerate(grid):
            step = step * g + pl.program_id(ax)

        def each(what):
            for j, r in enumerate(riders):
                if hasattr(r, what):
                    getattr(r, what)(rins[j], routs[j], sems[2 * j], sems[2 * j + 1])

        if grid:
            pl.when(step == 0)(lambda: each("start"))
        else:
            each("start")
        body(*pre, *ins, *outs, *scr)
        if grid:
            @pl.when(step == steps - 1)
            def _():
                each("near_end")
                each("finish")
        else:
            each("near_end")
            each("finish")

    aliases, off_in, off_out = {}, n_pre + n_in, n_out
    for r, ki, ko in zip(riders, r_in, r_out):
        for i, o in r.aliases.items():
            aliases[off_in + i] = off_out + o
        off_in, off_out = off_in + ki, off_out + ko
    sems = []
    for r in riders:
        sems += [pltpu.SemaphoreType.DMA((r.n_sems,)), pltpu.SemaphoreType.DMA((r.n_sems,))]
    layout = dict(grid=grid, in_specs=in_specs + [ANY] * sum(r_in), out_specs=out_specs + [ANY] * sum(r_out),
                  scratch_shapes=scratch_shapes + sems)
    if prefetch is not None:
        layout = dict(grid_spec=pltpu.PrefetchScalarGridSpec(num_scalar_prefetch=1, **layout))
    res = pl.pallas_call(
        wrapped, name=name, **layout,
        out_shape=out_shape + [s for r in riders for s in r.out_shapes], input_output_aliases=aliases,
        compiler_params=pltpu.CompilerParams(dimension_semantics=("arbitrary",) * len(grid),
                                             vmem_limit_bytes=vmem_mib * MIB, has_side_effects=True),
    )(*([] if prefetch is None else [prefetch]), *operands, *[a for r in riders for a in r.ins])
    res = list(res)
    outs, res = res[:n_out], res[n_out:]
    routs = []
    for k in r_out:
        routs.append(res[:k])
        res = res[k:]
    return outs, routs


def _exchange(riders, name):
    return _pallas(lambda: None, [], name=name, grid=(), in_specs=[], out_specs=[], out_shape=[], riders=riders)[1]


def _in_hbm(a):
    return pltpu.with_memory_space_constraint(a, pltpu.HBM)


def _place_shard(w, layer, dtype, name):
    _, rows, cols = w.shape
    half = rows // 2
    br = _block_rows(half)
    nb = half // br
    mine = 2 * lax.axis_index("x") + lax.axis_index("y")

    def body(q_ref, w_ref, o_ref):
        o_ref[...] = w_ref[...].astype(dtype)

    return pl.pallas_call(
        body, name=name,
        grid_spec=pltpu.PrefetchScalarGridSpec(
            num_scalar_prefetch=1, grid=(2, nb),
            in_specs=[pl.BlockSpec((None, br, cols), lambda h, i, q: (layer, h * nb + i, 0))],
            out_specs=pl.BlockSpec((None, None, br, cols), lambda h, i, q: (q[0], h, i, 0))),
        out_shape=pltpu.HBM((N_CHIPS, 2, half, cols), dtype),
        compiler_params=_params(16, 2),
    )(jnp.reshape(mine, (1,)).astype(jnp.int32), w)


def _add_pair(g, recv, name):
    _, _, r, cdim = g.shape
    br = _block_rows(r, 256)
    c = lax.axis_index("c")

    def body(c_ref, g_ref, r_ref, o_ref):
        o_ref[...] = (g_ref[...] + r_ref[...]).astype(BF16)

    return pl.pallas_call(
        body, name=name,
        grid_spec=pltpu.PrefetchScalarGridSpec(
            num_scalar_prefetch=1, grid=(N_CHIPS, r // br),
            in_specs=[pl.BlockSpec((None, None, br, cdim), lambda q, i, c_ref: (q, c_ref[0], i, 0)),
                      pl.BlockSpec((None, br, cdim), lambda q, i, c_ref: (q, i, 0))],
            out_specs=pl.BlockSpec((None, br, cdim), lambda q, i, c_ref: (q, i, 0))),
        out_shape=pltpu.HBM((N_CHIPS, r, cdim), BF16),
        compiler_params=_params(16, 2),
    )(jnp.reshape(c, (1,)).astype(jnp.int32), _in_hbm(g), _in_hbm(recv))


def _add_chips(own, recv, name):
    _, r, cdim = own.shape
    br = _block_rows(r, 256)
    x, y, c = _mesh_pos()

    def body(pos_ref, own_ref, r_ref, o_ref):
        acc = own_ref[...].astype(F32)
        for k in range(3):
            acc = acc + r_ref[k].astype(F32)
        o_ref[...] = acc

    return pl.pallas_call(
        body, name=name,
        grid_spec=pltpu.PrefetchScalarGridSpec(
            num_scalar_prefetch=1, grid=(r // br,),
            in_specs=[pl.BlockSpec((None, br, cdim), lambda i, pos: (pos[0], i, 0)),
                      pl.BlockSpec((3, br, cdim), lambda i, pos: (0, i, 0))],
            out_specs=pl.BlockSpec((None, br, cdim), lambda i, pos: (pos[1], i, 0))),
        out_shape=pltpu.HBM((2, r, cdim), F32),
        compiler_params=_params(16, 1),
    )(jnp.stack([2 * x + y, c]).astype(jnp.int32), _in_hbm(own), _in_hbm(recv))


def _adam_math(w, m, v, g):
    c1 = 1.0 / (1.0 - ADAM_B1 ** ADAM_STEP)
    c2 = 1.0 / (1.0 - ADAM_B2 ** ADAM_STEP)
    m_new = ADAM_B1 * m + (1.0 - ADAM_B1) * g
    v_new = ADAM_B2 * v + (1.0 - ADAM_B2) * (g * g)
    return -ADAM_LR * ((m_new * c1) / (jnp.sqrt(v_new * c2) + ADAM_EPS) + ADAM_WD * w), m_new, v_new


SMALL_WEIGHTS = [
    ("ev_norm_g", (1, D_MODEL), ["ev_norm_g"], None), ("ev_conv_a_b", (1, A_DIM), ["ev_conv_a_b"], None),
    ("ev_ln_a_g", (1, A_DIM), ["ev_ln_a_g"], None), ("ev_ln_a_b", (1, A_DIM), ["ev_ln_a_b"], None),
    ("od_w_s", (C_GROUPS, CHUNK, CHUNK), ["od_w_s_lo", "od_w_s_hi"], None), ("od_b_s", (C_GROUPS, CHUNK), ["od_b_s"], None),
    ("mlp_norm_g", (2, D_MODEL), ["mlp_norm_g0", "mlp_norm_g1"], None), ("final_norm_g", (1, D_MODEL), ["final_norm_g"], None),
    ("ev_conv_a_w", (A_CONV_WIDTH, A_DIM // N_CHIPS), ["ev_conv_a_w"], A_DIM // N_CHIPS),
    ("ev_conv_b_w", (B_CONV_WIDTH, B_DIM // N_CHIPS), ["ev_conv_b_w"], B_DIM // N_CHIPS),
    ("od_norm_g", (1, D_MODEL // N_CHIPS), ["od_norm_g"], D_MODEL // N_CHIPS),
    ("od_b_in", (1, 2 * C_DIM // N_CHIPS), ["od_b_in"], 2 * C_DIM // N_CHIPS),
    ("od_ln_v_g", (1, C_DIM // N_CHIPS), ["od_ln_v_g"], C_DIM // N_CHIPS),
    ("od_ln_v_b", (1, C_DIM // N_CHIPS), ["od_ln_v_b"], C_DIM // N_CHIPS),
]


def _small_update(own, landed, weights):
    names = list(own.keys())
    n_g, n_w = len(names), len(SMALL_WEIGHTS)

    def body(*refs):
        refs = list(refs)
        own_refs = dict(zip(names, refs[:n_g]))
        land_refs = dict(zip(names, refs[n_g:2 * n_g]))
        wmv = [refs[2 * n_g + 3 * i:2 * n_g + 3 * i + 3] for i in range(n_w)]
        o0 = 2 * n_g + 3 * n_w
        loss_ref = refs[o0]
        outs = [refs[o0 + 1 + 4 * i:o0 + 5 + 4 * i] for i in range(n_w)]
        acc = dict(zip(names, refs[o0 + 1 + 4 * n_w:]))
        x, y, c = _mesh_pos()
        mine, chip = 4 * x + 2 * y + c, 2 * x + y

        for nm in names:
            for d in range(N_DEV):
                def add(term, nm=nm, d=d):
                    acc[nm][...] = term if d == 0 else acc[nm][...] + term
                pl.when(mine == d)(lambda nm=nm, add=add: add(own_refs[nm][...]))
                pl.when(mine != d)(lambda nm=nm, d=d, add=add: add(land_refs[nm][d]))
        loss_ref[...] = acc["loss"][...]

        def update(i, rows, g):
            w_ref, m_ref, v_ref = wmv[i]
            delta, m_new, v_new = _adam_math(w_ref[rows], m_ref[rows], v_ref[rows], g)
            for ref, val in zip(outs[i], (g, delta, m_new, v_new)):
                ref[rows] = val

        for i, (_, shape, grads, per_chip) in enumerate(SMALL_WEIGHTS):
            for row, gname in enumerate(grads):
                per_grad = shape[0] // len(grads)
                rows = slice(row * per_grad, (row + 1) * per_grad)
                if per_chip is None:
                    update(i, rows, acc[gname][...])
                else:
                    for q in range(N_CHIPS):
                        pl.when(chip == q)(lambda i=i, rows=rows, gname=gname, q=q, per_chip=per_chip:
                                           update(i, rows, acc[gname][:, q * per_chip:(q + 1) * per_chip]))

    operands = [own[nm] for nm in names] + [landed[nm] for nm in names]
    for nm, _, _, _ in SMALL_WEIGHTS:
        operands += list(weights[nm])
    out_shape = [jax.ShapeDtypeStruct((1, 1), F32)]
    for _, shape, _, _ in SMALL_WEIGHTS:
        out_shape += [jax.ShapeDtypeStruct(shape, F32)] * 4
    res = pl.pallas_call(
        body, name="small_update", grid=(1,),
        in_specs=[_full_spec(a.shape) for a in operands], out_specs=[_full_spec(s.shape) for s in out_shape],
        out_shape=out_shape, scratch_shapes=[pltpu.VMEM(own[nm].shape, F32) for nm in names],
        compiler_params=_params(32, 1),
    )(*[_in_hbm(a) for a in operands])
    return res[0], {nm: res[1 + 4 * i:5 + 4 * i] for i, (nm, _, _, _) in enumerate(SMALL_WEIGHTS)}


def _adamw(w, m, v, grads, name, riders=()):
    layers, r, cdim = w.shape
    br = _block_rows(r, 256 if cdim > LANES else 1024)

    def body(*refs):
        w_ref, m_ref, v_ref = refs[:3]
        g_refs = refs[3:3 + layers]
        go_ref, d_ref, mo_ref, vo_ref = refs[3 + layers:]
        layer = pl.program_id(0)
        for l in range(layers):
            @pl.when(layer == l)
            def _(l=l):
                g = g_refs[l][...]
                go_ref[...] = g
                d_ref[...], mo_ref[...], vo_ref[...] = _adam_math(w_ref[...], m_ref[...], v_ref[...], g)

    spec3 = pl.BlockSpec((None, br, cdim), lambda l, i: (l, i, 0))
    spec2 = pl.BlockSpec((br, cdim), lambda l, i: (i, 0))
    out = jax.ShapeDtypeStruct((layers, r, cdim), F32)
    return _pallas(body, [w, m, v, *[_in_hbm(g) for g in grads]], name=name, grid=(layers, r // br),
                   in_specs=[spec3, spec3, spec3] + [spec2] * layers, out_specs=[spec3] * 4, out_shape=[out] * 4,
                   vmem_mib=32, riders=riders)


def _fill_shifted(buf, rows):
    for b in range(1, SUBLANES):
        buf[b, 0:rows - SUBLANES, :] = buf[0, b:b + rows - SUBLANES, :]


def _window(buf, start, size):
    return buf[start % SUBLANES, start - start % SUBLANES:start - start % SUBLANES + size, :]


def _conv31(src, w_ref, r0, base, init):
    acc = init
    for k in range(A_CONV_WIDTH):
        acc = acc + w_ref[k:k + 1, :] * _window(src, base + k + r0, CONV_ROWS)
    return acc


def _fwd_even(x, norm_g, w_in, conv_a_w, conv_a_b, ln_g, ln_b, conv_b_w, w_out, *, tm, seq, riders=()):
    tokens = x.shape[0]
    nt, tps = tokens // tm, seq // tm

    def body(x_ref, g_ref, win_hbm, caw_ref, cab_ref, lng_ref, lnb_ref, cbw_ref, wout_hbm,
             h_ref, n_ref, z_ref, a2_ref, cv_ref, mix_ref, win_v, wout_v, pa, pb, sem):
        i = pl.program_id(0)

        _load_weights([(win_hbm, win_v), (wout_hbm, wout_v)], sem)

        xv = x_ref[...]
        nf, _ = _rms_fwd(xv, g_ref[...])
        n = nf.astype(BF16)
        n_ref[...] = n
        z = jnp.concatenate([_dot(n, win_v[j]) for j in range(N_CHIPS)], axis=1)
        z_ref[...] = z.astype(BF16)
        a_val, a_gate = z[:, 0:A_DIM], z[:, A_DIM:2 * A_DIM]
        b_gate, c_gate, b_val = z[:, 1024:1536], z[:, 1536:2048], z[:, 2048:2560]

        first = (i % tps) == 0

        @pl.when(first)
        def _():
            pa[0, 0:A_HALO, :] = jnp.zeros((A_HALO, A_DIM), F32)
            pb[0:B_HALO, :] = jnp.zeros((B_HALO, B_DIM), F32)

        @pl.when(jnp.logical_not(first))
        def _():
            pa[0, 0:A_HALO, :] = pa[0, tm:tm + A_HALO, :]
            pb[0:B_HALO, :] = pb[tm:tm + B_HALO, :]

        pa[0, A_HALO:A_HALO + tm, :] = a_val * jax.nn.sigmoid(a_gate)
        pb[B_HALO:B_HALO + tm, :] = c_gate * b_val
        _fill_shifted(pa, A_HALO + tm)
        bias = jnp.broadcast_to(cab_ref[...], (CONV_ROWS, A_DIM))
        for r0 in range(0, tm, CONV_ROWS):
            a2_ref[r0:r0 + CONV_ROWS, :] = _conv31(pa, caw_ref, r0, A_HALO - (A_CONV_WIDTH - 1), bias)
        xhat, _ = _ln_stats(a2_ref[...])
        a3 = xhat * lng_ref[...] + lnb_ref[...]
        a4 = a3 * jax.nn.sigmoid(a3)
        cv = cbw_ref[0:1, :] * pb[B_HALO - 2:B_HALO - 2 + tm, :]
        cv = cv + cbw_ref[1:2, :] * pb[B_HALO - 1:B_HALO - 1 + tm, :]
        cv = cv + cbw_ref[2:3, :] * pb[B_HALO:B_HALO + tm, :]
        cv_ref[...] = cv.astype(BF16)
        mix = jnp.concatenate([a4, b_gate * cv], axis=1).astype(BF16)
        mix_ref[...] = mix
        h_ref[...] = xv + _dot(mix, wout_v[...])

    shp = lambda cols, dt: jax.ShapeDtypeStruct((tokens, cols), dt)
    return _pallas(
        body, [x, norm_g, w_in, conv_a_w, conv_a_b, ln_g, ln_b, conv_b_w, w_out], name="fwd_even", grid=(nt,),
        in_specs=[_row_spec(tm, D_MODEL), _full_spec((1, D_MODEL)), ANY, _full_spec((A_CONV_WIDTH, A_DIM)),
                  _full_spec((1, A_DIM)), _full_spec((1, A_DIM)), _full_spec((1, A_DIM)),
                  _full_spec((B_CONV_WIDTH, B_DIM)), ANY],
        out_specs=[_row_spec(tm, D_MODEL), _row_spec(tm, D_MODEL), _row_spec(tm, IN_EVEN), _row_spec(tm, A_DIM),
                   _row_spec(tm, B_DIM), _row_spec(tm, D_MODEL)],
        out_shape=[shp(D_MODEL, F32), shp(D_MODEL, BF16), shp(IN_EVEN, BF16), shp(A_DIM, F32), shp(B_DIM, BF16),
                   shp(D_MODEL, BF16)],
        scratch_shapes=[pltpu.VMEM((N_CHIPS, D_MODEL, IN_EVEN // N_CHIPS), BF16), pltpu.VMEM((D_MODEL, D_MODEL), BF16),
                        pltpu.VMEM((SUBLANES, A_HALO + tm, A_DIM), F32), pltpu.VMEM((B_HALO + tm, B_DIM), F32),
                        pltpu.SemaphoreType.DMA((N_LOADS,))],
        vmem_mib=56, riders=riders)


def _loss_tail(xv, g, target, loss_ref, dh_ref, dhb_ref, dg_ref):
    @pl.when(pl.program_id(0) == 0)
    def _():
        loss_ref[...] = jnp.zeros((1, 1), F32)
        dg_ref[...] = jnp.zeros((1, D_MODEL), F32)

    out, rstd = _rms_fwd(xv, g)
    err = out - target
    per_token = jnp.sum(err * err, axis=1, keepdims=True) * (1.0 / D_MODEL)
    loss_ref[...] += 0.5 * jnp.sum(per_token, axis=0, keepdims=True)
    dx, dg = _rms_bwd(err * (1.0 / D_MODEL), xv, rstd, g)
    dh_ref[...] = dx
    dhb_ref[...] = dx.astype(BF16)
    dg_ref[...] += dg


def _fwd_mlp(h, norm_g, w1, w2, layer, *, tm, riders=(), head=None):
    tokens = h.shape[0]
    nt = tokens // tm
    fs = D_FF // N_CHIPS
    n_in = 4 if head is None else 6

    def body(*refs):
        h_ref, g_ref, w1_hbm, w2_hbm = refs[:4]
        w1_v, w2_v, sem = refs[-3:]
        outs = refs[n_in:-3]
        n_ref, p_ref, q_ref = outs[1:4] if head is None else outs[0:3]
        _load_weights([(w1_hbm, w1_v), (w2_hbm, w2_v)], sem)

        xv = h_ref[...]
        nf, _ = _rms_fwd(xv, g_ref[...])
        n = nf.astype(BF16)
        n_ref[...] = n
        acc = xv
        for j in range(N_CHIPS):
            p = _dot(n, w1_v[j])
            p_ref[:, j * fs:(j + 1) * fs] = p.astype(BF16)
            r = jnp.maximum(p, 0.0)
            q = (r * r).astype(BF16)
            q_ref[:, j * fs:(j + 1) * fs] = q
            acc = acc + _dot(q, w2_v[j])
        if head is None:
            outs[0][...] = acc
        else:
            _loss_tail(acc, refs[4][...], refs[5][...], *outs[3:7])

    shp = lambda cols, dt: jax.ShapeDtypeStruct((tokens, cols), dt)
    saved_specs = [_row_spec(tm, D_MODEL), _row_spec(tm, D_FF), _row_spec(tm, D_FF)]
    saved_shapes = [shp(D_MODEL, BF16), shp(D_FF, BF16), shp(D_FF, BF16)]
    if head is None:
        operands, in_specs = [h, norm_g, w1, w2], [_row_spec(tm, D_MODEL), _full_spec((1, D_MODEL)), ANY, ANY]
        out_specs, out_shape = [_row_spec(tm, D_MODEL)] + saved_specs, [shp(D_MODEL, F32)] + saved_shapes
    else:
        operands = [h, norm_g, w1, w2, *head]
        in_specs = [_row_spec(tm, D_MODEL), _full_spec((1, D_MODEL)), ANY, ANY, _full_spec((1, D_MODEL)), _row_spec(tm, D_MODEL)]
        out_specs = saved_specs + [_full_spec((1, 1)), _row_spec(tm, D_MODEL), _row_spec(tm, D_MODEL), _full_spec((1, D_MODEL))]
        out_shape = saved_shapes + [jax.ShapeDtypeStruct((1, 1), F32), shp(D_MODEL, F32), shp(D_MODEL, BF16),
                                    jax.ShapeDtypeStruct((1, D_MODEL), F32)]
    return _pallas(
        body, operands, name=f"fwd_mlp{layer}", grid=(nt,), in_specs=in_specs, out_specs=out_specs, out_shape=out_shape,
        scratch_shapes=[pltpu.VMEM((N_CHIPS, D_MODEL, fs), BF16), pltpu.VMEM((N_CHIPS, fs, D_MODEL), BF16),
                        pltpu.SemaphoreType.DMA((N_LOADS,))],
        vmem_mib=56, riders=riders)


def _tril_mask():
    row = lax.broadcasted_iota(jnp.int32, (CHUNK, CHUNK), 0)
    col = lax.broadcasted_iota(jnp.int32, (CHUNK, CHUNK), 1)
    return row >= col


def _triu_mask():
    row = lax.broadcasted_iota(jnp.int32, (CHUNK, CHUNK), 0)
    col = lax.broadcasted_iota(jnp.int32, (CHUNK, CHUNK), 1)
    return row <= col


def _fwd_odd(h, norm_g, w_in, b_in, ln_g, ln_b, w_s, b_s_rows, w_out, *, tm, riders=()):
    tokens = h.shape[0]
    nt = tokens // tm
    cs = 2 * C_DIM // N_CHIPS

    def body(h_ref, g_ref, win_hbm, bin_ref, lng_ref, lnb_ref, ws_ref, bs_ref, wout_hbm,
             ho_ref, n_ref, s_ref, cdf_ref, sv_ref, y_ref, win_v, wout_v, bd, sem):
        _load_weights([(win_hbm, win_v), (wout_hbm, wout_v)], sem)

        @pl.when(pl.program_id(0) == 0)
        def _():
            mask = _tril_mask()
            bd[...] = jnp.zeros(bd.shape, BF16)
            for g in range(C_GROUPS):
                w = jnp.where(mask, ws_ref[g], 0.0).astype(BF16)
                bd[g, 0:CHUNK, 0:CHUNK] = w
                bd[g, CHUNK:PAIR, CHUNK:PAIR] = w

        xv = h_ref[...]
        nf, _ = _rms_fwd(xv, g_ref[...])
        n = nf.astype(BF16)
        n_ref[...] = n
        s = jnp.concatenate([_dot(n, win_v[j]) for j in range(N_CHIPS)], axis=1) + bin_ref[...]
        s_ref[...] = s.astype(BF16)
        cdf = _gelu_cdf(s)
        cdf_ref[...] = cdf.astype(BF16)
        zz = s * cdf
        u, v = zz[:, 0:C_DIM], zz[:, C_DIM:2 * C_DIM]
        xhat, _ = _ln_stats(v)
        vn = (xhat * lng_ref[...] + lnb_ref[...]).astype(BF16)
        for g in range(C_GROUPS):
            cols = slice(g * CHUNK, (g + 1) * CHUNK)
            bias = jnp.concatenate([bs_ref[g], bs_ref[g]], axis=0)
            for r0 in range(0, tm, PAIR):
                sv = _dot(bd[g], vn[r0:r0 + PAIR, cols]) + bias
                sv_ref[r0:r0 + PAIR, cols] = sv.astype(BF16)
                y_ref[r0:r0 + PAIR, cols] = (u[r0:r0 + PAIR, cols] * sv).astype(BF16)
        ho_ref[...] = xv + _dot(y_ref[...], wout_v[...])

    shp = lambda cols, dt: jax.ShapeDtypeStruct((tokens, cols), dt)
    return _pallas(
        body, [h, norm_g, w_in, b_in, ln_g, ln_b, w_s, b_s_rows, w_out], name="fwd_odd", grid=(nt,),
        in_specs=[_row_spec(tm, D_MODEL), _full_spec((1, D_MODEL)), ANY, _full_spec((1, 2 * C_DIM)),
                  _full_spec((1, C_DIM)), _full_spec((1, C_DIM)), _full_spec((C_GROUPS, CHUNK, CHUNK)),
                  _full_spec((C_GROUPS, CHUNK, CHUNK)), ANY],
        out_specs=[_row_spec(tm, D_MODEL), _row_spec(tm, D_MODEL), _row_spec(tm, 2 * C_DIM), _row_spec(tm, 2 * C_DIM),
                   _row_spec(tm, C_DIM), _row_spec(tm, C_DIM)],
        out_shape=[shp(D_MODEL, F32), shp(D_MODEL, BF16), shp(2 * C_DIM, BF16), shp(2 * C_DIM, BF16), shp(C_DIM, BF16),
                   shp(C_DIM, BF16)],
        scratch_shapes=[pltpu.VMEM((N_CHIPS, D_MODEL, cs), BF16), pltpu.VMEM((C_DIM, D_MODEL), BF16),
                        pltpu.VMEM((C_GROUPS, PAIR, PAIR), BF16), pltpu.SemaphoreType.DMA((N_LOADS,))],
        vmem_mib=56, riders=riders)


def _bwd_mlp(dh, h, norm_g, p, w1, w2, layer, *, tm, riders=()):
    tokens = h.shape[0]
    nt = tokens // tm
    fs = D_FF // N_CHIPS

    def body(dh_ref, h_ref, g_ref, p_ref, w1_hbm, w2_hbm, dx_ref, dxb_ref, dp_ref, dg_ref, w1_v, w2_v, sem):
        @pl.when(pl.program_id(0) == 0)
        def _():
            dg_ref[...] = jnp.zeros((1, D_MODEL), F32)

        _load_weights([(w1_hbm, w1_v), (w2_hbm, w2_v)], sem)

        dhv = dh_ref[...]
        dhb = dhv.astype(BF16)
        dn = jnp.zeros((tm, D_MODEL), F32)
        for j in range(N_CHIPS):
            dq = _dot_nt(dhb, w2_v[j])
            r = jnp.maximum(p_ref[:, j * fs:(j + 1) * fs].astype(F32), 0.0)
            dp = ((2.0 * r) * dq).astype(BF16)
            dp_ref[:, j * fs:(j + 1) * fs] = dp
            dn = dn + _dot_nt(dp, w1_v[j])
        xv = h_ref[...]
        g = g_ref[...]
        _, rstd = _rms_fwd(xv, g)
        dx, dg = _rms_bwd(dn, xv, rstd, g)
        dx_ref[...] = dhv + dx
        dxb_ref[...] = (dhv + dx).astype(BF16)
        dg_ref[...] += dg

    return _pallas(
        body, [dh, h, norm_g, p, w1, w2], name=f"bwd_mlp{layer}", grid=(nt,),
        in_specs=[_row_spec(tm, D_MODEL), _row_spec(tm, D_MODEL), _full_spec((1, D_MODEL)), _row_spec(tm, D_FF), ANY, ANY],
        out_specs=[_row_spec(tm, D_MODEL), _row_spec(tm, D_MODEL), _row_spec(tm, D_FF), _full_spec((1, D_MODEL))],
        out_shape=[jax.ShapeDtypeStruct((tokens, D_MODEL), F32), jax.ShapeDtypeStruct((tokens, D_MODEL), BF16),
                   jax.ShapeDtypeStruct((tokens, D_FF), BF16), jax.ShapeDtypeStruct((1, D_MODEL), F32)],
        scratch_shapes=[pltpu.VMEM((N_CHIPS, D_MODEL, fs), BF16), pltpu.VMEM((N_CHIPS, fs, D_MODEL), BF16),
                        pltpu.SemaphoreType.DMA((N_LOADS,))],
        vmem_mib=56, riders=riders)


def _bwd_odd(dh, h, norm_g, s, cdf, sv, w_in, ln_g, ln_b, w_s, w_out, *, tm, riders=()):
    tokens = h.shape[0]
    nt = tokens // tm
    cs = 2 * C_DIM // N_CHIPS

    def body(dh_ref, h_ref, g_ref, s_ref, cdf_ref, sv_ref, win_hbm, lng_ref, lnb_ref, ws_ref, wout_hbm,
             dx_ref, dxb_ref, ds_ref, dg_ref, dbin_ref, dlng_ref, dlnb_ref, dws_ref, dbs_ref,
             win_v, wout_v, bdt, dws_acc, dbs_acc, dvn, sem):
        i = pl.program_id(0)

        _load_weights([(win_hbm, win_v), (wout_hbm, wout_v)], sem)

        @pl.when(i == 0)
        def _():
            mask_t = _triu_mask()
            bdt[...] = jnp.zeros(bdt.shape, BF16)
            for g in range(C_GROUPS):
                wt = jnp.where(mask_t, ws_ref[g].T, 0.0).astype(BF16)
                bdt[g, 0:CHUNK, 0:CHUNK] = wt
                bdt[g, CHUNK:PAIR, CHUNK:PAIR] = wt
            dws_acc[...] = jnp.zeros(dws_acc.shape, F32)
            dbs_acc[...] = jnp.zeros(dbs_acc.shape, F32)
            dg_ref[...] = jnp.zeros(dg_ref.shape, F32)
            dbin_ref[...] = jnp.zeros(dbin_ref.shape, F32)
            dlng_ref[...] = jnp.zeros(dlng_ref.shape, F32)
            dlnb_ref[...] = jnp.zeros(dlnb_ref.shape, F32)

        dhv = dh_ref[...]
        dy = _dot_nt(dhv.astype(BF16), wout_v[...])
        sf = s_ref[...].astype(F32)
        cdf = cdf_ref[...].astype(F32)
        pdf = jnp.exp(-0.5 * sf * sf) * 0.3989422804014327
        zz = sf * cdf
        dgelu = cdf + sf * pdf
        u, v = zz[:, 0:C_DIM], zz[:, C_DIM:2 * C_DIM]
        xhat, rs = _ln_stats(v)
        lng = lng_ref[...]
        vn = (xhat * lng + lnb_ref[...]).astype(BF16)
        du = dy * sv_ref[...].astype(F32)
        dsv = dy * u
        dsvb = dsv.astype(BF16)
        for g in range(C_GROUPS):
            cols = slice(g * CHUNK, (g + 1) * CHUNK)
            for r0 in range(0, tm, PAIR):
                blk = dsvb[r0:r0 + PAIR, cols]
                dvn[r0:r0 + PAIR, cols] = _dot(bdt[g], blk)
                dws_acc[g] += _dot_nt(blk, vn[r0:r0 + PAIR, cols])
                dbs_acc[g] += dsv[r0:r0 + CHUNK, cols] + dsv[r0 + CHUNK:r0 + PAIR, cols]
        dv, dlng, dlnb = _ln_bwd(dvn[...], xhat, rs, lng)
        dlng_ref[...] += dlng
        dlnb_ref[...] += dlnb
        ds = jnp.concatenate([du, dv], axis=1) * dgelu
        dbin_ref[...] += jnp.sum(ds, axis=0, keepdims=True)
        dsb = ds.astype(BF16)
        ds_ref[...] = dsb
        dn = jnp.zeros((tm, D_MODEL), F32)
        for j in range(N_CHIPS):
            dn = dn + _dot_nt(dsb[:, j * cs:(j + 1) * cs], win_v[j])
        xv = h_ref[...]
        g = g_ref[...]
        _, rstd = _rms_fwd(xv, g)
        dx, dg = _rms_bwd(dn, xv, rstd, g)
        dx_ref[...] = dhv + dx
        dxb_ref[...] = (dhv + dx).astype(BF16)
        dg_ref[...] += dg

        @pl.when(i == nt - 1)
        def _():
            mask = _tril_mask()
            for g in range(C_GROUPS):
                full = dws_acc[g]
                dws_ref[g] = jnp.where(mask, full[0:CHUNK, 0:CHUNK] + full[CHUNK:PAIR, CHUNK:PAIR], 0.0)
                dbs_ref[g:g + 1, :] = jnp.sum(dbs_acc[g].T, axis=0, keepdims=True)

    row = lambda cols: jax.ShapeDtypeStruct((1, cols), F32)
    return _pallas(
        body, [dh, h, norm_g, s, cdf, sv, w_in, ln_g, ln_b, w_s, w_out], name="bwd_odd", grid=(nt,),
        in_specs=[_row_spec(tm, D_MODEL), _row_spec(tm, D_MODEL), _full_spec((1, D_MODEL)), _row_spec(tm, 2 * C_DIM),
                  _row_spec(tm, 2 * C_DIM), _row_spec(tm, C_DIM), ANY, _full_spec((1, C_DIM)), _full_spec((1, C_DIM)),
                  _full_spec((C_GROUPS, CHUNK, CHUNK)), ANY],
        out_specs=[_row_spec(tm, D_MODEL), _row_spec(tm, D_MODEL), _row_spec(tm, 2 * C_DIM), _full_spec((1, D_MODEL)),
                   _full_spec((1, 2 * C_DIM)),
                   _full_spec((1, C_DIM)), _full_spec((1, C_DIM)), _full_spec((C_GROUPS, CHUNK, CHUNK)),
                   _full_spec((C_GROUPS, CHUNK))],
        out_shape=[jax.ShapeDtypeStruct((tokens, D_MODEL), F32), jax.ShapeDtypeStruct((tokens, D_MODEL), BF16),
                   jax.ShapeDtypeStruct((tokens, 2 * C_DIM), BF16),
                   row(D_MODEL), row(2 * C_DIM), row(C_DIM), row(C_DIM),
                   jax.ShapeDtypeStruct((C_GROUPS, CHUNK, CHUNK), F32), jax.ShapeDtypeStruct((C_GROUPS, CHUNK), F32)],
        scratch_shapes=[pltpu.VMEM((N_CHIPS, D_MODEL, cs), BF16), pltpu.VMEM((C_DIM, D_MODEL), BF16),
                        pltpu.VMEM((C_GROUPS, PAIR, PAIR), BF16), pltpu.VMEM((C_GROUPS, PAIR, PAIR), F32),
                        pltpu.VMEM((C_GROUPS, CHUNK, CHUNK), F32), pltpu.VMEM((tm, C_DIM), F32),
                        pltpu.SemaphoreType.DMA((N_LOADS,))],
        vmem_mib=56, riders=riders)


def _bwd_even(dh, x, norm_g, z, a2, cv, w_in, conv_a_w, ln_g, ln_b, conv_b_w, w_out, *, tm, seq, riders=()):
    tokens = x.shape[0]
    nt, tps = tokens // tm, seq // tm
    ws = IN_EVEN // N_CHIPS

    def body(dh_ref, x_ref, g_ref, z_ref, a2_ref, cv_ref, win_hbm, caw_ref, lng_ref, lnb_ref, cbw_ref, wout_hbm,
             dx_ref, dz_ref, dg_ref, dcaw_ref, dcab_ref, dlng_ref, dlnb_ref, dcbw_ref,
             win_v, wout_v, ea, eb, a1s, da1s, sigs, wide, dw_acc, sem):
        i = pl.program_id(0)

        _load_weights([(win_hbm, win_v), (wout_hbm, wout_v)], sem)

        @pl.when(i == 0)
        def _():
            dw_acc[...] = jnp.zeros(dw_acc.shape, F32)
            for ref in (dg_ref, dcab_ref, dlng_ref, dlnb_ref, dcbw_ref):
                ref[...] = jnp.zeros(ref.shape, F32)

        last = ((nt - 1 - i) % tps) == tps - 1

        @pl.when(last)
        def _():
            ea[0, tm:tm + A_HALO, :] = jnp.zeros((A_HALO, A_DIM), F32)
            eb[tm:tm + B_HALO, :] = jnp.zeros((B_HALO, B_DIM), F32)

        @pl.when(jnp.logical_not(last))
        def _():
            ea[0, tm:tm + A_HALO, :] = ea[0, 0:A_HALO, :]
            eb[tm:tm + B_HALO, :] = eb[0:B_HALO, :]

        wide[...] = _dot_nt(dh_ref[...].astype(BF16), wout_v[...])
        lng, lnb = lng_ref[...], lnb_ref[...]
        zero_row = jnp.zeros((1, A_DIM), F32)
        dlng, dlnb, dcab = zero_row, zero_row, zero_row
        for r0 in range(0, tm, ELEM_ROWS):
            rows = slice(r0, r0 + ELEM_ROWS)
            a_val, a_gate = z_ref[rows, 0:A_DIM].astype(F32), z_ref[rows, A_DIM:2 * A_DIM].astype(F32)
            xhat, rs = _ln_stats(a2_ref[rows, :])
            a3 = xhat * lng + lnb
            sg = jax.nn.sigmoid(a3)
            da3 = wide[rows, 0:A_DIM] * (sg * (1.0 + a3 * (1.0 - sg)))
            da2, g_part, b_part = _ln_bwd(da3, xhat, rs, lng)
            dlng, dlnb, dcab = dlng + g_part, dlnb + b_part, dcab + jnp.sum(da2, axis=0, keepdims=True)
            ea[0, rows, :] = da2
            eb[rows, :] = wide[rows, A_DIM:A_DIM + B_DIM] * z_ref[rows, 1024:1536].astype(F32)
            sig = jax.nn.sigmoid(a_gate)
            sigs[rows, :] = sig
            a1s[rows, :] = a_val * sig
        dlng_ref[...] += dlng
        dlnb_ref[...] += dlnb
        dcab_ref[...] += dcab
        _fill_shifted(ea, tm + A_HALO)
        for r0 in range(0, tm, CONV_ROWS):
            acc = jnp.zeros((CONV_ROWS, A_DIM), F32)
            for j in range(A_CONV_WIDTH):
                acc = acc + caw_ref[A_CONV_WIDTH - 1 - j:A_CONV_WIDTH - j, :] * _window(ea, r0 + j, CONV_ROWS)
            da1s[r0:r0 + CONV_ROWS, :] = acc
        for j0 in range(0, A_CONV_WIDTH, DW_TAPS):
            taps = range(j0, min(j0 + DW_TAPS, A_CONV_WIDTH))
            part = [jnp.zeros((CONV_ROWS, A_DIM), F32) for _ in taps]
            for r0 in range(0, tm, CONV_ROWS):
                a1c = a1s[r0:r0 + CONV_ROWS, :]
                for u, j in enumerate(taps):
                    part[u] = part[u] + _window(ea, r0 + j, CONV_ROWS) * a1c
            for u, j in enumerate(taps):
                dw_acc[A_CONV_WIDTH - 1 - j] += part[u]
        dcbw = [jnp.zeros((1, B_DIM), F32) for _ in range(B_CONV_WIDTH)]
        for r0 in range(0, tm, ELEM_ROWS):
            rows = slice(r0, r0 + ELEM_ROWS)
            da1, sig = da1s[rows, :], sigs[rows, :]
            dz_ref[rows, 0:A_DIM] = (da1 * sig).astype(BF16)
            dz_ref[rows, A_DIM:2 * A_DIM] = (da1 * z_ref[rows, 0:A_DIM].astype(F32) * (sig * (1.0 - sig))).astype(BF16)
            c_gate, b_val = z_ref[rows, 1536:2048].astype(F32), z_ref[rows, 2048:2560].astype(F32)
            dz_ref[rows, 1024:1536] = (wide[rows, A_DIM:A_DIM + B_DIM] * cv_ref[rows, :].astype(F32)).astype(BF16)
            cb = c_gate * b_val
            dcb = jnp.zeros((ELEM_ROWS, B_DIM), F32)
            for j in range(B_CONV_WIDTH):
                k = B_CONV_WIDTH - 1 - j
                sl = eb[r0 + j:r0 + j + ELEM_ROWS, :]
                dcb = dcb + cbw_ref[k:k + 1, :] * sl
                dcbw[k] = dcbw[k] + jnp.sum(sl * cb, axis=0, keepdims=True)
            dz_ref[rows, 1536:2048] = (dcb * b_val).astype(BF16)
            dz_ref[rows, 2048:2560] = (dcb * c_gate).astype(BF16)
        for k in range(B_CONV_WIDTH):
            dcbw_ref[k:k + 1, :] += dcbw[k]
        dn = jnp.zeros((tm, D_MODEL), F32)
        for j in range(N_CHIPS):
            dn = dn + _dot_nt(dz_ref[:, j * ws:(j + 1) * ws], win_v[j])
        wide[...] = dn
        g = g_ref[...]
        dg = jnp.zeros((1, D_MODEL), F32)
        for r0 in range(0, tm, ELEM_ROWS):
            rows = slice(r0, r0 + ELEM_ROWS)
            xv = x_ref[rows, :]
            _, rstd = _rms_fwd(xv, g)
            dx, dg_part = _rms_bwd(wide[rows, :], xv, rstd, g)
            dx_ref[rows, :] = dh_ref[rows, :] + dx
            dg = dg + dg_part
        dg_ref[...] += dg

        @pl.when(i == nt - 1)
        def _():
            for k in range(A_CONV_WIDTH):
                dcaw_ref[k:k + 1, :] = jnp.sum(dw_acc[k], axis=0, keepdims=True)

    row = lambda cols: jax.ShapeDtypeStruct((1, cols), F32)
    rs_ = functools.partial(_row_spec, rev_nt=nt)
    return _pallas(
        body, [dh, x, norm_g, z, a2, cv, w_in, conv_a_w, ln_g, ln_b, conv_b_w, w_out], name="bwd_even", grid=(nt,),
        in_specs=[rs_(tm, D_MODEL), rs_(tm, D_MODEL), _full_spec((1, D_MODEL)), rs_(tm, IN_EVEN), rs_(tm, A_DIM),
                  rs_(tm, B_DIM), ANY, _full_spec((A_CONV_WIDTH, A_DIM)), _full_spec((1, A_DIM)), _full_spec((1, A_DIM)),
                  _full_spec((B_CONV_WIDTH, B_DIM)), ANY],
        out_specs=[rs_(tm, D_MODEL), rs_(tm, IN_EVEN), _full_spec((1, D_MODEL)), _full_spec((A_CONV_WIDTH, A_DIM)),
                   _full_spec((1, A_DIM)), _full_spec((1, A_DIM)), _full_spec((1, A_DIM)), _full_spec((B_CONV_WIDTH, B_DIM))],
        out_shape=[jax.ShapeDtypeStruct((tokens, D_MODEL), F32), jax.ShapeDtypeStruct((tokens, IN_EVEN), BF16),
                   row(D_MODEL), jax.ShapeDtypeStruct((A_CONV_WIDTH, A_DIM), F32), row(A_DIM), row(A_DIM), row(A_DIM),
                   jax.ShapeDtypeStruct((B_CONV_WIDTH, B_DIM), F32)],
        scratch_shapes=[pltpu.VMEM((N_CHIPS, D_MODEL, ws), BF16), pltpu.VMEM((D_MODEL, D_MODEL), BF16),
                        pltpu.VMEM((SUBLANES, tm + A_HALO, A_DIM), F32), pltpu.VMEM((tm + B_HALO, B_DIM), F32),
                        pltpu.VMEM((tm, A_DIM), F32), pltpu.VMEM((tm, A_DIM), F32), pltpu.VMEM((tm, A_DIM), F32),
                        pltpu.VMEM((tm, D_MODEL), F32),
                        pltpu.VMEM((A_CONV_WIDTH, CONV_ROWS, A_DIM), F32), pltpu.SemaphoreType.DMA((N_LOADS,))],
        vmem_mib=56, riders=riders)


def _wgrad(a, b, name, *, col_shards, riders=()):
    tokens, m = a.shape
    n = b.shape[1]
    kc = 512
    if col_shards:
        bm, bn = m // 2, n // N_CHIPS
        grid = (2, N_CHIPS)
        out_spec = pl.BlockSpec((None, None, bm, bn), lambda i, j: (j, i, 0, 0))
    elif m // 8 >= MXU_ROWS:
        bm, bn = m // 8, n
        grid = (8, 1)
        out_spec = pl.BlockSpec((None, None, bm, bn), lambda i, j: (i // 2, i % 2, 0, 0))
    else:
        bm, bn = m // N_CHIPS, n
        grid = (N_CHIPS, 1)
        out_spec = pl.BlockSpec((None, 2, bm // 2, bn), lambda i, j: (i, 0, 0, 0))

    def body(a_ref, b_ref, o_ref):
        acc = jnp.zeros((bm, bn), F32)
        for k0 in range(0, tokens, kc):
            acc = acc + _dot_tn(a_ref[k0:k0 + kc, :].astype(BF16), b_ref[k0:k0 + kc, :].astype(BF16))
        if len(o_ref.shape) == 3:
            o_ref[0] = acc[0:bm // 2]
            o_ref[1] = acc[bm // 2:bm]
        else:
            o_ref[...] = acc

    out_rows = m // 2 if col_shards else m // 8
    outs, routs = _pallas(
        body, [a, b], name=name, grid=grid,
        in_specs=[pl.BlockSpec((tokens, bm), lambda i, j: (0, i)), pl.BlockSpec((tokens, bn), lambda i, j: (0, j))],
        out_specs=[out_spec], out_shape=[jax.ShapeDtypeStruct((N_CHIPS, 2, out_rows, bn), F32)],
        vmem_mib=56, riders=riders)
    return outs[0], routs


def _wgrad_pair(a, b, name, *, col_shards, riders=(), to_chips=False):
    tokens, m = a.shape
    n = b.shape[1]
    kc = 512
    x0, y0, c0 = _mesh_pos()
    rot = 1 if to_chips else 0
    phases = [0, 0, 1, 0, 1, 0, 1, 1] if to_chips else [0, 0, 0, 0, 1, 1, 1, 1]
    tiles = [0, 1, 0, 2, 1, 3, 2, 3] if to_chips else [0, 1, 2, 3, 0, 1, 2, 3]
    out_tiles = [0, 0, 0, 0, 1, 1, 2, 3] if to_chips else [0, 0, 0, 0, 0, 1, 2, 3]
    steps = len(phases)
    P0, T0, O0 = 2, 2 + steps, 2 + 2 * steps

    def slab(t, pre):
        return (t + rot * (1 + pre[1])) % N_CHIPS

    def half(s, pre):
        return (pre[P0 + s] + 1 + pre[0]) % 2

    if col_shards:
        bm, bn = m // 2, n // N_CHIPS
        a_spec = pl.BlockSpec((tokens, bm), lambda s, pre: (0, half(s, pre)))
        b_spec = pl.BlockSpec((tokens, bn), lambda s, pre: (0, slab(pre[T0 + s], pre)))
    else:
        bm, bn = m // 8, n
        a_spec = pl.BlockSpec((tokens, bm), lambda s, pre: (0, 2 * slab(pre[T0 + s], pre) + half(s, pre)))
        b_spec = pl.BlockSpec((tokens, bn), lambda s, pre: (0, 0))

    def body(pre_ref, a_ref, b_ref, o_ref, *rest):
        if to_chips:
            land, give, got, mine, send_sems, recv_sems, chip_send, chip_recv = rest
        else:
            give, got, send_sems, recv_sems = rest
        step = pl.program_id(0)
        ph, q = pre_ref[P0 + step], pre_ref[T0 + step]
        acc = jnp.zeros((bm, bn), F32)
        for k0 in range(0, tokens, kc):
            acc = acc + _dot_tn(a_ref[k0:k0 + kc, :].astype(BF16), b_ref[k0:k0 + kc, :].astype(BF16))
        x, y, cc = _mesh_pos()

        def tile(t):
            return _remote(give.at[t], got.at[t], send_sems.at[t], recv_sems.at[t], (x, y, 1 - cc))

        def to_chip(s):
            t = (s + 1 + 2 * x + y) % N_CHIPS
            tx, ty = t // 2, t % 2
            k = 2 * (ty ^ y) + (tx ^ x) - 1
            return _remote(mine.at[s], land.at[k], chip_send.at[k], chip_recv.at[k], (tx, ty, cc))

        @pl.when(ph == 0)
        def _():
            give[q] = acc
            tile(q).start()

        @pl.when(ph == 1)
        def _():
            tile(q).wait_recv()
            total = (acc + got[q]).astype(BF16)
            o_ref[...] = total
            if to_chips:
                for s in range(N_CHIPS - 1):
                    @pl.when(q == s)
                    def _(s=s):
                        mine[s] = total
                        to_chip(s).start()

        @pl.when(step == steps - 1)
        def _():
            for t in range(N_CHIPS):
                tile(t).wait_send()
            if to_chips:
                for s in range(N_CHIPS - 1):
                    to_chip(s).wait()

    prefetch = jnp.concatenate([jnp.stack([c0, 2 * x0 + y0]).astype(jnp.int32),
                                jnp.asarray(phases + tiles + out_tiles, jnp.int32)])
    out_specs = [pl.BlockSpec((None, bm, bn), lambda s, pre: (slab(pre[O0 + s], pre), 0, 0))]
    out_shape = [jax.ShapeDtypeStruct((N_CHIPS, bm, bn), BF16)]
    scratch = [pltpu.VMEM((N_CHIPS, bm, bn), F32), pltpu.VMEM((N_CHIPS, bm, bn), F32)]
    sems = [pltpu.SemaphoreType.DMA((N_CHIPS,)), pltpu.SemaphoreType.DMA((N_CHIPS,))]
    if to_chips:
        out_specs.append(ANY)
        out_shape.append(jax.ShapeDtypeStruct((N_CHIPS - 1, bm, bn), BF16))
        scratch.append(pltpu.VMEM((N_CHIPS - 1, bm, bn), BF16))
        sems += [pltpu.SemaphoreType.DMA((N_CHIPS - 1,)), pltpu.SemaphoreType.DMA((N_CHIPS - 1,))]
    outs, routs = _pallas(
        body, [a, b], name=name, grid=(steps,), in_specs=[a_spec, b_spec], out_specs=out_specs, out_shape=out_shape,
        scratch_shapes=scratch + sems, vmem_mib=56, riders=riders, prefetch=prefetch)
    return (outs if to_chips else outs[0]), routs


class _GradReduce:
    def __init__(self, name, grad=None, chip_sum=None):
        self.name, self.grad, self.chip_sum = name, grad, chip_sum
        self.full = None

    def pair_swap(self):
        return _PairSwap([self.grad])

    def took_pair(self, outs):
        self.chip_sum = _in_hbm(_add_pair(self.grad, outs[0], f"pair_sum_{self.name}"))

    def chip_swap(self):
        return _ChipSwap([self.chip_sum])

    def took_chips(self, outs):
        self.full = _in_hbm(_add_chips(self.chip_sum, outs[0], f"chip_sum_{self.name}"))

    def pair_share(self):
        return _PairShare([self.full])

    def took_share(self, outs):
        self.full = outs[0]

    def reduced(self):
        return jnp.reshape(self.full, (2 * self.full.shape[1], self.full.shape[2]))


def _forward_backward(x2, tgt2, gathered, staged, conv_a_w, conv_b_w, od_norm, od_bias, od_lng, od_lnb,
                      ev_norm_g, ev_conv_a_b, ev_ln_a_g, ev_ln_a_b, od_w_s, od_b_s, mlp_norm_g, final_norm_g,
                      *, tm, seq, distributed=True):
    d = x2.shape[1]
    w = dict(gathered)
    b_s_rows = jnp.broadcast_to(od_b_s[0][:, :, None], (C_GROUPS, CHUNK, CHUNK))

    def ride(*names):
        return [_Gather([staged[nm] for nm in names])] if distributed else []

    def land(routs, *names):
        if distributed:
            for nm, buf in zip(names, routs[0]):
                w[nm] = buf

    def as_cols(buf):
        return jnp.reshape(buf, (N_CHIPS, 2 * buf.shape[2], buf.shape[3]))

    def as_rows(buf):
        return jnp.reshape(buf, (8 * buf.shape[2], buf.shape[3]))

    (h1, n0, z, a2, cv, mix), routs = _fwd_even(
        x2, ev_norm_g, as_cols(w["ev_in"]), conv_a_w, ev_conv_a_b, ev_ln_a_g, ev_ln_a_b, conv_b_w, as_rows(w["ev_out"]),
        tm=tm, seq=seq, riders=ride("w1_0", "w2_0"))
    land(routs, "w1_0", "w2_0")
    (h2, n1, p0, q0), routs = _fwd_mlp(h1, mlp_norm_g[0:1], as_cols(w["w1_0"]), as_cols(w["w2_0"]), 0, tm=tm,
                                       riders=ride("od_in", "od_out", "w1_1"))
    land(routs, "od_in", "od_out", "w1_1")
    (h3, n2, s, cdf, sv, y), routs = _fwd_odd(h2, od_norm, as_cols(w["od_in"]), od_bias, od_lng, od_lnb, od_w_s[0], b_s_rows,
                                         as_rows(w["od_out"]), tm=tm, riders=ride("w2_1"))
    land(routs, "w2_1")
    (n3, p1, q1, loss_part, dh4, dh4b, d_final_g), _ = _fwd_mlp(
        h3, mlp_norm_g[1:2], as_cols(w["w1_1"]), as_cols(w["w2_1"]), 1, tm=tm,
        head=(jnp.reshape(final_norm_g, (1, d)), tgt2))

    red = {}

    def swap(*names):
        return [red[nm].pair_swap() for nm in names] if distributed else []

    def chips(*names):
        return [red[nm].chip_swap() for nm in names] if distributed else []

    def share(*names):
        return [red[nm].pair_share() for nm in names] if distributed else []

    def took(routs, *steps):
        if distributed:
            for (nm, what), outs in zip(steps, routs):
                getattr(red[nm], what)(outs)

    def big(lhs, rhs, name, col_shards, riders=(), to_chips=False):
        if distributed and to_chips:
            (chip_sum, from_chips), routs = _wgrad_pair(lhs, rhs, f"wgrad_{name}", col_shards=col_shards, riders=riders,
                                                        to_chips=True)
            red[name] = _GradReduce(name, chip_sum=_in_hbm(chip_sum))
            red[name].took_chips([_in_hbm(from_chips)])
        elif distributed:
            chip_sum, routs = _wgrad_pair(lhs, rhs, f"wgrad_{name}", col_shards=col_shards, riders=riders)
            red[name] = _GradReduce(name, chip_sum=_in_hbm(chip_sum))
        else:
            g, routs = _wgrad(lhs, rhs, f"wgrad_{name}", col_shards=col_shards)
            red[name] = _GradReduce(name, grad=g)
        return routs

    big(q1, dh4b, "w2_1", False)
    (dh3, dh3b, dp1, d_mlp_g1), routs = _bwd_mlp(dh4, h3, mlp_norm_g[1:2], p1, as_cols(w["w1_1"]), as_cols(w["w2_1"]), 1, tm=tm,
                                           riders=chips("w2_1"))
    took(routs, ("w2_1", "took_chips"))
    big(n3, dp1, "w1_1", True)
    g, routs = _wgrad(y, dh3b, "wgrad_od_out", col_shards=False, riders=share("w2_1"))
    red["od_out"] = _GradReduce("od_out", grad=g)
    took(routs, ("w2_1", "took_share"))
    (dh2, dh2b, ds, d_od_norm, d_od_bin, d_od_lng, d_od_lnb, d_ws, d_bs), routs = _bwd_odd(
        dh3, h2, od_norm, s, cdf, sv, as_cols(w["od_in"]), od_lng, od_lnb, od_w_s[0], as_rows(w["od_out"]), tm=tm,
        riders=chips("w1_1") + swap("od_out"))
    took(routs, ("w1_1", "took_chips"), ("od_out", "took_pair"))
    routs = big(n2, ds, "od_in", True, riders=share("w1_1"))
    took(routs, ("w1_1", "took_share"))
    half_groups = C_GROUPS // 2
    early = {"loss": loss_part, "od_w_s_lo": d_ws[:half_groups], "od_b_s": d_bs, "mlp_norm_g1": d_mlp_g1, "final_norm_g": d_final_g,
             "od_norm_g": d_od_norm, "od_b_in": d_od_bin, "od_ln_v_g": d_od_lng, "od_ln_v_b": d_od_lnb}
    share_early = [_ShareAll(list(early.values()))] if distributed else []
    routs = big(q0, dh2b, "w2_0", False, riders=share_early)
    landed_early = routs[0] if distributed else []
    (dh1, dh1b, dp0, d_mlp_g0), routs = _bwd_mlp(dh2, h1, mlp_norm_g[0:1], p0, as_cols(w["w1_0"]), as_cols(w["w2_0"]), 0, tm=tm,
                                           riders=chips("od_out") + chips("od_in") + chips("w2_0"))
    took(routs, ("od_out", "took_chips"), ("od_in", "took_chips"), ("w2_0", "took_chips"))
    middle = {"od_w_s_hi": d_ws[half_groups:]}
    share_middle = [_ShareAll(list(middle.values()))] if distributed else []
    g, _ = _wgrad(mix, dh1b, "wgrad_ev_out", col_shards=False)
    red["ev_out"] = _GradReduce("ev_out", grad=g)
    routs = big(n1, dp0, "w1_0", True,
                riders=share("od_out") + share("od_in") + share("w2_0") + share_middle + swap("ev_out"))
    took(routs, ("od_out", "took_share"), ("od_in", "took_share"), ("w2_0", "took_share"))
    landed_middle = routs[3] if distributed else []
    if distributed:
        red["ev_out"].took_pair(routs[4])

    (dx, dz, d_ev_norm, d_caw, d_cab, d_ev_lng, d_ev_lnb, d_cbw), routs = _bwd_even(
        dh1, x2, ev_norm_g, z, a2, cv, as_cols(w["ev_in"]), conv_a_w, ev_ln_a_g, ev_ln_a_b, conv_b_w, as_rows(w["ev_out"]),
        tm=tm, seq=seq, riders=chips("w1_0") + chips("ev_out"))
    took(routs, ("w1_0", "took_chips"), ("ev_out", "took_chips"))
    late = {"mlp_norm_g0": d_mlp_g0, "ev_norm_g": d_ev_norm, "ev_conv_a_b": d_cab, "ev_ln_a_g": d_ev_lng,
            "ev_ln_a_b": d_ev_lnb, "ev_conv_a_w": d_caw, "ev_conv_b_w": d_cbw}
    share_late = [_ShareAll(list(late.values()))] if distributed else []
    routs2 = big(n0, dz, "ev_in", True, riders=share("ev_out") + share("w1_0") + share_late, to_chips=True)
    took(routs2, ("ev_out", "took_share"), ("w1_0", "took_share"))
    own = {**early, **middle, **late}
    landed = dict(zip(own.keys(), landed_early + landed_middle + routs2[2])) if distributed else None
    return dx, red, own, landed


def _rows128(a):
    rows = jnp.reshape(a, (-1, LANES))
    pad = (-rows.shape[0]) % SUBLANES
    return jnp.pad(rows, ((0, pad), (0, 0))) if pad else rows


def _pack(arrays):
    return jnp.concatenate([_rows128(a) for a in arrays], axis=0)


def _unpack(buf, shapes):
    out, r0 = [], 0
    for shp in shapes:
        size = 1
        for dim in shp:
            size *= dim
        nr = size // LANES
        out.append(jnp.reshape(buf[r0:r0 + nr], shp))
        r0 += nr + (-nr) % SUBLANES
    return out


def kernel(x, ev_norm_g, ev_w_in, ev_conv_a_w, ev_conv_a_b, ev_ln_a_g, ev_ln_a_b, ev_conv_b_w, ev_w_out, od_norm_g, od_w_in, od_b_in, od_ln_v_g, od_ln_v_b, od_w_s, od_b_s, od_w_out, mlp_norm_g, mlp_w1, mlp_w2, final_norm_g, loss_target, m_ev_norm_g, m_ev_w_in, m_ev_conv_a_w, m_ev_conv_a_b, m_ev_ln_a_g, m_ev_ln_a_b, m_ev_conv_b_w, m_ev_w_out, m_od_norm_g, m_od_w_in, m_od_b_in, m_od_ln_v_g, m_od_ln_v_b, m_od_w_s, m_od_b_s, m_od_w_out, m_mlp_norm_g, m_mlp_w1, m_mlp_w2, m_final_norm_g, v_ev_norm_g, v_ev_w_in, v_ev_conv_a_w, v_ev_conv_a_b, v_ev_ln_a_g, v_ev_ln_a_b, v_ev_conv_b_w, v_ev_w_out, v_od_norm_g, v_od_w_in, v_od_b_in, v_od_ln_v_g, v_od_ln_v_b, v_od_w_s, v_od_b_s, v_od_w_out, v_mlp_norm_g, v_mlp_w1, v_mlp_w2, v_final_norm_g):
    tm = TOKEN_TILE
    batch, seq, d = x.shape
    tokens = batch * seq
    x2 = jnp.reshape(x, (tokens, d))
    tgt2 = jnp.reshape(loss_target, (tokens, d))
    chip = 2 * lax.axis_index("x") + lax.axis_index("y")

    small_shapes = [(A_CONV_WIDTH, LANES), (B_CONV_WIDTH, LANES), (256,), (512,), (256,), (256,)]
    small_shard = _pack([ev_conv_a_w[0], ev_conv_b_w[0], od_norm_g[0], od_b_in[0], od_ln_v_g[0], od_ln_v_b[0]])
    small_shard = jnp.pad(small_shard, ((0, (-small_shard.shape[0]) % (2 * SUBLANES)), (0, 0)))
    first = [_place_shard(ev_w_in, 0, BF16, "place_ev_w_in"), _place_shard(ev_w_out, 0, BF16, "place_ev_w_out"),
             _place_shard(small_shard[None], 0, F32, "place_small")]
    staged = {
        "w1_0": _place_shard(mlp_w1, 0, BF16, "place_w1_0"), "w2_0": _place_shard(mlp_w2, 0, BF16, "place_w2_0"),
        "od_in": _place_shard(od_w_in, 0, BF16, "place_od_w_in"), "od_out": _place_shard(od_w_out, 0, BF16, "place_od_w_out"),
        "w1_1": _place_shard(mlp_w1, 1, BF16, "place_w1_1"), "w2_1": _place_shard(mlp_w2, 1, BF16, "place_w2_1"),
    }
    first = [_in_hbm(a) for a in first]
    staged = {nm: _in_hbm(a) for nm, a in staged.items()}
    (g_ev_in, g_ev_out, g_small), = _exchange([_Gather(first)], "gather_first")
    small_all = jnp.reshape(g_small, (N_CHIPS, -1, LANES))
    per_chip = [_unpack(small_all[q], small_shapes) for q in range(N_CHIPS)]
    conv_a_w = jnp.concatenate([pc[0] for pc in per_chip], axis=1)
    conv_b_w = jnp.concatenate([pc[1] for pc in per_chip], axis=1)
    od_norm = jnp.concatenate([pc[2] for pc in per_chip])[None, :]
    od_bias = jnp.concatenate([pc[3] for pc in per_chip])[None, :]
    od_lng = jnp.concatenate([pc[4] for pc in per_chip])[None, :]
    od_lnb = jnp.concatenate([pc[5] for pc in per_chip])[None, :]

    dx, red, own, landed = _forward_backward(
        x2, tgt2, {"ev_in": g_ev_in, "ev_out": g_ev_out}, staged, conv_a_w, conv_b_w, od_norm, od_bias, od_lng, od_lnb,
        ev_norm_g, ev_conv_a_b, ev_ln_a_g, ev_ln_a_b, od_w_s, od_b_s, mlp_norm_g, final_norm_g, tm=tm, seq=seq)

    routs = _exchange([red["ev_in"].pair_share()], "reduce_tail")
    red["ev_in"].took_share(routs[0])

    given = {"ev_norm_g": (ev_norm_g, m_ev_norm_g, v_ev_norm_g), "ev_conv_a_b": (ev_conv_a_b, m_ev_conv_a_b, v_ev_conv_a_b),
             "ev_ln_a_g": (ev_ln_a_g, m_ev_ln_a_g, v_ev_ln_a_g), "ev_ln_a_b": (ev_ln_a_b, m_ev_ln_a_b, v_ev_ln_a_b),
             "od_w_s": (od_w_s, m_od_w_s, v_od_w_s), "od_b_s": (od_b_s, m_od_b_s, v_od_b_s),
             "mlp_norm_g": (mlp_norm_g, m_mlp_norm_g, v_mlp_norm_g), "final_norm_g": (final_norm_g, m_final_norm_g, v_final_norm_g),
             "ev_conv_a_w": (ev_conv_a_w, m_ev_conv_a_w, v_ev_conv_a_w), "ev_conv_b_w": (ev_conv_b_w, m_ev_conv_b_w, v_ev_conv_b_w),
             "od_norm_g": (od_norm_g, m_od_norm_g, v_od_norm_g), "od_b_in": (od_b_in, m_od_b_in, v_od_b_in),
             "od_ln_v_g": (od_ln_v_g, m_od_ln_v_g, v_od_ln_v_g), "od_ln_v_b": (od_ln_v_b, m_od_ln_v_b, v_od_ln_v_b)}
    shaped = {nm: tuple(jnp.reshape(a, shape) for a in given[nm]) for nm, shape, _, _ in SMALL_WEIGHTS}
    loss11, small_upd = _small_update(own, landed, shaped)
    loss = loss11[0, 0]
    upd = {nm: [jnp.reshape(o, given[nm][0].shape) for o in outs] for nm, outs in small_upd.items()}

    def big_update(wt, m, v, names, call):
        grads = [red[nm].reduced() for nm in names]
        shp3 = (len(grads),) + grads[0].shape
        outs, _ = _adamw(jnp.reshape(wt, shp3), jnp.reshape(m, shp3), jnp.reshape(v, shp3), grads, call)
        return [jnp.reshape(o, wt.shape) for o in outs], None

    upd["mlp_w2"], _ = big_update(mlp_w2, m_mlp_w2, v_mlp_w2, ["w2_0", "w2_1"], "adamw_mlp_w2")
    upd["mlp_w1"], _ = big_update(mlp_w1, m_mlp_w1, v_mlp_w1, ["w1_0", "w1_1"], "adamw_mlp_w1")
    upd["ev_w_in"], _ = big_update(ev_w_in, m_ev_w_in, v_ev_w_in, ["ev_in"], "adamw_ev_w_in")
    upd["ev_w_out"], _ = big_update(ev_w_out, m_ev_w_out, v_ev_w_out, ["ev_out"], "adamw_ev_w_out")
    upd["od_w_in"], _ = big_update(od_w_in, m_od_w_in, v_od_w_in, ["od_in"], "adamw_od_w_in")
    upd["od_w_out"], _ = big_update(od_w_out, m_od_w_out, v_od_w_out, ["od_out"], "adamw_od_w_out")

    order = ["ev_norm_g", "ev_w_in", "ev_conv_a_w", "ev_conv_a_b", "ev_ln_a_g", "ev_ln_a_b", "ev_conv_b_w", "ev_w_out",
             "od_norm_g", "od_w_in", "od_b_in", "od_ln_v_g", "od_ln_v_b", "od_w_s", "od_b_s", "od_w_out", "mlp_norm_g",
             "mlp_w1", "mlp_w2", "final_norm_g"]
    grad_x = jnp.reshape(dx, x.shape)
    return (loss, grad_x, *[upd[nm][0] for nm in order], *[upd[nm][1] for nm in order],
            *[upd[nm][2] for nm in order], *[upd[nm][3] for nm in order])
```

```python
import functools

import jax
import jax.numpy as jnp
from jax import lax
from jax.experimental import pallas as pl
from jax.experimental.pallas import tpu as pltpu
from jax.experimental.pallas import tpu_sc as plsc

F32 = jnp.float32
BF16 = jnp.bfloat16

D_MODEL = 1024
A_DIM = 512
B_DIM = 512
IN_EVEN = 2 * A_DIM + 3 * B_DIM
A_CONV_WIDTH = 31
B_CONV_WIDTH = 3
CHUNK = 128
C_GROUPS = 8
C_DIM = 1024
D_FF = 4096
RMS_EPS = 1e-6
LN_EPS = 1e-5
ADAM_LR = 0.001
ADAM_B1 = 0.9
ADAM_B2 = 0.999
ADAM_EPS = 1e-08
ADAM_WD = 0.01
ADAM_STEP = 10

N_CHIPS = 4
N_DEV = 8
TOKEN_TILE = 512
A_HALO = 32
B_HALO = 8
CONV_ROWS = 16
DW_TAPS = 4
ELEM_ROWS = 16
PAIR = 2 * CHUNK
LANES = 128
SUBLANES = 8
MXU_ROWS = 256
MIB = 1024 * 1024
MESH = pl.DeviceIdType.MESH
ANY = pl.BlockSpec(memory_space=pl.ANY)


def _dot(a, b):
    return lax.dot_general(a, b, (((1,), (0,)), ((), ())), preferred_element_type=F32)


def _dot_nt(a, b):
    return lax.dot_general(a, b, (((1,), (1,)), ((), ())), preferred_element_type=F32)


def _dot_tn(a, b):
    return lax.dot_general(a, b, (((0,), (0,)), ((), ())), preferred_element_type=F32)


def _params(vmem_mib, n_axes=1):
    return pltpu.CompilerParams(dimension_semantics=("arbitrary",) * n_axes, vmem_limit_bytes=vmem_mib * MIB)


def _row_spec(tm, cols, rev_nt=None):
    if rev_nt is None:
        return pl.BlockSpec((tm, cols), lambda i: (i, 0))
    return pl.BlockSpec((tm, cols), lambda i: (rev_nt - 1 - i, 0))


def _full_spec(shape):
    nd = len(shape)
    return pl.BlockSpec(shape, lambda i: (0,) * nd)


def _block_rows(rows, cap=512):
    best = SUBLANES
    for br in range(SUBLANES, min(rows, cap) + 1, SUBLANES):
        if rows % br == 0:
            best = br
    return best


N_LOADS = 2 * 2 * N_CHIPS


def _load_weights(loads, sems):
    @pl.when(pl.program_id(0) == 0)
    def _():
        copies = []
        for src, dst, rows_of_one in loads:
            r = src.shape[2]
            for q in range(N_CHIPS):
                for h in range(2):
                    part = dst.at[pl.ds((2 * q + h) * r, r)] if rows_of_one else dst.at[q, pl.ds(h * r, r)]
                    copies.append(pltpu.make_async_copy(src.at[q, h], part, sems.at[len(copies)]))
        for cp in copies:
            cp.start()
        for cp in copies:
            cp.wait()


def _rms_fwd(x, g):
    rstd = lax.rsqrt(jnp.mean(x * x, axis=-1, keepdims=True) + RMS_EPS)
    return x * rstd * g, rstd


def _rms_bwd(dn, x, rstd, g):
    a = dn * g
    xh = x * rstd
    dx = rstd * (a - xh * jnp.mean(a * xh, axis=-1, keepdims=True))
    dg = jnp.sum(dn * xh, axis=0, keepdims=True)
    return dx, dg


def _ln_stats(v):
    mu = jnp.mean(v, axis=-1, keepdims=True)
    xc = v - mu
    rs = lax.rsqrt(jnp.mean(xc * xc, axis=-1, keepdims=True) + LN_EPS)
    return xc * rs, rs


def _ln_bwd(dy, xhat, rs, g):
    dxh = dy * g
    dv = rs * (dxh - jnp.mean(dxh, axis=-1, keepdims=True) - xhat * jnp.mean(dxh * xhat, axis=-1, keepdims=True))
    return dv, jnp.sum(dy * xhat, axis=0, keepdims=True), jnp.sum(dy, axis=0, keepdims=True)


def _gelu_cdf(s):
    return 0.5 * (1.0 + lax.erf(s * 0.7071067811865476))


def _mesh_pos():
    return lax.axis_index("x"), lax.axis_index("y"), lax.axis_index("c")


def _other_chips(x, y):
    return [(1 - x, y), (x, 1 - y), (1 - x, 1 - y)]


def _remote(src, dst, send_sem, recv_sem, to):
    return pltpu.make_async_remote_copy(src_ref=src, dst_ref=dst, send_sem=send_sem, recv_sem=recv_sem,
                                        device_id=to, device_id_type=MESH)


def _like(arrays):
    return [jax.ShapeDtypeStruct(a.shape, a.dtype) for a in arrays]


class _Gather:
    def __init__(self, bufs):
        self.ins = list(bufs)
        self.out_shapes = _like(bufs)
        self.aliases = {t: t for t in range(len(bufs))}
        self.n_sems = 6 * len(bufs)

    def _ici(self, ins, outs, send, recv, t, k, chip, mine, c):
        return _remote(ins[t].at[mine, c], outs[t].at[mine, c], send.at[6 * t + k], recv.at[6 * t + k], (*chip, c))

    def start(self, ins, outs, send, recv):
        x, y, c = _mesh_pos()
        for t in range(len(ins)):
            for k, chip in enumerate(_other_chips(x, y)):
                self._ici(ins, outs, send, recv, t, k, chip, 2 * x + y, c).start()

    def _pass_on(self, outs, send, recv, t, k, chip, c, to):
        blk = outs[t].at[2 * chip[0] + chip[1], c]
        return _remote(blk, blk, send.at[6 * t + 3 + k], recv.at[6 * t + 3 + k], to)

    def near_end(self, ins, outs, send, recv):
        x, y, c = _mesh_pos()
        for t in range(len(ins)):
            for k, chip in enumerate(_other_chips(x, y)):
                blk = outs[t].at[2 * chip[0] + chip[1], c]
                _remote(blk, blk, send.at[6 * t + k], recv.at[6 * t + k], (x, y, c)).wait_recv()
                self._pass_on(outs, send, recv, t, k, chip, c, (x, y, 1 - c)).start()

    def finish(self, ins, outs, send, recv):
        x, y, c = _mesh_pos()
        chips = _other_chips(x, y)
        for t in range(len(ins)):
            for k, chip in enumerate(chips):
                self._pass_on(outs, send, recv, t, k, chip, 1 - c, (x, y, c)).wait_recv()
        for t in range(len(ins)):
            for k, chip in enumerate(chips):
                self._ici(ins, outs, send, recv, t, k, chip, 2 * x + y, c).wait_send()
                self._pass_on(outs, send, recv, t, k, chip, c, (x, y, 1 - c)).wait_send()


class _PairSwap:
    def __init__(self, grads):
        self.ins = list(grads)
        self.out_shapes = [jax.ShapeDtypeStruct((g.shape[0],) + g.shape[2:], g.dtype) for g in grads]
        self.aliases = {}
        self.n_sems = len(grads)

    def _copies(self, ins, outs, send, recv):
        x, y, c = _mesh_pos()
        return [_remote(ins[t].at[:, 1 - c], outs[t], send.at[t], recv.at[t], (x, y, 1 - c)) for t in range(len(ins))]

    def start(self, ins, outs, send, recv):
        for cp in self._copies(ins, outs, send, recv):
            cp.start()

    def finish(self, ins, outs, send, recv):
        for cp in self._copies(ins, outs, send, recv):
            cp.wait()


class _ChipSwap:
    def __init__(self, parts):
        self.ins = list(parts)
        self.out_shapes = [jax.ShapeDtypeStruct((3,) + p.shape[1:], p.dtype) for p in parts]
        self.aliases = {}
        self.n_sems = 3 * len(parts)

    def _copies(self, ins, outs, send, recv):
        x, y, c = _mesh_pos()
        return [_remote(ins[t].at[2 * chip[0] + chip[1]], outs[t].at[k], send.at[3 * t + k], recv.at[3 * t + k], (*chip, c))
                for t in range(len(ins)) for k, chip in enumerate(_other_chips(x, y))]

    def start(self, ins, outs, send, recv):
        for cp in self._copies(ins, outs, send, recv):
            cp.start()

    def finish(self, ins, outs, send, recv):
        for cp in self._copies(ins, outs, send, recv):
            cp.wait()


class _PairShare:
    def __init__(self, fulls):
        self.ins = list(fulls)
        self.out_shapes = _like(fulls)
        self.aliases = {t: t for t in range(len(fulls))}
        self.n_sems = len(fulls)

    def _copies(self, ins, outs, send, recv):
        x, y, c = _mesh_pos()
        return [_remote(ins[t].at[c], outs[t].at[c], send.at[t], recv.at[t], (x, y, 1 - c)) for t in range(len(ins))]

    def start(self, ins, outs, send, recv):
        for cp in self._copies(ins, outs, send, recv):
            cp.start()

    def finish(self, ins, outs, send, recv):
        for cp in self._copies(ins, outs, send, recv):
            cp.wait()


class _ShareAll:
    def __init__(self, arrays):
        self.ins = list(arrays)
        self.out_shapes = [jax.ShapeDtypeStruct((N_DEV,) + a.shape, a.dtype) for a in arrays]
        self.aliases = {}
        self.n_sems = (N_DEV - 1) * len(arrays)

    def _peers(self):
        x, y, c = _mesh_pos()
        flips = [((r >> 2) & 1, (r >> 1) & 1, r & 1) for r in range(1, N_DEV)]
        return (x, y, c), [(x ^ fx, y ^ fy, c ^ fc) for fx, fy, fc in flips]

    def _sends(self, ins, outs, send, recv):
        (x, y, c), peers = self._peers()
        mine = 4 * x + 2 * y + c
        return [_remote(ins[a], outs[a].at[mine], send.at[7 * a + r], recv.at[7 * a + r], peer)
                for a in range(len(ins)) for r, peer in enumerate(peers)]

    def start(self, ins, outs, send, recv):
        for cp in self._sends(ins, outs, send, recv):
            cp.start()

    def finish(self, ins, outs, send, recv):
        (x, y, c), peers = self._peers()
        for a in range(len(ins)):
            for r, (px, py, pc) in enumerate(peers):
                blk = outs[a].at[4 * px + 2 * py + pc]
                _remote(blk, blk, send.at[7 * a + r], recv.at[7 * a + r], (x, y, c)).wait_recv()
        for cp in self._sends(ins, outs, send, recv):
            cp.wait_send()


def _gather_beside(bufs, name, collective_id):
    n = len(bufs)
    refs = [jax.new_ref(b, memory_space=pltpu.MemorySpace.HBM) for b in bufs]
    gather = _Gather(bufs)

    @pl.kernel(mesh=plsc.ScalarSubcoreMesh(axis_name="sequencer", num_cores=1), name=name,
               scratch_types=(pltpu.SemaphoreType.DMA((6 * n,)), pltpu.SemaphoreType.DMA((6 * n,))),
               compiler_params=pltpu.CompilerParams(collective_id=collective_id))
    def launch(send, recv):
        x, y, c = _mesh_pos()
        barrier = pltpu.get_barrier_semaphore()
        peers = [(*chip, c) for chip in _other_chips(x, y)] + [(x, y, 1 - c)]
        for peer in peers:
            pl.semaphore_signal(barrier, inc=1, device_id=peer, device_id_type=MESH)
        pl.semaphore_wait(barrier, len(peers))
        gather.start(refs, refs, send, recv)
        gather.near_end(refs, refs, send, recv)
        gather.finish(refs, refs, send, recv)

    launch()
    return [r[...] for r in refs]


def _pallas(body, operands, *, name, grid, in_specs, out_specs, out_shape, scratch_shapes=(), vmem_mib=32, riders=(),
            prefetch=None):
    in_specs, out_specs, out_shape, scratch_shapes = list(in_specs), list(out_specs), list(out_shape), list(scratch_shapes)
    if not riders and prefetch is None:
        outs = pl.pallas_call(body, name=name, grid=grid, in_specs=in_specs, out_specs=out_specs, out_shape=out_shape,
                              scratch_shapes=scratch_shapes, compiler_params=_params(vmem_mib, len(grid)))(*operands)
        return list(outs), []
    n_in, n_out, n_scr = len(in_specs), len(out_specs), len(scratch_shapes)
    r_in = [len(r.ins) for r in riders]
    r_out = [len(r.out_shapes) for r in riders]
    steps = 1
    for g in grid:
        steps *= g

    n_pre = 0 if prefetch is None else 1

    def wrapped(*refs):
        refs = list(refs)
        pre, refs = refs[:n_pre], refs[n_pre:]
        ins, refs = refs[:n_in], refs[n_in:]
        rins = []
        for k in r_in:
            rins.append(refs[:k])
            refs = refs[k:]
        outs, refs = refs[:n_out], refs[n_out:]
        routs = []
        for k in r_out:
            routs.append(refs[:k])
            refs = refs[k:]
        scr, sems = refs[:n_scr], refs[n_scr:]
        step = 0
        for ax, g in enumerate(grid):
            step = step * g + pl.program_id(ax)

        def each(what):
            for j, r in enumerate(riders):
                if hasattr(r, what):
                    getattr(r, what)(rins[j], routs[j], sems[2 * j], sems[2 * j + 1])

        if grid:
            pl.when(step == 0)(lambda: each("start"))
        else:
            each("start")
        body(*pre, *ins, *outs, *scr)
        if grid:
            @pl.when(step == steps - 1)
            def _():
                each("near_end")
                each("finish")
        else:
            each("near_end")
            each("finish")

    aliases, off_in, off_out = {}, n_pre + n_in, n_out
    for r, ki, ko in zip(riders, r_in, r_out):
        for i, o in r.aliases.items():
            aliases[off_in + i] = off_out + o
        off_in, off_out = off_in + ki, off_out + ko
    sems = []
    for r in riders:
        sems += [pltpu.SemaphoreType.DMA((r.n_sems,)), pltpu.SemaphoreType.DMA((r.n_sems,))]
    layout = dict(grid=grid, in_specs=in_specs + [ANY] * sum(r_in), out_specs=out_specs + [ANY] * sum(r_out),
                  scratch_shapes=scratch_shapes + sems)
    if prefetch is not None:
        layout = dict(grid_spec=pltpu.PrefetchScalarGridSpec(num_scalar_prefetch=1, **layout))
    res = pl.pallas_call(
        wrapped, name=name, **layout,
        out_shape=out_shape + [s for r in riders for s in r.out_shapes], input_output_aliases=aliases,
        compiler_params=pltpu.CompilerParams(dimension_semantics=("arbitrary",) * len(grid),
                                             vmem_limit_bytes=vmem_mib * MIB, has_side_effects=True),
    )(*([] if prefetch is None else [prefetch]), *operands, *[a for r in riders for a in r.ins])
    res = list(res)
    outs, res = res[:n_out], res[n_out:]
    routs = []
    for k in r_out:
        routs.append(res[:k])
        res = res[k:]
    return outs, routs


def _exchange(riders, name):
    return _pallas(lambda: None, [], name=name, grid=(), in_specs=[], out_specs=[], out_shape=[], riders=riders)[1]


def _in_hbm(a):
    return pltpu.with_memory_space_constraint(a, pltpu.HBM)


def _place_shard(w, layer, dtype, name):
    _, rows, cols = w.shape
    half = rows // 2
    br = _block_rows(half)
    nb = half // br
    mine = 2 * lax.axis_index("x") + lax.axis_index("y")

    def body(q_ref, w_ref, o_ref):
        o_ref[...] = w_ref[...].astype(dtype)

    return pl.pallas_call(
        body, name=name,
        grid_spec=pltpu.PrefetchScalarGridSpec(
            num_scalar_prefetch=1, grid=(2, nb),
            in_specs=[pl.BlockSpec((None, br, cols), lambda h, i, q: (layer, h * nb + i, 0))],
            out_specs=pl.BlockSpec((None, None, br, cols), lambda h, i, q: (q[0], h, i, 0))),
        out_shape=pltpu.HBM((N_CHIPS, 2, half, cols), dtype),
        compiler_params=_params(16, 2),
    )(jnp.reshape(mine, (1,)).astype(jnp.int32), w)


def _add_pair(g, recv, name):
    _, _, r, cdim = g.shape
    br = _block_rows(r, 256)
    c = lax.axis_index("c")

    def body(c_ref, g_ref, r_ref, o_ref):
        o_ref[...] = (g_ref[...] + r_ref[...]).astype(BF16)

    return pl.pallas_call(
        body, name=name,
        grid_spec=pltpu.PrefetchScalarGridSpec(
            num_scalar_prefetch=1, grid=(N_CHIPS, r // br),
            in_specs=[pl.BlockSpec((None, None, br, cdim), lambda q, i, c_ref: (q, c_ref[0], i, 0)),
                      pl.BlockSpec((None, br, cdim), lambda q, i, c_ref: (q, i, 0))],
            out_specs=pl.BlockSpec((None, br, cdim), lambda q, i, c_ref: (q, i, 0))),
        out_shape=pltpu.HBM((N_CHIPS, r, cdim), BF16),
        compiler_params=_params(16, 2),
    )(jnp.reshape(c, (1,)).astype(jnp.int32), _in_hbm(g), _in_hbm(recv))


def _add_chips(own, recv, name):
    _, r, cdim = own.shape
    br = _block_rows(r, 256)
    x, y, c = _mesh_pos()

    def body(pos_ref, own_ref, r_ref, o_ref):
        acc = own_ref[...].astype(F32)
        for k in range(3):
            acc = acc + r_ref[k].astype(F32)
        o_ref[...] = acc

    return pl.pallas_call(
        body, name=name,
        grid_spec=pltpu.PrefetchScalarGridSpec(
            num_scalar_prefetch=1, grid=(r // br,),
            in_specs=[pl.BlockSpec((None, br, cdim), lambda i, pos: (pos[0], i, 0)),
                      pl.BlockSpec((3, br, cdim), lambda i, pos: (0, i, 0))],
            out_specs=pl.BlockSpec((None, br, cdim), lambda i, pos: (pos[1], i, 0))),
        out_shape=pltpu.HBM((2, r, cdim), F32),
        compiler_params=_params(16, 1),
    )(jnp.stack([2 * x + y, c]).astype(jnp.int32), _in_hbm(own), _in_hbm(recv))


def _adam_math(w, m, v, g):
    c1 = 1.0 / (1.0 - ADAM_B1 ** ADAM_STEP)
    c2 = 1.0 / (1.0 - ADAM_B2 ** ADAM_STEP)
    m_new = ADAM_B1 * m + (1.0 - ADAM_B1) * g
    v_new = ADAM_B2 * v + (1.0 - ADAM_B2) * (g * g)
    return -ADAM_LR * ((m_new * c1) / (jnp.sqrt(v_new * c2) + ADAM_EPS) + ADAM_WD * w), m_new, v_new


SMALL_WEIGHTS = [
    ("ev_norm_g", (1, D_MODEL), ["ev_norm_g"], None), ("ev_conv_a_b", (1, A_DIM), ["ev_conv_a_b"], None),
    ("ev_ln_a_g", (1, A_DIM), ["ev_ln_a_g"], None), ("ev_ln_a_b", (1, A_DIM), ["ev_ln_a_b"], None),
    ("od_w_s", (C_GROUPS, CHUNK, CHUNK), ["od_w_s_lo", "od_w_s_hi"], None), ("od_b_s", (C_GROUPS, CHUNK), ["od_b_s"], None),
    ("mlp_norm_g", (2, D_MODEL), ["mlp_norm_g0", "mlp_norm_g1"], None), ("final_norm_g", (1, D_MODEL), ["final_norm_g"], None),
    ("ev_conv_a_w", (A_CONV_WIDTH, A_DIM // N_CHIPS), ["ev_conv_a_w"], A_DIM // N_CHIPS),
    ("ev_conv_b_w", (B_CONV_WIDTH, B_DIM // N_CHIPS), ["ev_conv_b_w"], B_DIM // N_CHIPS),
    ("od_norm_g", (1, D_MODEL // N_CHIPS), ["od_norm_g"], D_MODEL // N_CHIPS),
    ("od_b_in", (1, 2 * C_DIM // N_CHIPS), ["od_b_in"], 2 * C_DIM // N_CHIPS),
    ("od_ln_v_g", (1, C_DIM // N_CHIPS), ["od_ln_v_g"], C_DIM // N_CHIPS),
    ("od_ln_v_b", (1, C_DIM // N_CHIPS), ["od_ln_v_b"], C_DIM // N_CHIPS),
]


def _small_update(own, landed, weights):
    names = list(own.keys())
    n_g, n_w = len(names), len(SMALL_WEIGHTS)

    def body(*refs):
        refs = list(refs)
        own_refs = dict(zip(names, refs[:n_g]))
        land_refs = dict(zip(names, refs[n_g:2 * n_g]))
        wmv = [refs[2 * n_g + 3 * i:2 * n_g + 3 * i + 3] for i in range(n_w)]
        o0 = 2 * n_g + 3 * n_w
        loss_ref = refs[o0]
        outs = [refs[o0 + 1 + 4 * i:o0 + 5 + 4 * i] for i in range(n_w)]
        acc = dict(zip(names, refs[o0 + 1 + 4 * n_w:]))
        x, y, c = _mesh_pos()
        mine, chip = 4 * x + 2 * y + c, 2 * x + y

        for nm in names:
            for d in range(N_DEV):
                def add(term, nm=nm, d=d):
                    acc[nm][...] = term if d == 0 else acc[nm][...] + term
                pl.when(mine == d)(lambda nm=nm, add=add: add(own_refs[nm][...]))
                pl.when(mine != d)(lambda nm=nm, d=d, add=add: add(land_refs[nm][d]))
        loss_ref[...] = acc["loss"][...]

        def update(i, rows, g):
            w_ref, m_ref, v_ref = wmv[i]
            delta, m_new, v_new = _adam_math(w_ref[rows], m_ref[rows], v_ref[rows], g)
            for ref, val in zip(outs[i], (g, delta, m_new, v_new)):
                ref[rows] = val

        for i, (_, shape, grads, per_chip) in enumerate(SMALL_WEIGHTS):
            for row, gname in enumerate(grads):
                per_grad = shape[0] // len(grads)
                rows = slice(row * per_grad, (row + 1) * per_grad)
                if per_chip is None:
                    update(i, rows, acc[gname][...])
                else:
                    for q in range(N_CHIPS):
                        pl.when(chip == q)(lambda i=i, rows=rows, gname=gname, q=q, per_chip=per_chip:
                                           update(i, rows, acc[gname][:, q * per_chip:(q + 1) * per_chip]))

    operands = [own[nm] for nm in names] + [landed[nm] for nm in names]
    for nm, _, _, _ in SMALL_WEIGHTS:
        operands += list(weights[nm])
    out_shape = [jax.ShapeDtypeStruct((1, 1), F32)]
    for _, shape, _, _ in SMALL_WEIGHTS:
        out_shape += [jax.ShapeDtypeStruct(shape, F32)] * 4
    res = pl.pallas_call(
        body, name="small_update", grid=(1,),
        in_specs=[_full_spec(a.shape) for a in operands], out_specs=[_full_spec(s.shape) for s in out_shape],
        out_shape=out_shape, scratch_shapes=[pltpu.VMEM(own[nm].shape, F32) for nm in names],
        compiler_params=_params(32, 1),
    )(*[_in_hbm(a) for a in operands])
    return res[0], {nm: res[1 + 4 * i:5 + 4 * i] for i, (nm, _, _, _) in enumerate(SMALL_WEIGHTS)}


def _adamw(w, m, v, grads, name, riders=()):
    layers, r, cdim = w.shape
    br = _block_rows(r, 256 if cdim > LANES else 1024)

    def body(*refs):
        w_ref, m_ref, v_ref = refs[:3]
        g_refs = refs[3:3 + layers]
        go_ref, d_ref, mo_ref, vo_ref = refs[3 + layers:]
        layer = pl.program_id(0)
        for l in range(layers):
            @pl.when(layer == l)
            def _(l=l):
                g = g_refs[l][...]
                go_ref[...] = g
                d_ref[...], mo_ref[...], vo_ref[...] = _adam_math(w_ref[...], m_ref[...], v_ref[...], g)

    spec3 = pl.BlockSpec((None, br, cdim), lambda l, i: (l, i, 0))
    spec2 = pl.BlockSpec((br, cdim), lambda l, i: (i, 0))
    out = jax.ShapeDtypeStruct((layers, r, cdim), F32)
    return _pallas(body, [w, m, v, *[_in_hbm(g) for g in grads]], name=name, grid=(layers, r // br),
                   in_specs=[spec3, spec3, spec3] + [spec2] * layers, out_specs=[spec3] * 4, out_shape=[out] * 4,
                   vmem_mib=32, riders=riders)


def _fill_shifted(buf, rows):
    for b in range(1, SUBLANES):
        buf[b, 0:rows - SUBLANES, :] = buf[0, b:b + rows - SUBLANES, :]


def _window(buf, start, size):
    return buf[start % SUBLANES, start - start % SUBLANES:start - start % SUBLANES + size, :]


def _conv31(src, w_ref, r0, base, init):
    acc = init
    for k in range(A_CONV_WIDTH):
        acc = acc + w_ref[k:k + 1, :] * _window(src, base + k + r0, CONV_ROWS)
    return acc


def _fwd_even(x, norm_g, w_in, conv_a_w, conv_a_b, ln_g, ln_b, conv_b_w, w_out, *, tm, seq, riders=()):
    tokens = x.shape[0]
    nt, tps = tokens // tm, seq // tm

    def body(x_ref, g_ref, win_hbm, caw_ref, cab_ref, lng_ref, lnb_ref, cbw_ref, wout_hbm,
             h_ref, n_ref, z_ref, a2_ref, cv_ref, mix_ref, win_v, wout_v, pa, pb, sem):
        i = pl.program_id(0)

        _load_weights([(win_hbm, win_v, False), (wout_hbm, wout_v, True)], sem)

        xv = x_ref[...]
        nf, _ = _rms_fwd(xv, g_ref[...])
        n = nf.astype(BF16)
        n_ref[...] = n
        z = jnp.concatenate([_dot(n, win_v[j]) for j in range(N_CHIPS)], axis=1)
        z_ref[...] = z.astype(BF16)
        a_val, a_gate = z[:, 0:A_DIM], z[:, A_DIM:2 * A_DIM]
        b_gate, c_gate, b_val = z[:, 1024:1536], z[:, 1536:2048], z[:, 2048:2560]

        first = (i % tps) == 0

        @pl.when(first)
        def _():
            pa[0, 0:A_HALO, :] = jnp.zeros((A_HALO, A_DIM), F32)
            pb[0:B_HALO, :] = jnp.zeros((B_HALO, B_DIM), F32)

        @pl.when(jnp.logical_not(first))
        def _():
            pa[0, 0:A_HALO, :] = pa[0, tm:tm + A_HALO, :]
            pb[0:B_HALO, :] = pb[tm:tm + B_HALO, :]

        pa[0, A_HALO:A_HALO + tm, :] = a_val * jax.nn.sigmoid(a_gate)
        pb[B_HALO:B_HALO + tm, :] = c_gate * b_val
        _fill_shifted(pa, A_HALO + tm)
        bias = jnp.broadcast_to(cab_ref[...], (CONV_ROWS, A_DIM))
        for r0 in range(0, tm, CONV_ROWS):
            a2_ref[r0:r0 + CONV_ROWS, :] = _conv31(pa, caw_ref, r0, A_HALO - (A_CONV_WIDTH - 1), bias)
        xhat, _ = _ln_stats(a2_ref[...])
        a3 = xhat * lng_ref[...] + lnb_ref[...]
        a4 = a3 * jax.nn.sigmoid(a3)
        cv = cbw_ref[0:1, :] * pb[B_HALO - 2:B_HALO - 2 + tm, :]
        cv = cv + cbw_ref[1:2, :] * pb[B_HALO - 1:B_HALO - 1 + tm, :]
        cv = cv + cbw_ref[2:3, :] * pb[B_HALO:B_HALO + tm, :]
        cv_ref[...] = cv.astype(BF16)
        mix = jnp.concatenate([a4, b_gate * cv], axis=1).astype(BF16)
        mix_ref[...] = mix
        h_ref[...] = xv + _dot(mix, wout_v[...])

    shp = lambda cols, dt: jax.ShapeDtypeStruct((tokens, cols), dt)
    return _pallas(
        body, [x, norm_g, w_in, conv_a_w, conv_a_b, ln_g, ln_b, conv_b_w, w_out], name="fwd_even", grid=(nt,),
        in_specs=[_row_spec(tm, D_MODEL), _full_spec((1, D_MODEL)), ANY, _full_spec((A_CONV_WIDTH, A_DIM)),
                  _full_spec((1, A_DIM)), _full_spec((1, A_DIM)), _full_spec((1, A_DIM)),
                  _full_spec((B_CONV_WIDTH, B_DIM)), ANY],
        out_specs=[_row_spec(tm, D_MODEL), _row_spec(tm, D_MODEL), _row_spec(tm, IN_EVEN), _row_spec(tm, A_DIM),
                   _row_spec(tm, B_DIM), _row_spec(tm, D_MODEL)],
        out_shape=[shp(D_MODEL, F32), shp(D_MODEL, BF16), shp(IN_EVEN, BF16), shp(A_DIM, F32), shp(B_DIM, BF16),
                   shp(D_MODEL, BF16)],
        scratch_shapes=[pltpu.VMEM((N_CHIPS, D_MODEL, IN_EVEN // N_CHIPS), BF16), pltpu.VMEM((D_MODEL, D_MODEL), BF16),
                        pltpu.VMEM((SUBLANES, A_HALO + tm, A_DIM), F32), pltpu.VMEM((B_HALO + tm, B_DIM), F32),
                        pltpu.SemaphoreType.DMA((N_LOADS,))],
        vmem_mib=56, riders=riders)


def _loss_tail(xv, g, target, loss_ref, dh_ref, dhb_ref, dg_ref):
    @pl.when(pl.program_id(0) == 0)
    def _():
        loss_ref[...] = jnp.zeros((1, 1), F32)
        dg_ref[...] = jnp.zeros((1, D_MODEL), F32)

    out, rstd = _rms_fwd(xv, g)
    err = out - target
    per_token = jnp.sum(err * err, axis=1, keepdims=True) * (1.0 / D_MODEL)
    loss_ref[...] += 0.5 * jnp.sum(per_token, axis=0, keepdims=True)
    dx, dg = _rms_bwd(err * (1.0 / D_MODEL), xv, rstd, g)
    dh_ref[...] = dx
    dhb_ref[...] = dx.astype(BF16)
    dg_ref[...] += dg


def _fwd_mlp(h, norm_g, w1, w2, layer, *, tm, riders=(), head=None):
    tokens = h.shape[0]
    nt = tokens // tm
    fs = D_FF // N_CHIPS
    n_in = 4 if head is None else 6

    def body(*refs):
        h_ref, g_ref, w1_hbm, w2_hbm = refs[:4]
        w1_v, w2_v, sem = refs[-3:]
        outs = refs[n_in:-3]
        n_ref, p_ref, q_ref = outs[1:4] if head is None else outs[0:3]
        _load_weights([(w1_hbm, w1_v, False), (w2_hbm, w2_v, False)], sem)

        xv = h_ref[...]
        nf, _ = _rms_fwd(xv, g_ref[...])
        n = nf.astype(BF16)
        n_ref[...] = n
        acc = xv
        for j in range(N_CHIPS):
            p = _dot(n, w1_v[j])
            p_ref[:, j * fs:(j + 1) * fs] = p.astype(BF16)
            r = jnp.maximum(p, 0.0)
            q = (r * r).astype(BF16)
            q_ref[:, j * fs:(j + 1) * fs] = q
            acc = acc + _dot(q, w2_v[j])
        if head is None:
            outs[0][...] = acc
        else:
            _loss_tail(acc, refs[4][...], refs[5][...], *outs[3:7])

    shp = lambda cols, dt: jax.ShapeDtypeStruct((tokens, cols), dt)
    saved_specs = [_row_spec(tm, D_MODEL), _row_spec(tm, D_FF), _row_spec(tm, D_FF)]
    saved_shapes = [shp(D_MODEL, BF16), shp(D_FF, BF16), shp(D_FF, BF16)]
    if head is None:
        operands, in_specs = [h, norm_g, w1, w2], [_row_spec(tm, D_MODEL), _full_spec((1, D_MODEL)), ANY, ANY]
        out_specs, out_shape = [_row_spec(tm, D_MODEL)] + saved_specs, [shp(D_MODEL, F32)] + saved_shapes
    else:
        operands = [h, norm_g, w1, w2, *head]
        in_specs = [_row_spec(tm, D_MODEL), _full_spec((1, D_MODEL)), ANY, ANY, _full_spec((1, D_MODEL)), _row_spec(tm, D_MODEL)]
        out_specs = saved_specs + [_full_spec((1, 1)), _row_spec(tm, D_MODEL), _row_spec(tm, D_MODEL), _full_spec((1, D_MODEL))]
        out_shape = saved_shapes + [jax.ShapeDtypeStruct((1, 1), F32), shp(D_MODEL, F32), shp(D_MODEL, BF16),
                                    jax.ShapeDtypeStruct((1, D_MODEL), F32)]
    return _pallas(
        body, operands, name=f"fwd_mlp{layer}", grid=(nt,), in_specs=in_specs, out_specs=out_specs, out_shape=out_shape,
        scratch_shapes=[pltpu.VMEM((N_CHIPS, D_MODEL, fs), BF16), pltpu.VMEM((N_CHIPS, fs, D_MODEL), BF16),
                        pltpu.SemaphoreType.DMA((N_LOADS,))],
        vmem_mib=56, riders=riders)


def _tril_mask():
    row = lax.broadcasted_iota(jnp.int32, (CHUNK, CHUNK), 0)
    col = lax.broadcasted_iota(jnp.int32, (CHUNK, CHUNK), 1)
    return row >= col


def _triu_mask():
    row = lax.broadcasted_iota(jnp.int32, (CHUNK, CHUNK), 0)
    col = lax.broadcasted_iota(jnp.int32, (CHUNK, CHUNK), 1)
    return row <= col


def _fwd_odd(h, norm_g, w_in, b_in, ln_g, ln_b, w_s, b_s_rows, w_out, *, tm, riders=()):
    tokens = h.shape[0]
    nt = tokens // tm
    cs = 2 * C_DIM // N_CHIPS

    def body(h_ref, g_ref, win_hbm, bin_ref, lng_ref, lnb_ref, ws_ref, bs_ref, wout_hbm,
             ho_ref, n_ref, s_ref, cdf_ref, sv_ref, y_ref, win_v, wout_v, bd, sem):
        _load_weights([(win_hbm, win_v, False), (wout_hbm, wout_v, True)], sem)

        @pl.when(pl.program_id(0) == 0)
        def _():
            mask = _tril_mask()
            bd[...] = jnp.zeros(bd.shape, BF16)
            for g in range(C_GROUPS):
                w = jnp.where(mask, ws_ref[g], 0.0).astype(BF16)
                bd[g, 0:CHUNK, 0:CHUNK] = w
                bd[g, CHUNK:PAIR, CHUNK:PAIR] = w

        xv = h_ref[...]
        nf, _ = _rms_fwd(xv, g_ref[...])
        n = nf.astype(BF16)
        n_ref[...] = n
        s = jnp.concatenate([_dot(n, win_v[j]) for j in range(N_CHIPS)], axis=1) + bin_ref[...]
        s_ref[...] = s.astype(BF16)
        cdf = _gelu_cdf(s)
        cdf_ref[...] = cdf.astype(BF16)
        zz = s * cdf
        u, v = zz[:, 0:C_DIM], zz[:, C_DIM:2 * C_DIM]
        xhat, _ = _ln_stats(v)
        vn = (xhat * lng_ref[...] + lnb_ref[...]).astype(BF16)
        for g in range(C_GROUPS):
            cols = slice(g * CHUNK, (g + 1) * CHUNK)
            bias = jnp.concatenate([bs_ref[g], bs_ref[g]], axis=0)
            for r0 in range(0, tm, PAIR):
                sv = _dot(bd[g], vn[r0:r0 + PAIR, cols]) + bias
                sv_ref[r0:r0 + PAIR, cols] = sv.astype(BF16)
                y_ref[r0:r0 + PAIR, cols] = (u[r0:r0 + PAIR, cols] * sv).astype(BF16)
        ho_ref[...] = xv + _dot(y_ref[...], wout_v[...])

    shp = lambda cols, dt: jax.ShapeDtypeStruct((tokens, cols), dt)
    return _pallas(
        body, [h, norm_g, w_in, b_in, ln_g, ln_b, w_s, b_s_rows, w_out], name="fwd_odd", grid=(nt,),
        in_specs=[_row_spec(tm, D_MODEL), _full_spec((1, D_MODEL)), ANY, _full_spec((1, 2 * C_DIM)),
                  _full_spec((1, C_DIM)), _full_spec((1, C_DIM)), _full_spec((C_GROUPS, CHUNK, CHUNK)),
                  _full_spec((C_GROUPS, CHUNK, CHUNK)), ANY],
        out_specs=[_row_spec(tm, D_MODEL), _row_spec(tm, D_MODEL), _row_spec(tm, 2 * C_DIM), _row_spec(tm, 2 * C_DIM),
                   _row_spec(tm, C_DIM), _row_spec(tm, C_DIM)],
        out_shape=[shp(D_MODEL, F32), shp(D_MODEL, BF16), shp(2 * C_DIM, BF16), shp(2 * C_DIM, BF16), shp(C_DIM, BF16),
                   shp(C_DIM, BF16)],
        scratch_shapes=[pltpu.VMEM((N_CHIPS, D_MODEL, cs), BF16), pltpu.VMEM((C_DIM, D_MODEL), BF16),
                        pltpu.VMEM((C_GROUPS, PAIR, PAIR), BF16), pltpu.SemaphoreType.DMA((N_LOADS,))],
        vmem_mib=56, riders=riders)


def _bwd_mlp(dh, h, norm_g, p, w1, w2, layer, *, tm, riders=()):
    tokens = h.shape[0]
    nt = tokens // tm
    fs = D_FF // N_CHIPS

    def body(dh_ref, h_ref, g_ref, p_ref, w1_hbm, w2_hbm, dx_ref, dxb_ref, dp_ref, dg_ref, w1_v, w2_v, sem):
        @pl.when(pl.program_id(0) == 0)
        def _():
            dg_ref[...] = jnp.zeros((1, D_MODEL), F32)

        _load_weights([(w1_hbm, w1_v, False), (w2_hbm, w2_v, False)], sem)

        dhv = dh_ref[...]
        dhb = dhv.astype(BF16)
        dn = jnp.zeros((tm, D_MODEL), F32)
        for j in range(N_CHIPS):
            dq = _dot_nt(dhb, w2_v[j])
            r = jnp.maximum(p_ref[:, j * fs:(j + 1) * fs].astype(F32), 0.0)
            dp = ((2.0 * r) * dq).astype(BF16)
            dp_ref[:, j * fs:(j + 1) * fs] = dp
            dn = dn + _dot_nt(dp, w1_v[j])
        xv = h_ref[...]
        g = g_ref[...]
        _, rstd = _rms_fwd(xv, g)
        dx, dg = _rms_bwd(dn, xv, rstd, g)
        dx_ref[...] = dhv + dx
        dxb_ref[...] = (dhv + dx).astype(BF16)
        dg_ref[...] += dg

    return _pallas(
        body, [dh, h, norm_g, p, w1, w2], name=f"bwd_mlp{layer}", grid=(nt,),
        in_specs=[_row_spec(tm, D_MODEL), _row_spec(tm, D_MODEL), _full_spec((1, D_MODEL)), _row_spec(tm, D_FF), ANY, ANY],
        out_specs=[_row_spec(tm, D_MODEL), _row_spec(tm, D_MODEL), _row_spec(tm, D_FF), _full_spec((1, D_MODEL))],
        out_shape=[jax.ShapeDtypeStruct((tokens, D_MODEL), F32), jax.ShapeDtypeStruct((tokens, D_MODEL), BF16),
                   jax.ShapeDtypeStruct((tokens, D_FF), BF16), jax.ShapeDtypeStruct((1, D_MODEL), F32)],
        scratch_shapes=[pltpu.VMEM((N_CHIPS, D_MODEL, fs), BF16), pltpu.VMEM((N_CHIPS, fs, D_MODEL), BF16),
                        pltpu.SemaphoreType.DMA((N_LOADS,))],
        vmem_mib=56, riders=riders)


def _bwd_odd(dh, h, norm_g, s, cdf, sv, w_in, ln_g, ln_b, w_s, w_out, *, tm, riders=()):
    tokens = h.shape[0]
    nt = tokens // tm
    cs = 2 * C_DIM // N_CHIPS

    def body(dh_ref, h_ref, g_ref, s_ref, cdf_ref, sv_ref, win_hbm, lng_ref, lnb_ref, ws_ref, wout_hbm,
             dx_ref, dxb_ref, ds_ref, dg_ref, dbin_ref, dlng_ref, dlnb_ref, dws_ref, dbs_ref,
             win_v, wout_v, bdt, dws_acc, dbs_acc, dvn, sem):
        i = pl.program_id(0)

        _load_weights([(win_hbm, win_v, False), (wout_hbm, wout_v, True)], sem)

        @pl.when(i == 0)
        def _():
            mask_t = _triu_mask()
            bdt[...] = jnp.zeros(bdt.shape, BF16)
            for g in range(C_GROUPS):
                wt = jnp.where(mask_t, ws_ref[g].T, 0.0).astype(BF16)
                bdt[g, 0:CHUNK, 0:CHUNK] = wt
                bdt[g, CHUNK:PAIR, CHUNK:PAIR] = wt
            dws_acc[...] = jnp.zeros(dws_acc.shape, F32)
            dbs_acc[...] = jnp.zeros(dbs_acc.shape, F32)
            dg_ref[...] = jnp.zeros(dg_ref.shape, F32)
            dbin_ref[...] = jnp.zeros(dbin_ref.shape, F32)
            dlng_ref[...] = jnp.zeros(dlng_ref.shape, F32)
            dlnb_ref[...] = jnp.zeros(dlnb_ref.shape, F32)

        dhv = dh_ref[...]
        dy = _dot_nt(dhv.astype(BF16), wout_v[...])
        sf = s_ref[...].astype(F32)
        cdf = cdf_ref[...].astype(F32)
        pdf = jnp.exp(-0.5 * sf * sf) * 0.3989422804014327
        zz = sf * cdf
        dgelu = cdf + sf * pdf
        u, v = zz[:, 0:C_DIM], zz[:, C_DIM:2 * C_DIM]
        xhat, rs = _ln_stats(v)
        lng = lng_ref[...]
        vn = (xhat * lng + lnb_ref[...]).astype(BF16)
        du = dy * sv_ref[...].astype(F32)
        dsv = dy * u
        dsvb = dsv.astype(BF16)
        for g in range(C_GROUPS):
            cols = slice(g * CHUNK, (g + 1) * CHUNK)
            for r0 in range(0, tm, PAIR):
                blk = dsvb[r0:r0 + PAIR, cols]
                dvn[r0:r0 + PAIR, cols] = _dot(bdt[g], blk)
                dws_acc[g] += _dot_nt(blk, vn[r0:r0 + PAIR, cols])
                dbs_acc[g] += dsv[r0:r0 + CHUNK, cols] + dsv[r0 + CHUNK:r0 + PAIR, cols]
        dv, dlng, dlnb = _ln_bwd(dvn[...], xhat, rs, lng)
        dlng_ref[...] += dlng
        dlnb_ref[...] += dlnb
        ds = jnp.concatenate([du, dv], axis=1) * dgelu
        dbin_ref[...] += jnp.sum(ds, axis=0, keepdims=True)
        dsb = ds.astype(BF16)
        ds_ref[...] = dsb
        dn = jnp.zeros((tm, D_MODEL), F32)
        for j in range(N_CHIPS):
            dn = dn + _dot_nt(dsb[:, j * cs:(j + 1) * cs], win_v[j])
        xv = h_ref[...]
        g = g_ref[...]
        _, rstd = _rms_fwd(xv, g)
        dx, dg = _rms_bwd(dn, xv, rstd, g)
        dx_ref[...] = dhv + dx
        dxb_ref[...] = (dhv + dx).astype(BF16)
        dg_ref[...] += dg

        @pl.when(i == nt - 1)
        def _():
            mask = _tril_mask()
            for g in range(C_GROUPS):
                full = dws_acc[g]
                dws_ref[g] = jnp.where(mask, full[0:CHUNK, 0:CHUNK] + full[CHUNK:PAIR, CHUNK:PAIR], 0.0)
                dbs_ref[g:g + 1, :] = jnp.sum(dbs_acc[g].T, axis=0, keepdims=True)

    row = lambda cols: jax.ShapeDtypeStruct((1, cols), F32)
    return _pallas(
        body, [dh, h, norm_g, s, cdf, sv, w_in, ln_g, ln_b, w_s, w_out], name="bwd_odd", grid=(nt,),
        in_specs=[_row_spec(tm, D_MODEL), _row_spec(tm, D_MODEL), _full_spec((1, D_MODEL)), _row_spec(tm, 2 * C_DIM),
                  _row_spec(tm, 2 * C_DIM), _row_spec(tm, C_DIM), ANY, _full_spec((1, C_DIM)), _full_spec((1, C_DIM)),
                  _full_spec((C_GROUPS, CHUNK, CHUNK)), ANY],
        out_specs=[_row_spec(tm, D_MODEL), _row_spec(tm, D_MODEL), _row_spec(tm, 2 * C_DIM), _full_spec((1, D_MODEL)),
                   _full_spec((1, 2 * C_DIM)),
                   _full_spec((1, C_DIM)), _full_spec((1, C_DIM)), _full_spec((C_GROUPS, CHUNK, CHUNK)),
                   _full_spec((C_GROUPS, CHUNK))],
        out_shape=[jax.ShapeDtypeStruct((tokens, D_MODEL), F32), jax.ShapeDtypeStruct((tokens, D_MODEL), BF16),
                   jax.ShapeDtypeStruct((tokens, 2 * C_DIM), BF16),
                   row(D_MODEL), row(2 * C_DIM), row(C_DIM), row(C_DIM),
                   jax.ShapeDtypeStruct((C_GROUPS, CHUNK, CHUNK), F32), jax.ShapeDtypeStruct((C_GROUPS, CHUNK), F32)],
        scratch_shapes=[pltpu.VMEM((N_CHIPS, D_MODEL, cs), BF16), pltpu.VMEM((C_DIM, D_MODEL), BF16),
                        pltpu.VMEM((C_GROUPS, PAIR, PAIR), BF16), pltpu.VMEM((C_GROUPS, PAIR, PAIR), F32),
                        pltpu.VMEM((C_GROUPS, CHUNK, CHUNK), F32), pltpu.VMEM((tm, C_DIM), F32),
                        pltpu.SemaphoreType.DMA((N_LOADS,))],
        vmem_mib=56, riders=riders)


def _bwd_even(dh, x, norm_g, z, a2, cv, w_in, conv_a_w, ln_g, ln_b, conv_b_w, w_out, *, tm, seq, riders=()):
    tokens = x.shape[0]
    nt, tps = tokens // tm, seq // tm
    ws = IN_EVEN // N_CHIPS

    def body(dh_ref, x_ref, g_ref, z_ref, a2_ref, cv_ref, win_hbm, caw_ref, lng_ref, lnb_ref, cbw_ref, wout_hbm,
             dx_ref, dz_ref, dg_ref, dcaw_ref, dcab_ref, dlng_ref, dlnb_ref, dcbw_ref,
             win_v, wout_v, ea, eb, a1s, da1s, sigs, wide, dw_acc, sem):
        i = pl.program_id(0)

        _load_weights([(win_hbm, win_v, False), (wout_hbm, wout_v, True)], sem)

        @pl.when(i == 0)
        def _():
            dw_acc[...] = jnp.zeros(dw_acc.shape, F32)
            for ref in (dg_ref, dcab_ref, dlng_ref, dlnb_ref, dcbw_ref):
                ref[...] = jnp.zeros(ref.shape, F32)

        last = ((nt - 1 - i) % tps) == tps - 1

        @pl.when(last)
        def _():
            ea[0, tm:tm + A_HALO, :] = jnp.zeros((A_HALO, A_DIM), F32)
            eb[tm:tm + B_HALO, :] = jnp.zeros((B_HALO, B_DIM), F32)

        @pl.when(jnp.logical_not(last))
        def _():
            ea[0, tm:tm + A_HALO, :] = ea[0, 0:A_HALO, :]
            eb[tm:tm + B_HALO, :] = eb[0:B_HALO, :]

        wide[...] = _dot_nt(dh_ref[...].astype(BF16), wout_v[...])
        lng, lnb = lng_ref[...], lnb_ref[...]
        zero_row = jnp.zeros((1, A_DIM), F32)
        dlng, dlnb, dcab = zero_row, zero_row, zero_row
        for r0 in range(0, tm, ELEM_ROWS):
            rows = slice(r0, r0 + ELEM_ROWS)
            a_val, a_gate = z_ref[rows, 0:A_DIM].astype(F32), z_ref[rows, A_DIM:2 * A_DIM].astype(F32)
            xhat, rs = _ln_stats(a2_ref[rows, :])
            a3 = xhat * lng + lnb
            sg = jax.nn.sigmoid(a3)
            da3 = wide[rows, 0:A_DIM] * (sg * (1.0 + a3 * (1.0 - sg)))
            da2, g_part, b_part = _ln_bwd(da3, xhat, rs, lng)
            dlng, dlnb, dcab = dlng + g_part, dlnb + b_part, dcab + jnp.sum(da2, axis=0, keepdims=True)
            ea[0, rows, :] = da2
            eb[rows, :] = wide[rows, A_DIM:A_DIM + B_DIM] * z_ref[rows, 1024:1536].astype(F32)
            sig = jax.nn.sigmoid(a_gate)
            sigs[rows, :] = sig
            a1s[rows, :] = a_val * sig
        dlng_ref[...] += dlng
        dlnb_ref[...] += dlnb
        dcab_ref[...] += dcab
        _fill_shifted(ea, tm + A_HALO)
        for r0 in range(0, tm, CONV_ROWS):
            acc = jnp.zeros((CONV_ROWS, A_DIM), F32)
            for j in range(A_CONV_WIDTH):
                acc = acc + caw_ref[A_CONV_WIDTH - 1 - j:A_CONV_WIDTH - j, :] * _window(ea, r0 + j, CONV_ROWS)
            da1s[r0:r0 + CONV_ROWS, :] = acc
        for j0 in range(0, A_CONV_WIDTH, DW_TAPS):
            taps = range(j0, min(j0 + DW_TAPS, A_CONV_WIDTH))
            part = [jnp.zeros((CONV_ROWS, A_DIM), F32) for _ in taps]
            for r0 in range(0, tm, CONV_ROWS):
                a1c = a1s[r0:r0 + CONV_ROWS, :]
                for u, j in enumerate(taps):
                    part[u] = part[u] + _window(ea, r0 + j, CONV_ROWS) * a1c
            for u, j in enumerate(taps):
                dw_acc[A_CONV_WIDTH - 1 - j] += part[u]
        dcbw = [jnp.zeros((1, B_DIM), F32) for _ in range(B_CONV_WIDTH)]
        for r0 in range(0, tm, ELEM_ROWS):
            rows = slice(r0, r0 + ELEM_ROWS)
            da1, sig = da1s[rows, :], sigs[rows, :]
            dz_ref[rows, 0:A_DIM] = (da1 * sig).astype(BF16)
            dz_ref[rows, A_DIM:2 * A_DIM] = (da1 * z_ref[rows, 0:A_DIM].astype(F32) * (sig * (1.0 - sig))).astype(BF16)
            c_gate, b_val = z_ref[rows, 1536:2048].astype(F32), z_ref[rows, 2048:2560].astype(F32)
            dz_ref[rows, 1024:1536] = (wide[rows, A_DIM:A_DIM + B_DIM] * cv_ref[rows, :].astype(F32)).astype(BF16)
            cb = c_gate * b_val
            dcb = jnp.zeros((ELEM_ROWS, B_DIM), F32)
            for j in range(B_CONV_WIDTH):
                k = B_CONV_WIDTH - 1 - j
                sl = eb[r0 + j:r0 + j + ELEM_ROWS, :]
                dcb = dcb + cbw_ref[k:k + 1, :] * sl
                dcbw[k] = dcbw[k] + jnp.sum(sl * cb, axis=0, keepdims=True)
            dz_ref[rows, 1536:2048] = (dcb * b_val).astype(BF16)
            dz_ref[rows, 2048:2560] = (dcb * c_gate).astype(BF16)
        for k in range(B_CONV_WIDTH):
            dcbw_ref[k:k + 1, :] += dcbw[k]
        dn = jnp.zeros((tm, D_MODEL), F32)
        for j in range(N_CHIPS):
            dn = dn + _dot_nt(dz_ref[:, j * ws:(j + 1) * ws], win_v[j])
        wide[...] = dn
        g = g_ref[...]
        dg = jnp.zeros((1, D_MODEL), F32)
        for r0 in range(0, tm, ELEM_ROWS):
            rows = slice(r0, r0 + ELEM_ROWS)
            xv = x_ref[rows, :]
            _, rstd = _rms_fwd(xv, g)
            dx, dg_part = _rms_bwd(wide[rows, :], xv, rstd, g)
            dx_ref[rows, :] = dh_ref[rows, :] + dx
            dg = dg + dg_part
        dg_ref[...] += dg

        @pl.when(i == nt - 1)
        def _():
            for k in range(A_CONV_WIDTH):
                dcaw_ref[k:k + 1, :] = jnp.sum(dw_acc[k], axis=0, keepdims=True)

    row = lambda cols: jax.ShapeDtypeStruct((1, cols), F32)
    rs_ = functools.partial(_row_spec, rev_nt=nt)
    return _pallas(
        body, [dh, x, norm_g, z, a2, cv, w_in, conv_a_w, ln_g, ln_b, conv_b_w, w_out], name="bwd_even", grid=(nt,),
        in_specs=[rs_(tm, D_MODEL), rs_(tm, D_MODEL), _full_spec((1, D_MODEL)), rs_(tm, IN_EVEN), rs_(tm, A_DIM),
                  rs_(tm, B_DIM), ANY, _full_spec((A_CONV_WIDTH, A_DIM)), _full_spec((1, A_DIM)), _full_spec((1, A_DIM)),
                  _full_spec((B_CONV_WIDTH, B_DIM)), ANY],
        out_specs=[rs_(tm, D_MODEL), rs_(tm, IN_EVEN), _full_spec((1, D_MODEL)), _full_spec((A_CONV_WIDTH, A_DIM)),
                   _full_spec((1, A_DIM)), _full_spec((1, A_DIM)), _full_spec((1, A_DIM)), _full_spec((B_CONV_WIDTH, B_DIM))],
        out_shape=[jax.ShapeDtypeStruct((tokens, D_MODEL), F32), jax.ShapeDtypeStruct((tokens, IN_EVEN), BF16),
                   row(D_MODEL), jax.ShapeDtypeStruct((A_CONV_WIDTH, A_DIM), F32), row(A_DIM), row(A_DIM), row(A_DIM),
                   jax.ShapeDtypeStruct((B_CONV_WIDTH, B_DIM), F32)],
        scratch_shapes=[pltpu.VMEM((N_CHIPS, D_MODEL, ws), BF16), pltpu.VMEM((D_MODEL, D_MODEL), BF16),
                        pltpu.VMEM((SUBLANES, tm + A_HALO, A_DIM), F32), pltpu.VMEM((tm + B_HALO, B_DIM), F32),
                        pltpu.VMEM((tm, A_DIM), F32), pltpu.VMEM((tm, A_DIM), F32), pltpu.VMEM((tm, A_DIM), F32),
                        pltpu.VMEM((tm, D_MODEL), F32),
                        pltpu.VMEM((A_CONV_WIDTH, CONV_ROWS, A_DIM), F32), pltpu.SemaphoreType.DMA((N_LOADS,))],
        vmem_mib=56, riders=riders)


def _wgrad(a, b, name, *, col_shards, riders=()):
    tokens, m = a.shape
    n = b.shape[1]
    kc = 512
    if col_shards:
        bm, bn = m // 2, n // N_CHIPS
        grid = (2, N_CHIPS)
        out_spec = pl.BlockSpec((None, None, bm, bn), lambda i, j: (j, i, 0, 0))
    elif m // 8 >= MXU_ROWS:
        bm, bn = m // 8, n
        grid = (8, 1)
        out_spec = pl.BlockSpec((None, None, bm, bn), lambda i, j: (i // 2, i % 2, 0, 0))
    else:
        bm, bn = m // N_CHIPS, n
        grid = (N_CHIPS, 1)
        out_spec = pl.BlockSpec((None, 2, bm // 2, bn), lambda i, j: (i, 0, 0, 0))

    def body(a_ref, b_ref, o_ref):
        acc = jnp.zeros((bm, bn), F32)
        for k0 in range(0, tokens, kc):
            acc = acc + _dot_tn(a_ref[k0:k0 + kc, :].astype(BF16), b_ref[k0:k0 + kc, :].astype(BF16))
        if len(o_ref.shape) == 3:
            o_ref[0] = acc[0:bm // 2]
            o_ref[1] = acc[bm // 2:bm]
        else:
            o_ref[...] = acc

    out_rows = m // 2 if col_shards else m // 8
    outs, routs = _pallas(
        body, [a, b], name=name, grid=grid,
        in_specs=[pl.BlockSpec((tokens, bm), lambda i, j: (0, i)), pl.BlockSpec((tokens, bn), lambda i, j: (0, j))],
        out_specs=[out_spec], out_shape=[jax.ShapeDtypeStruct((N_CHIPS, 2, out_rows, bn), F32)],
        vmem_mib=56, riders=riders)
    return outs[0], routs


def _wgrad_pair(a, b, name, *, col_shards, riders=(), to_chips=False):
    tokens, m = a.shape
    n = b.shape[1]
    kc = 512
    x0, y0, c0 = _mesh_pos()
    rot = 1 if to_chips else 0
    phases = [0, 0, 1, 0, 1, 0, 1, 1] if to_chips else [0, 0, 0, 0, 1, 1, 1, 1]
    tiles = [0, 1, 0, 2, 1, 3, 2, 3] if to_chips else [0, 1, 2, 3, 0, 1, 2, 3]
    out_tiles = [0, 0, 0, 0, 1, 1, 2, 3] if to_chips else [0, 0, 0, 0, 0, 1, 2, 3]
    steps = len(phases)
    P0, T0, O0 = 2, 2 + steps, 2 + 2 * steps

    def slab(t, pre):
        return (t + rot * (1 + pre[1])) % N_CHIPS

    def half(s, pre):
        return (pre[P0 + s] + 1 + pre[0]) % 2

    if col_shards:
        bm, bn = m // 2, n // N_CHIPS
        a_spec = pl.BlockSpec((tokens, bm), lambda s, pre: (0, half(s, pre)))
        b_spec = pl.BlockSpec((tokens, bn), lambda s, pre: (0, slab(pre[T0 + s], pre)))
    else:
        bm, bn = m // 8, n
        a_spec = pl.BlockSpec((tokens, bm), lambda s, pre: (0, 2 * slab(pre[T0 + s], pre) + half(s, pre)))
        b_spec = pl.BlockSpec((tokens, bn), lambda s, pre: (0, 0))

    def body(pre_ref, a_ref, b_ref, o_ref, *rest):
        if to_chips:
            land, give, got, mine, send_sems, recv_sems, chip_send, chip_recv = rest
        else:
            give, got, send_sems, recv_sems = rest
        step = pl.program_id(0)
        ph, q = pre_ref[P0 + step], pre_ref[T0 + step]
        acc = jnp.zeros((bm, bn), F32)
        for k0 in range(0, tokens, kc):
            acc = acc + _dot_tn(a_ref[k0:k0 + kc, :].astype(BF16), b_ref[k0:k0 + kc, :].astype(BF16))
        x, y, cc = _mesh_pos()

        def tile(t):
            return _remote(give.at[t], got.at[t], send_sems.at[t], recv_sems.at[t], (x, y, 1 - cc))

        def to_chip(s):
            t = (s + 1 + 2 * x + y) % N_CHIPS
            tx, ty = t // 2, t % 2
            k = 2 * (ty ^ y) + (tx ^ x) - 1
            return _remote(mine.at[s], land.at[k], chip_send.at[k], chip_recv.at[k], (tx, ty, cc))

        @pl.when(ph == 0)
        def _():
            give[q] = acc
            tile(q).start()

        @pl.when(ph == 1)
        def _():
            tile(q).wait_recv()
            total = (acc + got[q]).astype(BF16)
            o_ref[...] = total
            if to_chips:
                for s in range(N_CHIPS - 1):
                    @pl.when(q == s)
                    def _(s=s):
                        mine[s] = total
                        to_chip(s).start()

        @pl.when(step == steps - 1)
        def _():
            for t in range(N_CHIPS):
                tile(t).wait_send()
            if to_chips:
                for s in range(N_CHIPS - 1):
                    to_chip(s).wait()

    prefetch = jnp.concatenate([jnp.stack([c0, 2 * x0 + y0]).astype(jnp.int32),
                                jnp.asarray(phases + tiles + out_tiles, jnp.int32)])
    out_specs = [pl.BlockSpec((None, bm, bn), lambda s, pre: (slab(pre[O0 + s], pre), 0, 0))]
    out_shape = [jax.ShapeDtypeStruct((N_CHIPS, bm, bn), BF16)]
    scratch = [pltpu.VMEM((N_CHIPS, bm, bn), F32), pltpu.VMEM((N_CHIPS, bm, bn), F32)]
    sems = [pltpu.SemaphoreType.DMA((N_CHIPS,)), pltpu.SemaphoreType.DMA((N_CHIPS,))]
    if to_chips:
        out_specs.append(ANY)
        out_shape.append(jax.ShapeDtypeStruct((N_CHIPS - 1, bm, bn), BF16))
        scratch.append(pltpu.VMEM((N_CHIPS - 1, bm, bn), BF16))
        sems += [pltpu.SemaphoreType.DMA((N_CHIPS - 1,)), pltpu.SemaphoreType.DMA((N_CHIPS - 1,))]
    outs, routs = _pallas(
        body, [a, b], name=name, grid=(steps,), in_specs=[a_spec, b_spec], out_specs=out_specs, out_shape=out_shape,
        scratch_shapes=scratch + sems, vmem_mib=56, riders=riders, prefetch=prefetch)
    return (outs if to_chips else outs[0]), routs


class _GradReduce:
    def __init__(self, name, grad=None, chip_sum=None):
        self.name, self.grad, self.chip_sum = name, grad, chip_sum
        self.full = None

    def pair_swap(self):
        return _PairSwap([self.grad])

    def took_pair(self, outs):
        self.chip_sum = _in_hbm(_add_pair(self.grad, outs[0], f"pair_sum_{self.name}"))

    def chip_swap(self):
        return _ChipSwap([self.chip_sum])

    def took_chips(self, outs):
        self.full = _in_hbm(_add_chips(self.chip_sum, outs[0], f"chip_sum_{self.name}"))

    def pair_share(self):
        return _PairShare([self.full])

    def took_share(self, outs):
        self.full = outs[0]

    def reduced(self):
        return jnp.reshape(self.full, (2 * self.full.shape[1], self.full.shape[2]))


def _forward_backward(x2, tgt2, gathered, staged, conv_a_w, conv_b_w, od_norm, od_bias, od_lng, od_lnb,
                      ev_norm_g, ev_conv_a_b, ev_ln_a_g, ev_ln_a_b, od_w_s, od_b_s, mlp_norm_g, final_norm_g,
                      *, tm, seq, distributed=True):
    d = x2.shape[1]
    w = dict(gathered)
    b_s_rows = jnp.broadcast_to(od_b_s[0][:, :, None], (C_GROUPS, CHUNK, CHUNK))

    def ride(*names):
        return [_Gather([staged[nm] for nm in names])] if distributed and staged else []

    def land(routs, *names):
        if distributed and staged:
            for nm, buf in zip(names, routs[0]):
                w[nm] = buf

    (h1, n0, z, a2, cv, mix), routs = _fwd_even(
        x2, ev_norm_g, w["ev_in"], conv_a_w, ev_conv_a_b, ev_ln_a_g, ev_ln_a_b, conv_b_w, w["ev_out"],
        tm=tm, seq=seq, riders=ride("w1_0", "w2_0"))
    land(routs, "w1_0", "w2_0")
    (h2, n1, p0, q0), routs = _fwd_mlp(h1, mlp_norm_g[0:1], w["w1_0"], w["w2_0"], 0, tm=tm,
                                       riders=ride("od_in", "od_out", "w1_1"))
    land(routs, "od_in", "od_out", "w1_1")
    (h3, n2, s, cdf, sv, y), routs = _fwd_odd(h2, od_norm, w["od_in"], od_bias, od_lng, od_lnb, od_w_s[0], b_s_rows,
                                         w["od_out"], tm=tm, riders=ride("w2_1"))
    land(routs, "w2_1")
    (n3, p1, q1, loss_part, dh4, dh4b, d_final_g), _ = _fwd_mlp(
        h3, mlp_norm_g[1:2], w["w1_1"], w["w2_1"], 1, tm=tm,
        head=(jnp.reshape(final_norm_g, (1, d)), tgt2))

    red = {}

    def swap(*names):
        return [red[nm].pair_swap() for nm in names] if distributed else []

    def chips(*names):
        return [red[nm].chip_swap() for nm in names] if distributed else []

    def share(*names):
        return [red[nm].pair_share() for nm in names] if distributed else []

    def took(routs, *steps):
        if distributed:
            for (nm, what), outs in zip(steps, routs):
                getattr(red[nm], what)(outs)

    def big(lhs, rhs, name, col_shards, riders=(), to_chips=False):
        if distributed and to_chips:
            (chip_sum, from_chips), routs = _wgrad_pair(lhs, rhs, f"wgrad_{name}", col_shards=col_shards, riders=riders,
                                                        to_chips=True)
            red[name] = _GradReduce(name, chip_sum=_in_hbm(chip_sum))
            red[name].took_chips([_in_hbm(from_chips)])
        elif distributed:
            chip_sum, routs = _wgrad_pair(lhs, rhs, f"wgrad_{name}", col_shards=col_shards, riders=riders)
            red[name] = _GradReduce(name, chip_sum=_in_hbm(chip_sum))
        else:
            g, routs = _wgrad(lhs, rhs, f"wgrad_{name}", col_shards=col_shards)
            red[name] = _GradReduce(name, grad=g)
        return routs

    big(q1, dh4b, "w2_1", False)
    (dh3, dh3b, dp1, d_mlp_g1), routs = _bwd_mlp(dh4, h3, mlp_norm_g[1:2], p1, w["w1_1"], w["w2_1"], 1, tm=tm,
                                           riders=chips("w2_1"))
    took(routs, ("w2_1", "took_chips"))
    big(n3, dp1, "w1_1", True)
    g, routs = _wgrad(y, dh3b, "wgrad_od_out", col_shards=False, riders=share("w2_1"))
    red["od_out"] = _GradReduce("od_out", grad=g)
    took(routs, ("w2_1", "took_share"))
    (dh2, dh2b, ds, d_od_norm, d_od_bin, d_od_lng, d_od_lnb, d_ws, d_bs), routs = _bwd_odd(
        dh3, h2, od_norm, s, cdf, sv, w["od_in"], od_lng, od_lnb, od_w_s[0], w["od_out"], tm=tm,
        riders=chips("w1_1") + swap("od_out"))
    took(routs, ("w1_1", "took_chips"), ("od_out", "took_pair"))
    routs = big(n2, ds, "od_in", True, riders=share("w1_1"))
    took(routs, ("w1_1", "took_share"))
    half_groups = C_GROUPS // 2
    early = {"loss": loss_part, "od_w_s_lo": d_ws[:half_groups], "od_b_s": d_bs, "mlp_norm_g1": d_mlp_g1, "final_norm_g": d_final_g,
             "od_norm_g": d_od_norm, "od_b_in": d_od_bin, "od_ln_v_g": d_od_lng, "od_ln_v_b": d_od_lnb}
    share_early = [_ShareAll(list(early.values()))] if distributed else []
    routs = big(q0, dh2b, "w2_0", False, riders=share_early)
    landed_early = routs[0] if distributed else []
    (dh1, dh1b, dp0, d_mlp_g0), routs = _bwd_mlp(dh2, h1, mlp_norm_g[0:1], p0, w["w1_0"], w["w2_0"], 0, tm=tm,
                                           riders=chips("od_out") + chips("od_in") + chips("w2_0"))
    took(routs, ("od_out", "took_chips"), ("od_in", "took_chips"), ("w2_0", "took_chips"))
    middle = {"od_w_s_hi": d_ws[half_groups:]}
    share_middle = [_ShareAll(list(middle.values()))] if distributed else []
    g, _ = _wgrad(mix, dh1b, "wgrad_ev_out", col_shards=False)
    red["ev_out"] = _GradReduce("ev_out", grad=g)
    routs = big(n1, dp0, "w1_0", True,
                riders=share("od_out") + share("od_in") + share("w2_0") + share_middle + swap("ev_out"))
    took(routs, ("od_out", "took_share"), ("od_in", "took_share"), ("w2_0", "took_share"))
    landed_middle = routs[3] if distributed else []
    if distributed:
        red["ev_out"].took_pair(routs[4])

    (dx, dz, d_ev_norm, d_caw, d_cab, d_ev_lng, d_ev_lnb, d_cbw), routs = _bwd_even(
        dh1, x2, ev_norm_g, z, a2, cv, w["ev_in"], conv_a_w, ev_ln_a_g, ev_ln_a_b, conv_b_w, w["ev_out"],
        tm=tm, seq=seq, riders=chips("w1_0") + chips("ev_out"))
    took(routs, ("w1_0", "took_chips"), ("ev_out", "took_chips"))
    late = {"mlp_norm_g0": d_mlp_g0, "ev_norm_g": d_ev_norm, "ev_conv_a_b": d_cab, "ev_ln_a_g": d_ev_lng,
            "ev_ln_a_b": d_ev_lnb, "ev_conv_a_w": d_caw, "ev_conv_b_w": d_cbw}
    share_late = [_ShareAll(list(late.values()))] if distributed else []
    routs2 = big(n0, dz, "ev_in", True, riders=share("ev_out") + share("w1_0") + share_late, to_chips=True)
    took(routs2, ("ev_out", "took_share"), ("w1_0", "took_share"))
    own = {**early, **middle, **late}
    landed = dict(zip(own.keys(), landed_early + landed_middle + routs2[2])) if distributed else None
    return dx, red, own, landed


def _rows128(a):
    rows = jnp.reshape(a, (-1, LANES))
    pad = (-rows.shape[0]) % SUBLANES
    return jnp.pad(rows, ((0, pad), (0, 0))) if pad else rows


def _pack(arrays):
    return jnp.concatenate([_rows128(a) for a in arrays], axis=0)


def _unpack(buf, shapes):
    out, r0 = [], 0
    for shp in shapes:
        size = 1
        for dim in shp:
            size *= dim
        nr = size // LANES
        out.append(jnp.reshape(buf[r0:r0 + nr], shp))
        r0 += nr + (-nr) % SUBLANES
    return out


def kernel(x, ev_norm_g, ev_w_in, ev_conv_a_w, ev_conv_a_b, ev_ln_a_g, ev_ln_a_b, ev_conv_b_w, ev_w_out, od_norm_g, od_w_in, od_b_in, od_ln_v_g, od_ln_v_b, od_w_s, od_b_s, od_w_out, mlp_norm_g, mlp_w1, mlp_w2, final_norm_g, loss_target, m_ev_norm_g, m_ev_w_in, m_ev_conv_a_w, m_ev_conv_a_b, m_ev_ln_a_g, m_ev_ln_a_b, m_ev_conv_b_w, m_ev_w_out, m_od_norm_g, m_od_w_in, m_od_b_in, m_od_ln_v_g, m_od_ln_v_b, m_od_w_s, m_od_b_s, m_od_w_out, m_mlp_norm_g, m_mlp_w1, m_mlp_w2, m_final_norm_g, v_ev_norm_g, v_ev_w_in, v_ev_conv_a_w, v_ev_conv_a_b, v_ev_ln_a_g, v_ev_ln_a_b, v_ev_conv_b_w, v_ev_w_out, v_od_norm_g, v_od_w_in, v_od_b_in, v_od_ln_v_g, v_od_ln_v_b, v_od_w_s, v_od_b_s, v_od_w_out, v_mlp_norm_g, v_mlp_w1, v_mlp_w2, v_final_norm_g):
    tm = TOKEN_TILE
    batch, seq, d = x.shape
    tokens = batch * seq
    x2 = jnp.reshape(x, (tokens, d))
    tgt2 = jnp.reshape(loss_target, (tokens, d))
    chip = 2 * lax.axis_index("x") + lax.axis_index("y")

    small_shapes = [(A_CONV_WIDTH, LANES), (B_CONV_WIDTH, LANES), (256,), (512,), (256,), (256,)]
    small_shard = _pack([ev_conv_a_w[0], ev_conv_b_w[0], od_norm_g[0], od_b_in[0], od_ln_v_g[0], od_ln_v_b[0]])
    small_shard = jnp.pad(small_shard, ((0, (-small_shard.shape[0]) % (2 * SUBLANES)), (0, 0)))
    first = [_place_shard(ev_w_in, 0, BF16, "place_ev_w_in"), _place_shard(ev_w_out, 0, BF16, "place_ev_w_out"),
             _place_shard(small_shard[None], 0, F32, "place_small")]
    staged = {
        "w1_0": _place_shard(mlp_w1, 0, BF16, "place_w1_0"), "w2_0": _place_shard(mlp_w2, 0, BF16, "place_w2_0"),
        "od_in": _place_shard(od_w_in, 0, BF16, "place_od_w_in"), "od_out": _place_shard(od_w_out, 0, BF16, "place_od_w_out"),
        "w1_1": _place_shard(mlp_w1, 1, BF16, "place_w1_1"), "w2_1": _place_shard(mlp_w2, 1, BF16, "place_w2_1"),
    }
    first = [_in_hbm(a) for a in first]
    staged = {nm: _in_hbm(a) for nm, a in staged.items()}
    (g_ev_in, g_ev_out, g_small), = _exchange([_Gather(first)], "gather_first")
    gathered = {"ev_in": g_ev_in, "ev_out": g_ev_out}
    for stage, names in enumerate((("w1_0", "w2_0"), ("od_in", "od_out", "w1_1"), ("w2_1",))):
        done = _gather_beside([staged[nm] for nm in names], f"gather_stage{stage + 1}", collective_id=stage + 1)
        gathered.update(zip(names, done))
    small_all = jnp.reshape(g_small, (N_CHIPS, -1, LANES))
    per_chip = [_unpack(small_all[q], small_shapes) for q in range(N_CHIPS)]
    conv_a_w = jnp.concatenate([pc[0] for pc in per_chip], axis=1)
    conv_b_w = jnp.concatenate([pc[1] for pc in per_chip], axis=1)
    od_norm = jnp.concatenate([pc[2] for pc in per_chip])[None, :]
    od_bias = jnp.concatenate([pc[3] for pc in per_chip])[None, :]
    od_lng = jnp.concatenate([pc[4] for pc in per_chip])[None, :]
    od_lnb = jnp.concatenate([pc[5] for pc in per_chip])[None, :]

    dx, red, own, landed = _forward_backward(
        x2, tgt2, gathered, {}, conv_a_w, conv_b_w, od_norm, od_bias, od_lng, od_lnb,
        ev_norm_g, ev_conv_a_b, ev_ln_a_g, ev_ln_a_b, od_w_s, od_b_s, mlp_norm_g, final_norm_g, tm=tm, seq=seq)

    routs = _exchange([red["ev_in"].pair_share()], "reduce_tail")
    red["ev_in"].took_share(routs[0])

    given = {"ev_norm_g": (ev_norm_g, m_ev_norm_g, v_ev_norm_g), "ev_conv_a_b": (ev_conv_a_b, m_ev_conv_a_b, v_ev_conv_a_b),
             "ev_ln_a_g": (ev_ln_a_g, m_ev_ln_a_g, v_ev_ln_a_g), "ev_ln_a_b": (ev_ln_a_b, m_ev_ln_a_b, v_ev_ln_a_b),
             "od_w_s": (od_w_s, m_od_w_s, v_od_w_s), "od_b_s": (od_b_s, m_od_b_s, v_od_b_s),
             "mlp_norm_g": (mlp_norm_g, m_mlp_norm_g, v_mlp_norm_g), "final_norm_g": (final_norm_g, m_final_norm_g, v_final_norm_g),
             "ev_conv_a_w": (ev_conv_a_w, m_ev_conv_a_w, v_ev_conv_a_w), "ev_conv_b_w": (ev_conv_b_w, m_ev_conv_b_w, v_ev_conv_b_w),
             "od_norm_g": (od_norm_g, m_od_norm_g, v_od_norm_g), "od_b_in": (od_b_in, m_od_b_in, v_od_b_in),
             "od_ln_v_g": (od_ln_v_g, m_od_ln_v_g, v_od_ln_v_g), "od_ln_v_b": (od_ln_v_b, m_od_ln_v_b, v_od_ln_v_b)}
    shaped = {nm: tuple(jnp.reshape(a, shape) for a in given[nm]) for nm, shape, _, _ in SMALL_WEIGHTS}
    loss11, small_upd = _small_update(own, landed, shaped)
    loss = loss11[0, 0]
    upd = {nm: [jnp.reshape(o, given[nm][0].shape) for o in outs] for nm, outs in small_upd.items()}

    def big_update(wt, m, v, names, call):
        grads = [red[nm].reduced() for nm in names]
        shp3 = (len(grads),) + grads[0].shape
        outs, _ = _adamw(jnp.reshape(wt, shp3), jnp.reshape(m, shp3), jnp.reshape(v, shp3), grads, call)
        return [jnp.reshape(o, wt.shape) for o in outs], None

    upd["mlp_w2"], _ = big_update(mlp_w2, m_mlp_w2, v_mlp_w2, ["w2_0", "w2_1"], "adamw_mlp_w2")
    upd["mlp_w1"], _ = big_update(mlp_w1, m_mlp_w1, v_mlp_w1, ["w1_0", "w1_1"], "adamw_mlp_w1")
    upd["ev_w_in"], _ = big_update(ev_w_in, m_ev_w_in, v_ev_w_in, ["ev_in"], "adamw_ev_w_in")
    upd["ev_w_out"], _ = big_update(ev_w_out, m_ev_w_out, v_ev_w_out, ["ev_out"], "adamw_ev_w_out")
    upd["od_w_in"], _ = big_update(od_w_in, m_od_w_in, v_od_w_in, ["od_in"], "adamw_od_w_in")
    upd["od_w_out"], _ = big_update(od_w_out, m_od_w_out, v_od_w_out, ["od_out"], "adamw_od_w_out")

    order = ["ev_norm_g", "ev_w_in", "ev_conv_a_w", "ev_conv_a_b", "ev_ln_a_g", "ev_ln_a_b", "ev_conv_b_w", "ev_w_out",
             "od_norm_g", "od_w_in", "od_b_in", "od_ln_v_g", "od_ln_v_b", "od_w_s", "od_b_s", "od_w_out", "mlp_norm_g",
             "mlp_w1", "mlp_w2", "final_norm_g"]
    grad_x = jnp.reshape(dx, x.shape)
    return (loss, grad_x, *[upd[nm][0] for nm in order], *[upd[nm][1] for nm in order],
            *[upd[nm][2] for nm in order], *[upd[nm][3] for nm in order])
```

```python
import functools

import jax
import jax.numpy as jnp
from jax import lax
from jax.experimental import pallas as pl
from jax.experimental.pallas import tpu as pltpu
from jax.experimental.pallas import tpu_sc as plsc

F32 = jnp.float32
BF16 = jnp.bfloat16

D_MODEL = 1024
A_DIM = 512
B_DIM = 512
IN_EVEN = 2 * A_DIM + 3 * B_DIM
A_CONV_WIDTH = 31
B_CONV_WIDTH = 3
CHUNK = 128
C_GROUPS = 8
C_DIM = 1024
D_FF = 4096
RMS_EPS = 1e-6
LN_EPS = 1e-5
ADAM_LR = 0.001
ADAM_B1 = 0.9
ADAM_B2 = 0.999
ADAM_EPS = 1e-08
ADAM_WD = 0.01
ADAM_STEP = 10

N_CHIPS = 4
N_DEV = 8
TOKEN_TILE = 512
A_HALO = 32
B_HALO = 8
CONV_ROWS = 16
DW_TAPS = 4
ELEM_ROWS = 16
PAIR = 2 * CHUNK
LANES = 128
SUBLANES = 8
MXU_ROWS = 256
MIB = 1024 * 1024
MESH = pl.DeviceIdType.MESH
ANY = pl.BlockSpec(memory_space=pl.ANY)


def _dot(a, b):
    return lax.dot_general(a, b, (((1,), (0,)), ((), ())), preferred_element_type=F32)


def _dot_nt(a, b):
    return lax.dot_general(a, b, (((1,), (1,)), ((), ())), preferred_element_type=F32)


def _dot_tn(a, b):
    return lax.dot_general(a, b, (((0,), (0,)), ((), ())), preferred_element_type=F32)


def _params(vmem_mib, n_axes=1):
    return pltpu.CompilerParams(dimension_semantics=("arbitrary",) * n_axes, vmem_limit_bytes=vmem_mib * MIB)


def _row_spec(tm, cols, rev_nt=None):
    if rev_nt is None:
        return pl.BlockSpec((tm, cols), lambda i: (i, 0))
    return pl.BlockSpec((tm, cols), lambda i: (rev_nt - 1 - i, 0))


def _full_spec(shape):
    nd = len(shape)
    return pl.BlockSpec(shape, lambda i: (0,) * nd)


def _block_rows(rows, cap=512):
    best = SUBLANES
    for br in range(SUBLANES, min(rows, cap) + 1, SUBLANES):
        if rows % br == 0:
            best = br
    return best


N_LOADS = 2 * 2 * N_CHIPS


def _load_weights(loads, sems):
    @pl.when(pl.program_id(0) == 0)
    def _():
        copies = []
        for src, dst, rows_of_one in loads:
            r = src.shape[2]
            for q in range(N_CHIPS):
                for h in range(2):
                    part = dst.at[pl.ds((2 * q + h) * r, r)] if rows_of_one else dst.at[q, pl.ds(h * r, r)]
                    copies.append(pltpu.make_async_copy(src.at[q, h], part, sems.at[len(copies)]))
        for cp in copies:
            cp.start()
        for cp in copies:
            cp.wait()


def _rms_fwd(x, g):
    rstd = lax.rsqrt(jnp.mean(x * x, axis=-1, keepdims=True) + RMS_EPS)
    return x * rstd * g, rstd


def _rms_bwd(dn, x, rstd, g):
    a = dn * g
    xh = x * rstd
    dx = rstd * (a - xh * jnp.mean(a * xh, axis=-1, keepdims=True))
    dg = jnp.sum(dn * xh, axis=0, keepdims=True)
    return dx, dg


def _ln_stats(v):
    mu = jnp.mean(v, axis=-1, keepdims=True)
    xc = v - mu
    rs = lax.rsqrt(jnp.mean(xc * xc, axis=-1, keepdims=True) + LN_EPS)
    return xc * rs, rs


def _ln_bwd(dy, xhat, rs, g):
    dxh = dy * g
    dv = rs * (dxh - jnp.mean(dxh, axis=-1, keepdims=True) - xhat * jnp.mean(dxh * xhat, axis=-1, keepdims=True))
    return dv, jnp.sum(dy * xhat, axis=0, keepdims=True), jnp.sum(dy, axis=0, keepdims=True)


def _gelu_cdf(s):
    return 0.5 * (1.0 + lax.erf(s * 0.7071067811865476))


def _mesh_pos():
    return lax.axis_index("x"), lax.axis_index("y"), lax.axis_index("c")


def _other_chips(x, y):
    return [(1 - x, y), (x, 1 - y), (1 - x, 1 - y)]


def _remote(src, dst, send_sem, recv_sem, to):
    return pltpu.make_async_remote_copy(src_ref=src, dst_ref=dst, send_sem=send_sem, recv_sem=recv_sem,
                                        device_id=to, device_id_type=MESH)


def _like(arrays):
    return [jax.ShapeDtypeStruct(a.shape, a.dtype) for a in arrays]


class _Gather:
    def __init__(self, bufs):
        self.ins = list(bufs)
        self.out_shapes = _like(bufs)
        self.aliases = {t: t for t in range(len(bufs))}
        self.n_sems = 6 * len(bufs)

    def _ici(self, ins, outs, send, recv, t, k, chip, mine, c):
        return _remote(ins[t].at[mine, c], outs[t].at[mine, c], send.at[6 * t + k], recv.at[6 * t + k], (*chip, c))

    def start(self, ins, outs, send, recv):
        x, y, c = _mesh_pos()
        for t in range(len(ins)):
            for k, chip in enumerate(_other_chips(x, y)):
                self._ici(ins, outs, send, recv, t, k, chip, 2 * x + y, c).start()

    def _pass_on(self, outs, send, recv, t, k, chip, c, to):
        blk = outs[t].at[2 * chip[0] + chip[1], c]
        return _remote(blk, blk, send.at[6 * t + 3 + k], recv.at[6 * t + 3 + k], to)

    def near_end(self, ins, outs, send, recv):
        x, y, c = _mesh_pos()
        for t in range(len(ins)):
            for k, chip in enumerate(_other_chips(x, y)):
                blk = outs[t].at[2 * chip[0] + chip[1], c]
                _remote(blk, blk, send.at[6 * t + k], recv.at[6 * t + k], (x, y, c)).wait_recv()
                self._pass_on(outs, send, recv, t, k, chip, c, (x, y, 1 - c)).start()

    def finish(self, ins, outs, send, recv):
        x, y, c = _mesh_pos()
        chips = _other_chips(x, y)
        for t in range(len(ins)):
            for k, chip in enumerate(chips):
                self._pass_on(outs, send, recv, t, k, chip, 1 - c, (x, y, c)).wait_recv()
        for t in range(len(ins)):
            for k, chip in enumerate(chips):
                self._ici(ins, outs, send, recv, t, k, chip, 2 * x + y, c).wait_send()
                self._pass_on(outs, send, recv, t, k, chip, c, (x, y, 1 - c)).wait_send()


class _PairSwap:
    def __init__(self, grads):
        self.ins = list(grads)
        self.out_shapes = [jax.ShapeDtypeStruct((g.shape[0],) + g.shape[2:], g.dtype) for g in grads]
        self.aliases = {}
        self.n_sems = len(grads)

    def _copies(self, ins, outs, send, recv):
        x, y, c = _mesh_pos()
        return [_remote(ins[t].at[:, 1 - c], outs[t], send.at[t], recv.at[t], (x, y, 1 - c)) for t in range(len(ins))]

    def start(self, ins, outs, send, recv):
        for cp in self._copies(ins, outs, send, recv):
            cp.start()

    def finish(self, ins, outs, send, recv):
        for cp in self._copies(ins, outs, send, recv):
            cp.wait()


class _ChipSwap:
    def __init__(self, parts):
        self.ins = list(parts)
        self.out_shapes = [jax.ShapeDtypeStruct((3,) + p.shape[1:], p.dtype) for p in parts]
        self.aliases = {}
        self.n_sems = 3 * len(parts)

    def _copies(self, ins, outs, send, recv):
        x, y, c = _mesh_pos()
        return [_remote(ins[t].at[2 * chip[0] + chip[1]], outs[t].at[k], send.at[3 * t + k], recv.at[3 * t + k], (*chip, c))
                for t in range(len(ins)) for k, chip in enumerate(_other_chips(x, y))]

    def start(self, ins, outs, send, recv):
        for cp in self._copies(ins, outs, send, recv):
            cp.start()

    def finish(self, ins, outs, send, recv):
        for cp in self._copies(ins, outs, send, recv):
            cp.wait()


class _PairShare:
    def __init__(self, fulls):
        self.ins = list(fulls)
        self.out_shapes = _like(fulls)
        self.aliases = {t: t for t in range(len(fulls))}
        self.n_sems = len(fulls)

    def _copies(self, ins, outs, send, recv):
        x, y, c = _mesh_pos()
        return [_remote(ins[t].at[c], outs[t].at[c], send.at[t], recv.at[t], (x, y, 1 - c)) for t in range(len(ins))]

    def start(self, ins, outs, send, recv):
        for cp in self._copies(ins, outs, send, recv):
            cp.start()

    def finish(self, ins, outs, send, recv):
        for cp in self._copies(ins, outs, send, recv):
            cp.wait()


class _ShareAll:
    def __init__(self, arrays):
        self.ins = list(arrays)
        self.out_shapes = [jax.ShapeDtypeStruct((N_DEV,) + a.shape, a.dtype) for a in arrays]
        self.aliases = {}
        self.n_sems = (N_DEV - 1) * len(arrays)

    def _peers(self):
        x, y, c = _mesh_pos()
        flips = [((r >> 2) & 1, (r >> 1) & 1, r & 1) for r in range(1, N_DEV)]
        return (x, y, c), [(x ^ fx, y ^ fy, c ^ fc) for fx, fy, fc in flips]

    def _sends(self, ins, outs, send, recv):
        (x, y, c), peers = self._peers()
        mine = 4 * x + 2 * y + c
        return [_remote(ins[a], outs[a].at[mine], send.at[7 * a + r], recv.at[7 * a + r], peer)
                for a in range(len(ins)) for r, peer in enumerate(peers)]

    def start(self, ins, outs, send, recv):
        for cp in self._sends(ins, outs, send, recv):
            cp.start()

    def finish(self, ins, outs, send, recv):
        (x, y, c), peers = self._peers()
        for a in range(len(ins)):
            for r, (px, py, pc) in enumerate(peers):
                blk = outs[a].at[4 * px + 2 * py + pc]
                _remote(blk, blk, send.at[7 * a + r], recv.at[7 * a + r], (x, y, c)).wait_recv()
        for cp in self._sends(ins, outs, send, recv):
            cp.wait_send()


def _gather_beside(bufs, name, collective_id):
    n = len(bufs)
    refs = [jax.new_ref(b, memory_space=pltpu.MemorySpace.HBM) for b in bufs]
    gather = _Gather(bufs)

    @pl.kernel(mesh=plsc.ScalarSubcoreMesh(axis_name="sequencer", num_cores=1), name=name,
               scratch_types=(pltpu.SemaphoreType.DMA((6 * n,)), pltpu.SemaphoreType.DMA((6 * n,))),
               compiler_params=pltpu.CompilerParams(collective_id=collective_id))
    def launch(send, recv):
        x, y, c = _mesh_pos()
        barrier = pltpu.get_barrier_semaphore()
        peers = [(*chip, c) for chip in _other_chips(x, y)] + [(x, y, 1 - c)]
        for peer in peers:
            pl.semaphore_signal(barrier, inc=1, device_id=peer, device_id_type=MESH)
        pl.semaphore_wait(barrier, len(peers))
        gather.start(refs, refs, send, recv)
        gather.near_end(refs, refs, send, recv)
        gather.finish(refs, refs, send, recv)

    launch()
    return [r[...] for r in refs]


def _pallas(body, operands, *, name, grid, in_specs, out_specs, out_shape, scratch_shapes=(), vmem_mib=32, riders=(),
            prefetch=None):
    in_specs, out_specs, out_shape, scratch_shapes = list(in_specs), list(out_specs), list(out_shape), list(scratch_shapes)
    if not riders and prefetch is None:
        outs = pl.pallas_call(body, name=name, grid=grid, in_specs=in_specs, out_specs=out_specs, out_shape=out_shape,
                              scratch_shapes=scratch_shapes, compiler_params=_params(vmem_mib, len(grid)))(*operands)
        return list(outs), []
    n_in, n_out, n_scr = len(in_specs), len(out_specs), len(scratch_shapes)
    r_in = [len(r.ins) for r in riders]
    r_out = [len(r.out_shapes) for r in riders]
    steps = 1
    for g in grid:
        steps *= g

    n_pre = 0 if prefetch is None else 1

    def wrapped(*refs):
        refs = list(refs)
        pre, refs = refs[:n_pre], refs[n_pre:]
        ins, refs = refs[:n_in], refs[n_in:]
        rins = []
        for k in r_in:
            rins.append(refs[:k])
            refs = refs[k:]
        outs, refs = refs[:n_out], refs[n_out:]
        routs = []
        for k in r_out:
            routs.append(refs[:k])
            refs = refs[k:]
        scr, sems = refs[:n_scr], refs[n_scr:]
        step = 0
        for ax, g in enumerate(grid):
            step = step * g + pl.program_id(ax)

        def each(what):
            for j, r in enumerate(riders):
                if hasattr(r, what):
                    getattr(r, what)(rins[j], routs[j], sems[2 * j], sems[2 * j + 1])

        if grid:
            pl.when(step == 0)(lambda: each("start"))
        else:
            each("start")
        body(*pre, *ins, *outs, *scr)
        if grid:
            @pl.when(step == steps - 1)
            def _():
                each("near_end")
                each("finish")
        else:
            each("near_end")
            each("finish")

    aliases, off_in, off_out = {}, n_pre + n_in, n_out
    for r, ki, ko in zip(riders, r_in, r_out):
        for i, o in r.aliases.items():
            aliases[off_in + i] = off_out + o
        off_in, off_out = off_in + ki, off_out + ko
    sems = []
    for r in riders:
        sems += [pltpu.SemaphoreType.DMA((r.n_sems,)), pltpu.SemaphoreType.DMA((r.n_sems,))]
    layout = dict(grid=grid, in_specs=in_specs + [ANY] * sum(r_in), out_specs=out_specs + [ANY] * sum(r_out),
                  scratch_shapes=scratch_shapes + sems)
    if prefetch is not None:
        layout = dict(grid_spec=pltpu.PrefetchScalarGridSpec(num_scalar_prefetch=1, **layout))
    res = pl.pallas_call(
        wrapped, name=name, **layout,
        out_shape=out_shape + [s for r in riders for s in r.out_shapes], input_output_aliases=aliases,
        compiler_params=pltpu.CompilerParams(dimension_semantics=("arbitrary",) * len(grid),
                                             vmem_limit_bytes=vmem_mib * MIB, has_side_effects=True),
    )(*([] if prefetch is None else [prefetch]), *operands, *[a for r in riders for a in r.ins])
    res = list(res)
    outs, res = res[:n_out], res[n_out:]
    routs = []
    for k in r_out:
        routs.append(res[:k])
        res = res[k:]
    return outs, routs


def _exchange(riders, name):
    return _pallas(lambda: None, [], name=name, grid=(), in_specs=[], out_specs=[], out_shape=[], riders=riders)[1]


def _in_hbm(a):
    return pltpu.with_memory_space_constraint(a, pltpu.HBM)


def _place_shard(w, layer, dtype, name):
    _, rows, cols = w.shape
    half = rows // 2
    br = _block_rows(half)
    nb = half // br
    mine = 2 * lax.axis_index("x") + lax.axis_index("y")

    def body(q_ref, w_ref, o_ref):
        o_ref[...] = w_ref[...].astype(dtype)

    return pl.pallas_call(
        body, name=name,
        grid_spec=pltpu.PrefetchScalarGridSpec(
            num_scalar_prefetch=1, grid=(2, nb),
            in_specs=[pl.BlockSpec((None, br, cols), lambda h, i, q: (layer, h * nb + i, 0))],
            out_specs=pl.BlockSpec((None, None, br, cols), lambda h, i, q: (q[0], h, i, 0))),
        out_shape=pltpu.HBM((N_CHIPS, 2, half, cols), dtype),
        compiler_params=_params(16, 2),
    )(jnp.reshape(mine, (1,)).astype(jnp.int32), w)


def _plain_copy(a, name):
    def body(a_ref, o_ref):
        o_ref[...] = a_ref[...]

    vmem = pl.BlockSpec(memory_space=pltpu.VMEM)
    return pl.pallas_call(body, name=name, in_specs=[vmem], out_specs=vmem,
                          out_shape=jax.ShapeDtypeStruct(a.shape, a.dtype))(a)


def _add_pair(g, recv, name):
    _, _, r, cdim = g.shape
    br = _block_rows(r, 256)
    c = lax.axis_index("c")

    def body(c_ref, g_ref, r_ref, o_ref):
        o_ref[...] = (g_ref[...] + r_ref[...]).astype(BF16)

    return pl.pallas_call(
        body, name=name,
        grid_spec=pltpu.PrefetchScalarGridSpec(
            num_scalar_prefetch=1, grid=(N_CHIPS, r // br),
            in_specs=[pl.BlockSpec((None, None, br, cdim), lambda q, i, c_ref: (q, c_ref[0], i, 0)),
                      pl.BlockSpec((None, br, cdim), lambda q, i, c_ref: (q, i, 0))],
            out_specs=pl.BlockSpec((None, br, cdim), lambda q, i, c_ref: (q, i, 0))),
        out_shape=pltpu.HBM((N_CHIPS, r, cdim), BF16),
        compiler_params=_params(16, 2),
    )(jnp.reshape(c, (1,)).astype(jnp.int32), _in_hbm(g), _in_hbm(recv))


def _add_chips(own, recv, name):
    _, r, cdim = own.shape
    br = _block_rows(r, 256)
    x, y, c = _mesh_pos()

    def body(pos_ref, own_ref, r_ref, o_ref):
        acc = own_ref[...].astype(F32)
        for k in range(3):
            acc = acc + r_ref[k].astype(F32)
        o_ref[...] = acc

    return pl.pallas_call(
        body, name=name,
        grid_spec=pltpu.PrefetchScalarGridSpec(
            num_scalar_prefetch=1, grid=(r // br,),
            in_specs=[pl.BlockSpec((None, br, cdim), lambda i, pos: (pos[0], i, 0)),
                      pl.BlockSpec((3, br, cdim), lambda i, pos: (0, i, 0))],
            out_specs=pl.BlockSpec((None, br, cdim), lambda i, pos: (pos[1], i, 0))),
        out_shape=pltpu.HBM((2, r, cdim), F32),
        compiler_params=_params(16, 1),
    )(jnp.stack([2 * x + y, c]).astype(jnp.int32), _in_hbm(own), _in_hbm(recv))


def _adam_math(w, m, v, g):
    c1 = 1.0 / (1.0 - ADAM_B1 ** ADAM_STEP)
    c2 = 1.0 / (1.0 - ADAM_B2 ** ADAM_STEP)
    m_new = ADAM_B1 * m + (1.0 - ADAM_B1) * g
    v_new = ADAM_B2 * v + (1.0 - ADAM_B2) * (g * g)
    return -ADAM_LR * ((m_new * c1) / (jnp.sqrt(v_new * c2) + ADAM_EPS) + ADAM_WD * w), m_new, v_new


SMALL_WEIGHTS = [
    ("ev_norm_g", (1, D_MODEL), ["ev_norm_g"], None), ("ev_conv_a_b", (1, A_DIM), ["ev_conv_a_b"], None),
    ("ev_ln_a_g", (1, A_DIM), ["ev_ln_a_g"], None), ("ev_ln_a_b", (1, A_DIM), ["ev_ln_a_b"], None),
    ("od_w_s", (C_GROUPS, CHUNK, CHUNK), ["od_w_s_lo", "od_w_s_hi"], None), ("od_b_s", (C_GROUPS, CHUNK), ["od_b_s"], None),
    ("mlp_norm_g", (2, D_MODEL), ["mlp_norm_g0", "mlp_norm_g1"], None), ("final_norm_g", (1, D_MODEL), ["final_norm_g"], None),
    ("ev_conv_a_w", (A_CONV_WIDTH, A_DIM // N_CHIPS), ["ev_conv_a_w"], A_DIM // N_CHIPS),
    ("ev_conv_b_w", (B_CONV_WIDTH, B_DIM // N_CHIPS), ["ev_conv_b_w"], B_DIM // N_CHIPS),
    ("od_norm_g", (1, D_MODEL // N_CHIPS), ["od_norm_g"], D_MODEL // N_CHIPS),
    ("od_b_in", (1, 2 * C_DIM // N_CHIPS), ["od_b_in"], 2 * C_DIM // N_CHIPS),
    ("od_ln_v_g", (1, C_DIM // N_CHIPS), ["od_ln_v_g"], C_DIM // N_CHIPS),
    ("od_ln_v_b", (1, C_DIM // N_CHIPS), ["od_ln_v_b"], C_DIM // N_CHIPS),
]


def _small_update(own, landed, weights):
    names = list(own.keys())
    n_g, n_w = len(names), len(SMALL_WEIGHTS)

    def body(*refs):
        refs = list(refs)
        own_refs = dict(zip(names, refs[:n_g]))
        land_refs = dict(zip(names, refs[n_g:2 * n_g]))
        wmv = [refs[2 * n_g + 3 * i:2 * n_g + 3 * i + 3] for i in range(n_w)]
        o0 = 2 * n_g + 3 * n_w
        loss_ref = refs[o0]
        outs = [refs[o0 + 1 + 4 * i:o0 + 5 + 4 * i] for i in range(n_w)]
        acc = dict(zip(names, refs[o0 + 1 + 4 * n_w:]))
        x, y, c = _mesh_pos()
        mine, chip = 4 * x + 2 * y + c, 2 * x + y

        for nm in names:
            for d in range(N_DEV):
                def add(term, nm=nm, d=d):
                    acc[nm][...] = term if d == 0 else acc[nm][...] + term
                pl.when(mine == d)(lambda nm=nm, add=add: add(own_refs[nm][...]))
                pl.when(mine != d)(lambda nm=nm, d=d, add=add: add(land_refs[nm][d]))
        loss_ref[...] = acc["loss"][...]

        def update(i, rows, g):
            w_ref, m_ref, v_ref = wmv[i]
            delta, m_new, v_new = _adam_math(w_ref[rows], m_ref[rows], v_ref[rows], g)
            for ref, val in zip(outs[i], (g, delta, m_new, v_new)):
                ref[rows] = val

        for i, (_, shape, grads, per_chip) in enumerate(SMALL_WEIGHTS):
            for row, gname in enumerate(grads):
                per_grad = shape[0] // len(grads)
                rows = slice(row * per_grad, (row + 1) * per_grad)
                if per_chip is None:
                    update(i, rows, acc[gname][...])
                else:
                    for q in range(N_CHIPS):
                        pl.when(chip == q)(lambda i=i, rows=rows, gname=gname, q=q, per_chip=per_chip:
                                           update(i, rows, acc[gname][:, q * per_chip:(q + 1) * per_chip]))

    operands = [own[nm] for nm in names] + [landed[nm] for nm in names]
    for nm, _, _, _ in SMALL_WEIGHTS:
        operands += list(weights[nm])
    out_shape = [jax.ShapeDtypeStruct((1, 1), F32)]
    for _, shape, _, _ in SMALL_WEIGHTS:
        out_shape += [jax.ShapeDtypeStruct(shape, F32)] * 4
    res = pl.pallas_call(
        body, name="small_update", grid=(1,),
        in_specs=[_full_spec(a.shape) for a in operands], out_specs=[_full_spec(s.shape) for s in out_shape],
        out_shape=out_shape, scratch_shapes=[pltpu.VMEM(own[nm].shape, F32) for nm in names],
        compiler_params=_params(32, 1),
    )(*[_in_hbm(a) for a in operands])
    return res[0], {nm: res[1 + 4 * i:5 + 4 * i] for i, (nm, _, _, _) in enumerate(SMALL_WEIGHTS)}


def _adamw(w, m, v, grads, name, riders=()):
    layers, r, cdim = w.shape
    br = _block_rows(r, 256 if cdim > LANES else 1024)

    def body(*refs):
        w_ref, m_ref, v_ref = refs[:3]
        g_refs = refs[3:3 + layers]
        go_ref, d_ref, mo_ref, vo_ref = refs[3 + layers:]
        layer = pl.program_id(0)
        for l in range(layers):
            @pl.when(layer == l)
            def _(l=l):
                g = g_refs[l][...]
                go_ref[...] = g
                d_ref[...], mo_ref[...], vo_ref[...] = _adam_math(w_ref[...], m_ref[...], v_ref[...], g)

    spec3 = pl.BlockSpec((None, br, cdim), lambda l, i: (l, i, 0))
    spec2 = pl.BlockSpec((br, cdim), lambda l, i: (i, 0))
    out = jax.ShapeDtypeStruct((layers, r, cdim), F32)
    return _pallas(body, [w, m, v, *[_in_hbm(g) for g in grads]], name=name, grid=(layers, r // br),
                   in_specs=[spec3, spec3, spec3] + [spec2] * layers, out_specs=[spec3] * 4, out_shape=[out] * 4,
                   vmem_mib=32, riders=riders)


def _fill_shifted(buf, rows):
    for b in range(1, SUBLANES):
        buf[b, 0:rows - SUBLANES, :] = buf[0, b:b + rows - SUBLANES, :]


def _window(buf, start, size):
    return buf[start % SUBLANES, start - start % SUBLANES:start - start % SUBLANES + size, :]


def _conv31(src, w_ref, r0, base, init):
    acc = init
    for k in range(A_CONV_WIDTH):
        acc = acc + w_ref[k:k + 1, :] * _window(src, base + k + r0, CONV_ROWS)
    return acc


def _fwd_even(x, norm_g, w_in, conv_a_w, conv_a_b, ln_g, ln_b, conv_b_w, w_out, *, tm, seq, riders=()):
    tokens = x.shape[0]
    nt, tps = tokens // tm, seq // tm

    def body(x_ref, g_ref, win_hbm, caw_ref, cab_ref, lng_ref, lnb_ref, cbw_ref, wout_hbm,
             h_ref, n_ref, z_ref, a2_ref, cv_ref, mix_ref, win_v, wout_v, pa, pb, sem):
        i = pl.program_id(0)

        _load_weights([(win_hbm, win_v, False), (wout_hbm, wout_v, True)], sem)

        xv = x_ref[...]
        nf, _ = _rms_fwd(xv, g_ref[...])
        n = nf.astype(BF16)
        n_ref[...] = n
        z = jnp.concatenate([_dot(n, win_v[j]) for j in range(N_CHIPS)], axis=1)
        z_ref[...] = z.astype(BF16)
        a_val, a_gate = z[:, 0:A_DIM], z[:, A_DIM:2 * A_DIM]
        b_gate, c_gate, b_val = z[:, 1024:1536], z[:, 1536:2048], z[:, 2048:2560]

        first = (i % tps) == 0

        @pl.when(first)
        def _():
            pa[0, 0:A_HALO, :] = jnp.zeros((A_HALO, A_DIM), F32)
            pb[0:B_HALO, :] = jnp.zeros((B_HALO, B_DIM), F32)

        @pl.when(jnp.logical_not(first))
        def _():
            pa[0, 0:A_HALO, :] = pa[0, tm:tm + A_HALO, :]
            pb[0:B_HALO, :] = pb[tm:tm + B_HALO, :]

        pa[0, A_HALO:A_HALO + tm, :] = a_val * jax.nn.sigmoid(a_gate)
        pb[B_HALO:B_HALO + tm, :] = c_gate * b_val
        _fill_shifted(pa, A_HALO + tm)
        bias = jnp.broadcast_to(cab_ref[...], (CONV_ROWS, A_DIM))
        for r0 in range(0, tm, CONV_ROWS):
            a2_ref[r0:r0 + CONV_ROWS, :] = _conv31(pa, caw_ref, r0, A_HALO - (A_CONV_WIDTH - 1), bias)
        xhat, _ = _ln_stats(a2_ref[...])
        a3 = xhat * lng_ref[...] + lnb_ref[...]
        a4 = a3 * jax.nn.sigmoid(a3)
        cv = cbw_ref[0:1, :] * pb[B_HALO - 2:B_HALO - 2 + tm, :]
        cv = cv + cbw_ref[1:2, :] * pb[B_HALO - 1:B_HALO - 1 + tm, :]
        cv = cv + cbw_ref[2:3, :] * pb[B_HALO:B_HALO + tm, :]
        cv_ref[...] = cv.astype(BF16)
        mix = jnp.concatenate([a4, b_gate * cv], axis=1).astype(BF16)
        mix_ref[...] = mix
        h_ref[...] = xv + _dot(mix, wout_v[...])

    shp = lambda cols, dt: jax.ShapeDtypeStruct((tokens, cols), dt)
    return _pallas(
        body, [x, norm_g, w_in, conv_a_w, conv_a_b, ln_g, ln_b, conv_b_w, w_out], name="fwd_even", grid=(nt,),
        in_specs=[_row_spec(tm, D_MODEL), _full_spec((1, D_MODEL)), ANY, _full_spec((A_CONV_WIDTH, A_DIM)),
                  _full_spec((1, A_DIM)), _full_spec((1, A_DIM)), _full_spec((1, A_DIM)),
                  _full_spec((B_CONV_WIDTH, B_DIM)), ANY],
        out_specs=[_row_spec(tm, D_MODEL), _row_spec(tm, D_MODEL), _row_spec(tm, IN_EVEN), _row_spec(tm, A_DIM),
                   _row_spec(tm, B_DIM), _row_spec(tm, D_MODEL)],
        out_shape=[shp(D_MODEL, F32), shp(D_MODEL, BF16), shp(IN_EVEN, BF16), shp(A_DIM, F32), shp(B_DIM, BF16),
                   shp(D_MODEL, BF16)],
        scratch_shapes=[pltpu.VMEM((N_CHIPS, D_MODEL, IN_EVEN // N_CHIPS), BF16), pltpu.VMEM((D_MODEL, D_MODEL), BF16),
                        pltpu.VMEM((SUBLANES, A_HALO + tm, A_DIM), F32), pltpu.VMEM((B_HALO + tm, B_DIM), F32),
                        pltpu.SemaphoreType.DMA((N_LOADS,))],
        vmem_mib=56, riders=riders)


def _loss_tail(xv, g, target, loss_ref, dh_ref, dhb_ref, dg_ref):
    @pl.when(pl.program_id(0) == 0)
    def _():
        loss_ref[...] = jnp.zeros((1, 1), F32)
        dg_ref[...] = jnp.zeros((1, D_MODEL), F32)

    out, rstd = _rms_fwd(xv, g)
    err = out - target
    per_token = jnp.sum(err * err, axis=1, keepdims=True) * (1.0 / D_MODEL)
    loss_ref[...] += 0.5 * jnp.sum(per_token, axis=0, keepdims=True)
    dx, dg = _rms_bwd(err * (1.0 / D_MODEL), xv, rstd, g)
    dh_ref[...] = dx
    dhb_ref[...] = dx.astype(BF16)
    dg_ref[...] += dg


def _fwd_mlp(h, norm_g, w1, w2, layer, *, tm, riders=(), head=None):
    tokens = h.shape[0]
    nt = tokens // tm
    fs = D_FF // N_CHIPS
    n_in = 4 if head is None else 6

    def body(*refs):
        h_ref, g_ref, w1_hbm, w2_hbm = refs[:4]
        w1_v, w2_v, sem = refs[-3:]
        outs = refs[n_in:-3]
        n_ref, p_ref, q_ref = outs[1:4] if head is None else outs[0:3]
        _load_weights([(w1_hbm, w1_v, False), (w2_hbm, w2_v, False)], sem)

        xv = h_ref[...]
        nf, _ = _rms_fwd(xv, g_ref[...])
        n = nf.astype(BF16)
        n_ref[...] = n
        acc = xv
        for j in range(N_CHIPS):
            p = _dot(n, w1_v[j])
            p_ref[:, j * fs:(j + 1) * fs] = p.astype(BF16)
            r = jnp.maximum(p, 0.0)
            q = (r * r).astype(BF16)
            q_ref[:, j * fs:(j + 1) * fs] = q
            acc = acc + _dot(q, w2_v[j])
        if head is None:
            outs[0][...] = acc
        else:
            _loss_tail(acc, refs[4][...], refs[5][...], *outs[3:7])

    shp = lambda cols, dt: jax.ShapeDtypeStruct((tokens, cols), dt)
    saved_specs = [_row_spec(tm, D_MODEL), _row_spec(tm, D_FF), _row_spec(tm, D_FF)]
    saved_shapes = [shp(D_MODEL, BF16), shp(D_FF, BF16), shp(D_FF, BF16)]
    if head is None:
        operands, in_specs = [h, norm_g, w1, w2], [_row_spec(tm, D_MODEL), _full_spec((1, D_MODEL)), ANY, ANY]
        out_specs, out_shape = [_row_spec(tm, D_MODEL)] + saved_specs, [shp(D_MODEL, F32)] + saved_shapes
    else:
        operands = [h, norm_g, w1, w2, *head]
        in_specs = [_row_spec(tm, D_MODEL), _full_spec((1, D_MODEL)), ANY, ANY, _full_spec((1, D_MODEL)), _row_spec(tm, D_MODEL)]
        out_specs = saved_specs + [_full_spec((1, 1)), _row_spec(tm, D_MODEL), _row_spec(tm, D_MODEL), _full_spec((1, D_MODEL))]
        out_shape = saved_shapes + [jax.ShapeDtypeStruct((1, 1), F32), shp(D_MODEL, F32), shp(D_MODEL, BF16),
                                    jax.ShapeDtypeStruct((1, D_MODEL), F32)]
    return _pallas(
        body, operands, name=f"fwd_mlp{layer}", grid=(nt,), in_specs=in_specs, out_specs=out_specs, out_shape=out_shape,
        scratch_shapes=[pltpu.VMEM((N_CHIPS, D_MODEL, fs), BF16), pltpu.VMEM((N_CHIPS, fs, D_MODEL), BF16),
                        pltpu.SemaphoreType.DMA((N_LOADS,))],
        vmem_mib=56, riders=riders)


def _tril_mask():
    row = lax.broadcasted_iota(jnp.int32, (CHUNK, CHUNK), 0)
    col = lax.broadcasted_iota(jnp.int32, (CHUNK, CHUNK), 1)
    return row >= col


def _triu_mask():
    row = lax.broadcasted_iota(jnp.int32, (CHUNK, CHUNK), 0)
    col = lax.broadcasted_iota(jnp.int32, (CHUNK, CHUNK), 1)
    return row <= col


def _fwd_odd(h, norm_g, w_in, b_in, ln_g, ln_b, w_s, b_s_rows, w_out, *, tm, riders=()):
    tokens = h.shape[0]
    nt = tokens // tm
    cs = 2 * C_DIM // N_CHIPS

    def body(h_ref, g_ref, win_hbm, bin_ref, lng_ref, lnb_ref, ws_ref, bs_ref, wout_hbm,
             ho_ref, n_ref, s_ref, cdf_ref, sv_ref, y_ref, win_v, wout_v, bd, sem):
        _load_weights([(win_hbm, win_v, False), (wout_hbm, wout_v, True)], sem)

        @pl.when(pl.program_id(0) == 0)
        def _():
            mask = _tril_mask()
            bd[...] = jnp.zeros(bd.shape, BF16)
            for g in range(C_GROUPS):
                w = jnp.where(mask, ws_ref[g], 0.0).astype(BF16)
                bd[g, 0:CHUNK, 0:CHUNK] = w
                bd[g, CHUNK:PAIR, CHUNK:PAIR] = w

        xv = h_ref[...]
        nf, _ = _rms_fwd(xv, g_ref[...])
        n = nf.astype(BF16)
        n_ref[...] = n
        s = jnp.concatenate([_dot(n, win_v[j]) for j in range(N_CHIPS)], axis=1) + bin_ref[...]
        s_ref[...] = s.astype(BF16)
        cdf = _gelu_cdf(s)
        cdf_ref[...] = cdf.astype(BF16)
        zz = s * cdf
        u, v = zz[:, 0:C_DIM], zz[:, C_DIM:2 * C_DIM]
        xhat, _ = _ln_stats(v)
        vn = (xhat * lng_ref[...] + lnb_ref[...]).astype(BF16)
        for g in range(C_GROUPS):
            cols = slice(g * CHUNK, (g + 1) * CHUNK)
            bias = jnp.concatenate([bs_ref[g], bs_ref[g]], axis=0)
            for r0 in range(0, tm, PAIR):
                sv = _dot(bd[g], vn[r0:r0 + PAIR, cols]) + bias
                sv_ref[r0:r0 + PAIR, cols] = sv.astype(BF16)
                y_ref[r0:r0 + PAIR, cols] = (u[r0:r0 + PAIR, cols] * sv).astype(BF16)
        ho_ref[...] = xv + _dot(y_ref[...], wout_v[...])

    shp = lambda cols, dt: jax.ShapeDtypeStruct((tokens, cols), dt)
    return _pallas(
        body, [h, norm_g, w_in, b_in, ln_g, ln_b, w_s, b_s_rows, w_out], name="fwd_odd", grid=(nt,),
        in_specs=[_row_spec(tm, D_MODEL), _full_spec((1, D_MODEL)), ANY, _full_spec((1, 2 * C_DIM)),
                  _full_spec((1, C_DIM)), _full_spec((1, C_DIM)), _full_spec((C_GROUPS, CHUNK, CHUNK)),
                  _full_spec((C_GROUPS, CHUNK, CHUNK)), ANY],
        out_specs=[_row_spec(tm, D_MODEL), _row_spec(tm, D_MODEL), _row_spec(tm, 2 * C_DIM), _row_spec(tm, 2 * C_DIM),
                   _row_spec(tm, C_DIM), _row_spec(tm, C_DIM)],
        out_shape=[shp(D_MODEL, F32), shp(D_MODEL, BF16), shp(2 * C_DIM, BF16), shp(2 * C_DIM, BF16), shp(C_DIM, BF16),
                   shp(C_DIM, BF16)],
        scratch_shapes=[pltpu.VMEM((N_CHIPS, D_MODEL, cs), BF16), pltpu.VMEM((C_DIM, D_MODEL), BF16),
                        pltpu.VMEM((C_GROUPS, PAIR, PAIR), BF16), pltpu.SemaphoreType.DMA((N_LOADS,))],
        vmem_mib=56, riders=riders)


def _bwd_mlp(dh, h, norm_g, p, w1, w2, layer, *, tm, riders=()):
    tokens = h.shape[0]
    nt = tokens // tm
    fs = D_FF // N_CHIPS

    def body(dh_ref, h_ref, g_ref, p_ref, w1_hbm, w2_hbm, dx_ref, dxb_ref, dp_ref, dg_ref, w1_v, w2_v, sem):
        @pl.when(pl.program_id(0) == 0)
        def _():
            dg_ref[...] = jnp.zeros((1, D_MODEL), F32)

        _load_weights([(w1_hbm, w1_v, False), (w2_hbm, w2_v, False)], sem)

        dhv = dh_ref[...]
        dhb = dhv.astype(BF16)
        dn = jnp.zeros((tm, D_MODEL), F32)
        for j in range(N_CHIPS):
            dq = _dot_nt(dhb, w2_v[j])
            r = jnp.maximum(p_ref[:, j * fs:(j + 1) * fs].astype(F32), 0.0)
            dp = ((2.0 * r) * dq).astype(BF16)
            dp_ref[:, j * fs:(j + 1) * fs] = dp
            dn = dn + _dot_nt(dp, w1_v[j])
        xv = h_ref[...]
        g = g_ref[...]
        _, rstd = _rms_fwd(xv, g)
        dx, dg = _rms_bwd(dn, xv, rstd, g)
        dx_ref[...] = dhv + dx
        dxb_ref[...] = (dhv + dx).astype(BF16)
        dg_ref[...] += dg

    return _pallas(
        body, [dh, h, norm_g, p, w1, w2], name=f"bwd_mlp{layer}", grid=(nt,),
        in_specs=[_row_spec(tm, D_MODEL), _row_spec(tm, D_MODEL), _full_spec((1, D_MODEL)), _row_spec(tm, D_FF), ANY, ANY],
        out_specs=[_row_spec(tm, D_MODEL), _row_spec(tm, D_MODEL), _row_spec(tm, D_FF), _full_spec((1, D_MODEL))],
        out_shape=[jax.ShapeDtypeStruct((tokens, D_MODEL), F32), jax.ShapeDtypeStruct((tokens, D_MODEL), BF16),
                   jax.ShapeDtypeStruct((tokens, D_FF), BF16), jax.ShapeDtypeStruct((1, D_MODEL), F32)],
        scratch_shapes=[pltpu.VMEM((N_CHIPS, D_MODEL, fs), BF16), pltpu.VMEM((N_CHIPS, fs, D_MODEL), BF16),
                        pltpu.SemaphoreType.DMA((N_LOADS,))],
        vmem_mib=56, riders=riders)


def _bwd_odd(dh, h, norm_g, s, cdf, sv, w_in, ln_g, ln_b, w_s, w_out, *, tm, riders=()):
    tokens = h.shape[0]
    nt = tokens // tm
    cs = 2 * C_DIM // N_CHIPS

    def body(dh_ref, h_ref, g_ref, s_ref, cdf_ref, sv_ref, win_hbm, lng_ref, lnb_ref, ws_ref, wout_hbm,
             dx_ref, dxb_ref, ds_ref, dg_ref, dbin_ref, dlng_ref, dlnb_ref, dws_ref, dbs_ref,
             win_v, wout_v, bdt, dws_acc, dbs_acc, dvn, sem):
        i = pl.program_id(0)

        _load_weights([(win_hbm, win_v, False), (wout_hbm, wout_v, True)], sem)

        @pl.when(i == 0)
        def _():
            mask_t = _triu_mask()
            bdt[...] = jnp.zeros(bdt.shape, BF16)
            for g in range(C_GROUPS):
                wt = jnp.where(mask_t, ws_ref[g].T, 0.0).astype(BF16)
                bdt[g, 0:CHUNK, 0:CHUNK] = wt
                bdt[g, CHUNK:PAIR, CHUNK:PAIR] = wt
            dws_acc[...] = jnp.zeros(dws_acc.shape, F32)
            dbs_acc[...] = jnp.zeros(dbs_acc.shape, F32)
            dg_ref[...] = jnp.zeros(dg_ref.shape, F32)
            dbin_ref[...] = jnp.zeros(dbin_ref.shape, F32)
            dlng_ref[...] = jnp.zeros(dlng_ref.shape, F32)
            dlnb_ref[...] = jnp.zeros(dlnb_ref.shape, F32)

        dhv = dh_ref[...]
        dy = _dot_nt(dhv.astype(BF16), wout_v[...])
        sf = s_ref[...].astype(F32)
        cdf = cdf_ref[...].astype(F32)
        pdf = jnp.exp(-0.5 * sf * sf) * 0.3989422804014327
        zz = sf * cdf
        dgelu = cdf + sf * pdf
        u, v = zz[:, 0:C_DIM], zz[:, C_DIM:2 * C_DIM]
        xhat, rs = _ln_stats(v)
        lng = lng_ref[...]
        vn = (xhat * lng + lnb_ref[...]).astype(BF16)
        du = dy * sv_ref[...].astype(F32)
        dsv = dy * u
        dsvb = dsv.astype(BF16)
        for g in range(C_GROUPS):
            cols = slice(g * CHUNK, (g + 1) * CHUNK)
            for r0 in range(0, tm, PAIR):
                blk = dsvb[r0:r0 + PAIR, cols]
                dvn[r0:r0 + PAIR, cols] = _dot(bdt[g], blk)
                dws_acc[g] += _dot_nt(blk, vn[r0:r0 + PAIR, cols])
                dbs_acc[g] += dsv[r0:r0 + CHUNK, cols] + dsv[r0 + CHUNK:r0 + PAIR, cols]
        dv, dlng, dlnb = _ln_bwd(dvn[...], xhat, rs, lng)
        dlng_ref[...] += dlng
        dlnb_ref[...] += dlnb
        ds = jnp.concatenate([du, dv], axis=1) * dgelu
        dbin_ref[...] += jnp.sum(ds, axis=0, keepdims=True)
        dsb = ds.astype(BF16)
        ds_ref[...] = dsb
        dn = jnp.zeros((tm, D_MODEL), F32)
        for j in range(N_CHIPS):
            dn = dn + _dot_nt(dsb[:, j * cs:(j + 1) * cs], win_v[j])
        xv = h_ref[...]
        g = g_ref[...]
        _, rstd = _rms_fwd(xv, g)
        dx, dg = _rms_bwd(dn, xv, rstd, g)
        dx_ref[...] = dhv + dx
        dxb_ref[...] = (dhv + dx).astype(BF16)
        dg_ref[...] += dg

        @pl.when(i == nt - 1)
        def _():
            mask = _tril_mask()
            for g in range(C_GROUPS):
                full = dws_acc[g]
                dws_ref[g] = jnp.where(mask, full[0:CHUNK, 0:CHUNK] + full[CHUNK:PAIR, CHUNK:PAIR], 0.0)
                dbs_ref[g:g + 1, :] = jnp.sum(dbs_acc[g].T, axis=0, keepdims=True)

    row = lambda cols: jax.ShapeDtypeStruct((1, cols), F32)
    return _pallas(
        body, [dh, h, norm_g, s, cdf, sv, w_in, ln_g, ln_b, w_s, w_out], name="bwd_odd", grid=(nt,),
        in_specs=[_row_spec(tm, D_MODEL), _row_spec(tm, D_MODEL), _full_spec((1, D_MODEL)), _row_spec(tm, 2 * C_DIM),
                  _row_spec(tm, 2 * C_DIM), _row_spec(tm, C_DIM), ANY, _full_spec((1, C_DIM)), _full_spec((1, C_DIM)),
                  _full_spec((C_GROUPS, CHUNK, CHUNK)), ANY],
        out_specs=[_row_spec(tm, D_MODEL), _row_spec(tm, D_MODEL), _row_spec(tm, 2 * C_DIM), _full_spec((1, D_MODEL)),
                   _full_spec((1, 2 * C_DIM)),
                   _full_spec((1, C_DIM)), _full_spec((1, C_DIM)), _full_spec((C_GROUPS, CHUNK, CHUNK)),
                   _full_spec((C_GROUPS, CHUNK))],
        out_shape=[jax.ShapeDtypeStruct((tokens, D_MODEL), F32), jax.ShapeDtypeStruct((tokens, D_MODEL), BF16),
                   jax.ShapeDtypeStruct((tokens, 2 * C_DIM), BF16),
                   row(D_MODEL), row(2 * C_DIM), row(C_DIM), row(C_DIM),
                   jax.ShapeDtypeStruct((C_GROUPS, CHUNK, CHUNK), F32), jax.ShapeDtypeStruct((C_GROUPS, CHUNK), F32)],
        scratch_shapes=[pltpu.VMEM((N_CHIPS, D_MODEL, cs), BF16), pltpu.VMEM((C_DIM, D_MODEL), BF16),
                        pltpu.VMEM((C_GROUPS, PAIR, PAIR), BF16), pltpu.VMEM((C_GROUPS, PAIR, PAIR), F32),
                        pltpu.VMEM((C_GROUPS, CHUNK, CHUNK), F32), pltpu.VMEM((tm, C_DIM), F32),
                        pltpu.SemaphoreType.DMA((N_LOADS,))],
        vmem_mib=56, riders=riders)


def _bwd_even(dh, x, norm_g, z, a2, cv, w_in, conv_a_w, ln_g, ln_b, conv_b_w, w_out, *, tm, seq, riders=()):
    tokens = x.shape[0]
    nt, tps = tokens // tm, seq // tm
    ws = IN_EVEN // N_CHIPS

    def body(dh_ref, x_ref, g_ref, z_ref, a2_ref, cv_ref, win_hbm, caw_ref, lng_ref, lnb_ref, cbw_ref, wout_hbm,
             dx_ref, dz_ref, dg_ref, dcaw_ref, dcab_ref, dlng_ref, dlnb_ref, dcbw_ref,
             win_v, wout_v, ea, eb, a1s, da1s, sigs, wide, dw_acc, sem):
        i = pl.program_id(0)

        _load_weights([(win_hbm, win_v, False), (wout_hbm, wout_v, True)], sem)

        @pl.when(i == 0)
        def _():
            dw_acc[...] = jnp.zeros(dw_acc.shape, F32)
            for ref in (dg_ref, dcab_ref, dlng_ref, dlnb_ref, dcbw_ref):
                ref[...] = jnp.zeros(ref.shape, F32)

        last = ((nt - 1 - i) % tps) == tps - 1

        @pl.when(last)
        def _():
            ea[0, tm:tm + A_HALO, :] = jnp.zeros((A_HALO, A_DIM), F32)
            eb[tm:tm + B_HALO, :] = jnp.zeros((B_HALO, B_DIM), F32)

        @pl.when(jnp.logical_not(last))
        def _():
            ea[0, tm:tm + A_HALO, :] = ea[0, 0:A_HALO, :]
            eb[tm:tm + B_HALO, :] = eb[0:B_HALO, :]

        wide[...] = _dot_nt(dh_ref[...].astype(BF16), wout_v[...])
        lng, lnb = lng_ref[...], lnb_ref[...]
        zero_row = jnp.zeros((1, A_DIM), F32)
        dlng, dlnb, dcab = zero_row, zero_row, zero_row
        for r0 in range(0, tm, ELEM_ROWS):
            rows = slice(r0, r0 + ELEM_ROWS)
            a_val, a_gate = z_ref[rows, 0:A_DIM].astype(F32), z_ref[rows, A_DIM:2 * A_DIM].astype(F32)
            xhat, rs = _ln_stats(a2_ref[rows, :])
            a3 = xhat * lng + lnb
            sg = jax.nn.sigmoid(a3)
            da3 = wide[rows, 0:A_DIM] * (sg * (1.0 + a3 * (1.0 - sg)))
            da2, g_part, b_part = _ln_bwd(da3, xhat, rs, lng)
            dlng, dlnb, dcab = dlng + g_part, dlnb + b_part, dcab + jnp.sum(da2, axis=0, keepdims=True)
            ea[0, rows, :] = da2
            eb[rows, :] = wide[rows, A_DIM:A_DIM + B_DIM] * z_ref[rows, 1024:1536].astype(F32)
            sig = jax.nn.sigmoid(a_gate)
            sigs[rows, :] = sig
            a1s[rows, :] = a_val * sig
        dlng_ref[...] += dlng
        dlnb_ref[...] += dlnb
        dcab_ref[...] += dcab
        _fill_shifted(ea, tm + A_HALO)
        for r0 in range(0, tm, CONV_ROWS):
            acc = jnp.zeros((CONV_ROWS, A_DIM), F32)
            for j in range(A_CONV_WIDTH):
                acc = acc + caw_ref[A_CONV_WIDTH - 1 - j:A_CONV_WIDTH - j, :] * _window(ea, r0 + j, CONV_ROWS)
            da1s[r0:r0 + CONV_ROWS, :] = acc
        for j0 in range(0, A_CONV_WIDTH, DW_TAPS):
            taps = range(j0, min(j0 + DW_TAPS, A_CONV_WIDTH))
            part = [jnp.zeros((CONV_ROWS, A_DIM), F32) for _ in taps]
            for r0 in range(0, tm, CONV_ROWS):
                a1c = a1s[r0:r0 + CONV_ROWS, :]
                for u, j in enumerate(taps):
                    part[u] = part[u] + _window(ea, r0 + j, CONV_ROWS) * a1c
            for u, j in enumerate(taps):
                dw_acc[A_CONV_WIDTH - 1 - j] += part[u]
        dcbw = [jnp.zeros((1, B_DIM), F32) for _ in range(B_CONV_WIDTH)]
        for r0 in range(0, tm, ELEM_ROWS):
            rows = slice(r0, r0 + ELEM_ROWS)
            da1, sig = da1s[rows, :], sigs[rows, :]
            dz_ref[rows, 0:A_DIM] = (da1 * sig).astype(BF16)
            dz_ref[rows, A_DIM:2 * A_DIM] = (da1 * z_ref[rows, 0:A_DIM].astype(F32) * (sig * (1.0 - sig))).astype(BF16)
            c_gate, b_val = z_ref[rows, 1536:2048].astype(F32), z_ref[rows, 2048:2560].astype(F32)
            dz_ref[rows, 1024:1536] = (wide[rows, A_DIM:A_DIM + B_DIM] * cv_ref[rows, :].astype(F32)).astype(BF16)
            cb = c_gate * b_val
            dcb = jnp.zeros((ELEM_ROWS, B_DIM), F32)
            for j in range(B_CONV_WIDTH):
                k = B_CONV_WIDTH - 1 - j
                sl = eb[r0 + j:r0 + j + ELEM_ROWS, :]
                dcb = dcb + cbw_ref[k:k + 1, :] * sl
                dcbw[k] = dcbw[k] + jnp.sum(sl * cb, axis=0, keepdims=True)
            dz_ref[rows, 1536:2048] = (dcb * b_val).astype(BF16)
            dz_ref[rows, 2048:2560] = (dcb * c_gate).astype(BF16)
        for k in range(B_CONV_WIDTH):
            dcbw_ref[k:k + 1, :] += dcbw[k]
        dn = jnp.zeros((tm, D_MODEL), F32)
        for j in range(N_CHIPS):
            dn = dn + _dot_nt(dz_ref[:, j * ws:(j + 1) * ws], win_v[j])
        wide[...] = dn
        g = g_ref[...]
        dg = jnp.zeros((1, D_MODEL), F32)
        for r0 in range(0, tm, ELEM_ROWS):
            rows = slice(r0, r0 + ELEM_ROWS)
            xv = x_ref[rows, :]
            _, rstd = _rms_fwd(xv, g)
            dx, dg_part = _rms_bwd(wide[rows, :], xv, rstd, g)
            dx_ref[rows, :] = dh_ref[rows, :] + dx
            dg = dg + dg_part
        dg_ref[...] += dg

        @pl.when(i == nt - 1)
        def _():
            for k in range(A_CONV_WIDTH):
                dcaw_ref[k:k + 1, :] = jnp.sum(dw_acc[k], axis=0, keepdims=True)

    row = lambda cols: jax.ShapeDtypeStruct((1, cols), F32)
    rs_ = functools.partial(_row_spec, rev_nt=nt)
    return _pallas(
        body, [dh, x, norm_g, z, a2, cv, w_in, conv_a_w, ln_g, ln_b, conv_b_w, w_out], name="bwd_even", grid=(nt,),
        in_specs=[rs_(tm, D_MODEL), rs_(tm, D_MODEL), _full_spec((1, D_MODEL)), rs_(tm, IN_EVEN), rs_(tm, A_DIM),
                  rs_(tm, B_DIM), ANY, _full_spec((A_CONV_WIDTH, A_DIM)), _full_spec((1, A_DIM)), _full_spec((1, A_DIM)),
                  _full_spec((B_CONV_WIDTH, B_DIM)), ANY],
        out_specs=[rs_(tm, D_MODEL), rs_(tm, IN_EVEN), _full_spec((1, D_MODEL)), _full_spec((A_CONV_WIDTH, A_DIM)),
                   _full_spec((1, A_DIM)), _full_spec((1, A_DIM)), _full_spec((1, A_DIM)), _full_spec((B_CONV_WIDTH, B_DIM))],
        out_shape=[jax.ShapeDtypeStruct((tokens, D_MODEL), F32), jax.ShapeDtypeStruct((tokens, IN_EVEN), BF16),
                   row(D_MODEL), jax.ShapeDtypeStruct((A_CONV_WIDTH, A_DIM), F32), row(A_DIM), row(A_DIM), row(A_DIM),
                   jax.ShapeDtypeStruct((B_CONV_WIDTH, B_DIM), F32)],
        scratch_shapes=[pltpu.VMEM((N_CHIPS, D_MODEL, ws), BF16), pltpu.VMEM((D_MODEL, D_MODEL), BF16),
                        pltpu.VMEM((SUBLANES, tm + A_HALO, A_DIM), F32), pltpu.VMEM((tm + B_HALO, B_DIM), F32),
                        pltpu.VMEM((tm, A_DIM), F32), pltpu.VMEM((tm, A_DIM), F32), pltpu.VMEM((tm, A_DIM), F32),
                        pltpu.VMEM((tm, D_MODEL), F32),
                        pltpu.VMEM((A_CONV_WIDTH, CONV_ROWS, A_DIM), F32), pltpu.SemaphoreType.DMA((N_LOADS,))],
        vmem_mib=56, riders=riders)


def _wgrad(a, b, name, *, col_shards, riders=()):
    tokens, m = a.shape
    n = b.shape[1]
    kc = 512
    if col_shards:
        bm, bn = m // 2, n // N_CHIPS
        grid = (2, N_CHIPS)
        out_spec = pl.BlockSpec((None, None, bm, bn), lambda i, j: (j, i, 0, 0))
    elif m // 8 >= MXU_ROWS:
        bm, bn = m // 8, n
        grid = (8, 1)
        out_spec = pl.BlockSpec((None, None, bm, bn), lambda i, j: (i // 2, i % 2, 0, 0))
    else:
        bm, bn = m // N_CHIPS, n
        grid = (N_CHIPS, 1)
        out_spec = pl.BlockSpec((None, 2, bm // 2, bn), lambda i, j: (i, 0, 0, 0))

    def body(a_ref, b_ref, o_ref):
        acc = jnp.zeros((bm, bn), F32)
        for k0 in range(0, tokens, kc):
            acc = acc + _dot_tn(a_ref[k0:k0 + kc, :].astype(BF16), b_ref[k0:k0 + kc, :].astype(BF16))
        if len(o_ref.shape) == 3:
            o_ref[0] = acc[0:bm // 2]
            o_ref[1] = acc[bm // 2:bm]
        else:
            o_ref[...] = acc

    out_rows = m // 2 if col_shards else m // 8
    outs, routs = _pallas(
        body, [a, b], name=name, grid=grid,
        in_specs=[pl.BlockSpec((tokens, bm), lambda i, j: (0, i)), pl.BlockSpec((tokens, bn), lambda i, j: (0, j))],
        out_specs=[out_spec], out_shape=[jax.ShapeDtypeStruct((N_CHIPS, 2, out_rows, bn), F32)],
        vmem_mib=56, riders=riders)
    return outs[0], routs


def _wgrad_pair(a, b, name, *, col_shards, riders=(), to_chips=False):
    tokens, m = a.shape
    n = b.shape[1]
    kc = 512
    x0, y0, c0 = _mesh_pos()
    rot = 1 if to_chips else 0
    phases = [0, 0, 1, 0, 1, 0, 1, 1] if to_chips else [0, 0, 0, 0, 1, 1, 1, 1]
    tiles = [0, 1, 0, 2, 1, 3, 2, 3] if to_chips else [0, 1, 2, 3, 0, 1, 2, 3]
    out_tiles = [0, 0, 0, 0, 1, 1, 2, 3] if to_chips else [0, 0, 0, 0, 0, 1, 2, 3]
    steps = len(phases)
    P0, T0, O0 = 2, 2 + steps, 2 + 2 * steps

    def slab(t, pre):
        return (t + rot * (1 + pre[1])) % N_CHIPS

    def half(s, pre):
        return (pre[P0 + s] + 1 + pre[0]) % 2

    if col_shards:
        bm, bn = m // 2, n // N_CHIPS
        a_spec = pl.BlockSpec((tokens, bm), lambda s, pre: (0, half(s, pre)))
        b_spec = pl.BlockSpec((tokens, bn), lambda s, pre: (0, slab(pre[T0 + s], pre)))
    else:
        bm, bn = m // 8, n
        a_spec = pl.BlockSpec((tokens, bm), lambda s, pre: (0, 2 * slab(pre[T0 + s], pre) + half(s, pre)))
        b_spec = pl.BlockSpec((tokens, bn), lambda s, pre: (0, 0))

    def body(pre_ref, a_ref, b_ref, o_ref, *rest):
        if to_chips:
            land, give, got, mine, send_sems, recv_sems, chip_send, chip_recv = rest
        else:
            give, got, send_sems, recv_sems = rest
        step = pl.program_id(0)
        ph, q = pre_ref[P0 + step], pre_ref[T0 + step]
        acc = jnp.zeros((bm, bn), F32)
        for k0 in range(0, tokens, kc):
            acc = acc + _dot_tn(a_ref[k0:k0 + kc, :].astype(BF16), b_ref[k0:k0 + kc, :].astype(BF16))
        x, y, cc = _mesh_pos()

        def tile(t):
            return _remote(give.at[t], got.at[t], send_sems.at[t], recv_sems.at[t], (x, y, 1 - cc))

        def to_chip(s):
            t = (s + 1 + 2 * x + y) % N_CHIPS
            tx, ty = t // 2, t % 2
            k = 2 * (ty ^ y) + (tx ^ x) - 1
            return _remote(mine.at[s], land.at[k], chip_send.at[k], chip_recv.at[k], (tx, ty, cc))

        @pl.when(ph == 0)
        def _():
            give[q] = acc
            tile(q).start()

        @pl.when(ph == 1)
        def _():
            tile(q).wait_recv()
            total = (acc + got[q]).astype(BF16)
            o_ref[...] = total
            if to_chips:
                for s in range(N_CHIPS - 1):
                    @pl.when(q == s)
                    def _(s=s):
                        mine[s] = total
                        to_chip(s).start()

        @pl.when(step == steps - 1)
        def _():
            for t in range(N_CHIPS):
                tile(t).wait_send()
            if to_chips:
                for s in range(N_CHIPS - 1):
                    to_chip(s).wait()

    prefetch = jnp.concatenate([jnp.stack([c0, 2 * x0 + y0]).astype(jnp.int32),
                                jnp.asarray(phases + tiles + out_tiles, jnp.int32)])
    out_specs = [pl.BlockSpec((None, bm, bn), lambda s, pre: (slab(pre[O0 + s], pre), 0, 0))]
    out_shape = [jax.ShapeDtypeStruct((N_CHIPS, bm, bn), BF16)]
    scratch = [pltpu.VMEM((N_CHIPS, bm, bn), F32), pltpu.VMEM((N_CHIPS, bm, bn), F32)]
    sems = [pltpu.SemaphoreType.DMA((N_CHIPS,)), pltpu.SemaphoreType.DMA((N_CHIPS,))]
    if to_chips:
        out_specs.append(ANY)
        out_shape.append(jax.ShapeDtypeStruct((N_CHIPS - 1, bm, bn), BF16))
        scratch.append(pltpu.VMEM((N_CHIPS - 1, bm, bn), BF16))
        sems += [pltpu.SemaphoreType.DMA((N_CHIPS - 1,)), pltpu.SemaphoreType.DMA((N_CHIPS - 1,))]
    outs, routs = _pallas(
        body, [a, b], name=name, grid=(steps,), in_specs=[a_spec, b_spec], out_specs=out_specs, out_shape=out_shape,
        scratch_shapes=scratch + sems, vmem_mib=56, riders=riders, prefetch=prefetch)
    return (outs if to_chips else outs[0]), routs


class _GradReduce:
    def __init__(self, name, grad=None, chip_sum=None):
        self.name, self.grad, self.chip_sum = name, grad, chip_sum
        self.full = None

    def pair_swap(self):
        return _PairSwap([self.grad])

    def took_pair(self, outs):
        self.chip_sum = _in_hbm(_add_pair(self.grad, outs[0], f"pair_sum_{self.name}"))

    def chip_swap(self):
        return _ChipSwap([self.chip_sum])

    def took_chips(self, outs):
        self.full = _in_hbm(_add_chips(self.chip_sum, outs[0], f"chip_sum_{self.name}"))

    def pair_share(self):
        return _PairShare([self.full])

    def took_share(self, outs):
        self.full = outs[0]

    def reduced(self):
        return jnp.reshape(self.full, (2 * self.full.shape[1], self.full.shape[2]))


def _forward_backward(x2, tgt2, gathered, staged, conv_a_w, conv_b_w, od_norm, od_bias, od_lng, od_lnb,
                      ev_norm_g, ev_conv_a_b, ev_ln_a_g, ev_ln_a_b, od_w_s, od_b_s, mlp_norm_g, final_norm_g,
                      *, tm, seq, distributed=True):
    d = x2.shape[1]
    w = dict(gathered)
    b_s_rows = jnp.broadcast_to(od_b_s[0][:, :, None], (C_GROUPS, CHUNK, CHUNK))

    def ride(*names):
        return [_Gather([staged[nm] for nm in names])] if distributed and staged else []

    def land(routs, *names):
        if distributed and staged:
            for nm, buf in zip(names, routs[0]):
                w[nm] = buf

    (h1, n0, z, a2, cv, mix), routs = _fwd_even(
        x2, ev_norm_g, w["ev_in"], conv_a_w, ev_conv_a_b, ev_ln_a_g, ev_ln_a_b, conv_b_w, w["ev_out"],
        tm=tm, seq=seq, riders=ride("w1_0", "w2_0"))
    land(routs, "w1_0", "w2_0")
    (h2, n1, p0, q0), routs = _fwd_mlp(h1, mlp_norm_g[0:1], w["w1_0"], w["w2_0"], 0, tm=tm,
                                       riders=ride("od_in", "od_out", "w1_1"))
    land(routs, "od_in", "od_out", "w1_1")
    (h3, n2, s, cdf, sv, y), routs = _fwd_odd(h2, od_norm, w["od_in"], od_bias, od_lng, od_lnb, od_w_s[0], b_s_rows,
                                         w["od_out"], tm=tm, riders=ride("w2_1"))
    land(routs, "w2_1")
    (n3, p1, q1, loss_part, dh4, dh4b, d_final_g), _ = _fwd_mlp(
        h3, mlp_norm_g[1:2], w["w1_1"], w["w2_1"], 1, tm=tm,
        head=(jnp.reshape(final_norm_g, (1, d)), tgt2))

    red = {}

    def swap(*names):
        return [red[nm].pair_swap() for nm in names] if distributed else []

    def chips(*names):
        return [red[nm].chip_swap() for nm in names] if distributed else []

    def share(*names):
        return [red[nm].pair_share() for nm in names] if distributed else []

    def took(routs, *steps):
        if distributed:
            for (nm, what), outs in zip(steps, routs):
                getattr(red[nm], what)(outs)

    def big(lhs, rhs, name, col_shards, riders=(), to_chips=False):
        if distributed and to_chips:
            (chip_sum, from_chips), routs = _wgrad_pair(lhs, rhs, f"wgrad_{name}", col_shards=col_shards, riders=riders,
                                                        to_chips=True)
            red[name] = _GradReduce(name, chip_sum=_in_hbm(chip_sum))
            red[name].took_chips([_in_hbm(from_chips)])
        elif distributed:
            chip_sum, routs = _wgrad_pair(lhs, rhs, f"wgrad_{name}", col_shards=col_shards, riders=riders)
            red[name] = _GradReduce(name, chip_sum=_in_hbm(chip_sum))
        else:
            g, routs = _wgrad(lhs, rhs, f"wgrad_{name}", col_shards=col_shards)
            red[name] = _GradReduce(name, grad=g)
        return routs

    big(q1, dh4b, "w2_1", False)
    (dh3, dh3b, dp1, d_mlp_g1), routs = _bwd_mlp(dh4, h3, mlp_norm_g[1:2], p1, w["w1_1"], w["w2_1"], 1, tm=tm,
                                           riders=chips("w2_1"))
    took(routs, ("w2_1", "took_chips"))
    big(n3, dp1, "w1_1", True)
    g, routs = _wgrad(y, dh3b, "wgrad_od_out", col_shards=False, riders=share("w2_1"))
    red["od_out"] = _GradReduce("od_out", grad=g)
    took(routs, ("w2_1", "took_share"))
    (dh2, dh2b, ds, d_od_norm, d_od_bin, d_od_lng, d_od_lnb, d_ws, d_bs), routs = _bwd_odd(
        dh3, h2, od_norm, s, cdf, sv, w["od_in"], od_lng, od_lnb, od_w_s[0], w["od_out"], tm=tm,
        riders=chips("w1_1") + swap("od_out"))
    took(routs, ("w1_1", "took_chips"), ("od_out", "took_pair"))
    routs = big(n2, ds, "od_in", True, riders=share("w1_1"))
    took(routs, ("w1_1", "took_share"))
    half_groups = C_GROUPS // 2
    early = {"loss": loss_part, "od_w_s_lo": d_ws[:half_groups], "od_b_s": d_bs, "mlp_norm_g1": d_mlp_g1, "final_norm_g": d_final_g,
             "od_norm_g": d_od_norm, "od_b_in": d_od_bin, "od_ln_v_g": d_od_lng, "od_ln_v_b": d_od_lnb}
    share_early = [_ShareAll(list(early.values()))] if distributed else []
    routs = big(q0, dh2b, "w2_0", False, riders=share_early)
    landed_early = routs[0] if distributed else []
    (dh1, dh1b, dp0, d_mlp_g0), routs = _bwd_mlp(dh2, h1, mlp_norm_g[0:1], p0, w["w1_0"], w["w2_0"], 0, tm=tm,
                                           riders=chips("od_out") + chips("od_in") + chips("w2_0"))
    took(routs, ("od_out", "took_chips"), ("od_in", "took_chips"), ("w2_0", "took_chips"))
    middle = {"od_w_s_hi": d_ws[half_groups:]}
    share_middle = [_ShareAll(list(middle.values()))] if distributed else []
    g, _ = _wgrad(mix, dh1b, "wgrad_ev_out", col_shards=False)
    red["ev_out"] = _GradReduce("ev_out", grad=g)
    routs = big(n1, dp0, "w1_0", True,
                riders=share("od_out") + share("od_in") + share("w2_0") + share_middle + swap("ev_out"))
    took(routs, ("od_out", "took_share"), ("od_in", "took_share"), ("w2_0", "took_share"))
    landed_middle = routs[3] if distributed else []
    if distributed:
        red["ev_out"].took_pair(routs[4])

    (dx, dz, d_ev_norm, d_caw, d_cab, d_ev_lng, d_ev_lnb, d_cbw), routs = _bwd_even(
        dh1, x2, ev_norm_g, z, a2, cv, w["ev_in"], conv_a_w, ev_ln_a_g, ev_ln_a_b, conv_b_w, w["ev_out"],
        tm=tm, seq=seq, riders=chips("w1_0") + chips("ev_out"))
    took(routs, ("w1_0", "took_chips"), ("ev_out", "took_chips"))
    late = {"mlp_norm_g0": d_mlp_g0, "ev_norm_g": d_ev_norm, "ev_conv_a_b": d_cab, "ev_ln_a_g": d_ev_lng,
            "ev_ln_a_b": d_ev_lnb, "ev_conv_a_w": d_caw, "ev_conv_b_w": d_cbw}
    share_late = [_ShareAll(list(late.values()))] if distributed else []
    routs2 = big(n0, dz, "ev_in", True, riders=share("ev_out") + share("w1_0") + share_late, to_chips=True)
    took(routs2, ("ev_out", "took_share"), ("w1_0", "took_share"))
    own = {**early, **middle, **late}
    landed = dict(zip(own.keys(), landed_early + landed_middle + routs2[2])) if distributed else None
    return dx, red, own, landed


def _rows128(a):
    rows = jnp.reshape(a, (-1, LANES))
    pad = (-rows.shape[0]) % SUBLANES
    return jnp.pad(rows, ((0, pad), (0, 0))) if pad else rows


def _pack(arrays):
    return jnp.concatenate([_rows128(a) for a in arrays], axis=0)


def _unpack(buf, shapes):
    out, r0 = [], 0
    for shp in shapes:
        size = 1
        for dim in shp:
            size *= dim
        nr = size // LANES
        out.append(jnp.reshape(buf[r0:r0 + nr], shp))
        r0 += nr + (-nr) % SUBLANES
    return out


def kernel(x, ev_norm_g, ev_w_in, ev_conv_a_w, ev_conv_a_b, ev_ln_a_g, ev_ln_a_b, ev_conv_b_w, ev_w_out, od_norm_g, od_w_in, od_b_in, od_ln_v_g, od_ln_v_b, od_w_s, od_b_s, od_w_out, mlp_norm_g, mlp_w1, mlp_w2, final_norm_g, loss_target, m_ev_norm_g, m_ev_w_in, m_ev_conv_a_w, m_ev_conv_a_b, m_ev_ln_a_g, m_ev_ln_a_b, m_ev_conv_b_w, m_ev_w_out, m_od_norm_g, m_od_w_in, m_od_b_in, m_od_ln_v_g, m_od_ln_v_b, m_od_w_s, m_od_b_s, m_od_w_out, m_mlp_norm_g, m_mlp_w1, m_mlp_w2, m_final_norm_g, v_ev_norm_g, v_ev_w_in, v_ev_conv_a_w, v_ev_conv_a_b, v_ev_ln_a_g, v_ev_ln_a_b, v_ev_conv_b_w, v_ev_w_out, v_od_norm_g, v_od_w_in, v_od_b_in, v_od_ln_v_g, v_od_ln_v_b, v_od_w_s, v_od_b_s, v_od_w_out, v_mlp_norm_g, v_mlp_w1, v_mlp_w2, v_final_norm_g):
    tm = TOKEN_TILE
    batch, seq, d = x.shape
    tokens = batch * seq
    x2 = jnp.reshape(x, (tokens, d))
    tgt2 = jnp.reshape(loss_target, (tokens, d))
    chip = 2 * lax.axis_index("x") + lax.axis_index("y")

    small_shapes = [(A_CONV_WIDTH, LANES), (B_CONV_WIDTH, LANES), (256,), (512,), (256,), (256,)]
    small_shard = _pack([ev_conv_a_w[0], ev_conv_b_w[0], od_norm_g[0], od_b_in[0], od_ln_v_g[0], od_ln_v_b[0]])
    small_shard = jnp.pad(small_shard, ((0, (-small_shard.shape[0]) % (2 * SUBLANES)), (0, 0)))
    first = [_place_shard(ev_w_in, 0, BF16, "place_ev_w_in"), _place_shard(ev_w_out, 0, BF16, "place_ev_w_out"),
             _place_shard(small_shard[None], 0, F32, "place_small")]
    staged = {
        "w1_0": _place_shard(mlp_w1, 0, BF16, "place_w1_0"), "w2_0": _place_shard(mlp_w2, 0, BF16, "place_w2_0"),
        "od_in": _place_shard(od_w_in, 0, BF16, "place_od_w_in"), "od_out": _place_shard(od_w_out, 0, BF16, "place_od_w_out"),
        "w1_1": _place_shard(mlp_w1, 1, BF16, "place_w1_1"), "w2_1": _place_shard(mlp_w2, 1, BF16, "place_w2_1"),
    }
    first = [_in_hbm(a) for a in first]
    staged = {nm: _in_hbm(a) for nm, a in staged.items()}
    g_ev_in, g_ev_out, g_small = _gather_beside(first, "gather_stage0", collective_id=1)
    gathered = {"ev_in": g_ev_in, "ev_out": g_ev_out}
    for stage, names in enumerate((("w1_0", "w2_0"), ("od_in", "od_out", "w1_1"), ("w2_1",))):
        done = _gather_beside([staged[nm] for nm in names], f"gather_stage{stage + 1}", collective_id=stage + 2)
        gathered.update(zip(names, done))
    small_all = jnp.reshape(_plain_copy(g_small, "small_weights_copy"), (N_CHIPS, -1, LANES))
    per_chip = [_unpack(small_all[q], small_shapes) for q in range(N_CHIPS)]
    conv_a_w = jnp.concatenate([pc[0] for pc in per_chip], axis=1)
    conv_b_w = jnp.concatenate([pc[1] for pc in per_chip], axis=1)
    od_norm = jnp.concatenate([pc[2] for pc in per_chip])[None, :]
    od_bias = jnp.concatenate([pc[3] for pc in per_chip])[None, :]
    od_lng = jnp.concatenate([pc[4] for pc in per_chip])[None, :]
    od_lnb = jnp.concatenate([pc[5] for pc in per_chip])[None, :]

    dx, red, own, landed = _forward_backward(
        x2, tgt2, gathered, {}, conv_a_w, conv_b_w, od_norm, od_bias, od_lng, od_lnb,
        ev_norm_g, ev_conv_a_b, ev_ln_a_g, ev_ln_a_b, od_w_s, od_b_s, mlp_norm_g, final_norm_g, tm=tm, seq=seq)

    routs = _exchange([red["ev_in"].pair_share()], "reduce_tail")
    red["ev_in"].took_share(routs[0])

    given = {"ev_norm_g": (ev_norm_g, m_ev_norm_g, v_ev_norm_g), "ev_conv_a_b": (ev_conv_a_b, m_ev_conv_a_b, v_ev_conv_a_b),
             "ev_ln_a_g": (ev_ln_a_g, m_ev_ln_a_g, v_ev_ln_a_g), "ev_ln_a_b": (ev_ln_a_b, m_ev_ln_a_b, v_ev_ln_a_b),
             "od_w_s": (od_w_s, m_od_w_s, v_od_w_s), "od_b_s": (od_b_s, m_od_b_s, v_od_b_s),
             "mlp_norm_g": (mlp_norm_g, m_mlp_norm_g, v_mlp_norm_g), "final_norm_g": (final_norm_g, m_final_norm_g, v_final_norm_g),
             "ev_conv_a_w": (ev_conv_a_w, m_ev_conv_a_w, v_ev_conv_a_w), "ev_conv_b_w": (ev_conv_b_w, m_ev_conv_b_w, v_ev_conv_b_w),
             "od_norm_g": (od_norm_g, m_od_norm_g, v_od_norm_g), "od_b_in": (od_b_in, m_od_b_in, v_od_b_in),
             "od_ln_v_g": (od_ln_v_g, m_od_ln_v_g, v_od_ln_v_g), "od_ln_v_b": (od_ln_v_b, m_od_ln_v_b, v_od_ln_v_b)}
    shaped = {nm: tuple(jnp.reshape(a, shape) for a in given[nm]) for nm, shape, _, _ in SMALL_WEIGHTS}
    loss11, small_upd = _small_update(own, landed, shaped)
    loss = loss11[0, 0]
    upd = {nm: [jnp.reshape(o, given[nm][0].shape) for o in outs] for nm, outs in small_upd.items()}

    def big_update(wt, m, v, names, call):
        grads = [red[nm].reduced() for nm in names]
        shp3 = (len(grads),) + grads[0].shape
        outs, _ = _adamw(jnp.reshape(wt, shp3), jnp.reshape(m, shp3), jnp.reshape(v, shp3), grads, call)
        return [jnp.reshape(o, wt.shape) for o in outs], None

    upd["mlp_w2"], _ = big_update(mlp_w2, m_mlp_w2, v_mlp_w2, ["w2_0", "w2_1"], "adamw_mlp_w2")
    upd["mlp_w1"], _ = big_update(mlp_w1, m_mlp_w1, v_mlp_w1, ["w1_0", "w1_1"], "adamw_mlp_w1")
    upd["ev_w_in"], _ = big_update(ev_w_in, m_ev_w_in, v_ev_w_in, ["ev_in"], "adamw_ev_w_in")
    upd["ev_w_out"], _ = big_update(ev_w_out, m_ev_w_out, v_ev_w_out, ["ev_out"], "adamw_ev_w_out")
    upd["od_w_in"], _ = big_update(od_w_in, m_od_w_in, v_od_w_in, ["od_in"], "adamw_od_w_in")
    upd["od_w_out"], _ = big_update(od_w_out, m_od_w_out, v_od_w_out, ["od_out"], "adamw_od_w_out")

    order = ["ev_norm_g", "ev_w_in", "ev_conv_a_w", "ev_conv_a_b", "ev_ln_a_g", "ev_ln_a_b", "ev_conv_b_w", "ev_w_out",
             "od_norm_g", "od_w_in", "od_b_in", "od_ln_v_g", "od_ln_v_b", "od_w_s", "od_b_s", "od_w_out", "mlp_norm_g",
             "mlp_w1", "mlp_w2", "final_norm_g"]
    grad_x = jnp.reshape(dx, x.shape)
    return (loss, grad_x, *[upd[nm][0] for nm in order], *[upd[nm][1] for nm in order],
            *[upd[nm][2] for nm in order], *[upd[nm][3] for nm in order])
```

```python
import functools

import jax
import jax.numpy as jnp
from jax import lax
from jax.experimental import pallas as pl
from jax.experimental.pallas import tpu as pltpu
from jax.experimental.pallas import tpu_sc as plsc

F32 = jnp.float32
BF16 = jnp.bfloat16

D_MODEL = 1024
A_DIM = 512
B_DIM = 512
IN_EVEN = 2 * A_DIM + 3 * B_DIM
A_CONV_WIDTH = 31
B_CONV_WIDTH = 3
CHUNK = 128
C_GROUPS = 8
C_DIM = 1024
D_FF = 4096
RMS_EPS = 1e-6
LN_EPS = 1e-5
ADAM_LR = 0.001
ADAM_B1 = 0.9
ADAM_B2 = 0.999
ADAM_EPS = 1e-08
ADAM_WD = 0.01
ADAM_STEP = 10

N_CHIPS = 4
N_DEV = 8
TOKEN_TILE = 512
A_HALO = 32
B_HALO = 8
CONV_ROWS = 16
DW_TAPS = 4
ELEM_ROWS = 16
PAIR = 2 * CHUNK
LANES = 128
SUBLANES = 8
MXU_ROWS = 256
MIB = 1024 * 1024
MESH = pl.DeviceIdType.MESH
ANY = pl.BlockSpec(memory_space=pl.ANY)


def _dot(a, b):
    return lax.dot_general(a, b, (((1,), (0,)), ((), ())), preferred_element_type=F32)


def _dot_nt(a, b):
    return lax.dot_general(a, b, (((1,), (1,)), ((), ())), preferred_element_type=F32)


def _dot_tn(a, b):
    return lax.dot_general(a, b, (((0,), (0,)), ((), ())), preferred_element_type=F32)


def _params(vmem_mib, n_axes=1):
    return pltpu.CompilerParams(dimension_semantics=("arbitrary",) * n_axes, vmem_limit_bytes=vmem_mib * MIB)


def _row_spec(tm, cols, rev_nt=None):
    if rev_nt is None:
        return pl.BlockSpec((tm, cols), lambda i: (i, 0))
    return pl.BlockSpec((tm, cols), lambda i: (rev_nt - 1 - i, 0))


def _full_spec(shape):
    nd = len(shape)
    return pl.BlockSpec(shape, lambda i: (0,) * nd)


def _block_rows(rows, cap=512):
    best = SUBLANES
    for br in range(SUBLANES, min(rows, cap) + 1, SUBLANES):
        if rows % br == 0:
            best = br
    return best


FIRST_SWAP_ID = 5
N_LOADS = 2 * 2 * N_CHIPS


def _load_weights(loads, sems):
    @pl.when(pl.program_id(0) == 0)
    def _():
        copies = []
        for src, dst, rows_of_one in loads:
            r = src.shape[2]
            for q in range(N_CHIPS):
                for h in range(2):
                    part = dst.at[pl.ds((2 * q + h) * r, r)] if rows_of_one else dst.at[q, pl.ds(h * r, r)]
                    copies.append(pltpu.make_async_copy(src.at[q, h], part, sems.at[len(copies)]))
        for cp in copies:
            cp.start()
        for cp in copies:
            cp.wait()


def _rms_fwd(x, g):
    rstd = lax.rsqrt(jnp.mean(x * x, axis=-1, keepdims=True) + RMS_EPS)
    return x * rstd * g, rstd


def _rms_bwd(dn, x, rstd, g):
    a = dn * g
    xh = x * rstd
    dx = rstd * (a - xh * jnp.mean(a * xh, axis=-1, keepdims=True))
    dg = jnp.sum(dn * xh, axis=0, keepdims=True)
    return dx, dg


def _ln_stats(v):
    mu = jnp.mean(v, axis=-1, keepdims=True)
    xc = v - mu
    rs = lax.rsqrt(jnp.mean(xc * xc, axis=-1, keepdims=True) + LN_EPS)
    return xc * rs, rs


def _ln_bwd(dy, xhat, rs, g):
    dxh = dy * g
    dv = rs * (dxh - jnp.mean(dxh, axis=-1, keepdims=True) - xhat * jnp.mean(dxh * xhat, axis=-1, keepdims=True))
    return dv, jnp.sum(dy * xhat, axis=0, keepdims=True), jnp.sum(dy, axis=0, keepdims=True)


def _gelu_cdf(s):
    return 0.5 * (1.0 + lax.erf(s * 0.7071067811865476))


def _mesh_pos():
    return lax.axis_index("x"), lax.axis_index("y"), lax.axis_index("c")


def _other_chips(x, y):
    return [(1 - x, y), (x, 1 - y), (1 - x, 1 - y)]


def _remote(src, dst, send_sem, recv_sem, to):
    return pltpu.make_async_remote_copy(src_ref=src, dst_ref=dst, send_sem=send_sem, recv_sem=recv_sem,
                                        device_id=to, device_id_type=MESH)


def _like(arrays):
    return [jax.ShapeDtypeStruct(a.shape, a.dtype) for a in arrays]


class _Gather:
    def __init__(self, bufs):
        self.ins = list(bufs)
        self.out_shapes = _like(bufs)
        self.aliases = {t: t for t in range(len(bufs))}
        self.n_sems = 6 * len(bufs)

    def _ici(self, ins, outs, send, recv, t, k, chip, mine, c):
        return _remote(ins[t].at[mine, c], outs[t].at[mine, c], send.at[6 * t + k], recv.at[6 * t + k], (*chip, c))

    def start(self, ins, outs, send, recv):
        x, y, c = _mesh_pos()
        for t in range(len(ins)):
            for k, chip in enumerate(_other_chips(x, y)):
                self._ici(ins, outs, send, recv, t, k, chip, 2 * x + y, c).start()

    def _pass_on(self, outs, send, recv, t, k, chip, c, to):
        blk = outs[t].at[2 * chip[0] + chip[1], c]
        return _remote(blk, blk, send.at[6 * t + 3 + k], recv.at[6 * t + 3 + k], to)

    def near_end(self, ins, outs, send, recv):
        x, y, c = _mesh_pos()
        for t in range(len(ins)):
            for k, chip in enumerate(_other_chips(x, y)):
                blk = outs[t].at[2 * chip[0] + chip[1], c]
                _remote(blk, blk, send.at[6 * t + k], recv.at[6 * t + k], (x, y, c)).wait_recv()
                self._pass_on(outs, send, recv, t, k, chip, c, (x, y, 1 - c)).start()

    def finish(self, ins, outs, send, recv):
        x, y, c = _mesh_pos()
        chips = _other_chips(x, y)
        for t in range(len(ins)):
            for k, chip in enumerate(chips):
                self._pass_on(outs, send, recv, t, k, chip, 1 - c, (x, y, c)).wait_recv()
        for t in range(len(ins)):
            for k, chip in enumerate(chips):
                self._ici(ins, outs, send, recv, t, k, chip, 2 * x + y, c).wait_send()
                self._pass_on(outs, send, recv, t, k, chip, c, (x, y, 1 - c)).wait_send()


class _PairSwap:
    def __init__(self, grads):
        self.ins = list(grads)
        self.out_shapes = [jax.ShapeDtypeStruct((g.shape[0],) + g.shape[2:], g.dtype) for g in grads]
        self.aliases = {}
        self.n_sems = len(grads)

    def _copies(self, ins, outs, send, recv):
        x, y, c = _mesh_pos()
        return [_remote(ins[t].at[:, 1 - c], outs[t], send.at[t], recv.at[t], (x, y, 1 - c)) for t in range(len(ins))]

    def start(self, ins, outs, send, recv):
        for cp in self._copies(ins, outs, send, recv):
            cp.start()

    def finish(self, ins, outs, send, recv):
        for cp in self._copies(ins, outs, send, recv):
            cp.wait()


class _ChipSwap:
    def __init__(self, parts):
        self.ins = list(parts)
        self.out_shapes = [jax.ShapeDtypeStruct((3,) + p.shape[1:], p.dtype) for p in parts]
        self.aliases = {}
        self.n_sems = 3 * len(parts)

    def _copies(self, ins, outs, send, recv):
        x, y, c = _mesh_pos()
        return [_remote(ins[t].at[2 * chip[0] + chip[1]], outs[t].at[k], send.at[3 * t + k], recv.at[3 * t + k], (*chip, c))
                for t in range(len(ins)) for k, chip in enumerate(_other_chips(x, y))]

    def start(self, ins, outs, send, recv):
        for cp in self._copies(ins, outs, send, recv):
            cp.start()

    def finish(self, ins, outs, send, recv):
        for cp in self._copies(ins, outs, send, recv):
            cp.wait()


class _PairShare:
    def __init__(self, fulls):
        self.ins = list(fulls)
        self.out_shapes = _like(fulls)
        self.aliases = {t: t for t in range(len(fulls))}
        self.n_sems = len(fulls)

    def _copies(self, ins, outs, send, recv):
        x, y, c = _mesh_pos()
        return [_remote(ins[t].at[c], outs[t].at[c], send.at[t], recv.at[t], (x, y, 1 - c)) for t in range(len(ins))]

    def start(self, ins, outs, send, recv):
        for cp in self._copies(ins, outs, send, recv):
            cp.start()

    def finish(self, ins, outs, send, recv):
        for cp in self._copies(ins, outs, send, recv):
            cp.wait()


class _ShareAll:
    def __init__(self, arrays):
        self.ins = list(arrays)
        self.out_shapes = [jax.ShapeDtypeStruct((N_DEV,) + a.shape, a.dtype) for a in arrays]
        self.aliases = {}
        self.n_sems = (N_DEV - 1) * len(arrays)

    def _peers(self):
        x, y, c = _mesh_pos()
        flips = [((r >> 2) & 1, (r >> 1) & 1, r & 1) for r in range(1, N_DEV)]
        return (x, y, c), [(x ^ fx, y ^ fy, c ^ fc) for fx, fy, fc in flips]

    def _sends(self, ins, outs, send, recv):
        (x, y, c), peers = self._peers()
        mine = 4 * x + 2 * y + c
        return [_remote(ins[a], outs[a].at[mine], send.at[7 * a + r], recv.at[7 * a + r], peer)
                for a in range(len(ins)) for r, peer in enumerate(peers)]

    def start(self, ins, outs, send, recv):
        for cp in self._sends(ins, outs, send, recv):
            cp.start()

    def finish(self, ins, outs, send, recv):
        (x, y, c), peers = self._peers()
        for a in range(len(ins)):
            for r, (px, py, pc) in enumerate(peers):
                blk = outs[a].at[4 * px + 2 * py + pc]
                _remote(blk, blk, send.at[7 * a + r], recv.at[7 * a + r], (x, y, c)).wait_recv()
        for cp in self._sends(ins, outs, send, recv):
            cp.wait_send()


def _gather_beside(bufs, name, collective_id):
    n = len(bufs)
    refs = [jax.new_ref(b, memory_space=pltpu.MemorySpace.HBM) for b in bufs]
    gather = _Gather(bufs)

    @pl.kernel(mesh=plsc.ScalarSubcoreMesh(axis_name="sequencer", num_cores=1), name=name,
               scratch_types=(pltpu.SemaphoreType.DMA((6 * n,)), pltpu.SemaphoreType.DMA((6 * n,))),
               compiler_params=pltpu.CompilerParams(collective_id=collective_id))
    def launch(send, recv):
        x, y, c = _mesh_pos()
        barrier = pltpu.get_barrier_semaphore()
        peers = [(*chip, c) for chip in _other_chips(x, y)] + [(x, y, 1 - c)]
        for peer in peers:
            pl.semaphore_signal(barrier, inc=1, device_id=peer, device_id_type=MESH)
        pl.semaphore_wait(barrier, len(peers))
        gather.start(refs, refs, send, recv)
        gather.near_end(refs, refs, send, recv)
        gather.finish(refs, refs, send, recv)

    launch()
    return [r[...] for r in refs]


def _chip_swap_beside(parts, name, collective_id):
    src = jax.new_ref(parts, memory_space=pltpu.MemorySpace.HBM)
    dst = jax.empty_ref(jax.ShapeDtypeStruct((N_CHIPS - 1,) + parts.shape[1:], parts.dtype),
                        memory_space=pltpu.MemorySpace.HBM)
    swap = _ChipSwap([parts])

    @pl.kernel(mesh=plsc.ScalarSubcoreMesh(axis_name="sequencer", num_cores=1), name=name,
               scratch_types=(pltpu.SemaphoreType.DMA((N_CHIPS - 1,)), pltpu.SemaphoreType.DMA((N_CHIPS - 1,))),
               compiler_params=pltpu.CompilerParams(collective_id=collective_id))
    def launch(send, recv):
        x, y, c = _mesh_pos()
        barrier = pltpu.get_barrier_semaphore()
        peers = [(*chip, c) for chip in _other_chips(x, y)]
        for peer in peers:
            pl.semaphore_signal(barrier, inc=1, device_id=peer, device_id_type=MESH)
        pl.semaphore_wait(barrier, len(peers))
        swap.start([src], [dst], send, recv)
        swap.finish([src], [dst], send, recv)

    launch()
    return dst[...]


def _pallas(body, operands, *, name, grid, in_specs, out_specs, out_shape, scratch_shapes=(), vmem_mib=32, riders=(),
            prefetch=None):
    in_specs, out_specs, out_shape, scratch_shapes = list(in_specs), list(out_specs), list(out_shape), list(scratch_shapes)
    if not riders and prefetch is None:
        outs = pl.pallas_call(body, name=name, grid=grid, in_specs=in_specs, out_specs=out_specs, out_shape=out_shape,
                              scratch_shapes=scratch_shapes, compiler_params=_params(vmem_mib, len(grid)))(*operands)
        return list(outs), []
    n_in, n_out, n_scr = len(in_specs), len(out_specs), len(scratch_shapes)
    r_in = [len(r.ins) for r in riders]
    r_out = [len(r.out_shapes) for r in riders]
    steps = 1
    for g in grid:
        steps *= g

    n_pre = 0 if prefetch is None else 1

    def wrapped(*refs):
        refs = list(refs)
        pre, refs = refs[:n_pre], refs[n_pre:]
        ins, refs = refs[:n_in], refs[n_in:]
        rins = []
        for k in r_in:
            rins.append(refs[:k])
            refs = refs[k:]
        outs, refs = refs[:n_out], refs[n_out:]
        routs = []
        for k in r_out:
            routs.append(refs[:k])
            refs = refs[k:]
        scr, sems = refs[:n_scr], refs[n_scr:]
        step = 0
        for ax, g in enumerate(grid):
            step = step * g + pl.program_id(ax)

        def each(what):
            for j, r in enumerate(riders):
                if hasattr(r, what):
                    getattr(r, what)(rins[j], routs[j], sems[2 * j], sems[2 * j + 1])

        if grid:
            pl.when(step == 0)(lambda: each("start"))
        else:
            each("start")
        body(*pre, *ins, *outs, *scr)
        if grid:
            @pl.when(step == steps - 1)
            def _():
                each("near_end")
                each("finish")
        else:
            each("near_end")
            each("finish")

    aliases, off_in, off_out = {}, n_pre + n_in, n_out
    for r, ki, ko in zip(riders, r_in, r_out):
        for i, o in r.aliases.items():
            aliases[off_in + i] = off_out + o
        off_in, off_out = off_in + ki, off_out + ko
    sems = []
    for r in riders:
        sems += [pltpu.SemaphoreType.DMA((r.n_sems,)), pltpu.SemaphoreType.DMA((r.n_sems,))]
    layout = dict(grid=grid, in_specs=in_specs + [ANY] * sum(r_in), out_specs=out_specs + [ANY] * sum(r_out),
                  scratch_shapes=scratch_shapes + sems)
    if prefetch is not None:
        layout = dict(grid_spec=pltpu.PrefetchScalarGridSpec(num_scalar_prefetch=1, **layout))
    res = pl.pallas_call(
        wrapped, name=name, **layout,
        out_shape=out_shape + [s for r in riders for s in r.out_shapes], input_output_aliases=aliases,
        compiler_params=pltpu.CompilerParams(dimension_semantics=("arbitrary",) * len(grid),
                                             vmem_limit_bytes=vmem_mib * MIB, has_side_effects=True),
    )(*([] if prefetch is None else [prefetch]), *operands, *[a for r in riders for a in r.ins])
    res = list(res)
    outs, res = res[:n_out], res[n_out:]
    routs = []
    for k in r_out:
        routs.append(res[:k])
        res = res[k:]
    return outs, routs


def _exchange(riders, name):
    return _pallas(lambda: None, [], name=name, grid=(), in_specs=[], out_specs=[], out_shape=[], riders=riders)[1]


def _in_hbm(a):
    return pltpu.with_memory_space_constraint(a, pltpu.HBM)


def _place_shard(w, layer, dtype, name):
    _, rows, cols = w.shape
    half = rows // 2
    br = _block_rows(half)
    nb = half // br
    mine = 2 * lax.axis_index("x") + lax.axis_index("y")

    def body(q_ref, w_ref, o_ref):
        o_ref[...] = w_ref[...].astype(dtype)

    return pl.pallas_call(
        body, name=name,
        grid_spec=pltpu.PrefetchScalarGridSpec(
            num_scalar_prefetch=1, grid=(2, nb),
            in_specs=[pl.BlockSpec((None, br, cols), lambda h, i, q: (layer, h * nb + i, 0))],
            out_specs=pl.BlockSpec((None, None, br, cols), lambda h, i, q: (q[0], h, i, 0))),
        out_shape=pltpu.HBM((N_CHIPS, 2, half, cols), dtype),
        compiler_params=_params(16, 2),
    )(jnp.reshape(mine, (1,)).astype(jnp.int32), w)


def _plain_copy(a, name):
    def body(a_ref, o_ref):
        o_ref[...] = a_ref[...]

    vmem = pl.BlockSpec(memory_space=pltpu.VMEM)
    return pl.pallas_call(body, name=name, in_specs=[vmem], out_specs=vmem,
                          out_shape=jax.ShapeDtypeStruct(a.shape, a.dtype))(a)


def _add_pair(g, recv, name):
    _, _, r, cdim = g.shape
    br = _block_rows(r, 256)
    c = lax.axis_index("c")

    def body(c_ref, g_ref, r_ref, o_ref):
        o_ref[...] = (g_ref[...] + r_ref[...]).astype(BF16)

    return pl.pallas_call(
        body, name=name,
        grid_spec=pltpu.PrefetchScalarGridSpec(
            num_scalar_prefetch=1, grid=(N_CHIPS, r // br),
            in_specs=[pl.BlockSpec((None, None, br, cdim), lambda q, i, c_ref: (q, c_ref[0], i, 0)),
                      pl.BlockSpec((None, br, cdim), lambda q, i, c_ref: (q, i, 0))],
            out_specs=pl.BlockSpec((None, br, cdim), lambda q, i, c_ref: (q, i, 0))),
        out_shape=pltpu.HBM((N_CHIPS, r, cdim), BF16),
        compiler_params=_params(16, 2),
    )(jnp.reshape(c, (1,)).astype(jnp.int32), _in_hbm(g), _in_hbm(recv))


def _add_chips(own, recv, name):
    _, r, cdim = own.shape
    br = _block_rows(r, 256)
    x, y, c = _mesh_pos()

    def body(pos_ref, own_ref, r_ref, o_ref):
        acc = own_ref[...].astype(F32)
        for k in range(3):
            acc = acc + r_ref[k].astype(F32)
        o_ref[...] = acc

    return pl.pallas_call(
        body, name=name,
        grid_spec=pltpu.PrefetchScalarGridSpec(
            num_scalar_prefetch=1, grid=(r // br,),
            in_specs=[pl.BlockSpec((None, br, cdim), lambda i, pos: (pos[0], i, 0)),
                      pl.BlockSpec((3, br, cdim), lambda i, pos: (0, i, 0))],
            out_specs=pl.BlockSpec((None, br, cdim), lambda i, pos: (pos[1], i, 0))),
        out_shape=pltpu.HBM((2, r, cdim), F32),
        compiler_params=_params(16, 1),
    )(jnp.stack([2 * x + y, c]).astype(jnp.int32), _in_hbm(own), recv)


def _adam_math(w, m, v, g):
    c1 = 1.0 / (1.0 - ADAM_B1 ** ADAM_STEP)
    c2 = 1.0 / (1.0 - ADAM_B2 ** ADAM_STEP)
    m_new = ADAM_B1 * m + (1.0 - ADAM_B1) * g
    v_new = ADAM_B2 * v + (1.0 - ADAM_B2) * (g * g)
    return -ADAM_LR * ((m_new * c1) / (jnp.sqrt(v_new * c2) + ADAM_EPS) + ADAM_WD * w), m_new, v_new


SMALL_WEIGHTS = [
    ("ev_norm_g", (1, D_MODEL), ["ev_norm_g"], None), ("ev_conv_a_b", (1, A_DIM), ["ev_conv_a_b"], None),
    ("ev_ln_a_g", (1, A_DIM), ["ev_ln_a_g"], None), ("ev_ln_a_b", (1, A_DIM), ["ev_ln_a_b"], None),
    ("od_w_s", (C_GROUPS, CHUNK, CHUNK), ["od_w_s_lo", "od_w_s_hi"], None), ("od_b_s", (C_GROUPS, CHUNK), ["od_b_s"], None),
    ("mlp_norm_g", (2, D_MODEL), ["mlp_norm_g0", "mlp_norm_g1"], None), ("final_norm_g", (1, D_MODEL), ["final_norm_g"], None),
    ("ev_conv_a_w", (A_CONV_WIDTH, A_DIM // N_CHIPS), ["ev_conv_a_w"], A_DIM // N_CHIPS),
    ("ev_conv_b_w", (B_CONV_WIDTH, B_DIM // N_CHIPS), ["ev_conv_b_w"], B_DIM // N_CHIPS),
    ("od_norm_g", (1, D_MODEL // N_CHIPS), ["od_norm_g"], D_MODEL // N_CHIPS),
    ("od_b_in", (1, 2 * C_DIM // N_CHIPS), ["od_b_in"], 2 * C_DIM // N_CHIPS),
    ("od_ln_v_g", (1, C_DIM // N_CHIPS), ["od_ln_v_g"], C_DIM // N_CHIPS),
    ("od_ln_v_b", (1, C_DIM // N_CHIPS), ["od_ln_v_b"], C_DIM // N_CHIPS),
]


def _small_update(own, landed, weights):
    names = list(own.keys())
    n_g, n_w = len(names), len(SMALL_WEIGHTS)

    def body(*refs):
        refs = list(refs)
        own_refs = dict(zip(names, refs[:n_g]))
        land_refs = dict(zip(names, refs[n_g:2 * n_g]))
        wmv = [refs[2 * n_g + 3 * i:2 * n_g + 3 * i + 3] for i in range(n_w)]
        o0 = 2 * n_g + 3 * n_w
        loss_ref = refs[o0]
        outs = [refs[o0 + 1 + 4 * i:o0 + 5 + 4 * i] for i in range(n_w)]
        acc = dict(zip(names, refs[o0 + 1 + 4 * n_w:]))
        x, y, c = _mesh_pos()
        mine, chip = 4 * x + 2 * y + c, 2 * x + y

        for nm in names:
            for d in range(N_DEV):
                def add(term, nm=nm, d=d):
                    acc[nm][...] = term if d == 0 else acc[nm][...] + term
                pl.when(mine == d)(lambda nm=nm, add=add: add(own_refs[nm][...]))
                pl.when(mine != d)(lambda nm=nm, d=d, add=add: add(land_refs[nm][d]))
        loss_ref[...] = acc["loss"][...]

        def update(i, rows, g):
            w_ref, m_ref, v_ref = wmv[i]
            delta, m_new, v_new = _adam_math(w_ref[rows], m_ref[rows], v_ref[rows], g)
            for ref, val in zip(outs[i], (g, delta, m_new, v_new)):
                ref[rows] = val

        for i, (_, shape, grads, per_chip) in enumerate(SMALL_WEIGHTS):
            for row, gname in enumerate(grads):
                per_grad = shape[0] // len(grads)
                rows = slice(row * per_grad, (row + 1) * per_grad)
                if per_chip is None:
                    update(i, rows, acc[gname][...])
                else:
                    for q in range(N_CHIPS):
                        pl.when(chip == q)(lambda i=i, rows=rows, gname=gname, q=q, per_chip=per_chip:
                                           update(i, rows, acc[gname][:, q * per_chip:(q + 1) * per_chip]))

    operands = [own[nm] for nm in names] + [landed[nm] for nm in names]
    for nm, _, _, _ in SMALL_WEIGHTS:
        operands += list(weights[nm])
    out_shape = [jax.ShapeDtypeStruct((1, 1), F32)]
    for _, shape, _, _ in SMALL_WEIGHTS:
        out_shape += [jax.ShapeDtypeStruct(shape, F32)] * 4
    res = pl.pallas_call(
        body, name="small_update", grid=(1,),
        in_specs=[_full_spec(a.shape) for a in operands], out_specs=[_full_spec(s.shape) for s in out_shape],
        out_shape=out_shape, scratch_shapes=[pltpu.VMEM(own[nm].shape, F32) for nm in names],
        compiler_params=_params(32, 1),
    )(*[_in_hbm(a) for a in operands])
    return res[0], {nm: res[1 + 4 * i:5 + 4 * i] for i, (nm, _, _, _) in enumerate(SMALL_WEIGHTS)}


def _adamw(w, m, v, grads, name, riders=()):
    layers, r, cdim = w.shape
    br = _block_rows(r, 256 if cdim > LANES else 1024)

    def body(*refs):
        w_ref, m_ref, v_ref = refs[:3]
        g_refs = refs[3:3 + layers]
        go_ref, d_ref, mo_ref, vo_ref = refs[3 + layers:]
        layer = pl.program_id(0)
        for l in range(layers):
            @pl.when(layer == l)
            def _(l=l):
                g = g_refs[l][...]
                go_ref[...] = g
                d_ref[...], mo_ref[...], vo_ref[...] = _adam_math(w_ref[...], m_ref[...], v_ref[...], g)

    spec3 = pl.BlockSpec((None, br, cdim), lambda l, i: (l, i, 0))
    spec2 = pl.BlockSpec((br, cdim), lambda l, i: (i, 0))
    out = jax.ShapeDtypeStruct((layers, r, cdim), F32)
    return _pallas(body, [w, m, v, *[_in_hbm(g) for g in grads]], name=name, grid=(layers, r // br),
                   in_specs=[spec3, spec3, spec3] + [spec2] * layers, out_specs=[spec3] * 4, out_shape=[out] * 4,
                   vmem_mib=32, riders=riders)


def _fill_shifted(buf, rows):
    for b in range(1, SUBLANES):
        buf[b, 0:rows - SUBLANES, :] = buf[0, b:b + rows - SUBLANES, :]


def _window(buf, start, size):
    return buf[start % SUBLANES, start - start % SUBLANES:start - start % SUBLANES + size, :]


def _conv31(src, w_ref, r0, base, init):
    acc = init
    for k in range(A_CONV_WIDTH):
        acc = acc + w_ref[k:k + 1, :] * _window(src, base + k + r0, CONV_ROWS)
    return acc


def _fwd_even(x, norm_g, w_in, conv_a_w, conv_a_b, ln_g, ln_b, conv_b_w, w_out, *, tm, seq, riders=()):
    tokens = x.shape[0]
    nt, tps = tokens // tm, seq // tm

    def body(x_ref, g_ref, win_hbm, caw_ref, cab_ref, lng_ref, lnb_ref, cbw_ref, wout_hbm,
             h_ref, n_ref, z_ref, a2_ref, cv_ref, mix_ref, win_v, wout_v, pa, pb, sem):
        i = pl.program_id(0)

        _load_weights([(win_hbm, win_v, False), (wout_hbm, wout_v, True)], sem)

        xv = x_ref[...]
        nf, _ = _rms_fwd(xv, g_ref[...])
        n = nf.astype(BF16)
        n_ref[...] = n
        z = jnp.concatenate([_dot(n, win_v[j]) for j in range(N_CHIPS)], axis=1)
        z_ref[...] = z.astype(BF16)
        a_val, a_gate = z[:, 0:A_DIM], z[:, A_DIM:2 * A_DIM]
        b_gate, c_gate, b_val = z[:, 1024:1536], z[:, 1536:2048], z[:, 2048:2560]

        first = (i % tps) == 0

        @pl.when(first)
        def _():
            pa[0, 0:A_HALO, :] = jnp.zeros((A_HALO, A_DIM), F32)
            pb[0:B_HALO, :] = jnp.zeros((B_HALO, B_DIM), F32)

        @pl.when(jnp.logical_not(first))
        def _():
            pa[0, 0:A_HALO, :] = pa[0, tm:tm + A_HALO, :]
            pb[0:B_HALO, :] = pb[tm:tm + B_HALO, :]

        pa[0, A_HALO:A_HALO + tm, :] = a_val * jax.nn.sigmoid(a_gate)
        pb[B_HALO:B_HALO + tm, :] = c_gate * b_val
        _fill_shifted(pa, A_HALO + tm)
        bias = jnp.broadcast_to(cab_ref[...], (CONV_ROWS, A_DIM))
        for r0 in range(0, tm, CONV_ROWS):
            a2_ref[r0:r0 + CONV_ROWS, :] = _conv31(pa, caw_ref, r0, A_HALO - (A_CONV_WIDTH - 1), bias)
        xhat, _ = _ln_stats(a2_ref[...])
        a3 = xhat * lng_ref[...] + lnb_ref[...]
        a4 = a3 * jax.nn.sigmoid(a3)
        cv = cbw_ref[0:1, :] * pb[B_HALO - 2:B_HALO - 2 + tm, :]
        cv = cv + cbw_ref[1:2, :] * pb[B_HALO - 1:B_HALO - 1 + tm, :]
        cv = cv + cbw_ref[2:3, :] * pb[B_HALO:B_HALO + tm, :]
        cv_ref[...] = cv.astype(BF16)
        mix = jnp.concatenate([a4, b_gate * cv], axis=1).astype(BF16)
        mix_ref[...] = mix
        h_ref[...] = xv + _dot(mix, wout_v[...])

    shp = lambda cols, dt: jax.ShapeDtypeStruct((tokens, cols), dt)
    return _pallas(
        body, [x, norm_g, w_in, conv_a_w, conv_a_b, ln_g, ln_b, conv_b_w, w_out], name="fwd_even", grid=(nt,),
        in_specs=[_row_spec(tm, D_MODEL), _full_spec((1, D_MODEL)), ANY, _full_spec((A_CONV_WIDTH, A_DIM)),
                  _full_spec((1, A_DIM)), _full_spec((1, A_DIM)), _full_spec((1, A_DIM)),
                  _full_spec((B_CONV_WIDTH, B_DIM)), ANY],
        out_specs=[_row_spec(tm, D_MODEL), _row_spec(tm, D_MODEL), _row_spec(tm, IN_EVEN), _row_spec(tm, A_DIM),
                   _row_spec(tm, B_DIM), _row_spec(tm, D_MODEL)],
        out_shape=[shp(D_MODEL, F32), shp(D_MODEL, BF16), shp(IN_EVEN, BF16), shp(A_DIM, F32), shp(B_DIM, BF16),
                   shp(D_MODEL, BF16)],
        scratch_shapes=[pltpu.VMEM((N_CHIPS, D_MODEL, IN_EVEN // N_CHIPS), BF16), pltpu.VMEM((D_MODEL, D_MODEL), BF16),
                        pltpu.VMEM((SUBLANES, A_HALO + tm, A_DIM), F32), pltpu.VMEM((B_HALO + tm, B_DIM), F32),
                        pltpu.SemaphoreType.DMA((N_LOADS,))],
        vmem_mib=56, riders=riders)


def _loss_tail(xv, g, target, loss_ref, dh_ref, dhb_ref, dg_ref):
    @pl.when(pl.program_id(0) == 0)
    def _():
        loss_ref[...] = jnp.zeros((1, 1), F32)
        dg_ref[...] = jnp.zeros((1, D_MODEL), F32)

    out, rstd = _rms_fwd(xv, g)
    err = out - target
    per_token = jnp.sum(err * err, axis=1, keepdims=True) * (1.0 / D_MODEL)
    loss_ref[...] += 0.5 * jnp.sum(per_token, axis=0, keepdims=True)
    dx, dg = _rms_bwd(err * (1.0 / D_MODEL), xv, rstd, g)
    dh_ref[...] = dx
    dhb_ref[...] = dx.astype(BF16)
    dg_ref[...] += dg


def _fwd_mlp(h, norm_g, w1, w2, layer, *, tm, riders=(), head=None):
    tokens = h.shape[0]
    nt = tokens // tm
    fs = D_FF // N_CHIPS
    n_in = 4 if head is None else 6

    def body(*refs):
        h_ref, g_ref, w1_hbm, w2_hbm = refs[:4]
        w1_v, w2_v, sem = refs[-3:]
        outs = refs[n_in:-3]
        n_ref, p_ref, q_ref = outs[1:4] if head is None else outs[0:3]
        _load_weights([(w1_hbm, w1_v, False), (w2_hbm, w2_v, False)], sem)

        xv = h_ref[...]
        nf, _ = _rms_fwd(xv, g_ref[...])
        n = nf.astype(BF16)
        n_ref[...] = n
        acc = xv
        for j in range(N_CHIPS):
            p = _dot(n, w1_v[j])
            p_ref[:, j * fs:(j + 1) * fs] = p.astype(BF16)
            r = jnp.maximum(p, 0.0)
            q = (r * r).astype(BF16)
            q_ref[:, j * fs:(j + 1) * fs] = q
            acc = acc + _dot(q, w2_v[j])
        if head is None:
            outs[0][...] = acc
        else:
            _loss_tail(acc, refs[4][...], refs[5][...], *outs[3:7])

    shp = lambda cols, dt: jax.ShapeDtypeStruct((tokens, cols), dt)
    saved_specs = [_row_spec(tm, D_MODEL), _row_spec(tm, D_FF), _row_spec(tm, D_FF)]
    saved_shapes = [shp(D_MODEL, BF16), shp(D_FF, BF16), shp(D_FF, BF16)]
    if head is None:
        operands, in_specs = [h, norm_g, w1, w2], [_row_spec(tm, D_MODEL), _full_spec((1, D_MODEL)), ANY, ANY]
        out_specs, out_shape = [_row_spec(tm, D_MODEL)] + saved_specs, [shp(D_MODEL, F32)] + saved_shapes
    else:
        operands = [h, norm_g, w1, w2, *head]
        in_specs = [_row_spec(tm, D_MODEL), _full_spec((1, D_MODEL)), ANY, ANY, _full_spec((1, D_MODEL)), _row_spec(tm, D_MODEL)]
        out_specs = saved_specs + [_full_spec((1, 1)), _row_spec(tm, D_MODEL), _row_spec(tm, D_MODEL), _full_spec((1, D_MODEL))]
        out_shape = saved_shapes + [jax.ShapeDtypeStruct((1, 1), F32), shp(D_MODEL, F32), shp(D_MODEL, BF16),
                                    jax.ShapeDtypeStruct((1, D_MODEL), F32)]
    return _pallas(
        body, operands, name=f"fwd_mlp{layer}", grid=(nt,), in_specs=in_specs, out_specs=out_specs, out_shape=out_shape,
        scratch_shapes=[pltpu.VMEM((N_CHIPS, D_MODEL, fs), BF16), pltpu.VMEM((N_CHIPS, fs, D_MODEL), BF16),
                        pltpu.SemaphoreType.DMA((N_LOADS,))],
        vmem_mib=56, riders=riders)


def _tril_mask():
    row = lax.broadcasted_iota(jnp.int32, (CHUNK, CHUNK), 0)
    col = lax.broadcasted_iota(jnp.int32, (CHUNK, CHUNK), 1)
    return row >= col


def _triu_mask():
    row = lax.broadcasted_iota(jnp.int32, (CHUNK, CHUNK), 0)
    col = lax.broadcasted_iota(jnp.int32, (CHUNK, CHUNK), 1)
    return row <= col


def _fwd_odd(h, norm_g, w_in, b_in, ln_g, ln_b, w_s, b_s_rows, w_out, *, tm, riders=()):
    tokens = h.shape[0]
    nt = tokens // tm
    cs = 2 * C_DIM // N_CHIPS

    def body(h_ref, g_ref, win_hbm, bin_ref, lng_ref, lnb_ref, ws_ref, bs_ref, wout_hbm,
             ho_ref, n_ref, s_ref, cdf_ref, sv_ref, y_ref, win_v, wout_v, bd, sem):
        _load_weights([(win_hbm, win_v, False), (wout_hbm, wout_v, True)], sem)

        @pl.when(pl.program_id(0) == 0)
        def _():
            mask = _tril_mask()
            bd[...] = jnp.zeros(bd.shape, BF16)
            for g in range(C_GROUPS):
                w = jnp.where(mask, ws_ref[g], 0.0).astype(BF16)
                bd[g, 0:CHUNK, 0:CHUNK] = w
                bd[g, CHUNK:PAIR, CHUNK:PAIR] = w

        xv = h_ref[...]
        nf, _ = _rms_fwd(xv, g_ref[...])
        n = nf.astype(BF16)
        n_ref[...] = n
        s = jnp.concatenate([_dot(n, win_v[j]) for j in range(N_CHIPS)], axis=1) + bin_ref[...]
        s_ref[...] = s.astype(BF16)
        cdf = _gelu_cdf(s)
        cdf_ref[...] = cdf.astype(BF16)
        zz = s * cdf
        u, v = zz[:, 0:C_DIM], zz[:, C_DIM:2 * C_DIM]
        xhat, _ = _ln_stats(v)
        vn = (xhat * lng_ref[...] + lnb_ref[...]).astype(BF16)
        for g in range(C_GROUPS):
            cols = slice(g * CHUNK, (g + 1) * CHUNK)
            bias = jnp.concatenate([bs_ref[g], bs_ref[g]], axis=0)
            for r0 in range(0, tm, PAIR):
                sv = _dot(bd[g], vn[r0:r0 + PAIR, cols]) + bias
                sv_ref[r0:r0 + PAIR, cols] = sv.astype(BF16)
                y_ref[r0:r0 + PAIR, cols] = (u[r0:r0 + PAIR, cols] * sv).astype(BF16)
        ho_ref[...] = xv + _dot(y_ref[...], wout_v[...])

    shp = lambda cols, dt: jax.ShapeDtypeStruct((tokens, cols), dt)
    return _pallas(
        body, [h, norm_g, w_in, b_in, ln_g, ln_b, w_s, b_s_rows, w_out], name="fwd_odd", grid=(nt,),
        in_specs=[_row_spec(tm, D_MODEL), _full_spec((1, D_MODEL)), ANY, _full_spec((1, 2 * C_DIM)),
                  _full_spec((1, C_DIM)), _full_spec((1, C_DIM)), _full_spec((C_GROUPS, CHUNK, CHUNK)),
                  _full_spec((C_GROUPS, CHUNK, CHUNK)), ANY],
        out_specs=[_row_spec(tm, D_MODEL), _row_spec(tm, D_MODEL), _row_spec(tm, 2 * C_DIM), _row_spec(tm, 2 * C_DIM),
                   _row_spec(tm, C_DIM), _row_spec(tm, C_DIM)],
        out_shape=[shp(D_MODEL, F32), shp(D_MODEL, BF16), shp(2 * C_DIM, BF16), shp(2 * C_DIM, BF16), shp(C_DIM, BF16),
                   shp(C_DIM, BF16)],
        scratch_shapes=[pltpu.VMEM((N_CHIPS, D_MODEL, cs), BF16), pltpu.VMEM((C_DIM, D_MODEL), BF16),
                        pltpu.VMEM((C_GROUPS, PAIR, PAIR), BF16), pltpu.SemaphoreType.DMA((N_LOADS,))],
        vmem_mib=56, riders=riders)


def _bwd_mlp(dh, h, norm_g, p, w1, w2, layer, *, tm, riders=()):
    tokens = h.shape[0]
    nt = tokens // tm
    fs = D_FF // N_CHIPS

    def body(dh_ref, h_ref, g_ref, p_ref, w1_hbm, w2_hbm, dx_ref, dxb_ref, dp_ref, dg_ref, w1_v, w2_v, sem):
        @pl.when(pl.program_id(0) == 0)
        def _():
            dg_ref[...] = jnp.zeros((1, D_MODEL), F32)

        _load_weights([(w1_hbm, w1_v, False), (w2_hbm, w2_v, False)], sem)

        dhv = dh_ref[...]
        dhb = dhv.astype(BF16)
        dn = jnp.zeros((tm, D_MODEL), F32)
        for j in range(N_CHIPS):
            dq = _dot_nt(dhb, w2_v[j])
            r = jnp.maximum(p_ref[:, j * fs:(j + 1) * fs].astype(F32), 0.0)
            dp = ((2.0 * r) * dq).astype(BF16)
            dp_ref[:, j * fs:(j + 1) * fs] = dp
            dn = dn + _dot_nt(dp, w1_v[j])
        xv = h_ref[...]
        g = g_ref[...]
        _, rstd = _rms_fwd(xv, g)
        dx, dg = _rms_bwd(dn, xv, rstd, g)
        dx_ref[...] = dhv + dx
        dxb_ref[...] = (dhv + dx).astype(BF16)
        dg_ref[...] += dg

    return _pallas(
        body, [dh, h, norm_g, p, w1, w2], name=f"bwd_mlp{layer}", grid=(nt,),
        in_specs=[_row_spec(tm, D_MODEL), _row_spec(tm, D_MODEL), _full_spec((1, D_MODEL)), _row_spec(tm, D_FF), ANY, ANY],
        out_specs=[_row_spec(tm, D_MODEL), _row_spec(tm, D_MODEL), _row_spec(tm, D_FF), _full_spec((1, D_MODEL))],
        out_shape=[jax.ShapeDtypeStruct((tokens, D_MODEL), F32), jax.ShapeDtypeStruct((tokens, D_MODEL), BF16),
                   jax.ShapeDtypeStruct((tokens, D_FF), BF16), jax.ShapeDtypeStruct((1, D_MODEL), F32)],
        scratch_shapes=[pltpu.VMEM((N_CHIPS, D_MODEL, fs), BF16), pltpu.VMEM((N_CHIPS, fs, D_MODEL), BF16),
                        pltpu.SemaphoreType.DMA((N_LOADS,))],
        vmem_mib=56, riders=riders)


def _bwd_odd(dh, h, norm_g, s, cdf, sv, w_in, ln_g, ln_b, w_s, w_out, *, tm, riders=()):
    tokens = h.shape[0]
    nt = tokens // tm
    cs = 2 * C_DIM // N_CHIPS

    def body(dh_ref, h_ref, g_ref, s_ref, cdf_ref, sv_ref, win_hbm, lng_ref, lnb_ref, ws_ref, wout_hbm,
             dx_ref, dxb_ref, ds_ref, dg_ref, dbin_ref, dlng_ref, dlnb_ref, dws_ref, dbs_ref,
             win_v, wout_v, bdt, dws_acc, dbs_acc, dvn, sem):
        i = pl.program_id(0)

        _load_weights([(win_hbm, win_v, False), (wout_hbm, wout_v, True)], sem)

        @pl.when(i == 0)
        def _():
            mask_t = _triu_mask()
            bdt[...] = jnp.zeros(bdt.shape, BF16)
            for g in range(C_GROUPS):
                wt = jnp.where(mask_t, ws_ref[g].T, 0.0).astype(BF16)
                bdt[g, 0:CHUNK, 0:CHUNK] = wt
                bdt[g, CHUNK:PAIR, CHUNK:PAIR] = wt
            dws_acc[...] = jnp.zeros(dws_acc.shape, F32)
            dbs_acc[...] = jnp.zeros(dbs_acc.shape, F32)
            dg_ref[...] = jnp.zeros(dg_ref.shape, F32)
            dbin_ref[...] = jnp.zeros(dbin_ref.shape, F32)
            dlng_ref[...] = jnp.zeros(dlng_ref.shape, F32)
            dlnb_ref[...] = jnp.zeros(dlnb_ref.shape, F32)

        dhv = dh_ref[...]
        dy = _dot_nt(dhv.astype(BF16), wout_v[...])
        sf = s_ref[...].astype(F32)
        cdf = cdf_ref[...].astype(F32)
        pdf = jnp.exp(-0.5 * sf * sf) * 0.3989422804014327
        zz = sf * cdf
        dgelu = cdf + sf * pdf
        u, v = zz[:, 0:C_DIM], zz[:, C_DIM:2 * C_DIM]
        xhat, rs = _ln_stats(v)
        lng = lng_ref[...]
        vn = (xhat * lng + lnb_ref[...]).astype(BF16)
        du = dy * sv_ref[...].astype(F32)
        dsv = dy * u
        dsvb = dsv.astype(BF16)
        for g in range(C_GROUPS):
            cols = slice(g * CHUNK, (g + 1) * CHUNK)
            for r0 in range(0, tm, PAIR):
                blk = dsvb[r0:r0 + PAIR, cols]
                dvn[r0:r0 + PAIR, cols] = _dot(bdt[g], blk)
                dws_acc[g] += _dot_nt(blk, vn[r0:r0 + PAIR, cols])
                dbs_acc[g] += dsv[r0:r0 + CHUNK, cols] + dsv[r0 + CHUNK:r0 + PAIR, cols]
        dv, dlng, dlnb = _ln_bwd(dvn[...], xhat, rs, lng)
        dlng_ref[...] += dlng
        dlnb_ref[...] += dlnb
        ds = jnp.concatenate([du, dv], axis=1) * dgelu
        dbin_ref[...] += jnp.sum(ds, axis=0, keepdims=True)
        dsb = ds.astype(BF16)
        ds_ref[...] = dsb
        dn = jnp.zeros((tm, D_MODEL), F32)
        for j in range(N_CHIPS):
            dn = dn + _dot_nt(dsb[:, j * cs:(j + 1) * cs], win_v[j])
        xv = h_ref[...]
        g = g_ref[...]
        _, rstd = _rms_fwd(xv, g)
        dx, dg = _rms_bwd(dn, xv, rstd, g)
        dx_ref[...] = dhv + dx
        dxb_ref[...] = (dhv + dx).astype(BF16)
        dg_ref[...] += dg

        @pl.when(i == nt - 1)
        def _():
            mask = _tril_mask()
            for g in range(C_GROUPS):
                full = dws_acc[g]
                dws_ref[g] = jnp.where(mask, full[0:CHUNK, 0:CHUNK] + full[CHUNK:PAIR, CHUNK:PAIR], 0.0)
                dbs_ref[g:g + 1, :] = jnp.sum(dbs_acc[g].T, axis=0, keepdims=True)

    row = lambda cols: jax.ShapeDtypeStruct((1, cols), F32)
    return _pallas(
        body, [dh, h, norm_g, s, cdf, sv, w_in, ln_g, ln_b, w_s, w_out], name="bwd_odd", grid=(nt,),
        in_specs=[_row_spec(tm, D_MODEL), _row_spec(tm, D_MODEL), _full_spec((1, D_MODEL)), _row_spec(tm, 2 * C_DIM),
                  _row_spec(tm, 2 * C_DIM), _row_spec(tm, C_DIM), ANY, _full_spec((1, C_DIM)), _full_spec((1, C_DIM)),
                  _full_spec((C_GROUPS, CHUNK, CHUNK)), ANY],
        out_specs=[_row_spec(tm, D_MODEL), _row_spec(tm, D_MODEL), _row_spec(tm, 2 * C_DIM), _full_spec((1, D_MODEL)),
                   _full_spec((1, 2 * C_DIM)),
                   _full_spec((1, C_DIM)), _full_spec((1, C_DIM)), _full_spec((C_GROUPS, CHUNK, CHUNK)),
                   _full_spec((C_GROUPS, CHUNK))],
        out_shape=[jax.ShapeDtypeStruct((tokens, D_MODEL), F32), jax.ShapeDtypeStruct((tokens, D_MODEL), BF16),
                   jax.ShapeDtypeStruct((tokens, 2 * C_DIM), BF16),
                   row(D_MODEL), row(2 * C_DIM), row(C_DIM), row(C_DIM),
                   jax.ShapeDtypeStruct((C_GROUPS, CHUNK, CHUNK), F32), jax.ShapeDtypeStruct((C_GROUPS, CHUNK), F32)],
        scratch_shapes=[pltpu.VMEM((N_CHIPS, D_MODEL, cs), BF16), pltpu.VMEM((C_DIM, D_MODEL), BF16),
                        pltpu.VMEM((C_GROUPS, PAIR, PAIR), BF16), pltpu.VMEM((C_GROUPS, PAIR, PAIR), F32),
                        pltpu.VMEM((C_GROUPS, CHUNK, CHUNK), F32), pltpu.VMEM((tm, C_DIM), F32),
                        pltpu.SemaphoreType.DMA((N_LOADS,))],
        vmem_mib=56, riders=riders)


def _bwd_even(dh, x, norm_g, z, a2, cv, w_in, conv_a_w, ln_g, ln_b, conv_b_w, w_out, *, tm, seq, riders=()):
    tokens = x.shape[0]
    nt, tps = tokens // tm, seq // tm
    ws = IN_EVEN // N_CHIPS

    def body(dh_ref, x_ref, g_ref, z_ref, a2_ref, cv_ref, win_hbm, caw_ref, lng_ref, lnb_ref, cbw_ref, wout_hbm,
             dx_ref, dz_ref, dg_ref, dcaw_ref, dcab_ref, dlng_ref, dlnb_ref, dcbw_ref,
             win_v, wout_v, ea, eb, a1s, da1s, sigs, wide, dw_acc, sem):
        i = pl.program_id(0)

        _load_weights([(win_hbm, win_v, False), (wout_hbm, wout_v, True)], sem)

        @pl.when(i == 0)
        def _():
            dw_acc[...] = jnp.zeros(dw_acc.shape, F32)
            for ref in (dg_ref, dcab_ref, dlng_ref, dlnb_ref, dcbw_ref):
                ref[...] = jnp.zeros(ref.shape, F32)

        last = ((nt - 1 - i) % tps) == tps - 1

        @pl.when(last)
        def _():
            ea[0, tm:tm + A_HALO, :] = jnp.zeros((A_HALO, A_DIM), F32)
            eb[tm:tm + B_HALO, :] = jnp.zeros((B_HALO, B_DIM), F32)

        @pl.when(jnp.logical_not(last))
        def _():
            ea[0, tm:tm + A_HALO, :] = ea[0, 0:A_HALO, :]
            eb[tm:tm + B_HALO, :] = eb[0:B_HALO, :]

        wide[...] = _dot_nt(dh_ref[...].astype(BF16), wout_v[...])
        lng, lnb = lng_ref[...], lnb_ref[...]
        zero_row = jnp.zeros((1, A_DIM), F32)
        dlng, dlnb, dcab = zero_row, zero_row, zero_row
        for r0 in range(0, tm, ELEM_ROWS):
            rows = slice(r0, r0 + ELEM_ROWS)
            a_val, a_gate = z_ref[rows, 0:A_DIM].astype(F32), z_ref[rows, A_DIM:2 * A_DIM].astype(F32)
            xhat, rs = _ln_stats(a2_ref[rows, :])
            a3 = xhat * lng + lnb
            sg = jax.nn.sigmoid(a3)
            da3 = wide[rows, 0:A_DIM] * (sg * (1.0 + a3 * (1.0 - sg)))
            da2, g_part, b_part = _ln_bwd(da3, xhat, rs, lng)
            dlng, dlnb, dcab = dlng + g_part, dlnb + b_part, dcab + jnp.sum(da2, axis=0, keepdims=True)
            ea[0, rows, :] = da2
            eb[rows, :] = wide[rows, A_DIM:A_DIM + B_DIM] * z_ref[rows, 1024:1536].astype(F32)
            sig = jax.nn.sigmoid(a_gate)
            sigs[rows, :] = sig
            a1s[rows, :] = a_val * sig
        dlng_ref[...] += dlng
        dlnb_ref[...] += dlnb
        dcab_ref[...] += dcab
        _fill_shifted(ea, tm + A_HALO)
        for r0 in range(0, tm, CONV_ROWS):
            acc = jnp.zeros((CONV_ROWS, A_DIM), F32)
            for j in range(A_CONV_WIDTH):
                acc = acc + caw_ref[A_CONV_WIDTH - 1 - j:A_CONV_WIDTH - j, :] * _window(ea, r0 + j, CONV_ROWS)
            da1s[r0:r0 + CONV_ROWS, :] = acc
        for j0 in range(0, A_CONV_WIDTH, DW_TAPS):
            taps = range(j0, min(j0 + DW_TAPS, A_CONV_WIDTH))
            part = [jnp.zeros((CONV_ROWS, A_DIM), F32) for _ in taps]
            for r0 in range(0, tm, CONV_ROWS):
                a1c = a1s[r0:r0 + CONV_ROWS, :]
                for u, j in enumerate(taps):
                    part[u] = part[u] + _window(ea, r0 + j, CONV_ROWS) * a1c
            for u, j in enumerate(taps):
                dw_acc[A_CONV_WIDTH - 1 - j] += part[u]
        dcbw = [jnp.zeros((1, B_DIM), F32) for _ in range(B_CONV_WIDTH)]
        for r0 in range(0, tm, ELEM_ROWS):
            rows = slice(r0, r0 + ELEM_ROWS)
            da1, sig = da1s[rows, :], sigs[rows, :]
            dz_ref[rows, 0:A_DIM] = (da1 * sig).astype(BF16)
            dz_ref[rows, A_DIM:2 * A_DIM] = (da1 * z_ref[rows, 0:A_DIM].astype(F32) * (sig * (1.0 - sig))).astype(BF16)
            c_gate, b_val = z_ref[rows, 1536:2048].astype(F32), z_ref[rows, 2048:2560].astype(F32)
            dz_ref[rows, 1024:1536] = (wide[rows, A_DIM:A_DIM + B_DIM] * cv_ref[rows, :].astype(F32)).astype(BF16)
            cb = c_gate * b_val
            dcb = jnp.zeros((ELEM_ROWS, B_DIM), F32)
            for j in range(B_CONV_WIDTH):
                k = B_CONV_WIDTH - 1 - j
                sl = eb[r0 + j:r0 + j + ELEM_ROWS, :]
                dcb = dcb + cbw_ref[k:k + 1, :] * sl
                dcbw[k] = dcbw[k] + jnp.sum(sl * cb, axis=0, keepdims=True)
            dz_ref[rows, 1536:2048] = (dcb * b_val).astype(BF16)
            dz_ref[rows, 2048:2560] = (dcb * c_gate).astype(BF16)
        for k in range(B_CONV_WIDTH):
            dcbw_ref[k:k + 1, :] += dcbw[k]
        dn = jnp.zeros((tm, D_MODEL), F32)
        for j in range(N_CHIPS):
            dn = dn + _dot_nt(dz_ref[:, j * ws:(j + 1) * ws], win_v[j])
        wide[...] = dn
        g = g_ref[...]
        dg = jnp.zeros((1, D_MODEL), F32)
        for r0 in range(0, tm, ELEM_ROWS):
            rows = slice(r0, r0 + ELEM_ROWS)
            xv = x_ref[rows, :]
            _, rstd = _rms_fwd(xv, g)
            dx, dg_part = _rms_bwd(wide[rows, :], xv, rstd, g)
            dx_ref[rows, :] = dh_ref[rows, :] + dx
            dg = dg + dg_part
        dg_ref[...] += dg

        @pl.when(i == nt - 1)
        def _():
            for k in range(A_CONV_WIDTH):
                dcaw_ref[k:k + 1, :] = jnp.sum(dw_acc[k], axis=0, keepdims=True)

    row = lambda cols: jax.ShapeDtypeStruct((1, cols), F32)
    rs_ = functools.partial(_row_spec, rev_nt=nt)
    return _pallas(
        body, [dh, x, norm_g, z, a2, cv, w_in, conv_a_w, ln_g, ln_b, conv_b_w, w_out], name="bwd_even", grid=(nt,),
        in_specs=[rs_(tm, D_MODEL), rs_(tm, D_MODEL), _full_spec((1, D_MODEL)), rs_(tm, IN_EVEN), rs_(tm, A_DIM),
                  rs_(tm, B_DIM), ANY, _full_spec((A_CONV_WIDTH, A_DIM)), _full_spec((1, A_DIM)), _full_spec((1, A_DIM)),
                  _full_spec((B_CONV_WIDTH, B_DIM)), ANY],
        out_specs=[rs_(tm, D_MODEL), rs_(tm, IN_EVEN), _full_spec((1, D_MODEL)), _full_spec((A_CONV_WIDTH, A_DIM)),
                   _full_spec((1, A_DIM)), _full_spec((1, A_DIM)), _full_spec((1, A_DIM)), _full_spec((B_CONV_WIDTH, B_DIM))],
        out_shape=[jax.ShapeDtypeStruct((tokens, D_MODEL), F32), jax.ShapeDtypeStruct((tokens, IN_EVEN), BF16),
                   row(D_MODEL), jax.ShapeDtypeStruct((A_CONV_WIDTH, A_DIM), F32), row(A_DIM), row(A_DIM), row(A_DIM),
                   jax.ShapeDtypeStruct((B_CONV_WIDTH, B_DIM), F32)],
        scratch_shapes=[pltpu.VMEM((N_CHIPS, D_MODEL, ws), BF16), pltpu.VMEM((D_MODEL, D_MODEL), BF16),
                        pltpu.VMEM((SUBLANES, tm + A_HALO, A_DIM), F32), pltpu.VMEM((tm + B_HALO, B_DIM), F32),
                        pltpu.VMEM((tm, A_DIM), F32), pltpu.VMEM((tm, A_DIM), F32), pltpu.VMEM((tm, A_DIM), F32),
                        pltpu.VMEM((tm, D_MODEL), F32),
                        pltpu.VMEM((A_CONV_WIDTH, CONV_ROWS, A_DIM), F32), pltpu.SemaphoreType.DMA((N_LOADS,))],
        vmem_mib=56, riders=riders)


def _wgrad(a, b, name, *, col_shards, riders=()):
    tokens, m = a.shape
    n = b.shape[1]
    kc = 512
    if col_shards:
        bm, bn = m // 2, n // N_CHIPS
        grid = (2, N_CHIPS)
        out_spec = pl.BlockSpec((None, None, bm, bn), lambda i, j: (j, i, 0, 0))
    elif m // 8 >= MXU_ROWS:
        bm, bn = m // 8, n
        grid = (8, 1)
        out_spec = pl.BlockSpec((None, None, bm, bn), lambda i, j: (i // 2, i % 2, 0, 0))
    else:
        bm, bn = m // N_CHIPS, n
        grid = (N_CHIPS, 1)
        out_spec = pl.BlockSpec((None, 2, bm // 2, bn), lambda i, j: (i, 0, 0, 0))

    def body(a_ref, b_ref, o_ref):
        acc = jnp.zeros((bm, bn), F32)
        for k0 in range(0, tokens, kc):
            acc = acc + _dot_tn(a_ref[k0:k0 + kc, :].astype(BF16), b_ref[k0:k0 + kc, :].astype(BF16))
        if len(o_ref.shape) == 3:
            o_ref[0] = acc[0:bm // 2]
            o_ref[1] = acc[bm // 2:bm]
        else:
            o_ref[...] = acc

    out_rows = m // 2 if col_shards else m // 8
    outs, routs = _pallas(
        body, [a, b], name=name, grid=grid,
        in_specs=[pl.BlockSpec((tokens, bm), lambda i, j: (0, i)), pl.BlockSpec((tokens, bn), lambda i, j: (0, j))],
        out_specs=[out_spec], out_shape=[jax.ShapeDtypeStruct((N_CHIPS, 2, out_rows, bn), F32)],
        vmem_mib=56, riders=riders)
    return outs[0], routs


def _wgrad_pair(a, b, name, *, col_shards, riders=(), to_chips=False):
    tokens, m = a.shape
    n = b.shape[1]
    kc = 512
    x0, y0, c0 = _mesh_pos()
    rot = 1 if to_chips else 0
    phases = [0, 0, 1, 0, 1, 0, 1, 1] if to_chips else [0, 0, 0, 0, 1, 1, 1, 1]
    tiles = [0, 1, 0, 2, 1, 3, 2, 3] if to_chips else [0, 1, 2, 3, 0, 1, 2, 3]
    out_tiles = [0, 0, 0, 0, 1, 1, 2, 3] if to_chips else [0, 0, 0, 0, 0, 1, 2, 3]
    steps = len(phases)
    P0, T0, O0 = 2, 2 + steps, 2 + 2 * steps

    def slab(t, pre):
        return (t + rot * (1 + pre[1])) % N_CHIPS

    def half(s, pre):
        return (pre[P0 + s] + 1 + pre[0]) % 2

    if col_shards:
        bm, bn = m // 2, n // N_CHIPS
        a_spec = pl.BlockSpec((tokens, bm), lambda s, pre: (0, half(s, pre)))
        b_spec = pl.BlockSpec((tokens, bn), lambda s, pre: (0, slab(pre[T0 + s], pre)))
    else:
        bm, bn = m // 8, n
        a_spec = pl.BlockSpec((tokens, bm), lambda s, pre: (0, 2 * slab(pre[T0 + s], pre) + half(s, pre)))
        b_spec = pl.BlockSpec((tokens, bn), lambda s, pre: (0, 0))

    def body(pre_ref, a_ref, b_ref, o_ref, *rest):
        if to_chips:
            land, give, got, mine, send_sems, recv_sems, chip_send, chip_recv = rest
        else:
            give, got, send_sems, recv_sems = rest
        step = pl.program_id(0)
        ph, q = pre_ref[P0 + step], pre_ref[T0 + step]
        acc = jnp.zeros((bm, bn), F32)
        for k0 in range(0, tokens, kc):
            acc = acc + _dot_tn(a_ref[k0:k0 + kc, :].astype(BF16), b_ref[k0:k0 + kc, :].astype(BF16))
        x, y, cc = _mesh_pos()

        def tile(t):
            return _remote(give.at[t], got.at[t], send_sems.at[t], recv_sems.at[t], (x, y, 1 - cc))

        def to_chip(s):
            t = (s + 1 + 2 * x + y) % N_CHIPS
            tx, ty = t // 2, t % 2
            k = 2 * (ty ^ y) + (tx ^ x) - 1
            return _remote(mine.at[s], land.at[k], chip_send.at[k], chip_recv.at[k], (tx, ty, cc))

        @pl.when(ph == 0)
        def _():
            give[q] = acc
            tile(q).start()

        @pl.when(ph == 1)
        def _():
            tile(q).wait_recv()
            total = (acc + got[q]).astype(BF16)
            o_ref[...] = total
            if to_chips:
                for s in range(N_CHIPS - 1):
                    @pl.when(q == s)
                    def _(s=s):
                        mine[s] = total
                        to_chip(s).start()

        @pl.when(step == steps - 1)
        def _():
            for t in range(N_CHIPS):
                tile(t).wait_send()
            if to_chips:
                for s in range(N_CHIPS - 1):
                    to_chip(s).wait()

    prefetch = jnp.concatenate([jnp.stack([c0, 2 * x0 + y0]).astype(jnp.int32),
                                jnp.asarray(phases + tiles + out_tiles, jnp.int32)])
    out_specs = [pl.BlockSpec((None, bm, bn), lambda s, pre: (slab(pre[O0 + s], pre), 0, 0))]
    out_shape = [jax.ShapeDtypeStruct((N_CHIPS, bm, bn), BF16)]
    scratch = [pltpu.VMEM((N_CHIPS, bm, bn), F32), pltpu.VMEM((N_CHIPS, bm, bn), F32)]
    sems = [pltpu.SemaphoreType.DMA((N_CHIPS,)), pltpu.SemaphoreType.DMA((N_CHIPS,))]
    if to_chips:
        out_specs.append(ANY)
        out_shape.append(jax.ShapeDtypeStruct((N_CHIPS - 1, bm, bn), BF16))
        scratch.append(pltpu.VMEM((N_CHIPS - 1, bm, bn), BF16))
        sems += [pltpu.SemaphoreType.DMA((N_CHIPS - 1,)), pltpu.SemaphoreType.DMA((N_CHIPS - 1,))]
    outs, routs = _pallas(
        body, [a, b], name=name, grid=(steps,), in_specs=[a_spec, b_spec], out_specs=out_specs, out_shape=out_shape,
        scratch_shapes=scratch + sems, vmem_mib=56, riders=riders, prefetch=prefetch)
    return (outs if to_chips else outs[0]), routs


class _GradReduce:
    def __init__(self, name, grad=None, chip_sum=None):
        self.name, self.grad, self.chip_sum = name, grad, chip_sum
        self.full = None

    def pair_swap(self):
        return _PairSwap([self.grad])

    def took_pair(self, outs):
        self.chip_sum = _in_hbm(_add_pair(self.grad, outs[0], f"pair_sum_{self.name}"))

    def chip_swap(self):
        return _ChipSwap([self.chip_sum])

    def took_chips(self, outs):
        self.full = _in_hbm(_add_chips(self.chip_sum, outs[0], f"chip_sum_{self.name}"))

    def chips_beside(self, collective_id):
        self.took_chips([_chip_swap_beside(self.chip_sum, f"chip_swap_{self.name}", collective_id)])

    def pair_share(self):
        return _PairShare([self.full])

    def took_share(self, outs):
        self.full = outs[0]

    def reduced(self):
        return jnp.reshape(self.full, (2 * self.full.shape[1], self.full.shape[2]))


def _forward_backward(x2, tgt2, gathered, staged, conv_a_w, conv_b_w, od_norm, od_bias, od_lng, od_lnb,
                      ev_norm_g, ev_conv_a_b, ev_ln_a_g, ev_ln_a_b, od_w_s, od_b_s, mlp_norm_g, final_norm_g,
                      *, tm, seq, distributed=True):
    d = x2.shape[1]
    w = dict(gathered)
    b_s_rows = jnp.broadcast_to(od_b_s[0][:, :, None], (C_GROUPS, CHUNK, CHUNK))

    def ride(*names):
        return [_Gather([staged[nm] for nm in names])] if distributed and staged else []

    def land(routs, *names):
        if distributed and staged:
            for nm, buf in zip(names, routs[0]):
                w[nm] = buf

    (h1, n0, z, a2, cv, mix), routs = _fwd_even(
        x2, ev_norm_g, w["ev_in"], conv_a_w, ev_conv_a_b, ev_ln_a_g, ev_ln_a_b, conv_b_w, w["ev_out"],
        tm=tm, seq=seq, riders=ride("w1_0", "w2_0"))
    land(routs, "w1_0", "w2_0")
    (h2, n1, p0, q0), routs = _fwd_mlp(h1, mlp_norm_g[0:1], w["w1_0"], w["w2_0"], 0, tm=tm,
                                       riders=ride("od_in", "od_out", "w1_1"))
    land(routs, "od_in", "od_out", "w1_1")
    (h3, n2, s, cdf, sv, y), routs = _fwd_odd(h2, od_norm, w["od_in"], od_bias, od_lng, od_lnb, od_w_s[0], b_s_rows,
                                         w["od_out"], tm=tm, riders=ride("w2_1"))
    land(routs, "w2_1")
    (n3, p1, q1, loss_part, dh4, dh4b, d_final_g), _ = _fwd_mlp(
        h3, mlp_norm_g[1:2], w["w1_1"], w["w2_1"], 1, tm=tm,
        head=(jnp.reshape(final_norm_g, (1, d)), tgt2))

    red = {}

    def swap(*names):
        return [red[nm].pair_swap() for nm in names] if distributed else []

    def share(*names):
        return [red[nm].pair_share() for nm in names] if distributed else []

    def took(routs, *steps):
        if distributed:
            for (nm, what), outs in zip(steps, routs):
                getattr(red[nm], what)(outs)

    swap_ids = iter(range(FIRST_SWAP_ID, FIRST_SWAP_ID + 8))

    def beside(name):
        if distributed:
            red[name].chips_beside(next(swap_ids))

    def big(lhs, rhs, name, col_shards, riders=()):
        if distributed:
            chip_sum, routs = _wgrad_pair(lhs, rhs, f"wgrad_{name}", col_shards=col_shards, riders=riders)
            red[name] = _GradReduce(name, chip_sum=_in_hbm(chip_sum))
        else:
            g, routs = _wgrad(lhs, rhs, f"wgrad_{name}", col_shards=col_shards)
            red[name] = _GradReduce(name, grad=g)
        return routs

    big(q1, dh4b, "w2_1", False)
    beside("w2_1")
    (dh3, dh3b, dp1, d_mlp_g1), _ = _bwd_mlp(dh4, h3, mlp_norm_g[1:2], p1, w["w1_1"], w["w2_1"], 1, tm=tm)
    big(n3, dp1, "w1_1", True)
    beside("w1_1")
    g, routs = _wgrad(y, dh3b, "wgrad_od_out", col_shards=False, riders=share("w2_1"))
    red["od_out"] = _GradReduce("od_out", grad=g)
    took(routs, ("w2_1", "took_share"))
    (dh2, dh2b, ds, d_od_norm, d_od_bin, d_od_lng, d_od_lnb, d_ws, d_bs), _ = _bwd_odd(
        dh3, h2, od_norm, s, cdf, sv, w["od_in"], od_lng, od_lnb, od_w_s[0], w["od_out"], tm=tm)
    routs = big(n2, ds, "od_in", True, riders=share("w1_1") + swap("od_out"))
    took(routs, ("w1_1", "took_share"), ("od_out", "took_pair"))
    beside("od_in")
    beside("od_out")
    half_groups = C_GROUPS // 2
    early = {"loss": loss_part, "od_w_s_lo": d_ws[:half_groups], "od_b_s": d_bs, "mlp_norm_g1": d_mlp_g1, "final_norm_g": d_final_g,
             "od_norm_g": d_od_norm, "od_b_in": d_od_bin, "od_ln_v_g": d_od_lng, "od_ln_v_b": d_od_lnb}
    share_early = [_ShareAll(list(early.values()))] if distributed else []
    routs = big(q0, dh2b, "w2_0", False, riders=share_early)
    landed_early = routs[0] if distributed else []
    beside("w2_0")
    (dh1, dh1b, dp0, d_mlp_g0), _ = _bwd_mlp(dh2, h1, mlp_norm_g[0:1], p0, w["w1_0"], w["w2_0"], 0, tm=tm)
    middle = {"od_w_s_hi": d_ws[half_groups:]}
    share_middle = [_ShareAll(list(middle.values()))] if distributed else []
    g, _ = _wgrad(mix, dh1b, "wgrad_ev_out", col_shards=False)
    red["ev_out"] = _GradReduce("ev_out", grad=g)
    routs = big(n1, dp0, "w1_0", True,
                riders=share("od_out") + share("od_in") + share("w2_0") + share_middle + swap("ev_out"))
    took(routs, ("od_out", "took_share"), ("od_in", "took_share"), ("w2_0", "took_share"))
    landed_middle = routs[3] if distributed else []
    if distributed:
        red["ev_out"].took_pair(routs[4])
    beside("w1_0")
    beside("ev_out")

    (dx, dz, d_ev_norm, d_caw, d_cab, d_ev_lng, d_ev_lnb, d_cbw), _ = _bwd_even(
        dh1, x2, ev_norm_g, z, a2, cv, w["ev_in"], conv_a_w, ev_ln_a_g, ev_ln_a_b, conv_b_w, w["ev_out"], tm=tm, seq=seq)
    late = {"mlp_norm_g0": d_mlp_g0, "ev_norm_g": d_ev_norm, "ev_conv_a_b": d_cab, "ev_ln_a_g": d_ev_lng,
            "ev_ln_a_b": d_ev_lnb, "ev_conv_a_w": d_caw, "ev_conv_b_w": d_cbw}
    share_late = [_ShareAll(list(late.values()))] if distributed else []
    routs2 = big(n0, dz, "ev_in", True, riders=share("ev_out") + share("w1_0") + share_late)
    took(routs2, ("ev_out", "took_share"), ("w1_0", "took_share"))
    beside("ev_in")
    own = {**early, **middle, **late}
    landed = dict(zip(own.keys(), landed_early + landed_middle + routs2[2])) if distributed else None
    return dx, red, own, landed


def _rows128(a):
    rows = jnp.reshape(a, (-1, LANES))
    pad = (-rows.shape[0]) % SUBLANES
    return jnp.pad(rows, ((0, pad), (0, 0))) if pad else rows


def _pack(arrays):
    return jnp.concatenate([_rows128(a) for a in arrays], axis=0)


def _unpack(buf, shapes):
    out, r0 = [], 0
    for shp in shapes:
        size = 1
        for dim in shp:
            size *= dim
        nr = size // LANES
        out.append(jnp.reshape(buf[r0:r0 + nr], shp))
        r0 += nr + (-nr) % SUBLANES
    return out


def kernel(x, ev_norm_g, ev_w_in, ev_conv_a_w, ev_conv_a_b, ev_ln_a_g, ev_ln_a_b, ev_conv_b_w, ev_w_out, od_norm_g, od_w_in, od_b_in, od_ln_v_g, od_ln_v_b, od_w_s, od_b_s, od_w_out, mlp_norm_g, mlp_w1, mlp_w2, final_norm_g, loss_target, m_ev_norm_g, m_ev_w_in, m_ev_conv_a_w, m_ev_conv_a_b, m_ev_ln_a_g, m_ev_ln_a_b, m_ev_conv_b_w, m_ev_w_out, m_od_norm_g, m_od_w_in, m_od_b_in, m_od_ln_v_g, m_od_ln_v_b, m_od_w_s, m_od_b_s, m_od_w_out, m_mlp_norm_g, m_mlp_w1, m_mlp_w2, m_final_norm_g, v_ev_norm_g, v_ev_w_in, v_ev_conv_a_w, v_ev_conv_a_b, v_ev_ln_a_g, v_ev_ln_a_b, v_ev_conv_b_w, v_ev_w_out, v_od_norm_g, v_od_w_in, v_od_b_in, v_od_ln_v_g, v_od_ln_v_b, v_od_w_s, v_od_b_s, v_od_w_out, v_mlp_norm_g, v_mlp_w1, v_mlp_w2, v_final_norm_g):
    tm = TOKEN_TILE
    batch, seq, d = x.shape
    tokens = batch * seq
    x2 = jnp.reshape(x, (tokens, d))
    tgt2 = jnp.reshape(loss_target, (tokens, d))
    chip = 2 * lax.axis_index("x") + lax.axis_index("y")

    small_shapes = [(A_CONV_WIDTH, LANES), (B_CONV_WIDTH, LANES), (256,), (512,), (256,), (256,)]
    small_shard = _pack([ev_conv_a_w[0], ev_conv_b_w[0], od_norm_g[0], od_b_in[0], od_ln_v_g[0], od_ln_v_b[0]])
    small_shard = jnp.pad(small_shard, ((0, (-small_shard.shape[0]) % (2 * SUBLANES)), (0, 0)))
    first = [_place_shard(ev_w_in, 0, BF16, "place_ev_w_in"), _place_shard(ev_w_out, 0, BF16, "place_ev_w_out"),
             _place_shard(small_shard[None], 0, F32, "place_small")]
    staged = {
        "w1_0": _place_shard(mlp_w1, 0, BF16, "place_w1_0"), "w2_0": _place_shard(mlp_w2, 0, BF16, "place_w2_0"),
        "od_in": _place_shard(od_w_in, 0, BF16, "place_od_w_in"), "od_out": _place_shard(od_w_out, 0, BF16, "place_od_w_out"),
        "w1_1": _place_shard(mlp_w1, 1, BF16, "place_w1_1"), "w2_1": _place_shard(mlp_w2, 1, BF16, "place_w2_1"),
    }
    first = [_in_hbm(a) for a in first]
    staged = {nm: _in_hbm(a) for nm, a in staged.items()}
    g_ev_in, g_ev_out, g_small = _gather_beside(first, "gather_stage0", collective_id=1)
    gathered = {"ev_in": g_ev_in, "ev_out": g_ev_out}
    for stage, names in enumerate((("w1_0", "w2_0"), ("od_in", "od_out", "w1_1"), ("w2_1",))):
        done = _gather_beside([staged[nm] for nm in names], f"gather_stage{stage + 1}", collective_id=stage + 2)
        gathered.update(zip(names, done))
    small_all = jnp.reshape(_plain_copy(g_small, "small_weights_copy"), (N_CHIPS, -1, LANES))
    per_chip = [_unpack(small_all[q], small_shapes) for q in range(N_CHIPS)]
    conv_a_w = jnp.concatenate([pc[0] for pc in per_chip], axis=1)
    conv_b_w = jnp.concatenate([pc[1] for pc in per_chip], axis=1)
    od_norm = jnp.concatenate([pc[2] for pc in per_chip])[None, :]
    od_bias = jnp.concatenate([pc[3] for pc in per_chip])[None, :]
    od_lng = jnp.concatenate([pc[4] for pc in per_chip])[None, :]
    od_lnb = jnp.concatenate([pc[5] for pc in per_chip])[None, :]

    dx, red, own, landed = _forward_backward(
        x2, tgt2, gathered, {}, conv_a_w, conv_b_w, od_norm, od_bias, od_lng, od_lnb,
        ev_norm_g, ev_conv_a_b, ev_ln_a_g, ev_ln_a_b, od_w_s, od_b_s, mlp_norm_g, final_norm_g, tm=tm, seq=seq)

    routs = _exchange([red["ev_in"].pair_share()], "reduce_tail")
    red["ev_in"].took_share(routs[0])

    given = {"ev_norm_g": (ev_norm_g, m_ev_norm_g, v_ev_norm_g), "ev_conv_a_b": (ev_conv_a_b, m_ev_conv_a_b, v_ev_conv_a_b),
             "ev_ln_a_g": (ev_ln_a_g, m_ev_ln_a_g, v_ev_ln_a_g), "ev_ln_a_b": (ev_ln_a_b, m_ev_ln_a_b, v_ev_ln_a_b),
             "od_w_s": (od_w_s, m_od_w_s, v_od_w_s), "od_b_s": (od_b_s, m_od_b_s, v_od_b_s),
             "mlp_norm_g": (mlp_norm_g, m_mlp_norm_g, v_mlp_norm_g), "final_norm_g": (final_norm_g, m_final_norm_g, v_final_norm_g),
             "ev_conv_a_w": (ev_conv_a_w, m_ev_conv_a_w, v_ev_conv_a_w), "ev_conv_b_w": (ev_conv_b_w, m_ev_conv_b_w, v_ev_conv_b_w),
             "od_norm_g": (od_norm_g, m_od_norm_g, v_od_norm_g), "od_b_in": (od_b_in, m_od_b_in, v_od_b_in),
             "od_ln_v_g": (od_ln_v_g, m_od_ln_v_g, v_od_ln_v_g), "od_ln_v_b": (od_ln_v_b, m_od_ln_v_b, v_od_ln_v_b)}
    shaped = {nm: tuple(jnp.reshape(a, shape) for a in given[nm]) for nm, shape, _, _ in SMALL_WEIGHTS}
    loss11, small_upd = _small_update(own, landed, shaped)
    loss = loss11[0, 0]
    upd = {nm: [jnp.reshape(o, given[nm][0].shape) for o in outs] for nm, outs in small_upd.items()}

    def big_update(wt, m, v, names, call):
        grads = [red[nm].reduced() for nm in names]
        shp3 = (len(grads),) + grads[0].shape
        outs, _ = _adamw(jnp.reshape(wt, shp3), jnp.reshape(m, shp3), jnp.reshape(v, shp3), grads, call)
        return [jnp.reshape(o, wt.shape) for o in outs], None

    upd["mlp_w2"], _ = big_update(mlp_w2, m_mlp_w2, v_mlp_w2, ["w2_0", "w2_1"], "adamw_mlp_w2")
    upd["mlp_w1"], _ = big_update(mlp_w1, m_mlp_w1, v_mlp_w1, ["w1_0", "w1_1"], "adamw_mlp_w1")
    upd["ev_w_in"], _ = big_update(ev_w_in, m_ev_w_in, v_ev_w_in, ["ev_in"], "adamw_ev_w_in")
    upd["ev_w_out"], _ = big_update(ev_w_out, m_ev_w_out, v_ev_w_out, ["ev_out"], "adamw_ev_w_out")
    upd["od_w_in"], _ = big_update(od_w_in, m_od_w_in, v_od_w_in, ["od_in"], "adamw_od_w_in")
    upd["od_w_out"], _ = big_update(od_w_out, m_od_w_out, v_od_w_out, ["od_out"], "adamw_od_w_out")

    order = ["ev_norm_g", "ev_w_in", "ev_conv_a_w", "ev_conv_a_b", "ev_ln_a_g", "ev_ln_a_b", "ev_conv_b_w", "ev_w_out",
             "od_norm_g", "od_w_in", "od_b_in", "od_ln_v_g", "od_ln_v_b", "od_w_s", "od_b_s", "od_w_out", "mlp_norm_g",
             "mlp_w1", "mlp_w2", "final_norm_g"]
    grad_x = jnp.reshape(dx, x.shape)
    return (loss, grad_x, *[upd[nm][0] for nm in order], *[upd[nm][1] for nm in order],
            *[upd[nm][2] for nm in order], *[upd[nm][3] for nm in order])
```

```python
import functools

import jax
import jax.numpy as jnp
from jax import lax
from jax.experimental import pallas as pl
from jax.experimental.pallas import tpu as pltpu
from jax.experimental.pallas import tpu_sc as plsc

F32 = jnp.float32
BF16 = jnp.bfloat16

D_MODEL = 1024
A_DIM = 512
B_DIM = 512
IN_EVEN = 2 * A_DIM + 3 * B_DIM
A_CONV_WIDTH = 31
B_CONV_WIDTH = 3
CHUNK = 128
C_GROUPS = 8
C_DIM = 1024
D_FF = 4096
RMS_EPS = 1e-6
LN_EPS = 1e-5
ADAM_LR = 0.001
ADAM_B1 = 0.9
ADAM_B2 = 0.999
ADAM_EPS = 1e-08
ADAM_WD = 0.01
ADAM_STEP = 10

N_CHIPS = 4
N_DEV = 8
TOKEN_TILE = 512
A_HALO = 32
B_HALO = 8
CONV_ROWS = 16
DW_TAPS = 4
ELEM_ROWS = 16
PAIR = 2 * CHUNK
LANES = 128
SUBLANES = 8
MXU_ROWS = 256
MIB = 1024 * 1024
MESH = pl.DeviceIdType.MESH
ANY = pl.BlockSpec(memory_space=pl.ANY)


def _dot(a, b):
    return lax.dot_general(a, b, (((1,), (0,)), ((), ())), preferred_element_type=F32)


def _dot_nt(a, b):
    return lax.dot_general(a, b, (((1,), (1,)), ((), ())), preferred_element_type=F32)


def _dot_tn(a, b):
    return lax.dot_general(a, b, (((0,), (0,)), ((), ())), preferred_element_type=F32)


def _params(vmem_mib, n_axes=1):
    return pltpu.CompilerParams(dimension_semantics=("arbitrary",) * n_axes, vmem_limit_bytes=vmem_mib * MIB)


def _row_spec(tm, cols, rev_nt=None):
    if rev_nt is None:
        return pl.BlockSpec((tm, cols), lambda i: (i, 0))
    return pl.BlockSpec((tm, cols), lambda i: (rev_nt - 1 - i, 0))


def _full_spec(shape):
    nd = len(shape)
    return pl.BlockSpec(shape, lambda i: (0,) * nd)


def _block_rows(rows, cap=512):
    best = SUBLANES
    for br in range(SUBLANES, min(rows, cap) + 1, SUBLANES):
        if rows % br == 0:
            best = br
    return best


FIRST_SWAP_ID = 5
N_LOADS = 2 * 2 * N_CHIPS


def _load_weights(loads, sems):
    @pl.when(pl.program_id(0) == 0)
    def _():
        copies = []
        for src, dst, rows_of_one in loads:
            r = src.shape[2]
            for q in range(N_CHIPS):
                for h in range(2):
                    part = dst.at[pl.ds((2 * q + h) * r, r)] if rows_of_one else dst.at[q, pl.ds(h * r, r)]
                    copies.append(pltpu.make_async_copy(src.at[q, h], part, sems.at[len(copies)]))
        for cp in copies:
            cp.start()
        for cp in copies:
            cp.wait()


def _rms_fwd(x, g):
    rstd = lax.rsqrt(jnp.mean(x * x, axis=-1, keepdims=True) + RMS_EPS)
    return x * rstd * g, rstd


def _rms_bwd(dn, x, rstd, g):
    a = dn * g
    xh = x * rstd
    dx = rstd * (a - xh * jnp.mean(a * xh, axis=-1, keepdims=True))
    dg = jnp.sum(dn * xh, axis=0, keepdims=True)
    return dx, dg


def _ln_stats(v):
    mu = jnp.mean(v, axis=-1, keepdims=True)
    xc = v - mu
    rs = lax.rsqrt(jnp.mean(xc * xc, axis=-1, keepdims=True) + LN_EPS)
    return xc * rs, rs


def _ln_bwd(dy, xhat, rs, g):
    dxh = dy * g
    dv = rs * (dxh - jnp.mean(dxh, axis=-1, keepdims=True) - xhat * jnp.mean(dxh * xhat, axis=-1, keepdims=True))
    return dv, jnp.sum(dy * xhat, axis=0, keepdims=True), jnp.sum(dy, axis=0, keepdims=True)


def _gelu_cdf(s):
    return 0.5 * (1.0 + lax.erf(s * 0.7071067811865476))


def _mesh_pos():
    return lax.axis_index("x"), lax.axis_index("y"), lax.axis_index("c")


def _other_chips(x, y):
    return [(1 - x, y), (x, 1 - y), (1 - x, 1 - y)]


def _remote(src, dst, send_sem, recv_sem, to):
    return pltpu.make_async_remote_copy(src_ref=src, dst_ref=dst, send_sem=send_sem, recv_sem=recv_sem,
                                        device_id=to, device_id_type=MESH)


def _like(arrays):
    return [jax.ShapeDtypeStruct(a.shape, a.dtype) for a in arrays]


class _Gather:
    def __init__(self, bufs):
        self.ins = list(bufs)
        self.out_shapes = _like(bufs)
        self.aliases = {t: t for t in range(len(bufs))}
        self.n_sems = 6 * len(bufs)

    def _ici(self, ins, outs, send, recv, t, k, chip, mine, c):
        return _remote(ins[t].at[mine, c], outs[t].at[mine, c], send.at[6 * t + k], recv.at[6 * t + k], (*chip, c))

    def start(self, ins, outs, send, recv):
        x, y, c = _mesh_pos()
        for t in range(len(ins)):
            for k, chip in enumerate(_other_chips(x, y)):
                self._ici(ins, outs, send, recv, t, k, chip, 2 * x + y, c).start()

    def _pass_on(self, outs, send, recv, t, k, chip, c, to):
        blk = outs[t].at[2 * chip[0] + chip[1], c]
        return _remote(blk, blk, send.at[6 * t + 3 + k], recv.at[6 * t + 3 + k], to)

    def near_end(self, ins, outs, send, recv):
        x, y, c = _mesh_pos()
        for t in range(len(ins)):
            for k, chip in enumerate(_other_chips(x, y)):
                blk = outs[t].at[2 * chip[0] + chip[1], c]
                _remote(blk, blk, send.at[6 * t + k], recv.at[6 * t + k], (x, y, c)).wait_recv()
                self._pass_on(outs, send, recv, t, k, chip, c, (x, y, 1 - c)).start()

    def finish(self, ins, outs, send, recv):
        x, y, c = _mesh_pos()
        chips = _other_chips(x, y)
        for t in range(len(ins)):
            for k, chip in enumerate(chips):
                self._pass_on(outs, send, recv, t, k, chip, 1 - c, (x, y, c)).wait_recv()
        for t in range(len(ins)):
            for k, chip in enumerate(chips):
                self._ici(ins, outs, send, recv, t, k, chip, 2 * x + y, c).wait_send()
                self._pass_on(outs, send, recv, t, k, chip, c, (x, y, 1 - c)).wait_send()


class _PairSwap:
    def __init__(self, grads):
        self.ins = list(grads)
        self.out_shapes = [jax.ShapeDtypeStruct((g.shape[0],) + g.shape[2:], g.dtype) for g in grads]
        self.aliases = {}
        self.n_sems = len(grads)

    def _copies(self, ins, outs, send, recv):
        x, y, c = _mesh_pos()
        return [_remote(ins[t].at[:, 1 - c], outs[t], send.at[t], recv.at[t], (x, y, 1 - c)) for t in range(len(ins))]

    def start(self, ins, outs, send, recv):
        for cp in self._copies(ins, outs, send, recv):
            cp.start()

    def finish(self, ins, outs, send, recv):
        for cp in self._copies(ins, outs, send, recv):
            cp.wait()


class _ChipSwap:
    def __init__(self, parts):
        self.ins = list(parts)
        self.out_shapes = [jax.ShapeDtypeStruct((3,) + p.shape[1:], p.dtype) for p in parts]
        self.aliases = {}
        self.n_sems = 3 * len(parts)

    def _copies(self, ins, outs, send, recv):
        x, y, c = _mesh_pos()
        return [_remote(ins[t].at[2 * chip[0] + chip[1]], outs[t].at[k], send.at[3 * t + k], recv.at[3 * t + k], (*chip, c))
                for t in range(len(ins)) for k, chip in enumerate(_other_chips(x, y))]

    def start(self, ins, outs, send, recv):
        for cp in self._copies(ins, outs, send, recv):
            cp.start()

    def finish(self, ins, outs, send, recv):
        for cp in self._copies(ins, outs, send, recv):
            cp.wait()


class _PairShare:
    def __init__(self, fulls):
        self.ins = list(fulls)
        self.out_shapes = _like(fulls)
        self.aliases = {t: t for t in range(len(fulls))}
        self.n_sems = len(fulls)

    def _copies(self, ins, outs, send, recv):
        x, y, c = _mesh_pos()
        return [_remote(ins[t].at[c], outs[t].at[c], send.at[t], recv.at[t], (x, y, 1 - c)) for t in range(len(ins))]

    def start(self, ins, outs, send, recv):
        for cp in self._copies(ins, outs, send, recv):
            cp.start()

    def finish(self, ins, outs, send, recv):
        for cp in self._copies(ins, outs, send, recv):
            cp.wait()


class _ShareAll:
    def __init__(self, arrays):
        self.ins = list(arrays)
        self.out_shapes = [jax.ShapeDtypeStruct((N_DEV,) + a.shape, a.dtype) for a in arrays]
        self.aliases = {}
        self.n_sems = (N_DEV - 1) * len(arrays)

    def _peers(self):
        x, y, c = _mesh_pos()
        flips = [((r >> 2) & 1, (r >> 1) & 1, r & 1) for r in range(1, N_DEV)]
        return (x, y, c), [(x ^ fx, y ^ fy, c ^ fc) for fx, fy, fc in flips]

    def _sends(self, ins, outs, send, recv):
        (x, y, c), peers = self._peers()
        mine = 4 * x + 2 * y + c
        return [_remote(ins[a], outs[a].at[mine], send.at[7 * a + r], recv.at[7 * a + r], peer)
                for a in range(len(ins)) for r, peer in enumerate(peers)]

    def start(self, ins, outs, send, recv):
        for cp in self._sends(ins, outs, send, recv):
            cp.start()

    def finish(self, ins, outs, send, recv):
        (x, y, c), peers = self._peers()
        for a in range(len(ins)):
            for r, (px, py, pc) in enumerate(peers):
                blk = outs[a].at[4 * px + 2 * py + pc]
                _remote(blk, blk, send.at[7 * a + r], recv.at[7 * a + r], (x, y, c)).wait_recv()
        for cp in self._sends(ins, outs, send, recv):
            cp.wait_send()


def _gather_beside(bufs, name, collective_id):
    n = len(bufs)
    refs = [jax.new_ref(b, memory_space=pltpu.MemorySpace.HBM) for b in bufs]
    gather = _Gather(bufs)

    @pl.kernel(mesh=plsc.ScalarSubcoreMesh(axis_name="sequencer", num_cores=1), name=name,
               scratch_types=(pltpu.SemaphoreType.DMA((6 * n,)), pltpu.SemaphoreType.DMA((6 * n,))),
               compiler_params=pltpu.CompilerParams(collective_id=collective_id))
    def launch(send, recv):
        x, y, c = _mesh_pos()
        barrier = pltpu.get_barrier_semaphore()
        peers = [(*chip, c) for chip in _other_chips(x, y)] + [(x, y, 1 - c)]
        for peer in peers:
            pl.semaphore_signal(barrier, inc=1, device_id=peer, device_id_type=MESH)
        pl.semaphore_wait(barrier, len(peers))
        gather.start(refs, refs, send, recv)
        gather.near_end(refs, refs, send, recv)
        gather.finish(refs, refs, send, recv)

    launch()
    return [r[...] for r in refs]


def _chip_swap_beside(parts, name, collective_id):
    src = jax.new_ref(parts, memory_space=pltpu.MemorySpace.HBM)
    dst = jax.empty_ref(jax.ShapeDtypeStruct((N_CHIPS - 1,) + parts.shape[1:], parts.dtype),
                        memory_space=pltpu.MemorySpace.HBM)
    swap = _ChipSwap([parts])

    @pl.kernel(mesh=plsc.ScalarSubcoreMesh(axis_name="sequencer", num_cores=1), name=name,
               scratch_types=(pltpu.SemaphoreType.DMA((N_CHIPS - 1,)), pltpu.SemaphoreType.DMA((N_CHIPS - 1,))),
               compiler_params=pltpu.CompilerParams(collective_id=collective_id))
    def launch(send, recv):
        x, y, c = _mesh_pos()
        barrier = pltpu.get_barrier_semaphore()
        peers = [(*chip, c) for chip in _other_chips(x, y)]
        for peer in peers:
            pl.semaphore_signal(barrier, inc=1, device_id=peer, device_id_type=MESH)
        pl.semaphore_wait(barrier, len(peers))
        swap.start([src], [dst], send, recv)
        swap.finish([src], [dst], send, recv)

    launch()
    return dst[...]


def _pallas(body, operands, *, name, grid, in_specs, out_specs, out_shape, scratch_shapes=(), vmem_mib=32, riders=(),
            prefetch=None):
    in_specs, out_specs, out_shape, scratch_shapes = list(in_specs), list(out_specs), list(out_shape), list(scratch_shapes)
    if not riders and prefetch is None:
        outs = pl.pallas_call(body, name=name, grid=grid, in_specs=in_specs, out_specs=out_specs, out_shape=out_shape,
                              scratch_shapes=scratch_shapes, compiler_params=_params(vmem_mib, len(grid)))(*operands)
        return list(outs), []
    n_in, n_out, n_scr = len(in_specs), len(out_specs), len(scratch_shapes)
    r_in = [len(r.ins) for r in riders]
    r_out = [len(r.out_shapes) for r in riders]
    steps = 1
    for g in grid:
        steps *= g

    n_pre = 0 if prefetch is None else 1

    def wrapped(*refs):
        refs = list(refs)
        pre, refs = refs[:n_pre], refs[n_pre:]
        ins, refs = refs[:n_in], refs[n_in:]
        rins = []
        for k in r_in:
            rins.append(refs[:k])
            refs = refs[k:]
        outs, refs = refs[:n_out], refs[n_out:]
        routs = []
        for k in r_out:
            routs.append(refs[:k])
            refs = refs[k:]
        scr, sems = refs[:n_scr], refs[n_scr:]
        step = 0
        for ax, g in enumerate(grid):
            step = step * g + pl.program_id(ax)

        def each(what):
            for j, r in enumerate(riders):
                if hasattr(r, what):
                    getattr(r, what)(rins[j], routs[j], sems[2 * j], sems[2 * j + 1])

        if grid:
            pl.when(step == 0)(lambda: each("start"))
        else:
            each("start")
        body(*pre, *ins, *outs, *scr)
        if grid:
            @pl.when(step == steps - 1)
            def _():
                each("near_end")
                each("finish")
        else:
            each("near_end")
            each("finish")

    aliases, off_in, off_out = {}, n_pre + n_in, n_out
    for r, ki, ko in zip(riders, r_in, r_out):
        for i, o in r.aliases.items():
            aliases[off_in + i] = off_out + o
        off_in, off_out = off_in + ki, off_out + ko
    sems = []
    for r in riders:
        sems += [pltpu.SemaphoreType.DMA((r.n_sems,)), pltpu.SemaphoreType.DMA((r.n_sems,))]
    layout = dict(grid=grid, in_specs=in_specs + [ANY] * sum(r_in), out_specs=out_specs + [ANY] * sum(r_out),
                  scratch_shapes=scratch_shapes + sems)
    if prefetch is not None:
        layout = dict(grid_spec=pltpu.PrefetchScalarGridSpec(num_scalar_prefetch=1, **layout))
    res = pl.pallas_call(
        wrapped, name=name, **layout,
        out_shape=out_shape + [s for r in riders for s in r.out_shapes], input_output_aliases=aliases,
        compiler_params=pltpu.CompilerParams(dimension_semantics=("arbitrary",) * len(grid),
                                             vmem_limit_bytes=vmem_mib * MIB, has_side_effects=True),
    )(*([] if prefetch is None else [prefetch]), *operands, *[a for r in riders for a in r.ins])
    res = list(res)
    outs, res = res[:n_out], res[n_out:]
    routs = []
    for k in r_out:
        routs.append(res[:k])
        res = res[k:]
    return outs, routs


def _exchange(riders, name):
    return _pallas(lambda: None, [], name=name, grid=(), in_specs=[], out_specs=[], out_shape=[], riders=riders)[1]


def _in_hbm(a):
    return pltpu.with_memory_space_constraint(a, pltpu.HBM)


def _place_shard(w, layer, dtype, name):
    _, rows, cols = w.shape
    half = rows // 2
    br = _block_rows(half)
    nb = half // br
    mine = 2 * lax.axis_index("x") + lax.axis_index("y")

    def body(q_ref, w_ref, o_ref):
        o_ref[...] = w_ref[...].astype(dtype)

    return pl.pallas_call(
        body, name=name,
        grid_spec=pltpu.PrefetchScalarGridSpec(
            num_scalar_prefetch=1, grid=(2, nb),
            in_specs=[pl.BlockSpec((None, br, cols), lambda h, i, q: (layer, h * nb + i, 0))],
            out_specs=pl.BlockSpec((None, None, br, cols), lambda h, i, q: (q[0], h, i, 0))),
        out_shape=pltpu.HBM((N_CHIPS, 2, half, cols), dtype),
        compiler_params=_params(16, 2),
    )(jnp.reshape(mine, (1,)).astype(jnp.int32), _in_hbm(w))


def _plain_copy(a, name):
    def body(a_ref, o_ref):
        o_ref[...] = a_ref[...]

    vmem = pl.BlockSpec(memory_space=pltpu.VMEM)
    return pl.pallas_call(body, name=name, in_specs=[vmem], out_specs=vmem,
                          out_shape=jax.ShapeDtypeStruct(a.shape, a.dtype))(a)


def _add_pair(g, recv, name):
    _, _, r, cdim = g.shape
    br = _block_rows(r, 256)
    c = lax.axis_index("c")

    def body(c_ref, g_ref, r_ref, o_ref):
        o_ref[...] = (g_ref[...] + r_ref[...]).astype(BF16)

    return pl.pallas_call(
        body, name=name,
        grid_spec=pltpu.PrefetchScalarGridSpec(
            num_scalar_prefetch=1, grid=(N_CHIPS, r // br),
            in_specs=[pl.BlockSpec((None, None, br, cdim), lambda q, i, c_ref: (q, c_ref[0], i, 0)),
                      pl.BlockSpec((None, br, cdim), lambda q, i, c_ref: (q, i, 0))],
            out_specs=pl.BlockSpec((None, br, cdim), lambda q, i, c_ref: (q, i, 0))),
        out_shape=pltpu.HBM((N_CHIPS, r, cdim), BF16),
        compiler_params=_params(16, 2),
    )(jnp.reshape(c, (1,)).astype(jnp.int32), _in_hbm(g), _in_hbm(recv))


def _add_chips(own, recv, name):
    _, r, cdim = own.shape
    br = _block_rows(r, 256)
    x, y, c = _mesh_pos()

    def body(pos_ref, own_ref, r_ref, o_ref):
        acc = own_ref[...].astype(F32)
        for k in range(3):
            acc = acc + r_ref[k].astype(F32)
        o_ref[...] = acc

    return pl.pallas_call(
        body, name=name,
        grid_spec=pltpu.PrefetchScalarGridSpec(
            num_scalar_prefetch=1, grid=(r // br,),
            in_specs=[pl.BlockSpec((None, br, cdim), lambda i, pos: (pos[0], i, 0)),
                      pl.BlockSpec((3, br, cdim), lambda i, pos: (0, i, 0))],
            out_specs=pl.BlockSpec((None, br, cdim), lambda i, pos: (pos[1], i, 0))),
        out_shape=pltpu.HBM((2, r, cdim), F32),
        compiler_params=_params(16, 1),
    )(jnp.stack([2 * x + y, c]).astype(jnp.int32), _in_hbm(own), recv)


def _adam_math(w, m, v, g):
    c1 = 1.0 / (1.0 - ADAM_B1 ** ADAM_STEP)
    c2 = 1.0 / (1.0 - ADAM_B2 ** ADAM_STEP)
    m_new = ADAM_B1 * m + (1.0 - ADAM_B1) * g
    v_new = ADAM_B2 * v + (1.0 - ADAM_B2) * (g * g)
    return -ADAM_LR * ((m_new * c1) / (jnp.sqrt(v_new * c2) + ADAM_EPS) + ADAM_WD * w), m_new, v_new


SMALL_WEIGHTS = [
    ("ev_norm_g", (1, D_MODEL), ["ev_norm_g"], None), ("ev_conv_a_b", (1, A_DIM), ["ev_conv_a_b"], None),
    ("ev_ln_a_g", (1, A_DIM), ["ev_ln_a_g"], None), ("ev_ln_a_b", (1, A_DIM), ["ev_ln_a_b"], None),
    ("od_w_s", (C_GROUPS, CHUNK, CHUNK), ["od_w_s_lo", "od_w_s_hi"], None), ("od_b_s", (C_GROUPS, CHUNK), ["od_b_s"], None),
    ("mlp_norm_g", (2, D_MODEL), ["mlp_norm_g0", "mlp_norm_g1"], None), ("final_norm_g", (1, D_MODEL), ["final_norm_g"], None),
    ("ev_conv_a_w", (A_CONV_WIDTH, A_DIM // N_CHIPS), ["ev_conv_a_w"], A_DIM // N_CHIPS),
    ("ev_conv_b_w", (B_CONV_WIDTH, B_DIM // N_CHIPS), ["ev_conv_b_w"], B_DIM // N_CHIPS),
    ("od_norm_g", (1, D_MODEL // N_CHIPS), ["od_norm_g"], D_MODEL // N_CHIPS),
    ("od_b_in", (1, 2 * C_DIM // N_CHIPS), ["od_b_in"], 2 * C_DIM // N_CHIPS),
    ("od_ln_v_g", (1, C_DIM // N_CHIPS), ["od_ln_v_g"], C_DIM // N_CHIPS),
    ("od_ln_v_b", (1, C_DIM // N_CHIPS), ["od_ln_v_b"], C_DIM // N_CHIPS),
]


def _small_update(own, landed, weights):
    names = list(own.keys())
    n_g, n_w = len(names), len(SMALL_WEIGHTS)

    def body(*refs):
        refs = list(refs)
        own_refs = dict(zip(names, refs[:n_g]))
        land_refs = dict(zip(names, refs[n_g:2 * n_g]))
        wmv = [refs[2 * n_g + 3 * i:2 * n_g + 3 * i + 3] for i in range(n_w)]
        o0 = 2 * n_g + 3 * n_w
        loss_ref = refs[o0]
        outs = [refs[o0 + 1 + 4 * i:o0 + 5 + 4 * i] for i in range(n_w)]
        acc = dict(zip(names, refs[o0 + 1 + 4 * n_w:]))
        x, y, c = _mesh_pos()
        mine, chip = 4 * x + 2 * y + c, 2 * x + y

        for nm in names:
            for d in range(N_DEV):
                def add(term, nm=nm, d=d):
                    acc[nm][...] = term if d == 0 else acc[nm][...] + term
                pl.when(mine == d)(lambda nm=nm, add=add: add(own_refs[nm][...]))
                pl.when(mine != d)(lambda nm=nm, d=d, add=add: add(land_refs[nm][d]))
        loss_ref[...] = acc["loss"][...]

        def update(i, rows, g):
            w_ref, m_ref, v_ref = wmv[i]
            delta, m_new, v_new = _adam_math(w_ref[rows], m_ref[rows], v_ref[rows], g)
            for ref, val in zip(outs[i], (g, delta, m_new, v_new)):
                ref[rows] = val

        for i, (_, shape, grads, per_chip) in enumerate(SMALL_WEIGHTS):
            for row, gname in enumerate(grads):
                per_grad = shape[0] // len(grads)
                rows = slice(row * per_grad, (row + 1) * per_grad)
                if per_chip is None:
                    update(i, rows, acc[gname][...])
                else:
                    for q in range(N_CHIPS):
                        pl.when(chip == q)(lambda i=i, rows=rows, gname=gname, q=q, per_chip=per_chip:
                                           update(i, rows, acc[gname][:, q * per_chip:(q + 1) * per_chip]))

    operands = [own[nm] for nm in names] + [landed[nm] for nm in names]
    for nm, _, _, _ in SMALL_WEIGHTS:
        operands += list(weights[nm])
    out_shape = [jax.ShapeDtypeStruct((1, 1), F32)]
    for _, shape, _, _ in SMALL_WEIGHTS:
        out_shape += [jax.ShapeDtypeStruct(shape, F32)] * 4
    res = pl.pallas_call(
        body, name="small_update", grid=(1,),
        in_specs=[_full_spec(a.shape) for a in operands], out_specs=[_full_spec(s.shape) for s in out_shape],
        out_shape=out_shape, scratch_shapes=[pltpu.VMEM(own[nm].shape, F32) for nm in names],
        compiler_params=_params(32, 1),
    )(*[_in_hbm(a) for a in operands])
    return res[0], {nm: res[1 + 4 * i:5 + 4 * i] for i, (nm, _, _, _) in enumerate(SMALL_WEIGHTS)}


def _adamw(w, m, v, grads, name, riders=()):
    layers, r, cdim = w.shape
    br = _block_rows(r, 256 if cdim > LANES else 1024)

    def body(*refs):
        w_ref, m_ref, v_ref = refs[:3]
        g_refs = refs[3:3 + layers]
        go_ref, d_ref, mo_ref, vo_ref = refs[3 + layers:]
        layer = pl.program_id(0)
        for l in range(layers):
            @pl.when(layer == l)
            def _(l=l):
                g = g_refs[l][...]
                go_ref[...] = g
                d_ref[...], mo_ref[...], vo_ref[...] = _adam_math(w_ref[...], m_ref[...], v_ref[...], g)

    spec3 = pl.BlockSpec((None, br, cdim), lambda l, i: (l, i, 0))
    spec2 = pl.BlockSpec((br, cdim), lambda l, i: (i, 0))
    out = jax.ShapeDtypeStruct((layers, r, cdim), F32)
    return _pallas(body, [_in_hbm(a) for a in (w, m, v, *grads)], name=name, grid=(layers, r // br),
                   in_specs=[spec3, spec3, spec3] + [spec2] * layers, out_specs=[spec3] * 4, out_shape=[out] * 4,
                   vmem_mib=32, riders=riders)


def _fill_shifted(buf, rows):
    for b in range(1, SUBLANES):
        buf[b, 0:rows - SUBLANES, :] = buf[0, b:b + rows - SUBLANES, :]


def _window(buf, start, size):
    return buf[start % SUBLANES, start - start % SUBLANES:start - start % SUBLANES + size, :]


def _conv31(src, w_ref, r0, base, init):
    acc = init
    for k in range(A_CONV_WIDTH):
        acc = acc + w_ref[k:k + 1, :] * _window(src, base + k + r0, CONV_ROWS)
    return acc


def _fwd_even(x, norm_g, w_in, conv_a_w, conv_a_b, ln_g, ln_b, conv_b_w, w_out, *, tm, seq, riders=()):
    tokens = x.shape[0]
    nt, tps = tokens // tm, seq // tm

    def body(x_ref, g_ref, win_hbm, caw_ref, cab_ref, lng_ref, lnb_ref, cbw_ref, wout_hbm,
             h_ref, n_ref, z_ref, a2_ref, cv_ref, mix_ref, win_v, wout_v, pa, pb, sem):
        i = pl.program_id(0)

        _load_weights([(win_hbm, win_v, False), (wout_hbm, wout_v, True)], sem)

        xv = x_ref[...]
        nf, _ = _rms_fwd(xv, g_ref[...])
        n = nf.astype(BF16)
        n_ref[...] = n
        z = jnp.concatenate([_dot(n, win_v[j]) for j in range(N_CHIPS)], axis=1)
        z_ref[...] = z.astype(BF16)
        a_val, a_gate = z[:, 0:A_DIM], z[:, A_DIM:2 * A_DIM]
        b_gate, c_gate, b_val = z[:, 1024:1536], z[:, 1536:2048], z[:, 2048:2560]

        first = (i % tps) == 0

        @pl.when(first)
        def _():
            pa[0, 0:A_HALO, :] = jnp.zeros((A_HALO, A_DIM), F32)
            pb[0:B_HALO, :] = jnp.zeros((B_HALO, B_DIM), F32)

        @pl.when(jnp.logical_not(first))
        def _():
            pa[0, 0:A_HALO, :] = pa[0, tm:tm + A_HALO, :]
            pb[0:B_HALO, :] = pb[tm:tm + B_HALO, :]

        pa[0, A_HALO:A_HALO + tm, :] = a_val * jax.nn.sigmoid(a_gate)
        pb[B_HALO:B_HALO + tm, :] = c_gate * b_val
        _fill_shifted(pa, A_HALO + tm)
        bias = jnp.broadcast_to(cab_ref[...], (CONV_ROWS, A_DIM))
        for r0 in range(0, tm, CONV_ROWS):
            a2_ref[r0:r0 + CONV_ROWS, :] = _conv31(pa, caw_ref, r0, A_HALO - (A_CONV_WIDTH - 1), bias)
        xhat, _ = _ln_stats(a2_ref[...])
        a3 = xhat * lng_ref[...] + lnb_ref[...]
        a4 = a3 * jax.nn.sigmoid(a3)
        cv = cbw_ref[0:1, :] * pb[B_HALO - 2:B_HALO - 2 + tm, :]
        cv = cv + cbw_ref[1:2, :] * pb[B_HALO - 1:B_HALO - 1 + tm, :]
        cv = cv + cbw_ref[2:3, :] * pb[B_HALO:B_HALO + tm, :]
        cv_ref[...] = cv.astype(BF16)
        mix = jnp.concatenate([a4, b_gate * cv], axis=1).astype(BF16)
        mix_ref[...] = mix
        h_ref[...] = xv + _dot(mix, wout_v[...])

    shp = lambda cols, dt: jax.ShapeDtypeStruct((tokens, cols), dt)
    return _pallas(
        body, [x, norm_g, w_in, conv_a_w, conv_a_b, ln_g, ln_b, conv_b_w, w_out], name="fwd_even", grid=(nt,),
        in_specs=[_row_spec(tm, D_MODEL), _full_spec((1, D_MODEL)), ANY, _full_spec((A_CONV_WIDTH, A_DIM)),
                  _full_spec((1, A_DIM)), _full_spec((1, A_DIM)), _full_spec((1, A_DIM)),
                  _full_spec((B_CONV_WIDTH, B_DIM)), ANY],
        out_specs=[_row_spec(tm, D_MODEL), _row_spec(tm, D_MODEL), _row_spec(tm, IN_EVEN), _row_spec(tm, A_DIM),
                   _row_spec(tm, B_DIM), _row_spec(tm, D_MODEL)],
        out_shape=[shp(D_MODEL, F32), shp(D_MODEL, BF16), shp(IN_EVEN, BF16), shp(A_DIM, F32), shp(B_DIM, BF16),
                   shp(D_MODEL, BF16)],
        scratch_shapes=[pltpu.VMEM((N_CHIPS, D_MODEL, IN_EVEN // N_CHIPS), BF16), pltpu.VMEM((D_MODEL, D_MODEL), BF16),
                        pltpu.VMEM((SUBLANES, A_HALO + tm, A_DIM), F32), pltpu.VMEM((B_HALO + tm, B_DIM), F32),
                        pltpu.SemaphoreType.DMA((N_LOADS,))],
        vmem_mib=56, riders=riders)


def _loss_tail(xv, g, target, loss_ref, dh_ref, dhb_ref, dg_ref):
    @pl.when(pl.program_id(0) == 0)
    def _():
        loss_ref[...] = jnp.zeros((1, 1), F32)
        dg_ref[...] = jnp.zeros((1, D_MODEL), F32)

    out, rstd = _rms_fwd(xv, g)
    err = out - target
    per_token = jnp.sum(err * err, axis=1, keepdims=True) * (1.0 / D_MODEL)
    loss_ref[...] += 0.5 * jnp.sum(per_token, axis=0, keepdims=True)
    dx, dg = _rms_bwd(err * (1.0 / D_MODEL), xv, rstd, g)
    dh_ref[...] = dx
    dhb_ref[...] = dx.astype(BF16)
    dg_ref[...] += dg


def _fwd_mlp(h, norm_g, w1, w2, layer, *, tm, riders=(), head=None):
    tokens = h.shape[0]
    nt = tokens // tm
    fs = D_FF // N_CHIPS
    n_in = 4 if head is None else 6

    def body(*refs):
        h_ref, g_ref, w1_hbm, w2_hbm = refs[:4]
        w1_v, w2_v, sem = refs[-3:]
        outs = refs[n_in:-3]
        n_ref, p_ref, q_ref = outs[1:4] if head is None else outs[0:3]
        _load_weights([(w1_hbm, w1_v, False), (w2_hbm, w2_v, False)], sem)

        xv = h_ref[...]
        nf, _ = _rms_fwd(xv, g_ref[...])
        n = nf.astype(BF16)
        n_ref[...] = n
        acc = xv
        for j in range(N_CHIPS):
            p = _dot(n, w1_v[j])
            p_ref[:, j * fs:(j + 1) * fs] = p.astype(BF16)
            r = jnp.maximum(p, 0.0)
            q = (r * r).astype(BF16)
            q_ref[:, j * fs:(j + 1) * fs] = q
            acc = acc + _dot(q, w2_v[j])
        if head is None:
            outs[0][...] = acc
        else:
            _loss_tail(acc, refs[4][...], refs[5][...], *outs[3:7])

    shp = lambda cols, dt: jax.ShapeDtypeStruct((tokens, cols), dt)
    saved_specs = [_row_spec(tm, D_MODEL), _row_spec(tm, D_FF), _row_spec(tm, D_FF)]
    saved_shapes = [shp(D_MODEL, BF16), shp(D_FF, BF16), shp(D_FF, BF16)]
    if head is None:
        operands, in_specs = [h, norm_g, w1, w2], [_row_spec(tm, D_MODEL), _full_spec((1, D_MODEL)), ANY, ANY]
        out_specs, out_shape = [_row_spec(tm, D_MODEL)] + saved_specs, [shp(D_MODEL, F32)] + saved_shapes
    else:
        operands = [h, norm_g, w1, w2, *head]
        in_specs = [_row_spec(tm, D_MODEL), _full_spec((1, D_MODEL)), ANY, ANY, _full_spec((1, D_MODEL)), _row_spec(tm, D_MODEL)]
        out_specs = saved_specs + [_full_spec((1, 1)), _row_spec(tm, D_MODEL), _row_spec(tm, D_MODEL), _full_spec((1, D_MODEL))]
        out_shape = saved_shapes + [jax.ShapeDtypeStruct((1, 1), F32), shp(D_MODEL, F32), shp(D_MODEL, BF16),
                                    jax.ShapeDtypeStruct((1, D_MODEL), F32)]
    return _pallas(
        body, operands, name=f"fwd_mlp{layer}", grid=(nt,), in_specs=in_specs, out_specs=out_specs, out_shape=out_shape,
        scratch_shapes=[pltpu.VMEM((N_CHIPS, D_MODEL, fs), BF16), pltpu.VMEM((N_CHIPS, fs, D_MODEL), BF16),
                        pltpu.SemaphoreType.DMA((N_LOADS,))],
        vmem_mib=56, riders=riders)


def _tril_mask():
    row = lax.broadcasted_iota(jnp.int32, (CHUNK, CHUNK), 0)
    col = lax.broadcasted_iota(jnp.int32, (CHUNK, CHUNK), 1)
    return row >= col


def _triu_mask():
    row = lax.broadcasted_iota(jnp.int32, (CHUNK, CHUNK), 0)
    col = lax.broadcasted_iota(jnp.int32, (CHUNK, CHUNK), 1)
    return row <= col


def _fwd_odd(h, norm_g, w_in, b_in, ln_g, ln_b, w_s, b_s_rows, w_out, *, tm, riders=()):
    tokens = h.shape[0]
    nt = tokens // tm
    cs = 2 * C_DIM // N_CHIPS

    def body(h_ref, g_ref, win_hbm, bin_ref, lng_ref, lnb_ref, ws_ref, bs_ref, wout_hbm,
             ho_ref, n_ref, s_ref, cdf_ref, sv_ref, y_ref, win_v, wout_v, bd, sem):
        _load_weights([(win_hbm, win_v, False), (wout_hbm, wout_v, True)], sem)

        @pl.when(pl.program_id(0) == 0)
        def _():
            mask = _tril_mask()
            bd[...] = jnp.zeros(bd.shape, BF16)
            for g in range(C_GROUPS):
                w = jnp.where(mask, ws_ref[g], 0.0).astype(BF16)
                bd[g, 0:CHUNK, 0:CHUNK] = w
                bd[g, CHUNK:PAIR, CHUNK:PAIR] = w

        xv = h_ref[...]
        nf, _ = _rms_fwd(xv, g_ref[...])
        n = nf.astype(BF16)
        n_ref[...] = n
        s = jnp.concatenate([_dot(n, win_v[j]) for j in range(N_CHIPS)], axis=1) + bin_ref[...]
        s_ref[...] = s.astype(BF16)
        cdf = _gelu_cdf(s)
        cdf_ref[...] = cdf.astype(BF16)
        zz = s * cdf
        u, v = zz[:, 0:C_DIM], zz[:, C_DIM:2 * C_DIM]
        xhat, _ = _ln_stats(v)
        vn = (xhat * lng_ref[...] + lnb_ref[...]).astype(BF16)
        for g in range(C_GROUPS):
            cols = slice(g * CHUNK, (g + 1) * CHUNK)
            bias = jnp.concatenate([bs_ref[g], bs_ref[g]], axis=0)
            for r0 in range(0, tm, PAIR):
                sv = _dot(bd[g], vn[r0:r0 + PAIR, cols]) + bias
                sv_ref[r0:r0 + PAIR, cols] = sv.astype(BF16)
                y_ref[r0:r0 + PAIR, cols] = (u[r0:r0 + PAIR, cols] * sv).astype(BF16)
        ho_ref[...] = xv + _dot(y_ref[...], wout_v[...])

    shp = lambda cols, dt: jax.ShapeDtypeStruct((tokens, cols), dt)
    return _pallas(
        body, [h, norm_g, w_in, b_in, ln_g, ln_b, w_s, b_s_rows, w_out], name="fwd_odd", grid=(nt,),
        in_specs=[_row_spec(tm, D_MODEL), _full_spec((1, D_MODEL)), ANY, _full_spec((1, 2 * C_DIM)),
                  _full_spec((1, C_DIM)), _full_spec((1, C_DIM)), _full_spec((C_GROUPS, CHUNK, CHUNK)),
                  _full_spec((C_GROUPS, CHUNK, CHUNK)), ANY],
        out_specs=[_row_spec(tm, D_MODEL), _row_spec(tm, D_MODEL), _row_spec(tm, 2 * C_DIM), _row_spec(tm, 2 * C_DIM),
                   _row_spec(tm, C_DIM), _row_spec(tm, C_DIM)],
        out_shape=[shp(D_MODEL, F32), shp(D_MODEL, BF16), shp(2 * C_DIM, BF16), shp(2 * C_DIM, BF16), shp(C_DIM, BF16),
                   shp(C_DIM, BF16)],
        scratch_shapes=[pltpu.VMEM((N_CHIPS, D_MODEL, cs), BF16), pltpu.VMEM((C_DIM, D_MODEL), BF16),
                        pltpu.VMEM((C_GROUPS, PAIR, PAIR), BF16), pltpu.SemaphoreType.DMA((N_LOADS,))],
        vmem_mib=56, riders=riders)


def _bwd_mlp(dh, h, norm_g, p, w1, w2, layer, *, tm, riders=()):
    tokens = h.shape[0]
    nt = tokens // tm
    fs = D_FF // N_CHIPS

    def body(dh_ref, h_ref, g_ref, p_ref, w1_hbm, w2_hbm, dx_ref, dxb_ref, dp_ref, dg_ref, w1_v, w2_v, sem):
        @pl.when(pl.program_id(0) == 0)
        def _():
            dg_ref[...] = jnp.zeros((1, D_MODEL), F32)

        _load_weights([(w1_hbm, w1_v, False), (w2_hbm, w2_v, False)], sem)

        dhv = dh_ref[...]
        dhb = dhv.astype(BF16)
        dn = jnp.zeros((tm, D_MODEL), F32)
        for j in range(N_CHIPS):
            dq = _dot_nt(dhb, w2_v[j])
            r = jnp.maximum(p_ref[:, j * fs:(j + 1) * fs].astype(F32), 0.0)
            dp = ((2.0 * r) * dq).astype(BF16)
            dp_ref[:, j * fs:(j + 1) * fs] = dp
            dn = dn + _dot_nt(dp, w1_v[j])
        xv = h_ref[...]
        g = g_ref[...]
        _, rstd = _rms_fwd(xv, g)
        dx, dg = _rms_bwd(dn, xv, rstd, g)
        dx_ref[...] = dhv + dx
        dxb_ref[...] = (dhv + dx).astype(BF16)
        dg_ref[...] += dg

    return _pallas(
        body, [dh, h, norm_g, p, w1, w2], name=f"bwd_mlp{layer}", grid=(nt,),
        in_specs=[_row_spec(tm, D_MODEL), _row_spec(tm, D_MODEL), _full_spec((1, D_MODEL)), _row_spec(tm, D_FF), ANY, ANY],
        out_specs=[_row_spec(tm, D_MODEL), _row_spec(tm, D_MODEL), _row_spec(tm, D_FF), _full_spec((1, D_MODEL))],
        out_shape=[jax.ShapeDtypeStruct((tokens, D_MODEL), F32), jax.ShapeDtypeStruct((tokens, D_MODEL), BF16),
                   jax.ShapeDtypeStruct((tokens, D_FF), BF16), jax.ShapeDtypeStruct((1, D_MODEL), F32)],
        scratch_shapes=[pltpu.VMEM((N_CHIPS, D_MODEL, fs), BF16), pltpu.VMEM((N_CHIPS, fs, D_MODEL), BF16),
                        pltpu.SemaphoreType.DMA((N_LOADS,))],
        vmem_mib=56, riders=riders)


def _bwd_odd(dh, h, norm_g, s, cdf, sv, w_in, ln_g, ln_b, w_s, w_out, *, tm, riders=()):
    tokens = h.shape[0]
    nt = tokens // tm
    cs = 2 * C_DIM // N_CHIPS

    def body(dh_ref, h_ref, g_ref, s_ref, cdf_ref, sv_ref, win_hbm, lng_ref, lnb_ref, ws_ref, wout_hbm,
             dx_ref, dxb_ref, ds_ref, dg_ref, dbin_ref, dlng_ref, dlnb_ref, dws_ref, dbs_ref,
             win_v, wout_v, bdt, dws_acc, dbs_acc, dvn, sem):
        i = pl.program_id(0)

        _load_weights([(win_hbm, win_v, False), (wout_hbm, wout_v, True)], sem)

        @pl.when(i == 0)
        def _():
            mask_t = _triu_mask()
            bdt[...] = jnp.zeros(bdt.shape, BF16)
            for g in range(C_GROUPS):
                wt = jnp.where(mask_t, ws_ref[g].T, 0.0).astype(BF16)
                bdt[g, 0:CHUNK, 0:CHUNK] = wt
                bdt[g, CHUNK:PAIR, CHUNK:PAIR] = wt
            dws_acc[...] = jnp.zeros(dws_acc.shape, F32)
            dbs_acc[...] = jnp.zeros(dbs_acc.shape, F32)
            dg_ref[...] = jnp.zeros(dg_ref.shape, F32)
            dbin_ref[...] = jnp.zeros(dbin_ref.shape, F32)
            dlng_ref[...] = jnp.zeros(dlng_ref.shape, F32)
            dlnb_ref[...] = jnp.zeros(dlnb_ref.shape, F32)

        dhv = dh_ref[...]
        dy = _dot_nt(dhv.astype(BF16), wout_v[...])
        sf = s_ref[...].astype(F32)
        cdf = cdf_ref[...].astype(F32)
        pdf = jnp.exp(-0.5 * sf * sf) * 0.3989422804014327
        zz = sf * cdf
        dgelu = cdf + sf * pdf
        u, v = zz[:, 0:C_DIM], zz[:, C_DIM:2 * C_DIM]
        xhat, rs = _ln_stats(v)
        lng = lng_ref[...]
        vn = (xhat * lng + lnb_ref[...]).astype(BF16)
        du = dy * sv_ref[...].astype(F32)
        dsv = dy * u
        dsvb = dsv.astype(BF16)
        for g in range(C_GROUPS):
            cols = slice(g * CHUNK, (g + 1) * CHUNK)
            for r0 in range(0, tm, PAIR):
                blk = dsvb[r0:r0 + PAIR, cols]
                dvn[r0:r0 + PAIR, cols] = _dot(bdt[g], blk)
                dws_acc[g] += _dot_nt(blk, vn[r0:r0 + PAIR, cols])
                dbs_acc[g] += dsv[r0:r0 + CHUNK, cols] + dsv[r0 + CHUNK:r0 + PAIR, cols]
        dv, dlng, dlnb = _ln_bwd(dvn[...], xhat, rs, lng)
        dlng_ref[...] += dlng
        dlnb_ref[...] += dlnb
        ds = jnp.concatenate([du, dv], axis=1) * dgelu
        dbin_ref[...] += jnp.sum(ds, axis=0, keepdims=True)
        dsb = ds.astype(BF16)
        ds_ref[...] = dsb
        dn = jnp.zeros((tm, D_MODEL), F32)
        for j in range(N_CHIPS):
            dn = dn + _dot_nt(dsb[:, j * cs:(j + 1) * cs], win_v[j])
        xv = h_ref[...]
        g = g_ref[...]
        _, rstd = _rms_fwd(xv, g)
        dx, dg = _rms_bwd(dn, xv, rstd, g)
        dx_ref[...] = dhv + dx
        dxb_ref[...] = (dhv + dx).astype(BF16)
        dg_ref[...] += dg

        @pl.when(i == nt - 1)
        def _():
            mask = _tril_mask()
            for g in range(C_GROUPS):
                full = dws_acc[g]
                dws_ref[g] = jnp.where(mask, full[0:CHUNK, 0:CHUNK] + full[CHUNK:PAIR, CHUNK:PAIR], 0.0)
                dbs_ref[g:g + 1, :] = jnp.sum(dbs_acc[g].T, axis=0, keepdims=True)

    row = lambda cols: jax.ShapeDtypeStruct((1, cols), F32)
    return _pallas(
        body, [dh, h, norm_g, s, cdf, sv, w_in, ln_g, ln_b, w_s, w_out], name="bwd_odd", grid=(nt,),
        in_specs=[_row_spec(tm, D_MODEL), _row_spec(tm, D_MODEL), _full_spec((1, D_MODEL)), _row_spec(tm, 2 * C_DIM),
                  _row_spec(tm, 2 * C_DIM), _row_spec(tm, C_DIM), ANY, _full_spec((1, C_DIM)), _full_spec((1, C_DIM)),
                  _full_spec((C_GROUPS, CHUNK, CHUNK)), ANY],
        out_specs=[_row_spec(tm, D_MODEL), _row_spec(tm, D_MODEL), _row_spec(tm, 2 * C_DIM), _full_spec((1, D_MODEL)),
                   _full_spec((1, 2 * C_DIM)),
                   _full_spec((1, C_DIM)), _full_spec((1, C_DIM)), _full_spec((C_GROUPS, CHUNK, CHUNK)),
                   _full_spec((C_GROUPS, CHUNK))],
        out_shape=[jax.ShapeDtypeStruct((tokens, D_MODEL), F32), jax.ShapeDtypeStruct((tokens, D_MODEL), BF16),
                   jax.ShapeDtypeStruct((tokens, 2 * C_DIM), BF16),
                   row(D_MODEL), row(2 * C_DIM), row(C_DIM), row(C_DIM),
                   jax.ShapeDtypeStruct((C_GROUPS, CHUNK, CHUNK), F32), jax.ShapeDtypeStruct((C_GROUPS, CHUNK), F32)],
        scratch_shapes=[pltpu.VMEM((N_CHIPS, D_MODEL, cs), BF16), pltpu.VMEM((C_DIM, D_MODEL), BF16),
                        pltpu.VMEM((C_GROUPS, PAIR, PAIR), BF16), pltpu.VMEM((C_GROUPS, PAIR, PAIR), F32),
                        pltpu.VMEM((C_GROUPS, CHUNK, CHUNK), F32), pltpu.VMEM((tm, C_DIM), F32),
                        pltpu.SemaphoreType.DMA((N_LOADS,))],
        vmem_mib=56, riders=riders)


def _bwd_even(dh, x, norm_g, z, a2, cv, w_in, conv_a_w, ln_g, ln_b, conv_b_w, w_out, *, tm, seq, riders=()):
    tokens = x.shape[0]
    nt, tps = tokens // tm, seq // tm
    ws = IN_EVEN // N_CHIPS

    def body(dh_ref, x_ref, g_ref, z_ref, a2_ref, cv_ref, win_hbm, caw_ref, lng_ref, lnb_ref, cbw_ref, wout_hbm,
             dx_ref, dz_ref, dg_ref, dcaw_ref, dcab_ref, dlng_ref, dlnb_ref, dcbw_ref,
             win_v, wout_v, ea, eb, a1s, da1s, sigs, wide, dw_acc, sem):
        i = pl.program_id(0)

        _load_weights([(win_hbm, win_v, False), (wout_hbm, wout_v, True)], sem)

        @pl.when(i == 0)
        def _():
            dw_acc[...] = jnp.zeros(dw_acc.shape, F32)
            for ref in (dg_ref, dcab_ref, dlng_ref, dlnb_ref, dcbw_ref):
                ref[...] = jnp.zeros(ref.shape, F32)

        last = ((nt - 1 - i) % tps) == tps - 1

        @pl.when(last)
        def _():
            ea[0, tm:tm + A_HALO, :] = jnp.zeros((A_HALO, A_DIM), F32)
            eb[tm:tm + B_HALO, :] = jnp.zeros((B_HALO, B_DIM), F32)

        @pl.when(jnp.logical_not(last))
        def _():
            ea[0, tm:tm + A_HALO, :] = ea[0, 0:A_HALO, :]
            eb[tm:tm + B_HALO, :] = eb[0:B_HALO, :]

        wide[...] = _dot_nt(dh_ref[...].astype(BF16), wout_v[...])
        lng, lnb = lng_ref[...], lnb_ref[...]
        zero_row = jnp.zeros((1, A_DIM), F32)
        dlng, dlnb, dcab = zero_row, zero_row, zero_row
        for r0 in range(0, tm, ELEM_ROWS):
            rows = slice(r0, r0 + ELEM_ROWS)
            a_val, a_gate = z_ref[rows, 0:A_DIM].astype(F32), z_ref[rows, A_DIM:2 * A_DIM].astype(F32)
            xhat, rs = _ln_stats(a2_ref[rows, :])
            a3 = xhat * lng + lnb
            sg = jax.nn.sigmoid(a3)
            da3 = wide[rows, 0:A_DIM] * (sg * (1.0 + a3 * (1.0 - sg)))
            da2, g_part, b_part = _ln_bwd(da3, xhat, rs, lng)
            dlng, dlnb, dcab = dlng + g_part, dlnb + b_part, dcab + jnp.sum(da2, axis=0, keepdims=True)
            ea[0, rows, :] = da2
            eb[rows, :] = wide[rows, A_DIM:A_DIM + B_DIM] * z_ref[rows, 1024:1536].astype(F32)
            sig = jax.nn.sigmoid(a_gate)
            sigs[rows, :] = sig
            a1s[rows, :] = a_val * sig
        dlng_ref[...] += dlng
        dlnb_ref[...] += dlnb
        dcab_ref[...] += dcab
        _fill_shifted(ea, tm + A_HALO)
        for r0 in range(0, tm, CONV_ROWS):
            acc = jnp.zeros((CONV_ROWS, A_DIM), F32)
            for j in range(A_CONV_WIDTH):
                acc = acc + caw_ref[A_CONV_WIDTH - 1 - j:A_CONV_WIDTH - j, :] * _window(ea, r0 + j, CONV_ROWS)
            da1s[r0:r0 + CONV_ROWS, :] = acc
        for j0 in range(0, A_CONV_WIDTH, DW_TAPS):
            taps = range(j0, min(j0 + DW_TAPS, A_CONV_WIDTH))
            part = [jnp.zeros((CONV_ROWS, A_DIM), F32) for _ in taps]
            for r0 in range(0, tm, CONV_ROWS):
                a1c = a1s[r0:r0 + CONV_ROWS, :]
                for u, j in enumerate(taps):
                    part[u] = part[u] + _window(ea, r0 + j, CONV_ROWS) * a1c
            for u, j in enumerate(taps):
                dw_acc[A_CONV_WIDTH - 1 - j] += part[u]
        dcbw = [jnp.zeros((1, B_DIM), F32) for _ in range(B_CONV_WIDTH)]
        for r0 in range(0, tm, ELEM_ROWS):
            rows = slice(r0, r0 + ELEM_ROWS)
            da1, sig = da1s[rows, :], sigs[rows, :]
            dz_ref[rows, 0:A_DIM] = (da1 * sig).astype(BF16)
            dz_ref[rows, A_DIM:2 * A_DIM] = (da1 * z_ref[rows, 0:A_DIM].astype(F32) * (sig * (1.0 - sig))).astype(BF16)
            c_gate, b_val = z_ref[rows, 1536:2048].astype(F32), z_ref[rows, 2048:2560].astype(F32)
            dz_ref[rows, 1024:1536] = (wide[rows, A_DIM:A_DIM + B_DIM] * cv_ref[rows, :].astype(F32)).astype(BF16)
            cb = c_gate * b_val
            dcb = jnp.zeros((ELEM_ROWS, B_DIM), F32)
            for j in range(B_CONV_WIDTH):
                k = B_CONV_WIDTH - 1 - j
                sl = eb[r0 + j:r0 + j + ELEM_ROWS, :]
                dcb = dcb + cbw_ref[k:k + 1, :] * sl
                dcbw[k] = dcbw[k] + jnp.sum(sl * cb, axis=0, keepdims=True)
            dz_ref[rows, 1536:2048] = (dcb * b_val).astype(BF16)
            dz_ref[rows, 2048:2560] = (dcb * c_gate).astype(BF16)
        for k in range(B_CONV_WIDTH):
            dcbw_ref[k:k + 1, :] += dcbw[k]
        dn = jnp.zeros((tm, D_MODEL), F32)
        for j in range(N_CHIPS):
            dn = dn + _dot_nt(dz_ref[:, j * ws:(j + 1) * ws], win_v[j])
        wide[...] = dn
        g = g_ref[...]
        dg = jnp.zeros((1, D_MODEL), F32)
        for r0 in range(0, tm, ELEM_ROWS):
            rows = slice(r0, r0 + ELEM_ROWS)
            xv = x_ref[rows, :]
            _, rstd = _rms_fwd(xv, g)
            dx, dg_part = _rms_bwd(wide[rows, :], xv, rstd, g)
            dx_ref[rows, :] = dh_ref[rows, :] + dx
            dg = dg + dg_part
        dg_ref[...] += dg

        @pl.when(i == nt - 1)
        def _():
            for k in range(A_CONV_WIDTH):
                dcaw_ref[k:k + 1, :] = jnp.sum(dw_acc[k], axis=0, keepdims=True)

    row = lambda cols: jax.ShapeDtypeStruct((1, cols), F32)
    rs_ = functools.partial(_row_spec, rev_nt=nt)
    return _pallas(
        body, [dh, x, norm_g, z, a2, cv, w_in, conv_a_w, ln_g, ln_b, conv_b_w, w_out], name="bwd_even", grid=(nt,),
        in_specs=[rs_(tm, D_MODEL), rs_(tm, D_MODEL), _full_spec((1, D_MODEL)), rs_(tm, IN_EVEN), rs_(tm, A_DIM),
                  rs_(tm, B_DIM), ANY, _full_spec((A_CONV_WIDTH, A_DIM)), _full_spec((1, A_DIM)), _full_spec((1, A_DIM)),
                  _full_spec((B_CONV_WIDTH, B_DIM)), ANY],
        out_specs=[rs_(tm, D_MODEL), rs_(tm, IN_EVEN), _full_spec((1, D_MODEL)), _full_spec((A_CONV_WIDTH, A_DIM)),
                   _full_spec((1, A_DIM)), _full_spec((1, A_DIM)), _full_spec((1, A_DIM)), _full_spec((B_CONV_WIDTH, B_DIM))],
        out_shape=[jax.ShapeDtypeStruct((tokens, D_MODEL), F32), jax.ShapeDtypeStruct((tokens, IN_EVEN), BF16),
                   row(D_MODEL), jax.ShapeDtypeStruct((A_CONV_WIDTH, A_DIM), F32), row(A_DIM), row(A_DIM), row(A_DIM),
                   jax.ShapeDtypeStruct((B_CONV_WIDTH, B_DIM), F32)],
        scratch_shapes=[pltpu.VMEM((N_CHIPS, D_MODEL, ws), BF16), pltpu.VMEM((D_MODEL, D_MODEL), BF16),
                        pltpu.VMEM((SUBLANES, tm + A_HALO, A_DIM), F32), pltpu.VMEM((tm + B_HALO, B_DIM), F32),
                        pltpu.VMEM((tm, A_DIM), F32), pltpu.VMEM((tm, A_DIM), F32), pltpu.VMEM((tm, A_DIM), F32),
                        pltpu.VMEM((tm, D_MODEL), F32),
                        pltpu.VMEM((A_CONV_WIDTH, CONV_ROWS, A_DIM), F32), pltpu.SemaphoreType.DMA((N_LOADS,))],
        vmem_mib=56, riders=riders)


def _wgrad(a, b, name, *, col_shards, riders=()):
    tokens, m = a.shape
    n = b.shape[1]
    kc = 512
    if col_shards:
        bm, bn = m // 2, n // N_CHIPS
        grid = (2, N_CHIPS)
        out_spec = pl.BlockSpec((None, None, bm, bn), lambda i, j: (j, i, 0, 0))
    elif m // 8 >= MXU_ROWS:
        bm, bn = m // 8, n
        grid = (8, 1)
        out_spec = pl.BlockSpec((None, None, bm, bn), lambda i, j: (i // 2, i % 2, 0, 0))
    else:
        bm, bn = m // N_CHIPS, n
        grid = (N_CHIPS, 1)
        out_spec = pl.BlockSpec((None, 2, bm // 2, bn), lambda i, j: (i, 0, 0, 0))

    def body(a_ref, b_ref, o_ref):
        acc = jnp.zeros((bm, bn), F32)
        for k0 in range(0, tokens, kc):
            acc = acc + _dot_tn(a_ref[k0:k0 + kc, :].astype(BF16), b_ref[k0:k0 + kc, :].astype(BF16))
        if len(o_ref.shape) == 3:
            o_ref[0] = acc[0:bm // 2]
            o_ref[1] = acc[bm // 2:bm]
        else:
            o_ref[...] = acc

    out_rows = m // 2 if col_shards else m // 8
    outs, routs = _pallas(
        body, [a, b], name=name, grid=grid,
        in_specs=[pl.BlockSpec((tokens, bm), lambda i, j: (0, i)), pl.BlockSpec((tokens, bn), lambda i, j: (0, j))],
        out_specs=[out_spec], out_shape=[jax.ShapeDtypeStruct((N_CHIPS, 2, out_rows, bn), F32)],
        vmem_mib=56, riders=riders)
    return outs[0], routs


def _wgrad_pair(a, b, name, *, col_shards, riders=()):
    tokens, m = a.shape
    n = b.shape[1]
    kc = 512
    c0 = lax.axis_index("c")

    def half(ph, pre):
        return (ph + 1 + pre[0]) % 2

    if col_shards:
        bm, bn = m // 2, n // N_CHIPS
        a_spec = pl.BlockSpec((tokens, bm), lambda ph, q, pre: (0, half(ph, pre)))
        b_spec = pl.BlockSpec((tokens, bn), lambda ph, q, pre: (0, q))
    else:
        bm, bn = m // 8, n
        a_spec = pl.BlockSpec((tokens, bm), lambda ph, q, pre: (0, 2 * q + half(ph, pre)))
        b_spec = pl.BlockSpec((tokens, bn), lambda ph, q, pre: (0, 0))

    def body(pre_ref, a_ref, b_ref, o_ref, give, got, send_sems, recv_sems):
        ph, q = pl.program_id(0), pl.program_id(1)
        acc = jnp.zeros((bm, bn), F32)
        for k0 in range(0, tokens, kc):
            acc = acc + _dot_tn(a_ref[k0:k0 + kc, :].astype(BF16), b_ref[k0:k0 + kc, :].astype(BF16))
        x, y, cc = _mesh_pos()

        def tile(t):
            return _remote(give.at[t], got.at[t], send_sems.at[t], recv_sems.at[t], (x, y, 1 - cc))

        @pl.when(ph == 0)
        def _():
            give[q] = acc
            tile(q).start()

        @pl.when(ph == 1)
        def _():
            tile(q).wait_recv()
            o_ref[...] = (acc + got[q]).astype(BF16)

        @pl.when((ph == 1) & (q == N_CHIPS - 1))
        def _():
            for t in range(N_CHIPS):
                tile(t).wait_send()

    outs, routs = _pallas(
        body, [a, b], name=name, grid=(2, N_CHIPS), in_specs=[a_spec, b_spec],
        out_specs=[pl.BlockSpec((None, bm, bn), lambda ph, q, pre: (ph * q, 0, 0))],
        out_shape=[jax.ShapeDtypeStruct((N_CHIPS, bm, bn), BF16)],
        scratch_shapes=[pltpu.VMEM((N_CHIPS, bm, bn), F32), pltpu.VMEM((N_CHIPS, bm, bn), F32),
                        pltpu.SemaphoreType.DMA((N_CHIPS,)), pltpu.SemaphoreType.DMA((N_CHIPS,))],
        vmem_mib=56, riders=riders, prefetch=jnp.reshape(c0, (1,)).astype(jnp.int32))
    return outs[0], routs


class _GradReduce:
    def __init__(self, name, grad=None, chip_sum=None):
        self.name, self.grad, self.chip_sum = name, grad, chip_sum
        self.full = None

    def pair_swap(self):
        return _PairSwap([self.grad])

    def took_pair(self, outs):
        self.chip_sum = _in_hbm(_add_pair(self.grad, outs[0], f"pair_sum_{self.name}"))

    def chip_swap(self):
        return _ChipSwap([self.chip_sum])

    def took_chips(self, outs):
        self.full = _in_hbm(_add_chips(self.chip_sum, outs[0], f"chip_sum_{self.name}"))

    def chips_beside(self, collective_id):
        self.took_chips([_chip_swap_beside(self.chip_sum, f"chip_swap_{self.name}", collective_id)])

    def pair_share(self):
        return _PairShare([self.full])

    def took_share(self, outs):
        self.full = outs[0]

    def reduced(self):
        return jnp.reshape(self.full, (2 * self.full.shape[1], self.full.shape[2]))


def _forward_backward(x2, tgt2, gathered, staged, conv_a_w, conv_b_w, od_norm, od_bias, od_lng, od_lnb,
                      ev_norm_g, ev_conv_a_b, ev_ln_a_g, ev_ln_a_b, od_w_s, od_b_s, mlp_norm_g, final_norm_g,
                      *, tm, seq, distributed=True):
    d = x2.shape[1]
    w = dict(gathered)
    b_s_rows = jnp.broadcast_to(od_b_s[0][:, :, None], (C_GROUPS, CHUNK, CHUNK))

    def ride(*names):
        return [_Gather([staged[nm] for nm in names])] if distributed and staged else []

    def land(routs, *names):
        if distributed and staged:
            for nm, buf in zip(names, routs[0]):
                w[nm] = buf

    (h1, n0, z, a2, cv, mix), routs = _fwd_even(
        x2, ev_norm_g, w["ev_in"], conv_a_w, ev_conv_a_b, ev_ln_a_g, ev_ln_a_b, conv_b_w, w["ev_out"],
        tm=tm, seq=seq, riders=ride("w1_0", "w2_0"))
    land(routs, "w1_0", "w2_0")
    (h2, n1, p0, q0), routs = _fwd_mlp(h1, mlp_norm_g[0:1], w["w1_0"], w["w2_0"], 0, tm=tm,
                                       riders=ride("od_in", "od_out", "w1_1"))
    land(routs, "od_in", "od_out", "w1_1")
    (h3, n2, s, cdf, sv, y), routs = _fwd_odd(h2, od_norm, w["od_in"], od_bias, od_lng, od_lnb, od_w_s[0], b_s_rows,
                                         w["od_out"], tm=tm, riders=ride("w2_1"))
    land(routs, "w2_1")
    (n3, p1, q1, loss_part, dh4, dh4b, d_final_g), _ = _fwd_mlp(
        h3, mlp_norm_g[1:2], w["w1_1"], w["w2_1"], 1, tm=tm,
        head=(jnp.reshape(final_norm_g, (1, d)), tgt2))

    red = {}

    def swap(*names):
        return [red[nm].pair_swap() for nm in names] if distributed else []

    def share(*names):
        return [red[nm].pair_share() for nm in names] if distributed else []

    def took(routs, *steps):
        if distributed:
            for (nm, what), outs in zip(steps, routs):
                getattr(red[nm], what)(outs)

    swap_ids = iter(range(FIRST_SWAP_ID, FIRST_SWAP_ID + 8))

    def beside(name):
        if distributed:
            red[name].chips_beside(next(swap_ids))

    def big(lhs, rhs, name, col_shards, riders=()):
        if distributed:
            chip_sum, routs = _wgrad_pair(lhs, rhs, f"wgrad_{name}", col_shards=col_shards, riders=riders)
            red[name] = _GradReduce(name, chip_sum=_in_hbm(chip_sum))
        else:
            g, routs = _wgrad(lhs, rhs, f"wgrad_{name}", col_shards=col_shards)
            red[name] = _GradReduce(name, grad=g)
        return routs

    big(q1, dh4b, "w2_1", False)
    beside("w2_1")
    (dh3, dh3b, dp1, d_mlp_g1), _ = _bwd_mlp(dh4, h3, mlp_norm_g[1:2], p1, w["w1_1"], w["w2_1"], 1, tm=tm)
    big(n3, dp1, "w1_1", True)
    beside("w1_1")
    g, routs = _wgrad(y, dh3b, "wgrad_od_out", col_shards=False, riders=share("w2_1"))
    red["od_out"] = _GradReduce("od_out", grad=g)
    took(routs, ("w2_1", "took_share"))
    (dh2, dh2b, ds, d_od_norm, d_od_bin, d_od_lng, d_od_lnb, d_ws, d_bs), _ = _bwd_odd(
        dh3, h2, od_norm, s, cdf, sv, w["od_in"], od_lng, od_lnb, od_w_s[0], w["od_out"], tm=tm)
    routs = big(n2, ds, "od_in", True, riders=share("w1_1") + swap("od_out"))
    took(routs, ("w1_1", "took_share"), ("od_out", "took_pair"))
    beside("od_in")
    beside("od_out")
    half_groups = C_GROUPS // 2
    early = {"loss": loss_part, "od_w_s_lo": d_ws[:half_groups], "od_b_s": d_bs, "mlp_norm_g1": d_mlp_g1, "final_norm_g": d_final_g,
             "od_norm_g": d_od_norm, "od_b_in": d_od_bin, "od_ln_v_g": d_od_lng, "od_ln_v_b": d_od_lnb}
    share_early = [_ShareAll(list(early.values()))] if distributed else []
    routs = big(q0, dh2b, "w2_0", False, riders=share_early)
    landed_early = routs[0] if distributed else []
    beside("w2_0")
    (dh1, dh1b, dp0, d_mlp_g0), _ = _bwd_mlp(dh2, h1, mlp_norm_g[0:1], p0, w["w1_0"], w["w2_0"], 0, tm=tm)
    middle = {"od_w_s_hi": d_ws[half_groups:]}
    share_middle = [_ShareAll(list(middle.values()))] if distributed else []
    g, _ = _wgrad(mix, dh1b, "wgrad_ev_out", col_shards=False)
    red["ev_out"] = _GradReduce("ev_out", grad=g)
    routs = big(n1, dp0, "w1_0", True,
                riders=share("od_out") + share("od_in") + share("w2_0") + share_middle + swap("ev_out"))
    took(routs, ("od_out", "took_share"), ("od_in", "took_share"), ("w2_0", "took_share"))
    landed_middle = routs[3] if distributed else []
    if distributed:
        red["ev_out"].took_pair(routs[4])
    beside("w1_0")
    beside("ev_out")

    (dx, dz, d_ev_norm, d_caw, d_cab, d_ev_lng, d_ev_lnb, d_cbw), _ = _bwd_even(
        dh1, x2, ev_norm_g, z, a2, cv, w["ev_in"], conv_a_w, ev_ln_a_g, ev_ln_a_b, conv_b_w, w["ev_out"], tm=tm, seq=seq)
    late = {"mlp_norm_g0": d_mlp_g0, "ev_norm_g": d_ev_norm, "ev_conv_a_b": d_cab, "ev_ln_a_g": d_ev_lng,
            "ev_ln_a_b": d_ev_lnb, "ev_conv_a_w": d_caw, "ev_conv_b_w": d_cbw}
    share_late = [_ShareAll(list(late.values()))] if distributed else []
    routs2 = big(n0, dz, "ev_in", True, riders=share("ev_out") + share("w1_0") + share_late)
    took(routs2, ("ev_out", "took_share"), ("w1_0", "took_share"))
    beside("ev_in")
    own = {**early, **middle, **late}
    landed = dict(zip(own.keys(), landed_early + landed_middle + routs2[2])) if distributed else None
    return dx, red, own, landed


def _rows128(a):
    rows = jnp.reshape(a, (-1, LANES))
    pad = (-rows.shape[0]) % SUBLANES
    return jnp.pad(rows, ((0, pad), (0, 0))) if pad else rows


def _pack(arrays):
    return jnp.concatenate([_rows128(a) for a in arrays], axis=0)


def _unpack(buf, shapes):
    out, r0 = [], 0
    for shp in shapes:
        size = 1
        for dim in shp:
            size *= dim
        nr = size // LANES
        out.append(jnp.reshape(buf[r0:r0 + nr], shp))
        r0 += nr + (-nr) % SUBLANES
    return out


def kernel(x, ev_norm_g, ev_w_in, ev_conv_a_w, ev_conv_a_b, ev_ln_a_g, ev_ln_a_b, ev_conv_b_w, ev_w_out, od_norm_g, od_w_in, od_b_in, od_ln_v_g, od_ln_v_b, od_w_s, od_b_s, od_w_out, mlp_norm_g, mlp_w1, mlp_w2, final_norm_g, loss_target, m_ev_norm_g, m_ev_w_in, m_ev_conv_a_w, m_ev_conv_a_b, m_ev_ln_a_g, m_ev_ln_a_b, m_ev_conv_b_w, m_ev_w_out, m_od_norm_g, m_od_w_in, m_od_b_in, m_od_ln_v_g, m_od_ln_v_b, m_od_w_s, m_od_b_s, m_od_w_out, m_mlp_norm_g, m_mlp_w1, m_mlp_w2, m_final_norm_g, v_ev_norm_g, v_ev_w_in, v_ev_conv_a_w, v_ev_conv_a_b, v_ev_ln_a_g, v_ev_ln_a_b, v_ev_conv_b_w, v_ev_w_out, v_od_norm_g, v_od_w_in, v_od_b_in, v_od_ln_v_g, v_od_ln_v_b, v_od_w_s, v_od_b_s, v_od_w_out, v_mlp_norm_g, v_mlp_w1, v_mlp_w2, v_final_norm_g):
    tm = TOKEN_TILE
    batch, seq, d = x.shape
    tokens = batch * seq
    x2 = jnp.reshape(x, (tokens, d))
    tgt2 = jnp.reshape(loss_target, (tokens, d))
    chip = 2 * lax.axis_index("x") + lax.axis_index("y")

    small_shapes = [(A_CONV_WIDTH, LANES), (B_CONV_WIDTH, LANES), (256,), (512,), (256,), (256,)]
    small_shard = _pack([ev_conv_a_w[0], ev_conv_b_w[0], od_norm_g[0], od_b_in[0], od_ln_v_g[0], od_ln_v_b[0]])
    small_shard = jnp.pad(small_shard, ((0, (-small_shard.shape[0]) % (2 * SUBLANES)), (0, 0)))
    first = [_place_shard(ev_w_in, 0, BF16, "place_ev_w_in"), _place_shard(ev_w_out, 0, BF16, "place_ev_w_out"),
             _place_shard(small_shard[None], 0, F32, "place_small")]
    staged = {
        "w1_0": _place_shard(mlp_w1, 0, BF16, "place_w1_0"), "w2_0": _place_shard(mlp_w2, 0, BF16, "place_w2_0"),
        "od_in": _place_shard(od_w_in, 0, BF16, "place_od_w_in"), "od_out": _place_shard(od_w_out, 0, BF16, "place_od_w_out"),
        "w1_1": _place_shard(mlp_w1, 1, BF16, "place_w1_1"), "w2_1": _place_shard(mlp_w2, 1, BF16, "place_w2_1"),
    }
    first = [_in_hbm(a) for a in first]
    staged = {nm: _in_hbm(a) for nm, a in staged.items()}
    g_ev_in, g_ev_out, g_small = _gather_beside(first, "gather_stage0", collective_id=1)
    gathered = {"ev_in": g_ev_in, "ev_out": g_ev_out}
    for stage, names in enumerate((("w1_0", "w2_0"), ("od_in", "od_out", "w1_1"), ("w2_1",))):
        done = _gather_beside([staged[nm] for nm in names], f"gather_stage{stage + 1}", collective_id=stage + 2)
        gathered.update(zip(names, done))
    small_all = jnp.reshape(_plain_copy(g_small, "small_weights_copy"), (N_CHIPS, -1, LANES))
    per_chip = [_unpack(small_all[q], small_shapes) for q in range(N_CHIPS)]
    conv_a_w = jnp.concatenate([pc[0] for pc in per_chip], axis=1)
    conv_b_w = jnp.concatenate([pc[1] for pc in per_chip], axis=1)
    od_norm = jnp.concatenate([pc[2] for pc in per_chip])[None, :]
    od_bias = jnp.concatenate([pc[3] for pc in per_chip])[None, :]
    od_lng = jnp.concatenate([pc[4] for pc in per_chip])[None, :]
    od_lnb = jnp.concatenate([pc[5] for pc in per_chip])[None, :]

    dx, red, own, landed = _forward_backward(
        x2, tgt2, gathered, {}, conv_a_w, conv_b_w, od_norm, od_bias, od_lng, od_lnb,
        ev_norm_g, ev_conv_a_b, ev_ln_a_g, ev_ln_a_b, od_w_s, od_b_s, mlp_norm_g, final_norm_g, tm=tm, seq=seq)

    routs = _exchange([red["ev_in"].pair_share()], "reduce_tail")
    red["ev_in"].took_share(routs[0])

    given = {"ev_norm_g": (ev_norm_g, m_ev_norm_g, v_ev_norm_g), "ev_conv_a_b": (ev_conv_a_b, m_ev_conv_a_b, v_ev_conv_a_b),
             "ev_ln_a_g": (ev_ln_a_g, m_ev_ln_a_g, v_ev_ln_a_g), "ev_ln_a_b": (ev_ln_a_b, m_ev_ln_a_b, v_ev_ln_a_b),
             "od_w_s": (od_w_s, m_od_w_s, v_od_w_s), "od_b_s": (od_b_s, m_od_b_s, v_od_b_s),
             "mlp_norm_g": (mlp_norm_g, m_mlp_norm_g, v_mlp_norm_g), "final_norm_g": (final_norm_g, m_final_norm_g, v_final_norm_g),
             "ev_conv_a_w": (ev_conv_a_w, m_ev_conv_a_w, v_ev_conv_a_w), "ev_conv_b_w": (ev_conv_b_w, m_ev_conv_b_w, v_ev_conv_b_w),
             "od_norm_g": (od_norm_g, m_od_norm_g, v_od_norm_g), "od_b_in": (od_b_in, m_od_b_in, v_od_b_in),
             "od_ln_v_g": (od_ln_v_g, m_od_ln_v_g, v_od_ln_v_g), "od_ln_v_b": (od_ln_v_b, m_od_ln_v_b, v_od_ln_v_b)}
    shaped = {nm: tuple(jnp.reshape(a, shape) for a in given[nm]) for nm, shape, _, _ in SMALL_WEIGHTS}
    loss11, small_upd = _small_update(own, landed, shaped)
    loss = loss11[0, 0]
    upd = {nm: [jnp.reshape(o, given[nm][0].shape) for o in outs] for nm, outs in small_upd.items()}

    def big_update(wt, m, v, names, call):
        grads = [red[nm].reduced() for nm in names]
        shp3 = (len(grads),) + grads[0].shape
        outs, _ = _adamw(jnp.reshape(wt, shp3), jnp.reshape(m, shp3), jnp.reshape(v, shp3), grads, call)
        return [jnp.reshape(o, wt.shape) for o in outs], None

    upd["mlp_w2"], _ = big_update(mlp_w2, m_mlp_w2, v_mlp_w2, ["w2_0", "w2_1"], "adamw_mlp_w2")
    upd["mlp_w1"], _ = big_update(mlp_w1, m_mlp_w1, v_mlp_w1, ["w1_0", "w1_1"], "adamw_mlp_w1")
    upd["ev_w_in"], _ = big_update(ev_w_in, m_ev_w_in, v_ev_w_in, ["ev_in"], "adamw_ev_w_in")
    upd["ev_w_out"], _ = big_update(ev_w_out, m_ev_w_out, v_ev_w_out, ["ev_out"], "adamw_ev_w_out")
    upd["od_w_in"], _ = big_update(od_w_in, m_od_w_in, v_od_w_in, ["od_in"], "adamw_od_w_in")
    upd["od_w_out"], _ = big_update(od_w_out, m_od_w_out, v_od_w_out, ["od_out"], "adamw_od_w_out")

    order = ["ev_norm_g", "ev_w_in", "ev_conv_a_w", "ev_conv_a_b", "ev_ln_a_g", "ev_ln_a_b", "ev_conv_b_w", "ev_w_out",
             "od_norm_g", "od_w_in", "od_b_in", "od_ln_v_g", "od_ln_v_b", "od_w_s", "od_b_s", "od_w_out", "mlp_norm_g",
             "mlp_w1", "mlp_w2", "final_norm_g"]
    grad_x = jnp.reshape(dx, x.shape)
    return (loss, grad_x, *[upd[nm][0] for nm in order], *[upd[nm][1] for nm in order],
            *[upd[nm][2] for nm in order], *[upd[nm][3] for nm in order])
```

```python
import functools

import jax
import jax.numpy as jnp
from jax import lax
from jax.experimental import pallas as pl
from jax.experimental.pallas import tpu as pltpu
from jax.experimental.pallas import tpu_sc as plsc

F32 = jnp.float32
BF16 = jnp.bfloat16

D_MODEL = 1024
A_DIM = 512
B_DIM = 512
IN_EVEN = 2 * A_DIM + 3 * B_DIM
A_CONV_WIDTH = 31
B_CONV_WIDTH = 3
CHUNK = 128
C_GROUPS = 8
C_DIM = 1024
D_FF = 4096
RMS_EPS = 1e-6
LN_EPS = 1e-5
ADAM_LR = 0.001
ADAM_B1 = 0.9
ADAM_B2 = 0.999
ADAM_EPS = 1e-08
ADAM_WD = 0.01
ADAM_STEP = 10

N_CHIPS = 4
N_DEV = 8
TOKEN_TILE = 512
A_HALO = 32
B_HALO = 8
CONV_ROWS = 16
DW_TAPS = 4
ELEM_ROWS = 16
PAIR = 2 * CHUNK
LANES = 128
SUBLANES = 8
MXU_ROWS = 256
MIB = 1024 * 1024
MESH = pl.DeviceIdType.MESH
ANY = pl.BlockSpec(memory_space=pl.ANY)


def _dot(a, b):
    return lax.dot_general(a, b, (((1,), (0,)), ((), ())), preferred_element_type=F32)


def _dot_nt(a, b):
    return lax.dot_general(a, b, (((1,), (1,)), ((), ())), preferred_element_type=F32)


def _dot_tn(a, b):
    return lax.dot_general(a, b, (((0,), (0,)), ((), ())), preferred_element_type=F32)


def _params(vmem_mib, n_axes=1):
    return pltpu.CompilerParams(dimension_semantics=("arbitrary",) * n_axes, vmem_limit_bytes=vmem_mib * MIB)


def _row_spec(tm, cols, rev_nt=None):
    if rev_nt is None:
        return pl.BlockSpec((tm, cols), lambda i: (i, 0))
    return pl.BlockSpec((tm, cols), lambda i: (rev_nt - 1 - i, 0))


def _full_spec(shape):
    nd = len(shape)
    return pl.BlockSpec(shape, lambda i: (0,) * nd)


def _block_rows(rows, cap=512):
    best = SUBLANES
    for br in range(SUBLANES, min(rows, cap) + 1, SUBLANES):
        if rows % br == 0:
            best = br
    return best


FIRST_SWAP_ID = 5
FIRST_SHARE_ID = 13
N_LOADS = 2 * 2 * N_CHIPS


def _load_weights(loads, sems):
    @pl.when(pl.program_id(0) == 0)
    def _():
        copies = []
        for src, dst, rows_of_one in loads:
            r = src.shape[2]
            for q in range(N_CHIPS):
                for h in range(2):
                    part = dst.at[pl.ds((2 * q + h) * r, r)] if rows_of_one else dst.at[q, pl.ds(h * r, r)]
                    copies.append(pltpu.make_async_copy(src.at[q, h], part, sems.at[len(copies)]))
        for cp in copies:
            cp.start()
        for cp in copies:
            cp.wait()


def _rms_fwd(x, g):
    rstd = lax.rsqrt(jnp.mean(x * x, axis=-1, keepdims=True) + RMS_EPS)
    return x * rstd * g, rstd


def _rms_bwd(dn, x, rstd, g):
    a = dn * g
    xh = x * rstd
    dx = rstd * (a - xh * jnp.mean(a * xh, axis=-1, keepdims=True))
    dg = jnp.sum(dn * xh, axis=0, keepdims=True)
    return dx, dg


def _ln_stats(v):
    mu = jnp.mean(v, axis=-1, keepdims=True)
    xc = v - mu
    rs = lax.rsqrt(jnp.mean(xc * xc, axis=-1, keepdims=True) + LN_EPS)
    return xc * rs, rs


def _ln_bwd(dy, xhat, rs, g):
    dxh = dy * g
    dv = rs * (dxh - jnp.mean(dxh, axis=-1, keepdims=True) - xhat * jnp.mean(dxh * xhat, axis=-1, keepdims=True))
    return dv, jnp.sum(dy * xhat, axis=0, keepdims=True), jnp.sum(dy, axis=0, keepdims=True)


def _gelu_cdf(s):
    return 0.5 * (1.0 + lax.erf(s * 0.7071067811865476))


def _mesh_pos():
    return lax.axis_index("x"), lax.axis_index("y"), lax.axis_index("c")


def _other_chips(x, y):
    return [(1 - x, y), (x, 1 - y), (1 - x, 1 - y)]


def _remote(src, dst, send_sem, recv_sem, to):
    return pltpu.make_async_remote_copy(src_ref=src, dst_ref=dst, send_sem=send_sem, recv_sem=recv_sem,
                                        device_id=to, device_id_type=MESH)


def _like(arrays):
    return [jax.ShapeDtypeStruct(a.shape, a.dtype) for a in arrays]


class _Gather:
    def __init__(self, bufs):
        self.ins = list(bufs)
        self.out_shapes = _like(bufs)
        self.aliases = {t: t for t in range(len(bufs))}
        self.n_sems = 6 * len(bufs)

    def _ici(self, ins, outs, send, recv, t, k, chip, mine, c):
        return _remote(ins[t].at[mine, c], outs[t].at[mine, c], send.at[6 * t + k], recv.at[6 * t + k], (*chip, c))

    def start(self, ins, outs, send, recv):
        x, y, c = _mesh_pos()
        for t in range(len(ins)):
            for k, chip in enumerate(_other_chips(x, y)):
                self._ici(ins, outs, send, recv, t, k, chip, 2 * x + y, c).start()

    def _pass_on(self, outs, send, recv, t, k, chip, c, to):
        blk = outs[t].at[2 * chip[0] + chip[1], c]
        return _remote(blk, blk, send.at[6 * t + 3 + k], recv.at[6 * t + 3 + k], to)

    def near_end(self, ins, outs, send, recv):
        x, y, c = _mesh_pos()
        for t in range(len(ins)):
            for k, chip in enumerate(_other_chips(x, y)):
                blk = outs[t].at[2 * chip[0] + chip[1], c]
                _remote(blk, blk, send.at[6 * t + k], recv.at[6 * t + k], (x, y, c)).wait_recv()
                self._pass_on(outs, send, recv, t, k, chip, c, (x, y, 1 - c)).start()

    def finish(self, ins, outs, send, recv):
        x, y, c = _mesh_pos()
        chips = _other_chips(x, y)
        for t in range(len(ins)):
            for k, chip in enumerate(chips):
                self._pass_on(outs, send, recv, t, k, chip, 1 - c, (x, y, c)).wait_recv()
        for t in range(len(ins)):
            for k, chip in enumerate(chips):
                self._ici(ins, outs, send, recv, t, k, chip, 2 * x + y, c).wait_send()
                self._pass_on(outs, send, recv, t, k, chip, c, (x, y, 1 - c)).wait_send()


class _PairSwap:
    def __init__(self, grads):
        self.ins = list(grads)
        self.out_shapes = [jax.ShapeDtypeStruct((g.shape[0],) + g.shape[2:], g.dtype) for g in grads]
        self.aliases = {}
        self.n_sems = len(grads)

    def _copies(self, ins, outs, send, recv):
        x, y, c = _mesh_pos()
        return [_remote(ins[t].at[:, 1 - c], outs[t], send.at[t], recv.at[t], (x, y, 1 - c)) for t in range(len(ins))]

    def start(self, ins, outs, send, recv):
        for cp in self._copies(ins, outs, send, recv):
            cp.start()

    def finish(self, ins, outs, send, recv):
        for cp in self._copies(ins, outs, send, recv):
            cp.wait()


class _ChipSwap:
    def __init__(self, parts):
        self.ins = list(parts)
        self.out_shapes = [jax.ShapeDtypeStruct((3,) + p.shape[1:], p.dtype) for p in parts]
        self.aliases = {}
        self.n_sems = 3 * len(parts)

    def _copies(self, ins, outs, send, recv):
        x, y, c = _mesh_pos()
        return [_remote(ins[t].at[2 * chip[0] + chip[1]], outs[t].at[k], send.at[3 * t + k], recv.at[3 * t + k], (*chip, c))
                for t in range(len(ins)) for k, chip in enumerate(_other_chips(x, y))]

    def start(self, ins, outs, send, recv):
        for cp in self._copies(ins, outs, send, recv):
            cp.start()

    def finish(self, ins, outs, send, recv):
        for cp in self._copies(ins, outs, send, recv):
            cp.wait()


class _PairShare:
    def __init__(self, fulls):
        self.ins = list(fulls)
        self.out_shapes = _like(fulls)
        self.aliases = {t: t for t in range(len(fulls))}
        self.n_sems = len(fulls)

    def _copies(self, ins, outs, send, recv):
        x, y, c = _mesh_pos()
        return [_remote(ins[t].at[c], outs[t].at[c], send.at[t], recv.at[t], (x, y, 1 - c)) for t in range(len(ins))]

    def start(self, ins, outs, send, recv):
        for cp in self._copies(ins, outs, send, recv):
            cp.start()

    def finish(self, ins, outs, send, recv):
        for cp in self._copies(ins, outs, send, recv):
            cp.wait()


class _ShareAll:
    def __init__(self, arrays):
        self.ins = list(arrays)
        self.out_shapes = [jax.ShapeDtypeStruct((N_DEV,) + a.shape, a.dtype) for a in arrays]
        self.aliases = {}
        self.n_sems = (N_DEV - 1) * len(arrays)

    def _peers(self):
        x, y, c = _mesh_pos()
        flips = [((r >> 2) & 1, (r >> 1) & 1, r & 1) for r in range(1, N_DEV)]
        return (x, y, c), [(x ^ fx, y ^ fy, c ^ fc) for fx, fy, fc in flips]

    def _sends(self, ins, outs, send, recv):
        (x, y, c), peers = self._peers()
        mine = 4 * x + 2 * y + c
        return [_remote(ins[a], outs[a].at[mine], send.at[7 * a + r], recv.at[7 * a + r], peer)
                for a in range(len(ins)) for r, peer in enumerate(peers)]

    def start(self, ins, outs, send, recv):
        for cp in self._sends(ins, outs, send, recv):
            cp.start()

    def finish(self, ins, outs, send, recv):
        (x, y, c), peers = self._peers()
        for a in range(len(ins)):
            for r, (px, py, pc) in enumerate(peers):
                blk = outs[a].at[4 * px + 2 * py + pc]
                _remote(blk, blk, send.at[7 * a + r], recv.at[7 * a + r], (x, y, c)).wait_recv()
        for cp in self._sends(ins, outs, send, recv):
            cp.wait_send()


def _gather_beside(bufs, name, collective_id):
    n = len(bufs)
    refs = [jax.new_ref(b, memory_space=pltpu.MemorySpace.HBM) for b in bufs]
    gather = _Gather(bufs)

    @pl.kernel(mesh=plsc.ScalarSubcoreMesh(axis_name="sequencer", num_cores=1), name=name,
               scratch_types=(pltpu.SemaphoreType.DMA((6 * n,)), pltpu.SemaphoreType.DMA((6 * n,))),
               compiler_params=pltpu.CompilerParams(collective_id=collective_id))
    def launch(send, recv):
        x, y, c = _mesh_pos()
        barrier = pltpu.get_barrier_semaphore()
        peers = [(*chip, c) for chip in _other_chips(x, y)] + [(x, y, 1 - c)]
        for peer in peers:
            pl.semaphore_signal(barrier, inc=1, device_id=peer, device_id_type=MESH)
        pl.semaphore_wait(barrier, len(peers))
        gather.start(refs, refs, send, recv)
        gather.near_end(refs, refs, send, recv)
        gather.finish(refs, refs, send, recv)

    launch()
    return [r[...] for r in refs]


def _chip_swap_beside(parts, name, collective_id):
    src = jax.new_ref(parts, memory_space=pltpu.MemorySpace.HBM)
    dst = jax.empty_ref(jax.ShapeDtypeStruct((N_CHIPS - 1,) + parts.shape[1:], parts.dtype),
                        memory_space=pltpu.MemorySpace.HBM)
    swap = _ChipSwap([parts])

    @pl.kernel(mesh=plsc.ScalarSubcoreMesh(axis_name="sequencer", num_cores=1), name=name,
               scratch_types=(pltpu.SemaphoreType.DMA((N_CHIPS - 1,)), pltpu.SemaphoreType.DMA((N_CHIPS - 1,))),
               compiler_params=pltpu.CompilerParams(collective_id=collective_id))
    def launch(send, recv):
        x, y, c = _mesh_pos()
        barrier = pltpu.get_barrier_semaphore()
        peers = [(*chip, c) for chip in _other_chips(x, y)]
        for peer in peers:
            pl.semaphore_signal(barrier, inc=1, device_id=peer, device_id_type=MESH)
        pl.semaphore_wait(barrier, len(peers))
        swap.start([src], [dst], send, recv)
        swap.finish([src], [dst], send, recv)

    launch()
    return dst[...]


def _share_all_beside(arrays, name, collective_id):
    n = len(arrays)
    srcs = [jax.new_ref(a, memory_space=pltpu.MemorySpace.HBM) for a in arrays]
    dsts = [jax.empty_ref(jax.ShapeDtypeStruct((N_DEV,) + a.shape, a.dtype), memory_space=pltpu.MemorySpace.HBM)
            for a in arrays]
    share = _ShareAll(arrays)

    @pl.kernel(mesh=plsc.ScalarSubcoreMesh(axis_name="sequencer", num_cores=1), name=name,
               scratch_types=(pltpu.SemaphoreType.DMA(((N_DEV - 1) * n,)), pltpu.SemaphoreType.DMA(((N_DEV - 1) * n,))),
               compiler_params=pltpu.CompilerParams(collective_id=collective_id))
    def launch(send, recv):
        _, peers = share._peers()
        barrier = pltpu.get_barrier_semaphore()
        for peer in peers:
            pl.semaphore_signal(barrier, inc=1, device_id=peer, device_id_type=MESH)
        pl.semaphore_wait(barrier, len(peers))
        share.start(srcs, dsts, send, recv)
        share.finish(srcs, dsts, send, recv)

    launch()
    return [d[...] for d in dsts]


def _pallas(body, operands, *, name, grid, in_specs, out_specs, out_shape, scratch_shapes=(), vmem_mib=32, riders=(),
            prefetch=None):
    in_specs, out_specs, out_shape, scratch_shapes = list(in_specs), list(out_specs), list(out_shape), list(scratch_shapes)
    if not riders and prefetch is None:
        outs = pl.pallas_call(body, name=name, grid=grid, in_specs=in_specs, out_specs=out_specs, out_shape=out_shape,
                              scratch_shapes=scratch_shapes, compiler_params=_params(vmem_mib, len(grid)))(*operands)
        return list(outs), []
    n_in, n_out, n_scr = len(in_specs), len(out_specs), len(scratch_shapes)
    r_in = [len(r.ins) for r in riders]
    r_out = [len(r.out_shapes) for r in riders]
    steps = 1
    for g in grid:
        steps *= g

    n_pre = 0 if prefetch is None else 1

    def wrapped(*refs):
        refs = list(refs)
        pre, refs = refs[:n_pre], refs[n_pre:]
        ins, refs = refs[:n_in], refs[n_in:]
        rins = []
        for k in r_in:
            rins.append(refs[:k])
            refs = refs[k:]
        outs, refs = refs[:n_out], refs[n_out:]
        routs = []
        for k in r_out:
            routs.append(refs[:k])
            refs = refs[k:]
        scr, sems = refs[:n_scr], refs[n_scr:]
        step = 0
        for ax, g in enumerate(grid):
            step = step * g + pl.program_id(ax)

        def each(what):
            for j, r in enumerate(riders):
                if hasattr(r, what):
                    getattr(r, what)(rins[j], routs[j], sems[2 * j], sems[2 * j + 1])

        if grid:
            pl.when(step == 0)(lambda: each("start"))
        else:
            each("start")
        body(*pre, *ins, *outs, *scr)
        if grid:
            @pl.when(step == steps - 1)
            def _():
                each("near_end")
                each("finish")
        else:
            each("near_end")
            each("finish")

    aliases, off_in, off_out = {}, n_pre + n_in, n_out
    for r, ki, ko in zip(riders, r_in, r_out):
        for i, o in r.aliases.items():
            aliases[off_in + i] = off_out + o
        off_in, off_out = off_in + ki, off_out + ko
    sems = []
    for r in riders:
        sems += [pltpu.SemaphoreType.DMA((r.n_sems,)), pltpu.SemaphoreType.DMA((r.n_sems,))]
    layout = dict(grid=grid, in_specs=in_specs + [ANY] * sum(r_in), out_specs=out_specs + [ANY] * sum(r_out),
                  scratch_shapes=scratch_shapes + sems)
    if prefetch is not None:
        layout = dict(grid_spec=pltpu.PrefetchScalarGridSpec(num_scalar_prefetch=1, **layout))
    res = pl.pallas_call(
        wrapped, name=name, **layout,
        out_shape=out_shape + [s for r in riders for s in r.out_shapes], input_output_aliases=aliases,
        compiler_params=pltpu.CompilerParams(dimension_semantics=("arbitrary",) * len(grid),
                                             vmem_limit_bytes=vmem_mib * MIB, has_side_effects=True),
    )(*([] if prefetch is None else [prefetch]), *operands, *[a for r in riders for a in r.ins])
    res = list(res)
    outs, res = res[:n_out], res[n_out:]
    routs = []
    for k in r_out:
        routs.append(res[:k])
        res = res[k:]
    return outs, routs


def _exchange(riders, name):
    return _pallas(lambda: None, [], name=name, grid=(), in_specs=[], out_specs=[], out_shape=[], riders=riders)[1]


def _in_hbm(a):
    return pltpu.with_memory_space_constraint(a, pltpu.HBM)


def _place_shard(w, layer, dtype, name):
    _, rows, cols = w.shape
    half = rows // 2
    br = _block_rows(half)
    nb = half // br
    mine = 2 * lax.axis_index("x") + lax.axis_index("y")

    def body(q_ref, w_ref, o_ref):
        o_ref[...] = w_ref[...].astype(dtype)

    return pl.pallas_call(
        body, name=name,
        grid_spec=pltpu.PrefetchScalarGridSpec(
            num_scalar_prefetch=1, grid=(2, nb),
            in_specs=[pl.BlockSpec((None, br, cols), lambda h, i, q: (layer, h * nb + i, 0))],
            out_specs=pl.BlockSpec((None, None, br, cols), lambda h, i, q: (q[0], h, i, 0))),
        out_shape=pltpu.HBM((N_CHIPS, 2, half, cols), dtype),
        compiler_params=_params(16, 2),
    )(jnp.reshape(mine, (1,)).astype(jnp.int32), _in_hbm(w))


def _plain_copy(a, name):
    def body(a_ref, o_ref):
        o_ref[...] = a_ref[...]

    vmem = pl.BlockSpec(memory_space=pltpu.VMEM)
    return pl.pallas_call(body, name=name, in_specs=[vmem], out_specs=vmem,
                          out_shape=jax.ShapeDtypeStruct(a.shape, a.dtype))(a)


def _add_pair(g, recv, name):
    _, _, r, cdim = g.shape
    br = _block_rows(r, 256)
    c = lax.axis_index("c")

    def body(c_ref, g_ref, r_ref, o_ref):
        o_ref[...] = (g_ref[...] + r_ref[...]).astype(BF16)

    return pl.pallas_call(
        body, name=name,
        grid_spec=pltpu.PrefetchScalarGridSpec(
            num_scalar_prefetch=1, grid=(N_CHIPS, r // br),
            in_specs=[pl.BlockSpec((None, None, br, cdim), lambda q, i, c_ref: (q, c_ref[0], i, 0)),
                      pl.BlockSpec((None, br, cdim), lambda q, i, c_ref: (q, i, 0))],
            out_specs=pl.BlockSpec((None, br, cdim), lambda q, i, c_ref: (q, i, 0))),
        out_shape=pltpu.HBM((N_CHIPS, r, cdim), BF16),
        compiler_params=_params(16, 2),
    )(jnp.reshape(c, (1,)).astype(jnp.int32), _in_hbm(g), _in_hbm(recv))


def _add_chips(own, recv, name):
    _, r, cdim = own.shape
    br = _block_rows(r, 256)
    x, y, c = _mesh_pos()

    def body(pos_ref, own_ref, r_ref, o_ref):
        acc = own_ref[...].astype(F32)
        for k in range(3):
            acc = acc + r_ref[k].astype(F32)
        o_ref[...] = acc

    return pl.pallas_call(
        body, name=name,
        grid_spec=pltpu.PrefetchScalarGridSpec(
            num_scalar_prefetch=1, grid=(r // br,),
            in_specs=[pl.BlockSpec((None, br, cdim), lambda i, pos: (pos[0], i, 0)),
                      pl.BlockSpec((3, br, cdim), lambda i, pos: (0, i, 0))],
            out_specs=pl.BlockSpec((None, br, cdim), lambda i, pos: (pos[1], i, 0))),
        out_shape=pltpu.HBM((2, r, cdim), F32),
        compiler_params=_params(16, 1),
    )(jnp.stack([2 * x + y, c]).astype(jnp.int32), _in_hbm(own), recv)


def _adam_math(w, m, v, g):
    c1 = 1.0 / (1.0 - ADAM_B1 ** ADAM_STEP)
    c2 = 1.0 / (1.0 - ADAM_B2 ** ADAM_STEP)
    m_new = ADAM_B1 * m + (1.0 - ADAM_B1) * g
    v_new = ADAM_B2 * v + (1.0 - ADAM_B2) * (g * g)
    return -ADAM_LR * ((m_new * c1) / (jnp.sqrt(v_new * c2) + ADAM_EPS) + ADAM_WD * w), m_new, v_new


SMALL_WEIGHTS = [
    ("ev_norm_g", (1, D_MODEL), ["ev_norm_g"], None), ("ev_conv_a_b", (1, A_DIM), ["ev_conv_a_b"], None),
    ("ev_ln_a_g", (1, A_DIM), ["ev_ln_a_g"], None), ("ev_ln_a_b", (1, A_DIM), ["ev_ln_a_b"], None),
    ("od_w_s", (C_GROUPS, CHUNK, CHUNK), ["od_w_s"], None), ("od_b_s", (C_GROUPS, CHUNK), ["od_b_s"], None),
    ("mlp_norm_g", (2, D_MODEL), ["mlp_norm_g0", "mlp_norm_g1"], None), ("final_norm_g", (1, D_MODEL), ["final_norm_g"], None),
    ("ev_conv_a_w", (A_CONV_WIDTH, A_DIM // N_CHIPS), ["ev_conv_a_w"], A_DIM // N_CHIPS),
    ("ev_conv_b_w", (B_CONV_WIDTH, B_DIM // N_CHIPS), ["ev_conv_b_w"], B_DIM // N_CHIPS),
    ("od_norm_g", (1, D_MODEL // N_CHIPS), ["od_norm_g"], D_MODEL // N_CHIPS),
    ("od_b_in", (1, 2 * C_DIM // N_CHIPS), ["od_b_in"], 2 * C_DIM // N_CHIPS),
    ("od_ln_v_g", (1, C_DIM // N_CHIPS), ["od_ln_v_g"], C_DIM // N_CHIPS),
    ("od_ln_v_b", (1, C_DIM // N_CHIPS), ["od_ln_v_b"], C_DIM // N_CHIPS),
]


def _small_update(own, landed, weights):
    names = list(own.keys())
    n_g, n_w = len(names), len(SMALL_WEIGHTS)

    def body(*refs):
        refs = list(refs)
        own_refs = dict(zip(names, refs[:n_g]))
        land_refs = dict(zip(names, refs[n_g:2 * n_g]))
        wmv = [refs[2 * n_g + 3 * i:2 * n_g + 3 * i + 3] for i in range(n_w)]
        o0 = 2 * n_g + 3 * n_w
        loss_ref = refs[o0]
        outs = [refs[o0 + 1 + 4 * i:o0 + 5 + 4 * i] for i in range(n_w)]
        acc = dict(zip(names, refs[o0 + 1 + 4 * n_w:]))
        x, y, c = _mesh_pos()
        mine, chip = 4 * x + 2 * y + c, 2 * x + y

        for nm in names:
            for d in range(N_DEV):
                def add(term, nm=nm, d=d):
                    acc[nm][...] = term if d == 0 else acc[nm][...] + term
                pl.when(mine == d)(lambda nm=nm, add=add: add(own_refs[nm][...]))
                pl.when(mine != d)(lambda nm=nm, d=d, add=add: add(land_refs[nm][d]))
        loss_ref[...] = acc["loss"][...]

        def update(i, rows, g):
            w_ref, m_ref, v_ref = wmv[i]
            delta, m_new, v_new = _adam_math(w_ref[rows], m_ref[rows], v_ref[rows], g)
            for ref, val in zip(outs[i], (g, delta, m_new, v_new)):
                ref[rows] = val

        for i, (_, shape, grads, per_chip) in enumerate(SMALL_WEIGHTS):
            for row, gname in enumerate(grads):
                per_grad = shape[0] // len(grads)
                rows = slice(row * per_grad, (row + 1) * per_grad)
                if per_chip is None:
                    update(i, rows, acc[gname][...])
                else:
                    for q in range(N_CHIPS):
                        pl.when(chip == q)(lambda i=i, rows=rows, gname=gname, q=q, per_chip=per_chip:
                                           update(i, rows, acc[gname][:, q * per_chip:(q + 1) * per_chip]))

    operands = [own[nm] for nm in names] + [landed[nm] for nm in names]
    for nm, _, _, _ in SMALL_WEIGHTS:
        operands += list(weights[nm])
    out_shape = [jax.ShapeDtypeStruct((1, 1), F32)]
    for _, shape, _, _ in SMALL_WEIGHTS:
        out_shape += [jax.ShapeDtypeStruct(shape, F32)] * 4
    res = pl.pallas_call(
        body, name="small_update", grid=(1,),
        in_specs=[_full_spec(a.shape) for a in operands], out_specs=[_full_spec(s.shape) for s in out_shape],
        out_shape=out_shape, scratch_shapes=[pltpu.VMEM(own[nm].shape, F32) for nm in names],
        compiler_params=_params(32, 1),
    )(*operands)
    return res[0], {nm: res[1 + 4 * i:5 + 4 * i] for i, (nm, _, _, _) in enumerate(SMALL_WEIGHTS)}


def _adamw(w, m, v, grads, name, riders=()):
    layers, r, cdim = w.shape
    br = _block_rows(r, 256 if cdim > LANES else 1024)

    def body(*refs):
        w_ref, m_ref, v_ref = refs[:3]
        g_refs = refs[3:3 + layers]
        go_ref, d_ref, mo_ref, vo_ref = refs[3 + layers:]
        layer = pl.program_id(0)
        for l in range(layers):
            @pl.when(layer == l)
            def _(l=l):
                g = g_refs[l][...]
                go_ref[...] = g
                d_ref[...], mo_ref[...], vo_ref[...] = _adam_math(w_ref[...], m_ref[...], v_ref[...], g)

    spec3 = pl.BlockSpec((None, br, cdim), lambda l, i: (l, i, 0))
    spec2 = pl.BlockSpec((br, cdim), lambda l, i: (i, 0))
    out = jax.ShapeDtypeStruct((layers, r, cdim), F32)
    return _pallas(body, [_in_hbm(a) for a in (w, m, v, *grads)], name=name, grid=(layers, r // br),
                   in_specs=[spec3, spec3, spec3] + [spec2] * layers, out_specs=[spec3] * 4, out_shape=[out] * 4,
                   vmem_mib=32, riders=riders)


def _fill_shifted(buf, rows):
    for b in range(1, SUBLANES):
        buf[b, 0:rows - SUBLANES, :] = buf[0, b:b + rows - SUBLANES, :]


def _window(buf, start, size):
    return buf[start % SUBLANES, start - start % SUBLANES:start - start % SUBLANES + size, :]


def _conv31(src, w_ref, r0, base, init):
    acc = init
    for k in range(A_CONV_WIDTH):
        acc = acc + w_ref[k:k + 1, :] * _window(src, base + k + r0, CONV_ROWS)
    return acc


def _fwd_even(x, norm_g, w_in, conv_a_w, conv_a_b, ln_g, ln_b, conv_b_w, w_out, *, tm, seq, riders=()):
    tokens = x.shape[0]
    nt, tps = tokens // tm, seq // tm

    def body(x_ref, g_ref, win_hbm, caw_ref, cab_ref, lng_ref, lnb_ref, cbw_ref, wout_hbm,
             h_ref, n_ref, z_ref, a2_ref, cv_ref, mix_ref, win_v, wout_v, pa, pb, sem):
        i = pl.program_id(0)

        _load_weights([(win_hbm, win_v, False), (wout_hbm, wout_v, True)], sem)

        xv = x_ref[...]
        nf, _ = _rms_fwd(xv, g_ref[...])
        n = nf.astype(BF16)
        n_ref[...] = n
        z = jnp.concatenate([_dot(n, win_v[j]) for j in range(N_CHIPS)], axis=1)
        z_ref[...] = z.astype(BF16)
        a_val, a_gate = z[:, 0:A_DIM], z[:, A_DIM:2 * A_DIM]
        b_gate, c_gate, b_val = z[:, 1024:1536], z[:, 1536:2048], z[:, 2048:2560]

        first = (i % tps) == 0

        @pl.when(first)
        def _():
            pa[0, 0:A_HALO, :] = jnp.zeros((A_HALO, A_DIM), F32)
            pb[0:B_HALO, :] = jnp.zeros((B_HALO, B_DIM), F32)

        @pl.when(jnp.logical_not(first))
        def _():
            pa[0, 0:A_HALO, :] = pa[0, tm:tm + A_HALO, :]
            pb[0:B_HALO, :] = pb[tm:tm + B_HALO, :]

        pa[0, A_HALO:A_HALO + tm, :] = a_val * jax.nn.sigmoid(a_gate)
        pb[B_HALO:B_HALO + tm, :] = c_gate * b_val
        _fill_shifted(pa, A_HALO + tm)
        bias = jnp.broadcast_to(cab_ref[...], (CONV_ROWS, A_DIM))
        for r0 in range(0, tm, CONV_ROWS):
            a2_ref[r0:r0 + CONV_ROWS, :] = _conv31(pa, caw_ref, r0, A_HALO - (A_CONV_WIDTH - 1), bias)
        xhat, _ = _ln_stats(a2_ref[...])
        a3 = xhat * lng_ref[...] + lnb_ref[...]
        a4 = a3 * jax.nn.sigmoid(a3)
        cv = cbw_ref[0:1, :] * pb[B_HALO - 2:B_HALO - 2 + tm, :]
        cv = cv + cbw_ref[1:2, :] * pb[B_HALO - 1:B_HALO - 1 + tm, :]
        cv = cv + cbw_ref[2:3, :] * pb[B_HALO:B_HALO + tm, :]
        cv_ref[...] = cv.astype(BF16)
        mix = jnp.concatenate([a4, b_gate * cv], axis=1).astype(BF16)
        mix_ref[...] = mix
        h_ref[...] = xv + _dot(mix, wout_v[...])

    shp = lambda cols, dt: jax.ShapeDtypeStruct((tokens, cols), dt)
    return _pallas(
        body, [x, norm_g, w_in, conv_a_w, conv_a_b, ln_g, ln_b, conv_b_w, w_out], name="fwd_even", grid=(nt,),
        in_specs=[_row_spec(tm, D_MODEL), _full_spec((1, D_MODEL)), ANY, _full_spec((A_CONV_WIDTH, A_DIM)),
                  _full_spec((1, A_DIM)), _full_spec((1, A_DIM)), _full_spec((1, A_DIM)),
                  _full_spec((B_CONV_WIDTH, B_DIM)), ANY],
        out_specs=[_row_spec(tm, D_MODEL), _row_spec(tm, D_MODEL), _row_spec(tm, IN_EVEN), _row_spec(tm, A_DIM),
                   _row_spec(tm, B_DIM), _row_spec(tm, D_MODEL)],
        out_shape=[shp(D_MODEL, F32), shp(D_MODEL, BF16), shp(IN_EVEN, BF16), shp(A_DIM, F32), shp(B_DIM, BF16),
                   shp(D_MODEL, BF16)],
        scratch_shapes=[pltpu.VMEM((N_CHIPS, D_MODEL, IN_EVEN // N_CHIPS), BF16), pltpu.VMEM((D_MODEL, D_MODEL), BF16),
                        pltpu.VMEM((SUBLANES, A_HALO + tm, A_DIM), F32), pltpu.VMEM((B_HALO + tm, B_DIM), F32),
                        pltpu.SemaphoreType.DMA((N_LOADS,))],
        vmem_mib=56, riders=riders)


def _loss_tail(xv, g, target, loss_ref, dh_ref, dhb_ref, dg_ref):
    @pl.when(pl.program_id(0) == 0)
    def _():
        loss_ref[...] = jnp.zeros((1, 1), F32)
        dg_ref[...] = jnp.zeros((1, D_MODEL), F32)

    out, rstd = _rms_fwd(xv, g)
    err = out - target
    per_token = jnp.sum(err * err, axis=1, keepdims=True) * (1.0 / D_MODEL)
    loss_ref[...] += 0.5 * jnp.sum(per_token, axis=0, keepdims=True)
    dx, dg = _rms_bwd(err * (1.0 / D_MODEL), xv, rstd, g)
    dh_ref[...] = dx
    dhb_ref[...] = dx.astype(BF16)
    dg_ref[...] += dg


def _fwd_mlp(h, norm_g, w1, w2, layer, *, tm, riders=(), head=None):
    tokens = h.shape[0]
    nt = tokens // tm
    fs = D_FF // N_CHIPS
    n_in = 4 if head is None else 6

    def body(*refs):
        h_ref, g_ref, w1_hbm, w2_hbm = refs[:4]
        w1_v, w2_v, sem = refs[-3:]
        outs = refs[n_in:-3]
        n_ref, p_ref, q_ref = outs[1:4] if head is None else outs[0:3]
        _load_weights([(w1_hbm, w1_v, False), (w2_hbm, w2_v, False)], sem)

        xv = h_ref[...]
        nf, _ = _rms_fwd(xv, g_ref[...])
        n = nf.astype(BF16)
        n_ref[...] = n
        acc = xv
        for j in range(N_CHIPS):
            p = _dot(n, w1_v[j])
            p_ref[:, j * fs:(j + 1) * fs] = p.astype(BF16)
            r = jnp.maximum(p, 0.0)
            q = (r * r).astype(BF16)
            q_ref[:, j * fs:(j + 1) * fs] = q
            acc = acc + _dot(q, w2_v[j])
        if head is None:
            outs[0][...] = acc
        else:
            _loss_tail(acc, refs[4][...], refs[5][...], *outs[3:7])

    shp = lambda cols, dt: jax.ShapeDtypeStruct((tokens, cols), dt)
    saved_specs = [_row_spec(tm, D_MODEL), _row_spec(tm, D_FF), _row_spec(tm, D_FF)]
    saved_shapes = [shp(D_MODEL, BF16), shp(D_FF, BF16), shp(D_FF, BF16)]
    if head is None:
        operands, in_specs = [h, norm_g, w1, w2], [_row_spec(tm, D_MODEL), _full_spec((1, D_MODEL)), ANY, ANY]
        out_specs, out_shape = [_row_spec(tm, D_MODEL)] + saved_specs, [shp(D_MODEL, F32)] + saved_shapes
    else:
        operands = [h, norm_g, w1, w2, *head]
        in_specs = [_row_spec(tm, D_MODEL), _full_spec((1, D_MODEL)), ANY, ANY, _full_spec((1, D_MODEL)), _row_spec(tm, D_MODEL)]
        out_specs = saved_specs + [_full_spec((1, 1)), _row_spec(tm, D_MODEL), _row_spec(tm, D_MODEL), _full_spec((1, D_MODEL))]
        out_shape = saved_shapes + [jax.ShapeDtypeStruct((1, 1), F32), shp(D_MODEL, F32), shp(D_MODEL, BF16),
                                    jax.ShapeDtypeStruct((1, D_MODEL), F32)]
    return _pallas(
        body, operands, name=f"fwd_mlp{layer}", grid=(nt,), in_specs=in_specs, out_specs=out_specs, out_shape=out_shape,
        scratch_shapes=[pltpu.VMEM((N_CHIPS, D_MODEL, fs), BF16), pltpu.VMEM((N_CHIPS, fs, D_MODEL), BF16),
                        pltpu.SemaphoreType.DMA((N_LOADS,))],
        vmem_mib=56, riders=riders)


def _tril_mask():
    row = lax.broadcasted_iota(jnp.int32, (CHUNK, CHUNK), 0)
    col = lax.broadcasted_iota(jnp.int32, (CHUNK, CHUNK), 1)
    return row >= col


def _triu_mask():
    row = lax.broadcasted_iota(jnp.int32, (CHUNK, CHUNK), 0)
    col = lax.broadcasted_iota(jnp.int32, (CHUNK, CHUNK), 1)
    return row <= col


def _fwd_odd(h, norm_g, w_in, b_in, ln_g, ln_b, w_s, b_s_rows, w_out, *, tm, riders=()):
    tokens = h.shape[0]
    nt = tokens // tm
    cs = 2 * C_DIM // N_CHIPS

    def body(h_ref, g_ref, win_hbm, bin_ref, lng_ref, lnb_ref, ws_ref, bs_ref, wout_hbm,
             ho_ref, n_ref, s_ref, cdf_ref, sv_ref, y_ref, win_v, wout_v, bd, sem):
        _load_weights([(win_hbm, win_v, False), (wout_hbm, wout_v, True)], sem)

        @pl.when(pl.program_id(0) == 0)
        def _():
            mask = _tril_mask()
            bd[...] = jnp.zeros(bd.shape, BF16)
            for g in range(C_GROUPS):
                w = jnp.where(mask, ws_ref[g], 0.0).astype(BF16)
                bd[g, 0:CHUNK, 0:CHUNK] = w
                bd[g, CHUNK:PAIR, CHUNK:PAIR] = w

        xv = h_ref[...]
        nf, _ = _rms_fwd(xv, g_ref[...])
        n = nf.astype(BF16)
        n_ref[...] = n
        s = jnp.concatenate([_dot(n, win_v[j]) for j in range(N_CHIPS)], axis=1) + bin_ref[...]
        s_ref[...] = s.astype(BF16)
        cdf = _gelu_cdf(s)
        cdf_ref[...] = cdf.astype(BF16)
        zz = s * cdf
        u, v = zz[:, 0:C_DIM], zz[:, C_DIM:2 * C_DIM]
        xhat, _ = _ln_stats(v)
        vn = (xhat * lng_ref[...] + lnb_ref[...]).astype(BF16)
        for g in range(C_GROUPS):
            cols = slice(g * CHUNK, (g + 1) * CHUNK)
            bias = jnp.concatenate([bs_ref[g], bs_ref[g]], axis=0)
            for r0 in range(0, tm, PAIR):
                sv = _dot(bd[g], vn[r0:r0 + PAIR, cols]) + bias
                sv_ref[r0:r0 + PAIR, cols] = sv.astype(BF16)
                y_ref[r0:r0 + PAIR, cols] = (u[r0:r0 + PAIR, cols] * sv).astype(BF16)
        ho_ref[...] = xv + _dot(y_ref[...], wout_v[...])

    shp = lambda cols, dt: jax.ShapeDtypeStruct((tokens, cols), dt)
    return _pallas(
        body, [h, norm_g, w_in, b_in, ln_g, ln_b, w_s, b_s_rows, w_out], name="fwd_odd", grid=(nt,),
        in_specs=[_row_spec(tm, D_MODEL), _full_spec((1, D_MODEL)), ANY, _full_spec((1, 2 * C_DIM)),
                  _full_spec((1, C_DIM)), _full_spec((1, C_DIM)), _full_spec((C_GROUPS, CHUNK, CHUNK)),
                  _full_spec((C_GROUPS, CHUNK, CHUNK)), ANY],
        out_specs=[_row_spec(tm, D_MODEL), _row_spec(tm, D_MODEL), _row_spec(tm, 2 * C_DIM), _row_spec(tm, 2 * C_DIM),
                   _row_spec(tm, C_DIM), _row_spec(tm, C_DIM)],
        out_shape=[shp(D_MODEL, F32), shp(D_MODEL, BF16), shp(2 * C_DIM, BF16), shp(2 * C_DIM, BF16), shp(C_DIM, BF16),
                   shp(C_DIM, BF16)],
        scratch_shapes=[pltpu.VMEM((N_CHIPS, D_MODEL, cs), BF16), pltpu.VMEM((C_DIM, D_MODEL), BF16),
                        pltpu.VMEM((C_GROUPS, PAIR, PAIR), BF16), pltpu.SemaphoreType.DMA((N_LOADS,))],
        vmem_mib=56, riders=riders)


def _bwd_mlp(dh, h, norm_g, p, w1, w2, layer, *, tm, riders=()):
    tokens = h.shape[0]
    nt = tokens // tm
    fs = D_FF // N_CHIPS

    def body(dh_ref, h_ref, g_ref, p_ref, w1_hbm, w2_hbm, dx_ref, dxb_ref, dp_ref, dg_ref, w1_v, w2_v, sem):
        @pl.when(pl.program_id(0) == 0)
        def _():
            dg_ref[...] = jnp.zeros((1, D_MODEL), F32)

        _load_weights([(w1_hbm, w1_v, False), (w2_hbm, w2_v, False)], sem)

        dhv = dh_ref[...]
        dhb = dhv.astype(BF16)
        dn = jnp.zeros((tm, D_MODEL), F32)
        for j in range(N_CHIPS):
            dq = _dot_nt(dhb, w2_v[j])
            r = jnp.maximum(p_ref[:, j * fs:(j + 1) * fs].astype(F32), 0.0)
            dp = ((2.0 * r) * dq).astype(BF16)
            dp_ref[:, j * fs:(j + 1) * fs] = dp
            dn = dn + _dot_nt(dp, w1_v[j])
        xv = h_ref[...]
        g = g_ref[...]
        _, rstd = _rms_fwd(xv, g)
        dx, dg = _rms_bwd(dn, xv, rstd, g)
        dx_ref[...] = dhv + dx
        dxb_ref[...] = (dhv + dx).astype(BF16)
        dg_ref[...] += dg

    return _pallas(
        body, [dh, h, norm_g, p, w1, w2], name=f"bwd_mlp{layer}", grid=(nt,),
        in_specs=[_row_spec(tm, D_MODEL), _row_spec(tm, D_MODEL), _full_spec((1, D_MODEL)), _row_spec(tm, D_FF), ANY, ANY],
        out_specs=[_row_spec(tm, D_MODEL), _row_spec(tm, D_MODEL), _row_spec(tm, D_FF), _full_spec((1, D_MODEL))],
        out_shape=[jax.ShapeDtypeStruct((tokens, D_MODEL), F32), jax.ShapeDtypeStruct((tokens, D_MODEL), BF16),
                   jax.ShapeDtypeStruct((tokens, D_FF), BF16), jax.ShapeDtypeStruct((1, D_MODEL), F32)],
        scratch_shapes=[pltpu.VMEM((N_CHIPS, D_MODEL, fs), BF16), pltpu.VMEM((N_CHIPS, fs, D_MODEL), BF16),
                        pltpu.SemaphoreType.DMA((N_LOADS,))],
        vmem_mib=56, riders=riders)


def _bwd_odd(dh, h, norm_g, s, cdf, sv, w_in, ln_g, ln_b, w_s, w_out, *, tm, riders=()):
    tokens = h.shape[0]
    nt = tokens // tm
    cs = 2 * C_DIM // N_CHIPS

    def body(dh_ref, h_ref, g_ref, s_ref, cdf_ref, sv_ref, win_hbm, lng_ref, lnb_ref, ws_ref, wout_hbm,
             dx_ref, dxb_ref, ds_ref, dg_ref, dbin_ref, dlng_ref, dlnb_ref, dws_ref, dbs_ref,
             win_v, wout_v, bdt, dws_acc, dbs_acc, dvn, sem):
        i = pl.program_id(0)

        _load_weights([(win_hbm, win_v, False), (wout_hbm, wout_v, True)], sem)

        @pl.when(i == 0)
        def _():
            mask_t = _triu_mask()
            bdt[...] = jnp.zeros(bdt.shape, BF16)
            for g in range(C_GROUPS):
                wt = jnp.where(mask_t, ws_ref[g].T, 0.0).astype(BF16)
                bdt[g, 0:CHUNK, 0:CHUNK] = wt
                bdt[g, CHUNK:PAIR, CHUNK:PAIR] = wt
            dws_acc[...] = jnp.zeros(dws_acc.shape, F32)
            dbs_acc[...] = jnp.zeros(dbs_acc.shape, F32)
            dg_ref[...] = jnp.zeros(dg_ref.shape, F32)
            dbin_ref[...] = jnp.zeros(dbin_ref.shape, F32)
            dlng_ref[...] = jnp.zeros(dlng_ref.shape, F32)
            dlnb_ref[...] = jnp.zeros(dlnb_ref.shape, F32)

        dhv = dh_ref[...]
        dy = _dot_nt(dhv.astype(BF16), wout_v[...])
        sf = s_ref[...].astype(F32)
        cdf = cdf_ref[...].astype(F32)
        pdf = jnp.exp(-0.5 * sf * sf) * 0.3989422804014327
        zz = sf * cdf
        dgelu = cdf + sf * pdf
        u, v = zz[:, 0:C_DIM], zz[:, C_DIM:2 * C_DIM]
        xhat, rs = _ln_stats(v)
        lng = lng_ref[...]
        vn = (xhat * lng + lnb_ref[...]).astype(BF16)
        du = dy * sv_ref[...].astype(F32)
        dsv = dy * u
        dsvb = dsv.astype(BF16)
        for g in range(C_GROUPS):
            cols = slice(g * CHUNK, (g + 1) * CHUNK)
            for r0 in range(0, tm, PAIR):
                blk = dsvb[r0:r0 + PAIR, cols]
                dvn[r0:r0 + PAIR, cols] = _dot(bdt[g], blk)
                dws_acc[g] += _dot_nt(blk, vn[r0:r0 + PAIR, cols])
                dbs_acc[g] += dsv[r0:r0 + CHUNK, cols] + dsv[r0 + CHUNK:r0 + PAIR, cols]
        dv, dlng, dlnb = _ln_bwd(dvn[...], xhat, rs, lng)
        dlng_ref[...] += dlng
        dlnb_ref[...] += dlnb
        ds = jnp.concatenate([du, dv], axis=1) * dgelu
        dbin_ref[...] += jnp.sum(ds, axis=0, keepdims=True)
        dsb = ds.astype(BF16)
        ds_ref[...] = dsb
        dn = jnp.zeros((tm, D_MODEL), F32)
        for j in range(N_CHIPS):
            dn = dn + _dot_nt(dsb[:, j * cs:(j + 1) * cs], win_v[j])
        xv = h_ref[...]
        g = g_ref[...]
        _, rstd = _rms_fwd(xv, g)
        dx, dg = _rms_bwd(dn, xv, rstd, g)
        dx_ref[...] = dhv + dx
        dxb_ref[...] = (dhv + dx).astype(BF16)
        dg_ref[...] += dg

        @pl.when(i == nt - 1)
        def _():
            mask = _tril_mask()
            for g in range(C_GROUPS):
                full = dws_acc[g]
                dws_ref[g] = jnp.where(mask, full[0:CHUNK, 0:CHUNK] + full[CHUNK:PAIR, CHUNK:PAIR], 0.0)
                dbs_ref[g:g + 1, :] = jnp.sum(dbs_acc[g].T, axis=0, keepdims=True)

    row = lambda cols: jax.ShapeDtypeStruct((1, cols), F32)
    return _pallas(
        body, [dh, h, norm_g, s, cdf, sv, w_in, ln_g, ln_b, w_s, w_out], name="bwd_odd", grid=(nt,),
        in_specs=[_row_spec(tm, D_MODEL), _row_spec(tm, D_MODEL), _full_spec((1, D_MODEL)), _row_spec(tm, 2 * C_DIM),
                  _row_spec(tm, 2 * C_DIM), _row_spec(tm, C_DIM), ANY, _full_spec((1, C_DIM)), _full_spec((1, C_DIM)),
                  _full_spec((C_GROUPS, CHUNK, CHUNK)), ANY],
        out_specs=[_row_spec(tm, D_MODEL), _row_spec(tm, D_MODEL), _row_spec(tm, 2 * C_DIM), _full_spec((1, D_MODEL)),
                   _full_spec((1, 2 * C_DIM)),
                   _full_spec((1, C_DIM)), _full_spec((1, C_DIM)), _full_spec((C_GROUPS, CHUNK, CHUNK)),
                   _full_spec((C_GROUPS, CHUNK))],
        out_shape=[jax.ShapeDtypeStruct((tokens, D_MODEL), F32), jax.ShapeDtypeStruct((tokens, D_MODEL), BF16),
                   jax.ShapeDtypeStruct((tokens, 2 * C_DIM), BF16),
                   row(D_MODEL), row(2 * C_DIM), row(C_DIM), row(C_DIM),
                   jax.ShapeDtypeStruct((C_GROUPS, CHUNK, CHUNK), F32), jax.ShapeDtypeStruct((C_GROUPS, CHUNK), F32)],
        scratch_shapes=[pltpu.VMEM((N_CHIPS, D_MODEL, cs), BF16), pltpu.VMEM((C_DIM, D_MODEL), BF16),
                        pltpu.VMEM((C_GROUPS, PAIR, PAIR), BF16), pltpu.VMEM((C_GROUPS, PAIR, PAIR), F32),
                        pltpu.VMEM((C_GROUPS, CHUNK, CHUNK), F32), pltpu.VMEM((tm, C_DIM), F32),
                        pltpu.SemaphoreType.DMA((N_LOADS,))],
        vmem_mib=56, riders=riders)


def _bwd_even(dh, x, norm_g, z, a2, cv, w_in, conv_a_w, ln_g, ln_b, conv_b_w, w_out, *, tm, seq, riders=()):
    tokens = x.shape[0]
    nt, tps = tokens // tm, seq // tm
    ws = IN_EVEN // N_CHIPS

    def body(dh_ref, x_ref, g_ref, z_ref, a2_ref, cv_ref, win_hbm, caw_ref, lng_ref, lnb_ref, cbw_ref, wout_hbm,
             dx_ref, dz_ref, dg_ref, dcaw_ref, dcab_ref, dlng_ref, dlnb_ref, dcbw_ref,
             win_v, wout_v, ea, eb, a1s, da1s, sigs, wide, dw_acc, sem):
        i = pl.program_id(0)

        _load_weights([(win_hbm, win_v, False), (wout_hbm, wout_v, True)], sem)

        @pl.when(i == 0)
        def _():
            dw_acc[...] = jnp.zeros(dw_acc.shape, F32)
            for ref in (dg_ref, dcab_ref, dlng_ref, dlnb_ref, dcbw_ref):
                ref[...] = jnp.zeros(ref.shape, F32)

        last = ((nt - 1 - i) % tps) == tps - 1

        @pl.when(last)
        def _():
            ea[0, tm:tm + A_HALO, :] = jnp.zeros((A_HALO, A_DIM), F32)
            eb[tm:tm + B_HALO, :] = jnp.zeros((B_HALO, B_DIM), F32)

        @pl.when(jnp.logical_not(last))
        def _():
            ea[0, tm:tm + A_HALO, :] = ea[0, 0:A_HALO, :]
            eb[tm:tm + B_HALO, :] = eb[0:B_HALO, :]

        wide[...] = _dot_nt(dh_ref[...].astype(BF16), wout_v[...])
        lng, lnb = lng_ref[...], lnb_ref[...]
        zero_row = jnp.zeros((1, A_DIM), F32)
        dlng, dlnb, dcab = zero_row, zero_row, zero_row
        for r0 in range(0, tm, ELEM_ROWS):
            rows = slice(r0, r0 + ELEM_ROWS)
            a_val, a_gate = z_ref[rows, 0:A_DIM].astype(F32), z_ref[rows, A_DIM:2 * A_DIM].astype(F32)
            xhat, rs = _ln_stats(a2_ref[rows, :])
            a3 = xhat * lng + lnb
            sg = jax.nn.sigmoid(a3)
            da3 = wide[rows, 0:A_DIM] * (sg * (1.0 + a3 * (1.0 - sg)))
            da2, g_part, b_part = _ln_bwd(da3, xhat, rs, lng)
            dlng, dlnb, dcab = dlng + g_part, dlnb + b_part, dcab + jnp.sum(da2, axis=0, keepdims=True)
            ea[0, rows, :] = da2
            eb[rows, :] = wide[rows, A_DIM:A_DIM + B_DIM] * z_ref[rows, 1024:1536].astype(F32)
            sig = jax.nn.sigmoid(a_gate)
            sigs[rows, :] = sig
            a1s[rows, :] = a_val * sig
        dlng_ref[...] += dlng
        dlnb_ref[...] += dlnb
        dcab_ref[...] += dcab
        _fill_shifted(ea, tm + A_HALO)
        for r0 in range(0, tm, CONV_ROWS):
            acc = jnp.zeros((CONV_ROWS, A_DIM), F32)
            for j in range(A_CONV_WIDTH):
                acc = acc + caw_ref[A_CONV_WIDTH - 1 - j:A_CONV_WIDTH - j, :] * _window(ea, r0 + j, CONV_ROWS)
            da1s[r0:r0 + CONV_ROWS, :] = acc
        for j0 in range(0, A_CONV_WIDTH, DW_TAPS):
            taps = range(j0, min(j0 + DW_TAPS, A_CONV_WIDTH))
            part = [jnp.zeros((CONV_ROWS, A_DIM), F32) for _ in taps]
            for r0 in range(0, tm, CONV_ROWS):
                a1c = a1s[r0:r0 + CONV_ROWS, :]
                for u, j in enumerate(taps):
                    part[u] = part[u] + _window(ea, r0 + j, CONV_ROWS) * a1c
            for u, j in enumerate(taps):
                dw_acc[A_CONV_WIDTH - 1 - j] += part[u]
        dcbw = [jnp.zeros((1, B_DIM), F32) for _ in range(B_CONV_WIDTH)]
        for r0 in range(0, tm, ELEM_ROWS):
            rows = slice(r0, r0 + ELEM_ROWS)
            da1, sig = da1s[rows, :], sigs[rows, :]
            dz_ref[rows, 0:A_DIM] = (da1 * sig).astype(BF16)
            dz_ref[rows, A_DIM:2 * A_DIM] = (da1 * z_ref[rows, 0:A_DIM].astype(F32) * (sig * (1.0 - sig))).astype(BF16)
            c_gate, b_val = z_ref[rows, 1536:2048].astype(F32), z_ref[rows, 2048:2560].astype(F32)
            dz_ref[rows, 1024:1536] = (wide[rows, A_DIM:A_DIM + B_DIM] * cv_ref[rows, :].astype(F32)).astype(BF16)
            cb = c_gate * b_val
            dcb = jnp.zeros((ELEM_ROWS, B_DIM), F32)
            for j in range(B_CONV_WIDTH):
                k = B_CONV_WIDTH - 1 - j
                sl = eb[r0 + j:r0 + j + ELEM_ROWS, :]
                dcb = dcb + cbw_ref[k:k + 1, :] * sl
                dcbw[k] = dcbw[k] + jnp.sum(sl * cb, axis=0, keepdims=True)
            dz_ref[rows, 1536:2048] = (dcb * b_val).astype(BF16)
            dz_ref[rows, 2048:2560] = (dcb * c_gate).astype(BF16)
        for k in range(B_CONV_WIDTH):
            dcbw_ref[k:k + 1, :] += dcbw[k]
        dn = jnp.zeros((tm, D_MODEL), F32)
        for j in range(N_CHIPS):
            dn = dn + _dot_nt(dz_ref[:, j * ws:(j + 1) * ws], win_v[j])
        wide[...] = dn
        g = g_ref[...]
        dg = jnp.zeros((1, D_MODEL), F32)
        for r0 in range(0, tm, ELEM_ROWS):
            rows = slice(r0, r0 + ELEM_ROWS)
            xv = x_ref[rows, :]
            _, rstd = _rms_fwd(xv, g)
            dx, dg_part = _rms_bwd(wide[rows, :], xv, rstd, g)
            dx_ref[rows, :] = dh_ref[rows, :] + dx
            dg = dg + dg_part
        dg_ref[...] += dg

        @pl.when(i == nt - 1)
        def _():
            for k in range(A_CONV_WIDTH):
                dcaw_ref[k:k + 1, :] = jnp.sum(dw_acc[k], axis=0, keepdims=True)

    row = lambda cols: jax.ShapeDtypeStruct((1, cols), F32)
    rs_ = functools.partial(_row_spec, rev_nt=nt)
    return _pallas(
        body, [dh, x, norm_g, z, a2, cv, w_in, conv_a_w, ln_g, ln_b, conv_b_w, w_out], name="bwd_even", grid=(nt,),
        in_specs=[rs_(tm, D_MODEL), rs_(tm, D_MODEL), _full_spec((1, D_MODEL)), rs_(tm, IN_EVEN), rs_(tm, A_DIM),
                  rs_(tm, B_DIM), ANY, _full_spec((A_CONV_WIDTH, A_DIM)), _full_spec((1, A_DIM)), _full_spec((1, A_DIM)),
                  _full_spec((B_CONV_WIDTH, B_DIM)), ANY],
        out_specs=[rs_(tm, D_MODEL), rs_(tm, IN_EVEN), _full_spec((1, D_MODEL)), _full_spec((A_CONV_WIDTH, A_DIM)),
                   _full_spec((1, A_DIM)), _full_spec((1, A_DIM)), _full_spec((1, A_DIM)), _full_spec((B_CONV_WIDTH, B_DIM))],
        out_shape=[jax.ShapeDtypeStruct((tokens, D_MODEL), F32), jax.ShapeDtypeStruct((tokens, IN_EVEN), BF16),
                   row(D_MODEL), jax.ShapeDtypeStruct((A_CONV_WIDTH, A_DIM), F32), row(A_DIM), row(A_DIM), row(A_DIM),
                   jax.ShapeDtypeStruct((B_CONV_WIDTH, B_DIM), F32)],
        scratch_shapes=[pltpu.VMEM((N_CHIPS, D_MODEL, ws), BF16), pltpu.VMEM((D_MODEL, D_MODEL), BF16),
                        pltpu.VMEM((SUBLANES, tm + A_HALO, A_DIM), F32), pltpu.VMEM((tm + B_HALO, B_DIM), F32),
                        pltpu.VMEM((tm, A_DIM), F32), pltpu.VMEM((tm, A_DIM), F32), pltpu.VMEM((tm, A_DIM), F32),
                        pltpu.VMEM((tm, D_MODEL), F32),
                        pltpu.VMEM((A_CONV_WIDTH, CONV_ROWS, A_DIM), F32), pltpu.SemaphoreType.DMA((N_LOADS,))],
        vmem_mib=56, riders=riders)


def _wgrad(a, b, name, *, col_shards, riders=()):
    tokens, m = a.shape
    n = b.shape[1]
    kc = 512
    if col_shards:
        bm, bn = m // 2, n // N_CHIPS
        grid = (2, N_CHIPS)
        out_spec = pl.BlockSpec((None, None, bm, bn), lambda i, j: (j, i, 0, 0))
    elif m // 8 >= MXU_ROWS:
        bm, bn = m // 8, n
        grid = (8, 1)
        out_spec = pl.BlockSpec((None, None, bm, bn), lambda i, j: (i // 2, i % 2, 0, 0))
    else:
        bm, bn = m // N_CHIPS, n
        grid = (N_CHIPS, 1)
        out_spec = pl.BlockSpec((None, 2, bm // 2, bn), lambda i, j: (i, 0, 0, 0))

    def body(a_ref, b_ref, o_ref):
        acc = jnp.zeros((bm, bn), F32)
        for k0 in range(0, tokens, kc):
            acc = acc + _dot_tn(a_ref[k0:k0 + kc, :].astype(BF16), b_ref[k0:k0 + kc, :].astype(BF16))
        if len(o_ref.shape) == 3:
            o_ref[0] = acc[0:bm // 2]
            o_ref[1] = acc[bm // 2:bm]
        else:
            o_ref[...] = acc

    out_rows = m // 2 if col_shards else m // 8
    outs, routs = _pallas(
        body, [a, b], name=name, grid=grid,
        in_specs=[pl.BlockSpec((tokens, bm), lambda i, j: (0, i)), pl.BlockSpec((tokens, bn), lambda i, j: (0, j))],
        out_specs=[out_spec], out_shape=[jax.ShapeDtypeStruct((N_CHIPS, 2, out_rows, bn), F32)],
        vmem_mib=56, riders=riders)
    return outs[0], routs


def _wgrad_pair(a, b, name, *, col_shards, riders=()):
    tokens, m = a.shape
    n = b.shape[1]
    kc = 512
    c0 = lax.axis_index("c")

    def half(ph, pre):
        return (ph + 1 + pre[0]) % 2

    if col_shards:
        bm, bn = m // 2, n // N_CHIPS
        a_spec = pl.BlockSpec((tokens, bm), lambda ph, q, pre: (0, half(ph, pre)))
        b_spec = pl.BlockSpec((tokens, bn), lambda ph, q, pre: (0, q))
    else:
        bm, bn = m // 8, n
        a_spec = pl.BlockSpec((tokens, bm), lambda ph, q, pre: (0, 2 * q + half(ph, pre)))
        b_spec = pl.BlockSpec((tokens, bn), lambda ph, q, pre: (0, 0))

    def body(pre_ref, a_ref, b_ref, o_ref, give, got, send_sems, recv_sems):
        ph, q = pl.program_id(0), pl.program_id(1)
        acc = jnp.zeros((bm, bn), F32)
        for k0 in range(0, tokens, kc):
            acc = acc + _dot_tn(a_ref[k0:k0 + kc, :].astype(BF16), b_ref[k0:k0 + kc, :].astype(BF16))
        x, y, cc = _mesh_pos()

        def tile(t):
            return _remote(give.at[t], got.at[t], send_sems.at[t], recv_sems.at[t], (x, y, 1 - cc))

        @pl.when(ph == 0)
        def _():
            give[q] = acc
            tile(q).start()

        @pl.when(ph == 1)
        def _():
            tile(q).wait_recv()
            o_ref[...] = (acc + got[q]).astype(BF16)

        @pl.when((ph == 1) & (q == N_CHIPS - 1))
        def _():
            for t in range(N_CHIPS):
                tile(t).wait_send()

    outs, routs = _pallas(
        body, [a, b], name=name, grid=(2, N_CHIPS), in_specs=[a_spec, b_spec],
        out_specs=[pl.BlockSpec((None, bm, bn), lambda ph, q, pre: (ph * q, 0, 0))],
        out_shape=[jax.ShapeDtypeStruct((N_CHIPS, bm, bn), BF16)],
        scratch_shapes=[pltpu.VMEM((N_CHIPS, bm, bn), F32), pltpu.VMEM((N_CHIPS, bm, bn), F32),
                        pltpu.SemaphoreType.DMA((N_CHIPS,)), pltpu.SemaphoreType.DMA((N_CHIPS,))],
        vmem_mib=56, riders=riders, prefetch=jnp.reshape(c0, (1,)).astype(jnp.int32))
    return outs[0], routs


class _GradReduce:
    def __init__(self, name, grad=None, chip_sum=None):
        self.name, self.grad, self.chip_sum = name, grad, chip_sum
        self.full = None

    def pair_swap(self):
        return _PairSwap([self.grad])

    def took_pair(self, outs):
        self.chip_sum = _in_hbm(_add_pair(self.grad, outs[0], f"pair_sum_{self.name}"))

    def took_chips(self, outs):
        self.full = _in_hbm(_add_chips(self.chip_sum, outs[0], f"chip_sum_{self.name}"))

    def chips_beside(self, collective_id):
        self.took_chips([_chip_swap_beside(self.chip_sum, f"chip_swap_{self.name}", collective_id)])

    def pair_share(self):
        return _PairShare([self.full])

    def took_share(self, outs):
        self.full = outs[0]

    def reduced(self):
        return jnp.reshape(self.full, (2 * self.full.shape[1], self.full.shape[2]))


def _forward_backward(x2, tgt2, w, conv_a_w, conv_b_w, od_norm, od_bias, od_lng, od_lnb,
                      ev_norm_g, ev_conv_a_b, ev_ln_a_g, ev_ln_a_b, od_w_s, od_b_s, mlp_norm_g, final_norm_g,
                      *, tm, seq, distributed=True):
    d = x2.shape[1]
    b_s_rows = jnp.broadcast_to(od_b_s[0][:, :, None], (C_GROUPS, CHUNK, CHUNK))
    (h1, n0, z, a2, cv, mix), _ = _fwd_even(
        x2, ev_norm_g, w["ev_in"], conv_a_w, ev_conv_a_b, ev_ln_a_g, ev_ln_a_b, conv_b_w, w["ev_out"], tm=tm, seq=seq)
    (h2, n1, p0, q0), _ = _fwd_mlp(h1, mlp_norm_g[0:1], w["w1_0"], w["w2_0"], 0, tm=tm)
    (h3, n2, s, cdf, sv, y), _ = _fwd_odd(h2, od_norm, w["od_in"], od_bias, od_lng, od_lnb, od_w_s[0], b_s_rows,
                                          w["od_out"], tm=tm)
    (n3, p1, q1, loss_part, dh4, dh4b, d_final_g), _ = _fwd_mlp(
        h3, mlp_norm_g[1:2], w["w1_1"], w["w2_1"], 1, tm=tm,
        head=(jnp.reshape(final_norm_g, (1, d)), tgt2))

    red = {}

    def swap(*names):
        return [red[nm].pair_swap() for nm in names] if distributed else []

    def share(*names):
        return [red[nm].pair_share() for nm in names] if distributed else []

    def took(routs, *steps):
        if distributed:
            for (nm, what), outs in zip(steps, routs):
                getattr(red[nm], what)(outs)

    swap_ids = iter(range(FIRST_SWAP_ID, FIRST_SWAP_ID + 8))

    def beside(name):
        if distributed:
            red[name].chips_beside(next(swap_ids))

    def big(lhs, rhs, name, col_shards, riders=()):
        if distributed:
            chip_sum, routs = _wgrad_pair(lhs, rhs, f"wgrad_{name}", col_shards=col_shards, riders=riders)
            red[name] = _GradReduce(name, chip_sum=_in_hbm(chip_sum))
        else:
            g, routs = _wgrad(lhs, rhs, f"wgrad_{name}", col_shards=col_shards)
            red[name] = _GradReduce(name, grad=g)
        return routs

    big(q1, dh4b, "w2_1", False)
    beside("w2_1")
    (dh3, dh3b, dp1, d_mlp_g1), _ = _bwd_mlp(dh4, h3, mlp_norm_g[1:2], p1, w["w1_1"], w["w2_1"], 1, tm=tm)
    big(n3, dp1, "w1_1", True)
    beside("w1_1")
    g, routs = _wgrad(y, dh3b, "wgrad_od_out", col_shards=False, riders=share("w2_1"))
    red["od_out"] = _GradReduce("od_out", grad=g)
    took(routs, ("w2_1", "took_share"))
    (dh2, dh2b, ds, d_od_norm, d_od_bin, d_od_lng, d_od_lnb, d_ws, d_bs), _ = _bwd_odd(
        dh3, h2, od_norm, s, cdf, sv, w["od_in"], od_lng, od_lnb, od_w_s[0], w["od_out"], tm=tm)
    routs = big(n2, ds, "od_in", True, riders=share("w1_1") + swap("od_out"))
    took(routs, ("w1_1", "took_share"), ("od_out", "took_pair"))
    beside("od_in")
    beside("od_out")
    early = {"loss": loss_part, "od_w_s": d_ws, "od_b_s": d_bs, "mlp_norm_g1": d_mlp_g1, "final_norm_g": d_final_g,
             "od_norm_g": d_od_norm, "od_b_in": d_od_bin, "od_ln_v_g": d_od_lng, "od_ln_v_b": d_od_lnb}
    landed_early = _share_all_beside(list(early.values()), "small_share_early", FIRST_SHARE_ID) if distributed else []
    big(q0, dh2b, "w2_0", False)
    beside("w2_0")
    (dh1, dh1b, dp0, d_mlp_g0), _ = _bwd_mlp(dh2, h1, mlp_norm_g[0:1], p0, w["w1_0"], w["w2_0"], 0, tm=tm)
    g, _ = _wgrad(mix, dh1b, "wgrad_ev_out", col_shards=False)
    red["ev_out"] = _GradReduce("ev_out", grad=g)
    routs = big(n1, dp0, "w1_0", True, riders=share("od_out") + share("od_in") + share("w2_0") + swap("ev_out"))
    took(routs, ("od_out", "took_share"), ("od_in", "took_share"), ("w2_0", "took_share"), ("ev_out", "took_pair"))
    beside("w1_0")
    beside("ev_out")

    (dx, dz, d_ev_norm, d_caw, d_cab, d_ev_lng, d_ev_lnb, d_cbw), _ = _bwd_even(
        dh1, x2, ev_norm_g, z, a2, cv, w["ev_in"], conv_a_w, ev_ln_a_g, ev_ln_a_b, conv_b_w, w["ev_out"], tm=tm, seq=seq)
    late = {"mlp_norm_g0": d_mlp_g0, "ev_norm_g": d_ev_norm, "ev_conv_a_b": d_cab, "ev_ln_a_g": d_ev_lng,
            "ev_ln_a_b": d_ev_lnb, "ev_conv_a_w": d_caw, "ev_conv_b_w": d_cbw}
    landed_late = _share_all_beside(list(late.values()), "small_share_late", FIRST_SHARE_ID + 1) if distributed else []
    routs = big(n0, dz, "ev_in", True, riders=share("ev_out") + share("w1_0"))
    took(routs, ("ev_out", "took_share"), ("w1_0", "took_share"))
    beside("ev_in")
    own = {**early, **late}
    landed = dict(zip(own.keys(), landed_early + landed_late)) if distributed else None
    return dx, red, own, landed


def _rows128(a):
    rows = jnp.reshape(a, (-1, LANES))
    pad = (-rows.shape[0]) % SUBLANES
    return jnp.pad(rows, ((0, pad), (0, 0))) if pad else rows


def _pack(arrays):
    return jnp.concatenate([_rows128(a) for a in arrays], axis=0)


def _unpack(buf, shapes):
    out, r0 = [], 0
    for shp in shapes:
        size = 1
        for dim in shp:
            size *= dim
        nr = size // LANES
        out.append(jnp.reshape(buf[r0:r0 + nr], shp))
        r0 += nr + (-nr) % SUBLANES
    return out


def kernel(x, ev_norm_g, ev_w_in, ev_conv_a_w, ev_conv_a_b, ev_ln_a_g, ev_ln_a_b, ev_conv_b_w, ev_w_out, od_norm_g, od_w_in, od_b_in, od_ln_v_g, od_ln_v_b, od_w_s, od_b_s, od_w_out, mlp_norm_g, mlp_w1, mlp_w2, final_norm_g, loss_target, m_ev_norm_g, m_ev_w_in, m_ev_conv_a_w, m_ev_conv_a_b, m_ev_ln_a_g, m_ev_ln_a_b, m_ev_conv_b_w, m_ev_w_out, m_od_norm_g, m_od_w_in, m_od_b_in, m_od_ln_v_g, m_od_ln_v_b, m_od_w_s, m_od_b_s, m_od_w_out, m_mlp_norm_g, m_mlp_w1, m_mlp_w2, m_final_norm_g, v_ev_norm_g, v_ev_w_in, v_ev_conv_a_w, v_ev_conv_a_b, v_ev_ln_a_g, v_ev_ln_a_b, v_ev_conv_b_w, v_ev_w_out, v_od_norm_g, v_od_w_in, v_od_b_in, v_od_ln_v_g, v_od_ln_v_b, v_od_w_s, v_od_b_s, v_od_w_out, v_mlp_norm_g, v_mlp_w1, v_mlp_w2, v_final_norm_g):
    tm = TOKEN_TILE
    batch, seq, d = x.shape
    tokens = batch * seq
    x2 = jnp.reshape(x, (tokens, d))
    tgt2 = jnp.reshape(loss_target, (tokens, d))
    chip = 2 * lax.axis_index("x") + lax.axis_index("y")

    small_shapes = [(A_CONV_WIDTH, LANES), (B_CONV_WIDTH, LANES), (256,), (512,), (256,), (256,)]
    small_shard = _pack([ev_conv_a_w[0], ev_conv_b_w[0], od_norm_g[0], od_b_in[0], od_ln_v_g[0], od_ln_v_b[0]])
    small_shard = jnp.pad(small_shard, ((0, (-small_shard.shape[0]) % (2 * SUBLANES)), (0, 0)))
    first = [_place_shard(ev_w_in, 0, BF16, "place_ev_w_in"), _place_shard(ev_w_out, 0, BF16, "place_ev_w_out"),
             _place_shard(small_shard[None], 0, F32, "place_small")]
    staged = {
        "w1_0": _place_shard(mlp_w1, 0, BF16, "place_w1_0"), "w2_0": _place_shard(mlp_w2, 0, BF16, "place_w2_0"),
        "od_in": _place_shard(od_w_in, 0, BF16, "place_od_w_in"), "od_out": _place_shard(od_w_out, 0, BF16, "place_od_w_out"),
        "w1_1": _place_shard(mlp_w1, 1, BF16, "place_w1_1"), "w2_1": _place_shard(mlp_w2, 1, BF16, "place_w2_1"),
    }
    first = [_in_hbm(a) for a in first]
    staged = {nm: _in_hbm(a) for nm, a in staged.items()}
    g_ev_in, g_ev_out, g_small = _gather_beside(first, "gather_stage0", collective_id=1)
    gathered = {"ev_in": g_ev_in, "ev_out": g_ev_out}
    for stage, names in enumerate((("w1_0", "w2_0"), ("od_in", "od_out", "w1_1"), ("w2_1",))):
        done = _gather_beside([staged[nm] for nm in names], f"gather_stage{stage + 1}", collective_id=stage + 2)
        gathered.update(zip(names, done))
    small_all = jnp.reshape(_plain_copy(g_small, "small_weights_copy"), (N_CHIPS, -1, LANES))
    per_chip = [_unpack(small_all[q], small_shapes) for q in range(N_CHIPS)]
    conv_a_w = jnp.concatenate([pc[0] for pc in per_chip], axis=1)
    conv_b_w = jnp.concatenate([pc[1] for pc in per_chip], axis=1)
    od_norm = jnp.concatenate([pc[2] for pc in per_chip])[None, :]
    od_bias = jnp.concatenate([pc[3] for pc in per_chip])[None, :]
    od_lng = jnp.concatenate([pc[4] for pc in per_chip])[None, :]
    od_lnb = jnp.concatenate([pc[5] for pc in per_chip])[None, :]

    dx, red, own, landed = _forward_backward(
        x2, tgt2, gathered, conv_a_w, conv_b_w, od_norm, od_bias, od_lng, od_lnb,
        ev_norm_g, ev_conv_a_b, ev_ln_a_g, ev_ln_a_b, od_w_s, od_b_s, mlp_norm_g, final_norm_g, tm=tm, seq=seq)

    routs = _exchange([red["ev_in"].pair_share()], "reduce_tail")
    red["ev_in"].took_share(routs[0])

    given = {"ev_norm_g": (ev_norm_g, m_ev_norm_g, v_ev_norm_g), "ev_conv_a_b": (ev_conv_a_b, m_ev_conv_a_b, v_ev_conv_a_b),
             "ev_ln_a_g": (ev_ln_a_g, m_ev_ln_a_g, v_ev_ln_a_g), "ev_ln_a_b": (ev_ln_a_b, m_ev_ln_a_b, v_ev_ln_a_b),
             "od_w_s": (od_w_s, m_od_w_s, v_od_w_s), "od_b_s": (od_b_s, m_od_b_s, v_od_b_s),
             "mlp_norm_g": (mlp_norm_g, m_mlp_norm_g, v_mlp_norm_g), "final_norm_g": (final_norm_g, m_final_norm_g, v_final_norm_g),
             "ev_conv_a_w": (ev_conv_a_w, m_ev_conv_a_w, v_ev_conv_a_w), "ev_conv_b_w": (ev_conv_b_w, m_ev_conv_b_w, v_ev_conv_b_w),
             "od_norm_g": (od_norm_g, m_od_norm_g, v_od_norm_g), "od_b_in": (od_b_in, m_od_b_in, v_od_b_in),
             "od_ln_v_g": (od_ln_v_g, m_od_ln_v_g, v_od_ln_v_g), "od_ln_v_b": (od_ln_v_b, m_od_ln_v_b, v_od_ln_v_b)}
    shaped = {nm: tuple(jnp.reshape(a, shape) for a in given[nm]) for nm, shape, _, _ in SMALL_WEIGHTS}
    loss11, small_upd = _small_update(own, landed, shaped)
    loss = loss11[0, 0]
    upd = {nm: [jnp.reshape(o, given[nm][0].shape) for o in outs] for nm, outs in small_upd.items()}

    def big_update(wt, m, v, names, call):
        grads = [red[nm].reduced() for nm in names]
        shp3 = (len(grads),) + grads[0].shape
        outs, _ = _adamw(jnp.reshape(wt, shp3), jnp.reshape(m, shp3), jnp.reshape(v, shp3), grads, call)
        return [jnp.reshape(o, wt.shape) for o in outs], None

    upd["mlp_w2"], _ = big_update(mlp_w2, m_mlp_w2, v_mlp_w2, ["w2_0", "w2_1"], "adamw_mlp_w2")
    upd["mlp_w1"], _ = big_update(mlp_w1, m_mlp_w1, v_mlp_w1, ["w1_0", "w1_1"], "adamw_mlp_w1")
    upd["ev_w_in"], _ = big_update(ev_w_in, m_ev_w_in, v_ev_w_in, ["ev_in"], "adamw_ev_w_in")
    upd["ev_w_out"], _ = big_update(ev_w_out, m_ev_w_out, v_ev_w_out, ["ev_out"], "adamw_ev_w_out")
    upd["od_w_in"], _ = big_update(od_w_in, m_od_w_in, v_od_w_in, ["od_in"], "adamw_od_w_in")
    upd["od_w_out"], _ = big_update(od_w_out, m_od_w_out, v_od_w_out, ["od_out"], "adamw_od_w_out")

    order = ["ev_norm_g", "ev_w_in", "ev_conv_a_w", "ev_conv_a_b", "ev_ln_a_g", "ev_ln_a_b", "ev_conv_b_w", "ev_w_out",
             "od_norm_g", "od_w_in", "od_b_in", "od_ln_v_g", "od_ln_v_b", "od_w_s", "od_b_s", "od_w_out", "mlp_norm_g",
             "mlp_w1", "mlp_w2", "final_norm_g"]
    grad_x = jnp.reshape(dx, x.shape)
    return (loss, grad_x, *[upd[nm][0] for nm in order], *[upd[nm][1] for nm in order],
            *[upd[nm][2] for nm in order], *[upd[nm][3] for nm in order])
```

```python
import functools

import jax
import jax.numpy as jnp
from jax import lax
from jax.experimental import pallas as pl
from jax.experimental.pallas import tpu as pltpu
from jax.experimental.pallas import tpu_sc as plsc

F32 = jnp.float32
BF16 = jnp.bfloat16

D_MODEL = 1024
A_DIM = 512
B_DIM = 512
IN_EVEN = 2 * A_DIM + 3 * B_DIM
A_CONV_WIDTH = 31
B_CONV_WIDTH = 3
CHUNK = 128
C_GROUPS = 8
C_DIM = 1024
D_FF = 4096
RMS_EPS = 1e-6
LN_EPS = 1e-5
ADAM_LR = 0.001
ADAM_B1 = 0.9
ADAM_B2 = 0.999
ADAM_EPS = 1e-08
ADAM_WD = 0.01
ADAM_STEP = 10

N_CHIPS = 4
N_DEV = 8
TOKEN_TILE = 512
A_HALO = 32
B_HALO = 8
CONV_ROWS = 16
DW_TAPS = 4
ELEM_ROWS = 16
PAIR = 2 * CHUNK
LANES = 128
SUBLANES = 8
MXU_ROWS = 256
MIB = 1024 * 1024
MESH = pl.DeviceIdType.MESH
ANY = pl.BlockSpec(memory_space=pl.ANY)


def _dot(a, b):
    return lax.dot_general(a, b, (((1,), (0,)), ((), ())), preferred_element_type=F32)


def _dot_nt(a, b):
    return lax.dot_general(a, b, (((1,), (1,)), ((), ())), preferred_element_type=F32)


def _dot_tn(a, b):
    return lax.dot_general(a, b, (((0,), (0,)), ((), ())), preferred_element_type=F32)


def _params(vmem_mib, n_axes=1):
    return pltpu.CompilerParams(dimension_semantics=("arbitrary",) * n_axes, vmem_limit_bytes=vmem_mib * MIB)


def _row_spec(tm, cols, rev_nt=None):
    if rev_nt is None:
        return pl.BlockSpec((tm, cols), lambda i: (i, 0))
    return pl.BlockSpec((tm, cols), lambda i: (rev_nt - 1 - i, 0))


def _full_spec(shape):
    nd = len(shape)
    return pl.BlockSpec(shape, lambda i: (0,) * nd)


def _block_rows(rows, cap=512):
    best = SUBLANES
    for br in range(SUBLANES, min(rows, cap) + 1, SUBLANES):
        if rows % br == 0:
            best = br
    return best


FIRST_SWAP_ID = 5
N_LOADS = 2 * 2 * N_CHIPS


def _load_weights(loads, sems):
    @pl.when(pl.program_id(0) == 0)
    def _():
        copies = []
        for src, dst, rows_of_one in loads:
            r = src.shape[2]
            for q in range(N_CHIPS):
                for h in range(2):
                    part = dst.at[pl.ds((2 * q + h) * r, r)] if rows_of_one else dst.at[q, pl.ds(h * r, r)]
                    copies.append(pltpu.make_async_copy(src.at[q, h], part, sems.at[len(copies)]))
        for cp in copies:
            cp.start()
        for cp in copies:
            cp.wait()


def _rms_fwd(x, g):
    rstd = lax.rsqrt(jnp.mean(x * x, axis=-1, keepdims=True) + RMS_EPS)
    return x * rstd * g, rstd


def _rms_bwd(dn, x, rstd, g):
    a = dn * g
    xh = x * rstd
    dx = rstd * (a - xh * jnp.mean(a * xh, axis=-1, keepdims=True))
    dg = jnp.sum(dn * xh, axis=0, keepdims=True)
    return dx, dg


def _ln_stats(v):
    mu = jnp.mean(v, axis=-1, keepdims=True)
    xc = v - mu
    rs = lax.rsqrt(jnp.mean(xc * xc, axis=-1, keepdims=True) + LN_EPS)
    return xc * rs, rs


def _ln_bwd(dy, xhat, rs, g):
    dxh = dy * g
    dv = rs * (dxh - jnp.mean(dxh, axis=-1, keepdims=True) - xhat * jnp.mean(dxh * xhat, axis=-1, keepdims=True))
    return dv, jnp.sum(dy * xhat, axis=0, keepdims=True), jnp.sum(dy, axis=0, keepdims=True)


def _gelu_cdf(s):
    return 0.5 * (1.0 + lax.erf(s * 0.7071067811865476))


def _mesh_pos():
    return lax.axis_index("x"), lax.axis_index("y"), lax.axis_index("c")


def _other_chips(x, y):
    return [(1 - x, y), (x, 1 - y), (1 - x, 1 - y)]


def _remote(src, dst, send_sem, recv_sem, to):
    return pltpu.make_async_remote_copy(src_ref=src, dst_ref=dst, send_sem=send_sem, recv_sem=recv_sem,
                                        device_id=to, device_id_type=MESH)


def _like(arrays):
    return [jax.ShapeDtypeStruct(a.shape, a.dtype) for a in arrays]


class _Gather:
    def __init__(self, bufs):
        self.ins = list(bufs)
        self.out_shapes = _like(bufs)
        self.aliases = {t: t for t in range(len(bufs))}
        self.n_sems = 6 * len(bufs)

    def _ici(self, ins, outs, send, recv, t, k, chip, mine, c):
        return _remote(ins[t].at[mine, c], outs[t].at[mine, c], send.at[6 * t + k], recv.at[6 * t + k], (*chip, c))

    def start(self, ins, outs, send, recv):
        x, y, c = _mesh_pos()
        for t in range(len(ins)):
            for k, chip in enumerate(_other_chips(x, y)):
                self._ici(ins, outs, send, recv, t, k, chip, 2 * x + y, c).start()

    def _pass_on(self, outs, send, recv, t, k, chip, c, to):
        blk = outs[t].at[2 * chip[0] + chip[1], c]
        return _remote(blk, blk, send.at[6 * t + 3 + k], recv.at[6 * t + 3 + k], to)

    def near_end(self, ins, outs, send, recv):
        x, y, c = _mesh_pos()
        for t in range(len(ins)):
            for k, chip in enumerate(_other_chips(x, y)):
                blk = outs[t].at[2 * chip[0] + chip[1], c]
                _remote(blk, blk, send.at[6 * t + k], recv.at[6 * t + k], (x, y, c)).wait_recv()
                self._pass_on(outs, send, recv, t, k, chip, c, (x, y, 1 - c)).start()

    def finish(self, ins, outs, send, recv):
        x, y, c = _mesh_pos()
        chips = _other_chips(x, y)
        for t in range(len(ins)):
            for k, chip in enumerate(chips):
                self._pass_on(outs, send, recv, t, k, chip, 1 - c, (x, y, c)).wait_recv()
        for t in range(len(ins)):
            for k, chip in enumerate(chips):
                self._ici(ins, outs, send, recv, t, k, chip, 2 * x + y, c).wait_send()
                self._pass_on(outs, send, recv, t, k, chip, c, (x, y, 1 - c)).wait_send()


class _PairSwap:
    def __init__(self, grads):
        self.ins = list(grads)
        self.out_shapes = [jax.ShapeDtypeStruct((g.shape[0],) + g.shape[2:], g.dtype) for g in grads]
        self.aliases = {}
        self.n_sems = len(grads)

    def _copies(self, ins, outs, send, recv):
        x, y, c = _mesh_pos()
        return [_remote(ins[t].at[:, 1 - c], outs[t], send.at[t], recv.at[t], (x, y, 1 - c)) for t in range(len(ins))]

    def start(self, ins, outs, send, recv):
        for cp in self._copies(ins, outs, send, recv):
            cp.start()

    def finish(self, ins, outs, send, recv):
        for cp in self._copies(ins, outs, send, recv):
            cp.wait()


class _ChipSwap:
    def __init__(self, parts):
        self.ins = list(parts)
        self.out_shapes = [jax.ShapeDtypeStruct((3,) + p.shape[1:], p.dtype) for p in parts]
        self.aliases = {}
        self.n_sems = 3 * len(parts)

    def _copies(self, ins, outs, send, recv):
        x, y, c = _mesh_pos()
        return [_remote(ins[t].at[2 * chip[0] + chip[1]], outs[t].at[k], send.at[3 * t + k], recv.at[3 * t + k], (*chip, c))
                for t in range(len(ins)) for k, chip in enumerate(_other_chips(x, y))]

    def start(self, ins, outs, send, recv):
        for cp in self._copies(ins, outs, send, recv):
            cp.start()

    def finish(self, ins, outs, send, recv):
        for cp in self._copies(ins, outs, send, recv):
            cp.wait()


class _PairShare:
    def __init__(self, fulls):
        self.ins = list(fulls)
        self.out_shapes = _like(fulls)
        self.aliases = {t: t for t in range(len(fulls))}
        self.n_sems = len(fulls)

    def _copies(self, ins, outs, send, recv):
        x, y, c = _mesh_pos()
        return [_remote(ins[t].at[c], outs[t].at[c], send.at[t], recv.at[t], (x, y, 1 - c)) for t in range(len(ins))]

    def start(self, ins, outs, send, recv):
        for cp in self._copies(ins, outs, send, recv):
            cp.start()

    def finish(self, ins, outs, send, recv):
        for cp in self._copies(ins, outs, send, recv):
            cp.wait()


class _ShareAll:
    def __init__(self, arrays):
        self.ins = list(arrays)
        self.out_shapes = [jax.ShapeDtypeStruct((N_DEV,) + a.shape, a.dtype) for a in arrays]
        self.aliases = {}
        self.n_sems = (N_DEV - 1) * len(arrays)

    def _peers(self):
        x, y, c = _mesh_pos()
        flips = [((r >> 2) & 1, (r >> 1) & 1, r & 1) for r in range(1, N_DEV)]
        return (x, y, c), [(x ^ fx, y ^ fy, c ^ fc) for fx, fy, fc in flips]

    def _sends(self, ins, outs, send, recv):
        (x, y, c), peers = self._peers()
        mine = 4 * x + 2 * y + c
        return [_remote(ins[a], outs[a].at[mine], send.at[7 * a + r], recv.at[7 * a + r], peer)
                for a in range(len(ins)) for r, peer in enumerate(peers)]

    def start(self, ins, outs, send, recv):
        for cp in self._sends(ins, outs, send, recv):
            cp.start()

    def finish(self, ins, outs, send, recv):
        (x, y, c), peers = self._peers()
        for a in range(len(ins)):
            for r, (px, py, pc) in enumerate(peers):
                blk = outs[a].at[4 * px + 2 * py + pc]
                _remote(blk, blk, send.at[7 * a + r], recv.at[7 * a + r], (x, y, c)).wait_recv()
        for cp in self._sends(ins, outs, send, recv):
            cp.wait_send()


def _gather_beside(bufs, name, collective_id):
    n = len(bufs)
    refs = [jax.new_ref(b, memory_space=pltpu.MemorySpace.HBM) for b in bufs]
    gather = _Gather(bufs)

    @pl.kernel(mesh=plsc.ScalarSubcoreMesh(axis_name="sequencer", num_cores=1), name=name,
               scratch_types=(pltpu.SemaphoreType.DMA((6 * n,)), pltpu.SemaphoreType.DMA((6 * n,))),
               compiler_params=pltpu.CompilerParams(collective_id=collective_id))
    def launch(send, recv):
        x, y, c = _mesh_pos()
        barrier = pltpu.get_barrier_semaphore()
        peers = [(*chip, c) for chip in _other_chips(x, y)] + [(x, y, 1 - c)]
        for peer in peers:
            pl.semaphore_signal(barrier, inc=1, device_id=peer, device_id_type=MESH)
        pl.semaphore_wait(barrier, len(peers))
        gather.start(refs, refs, send, recv)
        gather.near_end(refs, refs, send, recv)
        gather.finish(refs, refs, send, recv)

    launch()
    return [r[...] for r in refs]


def _chip_swap_beside(parts, name, collective_id):
    src = jax.new_ref(parts, memory_space=pltpu.MemorySpace.HBM)
    dst = jax.empty_ref(jax.ShapeDtypeStruct((N_CHIPS - 1,) + parts.shape[1:], parts.dtype),
                        memory_space=pltpu.MemorySpace.HBM)
    swap = _ChipSwap([parts])

    @pl.kernel(mesh=plsc.ScalarSubcoreMesh(axis_name="sequencer", num_cores=1), name=name,
               scratch_types=(pltpu.SemaphoreType.DMA((N_CHIPS - 1,)), pltpu.SemaphoreType.DMA((N_CHIPS - 1,))),
               compiler_params=pltpu.CompilerParams(collective_id=collective_id))
    def launch(send, recv):
        x, y, c = _mesh_pos()
        barrier = pltpu.get_barrier_semaphore()
        peers = [(*chip, c) for chip in _other_chips(x, y)]
        for peer in peers:
            pl.semaphore_signal(barrier, inc=1, device_id=peer, device_id_type=MESH)
        pl.semaphore_wait(barrier, len(peers))
        swap.start([src], [dst], send, recv)
        swap.finish([src], [dst], send, recv)

    launch()
    return dst[...]


def _pallas(body, operands, *, name, grid, in_specs, out_specs, out_shape, scratch_shapes=(), vmem_mib=32, riders=(),
            prefetch=None):
    in_specs, out_specs, out_shape, scratch_shapes = list(in_specs), list(out_specs), list(out_shape), list(scratch_shapes)
    if not riders and prefetch is None:
        outs = pl.pallas_call(body, name=name, grid=grid, in_specs=in_specs, out_specs=out_specs, out_shape=out_shape,
                              scratch_shapes=scratch_shapes, compiler_params=_params(vmem_mib, len(grid)))(*operands)
        return list(outs), []
    n_in, n_out, n_scr = len(in_specs), len(out_specs), len(scratch_shapes)
    r_in = [len(r.ins) for r in riders]
    r_out = [len(r.out_shapes) for r in riders]
    steps = 1
    for g in grid:
        steps *= g

    n_pre = 0 if prefetch is None else 1

    def wrapped(*refs):
        refs = list(refs)
        pre, refs = refs[:n_pre], refs[n_pre:]
        ins, refs = refs[:n_in], refs[n_in:]
        rins = []
        for k in r_in:
            rins.append(refs[:k])
            refs = refs[k:]
        outs, refs = refs[:n_out], refs[n_out:]
        routs = []
        for k in r_out:
            routs.append(refs[:k])
            refs = refs[k:]
        scr, sems = refs[:n_scr], refs[n_scr:]
        step = 0
        for ax, g in enumerate(grid):
            step = step * g + pl.program_id(ax)

        def each(what):
            for j, r in enumerate(riders):
                if hasattr(r, what):
                    getattr(r, what)(rins[j], routs[j], sems[2 * j], sems[2 * j + 1])

        if grid:
            pl.when(step == 0)(lambda: each("start"))
        else:
            each("start")
        body(*pre, *ins, *outs, *scr)
        if grid:
            @pl.when(step == steps - 1)
            def _():
                each("near_end")
                each("finish")
        else:
            each("near_end")
            each("finish")

    aliases, off_in, off_out = {}, n_pre + n_in, n_out
    for r, ki, ko in zip(riders, r_in, r_out):
        for i, o in r.aliases.items():
            aliases[off_in + i] = off_out + o
        off_in, off_out = off_in + ki, off_out + ko
    sems = []
    for r in riders:
        sems += [pltpu.SemaphoreType.DMA((r.n_sems,)), pltpu.SemaphoreType.DMA((r.n_sems,))]
    layout = dict(grid=grid, in_specs=in_specs + [ANY] * sum(r_in), out_specs=out_specs + [ANY] * sum(r_out),
                  scratch_shapes=scratch_shapes + sems)
    if prefetch is not None:
        layout = dict(grid_spec=pltpu.PrefetchScalarGridSpec(num_scalar_prefetch=1, **layout))
    res = pl.pallas_call(
        wrapped, name=name, **layout,
        out_shape=out_shape + [s for r in riders for s in r.out_shapes], input_output_aliases=aliases,
        compiler_params=pltpu.CompilerParams(dimension_semantics=("arbitrary",) * len(grid),
                                             vmem_limit_bytes=vmem_mib * MIB, has_side_effects=True),
    )(*([] if prefetch is None else [prefetch]), *operands, *[a for r in riders for a in r.ins])
    res = list(res)
    outs, res = res[:n_out], res[n_out:]
    routs = []
    for k in r_out:
        routs.append(res[:k])
        res = res[k:]
    return outs, routs


def _exchange(riders, name):
    return _pallas(lambda: None, [], name=name, grid=(), in_specs=[], out_specs=[], out_shape=[], riders=riders)[1]


def _in_hbm(a):
    return pltpu.with_memory_space_constraint(a, pltpu.HBM)


def _place_shard(w, layer, dtype, name):
    _, rows, cols = w.shape
    half = rows // 2
    br = _block_rows(half)
    nb = half // br
    mine = 2 * lax.axis_index("x") + lax.axis_index("y")

    def body(q_ref, w_ref, o_ref):
        o_ref[...] = w_ref[...].astype(dtype)

    return pl.pallas_call(
        body, name=name,
        grid_spec=pltpu.PrefetchScalarGridSpec(
            num_scalar_prefetch=1, grid=(2, nb),
            in_specs=[pl.BlockSpec((None, br, cols), lambda h, i, q: (layer, h * nb + i, 0))],
            out_specs=pl.BlockSpec((None, None, br, cols), lambda h, i, q: (q[0], h, i, 0))),
        out_shape=pltpu.HBM((N_CHIPS, 2, half, cols), dtype),
        compiler_params=_params(16, 2),
    )(jnp.reshape(mine, (1,)).astype(jnp.int32), _in_hbm(w))


def _plain_copy(a, name):
    def body(a_ref, o_ref):
        o_ref[...] = a_ref[...]

    vmem = pl.BlockSpec(memory_space=pltpu.VMEM)
    return pl.pallas_call(body, name=name, in_specs=[vmem], out_specs=vmem,
                          out_shape=jax.ShapeDtypeStruct(a.shape, a.dtype))(a)


def _add_pair(g, recv, name):
    _, _, r, cdim = g.shape
    br = _block_rows(r, 256)
    c = lax.axis_index("c")

    def body(c_ref, g_ref, r_ref, o_ref):
        o_ref[...] = (g_ref[...] + r_ref[...]).astype(BF16)

    return pl.pallas_call(
        body, name=name,
        grid_spec=pltpu.PrefetchScalarGridSpec(
            num_scalar_prefetch=1, grid=(N_CHIPS, r // br),
            in_specs=[pl.BlockSpec((None, None, br, cdim), lambda q, i, c_ref: (q, c_ref[0], i, 0)),
                      pl.BlockSpec((None, br, cdim), lambda q, i, c_ref: (q, i, 0))],
            out_specs=pl.BlockSpec((None, br, cdim), lambda q, i, c_ref: (q, i, 0))),
        out_shape=pltpu.HBM((N_CHIPS, r, cdim), BF16),
        compiler_params=_params(16, 2),
    )(jnp.reshape(c, (1,)).astype(jnp.int32), _in_hbm(g), _in_hbm(recv))


def _add_chips(own, recv, name):
    _, r, cdim = own.shape
    br = _block_rows(r, 256)
    x, y, c = _mesh_pos()

    def body(pos_ref, own_ref, r_ref, o_ref):
        acc = own_ref[...].astype(F32)
        for k in range(3):
            acc = acc + r_ref[k].astype(F32)
        o_ref[...] = acc

    return pl.pallas_call(
        body, name=name,
        grid_spec=pltpu.PrefetchScalarGridSpec(
            num_scalar_prefetch=1, grid=(r // br,),
            in_specs=[pl.BlockSpec((None, br, cdim), lambda i, pos: (pos[0], i, 0)),
                      pl.BlockSpec((3, br, cdim), lambda i, pos: (0, i, 0))],
            out_specs=pl.BlockSpec((None, br, cdim), lambda i, pos: (pos[1], i, 0))),
        out_shape=pltpu.HBM((2, r, cdim), F32),
        compiler_params=_params(16, 1),
    )(jnp.stack([2 * x + y, c]).astype(jnp.int32), _in_hbm(own), recv)


def _adam_math(w, m, v, g):
    c1 = 1.0 / (1.0 - ADAM_B1 ** ADAM_STEP)
    c2 = 1.0 / (1.0 - ADAM_B2 ** ADAM_STEP)
    m_new = ADAM_B1 * m + (1.0 - ADAM_B1) * g
    v_new = ADAM_B2 * v + (1.0 - ADAM_B2) * (g * g)
    return -ADAM_LR * ((m_new * c1) / (jnp.sqrt(v_new * c2) + ADAM_EPS) + ADAM_WD * w), m_new, v_new


SMALL_WEIGHTS = [
    ("ev_norm_g", (1, D_MODEL), ["ev_norm_g"], None), ("ev_conv_a_b", (1, A_DIM), ["ev_conv_a_b"], None),
    ("ev_ln_a_g", (1, A_DIM), ["ev_ln_a_g"], None), ("ev_ln_a_b", (1, A_DIM), ["ev_ln_a_b"], None),
    ("od_w_s", (C_GROUPS, CHUNK, CHUNK), ["od_w_s_lo", "od_w_s_hi"], None), ("od_b_s", (C_GROUPS, CHUNK), ["od_b_s"], None),
    ("mlp_norm_g", (2, D_MODEL), ["mlp_norm_g0", "mlp_norm_g1"], None), ("final_norm_g", (1, D_MODEL), ["final_norm_g"], None),
    ("ev_conv_a_w", (A_CONV_WIDTH, A_DIM // N_CHIPS), ["ev_conv_a_w"], A_DIM // N_CHIPS),
    ("ev_conv_b_w", (B_CONV_WIDTH, B_DIM // N_CHIPS), ["ev_conv_b_w"], B_DIM // N_CHIPS),
    ("od_norm_g", (1, D_MODEL // N_CHIPS), ["od_norm_g"], D_MODEL // N_CHIPS),
    ("od_b_in", (1, 2 * C_DIM // N_CHIPS), ["od_b_in"], 2 * C_DIM // N_CHIPS),
    ("od_ln_v_g", (1, C_DIM // N_CHIPS), ["od_ln_v_g"], C_DIM // N_CHIPS),
    ("od_ln_v_b", (1, C_DIM // N_CHIPS), ["od_ln_v_b"], C_DIM // N_CHIPS),
]


def _small_update(own, landed, weights):
    names = list(own.keys())
    n_g, n_w = len(names), len(SMALL_WEIGHTS)

    def body(*refs):
        refs = list(refs)
        own_refs = dict(zip(names, refs[:n_g]))
        land_refs = dict(zip(names, refs[n_g:2 * n_g]))
        wmv = [refs[2 * n_g + 3 * i:2 * n_g + 3 * i + 3] for i in range(n_w)]
        o0 = 2 * n_g + 3 * n_w
        loss_ref = refs[o0]
        outs = [refs[o0 + 1 + 4 * i:o0 + 5 + 4 * i] for i in range(n_w)]
        acc = dict(zip(names, refs[o0 + 1 + 4 * n_w:]))
        x, y, c = _mesh_pos()
        mine, chip = 4 * x + 2 * y + c, 2 * x + y

        for nm in names:
            for d in range(N_DEV):
                def add(term, nm=nm, d=d):
                    acc[nm][...] = term if d == 0 else acc[nm][...] + term
                pl.when(mine == d)(lambda nm=nm, add=add: add(own_refs[nm][...]))
                pl.when(mine != d)(lambda nm=nm, d=d, add=add: add(land_refs[nm][d]))
        loss_ref[...] = acc["loss"][...]

        def update(i, rows, g):
            w_ref, m_ref, v_ref = wmv[i]
            delta, m_new, v_new = _adam_math(w_ref[rows], m_ref[rows], v_ref[rows], g)
            for ref, val in zip(outs[i], (g, delta, m_new, v_new)):
                ref[rows] = val

        for i, (_, shape, grads, per_chip) in enumerate(SMALL_WEIGHTS):
            for row, gname in enumerate(grads):
                per_grad = shape[0] // len(grads)
                rows = slice(row * per_grad, (row + 1) * per_grad)
                if per_chip is None:
                    update(i, rows, acc[gname][...])
                else:
                    for q in range(N_CHIPS):
                        pl.when(chip == q)(lambda i=i, rows=rows, gname=gname, q=q, per_chip=per_chip:
                                           update(i, rows, acc[gname][:, q * per_chip:(q + 1) * per_chip]))

    operands = [own[nm] for nm in names] + [landed[nm] for nm in names]
    for nm, _, _, _ in SMALL_WEIGHTS:
        operands += list(weights[nm])
    out_shape = [jax.ShapeDtypeStruct((1, 1), F32)]
    for _, shape, _, _ in SMALL_WEIGHTS:
        out_shape += [jax.ShapeDtypeStruct(shape, F32)] * 4
    res = pl.pallas_call(
        body, name="small_update", grid=(1,),
        in_specs=[_full_spec(a.shape) for a in operands], out_specs=[_full_spec(s.shape) for s in out_shape],
        out_shape=out_shape, scratch_shapes=[pltpu.VMEM(own[nm].shape, F32) for nm in names],
        compiler_params=_params(32, 1),
    )(*[_in_hbm(a) for a in operands])
    return res[0], {nm: res[1 + 4 * i:5 + 4 * i] for i, (nm, _, _, _) in enumerate(SMALL_WEIGHTS)}


def _adamw(w, m, v, grads, name, riders=()):
    layers, r, cdim = w.shape
    br = _block_rows(r, 256 if cdim > LANES else 1024)

    def body(*refs):
        w_ref, m_ref, v_ref = refs[:3]
        g_refs = refs[3:3 + layers]
        go_ref, d_ref, mo_ref, vo_ref = refs[3 + layers:]
        layer = pl.program_id(0)
        for l in range(layers):
            @pl.when(layer == l)
            def _(l=l):
                g = g_refs[l][...]
                go_ref[...] = g
                d_ref[...], mo_ref[...], vo_ref[...] = _adam_math(w_ref[...], m_ref[...], v_ref[...], g)

    spec3 = pl.BlockSpec((None, br, cdim), lambda l, i: (l, i, 0))
    spec2 = pl.BlockSpec((br, cdim), lambda l, i: (i, 0))
    out = jax.ShapeDtypeStruct((layers, r, cdim), F32)
    return _pallas(body, [_in_hbm(a) for a in (w, m, v, *grads)], name=name, grid=(layers, r // br),
                   in_specs=[spec3, spec3, spec3] + [spec2] * layers, out_specs=[spec3] * 4, out_shape=[out] * 4,
                   vmem_mib=32, riders=riders)


def _fill_shifted(buf, rows):
    for b in range(1, SUBLANES):
        buf[b, 0:rows - SUBLANES, :] = buf[0, b:b + rows - SUBLANES, :]


def _window(buf, start, size):
    return buf[start % SUBLANES, start - start % SUBLANES:start - start % SUBLANES + size, :]


def _conv31(src, w_ref, r0, base, init):
    acc = init
    for k in range(A_CONV_WIDTH):
        acc = acc + w_ref[k:k + 1, :] * _window(src, base + k + r0, CONV_ROWS)
    return acc


def _fwd_even(x, norm_g, w_in, conv_a_w, conv_a_b, ln_g, ln_b, conv_b_w, w_out, *, tm, seq, riders=()):
    tokens = x.shape[0]
    nt, tps = tokens // tm, seq // tm

    def body(x_ref, g_ref, win_hbm, caw_ref, cab_ref, lng_ref, lnb_ref, cbw_ref, wout_hbm,
             h_ref, n_ref, z_ref, a2_ref, cv_ref, mix_ref, win_v, wout_v, pa, pb, sem):
        i = pl.program_id(0)

        _load_weights([(win_hbm, win_v, False), (wout_hbm, wout_v, True)], sem)

        xv = x_ref[...]
        nf, _ = _rms_fwd(xv, g_ref[...])
        n = nf.astype(BF16)
        n_ref[...] = n
        z = jnp.concatenate([_dot(n, win_v[j]) for j in range(N_CHIPS)], axis=1)
        z_ref[...] = z.astype(BF16)
        a_val, a_gate = z[:, 0:A_DIM], z[:, A_DIM:2 * A_DIM]
        b_gate, c_gate, b_val = z[:, 1024:1536], z[:, 1536:2048], z[:, 2048:2560]

        first = (i % tps) == 0

        @pl.when(first)
        def _():
            pa[0, 0:A_HALO, :] = jnp.zeros((A_HALO, A_DIM), F32)
            pb[0:B_HALO, :] = jnp.zeros((B_HALO, B_DIM), F32)

        @pl.when(jnp.logical_not(first))
        def _():
            pa[0, 0:A_HALO, :] = pa[0, tm:tm + A_HALO, :]
            pb[0:B_HALO, :] = pb[tm:tm + B_HALO, :]

        pa[0, A_HALO:A_HALO + tm, :] = a_val * jax.nn.sigmoid(a_gate)
        pb[B_HALO:B_HALO + tm, :] = c_gate * b_val
        _fill_shifted(pa, A_HALO + tm)
        bias = jnp.broadcast_to(cab_ref[...], (CONV_ROWS, A_DIM))
        for r0 in range(0, tm, CONV_ROWS):
            a2_ref[r0:r0 + CONV_ROWS, :] = _conv31(pa, caw_ref, r0, A_HALO - (A_CONV_WIDTH - 1), bias)
        xhat, _ = _ln_stats(a2_ref[...])
        a3 = xhat * lng_ref[...] + lnb_ref[...]
        a4 = a3 * jax.nn.sigmoid(a3)
        cv = cbw_ref[0:1, :] * pb[B_HALO - 2:B_HALO - 2 + tm, :]
        cv = cv + cbw_ref[1:2, :] * pb[B_HALO - 1:B_HALO - 1 + tm, :]
        cv = cv + cbw_ref[2:3, :] * pb[B_HALO:B_HALO + tm, :]
        cv_ref[...] = cv.astype(BF16)
        mix = jnp.concatenate([a4, b_gate * cv], axis=1).astype(BF16)
        mix_ref[...] = mix
        h_ref[...] = xv + _dot(mix, wout_v[...])

    shp = lambda cols, dt: jax.ShapeDtypeStruct((tokens, cols), dt)
    return _pallas(
        body, [x, norm_g, w_in, conv_a_w, conv_a_b, ln_g, ln_b, conv_b_w, w_out], name="fwd_even", grid=(nt,),
        in_specs=[_row_spec(tm, D_MODEL), _full_spec((1, D_MODEL)), ANY, _full_spec((A_CONV_WIDTH, A_DIM)),
                  _full_spec((1, A_DIM)), _full_spec((1, A_DIM)), _full_spec((1, A_DIM)),
                  _full_spec((B_CONV_WIDTH, B_DIM)), ANY],
        out_specs=[_row_spec(tm, D_MODEL), _row_spec(tm, D_MODEL), _row_spec(tm, IN_EVEN), _row_spec(tm, A_DIM),
                   _row_spec(tm, B_DIM), _row_spec(tm, D_MODEL)],
        out_shape=[shp(D_MODEL, F32), shp(D_MODEL, BF16), shp(IN_EVEN, BF16), shp(A_DIM, F32), shp(B_DIM, BF16),
                   shp(D_MODEL, BF16)],
        scratch_shapes=[pltpu.VMEM((N_CHIPS, D_MODEL, IN_EVEN // N_CHIPS), BF16), pltpu.VMEM((D_MODEL, D_MODEL), BF16),
                        pltpu.VMEM((SUBLANES, A_HALO + tm, A_DIM), F32), pltpu.VMEM((B_HALO + tm, B_DIM), F32),
                        pltpu.SemaphoreType.DMA((N_LOADS,))],
        vmem_mib=56, riders=riders)


def _loss_tail(xv, g, target, loss_ref, dh_ref, dhb_ref, dg_ref):
    @pl.when(pl.program_id(0) == 0)
    def _():
        loss_ref[...] = jnp.zeros((1, 1), F32)
        dg_ref[...] = jnp.zeros((1, D_MODEL), F32)

    out, rstd = _rms_fwd(xv, g)
    err = out - target
    per_token = jnp.sum(err * err, axis=1, keepdims=True) * (1.0 / D_MODEL)
    loss_ref[...] += 0.5 * jnp.sum(per_token, axis=0, keepdims=True)
    dx, dg = _rms_bwd(err * (1.0 / D_MODEL), xv, rstd, g)
    dh_ref[...] = dx
    dhb_ref[...] = dx.astype(BF16)
    dg_ref[...] += dg


def _fwd_mlp(h, norm_g, w1, w2, layer, *, tm, riders=(), head=None):
    tokens = h.shape[0]
    nt = tokens // tm
    fs = D_FF // N_CHIPS
    n_in = 4 if head is None else 6

    def body(*refs):
        h_ref, g_ref, w1_hbm, w2_hbm = refs[:4]
        w1_v, w2_v, sem = refs[-3:]
        outs = refs[n_in:-3]
        n_ref, p_ref, q_ref = outs[1:4] if head is None else outs[0:3]
        _load_weights([(w1_hbm, w1_v, False), (w2_hbm, w2_v, False)], sem)

        xv = h_ref[...]
        nf, _ = _rms_fwd(xv, g_ref[...])
        n = nf.astype(BF16)
        n_ref[...] = n
        acc = xv
        for j in range(N_CHIPS):
            p = _dot(n, w1_v[j])
            p_ref[:, j * fs:(j + 1) * fs] = p.astype(BF16)
            r = jnp.maximum(p, 0.0)
            q = (r * r).astype(BF16)
            q_ref[:, j * fs:(j + 1) * fs] = q
            acc = acc + _dot(q, w2_v[j])
        if head is None:
            outs[0][...] = acc
        else:
            _loss_tail(acc, refs[4][...], refs[5][...], *outs[3:7])

    shp = lambda cols, dt: jax.ShapeDtypeStruct((tokens, cols), dt)
    saved_specs = [_row_spec(tm, D_MODEL), _row_spec(tm, D_FF), _row_spec(tm, D_FF)]
    saved_shapes = [shp(D_MODEL, BF16), shp(D_FF, BF16), shp(D_FF, BF16)]
    if head is None:
        operands, in_specs = [h, norm_g, w1, w2], [_row_spec(tm, D_MODEL), _full_spec((1, D_MODEL)), ANY, ANY]
        out_specs, out_shape = [_row_spec(tm, D_MODEL)] + saved_specs, [shp(D_MODEL, F32)] + saved_shapes
    else:
        operands = [h, norm_g, w1, w2, *head]
        in_specs = [_row_spec(tm, D_MODEL), _full_spec((1, D_MODEL)), ANY, ANY, _full_spec((1, D_MODEL)), _row_spec(tm, D_MODEL)]
        out_specs = saved_specs + [_full_spec((1, 1)), _row_spec(tm, D_MODEL), _row_spec(tm, D_MODEL), _full_spec((1, D_MODEL))]
        out_shape = saved_shapes + [jax.ShapeDtypeStruct((1, 1), F32), shp(D_MODEL, F32), shp(D_MODEL, BF16),
                                    jax.ShapeDtypeStruct((1, D_MODEL), F32)]
    return _pallas(
        body, operands, name=f"fwd_mlp{layer}", grid=(nt,), in_specs=in_specs, out_specs=out_specs, out_shape=out_shape,
        scratch_shapes=[pltpu.VMEM((N_CHIPS, D_MODEL, fs), BF16), pltpu.VMEM((N_CHIPS, fs, D_MODEL), BF16),
                        pltpu.SemaphoreType.DMA((N_LOADS,))],
        vmem_mib=56, riders=riders)


def _tril_mask():
    row = lax.broadcasted_iota(jnp.int32, (CHUNK, CHUNK), 0)
    col = lax.broadcasted_iota(jnp.int32, (CHUNK, CHUNK), 1)
    return row >= col


def _triu_mask():
    row = lax.broadcasted_iota(jnp.int32, (CHUNK, CHUNK), 0)
    col = lax.broadcasted_iota(jnp.int32, (CHUNK, CHUNK), 1)
    return row <= col


def _fwd_odd(h, norm_g, w_in, b_in, ln_g, ln_b, w_s, b_s_rows, w_out, *, tm, riders=()):
    tokens = h.shape[0]
    nt = tokens // tm
    cs = 2 * C_DIM // N_CHIPS

    def body(h_ref, g_ref, win_hbm, bin_ref, lng_ref, lnb_ref, ws_ref, bs_ref, wout_hbm,
             ho_ref, n_ref, s_ref, cdf_ref, sv_ref, y_ref, win_v, wout_v, bd, sem):
        _load_weights([(win_hbm, win_v, False), (wout_hbm, wout_v, True)], sem)

        @pl.when(pl.program_id(0) == 0)
        def _():
            mask = _tril_mask()
            bd[...] = jnp.zeros(bd.shape, BF16)
            for g in range(C_GROUPS):
                w = jnp.where(mask, ws_ref[g], 0.0).astype(BF16)
                bd[g, 0:CHUNK, 0:CHUNK] = w
                bd[g, CHUNK:PAIR, CHUNK:PAIR] = w

        xv = h_ref[...]
        nf, _ = _rms_fwd(xv, g_ref[...])
        n = nf.astype(BF16)
        n_ref[...] = n
        s = jnp.concatenate([_dot(n, win_v[j]) for j in range(N_CHIPS)], axis=1) + bin_ref[...]
        s_ref[...] = s.astype(BF16)
        cdf = _gelu_cdf(s)
        cdf_ref[...] = cdf.astype(BF16)
        zz = s * cdf
        u, v = zz[:, 0:C_DIM], zz[:, C_DIM:2 * C_DIM]
        xhat, _ = _ln_stats(v)
        vn = (xhat * lng_ref[...] + lnb_ref[...]).astype(BF16)
        for g in range(C_GROUPS):
            cols = slice(g * CHUNK, (g + 1) * CHUNK)
            bias = jnp.concatenate([bs_ref[g], bs_ref[g]], axis=0)
            for r0 in range(0, tm, PAIR):
                sv = _dot(bd[g], vn[r0:r0 + PAIR, cols]) + bias
                sv_ref[r0:r0 + PAIR, cols] = sv.astype(BF16)
                y_ref[r0:r0 + PAIR, cols] = (u[r0:r0 + PAIR, cols] * sv).astype(BF16)
        ho_ref[...] = xv + _dot(y_ref[...], wout_v[...])

    shp = lambda cols, dt: jax.ShapeDtypeStruct((tokens, cols), dt)
    return _pallas(
        body, [h, norm_g, w_in, b_in, ln_g, ln_b, w_s, b_s_rows, w_out], name="fwd_odd", grid=(nt,),
        in_specs=[_row_spec(tm, D_MODEL), _full_spec((1, D_MODEL)), ANY, _full_spec((1, 2 * C_DIM)),
                  _full_spec((1, C_DIM)), _full_spec((1, C_DIM)), _full_spec((C_GROUPS, CHUNK, CHUNK)),
                  _full_spec((C_GROUPS, CHUNK, CHUNK)), ANY],
        out_specs=[_row_spec(tm, D_MODEL), _row_spec(tm, D_MODEL), _row_spec(tm, 2 * C_DIM), _row_spec(tm, 2 * C_DIM),
                   _row_spec(tm, C_DIM), _row_spec(tm, C_DIM)],
        out_shape=[shp(D_MODEL, F32), shp(D_MODEL, BF16), shp(2 * C_DIM, BF16), shp(2 * C_DIM, BF16), shp(C_DIM, BF16),
                   shp(C_DIM, BF16)],
        scratch_shapes=[pltpu.VMEM((N_CHIPS, D_MODEL, cs), BF16), pltpu.VMEM((C_DIM, D_MODEL), BF16),
                        pltpu.VMEM((C_GROUPS, PAIR, PAIR), BF16), pltpu.SemaphoreType.DMA((N_LOADS,))],
        vmem_mib=56, riders=riders)


def _bwd_mlp(dh, h, norm_g, p, w1, w2, layer, *, tm, riders=()):
    tokens = h.shape[0]
    nt = tokens // tm
    fs = D_FF // N_CHIPS

    def body(dh_ref, h_ref, g_ref, p_ref, w1_hbm, w2_hbm, dx_ref, dxb_ref, dp_ref, dg_ref, w1_v, w2_v, sem):
        @pl.when(pl.program_id(0) == 0)
        def _():
            dg_ref[...] = jnp.zeros((1, D_MODEL), F32)

        _load_weights([(w1_hbm, w1_v, False), (w2_hbm, w2_v, False)], sem)

        dhv = dh_ref[...]
        dhb = dhv.astype(BF16)
        dn = jnp.zeros((tm, D_MODEL), F32)
        for j in range(N_CHIPS):
            dq = _dot_nt(dhb, w2_v[j])
            r = jnp.maximum(p_ref[:, j * fs:(j + 1) * fs].astype(F32), 0.0)
            dp = ((2.0 * r) * dq).astype(BF16)
            dp_ref[:, j * fs:(j + 1) * fs] = dp
            dn = dn + _dot_nt(dp, w1_v[j])
        xv = h_ref[...]
        g = g_ref[...]
        _, rstd = _rms_fwd(xv, g)
        dx, dg = _rms_bwd(dn, xv, rstd, g)
        dx_ref[...] = dhv + dx
        dxb_ref[...] = (dhv + dx).astype(BF16)
        dg_ref[...] += dg

    return _pallas(
        body, [dh, h, norm_g, p, w1, w2], name=f"bwd_mlp{layer}", grid=(nt,),
        in_specs=[_row_spec(tm, D_MODEL), _row_spec(tm, D_MODEL), _full_spec((1, D_MODEL)), _row_spec(tm, D_FF), ANY, ANY],
        out_specs=[_row_spec(tm, D_MODEL), _row_spec(tm, D_MODEL), _row_spec(tm, D_FF), _full_spec((1, D_MODEL))],
        out_shape=[jax.ShapeDtypeStruct((tokens, D_MODEL), F32), jax.ShapeDtypeStruct((tokens, D_MODEL), BF16),
                   jax.ShapeDtypeStruct((tokens, D_FF), BF16), jax.ShapeDtypeStruct((1, D_MODEL), F32)],
        scratch_shapes=[pltpu.VMEM((N_CHIPS, D_MODEL, fs), BF16), pltpu.VMEM((N_CHIPS, fs, D_MODEL), BF16),
                        pltpu.SemaphoreType.DMA((N_LOADS,))],
        vmem_mib=56, riders=riders)


def _bwd_odd(dh, h, norm_g, s, cdf, sv, w_in, ln_g, ln_b, w_s, w_out, *, tm, riders=()):
    tokens = h.shape[0]
    nt = tokens // tm
    cs = 2 * C_DIM // N_CHIPS

    def body(dh_ref, h_ref, g_ref, s_ref, cdf_ref, sv_ref, win_hbm, lng_ref, lnb_ref, ws_ref, wout_hbm,
             dx_ref, dxb_ref, ds_ref, dg_ref, dbin_ref, dlng_ref, dlnb_ref, dws_ref, dbs_ref,
             win_v, wout_v, bdt, dws_acc, dbs_acc, dvn, sem):
        i = pl.program_id(0)

        _load_weights([(win_hbm, win_v, False), (wout_hbm, wout_v, True)], sem)

        @pl.when(i == 0)
        def _():
            mask_t = _triu_mask()
            bdt[...] = jnp.zeros(bdt.shape, BF16)
            for g in range(C_GROUPS):
                wt = jnp.where(mask_t, ws_ref[g].T, 0.0).astype(BF16)
                bdt[g, 0:CHUNK, 0:CHUNK] = wt
                bdt[g, CHUNK:PAIR, CHUNK:PAIR] = wt
            dws_acc[...] = jnp.zeros(dws_acc.shape, F32)
            dbs_acc[...] = jnp.zeros(dbs_acc.shape, F32)
            dg_ref[...] = jnp.zeros(dg_ref.shape, F32)
            dbin_ref[...] = jnp.zeros(dbin_ref.shape, F32)
            dlng_ref[...] = jnp.zeros(dlng_ref.shape, F32)
            dlnb_ref[...] = jnp.zeros(dlnb_ref.shape, F32)

        dhv = dh_ref[...]
        dy = _dot_nt(dhv.astype(BF16), wout_v[...])
        sf = s_ref[...].astype(F32)
        cdf = cdf_ref[...].astype(F32)
        pdf = jnp.exp(-0.5 * sf * sf) * 0.3989422804014327
        zz = sf * cdf
        dgelu = cdf + sf * pdf
        u, v = zz[:, 0:C_DIM], zz[:, C_DIM:2 * C_DIM]
        xhat, rs = _ln_stats(v)
        lng = lng_ref[...]
        vn = (xhat * lng + lnb_ref[...]).astype(BF16)
        du = dy * sv_ref[...].astype(F32)
        dsv = dy * u
        dsvb = dsv.astype(BF16)
        for g in range(C_GROUPS):
            cols = slice(g * CHUNK, (g + 1) * CHUNK)
            for r0 in range(0, tm, PAIR):
                blk = dsvb[r0:r0 + PAIR, cols]
                dvn[r0:r0 + PAIR, cols] = _dot(bdt[g], blk)
                dws_acc[g] += _dot_nt(blk, vn[r0:r0 + PAIR, cols])
                dbs_acc[g] += dsv[r0:r0 + CHUNK, cols] + dsv[r0 + CHUNK:r0 + PAIR, cols]
        dv, dlng, dlnb = _ln_bwd(dvn[...], xhat, rs, lng)
        dlng_ref[...] += dlng
        dlnb_ref[...] += dlnb
        ds = jnp.concatenate([du, dv], axis=1) * dgelu
        dbin_ref[...] += jnp.sum(ds, axis=0, keepdims=True)
        dsb = ds.astype(BF16)
        ds_ref[...] = dsb
        dn = jnp.zeros((tm, D_MODEL), F32)
        for j in range(N_CHIPS):
            dn = dn + _dot_nt(dsb[:, j * cs:(j + 1) * cs], win_v[j])
        xv = h_ref[...]
        g = g_ref[...]
        _, rstd = _rms_fwd(xv, g)
        dx, dg = _rms_bwd(dn, xv, rstd, g)
        dx_ref[...] = dhv + dx
        dxb_ref[...] = (dhv + dx).astype(BF16)
        dg_ref[...] += dg

        @pl.when(i == nt - 1)
        def _():
            mask = _tril_mask()
            for g in range(C_GROUPS):
                full = dws_acc[g]
                dws_ref[g] = jnp.where(mask, full[0:CHUNK, 0:CHUNK] + full[CHUNK:PAIR, CHUNK:PAIR], 0.0)
                dbs_ref[g:g + 1, :] = jnp.sum(dbs_acc[g].T, axis=0, keepdims=True)

    row = lambda cols: jax.ShapeDtypeStruct((1, cols), F32)
    return _pallas(
        body, [dh, h, norm_g, s, cdf, sv, w_in, ln_g, ln_b, w_s, w_out], name="bwd_odd", grid=(nt,),
        in_specs=[_row_spec(tm, D_MODEL), _row_spec(tm, D_MODEL), _full_spec((1, D_MODEL)), _row_spec(tm, 2 * C_DIM),
                  _row_spec(tm, 2 * C_DIM), _row_spec(tm, C_DIM), ANY, _full_spec((1, C_DIM)), _full_spec((1, C_DIM)),
                  _full_spec((C_GROUPS, CHUNK, CHUNK)), ANY],
        out_specs=[_row_spec(tm, D_MODEL), _row_spec(tm, D_MODEL), _row_spec(tm, 2 * C_DIM), _full_spec((1, D_MODEL)),
                   _full_spec((1, 2 * C_DIM)),
                   _full_spec((1, C_DIM)), _full_spec((1, C_DIM)), _full_spec((C_GROUPS, CHUNK, CHUNK)),
                   _full_spec((C_GROUPS, CHUNK))],
        out_shape=[jax.ShapeDtypeStruct((tokens, D_MODEL), F32), jax.ShapeDtypeStruct((tokens, D_MODEL), BF16),
                   jax.ShapeDtypeStruct((tokens, 2 * C_DIM), BF16),
                   row(D_MODEL), row(2 * C_DIM), row(C_DIM), row(C_DIM),
                   jax.ShapeDtypeStruct((C_GROUPS, CHUNK, CHUNK), F32), jax.ShapeDtypeStruct((C_GROUPS, CHUNK), F32)],
        scratch_shapes=[pltpu.VMEM((N_CHIPS, D_MODEL, cs), BF16), pltpu.VMEM((C_DIM, D_MODEL), BF16),
                        pltpu.VMEM((C_GROUPS, PAIR, PAIR), BF16), pltpu.VMEM((C_GROUPS, PAIR, PAIR), F32),
                        pltpu.VMEM((C_GROUPS, CHUNK, CHUNK), F32), pltpu.VMEM((tm, C_DIM), F32),
                        pltpu.SemaphoreType.DMA((N_LOADS,))],
        vmem_mib=56, riders=riders)


def _bwd_even(dh, x, norm_g, z, a2, cv, w_in, conv_a_w, ln_g, ln_b, conv_b_w, w_out, *, tm, seq, riders=()):
    tokens = x.shape[0]
    nt, tps = tokens // tm, seq // tm
    ws = IN_EVEN // N_CHIPS

    def body(dh_ref, x_ref, g_ref, z_ref, a2_ref, cv_ref, win_hbm, caw_ref, lng_ref, lnb_ref, cbw_ref, wout_hbm,
             dx_ref, dz_ref, dg_ref, dcaw_ref, dcab_ref, dlng_ref, dlnb_ref, dcbw_ref,
             win_v, wout_v, ea, eb, a1s, da1s, sigs, wide, dw_acc, sem):
        i = pl.program_id(0)

        _load_weights([(win_hbm, win_v, False), (wout_hbm, wout_v, True)], sem)

        @pl.when(i == 0)
        def _():
            dw_acc[...] = jnp.zeros(dw_acc.shape, F32)
            for ref in (dg_ref, dcab_ref, dlng_ref, dlnb_ref, dcbw_ref):
                ref[...] = jnp.zeros(ref.shape, F32)

        last = ((nt - 1 - i) % tps) == tps - 1

        @pl.when(last)
        def _():
            ea[0, tm:tm + A_HALO, :] = jnp.zeros((A_HALO, A_DIM), F32)
            eb[tm:tm + B_HALO, :] = jnp.zeros((B_HALO, B_DIM), F32)

        @pl.when(jnp.logical_not(last))
        def _():
            ea[0, tm:tm + A_HALO, :] = ea[0, 0:A_HALO, :]
            eb[tm:tm + B_HALO, :] = eb[0:B_HALO, :]

        wide[...] = _dot_nt(dh_ref[...].astype(BF16), wout_v[...])
        lng, lnb = lng_ref[...], lnb_ref[...]
        zero_row = jnp.zeros((1, A_DIM), F32)
        dlng, dlnb, dcab = zero_row, zero_row, zero_row
        for r0 in range(0, tm, ELEM_ROWS):
            rows = slice(r0, r0 + ELEM_ROWS)
            a_val, a_gate = z_ref[rows, 0:A_DIM].astype(F32), z_ref[rows, A_DIM:2 * A_DIM].astype(F32)
            xhat, rs = _ln_stats(a2_ref[rows, :])
            a3 = xhat * lng + lnb
            sg = jax.nn.sigmoid(a3)
            da3 = wide[rows, 0:A_DIM] * (sg * (1.0 + a3 * (1.0 - sg)))
            da2, g_part, b_part = _ln_bwd(da3, xhat, rs, lng)
            dlng, dlnb, dcab = dlng + g_part, dlnb + b_part, dcab + jnp.sum(da2, axis=0, keepdims=True)
            ea[0, rows, :] = da2
            eb[rows, :] = wide[rows, A_DIM:A_DIM + B_DIM] * z_ref[rows, 1024:1536].astype(F32)
            sig = jax.nn.sigmoid(a_gate)
            sigs[rows, :] = sig
            a1s[rows, :] = a_val * sig
        dlng_ref[...] += dlng
        dlnb_ref[...] += dlnb
        dcab_ref[...] += dcab
        _fill_shifted(ea, tm + A_HALO)
        for r0 in range(0, tm, CONV_ROWS):
            acc = jnp.zeros((CONV_ROWS, A_DIM), F32)
            for j in range(A_CONV_WIDTH):
                acc = acc + caw_ref[A_CONV_WIDTH - 1 - j:A_CONV_WIDTH - j, :] * _window(ea, r0 + j, CONV_ROWS)
            da1s[r0:r0 + CONV_ROWS, :] = acc
        for j0 in range(0, A_CONV_WIDTH, DW_TAPS):
            taps = range(j0, min(j0 + DW_TAPS, A_CONV_WIDTH))
            part = [jnp.zeros((CONV_ROWS, A_DIM), F32) for _ in taps]
            for r0 in range(0, tm, CONV_ROWS):
                a1c = a1s[r0:r0 + CONV_ROWS, :]
                for u, j in enumerate(taps):
                    part[u] = part[u] + _window(ea, r0 + j, CONV_ROWS) * a1c
            for u, j in enumerate(taps):
                dw_acc[A_CONV_WIDTH - 1 - j] += part[u]
        dcbw = [jnp.zeros((1, B_DIM), F32) for _ in range(B_CONV_WIDTH)]
        for r0 in range(0, tm, ELEM_ROWS):
            rows = slice(r0, r0 + ELEM_ROWS)
            da1, sig = da1s[rows, :], sigs[rows, :]
            dz_ref[rows, 0:A_DIM] = (da1 * sig).astype(BF16)
            dz_ref[rows, A_DIM:2 * A_DIM] = (da1 * z_ref[rows, 0:A_DIM].astype(F32) * (sig * (1.0 - sig))).astype(BF16)
            c_gate, b_val = z_ref[rows, 1536:2048].astype(F32), z_ref[rows, 2048:2560].astype(F32)
            dz_ref[rows, 1024:1536] = (wide[rows, A_DIM:A_DIM + B_DIM] * cv_ref[rows, :].astype(F32)).astype(BF16)
            cb = c_gate * b_val
            dcb = jnp.zeros((ELEM_ROWS, B_DIM), F32)
            for j in range(B_CONV_WIDTH):
                k = B_CONV_WIDTH - 1 - j
                sl = eb[r0 + j:r0 + j + ELEM_ROWS, :]
                dcb = dcb + cbw_ref[k:k + 1, :] * sl
                dcbw[k] = dcbw[k] + jnp.sum(sl * cb, axis=0, keepdims=True)
            dz_ref[rows, 1536:2048] = (dcb * b_val).astype(BF16)
            dz_ref[rows, 2048:2560] = (dcb * c_gate).astype(BF16)
        for k in range(B_CONV_WIDTH):
            dcbw_ref[k:k + 1, :] += dcbw[k]
        dn = jnp.zeros((tm, D_MODEL), F32)
        for j in range(N_CHIPS):
            dn = dn + _dot_nt(dz_ref[:, j * ws:(j + 1) * ws], win_v[j])
        wide[...] = dn
        g = g_ref[...]
        dg = jnp.zeros((1, D_MODEL), F32)
        for r0 in range(0, tm, ELEM_ROWS):
            rows = slice(r0, r0 + ELEM_ROWS)
            xv = x_ref[rows, :]
            _, rstd = _rms_fwd(xv, g)
            dx, dg_part = _rms_bwd(wide[rows, :], xv, rstd, g)
            dx_ref[rows, :] = dh_ref[rows, :] + dx
            dg = dg + dg_part
        dg_ref[...] += dg

        @pl.when(i == nt - 1)
        def _():
            for k in range(A_CONV_WIDTH):
                dcaw_ref[k:k + 1, :] = jnp.sum(dw_acc[k], axis=0, keepdims=True)

    row = lambda cols: jax.ShapeDtypeStruct((1, cols), F32)
    rs_ = functools.partial(_row_spec, rev_nt=nt)
    return _pallas(
        body, [dh, x, norm_g, z, a2, cv, w_in, conv_a_w, ln_g, ln_b, conv_b_w, w_out], name="bwd_even", grid=(nt,),
        in_specs=[rs_(tm, D_MODEL), rs_(tm, D_MODEL), _full_spec((1, D_MODEL)), rs_(tm, IN_EVEN), rs_(tm, A_DIM),
                  rs_(tm, B_DIM), ANY, _full_spec((A_CONV_WIDTH, A_DIM)), _full_spec((1, A_DIM)), _full_spec((1, A_DIM)),
                  _full_spec((B_CONV_WIDTH, B_DIM)), ANY],
        out_specs=[rs_(tm, D_MODEL), rs_(tm, IN_EVEN), _full_spec((1, D_MODEL)), _full_spec((A_CONV_WIDTH, A_DIM)),
                   _full_spec((1, A_DIM)), _full_spec((1, A_DIM)), _full_spec((1, A_DIM)), _full_spec((B_CONV_WIDTH, B_DIM))],
        out_shape=[jax.ShapeDtypeStruct((tokens, D_MODEL), F32), jax.ShapeDtypeStruct((tokens, IN_EVEN), BF16),
                   row(D_MODEL), jax.ShapeDtypeStruct((A_CONV_WIDTH, A_DIM), F32), row(A_DIM), row(A_DIM), row(A_DIM),
                   jax.ShapeDtypeStruct((B_CONV_WIDTH, B_DIM), F32)],
        scratch_shapes=[pltpu.VMEM((N_CHIPS, D_MODEL, ws), BF16), pltpu.VMEM((D_MODEL, D_MODEL), BF16),
                        pltpu.VMEM((SUBLANES, tm + A_HALO, A_DIM), F32), pltpu.VMEM((tm + B_HALO, B_DIM), F32),
                        pltpu.VMEM((tm, A_DIM), F32), pltpu.VMEM((tm, A_DIM), F32), pltpu.VMEM((tm, A_DIM), F32),
                        pltpu.VMEM((tm, D_MODEL), F32),
                        pltpu.VMEM((A_CONV_WIDTH, CONV_ROWS, A_DIM), F32), pltpu.SemaphoreType.DMA((N_LOADS,))],
        vmem_mib=56, riders=riders)


def _wgrad(a, b, name, *, col_shards, riders=()):
    tokens, m = a.shape
    n = b.shape[1]
    kc = 512
    if col_shards:
        bm, bn = m // 2, n // N_CHIPS
        grid = (2, N_CHIPS)
        out_spec = pl.BlockSpec((None, None, bm, bn), lambda i, j: (j, i, 0, 0))
    elif m // 8 >= MXU_ROWS:
        bm, bn = m // 8, n
        grid = (8, 1)
        out_spec = pl.BlockSpec((None, None, bm, bn), lambda i, j: (i // 2, i % 2, 0, 0))
    else:
        bm, bn = m // N_CHIPS, n
        grid = (N_CHIPS, 1)
        out_spec = pl.BlockSpec((None, 2, bm // 2, bn), lambda i, j: (i, 0, 0, 0))

    def body(a_ref, b_ref, o_ref):
        acc = jnp.zeros((bm, bn), F32)
        for k0 in range(0, tokens, kc):
            acc = acc + _dot_tn(a_ref[k0:k0 + kc, :].astype(BF16), b_ref[k0:k0 + kc, :].astype(BF16))
        if len(o_ref.shape) == 3:
            o_ref[0] = acc[0:bm // 2]
            o_ref[1] = acc[bm // 2:bm]
        else:
            o_ref[...] = acc

    out_rows = m // 2 if col_shards else m // 8
    outs, routs = _pallas(
        body, [a, b], name=name, grid=grid,
        in_specs=[pl.BlockSpec((tokens, bm), lambda i, j: (0, i)), pl.BlockSpec((tokens, bn), lambda i, j: (0, j))],
        out_specs=[out_spec], out_shape=[jax.ShapeDtypeStruct((N_CHIPS, 2, out_rows, bn), F32)],
        vmem_mib=56, riders=riders)
    return outs[0], routs


def _wgrad_pair(a, b, name, *, col_shards, riders=()):
    tokens, m = a.shape
    n = b.shape[1]
    kc = 512
    c0 = lax.axis_index("c")

    def half(ph, pre):
        return (ph + 1 + pre[0]) % 2

    if col_shards:
        bm, bn = m // 2, n // N_CHIPS
        a_spec = pl.BlockSpec((tokens, bm), lambda ph, q, pre: (0, half(ph, pre)))
        b_spec = pl.BlockSpec((tokens, bn), lambda ph, q, pre: (0, q))
    else:
        bm, bn = m // 8, n
        a_spec = pl.BlockSpec((tokens, bm), lambda ph, q, pre: (0, 2 * q + half(ph, pre)))
        b_spec = pl.BlockSpec((tokens, bn), lambda ph, q, pre: (0, 0))

    def body(pre_ref, a_ref, b_ref, o_ref, give, got, send_sems, recv_sems):
        ph, q = pl.program_id(0), pl.program_id(1)
        acc = jnp.zeros((bm, bn), F32)
        for k0 in range(0, tokens, kc):
            acc = acc + _dot_tn(a_ref[k0:k0 + kc, :].astype(BF16), b_ref[k0:k0 + kc, :].astype(BF16))
        x, y, cc = _mesh_pos()

        def tile(t):
            return _remote(give.at[t], got.at[t], send_sems.at[t], recv_sems.at[t], (x, y, 1 - cc))

        @pl.when(ph == 0)
        def _():
            give[q] = acc
            tile(q).start()

        @pl.when(ph == 1)
        def _():
            tile(q).wait_recv()
            o_ref[...] = (acc + got[q]).astype(BF16)

        @pl.when((ph == 1) & (q == N_CHIPS - 1))
        def _():
            for t in range(N_CHIPS):
                tile(t).wait_send()

    outs, routs = _pallas(
        body, [a, b], name=name, grid=(2, N_CHIPS), in_specs=[a_spec, b_spec],
        out_specs=[pl.BlockSpec((None, bm, bn), lambda ph, q, pre: (ph * q, 0, 0))],
        out_shape=[jax.ShapeDtypeStruct((N_CHIPS, bm, bn), BF16)],
        scratch_shapes=[pltpu.VMEM((N_CHIPS, bm, bn), F32), pltpu.VMEM((N_CHIPS, bm, bn), F32),
                        pltpu.SemaphoreType.DMA((N_CHIPS,)), pltpu.SemaphoreType.DMA((N_CHIPS,))],
        vmem_mib=56, riders=riders, prefetch=jnp.reshape(c0, (1,)).astype(jnp.int32))
    return outs[0], routs


class _GradReduce:
    def __init__(self, name, grad=None, chip_sum=None):
        self.name, self.grad, self.chip_sum = name, grad, chip_sum
        self.full = None

    def pair_swap(self):
        return _PairSwap([self.grad])

    def took_pair(self, outs):
        self.chip_sum = _in_hbm(_add_pair(self.grad, outs[0], f"pair_sum_{self.name}"))

    def took_chips(self, outs):
        self.full = _in_hbm(_add_chips(self.chip_sum, outs[0], f"chip_sum_{self.name}"))

    def chips_beside(self, collective_id):
        self.took_chips([_chip_swap_beside(self.chip_sum, f"chip_swap_{self.name}", collective_id)])

    def pair_share(self):
        return _PairShare([self.full])

    def took_share(self, outs):
        self.full = outs[0]

    def reduced(self):
        return jnp.reshape(self.full, (2 * self.full.shape[1], self.full.shape[2]))


def _forward_backward(x2, tgt2, w, conv_a_w, conv_b_w, od_norm, od_bias, od_lng, od_lnb,
                      ev_norm_g, ev_conv_a_b, ev_ln_a_g, ev_ln_a_b, od_w_s, od_b_s, mlp_norm_g, final_norm_g,
                      *, tm, seq, distributed=True):
    d = x2.shape[1]
    b_s_rows = jnp.broadcast_to(od_b_s[0][:, :, None], (C_GROUPS, CHUNK, CHUNK))
    (h1, n0, z, a2, cv, mix), _ = _fwd_even(
        x2, ev_norm_g, w["ev_in"], conv_a_w, ev_conv_a_b, ev_ln_a_g, ev_ln_a_b, conv_b_w, w["ev_out"], tm=tm, seq=seq)
    (h2, n1, p0, q0), _ = _fwd_mlp(h1, mlp_norm_g[0:1], w["w1_0"], w["w2_0"], 0, tm=tm)
    (h3, n2, s, cdf, sv, y), _ = _fwd_odd(h2, od_norm, w["od_in"], od_bias, od_lng, od_lnb, od_w_s[0], b_s_rows,
                                          w["od_out"], tm=tm)
    (n3, p1, q1, loss_part, dh4, dh4b, d_final_g), _ = _fwd_mlp(
        h3, mlp_norm_g[1:2], w["w1_1"], w["w2_1"], 1, tm=tm,
        head=(jnp.reshape(final_norm_g, (1, d)), tgt2))

    red = {}

    def swap(*names):
        return [red[nm].pair_swap() for nm in names] if distributed else []

    def share(*names):
        return [red[nm].pair_share() for nm in names] if distributed else []

    def took(routs, *steps):
        if distributed:
            for (nm, what), outs in zip(steps, routs):
                getattr(red[nm], what)(outs)

    swap_ids = iter(range(FIRST_SWAP_ID, FIRST_SWAP_ID + 8))

    def beside(name):
        if distributed:
            red[name].chips_beside(next(swap_ids))

    def big(lhs, rhs, name, col_shards, riders=()):
        if distributed:
            chip_sum, routs = _wgrad_pair(lhs, rhs, f"wgrad_{name}", col_shards=col_shards, riders=riders)
            red[name] = _GradReduce(name, chip_sum=_in_hbm(chip_sum))
        else:
            g, routs = _wgrad(lhs, rhs, f"wgrad_{name}", col_shards=col_shards)
            red[name] = _GradReduce(name, grad=g)
        return routs

    big(q1, dh4b, "w2_1", False)
    beside("w2_1")
    (dh3, dh3b, dp1, d_mlp_g1), _ = _bwd_mlp(dh4, h3, mlp_norm_g[1:2], p1, w["w1_1"], w["w2_1"], 1, tm=tm)
    big(n3, dp1, "w1_1", True)
    beside("w1_1")
    g, routs = _wgrad(y, dh3b, "wgrad_od_out", col_shards=False, riders=share("w2_1"))
    red["od_out"] = _GradReduce("od_out", grad=g)
    took(routs, ("w2_1", "took_share"))
    (dh2, dh2b, ds, d_od_norm, d_od_bin, d_od_lng, d_od_lnb, d_ws, d_bs), _ = _bwd_odd(
        dh3, h2, od_norm, s, cdf, sv, w["od_in"], od_lng, od_lnb, od_w_s[0], w["od_out"], tm=tm)
    routs = big(n2, ds, "od_in", True, riders=share("w1_1") + swap("od_out"))
    took(routs, ("w1_1", "took_share"), ("od_out", "took_pair"))
    beside("od_in")
    beside("od_out")
    half_groups = C_GROUPS // 2
    early = {"loss": loss_part, "od_w_s_lo": d_ws[:half_groups], "od_b_s": d_bs, "mlp_norm_g1": d_mlp_g1, "final_norm_g": d_final_g,
             "od_norm_g": d_od_norm, "od_b_in": d_od_bin, "od_ln_v_g": d_od_lng, "od_ln_v_b": d_od_lnb}
    share_early = [_ShareAll(list(early.values()))] if distributed else []
    routs = big(q0, dh2b, "w2_0", False, riders=share_early)
    landed_early = routs[0] if distributed else []
    beside("w2_0")
    (dh1, dh1b, dp0, d_mlp_g0), _ = _bwd_mlp(dh2, h1, mlp_norm_g[0:1], p0, w["w1_0"], w["w2_0"], 0, tm=tm)
    middle = {"od_w_s_hi": d_ws[half_groups:]}
    share_middle = [_ShareAll(list(middle.values()))] if distributed else []
    g, _ = _wgrad(mix, dh1b, "wgrad_ev_out", col_shards=False)
    red["ev_out"] = _GradReduce("ev_out", grad=g)
    routs = big(n1, dp0, "w1_0", True,
                riders=share("od_out") + share("od_in") + share("w2_0") + swap("ev_out") + share_middle)
    took(routs, ("od_out", "took_share"), ("od_in", "took_share"), ("w2_0", "took_share"), ("ev_out", "took_pair"))
    landed_middle = routs[4] if distributed else []
    beside("w1_0")
    beside("ev_out")

    (dx, dz, d_ev_norm, d_caw, d_cab, d_ev_lng, d_ev_lnb, d_cbw), _ = _bwd_even(
        dh1, x2, ev_norm_g, z, a2, cv, w["ev_in"], conv_a_w, ev_ln_a_g, ev_ln_a_b, conv_b_w, w["ev_out"], tm=tm, seq=seq)
    late = {"mlp_norm_g0": d_mlp_g0, "ev_norm_g": d_ev_norm, "ev_conv_a_b": d_cab, "ev_ln_a_g": d_ev_lng,
            "ev_ln_a_b": d_ev_lnb, "ev_conv_a_w": d_caw, "ev_conv_b_w": d_cbw}
    share_late = [_ShareAll(list(late.values()))] if distributed else []
    routs = big(n0, dz, "ev_in", True, riders=share("ev_out") + share("w1_0") + share_late)
    took(routs, ("ev_out", "took_share"), ("w1_0", "took_share"))
    beside("ev_in")
    own = {**early, **middle, **late}
    landed = dict(zip(own.keys(), landed_early + landed_middle + routs[2])) if distributed else None
    return dx, red, own, landed


def _rows128(a):
    rows = jnp.reshape(a, (-1, LANES))
    pad = (-rows.shape[0]) % SUBLANES
    return jnp.pad(rows, ((0, pad), (0, 0))) if pad else rows


def _pack(arrays):
    return jnp.concatenate([_rows128(a) for a in arrays], axis=0)


def _unpack(buf, shapes):
    out, r0 = [], 0
    for shp in shapes:
        size = 1
        for dim in shp:
            size *= dim
        nr = size // LANES
        out.append(jnp.reshape(buf[r0:r0 + nr], shp))
        r0 += nr + (-nr) % SUBLANES
    return out


def kernel(x, ev_norm_g, ev_w_in, ev_conv_a_w, ev_conv_a_b, ev_ln_a_g, ev_ln_a_b, ev_conv_b_w, ev_w_out, od_norm_g, od_w_in, od_b_in, od_ln_v_g, od_ln_v_b, od_w_s, od_b_s, od_w_out, mlp_norm_g, mlp_w1, mlp_w2, final_norm_g, loss_target, m_ev_norm_g, m_ev_w_in, m_ev_conv_a_w, m_ev_conv_a_b, m_ev_ln_a_g, m_ev_ln_a_b, m_ev_conv_b_w, m_ev_w_out, m_od_norm_g, m_od_w_in, m_od_b_in, m_od_ln_v_g, m_od_ln_v_b, m_od_w_s, m_od_b_s, m_od_w_out, m_mlp_norm_g, m_mlp_w1, m_mlp_w2, m_final_norm_g, v_ev_norm_g, v_ev_w_in, v_ev_conv_a_w, v_ev_conv_a_b, v_ev_ln_a_g, v_ev_ln_a_b, v_ev_conv_b_w, v_ev_w_out, v_od_norm_g, v_od_w_in, v_od_b_in, v_od_ln_v_g, v_od_ln_v_b, v_od_w_s, v_od_b_s, v_od_w_out, v_mlp_norm_g, v_mlp_w1, v_mlp_w2, v_final_norm_g):
    tm = TOKEN_TILE
    batch, seq, d = x.shape
    tokens = batch * seq
    x2 = jnp.reshape(x, (tokens, d))
    tgt2 = jnp.reshape(loss_target, (tokens, d))
    chip = 2 * lax.axis_index("x") + lax.axis_index("y")

    small_shapes = [(A_CONV_WIDTH, LANES), (B_CONV_WIDTH, LANES), (256,), (512,), (256,), (256,)]
    small_shard = _pack([ev_conv_a_w[0], ev_conv_b_w[0], od_norm_g[0], od_b_in[0], od_ln_v_g[0], od_ln_v_b[0]])
    small_shard = jnp.pad(small_shard, ((0, (-small_shard.shape[0]) % (2 * SUBLANES)), (0, 0)))
    first = [_place_shard(ev_w_in, 0, BF16, "place_ev_w_in"), _place_shard(ev_w_out, 0, BF16, "place_ev_w_out"),
             _place_shard(small_shard[None], 0, F32, "place_small")]
    staged = {
        "w1_0": _place_shard(mlp_w1, 0, BF16, "place_w1_0"), "w2_0": _place_shard(mlp_w2, 0, BF16, "place_w2_0"),
        "od_in": _place_shard(od_w_in, 0, BF16, "place_od_w_in"), "od_out": _place_shard(od_w_out, 0, BF16, "place_od_w_out"),
        "w1_1": _place_shard(mlp_w1, 1, BF16, "place_w1_1"), "w2_1": _place_shard(mlp_w2, 1, BF16, "place_w2_1"),
    }
    first = [_in_hbm(a) for a in first]
    staged = {nm: _in_hbm(a) for nm, a in staged.items()}
    g_ev_in, g_ev_out, g_small = _gather_beside(first, "gather_stage0", collective_id=1)
    gathered = {"ev_in": g_ev_in, "ev_out": g_ev_out}
    for stage, names in enumerate((("w1_0", "w2_0"), ("od_in", "od_out", "w1_1"), ("w2_1",))):
        done = _gather_beside([staged[nm] for nm in names], f"gather_stage{stage + 1}", collective_id=stage + 2)
        gathered.update(zip(names, done))
    small_all = jnp.reshape(_plain_copy(g_small, "small_weights_copy"), (N_CHIPS, -1, LANES))
    per_chip = [_unpack(small_all[q], small_shapes) for q in range(N_CHIPS)]
    conv_a_w = jnp.concatenate([pc[0] for pc in per_chip], axis=1)
    conv_b_w = jnp.concatenate([pc[1] for pc in per_chip], axis=1)
    od_norm = jnp.concatenate([pc[2] for pc in per_chip])[None, :]
    od_bias = jnp.concatenate([pc[3] for pc in per_chip])[None, :]
    od_lng = jnp.concatenate([pc[4] for pc in per_chip])[None, :]
    od_lnb = jnp.concatenate([pc[5] for pc in per_chip])[None, :]

    dx, red, own, landed = _forward_backward(
        x2, tgt2, gathered, conv_a_w, conv_b_w, od_norm, od_bias, od_lng, od_lnb,
        ev_norm_g, ev_conv_a_b, ev_ln_a_g, ev_ln_a_b, od_w_s, od_b_s, mlp_norm_g, final_norm_g, tm=tm, seq=seq)

    routs = _exchange([red["ev_in"].pair_share()], "reduce_tail")
    red["ev_in"].took_share(routs[0])

    given = {"ev_norm_g": (ev_norm_g, m_ev_norm_g, v_ev_norm_g), "ev_conv_a_b": (ev_conv_a_b, m_ev_conv_a_b, v_ev_conv_a_b),
             "ev_ln_a_g": (ev_ln_a_g, m_ev_ln_a_g, v_ev_ln_a_g), "ev_ln_a_b": (ev_ln_a_b, m_ev_ln_a_b, v_ev_ln_a_b),
             "od_w_s": (od_w_s, m_od_w_s, v_od_w_s), "od_b_s": (od_b_s, m_od_b_s, v_od_b_s),
             "mlp_norm_g": (mlp_norm_g, m_mlp_norm_g, v_mlp_norm_g), "final_norm_g": (final_norm_g, m_final_norm_g, v_final_norm_g),
             "ev_conv_a_w": (ev_conv_a_w, m_ev_conv_a_w, v_ev_conv_a_w), "ev_conv_b_w": (ev_conv_b_w, m_ev_conv_b_w, v_ev_conv_b_w),
             "od_norm_g": (od_norm_g, m_od_norm_g, v_od_norm_g), "od_b_in": (od_b_in, m_od_b_in, v_od_b_in),
             "od_ln_v_g": (od_ln_v_g, m_od_ln_v_g, v_od_ln_v_g), "od_ln_v_b": (od_ln_v_b, m_od_ln_v_b, v_od_ln_v_b)}
    shaped = {nm: tuple(jnp.reshape(a, shape) for a in given[nm]) for nm, shape, _, _ in SMALL_WEIGHTS}
    loss11, small_upd = _small_update(own, landed, shaped)
    loss = loss11[0, 0]
    upd = {nm: [jnp.reshape(o, given[nm][0].shape) for o in outs] for nm, outs in small_upd.items()}

    def big_update(wt, m, v, names, call):
        grads = [red[nm].reduced() for nm in names]
        shp3 = (len(grads),) + grads[0].shape
        outs, _ = _adamw(jnp.reshape(wt, shp3), jnp.reshape(m, shp3), jnp.reshape(v, shp3), grads, call)
        return [jnp.reshape(o, wt.shape) for o in outs], None

    upd["mlp_w2"], _ = big_update(mlp_w2, m_mlp_w2, v_mlp_w2, ["w2_0", "w2_1"], "adamw_mlp_w2")
    upd["mlp_w1"], _ = big_update(mlp_w1, m_mlp_w1, v_mlp_w1, ["w1_0", "w1_1"], "adamw_mlp_w1")
    upd["ev_w_in"], _ = big_update(ev_w_in, m_ev_w_in, v_ev_w_in, ["ev_in"], "adamw_ev_w_in")
    upd["ev_w_out"], _ = big_update(ev_w_out, m_ev_w_out, v_ev_w_out, ["ev_out"], "adamw_ev_w_out")
    upd["od_w_in"], _ = big_update(od_w_in, m_od_w_in, v_od_w_in, ["od_in"], "adamw_od_w_in")
    upd["od_w_out"], _ = big_update(od_w_out, m_od_w_out, v_od_w_out, ["od_out"], "adamw_od_w_out")

    order = ["ev_norm_g", "ev_w_in", "ev_conv_a_w", "ev_conv_a_b", "ev_ln_a_g", "ev_ln_a_b", "ev_conv_b_w", "ev_w_out",
             "od_norm_g", "od_w_in", "od_b_in", "od_ln_v_g", "od_ln_v_b", "od_w_s", "od_b_s", "od_w_out", "mlp_norm_g",
             "mlp_w1", "mlp_w2", "final_norm_g"]
    grad_x = jnp.reshape(dx, x.shape)
    return (loss, grad_x, *[upd[nm][0] for nm in order], *[upd[nm][1] for nm in order],
            *[upd[nm][2] for nm in order], *[upd[nm][3] for nm in order])
```

```python
import functools

import jax
import jax.numpy as jnp
from jax import lax
from jax.experimental import pallas as pl
from jax.experimental.pallas import tpu as pltpu
from jax.experimental.pallas import tpu_sc as plsc

F32 = jnp.float32
BF16 = jnp.bfloat16

D_MODEL = 1024
A_DIM = 512
B_DIM = 512
IN_EVEN = 2 * A_DIM + 3 * B_DIM
A_CONV_WIDTH = 31
B_CONV_WIDTH = 3
CHUNK = 128
C_GROUPS = 8
C_DIM = 1024
D_FF = 4096
RMS_EPS = 1e-6
LN_EPS = 1e-5
ADAM_LR = 0.001
ADAM_B1 = 0.9
ADAM_B2 = 0.999
ADAM_EPS = 1e-08
ADAM_WD = 0.01
ADAM_STEP = 10

N_CHIPS = 4
N_DEV = 8
TOKEN_TILE = 512
A_HALO = 32
B_HALO = 8
CONV_ROWS = 16
DW_TAPS = 4
ELEM_ROWS = 16
PAIR = 2 * CHUNK
LANES = 128
SUBLANES = 8
MXU_ROWS = 256
MIB = 1024 * 1024
MESH = pl.DeviceIdType.MESH
ANY = pl.BlockSpec(memory_space=pl.ANY)


def _dot(a, b):
    return lax.dot_general(a, b, (((1,), (0,)), ((), ())), preferred_element_type=F32)


def _dot_nt(a, b):
    return lax.dot_general(a, b, (((1,), (1,)), ((), ())), preferred_element_type=F32)


def _dot_tn(a, b):
    return lax.dot_general(a, b, (((0,), (0,)), ((), ())), preferred_element_type=F32)


def _params(vmem_mib, n_axes=1):
    return pltpu.CompilerParams(dimension_semantics=("arbitrary",) * n_axes, vmem_limit_bytes=vmem_mib * MIB)


def _row_spec(tm, cols, rev_nt=None):
    if rev_nt is None:
        return pl.BlockSpec((tm, cols), lambda i: (i, 0))
    return pl.BlockSpec((tm, cols), lambda i: (rev_nt - 1 - i, 0))


def _full_spec(shape):
    nd = len(shape)
    return pl.BlockSpec(shape, lambda i: (0,) * nd)


def _block_rows(rows, cap=512):
    best = SUBLANES
    for br in range(SUBLANES, min(rows, cap) + 1, SUBLANES):
        if rows % br == 0:
            best = br
    return best


FIRST_SWAP_ID = 5
N_LOADS = 2 * 2 * N_CHIPS


def _load_weights(loads, sems):
    @pl.when(pl.program_id(0) == 0)
    def _():
        copies = []
        for src, dst, rows_of_one in loads:
            r = src.shape[2]
            for q in range(N_CHIPS):
                for h in range(2):
                    part = dst.at[pl.ds((2 * q + h) * r, r)] if rows_of_one else dst.at[q, pl.ds(h * r, r)]
                    copies.append(pltpu.make_async_copy(src.at[q, h], part, sems.at[len(copies)]))
        for cp in copies:
            cp.start()
        for cp in copies:
            cp.wait()


def _rms_fwd(x, g):
    rstd = lax.rsqrt(jnp.mean(x * x, axis=-1, keepdims=True) + RMS_EPS)
    return x * rstd * g, rstd


def _rms_bwd(dn, x, rstd, g):
    a = dn * g
    xh = x * rstd
    dx = rstd * (a - xh * jnp.mean(a * xh, axis=-1, keepdims=True))
    dg = jnp.sum(dn * xh, axis=0, keepdims=True)
    return dx, dg


def _ln_stats(v):
    mu = jnp.mean(v, axis=-1, keepdims=True)
    xc = v - mu
    rs = lax.rsqrt(jnp.mean(xc * xc, axis=-1, keepdims=True) + LN_EPS)
    return xc * rs, rs


def _ln_bwd(dy, xhat, rs, g):
    dxh = dy * g
    dv = rs * (dxh - jnp.mean(dxh, axis=-1, keepdims=True) - xhat * jnp.mean(dxh * xhat, axis=-1, keepdims=True))
    return dv, jnp.sum(dy * xhat, axis=0, keepdims=True), jnp.sum(dy, axis=0, keepdims=True)


def _gelu_cdf(s):
    return 0.5 * (1.0 + lax.erf(s * 0.7071067811865476))


def _mesh_pos():
    return lax.axis_index("x"), lax.axis_index("y"), lax.axis_index("c")


def _other_chips(x, y):
    return [(1 - x, y), (x, 1 - y), (1 - x, 1 - y)]


def _remote(src, dst, send_sem, recv_sem, to):
    return pltpu.make_async_remote_copy(src_ref=src, dst_ref=dst, send_sem=send_sem, recv_sem=recv_sem,
                                        device_id=to, device_id_type=MESH)


def _like(arrays):
    return [jax.ShapeDtypeStruct(a.shape, a.dtype) for a in arrays]


class _PairSwap:
    def __init__(self, grads):
        self.ins = list(grads)
        self.out_shapes = [jax.ShapeDtypeStruct((g.shape[0],) + g.shape[2:], g.dtype) for g in grads]
        self.aliases = {}
        self.n_sems = len(grads)

    def _copies(self, ins, outs, send, recv):
        x, y, c = _mesh_pos()
        return [_remote(ins[t].at[:, 1 - c], outs[t], send.at[t], recv.at[t], (x, y, 1 - c)) for t in range(len(ins))]

    def start(self, ins, outs, send, recv):
        for cp in self._copies(ins, outs, send, recv):
            cp.start()

    def finish(self, ins, outs, send, recv):
        for cp in self._copies(ins, outs, send, recv):
            cp.wait()


class _ChipSwap:
    def __init__(self, parts):
        self.ins = list(parts)
        self.out_shapes = [jax.ShapeDtypeStruct((3,) + p.shape[1:], p.dtype) for p in parts]
        self.aliases = {}
        self.n_sems = 3 * len(parts)

    def _copies(self, ins, outs, send, recv):
        x, y, c = _mesh_pos()
        return [_remote(ins[t].at[2 * chip[0] + chip[1]], outs[t].at[k], send.at[3 * t + k], recv.at[3 * t + k], (*chip, c))
                for t in range(len(ins)) for k, chip in enumerate(_other_chips(x, y))]

    def start(self, ins, outs, send, recv):
        for cp in self._copies(ins, outs, send, recv):
            cp.start()

    def finish(self, ins, outs, send, recv):
        for cp in self._copies(ins, outs, send, recv):
            cp.wait()


class _PairShare:
    def __init__(self, fulls):
        self.ins = list(fulls)
        self.out_shapes = _like(fulls)
        self.aliases = {t: t for t in range(len(fulls))}
        self.n_sems = len(fulls)

    def _copies(self, ins, outs, send, recv):
        x, y, c = _mesh_pos()
        return [_remote(ins[t].at[c], outs[t].at[c], send.at[t], recv.at[t], (x, y, 1 - c)) for t in range(len(ins))]

    def start(self, ins, outs, send, recv):
        for cp in self._copies(ins, outs, send, recv):
            cp.start()

    def finish(self, ins, outs, send, recv):
        for cp in self._copies(ins, outs, send, recv):
            cp.wait()


class _ShareAll:
    def __init__(self, arrays):
        self.ins = list(arrays)
        self.out_shapes = [jax.ShapeDtypeStruct((N_DEV,) + a.shape, a.dtype) for a in arrays]
        self.aliases = {}
        self.n_sems = (N_DEV - 1) * len(arrays)

    def _peers(self):
        x, y, c = _mesh_pos()
        flips = [((r >> 2) & 1, (r >> 1) & 1, r & 1) for r in range(1, N_DEV)]
        return (x, y, c), [(x ^ fx, y ^ fy, c ^ fc) for fx, fy, fc in flips]

    def _sends(self, ins, outs, send, recv):
        (x, y, c), peers = self._peers()
        mine = 4 * x + 2 * y + c
        return [_remote(ins[a], outs[a].at[mine], send.at[7 * a + r], recv.at[7 * a + r], peer)
                for a in range(len(ins)) for r, peer in enumerate(peers)]

    def start(self, ins, outs, send, recv):
        for cp in self._sends(ins, outs, send, recv):
            cp.start()

    def finish(self, ins, outs, send, recv):
        (x, y, c), peers = self._peers()
        for a in range(len(ins)):
            for r, (px, py, pc) in enumerate(peers):
                blk = outs[a].at[4 * px + 2 * py + pc]
                _remote(blk, blk, send.at[7 * a + r], recv.at[7 * a + r], (x, y, c)).wait_recv()
        for cp in self._sends(ins, outs, send, recv):
            cp.wait_send()


def _gather_beside(bufs, name, collective_id):
    n = len(bufs)
    per = 7
    refs = [jax.new_ref(b, memory_space=pltpu.MemorySpace.HBM) for b in bufs]

    @pl.kernel(mesh=plsc.ScalarSubcoreMesh(axis_name="sequencer", num_cores=1), name=name,
               scratch_types=(pltpu.SemaphoreType.DMA((per * n,)), pltpu.SemaphoreType.DMA((per * n,))),
               compiler_params=pltpu.CompilerParams(collective_id=collective_id))
    def launch(send, recv):
        x, y, c = _mesh_pos()
        me, sibling = (x, y, c), (x, y, 1 - c)
        x_nbr, y_nbr = (1 - x, y, c), (x, 1 - y, c)
        mine, via_x, via_y, diag = 2 * x + y, 2 * (1 - x) + y, 2 * x + (1 - y), 2 * (1 - x) + (1 - y)
        barrier = pltpu.get_barrier_semaphore()
        peers = [x_nbr, y_nbr, sibling]
        for peer in peers:
            pl.semaphore_signal(barrier, inc=1, device_id=peer, device_id_type=MESH)
        pl.semaphore_wait(barrier, len(peers))

        def copy(t, k, src, dst, to):
            return _remote(src, dst, send.at[per * t + k], recv.at[per * t + k], to)

        def piece(t, chip, half, rows=None):
            blk = refs[t].at[chip, half]
            return blk if rows is None else blk.at[rows]

        started = []

        def go(cp):
            cp.start()
            started.append(cp)

        upper = [pl.ds(0, r.shape[2] // 2) for r in refs]
        lower = [pl.ds(r.shape[2] // 2, r.shape[2] // 2) for r in refs]
        for t in range(n):
            go(copy(t, 0, piece(t, mine, c), piece(t, mine, c), x_nbr))
            go(copy(t, 1, piece(t, mine, c), piece(t, mine, c), y_nbr))
        for t in range(n):
            copy(t, 0, piece(t, via_x, c), piece(t, via_x, c), me).wait_recv()
            go(copy(t, 2, piece(t, via_x, c, upper[t]), piece(t, via_x, c, upper[t]), y_nbr))
            go(copy(t, 4, piece(t, via_x, c), piece(t, via_x, c), sibling))
            copy(t, 1, piece(t, via_y, c), piece(t, via_y, c), me).wait_recv()
            go(copy(t, 3, piece(t, via_y, c, lower[t]), piece(t, via_y, c, lower[t]), x_nbr))
            go(copy(t, 5, piece(t, via_y, c), piece(t, via_y, c), sibling))
        for t in range(n):
            copy(t, 2, piece(t, diag, c, upper[t]), piece(t, diag, c, upper[t]), me).wait_recv()
            copy(t, 3, piece(t, diag, c, lower[t]), piece(t, diag, c, lower[t]), me).wait_recv()
            go(copy(t, 6, piece(t, diag, c), piece(t, diag, c), sibling))
        for t in range(n):
            for k, chip in ((4, via_x), (5, via_y), (6, diag)):
                copy(t, k, piece(t, chip, 1 - c), piece(t, chip, 1 - c), me).wait_recv()
        for cp in started:
            cp.wait_send()

    launch()
    return [r[...] for r in refs]


def _chip_swap_beside(parts, name, collective_id):
    src = jax.new_ref(parts, memory_space=pltpu.MemorySpace.HBM)
    dst = jax.empty_ref(jax.ShapeDtypeStruct((N_CHIPS - 1,) + parts.shape[1:], parts.dtype),
                        memory_space=pltpu.MemorySpace.HBM)
    swap = _ChipSwap([parts])

    @pl.kernel(mesh=plsc.ScalarSubcoreMesh(axis_name="sequencer", num_cores=1), name=name,
               scratch_types=(pltpu.SemaphoreType.DMA((N_CHIPS - 1,)), pltpu.SemaphoreType.DMA((N_CHIPS - 1,))),
               compiler_params=pltpu.CompilerParams(collective_id=collective_id))
    def launch(send, recv):
        x, y, c = _mesh_pos()
        barrier = pltpu.get_barrier_semaphore()
        peers = [(*chip, c) for chip in _other_chips(x, y)]
        for peer in peers:
            pl.semaphore_signal(barrier, inc=1, device_id=peer, device_id_type=MESH)
        pl.semaphore_wait(barrier, len(peers))
        swap.start([src], [dst], send, recv)
        swap.finish([src], [dst], send, recv)

    launch()
    return dst[...]


def _pallas(body, operands, *, name, grid, in_specs, out_specs, out_shape, scratch_shapes=(), vmem_mib=32, riders=(),
            prefetch=None):
    in_specs, out_specs, out_shape, scratch_shapes = list(in_specs), list(out_specs), list(out_shape), list(scratch_shapes)
    if not riders and prefetch is None:
        outs = pl.pallas_call(body, name=name, grid=grid, in_specs=in_specs, out_specs=out_specs, out_shape=out_shape,
                              scratch_shapes=scratch_shapes, compiler_params=_params(vmem_mib, len(grid)))(*operands)
        return list(outs), []
    n_in, n_out, n_scr = len(in_specs), len(out_specs), len(scratch_shapes)
    r_in = [len(r.ins) for r in riders]
    r_out = [len(r.out_shapes) for r in riders]
    steps = 1
    for g in grid:
        steps *= g

    n_pre = 0 if prefetch is None else 1

    def wrapped(*refs):
        refs = list(refs)
        pre, refs = refs[:n_pre], refs[n_pre:]
        ins, refs = refs[:n_in], refs[n_in:]
        rins = []
        for k in r_in:
            rins.append(refs[:k])
            refs = refs[k:]
        outs, refs = refs[:n_out], refs[n_out:]
        routs = []
        for k in r_out:
            routs.append(refs[:k])
            refs = refs[k:]
        scr, sems = refs[:n_scr], refs[n_scr:]
        step = 0
        for ax, g in enumerate(grid):
            step = step * g + pl.program_id(ax)

        def each(what):
            for j, r in enumerate(riders):
                if hasattr(r, what):
                    getattr(r, what)(rins[j], routs[j], sems[2 * j], sems[2 * j + 1])

        if grid:
            pl.when(step == 0)(lambda: each("start"))
        else:
            each("start")
        body(*pre, *ins, *outs, *scr)
        if grid:
            @pl.when(step == steps - 1)
            def _():
                each("near_end")
                each("finish")
        else:
            each("near_end")
            each("finish")

    aliases, off_in, off_out = {}, n_pre + n_in, n_out
    for r, ki, ko in zip(riders, r_in, r_out):
        for i, o in r.aliases.items():
            aliases[off_in + i] = off_out + o
        off_in, off_out = off_in + ki, off_out + ko
    sems = []
    for r in riders:
        sems += [pltpu.SemaphoreType.DMA((r.n_sems,)), pltpu.SemaphoreType.DMA((r.n_sems,))]
    layout = dict(grid=grid, in_specs=in_specs + [ANY] * sum(r_in), out_specs=out_specs + [ANY] * sum(r_out),
                  scratch_shapes=scratch_shapes + sems)
    if prefetch is not None:
        layout = dict(grid_spec=pltpu.PrefetchScalarGridSpec(num_scalar_prefetch=1, **layout))
    res = pl.pallas_call(
        wrapped, name=name, **layout,
        out_shape=out_shape + [s for r in riders for s in r.out_shapes], input_output_aliases=aliases,
        compiler_params=pltpu.CompilerParams(dimension_semantics=("arbitrary",) * len(grid),
                                             vmem_limit_bytes=vmem_mib * MIB, has_side_effects=True),
    )(*([] if prefetch is None else [prefetch]), *operands, *[a for r in riders for a in r.ins])
    res = list(res)
    outs, res = res[:n_out], res[n_out:]
    routs = []
    for k in r_out:
        routs.append(res[:k])
        res = res[k:]
    return outs, routs


def _exchange(riders, name):
    return _pallas(lambda: None, [], name=name, grid=(), in_specs=[], out_specs=[], out_shape=[], riders=riders)[1]


def _in_hbm(a):
    return pltpu.with_memory_space_constraint(a, pltpu.HBM)


def _place_shard(w, layer, dtype, name):
    _, rows, cols = w.shape
    half = rows // 2
    br = _block_rows(half)
    nb = half // br
    mine = 2 * lax.axis_index("x") + lax.axis_index("y")

    def body(q_ref, w_ref, o_ref):
        o_ref[...] = w_ref[...].astype(dtype)

    return pl.pallas_call(
        body, name=name,
        grid_spec=pltpu.PrefetchScalarGridSpec(
            num_scalar_prefetch=1, grid=(2, nb),
            in_specs=[pl.BlockSpec((None, br, cols), lambda h, i, q: (layer, h * nb + i, 0))],
            out_specs=pl.BlockSpec((None, None, br, cols), lambda h, i, q: (q[0], h, i, 0))),
        out_shape=pltpu.HBM((N_CHIPS, 2, half, cols), dtype),
        compiler_params=_params(16, 2),
    )(jnp.reshape(mine, (1,)).astype(jnp.int32), _in_hbm(w))


def _plain_copy(a, name):
    def body(a_ref, o_ref):
        o_ref[...] = a_ref[...]

    vmem = pl.BlockSpec(memory_space=pltpu.VMEM)
    return pl.pallas_call(body, name=name, in_specs=[vmem], out_specs=vmem,
                          out_shape=jax.ShapeDtypeStruct(a.shape, a.dtype))(a)


def _add_pair(g, recv, name):
    _, _, r, cdim = g.shape
    br = _block_rows(r, 256)
    c = lax.axis_index("c")

    def body(c_ref, g_ref, r_ref, o_ref):
        o_ref[...] = (g_ref[...] + r_ref[...]).astype(BF16)

    return pl.pallas_call(
        body, name=name,
        grid_spec=pltpu.PrefetchScalarGridSpec(
            num_scalar_prefetch=1, grid=(N_CHIPS, r // br),
            in_specs=[pl.BlockSpec((None, None, br, cdim), lambda q, i, c_ref: (q, c_ref[0], i, 0)),
                      pl.BlockSpec((None, br, cdim), lambda q, i, c_ref: (q, i, 0))],
            out_specs=pl.BlockSpec((None, br, cdim), lambda q, i, c_ref: (q, i, 0))),
        out_shape=pltpu.HBM((N_CHIPS, r, cdim), BF16),
        compiler_params=_params(16, 2),
    )(jnp.reshape(c, (1,)).astype(jnp.int32), _in_hbm(g), _in_hbm(recv))


def _add_chips(own, recv, name):
    _, r, cdim = own.shape
    br = _block_rows(r, 256)
    x, y, c = _mesh_pos()

    def body(pos_ref, own_ref, r_ref, o_ref):
        acc = own_ref[...].astype(F32)
        for k in range(3):
            acc = acc + r_ref[k].astype(F32)
        o_ref[...] = acc

    return pl.pallas_call(
        body, name=name,
        grid_spec=pltpu.PrefetchScalarGridSpec(
            num_scalar_prefetch=1, grid=(r // br,),
            in_specs=[pl.BlockSpec((None, br, cdim), lambda i, pos: (pos[0], i, 0)),
                      pl.BlockSpec((3, br, cdim), lambda i, pos: (0, i, 0))],
            out_specs=pl.BlockSpec((None, br, cdim), lambda i, pos: (pos[1], i, 0))),
        out_shape=pltpu.HBM((2, r, cdim), F32),
        compiler_params=_params(16, 1),
    )(jnp.stack([2 * x + y, c]).astype(jnp.int32), _in_hbm(own), recv)


def _adam_math(w, m, v, g):
    c1 = 1.0 / (1.0 - ADAM_B1 ** ADAM_STEP)
    c2 = 1.0 / (1.0 - ADAM_B2 ** ADAM_STEP)
    m_new = ADAM_B1 * m + (1.0 - ADAM_B1) * g
    v_new = ADAM_B2 * v + (1.0 - ADAM_B2) * (g * g)
    return -ADAM_LR * ((m_new * c1) / (jnp.sqrt(v_new * c2) + ADAM_EPS) + ADAM_WD * w), m_new, v_new


SMALL_WEIGHTS = [
    ("ev_norm_g", (1, D_MODEL), ["ev_norm_g"], None), ("ev_conv_a_b", (1, A_DIM), ["ev_conv_a_b"], None),
    ("ev_ln_a_g", (1, A_DIM), ["ev_ln_a_g"], None), ("ev_ln_a_b", (1, A_DIM), ["ev_ln_a_b"], None),
    ("od_w_s", (C_GROUPS, CHUNK, CHUNK), ["od_w_s_lo", "od_w_s_hi"], None), ("od_b_s", (C_GROUPS, CHUNK), ["od_b_s"], None),
    ("mlp_norm_g", (2, D_MODEL), ["mlp_norm_g0", "mlp_norm_g1"], None), ("final_norm_g", (1, D_MODEL), ["final_norm_g"], None),
    ("ev_conv_a_w", (A_CONV_WIDTH, A_DIM // N_CHIPS), ["ev_conv_a_w"], A_DIM // N_CHIPS),
    ("ev_conv_b_w", (B_CONV_WIDTH, B_DIM // N_CHIPS), ["ev_conv_b_w"], B_DIM // N_CHIPS),
    ("od_norm_g", (1, D_MODEL // N_CHIPS), ["od_norm_g"], D_MODEL // N_CHIPS),
    ("od_b_in", (1, 2 * C_DIM // N_CHIPS), ["od_b_in"], 2 * C_DIM // N_CHIPS),
    ("od_ln_v_g", (1, C_DIM // N_CHIPS), ["od_ln_v_g"], C_DIM // N_CHIPS),
    ("od_ln_v_b", (1, C_DIM // N_CHIPS), ["od_ln_v_b"], C_DIM // N_CHIPS),
]


def _small_update(own, landed, weights):
    names = list(own.keys())
    n_g, n_w = len(names), len(SMALL_WEIGHTS)

    def body(*refs):
        refs = list(refs)
        own_refs = dict(zip(names, refs[:n_g]))
        land_refs = dict(zip(names, refs[n_g:2 * n_g]))
        wmv = [refs[2 * n_g + 3 * i:2 * n_g + 3 * i + 3] for i in range(n_w)]
        o0 = 2 * n_g + 3 * n_w
        loss_ref = refs[o0]
        outs = [refs[o0 + 1 + 4 * i:o0 + 5 + 4 * i] for i in range(n_w)]
        acc = dict(zip(names, refs[o0 + 1 + 4 * n_w:]))
        x, y, c = _mesh_pos()
        mine, chip = 4 * x + 2 * y + c, 2 * x + y

        for nm in names:
            for d in range(N_DEV):
                def add(term, nm=nm, d=d):
                    acc[nm][...] = term if d == 0 else acc[nm][...] + term
                pl.when(mine == d)(lambda nm=nm, add=add: add(own_refs[nm][...]))
                pl.when(mine != d)(lambda nm=nm, d=d, add=add: add(land_refs[nm][d]))
        loss_ref[...] = acc["loss"][...]

        def update(i, rows, g):
            w_ref, m_ref, v_ref = wmv[i]
            delta, m_new, v_new = _adam_math(w_ref[rows], m_ref[rows], v_ref[rows], g)
            for ref, val in zip(outs[i], (g, delta, m_new, v_new)):
                ref[rows] = val

        for i, (_, shape, grads, per_chip) in enumerate(SMALL_WEIGHTS):
            for row, gname in enumerate(grads):
                per_grad = shape[0] // len(grads)
                rows = slice(row * per_grad, (row + 1) * per_grad)
                if per_chip is None:
                    update(i, rows, acc[gname][...])
                else:
                    for q in range(N_CHIPS):
                        pl.when(chip == q)(lambda i=i, rows=rows, gname=gname, q=q, per_chip=per_chip:
                                           update(i, rows, acc[gname][:, q * per_chip:(q + 1) * per_chip]))

    operands = [own[nm] for nm in names] + [landed[nm] for nm in names]
    for nm, _, _, _ in SMALL_WEIGHTS:
        operands += list(weights[nm])
    out_shape = [jax.ShapeDtypeStruct((1, 1), F32)]
    for _, shape, _, _ in SMALL_WEIGHTS:
        out_shape += [jax.ShapeDtypeStruct(shape, F32)] * 4
    res = pl.pallas_call(
        body, name="small_update", grid=(1,),
        in_specs=[_full_spec(a.shape) for a in operands], out_specs=[_full_spec(s.shape) for s in out_shape],
        out_shape=out_shape, scratch_shapes=[pltpu.VMEM(own[nm].shape, F32) for nm in names],
        compiler_params=_params(32, 1),
    )(*[_in_hbm(a) for a in operands])
    return res[0], {nm: res[1 + 4 * i:5 + 4 * i] for i, (nm, _, _, _) in enumerate(SMALL_WEIGHTS)}


def _adamw(w, m, v, grads, name, riders=()):
    layers, r, cdim = w.shape
    br = _block_rows(r, 256 if cdim > LANES else 1024)

    def body(*refs):
        w_ref, m_ref, v_ref = refs[:3]
        g_refs = refs[3:3 + layers]
        go_ref, d_ref, mo_ref, vo_ref = refs[3 + layers:]
        layer = pl.program_id(0)
        for l in range(layers):
            @pl.when(layer == l)
            def _(l=l):
                g = g_refs[l][...]
                go_ref[...] = g
                d_ref[...], mo_ref[...], vo_ref[...] = _adam_math(w_ref[...], m_ref[...], v_ref[...], g)

    spec3 = pl.BlockSpec((None, br, cdim), lambda l, i: (l, i, 0))
    spec2 = pl.BlockSpec((br, cdim), lambda l, i: (i, 0))
    out = jax.ShapeDtypeStruct((layers, r, cdim), F32)
    return _pallas(body, [_in_hbm(a) for a in (w, m, v, *grads)], name=name, grid=(layers, r // br),
                   in_specs=[spec3, spec3, spec3] + [spec2] * layers, out_specs=[spec3] * 4, out_shape=[out] * 4,
                   vmem_mib=32, riders=riders)


def _fill_shifted(buf, rows):
    for b in range(1, SUBLANES):
        buf[b, 0:rows - SUBLANES, :] = buf[0, b:b + rows - SUBLANES, :]


def _window(buf, start, size):
    return buf[start % SUBLANES, start - start % SUBLANES:start - start % SUBLANES + size, :]


def _conv31(src, w_ref, r0, base, init):
    acc = init
    for k in range(A_CONV_WIDTH):
        acc = acc + w_ref[k:k + 1, :] * _window(src, base + k + r0, CONV_ROWS)
    return acc


def _fwd_even(x, norm_g, w_in, conv_a_w, conv_a_b, ln_g, ln_b, conv_b_w, w_out, *, tm, seq, riders=()):
    tokens = x.shape[0]
    nt, tps = tokens // tm, seq // tm

    def body(x_ref, g_ref, win_hbm, caw_ref, cab_ref, lng_ref, lnb_ref, cbw_ref, wout_hbm,
             h_ref, n_ref, z_ref, a2_ref, cv_ref, mix_ref, win_v, wout_v, pa, pb, sem):
        i = pl.program_id(0)

        _load_weights([(win_hbm, win_v, False), (wout_hbm, wout_v, True)], sem)

        xv = x_ref[...]
        nf, _ = _rms_fwd(xv, g_ref[...])
        n = nf.astype(BF16)
        n_ref[...] = n
        z = jnp.concatenate([_dot(n, win_v[j]) for j in range(N_CHIPS)], axis=1)
        z_ref[...] = z.astype(BF16)
        a_val, a_gate = z[:, 0:A_DIM], z[:, A_DIM:2 * A_DIM]
        b_gate, c_gate, b_val = z[:, 1024:1536], z[:, 1536:2048], z[:, 2048:2560]

        first = (i % tps) == 0

        @pl.when(first)
        def _():
            pa[0, 0:A_HALO, :] = jnp.zeros((A_HALO, A_DIM), F32)
            pb[0:B_HALO, :] = jnp.zeros((B_HALO, B_DIM), F32)

        @pl.when(jnp.logical_not(first))
        def _():
            pa[0, 0:A_HALO, :] = pa[0, tm:tm + A_HALO, :]
            pb[0:B_HALO, :] = pb[tm:tm + B_HALO, :]

        pa[0, A_HALO:A_HALO + tm, :] = a_val * jax.nn.sigmoid(a_gate)
        pb[B_HALO:B_HALO + tm, :] = c_gate * b_val
        _fill_shifted(pa, A_HALO + tm)
        bias = jnp.broadcast_to(cab_ref[...], (CONV_ROWS, A_DIM))
        for r0 in range(0, tm, CONV_ROWS):
            a2_ref[r0:r0 + CONV_ROWS, :] = _conv31(pa, caw_ref, r0, A_HALO - (A_CONV_WIDTH - 1), bias)
        xhat, _ = _ln_stats(a2_ref[...])
        a3 = xhat * lng_ref[...] + lnb_ref[...]
        a4 = a3 * jax.nn.sigmoid(a3)
        cv = cbw_ref[0:1, :] * pb[B_HALO - 2:B_HALO - 2 + tm, :]
        cv = cv + cbw_ref[1:2, :] * pb[B_HALO - 1:B_HALO - 1 + tm, :]
        cv = cv + cbw_ref[2:3, :] * pb[B_HALO:B_HALO + tm, :]
        cv_ref[...] = cv.astype(BF16)
        mix = jnp.concatenate([a4, b_gate * cv], axis=1).astype(BF16)
        mix_ref[...] = mix
        h_ref[...] = xv + _dot(mix, wout_v[...])

    shp = lambda cols, dt: jax.ShapeDtypeStruct((tokens, cols), dt)
    return _pallas(
        body, [x, norm_g, w_in, conv_a_w, conv_a_b, ln_g, ln_b, conv_b_w, w_out], name="fwd_even", grid=(nt,),
        in_specs=[_row_spec(tm, D_MODEL), _full_spec((1, D_MODEL)), ANY, _full_spec((A_CONV_WIDTH, A_DIM)),
                  _full_spec((1, A_DIM)), _full_spec((1, A_DIM)), _full_spec((1, A_DIM)),
                  _full_spec((B_CONV_WIDTH, B_DIM)), ANY],
        out_specs=[_row_spec(tm, D_MODEL), _row_spec(tm, D_MODEL), _row_spec(tm, IN_EVEN), _row_spec(tm, A_DIM),
                   _row_spec(tm, B_DIM), _row_spec(tm, D_MODEL)],
        out_shape=[shp(D_MODEL, F32), shp(D_MODEL, BF16), shp(IN_EVEN, BF16), shp(A_DIM, F32), shp(B_DIM, BF16),
                   shp(D_MODEL, BF16)],
        scratch_shapes=[pltpu.VMEM((N_CHIPS, D_MODEL, IN_EVEN // N_CHIPS), BF16), pltpu.VMEM((D_MODEL, D_MODEL), BF16),
                        pltpu.VMEM((SUBLANES, A_HALO + tm, A_DIM), F32), pltpu.VMEM((B_HALO + tm, B_DIM), F32),
                        pltpu.SemaphoreType.DMA((N_LOADS,))],
        vmem_mib=56, riders=riders)


def _loss_tail(xv, g, target, loss_ref, dh_ref, dhb_ref, dg_ref):
    @pl.when(pl.program_id(0) == 0)
    def _():
        loss_ref[...] = jnp.zeros((1, 1), F32)
        dg_ref[...] = jnp.zeros((1, D_MODEL), F32)

    out, rstd = _rms_fwd(xv, g)
    err = out - target
    per_token = jnp.sum(err * err, axis=1, keepdims=True) * (1.0 / D_MODEL)
    loss_ref[...] += 0.5 * jnp.sum(per_token, axis=0, keepdims=True)
    dx, dg = _rms_bwd(err * (1.0 / D_MODEL), xv, rstd, g)
    dh_ref[...] = dx
    dhb_ref[...] = dx.astype(BF16)
    dg_ref[...] += dg


def _fwd_mlp(h, norm_g, w1, w2, layer, *, tm, riders=(), head=None):
    tokens = h.shape[0]
    nt = tokens // tm
    fs = D_FF // N_CHIPS
    n_in = 4 if head is None else 6

    def body(*refs):
        h_ref, g_ref, w1_hbm, w2_hbm = refs[:4]
        w1_v, w2_v, sem = refs[-3:]
        outs = refs[n_in:-3]
        n_ref, p_ref, q_ref = outs[1:4] if head is None else outs[0:3]
        _load_weights([(w1_hbm, w1_v, False), (w2_hbm, w2_v, False)], sem)

        xv = h_ref[...]
        nf, _ = _rms_fwd(xv, g_ref[...])
        n = nf.astype(BF16)
        n_ref[...] = n
        acc = xv
        for j in range(N_CHIPS):
            p = _dot(n, w1_v[j])
            p_ref[:, j * fs:(j + 1) * fs] = p.astype(BF16)
            r = jnp.maximum(p, 0.0)
            q = (r * r).astype(BF16)
            q_ref[:, j * fs:(j + 1) * fs] = q
            acc = acc + _dot(q, w2_v[j])
        if head is None:
            outs[0][...] = acc
        else:
            _loss_tail(acc, refs[4][...], refs[5][...], *outs[3:7])

    shp = lambda cols, dt: jax.ShapeDtypeStruct((tokens, cols), dt)
    saved_specs = [_row_spec(tm, D_MODEL), _row_spec(tm, D_FF), _row_spec(tm, D_FF)]
    saved_shapes = [shp(D_MODEL, BF16), shp(D_FF, BF16), shp(D_FF, BF16)]
    if head is None:
        operands, in_specs = [h, norm_g, w1, w2], [_row_spec(tm, D_MODEL), _full_spec((1, D_MODEL)), ANY, ANY]
        out_specs, out_shape = [_row_spec(tm, D_MODEL)] + saved_specs, [shp(D_MODEL, F32)] + saved_shapes
    else:
        operands = [h, norm_g, w1, w2, *head]
        in_specs = [_row_spec(tm, D_MODEL), _full_spec((1, D_MODEL)), ANY, ANY, _full_spec((1, D_MODEL)), _row_spec(tm, D_MODEL)]
        out_specs = saved_specs + [_full_spec((1, 1)), _row_spec(tm, D_MODEL), _row_spec(tm, D_MODEL), _full_spec((1, D_MODEL))]
        out_shape = saved_shapes + [jax.ShapeDtypeStruct((1, 1), F32), shp(D_MODEL, F32), shp(D_MODEL, BF16),
                                    jax.ShapeDtypeStruct((1, D_MODEL), F32)]
    return _pallas(
        body, operands, name=f"fwd_mlp{layer}", grid=(nt,), in_specs=in_specs, out_specs=out_specs, out_shape=out_shape,
        scratch_shapes=[pltpu.VMEM((N_CHIPS, D_MODEL, fs), BF16), pltpu.VMEM((N_CHIPS, fs, D_MODEL), BF16),
                        pltpu.SemaphoreType.DMA((N_LOADS,))],
        vmem_mib=56, riders=riders)


def _tril_mask():
    row = lax.broadcasted_iota(jnp.int32, (CHUNK, CHUNK), 0)
    col = lax.broadcasted_iota(jnp.int32, (CHUNK, CHUNK), 1)
    return row >= col


def _triu_mask():
    row = lax.broadcasted_iota(jnp.int32, (CHUNK, CHUNK), 0)
    col = lax.broadcasted_iota(jnp.int32, (CHUNK, CHUNK), 1)
    return row <= col


def _fwd_odd(h, norm_g, w_in, b_in, ln_g, ln_b, w_s, b_s_rows, w_out, *, tm, riders=()):
    tokens = h.shape[0]
    nt = tokens // tm
    cs = 2 * C_DIM // N_CHIPS

    def body(h_ref, g_ref, win_hbm, bin_ref, lng_ref, lnb_ref, ws_ref, bs_ref, wout_hbm,
             ho_ref, n_ref, s_ref, cdf_ref, sv_ref, y_ref, win_v, wout_v, bd, sem):
        _load_weights([(win_hbm, win_v, False), (wout_hbm, wout_v, True)], sem)

        @pl.when(pl.program_id(0) == 0)
        def _():
            mask = _tril_mask()
            bd[...] = jnp.zeros(bd.shape, BF16)
            for g in range(C_GROUPS):
                w = jnp.where(mask, ws_ref[g], 0.0).astype(BF16)
                bd[g, 0:CHUNK, 0:CHUNK] = w
                bd[g, CHUNK:PAIR, CHUNK:PAIR] = w

        xv = h_ref[...]
        nf, _ = _rms_fwd(xv, g_ref[...])
        n = nf.astype(BF16)
        n_ref[...] = n
        s = jnp.concatenate([_dot(n, win_v[j]) for j in range(N_CHIPS)], axis=1) + bin_ref[...]
        s_ref[...] = s.astype(BF16)
        cdf = _gelu_cdf(s)
        cdf_ref[...] = cdf.astype(BF16)
        zz = s * cdf
        u, v = zz[:, 0:C_DIM], zz[:, C_DIM:2 * C_DIM]
        xhat, _ = _ln_stats(v)
        vn = (xhat * lng_ref[...] + lnb_ref[...]).astype(BF16)
        for g in range(C_GROUPS):
            cols = slice(g * CHUNK, (g + 1) * CHUNK)
            bias = jnp.concatenate([bs_ref[g], bs_ref[g]], axis=0)
            for r0 in range(0, tm, PAIR):
                sv = _dot(bd[g], vn[r0:r0 + PAIR, cols]) + bias
                sv_ref[r0:r0 + PAIR, cols] = sv.astype(BF16)
                y_ref[r0:r0 + PAIR, cols] = (u[r0:r0 + PAIR, cols] * sv).astype(BF16)
        ho_ref[...] = xv + _dot(y_ref[...], wout_v[...])

    shp = lambda cols, dt: jax.ShapeDtypeStruct((tokens, cols), dt)
    return _pallas(
        body, [h, norm_g, w_in, b_in, ln_g, ln_b, w_s, b_s_rows, w_out], name="fwd_odd", grid=(nt,),
        in_specs=[_row_spec(tm, D_MODEL), _full_spec((1, D_MODEL)), ANY, _full_spec((1, 2 * C_DIM)),
                  _full_spec((1, C_DIM)), _full_spec((1, C_DIM)), _full_spec((C_GROUPS, CHUNK, CHUNK)),
                  _full_spec((C_GROUPS, CHUNK, CHUNK)), ANY],
        out_specs=[_row_spec(tm, D_MODEL), _row_spec(tm, D_MODEL), _row_spec(tm, 2 * C_DIM), _row_spec(tm, 2 * C_DIM),
                   _row_spec(tm, C_DIM), _row_spec(tm, C_DIM)],
        out_shape=[shp(D_MODEL, F32), shp(D_MODEL, BF16), shp(2 * C_DIM, BF16), shp(2 * C_DIM, BF16), shp(C_DIM, BF16),
                   shp(C_DIM, BF16)],
        scratch_shapes=[pltpu.VMEM((N_CHIPS, D_MODEL, cs), BF16), pltpu.VMEM((C_DIM, D_MODEL), BF16),
                        pltpu.VMEM((C_GROUPS, PAIR, PAIR), BF16), pltpu.SemaphoreType.DMA((N_LOADS,))],
        vmem_mib=56, riders=riders)


def _bwd_mlp(dh, h, norm_g, p, w1, w2, layer, *, tm, riders=()):
    tokens = h.shape[0]
    nt = tokens // tm
    fs = D_FF // N_CHIPS

    def body(dh_ref, h_ref, g_ref, p_ref, w1_hbm, w2_hbm, dx_ref, dxb_ref, dp_ref, dg_ref, w1_v, w2_v, sem):
        @pl.when(pl.program_id(0) == 0)
        def _():
            dg_ref[...] = jnp.zeros((1, D_MODEL), F32)

        _load_weights([(w1_hbm, w1_v, False), (w2_hbm, w2_v, False)], sem)

        dhv = dh_ref[...]
        dhb = dhv.astype(BF16)
        dn = jnp.zeros((tm, D_MODEL), F32)
        for j in range(N_CHIPS):
            dq = _dot_nt(dhb, w2_v[j])
            r = jnp.maximum(p_ref[:, j * fs:(j + 1) * fs].astype(F32), 0.0)
            dp = ((2.0 * r) * dq).astype(BF16)
            dp_ref[:, j * fs:(j + 1) * fs] = dp
            dn = dn + _dot_nt(dp, w1_v[j])
        xv = h_ref[...]
        g = g_ref[...]
        _, rstd = _rms_fwd(xv, g)
        dx, dg = _rms_bwd(dn, xv, rstd, g)
        dx_ref[...] = dhv + dx
        dxb_ref[...] = (dhv + dx).astype(BF16)
        dg_ref[...] += dg

    return _pallas(
        body, [dh, h, norm_g, p, w1, w2], name=f"bwd_mlp{layer}", grid=(nt,),
        in_specs=[_row_spec(tm, D_MODEL), _row_spec(tm, D_MODEL), _full_spec((1, D_MODEL)), _row_spec(tm, D_FF), ANY, ANY],
        out_specs=[_row_spec(tm, D_MODEL), _row_spec(tm, D_MODEL), _row_spec(tm, D_FF), _full_spec((1, D_MODEL))],
        out_shape=[jax.ShapeDtypeStruct((tokens, D_MODEL), F32), jax.ShapeDtypeStruct((tokens, D_MODEL), BF16),
                   jax.ShapeDtypeStruct((tokens, D_FF), BF16), jax.ShapeDtypeStruct((1, D_MODEL), F32)],
        scratch_shapes=[pltpu.VMEM((N_CHIPS, D_MODEL, fs), BF16), pltpu.VMEM((N_CHIPS, fs, D_MODEL), BF16),
                        pltpu.SemaphoreType.DMA((N_LOADS,))],
        vmem_mib=56, riders=riders)


def _bwd_odd(dh, h, norm_g, s, cdf, sv, w_in, ln_g, ln_b, w_s, w_out, *, tm, riders=()):
    tokens = h.shape[0]
    nt = tokens // tm
    cs = 2 * C_DIM // N_CHIPS

    def body(dh_ref, h_ref, g_ref, s_ref, cdf_ref, sv_ref, win_hbm, lng_ref, lnb_ref, ws_ref, wout_hbm,
             dx_ref, dxb_ref, ds_ref, dg_ref, dbin_ref, dlng_ref, dlnb_ref, dws_ref, dbs_ref,
             win_v, wout_v, bdt, dws_acc, dbs_acc, dvn, sem):
        i = pl.program_id(0)

        _load_weights([(win_hbm, win_v, False), (wout_hbm, wout_v, True)], sem)

        @pl.when(i == 0)
        def _():
            mask_t = _triu_mask()
            bdt[...] = jnp.zeros(bdt.shape, BF16)
            for g in range(C_GROUPS):
                wt = jnp.where(mask_t, ws_ref[g].T, 0.0).astype(BF16)
                bdt[g, 0:CHUNK, 0:CHUNK] = wt
                bdt[g, CHUNK:PAIR, CHUNK:PAIR] = wt
            dws_acc[...] = jnp.zeros(dws_acc.shape, F32)
            dbs_acc[...] = jnp.zeros(dbs_acc.shape, F32)
            dg_ref[...] = jnp.zeros(dg_ref.shape, F32)
            dbin_ref[...] = jnp.zeros(dbin_ref.shape, F32)
            dlng_ref[...] = jnp.zeros(dlng_ref.shape, F32)
            dlnb_ref[...] = jnp.zeros(dlnb_ref.shape, F32)

        dhv = dh_ref[...]
        dy = _dot_nt(dhv.astype(BF16), wout_v[...])
        sf = s_ref[...].astype(F32)
        cdf = cdf_ref[...].astype(F32)
        pdf = jnp.exp(-0.5 * sf * sf) * 0.3989422804014327
        zz = sf * cdf
        dgelu = cdf + sf * pdf
        u, v = zz[:, 0:C_DIM], zz[:, C_DIM:2 * C_DIM]
        xhat, rs = _ln_stats(v)
        lng = lng_ref[...]
        vn = (xhat * lng + lnb_ref[...]).astype(BF16)
        du = dy * sv_ref[...].astype(F32)
        dsv = dy * u
        dsvb = dsv.astype(BF16)
        for g in range(C_GROUPS):
            cols = slice(g * CHUNK, (g + 1) * CHUNK)
            for r0 in range(0, tm, PAIR):
                blk = dsvb[r0:r0 + PAIR, cols]
                dvn[r0:r0 + PAIR, cols] = _dot(bdt[g], blk)
                dws_acc[g] += _dot_nt(blk, vn[r0:r0 + PAIR, cols])
                dbs_acc[g] += dsv[r0:r0 + CHUNK, cols] + dsv[r0 + CHUNK:r0 + PAIR, cols]
        dv, dlng, dlnb = _ln_bwd(dvn[...], xhat, rs, lng)
        dlng_ref[...] += dlng
        dlnb_ref[...] += dlnb
        ds = jnp.concatenate([du, dv], axis=1) * dgelu
        dbin_ref[...] += jnp.sum(ds, axis=0, keepdims=True)
        dsb = ds.astype(BF16)
        ds_ref[...] = dsb
        dn = jnp.zeros((tm, D_MODEL), F32)
        for j in range(N_CHIPS):
            dn = dn + _dot_nt(dsb[:, j * cs:(j + 1) * cs], win_v[j])
        xv = h_ref[...]
        g = g_ref[...]
        _, rstd = _rms_fwd(xv, g)
        dx, dg = _rms_bwd(dn, xv, rstd, g)
        dx_ref[...] = dhv + dx
        dxb_ref[...] = (dhv + dx).astype(BF16)
        dg_ref[...] += dg

        @pl.when(i == nt - 1)
        def _():
            mask = _tril_mask()
            for g in range(C_GROUPS):
                full = dws_acc[g]
                dws_ref[g] = jnp.where(mask, full[0:CHUNK, 0:CHUNK] + full[CHUNK:PAIR, CHUNK:PAIR], 0.0)
                dbs_ref[g:g + 1, :] = jnp.sum(dbs_acc[g].T, axis=0, keepdims=True)

    row = lambda cols: jax.ShapeDtypeStruct((1, cols), F32)
    return _pallas(
        body, [dh, h, norm_g, s, cdf, sv, w_in, ln_g, ln_b, w_s, w_out], name="bwd_odd", grid=(nt,),
        in_specs=[_row_spec(tm, D_MODEL), _row_spec(tm, D_MODEL), _full_spec((1, D_MODEL)), _row_spec(tm, 2 * C_DIM),
                  _row_spec(tm, 2 * C_DIM), _row_spec(tm, C_DIM), ANY, _full_spec((1, C_DIM)), _full_spec((1, C_DIM)),
                  _full_spec((C_GROUPS, CHUNK, CHUNK)), ANY],
        out_specs=[_row_spec(tm, D_MODEL), _row_spec(tm, D_MODEL), _row_spec(tm, 2 * C_DIM), _full_spec((1, D_MODEL)),
                   _full_spec((1, 2 * C_DIM)),
                   _full_spec((1, C_DIM)), _full_spec((1, C_DIM)), _full_spec((C_GROUPS, CHUNK, CHUNK)),
                   _full_spec((C_GROUPS, CHUNK))],
        out_shape=[jax.ShapeDtypeStruct((tokens, D_MODEL), F32), jax.ShapeDtypeStruct((tokens, D_MODEL), BF16),
                   jax.ShapeDtypeStruct((tokens, 2 * C_DIM), BF16),
                   row(D_MODEL), row(2 * C_DIM), row(C_DIM), row(C_DIM),
                   jax.ShapeDtypeStruct((C_GROUPS, CHUNK, CHUNK), F32), jax.ShapeDtypeStruct((C_GROUPS, CHUNK), F32)],
        scratch_shapes=[pltpu.VMEM((N_CHIPS, D_MODEL, cs), BF16), pltpu.VMEM((C_DIM, D_MODEL), BF16),
                        pltpu.VMEM((C_GROUPS, PAIR, PAIR), BF16), pltpu.VMEM((C_GROUPS, PAIR, PAIR), F32),
                        pltpu.VMEM((C_GROUPS, CHUNK, CHUNK), F32), pltpu.VMEM((tm, C_DIM), F32),
                        pltpu.SemaphoreType.DMA((N_LOADS,))],
        vmem_mib=56, riders=riders)


def _bwd_even(dh, x, norm_g, z, a2, cv, w_in, conv_a_w, ln_g, ln_b, conv_b_w, w_out, *, tm, seq, riders=()):
    tokens = x.shape[0]
    nt, tps = tokens // tm, seq // tm
    ws = IN_EVEN // N_CHIPS

    def body(dh_ref, x_ref, g_ref, z_ref, a2_ref, cv_ref, win_hbm, caw_ref, lng_ref, lnb_ref, cbw_ref, wout_hbm,
             dx_ref, dz_ref, dg_ref, dcaw_ref, dcab_ref, dlng_ref, dlnb_ref, dcbw_ref,
             win_v, wout_v, ea, eb, a1s, da1s, sigs, wide, dw_acc, sem):
        i = pl.program_id(0)

        _load_weights([(win_hbm, win_v, False), (wout_hbm, wout_v, True)], sem)

        @pl.when(i == 0)
        def _():
            dw_acc[...] = jnp.zeros(dw_acc.shape, F32)
            for ref in (dg_ref, dcab_ref, dlng_ref, dlnb_ref, dcbw_ref):
                ref[...] = jnp.zeros(ref.shape, F32)

        last = ((nt - 1 - i) % tps) == tps - 1

        @pl.when(last)
        def _():
            ea[0, tm:tm + A_HALO, :] = jnp.zeros((A_HALO, A_DIM), F32)
            eb[tm:tm + B_HALO, :] = jnp.zeros((B_HALO, B_DIM), F32)

        @pl.when(jnp.logical_not(last))
        def _():
            ea[0, tm:tm + A_HALO, :] = ea[0, 0:A_HALO, :]
            eb[tm:tm + B_HALO, :] = eb[0:B_HALO, :]

        wide[...] = _dot_nt(dh_ref[...].astype(BF16), wout_v[...])
        lng, lnb = lng_ref[...], lnb_ref[...]
        zero_row = jnp.zeros((1, A_DIM), F32)
        dlng, dlnb, dcab = zero_row, zero_row, zero_row
        for r0 in range(0, tm, ELEM_ROWS):
            rows = slice(r0, r0 + ELEM_ROWS)
            a_val, a_gate = z_ref[rows, 0:A_DIM].astype(F32), z_ref[rows, A_DIM:2 * A_DIM].astype(F32)
            xhat, rs = _ln_stats(a2_ref[rows, :])
            a3 = xhat * lng + lnb
            sg = jax.nn.sigmoid(a3)
            da3 = wide[rows, 0:A_DIM] * (sg * (1.0 + a3 * (1.0 - sg)))
            da2, g_part, b_part = _ln_bwd(da3, xhat, rs, lng)
            dlng, dlnb, dcab = dlng + g_part, dlnb + b_part, dcab + jnp.sum(da2, axis=0, keepdims=True)
            ea[0, rows, :] = da2
            eb[rows, :] = wide[rows, A_DIM:A_DIM + B_DIM] * z_ref[rows, 1024:1536].astype(F32)
            sig = jax.nn.sigmoid(a_gate)
            sigs[rows, :] = sig
            a1s[rows, :] = a_val * sig
        dlng_ref[...] += dlng
        dlnb_ref[...] += dlnb
        dcab_ref[...] += dcab
        _fill_shifted(ea, tm + A_HALO)
        for r0 in range(0, tm, CONV_ROWS):
            acc = jnp.zeros((CONV_ROWS, A_DIM), F32)
            for j in range(A_CONV_WIDTH):
                acc = acc + caw_ref[A_CONV_WIDTH - 1 - j:A_CONV_WIDTH - j, :] * _window(ea, r0 + j, CONV_ROWS)
            da1s[r0:r0 + CONV_ROWS, :] = acc
        for j0 in range(0, A_CONV_WIDTH, DW_TAPS):
            taps = range(j0, min(j0 + DW_TAPS, A_CONV_WIDTH))
            part = [jnp.zeros((CONV_ROWS, A_DIM), F32) for _ in taps]
            for r0 in range(0, tm, CONV_ROWS):
                a1c = a1s[r0:r0 + CONV_ROWS, :]
                for u, j in enumerate(taps):
                    part[u] = part[u] + _window(ea, r0 + j, CONV_ROWS) * a1c
            for u, j in enumerate(taps):
                dw_acc[A_CONV_WIDTH - 1 - j] += part[u]
        dcbw = [jnp.zeros((1, B_DIM), F32) for _ in range(B_CONV_WIDTH)]
        for r0 in range(0, tm, ELEM_ROWS):
            rows = slice(r0, r0 + ELEM_ROWS)
            da1, sig = da1s[rows, :], sigs[rows, :]
            dz_ref[rows, 0:A_DIM] = (da1 * sig).astype(BF16)
            dz_ref[rows, A_DIM:2 * A_DIM] = (da1 * z_ref[rows, 0:A_DIM].astype(F32) * (sig * (1.0 - sig))).astype(BF16)
            c_gate, b_val = z_ref[rows, 1536:2048].astype(F32), z_ref[rows, 2048:2560].astype(F32)
            dz_ref[rows, 1024:1536] = (wide[rows, A_DIM:A_DIM + B_DIM] * cv_ref[rows, :].astype(F32)).astype(BF16)
            cb = c_gate * b_val
            dcb = jnp.zeros((ELEM_ROWS, B_DIM), F32)
            for j in range(B_CONV_WIDTH):
                k = B_CONV_WIDTH - 1 - j
                sl = eb[r0 + j:r0 + j + ELEM_ROWS, :]
                dcb = dcb + cbw_ref[k:k + 1, :] * sl
                dcbw[k] = dcbw[k] + jnp.sum(sl * cb, axis=0, keepdims=True)
            dz_ref[rows, 1536:2048] = (dcb * b_val).astype(BF16)
            dz_ref[rows, 2048:2560] = (dcb * c_gate).astype(BF16)
        for k in range(B_CONV_WIDTH):
            dcbw_ref[k:k + 1, :] += dcbw[k]
        dn = jnp.zeros((tm, D_MODEL), F32)
        for j in range(N_CHIPS):
            dn = dn + _dot_nt(dz_ref[:, j * ws:(j + 1) * ws], win_v[j])
        wide[...] = dn
        g = g_ref[...]
        dg = jnp.zeros((1, D_MODEL), F32)
        for r0 in range(0, tm, ELEM_ROWS):
            rows = slice(r0, r0 + ELEM_ROWS)
            xv = x_ref[rows, :]
            _, rstd = _rms_fwd(xv, g)
            dx, dg_part = _rms_bwd(wide[rows, :], xv, rstd, g)
            dx_ref[rows, :] = dh_ref[rows, :] + dx
            dg = dg + dg_part
        dg_ref[...] += dg

        @pl.when(i == nt - 1)
        def _():
            for k in range(A_CONV_WIDTH):
                dcaw_ref[k:k + 1, :] = jnp.sum(dw_acc[k], axis=0, keepdims=True)

    row = lambda cols: jax.ShapeDtypeStruct((1, cols), F32)
    rs_ = functools.partial(_row_spec, rev_nt=nt)
    return _pallas(
        body, [dh, x, norm_g, z, a2, cv, w_in, conv_a_w, ln_g, ln_b, conv_b_w, w_out], name="bwd_even", grid=(nt,),
        in_specs=[rs_(tm, D_MODEL), rs_(tm, D_MODEL), _full_spec((1, D_MODEL)), rs_(tm, IN_EVEN), rs_(tm, A_DIM),
                  rs_(tm, B_DIM), ANY, _full_spec((A_CONV_WIDTH, A_DIM)), _full_spec((1, A_DIM)), _full_spec((1, A_DIM)),
                  _full_spec((B_CONV_WIDTH, B_DIM)), ANY],
        out_specs=[rs_(tm, D_MODEL), rs_(tm, IN_EVEN), _full_spec((1, D_MODEL)), _full_spec((A_CONV_WIDTH, A_DIM)),
                   _full_spec((1, A_DIM)), _full_spec((1, A_DIM)), _full_spec((1, A_DIM)), _full_spec((B_CONV_WIDTH, B_DIM))],
        out_shape=[jax.ShapeDtypeStruct((tokens, D_MODEL), F32), jax.ShapeDtypeStruct((tokens, IN_EVEN), BF16),
                   row(D_MODEL), jax.ShapeDtypeStruct((A_CONV_WIDTH, A_DIM), F32), row(A_DIM), row(A_DIM), row(A_DIM),
                   jax.ShapeDtypeStruct((B_CONV_WIDTH, B_DIM), F32)],
        scratch_shapes=[pltpu.VMEM((N_CHIPS, D_MODEL, ws), BF16), pltpu.VMEM((D_MODEL, D_MODEL), BF16),
                        pltpu.VMEM((SUBLANES, tm + A_HALO, A_DIM), F32), pltpu.VMEM((tm + B_HALO, B_DIM), F32),
                        pltpu.VMEM((tm, A_DIM), F32), pltpu.VMEM((tm, A_DIM), F32), pltpu.VMEM((tm, A_DIM), F32),
                        pltpu.VMEM((tm, D_MODEL), F32),
                        pltpu.VMEM((A_CONV_WIDTH, CONV_ROWS, A_DIM), F32), pltpu.SemaphoreType.DMA((N_LOADS,))],
        vmem_mib=56, riders=riders)


def _wgrad(a, b, name, *, col_shards, riders=()):
    tokens, m = a.shape
    n = b.shape[1]
    kc = 512
    if col_shards:
        bm, bn = m // 2, n // N_CHIPS
        grid = (2, N_CHIPS)
        out_spec = pl.BlockSpec((None, None, bm, bn), lambda i, j: (j, i, 0, 0))
    elif m // 8 >= MXU_ROWS:
        bm, bn = m // 8, n
        grid = (8, 1)
        out_spec = pl.BlockSpec((None, None, bm, bn), lambda i, j: (i // 2, i % 2, 0, 0))
    else:
        bm, bn = m // N_CHIPS, n
        grid = (N_CHIPS, 1)
        out_spec = pl.BlockSpec((None, 2, bm // 2, bn), lambda i, j: (i, 0, 0, 0))

    def body(a_ref, b_ref, o_ref):
        acc = jnp.zeros((bm, bn), F32)
        for k0 in range(0, tokens, kc):
            acc = acc + _dot_tn(a_ref[k0:k0 + kc, :].astype(BF16), b_ref[k0:k0 + kc, :].astype(BF16))
        if len(o_ref.shape) == 3:
            o_ref[0] = acc[0:bm // 2]
            o_ref[1] = acc[bm // 2:bm]
        else:
            o_ref[...] = acc

    out_rows = m // 2 if col_shards else m // 8
    outs, routs = _pallas(
        body, [a, b], name=name, grid=grid,
        in_specs=[pl.BlockSpec((tokens, bm), lambda i, j: (0, i)), pl.BlockSpec((tokens, bn), lambda i, j: (0, j))],
        out_specs=[out_spec], out_shape=[jax.ShapeDtypeStruct((N_CHIPS, 2, out_rows, bn), F32)],
        vmem_mib=56, riders=riders)
    return outs[0], routs


def _wgrad_pair(a, b, name, *, col_shards, riders=()):
    tokens, m = a.shape
    n = b.shape[1]
    kc = 512
    c0 = lax.axis_index("c")

    def half(ph, pre):
        return (ph + 1 + pre[0]) % 2

    if col_shards:
        bm, bn = m // 2, n // N_CHIPS
        a_spec = pl.BlockSpec((tokens, bm), lambda ph, q, pre: (0, half(ph, pre)))
        b_spec = pl.BlockSpec((tokens, bn), lambda ph, q, pre: (0, q))
    else:
        bm, bn = m // 8, n
        a_spec = pl.BlockSpec((tokens, bm), lambda ph, q, pre: (0, 2 * q + half(ph, pre)))
        b_spec = pl.BlockSpec((tokens, bn), lambda ph, q, pre: (0, 0))

    def body(pre_ref, a_ref, b_ref, o_ref, give, got, send_sems, recv_sems):
        ph, q = pl.program_id(0), pl.program_id(1)
        acc = jnp.zeros((bm, bn), F32)
        for k0 in range(0, tokens, kc):
            acc = acc + _dot_tn(a_ref[k0:k0 + kc, :].astype(BF16), b_ref[k0:k0 + kc, :].astype(BF16))
        x, y, cc = _mesh_pos()

        def tile(t):
            return _remote(give.at[t], got.at[t], send_sems.at[t], recv_sems.at[t], (x, y, 1 - cc))

        @pl.when(ph == 0)
        def _():
            give[q] = acc
            tile(q).start()

        @pl.when(ph == 1)
        def _():
            tile(q).wait_recv()
            o_ref[...] = (acc + got[q]).astype(BF16)

        @pl.when((ph == 1) & (q == N_CHIPS - 1))
        def _():
            for t in range(N_CHIPS):
                tile(t).wait_send()

    outs, routs = _pallas(
        body, [a, b], name=name, grid=(2, N_CHIPS), in_specs=[a_spec, b_spec],
        out_specs=[pl.BlockSpec((None, bm, bn), lambda ph, q, pre: (ph * q, 0, 0))],
        out_shape=[jax.ShapeDtypeStruct((N_CHIPS, bm, bn), BF16)],
        scratch_shapes=[pltpu.VMEM((N_CHIPS, bm, bn), F32), pltpu.VMEM((N_CHIPS, bm, bn), F32),
                        pltpu.SemaphoreType.DMA((N_CHIPS,)), pltpu.SemaphoreType.DMA((N_CHIPS,))],
        vmem_mib=56, riders=riders, prefetch=jnp.reshape(c0, (1,)).astype(jnp.int32))
    return outs[0], routs


class _GradReduce:
    def __init__(self, name, grad=None, chip_sum=None):
        self.name, self.grad, self.chip_sum = name, grad, chip_sum
        self.full = None

    def pair_swap(self):
        return _PairSwap([self.grad])

    def took_pair(self, outs):
        self.chip_sum = _in_hbm(_add_pair(self.grad, outs[0], f"pair_sum_{self.name}"))

    def took_chips(self, outs):
        self.full = _in_hbm(_add_chips(self.chip_sum, outs[0], f"chip_sum_{self.name}"))

    def chips_beside(self, collective_id):
        self.took_chips([_chip_swap_beside(self.chip_sum, f"chip_swap_{self.name}", collective_id)])

    def pair_share(self):
        return _PairShare([self.full])

    def took_share(self, outs):
        self.full = outs[0]

    def reduced(self):
        return jnp.reshape(self.full, (2 * self.full.shape[1], self.full.shape[2]))


def _forward_backward(x2, tgt2, w, conv_a_w, conv_b_w, od_norm, od_bias, od_lng, od_lnb,
                      ev_norm_g, ev_conv_a_b, ev_ln_a_g, ev_ln_a_b, od_w_s, od_b_s, mlp_norm_g, final_norm_g,
                      *, tm, seq, distributed=True):
    d = x2.shape[1]
    b_s_rows = jnp.broadcast_to(od_b_s[0][:, :, None], (C_GROUPS, CHUNK, CHUNK))
    (h1, n0, z, a2, cv, mix), _ = _fwd_even(
        x2, ev_norm_g, w["ev_in"], conv_a_w, ev_conv_a_b, ev_ln_a_g, ev_ln_a_b, conv_b_w, w["ev_out"], tm=tm, seq=seq)
    (h2, n1, p0, q0), _ = _fwd_mlp(h1, mlp_norm_g[0:1], w["w1_0"], w["w2_0"], 0, tm=tm)
    (h3, n2, s, cdf, sv, y), _ = _fwd_odd(h2, od_norm, w["od_in"], od_bias, od_lng, od_lnb, od_w_s[0], b_s_rows,
                                          w["od_out"], tm=tm)
    (n3, p1, q1, loss_part, dh4, dh4b, d_final_g), _ = _fwd_mlp(
        h3, mlp_norm_g[1:2], w["w1_1"], w["w2_1"], 1, tm=tm,
        head=(jnp.reshape(final_norm_g, (1, d)), tgt2))

    red = {}

    def swap(*names):
        return [red[nm].pair_swap() for nm in names] if distributed else []

    def share(*names):
        return [red[nm].pair_share() for nm in names] if distributed else []

    def took(routs, *steps):
        if distributed:
            for (nm, what), outs in zip(steps, routs):
                getattr(red[nm], what)(outs)

    swap_ids = iter(range(FIRST_SWAP_ID, FIRST_SWAP_ID + 8))

    def beside(name):
        if distributed:
            red[name].chips_beside(next(swap_ids))

    def big(lhs, rhs, name, col_shards, riders=()):
        if distributed:
            chip_sum, routs = _wgrad_pair(lhs, rhs, f"wgrad_{name}", col_shards=col_shards, riders=riders)
            red[name] = _GradReduce(name, chip_sum=_in_hbm(chip_sum))
        else:
            g, routs = _wgrad(lhs, rhs, f"wgrad_{name}", col_shards=col_shards)
            red[name] = _GradReduce(name, grad=g)
        return routs

    big(q1, dh4b, "w2_1", False)
    beside("w2_1")
    (dh3, dh3b, dp1, d_mlp_g1), _ = _bwd_mlp(dh4, h3, mlp_norm_g[1:2], p1, w["w1_1"], w["w2_1"], 1, tm=tm)
    big(n3, dp1, "w1_1", True)
    beside("w1_1")
    g, routs = _wgrad(y, dh3b, "wgrad_od_out", col_shards=False, riders=share("w2_1"))
    red["od_out"] = _GradReduce("od_out", grad=g)
    took(routs, ("w2_1", "took_share"))
    (dh2, dh2b, ds, d_od_norm, d_od_bin, d_od_lng, d_od_lnb, d_ws, d_bs), _ = _bwd_odd(
        dh3, h2, od_norm, s, cdf, sv, w["od_in"], od_lng, od_lnb, od_w_s[0], w["od_out"], tm=tm)
    routs = big(n2, ds, "od_in", True, riders=share("w1_1") + swap("od_out"))
    took(routs, ("w1_1", "took_share"), ("od_out", "took_pair"))
    beside("od_in")
    beside("od_out")
    half_groups = C_GROUPS // 2
    early = {"loss": loss_part, "od_w_s_lo": d_ws[:half_groups], "od_b_s": d_bs, "mlp_norm_g1": d_mlp_g1, "final_norm_g": d_final_g,
             "od_norm_g": d_od_norm, "od_b_in": d_od_bin, "od_ln_v_g": d_od_lng, "od_ln_v_b": d_od_lnb}
    share_early = [_ShareAll(list(early.values()))] if distributed else []
    routs = big(q0, dh2b, "w2_0", False, riders=share_early)
    landed_early = routs[0] if distributed else []
    beside("w2_0")
    (dh1, dh1b, dp0, d_mlp_g0), _ = _bwd_mlp(dh2, h1, mlp_norm_g[0:1], p0, w["w1_0"], w["w2_0"], 0, tm=tm)
    middle = {"od_w_s_hi": d_ws[half_groups:]}
    share_middle = [_ShareAll(list(middle.values()))] if distributed else []
    g, _ = _wgrad(mix, dh1b, "wgrad_ev_out", col_shards=False)
    red["ev_out"] = _GradReduce("ev_out", grad=g)
    routs = big(n1, dp0, "w1_0", True,
                riders=share("od_out") + share("od_in") + share("w2_0") + swap("ev_out") + share_middle)
    took(routs, ("od_out", "took_share"), ("od_in", "took_share"), ("w2_0", "took_share"), ("ev_out", "took_pair"))
    landed_middle = routs[4] if distributed else []
    beside("w1_0")
    beside("ev_out")

    (dx, dz, d_ev_norm, d_caw, d_cab, d_ev_lng, d_ev_lnb, d_cbw), _ = _bwd_even(
        dh1, x2, ev_norm_g, z, a2, cv, w["ev_in"], conv_a_w, ev_ln_a_g, ev_ln_a_b, conv_b_w, w["ev_out"], tm=tm, seq=seq)
    late = {"mlp_norm_g0": d_mlp_g0, "ev_norm_g": d_ev_norm, "ev_conv_a_b": d_cab, "ev_ln_a_g": d_ev_lng,
            "ev_ln_a_b": d_ev_lnb, "ev_conv_a_w": d_caw, "ev_conv_b_w": d_cbw}
    share_late = [_ShareAll(list(late.values()))] if distributed else []
    routs = big(n0, dz, "ev_in", True, riders=share("ev_out") + share("w1_0") + share_late)
    took(routs, ("ev_out", "took_share"), ("w1_0", "took_share"))
    beside("ev_in")
    own = {**early, **middle, **late}
    landed = dict(zip(own.keys(), landed_early + landed_middle + routs[2])) if distributed else None
    return dx, red, own, landed


def _rows128(a):
    rows = jnp.reshape(a, (-1, LANES))
    pad = (-rows.shape[0]) % SUBLANES
    return jnp.pad(rows, ((0, pad), (0, 0))) if pad else rows


def _pack(arrays):
    return jnp.concatenate([_rows128(a) for a in arrays], axis=0)


def _unpack(buf, shapes):
    out, r0 = [], 0
    for shp in shapes:
        size = 1
        for dim in shp:
            size *= dim
        nr = size // LANES
        out.append(jnp.reshape(buf[r0:r0 + nr], shp))
        r0 += nr + (-nr) % SUBLANES
    return out


def kernel(x, ev_norm_g, ev_w_in, ev_conv_a_w, ev_conv_a_b, ev_ln_a_g, ev_ln_a_b, ev_conv_b_w, ev_w_out, od_norm_g, od_w_in, od_b_in, od_ln_v_g, od_ln_v_b, od_w_s, od_b_s, od_w_out, mlp_norm_g, mlp_w1, mlp_w2, final_norm_g, loss_target, m_ev_norm_g, m_ev_w_in, m_ev_conv_a_w, m_ev_conv_a_b, m_ev_ln_a_g, m_ev_ln_a_b, m_ev_conv_b_w, m_ev_w_out, m_od_norm_g, m_od_w_in, m_od_b_in, m_od_ln_v_g, m_od_ln_v_b, m_od_w_s, m_od_b_s, m_od_w_out, m_mlp_norm_g, m_mlp_w1, m_mlp_w2, m_final_norm_g, v_ev_norm_g, v_ev_w_in, v_ev_conv_a_w, v_ev_conv_a_b, v_ev_ln_a_g, v_ev_ln_a_b, v_ev_conv_b_w, v_ev_w_out, v_od_norm_g, v_od_w_in, v_od_b_in, v_od_ln_v_g, v_od_ln_v_b, v_od_w_s, v_od_b_s, v_od_w_out, v_mlp_norm_g, v_mlp_w1, v_mlp_w2, v_final_norm_g):
    tm = TOKEN_TILE
    batch, seq, d = x.shape
    tokens = batch * seq
    x2 = jnp.reshape(x, (tokens, d))
    tgt2 = jnp.reshape(loss_target, (tokens, d))
    chip = 2 * lax.axis_index("x") + lax.axis_index("y")

    small_shapes = [(A_CONV_WIDTH, LANES), (B_CONV_WIDTH, LANES), (256,), (512,), (256,), (256,)]
    small_shard = _pack([ev_conv_a_w[0], ev_conv_b_w[0], od_norm_g[0], od_b_in[0], od_ln_v_g[0], od_ln_v_b[0]])
    small_shard = jnp.pad(small_shard, ((0, (-small_shard.shape[0]) % (4 * SUBLANES)), (0, 0)))
    first = [_place_shard(ev_w_in, 0, BF16, "place_ev_w_in"), _place_shard(ev_w_out, 0, BF16, "place_ev_w_out"),
             _place_shard(small_shard[None], 0, F32, "place_small")]
    staged = {
        "w1_0": _place_shard(mlp_w1, 0, BF16, "place_w1_0"), "w2_0": _place_shard(mlp_w2, 0, BF16, "place_w2_0"),
        "od_in": _place_shard(od_w_in, 0, BF16, "place_od_w_in"), "od_out": _place_shard(od_w_out, 0, BF16, "place_od_w_out"),
        "w1_1": _place_shard(mlp_w1, 1, BF16, "place_w1_1"), "w2_1": _place_shard(mlp_w2, 1, BF16, "place_w2_1"),
    }
    first = [_in_hbm(a) for a in first]
    staged = {nm: _in_hbm(a) for nm, a in staged.items()}
    g_ev_in, g_ev_out, g_small = _gather_beside(first, "gather_stage0", collective_id=1)
    gathered = {"ev_in": g_ev_in, "ev_out": g_ev_out}
    for stage, names in enumerate((("w1_0", "w2_0"), ("od_in", "od_out", "w1_1"), ("w2_1",))):
        done = _gather_beside([staged[nm] for nm in names], f"gather_stage{stage + 1}", collective_id=stage + 2)
        gathered.update(zip(names, done))
    small_all = jnp.reshape(_plain_copy(g_small, "small_weights_copy"), (N_CHIPS, -1, LANES))
    per_chip = [_unpack(small_all[q], small_shapes) for q in range(N_CHIPS)]
    conv_a_w = jnp.concatenate([pc[0] for pc in per_chip], axis=1)
    conv_b_w = jnp.concatenate([pc[1] for pc in per_chip], axis=1)
    od_norm = jnp.concatenate([pc[2] for pc in per_chip])[None, :]
    od_bias = jnp.concatenate([pc[3] for pc in per_chip])[None, :]
    od_lng = jnp.concatenate([pc[4] for pc in per_chip])[None, :]
    od_lnb = jnp.concatenate([pc[5] for pc in per_chip])[None, :]

    dx, red, own, landed = _forward_backward(
        x2, tgt2, gathered, conv_a_w, conv_b_w, od_norm, od_bias, od_lng, od_lnb,
        ev_norm_g, ev_conv_a_b, ev_ln_a_g, ev_ln_a_b, od_w_s, od_b_s, mlp_norm_g, final_norm_g, tm=tm, seq=seq)

    routs = _exchange([red["ev_in"].pair_share()], "reduce_tail")
    red["ev_in"].took_share(routs[0])

    given = {"ev_norm_g": (ev_norm_g, m_ev_norm_g, v_ev_norm_g), "ev_conv_a_b": (ev_conv_a_b, m_ev_conv_a_b, v_ev_conv_a_b),
             "ev_ln_a_g": (ev_ln_a_g, m_ev_ln_a_g, v_ev_ln_a_g), "ev_ln_a_b": (ev_ln_a_b, m_ev_ln_a_b, v_ev_ln_a_b),
             "od_w_s": (od_w_s, m_od_w_s, v_od_w_s), "od_b_s": (od_b_s, m_od_b_s, v_od_b_s),
             "mlp_norm_g": (mlp_norm_g, m_mlp_norm_g, v_mlp_norm_g), "final_norm_g": (final_norm_g, m_final_norm_g, v_final_norm_g),
             "ev_conv_a_w": (ev_conv_a_w, m_ev_conv_a_w, v_ev_conv_a_w), "ev_conv_b_w": (ev_conv_b_w, m_ev_conv_b_w, v_ev_conv_b_w),
             "od_norm_g": (od_norm_g, m_od_norm_g, v_od_norm_g), "od_b_in": (od_b_in, m_od_b_in, v_od_b_in),
             "od_ln_v_g": (od_ln_v_g, m_od_ln_v_g, v_od_ln_v_g), "od_ln_v_b": (od_ln_v_b, m_od_ln_v_b, v_od_ln_v_b)}
    shaped = {nm: tuple(jnp.reshape(a, shape) for a in given[nm]) for nm, shape, _, _ in SMALL_WEIGHTS}
    loss11, small_upd = _small_update(own, landed, shaped)
    loss = loss11[0, 0]
    upd = {nm: [jnp.reshape(o, given[nm][0].shape) for o in outs] for nm, outs in small_upd.items()}

    def big_update(wt, m, v, names, call):
        grads = [red[nm].reduced() for nm in names]
        shp3 = (len(grads),) + grads[0].shape
        outs, _ = _adamw(jnp.reshape(wt, shp3), jnp.reshape(m, shp3), jnp.reshape(v, shp3), grads, call)
        return [jnp.reshape(o, wt.shape) for o in outs], None

    upd["mlp_w2"], _ = big_update(mlp_w2, m_mlp_w2, v_mlp_w2, ["w2_0", "w2_1"], "adamw_mlp_w2")
    upd["mlp_w1"], _ = big_update(mlp_w1, m_mlp_w1, v_mlp_w1, ["w1_0", "w1_1"], "adamw_mlp_w1")
    upd["ev_w_in"], _ = big_update(ev_w_in, m_ev_w_in, v_ev_w_in, ["ev_in"], "adamw_ev_w_in")
    upd["ev_w_out"], _ = big_update(ev_w_out, m_ev_w_out, v_ev_w_out, ["ev_out"], "adamw_ev_w_out")
    upd["od_w_in"], _ = big_update(od_w_in, m_od_w_in, v_od_w_in, ["od_in"], "adamw_od_w_in")
    upd["od_w_out"], _ = big_update(od_w_out, m_od_w_out, v_od_w_out, ["od_out"], "adamw_od_w_out")

    order = ["ev_norm_g", "ev_w_in", "ev_conv_a_w", "ev_conv_a_b", "ev_ln_a_g", "ev_ln_a_b", "ev_conv_b_w", "ev_w_out",
             "od_norm_g", "od_w_in", "od_b_in", "od_ln_v_g", "od_ln_v_b", "od_w_s", "od_b_s", "od_w_out", "mlp_norm_g",
             "mlp_w1", "mlp_w2", "final_norm_g"]
    grad_x = jnp.reshape(dx, x.shape)
    return (loss, grad_x, *[upd[nm][0] for nm in order], *[upd[nm][1] for nm in order],
            *[upd[nm][2] for nm in order], *[upd[nm][3] for nm in order])
```

```python
import functools

import jax
import jax.numpy as jnp
from jax import lax
from jax.experimental import pallas as pl
from jax.experimental.pallas import tpu as pltpu
from jax.experimental.pallas import tpu_sc as plsc

F32 = jnp.float32
BF16 = jnp.bfloat16

D_MODEL = 1024
A_DIM = 512
B_DIM = 512
IN_EVEN = 2 * A_DIM + 3 * B_DIM
A_CONV_WIDTH = 31
B_CONV_WIDTH = 3
CHUNK = 128
C_GROUPS = 8
C_DIM = 1024
D_FF = 4096
RMS_EPS = 1e-6
LN_EPS = 1e-5
ADAM_LR = 0.001
ADAM_B1 = 0.9
ADAM_B2 = 0.999
ADAM_EPS = 1e-08
ADAM_WD = 0.01
ADAM_STEP = 10

N_CHIPS = 4
N_DEV = 8
TOKEN_TILE = 512
A_HALO = 32
B_HALO = 8
CONV_ROWS = 16
DW_TAPS = 4
ELEM_ROWS = 16
PAIR = 2 * CHUNK
LANES = 128
SUBLANES = 8
MXU_ROWS = 256
MIB = 1024 * 1024
MESH = pl.DeviceIdType.MESH
ANY = pl.BlockSpec(memory_space=pl.ANY)


def _dot(a, b):
    return lax.dot_general(a, b, (((1,), (0,)), ((), ())), preferred_element_type=F32)


def _dot_nt(a, b):
    return lax.dot_general(a, b, (((1,), (1,)), ((), ())), preferred_element_type=F32)


def _dot_tn(a, b):
    return lax.dot_general(a, b, (((0,), (0,)), ((), ())), preferred_element_type=F32)


def _params(vmem_mib, n_axes=1):
    return pltpu.CompilerParams(dimension_semantics=("arbitrary",) * n_axes, vmem_limit_bytes=vmem_mib * MIB)


def _row_spec(tm, cols, rev_nt=None):
    if rev_nt is None:
        return pl.BlockSpec((tm, cols), lambda i: (i, 0))
    return pl.BlockSpec((tm, cols), lambda i: (rev_nt - 1 - i, 0))


def _full_spec(shape):
    nd = len(shape)
    return pl.BlockSpec(shape, lambda i: (0,) * nd)


def _block_rows(rows, cap=512):
    best = SUBLANES
    for br in range(SUBLANES, min(rows, cap) + 1, SUBLANES):
        if rows % br == 0:
            best = br
    return best


FIRST_SWAP_ID = 5
N_LOADS = 2 * 2 * N_CHIPS


def _load_weights(loads, sems):
    @pl.when(pl.program_id(0) == 0)
    def _():
        copies = []
        for src, dst, rows_of_one in loads:
            r = src.shape[2]
            for q in range(N_CHIPS):
                for h in range(2):
                    part = dst.at[pl.ds((2 * q + h) * r, r)] if rows_of_one else dst.at[q, pl.ds(h * r, r)]
                    copies.append(pltpu.make_async_copy(src.at[q, h], part, sems.at[len(copies)]))
        for cp in copies:
            cp.start()
        for cp in copies:
            cp.wait()


def _rms_fwd(x, g):
    rstd = lax.rsqrt(jnp.mean(x * x, axis=-1, keepdims=True) + RMS_EPS)
    return x * rstd * g, rstd


def _rms_bwd(dn, x, rstd, g):
    a = dn * g
    xh = x * rstd
    dx = rstd * (a - xh * jnp.mean(a * xh, axis=-1, keepdims=True))
    dg = jnp.sum(dn * xh, axis=0, keepdims=True)
    return dx, dg


def _ln_stats(v):
    mu = jnp.mean(v, axis=-1, keepdims=True)
    xc = v - mu
    rs = lax.rsqrt(jnp.mean(xc * xc, axis=-1, keepdims=True) + LN_EPS)
    return xc * rs, rs


def _ln_bwd(dy, xhat, rs, g):
    dxh = dy * g
    dv = rs * (dxh - jnp.mean(dxh, axis=-1, keepdims=True) - xhat * jnp.mean(dxh * xhat, axis=-1, keepdims=True))
    return dv, jnp.sum(dy * xhat, axis=0, keepdims=True), jnp.sum(dy, axis=0, keepdims=True)


def _gelu_cdf(s):
    return 0.5 * (1.0 + lax.erf(s * 0.7071067811865476))


def _mesh_pos():
    return lax.axis_index("x"), lax.axis_index("y"), lax.axis_index("c")


def _other_chips(x, y):
    return [(1 - x, y), (x, 1 - y), (1 - x, 1 - y)]


def _remote(src, dst, send_sem, recv_sem, to):
    return pltpu.make_async_remote_copy(src_ref=src, dst_ref=dst, send_sem=send_sem, recv_sem=recv_sem,
                                        device_id=to, device_id_type=MESH)


def _like(arrays):
    return [jax.ShapeDtypeStruct(a.shape, a.dtype) for a in arrays]


class _PairSwap:
    def __init__(self, grads):
        self.ins = list(grads)
        self.out_shapes = [jax.ShapeDtypeStruct((g.shape[0],) + g.shape[2:], g.dtype) for g in grads]
        self.aliases = {}
        self.n_sems = len(grads)

    def _copies(self, ins, outs, send, recv):
        x, y, c = _mesh_pos()
        return [_remote(ins[t].at[:, 1 - c], outs[t], send.at[t], recv.at[t], (x, y, 1 - c)) for t in range(len(ins))]

    def start(self, ins, outs, send, recv):
        for cp in self._copies(ins, outs, send, recv):
            cp.start()

    def finish(self, ins, outs, send, recv):
        for cp in self._copies(ins, outs, send, recv):
            cp.wait()


class _ChipSwap:
    def __init__(self, parts):
        self.ins = list(parts)
        self.out_shapes = [jax.ShapeDtypeStruct((3,) + p.shape[1:], p.dtype) for p in parts]
        self.aliases = {}
        self.n_sems = 3 * len(parts)

    def _copies(self, ins, outs, send, recv):
        x, y, c = _mesh_pos()
        return [_remote(ins[t].at[2 * chip[0] + chip[1]], outs[t].at[k], send.at[3 * t + k], recv.at[3 * t + k], (*chip, c))
                for t in range(len(ins)) for k, chip in enumerate(_other_chips(x, y))]

    def start(self, ins, outs, send, recv):
        for cp in self._copies(ins, outs, send, recv):
            cp.start()

    def finish(self, ins, outs, send, recv):
        for cp in self._copies(ins, outs, send, recv):
            cp.wait()


class _PairShare:
    def __init__(self, fulls):
        self.ins = list(fulls)
        self.out_shapes = _like(fulls)
        self.aliases = {t: t for t in range(len(fulls))}
        self.n_sems = len(fulls)

    def _copies(self, ins, outs, send, recv):
        x, y, c = _mesh_pos()
        return [_remote(ins[t].at[c], outs[t].at[c], send.at[t], recv.at[t], (x, y, 1 - c)) for t in range(len(ins))]

    def start(self, ins, outs, send, recv):
        for cp in self._copies(ins, outs, send, recv):
            cp.start()

    def finish(self, ins, outs, send, recv):
        for cp in self._copies(ins, outs, send, recv):
            cp.wait()


class _ShareAll:
    def __init__(self, arrays):
        self.ins = list(arrays)
        self.out_shapes = [jax.ShapeDtypeStruct((N_DEV,) + a.shape, a.dtype) for a in arrays]
        self.aliases = {}
        self.n_sems = (N_DEV - 1) * len(arrays)

    def _peers(self):
        x, y, c = _mesh_pos()
        flips = [((r >> 2) & 1, (r >> 1) & 1, r & 1) for r in range(1, N_DEV)]
        return (x, y, c), [(x ^ fx, y ^ fy, c ^ fc) for fx, fy, fc in flips]

    def _sends(self, ins, outs, send, recv):
        (x, y, c), peers = self._peers()
        mine = 4 * x + 2 * y + c
        return [_remote(ins[a], outs[a].at[mine], send.at[7 * a + r], recv.at[7 * a + r], peer)
                for a in range(len(ins)) for r, peer in enumerate(peers)]

    def start(self, ins, outs, send, recv):
        for cp in self._sends(ins, outs, send, recv):
            cp.start()

    def finish(self, ins, outs, send, recv):
        (x, y, c), peers = self._peers()
        for a in range(len(ins)):
            for r, (px, py, pc) in enumerate(peers):
                blk = outs[a].at[4 * px + 2 * py + pc]
                _remote(blk, blk, send.at[7 * a + r], recv.at[7 * a + r], (x, y, c)).wait_recv()
        for cp in self._sends(ins, outs, send, recv):
            cp.wait_send()


def _gather_beside(bufs, name, collective_id):
    n = len(bufs)
    per = 7
    refs = [jax.new_ref(b, memory_space=pltpu.MemorySpace.HBM) for b in bufs]

    @pl.kernel(mesh=plsc.ScalarSubcoreMesh(axis_name="sequencer", num_cores=1), name=name,
               scratch_types=(pltpu.SemaphoreType.DMA((per * n,)), pltpu.SemaphoreType.DMA((per * n,))),
               compiler_params=pltpu.CompilerParams(collective_id=collective_id))
    def launch(send, recv):
        x, y, c = _mesh_pos()
        me, sibling = (x, y, c), (x, y, 1 - c)
        x_nbr, y_nbr = (1 - x, y, c), (x, 1 - y, c)
        mine, via_x, via_y, diag = 2 * x + y, 2 * (1 - x) + y, 2 * x + (1 - y), 2 * (1 - x) + (1 - y)
        barrier = pltpu.get_barrier_semaphore()
        peers = [x_nbr, y_nbr, sibling]
        for peer in peers:
            pl.semaphore_signal(barrier, inc=1, device_id=peer, device_id_type=MESH)
        pl.semaphore_wait(barrier, len(peers))

        def copy(t, k, src, dst, to):
            return _remote(src, dst, send.at[per * t + k], recv.at[per * t + k], to)

        def piece(t, chip, half, rows=None):
            blk = refs[t].at[chip, half]
            return blk if rows is None else blk.at[rows]

        started = []

        def go(cp):
            cp.start()
            started.append(cp)

        upper = [pl.ds(0, r.shape[2] // 2) for r in refs]
        lower = [pl.ds(r.shape[2] // 2, r.shape[2] // 2) for r in refs]
        for t in range(n):
            go(copy(t, 0, piece(t, mine, c), piece(t, mine, c), x_nbr))
            go(copy(t, 1, piece(t, mine, c), piece(t, mine, c), y_nbr))
        for t in range(n):
            copy(t, 0, piece(t, via_x, c), piece(t, via_x, c), me).wait_recv()
            go(copy(t, 2, piece(t, via_x, c, upper[t]), piece(t, via_x, c, upper[t]), y_nbr))
            go(copy(t, 4, piece(t, via_x, c), piece(t, via_x, c), sibling))
            copy(t, 1, piece(t, via_y, c), piece(t, via_y, c), me).wait_recv()
            go(copy(t, 3, piece(t, via_y, c, lower[t]), piece(t, via_y, c, lower[t]), x_nbr))
            go(copy(t, 5, piece(t, via_y, c), piece(t, via_y, c), sibling))
        for t in range(n):
            copy(t, 2, piece(t, diag, c, upper[t]), piece(t, diag, c, upper[t]), me).wait_recv()
            copy(t, 3, piece(t, diag, c, lower[t]), piece(t, diag, c, lower[t]), me).wait_recv()
            go(copy(t, 6, piece(t, diag, c), piece(t, diag, c), sibling))
        for t in range(n):
            for k, chip in ((4, via_x), (5, via_y), (6, diag)):
                copy(t, k, piece(t, chip, 1 - c), piece(t, chip, 1 - c), me).wait_recv()
        for cp in started:
            cp.wait_send()

    launch()
    return [r[...] for r in refs]


def _chip_swap_beside(parts, name, collective_id):
    src = jax.new_ref(parts, memory_space=pltpu.MemorySpace.HBM)
    dst = jax.empty_ref(jax.ShapeDtypeStruct((N_CHIPS - 1,) + parts.shape[1:], parts.dtype),
                        memory_space=pltpu.MemorySpace.HBM)
    swap = _ChipSwap([parts])

    @pl.kernel(mesh=plsc.ScalarSubcoreMesh(axis_name="sequencer", num_cores=1), name=name,
               scratch_types=(pltpu.SemaphoreType.DMA((N_CHIPS - 1,)), pltpu.SemaphoreType.DMA((N_CHIPS - 1,))),
               compiler_params=pltpu.CompilerParams(collective_id=collective_id))
    def launch(send, recv):
        x, y, c = _mesh_pos()
        barrier = pltpu.get_barrier_semaphore()
        peers = [(*chip, c) for chip in _other_chips(x, y)]
        for peer in peers:
            pl.semaphore_signal(barrier, inc=1, device_id=peer, device_id_type=MESH)
        pl.semaphore_wait(barrier, len(peers))
        swap.start([src], [dst], send, recv)
        swap.finish([src], [dst], send, recv)

    launch()
    return dst[...]


def _pallas(body, operands, *, name, grid, in_specs, out_specs, out_shape, scratch_shapes=(), vmem_mib=32, riders=(),
            prefetch=None):
    in_specs, out_specs, out_shape, scratch_shapes = list(in_specs), list(out_specs), list(out_shape), list(scratch_shapes)
    if not riders and prefetch is None:
        outs = pl.pallas_call(body, name=name, grid=grid, in_specs=in_specs, out_specs=out_specs, out_shape=out_shape,
                              scratch_shapes=scratch_shapes, compiler_params=_params(vmem_mib, len(grid)))(*operands)
        return list(outs), []
    n_in, n_out, n_scr = len(in_specs), len(out_specs), len(scratch_shapes)
    r_in = [len(r.ins) for r in riders]
    r_out = [len(r.out_shapes) for r in riders]
    steps = 1
    for g in grid:
        steps *= g

    n_pre = 0 if prefetch is None else 1

    def wrapped(*refs):
        refs = list(refs)
        pre, refs = refs[:n_pre], refs[n_pre:]
        ins, refs = refs[:n_in], refs[n_in:]
        rins = []
        for k in r_in:
            rins.append(refs[:k])
            refs = refs[k:]
        outs, refs = refs[:n_out], refs[n_out:]
        routs = []
        for k in r_out:
            routs.append(refs[:k])
            refs = refs[k:]
        scr, sems = refs[:n_scr], refs[n_scr:]
        step = 0
        for ax, g in enumerate(grid):
            step = step * g + pl.program_id(ax)

        def each(what):
            for j, r in enumerate(riders):
                if hasattr(r, what):
                    getattr(r, what)(rins[j], routs[j], sems[2 * j], sems[2 * j + 1])

        if grid:
            pl.when(step == 0)(lambda: each("start"))
        else:
            each("start")
        body(*pre, *ins, *outs, *scr)
        if grid:
            @pl.when(step == steps - 1)
            def _():
                each("near_end")
                each("finish")
        else:
            each("near_end")
            each("finish")

    aliases, off_in, off_out = {}, n_pre + n_in, n_out
    for r, ki, ko in zip(riders, r_in, r_out):
        for i, o in r.aliases.items():
            aliases[off_in + i] = off_out + o
        off_in, off_out = off_in + ki, off_out + ko
    sems = []
    for r in riders:
        sems += [pltpu.SemaphoreType.DMA((r.n_sems,)), pltpu.SemaphoreType.DMA((r.n_sems,))]
    layout = dict(grid=grid, in_specs=in_specs + [ANY] * sum(r_in), out_specs=out_specs + [ANY] * sum(r_out),
                  scratch_shapes=scratch_shapes + sems)
    if prefetch is not None:
        layout = dict(grid_spec=pltpu.PrefetchScalarGridSpec(num_scalar_prefetch=1, **layout))
    res = pl.pallas_call(
        wrapped, name=name, **layout,
        out_shape=out_shape + [s for r in riders for s in r.out_shapes], input_output_aliases=aliases,
        compiler_params=pltpu.CompilerParams(dimension_semantics=("arbitrary",) * len(grid),
                                             vmem_limit_bytes=vmem_mib * MIB, has_side_effects=True),
    )(*([] if prefetch is None else [prefetch]), *operands, *[a for r in riders for a in r.ins])
    res = list(res)
    outs, res = res[:n_out], res[n_out:]
    routs = []
    for k in r_out:
        routs.append(res[:k])
        res = res[k:]
    return outs, routs


def _exchange(riders, name):
    return _pallas(lambda: None, [], name=name, grid=(), in_specs=[], out_specs=[], out_shape=[], riders=riders)[1]


def _in_hbm(a):
    return pltpu.with_memory_space_constraint(a, pltpu.HBM)


def _place_shard(w, layer, dtype, name):
    _, rows, cols = w.shape
    half = rows // 2
    br = _block_rows(half)
    nb = half // br
    mine = 2 * lax.axis_index("x") + lax.axis_index("y")

    def body(q_ref, w_ref, o_ref):
        o_ref[...] = w_ref[...].astype(dtype)

    return pl.pallas_call(
        body, name=name,
        grid_spec=pltpu.PrefetchScalarGridSpec(
            num_scalar_prefetch=1, grid=(2, nb),
            in_specs=[pl.BlockSpec((None, br, cols), lambda h, i, q: (layer, h * nb + i, 0))],
            out_specs=pl.BlockSpec((None, None, br, cols), lambda h, i, q: (q[0], h, i, 0))),
        out_shape=pltpu.HBM((N_CHIPS, 2, half, cols), dtype),
        compiler_params=_params(16, 2),
    )(jnp.reshape(mine, (1,)).astype(jnp.int32), _in_hbm(w))


def _plain_copy(a, name):
    def body(a_ref, o_ref):
        o_ref[...] = a_ref[...]

    vmem = pl.BlockSpec(memory_space=pltpu.VMEM)
    return pl.pallas_call(body, name=name, in_specs=[vmem], out_specs=vmem,
                          out_shape=jax.ShapeDtypeStruct(a.shape, a.dtype))(a)


def _add_pair(g, recv, name):
    _, _, r, cdim = g.shape
    br = _block_rows(r, 256)
    c = lax.axis_index("c")

    def body(c_ref, g_ref, r_ref, o_ref):
        o_ref[...] = (g_ref[...] + r_ref[...]).astype(BF16)

    return pl.pallas_call(
        body, name=name,
        grid_spec=pltpu.PrefetchScalarGridSpec(
            num_scalar_prefetch=1, grid=(N_CHIPS, r // br),
            in_specs=[pl.BlockSpec((None, None, br, cdim), lambda q, i, c_ref: (q, c_ref[0], i, 0)),
                      pl.BlockSpec((None, br, cdim), lambda q, i, c_ref: (q, i, 0))],
            out_specs=pl.BlockSpec((None, br, cdim), lambda q, i, c_ref: (q, i, 0))),
        out_shape=pltpu.HBM((N_CHIPS, r, cdim), BF16),
        compiler_params=_params(16, 2),
    )(jnp.reshape(c, (1,)).astype(jnp.int32), _in_hbm(g), _in_hbm(recv))


def _add_chips(own, recv, name):
    _, r, cdim = own.shape
    br = _block_rows(r, 256)
    x, y, c = _mesh_pos()

    def body(pos_ref, own_ref, r_ref, o_ref):
        acc = own_ref[...].astype(F32)
        for k in range(3):
            acc = acc + r_ref[k].astype(F32)
        o_ref[...] = acc

    return pl.pallas_call(
        body, name=name,
        grid_spec=pltpu.PrefetchScalarGridSpec(
            num_scalar_prefetch=1, grid=(r // br,),
            in_specs=[pl.BlockSpec((None, br, cdim), lambda i, pos: (pos[0], i, 0)),
                      pl.BlockSpec((3, br, cdim), lambda i, pos: (0, i, 0))],
            out_specs=pl.BlockSpec((None, br, cdim), lambda i, pos: (pos[1], i, 0))),
        out_shape=pltpu.HBM((2, r, cdim), F32),
        compiler_params=_params(16, 1),
    )(jnp.stack([2 * x + y, c]).astype(jnp.int32), _in_hbm(own), _in_hbm(recv))


def _adam_math(w, m, v, g):
    c1 = 1.0 / (1.0 - ADAM_B1 ** ADAM_STEP)
    c2 = 1.0 / (1.0 - ADAM_B2 ** ADAM_STEP)
    m_new = ADAM_B1 * m + (1.0 - ADAM_B1) * g
    v_new = ADAM_B2 * v + (1.0 - ADAM_B2) * (g * g)
    return -ADAM_LR * ((m_new * c1) / (jnp.sqrt(v_new * c2) + ADAM_EPS) + ADAM_WD * w), m_new, v_new


SMALL_WEIGHTS = [
    ("ev_norm_g", (1, D_MODEL), ["ev_norm_g"], None), ("ev_conv_a_b", (1, A_DIM), ["ev_conv_a_b"], None),
    ("ev_ln_a_g", (1, A_DIM), ["ev_ln_a_g"], None), ("ev_ln_a_b", (1, A_DIM), ["ev_ln_a_b"], None),
    ("od_w_s", (C_GROUPS, CHUNK, CHUNK), ["od_w_s_lo", "od_w_s_hi"], None), ("od_b_s", (C_GROUPS, CHUNK), ["od_b_s"], None),
    ("mlp_norm_g", (2, D_MODEL), ["mlp_norm_g0", "mlp_norm_g1"], None), ("final_norm_g", (1, D_MODEL), ["final_norm_g"], None),
    ("ev_conv_a_w", (A_CONV_WIDTH, A_DIM // N_CHIPS), ["ev_conv_a_w"], A_DIM // N_CHIPS),
    ("ev_conv_b_w", (B_CONV_WIDTH, B_DIM // N_CHIPS), ["ev_conv_b_w"], B_DIM // N_CHIPS),
    ("od_norm_g", (1, D_MODEL // N_CHIPS), ["od_norm_g"], D_MODEL // N_CHIPS),
    ("od_b_in", (1, 2 * C_DIM // N_CHIPS), ["od_b_in"], 2 * C_DIM // N_CHIPS),
    ("od_ln_v_g", (1, C_DIM // N_CHIPS), ["od_ln_v_g"], C_DIM // N_CHIPS),
    ("od_ln_v_b", (1, C_DIM // N_CHIPS), ["od_ln_v_b"], C_DIM // N_CHIPS),
]


def _small_update(own, landed, weights):
    names = list(own.keys())
    n_g, n_w = len(names), len(SMALL_WEIGHTS)

    def body(*refs):
        refs = list(refs)
        own_refs = dict(zip(names, refs[:n_g]))
        land_refs = dict(zip(names, refs[n_g:2 * n_g]))
        wmv = [refs[2 * n_g + 3 * i:2 * n_g + 3 * i + 3] for i in range(n_w)]
        o0 = 2 * n_g + 3 * n_w
        loss_ref = refs[o0]
        outs = [refs[o0 + 1 + 4 * i:o0 + 5 + 4 * i] for i in range(n_w)]
        acc = dict(zip(names, refs[o0 + 1 + 4 * n_w:]))
        x, y, c = _mesh_pos()
        mine, chip = 4 * x + 2 * y + c, 2 * x + y

        for nm in names:
            for d in range(N_DEV):
                def add(term, nm=nm, d=d):
                    acc[nm][...] = term if d == 0 else acc[nm][...] + term
                pl.when(mine == d)(lambda nm=nm, add=add: add(own_refs[nm][...]))
                pl.when(mine != d)(lambda nm=nm, d=d, add=add: add(land_refs[nm][d]))
        loss_ref[...] = acc["loss"][...]

        def update(i, rows, g):
            w_ref, m_ref, v_ref = wmv[i]
            delta, m_new, v_new = _adam_math(w_ref[rows], m_ref[rows], v_ref[rows], g)
            for ref, val in zip(outs[i], (g, delta, m_new, v_new)):
                ref[rows] = val

        for i, (_, shape, grads, per_chip) in enumerate(SMALL_WEIGHTS):
            for row, gname in enumerate(grads):
                per_grad = shape[0] // len(grads)
                rows = slice(row * per_grad, (row + 1) * per_grad)
                if per_chip is None:
                    update(i, rows, acc[gname][...])
                else:
                    for q in range(N_CHIPS):
                        pl.when(chip == q)(lambda i=i, rows=rows, gname=gname, q=q, per_chip=per_chip:
                                           update(i, rows, acc[gname][:, q * per_chip:(q + 1) * per_chip]))

    operands = [own[nm] for nm in names] + [landed[nm] for nm in names]
    for nm, _, _, _ in SMALL_WEIGHTS:
        operands += list(weights[nm])
    out_shape = [jax.ShapeDtypeStruct((1, 1), F32)]
    for _, shape, _, _ in SMALL_WEIGHTS:
        out_shape += [jax.ShapeDtypeStruct(shape, F32)] * 4
    res = pl.pallas_call(
        body, name="small_update", grid=(1,),
        in_specs=[_full_spec(a.shape) for a in operands], out_specs=[_full_spec(s.shape) for s in out_shape],
        out_shape=out_shape, scratch_shapes=[pltpu.VMEM(own[nm].shape, F32) for nm in names],
        compiler_params=_params(32, 1),
    )(*[_in_hbm(a) for a in operands])
    return res[0], {nm: res[1 + 4 * i:5 + 4 * i] for i, (nm, _, _, _) in enumerate(SMALL_WEIGHTS)}


def _adamw(w, m, v, grads, name, riders=()):
    layers, r, cdim = w.shape
    br = _block_rows(r, 256 if cdim > LANES else 1024)

    def body(*refs):
        w_ref, m_ref, v_ref = refs[:3]
        g_refs = refs[3:3 + layers]
        go_ref, d_ref, mo_ref, vo_ref = refs[3 + layers:]
        layer = pl.program_id(0)
        for l in range(layers):
            @pl.when(layer == l)
            def _(l=l):
                g = g_refs[l][...]
                go_ref[...] = g
                d_ref[...], mo_ref[...], vo_ref[...] = _adam_math(w_ref[...], m_ref[...], v_ref[...], g)

    spec3 = pl.BlockSpec((None, br, cdim), lambda l, i: (l, i, 0))
    spec2 = pl.BlockSpec((br, cdim), lambda l, i: (i, 0))
    out = jax.ShapeDtypeStruct((layers, r, cdim), F32)
    return _pallas(body, [_in_hbm(a) for a in (w, m, v, *grads)], name=name, grid=(layers, r // br),
                   in_specs=[spec3, spec3, spec3] + [spec2] * layers, out_specs=[spec3] * 4, out_shape=[out] * 4,
                   vmem_mib=32, riders=riders)


def _fill_shifted(buf, rows):
    for b in range(1, SUBLANES):
        buf[b, 0:rows - SUBLANES, :] = buf[0, b:b + rows - SUBLANES, :]


def _window(buf, start, size):
    return buf[start % SUBLANES, start - start % SUBLANES:start - start % SUBLANES + size, :]


def _conv31(src, w_ref, r0, base, init):
    acc = init
    for k in range(A_CONV_WIDTH):
        acc = acc + w_ref[k:k + 1, :] * _window(src, base + k + r0, CONV_ROWS)
    return acc


def _fwd_even(x, norm_g, w_in, conv_a_w, conv_a_b, ln_g, ln_b, conv_b_w, w_out, *, tm, seq, riders=()):
    tokens = x.shape[0]
    nt, tps = tokens // tm, seq // tm

    def body(x_ref, g_ref, win_hbm, caw_ref, cab_ref, lng_ref, lnb_ref, cbw_ref, wout_hbm,
             h_ref, n_ref, z_ref, a2_ref, cv_ref, mix_ref, win_v, wout_v, pa, pb, sem):
        i = pl.program_id(0)

        _load_weights([(win_hbm, win_v, False), (wout_hbm, wout_v, True)], sem)

        xv = x_ref[...]
        nf, _ = _rms_fwd(xv, g_ref[...])
        n = nf.astype(BF16)
        n_ref[...] = n
        z = jnp.concatenate([_dot(n, win_v[j]) for j in range(N_CHIPS)], axis=1)
        z_ref[...] = z.astype(BF16)
        a_val, a_gate = z[:, 0:A_DIM], z[:, A_DIM:2 * A_DIM]
        b_gate, c_gate, b_val = z[:, 1024:1536], z[:, 1536:2048], z[:, 2048:2560]

        first = (i % tps) == 0

        @pl.when(first)
        def _():
            pa[0, 0:A_HALO, :] = jnp.zeros((A_HALO, A_DIM), F32)
            pb[0:B_HALO, :] = jnp.zeros((B_HALO, B_DIM), F32)

        @pl.when(jnp.logical_not(first))
        def _():
            pa[0, 0:A_HALO, :] = pa[0, tm:tm + A_HALO, :]
            pb[0:B_HALO, :] = pb[tm:tm + B_HALO, :]

        pa[0, A_HALO:A_HALO + tm, :] = a_val * jax.nn.sigmoid(a_gate)
        pb[B_HALO:B_HALO + tm, :] = c_gate * b_val
        _fill_shifted(pa, A_HALO + tm)
        bias = jnp.broadcast_to(cab_ref[...], (CONV_ROWS, A_DIM))
        for r0 in range(0, tm, CONV_ROWS):
            a2_ref[r0:r0 + CONV_ROWS, :] = _conv31(pa, caw_ref, r0, A_HALO - (A_CONV_WIDTH - 1), bias)
        xhat, _ = _ln_stats(a2_ref[...])
        a3 = xhat * lng_ref[...] + lnb_ref[...]
        a4 = a3 * jax.nn.sigmoid(a3)
        cv = cbw_ref[0:1, :] * pb[B_HALO - 2:B_HALO - 2 + tm, :]
        cv = cv + cbw_ref[1:2, :] * pb[B_HALO - 1:B_HALO - 1 + tm, :]
        cv = cv + cbw_ref[2:3, :] * pb[B_HALO:B_HALO + tm, :]
        cv_ref[...] = cv.astype(BF16)
        mix = jnp.concatenate([a4, b_gate * cv], axis=1).astype(BF16)
        mix_ref[...] = mix
        h_ref[...] = xv + _dot(mix, wout_v[...])

    shp = lambda cols, dt: jax.ShapeDtypeStruct((tokens, cols), dt)
    return _pallas(
        body, [x, norm_g, w_in, conv_a_w, conv_a_b, ln_g, ln_b, conv_b_w, w_out], name="fwd_even", grid=(nt,),
        in_specs=[_row_spec(tm, D_MODEL), _full_spec((1, D_MODEL)), ANY, _full_spec((A_CONV_WIDTH, A_DIM)),
                  _full_spec((1, A_DIM)), _full_spec((1, A_DIM)), _full_spec((1, A_DIM)),
                  _full_spec((B_CONV_WIDTH, B_DIM)), ANY],
        out_specs=[_row_spec(tm, D_MODEL), _row_spec(tm, D_MODEL), _row_spec(tm, IN_EVEN), _row_spec(tm, A_DIM),
                   _row_spec(tm, B_DIM), _row_spec(tm, D_MODEL)],
        out_shape=[shp(D_MODEL, F32), shp(D_MODEL, BF16), shp(IN_EVEN, BF16), shp(A_DIM, F32), shp(B_DIM, BF16),
                   shp(D_MODEL, BF16)],
        scratch_shapes=[pltpu.VMEM((N_CHIPS, D_MODEL, IN_EVEN // N_CHIPS), BF16), pltpu.VMEM((D_MODEL, D_MODEL), BF16),
                        pltpu.VMEM((SUBLANES, A_HALO + tm, A_DIM), F32), pltpu.VMEM((B_HALO + tm, B_DIM), F32),
                        pltpu.SemaphoreType.DMA((N_LOADS,))],
        vmem_mib=56, riders=riders)


def _loss_tail(xv, g, target, loss_ref, dh_ref, dhb_ref, dg_ref):
    @pl.when(pl.program_id(0) == 0)
    def _():
        loss_ref[...] = jnp.zeros((1, 1), F32)
        dg_ref[...] = jnp.zeros((1, D_MODEL), F32)

    out, rstd = _rms_fwd(xv, g)
    err = out - target
    per_token = jnp.sum(err * err, axis=1, keepdims=True) * (1.0 / D_MODEL)
    loss_ref[...] += 0.5 * jnp.sum(per_token, axis=0, keepdims=True)
    dx, dg = _rms_bwd(err * (1.0 / D_MODEL), xv, rstd, g)
    dh_ref[...] = dx
    dhb_ref[...] = dx.astype(BF16)
    dg_ref[...] += dg


def _fwd_mlp(h, norm_g, w1, w2, layer, *, tm, riders=(), head=None):
    tokens = h.shape[0]
    nt = tokens // tm
    fs = D_FF // N_CHIPS
    n_in = 4 if head is None else 6

    def body(*refs):
        h_ref, g_ref, w1_hbm, w2_hbm = refs[:4]
        w1_v, w2_v, sem = refs[-3:]
        outs = refs[n_in:-3]
        n_ref, p_ref, q_ref = outs[1:4] if head is None else outs[0:3]
        _load_weights([(w1_hbm, w1_v, False), (w2_hbm, w2_v, False)], sem)

        xv = h_ref[...]
        nf, _ = _rms_fwd(xv, g_ref[...])
        n = nf.astype(BF16)
        n_ref[...] = n
        acc = xv
        for j in range(N_CHIPS):
            p = _dot(n, w1_v[j])
            p_ref[:, j * fs:(j + 1) * fs] = p.astype(BF16)
            r = jnp.maximum(p, 0.0)
            q = (r * r).astype(BF16)
            q_ref[:, j * fs:(j + 1) * fs] = q
            acc = acc + _dot(q, w2_v[j])
        if head is None:
            outs[0][...] = acc
        else:
            _loss_tail(acc, refs[4][...], refs[5][...], *outs[3:7])

    shp = lambda cols, dt: jax.ShapeDtypeStruct((tokens, cols), dt)
    saved_specs = [_row_spec(tm, D_MODEL), _row_spec(tm, D_FF), _row_spec(tm, D_FF)]
    saved_shapes = [shp(D_MODEL, BF16), shp(D_FF, BF16), shp(D_FF, BF16)]
    if head is None:
        operands, in_specs = [h, norm_g, w1, w2], [_row_spec(tm, D_MODEL), _full_spec((1, D_MODEL)), ANY, ANY]
        out_specs, out_shape = [_row_spec(tm, D_MODEL)] + saved_specs, [shp(D_MODEL, F32)] + saved_shapes
    else:
        operands = [h, norm_g, w1, w2, *head]
        in_specs = [_row_spec(tm, D_MODEL), _full_spec((1, D_MODEL)), ANY, ANY, _full_spec((1, D_MODEL)), _row_spec(tm, D_MODEL)]
        out_specs = saved_specs + [_full_spec((1, 1)), _row_spec(tm, D_MODEL), _row_spec(tm, D_MODEL), _full_spec((1, D_MODEL))]
        out_shape = saved_shapes + [jax.ShapeDtypeStruct((1, 1), F32), shp(D_MODEL, F32), shp(D_MODEL, BF16),
                                    jax.ShapeDtypeStruct((1, D_MODEL), F32)]
    return _pallas(
        body, operands, name=f"fwd_mlp{layer}", grid=(nt,), in_specs=in_specs, out_specs=out_specs, out_shape=out_shape,
        scratch_shapes=[pltpu.VMEM((N_CHIPS, D_MODEL, fs), BF16), pltpu.VMEM((N_CHIPS, fs, D_MODEL), BF16),
                        pltpu.SemaphoreType.DMA((N_LOADS,))],
        vmem_mib=56, riders=riders)


def _tril_mask():
    row = lax.broadcasted_iota(jnp.int32, (CHUNK, CHUNK), 0)
    col = lax.broadcasted_iota(jnp.int32, (CHUNK, CHUNK), 1)
    return row >= col


def _triu_mask():
    row = lax.broadcasted_iota(jnp.int32, (CHUNK, CHUNK), 0)
    col = lax.broadcasted_iota(jnp.int32, (CHUNK, CHUNK), 1)
    return row <= col


def _fwd_odd(h, norm_g, w_in, b_in, ln_g, ln_b, w_s, b_s_rows, w_out, *, tm, riders=()):
    tokens = h.shape[0]
    nt = tokens // tm
    cs = 2 * C_DIM // N_CHIPS

    def body(h_ref, g_ref, win_hbm, bin_ref, lng_ref, lnb_ref, ws_ref, bs_ref, wout_hbm,
             ho_ref, n_ref, s_ref, cdf_ref, sv_ref, y_ref, win_v, wout_v, bd, sem):
        _load_weights([(win_hbm, win_v, False), (wout_hbm, wout_v, True)], sem)

        @pl.when(pl.program_id(0) == 0)
        def _():
            mask = _tril_mask()
            bd[...] = jnp.zeros(bd.shape, BF16)
            for g in range(C_GROUPS):
                w = jnp.where(mask, ws_ref[g], 0.0).astype(BF16)
                bd[g, 0:CHUNK, 0:CHUNK] = w
                bd[g, CHUNK:PAIR, CHUNK:PAIR] = w

        xv = h_ref[...]
        nf, _ = _rms_fwd(xv, g_ref[...])
        n = nf.astype(BF16)
        n_ref[...] = n
        s = jnp.concatenate([_dot(n, win_v[j]) for j in range(N_CHIPS)], axis=1) + bin_ref[...]
        s_ref[...] = s.astype(BF16)
        cdf = _gelu_cdf(s)
        cdf_ref[...] = cdf.astype(BF16)
        zz = s * cdf
        u, v = zz[:, 0:C_DIM], zz[:, C_DIM:2 * C_DIM]
        xhat, _ = _ln_stats(v)
        vn = (xhat * lng_ref[...] + lnb_ref[...]).astype(BF16)
        for g in range(C_GROUPS):
            cols = slice(g * CHUNK, (g + 1) * CHUNK)
            bias = jnp.concatenate([bs_ref[g], bs_ref[g]], axis=0)
            for r0 in range(0, tm, PAIR):
                sv = _dot(bd[g], vn[r0:r0 + PAIR, cols]) + bias
                sv_ref[r0:r0 + PAIR, cols] = sv.astype(BF16)
                y_ref[r0:r0 + PAIR, cols] = (u[r0:r0 + PAIR, cols] * sv).astype(BF16)
        ho_ref[...] = xv + _dot(y_ref[...], wout_v[...])

    shp = lambda cols, dt: jax.ShapeDtypeStruct((tokens, cols), dt)
    return _pallas(
        body, [h, norm_g, w_in, b_in, ln_g, ln_b, w_s, b_s_rows, w_out], name="fwd_odd", grid=(nt,),
        in_specs=[_row_spec(tm, D_MODEL), _full_spec((1, D_MODEL)), ANY, _full_spec((1, 2 * C_DIM)),
                  _full_spec((1, C_DIM)), _full_spec((1, C_DIM)), _full_spec((C_GROUPS, CHUNK, CHUNK)),
                  _full_spec((C_GROUPS, CHUNK, CHUNK)), ANY],
        out_specs=[_row_spec(tm, D_MODEL), _row_spec(tm, D_MODEL), _row_spec(tm, 2 * C_DIM), _row_spec(tm, 2 * C_DIM),
                   _row_spec(tm, C_DIM), _row_spec(tm, C_DIM)],
        out_shape=[shp(D_MODEL, F32), shp(D_MODEL, BF16), shp(2 * C_DIM, BF16), shp(2 * C_DIM, BF16), shp(C_DIM, BF16),
                   shp(C_DIM, BF16)],
        scratch_shapes=[pltpu.VMEM((N_CHIPS, D_MODEL, cs), BF16), pltpu.VMEM((C_DIM, D_MODEL), BF16),
                        pltpu.VMEM((C_GROUPS, PAIR, PAIR), BF16), pltpu.SemaphoreType.DMA((N_LOADS,))],
        vmem_mib=56, riders=riders)


def _bwd_mlp(dh, h, norm_g, p, w1, w2, layer, *, tm, riders=()):
    tokens = h.shape[0]
    nt = tokens // tm
    fs = D_FF // N_CHIPS

    def body(dh_ref, h_ref, g_ref, p_ref, w1_hbm, w2_hbm, dx_ref, dxb_ref, dp_ref, dg_ref, w1_v, w2_v, sem):
        @pl.when(pl.program_id(0) == 0)
        def _():
            dg_ref[...] = jnp.zeros((1, D_MODEL), F32)

        _load_weights([(w1_hbm, w1_v, False), (w2_hbm, w2_v, False)], sem)

        dhv = dh_ref[...]
        dhb = dhv.astype(BF16)
        dn = jnp.zeros((tm, D_MODEL), F32)
        for j in range(N_CHIPS):
            dq = _dot_nt(dhb, w2_v[j])
            r = jnp.maximum(p_ref[:, j * fs:(j + 1) * fs].astype(F32), 0.0)
            dp = ((2.0 * r) * dq).astype(BF16)
            dp_ref[:, j * fs:(j + 1) * fs] = dp
            dn = dn + _dot_nt(dp, w1_v[j])
        xv = h_ref[...]
        g = g_ref[...]
        _, rstd = _rms_fwd(xv, g)
        dx, dg = _rms_bwd(dn, xv, rstd, g)
        dx_ref[...] = dhv + dx
        dxb_ref[...] = (dhv + dx).astype(BF16)
        dg_ref[...] += dg

    return _pallas(
        body, [dh, h, norm_g, p, w1, w2], name=f"bwd_mlp{layer}", grid=(nt,),
        in_specs=[_row_spec(tm, D_MODEL), _row_spec(tm, D_MODEL), _full_spec((1, D_MODEL)), _row_spec(tm, D_FF), ANY, ANY],
        out_specs=[_row_spec(tm, D_MODEL), _row_spec(tm, D_MODEL), _row_spec(tm, D_FF), _full_spec((1, D_MODEL))],
        out_shape=[jax.ShapeDtypeStruct((tokens, D_MODEL), F32), jax.ShapeDtypeStruct((tokens, D_MODEL), BF16),
                   jax.ShapeDtypeStruct((tokens, D_FF), BF16), jax.ShapeDtypeStruct((1, D_MODEL), F32)],
        scratch_shapes=[pltpu.VMEM((N_CHIPS, D_MODEL, fs), BF16), pltpu.VMEM((N_CHIPS, fs, D_MODEL), BF16),
                        pltpu.SemaphoreType.DMA((N_LOADS,))],
        vmem_mib=56, riders=riders)


def _bwd_odd(dh, h, norm_g, s, cdf, sv, w_in, ln_g, ln_b, w_s, w_out, *, tm, riders=()):
    tokens = h.shape[0]
    nt = tokens // tm
    cs = 2 * C_DIM // N_CHIPS

    def body(dh_ref, h_ref, g_ref, s_ref, cdf_ref, sv_ref, win_hbm, lng_ref, lnb_ref, ws_ref, wout_hbm,
             dx_ref, dxb_ref, ds_ref, dg_ref, dbin_ref, dlng_ref, dlnb_ref, dws_ref, dbs_ref,
             win_v, wout_v, bdt, dws_acc, dbs_acc, dvn, sem):
        i = pl.program_id(0)

        _load_weights([(win_hbm, win_v, False), (wout_hbm, wout_v, True)], sem)

        @pl.when(i == 0)
        def _():
            mask_t = _triu_mask()
            bdt[...] = jnp.zeros(bdt.shape, BF16)
            for g in range(C_GROUPS):
                wt = jnp.where(mask_t, ws_ref[g].T, 0.0).astype(BF16)
                bdt[g, 0:CHUNK, 0:CHUNK] = wt
                bdt[g, CHUNK:PAIR, CHUNK:PAIR] = wt
            dws_acc[...] = jnp.zeros(dws_acc.shape, F32)
            dbs_acc[...] = jnp.zeros(dbs_acc.shape, F32)
            dg_ref[...] = jnp.zeros(dg_ref.shape, F32)
            dbin_ref[...] = jnp.zeros(dbin_ref.shape, F32)
            dlng_ref[...] = jnp.zeros(dlng_ref.shape, F32)
            dlnb_ref[...] = jnp.zeros(dlnb_ref.shape, F32)

        dhv = dh_ref[...]
        dy = _dot_nt(dhv.astype(BF16), wout_v[...])
        sf = s_ref[...].astype(F32)
        cdf = cdf_ref[...].astype(F32)
        pdf = jnp.exp(-0.5 * sf * sf) * 0.3989422804014327
        zz = sf * cdf
        dgelu = cdf + sf * pdf
        u, v = zz[:, 0:C_DIM], zz[:, C_DIM:2 * C_DIM]
        xhat, rs = _ln_stats(v)
        lng = lng_ref[...]
        vn = (xhat * lng + lnb_ref[...]).astype(BF16)
        du = dy * sv_ref[...].astype(F32)
        dsv = dy * u
        dsvb = dsv.astype(BF16)
        for g in range(C_GROUPS):
            cols = slice(g * CHUNK, (g + 1) * CHUNK)
            for r0 in range(0, tm, PAIR):
                blk = dsvb[r0:r0 + PAIR, cols]
                dvn[r0:r0 + PAIR, cols] = _dot(bdt[g], blk)
                dws_acc[g] += _dot_nt(blk, vn[r0:r0 + PAIR, cols])
                dbs_acc[g] += dsv[r0:r0 + CHUNK, cols] + dsv[r0 + CHUNK:r0 + PAIR, cols]
        dv, dlng, dlnb = _ln_bwd(dvn[...], xhat, rs, lng)
        dlng_ref[...] += dlng
        dlnb_ref[...] += dlnb
        ds = jnp.concatenate([du, dv], axis=1) * dgelu
        dbin_ref[...] += jnp.sum(ds, axis=0, keepdims=True)
        dsb = ds.astype(BF16)
        ds_ref[...] = dsb
        dn = jnp.zeros((tm, D_MODEL), F32)
        for j in range(N_CHIPS):
            dn = dn + _dot_nt(dsb[:, j * cs:(j + 1) * cs], win_v[j])
        xv = h_ref[...]
        g = g_ref[...]
        _, rstd = _rms_fwd(xv, g)
        dx, dg = _rms_bwd(dn, xv, rstd, g)
        dx_ref[...] = dhv + dx
        dxb_ref[...] = (dhv + dx).astype(BF16)
        dg_ref[...] += dg

        @pl.when(i == nt - 1)
        def _():
            mask = _tril_mask()
            for g in range(C_GROUPS):
                full = dws_acc[g]
                dws_ref[g] = jnp.where(mask, full[0:CHUNK, 0:CHUNK] + full[CHUNK:PAIR, CHUNK:PAIR], 0.0)
                dbs_ref[g:g + 1, :] = jnp.sum(dbs_acc[g].T, axis=0, keepdims=True)

    row = lambda cols: jax.ShapeDtypeStruct((1, cols), F32)
    return _pallas(
        body, [dh, h, norm_g, s, cdf, sv, w_in, ln_g, ln_b, w_s, w_out], name="bwd_odd", grid=(nt,),
        in_specs=[_row_spec(tm, D_MODEL), _row_spec(tm, D_MODEL), _full_spec((1, D_MODEL)), _row_spec(tm, 2 * C_DIM),
                  _row_spec(tm, 2 * C_DIM), _row_spec(tm, C_DIM), ANY, _full_spec((1, C_DIM)), _full_spec((1, C_DIM)),
                  _full_spec((C_GROUPS, CHUNK, CHUNK)), ANY],
        out_specs=[_row_spec(tm, D_MODEL), _row_spec(tm, D_MODEL), _row_spec(tm, 2 * C_DIM), _full_spec((1, D_MODEL)),
                   _full_spec((1, 2 * C_DIM)),
                   _full_spec((1, C_DIM)), _full_spec((1, C_DIM)), _full_spec((C_GROUPS, CHUNK, CHUNK)),
                   _full_spec((C_GROUPS, CHUNK))],
        out_shape=[jax.ShapeDtypeStruct((tokens, D_MODEL), F32), jax.ShapeDtypeStruct((tokens, D_MODEL), BF16),
                   jax.ShapeDtypeStruct((tokens, 2 * C_DIM), BF16),
                   row(D_MODEL), row(2 * C_DIM), row(C_DIM), row(C_DIM),
                   jax.ShapeDtypeStruct((C_GROUPS, CHUNK, CHUNK), F32), jax.ShapeDtypeStruct((C_GROUPS, CHUNK), F32)],
        scratch_shapes=[pltpu.VMEM((N_CHIPS, D_MODEL, cs), BF16), pltpu.VMEM((C_DIM, D_MODEL), BF16),
                        pltpu.VMEM((C_GROUPS, PAIR, PAIR), BF16), pltpu.VMEM((C_GROUPS, PAIR, PAIR), F32),
                        pltpu.VMEM((C_GROUPS, CHUNK, CHUNK), F32), pltpu.VMEM((tm, C_DIM), F32),
                        pltpu.SemaphoreType.DMA((N_LOADS,))],
        vmem_mib=56, riders=riders)


def _bwd_even(dh, x, norm_g, z, a2, cv, w_in, conv_a_w, ln_g, ln_b, conv_b_w, w_out, *, tm, seq, riders=()):
    tokens = x.shape[0]
    nt, tps = tokens // tm, seq // tm
    ws = IN_EVEN // N_CHIPS

    def body(dh_ref, x_ref, g_ref, z_ref, a2_ref, cv_ref, win_hbm, caw_ref, lng_ref, lnb_ref, cbw_ref, wout_hbm,
             dx_ref, dz_ref, dg_ref, dcaw_ref, dcab_ref, dlng_ref, dlnb_ref, dcbw_ref,
             win_v, wout_v, ea, eb, a1s, da1s, sigs, wide, dw_acc, sem):
        i = pl.program_id(0)

        _load_weights([(win_hbm, win_v, False), (wout_hbm, wout_v, True)], sem)

        @pl.when(i == 0)
        def _():
            dw_acc[...] = jnp.zeros(dw_acc.shape, F32)
            for ref in (dg_ref, dcab_ref, dlng_ref, dlnb_ref, dcbw_ref):
                ref[...] = jnp.zeros(ref.shape, F32)

        last = ((nt - 1 - i) % tps) == tps - 1

        @pl.when(last)
        def _():
            ea[0, tm:tm + A_HALO, :] = jnp.zeros((A_HALO, A_DIM), F32)
            eb[tm:tm + B_HALO, :] = jnp.zeros((B_HALO, B_DIM), F32)

        @pl.when(jnp.logical_not(last))
        def _():
            ea[0, tm:tm + A_HALO, :] = ea[0, 0:A_HALO, :]
            eb[tm:tm + B_HALO, :] = eb[0:B_HALO, :]

        wide[...] = _dot_nt(dh_ref[...].astype(BF16), wout_v[...])
        lng, lnb = lng_ref[...], lnb_ref[...]
        zero_row = jnp.zeros((1, A_DIM), F32)
        dlng, dlnb, dcab = zero_row, zero_row, zero_row
        for r0 in range(0, tm, ELEM_ROWS):
            rows = slice(r0, r0 + ELEM_ROWS)
            a_val, a_gate = z_ref[rows, 0:A_DIM].astype(F32), z_ref[rows, A_DIM:2 * A_DIM].astype(F32)
            xhat, rs = _ln_stats(a2_ref[rows, :])
            a3 = xhat * lng + lnb
            sg = jax.nn.sigmoid(a3)
            da3 = wide[rows, 0:A_DIM] * (sg * (1.0 + a3 * (1.0 - sg)))
            da2, g_part, b_part = _ln_bwd(da3, xhat, rs, lng)
            dlng, dlnb, dcab = dlng + g_part, dlnb + b_part, dcab + jnp.sum(da2, axis=0, keepdims=True)
            ea[0, rows, :] = da2
            eb[rows, :] = wide[rows, A_DIM:A_DIM + B_DIM] * z_ref[rows, 1024:1536].astype(F32)
            sig = jax.nn.sigmoid(a_gate)
            sigs[rows, :] = sig
            a1s[rows, :] = a_val * sig
        dlng_ref[...] += dlng
        dlnb_ref[...] += dlnb
        dcab_ref[...] += dcab
        _fill_shifted(ea, tm + A_HALO)
        for r0 in range(0, tm, CONV_ROWS):
            acc = jnp.zeros((CONV_ROWS, A_DIM), F32)
            for j in range(A_CONV_WIDTH):
                acc = acc + caw_ref[A_CONV_WIDTH - 1 - j:A_CONV_WIDTH - j, :] * _window(ea, r0 + j, CONV_ROWS)
            da1s[r0:r0 + CONV_ROWS, :] = acc
        for j0 in range(0, A_CONV_WIDTH, DW_TAPS):
            taps = range(j0, min(j0 + DW_TAPS, A_CONV_WIDTH))
            part = [jnp.zeros((CONV_ROWS, A_DIM), F32) for _ in taps]
            for r0 in range(0, tm, CONV_ROWS):
                a1c = a1s[r0:r0 + CONV_ROWS, :]
                for u, j in enumerate(taps):
                    part[u] = part[u] + _window(ea, r0 + j, CONV_ROWS) * a1c
            for u, j in enumerate(taps):
                dw_acc[A_CONV_WIDTH - 1 - j] += part[u]
        dcbw = [jnp.zeros((1, B_DIM), F32) for _ in range(B_CONV_WIDTH)]
        for r0 in range(0, tm, ELEM_ROWS):
            rows = slice(r0, r0 + ELEM_ROWS)
            da1, sig = da1s[rows, :], sigs[rows, :]
            dz_ref[rows, 0:A_DIM] = (da1 * sig).astype(BF16)
            dz_ref[rows, A_DIM:2 * A_DIM] = (da1 * z_ref[rows, 0:A_DIM].astype(F32) * (sig * (1.0 - sig))).astype(BF16)
            c_gate, b_val = z_ref[rows, 1536:2048].astype(F32), z_ref[rows, 2048:2560].astype(F32)
            dz_ref[rows, 1024:1536] = (wide[rows, A_DIM:A_DIM + B_DIM] * cv_ref[rows, :].astype(F32)).astype(BF16)
            cb = c_gate * b_val
            dcb = jnp.zeros((ELEM_ROWS, B_DIM), F32)
            for j in range(B_CONV_WIDTH):
                k = B_CONV_WIDTH - 1 - j
                sl = eb[r0 + j:r0 + j + ELEM_ROWS, :]
                dcb = dcb + cbw_ref[k:k + 1, :] * sl
                dcbw[k] = dcbw[k] + jnp.sum(sl * cb, axis=0, keepdims=True)
            dz_ref[rows, 1536:2048] = (dcb * b_val).astype(BF16)
            dz_ref[rows, 2048:2560] = (dcb * c_gate).astype(BF16)
        for k in range(B_CONV_WIDTH):
            dcbw_ref[k:k + 1, :] += dcbw[k]
        dn = jnp.zeros((tm, D_MODEL), F32)
        for j in range(N_CHIPS):
            dn = dn + _dot_nt(dz_ref[:, j * ws:(j + 1) * ws], win_v[j])
        wide[...] = dn
        g = g_ref[...]
        dg = jnp.zeros((1, D_MODEL), F32)
        for r0 in range(0, tm, ELEM_ROWS):
            rows = slice(r0, r0 + ELEM_ROWS)
            xv = x_ref[rows, :]
            _, rstd = _rms_fwd(xv, g)
            dx, dg_part = _rms_bwd(wide[rows, :], xv, rstd, g)
            dx_ref[rows, :] = dh_ref[rows, :] + dx
            dg = dg + dg_part
        dg_ref[...] += dg

        @pl.when(i == nt - 1)
        def _():
            for k in range(A_CONV_WIDTH):
                dcaw_ref[k:k + 1, :] = jnp.sum(dw_acc[k], axis=0, keepdims=True)

    row = lambda cols: jax.ShapeDtypeStruct((1, cols), F32)
    rs_ = functools.partial(_row_spec, rev_nt=nt)
    return _pallas(
        body, [dh, x, norm_g, z, a2, cv, w_in, conv_a_w, ln_g, ln_b, conv_b_w, w_out], name="bwd_even", grid=(nt,),
        in_specs=[rs_(tm, D_MODEL), rs_(tm, D_MODEL), _full_spec((1, D_MODEL)), rs_(tm, IN_EVEN), rs_(tm, A_DIM),
                  rs_(tm, B_DIM), ANY, _full_spec((A_CONV_WIDTH, A_DIM)), _full_spec((1, A_DIM)), _full_spec((1, A_DIM)),
                  _full_spec((B_CONV_WIDTH, B_DIM)), ANY],
        out_specs=[rs_(tm, D_MODEL), rs_(tm, IN_EVEN), _full_spec((1, D_MODEL)), _full_spec((A_CONV_WIDTH, A_DIM)),
                   _full_spec((1, A_DIM)), _full_spec((1, A_DIM)), _full_spec((1, A_DIM)), _full_spec((B_CONV_WIDTH, B_DIM))],
        out_shape=[jax.ShapeDtypeStruct((tokens, D_MODEL), F32), jax.ShapeDtypeStruct((tokens, IN_EVEN), BF16),
                   row(D_MODEL), jax.ShapeDtypeStruct((A_CONV_WIDTH, A_DIM), F32), row(A_DIM), row(A_DIM), row(A_DIM),
                   jax.ShapeDtypeStruct((B_CONV_WIDTH, B_DIM), F32)],
        scratch_shapes=[pltpu.VMEM((N_CHIPS, D_MODEL, ws), BF16), pltpu.VMEM((D_MODEL, D_MODEL), BF16),
                        pltpu.VMEM((SUBLANES, tm + A_HALO, A_DIM), F32), pltpu.VMEM((tm + B_HALO, B_DIM), F32),
                        pltpu.VMEM((tm, A_DIM), F32), pltpu.VMEM((tm, A_DIM), F32), pltpu.VMEM((tm, A_DIM), F32),
                        pltpu.VMEM((tm, D_MODEL), F32),
                        pltpu.VMEM((A_CONV_WIDTH, CONV_ROWS, A_DIM), F32), pltpu.SemaphoreType.DMA((N_LOADS,))],
        vmem_mib=56, riders=riders)


def _wgrad(a, b, name, *, col_shards, riders=()):
    tokens, m = a.shape
    n = b.shape[1]
    kc = 512
    if col_shards:
        bm, bn = m // 2, n // N_CHIPS
        grid = (2, N_CHIPS)
        out_spec = pl.BlockSpec((None, None, bm, bn), lambda i, j: (j, i, 0, 0))
    elif m // 8 >= MXU_ROWS:
        bm, bn = m // 8, n
        grid = (8, 1)
        out_spec = pl.BlockSpec((None, None, bm, bn), lambda i, j: (i // 2, i % 2, 0, 0))
    else:
        bm, bn = m // N_CHIPS, n
        grid = (N_CHIPS, 1)
        out_spec = pl.BlockSpec((None, 2, bm // 2, bn), lambda i, j: (i, 0, 0, 0))

    def body(a_ref, b_ref, o_ref):
        acc = jnp.zeros((bm, bn), F32)
        for k0 in range(0, tokens, kc):
            acc = acc + _dot_tn(a_ref[k0:k0 + kc, :].astype(BF16), b_ref[k0:k0 + kc, :].astype(BF16))
        if len(o_ref.shape) == 3:
            o_ref[0] = acc[0:bm // 2]
            o_ref[1] = acc[bm // 2:bm]
        else:
            o_ref[...] = acc

    out_rows = m // 2 if col_shards else m // 8
    outs, routs = _pallas(
        body, [a, b], name=name, grid=grid,
        in_specs=[pl.BlockSpec((tokens, bm), lambda i, j: (0, i)), pl.BlockSpec((tokens, bn), lambda i, j: (0, j))],
        out_specs=[out_spec], out_shape=[jax.ShapeDtypeStruct((N_CHIPS, 2, out_rows, bn), F32)],
        vmem_mib=56, riders=riders)
    return outs[0], routs


def _wgrad_pair(a, b, name, *, col_shards, riders=()):
    tokens, m = a.shape
    n = b.shape[1]
    kc = 512
    c0 = lax.axis_index("c")

    def half(ph, pre):
        return (ph + 1 + pre[0]) % 2

    if col_shards:
        bm, bn = m // 2, n // N_CHIPS
        a_spec = pl.BlockSpec((tokens, bm), lambda ph, q, pre: (0, half(ph, pre)))
        b_spec = pl.BlockSpec((tokens, bn), lambda ph, q, pre: (0, q))
    else:
        bm, bn = m // 8, n
        a_spec = pl.BlockSpec((tokens, bm), lambda ph, q, pre: (0, 2 * q + half(ph, pre)))
        b_spec = pl.BlockSpec((tokens, bn), lambda ph, q, pre: (0, 0))

    def body(pre_ref, a_ref, b_ref, o_ref, give, got, send_sems, recv_sems):
        ph, q = pl.program_id(0), pl.program_id(1)
        acc = jnp.zeros((bm, bn), F32)
        for k0 in range(0, tokens, kc):
            acc = acc + _dot_tn(a_ref[k0:k0 + kc, :].astype(BF16), b_ref[k0:k0 + kc, :].astype(BF16))
        x, y, cc = _mesh_pos()

        def tile(t):
            return _remote(give.at[t], got.at[t], send_sems.at[t], recv_sems.at[t], (x, y, 1 - cc))

        @pl.when(ph == 0)
        def _():
            give[q] = acc
            tile(q).start()

        @pl.when(ph == 1)
        def _():
            tile(q).wait_recv()
            o_ref[...] = (acc + got[q]).astype(BF16)

        @pl.when((ph == 1) & (q == N_CHIPS - 1))
        def _():
            for t in range(N_CHIPS):
                tile(t).wait_send()

    outs, routs = _pallas(
        body, [a, b], name=name, grid=(2, N_CHIPS), in_specs=[a_spec, b_spec],
        out_specs=[pl.BlockSpec((None, bm, bn), lambda ph, q, pre: (ph * q, 0, 0))],
        out_shape=[jax.ShapeDtypeStruct((N_CHIPS, bm, bn), BF16)],
        scratch_shapes=[pltpu.VMEM((N_CHIPS, bm, bn), F32), pltpu.VMEM((N_CHIPS, bm, bn), F32),
                        pltpu.SemaphoreType.DMA((N_CHIPS,)), pltpu.SemaphoreType.DMA((N_CHIPS,))],
        vmem_mib=56, riders=riders, prefetch=jnp.reshape(c0, (1,)).astype(jnp.int32))
    return outs[0], routs


class _GradReduce:
    def __init__(self, name, grad=None, chip_sum=None):
        self.name, self.grad, self.chip_sum = name, grad, chip_sum
        self.full = None

    def pair_swap(self):
        return _PairSwap([self.grad])

    def took_pair(self, outs):
        self.chip_sum = _in_hbm(_add_pair(self.grad, outs[0], f"pair_sum_{self.name}"))

    def took_chips(self, outs):
        self.full = _in_hbm(_add_chips(self.chip_sum, outs[0], f"chip_sum_{self.name}"))

    def chips_beside(self, collective_id):
        self.took_chips([_chip_swap_beside(self.chip_sum, f"chip_swap_{self.name}", collective_id)])

    def pair_share(self):
        return _PairShare([self.full])

    def took_share(self, outs):
        self.full = outs[0]

    def reduced(self):
        return jnp.reshape(self.full, (2 * self.full.shape[1], self.full.shape[2]))


def _forward_backward(x2, tgt2, w, conv_a_w, conv_b_w, od_norm, od_bias, od_lng, od_lnb,
                      ev_norm_g, ev_conv_a_b, ev_ln_a_g, ev_ln_a_b, od_w_s, od_b_s, mlp_norm_g, final_norm_g,
                      *, tm, seq, distributed=True):
    d = x2.shape[1]
    b_s_rows = jnp.broadcast_to(od_b_s[0][:, :, None], (C_GROUPS, CHUNK, CHUNK))
    (h1, n0, z, a2, cv, mix), _ = _fwd_even(
        x2, ev_norm_g, w["ev_in"], conv_a_w, ev_conv_a_b, ev_ln_a_g, ev_ln_a_b, conv_b_w, w["ev_out"], tm=tm, seq=seq)
    (h2, n1, p0, q0), _ = _fwd_mlp(h1, mlp_norm_g[0:1], w["w1_0"], w["w2_0"], 0, tm=tm)
    (h3, n2, s, cdf, sv, y), _ = _fwd_odd(h2, od_norm, w["od_in"], od_bias, od_lng, od_lnb, od_w_s[0], b_s_rows,
                                          w["od_out"], tm=tm)
    (n3, p1, q1, loss_part, dh4, dh4b, d_final_g), _ = _fwd_mlp(
        h3, mlp_norm_g[1:2], w["w1_1"], w["w2_1"], 1, tm=tm,
        head=(jnp.reshape(final_norm_g, (1, d)), tgt2))

    red = {}

    def swap(*names):
        return [red[nm].pair_swap() for nm in names] if distributed else []

    def share(*names):
        return [red[nm].pair_share() for nm in names] if distributed else []

    def took(routs, *steps):
        if distributed:
            for (nm, what), outs in zip(steps, routs):
                getattr(red[nm], what)(outs)

    swap_ids = iter(range(FIRST_SWAP_ID, FIRST_SWAP_ID + 8))

    def beside(name):
        if distributed:
            red[name].chips_beside(next(swap_ids))

    def big(lhs, rhs, name, col_shards, riders=()):
        if distributed:
            chip_sum, routs = _wgrad_pair(lhs, rhs, f"wgrad_{name}", col_shards=col_shards, riders=riders)
            red[name] = _GradReduce(name, chip_sum=_in_hbm(chip_sum))
        else:
            g, routs = _wgrad(lhs, rhs, f"wgrad_{name}", col_shards=col_shards)
            red[name] = _GradReduce(name, grad=g)
        return routs

    big(q1, dh4b, "w2_1", False)
    beside("w2_1")
    (dh3, dh3b, dp1, d_mlp_g1), _ = _bwd_mlp(dh4, h3, mlp_norm_g[1:2], p1, w["w1_1"], w["w2_1"], 1, tm=tm)
    big(n3, dp1, "w1_1", True)
    beside("w1_1")
    g, routs = _wgrad(y, dh3b, "wgrad_od_out", col_shards=False, riders=share("w2_1"))
    red["od_out"] = _GradReduce("od_out", grad=g)
    took(routs, ("w2_1", "took_share"))
    (dh2, dh2b, ds, d_od_norm, d_od_bin, d_od_lng, d_od_lnb, d_ws, d_bs), _ = _bwd_odd(
        dh3, h2, od_norm, s, cdf, sv, w["od_in"], od_lng, od_lnb, od_w_s[0], w["od_out"], tm=tm)
    routs = big(n2, ds, "od_in", True, riders=share("w1_1") + swap("od_out"))
    took(routs, ("w1_1", "took_share"), ("od_out", "took_pair"))
    beside("od_in")
    beside("od_out")
    half_groups = C_GROUPS // 2
    early = {"loss": loss_part, "od_w_s_lo": d_ws[:half_groups], "od_b_s": d_bs, "mlp_norm_g1": d_mlp_g1, "final_norm_g": d_final_g,
             "od_norm_g": d_od_norm, "od_b_in": d_od_bin, "od_ln_v_g": d_od_lng, "od_ln_v_b": d_od_lnb}
    share_early = [_ShareAll(list(early.values()))] if distributed else []
    routs = big(q0, dh2b, "w2_0", False, riders=share_early)
    landed_early = routs[0] if distributed else []
    beside("w2_0")
    (dh1, dh1b, dp0, d_mlp_g0), _ = _bwd_mlp(dh2, h1, mlp_norm_g[0:1], p0, w["w1_0"], w["w2_0"], 0, tm=tm)
    middle = {"od_w_s_hi": d_ws[half_groups:]}
    share_middle = [_ShareAll(list(middle.values()))] if distributed else []
    g, _ = _wgrad(mix, dh1b, "wgrad_ev_out", col_shards=False)
    red["ev_out"] = _GradReduce("ev_out", grad=g)
    routs = big(n1, dp0, "w1_0", True,
                riders=share("od_out") + share("od_in") + share("w2_0") + swap("ev_out") + share_middle)
    took(routs, ("od_out", "took_share"), ("od_in", "took_share"), ("w2_0", "took_share"), ("ev_out", "took_pair"))
    landed_middle = routs[4] if distributed else []
    beside("w1_0")
    beside("ev_out")

    (dx, dz, d_ev_norm, d_caw, d_cab, d_ev_lng, d_ev_lnb, d_cbw), _ = _bwd_even(
        dh1, x2, ev_norm_g, z, a2, cv, w["ev_in"], conv_a_w, ev_ln_a_g, ev_ln_a_b, conv_b_w, w["ev_out"], tm=tm, seq=seq)
    late = {"mlp_norm_g0": d_mlp_g0, "ev_norm_g": d_ev_norm, "ev_conv_a_b": d_cab, "ev_ln_a_g": d_ev_lng,
            "ev_ln_a_b": d_ev_lnb, "ev_conv_a_w": d_caw, "ev_conv_b_w": d_cbw}
    share_late = [_ShareAll(list(late.values()))] if distributed else []
    routs = big(n0, dz, "ev_in", True, riders=share("ev_out") + share("w1_0") + share_late)
    took(routs, ("ev_out", "took_share"), ("w1_0", "took_share"))
    beside("ev_in")
    own = {**early, **middle, **late}
    landed = dict(zip(own.keys(), landed_early + landed_middle + routs[2])) if distributed else None
    return dx, red, own, landed


def _rows128(a):
    rows = jnp.reshape(a, (-1, LANES))
    pad = (-rows.shape[0]) % SUBLANES
    return jnp.pad(rows, ((0, pad), (0, 0))) if pad else rows


def _pack(arrays):
    return jnp.concatenate([_rows128(a) for a in arrays], axis=0)


def _unpack(buf, shapes):
    out, r0 = [], 0
    for shp in shapes:
        size = 1
        for dim in shp:
            size *= dim
        nr = size // LANES
        out.append(jnp.reshape(buf[r0:r0 + nr], shp))
        r0 += nr + (-nr) % SUBLANES
    return out


def kernel(x, ev_norm_g, ev_w_in, ev_conv_a_w, ev_conv_a_b, ev_ln_a_g, ev_ln_a_b, ev_conv_b_w, ev_w_out, od_norm_g, od_w_in, od_b_in, od_ln_v_g, od_ln_v_b, od_w_s, od_b_s, od_w_out, mlp_norm_g, mlp_w1, mlp_w2, final_norm_g, loss_target, m_ev_norm_g, m_ev_w_in, m_ev_conv_a_w, m_ev_conv_a_b, m_ev_ln_a_g, m_ev_ln_a_b, m_ev_conv_b_w, m_ev_w_out, m_od_norm_g, m_od_w_in, m_od_b_in, m_od_ln_v_g, m_od_ln_v_b, m_od_w_s, m_od_b_s, m_od_w_out, m_mlp_norm_g, m_mlp_w1, m_mlp_w2, m_final_norm_g, v_ev_norm_g, v_ev_w_in, v_ev_conv_a_w, v_ev_conv_a_b, v_ev_ln_a_g, v_ev_ln_a_b, v_ev_conv_b_w, v_ev_w_out, v_od_norm_g, v_od_w_in, v_od_b_in, v_od_ln_v_g, v_od_ln_v_b, v_od_w_s, v_od_b_s, v_od_w_out, v_mlp_norm_g, v_mlp_w1, v_mlp_w2, v_final_norm_g):
    tm = TOKEN_TILE
    batch, seq, d = x.shape
    tokens = batch * seq
    x2 = jnp.reshape(x, (tokens, d))
    tgt2 = jnp.reshape(loss_target, (tokens, d))
    chip = 2 * lax.axis_index("x") + lax.axis_index("y")

    small_shapes = [(A_CONV_WIDTH, LANES), (B_CONV_WIDTH, LANES), (256,), (512,), (256,), (256,)]
    small_shard = _pack([ev_conv_a_w[0], ev_conv_b_w[0], od_norm_g[0], od_b_in[0], od_ln_v_g[0], od_ln_v_b[0]])
    small_shard = jnp.pad(small_shard, ((0, (-small_shard.shape[0]) % (4 * SUBLANES)), (0, 0)))
    first = [_place_shard(ev_w_in, 0, BF16, "place_ev_w_in"), _place_shard(ev_w_out, 0, BF16, "place_ev_w_out"),
             _place_shard(small_shard[None], 0, F32, "place_small")]
    staged = {
        "w1_0": _place_shard(mlp_w1, 0, BF16, "place_w1_0"), "w2_0": _place_shard(mlp_w2, 0, BF16, "place_w2_0"),
        "od_in": _place_shard(od_w_in, 0, BF16, "place_od_w_in"), "od_out": _place_shard(od_w_out, 0, BF16, "place_od_w_out"),
        "w1_1": _place_shard(mlp_w1, 1, BF16, "place_w1_1"), "w2_1": _place_shard(mlp_w2, 1, BF16, "place_w2_1"),
    }
    first = [_in_hbm(a) for a in first]
    staged = {nm: _in_hbm(a) for nm, a in staged.items()}
    g_ev_in, g_ev_out, g_small = _gather_beside(first, "gather_stage0", collective_id=1)
    gathered = {"ev_in": g_ev_in, "ev_out": g_ev_out}
    for stage, names in enumerate((("w1_0", "w2_0"), ("od_in", "od_out", "w1_1"), ("w2_1",))):
        done = _gather_beside([staged[nm] for nm in names], f"gather_stage{stage + 1}", collective_id=stage + 2)
        gathered.update(zip(names, done))
    small_all = jnp.reshape(_plain_copy(g_small, "small_weights_copy"), (N_CHIPS, -1, LANES))
    per_chip = [_unpack(small_all[q], small_shapes) for q in range(N_CHIPS)]
    conv_a_w = jnp.concatenate([pc[0] for pc in per_chip], axis=1)
    conv_b_w = jnp.concatenate([pc[1] for pc in per_chip], axis=1)
    od_norm = jnp.concatenate([pc[2] for pc in per_chip])[None, :]
    od_bias = jnp.concatenate([pc[3] for pc in per_chip])[None, :]
    od_lng = jnp.concatenate([pc[4] for pc in per_chip])[None, :]
    od_lnb = jnp.concatenate([pc[5] for pc in per_chip])[None, :]

    dx, red, own, landed = _forward_backward(
        x2, tgt2, gathered, conv_a_w, conv_b_w, od_norm, od_bias, od_lng, od_lnb,
        ev_norm_g, ev_conv_a_b, ev_ln_a_g, ev_ln_a_b, od_w_s, od_b_s, mlp_norm_g, final_norm_g, tm=tm, seq=seq)

    routs = _exchange([red["ev_in"].pair_share()], "reduce_tail")
    red["ev_in"].took_share(routs[0])

    given = {"ev_norm_g": (ev_norm_g, m_ev_norm_g, v_ev_norm_g), "ev_conv_a_b": (ev_conv_a_b, m_ev_conv_a_b, v_ev_conv_a_b),
             "ev_ln_a_g": (ev_ln_a_g, m_ev_ln_a_g, v_ev_ln_a_g), "ev_ln_a_b": (ev_ln_a_b, m_ev_ln_a_b, v_ev_ln_a_b),
             "od_w_s": (od_w_s, m_od_w_s, v_od_w_s), "od_b_s": (od_b_s, m_od_b_s, v_od_b_s),
             "mlp_norm_g": (mlp_norm_g, m_mlp_norm_g, v_mlp_norm_g), "final_norm_g": (final_norm_g, m_final_norm_g, v_final_norm_g),
             "ev_conv_a_w": (ev_conv_a_w, m_ev_conv_a_w, v_ev_conv_a_w), "ev_conv_b_w": (ev_conv_b_w, m_ev_conv_b_w, v_ev_conv_b_w),
             "od_norm_g": (od_norm_g, m_od_norm_g, v_od_norm_g), "od_b_in": (od_b_in, m_od_b_in, v_od_b_in),
             "od_ln_v_g": (od_ln_v_g, m_od_ln_v_g, v_od_ln_v_g), "od_ln_v_b": (od_ln_v_b, m_od_ln_v_b, v_od_ln_v_b)}
    shaped = {nm: tuple(jnp.reshape(a, shape) for a in given[nm]) for nm, shape, _, _ in SMALL_WEIGHTS}
    loss11, small_upd = _small_update(own, landed, shaped)
    loss = loss11[0, 0]
    upd = {nm: [jnp.reshape(o, given[nm][0].shape) for o in outs] for nm, outs in small_upd.items()}

    def big_update(wt, m, v, names, call):
        grads = [red[nm].reduced() for nm in names]
        shp3 = (len(grads),) + grads[0].shape
        outs, _ = _adamw(jnp.reshape(wt, shp3), jnp.reshape(m, shp3), jnp.reshape(v, shp3), grads, call)
        return [jnp.reshape(o, wt.shape) for o in outs], None

    upd["mlp_w2"], _ = big_update(mlp_w2, m_mlp_w2, v_mlp_w2, ["w2_0", "w2_1"], "adamw_mlp_w2")
    upd["mlp_w1"], _ = big_update(mlp_w1, m_mlp_w1, v_mlp_w1, ["w1_0", "w1_1"], "adamw_mlp_w1")
    upd["ev_w_in"], _ = big_update(ev_w_in, m_ev_w_in, v_ev_w_in, ["ev_in"], "adamw_ev_w_in")
    upd["ev_w_out"], _ = big_update(ev_w_out, m_ev_w_out, v_ev_w_out, ["ev_out"], "adamw_ev_w_out")
    upd["od_w_in"], _ = big_update(od_w_in, m_od_w_in, v_od_w_in, ["od_in"], "adamw_od_w_in")
    upd["od_w_out"], _ = big_update(od_w_out, m_od_w_out, v_od_w_out, ["od_out"], "adamw_od_w_out")

    order = ["ev_norm_g", "ev_w_in", "ev_conv_a_w", "ev_conv_a_b", "ev_ln_a_g", "ev_ln_a_b", "ev_conv_b_w", "ev_w_out",
             "od_norm_g", "od_w_in", "od_b_in", "od_ln_v_g", "od_ln_v_b", "od_w_s", "od_b_s", "od_w_out", "mlp_norm_g",
             "mlp_w1", "mlp_w2", "final_norm_g"]
    grad_x = jnp.reshape(dx, x.shape)
    return (loss, grad_x, *[upd[nm][0] for nm in order], *[upd[nm][1] for nm in order],
            *[upd[nm][2] for nm in order], *[upd[nm][3] for nm in order])
```

```python
import functools

import jax
import jax.numpy as jnp
from jax import lax
from jax.experimental import pallas as pl
from jax.experimental.pallas import tpu as pltpu
from jax.experimental.pallas import tpu_sc as plsc

F32 = jnp.float32
BF16 = jnp.bfloat16

D_MODEL = 1024
A_DIM = 512
B_DIM = 512
IN_EVEN = 2 * A_DIM + 3 * B_DIM
A_CONV_WIDTH = 31
B_CONV_WIDTH = 3
CHUNK = 128
C_GROUPS = 8
C_DIM = 1024
D_FF = 4096
RMS_EPS = 1e-6
LN_EPS = 1e-5
ADAM_LR = 0.001
ADAM_B1 = 0.9
ADAM_B2 = 0.999
ADAM_EPS = 1e-08
ADAM_WD = 0.01
ADAM_STEP = 10

N_CHIPS = 4
N_DEV = 8
TOKEN_TILE = 512
A_HALO = 32
B_HALO = 8
CONV_ROWS = 16
DW_TAPS = 4
ELEM_ROWS = 16
PAIR = 2 * CHUNK
LANES = 128
SUBLANES = 8
MXU_ROWS = 256
MIB = 1024 * 1024
MESH = pl.DeviceIdType.MESH
ANY = pl.BlockSpec(memory_space=pl.ANY)


def _dot(a, b):
    return lax.dot_general(a, b, (((1,), (0,)), ((), ())), preferred_element_type=F32)


def _dot_nt(a, b):
    return lax.dot_general(a, b, (((1,), (1,)), ((), ())), preferred_element_type=F32)


def _dot_tn(a, b):
    return lax.dot_general(a, b, (((0,), (0,)), ((), ())), preferred_element_type=F32)


def _params(vmem_mib, n_axes=1):
    return pltpu.CompilerParams(dimension_semantics=("arbitrary",) * n_axes, vmem_limit_bytes=vmem_mib * MIB)


def _row_spec(tm, cols, rev_nt=None):
    if rev_nt is None:
        return pl.BlockSpec((tm, cols), lambda i: (i, 0))
    return pl.BlockSpec((tm, cols), lambda i: (rev_nt - 1 - i, 0))


def _full_spec(shape):
    nd = len(shape)
    return pl.BlockSpec(shape, lambda i: (0,) * nd)


def _block_rows(rows, cap=512):
    best = SUBLANES
    for br in range(SUBLANES, min(rows, cap) + 1, SUBLANES):
        if rows % br == 0:
            best = br
    return best


FIRST_SWAP_ID = 5
N_LOADS = 2 * 2 * N_CHIPS


def _load_weights(loads, sems):
    @pl.when(pl.program_id(0) == 0)
    def _():
        copies = []
        for src, dst, rows_of_one in loads:
            r = src.shape[2]
            for q in range(N_CHIPS):
                for h in range(2):
                    part = dst.at[pl.ds((2 * q + h) * r, r)] if rows_of_one else dst.at[q, pl.ds(h * r, r)]
                    copies.append(pltpu.make_async_copy(src.at[q, h], part, sems.at[len(copies)]))
        for cp in copies:
            cp.start()
        for cp in copies:
            cp.wait()


def _rms_fwd(x, g):
    rstd = lax.rsqrt(jnp.mean(x * x, axis=-1, keepdims=True) + RMS_EPS)
    return x * rstd * g, rstd


def _rms_bwd(dn, x, rstd, g):
    a = dn * g
    xh = x * rstd
    dx = rstd * (a - xh * jnp.mean(a * xh, axis=-1, keepdims=True))
    dg = jnp.sum(dn * xh, axis=0, keepdims=True)
    return dx, dg


def _ln_stats(v):
    mu = jnp.mean(v, axis=-1, keepdims=True)
    xc = v - mu
    rs = lax.rsqrt(jnp.mean(xc * xc, axis=-1, keepdims=True) + LN_EPS)
    return xc * rs, rs


def _ln_bwd(dy, xhat, rs, g):
    dxh = dy * g
    dv = rs * (dxh - jnp.mean(dxh, axis=-1, keepdims=True) - xhat * jnp.mean(dxh * xhat, axis=-1, keepdims=True))
    return dv, jnp.sum(dy * xhat, axis=0, keepdims=True), jnp.sum(dy, axis=0, keepdims=True)


def _gelu_cdf(s):
    return 0.5 * (1.0 + lax.erf(s * 0.7071067811865476))


def _mesh_pos():
    return lax.axis_index("x"), lax.axis_index("y"), lax.axis_index("c")


def _other_chips(x, y):
    return [(1 - x, y), (x, 1 - y), (1 - x, 1 - y)]


def _remote(src, dst, send_sem, recv_sem, to):
    return pltpu.make_async_remote_copy(src_ref=src, dst_ref=dst, send_sem=send_sem, recv_sem=recv_sem,
                                        device_id=to, device_id_type=MESH)


def _like(arrays):
    return [jax.ShapeDtypeStruct(a.shape, a.dtype) for a in arrays]


class _PairSwap:
    def __init__(self, grads):
        self.ins = list(grads)
        self.out_shapes = [jax.ShapeDtypeStruct((g.shape[0],) + g.shape[2:], g.dtype) for g in grads]
        self.aliases = {}
        self.n_sems = len(grads)

    def _copies(self, ins, outs, send, recv):
        x, y, c = _mesh_pos()
        return [_remote(ins[t].at[:, 1 - c], outs[t], send.at[t], recv.at[t], (x, y, 1 - c)) for t in range(len(ins))]

    def start(self, ins, outs, send, recv):
        for cp in self._copies(ins, outs, send, recv):
            cp.start()

    def finish(self, ins, outs, send, recv):
        for cp in self._copies(ins, outs, send, recv):
            cp.wait()


class _ChipSwap:
    def __init__(self, parts):
        self.ins = list(parts)
        self.out_shapes = [jax.ShapeDtypeStruct((3,) + p.shape[1:], p.dtype) for p in parts]
        self.aliases = {}
        self.n_sems = 3 * len(parts)

    def _copies(self, ins, outs, send, recv):
        x, y, c = _mesh_pos()
        return [_remote(ins[t].at[2 * chip[0] + chip[1]], outs[t].at[k], send.at[3 * t + k], recv.at[3 * t + k], (*chip, c))
                for t in range(len(ins)) for k, chip in enumerate(_other_chips(x, y))]

    def start(self, ins, outs, send, recv):
        for cp in self._copies(ins, outs, send, recv):
            cp.start()

    def finish(self, ins, outs, send, recv):
        for cp in self._copies(ins, outs, send, recv):
            cp.wait()


class _PairShare:
    def __init__(self, fulls):
        self.ins = list(fulls)
        self.out_shapes = _like(fulls)
        self.aliases = {t: t for t in range(len(fulls))}
        self.n_sems = len(fulls)

    def _copies(self, ins, outs, send, recv):
        x, y, c = _mesh_pos()
        return [_remote(ins[t].at[c], outs[t].at[c], send.at[t], recv.at[t], (x, y, 1 - c)) for t in range(len(ins))]

    def start(self, ins, outs, send, recv):
        for cp in self._copies(ins, outs, send, recv):
            cp.start()

    def finish(self, ins, outs, send, recv):
        for cp in self._copies(ins, outs, send, recv):
            cp.wait()


class _ShareAll:
    def __init__(self, arrays):
        self.ins = list(arrays)
        self.out_shapes = [jax.ShapeDtypeStruct((N_DEV,) + a.shape, a.dtype) for a in arrays]
        self.aliases = {}
        self.n_sems = (N_DEV - 1) * len(arrays)

    def _peers(self):
        x, y, c = _mesh_pos()
        flips = [((r >> 2) & 1, (r >> 1) & 1, r & 1) for r in range(1, N_DEV)]
        return (x, y, c), [(x ^ fx, y ^ fy, c ^ fc) for fx, fy, fc in flips]

    def _sends(self, ins, outs, send, recv):
        (x, y, c), peers = self._peers()
        mine = 4 * x + 2 * y + c
        return [_remote(ins[a], outs[a].at[mine], send.at[7 * a + r], recv.at[7 * a + r], peer)
                for a in range(len(ins)) for r, peer in enumerate(peers)]

    def start(self, ins, outs, send, recv):
        for cp in self._sends(ins, outs, send, recv):
            cp.start()

    def finish(self, ins, outs, send, recv):
        (x, y, c), peers = self._peers()
        for a in range(len(ins)):
            for r, (px, py, pc) in enumerate(peers):
                blk = outs[a].at[4 * px + 2 * py + pc]
                _remote(blk, blk, send.at[7 * a + r], recv.at[7 * a + r], (x, y, c)).wait_recv()
        for cp in self._sends(ins, outs, send, recv):
            cp.wait_send()


def _gather_beside(bufs, name, collective_id):
    n = len(bufs)
    per = 7
    refs = [jax.new_ref(b, memory_space=pltpu.MemorySpace.HBM) for b in bufs]

    @pl.kernel(mesh=plsc.ScalarSubcoreMesh(axis_name="sequencer", num_cores=1), name=name,
               scratch_types=(pltpu.SemaphoreType.DMA((per * n,)), pltpu.SemaphoreType.DMA((per * n,))),
               compiler_params=pltpu.CompilerParams(collective_id=collective_id))
    def launch(send, recv):
        x, y, c = _mesh_pos()
        me, sibling = (x, y, c), (x, y, 1 - c)
        x_nbr, y_nbr = (1 - x, y, c), (x, 1 - y, c)
        mine, via_x, via_y, diag = 2 * x + y, 2 * (1 - x) + y, 2 * x + (1 - y), 2 * (1 - x) + (1 - y)
        barrier = pltpu.get_barrier_semaphore()
        peers = [x_nbr, y_nbr, sibling]
        for peer in peers:
            pl.semaphore_signal(barrier, inc=1, device_id=peer, device_id_type=MESH)
        pl.semaphore_wait(barrier, len(peers))

        def copy(t, k, src, dst, to):
            return _remote(src, dst, send.at[per * t + k], recv.at[per * t + k], to)

        def piece(t, chip, half, rows=None):
            blk = refs[t].at[chip, half]
            return blk if rows is None else blk.at[rows]

        started = []

        def go(cp):
            cp.start()
            started.append(cp)

        upper = [pl.ds(0, r.shape[2] // 2) for r in refs]
        lower = [pl.ds(r.shape[2] // 2, r.shape[2] // 2) for r in refs]
        for t in range(n):
            go(copy(t, 0, piece(t, mine, c), piece(t, mine, c), x_nbr))
            go(copy(t, 1, piece(t, mine, c), piece(t, mine, c), y_nbr))
        for t in range(n):
            copy(t, 0, piece(t, via_x, c), piece(t, via_x, c), me).wait_recv()
            go(copy(t, 2, piece(t, via_x, c, upper[t]), piece(t, via_x, c, upper[t]), y_nbr))
            go(copy(t, 4, piece(t, via_x, c), piece(t, via_x, c), sibling))
            copy(t, 1, piece(t, via_y, c), piece(t, via_y, c), me).wait_recv()
            go(copy(t, 3, piece(t, via_y, c, lower[t]), piece(t, via_y, c, lower[t]), x_nbr))
            go(copy(t, 5, piece(t, via_y, c), piece(t, via_y, c), sibling))
        for t in range(n):
            copy(t, 2, piece(t, diag, c, upper[t]), piece(t, diag, c, upper[t]), me).wait_recv()
            copy(t, 3, piece(t, diag, c, lower[t]), piece(t, diag, c, lower[t]), me).wait_recv()
            go(copy(t, 6, piece(t, diag, c), piece(t, diag, c), sibling))
        for t in range(n):
            for k, chip in ((4, via_x), (5, via_y), (6, diag)):
                copy(t, k, piece(t, chip, 1 - c), piece(t, chip, 1 - c), me).wait_recv()
        for cp in started:
            cp.wait_send()

    launch()
    return [r[...] for r in refs]


def _chip_swap_beside(parts, name, collective_id):
    src = jax.new_ref(parts, memory_space=pltpu.MemorySpace.HBM)
    dst = jax.empty_ref(jax.ShapeDtypeStruct((N_CHIPS - 1,) + parts.shape[1:], parts.dtype),
                        memory_space=pltpu.MemorySpace.HBM)
    swap = _ChipSwap([parts])

    @pl.kernel(mesh=plsc.ScalarSubcoreMesh(axis_name="sequencer", num_cores=1), name=name,
               scratch_types=(pltpu.SemaphoreType.DMA((N_CHIPS - 1,)), pltpu.SemaphoreType.DMA((N_CHIPS - 1,))),
               compiler_params=pltpu.CompilerParams(collective_id=collective_id))
    def launch(send, recv):
        x, y, c = _mesh_pos()
        barrier = pltpu.get_barrier_semaphore()
        peers = [(*chip, c) for chip in _other_chips(x, y)]
        for peer in peers:
            pl.semaphore_signal(barrier, inc=1, device_id=peer, device_id_type=MESH)
        pl.semaphore_wait(barrier, len(peers))
        swap.start([src], [dst], send, recv)
        swap.finish([src], [dst], send, recv)

    launch()
    return dst[...]


def _pallas(body, operands, *, name, grid, in_specs, out_specs, out_shape, scratch_shapes=(), vmem_mib=32, riders=(),
            prefetch=None):
    in_specs, out_specs, out_shape, scratch_shapes = list(in_specs), list(out_specs), list(out_shape), list(scratch_shapes)
    if not riders and prefetch is None:
        outs = pl.pallas_call(body, name=name, grid=grid, in_specs=in_specs, out_specs=out_specs, out_shape=out_shape,
                              scratch_shapes=scratch_shapes, compiler_params=_params(vmem_mib, len(grid)))(*operands)
        return list(outs), []
    n_in, n_out, n_scr = len(in_specs), len(out_specs), len(scratch_shapes)
    r_in = [len(r.ins) for r in riders]
    r_out = [len(r.out_shapes) for r in riders]
    steps = 1
    for g in grid:
        steps *= g

    n_pre = 0 if prefetch is None else 1

    def wrapped(*refs):
        refs = list(refs)
        pre, refs = refs[:n_pre], refs[n_pre:]
        ins, refs = refs[:n_in], refs[n_in:]
        rins = []
        for k in r_in:
            rins.append(refs[:k])
            refs = refs[k:]
        outs, refs = refs[:n_out], refs[n_out:]
        routs = []
        for k in r_out:
            routs.append(refs[:k])
            refs = refs[k:]
        scr, sems = refs[:n_scr], refs[n_scr:]
        step = 0
        for ax, g in enumerate(grid):
            step = step * g + pl.program_id(ax)

        def each(what):
            for j, r in enumerate(riders):
                getattr(r, what)(rins[j], routs[j], sems[2 * j], sems[2 * j + 1])

        if grid:
            pl.when(step == 0)(lambda: each("start"))
        else:
            each("start")
        body(*pre, *ins, *outs, *scr)
        if grid:
            pl.when(step == steps - 1)(lambda: each("finish"))
        else:
            each("finish")

    aliases, off_in, off_out = {}, n_pre + n_in, n_out
    for r, ki, ko in zip(riders, r_in, r_out):
        for i, o in r.aliases.items():
            aliases[off_in + i] = off_out + o
        off_in, off_out = off_in + ki, off_out + ko
    sems = []
    for r in riders:
        sems += [pltpu.SemaphoreType.DMA((r.n_sems,)), pltpu.SemaphoreType.DMA((r.n_sems,))]
    layout = dict(grid=grid, in_specs=in_specs + [ANY] * sum(r_in), out_specs=out_specs + [ANY] * sum(r_out),
                  scratch_shapes=scratch_shapes + sems)
    if prefetch is not None:
        layout = dict(grid_spec=pltpu.PrefetchScalarGridSpec(num_scalar_prefetch=1, **layout))
    res = pl.pallas_call(
        wrapped, name=name, **layout,
        out_shape=out_shape + [s for r in riders for s in r.out_shapes], input_output_aliases=aliases,
        compiler_params=pltpu.CompilerParams(dimension_semantics=("arbitrary",) * len(grid),
                                             vmem_limit_bytes=vmem_mib * MIB, has_side_effects=True),
    )(*([] if prefetch is None else [prefetch]), *operands, *[a for r in riders for a in r.ins])
    res = list(res)
    outs, res = res[:n_out], res[n_out:]
    routs = []
    for k in r_out:
        routs.append(res[:k])
        res = res[k:]
    return outs, routs


def _exchange(riders, name):
    return _pallas(lambda: None, [], name=name, grid=(), in_specs=[], out_specs=[], out_shape=[], riders=riders)[1]


def _in_hbm(a):
    return pltpu.with_memory_space_constraint(a, pltpu.HBM)


def _place_shard(w, layer, dtype, name):
    _, rows, cols = w.shape
    half = rows // 2
    br = _block_rows(half)
    nb = half // br
    mine = 2 * lax.axis_index("x") + lax.axis_index("y")

    def body(q_ref, w_ref, o_ref):
        o_ref[...] = w_ref[...].astype(dtype)

    return pl.pallas_call(
        body, name=name,
        grid_spec=pltpu.PrefetchScalarGridSpec(
            num_scalar_prefetch=1, grid=(2, nb),
            in_specs=[pl.BlockSpec((None, br, cols), lambda h, i, q: (layer, h * nb + i, 0))],
            out_specs=pl.BlockSpec((None, None, br, cols), lambda h, i, q: (q[0], h, i, 0))),
        out_shape=pltpu.HBM((N_CHIPS, 2, half, cols), dtype),
        compiler_params=_params(16, 2),
    )(jnp.reshape(mine, (1,)).astype(jnp.int32), _in_hbm(w))


def _plain_copy(a, name):
    def body(a_ref, o_ref):
        o_ref[...] = a_ref[...]

    vmem = pl.BlockSpec(memory_space=pltpu.VMEM)
    return pl.pallas_call(body, name=name, in_specs=[vmem], out_specs=vmem,
                          out_shape=jax.ShapeDtypeStruct(a.shape, a.dtype))(a)


def _add_pair(g, recv, name):
    _, _, r, cdim = g.shape
    br = _block_rows(r, 256)
    c = lax.axis_index("c")

    def body(c_ref, g_ref, r_ref, o_ref):
        o_ref[...] = (g_ref[...] + r_ref[...]).astype(BF16)

    return pl.pallas_call(
        body, name=name,
        grid_spec=pltpu.PrefetchScalarGridSpec(
            num_scalar_prefetch=1, grid=(N_CHIPS, r // br),
            in_specs=[pl.BlockSpec((None, None, br, cdim), lambda q, i, c_ref: (q, c_ref[0], i, 0)),
                      pl.BlockSpec((None, br, cdim), lambda q, i, c_ref: (q, i, 0))],
            out_specs=pl.BlockSpec((None, br, cdim), lambda q, i, c_ref: (q, i, 0))),
        out_shape=pltpu.HBM((N_CHIPS, r, cdim), BF16),
        compiler_params=_params(16, 2),
    )(jnp.reshape(c, (1,)).astype(jnp.int32), _in_hbm(g), _in_hbm(recv))


def _add_chips(own, recv, name):
    _, r, cdim = own.shape
    br = _block_rows(r, 256)
    x, y, c = _mesh_pos()

    def body(pos_ref, own_ref, r_ref, o_ref):
        acc = own_ref[...].astype(F32)
        for k in range(3):
            acc = acc + r_ref[k].astype(F32)
        o_ref[...] = acc

    return pl.pallas_call(
        body, name=name,
        grid_spec=pltpu.PrefetchScalarGridSpec(
            num_scalar_prefetch=1, grid=(r // br,),
            in_specs=[pl.BlockSpec((None, br, cdim), lambda i, pos: (pos[0], i, 0)),
                      pl.BlockSpec((3, br, cdim), lambda i, pos: (0, i, 0))],
            out_specs=pl.BlockSpec((None, br, cdim), lambda i, pos: (pos[1], i, 0))),
        out_shape=pltpu.HBM((2, r, cdim), F32),
        compiler_params=_params(16, 1),
    )(jnp.stack([2 * x + y, c]).astype(jnp.int32), _in_hbm(own), _in_hbm(recv))


def _adam_math(w, m, v, g):
    c1 = 1.0 / (1.0 - ADAM_B1 ** ADAM_STEP)
    c2 = 1.0 / (1.0 - ADAM_B2 ** ADAM_STEP)
    m_new = ADAM_B1 * m + (1.0 - ADAM_B1) * g
    v_new = ADAM_B2 * v + (1.0 - ADAM_B2) * (g * g)
    return -ADAM_LR * ((m_new * c1) / (jnp.sqrt(v_new * c2) + ADAM_EPS) + ADAM_WD * w), m_new, v_new


SMALL_WEIGHTS = [
    ("ev_norm_g", (1, D_MODEL), ["ev_norm_g"], None), ("ev_conv_a_b", (1, A_DIM), ["ev_conv_a_b"], None),
    ("ev_ln_a_g", (1, A_DIM), ["ev_ln_a_g"], None), ("ev_ln_a_b", (1, A_DIM), ["ev_ln_a_b"], None),
    ("od_w_s", (C_GROUPS, CHUNK, CHUNK), ["od_w_s_lo", "od_w_s_hi"], None), ("od_b_s", (C_GROUPS, CHUNK), ["od_b_s"], None),
    ("mlp_norm_g", (2, D_MODEL), ["mlp_norm_g0", "mlp_norm_g1"], None), ("final_norm_g", (1, D_MODEL), ["final_norm_g"], None),
    ("ev_conv_a_w", (A_CONV_WIDTH, A_DIM // N_CHIPS), ["ev_conv_a_w"], A_DIM // N_CHIPS),
    ("ev_conv_b_w", (B_CONV_WIDTH, B_DIM // N_CHIPS), ["ev_conv_b_w"], B_DIM // N_CHIPS),
    ("od_norm_g", (1, D_MODEL // N_CHIPS), ["od_norm_g"], D_MODEL // N_CHIPS),
    ("od_b_in", (1, 2 * C_DIM // N_CHIPS), ["od_b_in"], 2 * C_DIM // N_CHIPS),
    ("od_ln_v_g", (1, C_DIM // N_CHIPS), ["od_ln_v_g"], C_DIM // N_CHIPS),
    ("od_ln_v_b", (1, C_DIM // N_CHIPS), ["od_ln_v_b"], C_DIM // N_CHIPS),
]


def _small_update(own, landed, weights):
    names = list(own.keys())
    n_g, n_w = len(names), len(SMALL_WEIGHTS)

    def body(*refs):
        refs = list(refs)
        own_refs = dict(zip(names, refs[:n_g]))
        land_refs = dict(zip(names, refs[n_g:2 * n_g]))
        wmv = [refs[2 * n_g + 3 * i:2 * n_g + 3 * i + 3] for i in range(n_w)]
        o0 = 2 * n_g + 3 * n_w
        loss_ref = refs[o0]
        outs = [refs[o0 + 1 + 4 * i:o0 + 5 + 4 * i] for i in range(n_w)]
        acc = dict(zip(names, refs[o0 + 1 + 4 * n_w:]))
        x, y, c = _mesh_pos()
        mine, chip = 4 * x + 2 * y + c, 2 * x + y

        for nm in names:
            for d in range(N_DEV):
                def add(term, nm=nm, d=d):
                    acc[nm][...] = term if d == 0 else acc[nm][...] + term
                pl.when(mine == d)(lambda nm=nm, add=add: add(own_refs[nm][...]))
                pl.when(mine != d)(lambda nm=nm, d=d, add=add: add(land_refs[nm][d]))
        loss_ref[...] = acc["loss"][...]

        def update(i, rows, g):
            w_ref, m_ref, v_ref = wmv[i]
            delta, m_new, v_new = _adam_math(w_ref[rows], m_ref[rows], v_ref[rows], g)
            for ref, val in zip(outs[i], (g, delta, m_new, v_new)):
                ref[rows] = val

        for i, (_, shape, grads, per_chip) in enumerate(SMALL_WEIGHTS):
            for row, gname in enumerate(grads):
                per_grad = shape[0] // len(grads)
                rows = slice(row * per_grad, (row + 1) * per_grad)
                if per_chip is None:
                    update(i, rows, acc[gname][...])
                else:
                    for q in range(N_CHIPS):
                        pl.when(chip == q)(lambda i=i, rows=rows, gname=gname, q=q, per_chip=per_chip:
                                           update(i, rows, acc[gname][:, q * per_chip:(q + 1) * per_chip]))

    operands = [own[nm] for nm in names] + [landed[nm] for nm in names]
    for nm, _, _, _ in SMALL_WEIGHTS:
        operands += list(weights[nm])
    out_shape = [jax.ShapeDtypeStruct((1, 1), F32)]
    for _, shape, _, _ in SMALL_WEIGHTS:
        out_shape += [jax.ShapeDtypeStruct(shape, F32)] * 4
    res = pl.pallas_call(
        body, name="small_update", grid=(1,),
        in_specs=[_full_spec(a.shape) for a in operands], out_specs=[_full_spec(s.shape) for s in out_shape],
        out_shape=out_shape, scratch_shapes=[pltpu.VMEM(own[nm].shape, F32) for nm in names],
        compiler_params=_params(32, 1),
    )(*[_in_hbm(a) for a in operands])
    return res[0], {nm: res[1 + 4 * i:5 + 4 * i] for i, (nm, _, _, _) in enumerate(SMALL_WEIGHTS)}


def _adamw(w, m, v, grads, name):
    layers, r, cdim = w.shape
    br = _block_rows(r, 256 if cdim > LANES else 1024)
    blocks = r // br

    def body(*refs):
        w_ref, m_ref, v_ref = refs[:3]
        g_refs = refs[3:3 + layers]
        go_ref, d_ref, mo_ref, vo_ref = refs[3 + layers:]
        layer = pl.program_id(0)
        for l in range(layers):
            @pl.when(layer == l)
            def _(l=l):
                g = g_refs[l][...]
                go_ref[...] = g
                d_ref[...], mo_ref[...], vo_ref[...] = _adam_math(w_ref[...], m_ref[...], v_ref[...], g)

    spec3 = pl.BlockSpec((None, br, cdim), lambda l, i: (l, i, 0))
    g_specs = [pl.BlockSpec((br, cdim), lambda l, i, own=own: (jnp.clip(i + (l - own) * blocks, 0, blocks - 1), 0))
               for own in range(layers)]
    out = jax.ShapeDtypeStruct((layers, r, cdim), F32)
    outs, _ = _pallas(body, [_in_hbm(a) for a in (w, m, v, *grads)], name=name, grid=(layers, blocks),
                      in_specs=[spec3, spec3, spec3] + g_specs, out_specs=[spec3] * 4, out_shape=[out] * 4, vmem_mib=32)
    return outs


def _fill_shifted(buf, rows):
    for b in range(1, SUBLANES):
        buf[b, 0:rows - SUBLANES, :] = buf[0, b:b + rows - SUBLANES, :]


def _window(buf, start, size):
    return buf[start % SUBLANES, start - start % SUBLANES:start - start % SUBLANES + size, :]


def _conv31(src, w_ref, r0, base, init):
    acc = init
    for k in range(A_CONV_WIDTH):
        acc = acc + w_ref[k:k + 1, :] * _window(src, base + k + r0, CONV_ROWS)
    return acc


def _fwd_even(x, norm_g, w_in, conv_a_w, conv_a_b, ln_g, ln_b, conv_b_w, w_out, *, tm, seq, riders=()):
    tokens = x.shape[0]
    nt, tps = tokens // tm, seq // tm

    def body(x_ref, g_ref, win_hbm, caw_ref, cab_ref, lng_ref, lnb_ref, cbw_ref, wout_hbm,
             h_ref, n_ref, z_ref, a2_ref, cv_ref, mix_ref, win_v, wout_v, pa, pb, sem):
        i = pl.program_id(0)

        _load_weights([(win_hbm, win_v, False), (wout_hbm, wout_v, True)], sem)

        xv = x_ref[...]
        nf, _ = _rms_fwd(xv, g_ref[...])
        n = nf.astype(BF16)
        n_ref[...] = n
        z = jnp.concatenate([_dot(n, win_v[j]) for j in range(N_CHIPS)], axis=1)
        z_ref[...] = z.astype(BF16)
        a_val, a_gate = z[:, 0:A_DIM], z[:, A_DIM:2 * A_DIM]
        b_gate, c_gate, b_val = z[:, 1024:1536], z[:, 1536:2048], z[:, 2048:2560]

        first = (i % tps) == 0

        @pl.when(first)
        def _():
            pa[0, 0:A_HALO, :] = jnp.zeros((A_HALO, A_DIM), F32)
            pb[0:B_HALO, :] = jnp.zeros((B_HALO, B_DIM), F32)

        @pl.when(jnp.logical_not(first))
        def _():
            pa[0, 0:A_HALO, :] = pa[0, tm:tm + A_HALO, :]
            pb[0:B_HALO, :] = pb[tm:tm + B_HALO, :]

        pa[0, A_HALO:A_HALO + tm, :] = a_val * jax.nn.sigmoid(a_gate)
        pb[B_HALO:B_HALO + tm, :] = c_gate * b_val
        _fill_shifted(pa, A_HALO + tm)
        bias = jnp.broadcast_to(cab_ref[...], (CONV_ROWS, A_DIM))
        for r0 in range(0, tm, CONV_ROWS):
            a2_ref[r0:r0 + CONV_ROWS, :] = _conv31(pa, caw_ref, r0, A_HALO - (A_CONV_WIDTH - 1), bias)
        xhat, _ = _ln_stats(a2_ref[...])
        a3 = xhat * lng_ref[...] + lnb_ref[...]
        a4 = a3 * jax.nn.sigmoid(a3)
        cv = cbw_ref[0:1, :] * pb[B_HALO - 2:B_HALO - 2 + tm, :]
        cv = cv + cbw_ref[1:2, :] * pb[B_HALO - 1:B_HALO - 1 + tm, :]
        cv = cv + cbw_ref[2:3, :] * pb[B_HALO:B_HALO + tm, :]
        cv_ref[...] = cv.astype(BF16)
        mix = jnp.concatenate([a4, b_gate * cv], axis=1).astype(BF16)
        mix_ref[...] = mix
        h_ref[...] = xv + _dot(mix, wout_v[...])

    shp = lambda cols, dt: jax.ShapeDtypeStruct((tokens, cols), dt)
    return _pallas(
        body, [x, norm_g, w_in, conv_a_w, conv_a_b, ln_g, ln_b, conv_b_w, w_out], name="fwd_even", grid=(nt,),
        in_specs=[_row_spec(tm, D_MODEL), _full_spec((1, D_MODEL)), ANY, _full_spec((A_CONV_WIDTH, A_DIM)),
                  _full_spec((1, A_DIM)), _full_spec((1, A_DIM)), _full_spec((1, A_DIM)),
                  _full_spec((B_CONV_WIDTH, B_DIM)), ANY],
        out_specs=[_row_spec(tm, D_MODEL), _row_spec(tm, D_MODEL), _row_spec(tm, IN_EVEN), _row_spec(tm, A_DIM),
                   _row_spec(tm, B_DIM), _row_spec(tm, D_MODEL)],
        out_shape=[shp(D_MODEL, F32), shp(D_MODEL, BF16), shp(IN_EVEN, BF16), shp(A_DIM, F32), shp(B_DIM, BF16),
                   shp(D_MODEL, BF16)],
        scratch_shapes=[pltpu.VMEM((N_CHIPS, D_MODEL, IN_EVEN // N_CHIPS), BF16), pltpu.VMEM((D_MODEL, D_MODEL), BF16),
                        pltpu.VMEM((SUBLANES, A_HALO + tm, A_DIM), F32), pltpu.VMEM((B_HALO + tm, B_DIM), F32),
                        pltpu.SemaphoreType.DMA((N_LOADS,))],
        vmem_mib=56, riders=riders)


def _loss_tail(xv, g, target, loss_ref, dh_ref, dhb_ref, dg_ref):
    @pl.when(pl.program_id(0) == 0)
    def _():
        loss_ref[...] = jnp.zeros((1, 1), F32)
        dg_ref[...] = jnp.zeros((1, D_MODEL), F32)

    out, rstd = _rms_fwd(xv, g)
    err = out - target
    per_token = jnp.sum(err * err, axis=1, keepdims=True) * (1.0 / D_MODEL)
    loss_ref[...] += 0.5 * jnp.sum(per_token, axis=0, keepdims=True)
    dx, dg = _rms_bwd(err * (1.0 / D_MODEL), xv, rstd, g)
    dh_ref[...] = dx
    dhb_ref[...] = dx.astype(BF16)
    dg_ref[...] += dg


def _fwd_mlp(h, norm_g, w1, w2, layer, *, tm, riders=(), head=None):
    tokens = h.shape[0]
    nt = tokens // tm
    fs = D_FF // N_CHIPS
    n_in = 4 if head is None else 6

    def body(*refs):
        h_ref, g_ref, w1_hbm, w2_hbm = refs[:4]
        w1_v, w2_v, sem = refs[-3:]
        outs = refs[n_in:-3]
        n_ref, p_ref, q_ref = outs[1:4] if head is None else outs[0:3]
        _load_weights([(w1_hbm, w1_v, False), (w2_hbm, w2_v, False)], sem)

        xv = h_ref[...]
        nf, _ = _rms_fwd(xv, g_ref[...])
        n = nf.astype(BF16)
        n_ref[...] = n
        acc = xv
        for j in range(N_CHIPS):
            p = _dot(n, w1_v[j])
            p_ref[:, j * fs:(j + 1) * fs] = p.astype(BF16)
            r = jnp.maximum(p, 0.0)
            q = (r * r).astype(BF16)
            q_ref[:, j * fs:(j + 1) * fs] = q
            acc = acc + _dot(q, w2_v[j])
        if head is None:
            outs[0][...] = acc
        else:
            _loss_tail(acc, refs[4][...], refs[5][...], *outs[3:7])

    shp = lambda cols, dt: jax.ShapeDtypeStruct((tokens, cols), dt)
    saved_specs = [_row_spec(tm, D_MODEL), _row_spec(tm, D_FF), _row_spec(tm, D_FF)]
    saved_shapes = [shp(D_MODEL, BF16), shp(D_FF, BF16), shp(D_FF, BF16)]
    if head is None:
        operands, in_specs = [h, norm_g, w1, w2], [_row_spec(tm, D_MODEL), _full_spec((1, D_MODEL)), ANY, ANY]
        out_specs, out_shape = [_row_spec(tm, D_MODEL)] + saved_specs, [shp(D_MODEL, F32)] + saved_shapes
    else:
        operands = [h, norm_g, w1, w2, *head]
        in_specs = [_row_spec(tm, D_MODEL), _full_spec((1, D_MODEL)), ANY, ANY, _full_spec((1, D_MODEL)), _row_spec(tm, D_MODEL)]
        out_specs = saved_specs + [_full_spec((1, 1)), _row_spec(tm, D_MODEL), _row_spec(tm, D_MODEL), _full_spec((1, D_MODEL))]
        out_shape = saved_shapes + [jax.ShapeDtypeStruct((1, 1), F32), shp(D_MODEL, F32), shp(D_MODEL, BF16),
                                    jax.ShapeDtypeStruct((1, D_MODEL), F32)]
    return _pallas(
        body, operands, name=f"fwd_mlp{layer}", grid=(nt,), in_specs=in_specs, out_specs=out_specs, out_shape=out_shape,
        scratch_shapes=[pltpu.VMEM((N_CHIPS, D_MODEL, fs), BF16), pltpu.VMEM((N_CHIPS, fs, D_MODEL), BF16),
                        pltpu.SemaphoreType.DMA((N_LOADS,))],
        vmem_mib=56, riders=riders)


def _tril_mask():
    row = lax.broadcasted_iota(jnp.int32, (CHUNK, CHUNK), 0)
    col = lax.broadcasted_iota(jnp.int32, (CHUNK, CHUNK), 1)
    return row >= col


def _triu_mask():
    row = lax.broadcasted_iota(jnp.int32, (CHUNK, CHUNK), 0)
    col = lax.broadcasted_iota(jnp.int32, (CHUNK, CHUNK), 1)
    return row <= col


def _fwd_odd(h, norm_g, w_in, b_in, ln_g, ln_b, w_s, b_s_rows, w_out, *, tm, riders=()):
    tokens = h.shape[0]
    nt = tokens // tm
    cs = 2 * C_DIM // N_CHIPS

    def body(h_ref, g_ref, win_hbm, bin_ref, lng_ref, lnb_ref, ws_ref, bs_ref, wout_hbm,
             ho_ref, n_ref, s_ref, cdf_ref, sv_ref, y_ref, win_v, wout_v, bd, sem):
        _load_weights([(win_hbm, win_v, False), (wout_hbm, wout_v, True)], sem)

        @pl.when(pl.program_id(0) == 0)
        def _():
            mask = _tril_mask()
            bd[...] = jnp.zeros(bd.shape, BF16)
            for g in range(C_GROUPS):
                w = jnp.where(mask, ws_ref[g], 0.0).astype(BF16)
                bd[g, 0:CHUNK, 0:CHUNK] = w
                bd[g, CHUNK:PAIR, CHUNK:PAIR] = w

        xv = h_ref[...]
        nf, _ = _rms_fwd(xv, g_ref[...])
        n = nf.astype(BF16)
        n_ref[...] = n
        s = jnp.concatenate([_dot(n, win_v[j]) for j in range(N_CHIPS)], axis=1) + bin_ref[...]
        s_ref[...] = s.astype(BF16)
        cdf = _gelu_cdf(s)
        cdf_ref[...] = cdf.astype(BF16)
        zz = s * cdf
        u, v = zz[:, 0:C_DIM], zz[:, C_DIM:2 * C_DIM]
        xhat, _ = _ln_stats(v)
        vn = (xhat * lng_ref[...] + lnb_ref[...]).astype(BF16)
        for g in range(C_GROUPS):
            cols = slice(g * CHUNK, (g + 1) * CHUNK)
            bias = jnp.concatenate([bs_ref[g], bs_ref[g]], axis=0)
            for r0 in range(0, tm, PAIR):
                sv = _dot(bd[g], vn[r0:r0 + PAIR, cols]) + bias
                sv_ref[r0:r0 + PAIR, cols] = sv.astype(BF16)
                y_ref[r0:r0 + PAIR, cols] = (u[r0:r0 + PAIR, cols] * sv).astype(BF16)
        ho_ref[...] = xv + _dot(y_ref[...], wout_v[...])

    shp = lambda cols, dt: jax.ShapeDtypeStruct((tokens, cols), dt)
    return _pallas(
        body, [h, norm_g, w_in, b_in, ln_g, ln_b, w_s, b_s_rows, w_out], name="fwd_odd", grid=(nt,),
        in_specs=[_row_spec(tm, D_MODEL), _full_spec((1, D_MODEL)), ANY, _full_spec((1, 2 * C_DIM)),
                  _full_spec((1, C_DIM)), _full_spec((1, C_DIM)), _full_spec((C_GROUPS, CHUNK, CHUNK)),
                  _full_spec((C_GROUPS, CHUNK, CHUNK)), ANY],
        out_specs=[_row_spec(tm, D_MODEL), _row_spec(tm, D_MODEL), _row_spec(tm, 2 * C_DIM), _row_spec(tm, 2 * C_DIM),
                   _row_spec(tm, C_DIM), _row_spec(tm, C_DIM)],
        out_shape=[shp(D_MODEL, F32), shp(D_MODEL, BF16), shp(2 * C_DIM, BF16), shp(2 * C_DIM, BF16), shp(C_DIM, BF16),
                   shp(C_DIM, BF16)],
        scratch_shapes=[pltpu.VMEM((N_CHIPS, D_MODEL, cs), BF16), pltpu.VMEM((C_DIM, D_MODEL), BF16),
                        pltpu.VMEM((C_GROUPS, PAIR, PAIR), BF16), pltpu.SemaphoreType.DMA((N_LOADS,))],
        vmem_mib=56, riders=riders)


def _bwd_mlp(dh, h, norm_g, p, w1, w2, layer, *, tm, riders=()):
    tokens = h.shape[0]
    nt = tokens // tm
    fs = D_FF // N_CHIPS

    def body(dh_ref, h_ref, g_ref, p_ref, w1_hbm, w2_hbm, dx_ref, dxb_ref, dp_ref, dg_ref, w1_v, w2_v, sem):
        @pl.when(pl.program_id(0) == 0)
        def _():
            dg_ref[...] = jnp.zeros((1, D_MODEL), F32)

        _load_weights([(w1_hbm, w1_v, False), (w2_hbm, w2_v, False)], sem)

        dhv = dh_ref[...]
        dhb = dhv.astype(BF16)
        dn = jnp.zeros((tm, D_MODEL), F32)
        for j in range(N_CHIPS):
            dq = _dot_nt(dhb, w2_v[j])
            r = jnp.maximum(p_ref[:, j * fs:(j + 1) * fs].astype(F32), 0.0)
            dp = ((2.0 * r) * dq).astype(BF16)
            dp_ref[:, j * fs:(j + 1) * fs] = dp
            dn = dn + _dot_nt(dp, w1_v[j])
        xv = h_ref[...]
        g = g_ref[...]
        _, rstd = _rms_fwd(xv, g)
        dx, dg = _rms_bwd(dn, xv, rstd, g)
        dx_ref[...] = dhv + dx
        dxb_ref[...] = (dhv + dx).astype(BF16)
        dg_ref[...] += dg

    return _pallas(
        body, [dh, h, norm_g, p, w1, w2], name=f"bwd_mlp{layer}", grid=(nt,),
        in_specs=[_row_spec(tm, D_MODEL), _row_spec(tm, D_MODEL), _full_spec((1, D_MODEL)), _row_spec(tm, D_FF), ANY, ANY],
        out_specs=[_row_spec(tm, D_MODEL), _row_spec(tm, D_MODEL), _row_spec(tm, D_FF), _full_spec((1, D_MODEL))],
        out_shape=[jax.ShapeDtypeStruct((tokens, D_MODEL), F32), jax.ShapeDtypeStruct((tokens, D_MODEL), BF16),
                   jax.ShapeDtypeStruct((tokens, D_FF), BF16), jax.ShapeDtypeStruct((1, D_MODEL), F32)],
        scratch_shapes=[pltpu.VMEM((N_CHIPS, D_MODEL, fs), BF16), pltpu.VMEM((N_CHIPS, fs, D_MODEL), BF16),
                        pltpu.SemaphoreType.DMA((N_LOADS,))],
        vmem_mib=56, riders=riders)


def _bwd_odd(dh, h, norm_g, s, cdf, sv, w_in, ln_g, ln_b, w_s, w_out, *, tm, riders=()):
    tokens = h.shape[0]
    nt = tokens // tm
    cs = 2 * C_DIM // N_CHIPS

    def body(dh_ref, h_ref, g_ref, s_ref, cdf_ref, sv_ref, win_hbm, lng_ref, lnb_ref, ws_ref, wout_hbm,
             dx_ref, dxb_ref, ds_ref, dg_ref, dbin_ref, dlng_ref, dlnb_ref, dws_ref, dbs_ref,
             win_v, wout_v, bdt, dws_acc, dbs_acc, dvn, sem):
        i = pl.program_id(0)

        _load_weights([(win_hbm, win_v, False), (wout_hbm, wout_v, True)], sem)

        @pl.when(i == 0)
        def _():
            mask_t = _triu_mask()
            bdt[...] = jnp.zeros(bdt.shape, BF16)
            for g in range(C_GROUPS):
                wt = jnp.where(mask_t, ws_ref[g].T, 0.0).astype(BF16)
                bdt[g, 0:CHUNK, 0:CHUNK] = wt
                bdt[g, CHUNK:PAIR, CHUNK:PAIR] = wt
            dws_acc[...] = jnp.zeros(dws_acc.shape, F32)
            dbs_acc[...] = jnp.zeros(dbs_acc.shape, F32)
            dg_ref[...] = jnp.zeros(dg_ref.shape, F32)
            dbin_ref[...] = jnp.zeros(dbin_ref.shape, F32)
            dlng_ref[...] = jnp.zeros(dlng_ref.shape, F32)
            dlnb_ref[...] = jnp.zeros(dlnb_ref.shape, F32)

        dhv = dh_ref[...]
        dy = _dot_nt(dhv.astype(BF16), wout_v[...])
        sf = s_ref[...].astype(F32)
        cdf = cdf_ref[...].astype(F32)
        pdf = jnp.exp(-0.5 * sf * sf) * 0.3989422804014327
        zz = sf * cdf
        dgelu = cdf + sf * pdf
        u, v = zz[:, 0:C_DIM], zz[:, C_DIM:2 * C_DIM]
        xhat, rs = _ln_stats(v)
        lng = lng_ref[...]
        vn = (xhat * lng + lnb_ref[...]).astype(BF16)
        du = dy * sv_ref[...].astype(F32)
        dsv = dy * u
        dsvb = dsv.astype(BF16)
        for g in range(C_GROUPS):
            cols = slice(g * CHUNK, (g + 1) * CHUNK)
            for r0 in range(0, tm, PAIR):
                blk = dsvb[r0:r0 + PAIR, cols]
                dvn[r0:r0 + PAIR, cols] = _dot(bdt[g], blk)
                dws_acc[g] += _dot_nt(blk, vn[r0:r0 + PAIR, cols])
                dbs_acc[g] += dsv[r0:r0 + CHUNK, cols] + dsv[r0 + CHUNK:r0 + PAIR, cols]
        dv, dlng, dlnb = _ln_bwd(dvn[...], xhat, rs, lng)
        dlng_ref[...] += dlng
        dlnb_ref[...] += dlnb
        ds = jnp.concatenate([du, dv], axis=1) * dgelu
        dbin_ref[...] += jnp.sum(ds, axis=0, keepdims=True)
        dsb = ds.astype(BF16)
        ds_ref[...] = dsb
        dn = jnp.zeros((tm, D_MODEL), F32)
        for j in range(N_CHIPS):
            dn = dn + _dot_nt(dsb[:, j * cs:(j + 1) * cs], win_v[j])
        xv = h_ref[...]
        g = g_ref[...]
        _, rstd = _rms_fwd(xv, g)
        dx, dg = _rms_bwd(dn, xv, rstd, g)
        dx_ref[...] = dhv + dx
        dxb_ref[...] = (dhv + dx).astype(BF16)
        dg_ref[...] += dg

        @pl.when(i == nt - 1)
        def _():
            mask = _tril_mask()
            for g in range(C_GROUPS):
                full = dws_acc[g]
                dws_ref[g] = jnp.where(mask, full[0:CHUNK, 0:CHUNK] + full[CHUNK:PAIR, CHUNK:PAIR], 0.0)
                dbs_ref[g:g + 1, :] = jnp.sum(dbs_acc[g].T, axis=0, keepdims=True)

    row = lambda cols: jax.ShapeDtypeStruct((1, cols), F32)
    return _pallas(
        body, [dh, h, norm_g, s, cdf, sv, w_in, ln_g, ln_b, w_s, w_out], name="bwd_odd", grid=(nt,),
        in_specs=[_row_spec(tm, D_MODEL), _row_spec(tm, D_MODEL), _full_spec((1, D_MODEL)), _row_spec(tm, 2 * C_DIM),
                  _row_spec(tm, 2 * C_DIM), _row_spec(tm, C_DIM), ANY, _full_spec((1, C_DIM)), _full_spec((1, C_DIM)),
                  _full_spec((C_GROUPS, CHUNK, CHUNK)), ANY],
        out_specs=[_row_spec(tm, D_MODEL), _row_spec(tm, D_MODEL), _row_spec(tm, 2 * C_DIM), _full_spec((1, D_MODEL)),
                   _full_spec((1, 2 * C_DIM)),
                   _full_spec((1, C_DIM)), _full_spec((1, C_DIM)), _full_spec((C_GROUPS, CHUNK, CHUNK)),
                   _full_spec((C_GROUPS, CHUNK))],
        out_shape=[jax.ShapeDtypeStruct((tokens, D_MODEL), F32), jax.ShapeDtypeStruct((tokens, D_MODEL), BF16),
                   jax.ShapeDtypeStruct((tokens, 2 * C_DIM), BF16),
                   row(D_MODEL), row(2 * C_DIM), row(C_DIM), row(C_DIM),
                   jax.ShapeDtypeStruct((C_GROUPS, CHUNK, CHUNK), F32), jax.ShapeDtypeStruct((C_GROUPS, CHUNK), F32)],
        scratch_shapes=[pltpu.VMEM((N_CHIPS, D_MODEL, cs), BF16), pltpu.VMEM((C_DIM, D_MODEL), BF16),
                        pltpu.VMEM((C_GROUPS, PAIR, PAIR), BF16), pltpu.VMEM((C_GROUPS, PAIR, PAIR), F32),
                        pltpu.VMEM((C_GROUPS, CHUNK, CHUNK), F32), pltpu.VMEM((tm, C_DIM), F32),
                        pltpu.SemaphoreType.DMA((N_LOADS,))],
        vmem_mib=56, riders=riders)


def _bwd_even(dh, x, norm_g, z, a2, cv, w_in, conv_a_w, ln_g, ln_b, conv_b_w, w_out, *, tm, seq, riders=()):
    tokens = x.shape[0]
    nt, tps = tokens // tm, seq // tm
    ws = IN_EVEN // N_CHIPS

    def body(dh_ref, x_ref, g_ref, z_ref, a2_ref, cv_ref, win_hbm, caw_ref, lng_ref, lnb_ref, cbw_ref, wout_hbm,
             dx_ref, dz_ref, dg_ref, dcaw_ref, dcab_ref, dlng_ref, dlnb_ref, dcbw_ref,
             win_v, wout_v, ea, eb, a1s, da1s, sigs, wide, dw_acc, sem):
        i = pl.program_id(0)

        _load_weights([(win_hbm, win_v, False), (wout_hbm, wout_v, True)], sem)

        @pl.when(i == 0)
        def _():
            dw_acc[...] = jnp.zeros(dw_acc.shape, F32)
            for ref in (dg_ref, dcab_ref, dlng_ref, dlnb_ref, dcbw_ref):
                ref[...] = jnp.zeros(ref.shape, F32)

        last = ((nt - 1 - i) % tps) == tps - 1

        @pl.when(last)
        def _():
            ea[0, tm:tm + A_HALO, :] = jnp.zeros((A_HALO, A_DIM), F32)
            eb[tm:tm + B_HALO, :] = jnp.zeros((B_HALO, B_DIM), F32)

        @pl.when(jnp.logical_not(last))
        def _():
            ea[0, tm:tm + A_HALO, :] = ea[0, 0:A_HALO, :]
            eb[tm:tm + B_HALO, :] = eb[0:B_HALO, :]

        wide[...] = _dot_nt(dh_ref[...].astype(BF16), wout_v[...])
        lng, lnb = lng_ref[...], lnb_ref[...]
        zero_row = jnp.zeros((1, A_DIM), F32)
        dlng, dlnb, dcab = zero_row, zero_row, zero_row
        for r0 in range(0, tm, ELEM_ROWS):
            rows = slice(r0, r0 + ELEM_ROWS)
            a_val, a_gate = z_ref[rows, 0:A_DIM].astype(F32), z_ref[rows, A_DIM:2 * A_DIM].astype(F32)
            xhat, rs = _ln_stats(a2_ref[rows, :])
            a3 = xhat * lng + lnb
            sg = jax.nn.sigmoid(a3)
            da3 = wide[rows, 0:A_DIM] * (sg * (1.0 + a3 * (1.0 - sg)))
            da2, g_part, b_part = _ln_bwd(da3, xhat, rs, lng)
            dlng, dlnb, dcab = dlng + g_part, dlnb + b_part, dcab + jnp.sum(da2, axis=0, keepdims=True)
            ea[0, rows, :] = da2
            eb[rows, :] = wide[rows, A_DIM:A_DIM + B_DIM] * z_ref[rows, 1024:1536].astype(F32)
            sig = jax.nn.sigmoid(a_gate)
            sigs[rows, :] = sig
            a1s[rows, :] = a_val * sig
        dlng_ref[...] += dlng
        dlnb_ref[...] += dlnb
        dcab_ref[...] += dcab
        _fill_shifted(ea, tm + A_HALO)
        for r0 in range(0, tm, CONV_ROWS):
            acc = jnp.zeros((CONV_ROWS, A_DIM), F32)
            for j in range(A_CONV_WIDTH):
                acc = acc + caw_ref[A_CONV_WIDTH - 1 - j:A_CONV_WIDTH - j, :] * _window(ea, r0 + j, CONV_ROWS)
            da1s[r0:r0 + CONV_ROWS, :] = acc
        for j0 in range(0, A_CONV_WIDTH, DW_TAPS):
            taps = range(j0, min(j0 + DW_TAPS, A_CONV_WIDTH))
            part = [jnp.zeros((CONV_ROWS, A_DIM), F32) for _ in taps]
            for r0 in range(0, tm, CONV_ROWS):
                a1c = a1s[r0:r0 + CONV_ROWS, :]
                for u, j in enumerate(taps):
                    part[u] = part[u] + _window(ea, r0 + j, CONV_ROWS) * a1c
            for u, j in enumerate(taps):
                dw_acc[A_CONV_WIDTH - 1 - j] += part[u]
        dcbw = [jnp.zeros((1, B_DIM), F32) for _ in range(B_CONV_WIDTH)]
        for r0 in range(0, tm, ELEM_ROWS):
            rows = slice(r0, r0 + ELEM_ROWS)
            da1, sig = da1s[rows, :], sigs[rows, :]
            dz_ref[rows, 0:A_DIM] = (da1 * sig).astype(BF16)
            dz_ref[rows, A_DIM:2 * A_DIM] = (da1 * z_ref[rows, 0:A_DIM].astype(F32) * (sig * (1.0 - sig))).astype(BF16)
            c_gate, b_val = z_ref[rows, 1536:2048].astype(F32), z_ref[rows, 2048:2560].astype(F32)
            dz_ref[rows, 1024:1536] = (wide[rows, A_DIM:A_DIM + B_DIM] * cv_ref[rows, :].astype(F32)).astype(BF16)
            cb = c_gate * b_val
            dcb = jnp.zeros((ELEM_ROWS, B_DIM), F32)
            for j in range(B_CONV_WIDTH):
                k = B_CONV_WIDTH - 1 - j
                sl = eb[r0 + j:r0 + j + ELEM_ROWS, :]
                dcb = dcb + cbw_ref[k:k + 1, :] * sl
                dcbw[k] = dcbw[k] + jnp.sum(sl * cb, axis=0, keepdims=True)
            dz_ref[rows, 1536:2048] = (dcb * b_val).astype(BF16)
            dz_ref[rows, 2048:2560] = (dcb * c_gate).astype(BF16)
        for k in range(B_CONV_WIDTH):
            dcbw_ref[k:k + 1, :] += dcbw[k]
        dn = jnp.zeros((tm, D_MODEL), F32)
        for j in range(N_CHIPS):
            dn = dn + _dot_nt(dz_ref[:, j * ws:(j + 1) * ws], win_v[j])
        wide[...] = dn
        g = g_ref[...]
        dg = jnp.zeros((1, D_MODEL), F32)
        for r0 in range(0, tm, ELEM_ROWS):
            rows = slice(r0, r0 + ELEM_ROWS)
            xv = x_ref[rows, :]
            _, rstd = _rms_fwd(xv, g)
            dx, dg_part = _rms_bwd(wide[rows, :], xv, rstd, g)
            dx_ref[rows, :] = dh_ref[rows, :] + dx
            dg = dg + dg_part
        dg_ref[...] += dg

        @pl.when(i == nt - 1)
        def _():
            for k in range(A_CONV_WIDTH):
                dcaw_ref[k:k + 1, :] = jnp.sum(dw_acc[k], axis=0, keepdims=True)

    row = lambda cols: jax.ShapeDtypeStruct((1, cols), F32)
    rs_ = functools.partial(_row_spec, rev_nt=nt)
    return _pallas(
        body, [dh, x, norm_g, z, a2, cv, w_in, conv_a_w, ln_g, ln_b, conv_b_w, w_out], name="bwd_even", grid=(nt,),
        in_specs=[rs_(tm, D_MODEL), rs_(tm, D_MODEL), _full_spec((1, D_MODEL)), rs_(tm, IN_EVEN), rs_(tm, A_DIM),
                  rs_(tm, B_DIM), ANY, _full_spec((A_CONV_WIDTH, A_DIM)), _full_spec((1, A_DIM)), _full_spec((1, A_DIM)),
                  _full_spec((B_CONV_WIDTH, B_DIM)), ANY],
        out_specs=[rs_(tm, D_MODEL), rs_(tm, IN_EVEN), _full_spec((1, D_MODEL)), _full_spec((A_CONV_WIDTH, A_DIM)),
                   _full_spec((1, A_DIM)), _full_spec((1, A_DIM)), _full_spec((1, A_DIM)), _full_spec((B_CONV_WIDTH, B_DIM))],
        out_shape=[jax.ShapeDtypeStruct((tokens, D_MODEL), F32), jax.ShapeDtypeStruct((tokens, IN_EVEN), BF16),
                   row(D_MODEL), jax.ShapeDtypeStruct((A_CONV_WIDTH, A_DIM), F32), row(A_DIM), row(A_DIM), row(A_DIM),
                   jax.ShapeDtypeStruct((B_CONV_WIDTH, B_DIM), F32)],
        scratch_shapes=[pltpu.VMEM((N_CHIPS, D_MODEL, ws), BF16), pltpu.VMEM((D_MODEL, D_MODEL), BF16),
                        pltpu.VMEM((SUBLANES, tm + A_HALO, A_DIM), F32), pltpu.VMEM((tm + B_HALO, B_DIM), F32),
                        pltpu.VMEM((tm, A_DIM), F32), pltpu.VMEM((tm, A_DIM), F32), pltpu.VMEM((tm, A_DIM), F32),
                        pltpu.VMEM((tm, D_MODEL), F32),
                        pltpu.VMEM((A_CONV_WIDTH, CONV_ROWS, A_DIM), F32), pltpu.SemaphoreType.DMA((N_LOADS,))],
        vmem_mib=56, riders=riders)


def _wgrad(a, b, name, *, col_shards, riders=()):
    tokens, m = a.shape
    n = b.shape[1]
    kc = 512
    if col_shards:
        bm, bn = m // 2, n // N_CHIPS
        grid = (2, N_CHIPS)
        out_spec = pl.BlockSpec((None, None, bm, bn), lambda i, j: (j, i, 0, 0))
    elif m // 8 >= MXU_ROWS:
        bm, bn = m // 8, n
        grid = (8, 1)
        out_spec = pl.BlockSpec((None, None, bm, bn), lambda i, j: (i // 2, i % 2, 0, 0))
    else:
        bm, bn = m // N_CHIPS, n
        grid = (N_CHIPS, 1)
        out_spec = pl.BlockSpec((None, 2, bm // 2, bn), lambda i, j: (i, 0, 0, 0))

    def body(a_ref, b_ref, o_ref):
        acc = jnp.zeros((bm, bn), F32)
        for k0 in range(0, tokens, kc):
            acc = acc + _dot_tn(a_ref[k0:k0 + kc, :].astype(BF16), b_ref[k0:k0 + kc, :].astype(BF16))
        if len(o_ref.shape) == 3:
            o_ref[0] = acc[0:bm // 2]
            o_ref[1] = acc[bm // 2:bm]
        else:
            o_ref[...] = acc

    out_rows = m // 2 if col_shards else m // 8
    outs, routs = _pallas(
        body, [a, b], name=name, grid=grid,
        in_specs=[pl.BlockSpec((tokens, bm), lambda i, j: (0, i)), pl.BlockSpec((tokens, bn), lambda i, j: (0, j))],
        out_specs=[out_spec], out_shape=[jax.ShapeDtypeStruct((N_CHIPS, 2, out_rows, bn), F32)],
        vmem_mib=56, riders=riders)
    return outs[0], routs


def _wgrad_pair(a, b, name, *, col_shards, riders=()):
    tokens, m = a.shape
    n = b.shape[1]
    kc = 512
    c0 = lax.axis_index("c")

    def half(ph, pre):
        return (ph + 1 + pre[0]) % 2

    if col_shards:
        bm, bn = m // 2, n // N_CHIPS
        a_spec = pl.BlockSpec((tokens, bm), lambda ph, q, pre: (0, half(ph, pre)))
        b_spec = pl.BlockSpec((tokens, bn), lambda ph, q, pre: (0, q))
    else:
        bm, bn = m // 8, n
        a_spec = pl.BlockSpec((tokens, bm), lambda ph, q, pre: (0, 2 * q + half(ph, pre)))
        b_spec = pl.BlockSpec((tokens, bn), lambda ph, q, pre: (0, 0))

    def body(pre_ref, a_ref, b_ref, o_ref, give, got, send_sems, recv_sems):
        ph, q = pl.program_id(0), pl.program_id(1)
        acc = jnp.zeros((bm, bn), F32)
        for k0 in range(0, tokens, kc):
            acc = acc + _dot_tn(a_ref[k0:k0 + kc, :].astype(BF16), b_ref[k0:k0 + kc, :].astype(BF16))
        x, y, cc = _mesh_pos()

        def tile(t):
            return _remote(give.at[t], got.at[t], send_sems.at[t], recv_sems.at[t], (x, y, 1 - cc))

        @pl.when(ph == 0)
        def _():
            give[q] = acc
            tile(q).start()

        @pl.when(ph == 1)
        def _():
            tile(q).wait_recv()
            o_ref[...] = (acc + got[q]).astype(BF16)

        @pl.when((ph == 1) & (q == N_CHIPS - 1))
        def _():
            for t in range(N_CHIPS):
                tile(t).wait_send()

    outs, routs = _pallas(
        body, [a, b], name=name, grid=(2, N_CHIPS), in_specs=[a_spec, b_spec],
        out_specs=[pl.BlockSpec((None, bm, bn), lambda ph, q, pre: (ph * q, 0, 0))],
        out_shape=[jax.ShapeDtypeStruct((N_CHIPS, bm, bn), BF16)],
        scratch_shapes=[pltpu.VMEM((N_CHIPS, bm, bn), F32), pltpu.VMEM((N_CHIPS, bm, bn), F32),
                        pltpu.SemaphoreType.DMA((N_CHIPS,)), pltpu.SemaphoreType.DMA((N_CHIPS,))],
        vmem_mib=56, riders=riders, prefetch=jnp.reshape(c0, (1,)).astype(jnp.int32))
    return outs[0], routs


class _GradReduce:
    def __init__(self, name, grad=None, chip_sum=None):
        self.name, self.grad, self.chip_sum = name, grad, chip_sum
        self.full = None

    def pair_swap(self):
        return _PairSwap([self.grad])

    def took_pair(self, outs):
        self.chip_sum = _in_hbm(_add_pair(self.grad, outs[0], f"pair_sum_{self.name}"))

    def took_chips(self, outs):
        self.full = _in_hbm(_add_chips(self.chip_sum, outs[0], f"chip_sum_{self.name}"))

    def chips_beside(self, collective_id):
        self.took_chips([_chip_swap_beside(self.chip_sum, f"chip_swap_{self.name}", collective_id)])

    def pair_share(self):
        return _PairShare([self.full])

    def took_share(self, outs):
        self.full = outs[0]

    def reduced(self):
        return jnp.reshape(self.full, (2 * self.full.shape[1], self.full.shape[2]))


def _forward_backward(x2, tgt2, w, conv_a_w, conv_b_w, od_norm, od_bias, od_lng, od_lnb,
                      ev_norm_g, ev_conv_a_b, ev_ln_a_g, ev_ln_a_b, od_w_s, od_b_s, mlp_norm_g, final_norm_g,
                      *, tm, seq, distributed=True):
    d = x2.shape[1]
    b_s_rows = jnp.broadcast_to(od_b_s[0][:, :, None], (C_GROUPS, CHUNK, CHUNK))
    (h1, n0, z, a2, cv, mix), _ = _fwd_even(
        x2, ev_norm_g, w["ev_in"], conv_a_w, ev_conv_a_b, ev_ln_a_g, ev_ln_a_b, conv_b_w, w["ev_out"], tm=tm, seq=seq)
    (h2, n1, p0, q0), _ = _fwd_mlp(h1, mlp_norm_g[0:1], w["w1_0"], w["w2_0"], 0, tm=tm)
    (h3, n2, s, cdf, sv, y), _ = _fwd_odd(h2, od_norm, w["od_in"], od_bias, od_lng, od_lnb, od_w_s[0], b_s_rows,
                                          w["od_out"], tm=tm)
    (n3, p1, q1, loss_part, dh4, dh4b, d_final_g), _ = _fwd_mlp(
        h3, mlp_norm_g[1:2], w["w1_1"], w["w2_1"], 1, tm=tm,
        head=(jnp.reshape(final_norm_g, (1, d)), tgt2))

    red = {}

    def swap(*names):
        return [red[nm].pair_swap() for nm in names] if distributed else []

    def share(*names):
        return [red[nm].pair_share() for nm in names] if distributed else []

    def took(routs, *steps):
        if distributed:
            for (nm, what), outs in zip(steps, routs):
                getattr(red[nm], what)(outs)

    swap_ids = iter(range(FIRST_SWAP_ID, FIRST_SWAP_ID + 8))

    def beside(name):
        if distributed:
            red[name].chips_beside(next(swap_ids))

    def big(lhs, rhs, name, col_shards, riders=()):
        if distributed:
            chip_sum, routs = _wgrad_pair(lhs, rhs, f"wgrad_{name}", col_shards=col_shards, riders=riders)
            red[name] = _GradReduce(name, chip_sum=_in_hbm(chip_sum))
        else:
            g, routs = _wgrad(lhs, rhs, f"wgrad_{name}", col_shards=col_shards)
            red[name] = _GradReduce(name, grad=g)
        return routs

    big(q1, dh4b, "w2_1", False)
    beside("w2_1")
    (dh3, dh3b, dp1, d_mlp_g1), _ = _bwd_mlp(dh4, h3, mlp_norm_g[1:2], p1, w["w1_1"], w["w2_1"], 1, tm=tm)
    big(n3, dp1, "w1_1", True)
    beside("w1_1")
    g, routs = _wgrad(y, dh3b, "wgrad_od_out", col_shards=False, riders=share("w2_1"))
    red["od_out"] = _GradReduce("od_out", grad=g)
    took(routs, ("w2_1", "took_share"))
    (dh2, dh2b, ds, d_od_norm, d_od_bin, d_od_lng, d_od_lnb, d_ws, d_bs), _ = _bwd_odd(
        dh3, h2, od_norm, s, cdf, sv, w["od_in"], od_lng, od_lnb, od_w_s[0], w["od_out"], tm=tm)
    routs = big(n2, ds, "od_in", True, riders=share("w1_1") + swap("od_out"))
    took(routs, ("w1_1", "took_share"), ("od_out", "took_pair"))
    beside("od_in")
    beside("od_out")
    half_groups = C_GROUPS // 2
    early = {"loss": loss_part, "od_w_s_lo": d_ws[:half_groups], "od_b_s": d_bs, "mlp_norm_g1": d_mlp_g1, "final_norm_g": d_final_g,
             "od_norm_g": d_od_norm, "od_b_in": d_od_bin, "od_ln_v_g": d_od_lng, "od_ln_v_b": d_od_lnb}
    share_early = [_ShareAll(list(early.values()))] if distributed else []
    routs = big(q0, dh2b, "w2_0", False, riders=share_early)
    landed_early = routs[0] if distributed else []
    beside("w2_0")
    (dh1, dh1b, dp0, d_mlp_g0), _ = _bwd_mlp(dh2, h1, mlp_norm_g[0:1], p0, w["w1_0"], w["w2_0"], 0, tm=tm)
    middle = {"od_w_s_hi": d_ws[half_groups:]}
    share_middle = [_ShareAll(list(middle.values()))] if distributed else []
    g, _ = _wgrad(mix, dh1b, "wgrad_ev_out", col_shards=False)
    red["ev_out"] = _GradReduce("ev_out", grad=g)
    routs = big(n1, dp0, "w1_0", True,
                riders=share("od_out") + share("od_in") + share("w2_0") + swap("ev_out") + share_middle)
    took(routs, ("od_out", "took_share"), ("od_in", "took_share"), ("w2_0", "took_share"), ("ev_out", "took_pair"))
    landed_middle = routs[4] if distributed else []
    beside("w1_0")
    beside("ev_out")

    (dx, dz, d_ev_norm, d_caw, d_cab, d_ev_lng, d_ev_lnb, d_cbw), _ = _bwd_even(
        dh1, x2, ev_norm_g, z, a2, cv, w["ev_in"], conv_a_w, ev_ln_a_g, ev_ln_a_b, conv_b_w, w["ev_out"], tm=tm, seq=seq)
    late = {"mlp_norm_g0": d_mlp_g0, "ev_norm_g": d_ev_norm, "ev_conv_a_b": d_cab, "ev_ln_a_g": d_ev_lng,
            "ev_ln_a_b": d_ev_lnb, "ev_conv_a_w": d_caw, "ev_conv_b_w": d_cbw}
    share_late = [_ShareAll(list(late.values()))] if distributed else []
    routs = big(n0, dz, "ev_in", True, riders=share("ev_out") + share("w1_0") + share_late)
    took(routs, ("ev_out", "took_share"), ("w1_0", "took_share"))
    beside("ev_in")
    own = {**early, **middle, **late}
    landed = dict(zip(own.keys(), landed_early + landed_middle + routs[2])) if distributed else None
    return dx, red, own, landed


def _rows128(a):
    rows = jnp.reshape(a, (-1, LANES))
    pad = (-rows.shape[0]) % SUBLANES
    return jnp.pad(rows, ((0, pad), (0, 0))) if pad else rows


def _pack(arrays):
    return jnp.concatenate([_rows128(a) for a in arrays], axis=0)


def _unpack(buf, shapes):
    out, r0 = [], 0
    for shp in shapes:
        size = 1
        for dim in shp:
            size *= dim
        nr = size // LANES
        out.append(jnp.reshape(buf[r0:r0 + nr], shp))
        r0 += nr + (-nr) % SUBLANES
    return out


def kernel(x, ev_norm_g, ev_w_in, ev_conv_a_w, ev_conv_a_b, ev_ln_a_g, ev_ln_a_b, ev_conv_b_w, ev_w_out, od_norm_g, od_w_in, od_b_in, od_ln_v_g, od_ln_v_b, od_w_s, od_b_s, od_w_out, mlp_norm_g, mlp_w1, mlp_w2, final_norm_g, loss_target, m_ev_norm_g, m_ev_w_in, m_ev_conv_a_w, m_ev_conv_a_b, m_ev_ln_a_g, m_ev_ln_a_b, m_ev_conv_b_w, m_ev_w_out, m_od_norm_g, m_od_w_in, m_od_b_in, m_od_ln_v_g, m_od_ln_v_b, m_od_w_s, m_od_b_s, m_od_w_out, m_mlp_norm_g, m_mlp_w1, m_mlp_w2, m_final_norm_g, v_ev_norm_g, v_ev_w_in, v_ev_conv_a_w, v_ev_conv_a_b, v_ev_ln_a_g, v_ev_ln_a_b, v_ev_conv_b_w, v_ev_w_out, v_od_norm_g, v_od_w_in, v_od_b_in, v_od_ln_v_g, v_od_ln_v_b, v_od_w_s, v_od_b_s, v_od_w_out, v_mlp_norm_g, v_mlp_w1, v_mlp_w2, v_final_norm_g):
    tm = TOKEN_TILE
    batch, seq, d = x.shape
    tokens = batch * seq
    x2 = jnp.reshape(x, (tokens, d))
    tgt2 = jnp.reshape(loss_target, (tokens, d))
    chip = 2 * lax.axis_index("x") + lax.axis_index("y")

    small_shapes = [(A_CONV_WIDTH, LANES), (B_CONV_WIDTH, LANES), (256,), (512,), (256,), (256,)]
    small_shard = _pack([ev_conv_a_w[0], ev_conv_b_w[0], od_norm_g[0], od_b_in[0], od_ln_v_g[0], od_ln_v_b[0]])
    small_shard = jnp.pad(small_shard, ((0, (-small_shard.shape[0]) % (4 * SUBLANES)), (0, 0)))
    first = [_place_shard(ev_w_in, 0, BF16, "place_ev_w_in"), _place_shard(ev_w_out, 0, BF16, "place_ev_w_out"),
             _place_shard(small_shard[None], 0, F32, "place_small")]
    staged = {
        "w1_0": _place_shard(mlp_w1, 0, BF16, "place_w1_0"), "w2_0": _place_shard(mlp_w2, 0, BF16, "place_w2_0"),
        "od_in": _place_shard(od_w_in, 0, BF16, "place_od_w_in"), "od_out": _place_shard(od_w_out, 0, BF16, "place_od_w_out"),
        "w1_1": _place_shard(mlp_w1, 1, BF16, "place_w1_1"), "w2_1": _place_shard(mlp_w2, 1, BF16, "place_w2_1"),
    }
    first = [_in_hbm(a) for a in first]
    staged = {nm: _in_hbm(a) for nm, a in staged.items()}
    g_ev_in, g_ev_out, g_small = _gather_beside(first, "gather_stage0", collective_id=1)
    gathered = {"ev_in": g_ev_in, "ev_out": g_ev_out}
    for stage, names in enumerate((("w1_0", "w2_0"), ("od_in", "od_out", "w1_1"), ("w2_1",))):
        done = _gather_beside([staged[nm] for nm in names], f"gather_stage{stage + 1}", collective_id=stage + 2)
        gathered.update(zip(names, done))
    small_all = jnp.reshape(_plain_copy(g_small, "small_weights_copy"), (N_CHIPS, -1, LANES))
    per_chip = [_unpack(small_all[q], small_shapes) for q in range(N_CHIPS)]
    conv_a_w = jnp.concatenate([pc[0] for pc in per_chip], axis=1)
    conv_b_w = jnp.concatenate([pc[1] for pc in per_chip], axis=1)
    od_norm = jnp.concatenate([pc[2] for pc in per_chip])[None, :]
    od_bias = jnp.concatenate([pc[3] for pc in per_chip])[None, :]
    od_lng = jnp.concatenate([pc[4] for pc in per_chip])[None, :]
    od_lnb = jnp.concatenate([pc[5] for pc in per_chip])[None, :]

    dx, red, own, landed = _forward_backward(
        x2, tgt2, gathered, conv_a_w, conv_b_w, od_norm, od_bias, od_lng, od_lnb,
        ev_norm_g, ev_conv_a_b, ev_ln_a_g, ev_ln_a_b, od_w_s, od_b_s, mlp_norm_g, final_norm_g, tm=tm, seq=seq)

    routs = _exchange([red["ev_in"].pair_share()], "reduce_tail")
    red["ev_in"].took_share(routs[0])

    given = {"ev_norm_g": (ev_norm_g, m_ev_norm_g, v_ev_norm_g), "ev_conv_a_b": (ev_conv_a_b, m_ev_conv_a_b, v_ev_conv_a_b),
             "ev_ln_a_g": (ev_ln_a_g, m_ev_ln_a_g, v_ev_ln_a_g), "ev_ln_a_b": (ev_ln_a_b, m_ev_ln_a_b, v_ev_ln_a_b),
             "od_w_s": (od_w_s, m_od_w_s, v_od_w_s), "od_b_s": (od_b_s, m_od_b_s, v_od_b_s),
             "mlp_norm_g": (mlp_norm_g, m_mlp_norm_g, v_mlp_norm_g), "final_norm_g": (final_norm_g, m_final_norm_g, v_final_norm_g),
             "ev_conv_a_w": (ev_conv_a_w, m_ev_conv_a_w, v_ev_conv_a_w), "ev_conv_b_w": (ev_conv_b_w, m_ev_conv_b_w, v_ev_conv_b_w),
             "od_norm_g": (od_norm_g, m_od_norm_g, v_od_norm_g), "od_b_in": (od_b_in, m_od_b_in, v_od_b_in),
             "od_ln_v_g": (od_ln_v_g, m_od_ln_v_g, v_od_ln_v_g), "od_ln_v_b": (od_ln_v_b, m_od_ln_v_b, v_od_ln_v_b)}
    shaped = {nm: tuple(jnp.reshape(a, shape) for a in given[nm]) for nm, shape, _, _ in SMALL_WEIGHTS}
    loss11, small_upd = _small_update(own, landed, shaped)
    loss = loss11[0, 0]
    upd = {nm: [jnp.reshape(o, given[nm][0].shape) for o in outs] for nm, outs in small_upd.items()}

    def big_update(wt, m, v, names, call):
        grads = [red[nm].reduced() for nm in names]
        shp3 = (len(grads),) + grads[0].shape
        outs = _adamw(jnp.reshape(wt, shp3), jnp.reshape(m, shp3), jnp.reshape(v, shp3), grads, call)
        return [jnp.reshape(o, wt.shape) for o in outs]

    upd["mlp_w2"] = big_update(mlp_w2, m_mlp_w2, v_mlp_w2, ["w2_0", "w2_1"], "adamw_mlp_w2")
    upd["mlp_w1"] = big_update(mlp_w1, m_mlp_w1, v_mlp_w1, ["w1_0", "w1_1"], "adamw_mlp_w1")
    upd["ev_w_in"] = big_update(ev_w_in, m_ev_w_in, v_ev_w_in, ["ev_in"], "adamw_ev_w_in")
    upd["ev_w_out"] = big_update(ev_w_out, m_ev_w_out, v_ev_w_out, ["ev_out"], "adamw_ev_w_out")
    upd["od_w_in"] = big_update(od_w_in, m_od_w_in, v_od_w_in, ["od_in"], "adamw_od_w_in")
    upd["od_w_out"] = big_update(od_w_out, m_od_w_out, v_od_w_out, ["od_out"], "adamw_od_w_out")

    order = ["ev_norm_g", "ev_w_in", "ev_conv_a_w", "ev_conv_a_b", "ev_ln_a_g", "ev_ln_a_b", "ev_conv_b_w", "ev_w_out",
             "od_norm_g", "od_w_in", "od_b_in", "od_ln_v_g", "od_ln_v_b", "od_w_s", "od_b_s", "od_w_out", "mlp_norm_g",
             "mlp_w1", "mlp_w2", "final_norm_g"]
    grad_x = jnp.reshape(dx, x.shape)
    return (loss, grad_x, *[upd[nm][0] for nm in order], *[upd[nm][1] for nm in order],
            *[upd[nm][2] for nm in order], *[upd[nm][3] for nm in order])
```

```python
import functools

import jax
import jax.numpy as jnp
from jax import lax
from jax.experimental import pallas as pl
from jax.experimental.pallas import tpu as pltpu
from jax.experimental.pallas import tpu_sc as plsc

F32 = jnp.float32
BF16 = jnp.bfloat16

D_MODEL = 1024
A_DIM = 512
B_DIM = 512
IN_EVEN = 2 * A_DIM + 3 * B_DIM
A_CONV_WIDTH = 31
B_CONV_WIDTH = 3
CHUNK = 128
C_GROUPS = 8
C_DIM = 1024
D_FF = 4096
RMS_EPS = 1e-6
LN_EPS = 1e-5
ADAM_LR = 0.001
ADAM_B1 = 0.9
ADAM_B2 = 0.999
ADAM_EPS = 1e-08
ADAM_WD = 0.01
ADAM_STEP = 10

N_CHIPS = 4
N_DEV = 8
TOKEN_TILE = 512
A_HALO = 32
B_HALO = 8
CONV_ROWS = 16
DW_TAPS = 4
ELEM_ROWS = 16
PAIR = 2 * CHUNK
LANES = 128
SUBLANES = 8
MXU_ROWS = 256
MIB = 1024 * 1024
MESH = pl.DeviceIdType.MESH
ANY = pl.BlockSpec(memory_space=pl.ANY)


def _dot(a, b):
    return lax.dot_general(a, b, (((1,), (0,)), ((), ())), preferred_element_type=F32)


def _dot_nt(a, b):
    return lax.dot_general(a, b, (((1,), (1,)), ((), ())), preferred_element_type=F32)


def _dot_tn(a, b):
    return lax.dot_general(a, b, (((0,), (0,)), ((), ())), preferred_element_type=F32)


def _params(vmem_mib, n_axes=1):
    return pltpu.CompilerParams(dimension_semantics=("arbitrary",) * n_axes, vmem_limit_bytes=vmem_mib * MIB)


def _row_spec(tm, cols, rev_nt=None):
    if rev_nt is None:
        return pl.BlockSpec((tm, cols), lambda i: (i, 0))
    return pl.BlockSpec((tm, cols), lambda i: (rev_nt - 1 - i, 0))


def _full_spec(shape):
    nd = len(shape)
    return pl.BlockSpec(shape, lambda i: (0,) * nd)


def _block_rows(rows, cap=512):
    best = SUBLANES
    for br in range(SUBLANES, min(rows, cap) + 1, SUBLANES):
        if rows % br == 0:
            best = br
    return best


FIRST_SWAP_ID = 5
N_LOADS = 2 * 2 * N_CHIPS


def _load_weights(loads, sems):
    @pl.when(pl.program_id(0) == 0)
    def _():
        copies = []
        for src, dst, rows_of_one in loads:
            r = src.shape[2]
            for q in range(N_CHIPS):
                for h in range(2):
                    part = dst.at[pl.ds((2 * q + h) * r, r)] if rows_of_one else dst.at[q, pl.ds(h * r, r)]
                    copies.append(pltpu.make_async_copy(src.at[q, h], part, sems.at[len(copies)]))
        for cp in copies:
            cp.start()
        for cp in copies:
            cp.wait()


def _rms_fwd(x, g):
    rstd = lax.rsqrt(jnp.mean(x * x, axis=-1, keepdims=True) + RMS_EPS)
    return x * rstd * g, rstd


def _rms_bwd(dn, x, rstd, g):
    a = dn * g
    xh = x * rstd
    dx = rstd * (a - xh * jnp.mean(a * xh, axis=-1, keepdims=True))
    dg = jnp.sum(dn * xh, axis=0, keepdims=True)
    return dx, dg


def _ln_stats(v):
    mu = jnp.mean(v, axis=-1, keepdims=True)
    xc = v - mu
    rs = lax.rsqrt(jnp.mean(xc * xc, axis=-1, keepdims=True) + LN_EPS)
    return xc * rs, rs


def _ln_bwd(dy, xhat, rs, g):
    dxh = dy * g
    dv = rs * (dxh - jnp.mean(dxh, axis=-1, keepdims=True) - xhat * jnp.mean(dxh * xhat, axis=-1, keepdims=True))
    return dv, jnp.sum(dy * xhat, axis=0, keepdims=True), jnp.sum(dy, axis=0, keepdims=True)


def _gelu_cdf(s):
    return 0.5 * (1.0 + lax.erf(s * 0.7071067811865476))


def _mesh_pos():
    return lax.axis_index("x"), lax.axis_index("y"), lax.axis_index("c")


def _other_chips(x, y):
    return [(1 - x, y), (x, 1 - y), (1 - x, 1 - y)]


def _remote(src, dst, send_sem, recv_sem, to):
    return pltpu.make_async_remote_copy(src_ref=src, dst_ref=dst, send_sem=send_sem, recv_sem=recv_sem,
                                        device_id=to, device_id_type=MESH)


def _like(arrays):
    return [jax.ShapeDtypeStruct(a.shape, a.dtype) for a in arrays]


class _PairSwap:
    def __init__(self, grads):
        self.ins = list(grads)
        self.out_shapes = [jax.ShapeDtypeStruct((g.shape[0],) + g.shape[2:], g.dtype) for g in grads]
        self.aliases = {}
        self.n_sems = len(grads)

    def _copies(self, ins, outs, send, recv):
        x, y, c = _mesh_pos()
        return [_remote(ins[t].at[:, 1 - c], outs[t], send.at[t], recv.at[t], (x, y, 1 - c)) for t in range(len(ins))]

    def start(self, ins, outs, send, recv):
        for cp in self._copies(ins, outs, send, recv):
            cp.start()

    def finish(self, ins, outs, send, recv):
        for cp in self._copies(ins, outs, send, recv):
            cp.wait()


class _ChipSwap:
    def __init__(self, parts):
        self.ins = list(parts)
        self.out_shapes = [jax.ShapeDtypeStruct((3,) + p.shape[1:], p.dtype) for p in parts]
        self.aliases = {}
        self.n_sems = 3 * len(parts)

    def _copies(self, ins, outs, send, recv):
        x, y, c = _mesh_pos()
        return [_remote(ins[t].at[2 * chip[0] + chip[1]], outs[t].at[k], send.at[3 * t + k], recv.at[3 * t + k], (*chip, c))
                for t in range(len(ins)) for k, chip in enumerate(_other_chips(x, y))]

    def start(self, ins, outs, send, recv):
        for cp in self._copies(ins, outs, send, recv):
            cp.start()

    def finish(self, ins, outs, send, recv):
        for cp in self._copies(ins, outs, send, recv):
            cp.wait()


class _PairShare:
    def __init__(self, fulls):
        self.ins = list(fulls)
        self.out_shapes = _like(fulls)
        self.aliases = {t: t for t in range(len(fulls))}
        self.n_sems = len(fulls)

    def _copies(self, ins, outs, send, recv):
        x, y, c = _mesh_pos()
        return [_remote(ins[t].at[c], outs[t].at[c], send.at[t], recv.at[t], (x, y, 1 - c)) for t in range(len(ins))]

    def start(self, ins, outs, send, recv):
        for cp in self._copies(ins, outs, send, recv):
            cp.start()

    def finish(self, ins, outs, send, recv):
        for cp in self._copies(ins, outs, send, recv):
            cp.wait()


class _ShareAll:
    def __init__(self, arrays):
        self.ins = list(arrays)
        self.out_shapes = [jax.ShapeDtypeStruct((N_DEV,) + a.shape, a.dtype) for a in arrays]
        self.aliases = {}
        self.n_sems = (N_DEV - 1) * len(arrays)

    def _peers(self):
        x, y, c = _mesh_pos()
        flips = [((r >> 2) & 1, (r >> 1) & 1, r & 1) for r in range(1, N_DEV)]
        return (x, y, c), [(x ^ fx, y ^ fy, c ^ fc) for fx, fy, fc in flips]

    def _sends(self, ins, outs, send, recv):
        (x, y, c), peers = self._peers()
        mine = 4 * x + 2 * y + c
        return [_remote(ins[a], outs[a].at[mine], send.at[7 * a + r], recv.at[7 * a + r], peer)
                for a in range(len(ins)) for r, peer in enumerate(peers)]

    def start(self, ins, outs, send, recv):
        for cp in self._sends(ins, outs, send, recv):
            cp.start()

    def finish(self, ins, outs, send, recv):
        (x, y, c), peers = self._peers()
        for a in range(len(ins)):
            for r, (px, py, pc) in enumerate(peers):
                blk = outs[a].at[4 * px + 2 * py + pc]
                _remote(blk, blk, send.at[7 * a + r], recv.at[7 * a + r], (x, y, c)).wait_recv()
        for cp in self._sends(ins, outs, send, recv):
            cp.wait_send()


def _gather_beside(bufs, name, collective_id):
    n = len(bufs)
    per = 7
    refs = [jax.new_ref(b, memory_space=pltpu.MemorySpace.HBM) for b in bufs]

    @pl.kernel(mesh=plsc.ScalarSubcoreMesh(axis_name="sequencer", num_cores=1), name=name,
               scratch_types=(pltpu.SemaphoreType.DMA((per * n,)), pltpu.SemaphoreType.DMA((per * n,))),
               compiler_params=pltpu.CompilerParams(collective_id=collective_id))
    def launch(send, recv):
        x, y, c = _mesh_pos()
        me, sibling = (x, y, c), (x, y, 1 - c)
        x_nbr, y_nbr = (1 - x, y, c), (x, 1 - y, c)
        mine, via_x, via_y, diag = 2 * x + y, 2 * (1 - x) + y, 2 * x + (1 - y), 2 * (1 - x) + (1 - y)
        barrier = pltpu.get_barrier_semaphore()
        peers = [x_nbr, y_nbr, sibling]
        for peer in peers:
            pl.semaphore_signal(barrier, inc=1, device_id=peer, device_id_type=MESH)
        pl.semaphore_wait(barrier, len(peers))

        def copy(t, k, src, dst, to):
            return _remote(src, dst, send.at[per * t + k], recv.at[per * t + k], to)

        def piece(t, chip, half, rows=None):
            blk = refs[t].at[chip, half]
            return blk if rows is None else blk.at[rows]

        started = []

        def go(cp):
            cp.start()
            started.append(cp)

        upper = [pl.ds(0, r.shape[2] // 2) for r in refs]
        lower = [pl.ds(r.shape[2] // 2, r.shape[2] // 2) for r in refs]
        for t in range(n):
            go(copy(t, 0, piece(t, mine, c), piece(t, mine, c), x_nbr))
            go(copy(t, 1, piece(t, mine, c), piece(t, mine, c), y_nbr))
        for t in range(n):
            copy(t, 0, piece(t, via_x, c), piece(t, via_x, c), me).wait_recv()
            go(copy(t, 2, piece(t, via_x, c, upper[t]), piece(t, via_x, c, upper[t]), y_nbr))
            go(copy(t, 4, piece(t, via_x, c), piece(t, via_x, c), sibling))
            copy(t, 1, piece(t, via_y, c), piece(t, via_y, c), me).wait_recv()
            go(copy(t, 3, piece(t, via_y, c, lower[t]), piece(t, via_y, c, lower[t]), x_nbr))
            go(copy(t, 5, piece(t, via_y, c), piece(t, via_y, c), sibling))
        for t in range(n):
            copy(t, 2, piece(t, diag, c, upper[t]), piece(t, diag, c, upper[t]), me).wait_recv()
            copy(t, 3, piece(t, diag, c, lower[t]), piece(t, diag, c, lower[t]), me).wait_recv()
            go(copy(t, 6, piece(t, diag, c), piece(t, diag, c), sibling))
        for t in range(n):
            for k, chip in ((4, via_x), (5, via_y), (6, diag)):
                copy(t, k, piece(t, chip, 1 - c), piece(t, chip, 1 - c), me).wait_recv()
        for cp in started:
            cp.wait_send()

    launch()
    return [r[...] for r in refs]


def _chip_swap_beside(parts, name, collective_id):
    src = jax.new_ref(parts, memory_space=pltpu.MemorySpace.HBM)
    dst = jax.empty_ref(jax.ShapeDtypeStruct((N_CHIPS - 1,) + parts.shape[1:], parts.dtype),
                        memory_space=pltpu.MemorySpace.HBM)
    swap = _ChipSwap([parts])

    @pl.kernel(mesh=plsc.ScalarSubcoreMesh(axis_name="sequencer", num_cores=1), name=name,
               scratch_types=(pltpu.SemaphoreType.DMA((N_CHIPS - 1,)), pltpu.SemaphoreType.DMA((N_CHIPS - 1,))),
               compiler_params=pltpu.CompilerParams(collective_id=collective_id))
    def launch(send, recv):
        x, y, c = _mesh_pos()
        barrier = pltpu.get_barrier_semaphore()
        peers = [(*chip, c) for chip in _other_chips(x, y)]
        for peer in peers:
            pl.semaphore_signal(barrier, inc=1, device_id=peer, device_id_type=MESH)
        pl.semaphore_wait(barrier, len(peers))
        swap.start([src], [dst], send, recv)
        swap.finish([src], [dst], send, recv)

    launch()
    return dst[...]


def _pallas(body, operands, *, name, grid, in_specs, out_specs, out_shape, scratch_shapes=(), vmem_mib=32, riders=(),
            prefetch=None):
    in_specs, out_specs, out_shape, scratch_shapes = list(in_specs), list(out_specs), list(out_shape), list(scratch_shapes)
    if not riders and prefetch is None:
        outs = pl.pallas_call(body, name=name, grid=grid, in_specs=in_specs, out_specs=out_specs, out_shape=out_shape,
                              scratch_shapes=scratch_shapes, compiler_params=_params(vmem_mib, len(grid)))(*operands)
        return list(outs), []
    n_in, n_out, n_scr = len(in_specs), len(out_specs), len(scratch_shapes)
    r_in = [len(r.ins) for r in riders]
    r_out = [len(r.out_shapes) for r in riders]
    steps = 1
    for g in grid:
        steps *= g

    n_pre = 0 if prefetch is None else 1

    def wrapped(*refs):
        refs = list(refs)
        pre, refs = refs[:n_pre], refs[n_pre:]
        ins, refs = refs[:n_in], refs[n_in:]
        rins = []
        for k in r_in:
            rins.append(refs[:k])
            refs = refs[k:]
        outs, refs = refs[:n_out], refs[n_out:]
        routs = []
        for k in r_out:
            routs.append(refs[:k])
            refs = refs[k:]
        scr, sems = refs[:n_scr], refs[n_scr:]
        step = 0
        for ax, g in enumerate(grid):
            step = step * g + pl.program_id(ax)

        def each(what):
            for j, r in enumerate(riders):
                getattr(r, what)(rins[j], routs[j], sems[2 * j], sems[2 * j + 1])

        if grid:
            pl.when(step == 0)(lambda: each("start"))
        else:
            each("start")
        body(*pre, *ins, *outs, *scr)
        if grid:
            pl.when(step == steps - 1)(lambda: each("finish"))
        else:
            each("finish")

    aliases, off_in, off_out = {}, n_pre + n_in, n_out
    for r, ki, ko in zip(riders, r_in, r_out):
        for i, o in r.aliases.items():
            aliases[off_in + i] = off_out + o
        off_in, off_out = off_in + ki, off_out + ko
    sems = []
    for r in riders:
        sems += [pltpu.SemaphoreType.DMA((r.n_sems,)), pltpu.SemaphoreType.DMA((r.n_sems,))]
    layout = dict(grid=grid, in_specs=in_specs + [ANY] * sum(r_in), out_specs=out_specs + [ANY] * sum(r_out),
                  scratch_shapes=scratch_shapes + sems)
    if prefetch is not None:
        layout = dict(grid_spec=pltpu.PrefetchScalarGridSpec(num_scalar_prefetch=1, **layout))
    res = pl.pallas_call(
        wrapped, name=name, **layout,
        out_shape=out_shape + [s for r in riders for s in r.out_shapes], input_output_aliases=aliases,
        compiler_params=pltpu.CompilerParams(dimension_semantics=("arbitrary",) * len(grid),
                                             vmem_limit_bytes=vmem_mib * MIB, has_side_effects=True),
    )(*([] if prefetch is None else [prefetch]), *operands, *[a for r in riders for a in r.ins])
    res = list(res)
    outs, res = res[:n_out], res[n_out:]
    routs = []
    for k in r_out:
        routs.append(res[:k])
        res = res[k:]
    return outs, routs


def _exchange(riders, name):
    return _pallas(lambda: None, [], name=name, grid=(), in_specs=[], out_specs=[], out_shape=[], riders=riders)[1]


def _in_hbm(a):
    return pltpu.with_memory_space_constraint(a, pltpu.HBM)


def _place_shard(w, layer, dtype, name):
    _, rows, cols = w.shape
    half = rows // 2
    br = _block_rows(half)
    nb = half // br
    mine = 2 * lax.axis_index("x") + lax.axis_index("y")

    def body(q_ref, w_ref, o_ref):
        o_ref[...] = w_ref[...].astype(dtype)

    return pl.pallas_call(
        body, name=name,
        grid_spec=pltpu.PrefetchScalarGridSpec(
            num_scalar_prefetch=1, grid=(2, nb),
            in_specs=[pl.BlockSpec((None, br, cols), lambda h, i, q: (layer, h * nb + i, 0))],
            out_specs=pl.BlockSpec((None, None, br, cols), lambda h, i, q: (q[0], h, i, 0))),
        out_shape=pltpu.HBM((N_CHIPS, 2, half, cols), dtype),
        compiler_params=_params(16, 2),
    )(jnp.reshape(mine, (1,)).astype(jnp.int32), _in_hbm(w))


def _plain_copy(a, name):
    def body(a_ref, o_ref):
        o_ref[...] = a_ref[...]

    vmem = pl.BlockSpec(memory_space=pltpu.VMEM)
    return pl.pallas_call(body, name=name, in_specs=[vmem], out_specs=vmem,
                          out_shape=jax.ShapeDtypeStruct(a.shape, a.dtype))(a)


def _add_pair(g, recv, name):
    _, _, r, cdim = g.shape
    br = _block_rows(r, 256)
    c = lax.axis_index("c")

    def body(c_ref, g_ref, r_ref, o_ref):
        o_ref[...] = (g_ref[...] + r_ref[...]).astype(BF16)

    return pl.pallas_call(
        body, name=name,
        grid_spec=pltpu.PrefetchScalarGridSpec(
            num_scalar_prefetch=1, grid=(N_CHIPS, r // br),
            in_specs=[pl.BlockSpec((None, None, br, cdim), lambda q, i, c_ref: (q, c_ref[0], i, 0)),
                      pl.BlockSpec((None, br, cdim), lambda q, i, c_ref: (q, i, 0))],
            out_specs=pl.BlockSpec((None, br, cdim), lambda q, i, c_ref: (q, i, 0))),
        out_shape=pltpu.HBM((N_CHIPS, r, cdim), BF16),
        compiler_params=_params(16, 2),
    )(jnp.reshape(c, (1,)).astype(jnp.int32), _in_hbm(g), _in_hbm(recv))


def _add_chips(own, recv, name):
    _, r, cdim = own.shape
    br = _block_rows(r, 256)
    x, y, c = _mesh_pos()

    def body(pos_ref, own_ref, r_ref, o_ref):
        acc = own_ref[...].astype(F32)
        for k in range(3):
            acc = acc + r_ref[k].astype(F32)
        o_ref[...] = acc

    return pl.pallas_call(
        body, name=name,
        grid_spec=pltpu.PrefetchScalarGridSpec(
            num_scalar_prefetch=1, grid=(r // br,),
            in_specs=[pl.BlockSpec((None, br, cdim), lambda i, pos: (pos[0], i, 0)),
                      pl.BlockSpec((3, br, cdim), lambda i, pos: (0, i, 0))],
            out_specs=pl.BlockSpec((None, br, cdim), lambda i, pos: (pos[1], i, 0))),
        out_shape=pltpu.HBM((2, r, cdim), F32),
        compiler_params=_params(16, 1),
    )(jnp.stack([2 * x + y, c]).astype(jnp.int32), _in_hbm(own), _in_hbm(recv))


def _adam_math(w, m, v, g):
    c1 = 1.0 / (1.0 - ADAM_B1 ** ADAM_STEP)
    c2 = 1.0 / (1.0 - ADAM_B2 ** ADAM_STEP)
    m_new = ADAM_B1 * m + (1.0 - ADAM_B1) * g
    v_new = ADAM_B2 * v + (1.0 - ADAM_B2) * (g * g)
    return -ADAM_LR * ((m_new * c1) / (jnp.sqrt(v_new * c2) + ADAM_EPS) + ADAM_WD * w), m_new, v_new


SMALL_WEIGHTS = [
    ("ev_norm_g", (1, D_MODEL), ["ev_norm_g"], None), ("ev_conv_a_b", (1, A_DIM), ["ev_conv_a_b"], None),
    ("ev_ln_a_g", (1, A_DIM), ["ev_ln_a_g"], None), ("ev_ln_a_b", (1, A_DIM), ["ev_ln_a_b"], None),
    ("od_w_s", (C_GROUPS, CHUNK, CHUNK), ["od_w_s_lo", "od_w_s_hi"], None), ("od_b_s", (C_GROUPS, CHUNK), ["od_b_s"], None),
    ("mlp_norm_g", (2, D_MODEL), ["mlp_norm_g0", "mlp_norm_g1"], None), ("final_norm_g", (1, D_MODEL), ["final_norm_g"], None),
    ("ev_conv_a_w", (A_CONV_WIDTH, A_DIM // N_CHIPS), ["ev_conv_a_w"], A_DIM // N_CHIPS),
    ("ev_conv_b_w", (B_CONV_WIDTH, B_DIM // N_CHIPS), ["ev_conv_b_w"], B_DIM // N_CHIPS),
    ("od_norm_g", (1, D_MODEL // N_CHIPS), ["od_norm_g"], D_MODEL // N_CHIPS),
    ("od_b_in", (1, 2 * C_DIM // N_CHIPS), ["od_b_in"], 2 * C_DIM // N_CHIPS),
    ("od_ln_v_g", (1, C_DIM // N_CHIPS), ["od_ln_v_g"], C_DIM // N_CHIPS),
    ("od_ln_v_b", (1, C_DIM // N_CHIPS), ["od_ln_v_b"], C_DIM // N_CHIPS),
]


def _small_update(own, landed, weights):
    names = list(own.keys())
    n_g, n_w = len(names), len(SMALL_WEIGHTS)

    def body(*refs):
        refs = list(refs)
        own_refs = dict(zip(names, refs[:n_g]))
        land_refs = dict(zip(names, refs[n_g:2 * n_g]))
        wmv = [refs[2 * n_g + 3 * i:2 * n_g + 3 * i + 3] for i in range(n_w)]
        o0 = 2 * n_g + 3 * n_w
        loss_ref = refs[o0]
        outs = [refs[o0 + 1 + 4 * i:o0 + 5 + 4 * i] for i in range(n_w)]
        acc = dict(zip(names, refs[o0 + 1 + 4 * n_w:]))
        x, y, c = _mesh_pos()
        mine, chip = 4 * x + 2 * y + c, 2 * x + y

        for nm in names:
            for d in range(N_DEV):
                def add(term, nm=nm, d=d):
                    acc[nm][...] = term if d == 0 else acc[nm][...] + term
                pl.when(mine == d)(lambda nm=nm, add=add: add(own_refs[nm][...]))
                pl.when(mine != d)(lambda nm=nm, d=d, add=add: add(land_refs[nm][d]))
        loss_ref[...] = acc["loss"][...]

        def update(i, rows, g):
            w_ref, m_ref, v_ref = wmv[i]
            delta, m_new, v_new = _adam_math(w_ref[rows], m_ref[rows], v_ref[rows], g)
            for ref, val in zip(outs[i], (g, delta, m_new, v_new)):
                ref[rows] = val

        for i, (_, shape, grads, per_chip) in enumerate(SMALL_WEIGHTS):
            for row, gname in enumerate(grads):
                per_grad = shape[0] // len(grads)
                rows = slice(row * per_grad, (row + 1) * per_grad)
                if per_chip is None:
                    update(i, rows, acc[gname][...])
                else:
                    for q in range(N_CHIPS):
                        pl.when(chip == q)(lambda i=i, rows=rows, gname=gname, q=q, per_chip=per_chip:
                                           update(i, rows, acc[gname][:, q * per_chip:(q + 1) * per_chip]))

    operands = [own[nm] for nm in names] + [landed[nm] for nm in names]
    for nm, _, _, _ in SMALL_WEIGHTS:
        operands += list(weights[nm])
    out_shape = [jax.ShapeDtypeStruct((1, 1), F32)]
    for _, shape, _, _ in SMALL_WEIGHTS:
        out_shape += [jax.ShapeDtypeStruct(shape, F32)] * 4
    res = pl.pallas_call(
        body, name="small_update", grid=(1,),
        in_specs=[_full_spec(a.shape) for a in operands], out_specs=[_full_spec(s.shape) for s in out_shape],
        out_shape=out_shape, scratch_shapes=[pltpu.VMEM(own[nm].shape, F32) for nm in names],
        compiler_params=_params(32, 1),
    )(*[_in_hbm(a) for a in operands])
    return res[0], {nm: res[1 + 4 * i:5 + 4 * i] for i, (nm, _, _, _) in enumerate(SMALL_WEIGHTS)}


def _adamw(w, m, v, grads, name, cap=256):
    layers, r, cdim = w.shape
    br = _block_rows(r, cap if cdim > LANES else 1024)
    blocks = r // br

    def body(*refs):
        w_ref, m_ref, v_ref = refs[:3]
        g_refs = refs[3:3 + layers]
        go_ref, d_ref, mo_ref, vo_ref = refs[3 + layers:]
        layer = pl.program_id(0)
        for l in range(layers):
            @pl.when(layer == l)
            def _(l=l):
                g = g_refs[l][...]
                go_ref[...] = g
                d_ref[...], mo_ref[...], vo_ref[...] = _adam_math(w_ref[...], m_ref[...], v_ref[...], g)

    spec3 = pl.BlockSpec((None, br, cdim), lambda l, i: (l, i, 0))
    g_specs = [pl.BlockSpec((br, cdim), lambda l, i, own=own: (jnp.clip(i + (l - own) * blocks, 0, blocks - 1), 0))
               for own in range(layers)]
    out = jax.ShapeDtypeStruct((layers, r, cdim), F32)
    outs, _ = _pallas(body, [_in_hbm(a) for a in (w, m, v, *grads)], name=name, grid=(layers, blocks),
                      in_specs=[spec3, spec3, spec3] + g_specs, out_specs=[spec3] * 4, out_shape=[out] * 4,
                      vmem_mib=32 if cap <= 256 else 48)
    return outs


def _fill_shifted(buf, rows):
    for b in range(1, SUBLANES):
        buf[b, 0:rows - SUBLANES, :] = buf[0, b:b + rows - SUBLANES, :]


def _window(buf, start, size):
    return buf[start % SUBLANES, start - start % SUBLANES:start - start % SUBLANES + size, :]


def _conv31(src, w_ref, r0, base, init):
    acc = init
    for k in range(A_CONV_WIDTH):
        acc = acc + w_ref[k:k + 1, :] * _window(src, base + k + r0, CONV_ROWS)
    return acc


def _fwd_even(x, norm_g, w_in, conv_a_w, conv_a_b, ln_g, ln_b, conv_b_w, w_out, *, tm, seq, riders=()):
    tokens = x.shape[0]
    nt, tps = tokens // tm, seq // tm

    def body(x_ref, g_ref, win_hbm, caw_ref, cab_ref, lng_ref, lnb_ref, cbw_ref, wout_hbm,
             h_ref, n_ref, z_ref, a2_ref, cv_ref, mix_ref, win_v, wout_v, pa, pb, sem):
        i = pl.program_id(0)

        _load_weights([(win_hbm, win_v, False), (wout_hbm, wout_v, True)], sem)

        xv = x_ref[...]
        nf, _ = _rms_fwd(xv, g_ref[...])
        n = nf.astype(BF16)
        n_ref[...] = n
        z = jnp.concatenate([_dot(n, win_v[j]) for j in range(N_CHIPS)], axis=1)
        z_ref[...] = z.astype(BF16)
        a_val, a_gate = z[:, 0:A_DIM], z[:, A_DIM:2 * A_DIM]
        b_gate, c_gate, b_val = z[:, 1024:1536], z[:, 1536:2048], z[:, 2048:2560]

        first = (i % tps) == 0

        @pl.when(first)
        def _():
            pa[0, 0:A_HALO, :] = jnp.zeros((A_HALO, A_DIM), F32)
            pb[0:B_HALO, :] = jnp.zeros((B_HALO, B_DIM), F32)

        @pl.when(jnp.logical_not(first))
        def _():
            pa[0, 0:A_HALO, :] = pa[0, tm:tm + A_HALO, :]
            pb[0:B_HALO, :] = pb[tm:tm + B_HALO, :]

        pa[0, A_HALO:A_HALO + tm, :] = a_val * jax.nn.sigmoid(a_gate)
        pb[B_HALO:B_HALO + tm, :] = c_gate * b_val
        _fill_shifted(pa, A_HALO + tm)
        bias = jnp.broadcast_to(cab_ref[...], (CONV_ROWS, A_DIM))
        for r0 in range(0, tm, CONV_ROWS):
            a2_ref[r0:r0 + CONV_ROWS, :] = _conv31(pa, caw_ref, r0, A_HALO - (A_CONV_WIDTH - 1), bias)
        xhat, _ = _ln_stats(a2_ref[...])
        a3 = xhat * lng_ref[...] + lnb_ref[...]
        a4 = a3 * jax.nn.sigmoid(a3)
        cv = cbw_ref[0:1, :] * pb[B_HALO - 2:B_HALO - 2 + tm, :]
        cv = cv + cbw_ref[1:2, :] * pb[B_HALO - 1:B_HALO - 1 + tm, :]
        cv = cv + cbw_ref[2:3, :] * pb[B_HALO:B_HALO + tm, :]
        cv_ref[...] = cv.astype(BF16)
        mix = jnp.concatenate([a4, b_gate * cv], axis=1).astype(BF16)
        mix_ref[...] = mix
        h_ref[...] = xv + _dot(mix, wout_v[...])

    shp = lambda cols, dt: jax.ShapeDtypeStruct((tokens, cols), dt)
    return _pallas(
        body, [x, norm_g, w_in, conv_a_w, conv_a_b, ln_g, ln_b, conv_b_w, w_out], name="fwd_even", grid=(nt,),
        in_specs=[_row_spec(tm, D_MODEL), _full_spec((1, D_MODEL)), ANY, _full_spec((A_CONV_WIDTH, A_DIM)),
                  _full_spec((1, A_DIM)), _full_spec((1, A_DIM)), _full_spec((1, A_DIM)),
                  _full_spec((B_CONV_WIDTH, B_DIM)), ANY],
        out_specs=[_row_spec(tm, D_MODEL), _row_spec(tm, D_MODEL), _row_spec(tm, IN_EVEN), _row_spec(tm, A_DIM),
                   _row_spec(tm, B_DIM), _row_spec(tm, D_MODEL)],
        out_shape=[shp(D_MODEL, F32), shp(D_MODEL, BF16), shp(IN_EVEN, BF16), shp(A_DIM, F32), shp(B_DIM, BF16),
                   shp(D_MODEL, BF16)],
        scratch_shapes=[pltpu.VMEM((N_CHIPS, D_MODEL, IN_EVEN // N_CHIPS), BF16), pltpu.VMEM((D_MODEL, D_MODEL), BF16),
                        pltpu.VMEM((SUBLANES, A_HALO + tm, A_DIM), F32), pltpu.VMEM((B_HALO + tm, B_DIM), F32),
                        pltpu.SemaphoreType.DMA((N_LOADS,))],
        vmem_mib=56, riders=riders)


def _loss_tail(xv, g, target, loss_ref, dh_ref, dhb_ref, dg_ref):
    @pl.when(pl.program_id(0) == 0)
    def _():
        loss_ref[...] = jnp.zeros((1, 1), F32)
        dg_ref[...] = jnp.zeros((1, D_MODEL), F32)

    out, rstd = _rms_fwd(xv, g)
    err = out - target
    per_token = jnp.sum(err * err, axis=1, keepdims=True) * (1.0 / D_MODEL)
    loss_ref[...] += 0.5 * jnp.sum(per_token, axis=0, keepdims=True)
    dx, dg = _rms_bwd(err * (1.0 / D_MODEL), xv, rstd, g)
    dh_ref[...] = dx
    dhb_ref[...] = dx.astype(BF16)
    dg_ref[...] += dg


def _fwd_mlp(h, norm_g, w1, w2, layer, *, tm, riders=(), head=None):
    tokens = h.shape[0]
    nt = tokens // tm
    fs = D_FF // N_CHIPS
    n_in = 4 if head is None else 6

    def body(*refs):
        h_ref, g_ref, w1_hbm, w2_hbm = refs[:4]
        w1_v, w2_v, sem = refs[-3:]
        outs = refs[n_in:-3]
        n_ref, p_ref, q_ref = outs[1:4] if head is None else outs[0:3]
        _load_weights([(w1_hbm, w1_v, False), (w2_hbm, w2_v, False)], sem)

        xv = h_ref[...]
        nf, _ = _rms_fwd(xv, g_ref[...])
        n = nf.astype(BF16)
        n_ref[...] = n
        acc = xv
        for j in range(N_CHIPS):
            p = _dot(n, w1_v[j])
            p_ref[:, j * fs:(j + 1) * fs] = p.astype(BF16)
            r = jnp.maximum(p, 0.0)
            q = (r * r).astype(BF16)
            q_ref[:, j * fs:(j + 1) * fs] = q
            acc = acc + _dot(q, w2_v[j])
        if head is None:
            outs[0][...] = acc
        else:
            _loss_tail(acc, refs[4][...], refs[5][...], *outs[3:7])

    shp = lambda cols, dt: jax.ShapeDtypeStruct((tokens, cols), dt)
    saved_specs = [_row_spec(tm, D_MODEL), _row_spec(tm, D_FF), _row_spec(tm, D_FF)]
    saved_shapes = [shp(D_MODEL, BF16), shp(D_FF, BF16), shp(D_FF, BF16)]
    if head is None:
        operands, in_specs = [h, norm_g, w1, w2], [_row_spec(tm, D_MODEL), _full_spec((1, D_MODEL)), ANY, ANY]
        out_specs, out_shape = [_row_spec(tm, D_MODEL)] + saved_specs, [shp(D_MODEL, F32)] + saved_shapes
    else:
        operands = [h, norm_g, w1, w2, *head]
        in_specs = [_row_spec(tm, D_MODEL), _full_spec((1, D_MODEL)), ANY, ANY, _full_spec((1, D_MODEL)), _row_spec(tm, D_MODEL)]
        out_specs = saved_specs + [_full_spec((1, 1)), _row_spec(tm, D_MODEL), _row_spec(tm, D_MODEL), _full_spec((1, D_MODEL))]
        out_shape = saved_shapes + [jax.ShapeDtypeStruct((1, 1), F32), shp(D_MODEL, F32), shp(D_MODEL, BF16),
                                    jax.ShapeDtypeStruct((1, D_MODEL), F32)]
    return _pallas(
        body, operands, name=f"fwd_mlp{layer}", grid=(nt,), in_specs=in_specs, out_specs=out_specs, out_shape=out_shape,
        scratch_shapes=[pltpu.VMEM((N_CHIPS, D_MODEL, fs), BF16), pltpu.VMEM((N_CHIPS, fs, D_MODEL), BF16),
                        pltpu.SemaphoreType.DMA((N_LOADS,))],
        vmem_mib=56, riders=riders)


def _tril_mask():
    row = lax.broadcasted_iota(jnp.int32, (CHUNK, CHUNK), 0)
    col = lax.broadcasted_iota(jnp.int32, (CHUNK, CHUNK), 1)
    return row >= col


def _triu_mask():
    row = lax.broadcasted_iota(jnp.int32, (CHUNK, CHUNK), 0)
    col = lax.broadcasted_iota(jnp.int32, (CHUNK, CHUNK), 1)
    return row <= col


def _fwd_odd(h, norm_g, w_in, b_in, ln_g, ln_b, w_s, b_s_rows, w_out, *, tm, riders=()):
    tokens = h.shape[0]
    nt = tokens // tm
    cs = 2 * C_DIM // N_CHIPS

    def body(h_ref, g_ref, win_hbm, bin_ref, lng_ref, lnb_ref, ws_ref, bs_ref, wout_hbm,
             ho_ref, n_ref, s_ref, cdf_ref, sv_ref, y_ref, win_v, wout_v, bd, sem):
        _load_weights([(win_hbm, win_v, False), (wout_hbm, wout_v, True)], sem)

        @pl.when(pl.program_id(0) == 0)
        def _():
            mask = _tril_mask()
            bd[...] = jnp.zeros(bd.shape, BF16)
            for g in range(C_GROUPS):
                w = jnp.where(mask, ws_ref[g], 0.0).astype(BF16)
                bd[g, 0:CHUNK, 0:CHUNK] = w
                bd[g, CHUNK:PAIR, CHUNK:PAIR] = w

        xv = h_ref[...]
        nf, _ = _rms_fwd(xv, g_ref[...])
        n = nf.astype(BF16)
        n_ref[...] = n
        s = jnp.concatenate([_dot(n, win_v[j]) for j in range(N_CHIPS)], axis=1) + bin_ref[...]
        s_ref[...] = s.astype(BF16)
        cdf = _gelu_cdf(s)
        cdf_ref[...] = cdf.astype(BF16)
        zz = s * cdf
        u, v = zz[:, 0:C_DIM], zz[:, C_DIM:2 * C_DIM]
        xhat, _ = _ln_stats(v)
        vn = (xhat * lng_ref[...] + lnb_ref[...]).astype(BF16)
        for g in range(C_GROUPS):
            cols = slice(g * CHUNK, (g + 1) * CHUNK)
            bias = jnp.concatenate([bs_ref[g], bs_ref[g]], axis=0)
            for r0 in range(0, tm, PAIR):
                sv = _dot(bd[g], vn[r0:r0 + PAIR, cols]) + bias
                sv_ref[r0:r0 + PAIR, cols] = sv.astype(BF16)
                y_ref[r0:r0 + PAIR, cols] = (u[r0:r0 + PAIR, cols] * sv).astype(BF16)
        ho_ref[...] = xv + _dot(y_ref[...], wout_v[...])

    shp = lambda cols, dt: jax.ShapeDtypeStruct((tokens, cols), dt)
    return _pallas(
        body, [h, norm_g, w_in, b_in, ln_g, ln_b, w_s, b_s_rows, w_out], name="fwd_odd", grid=(nt,),
        in_specs=[_row_spec(tm, D_MODEL), _full_spec((1, D_MODEL)), ANY, _full_spec((1, 2 * C_DIM)),
                  _full_spec((1, C_DIM)), _full_spec((1, C_DIM)), _full_spec((C_GROUPS, CHUNK, CHUNK)),
                  _full_spec((C_GROUPS, CHUNK, CHUNK)), ANY],
        out_specs=[_row_spec(tm, D_MODEL), _row_spec(tm, D_MODEL), _row_spec(tm, 2 * C_DIM), _row_spec(tm, 2 * C_DIM),
                   _row_spec(tm, C_DIM), _row_spec(tm, C_DIM)],
        out_shape=[shp(D_MODEL, F32), shp(D_MODEL, BF16), shp(2 * C_DIM, BF16), shp(2 * C_DIM, BF16), shp(C_DIM, BF16),
                   shp(C_DIM, BF16)],
        scratch_shapes=[pltpu.VMEM((N_CHIPS, D_MODEL, cs), BF16), pltpu.VMEM((C_DIM, D_MODEL), BF16),
                        pltpu.VMEM((C_GROUPS, PAIR, PAIR), BF16), pltpu.SemaphoreType.DMA((N_LOADS,))],
        vmem_mib=56, riders=riders)


def _bwd_mlp(dh, h, norm_g, p, w1, w2, layer, *, tm, riders=()):
    tokens = h.shape[0]
    nt = tokens // tm
    fs = D_FF // N_CHIPS

    def body(dh_ref, h_ref, g_ref, p_ref, w1_hbm, w2_hbm, dx_ref, dxb_ref, dp_ref, dg_ref, w1_v, w2_v, sem):
        @pl.when(pl.program_id(0) == 0)
        def _():
            dg_ref[...] = jnp.zeros((1, D_MODEL), F32)

        _load_weights([(w1_hbm, w1_v, False), (w2_hbm, w2_v, False)], sem)

        dhv = dh_ref[...]
        dhb = dhv.astype(BF16)
        dn = jnp.zeros((tm, D_MODEL), F32)
        for j in range(N_CHIPS):
            dq = _dot_nt(dhb, w2_v[j])
            r = jnp.maximum(p_ref[:, j * fs:(j + 1) * fs].astype(F32), 0.0)
            dp = ((2.0 * r) * dq).astype(BF16)
            dp_ref[:, j * fs:(j + 1) * fs] = dp
            dn = dn + _dot_nt(dp, w1_v[j])
        xv = h_ref[...]
        g = g_ref[...]
        _, rstd = _rms_fwd(xv, g)
        dx, dg = _rms_bwd(dn, xv, rstd, g)
        dx_ref[...] = dhv + dx
        dxb_ref[...] = (dhv + dx).astype(BF16)
        dg_ref[...] += dg

    return _pallas(
        body, [dh, h, norm_g, p, w1, w2], name=f"bwd_mlp{layer}", grid=(nt,),
        in_specs=[_row_spec(tm, D_MODEL), _row_spec(tm, D_MODEL), _full_spec((1, D_MODEL)), _row_spec(tm, D_FF), ANY, ANY],
        out_specs=[_row_spec(tm, D_MODEL), _row_spec(tm, D_MODEL), _row_spec(tm, D_FF), _full_spec((1, D_MODEL))],
        out_shape=[jax.ShapeDtypeStruct((tokens, D_MODEL), F32), jax.ShapeDtypeStruct((tokens, D_MODEL), BF16),
                   jax.ShapeDtypeStruct((tokens, D_FF), BF16), jax.ShapeDtypeStruct((1, D_MODEL), F32)],
        scratch_shapes=[pltpu.VMEM((N_CHIPS, D_MODEL, fs), BF16), pltpu.VMEM((N_CHIPS, fs, D_MODEL), BF16),
                        pltpu.SemaphoreType.DMA((N_LOADS,))],
        vmem_mib=56, riders=riders)


def _bwd_odd(dh, h, norm_g, s, cdf, sv, w_in, ln_g, ln_b, w_s, w_out, *, tm, riders=()):
    tokens = h.shape[0]
    nt = tokens // tm
    cs = 2 * C_DIM // N_CHIPS

    def body(dh_ref, h_ref, g_ref, s_ref, cdf_ref, sv_ref, win_hbm, lng_ref, lnb_ref, ws_ref, wout_hbm,
             dx_ref, dxb_ref, ds_ref, dg_ref, dbin_ref, dlng_ref, dlnb_ref, dws_ref, dbs_ref,
             win_v, wout_v, bdt, dws_acc, dbs_acc, dvn, sem):
        i = pl.program_id(0)

        _load_weights([(win_hbm, win_v, False), (wout_hbm, wout_v, True)], sem)

        @pl.when(i == 0)
        def _():
            mask_t = _triu_mask()
            bdt[...] = jnp.zeros(bdt.shape, BF16)
            for g in range(C_GROUPS):
                wt = jnp.where(mask_t, ws_ref[g].T, 0.0).astype(BF16)
                bdt[g, 0:CHUNK, 0:CHUNK] = wt
                bdt[g, CHUNK:PAIR, CHUNK:PAIR] = wt
            dws_acc[...] = jnp.zeros(dws_acc.shape, F32)
            dbs_acc[...] = jnp.zeros(dbs_acc.shape, F32)
            dg_ref[...] = jnp.zeros(dg_ref.shape, F32)
            dbin_ref[...] = jnp.zeros(dbin_ref.shape, F32)
            dlng_ref[...] = jnp.zeros(dlng_ref.shape, F32)
            dlnb_ref[...] = jnp.zeros(dlnb_ref.shape, F32)

        dhv = dh_ref[...]
        dy = _dot_nt(dhv.astype(BF16), wout_v[...])
        sf = s_ref[...].astype(F32)
        cdf = cdf_ref[...].astype(F32)
        pdf = jnp.exp(-0.5 * sf * sf) * 0.3989422804014327
        zz = sf * cdf
        dgelu = cdf + sf * pdf
        u, v = zz[:, 0:C_DIM], zz[:, C_DIM:2 * C_DIM]
        xhat, rs = _ln_stats(v)
        lng = lng_ref[...]
        vn = (xhat * lng + lnb_ref[...]).astype(BF16)
        du = dy * sv_ref[...].astype(F32)
        dsv = dy * u
        dsvb = dsv.astype(BF16)
        for g in range(C_GROUPS):
            cols = slice(g * CHUNK, (g + 1) * CHUNK)
            for r0 in range(0, tm, PAIR):
                blk = dsvb[r0:r0 + PAIR, cols]
                dvn[r0:r0 + PAIR, cols] = _dot(bdt[g], blk)
                dws_acc[g] += _dot_nt(blk, vn[r0:r0 + PAIR, cols])
                dbs_acc[g] += dsv[r0:r0 + CHUNK, cols] + dsv[r0 + CHUNK:r0 + PAIR, cols]
        dv, dlng, dlnb = _ln_bwd(dvn[...], xhat, rs, lng)
        dlng_ref[...] += dlng
        dlnb_ref[...] += dlnb
        ds = jnp.concatenate([du, dv], axis=1) * dgelu
        dbin_ref[...] += jnp.sum(ds, axis=0, keepdims=True)
        dsb = ds.astype(BF16)
        ds_ref[...] = dsb
        dn = jnp.zeros((tm, D_MODEL), F32)
        for j in range(N_CHIPS):
            dn = dn + _dot_nt(dsb[:, j * cs:(j + 1) * cs], win_v[j])
        xv = h_ref[...]
        g = g_ref[...]
        _, rstd = _rms_fwd(xv, g)
        dx, dg = _rms_bwd(dn, xv, rstd, g)
        dx_ref[...] = dhv + dx
        dxb_ref[...] = (dhv + dx).astype(BF16)
        dg_ref[...] += dg

        @pl.when(i == nt - 1)
        def _():
            mask = _tril_mask()
            for g in range(C_GROUPS):
                full = dws_acc[g]
                dws_ref[g] = jnp.where(mask, full[0:CHUNK, 0:CHUNK] + full[CHUNK:PAIR, CHUNK:PAIR], 0.0)
                dbs_ref[g:g + 1, :] = jnp.sum(dbs_acc[g].T, axis=0, keepdims=True)

    row = lambda cols: jax.ShapeDtypeStruct((1, cols), F32)
    return _pallas(
        body, [dh, h, norm_g, s, cdf, sv, w_in, ln_g, ln_b, w_s, w_out], name="bwd_odd", grid=(nt,),
        in_specs=[_row_spec(tm, D_MODEL), _row_spec(tm, D_MODEL), _full_spec((1, D_MODEL)), _row_spec(tm, 2 * C_DIM),
                  _row_spec(tm, 2 * C_DIM), _row_spec(tm, C_DIM), ANY, _full_spec((1, C_DIM)), _full_spec((1, C_DIM)),
                  _full_spec((C_GROUPS, CHUNK, CHUNK)), ANY],
        out_specs=[_row_spec(tm, D_MODEL), _row_spec(tm, D_MODEL), _row_spec(tm, 2 * C_DIM), _full_spec((1, D_MODEL)),
                   _full_spec((1, 2 * C_DIM)),
                   _full_spec((1, C_DIM)), _full_spec((1, C_DIM)), _full_spec((C_GROUPS, CHUNK, CHUNK)),
                   _full_spec((C_GROUPS, CHUNK))],
        out_shape=[jax.ShapeDtypeStruct((tokens, D_MODEL), F32), jax.ShapeDtypeStruct((tokens, D_MODEL), BF16),
                   jax.ShapeDtypeStruct((tokens, 2 * C_DIM), BF16),
                   row(D_MODEL), row(2 * C_DIM), row(C_DIM), row(C_DIM),
                   jax.ShapeDtypeStruct((C_GROUPS, CHUNK, CHUNK), F32), jax.ShapeDtypeStruct((C_GROUPS, CHUNK), F32)],
        scratch_shapes=[pltpu.VMEM((N_CHIPS, D_MODEL, cs), BF16), pltpu.VMEM((C_DIM, D_MODEL), BF16),
                        pltpu.VMEM((C_GROUPS, PAIR, PAIR), BF16), pltpu.VMEM((C_GROUPS, PAIR, PAIR), F32),
                        pltpu.VMEM((C_GROUPS, CHUNK, CHUNK), F32), pltpu.VMEM((tm, C_DIM), F32),
                        pltpu.SemaphoreType.DMA((N_LOADS,))],
        vmem_mib=56, riders=riders)


def _bwd_even(dh, x, norm_g, z, a2, cv, w_in, conv_a_w, ln_g, ln_b, conv_b_w, w_out, *, tm, seq, riders=()):
    tokens = x.shape[0]
    nt, tps = tokens // tm, seq // tm
    ws = IN_EVEN // N_CHIPS

    def body(dh_ref, x_ref, g_ref, z_ref, a2_ref, cv_ref, win_hbm, caw_ref, lng_ref, lnb_ref, cbw_ref, wout_hbm,
             dx_ref, dz_ref, dg_ref, dcaw_ref, dcab_ref, dlng_ref, dlnb_ref, dcbw_ref,
             win_v, wout_v, ea, eb, a1s, da1s, sigs, wide, dw_acc, sem):
        i = pl.program_id(0)

        _load_weights([(win_hbm, win_v, False), (wout_hbm, wout_v, True)], sem)

        @pl.when(i == 0)
        def _():
            dw_acc[...] = jnp.zeros(dw_acc.shape, F32)
            for ref in (dg_ref, dcab_ref, dlng_ref, dlnb_ref, dcbw_ref):
                ref[...] = jnp.zeros(ref.shape, F32)

        last = ((nt - 1 - i) % tps) == tps - 1

        @pl.when(last)
        def _():
            ea[0, tm:tm + A_HALO, :] = jnp.zeros((A_HALO, A_DIM), F32)
            eb[tm:tm + B_HALO, :] = jnp.zeros((B_HALO, B_DIM), F32)

        @pl.when(jnp.logical_not(last))
        def _():
            ea[0, tm:tm + A_HALO, :] = ea[0, 0:A_HALO, :]
            eb[tm:tm + B_HALO, :] = eb[0:B_HALO, :]

        wide[...] = _dot_nt(dh_ref[...].astype(BF16), wout_v[...])
        lng, lnb = lng_ref[...], lnb_ref[...]
        zero_row = jnp.zeros((1, A_DIM), F32)
        dlng, dlnb, dcab = zero_row, zero_row, zero_row
        for r0 in range(0, tm, ELEM_ROWS):
            rows = slice(r0, r0 + ELEM_ROWS)
            a_val, a_gate = z_ref[rows, 0:A_DIM].astype(F32), z_ref[rows, A_DIM:2 * A_DIM].astype(F32)
            xhat, rs = _ln_stats(a2_ref[rows, :])
            a3 = xhat * lng + lnb
            sg = jax.nn.sigmoid(a3)
            da3 = wide[rows, 0:A_DIM] * (sg * (1.0 + a3 * (1.0 - sg)))
            da2, g_part, b_part = _ln_bwd(da3, xhat, rs, lng)
            dlng, dlnb, dcab = dlng + g_part, dlnb + b_part, dcab + jnp.sum(da2, axis=0, keepdims=True)
            ea[0, rows, :] = da2
            eb[rows, :] = wide[rows, A_DIM:A_DIM + B_DIM] * z_ref[rows, 1024:1536].astype(F32)
            sig = jax.nn.sigmoid(a_gate)
            sigs[rows, :] = sig
            a1s[rows, :] = a_val * sig
        dlng_ref[...] += dlng
        dlnb_ref[...] += dlnb
        dcab_ref[...] += dcab
        _fill_shifted(ea, tm + A_HALO)
        for r0 in range(0, tm, CONV_ROWS):
            acc = jnp.zeros((CONV_ROWS, A_DIM), F32)
            for j in range(A_CONV_WIDTH):
                acc = acc + caw_ref[A_CONV_WIDTH - 1 - j:A_CONV_WIDTH - j, :] * _window(ea, r0 + j, CONV_ROWS)
            da1s[r0:r0 + CONV_ROWS, :] = acc
        for j0 in range(0, A_CONV_WIDTH, DW_TAPS):
            taps = range(j0, min(j0 + DW_TAPS, A_CONV_WIDTH))
            part = [jnp.zeros((CONV_ROWS, A_DIM), F32) for _ in taps]
            for r0 in range(0, tm, CONV_ROWS):
                a1c = a1s[r0:r0 + CONV_ROWS, :]
                for u, j in enumerate(taps):
                    part[u] = part[u] + _window(ea, r0 + j, CONV_ROWS) * a1c
            for u, j in enumerate(taps):
                dw_acc[A_CONV_WIDTH - 1 - j] += part[u]
        dcbw = [jnp.zeros((1, B_DIM), F32) for _ in range(B_CONV_WIDTH)]
        for r0 in range(0, tm, ELEM_ROWS):
            rows = slice(r0, r0 + ELEM_ROWS)
            da1, sig = da1s[rows, :], sigs[rows, :]
            dz_ref[rows, 0:A_DIM] = (da1 * sig).astype(BF16)
            dz_ref[rows, A_DIM:2 * A_DIM] = (da1 * z_ref[rows, 0:A_DIM].astype(F32) * (sig * (1.0 - sig))).astype(BF16)
            c_gate, b_val = z_ref[rows, 1536:2048].astype(F32), z_ref[rows, 2048:2560].astype(F32)
            dz_ref[rows, 1024:1536] = (wide[rows, A_DIM:A_DIM + B_DIM] * cv_ref[rows, :].astype(F32)).astype(BF16)
            cb = c_gate * b_val
            dcb = jnp.zeros((ELEM_ROWS, B_DIM), F32)
            for j in range(B_CONV_WIDTH):
                k = B_CONV_WIDTH - 1 - j
                sl = eb[r0 + j:r0 + j + ELEM_ROWS, :]
                dcb = dcb + cbw_ref[k:k + 1, :] * sl
                dcbw[k] = dcbw[k] + jnp.sum(sl * cb, axis=0, keepdims=True)
            dz_ref[rows, 1536:2048] = (dcb * b_val).astype(BF16)
            dz_ref[rows, 2048:2560] = (dcb * c_gate).astype(BF16)
        for k in range(B_CONV_WIDTH):
            dcbw_ref[k:k + 1, :] += dcbw[k]
        dn = jnp.zeros((tm, D_MODEL), F32)
        for j in range(N_CHIPS):
            dn = dn + _dot_nt(dz_ref[:, j * ws:(j + 1) * ws], win_v[j])
        wide[...] = dn
        g = g_ref[...]
        dg = jnp.zeros((1, D_MODEL), F32)
        for r0 in range(0, tm, ELEM_ROWS):
            rows = slice(r0, r0 + ELEM_ROWS)
            xv = x_ref[rows, :]
            _, rstd = _rms_fwd(xv, g)
            dx, dg_part = _rms_bwd(wide[rows, :], xv, rstd, g)
            dx_ref[rows, :] = dh_ref[rows, :] + dx
            dg = dg + dg_part
        dg_ref[...] += dg

        @pl.when(i == nt - 1)
        def _():
            for k in range(A_CONV_WIDTH):
                dcaw_ref[k:k + 1, :] = jnp.sum(dw_acc[k], axis=0, keepdims=True)

    row = lambda cols: jax.ShapeDtypeStruct((1, cols), F32)
    rs_ = functools.partial(_row_spec, rev_nt=nt)
    return _pallas(
        body, [dh, x, norm_g, z, a2, cv, w_in, conv_a_w, ln_g, ln_b, conv_b_w, w_out], name="bwd_even", grid=(nt,),
        in_specs=[rs_(tm, D_MODEL), rs_(tm, D_MODEL), _full_spec((1, D_MODEL)), rs_(tm, IN_EVEN), rs_(tm, A_DIM),
                  rs_(tm, B_DIM), ANY, _full_spec((A_CONV_WIDTH, A_DIM)), _full_spec((1, A_DIM)), _full_spec((1, A_DIM)),
                  _full_spec((B_CONV_WIDTH, B_DIM)), ANY],
        out_specs=[rs_(tm, D_MODEL), rs_(tm, IN_EVEN), _full_spec((1, D_MODEL)), _full_spec((A_CONV_WIDTH, A_DIM)),
                   _full_spec((1, A_DIM)), _full_spec((1, A_DIM)), _full_spec((1, A_DIM)), _full_spec((B_CONV_WIDTH, B_DIM))],
        out_shape=[jax.ShapeDtypeStruct((tokens, D_MODEL), F32), jax.ShapeDtypeStruct((tokens, IN_EVEN), BF16),
                   row(D_MODEL), jax.ShapeDtypeStruct((A_CONV_WIDTH, A_DIM), F32), row(A_DIM), row(A_DIM), row(A_DIM),
                   jax.ShapeDtypeStruct((B_CONV_WIDTH, B_DIM), F32)],
        scratch_shapes=[pltpu.VMEM((N_CHIPS, D_MODEL, ws), BF16), pltpu.VMEM((D_MODEL, D_MODEL), BF16),
                        pltpu.VMEM((SUBLANES, tm + A_HALO, A_DIM), F32), pltpu.VMEM((tm + B_HALO, B_DIM), F32),
                        pltpu.VMEM((tm, A_DIM), F32), pltpu.VMEM((tm, A_DIM), F32), pltpu.VMEM((tm, A_DIM), F32),
                        pltpu.VMEM((tm, D_MODEL), F32),
                        pltpu.VMEM((A_CONV_WIDTH, CONV_ROWS, A_DIM), F32), pltpu.SemaphoreType.DMA((N_LOADS,))],
        vmem_mib=56, riders=riders)


def _wgrad(a, b, name, *, col_shards, riders=()):
    tokens, m = a.shape
    n = b.shape[1]
    kc = 512
    if col_shards:
        bm, bn = m // 2, n // N_CHIPS
        grid = (2, N_CHIPS)
        out_spec = pl.BlockSpec((None, None, bm, bn), lambda i, j: (j, i, 0, 0))
    elif m // 8 >= MXU_ROWS:
        bm, bn = m // 8, n
        grid = (8, 1)
        out_spec = pl.BlockSpec((None, None, bm, bn), lambda i, j: (i // 2, i % 2, 0, 0))
    else:
        bm, bn = m // N_CHIPS, n
        grid = (N_CHIPS, 1)
        out_spec = pl.BlockSpec((None, 2, bm // 2, bn), lambda i, j: (i, 0, 0, 0))

    def body(a_ref, b_ref, o_ref):
        acc = jnp.zeros((bm, bn), F32)
        for k0 in range(0, tokens, kc):
            acc = acc + _dot_tn(a_ref[k0:k0 + kc, :].astype(BF16), b_ref[k0:k0 + kc, :].astype(BF16))
        if len(o_ref.shape) == 3:
            o_ref[0] = acc[0:bm // 2]
            o_ref[1] = acc[bm // 2:bm]
        else:
            o_ref[...] = acc

    out_rows = m // 2 if col_shards else m // 8
    outs, routs = _pallas(
        body, [a, b], name=name, grid=grid,
        in_specs=[pl.BlockSpec((tokens, bm), lambda i, j: (0, i)), pl.BlockSpec((tokens, bn), lambda i, j: (0, j))],
        out_specs=[out_spec], out_shape=[jax.ShapeDtypeStruct((N_CHIPS, 2, out_rows, bn), F32)],
        vmem_mib=56, riders=riders)
    return outs[0], routs


def _wgrad_pair(a, b, name, *, col_shards, riders=()):
    tokens, m = a.shape
    n = b.shape[1]
    kc = 512
    c0 = lax.axis_index("c")

    def half(ph, pre):
        return (ph + 1 + pre[0]) % 2

    if col_shards:
        bm, bn = m // 2, n // N_CHIPS
        a_spec = pl.BlockSpec((tokens, bm), lambda ph, q, pre: (0, half(ph, pre)))
        b_spec = pl.BlockSpec((tokens, bn), lambda ph, q, pre: (0, q))
    else:
        bm, bn = m // 8, n
        a_spec = pl.BlockSpec((tokens, bm), lambda ph, q, pre: (0, 2 * q + half(ph, pre)))
        b_spec = pl.BlockSpec((tokens, bn), lambda ph, q, pre: (0, 0))

    def body(pre_ref, a_ref, b_ref, o_ref, give, got, send_sems, recv_sems):
        ph, q = pl.program_id(0), pl.program_id(1)
        acc = jnp.zeros((bm, bn), F32)
        for k0 in range(0, tokens, kc):
            acc = acc + _dot_tn(a_ref[k0:k0 + kc, :].astype(BF16), b_ref[k0:k0 + kc, :].astype(BF16))
        x, y, cc = _mesh_pos()

        def tile(t):
            return _remote(give.at[t], got.at[t], send_sems.at[t], recv_sems.at[t], (x, y, 1 - cc))

        @pl.when(ph == 0)
        def _():
            give[q] = acc
            tile(q).start()

        @pl.when(ph == 1)
        def _():
            tile(q).wait_recv()
            o_ref[...] = (acc + got[q]).astype(BF16)

        @pl.when((ph == 1) & (q == N_CHIPS - 1))
        def _():
            for t in range(N_CHIPS):
                tile(t).wait_send()

    outs, routs = _pallas(
        body, [a, b], name=name, grid=(2, N_CHIPS), in_specs=[a_spec, b_spec],
        out_specs=[pl.BlockSpec((None, bm, bn), lambda ph, q, pre: (ph * q, 0, 0))],
        out_shape=[jax.ShapeDtypeStruct((N_CHIPS, bm, bn), BF16)],
        scratch_shapes=[pltpu.VMEM((N_CHIPS, bm, bn), F32), pltpu.VMEM((N_CHIPS, bm, bn), F32),
                        pltpu.SemaphoreType.DMA((N_CHIPS,)), pltpu.SemaphoreType.DMA((N_CHIPS,))],
        vmem_mib=56, riders=riders, prefetch=jnp.reshape(c0, (1,)).astype(jnp.int32))
    return outs[0], routs


class _GradReduce:
    def __init__(self, name, grad=None, chip_sum=None):
        self.name, self.grad, self.chip_sum = name, grad, chip_sum
        self.full = None

    def pair_swap(self):
        return _PairSwap([self.grad])

    def took_pair(self, outs):
        self.chip_sum = _in_hbm(_add_pair(self.grad, outs[0], f"pair_sum_{self.name}"))

    def took_chips(self, outs):
        self.full = _in_hbm(_add_chips(self.chip_sum, outs[0], f"chip_sum_{self.name}"))

    def chips_beside(self, collective_id):
        self.took_chips([_chip_swap_beside(self.chip_sum, f"chip_swap_{self.name}", collective_id)])

    def pair_share(self):
        return _PairShare([self.full])

    def took_share(self, outs):
        self.full = outs[0]

    def reduced(self):
        return jnp.reshape(self.full, (2 * self.full.shape[1], self.full.shape[2]))


def _forward_backward(x2, tgt2, w, conv_a_w, conv_b_w, od_norm, od_bias, od_lng, od_lnb,
                      ev_norm_g, ev_conv_a_b, ev_ln_a_g, ev_ln_a_b, od_w_s, od_b_s, mlp_norm_g, final_norm_g,
                      *, tm, seq, distributed=True):
    d = x2.shape[1]
    b_s_rows = jnp.broadcast_to(od_b_s[0][:, :, None], (C_GROUPS, CHUNK, CHUNK))
    (h1, n0, z, a2, cv, mix), _ = _fwd_even(
        x2, ev_norm_g, w["ev_in"], conv_a_w, ev_conv_a_b, ev_ln_a_g, ev_ln_a_b, conv_b_w, w["ev_out"], tm=tm, seq=seq)
    (h2, n1, p0, q0), _ = _fwd_mlp(h1, mlp_norm_g[0:1], w["w1_0"], w["w2_0"], 0, tm=tm)
    (h3, n2, s, cdf, sv, y), _ = _fwd_odd(h2, od_norm, w["od_in"], od_bias, od_lng, od_lnb, od_w_s[0], b_s_rows,
                                          w["od_out"], tm=tm)
    (n3, p1, q1, loss_part, dh4, dh4b, d_final_g), _ = _fwd_mlp(
        h3, mlp_norm_g[1:2], w["w1_1"], w["w2_1"], 1, tm=tm,
        head=(jnp.reshape(final_norm_g, (1, d)), tgt2))

    red = {}

    def swap(*names):
        return [red[nm].pair_swap() for nm in names] if distributed else []

    def share(*names):
        return [red[nm].pair_share() for nm in names] if distributed else []

    def took(routs, *steps):
        if distributed:
            for (nm, what), outs in zip(steps, routs):
                getattr(red[nm], what)(outs)

    swap_ids = iter(range(FIRST_SWAP_ID, FIRST_SWAP_ID + 8))

    def beside(name):
        if distributed:
            red[name].chips_beside(next(swap_ids))

    def big(lhs, rhs, name, col_shards, riders=()):
        if distributed:
            chip_sum, routs = _wgrad_pair(lhs, rhs, f"wgrad_{name}", col_shards=col_shards, riders=riders)
            red[name] = _GradReduce(name, chip_sum=_in_hbm(chip_sum))
        else:
            g, routs = _wgrad(lhs, rhs, f"wgrad_{name}", col_shards=col_shards)
            red[name] = _GradReduce(name, grad=g)
        return routs

    big(q1, dh4b, "w2_1", False)
    beside("w2_1")
    (dh3, dh3b, dp1, d_mlp_g1), _ = _bwd_mlp(dh4, h3, mlp_norm_g[1:2], p1, w["w1_1"], w["w2_1"], 1, tm=tm)
    big(n3, dp1, "w1_1", True)
    beside("w1_1")
    g, routs = _wgrad(y, dh3b, "wgrad_od_out", col_shards=False, riders=share("w2_1"))
    red["od_out"] = _GradReduce("od_out", grad=g)
    took(routs, ("w2_1", "took_share"))
    (dh2, dh2b, ds, d_od_norm, d_od_bin, d_od_lng, d_od_lnb, d_ws, d_bs), _ = _bwd_odd(
        dh3, h2, od_norm, s, cdf, sv, w["od_in"], od_lng, od_lnb, od_w_s[0], w["od_out"], tm=tm)
    routs = big(n2, ds, "od_in", True, riders=share("w1_1") + swap("od_out"))
    took(routs, ("w1_1", "took_share"), ("od_out", "took_pair"))
    beside("od_in")
    beside("od_out")
    half_groups = C_GROUPS // 2
    early = {"loss": loss_part, "od_w_s_lo": d_ws[:half_groups], "od_b_s": d_bs, "mlp_norm_g1": d_mlp_g1, "final_norm_g": d_final_g,
             "od_norm_g": d_od_norm, "od_b_in": d_od_bin, "od_ln_v_g": d_od_lng, "od_ln_v_b": d_od_lnb}
    share_early = [_ShareAll(list(early.values()))] if distributed else []
    routs = big(q0, dh2b, "w2_0", False, riders=share_early)
    landed_early = routs[0] if distributed else []
    beside("w2_0")
    (dh1, dh1b, dp0, d_mlp_g0), _ = _bwd_mlp(dh2, h1, mlp_norm_g[0:1], p0, w["w1_0"], w["w2_0"], 0, tm=tm)
    middle = {"od_w_s_hi": d_ws[half_groups:]}
    share_middle = [_ShareAll(list(middle.values()))] if distributed else []
    g, _ = _wgrad(mix, dh1b, "wgrad_ev_out", col_shards=False)
    red["ev_out"] = _GradReduce("ev_out", grad=g)
    routs = big(n1, dp0, "w1_0", True,
                riders=share("od_out") + share("od_in") + share("w2_0") + swap("ev_out") + share_middle)
    took(routs, ("od_out", "took_share"), ("od_in", "took_share"), ("w2_0", "took_share"), ("ev_out", "took_pair"))
    landed_middle = routs[4] if distributed else []
    beside("w1_0")
    beside("ev_out")

    (dx, dz, d_ev_norm, d_caw, d_cab, d_ev_lng, d_ev_lnb, d_cbw), _ = _bwd_even(
        dh1, x2, ev_norm_g, z, a2, cv, w["ev_in"], conv_a_w, ev_ln_a_g, ev_ln_a_b, conv_b_w, w["ev_out"], tm=tm, seq=seq)
    late = {"mlp_norm_g0": d_mlp_g0, "ev_norm_g": d_ev_norm, "ev_conv_a_b": d_cab, "ev_ln_a_g": d_ev_lng,
            "ev_ln_a_b": d_ev_lnb, "ev_conv_a_w": d_caw, "ev_conv_b_w": d_cbw}
    share_late = [_ShareAll(list(late.values()))] if distributed else []
    routs = big(n0, dz, "ev_in", True, riders=share("ev_out") + share("w1_0") + share_late)
    took(routs, ("ev_out", "took_share"), ("w1_0", "took_share"))
    beside("ev_in")
    own = {**early, **middle, **late}
    landed = dict(zip(own.keys(), landed_early + landed_middle + routs[2])) if distributed else None
    return dx, red, own, landed


def _rows128(a):
    rows = jnp.reshape(a, (-1, LANES))
    pad = (-rows.shape[0]) % SUBLANES
    return jnp.pad(rows, ((0, pad), (0, 0))) if pad else rows


def _pack(arrays):
    return jnp.concatenate([_rows128(a) for a in arrays], axis=0)


def _unpack(buf, shapes):
    out, r0 = [], 0
    for shp in shapes:
        size = 1
        for dim in shp:
            size *= dim
        nr = size // LANES
        out.append(jnp.reshape(buf[r0:r0 + nr], shp))
        r0 += nr + (-nr) % SUBLANES
    return out


def kernel(x, ev_norm_g, ev_w_in, ev_conv_a_w, ev_conv_a_b, ev_ln_a_g, ev_ln_a_b, ev_conv_b_w, ev_w_out, od_norm_g, od_w_in, od_b_in, od_ln_v_g, od_ln_v_b, od_w_s, od_b_s, od_w_out, mlp_norm_g, mlp_w1, mlp_w2, final_norm_g, loss_target, m_ev_norm_g, m_ev_w_in, m_ev_conv_a_w, m_ev_conv_a_b, m_ev_ln_a_g, m_ev_ln_a_b, m_ev_conv_b_w, m_ev_w_out, m_od_norm_g, m_od_w_in, m_od_b_in, m_od_ln_v_g, m_od_ln_v_b, m_od_w_s, m_od_b_s, m_od_w_out, m_mlp_norm_g, m_mlp_w1, m_mlp_w2, m_final_norm_g, v_ev_norm_g, v_ev_w_in, v_ev_conv_a_w, v_ev_conv_a_b, v_ev_ln_a_g, v_ev_ln_a_b, v_ev_conv_b_w, v_ev_w_out, v_od_norm_g, v_od_w_in, v_od_b_in, v_od_ln_v_g, v_od_ln_v_b, v_od_w_s, v_od_b_s, v_od_w_out, v_mlp_norm_g, v_mlp_w1, v_mlp_w2, v_final_norm_g):
    tm = TOKEN_TILE
    batch, seq, d = x.shape
    tokens = batch * seq
    x2 = jnp.reshape(x, (tokens, d))
    tgt2 = jnp.reshape(loss_target, (tokens, d))
    chip = 2 * lax.axis_index("x") + lax.axis_index("y")

    small_shapes = [(A_CONV_WIDTH, LANES), (B_CONV_WIDTH, LANES), (256,), (512,), (256,), (256,)]
    small_shard = _pack([ev_conv_a_w[0], ev_conv_b_w[0], od_norm_g[0], od_b_in[0], od_ln_v_g[0], od_ln_v_b[0]])
    small_shard = jnp.pad(small_shard, ((0, (-small_shard.shape[0]) % (4 * SUBLANES)), (0, 0)))
    first = [_place_shard(ev_w_in, 0, BF16, "place_ev_w_in"), _place_shard(ev_w_out, 0, BF16, "place_ev_w_out"),
             _place_shard(small_shard[None], 0, F32, "place_small")]
    staged = {
        "w1_0": _place_shard(mlp_w1, 0, BF16, "place_w1_0"), "w2_0": _place_shard(mlp_w2, 0, BF16, "place_w2_0"),
        "od_in": _place_shard(od_w_in, 0, BF16, "place_od_w_in"), "od_out": _place_shard(od_w_out, 0, BF16, "place_od_w_out"),
        "w1_1": _place_shard(mlp_w1, 1, BF16, "place_w1_1"), "w2_1": _place_shard(mlp_w2, 1, BF16, "place_w2_1"),
    }
    first = [_in_hbm(a) for a in first]
    staged = {nm: _in_hbm(a) for nm, a in staged.items()}
    g_ev_in, g_ev_out, g_small = _gather_beside(first, "gather_stage0", collective_id=1)
    gathered = {"ev_in": g_ev_in, "ev_out": g_ev_out}
    for stage, names in enumerate((("w1_0", "w2_0"), ("od_in", "od_out", "w1_1"), ("w2_1",))):
        done = _gather_beside([staged[nm] for nm in names], f"gather_stage{stage + 1}", collective_id=stage + 2)
        gathered.update(zip(names, done))
    small_all = jnp.reshape(_plain_copy(g_small, "small_weights_copy"), (N_CHIPS, -1, LANES))
    per_chip = [_unpack(small_all[q], small_shapes) for q in range(N_CHIPS)]
    conv_a_w = jnp.concatenate([pc[0] for pc in per_chip], axis=1)
    conv_b_w = jnp.concatenate([pc[1] for pc in per_chip], axis=1)
    od_norm = jnp.concatenate([pc[2] for pc in per_chip])[None, :]
    od_bias = jnp.concatenate([pc[3] for pc in per_chip])[None, :]
    od_lng = jnp.concatenate([pc[4] for pc in per_chip])[None, :]
    od_lnb = jnp.concatenate([pc[5] for pc in per_chip])[None, :]

    dx, red, own, landed = _forward_backward(
        x2, tgt2, gathered, conv_a_w, conv_b_w, od_norm, od_bias, od_lng, od_lnb,
        ev_norm_g, ev_conv_a_b, ev_ln_a_g, ev_ln_a_b, od_w_s, od_b_s, mlp_norm_g, final_norm_g, tm=tm, seq=seq)

    routs = _exchange([red["ev_in"].pair_share()], "reduce_tail")
    red["ev_in"].took_share(routs[0])

    given = {"ev_norm_g": (ev_norm_g, m_ev_norm_g, v_ev_norm_g), "ev_conv_a_b": (ev_conv_a_b, m_ev_conv_a_b, v_ev_conv_a_b),
             "ev_ln_a_g": (ev_ln_a_g, m_ev_ln_a_g, v_ev_ln_a_g), "ev_ln_a_b": (ev_ln_a_b, m_ev_ln_a_b, v_ev_ln_a_b),
             "od_w_s": (od_w_s, m_od_w_s, v_od_w_s), "od_b_s": (od_b_s, m_od_b_s, v_od_b_s),
             "mlp_norm_g": (mlp_norm_g, m_mlp_norm_g, v_mlp_norm_g), "final_norm_g": (final_norm_g, m_final_norm_g, v_final_norm_g),
             "ev_conv_a_w": (ev_conv_a_w, m_ev_conv_a_w, v_ev_conv_a_w), "ev_conv_b_w": (ev_conv_b_w, m_ev_conv_b_w, v_ev_conv_b_w),
             "od_norm_g": (od_norm_g, m_od_norm_g, v_od_norm_g), "od_b_in": (od_b_in, m_od_b_in, v_od_b_in),
             "od_ln_v_g": (od_ln_v_g, m_od_ln_v_g, v_od_ln_v_g), "od_ln_v_b": (od_ln_v_b, m_od_ln_v_b, v_od_ln_v_b)}
    shaped = {nm: tuple(jnp.reshape(a, shape) for a in given[nm]) for nm, shape, _, _ in SMALL_WEIGHTS}
    loss11, small_upd = _small_update(own, landed, shaped)
    loss = loss11[0, 0]
    upd = {nm: [jnp.reshape(o, given[nm][0].shape) for o in outs] for nm, outs in small_upd.items()}

    def big_update(wt, m, v, names, call, cap=256):
        grads = [red[nm].reduced() for nm in names]
        shp3 = (len(grads),) + grads[0].shape
        outs = _adamw(jnp.reshape(wt, shp3), jnp.reshape(m, shp3), jnp.reshape(v, shp3), grads, call, cap)
        return [jnp.reshape(o, wt.shape) for o in outs]

    upd["mlp_w2"] = big_update(mlp_w2, m_mlp_w2, v_mlp_w2, ["w2_0", "w2_1"], "adamw_mlp_w2", 128)
    upd["mlp_w1"] = big_update(mlp_w1, m_mlp_w1, v_mlp_w1, ["w1_0", "w1_1"], "adamw_mlp_w1", 512)
    upd["ev_w_in"] = big_update(ev_w_in, m_ev_w_in, v_ev_w_in, ["ev_in"], "adamw_ev_w_in")
    upd["ev_w_out"] = big_update(ev_w_out, m_ev_w_out, v_ev_w_out, ["ev_out"], "adamw_ev_w_out")
    upd["od_w_in"] = big_update(od_w_in, m_od_w_in, v_od_w_in, ["od_in"], "adamw_od_w_in")
    upd["od_w_out"] = big_update(od_w_out, m_od_w_out, v_od_w_out, ["od_out"], "adamw_od_w_out")

    order = ["ev_norm_g", "ev_w_in", "ev_conv_a_w", "ev_conv_a_b", "ev_ln_a_g", "ev_ln_a_b", "ev_conv_b_w", "ev_w_out",
             "od_norm_g", "od_w_in", "od_b_in", "od_ln_v_g", "od_ln_v_b", "od_w_s", "od_b_s", "od_w_out", "mlp_norm_g",
             "mlp_w1", "mlp_w2", "final_norm_g"]
    grad_x = jnp.reshape(dx, x.shape)
    return (loss, grad_x, *[upd[nm][0] for nm in order], *[upd[nm][1] for nm in order],
            *[upd[nm][2] for nm in order], *[upd[nm][3] for nm in order])
```

```python
import functools

import jax
import jax.numpy as jnp
from jax import lax
from jax.experimental import pallas as pl
from jax.experimental.pallas import tpu as pltpu
from jax.experimental.pallas import tpu_sc as plsc

F32 = jnp.float32
BF16 = jnp.bfloat16

D_MODEL = 1024
A_DIM = 512
B_DIM = 512
IN_EVEN = 2 * A_DIM + 3 * B_DIM
A_CONV_WIDTH = 31
B_CONV_WIDTH = 3
CHUNK = 128
C_GROUPS = 8
C_DIM = 1024
D_FF = 4096
RMS_EPS = 1e-6
LN_EPS = 1e-5
ADAM_LR = 0.001
ADAM_B1 = 0.9
ADAM_B2 = 0.999
ADAM_EPS = 1e-08
ADAM_WD = 0.01
ADAM_STEP = 10

N_CHIPS = 4
N_DEV = 8
TOKEN_TILE = 512
A_HALO = 32
B_HALO = 8
CONV_ROWS = 16
DW_TAPS = 4
ELEM_ROWS = 16
PAIR = 2 * CHUNK
LANES = 128
SUBLANES = 8
MXU_ROWS = 256
MIB = 1024 * 1024
MESH = pl.DeviceIdType.MESH
ANY = pl.BlockSpec(memory_space=pl.ANY)


def _dot(a, b):
    return lax.dot_general(a, b, (((1,), (0,)), ((), ())), preferred_element_type=F32)


def _dot_nt(a, b):
    return lax.dot_general(a, b, (((1,), (1,)), ((), ())), preferred_element_type=F32)


def _dot_tn(a, b):
    return lax.dot_general(a, b, (((0,), (0,)), ((), ())), preferred_element_type=F32)


def _params(vmem_mib, n_axes=1):
    return pltpu.CompilerParams(dimension_semantics=("arbitrary",) * n_axes, vmem_limit_bytes=vmem_mib * MIB)


def _row_spec(tm, cols, rev_nt=None):
    if rev_nt is None:
        return pl.BlockSpec((tm, cols), lambda i: (i, 0))
    return pl.BlockSpec((tm, cols), lambda i: (rev_nt - 1 - i, 0))


def _full_spec(shape):
    nd = len(shape)
    return pl.BlockSpec(shape, lambda i: (0,) * nd)


def _block_rows(rows, cap=512):
    best = SUBLANES
    for br in range(SUBLANES, min(rows, cap) + 1, SUBLANES):
        if rows % br == 0:
            best = br
    return best


FIRST_SWAP_ID = 5
N_LOADS = 2 * 2 * N_CHIPS


def _load_weights(loads, sems):
    @pl.when(pl.program_id(0) == 0)
    def _():
        copies = []
        for src, dst, rows_of_one in loads:
            r = src.shape[2]
            for q in range(N_CHIPS):
                for h in range(2):
                    part = dst.at[pl.ds((2 * q + h) * r, r)] if rows_of_one else dst.at[q, pl.ds(h * r, r)]
                    copies.append(pltpu.make_async_copy(src.at[q, h], part, sems.at[len(copies)]))
        for cp in copies:
            cp.start()
        for cp in copies:
            cp.wait()


def _rms_fwd(x, g):
    rstd = lax.rsqrt(jnp.mean(x * x, axis=-1, keepdims=True) + RMS_EPS)
    return x * rstd * g, rstd


def _rms_bwd(dn, x, rstd, g):
    a = dn * g
    xh = x * rstd
    dx = rstd * (a - xh * jnp.mean(a * xh, axis=-1, keepdims=True))
    dg = jnp.sum(dn * xh, axis=0, keepdims=True)
    return dx, dg


def _ln_stats(v):
    mu = jnp.mean(v, axis=-1, keepdims=True)
    xc = v - mu
    rs = lax.rsqrt(jnp.mean(xc * xc, axis=-1, keepdims=True) + LN_EPS)
    return xc * rs, rs


def _ln_bwd(dy, xhat, rs, g):
    dxh = dy * g
    dv = rs * (dxh - jnp.mean(dxh, axis=-1, keepdims=True) - xhat * jnp.mean(dxh * xhat, axis=-1, keepdims=True))
    return dv, jnp.sum(dy * xhat, axis=0, keepdims=True), jnp.sum(dy, axis=0, keepdims=True)


def _gelu_cdf(s):
    return 0.5 * (1.0 + lax.erf(s * 0.7071067811865476))


def _mesh_pos():
    return lax.axis_index("x"), lax.axis_index("y"), lax.axis_index("c")


def _other_chips(x, y):
    return [(1 - x, y), (x, 1 - y), (1 - x, 1 - y)]


def _remote(src, dst, send_sem, recv_sem, to):
    return pltpu.make_async_remote_copy(src_ref=src, dst_ref=dst, send_sem=send_sem, recv_sem=recv_sem,
                                        device_id=to, device_id_type=MESH)


def _like(arrays):
    return [jax.ShapeDtypeStruct(a.shape, a.dtype) for a in arrays]


class _PairSwap:
    def __init__(self, grads):
        self.ins = list(grads)
        self.out_shapes = [jax.ShapeDtypeStruct((g.shape[0],) + g.shape[2:], g.dtype) for g in grads]
        self.aliases = {}
        self.n_sems = len(grads)

    def _copies(self, ins, outs, send, recv):
        x, y, c = _mesh_pos()
        return [_remote(ins[t].at[:, 1 - c], outs[t], send.at[t], recv.at[t], (x, y, 1 - c)) for t in range(len(ins))]

    def start(self, ins, outs, send, recv):
        for cp in self._copies(ins, outs, send, recv):
            cp.start()

    def finish(self, ins, outs, send, recv):
        for cp in self._copies(ins, outs, send, recv):
            cp.wait()


class _ChipSwap:
    def __init__(self, parts):
        self.ins = list(parts)
        self.out_shapes = [jax.ShapeDtypeStruct((3,) + p.shape[1:], p.dtype) for p in parts]
        self.aliases = {}
        self.n_sems = 3 * len(parts)

    def _copies(self, ins, outs, send, recv):
        x, y, c = _mesh_pos()
        return [_remote(ins[t].at[2 * chip[0] + chip[1]], outs[t].at[k], send.at[3 * t + k], recv.at[3 * t + k], (*chip, c))
                for t in range(len(ins)) for k, chip in enumerate(_other_chips(x, y))]

    def start(self, ins, outs, send, recv):
        for cp in self._copies(ins, outs, send, recv):
            cp.start()

    def finish(self, ins, outs, send, recv):
        for cp in self._copies(ins, outs, send, recv):
            cp.wait()


class _PairShare:
    def __init__(self, fulls):
        self.ins = list(fulls)
        self.out_shapes = _like(fulls)
        self.aliases = {t: t for t in range(len(fulls))}
        self.n_sems = len(fulls)

    def _copies(self, ins, outs, send, recv):
        x, y, c = _mesh_pos()
        return [_remote(ins[t].at[c], outs[t].at[c], send.at[t], recv.at[t], (x, y, 1 - c)) for t in range(len(ins))]

    def start(self, ins, outs, send, recv):
        for cp in self._copies(ins, outs, send, recv):
            cp.start()

    def finish(self, ins, outs, send, recv):
        for cp in self._copies(ins, outs, send, recv):
            cp.wait()


class _ShareAll:
    def __init__(self, arrays):
        self.ins = list(arrays)
        self.out_shapes = [jax.ShapeDtypeStruct((N_DEV,) + a.shape, a.dtype) for a in arrays]
        self.aliases = {}
        self.n_sems = (N_DEV - 1) * len(arrays)

    def _peers(self):
        x, y, c = _mesh_pos()
        flips = [((r >> 2) & 1, (r >> 1) & 1, r & 1) for r in range(1, N_DEV)]
        return (x, y, c), [(x ^ fx, y ^ fy, c ^ fc) for fx, fy, fc in flips]

    def _sends(self, ins, outs, send, recv):
        (x, y, c), peers = self._peers()
        mine = 4 * x + 2 * y + c
        return [_remote(ins[a], outs[a].at[mine], send.at[7 * a + r], recv.at[7 * a + r], peer)
                for a in range(len(ins)) for r, peer in enumerate(peers)]

    def start(self, ins, outs, send, recv):
        for cp in self._sends(ins, outs, send, recv):
            cp.start()

    def finish(self, ins, outs, send, recv):
        (x, y, c), peers = self._peers()
        for a in range(len(ins)):
            for r, (px, py, pc) in enumerate(peers):
                blk = outs[a].at[4 * px + 2 * py + pc]
                _remote(blk, blk, send.at[7 * a + r], recv.at[7 * a + r], (x, y, c)).wait_recv()
        for cp in self._sends(ins, outs, send, recv):
            cp.wait_send()


def _gather_beside(bufs, name, collective_id):
    n = len(bufs)
    per = 7
    refs = [jax.new_ref(b, memory_space=pltpu.MemorySpace.HBM) for b in bufs]

    @pl.kernel(mesh=plsc.ScalarSubcoreMesh(axis_name="sequencer", num_cores=1), name=name,
               scratch_types=(pltpu.SemaphoreType.DMA((per * n,)), pltpu.SemaphoreType.DMA((per * n,))),
               compiler_params=pltpu.CompilerParams(collective_id=collective_id))
    def launch(send, recv):
        x, y, c = _mesh_pos()
        me, sibling = (x, y, c), (x, y, 1 - c)
        x_nbr, y_nbr = (1 - x, y, c), (x, 1 - y, c)
        mine, via_x, via_y, diag = 2 * x + y, 2 * (1 - x) + y, 2 * x + (1 - y), 2 * (1 - x) + (1 - y)
        barrier = pltpu.get_barrier_semaphore()
        peers = [x_nbr, y_nbr, sibling]
        for peer in peers:
            pl.semaphore_signal(barrier, inc=1, device_id=peer, device_id_type=MESH)
        pl.semaphore_wait(barrier, len(peers))

        def copy(t, k, src, dst, to):
            return _remote(src, dst, send.at[per * t + k], recv.at[per * t + k], to)

        def piece(t, chip, half, rows=None):
            blk = refs[t].at[chip, half]
            return blk if rows is None else blk.at[rows]

        started = []

        def go(cp):
            cp.start()
            started.append(cp)

        upper = [pl.ds(0, r.shape[2] // 2) for r in refs]
        lower = [pl.ds(r.shape[2] // 2, r.shape[2] // 2) for r in refs]
        for t in range(n):
            go(copy(t, 0, piece(t, mine, c), piece(t, mine, c), x_nbr))
            go(copy(t, 1, piece(t, mine, c), piece(t, mine, c), y_nbr))
        for t in range(n):
            copy(t, 0, piece(t, via_x, c), piece(t, via_x, c), me).wait_recv()
            go(copy(t, 2, piece(t, via_x, c, upper[t]), piece(t, via_x, c, upper[t]), y_nbr))
            go(copy(t, 4, piece(t, via_x, c), piece(t, via_x, c), sibling))
            copy(t, 1, piece(t, via_y, c), piece(t, via_y, c), me).wait_recv()
            go(copy(t, 3, piece(t, via_y, c, lower[t]), piece(t, via_y, c, lower[t]), x_nbr))
            go(copy(t, 5, piece(t, via_y, c), piece(t, via_y, c), sibling))
        for t in range(n):
            copy(t, 2, piece(t, diag, c, upper[t]), piece(t, diag, c, upper[t]), me).wait_recv()
            copy(t, 3, piece(t, diag, c, lower[t]), piece(t, diag, c, lower[t]), me).wait_recv()
            go(copy(t, 6, piece(t, diag, c), piece(t, diag, c), sibling))
        for t in range(n):
            for k, chip in ((4, via_x), (5, via_y), (6, diag)):
                copy(t, k, piece(t, chip, 1 - c), piece(t, chip, 1 - c), me).wait_recv()
        for cp in started:
            cp.wait_send()

    launch()
    return [r[...] for r in refs]


def _chip_swap_beside(parts, name, collective_id):
    src = jax.new_ref(parts, memory_space=pltpu.MemorySpace.HBM)
    dst = jax.empty_ref(jax.ShapeDtypeStruct((N_CHIPS - 1,) + parts.shape[1:], parts.dtype),
                        memory_space=pltpu.MemorySpace.HBM)
    swap = _ChipSwap([parts])

    @pl.kernel(mesh=plsc.ScalarSubcoreMesh(axis_name="sequencer", num_cores=1), name=name,
               scratch_types=(pltpu.SemaphoreType.DMA((N_CHIPS - 1,)), pltpu.SemaphoreType.DMA((N_CHIPS - 1,))),
               compiler_params=pltpu.CompilerParams(collective_id=collective_id))
    def launch(send, recv):
        x, y, c = _mesh_pos()
        barrier = pltpu.get_barrier_semaphore()
        peers = [(*chip, c) for chip in _other_chips(x, y)]
        for peer in peers:
            pl.semaphore_signal(barrier, inc=1, device_id=peer, device_id_type=MESH)
        pl.semaphore_wait(barrier, len(peers))
        swap.start([src], [dst], send, recv)
        swap.finish([src], [dst], send, recv)

    launch()
    return dst[...]


def _pallas(body, operands, *, name, grid, in_specs, out_specs, out_shape, scratch_shapes=(), vmem_mib=32, riders=(),
            prefetch=None):
    in_specs, out_specs, out_shape, scratch_shapes = list(in_specs), list(out_specs), list(out_shape), list(scratch_shapes)
    if not riders and prefetch is None:
        outs = pl.pallas_call(body, name=name, grid=grid, in_specs=in_specs, out_specs=out_specs, out_shape=out_shape,
                              scratch_shapes=scratch_shapes, compiler_params=_params(vmem_mib, len(grid)))(*operands)
        return list(outs), []
    n_in, n_out, n_scr = len(in_specs), len(out_specs), len(scratch_shapes)
    r_in = [len(r.ins) for r in riders]
    r_out = [len(r.out_shapes) for r in riders]
    steps = 1
    for g in grid:
        steps *= g

    n_pre = 0 if prefetch is None else 1

    def wrapped(*refs):
        refs = list(refs)
        pre, refs = refs[:n_pre], refs[n_pre:]
        ins, refs = refs[:n_in], refs[n_in:]
        rins = []
        for k in r_in:
            rins.append(refs[:k])
            refs = refs[k:]
        outs, refs = refs[:n_out], refs[n_out:]
        routs = []
        for k in r_out:
            routs.append(refs[:k])
            refs = refs[k:]
        scr, sems = refs[:n_scr], refs[n_scr:]
        step = 0
        for ax, g in enumerate(grid):
            step = step * g + pl.program_id(ax)

        def each(what):
            for j, r in enumerate(riders):
                getattr(r, what)(rins[j], routs[j], sems[2 * j], sems[2 * j + 1])

        if grid:
            pl.when(step == 0)(lambda: each("start"))
        else:
            each("start")
        body(*pre, *ins, *outs, *scr)
        if grid:
            pl.when(step == steps - 1)(lambda: each("finish"))
        else:
            each("finish")

    aliases, off_in, off_out = {}, n_pre + n_in, n_out
    for r, ki, ko in zip(riders, r_in, r_out):
        for i, o in r.aliases.items():
            aliases[off_in + i] = off_out + o
        off_in, off_out = off_in + ki, off_out + ko
    sems = []
    for r in riders:
        sems += [pltpu.SemaphoreType.DMA((r.n_sems,)), pltpu.SemaphoreType.DMA((r.n_sems,))]
    layout = dict(grid=grid, in_specs=in_specs + [ANY] * sum(r_in), out_specs=out_specs + [ANY] * sum(r_out),
                  scratch_shapes=scratch_shapes + sems)
    if prefetch is not None:
        layout = dict(grid_spec=pltpu.PrefetchScalarGridSpec(num_scalar_prefetch=1, **layout))
    res = pl.pallas_call(
        wrapped, name=name, **layout,
        out_shape=out_shape + [s for r in riders for s in r.out_shapes], input_output_aliases=aliases,
        compiler_params=pltpu.CompilerParams(dimension_semantics=("arbitrary",) * len(grid),
                                             vmem_limit_bytes=vmem_mib * MIB, has_side_effects=True),
    )(*([] if prefetch is None else [prefetch]), *operands, *[a for r in riders for a in r.ins])
    res = list(res)
    outs, res = res[:n_out], res[n_out:]
    routs = []
    for k in r_out:
        routs.append(res[:k])
        res = res[k:]
    return outs, routs


def _exchange(riders, name):
    return _pallas(lambda: None, [], name=name, grid=(), in_specs=[], out_specs=[], out_shape=[], riders=riders)[1]


def _in_hbm(a):
    return pltpu.with_memory_space_constraint(a, pltpu.HBM)


def _place_shard(w, layer, dtype, name):
    _, rows, cols = w.shape
    half = rows // 2
    br = _block_rows(half)
    nb = half // br
    mine = 2 * lax.axis_index("x") + lax.axis_index("y")

    def body(q_ref, w_ref, o_ref):
        o_ref[...] = w_ref[...].astype(dtype)

    return pl.pallas_call(
        body, name=name,
        grid_spec=pltpu.PrefetchScalarGridSpec(
            num_scalar_prefetch=1, grid=(2, nb),
            in_specs=[pl.BlockSpec((None, br, cols), lambda h, i, q: (layer, h * nb + i, 0))],
            out_specs=pl.BlockSpec((None, None, br, cols), lambda h, i, q: (q[0], h, i, 0))),
        out_shape=pltpu.HBM((N_CHIPS, 2, half, cols), dtype),
        compiler_params=_params(16, 2),
    )(jnp.reshape(mine, (1,)).astype(jnp.int32), _in_hbm(w))


def _plain_copy(a, name):
    def body(a_ref, o_ref):
        o_ref[...] = a_ref[...]

    vmem = pl.BlockSpec(memory_space=pltpu.VMEM)
    return pl.pallas_call(body, name=name, in_specs=[vmem], out_specs=vmem,
                          out_shape=jax.ShapeDtypeStruct(a.shape, a.dtype))(a)


def _add_pair(g, recv, name):
    _, _, r, cdim = g.shape
    br = _block_rows(r, 256)
    c = lax.axis_index("c")

    def body(c_ref, g_ref, r_ref, o_ref):
        o_ref[...] = (g_ref[...] + r_ref[...]).astype(BF16)

    return pl.pallas_call(
        body, name=name,
        grid_spec=pltpu.PrefetchScalarGridSpec(
            num_scalar_prefetch=1, grid=(N_CHIPS, r // br),
            in_specs=[pl.BlockSpec((None, None, br, cdim), lambda q, i, c_ref: (q, c_ref[0], i, 0)),
                      pl.BlockSpec((None, br, cdim), lambda q, i, c_ref: (q, i, 0))],
            out_specs=pl.BlockSpec((None, br, cdim), lambda q, i, c_ref: (q, i, 0))),
        out_shape=pltpu.HBM((N_CHIPS, r, cdim), BF16),
        compiler_params=_params(16, 2),
    )(jnp.reshape(c, (1,)).astype(jnp.int32), _in_hbm(g), _in_hbm(recv))


def _add_chips(own, recv, name):
    _, r, cdim = own.shape
    br = _block_rows(r, 256)
    x, y, c = _mesh_pos()

    def body(pos_ref, own_ref, r_ref, o_ref):
        acc = own_ref[...].astype(F32)
        for k in range(3):
            acc = acc + r_ref[k].astype(F32)
        o_ref[...] = acc

    return pl.pallas_call(
        body, name=name,
        grid_spec=pltpu.PrefetchScalarGridSpec(
            num_scalar_prefetch=1, grid=(r // br,),
            in_specs=[pl.BlockSpec((None, br, cdim), lambda i, pos: (pos[0], i, 0)),
                      pl.BlockSpec((3, br, cdim), lambda i, pos: (0, i, 0))],
            out_specs=pl.BlockSpec((None, br, cdim), lambda i, pos: (pos[1], i, 0))),
        out_shape=pltpu.HBM((2, r, cdim), F32),
        compiler_params=_params(16, 1),
    )(jnp.stack([2 * x + y, c]).astype(jnp.int32), _in_hbm(own), _in_hbm(recv))


def _adam_math(w, m, v, g):
    c1 = 1.0 / (1.0 - ADAM_B1 ** ADAM_STEP)
    c2 = 1.0 / (1.0 - ADAM_B2 ** ADAM_STEP)
    m_new = ADAM_B1 * m + (1.0 - ADAM_B1) * g
    v_new = ADAM_B2 * v + (1.0 - ADAM_B2) * (g * g)
    return -ADAM_LR * ((m_new * c1) / (jnp.sqrt(v_new * c2) + ADAM_EPS) + ADAM_WD * w), m_new, v_new


SMALL_WEIGHTS = [
    ("ev_norm_g", (1, D_MODEL), ["ev_norm_g"], None), ("ev_conv_a_b", (1, A_DIM), ["ev_conv_a_b"], None),
    ("ev_ln_a_g", (1, A_DIM), ["ev_ln_a_g"], None), ("ev_ln_a_b", (1, A_DIM), ["ev_ln_a_b"], None),
    ("od_w_s", (C_GROUPS, CHUNK, CHUNK), ["od_w_s_lo", "od_w_s_hi"], None), ("od_b_s", (C_GROUPS, CHUNK), ["od_b_s"], None),
    ("mlp_norm_g", (2, D_MODEL), ["mlp_norm_g0", "mlp_norm_g1"], None), ("final_norm_g", (1, D_MODEL), ["final_norm_g"], None),
    ("ev_conv_a_w", (A_CONV_WIDTH, A_DIM // N_CHIPS), ["ev_conv_a_w"], A_DIM // N_CHIPS),
    ("ev_conv_b_w", (B_CONV_WIDTH, B_DIM // N_CHIPS), ["ev_conv_b_w"], B_DIM // N_CHIPS),
    ("od_norm_g", (1, D_MODEL // N_CHIPS), ["od_norm_g"], D_MODEL // N_CHIPS),
    ("od_b_in", (1, 2 * C_DIM // N_CHIPS), ["od_b_in"], 2 * C_DIM // N_CHIPS),
    ("od_ln_v_g", (1, C_DIM // N_CHIPS), ["od_ln_v_g"], C_DIM // N_CHIPS),
    ("od_ln_v_b", (1, C_DIM // N_CHIPS), ["od_ln_v_b"], C_DIM // N_CHIPS),
]


def _small_update(own, landed, weights):
    names = list(own.keys())
    n_g, n_w = len(names), len(SMALL_WEIGHTS)

    def body(*refs):
        refs = list(refs)
        own_refs = dict(zip(names, refs[:n_g]))
        land_refs = dict(zip(names, refs[n_g:2 * n_g]))
        wmv = [refs[2 * n_g + 3 * i:2 * n_g + 3 * i + 3] for i in range(n_w)]
        o0 = 2 * n_g + 3 * n_w
        loss_ref = refs[o0]
        outs = [refs[o0 + 1 + 4 * i:o0 + 5 + 4 * i] for i in range(n_w)]
        acc = dict(zip(names, refs[o0 + 1 + 4 * n_w:]))
        x, y, c = _mesh_pos()
        mine, chip = 4 * x + 2 * y + c, 2 * x + y

        for nm in names:
            for d in range(N_DEV):
                def add(term, nm=nm, d=d):
                    acc[nm][...] = term if d == 0 else acc[nm][...] + term
                pl.when(mine == d)(lambda nm=nm, add=add: add(own_refs[nm][...]))
                pl.when(mine != d)(lambda nm=nm, d=d, add=add: add(land_refs[nm][d]))
        loss_ref[...] = acc["loss"][...]

        def update(i, rows, g):
            w_ref, m_ref, v_ref = wmv[i]
            delta, m_new, v_new = _adam_math(w_ref[rows], m_ref[rows], v_ref[rows], g)
            for ref, val in zip(outs[i], (g, delta, m_new, v_new)):
                ref[rows] = val

        for i, (_, shape, grads, per_chip) in enumerate(SMALL_WEIGHTS):
            for row, gname in enumerate(grads):
                per_grad = shape[0] // len(grads)
                rows = slice(row * per_grad, (row + 1) * per_grad)
                if per_chip is None:
                    update(i, rows, acc[gname][...])
                else:
                    for q in range(N_CHIPS):
                        pl.when(chip == q)(lambda i=i, rows=rows, gname=gname, q=q, per_chip=per_chip:
                                           update(i, rows, acc[gname][:, q * per_chip:(q + 1) * per_chip]))

    operands = [own[nm] for nm in names] + [landed[nm] for nm in names]
    for nm, _, _, _ in SMALL_WEIGHTS:
        operands += list(weights[nm])
    out_shape = [jax.ShapeDtypeStruct((1, 1), F32)]
    for _, shape, _, _ in SMALL_WEIGHTS:
        out_shape += [jax.ShapeDtypeStruct(shape, F32)] * 4
    res = pl.pallas_call(
        body, name="small_update", grid=(1,),
        in_specs=[_full_spec(a.shape) for a in operands], out_specs=[_full_spec(s.shape) for s in out_shape],
        out_shape=out_shape, scratch_shapes=[pltpu.VMEM(own[nm].shape, F32) for nm in names],
        compiler_params=_params(32, 1),
    )(*[_in_hbm(a) for a in operands])
    return res[0], {nm: res[1 + 4 * i:5 + 4 * i] for i, (nm, _, _, _) in enumerate(SMALL_WEIGHTS)}


def _adamw(w, m, v, grads, name):
    layers, r, cdim = w.shape
    br = _block_rows(r, 256 if cdim > LANES else 1024)
    blocks = r // br

    def body(*refs):
        w_ref, m_ref, v_ref = refs[:3]
        g_refs = refs[3:3 + layers]
        go_ref, d_ref, mo_ref, vo_ref = refs[3 + layers:]
        layer = pl.program_id(0)
        for l in range(layers):
            @pl.when(layer == l)
            def _(l=l):
                g = g_refs[l][...]
                go_ref[...] = g
                d_ref[...], mo_ref[...], vo_ref[...] = _adam_math(w_ref[...], m_ref[...], v_ref[...], g)

    spec3 = pl.BlockSpec((None, br, cdim), lambda l, i: (l, i, 0))
    g_specs = [pl.BlockSpec((br, cdim), lambda l, i, own=own: (jnp.clip(i + (l - own) * blocks, 0, blocks - 1), 0))
               for own in range(layers)]
    out = jax.ShapeDtypeStruct((layers, r, cdim), F32)
    outs, _ = _pallas(body, [_in_hbm(a) for a in (w, m, v, *grads)], name=name, grid=(layers, blocks),
                      in_specs=[spec3, spec3, spec3] + g_specs, out_specs=[spec3] * 4, out_shape=[out] * 4, vmem_mib=32)
    return outs


ADAMW_ROWS = 256
ADAMW_SLOTS = 3


def _adamw_stream(weights, name):
    cdim = weights[0][0].shape[2]
    flat = lambda a: jnp.reshape(a, (-1, cdim))
    operands, work = [], []
    for k, (w, m, v, grads) in enumerate(weights):
        layers, r, _ = w.shape
        assert r % ADAMW_ROWS == 0 and w.shape[2] == cdim, (name, w.shape)
        base = len(operands)
        operands += [flat(w), flat(m), flat(v), *grads]
        for l in range(layers):
            for b in range(0, r, ADAMW_ROWS):
                work.append((base, base + 3 + l, 4 * k, l * r + b, b))
    n_in = len(operands)

    def body(*refs):
        ins, outs = refs[:n_in], refs[n_in:n_in + 4 * len(weights)]
        buf_in, buf_out, sem_in, sem_out = refs[n_in + 4 * len(weights):]

        def reads(t):
            base, g_at, _, rows, g_rows = work[t]
            slot = t % ADAMW_SLOTS
            srcs = [ins[base + j].at[pl.ds(rows, ADAMW_ROWS)] for j in range(3)] + [ins[g_at].at[pl.ds(g_rows, ADAMW_ROWS)]]
            return [pltpu.make_async_copy(src, buf_in.at[slot, j], sem_in.at[slot, j]) for j, src in enumerate(srcs)]

        def writes(t):
            _, _, out_at, rows, _ = work[t]
            slot = t % ADAMW_SLOTS
            return [pltpu.make_async_copy(buf_out.at[slot, j], outs[out_at + j].at[pl.ds(rows, ADAMW_ROWS)],
                                          sem_out.at[slot, j]) for j in range(4)]

        for t in range(min(ADAMW_SLOTS - 1, len(work))):
            for cp in reads(t):
                cp.start()
        for t in range(len(work)):
            slot = t % ADAMW_SLOTS
            if t + ADAMW_SLOTS - 1 < len(work):
                for cp in reads(t + ADAMW_SLOTS - 1):
                    cp.start()
            for cp in reads(t):
                cp.wait()
            if t >= ADAMW_SLOTS:
                for cp in writes(t - ADAMW_SLOTS):
                    cp.wait()
            g = buf_in[slot, 3]
            buf_out[slot, 0] = g
            buf_out[slot, 1], buf_out[slot, 2], buf_out[slot, 3] = _adam_math(buf_in[slot, 0], buf_in[slot, 1],
                                                                             buf_in[slot, 2], g)
            for cp in writes(t):
                cp.start()
        for t in range(max(0, len(work) - ADAMW_SLOTS), len(work)):
            for cp in writes(t):
                cp.wait()

    out_shape = [jax.ShapeDtypeStruct((w.shape[0] * w.shape[1], cdim), F32) for w, _, _, _ in weights for _ in range(4)]
    outs, _ = _pallas(body, [_in_hbm(a) for a in operands], name=name, grid=(1,), in_specs=[ANY] * n_in,
                      out_specs=[ANY] * len(out_shape), out_shape=out_shape,
                      scratch_shapes=[pltpu.VMEM((ADAMW_SLOTS, 4, ADAMW_ROWS, cdim), F32),
                                      pltpu.VMEM((ADAMW_SLOTS, 4, ADAMW_ROWS, cdim), F32),
                                      pltpu.SemaphoreType.DMA((ADAMW_SLOTS, 4)), pltpu.SemaphoreType.DMA((ADAMW_SLOTS, 4))],
                      vmem_mib=40)
    return [[jnp.reshape(o, w.shape) for o in outs[4 * k:4 * k + 4]] for k, (w, _, _, _) in enumerate(weights)]


def _fill_shifted(buf, rows):
    for b in range(1, SUBLANES):
        buf[b, 0:rows - SUBLANES, :] = buf[0, b:b + rows - SUBLANES, :]


def _window(buf, start, size):
    return buf[start % SUBLANES, start - start % SUBLANES:start - start % SUBLANES + size, :]


def _conv31(src, w_ref, r0, base, init):
    acc = init
    for k in range(A_CONV_WIDTH):
        acc = acc + w_ref[k:k + 1, :] * _window(src, base + k + r0, CONV_ROWS)
    return acc


def _fwd_even(x, norm_g, w_in, conv_a_w, conv_a_b, ln_g, ln_b, conv_b_w, w_out, *, tm, seq, riders=()):
    tokens = x.shape[0]
    nt, tps = tokens // tm, seq // tm

    def body(x_ref, g_ref, win_hbm, caw_ref, cab_ref, lng_ref, lnb_ref, cbw_ref, wout_hbm,
             h_ref, n_ref, z_ref, a2_ref, cv_ref, mix_ref, win_v, wout_v, pa, pb, sem):
        i = pl.program_id(0)

        _load_weights([(win_hbm, win_v, False), (wout_hbm, wout_v, True)], sem)

        xv = x_ref[...]
        nf, _ = _rms_fwd(xv, g_ref[...])
        n = nf.astype(BF16)
        n_ref[...] = n
        z = jnp.concatenate([_dot(n, win_v[j]) for j in range(N_CHIPS)], axis=1)
        z_ref[...] = z.astype(BF16)
        a_val, a_gate = z[:, 0:A_DIM], z[:, A_DIM:2 * A_DIM]
        b_gate, c_gate, b_val = z[:, 1024:1536], z[:, 1536:2048], z[:, 2048:2560]

        first = (i % tps) == 0

        @pl.when(first)
        def _():
            pa[0, 0:A_HALO, :] = jnp.zeros((A_HALO, A_DIM), F32)
            pb[0:B_HALO, :] = jnp.zeros((B_HALO, B_DIM), F32)

        @pl.when(jnp.logical_not(first))
        def _():
            pa[0, 0:A_HALO, :] = pa[0, tm:tm + A_HALO, :]
            pb[0:B_HALO, :] = pb[tm:tm + B_HALO, :]

        pa[0, A_HALO:A_HALO + tm, :] = a_val * jax.nn.sigmoid(a_gate)
        pb[B_HALO:B_HALO + tm, :] = c_gate * b_val
        _fill_shifted(pa, A_HALO + tm)
        bias = jnp.broadcast_to(cab_ref[...], (CONV_ROWS, A_DIM))
        for r0 in range(0, tm, CONV_ROWS):
            a2_ref[r0:r0 + CONV_ROWS, :] = _conv31(pa, caw_ref, r0, A_HALO - (A_CONV_WIDTH - 1), bias)
        xhat, _ = _ln_stats(a2_ref[...])
        a3 = xhat * lng_ref[...] + lnb_ref[...]
        a4 = a3 * jax.nn.sigmoid(a3)
        cv = cbw_ref[0:1, :] * pb[B_HALO - 2:B_HALO - 2 + tm, :]
        cv = cv + cbw_ref[1:2, :] * pb[B_HALO - 1:B_HALO - 1 + tm, :]
        cv = cv + cbw_ref[2:3, :] * pb[B_HALO:B_HALO + tm, :]
        cv_ref[...] = cv.astype(BF16)
        mix = jnp.concatenate([a4, b_gate * cv], axis=1).astype(BF16)
        mix_ref[...] = mix
        h_ref[...] = xv + _dot(mix, wout_v[...])

    shp = lambda cols, dt: jax.ShapeDtypeStruct((tokens, cols), dt)
    return _pallas(
        body, [x, norm_g, w_in, conv_a_w, conv_a_b, ln_g, ln_b, conv_b_w, w_out], name="fwd_even", grid=(nt,),
        in_specs=[_row_spec(tm, D_MODEL), _full_spec((1, D_MODEL)), ANY, _full_spec((A_CONV_WIDTH, A_DIM)),
                  _full_spec((1, A_DIM)), _full_spec((1, A_DIM)), _full_spec((1, A_DIM)),
                  _full_spec((B_CONV_WIDTH, B_DIM)), ANY],
        out_specs=[_row_spec(tm, D_MODEL), _row_spec(tm, D_MODEL), _row_spec(tm, IN_EVEN), _row_spec(tm, A_DIM),
                   _row_spec(tm, B_DIM), _row_spec(tm, D_MODEL)],
        out_shape=[shp(D_MODEL, F32), shp(D_MODEL, BF16), shp(IN_EVEN, BF16), shp(A_DIM, F32), shp(B_DIM, BF16),
                   shp(D_MODEL, BF16)],
        scratch_shapes=[pltpu.VMEM((N_CHIPS, D_MODEL, IN_EVEN // N_CHIPS), BF16), pltpu.VMEM((D_MODEL, D_MODEL), BF16),
                        pltpu.VMEM((SUBLANES, A_HALO + tm, A_DIM), F32), pltpu.VMEM((B_HALO + tm, B_DIM), F32),
                        pltpu.SemaphoreType.DMA((N_LOADS,))],
        vmem_mib=56, riders=riders)


def _loss_tail(xv, g, target, loss_ref, dh_ref, dhb_ref, dg_ref):
    @pl.when(pl.program_id(0) == 0)
    def _():
        loss_ref[...] = jnp.zeros((1, 1), F32)
        dg_ref[...] = jnp.zeros((1, D_MODEL), F32)

    out, rstd = _rms_fwd(xv, g)
    err = out - target
    per_token = jnp.sum(err * err, axis=1, keepdims=True) * (1.0 / D_MODEL)
    loss_ref[...] += 0.5 * jnp.sum(per_token, axis=0, keepdims=True)
    dx, dg = _rms_bwd(err * (1.0 / D_MODEL), xv, rstd, g)
    dh_ref[...] = dx
    dhb_ref[...] = dx.astype(BF16)
    dg_ref[...] += dg


def _fwd_mlp(h, norm_g, w1, w2, layer, *, tm, riders=(), head=None):
    tokens = h.shape[0]
    nt = tokens // tm
    fs = D_FF // N_CHIPS
    n_in = 4 if head is None else 6

    def body(*refs):
        h_ref, g_ref, w1_hbm, w2_hbm = refs[:4]
        w1_v, w2_v, sem = refs[-3:]
        outs = refs[n_in:-3]
        n_ref, p_ref, q_ref = outs[1:4] if head is None else outs[0:3]
        _load_weights([(w1_hbm, w1_v, False), (w2_hbm, w2_v, False)], sem)

        xv = h_ref[...]
        nf, _ = _rms_fwd(xv, g_ref[...])
        n = nf.astype(BF16)
        n_ref[...] = n
        acc = xv
        for j in range(N_CHIPS):
            p = _dot(n, w1_v[j])
            p_ref[:, j * fs:(j + 1) * fs] = p.astype(BF16)
            r = jnp.maximum(p, 0.0)
            q = (r * r).astype(BF16)
            q_ref[:, j * fs:(j + 1) * fs] = q
            acc = acc + _dot(q, w2_v[j])
        if head is None:
            outs[0][...] = acc
        else:
            _loss_tail(acc, refs[4][...], refs[5][...], *outs[3:7])

    shp = lambda cols, dt: jax.ShapeDtypeStruct((tokens, cols), dt)
    saved_specs = [_row_spec(tm, D_MODEL), _row_spec(tm, D_FF), _row_spec(tm, D_FF)]
    saved_shapes = [shp(D_MODEL, BF16), shp(D_FF, BF16), shp(D_FF, BF16)]
    if head is None:
        operands, in_specs = [h, norm_g, w1, w2], [_row_spec(tm, D_MODEL), _full_spec((1, D_MODEL)), ANY, ANY]
        out_specs, out_shape = [_row_spec(tm, D_MODEL)] + saved_specs, [shp(D_MODEL, F32)] + saved_shapes
    else:
        operands = [h, norm_g, w1, w2, *head]
        in_specs = [_row_spec(tm, D_MODEL), _full_spec((1, D_MODEL)), ANY, ANY, _full_spec((1, D_MODEL)), _row_spec(tm, D_MODEL)]
        out_specs = saved_specs + [_full_spec((1, 1)), _row_spec(tm, D_MODEL), _row_spec(tm, D_MODEL), _full_spec((1, D_MODEL))]
        out_shape = saved_shapes + [jax.ShapeDtypeStruct((1, 1), F32), shp(D_MODEL, F32), shp(D_MODEL, BF16),
                                    jax.ShapeDtypeStruct((1, D_MODEL), F32)]
    return _pallas(
        body, operands, name=f"fwd_mlp{layer}", grid=(nt,), in_specs=in_specs, out_specs=out_specs, out_shape=out_shape,
        scratch_shapes=[pltpu.VMEM((N_CHIPS, D_MODEL, fs), BF16), pltpu.VMEM((N_CHIPS, fs, D_MODEL), BF16),
                        pltpu.SemaphoreType.DMA((N_LOADS,))],
        vmem_mib=56, riders=riders)


def _tril_mask():
    row = lax.broadcasted_iota(jnp.int32, (CHUNK, CHUNK), 0)
    col = lax.broadcasted_iota(jnp.int32, (CHUNK, CHUNK), 1)
    return row >= col


def _triu_mask():
    row = lax.broadcasted_iota(jnp.int32, (CHUNK, CHUNK), 0)
    col = lax.broadcasted_iota(jnp.int32, (CHUNK, CHUNK), 1)
    return row <= col


def _fwd_odd(h, norm_g, w_in, b_in, ln_g, ln_b, w_s, b_s_rows, w_out, *, tm, riders=()):
    tokens = h.shape[0]
    nt = tokens // tm
    cs = 2 * C_DIM // N_CHIPS

    def body(h_ref, g_ref, win_hbm, bin_ref, lng_ref, lnb_ref, ws_ref, bs_ref, wout_hbm,
             ho_ref, n_ref, s_ref, cdf_ref, sv_ref, y_ref, win_v, wout_v, bd, sem):
        _load_weights([(win_hbm, win_v, False), (wout_hbm, wout_v, True)], sem)

        @pl.when(pl.program_id(0) == 0)
        def _():
            mask = _tril_mask()
            bd[...] = jnp.zeros(bd.shape, BF16)
            for g in range(C_GROUPS):
                w = jnp.where(mask, ws_ref[g], 0.0).astype(BF16)
                bd[g, 0:CHUNK, 0:CHUNK] = w
                bd[g, CHUNK:PAIR, CHUNK:PAIR] = w

        xv = h_ref[...]
        nf, _ = _rms_fwd(xv, g_ref[...])
        n = nf.astype(BF16)
        n_ref[...] = n
        s = jnp.concatenate([_dot(n, win_v[j]) for j in range(N_CHIPS)], axis=1) + bin_ref[...]
        s_ref[...] = s.astype(BF16)
        cdf = _gelu_cdf(s)
        cdf_ref[...] = cdf.astype(BF16)
        zz = s * cdf
        u, v = zz[:, 0:C_DIM], zz[:, C_DIM:2 * C_DIM]
        xhat, _ = _ln_stats(v)
        vn = (xhat * lng_ref[...] + lnb_ref[...]).astype(BF16)
        for g in range(C_GROUPS):
            cols = slice(g * CHUNK, (g + 1) * CHUNK)
            bias = jnp.concatenate([bs_ref[g], bs_ref[g]], axis=0)
            for r0 in range(0, tm, PAIR):
                sv = _dot(bd[g], vn[r0:r0 + PAIR, cols]) + bias
                sv_ref[r0:r0 + PAIR, cols] = sv.astype(BF16)
                y_ref[r0:r0 + PAIR, cols] = (u[r0:r0 + PAIR, cols] * sv).astype(BF16)
        ho_ref[...] = xv + _dot(y_ref[...], wout_v[...])

    shp = lambda cols, dt: jax.ShapeDtypeStruct((tokens, cols), dt)
    return _pallas(
        body, [h, norm_g, w_in, b_in, ln_g, ln_b, w_s, b_s_rows, w_out], name="fwd_odd", grid=(nt,),
        in_specs=[_row_spec(tm, D_MODEL), _full_spec((1, D_MODEL)), ANY, _full_spec((1, 2 * C_DIM)),
                  _full_spec((1, C_DIM)), _full_spec((1, C_DIM)), _full_spec((C_GROUPS, CHUNK, CHUNK)),
                  _full_spec((C_GROUPS, CHUNK, CHUNK)), ANY],
        out_specs=[_row_spec(tm, D_MODEL), _row_spec(tm, D_MODEL), _row_spec(tm, 2 * C_DIM), _row_spec(tm, 2 * C_DIM),
                   _row_spec(tm, C_DIM), _row_spec(tm, C_DIM)],
        out_shape=[shp(D_MODEL, F32), shp(D_MODEL, BF16), shp(2 * C_DIM, BF16), shp(2 * C_DIM, BF16), shp(C_DIM, BF16),
                   shp(C_DIM, BF16)],
        scratch_shapes=[pltpu.VMEM((N_CHIPS, D_MODEL, cs), BF16), pltpu.VMEM((C_DIM, D_MODEL), BF16),
                        pltpu.VMEM((C_GROUPS, PAIR, PAIR), BF16), pltpu.SemaphoreType.DMA((N_LOADS,))],
        vmem_mib=56, riders=riders)


def _bwd_mlp(dh, h, norm_g, p, w1, w2, layer, *, tm, riders=()):
    tokens = h.shape[0]
    nt = tokens // tm
    fs = D_FF // N_CHIPS

    def body(dh_ref, h_ref, g_ref, p_ref, w1_hbm, w2_hbm, dx_ref, dxb_ref, dp_ref, dg_ref, w1_v, w2_v, sem):
        @pl.when(pl.program_id(0) == 0)
        def _():
            dg_ref[...] = jnp.zeros((1, D_MODEL), F32)

        _load_weights([(w1_hbm, w1_v, False), (w2_hbm, w2_v, False)], sem)

        dhv = dh_ref[...]
        dhb = dhv.astype(BF16)
        dn = jnp.zeros((tm, D_MODEL), F32)
        for j in range(N_CHIPS):
            dq = _dot_nt(dhb, w2_v[j])
            r = jnp.maximum(p_ref[:, j * fs:(j + 1) * fs].astype(F32), 0.0)
            dp = ((2.0 * r) * dq).astype(BF16)
            dp_ref[:, j * fs:(j + 1) * fs] = dp
            dn = dn + _dot_nt(dp, w1_v[j])
        xv = h_ref[...]
        g = g_ref[...]
        _, rstd = _rms_fwd(xv, g)
        dx, dg = _rms_bwd(dn, xv, rstd, g)
        dx_ref[...] = dhv + dx
        dxb_ref[...] = (dhv + dx).astype(BF16)
        dg_ref[...] += dg

    return _pallas(
        body, [dh, h, norm_g, p, w1, w2], name=f"bwd_mlp{layer}", grid=(nt,),
        in_specs=[_row_spec(tm, D_MODEL), _row_spec(tm, D_MODEL), _full_spec((1, D_MODEL)), _row_spec(tm, D_FF), ANY, ANY],
        out_specs=[_row_spec(tm, D_MODEL), _row_spec(tm, D_MODEL), _row_spec(tm, D_FF), _full_spec((1, D_MODEL))],
        out_shape=[jax.ShapeDtypeStruct((tokens, D_MODEL), F32), jax.ShapeDtypeStruct((tokens, D_MODEL), BF16),
                   jax.ShapeDtypeStruct((tokens, D_FF), BF16), jax.ShapeDtypeStruct((1, D_MODEL), F32)],
        scratch_shapes=[pltpu.VMEM((N_CHIPS, D_MODEL, fs), BF16), pltpu.VMEM((N_CHIPS, fs, D_MODEL), BF16),
                        pltpu.SemaphoreType.DMA((N_LOADS,))],
        vmem_mib=56, riders=riders)


def _bwd_odd(dh, h, norm_g, s, cdf, sv, w_in, ln_g, ln_b, w_s, w_out, *, tm, riders=()):
    tokens = h.shape[0]
    nt = tokens // tm
    cs = 2 * C_DIM // N_CHIPS

    def body(dh_ref, h_ref, g_ref, s_ref, cdf_ref, sv_ref, win_hbm, lng_ref, lnb_ref, ws_ref, wout_hbm,
             dx_ref, dxb_ref, ds_ref, dg_ref, dbin_ref, dlng_ref, dlnb_ref, dws_ref, dbs_ref,
             win_v, wout_v, bdt, dws_acc, dbs_acc, dvn, sem):
        i = pl.program_id(0)

        _load_weights([(win_hbm, win_v, False), (wout_hbm, wout_v, True)], sem)

        @pl.when(i == 0)
        def _():
            mask_t = _triu_mask()
            bdt[...] = jnp.zeros(bdt.shape, BF16)
            for g in range(C_GROUPS):
                wt = jnp.where(mask_t, ws_ref[g].T, 0.0).astype(BF16)
                bdt[g, 0:CHUNK, 0:CHUNK] = wt
                bdt[g, CHUNK:PAIR, CHUNK:PAIR] = wt
            dws_acc[...] = jnp.zeros(dws_acc.shape, F32)
            dbs_acc[...] = jnp.zeros(dbs_acc.shape, F32)
            dg_ref[...] = jnp.zeros(dg_ref.shape, F32)
            dbin_ref[...] = jnp.zeros(dbin_ref.shape, F32)
            dlng_ref[...] = jnp.zeros(dlng_ref.shape, F32)
            dlnb_ref[...] = jnp.zeros(dlnb_ref.shape, F32)

        dhv = dh_ref[...]
        dy = _dot_nt(dhv.astype(BF16), wout_v[...])
        sf = s_ref[...].astype(F32)
        cdf = cdf_ref[...].astype(F32)
        pdf = jnp.exp(-0.5 * sf * sf) * 0.3989422804014327
        zz = sf * cdf
        dgelu = cdf + sf * pdf
        u, v = zz[:, 0:C_DIM], zz[:, C_DIM:2 * C_DIM]
        xhat, rs = _ln_stats(v)
        lng = lng_ref[...]
        vn = (xhat * lng + lnb_ref[...]).astype(BF16)
        du = dy * sv_ref[...].astype(F32)
        dsv = dy * u
        dsvb = dsv.astype(BF16)
        for g in range(C_GROUPS):
            cols = slice(g * CHUNK, (g + 1) * CHUNK)
            for r0 in range(0, tm, PAIR):
                blk = dsvb[r0:r0 + PAIR, cols]
                dvn[r0:r0 + PAIR, cols] = _dot(bdt[g], blk)
                dws_acc[g] += _dot_nt(blk, vn[r0:r0 + PAIR, cols])
                dbs_acc[g] += dsv[r0:r0 + CHUNK, cols] + dsv[r0 + CHUNK:r0 + PAIR, cols]
        dv, dlng, dlnb = _ln_bwd(dvn[...], xhat, rs, lng)
        dlng_ref[...] += dlng
        dlnb_ref[...] += dlnb
        ds = jnp.concatenate([du, dv], axis=1) * dgelu
        dbin_ref[...] += jnp.sum(ds, axis=0, keepdims=True)
        dsb = ds.astype(BF16)
        ds_ref[...] = dsb
        dn = jnp.zeros((tm, D_MODEL), F32)
        for j in range(N_CHIPS):
            dn = dn + _dot_nt(dsb[:, j * cs:(j + 1) * cs], win_v[j])
        xv = h_ref[...]
        g = g_ref[...]
        _, rstd = _rms_fwd(xv, g)
        dx, dg = _rms_bwd(dn, xv, rstd, g)
        dx_ref[...] = dhv + dx
        dxb_ref[...] = (dhv + dx).astype(BF16)
        dg_ref[...] += dg

        @pl.when(i == nt - 1)
        def _():
            mask = _tril_mask()
            for g in range(C_GROUPS):
                full = dws_acc[g]
                dws_ref[g] = jnp.where(mask, full[0:CHUNK, 0:CHUNK] + full[CHUNK:PAIR, CHUNK:PAIR], 0.0)
                dbs_ref[g:g + 1, :] = jnp.sum(dbs_acc[g].T, axis=0, keepdims=True)

    row = lambda cols: jax.ShapeDtypeStruct((1, cols), F32)
    return _pallas(
        body, [dh, h, norm_g, s, cdf, sv, w_in, ln_g, ln_b, w_s, w_out], name="bwd_odd", grid=(nt,),
        in_specs=[_row_spec(tm, D_MODEL), _row_spec(tm, D_MODEL), _full_spec((1, D_MODEL)), _row_spec(tm, 2 * C_DIM),
                  _row_spec(tm, 2 * C_DIM), _row_spec(tm, C_DIM), ANY, _full_spec((1, C_DIM)), _full_spec((1, C_DIM)),
                  _full_spec((C_GROUPS, CHUNK, CHUNK)), ANY],
        out_specs=[_row_spec(tm, D_MODEL), _row_spec(tm, D_MODEL), _row_spec(tm, 2 * C_DIM), _full_spec((1, D_MODEL)),
                   _full_spec((1, 2 * C_DIM)),
                   _full_spec((1, C_DIM)), _full_spec((1, C_DIM)), _full_spec((C_GROUPS, CHUNK, CHUNK)),
                   _full_spec((C_GROUPS, CHUNK))],
        out_shape=[jax.ShapeDtypeStruct((tokens, D_MODEL), F32), jax.ShapeDtypeStruct((tokens, D_MODEL), BF16),
                   jax.ShapeDtypeStruct((tokens, 2 * C_DIM), BF16),
                   row(D_MODEL), row(2 * C_DIM), row(C_DIM), row(C_DIM),
                   jax.ShapeDtypeStruct((C_GROUPS, CHUNK, CHUNK), F32), jax.ShapeDtypeStruct((C_GROUPS, CHUNK), F32)],
        scratch_shapes=[pltpu.VMEM((N_CHIPS, D_MODEL, cs), BF16), pltpu.VMEM((C_DIM, D_MODEL), BF16),
                        pltpu.VMEM((C_GROUPS, PAIR, PAIR), BF16), pltpu.VMEM((C_GROUPS, PAIR, PAIR), F32),
                        pltpu.VMEM((C_GROUPS, CHUNK, CHUNK), F32), pltpu.VMEM((tm, C_DIM), F32),
                        pltpu.SemaphoreType.DMA((N_LOADS,))],
        vmem_mib=56, riders=riders)


def _bwd_even(dh, x, norm_g, z, a2, cv, w_in, conv_a_w, ln_g, ln_b, conv_b_w, w_out, *, tm, seq, riders=()):
    tokens = x.shape[0]
    nt, tps = tokens // tm, seq // tm
    ws = IN_EVEN // N_CHIPS

    def body(dh_ref, x_ref, g_ref, z_ref, a2_ref, cv_ref, win_hbm, caw_ref, lng_ref, lnb_ref, cbw_ref, wout_hbm,
             dx_ref, dz_ref, dg_ref, dcaw_ref, dcab_ref, dlng_ref, dlnb_ref, dcbw_ref,
             win_v, wout_v, ea, eb, a1s, da1s, sigs, wide, dw_acc, sem):
        i = pl.program_id(0)

        _load_weights([(win_hbm, win_v, False), (wout_hbm, wout_v, True)], sem)

        @pl.when(i == 0)
        def _():
            dw_acc[...] = jnp.zeros(dw_acc.shape, F32)
            for ref in (dg_ref, dcab_ref, dlng_ref, dlnb_ref, dcbw_ref):
                ref[...] = jnp.zeros(ref.shape, F32)

        last = ((nt - 1 - i) % tps) == tps - 1

        @pl.when(last)
        def _():
            ea[0, tm:tm + A_HALO, :] = jnp.zeros((A_HALO, A_DIM), F32)
            eb[tm:tm + B_HALO, :] = jnp.zeros((B_HALO, B_DIM), F32)

        @pl.when(jnp.logical_not(last))
        def _():
            ea[0, tm:tm + A_HALO, :] = ea[0, 0:A_HALO, :]
            eb[tm:tm + B_HALO, :] = eb[0:B_HALO, :]

        wide[...] = _dot_nt(dh_ref[...].astype(BF16), wout_v[...])
        lng, lnb = lng_ref[...], lnb_ref[...]
        zero_row = jnp.zeros((1, A_DIM), F32)
        dlng, dlnb, dcab = zero_row, zero_row, zero_row
        for r0 in range(0, tm, ELEM_ROWS):
            rows = slice(r0, r0 + ELEM_ROWS)
            a_val, a_gate = z_ref[rows, 0:A_DIM].astype(F32), z_ref[rows, A_DIM:2 * A_DIM].astype(F32)
            xhat, rs = _ln_stats(a2_ref[rows, :])
            a3 = xhat * lng + lnb
            sg = jax.nn.sigmoid(a3)
            da3 = wide[rows, 0:A_DIM] * (sg * (1.0 + a3 * (1.0 - sg)))
            da2, g_part, b_part = _ln_bwd(da3, xhat, rs, lng)
            dlng, dlnb, dcab = dlng + g_part, dlnb + b_part, dcab + jnp.sum(da2, axis=0, keepdims=True)
            ea[0, rows, :] = da2
            eb[rows, :] = wide[rows, A_DIM:A_DIM + B_DIM] * z_ref[rows, 1024:1536].astype(F32)
            sig = jax.nn.sigmoid(a_gate)
            sigs[rows, :] = sig
            a1s[rows, :] = a_val * sig
        dlng_ref[...] += dlng
        dlnb_ref[...] += dlnb
        dcab_ref[...] += dcab
        _fill_shifted(ea, tm + A_HALO)
        for r0 in range(0, tm, CONV_ROWS):
            acc = jnp.zeros((CONV_ROWS, A_DIM), F32)
            for j in range(A_CONV_WIDTH):
                acc = acc + caw_ref[A_CONV_WIDTH - 1 - j:A_CONV_WIDTH - j, :] * _window(ea, r0 + j, CONV_ROWS)
            da1s[r0:r0 + CONV_ROWS, :] = acc
        for j0 in range(0, A_CONV_WIDTH, DW_TAPS):
            taps = range(j0, min(j0 + DW_TAPS, A_CONV_WIDTH))
            part = [jnp.zeros((CONV_ROWS, A_DIM), F32) for _ in taps]
            for r0 in range(0, tm, CONV_ROWS):
                a1c = a1s[r0:r0 + CONV_ROWS, :]
                for u, j in enumerate(taps):
                    part[u] = part[u] + _window(ea, r0 + j, CONV_ROWS) * a1c
            for u, j in enumerate(taps):
                dw_acc[A_CONV_WIDTH - 1 - j] += part[u]
        dcbw = [jnp.zeros((1, B_DIM), F32) for _ in range(B_CONV_WIDTH)]
        for r0 in range(0, tm, ELEM_ROWS):
            rows = slice(r0, r0 + ELEM_ROWS)
            da1, sig = da1s[rows, :], sigs[rows, :]
            dz_ref[rows, 0:A_DIM] = (da1 * sig).astype(BF16)
            dz_ref[rows, A_DIM:2 * A_DIM] = (da1 * z_ref[rows, 0:A_DIM].astype(F32) * (sig * (1.0 - sig))).astype(BF16)
            c_gate, b_val = z_ref[rows, 1536:2048].astype(F32), z_ref[rows, 2048:2560].astype(F32)
            dz_ref[rows, 1024:1536] = (wide[rows, A_DIM:A_DIM + B_DIM] * cv_ref[rows, :].astype(F32)).astype(BF16)
            cb = c_gate * b_val
            dcb = jnp.zeros((ELEM_ROWS, B_DIM), F32)
            for j in range(B_CONV_WIDTH):
                k = B_CONV_WIDTH - 1 - j
                sl = eb[r0 + j:r0 + j + ELEM_ROWS, :]
                dcb = dcb + cbw_ref[k:k + 1, :] * sl
                dcbw[k] = dcbw[k] + jnp.sum(sl * cb, axis=0, keepdims=True)
            dz_ref[rows, 1536:2048] = (dcb * b_val).astype(BF16)
            dz_ref[rows, 2048:2560] = (dcb * c_gate).astype(BF16)
        for k in range(B_CONV_WIDTH):
            dcbw_ref[k:k + 1, :] += dcbw[k]
        dn = jnp.zeros((tm, D_MODEL), F32)
        for j in range(N_CHIPS):
            dn = dn + _dot_nt(dz_ref[:, j * ws:(j + 1) * ws], win_v[j])
        wide[...] = dn
        g = g_ref[...]
        dg = jnp.zeros((1, D_MODEL), F32)
        for r0 in range(0, tm, ELEM_ROWS):
            rows = slice(r0, r0 + ELEM_ROWS)
            xv = x_ref[rows, :]
            _, rstd = _rms_fwd(xv, g)
            dx, dg_part = _rms_bwd(wide[rows, :], xv, rstd, g)
            dx_ref[rows, :] = dh_ref[rows, :] + dx
            dg = dg + dg_part
        dg_ref[...] += dg

        @pl.when(i == nt - 1)
        def _():
            for k in range(A_CONV_WIDTH):
                dcaw_ref[k:k + 1, :] = jnp.sum(dw_acc[k], axis=0, keepdims=True)

    row = lambda cols: jax.ShapeDtypeStruct((1, cols), F32)
    rs_ = functools.partial(_row_spec, rev_nt=nt)
    return _pallas(
        body, [dh, x, norm_g, z, a2, cv, w_in, conv_a_w, ln_g, ln_b, conv_b_w, w_out], name="bwd_even", grid=(nt,),
        in_specs=[rs_(tm, D_MODEL), rs_(tm, D_MODEL), _full_spec((1, D_MODEL)), rs_(tm, IN_EVEN), rs_(tm, A_DIM),
                  rs_(tm, B_DIM), ANY, _full_spec((A_CONV_WIDTH, A_DIM)), _full_spec((1, A_DIM)), _full_spec((1, A_DIM)),
                  _full_spec((B_CONV_WIDTH, B_DIM)), ANY],
        out_specs=[rs_(tm, D_MODEL), rs_(tm, IN_EVEN), _full_spec((1, D_MODEL)), _full_spec((A_CONV_WIDTH, A_DIM)),
                   _full_spec((1, A_DIM)), _full_spec((1, A_DIM)), _full_spec((1, A_DIM)), _full_spec((B_CONV_WIDTH, B_DIM))],
        out_shape=[jax.ShapeDtypeStruct((tokens, D_MODEL), F32), jax.ShapeDtypeStruct((tokens, IN_EVEN), BF16),
                   row(D_MODEL), jax.ShapeDtypeStruct((A_CONV_WIDTH, A_DIM), F32), row(A_DIM), row(A_DIM), row(A_DIM),
                   jax.ShapeDtypeStruct((B_CONV_WIDTH, B_DIM), F32)],
        scratch_shapes=[pltpu.VMEM((N_CHIPS, D_MODEL, ws), BF16), pltpu.VMEM((D_MODEL, D_MODEL), BF16),
                        pltpu.VMEM((SUBLANES, tm + A_HALO, A_DIM), F32), pltpu.VMEM((tm + B_HALO, B_DIM), F32),
                        pltpu.VMEM((tm, A_DIM), F32), pltpu.VMEM((tm, A_DIM), F32), pltpu.VMEM((tm, A_DIM), F32),
                        pltpu.VMEM((tm, D_MODEL), F32),
                        pltpu.VMEM((A_CONV_WIDTH, CONV_ROWS, A_DIM), F32), pltpu.SemaphoreType.DMA((N_LOADS,))],
        vmem_mib=56, riders=riders)


def _wgrad(a, b, name, *, col_shards, riders=()):
    tokens, m = a.shape
    n = b.shape[1]
    kc = 512
    if col_shards:
        bm, bn = m // 2, n // N_CHIPS
        grid = (2, N_CHIPS)
        out_spec = pl.BlockSpec((None, None, bm, bn), lambda i, j: (j, i, 0, 0))
    elif m // 8 >= MXU_ROWS:
        bm, bn = m // 8, n
        grid = (8, 1)
        out_spec = pl.BlockSpec((None, None, bm, bn), lambda i, j: (i // 2, i % 2, 0, 0))
    else:
        bm, bn = m // N_CHIPS, n
        grid = (N_CHIPS, 1)
        out_spec = pl.BlockSpec((None, 2, bm // 2, bn), lambda i, j: (i, 0, 0, 0))

    def body(a_ref, b_ref, o_ref):
        acc = jnp.zeros((bm, bn), F32)
        for k0 in range(0, tokens, kc):
            acc = acc + _dot_tn(a_ref[k0:k0 + kc, :].astype(BF16), b_ref[k0:k0 + kc, :].astype(BF16))
        if len(o_ref.shape) == 3:
            o_ref[0] = acc[0:bm // 2]
            o_ref[1] = acc[bm // 2:bm]
        else:
            o_ref[...] = acc

    out_rows = m // 2 if col_shards else m // 8
    outs, routs = _pallas(
        body, [a, b], name=name, grid=grid,
        in_specs=[pl.BlockSpec((tokens, bm), lambda i, j: (0, i)), pl.BlockSpec((tokens, bn), lambda i, j: (0, j))],
        out_specs=[out_spec], out_shape=[jax.ShapeDtypeStruct((N_CHIPS, 2, out_rows, bn), F32)],
        vmem_mib=56, riders=riders)
    return outs[0], routs


def _wgrad_pair(a, b, name, *, col_shards, riders=()):
    tokens, m = a.shape
    n = b.shape[1]
    kc = 512
    c0 = lax.axis_index("c")

    def half(ph, pre):
        return (ph + 1 + pre[0]) % 2

    if col_shards:
        bm, bn = m // 2, n // N_CHIPS
        a_spec = pl.BlockSpec((tokens, bm), lambda ph, q, pre: (0, half(ph, pre)))
        b_spec = pl.BlockSpec((tokens, bn), lambda ph, q, pre: (0, q))
    else:
        bm, bn = m // 8, n
        a_spec = pl.BlockSpec((tokens, bm), lambda ph, q, pre: (0, 2 * q + half(ph, pre)))
        b_spec = pl.BlockSpec((tokens, bn), lambda ph, q, pre: (0, 0))

    def body(pre_ref, a_ref, b_ref, o_ref, give, got, send_sems, recv_sems):
        ph, q = pl.program_id(0), pl.program_id(1)
        acc = jnp.zeros((bm, bn), F32)
        for k0 in range(0, tokens, kc):
            acc = acc + _dot_tn(a_ref[k0:k0 + kc, :].astype(BF16), b_ref[k0:k0 + kc, :].astype(BF16))
        x, y, cc = _mesh_pos()

        def tile(t):
            return _remote(give.at[t], got.at[t], send_sems.at[t], recv_sems.at[t], (x, y, 1 - cc))

        @pl.when(ph == 0)
        def _():
            give[q] = acc
            tile(q).start()

        @pl.when(ph == 1)
        def _():
            tile(q).wait_recv()
            o_ref[...] = (acc + got[q]).astype(BF16)

        @pl.when((ph == 1) & (q == N_CHIPS - 1))
        def _():
            for t in range(N_CHIPS):
                tile(t).wait_send()

    outs, routs = _pallas(
        body, [a, b], name=name, grid=(2, N_CHIPS), in_specs=[a_spec, b_spec],
        out_specs=[pl.BlockSpec((None, bm, bn), lambda ph, q, pre: (ph * q, 0, 0))],
        out_shape=[jax.ShapeDtypeStruct((N_CHIPS, bm, bn), BF16)],
        scratch_shapes=[pltpu.VMEM((N_CHIPS, bm, bn), F32), pltpu.VMEM((N_CHIPS, bm, bn), F32),
                        pltpu.SemaphoreType.DMA((N_CHIPS,)), pltpu.SemaphoreType.DMA((N_CHIPS,))],
        vmem_mib=56, riders=riders, prefetch=jnp.reshape(c0, (1,)).astype(jnp.int32))
    return outs[0], routs


class _GradReduce:
    def __init__(self, name, grad=None, chip_sum=None):
        self.name, self.grad, self.chip_sum = name, grad, chip_sum
        self.full = None

    def pair_swap(self):
        return _PairSwap([self.grad])

    def took_pair(self, outs):
        self.chip_sum = _in_hbm(_add_pair(self.grad, outs[0], f"pair_sum_{self.name}"))

    def took_chips(self, outs):
        self.full = _in_hbm(_add_chips(self.chip_sum, outs[0], f"chip_sum_{self.name}"))

    def chips_beside(self, collective_id):
        self.took_chips([_chip_swap_beside(self.chip_sum, f"chip_swap_{self.name}", collective_id)])

    def pair_share(self):
        return _PairShare([self.full])

    def took_share(self, outs):
        self.full = outs[0]

    def reduced(self):
        return jnp.reshape(self.full, (2 * self.full.shape[1], self.full.shape[2]))


def _forward_backward(x2, tgt2, w, conv_a_w, conv_b_w, od_norm, od_bias, od_lng, od_lnb,
                      ev_norm_g, ev_conv_a_b, ev_ln_a_g, ev_ln_a_b, od_w_s, od_b_s, mlp_norm_g, final_norm_g,
                      *, tm, seq, distributed=True):
    d = x2.shape[1]
    b_s_rows = jnp.broadcast_to(od_b_s[0][:, :, None], (C_GROUPS, CHUNK, CHUNK))
    (h1, n0, z, a2, cv, mix), _ = _fwd_even(
        x2, ev_norm_g, w["ev_in"], conv_a_w, ev_conv_a_b, ev_ln_a_g, ev_ln_a_b, conv_b_w, w["ev_out"], tm=tm, seq=seq)
    (h2, n1, p0, q0), _ = _fwd_mlp(h1, mlp_norm_g[0:1], w["w1_0"], w["w2_0"], 0, tm=tm)
    (h3, n2, s, cdf, sv, y), _ = _fwd_odd(h2, od_norm, w["od_in"], od_bias, od_lng, od_lnb, od_w_s[0], b_s_rows,
                                          w["od_out"], tm=tm)
    (n3, p1, q1, loss_part, dh4, dh4b, d_final_g), _ = _fwd_mlp(
        h3, mlp_norm_g[1:2], w["w1_1"], w["w2_1"], 1, tm=tm,
        head=(jnp.reshape(final_norm_g, (1, d)), tgt2))

    red = {}

    def swap(*names):
        return [red[nm].pair_swap() for nm in names] if distributed else []

    def share(*names):
        return [red[nm].pair_share() for nm in names] if distributed else []

    def took(routs, *steps):
        if distributed:
            for (nm, what), outs in zip(steps, routs):
                getattr(red[nm], what)(outs)

    swap_ids = iter(range(FIRST_SWAP_ID, FIRST_SWAP_ID + 8))

    def beside(name):
        if distributed:
            red[name].chips_beside(next(swap_ids))

    def big(lhs, rhs, name, col_shards, riders=()):
        if distributed:
            chip_sum, routs = _wgrad_pair(lhs, rhs, f"wgrad_{name}", col_shards=col_shards, riders=riders)
            red[name] = _GradReduce(name, chip_sum=_in_hbm(chip_sum))
        else:
            g, routs = _wgrad(lhs, rhs, f"wgrad_{name}", col_shards=col_shards)
            red[name] = _GradReduce(name, grad=g)
        return routs

    big(q1, dh4b, "w2_1", False)
    beside("w2_1")
    (dh3, dh3b, dp1, d_mlp_g1), _ = _bwd_mlp(dh4, h3, mlp_norm_g[1:2], p1, w["w1_1"], w["w2_1"], 1, tm=tm)
    big(n3, dp1, "w1_1", True)
    beside("w1_1")
    g, routs = _wgrad(y, dh3b, "wgrad_od_out", col_shards=False, riders=share("w2_1"))
    red["od_out"] = _GradReduce("od_out", grad=g)
    took(routs, ("w2_1", "took_share"))
    (dh2, dh2b, ds, d_od_norm, d_od_bin, d_od_lng, d_od_lnb, d_ws, d_bs), _ = _bwd_odd(
        dh3, h2, od_norm, s, cdf, sv, w["od_in"], od_lng, od_lnb, od_w_s[0], w["od_out"], tm=tm)
    routs = big(n2, ds, "od_in", True, riders=share("w1_1") + swap("od_out"))
    took(routs, ("w1_1", "took_share"), ("od_out", "took_pair"))
    beside("od_in")
    beside("od_out")
    half_groups = C_GROUPS // 2
    early = {"loss": loss_part, "od_w_s_lo": d_ws[:half_groups], "od_b_s": d_bs, "mlp_norm_g1": d_mlp_g1, "final_norm_g": d_final_g,
             "od_norm_g": d_od_norm, "od_b_in": d_od_bin, "od_ln_v_g": d_od_lng, "od_ln_v_b": d_od_lnb}
    share_early = [_ShareAll(list(early.values()))] if distributed else []
    routs = big(q0, dh2b, "w2_0", False, riders=share_early)
    landed_early = routs[0] if distributed else []
    beside("w2_0")
    (dh1, dh1b, dp0, d_mlp_g0), _ = _bwd_mlp(dh2, h1, mlp_norm_g[0:1], p0, w["w1_0"], w["w2_0"], 0, tm=tm)
    middle = {"od_w_s_hi": d_ws[half_groups:]}
    share_middle = [_ShareAll(list(middle.values()))] if distributed else []
    g, _ = _wgrad(mix, dh1b, "wgrad_ev_out", col_shards=False)
    red["ev_out"] = _GradReduce("ev_out", grad=g)
    routs = big(n1, dp0, "w1_0", True,
                riders=share("od_out") + share("od_in") + share("w2_0") + swap("ev_out") + share_middle)
    took(routs, ("od_out", "took_share"), ("od_in", "took_share"), ("w2_0", "took_share"), ("ev_out", "took_pair"))
    landed_middle = routs[4] if distributed else []
    beside("w1_0")
    beside("ev_out")

    (dx, dz, d_ev_norm, d_caw, d_cab, d_ev_lng, d_ev_lnb, d_cbw), _ = _bwd_even(
        dh1, x2, ev_norm_g, z, a2, cv, w["ev_in"], conv_a_w, ev_ln_a_g, ev_ln_a_b, conv_b_w, w["ev_out"], tm=tm, seq=seq)
    late = {"mlp_norm_g0": d_mlp_g0, "ev_norm_g": d_ev_norm, "ev_conv_a_b": d_cab, "ev_ln_a_g": d_ev_lng,
            "ev_ln_a_b": d_ev_lnb, "ev_conv_a_w": d_caw, "ev_conv_b_w": d_cbw}
    share_late = [_ShareAll(list(late.values()))] if distributed else []
    routs = big(n0, dz, "ev_in", True, riders=share("ev_out") + share("w1_0") + share_late)
    took(routs, ("ev_out", "took_share"), ("w1_0", "took_share"))
    beside("ev_in")
    own = {**early, **middle, **late}
    landed = dict(zip(own.keys(), landed_early + landed_middle + routs[2])) if distributed else None
    return dx, red, own, landed


def _rows128(a):
    rows = jnp.reshape(a, (-1, LANES))
    pad = (-rows.shape[0]) % SUBLANES
    return jnp.pad(rows, ((0, pad), (0, 0))) if pad else rows


def _pack(arrays):
    return jnp.concatenate([_rows128(a) for a in arrays], axis=0)


def _unpack(buf, shapes):
    out, r0 = [], 0
    for shp in shapes:
        size = 1
        for dim in shp:
            size *= dim
        nr = size // LANES
        out.append(jnp.reshape(buf[r0:r0 + nr], shp))
        r0 += nr + (-nr) % SUBLANES
    return out


def kernel(x, ev_norm_g, ev_w_in, ev_conv_a_w, ev_conv_a_b, ev_ln_a_g, ev_ln_a_b, ev_conv_b_w, ev_w_out, od_norm_g, od_w_in, od_b_in, od_ln_v_g, od_ln_v_b, od_w_s, od_b_s, od_w_out, mlp_norm_g, mlp_w1, mlp_w2, final_norm_g, loss_target, m_ev_norm_g, m_ev_w_in, m_ev_conv_a_w, m_ev_conv_a_b, m_ev_ln_a_g, m_ev_ln_a_b, m_ev_conv_b_w, m_ev_w_out, m_od_norm_g, m_od_w_in, m_od_b_in, m_od_ln_v_g, m_od_ln_v_b, m_od_w_s, m_od_b_s, m_od_w_out, m_mlp_norm_g, m_mlp_w1, m_mlp_w2, m_final_norm_g, v_ev_norm_g, v_ev_w_in, v_ev_conv_a_w, v_ev_conv_a_b, v_ev_ln_a_g, v_ev_ln_a_b, v_ev_conv_b_w, v_ev_w_out, v_od_norm_g, v_od_w_in, v_od_b_in, v_od_ln_v_g, v_od_ln_v_b, v_od_w_s, v_od_b_s, v_od_w_out, v_mlp_norm_g, v_mlp_w1, v_mlp_w2, v_final_norm_g):
    tm = TOKEN_TILE
    batch, seq, d = x.shape
    tokens = batch * seq
    x2 = jnp.reshape(x, (tokens, d))
    tgt2 = jnp.reshape(loss_target, (tokens, d))
    chip = 2 * lax.axis_index("x") + lax.axis_index("y")

    small_shapes = [(A_CONV_WIDTH, LANES), (B_CONV_WIDTH, LANES), (256,), (512,), (256,), (256,)]
    small_shard = _pack([ev_conv_a_w[0], ev_conv_b_w[0], od_norm_g[0], od_b_in[0], od_ln_v_g[0], od_ln_v_b[0]])
    small_shard = jnp.pad(small_shard, ((0, (-small_shard.shape[0]) % (4 * SUBLANES)), (0, 0)))
    first = [_place_shard(ev_w_in, 0, BF16, "place_ev_w_in"), _place_shard(ev_w_out, 0, BF16, "place_ev_w_out"),
             _place_shard(small_shard[None], 0, F32, "place_small")]
    staged = {
        "w1_0": _place_shard(mlp_w1, 0, BF16, "place_w1_0"), "w2_0": _place_shard(mlp_w2, 0, BF16, "place_w2_0"),
        "od_in": _place_shard(od_w_in, 0, BF16, "place_od_w_in"), "od_out": _place_shard(od_w_out, 0, BF16, "place_od_w_out"),
        "w1_1": _place_shard(mlp_w1, 1, BF16, "place_w1_1"), "w2_1": _place_shard(mlp_w2, 1, BF16, "place_w2_1"),
    }
    first = [_in_hbm(a) for a in first]
    staged = {nm: _in_hbm(a) for nm, a in staged.items()}
    g_ev_in, g_ev_out, g_small = _gather_beside(first, "gather_stage0", collective_id=1)
    gathered = {"ev_in": g_ev_in, "ev_out": g_ev_out}
    for stage, names in enumerate((("w1_0", "w2_0"), ("od_in", "od_out", "w1_1"), ("w2_1",))):
        done = _gather_beside([staged[nm] for nm in names], f"gather_stage{stage + 1}", collective_id=stage + 2)
        gathered.update(zip(names, done))
    small_all = jnp.reshape(_plain_copy(g_small, "small_weights_copy"), (N_CHIPS, -1, LANES))
    per_chip = [_unpack(small_all[q], small_shapes) for q in range(N_CHIPS)]
    conv_a_w = jnp.concatenate([pc[0] for pc in per_chip], axis=1)
    conv_b_w = jnp.concatenate([pc[1] for pc in per_chip], axis=1)
    od_norm = jnp.concatenate([pc[2] for pc in per_chip])[None, :]
    od_bias = jnp.concatenate([pc[3] for pc in per_chip])[None, :]
    od_lng = jnp.concatenate([pc[4] for pc in per_chip])[None, :]
    od_lnb = jnp.concatenate([pc[5] for pc in per_chip])[None, :]

    dx, red, own, landed = _forward_backward(
        x2, tgt2, gathered, conv_a_w, conv_b_w, od_norm, od_bias, od_lng, od_lnb,
        ev_norm_g, ev_conv_a_b, ev_ln_a_g, ev_ln_a_b, od_w_s, od_b_s, mlp_norm_g, final_norm_g, tm=tm, seq=seq)

    routs = _exchange([red["ev_in"].pair_share()], "reduce_tail")
    red["ev_in"].took_share(routs[0])

    given = {"ev_norm_g": (ev_norm_g, m_ev_norm_g, v_ev_norm_g), "ev_conv_a_b": (ev_conv_a_b, m_ev_conv_a_b, v_ev_conv_a_b),
             "ev_ln_a_g": (ev_ln_a_g, m_ev_ln_a_g, v_ev_ln_a_g), "ev_ln_a_b": (ev_ln_a_b, m_ev_ln_a_b, v_ev_ln_a_b),
             "od_w_s": (od_w_s, m_od_w_s, v_od_w_s), "od_b_s": (od_b_s, m_od_b_s, v_od_b_s),
             "mlp_norm_g": (mlp_norm_g, m_mlp_norm_g, v_mlp_norm_g), "final_norm_g": (final_norm_g, m_final_norm_g, v_final_norm_g),
             "ev_conv_a_w": (ev_conv_a_w, m_ev_conv_a_w, v_ev_conv_a_w), "ev_conv_b_w": (ev_conv_b_w, m_ev_conv_b_w, v_ev_conv_b_w),
             "od_norm_g": (od_norm_g, m_od_norm_g, v_od_norm_g), "od_b_in": (od_b_in, m_od_b_in, v_od_b_in),
             "od_ln_v_g": (od_ln_v_g, m_od_ln_v_g, v_od_ln_v_g), "od_ln_v_b": (od_ln_v_b, m_od_ln_v_b, v_od_ln_v_b)}
    shaped = {nm: tuple(jnp.reshape(a, shape) for a in given[nm]) for nm, shape, _, _ in SMALL_WEIGHTS}
    loss11, small_upd = _small_update(own, landed, shaped)
    loss = loss11[0, 0]
    upd = {nm: [jnp.reshape(o, given[nm][0].shape) for o in outs] for nm, outs in small_upd.items()}

    def big_update(wt, m, v, names, call):
        grads = [red[nm].reduced() for nm in names]
        shp3 = (len(grads),) + grads[0].shape
        outs = _adamw(jnp.reshape(wt, shp3), jnp.reshape(m, shp3), jnp.reshape(v, shp3), grads, call)
        return [jnp.reshape(o, wt.shape) for o in outs]

    wide = {"mlp_w2": (mlp_w2, m_mlp_w2, v_mlp_w2, ["w2_0", "w2_1"]), "mlp_w1": (mlp_w1, m_mlp_w1, v_mlp_w1, ["w1_0", "w1_1"]),
            "ev_w_out": (ev_w_out, m_ev_w_out, v_ev_w_out, ["ev_out"]), "od_w_out": (od_w_out, m_od_w_out, v_od_w_out, ["od_out"])}
    streamed = []
    for wt, m, v, names in wide.values():
        grads = [red[nm].reduced() for nm in names]
        shp3 = (len(grads),) + grads[0].shape
        streamed.append((jnp.reshape(wt, shp3), jnp.reshape(m, shp3), jnp.reshape(v, shp3), grads))
    for (nm, (wt, _, _, _)), outs in zip(wide.items(), _adamw_stream(streamed, "adamw_wide")):
        upd[nm] = [jnp.reshape(o, wt.shape) for o in outs]
    upd["ev_w_in"] = big_update(ev_w_in, m_ev_w_in, v_ev_w_in, ["ev_in"], "adamw_ev_w_in")
    upd["od_w_in"] = big_update(od_w_in, m_od_w_in, v_od_w_in, ["od_in"], "adamw_od_w_in")

    order = ["ev_norm_g", "ev_w_in", "ev_conv_a_w", "ev_conv_a_b", "ev_ln_a_g", "ev_ln_a_b", "ev_conv_b_w", "ev_w_out",
             "od_norm_g", "od_w_in", "od_b_in", "od_ln_v_g", "od_ln_v_b", "od_w_s", "od_b_s", "od_w_out", "mlp_norm_g",
             "mlp_w1", "mlp_w2", "final_norm_g"]
    grad_x = jnp.reshape(dx, x.shape)
    return (loss, grad_x, *[upd[nm][0] for nm in order], *[upd[nm][1] for nm in order],
            *[upd[nm][2] for nm in order], *[upd[nm][3] for nm in order])
```

```python
import functools

import jax
import jax.numpy as jnp
from jax import lax
from jax.experimental import pallas as pl
from jax.experimental.pallas import tpu as pltpu
from jax.experimental.pallas import tpu_sc as plsc

F32 = jnp.float32
BF16 = jnp.bfloat16

D_MODEL = 1024
A_DIM = 512
B_DIM = 512
IN_EVEN = 2 * A_DIM + 3 * B_DIM
A_CONV_WIDTH = 31
B_CONV_WIDTH = 3
CHUNK = 128
C_GROUPS = 8
C_DIM = 1024
D_FF = 4096
RMS_EPS = 1e-6
LN_EPS = 1e-5
ADAM_LR = 0.001
ADAM_B1 = 0.9
ADAM_B2 = 0.999
ADAM_EPS = 1e-08
ADAM_WD = 0.01
ADAM_STEP = 10

N_CHIPS = 4
N_DEV = 8
TOKEN_TILE = 512
A_HALO = 32
B_HALO = 8
CONV_ROWS = 16
DW_TAPS = 4
ELEM_ROWS = 16
PAIR = 2 * CHUNK
LANES = 128
SUBLANES = 8
MXU_ROWS = 256
MIB = 1024 * 1024
MESH = pl.DeviceIdType.MESH
ANY = pl.BlockSpec(memory_space=pl.ANY)


def _dot(a, b):
    return lax.dot_general(a, b, (((1,), (0,)), ((), ())), preferred_element_type=F32)


def _dot_nt(a, b):
    return lax.dot_general(a, b, (((1,), (1,)), ((), ())), preferred_element_type=F32)


def _dot_tn(a, b):
    return lax.dot_general(a, b, (((0,), (0,)), ((), ())), preferred_element_type=F32)


def _params(vmem_mib, n_axes=1):
    return pltpu.CompilerParams(dimension_semantics=("arbitrary",) * n_axes, vmem_limit_bytes=vmem_mib * MIB)


def _row_spec(tm, cols, rev_nt=None):
    if rev_nt is None:
        return pl.BlockSpec((tm, cols), lambda i: (i, 0))
    return pl.BlockSpec((tm, cols), lambda i: (rev_nt - 1 - i, 0))


def _full_spec(shape):
    nd = len(shape)
    return pl.BlockSpec(shape, lambda i: (0,) * nd)


def _block_rows(rows, cap=512):
    best = SUBLANES
    for br in range(SUBLANES, min(rows, cap) + 1, SUBLANES):
        if rows % br == 0:
            best = br
    return best


FIRST_SWAP_ID = 5
N_LOADS = 2 * 2 * N_CHIPS


def _load_weights(loads, sems):
    @pl.when(pl.program_id(0) == 0)
    def _():
        copies = []
        for src, dst, rows_of_one in loads:
            r = src.shape[2]
            for q in range(N_CHIPS):
                for h in range(2):
                    part = dst.at[pl.ds((2 * q + h) * r, r)] if rows_of_one else dst.at[q, pl.ds(h * r, r)]
                    copies.append(pltpu.make_async_copy(src.at[q, h], part, sems.at[len(copies)]))
        for cp in copies:
            cp.start()
        for cp in copies:
            cp.wait()


def _rms_fwd(x, g):
    rstd = lax.rsqrt(jnp.mean(x * x, axis=-1, keepdims=True) + RMS_EPS)
    return x * rstd * g, rstd


def _rms_bwd(dn, x, rstd, g):
    a = dn * g
    xh = x * rstd
    dx = rstd * (a - xh * jnp.mean(a * xh, axis=-1, keepdims=True))
    dg = jnp.sum(dn * xh, axis=0, keepdims=True)
    return dx, dg


def _ln_stats(v):
    mu = jnp.mean(v, axis=-1, keepdims=True)
    xc = v - mu
    rs = lax.rsqrt(jnp.mean(xc * xc, axis=-1, keepdims=True) + LN_EPS)
    return xc * rs, rs


def _ln_bwd(dy, xhat, rs, g):
    dxh = dy * g
    dv = rs * (dxh - jnp.mean(dxh, axis=-1, keepdims=True) - xhat * jnp.mean(dxh * xhat, axis=-1, keepdims=True))
    return dv, jnp.sum(dy * xhat, axis=0, keepdims=True), jnp.sum(dy, axis=0, keepdims=True)


def _gelu_cdf(s):
    return 0.5 * (1.0 + lax.erf(s * 0.7071067811865476))


def _mesh_pos():
    return lax.axis_index("x"), lax.axis_index("y"), lax.axis_index("c")


def _other_chips(x, y):
    return [(1 - x, y), (x, 1 - y), (1 - x, 1 - y)]


def _remote(src, dst, send_sem, recv_sem, to):
    return pltpu.make_async_remote_copy(src_ref=src, dst_ref=dst, send_sem=send_sem, recv_sem=recv_sem,
                                        device_id=to, device_id_type=MESH)


def _like(arrays):
    return [jax.ShapeDtypeStruct(a.shape, a.dtype) for a in arrays]


class _PairSwap:
    def __init__(self, grads):
        self.ins = list(grads)
        self.out_shapes = [jax.ShapeDtypeStruct((g.shape[0],) + g.shape[2:], g.dtype) for g in grads]
        self.aliases = {}
        self.n_sems = len(grads)

    def _copies(self, ins, outs, send, recv):
        x, y, c = _mesh_pos()
        return [_remote(ins[t].at[:, 1 - c], outs[t], send.at[t], recv.at[t], (x, y, 1 - c)) for t in range(len(ins))]

    def start(self, ins, outs, send, recv):
        for cp in self._copies(ins, outs, send, recv):
            cp.start()

    def finish(self, ins, outs, send, recv):
        for cp in self._copies(ins, outs, send, recv):
            cp.wait()


class _ChipSwap:
    def __init__(self, parts):
        self.ins = list(parts)
        self.out_shapes = [jax.ShapeDtypeStruct((3,) + p.shape[1:], p.dtype) for p in parts]
        self.aliases = {}
        self.n_sems = 3 * len(parts)

    def _copies(self, ins, outs, send, recv):
        x, y, c = _mesh_pos()
        return [_remote(ins[t].at[2 * chip[0] + chip[1]], outs[t].at[k], send.at[3 * t + k], recv.at[3 * t + k], (*chip, c))
                for t in range(len(ins)) for k, chip in enumerate(_other_chips(x, y))]

    def start(self, ins, outs, send, recv):
        for cp in self._copies(ins, outs, send, recv):
            cp.start()

    def finish(self, ins, outs, send, recv):
        for cp in self._copies(ins, outs, send, recv):
            cp.wait()


class _PairShare:
    def __init__(self, fulls):
        self.ins = list(fulls)
        self.out_shapes = _like(fulls)
        self.aliases = {t: t for t in range(len(fulls))}
        self.n_sems = len(fulls)

    def _copies(self, ins, outs, send, recv):
        x, y, c = _mesh_pos()
        return [_remote(ins[t].at[c], outs[t].at[c], send.at[t], recv.at[t], (x, y, 1 - c)) for t in range(len(ins))]

    def start(self, ins, outs, send, recv):
        for cp in self._copies(ins, outs, send, recv):
            cp.start()

    def finish(self, ins, outs, send, recv):
        for cp in self._copies(ins, outs, send, recv):
            cp.wait()


class _ShareAll:
    def __init__(self, arrays):
        self.ins = list(arrays)
        self.out_shapes = [jax.ShapeDtypeStruct((N_DEV,) + a.shape, a.dtype) for a in arrays]
        self.aliases = {}
        self.n_sems = (N_DEV - 1) * len(arrays)

    def _peers(self):
        x, y, c = _mesh_pos()
        flips = [((r >> 2) & 1, (r >> 1) & 1, r & 1) for r in range(1, N_DEV)]
        return (x, y, c), [(x ^ fx, y ^ fy, c ^ fc) for fx, fy, fc in flips]

    def _sends(self, ins, outs, send, recv):
        (x, y, c), peers = self._peers()
        mine = 4 * x + 2 * y + c
        return [_remote(ins[a], outs[a].at[mine], send.at[7 * a + r], recv.at[7 * a + r], peer)
                for a in range(len(ins)) for r, peer in enumerate(peers)]

    def start(self, ins, outs, send, recv):
        for cp in self._sends(ins, outs, send, recv):
            cp.start()

    def finish(self, ins, outs, send, recv):
        (x, y, c), peers = self._peers()
        for a in range(len(ins)):
            for r, (px, py, pc) in enumerate(peers):
                blk = outs[a].at[4 * px + 2 * py + pc]
                _remote(blk, blk, send.at[7 * a + r], recv.at[7 * a + r], (x, y, c)).wait_recv()
        for cp in self._sends(ins, outs, send, recv):
            cp.wait_send()


def _gather_beside(bufs, name, collective_id):
    n = len(bufs)
    per = 7
    refs = [jax.new_ref(b, memory_space=pltpu.MemorySpace.HBM) for b in bufs]

    @pl.kernel(mesh=plsc.ScalarSubcoreMesh(axis_name="sequencer", num_cores=1), name=name,
               scratch_types=(pltpu.SemaphoreType.DMA((per * n,)), pltpu.SemaphoreType.DMA((per * n,))),
               compiler_params=pltpu.CompilerParams(collective_id=collective_id))
    def launch(send, recv):
        x, y, c = _mesh_pos()
        me, sibling = (x, y, c), (x, y, 1 - c)
        x_nbr, y_nbr = (1 - x, y, c), (x, 1 - y, c)
        mine, via_x, via_y, diag = 2 * x + y, 2 * (1 - x) + y, 2 * x + (1 - y), 2 * (1 - x) + (1 - y)
        barrier = pltpu.get_barrier_semaphore()
        peers = [x_nbr, y_nbr, sibling]
        for peer in peers:
            pl.semaphore_signal(barrier, inc=1, device_id=peer, device_id_type=MESH)
        pl.semaphore_wait(barrier, len(peers))

        def copy(t, k, src, dst, to):
            return _remote(src, dst, send.at[per * t + k], recv.at[per * t + k], to)

        def piece(t, chip, half, rows=None):
            blk = refs[t].at[chip, half]
            return blk if rows is None else blk.at[rows]

        started = []

        def go(cp):
            cp.start()
            started.append(cp)

        upper = [pl.ds(0, r.shape[2] // 2) for r in refs]
        lower = [pl.ds(r.shape[2] // 2, r.shape[2] // 2) for r in refs]
        for t in range(n):
            go(copy(t, 0, piece(t, mine, c), piece(t, mine, c), x_nbr))
            go(copy(t, 1, piece(t, mine, c), piece(t, mine, c), y_nbr))
        for t in range(n):
            copy(t, 0, piece(t, via_x, c), piece(t, via_x, c), me).wait_recv()
            go(copy(t, 2, piece(t, via_x, c, upper[t]), piece(t, via_x, c, upper[t]), y_nbr))
            go(copy(t, 4, piece(t, via_x, c), piece(t, via_x, c), sibling))
            copy(t, 1, piece(t, via_y, c), piece(t, via_y, c), me).wait_recv()
            go(copy(t, 3, piece(t, via_y, c, lower[t]), piece(t, via_y, c, lower[t]), x_nbr))
            go(copy(t, 5, piece(t, via_y, c), piece(t, via_y, c), sibling))
        for t in range(n):
            copy(t, 2, piece(t, diag, c, upper[t]), piece(t, diag, c, upper[t]), me).wait_recv()
            copy(t, 3, piece(t, diag, c, lower[t]), piece(t, diag, c, lower[t]), me).wait_recv()
            go(copy(t, 6, piece(t, diag, c), piece(t, diag, c), sibling))
        for t in range(n):
            for k, chip in ((4, via_x), (5, via_y), (6, diag)):
                copy(t, k, piece(t, chip, 1 - c), piece(t, chip, 1 - c), me).wait_recv()
        for cp in started:
            cp.wait_send()

    launch()
    return [r[...] for r in refs]


def _chip_swap_beside(parts, name, collective_id):
    src = jax.new_ref(parts, memory_space=pltpu.MemorySpace.HBM)
    dst = jax.empty_ref(jax.ShapeDtypeStruct((N_CHIPS - 1,) + parts.shape[1:], parts.dtype),
                        memory_space=pltpu.MemorySpace.HBM)
    swap = _ChipSwap([parts])

    @pl.kernel(mesh=plsc.ScalarSubcoreMesh(axis_name="sequencer", num_cores=1), name=name,
               scratch_types=(pltpu.SemaphoreType.DMA((N_CHIPS - 1,)), pltpu.SemaphoreType.DMA((N_CHIPS - 1,))),
               compiler_params=pltpu.CompilerParams(collective_id=collective_id))
    def launch(send, recv):
        x, y, c = _mesh_pos()
        barrier = pltpu.get_barrier_semaphore()
        peers = [(*chip, c) for chip in _other_chips(x, y)]
        for peer in peers:
            pl.semaphore_signal(barrier, inc=1, device_id=peer, device_id_type=MESH)
        pl.semaphore_wait(barrier, len(peers))
        swap.start([src], [dst], send, recv)
        swap.finish([src], [dst], send, recv)

    launch()
    return dst[...]


def _pallas(body, operands, *, name, grid, in_specs, out_specs, out_shape, scratch_shapes=(), vmem_mib=32, riders=(),
            prefetch=None):
    in_specs, out_specs, out_shape, scratch_shapes = list(in_specs), list(out_specs), list(out_shape), list(scratch_shapes)
    if not riders and prefetch is None:
        outs = pl.pallas_call(body, name=name, grid=grid, in_specs=in_specs, out_specs=out_specs, out_shape=out_shape,
                              scratch_shapes=scratch_shapes, compiler_params=_params(vmem_mib, len(grid)))(*operands)
        return list(outs), []
    n_in, n_out, n_scr = len(in_specs), len(out_specs), len(scratch_shapes)
    r_in = [len(r.ins) for r in riders]
    r_out = [len(r.out_shapes) for r in riders]
    steps = 1
    for g in grid:
        steps *= g

    n_pre = 0 if prefetch is None else 1

    def wrapped(*refs):
        refs = list(refs)
        pre, refs = refs[:n_pre], refs[n_pre:]
        ins, refs = refs[:n_in], refs[n_in:]
        rins = []
        for k in r_in:
            rins.append(refs[:k])
            refs = refs[k:]
        outs, refs = refs[:n_out], refs[n_out:]
        routs = []
        for k in r_out:
            routs.append(refs[:k])
            refs = refs[k:]
        scr, sems = refs[:n_scr], refs[n_scr:]
        step = 0
        for ax, g in enumerate(grid):
            step = step * g + pl.program_id(ax)

        def each(what):
            for j, r in enumerate(riders):
                getattr(r, what)(rins[j], routs[j], sems[2 * j], sems[2 * j + 1])

        if grid:
            pl.when(step == 0)(lambda: each("start"))
        else:
            each("start")
        body(*pre, *ins, *outs, *scr)
        if grid:
            pl.when(step == steps - 1)(lambda: each("finish"))
        else:
            each("finish")

    aliases, off_in, off_out = {}, n_pre + n_in, n_out
    for r, ki, ko in zip(riders, r_in, r_out):
        for i, o in r.aliases.items():
            aliases[off_in + i] = off_out + o
        off_in, off_out = off_in + ki, off_out + ko
    sems = []
    for r in riders:
        sems += [pltpu.SemaphoreType.DMA((r.n_sems,)), pltpu.SemaphoreType.DMA((r.n_sems,))]
    layout = dict(grid=grid, in_specs=in_specs + [ANY] * sum(r_in), out_specs=out_specs + [ANY] * sum(r_out),
                  scratch_shapes=scratch_shapes + sems)
    if prefetch is not None:
        layout = dict(grid_spec=pltpu.PrefetchScalarGridSpec(num_scalar_prefetch=1, **layout))
    res = pl.pallas_call(
        wrapped, name=name, **layout,
        out_shape=out_shape + [s for r in riders for s in r.out_shapes], input_output_aliases=aliases,
        compiler_params=pltpu.CompilerParams(dimension_semantics=("arbitrary",) * len(grid),
                                             vmem_limit_bytes=vmem_mib * MIB, has_side_effects=True),
    )(*([] if prefetch is None else [prefetch]), *operands, *[a for r in riders for a in r.ins])
    res = list(res)
    outs, res = res[:n_out], res[n_out:]
    routs = []
    for k in r_out:
        routs.append(res[:k])
        res = res[k:]
    return outs, routs


def _exchange(riders, name):
    return _pallas(lambda: None, [], name=name, grid=(), in_specs=[], out_specs=[], out_shape=[], riders=riders)[1]


def _in_hbm(a):
    return pltpu.with_memory_space_constraint(a, pltpu.HBM)


def _place_shard(w, layer, dtype, name):
    _, rows, cols = w.shape
    half = rows // 2
    br = _block_rows(half)
    nb = half // br
    mine = 2 * lax.axis_index("x") + lax.axis_index("y")

    def body(q_ref, w_ref, o_ref):
        o_ref[...] = w_ref[...].astype(dtype)

    return pl.pallas_call(
        body, name=name,
        grid_spec=pltpu.PrefetchScalarGridSpec(
            num_scalar_prefetch=1, grid=(2, nb),
            in_specs=[pl.BlockSpec((None, br, cols), lambda h, i, q: (layer, h * nb + i, 0))],
            out_specs=pl.BlockSpec((None, None, br, cols), lambda h, i, q: (q[0], h, i, 0))),
        out_shape=pltpu.HBM((N_CHIPS, 2, half, cols), dtype),
        compiler_params=_params(16, 2),
    )(jnp.reshape(mine, (1,)).astype(jnp.int32), _in_hbm(w))


def _plain_copy(a, name):
    def body(a_ref, o_ref):
        o_ref[...] = a_ref[...]

    vmem = pl.BlockSpec(memory_space=pltpu.VMEM)
    return pl.pallas_call(body, name=name, in_specs=[vmem], out_specs=vmem,
                          out_shape=jax.ShapeDtypeStruct(a.shape, a.dtype))(a)


def _add_pair(g, recv, name):
    _, _, r, cdim = g.shape
    br = _block_rows(r, 256)
    c = lax.axis_index("c")

    def body(c_ref, g_ref, r_ref, o_ref):
        o_ref[...] = (g_ref[...] + r_ref[...]).astype(BF16)

    return pl.pallas_call(
        body, name=name,
        grid_spec=pltpu.PrefetchScalarGridSpec(
            num_scalar_prefetch=1, grid=(N_CHIPS, r // br),
            in_specs=[pl.BlockSpec((None, None, br, cdim), lambda q, i, c_ref: (q, c_ref[0], i, 0)),
                      pl.BlockSpec((None, br, cdim), lambda q, i, c_ref: (q, i, 0))],
            out_specs=pl.BlockSpec((None, br, cdim), lambda q, i, c_ref: (q, i, 0))),
        out_shape=pltpu.HBM((N_CHIPS, r, cdim), BF16),
        compiler_params=_params(16, 2),
    )(jnp.reshape(c, (1,)).astype(jnp.int32), _in_hbm(g), _in_hbm(recv))


def _add_chips(own, recv, name):
    _, r, cdim = own.shape
    br = _block_rows(r, 256)
    x, y, c = _mesh_pos()

    def body(pos_ref, own_ref, r_ref, o_ref):
        acc = own_ref[...].astype(F32)
        for k in range(3):
            acc = acc + r_ref[k].astype(F32)
        o_ref[...] = acc

    return pl.pallas_call(
        body, name=name,
        grid_spec=pltpu.PrefetchScalarGridSpec(
            num_scalar_prefetch=1, grid=(r // br,),
            in_specs=[pl.BlockSpec((None, br, cdim), lambda i, pos: (pos[0], i, 0)),
                      pl.BlockSpec((3, br, cdim), lambda i, pos: (0, i, 0))],
            out_specs=pl.BlockSpec((None, br, cdim), lambda i, pos: (pos[1], i, 0))),
        out_shape=pltpu.HBM((2, r, cdim), F32),
        compiler_params=_params(16, 1),
    )(jnp.stack([2 * x + y, c]).astype(jnp.int32), _in_hbm(own), _in_hbm(recv))


def _adam_math(w, m, v, g):
    c1 = 1.0 / (1.0 - ADAM_B1 ** ADAM_STEP)
    c2 = 1.0 / (1.0 - ADAM_B2 ** ADAM_STEP)
    m_new = ADAM_B1 * m + (1.0 - ADAM_B1) * g
    v_new = ADAM_B2 * v + (1.0 - ADAM_B2) * (g * g)
    return -ADAM_LR * ((m_new * c1) / (jnp.sqrt(v_new * c2) + ADAM_EPS) + ADAM_WD * w), m_new, v_new


SMALL_WEIGHTS = [
    ("ev_norm_g", (1, D_MODEL), ["ev_norm_g"], None), ("ev_conv_a_b", (1, A_DIM), ["ev_conv_a_b"], None),
    ("ev_ln_a_g", (1, A_DIM), ["ev_ln_a_g"], None), ("ev_ln_a_b", (1, A_DIM), ["ev_ln_a_b"], None),
    ("od_w_s", (C_GROUPS, CHUNK, CHUNK), ["od_w_s_lo", "od_w_s_hi"], None), ("od_b_s", (C_GROUPS, CHUNK), ["od_b_s"], None),
    ("mlp_norm_g", (2, D_MODEL), ["mlp_norm_g0", "mlp_norm_g1"], None), ("final_norm_g", (1, D_MODEL), ["final_norm_g"], None),
    ("ev_conv_a_w", (A_CONV_WIDTH, A_DIM // N_CHIPS), ["ev_conv_a_w"], A_DIM // N_CHIPS),
    ("ev_conv_b_w", (B_CONV_WIDTH, B_DIM // N_CHIPS), ["ev_conv_b_w"], B_DIM // N_CHIPS),
    ("od_norm_g", (1, D_MODEL // N_CHIPS), ["od_norm_g"], D_MODEL // N_CHIPS),
    ("od_b_in", (1, 2 * C_DIM // N_CHIPS), ["od_b_in"], 2 * C_DIM // N_CHIPS),
    ("od_ln_v_g", (1, C_DIM // N_CHIPS), ["od_ln_v_g"], C_DIM // N_CHIPS),
    ("od_ln_v_b", (1, C_DIM // N_CHIPS), ["od_ln_v_b"], C_DIM // N_CHIPS),
]


def _small_update(own, landed, weights):
    names = list(own.keys())
    n_g, n_w = len(names), len(SMALL_WEIGHTS)

    def body(*refs):
        refs = list(refs)
        own_refs = dict(zip(names, refs[:n_g]))
        land_refs = dict(zip(names, refs[n_g:2 * n_g]))
        wmv = [refs[2 * n_g + 3 * i:2 * n_g + 3 * i + 3] for i in range(n_w)]
        o0 = 2 * n_g + 3 * n_w
        loss_ref = refs[o0]
        outs = [refs[o0 + 1 + 4 * i:o0 + 5 + 4 * i] for i in range(n_w)]
        acc = dict(zip(names, refs[o0 + 1 + 4 * n_w:]))
        x, y, c = _mesh_pos()
        mine, chip = 4 * x + 2 * y + c, 2 * x + y

        for nm in names:
            for d in range(N_DEV):
                def add(term, nm=nm, d=d):
                    acc[nm][...] = term if d == 0 else acc[nm][...] + term
                pl.when(mine == d)(lambda nm=nm, add=add: add(own_refs[nm][...]))
                pl.when(mine != d)(lambda nm=nm, d=d, add=add: add(land_refs[nm][d]))
        loss_ref[...] = acc["loss"][...]

        def update(i, rows, g):
            w_ref, m_ref, v_ref = wmv[i]
            delta, m_new, v_new = _adam_math(w_ref[rows], m_ref[rows], v_ref[rows], g)
            for ref, val in zip(outs[i], (g, delta, m_new, v_new)):
                ref[rows] = val

        for i, (_, shape, grads, per_chip) in enumerate(SMALL_WEIGHTS):
            for row, gname in enumerate(grads):
                per_grad = shape[0] // len(grads)
                rows = slice(row * per_grad, (row + 1) * per_grad)
                if per_chip is None:
                    update(i, rows, acc[gname][...])
                else:
                    for q in range(N_CHIPS):
                        pl.when(chip == q)(lambda i=i, rows=rows, gname=gname, q=q, per_chip=per_chip:
                                           update(i, rows, acc[gname][:, q * per_chip:(q + 1) * per_chip]))

    operands = [own[nm] for nm in names] + [landed[nm] for nm in names]
    for nm, _, _, _ in SMALL_WEIGHTS:
        operands += list(weights[nm])
    out_shape = [jax.ShapeDtypeStruct((1, 1), F32)]
    for _, shape, _, _ in SMALL_WEIGHTS:
        out_shape += [jax.ShapeDtypeStruct(shape, F32)] * 4
    res = pl.pallas_call(
        body, name="small_update", grid=(1,),
        in_specs=[_full_spec(a.shape) for a in operands], out_specs=[_full_spec(s.shape) for s in out_shape],
        out_shape=out_shape, scratch_shapes=[pltpu.VMEM(own[nm].shape, F32) for nm in names],
        compiler_params=_params(32, 1),
    )(*[_in_hbm(a) for a in operands])
    return res[0], {nm: res[1 + 4 * i:5 + 4 * i] for i, (nm, _, _, _) in enumerate(SMALL_WEIGHTS)}


def _adamw(w, m, v, grads, name):
    layers, r, cdim = w.shape
    br = _block_rows(r, 256 if cdim > LANES else 1024)
    blocks = r // br

    def body(*refs):
        w_ref, m_ref, v_ref = refs[:3]
        g_refs = refs[3:3 + layers]
        go_ref, d_ref, mo_ref, vo_ref = refs[3 + layers:]
        layer = pl.program_id(0)
        for l in range(layers):
            @pl.when(layer == l)
            def _(l=l):
                g = g_refs[l][...]
                go_ref[...] = g
                d_ref[...], mo_ref[...], vo_ref[...] = _adam_math(w_ref[...], m_ref[...], v_ref[...], g)

    spec3 = pl.BlockSpec((None, br, cdim), lambda l, i: (l, i, 0))
    g_specs = [pl.BlockSpec((br, cdim), lambda l, i, own=own: (jnp.clip(i + (l - own) * blocks, 0, blocks - 1), 0))
               for own in range(layers)]
    out = jax.ShapeDtypeStruct((layers, r, cdim), F32)
    outs, _ = _pallas(body, [_in_hbm(a) for a in (w, m, v, *grads)], name=name, grid=(layers, blocks),
                      in_specs=[spec3, spec3, spec3] + g_specs, out_specs=[spec3] * 4, out_shape=[out] * 4, vmem_mib=32)
    return outs


ADAMW_ROWS = 256
ADAMW_SLOTS = 3


def _adamw_stream(weights, name):
    cdim = max(w.shape[2] for w, _, _, _ in weights)
    operands, work = [], []
    for k, (w, m, v, grads) in enumerate(weights):
        layers, r, c = w.shape
        assert r % ADAMW_ROWS == 0 and c % LANES == 0, (name, w.shape)
        base = len(operands)
        operands += [jnp.reshape(a, (layers * r, c)) for a in (w, m, v)] + list(grads)
        for l in range(layers):
            for b in range(0, r, ADAMW_ROWS):
                work.append((base, base + 3 + l, 4 * k, l * r + b, b, c))
    n_in = len(operands)

    def body(*refs):
        ins, outs = refs[:n_in], refs[n_in:n_in + 4 * len(weights)]
        buf_in, buf_out, sem_in, sem_out = refs[n_in + 4 * len(weights):]

        def reads(t):
            base, g_at, _, rows, g_rows, c = work[t]
            slot = t % ADAMW_SLOTS
            srcs = [ins[base + j].at[pl.ds(rows, ADAMW_ROWS)] for j in range(3)] + [ins[g_at].at[pl.ds(g_rows, ADAMW_ROWS)]]
            return [pltpu.make_async_copy(src, buf_in.at[slot, j, :, pl.ds(0, c)], sem_in.at[slot, j])
                    for j, src in enumerate(srcs)]

        def writes(t):
            _, _, out_at, rows, _, c = work[t]
            slot = t % ADAMW_SLOTS
            return [pltpu.make_async_copy(buf_out.at[slot, j, :, pl.ds(0, c)], outs[out_at + j].at[pl.ds(rows, ADAMW_ROWS)],
                                          sem_out.at[slot, j]) for j in range(4)]

        for t in range(min(ADAMW_SLOTS - 1, len(work))):
            for cp in reads(t):
                cp.start()
        for t in range(len(work)):
            slot = t % ADAMW_SLOTS
            if t + ADAMW_SLOTS - 1 < len(work):
                for cp in reads(t + ADAMW_SLOTS - 1):
                    cp.start()
            for cp in reads(t):
                cp.wait()
            if t >= ADAMW_SLOTS:
                for cp in writes(t - ADAMW_SLOTS):
                    cp.wait()
            c = work[t][5]
            g = buf_in[slot, 3, :, 0:c]
            buf_out[slot, 0, :, 0:c] = g
            buf_out[slot, 1, :, 0:c], buf_out[slot, 2, :, 0:c], buf_out[slot, 3, :, 0:c] = _adam_math(
                buf_in[slot, 0, :, 0:c], buf_in[slot, 1, :, 0:c], buf_in[slot, 2, :, 0:c], g)
            for cp in writes(t):
                cp.start()
        for t in range(max(0, len(work) - ADAMW_SLOTS), len(work)):
            for cp in writes(t):
                cp.wait()

    out_shape = [jax.ShapeDtypeStruct((w.shape[0] * w.shape[1], w.shape[2]), F32) for w, _, _, _ in weights for _ in range(4)]
    outs, _ = _pallas(body, [_in_hbm(a) for a in operands], name=name, grid=(1,), in_specs=[ANY] * n_in,
                      out_specs=[ANY] * len(out_shape), out_shape=out_shape,
                      scratch_shapes=[pltpu.VMEM((ADAMW_SLOTS, 4, ADAMW_ROWS, cdim), F32),
                                      pltpu.VMEM((ADAMW_SLOTS, 4, ADAMW_ROWS, cdim), F32),
                                      pltpu.SemaphoreType.DMA((ADAMW_SLOTS, 4)), pltpu.SemaphoreType.DMA((ADAMW_SLOTS, 4))],
                      vmem_mib=40)
    return [[jnp.reshape(o, w.shape) for o in outs[4 * k:4 * k + 4]] for k, (w, _, _, _) in enumerate(weights)]


def _fill_shifted(buf, rows):
    for b in range(1, SUBLANES):
        buf[b, 0:rows - SUBLANES, :] = buf[0, b:b + rows - SUBLANES, :]


def _window(buf, start, size):
    return buf[start % SUBLANES, start - start % SUBLANES:start - start % SUBLANES + size, :]


def _conv31(src, w_ref, r0, base, init):
    acc = init
    for k in range(A_CONV_WIDTH):
        acc = acc + w_ref[k:k + 1, :] * _window(src, base + k + r0, CONV_ROWS)
    return acc


def _fwd_even(x, norm_g, w_in, conv_a_w, conv_a_b, ln_g, ln_b, conv_b_w, w_out, *, tm, seq, riders=()):
    tokens = x.shape[0]
    nt, tps = tokens // tm, seq // tm

    def body(x_ref, g_ref, win_hbm, caw_ref, cab_ref, lng_ref, lnb_ref, cbw_ref, wout_hbm,
             h_ref, n_ref, z_ref, a2_ref, cv_ref, mix_ref, win_v, wout_v, pa, pb, sem):
        i = pl.program_id(0)

        _load_weights([(win_hbm, win_v, False), (wout_hbm, wout_v, True)], sem)

        xv = x_ref[...]
        nf, _ = _rms_fwd(xv, g_ref[...])
        n = nf.astype(BF16)
        n_ref[...] = n
        z = jnp.concatenate([_dot(n, win_v[j]) for j in range(N_CHIPS)], axis=1)
        z_ref[...] = z.astype(BF16)
        a_val, a_gate = z[:, 0:A_DIM], z[:, A_DIM:2 * A_DIM]
        b_gate, c_gate, b_val = z[:, 1024:1536], z[:, 1536:2048], z[:, 2048:2560]

        first = (i % tps) == 0

        @pl.when(first)
        def _():
            pa[0, 0:A_HALO, :] = jnp.zeros((A_HALO, A_DIM), F32)
            pb[0:B_HALO, :] = jnp.zeros((B_HALO, B_DIM), F32)

        @pl.when(jnp.logical_not(first))
        def _():
            pa[0, 0:A_HALO, :] = pa[0, tm:tm + A_HALO, :]
            pb[0:B_HALO, :] = pb[tm:tm + B_HALO, :]

        pa[0, A_HALO:A_HALO + tm, :] = a_val * jax.nn.sigmoid(a_gate)
        pb[B_HALO:B_HALO + tm, :] = c_gate * b_val
        _fill_shifted(pa, A_HALO + tm)
        bias = jnp.broadcast_to(cab_ref[...], (CONV_ROWS, A_DIM))
        for r0 in range(0, tm, CONV_ROWS):
            a2_ref[r0:r0 + CONV_ROWS, :] = _conv31(pa, caw_ref, r0, A_HALO - (A_CONV_WIDTH - 1), bias)
        xhat, _ = _ln_stats(a2_ref[...])
        a3 = xhat * lng_ref[...] + lnb_ref[...]
        a4 = a3 * jax.nn.sigmoid(a3)
        cv = cbw_ref[0:1, :] * pb[B_HALO - 2:B_HALO - 2 + tm, :]
        cv = cv + cbw_ref[1:2, :] * pb[B_HALO - 1:B_HALO - 1 + tm, :]
        cv = cv + cbw_ref[2:3, :] * pb[B_HALO:B_HALO + tm, :]
        cv_ref[...] = cv.astype(BF16)
        mix = jnp.concatenate([a4, b_gate * cv], axis=1).astype(BF16)
        mix_ref[...] = mix
        h_ref[...] = xv + _dot(mix, wout_v[...])

    shp = lambda cols, dt: jax.ShapeDtypeStruct((tokens, cols), dt)
    return _pallas(
        body, [x, norm_g, w_in, conv_a_w, conv_a_b, ln_g, ln_b, conv_b_w, w_out], name="fwd_even", grid=(nt,),
        in_specs=[_row_spec(tm, D_MODEL), _full_spec((1, D_MODEL)), ANY, _full_spec((A_CONV_WIDTH, A_DIM)),
                  _full_spec((1, A_DIM)), _full_spec((1, A_DIM)), _full_spec((1, A_DIM)),
                  _full_spec((B_CONV_WIDTH, B_DIM)), ANY],
        out_specs=[_row_spec(tm, D_MODEL), _row_spec(tm, D_MODEL), _row_spec(tm, IN_EVEN), _row_spec(tm, A_DIM),
                   _row_spec(tm, B_DIM), _row_spec(tm, D_MODEL)],
        out_shape=[shp(D_MODEL, F32), shp(D_MODEL, BF16), shp(IN_EVEN, BF16), shp(A_DIM, F32), shp(B_DIM, BF16),
                   shp(D_MODEL, BF16)],
        scratch_shapes=[pltpu.VMEM((N_CHIPS, D_MODEL, IN_EVEN // N_CHIPS), BF16), pltpu.VMEM((D_MODEL, D_MODEL), BF16),
                        pltpu.VMEM((SUBLANES, A_HALO + tm, A_DIM), F32), pltpu.VMEM((B_HALO + tm, B_DIM), F32),
                        pltpu.SemaphoreType.DMA((N_LOADS,))],
        vmem_mib=56, riders=riders)


def _loss_tail(xv, g, target, loss_ref, dh_ref, dhb_ref, dg_ref):
    @pl.when(pl.program_id(0) == 0)
    def _():
        loss_ref[...] = jnp.zeros((1, 1), F32)
        dg_ref[...] = jnp.zeros((1, D_MODEL), F32)

    out, rstd = _rms_fwd(xv, g)
    err = out - target
    per_token = jnp.sum(err * err, axis=1, keepdims=True) * (1.0 / D_MODEL)
    loss_ref[...] += 0.5 * jnp.sum(per_token, axis=0, keepdims=True)
    dx, dg = _rms_bwd(err * (1.0 / D_MODEL), xv, rstd, g)
    dh_ref[...] = dx
    dhb_ref[...] = dx.astype(BF16)
    dg_ref[...] += dg


def _fwd_mlp(h, norm_g, w1, w2, layer, *, tm, riders=(), head=None):
    tokens = h.shape[0]
    nt = tokens // tm
    fs = D_FF // N_CHIPS
    n_in = 4 if head is None else 6

    def body(*refs):
        h_ref, g_ref, w1_hbm, w2_hbm = refs[:4]
        w1_v, w2_v, sem = refs[-3:]
        outs = refs[n_in:-3]
        n_ref, p_ref, q_ref = outs[1:4] if head is None else outs[0:3]
        _load_weights([(w1_hbm, w1_v, False), (w2_hbm, w2_v, False)], sem)

        xv = h_ref[...]
        nf, _ = _rms_fwd(xv, g_ref[...])
        n = nf.astype(BF16)
        n_ref[...] = n
        acc = xv
        for j in range(N_CHIPS):
            p = _dot(n, w1_v[j])
            p_ref[:, j * fs:(j + 1) * fs] = p.astype(BF16)
            r = jnp.maximum(p, 0.0)
            q = (r * r).astype(BF16)
            q_ref[:, j * fs:(j + 1) * fs] = q
            acc = acc + _dot(q, w2_v[j])
        if head is None:
            outs[0][...] = acc
        else:
            _loss_tail(acc, refs[4][...], refs[5][...], *outs[3:7])

    shp = lambda cols, dt: jax.ShapeDtypeStruct((tokens, cols), dt)
    saved_specs = [_row_spec(tm, D_MODEL), _row_spec(tm, D_FF), _row_spec(tm, D_FF)]
    saved_shapes = [shp(D_MODEL, BF16), shp(D_FF, BF16), shp(D_FF, BF16)]
    if head is None:
        operands, in_specs = [h, norm_g, w1, w2], [_row_spec(tm, D_MODEL), _full_spec((1, D_MODEL)), ANY, ANY]
        out_specs, out_shape = [_row_spec(tm, D_MODEL)] + saved_specs, [shp(D_MODEL, F32)] + saved_shapes
    else:
        operands = [h, norm_g, w1, w2, *head]
        in_specs = [_row_spec(tm, D_MODEL), _full_spec((1, D_MODEL)), ANY, ANY, _full_spec((1, D_MODEL)), _row_spec(tm, D_MODEL)]
        out_specs = saved_specs + [_full_spec((1, 1)), _row_spec(tm, D_MODEL), _row_spec(tm, D_MODEL), _full_spec((1, D_MODEL))]
        out_shape = saved_shapes + [jax.ShapeDtypeStruct((1, 1), F32), shp(D_MODEL, F32), shp(D_MODEL, BF16),
                                    jax.ShapeDtypeStruct((1, D_MODEL), F32)]
    return _pallas(
        body, operands, name=f"fwd_mlp{layer}", grid=(nt,), in_specs=in_specs, out_specs=out_specs, out_shape=out_shape,
        scratch_shapes=[pltpu.VMEM((N_CHIPS, D_MODEL, fs), BF16), pltpu.VMEM((N_CHIPS, fs, D_MODEL), BF16),
                        pltpu.SemaphoreType.DMA((N_LOADS,))],
        vmem_mib=56, riders=riders)


def _tril_mask():
    row = lax.broadcasted_iota(jnp.int32, (CHUNK, CHUNK), 0)
    col = lax.broadcasted_iota(jnp.int32, (CHUNK, CHUNK), 1)
    return row >= col


def _triu_mask():
    row = lax.broadcasted_iota(jnp.int32, (CHUNK, CHUNK), 0)
    col = lax.broadcasted_iota(jnp.int32, (CHUNK, CHUNK), 1)
    return row <= col


def _fwd_odd(h, norm_g, w_in, b_in, ln_g, ln_b, w_s, b_s_rows, w_out, *, tm, riders=()):
    tokens = h.shape[0]
    nt = tokens // tm
    cs = 2 * C_DIM // N_CHIPS

    def body(h_ref, g_ref, win_hbm, bin_ref, lng_ref, lnb_ref, ws_ref, bs_ref, wout_hbm,
             ho_ref, n_ref, s_ref, cdf_ref, sv_ref, y_ref, win_v, wout_v, bd, sem):
        _load_weights([(win_hbm, win_v, False), (wout_hbm, wout_v, True)], sem)

        @pl.when(pl.program_id(0) == 0)
        def _():
            mask = _tril_mask()
            bd[...] = jnp.zeros(bd.shape, BF16)
            for g in range(C_GROUPS):
                w = jnp.where(mask, ws_ref[g], 0.0).astype(BF16)
                bd[g, 0:CHUNK, 0:CHUNK] = w
                bd[g, CHUNK:PAIR, CHUNK:PAIR] = w

        xv = h_ref[...]
        nf, _ = _rms_fwd(xv, g_ref[...])
        n = nf.astype(BF16)
        n_ref[...] = n
        s = jnp.concatenate([_dot(n, win_v[j]) for j in range(N_CHIPS)], axis=1) + bin_ref[...]
        s_ref[...] = s.astype(BF16)
        cdf = _gelu_cdf(s)
        cdf_ref[...] = cdf.astype(BF16)
        zz = s * cdf
        u, v = zz[:, 0:C_DIM], zz[:, C_DIM:2 * C_DIM]
        xhat, _ = _ln_stats(v)
        vn = (xhat * lng_ref[...] + lnb_ref[...]).astype(BF16)
        for g in range(C_GROUPS):
            cols = slice(g * CHUNK, (g + 1) * CHUNK)
            bias = jnp.concatenate([bs_ref[g], bs_ref[g]], axis=0)
            for r0 in range(0, tm, PAIR):
                sv = _dot(bd[g], vn[r0:r0 + PAIR, cols]) + bias
                sv_ref[r0:r0 + PAIR, cols] = sv.astype(BF16)
                y_ref[r0:r0 + PAIR, cols] = (u[r0:r0 + PAIR, cols] * sv).astype(BF16)
        ho_ref[...] = xv + _dot(y_ref[...], wout_v[...])

    shp = lambda cols, dt: jax.ShapeDtypeStruct((tokens, cols), dt)
    return _pallas(
        body, [h, norm_g, w_in, b_in, ln_g, ln_b, w_s, b_s_rows, w_out], name="fwd_odd", grid=(nt,),
        in_specs=[_row_spec(tm, D_MODEL), _full_spec((1, D_MODEL)), ANY, _full_spec((1, 2 * C_DIM)),
                  _full_spec((1, C_DIM)), _full_spec((1, C_DIM)), _full_spec((C_GROUPS, CHUNK, CHUNK)),
                  _full_spec((C_GROUPS, CHUNK, CHUNK)), ANY],
        out_specs=[_row_spec(tm, D_MODEL), _row_spec(tm, D_MODEL), _row_spec(tm, 2 * C_DIM), _row_spec(tm, 2 * C_DIM),
                   _row_spec(tm, C_DIM), _row_spec(tm, C_DIM)],
        out_shape=[shp(D_MODEL, F32), shp(D_MODEL, BF16), shp(2 * C_DIM, BF16), shp(2 * C_DIM, BF16), shp(C_DIM, BF16),
                   shp(C_DIM, BF16)],
        scratch_shapes=[pltpu.VMEM((N_CHIPS, D_MODEL, cs), BF16), pltpu.VMEM((C_DIM, D_MODEL), BF16),
                        pltpu.VMEM((C_GROUPS, PAIR, PAIR), BF16), pltpu.SemaphoreType.DMA((N_LOADS,))],
        vmem_mib=56, riders=riders)


def _bwd_mlp(dh, h, norm_g, p, w1, w2, layer, *, tm, riders=()):
    tokens = h.shape[0]
    nt = tokens // tm
    fs = D_FF // N_CHIPS

    def body(dh_ref, h_ref, g_ref, p_ref, w1_hbm, w2_hbm, dx_ref, dxb_ref, dp_ref, dg_ref, w1_v, w2_v, sem):
        @pl.when(pl.program_id(0) == 0)
        def _():
            dg_ref[...] = jnp.zeros((1, D_MODEL), F32)

        _load_weights([(w1_hbm, w1_v, False), (w2_hbm, w2_v, False)], sem)

        dhv = dh_ref[...]
        dhb = dhv.astype(BF16)
        dn = jnp.zeros((tm, D_MODEL), F32)
        for j in range(N_CHIPS):
            dq = _dot_nt(dhb, w2_v[j])
            r = jnp.maximum(p_ref[:, j * fs:(j + 1) * fs].astype(F32), 0.0)
            dp = ((2.0 * r) * dq).astype(BF16)
            dp_ref[:, j * fs:(j + 1) * fs] = dp
            dn = dn + _dot_nt(dp, w1_v[j])
        xv = h_ref[...]
        g = g_ref[...]
        _, rstd = _rms_fwd(xv, g)
        dx, dg = _rms_bwd(dn, xv, rstd, g)
        dx_ref[...] = dhv + dx
        dxb_ref[...] = (dhv + dx).astype(BF16)
        dg_ref[...] += dg

    return _pallas(
        body, [dh, h, norm_g, p, w1, w2], name=f"bwd_mlp{layer}", grid=(nt,),
        in_specs=[_row_spec(tm, D_MODEL), _row_spec(tm, D_MODEL), _full_spec((1, D_MODEL)), _row_spec(tm, D_FF), ANY, ANY],
        out_specs=[_row_spec(tm, D_MODEL), _row_spec(tm, D_MODEL), _row_spec(tm, D_FF), _full_spec((1, D_MODEL))],
        out_shape=[jax.ShapeDtypeStruct((tokens, D_MODEL), F32), jax.ShapeDtypeStruct((tokens, D_MODEL), BF16),
                   jax.ShapeDtypeStruct((tokens, D_FF), BF16), jax.ShapeDtypeStruct((1, D_MODEL), F32)],
        scratch_shapes=[pltpu.VMEM((N_CHIPS, D_MODEL, fs), BF16), pltpu.VMEM((N_CHIPS, fs, D_MODEL), BF16),
                        pltpu.SemaphoreType.DMA((N_LOADS,))],
        vmem_mib=56, riders=riders)


def _bwd_odd(dh, h, norm_g, s, cdf, sv, w_in, ln_g, ln_b, w_s, w_out, *, tm, riders=()):
    tokens = h.shape[0]
    nt = tokens // tm
    cs = 2 * C_DIM // N_CHIPS

    def body(dh_ref, h_ref, g_ref, s_ref, cdf_ref, sv_ref, win_hbm, lng_ref, lnb_ref, ws_ref, wout_hbm,
             dx_ref, dxb_ref, ds_ref, dg_ref, dbin_ref, dlng_ref, dlnb_ref, dws_ref, dbs_ref,
             win_v, wout_v, bdt, dws_acc, dbs_acc, dvn, sem):
        i = pl.program_id(0)

        _load_weights([(win_hbm, win_v, False), (wout_hbm, wout_v, True)], sem)

        @pl.when(i == 0)
        def _():
            mask_t = _triu_mask()
            bdt[...] = jnp.zeros(bdt.shape, BF16)
            for g in range(C_GROUPS):
                wt = jnp.where(mask_t, ws_ref[g].T, 0.0).astype(BF16)
                bdt[g, 0:CHUNK, 0:CHUNK] = wt
                bdt[g, CHUNK:PAIR, CHUNK:PAIR] = wt
            dws_acc[...] = jnp.zeros(dws_acc.shape, F32)
            dbs_acc[...] = jnp.zeros(dbs_acc.shape, F32)
            dg_ref[...] = jnp.zeros(dg_ref.shape, F32)
            dbin_ref[...] = jnp.zeros(dbin_ref.shape, F32)
            dlng_ref[...] = jnp.zeros(dlng_ref.shape, F32)
            dlnb_ref[...] = jnp.zeros(dlnb_ref.shape, F32)

        dhv = dh_ref[...]
        dy = _dot_nt(dhv.astype(BF16), wout_v[...])
        sf = s_ref[...].astype(F32)
        cdf = cdf_ref[...].astype(F32)
        pdf = jnp.exp(-0.5 * sf * sf) * 0.3989422804014327
        zz = sf * cdf
        dgelu = cdf + sf * pdf
        u, v = zz[:, 0:C_DIM], zz[:, C_DIM:2 * C_DIM]
        xhat, rs = _ln_stats(v)
        lng = lng_ref[...]
        vn = (xhat * lng + lnb_ref[...]).astype(BF16)
        du = dy * sv_ref[...].astype(F32)
        dsv = dy * u
        dsvb = dsv.astype(BF16)
        for g in range(C_GROUPS):
            cols = slice(g * CHUNK, (g + 1) * CHUNK)
            for r0 in range(0, tm, PAIR):
                blk = dsvb[r0:r0 + PAIR, cols]
                dvn[r0:r0 + PAIR, cols] = _dot(bdt[g], blk)
                dws_acc[g] += _dot_nt(blk, vn[r0:r0 + PAIR, cols])
                dbs_acc[g] += dsv[r0:r0 + CHUNK, cols] + dsv[r0 + CHUNK:r0 + PAIR, cols]
        dv, dlng, dlnb = _ln_bwd(dvn[...], xhat, rs, lng)
        dlng_ref[...] += dlng
        dlnb_ref[...] += dlnb
        ds = jnp.concatenate([du, dv], axis=1) * dgelu
        dbin_ref[...] += jnp.sum(ds, axis=0, keepdims=True)
        dsb = ds.astype(BF16)
        ds_ref[...] = dsb
        dn = jnp.zeros((tm, D_MODEL), F32)
        for j in range(N_CHIPS):
            dn = dn + _dot_nt(dsb[:, j * cs:(j + 1) * cs], win_v[j])
        xv = h_ref[...]
        g = g_ref[...]
        _, rstd = _rms_fwd(xv, g)
        dx, dg = _rms_bwd(dn, xv, rstd, g)
        dx_ref[...] = dhv + dx
        dxb_ref[...] = (dhv + dx).astype(BF16)
        dg_ref[...] += dg

        @pl.when(i == nt - 1)
        def _():
            mask = _tril_mask()
            for g in range(C_GROUPS):
                full = dws_acc[g]
                dws_ref[g] = jnp.where(mask, full[0:CHUNK, 0:CHUNK] + full[CHUNK:PAIR, CHUNK:PAIR], 0.0)
                dbs_ref[g:g + 1, :] = jnp.sum(dbs_acc[g].T, axis=0, keepdims=True)

    row = lambda cols: jax.ShapeDtypeStruct((1, cols), F32)
    return _pallas(
        body, [dh, h, norm_g, s, cdf, sv, w_in, ln_g, ln_b, w_s, w_out], name="bwd_odd", grid=(nt,),
        in_specs=[_row_spec(tm, D_MODEL), _row_spec(tm, D_MODEL), _full_spec((1, D_MODEL)), _row_spec(tm, 2 * C_DIM),
                  _row_spec(tm, 2 * C_DIM), _row_spec(tm, C_DIM), ANY, _full_spec((1, C_DIM)), _full_spec((1, C_DIM)),
                  _full_spec((C_GROUPS, CHUNK, CHUNK)), ANY],
        out_specs=[_row_spec(tm, D_MODEL), _row_spec(tm, D_MODEL), _row_spec(tm, 2 * C_DIM), _full_spec((1, D_MODEL)),
                   _full_spec((1, 2 * C_DIM)),
                   _full_spec((1, C_DIM)), _full_spec((1, C_DIM)), _full_spec((C_GROUPS, CHUNK, CHUNK)),
                   _full_spec((C_GROUPS, CHUNK))],
        out_shape=[jax.ShapeDtypeStruct((tokens, D_MODEL), F32), jax.ShapeDtypeStruct((tokens, D_MODEL), BF16),
                   jax.ShapeDtypeStruct((tokens, 2 * C_DIM), BF16),
                   row(D_MODEL), row(2 * C_DIM), row(C_DIM), row(C_DIM),
                   jax.ShapeDtypeStruct((C_GROUPS, CHUNK, CHUNK), F32), jax.ShapeDtypeStruct((C_GROUPS, CHUNK), F32)],
        scratch_shapes=[pltpu.VMEM((N_CHIPS, D_MODEL, cs), BF16), pltpu.VMEM((C_DIM, D_MODEL), BF16),
                        pltpu.VMEM((C_GROUPS, PAIR, PAIR), BF16), pltpu.VMEM((C_GROUPS, PAIR, PAIR), F32),
                        pltpu.VMEM((C_GROUPS, CHUNK, CHUNK), F32), pltpu.VMEM((tm, C_DIM), F32),
                        pltpu.SemaphoreType.DMA((N_LOADS,))],
        vmem_mib=56, riders=riders)


def _bwd_even(dh, x, norm_g, z, a2, cv, w_in, conv_a_w, ln_g, ln_b, conv_b_w, w_out, *, tm, seq, riders=()):
    tokens = x.shape[0]
    nt, tps = tokens // tm, seq // tm
    ws = IN_EVEN // N_CHIPS

    def body(dh_ref, x_ref, g_ref, z_ref, a2_ref, cv_ref, win_hbm, caw_ref, lng_ref, lnb_ref, cbw_ref, wout_hbm,
             dx_ref, dz_ref, dg_ref, dcaw_ref, dcab_ref, dlng_ref, dlnb_ref, dcbw_ref,
             win_v, wout_v, ea, eb, a1s, da1s, sigs, wide, dw_acc, sem):
        i = pl.program_id(0)

        _load_weights([(win_hbm, win_v, False), (wout_hbm, wout_v, True)], sem)

        @pl.when(i == 0)
        def _():
            dw_acc[...] = jnp.zeros(dw_acc.shape, F32)
            for ref in (dg_ref, dcab_ref, dlng_ref, dlnb_ref, dcbw_ref):
                ref[...] = jnp.zeros(ref.shape, F32)

        last = ((nt - 1 - i) % tps) == tps - 1

        @pl.when(last)
        def _():
            ea[0, tm:tm + A_HALO, :] = jnp.zeros((A_HALO, A_DIM), F32)
            eb[tm:tm + B_HALO, :] = jnp.zeros((B_HALO, B_DIM), F32)

        @pl.when(jnp.logical_not(last))
        def _():
            ea[0, tm:tm + A_HALO, :] = ea[0, 0:A_HALO, :]
            eb[tm:tm + B_HALO, :] = eb[0:B_HALO, :]

        wide[...] = _dot_nt(dh_ref[...].astype(BF16), wout_v[...])
        lng, lnb = lng_ref[...], lnb_ref[...]
        zero_row = jnp.zeros((1, A_DIM), F32)
        dlng, dlnb, dcab = zero_row, zero_row, zero_row
        for r0 in range(0, tm, ELEM_ROWS):
            rows = slice(r0, r0 + ELEM_ROWS)
            a_val, a_gate = z_ref[rows, 0:A_DIM].astype(F32), z_ref[rows, A_DIM:2 * A_DIM].astype(F32)
            xhat, rs = _ln_stats(a2_ref[rows, :])
            a3 = xhat * lng + lnb
            sg = jax.nn.sigmoid(a3)
            da3 = wide[rows, 0:A_DIM] * (sg * (1.0 + a3 * (1.0 - sg)))
            da2, g_part, b_part = _ln_bwd(da3, xhat, rs, lng)
            dlng, dlnb, dcab = dlng + g_part, dlnb + b_part, dcab + jnp.sum(da2, axis=0, keepdims=True)
            ea[0, rows, :] = da2
            eb[rows, :] = wide[rows, A_DIM:A_DIM + B_DIM] * z_ref[rows, 1024:1536].astype(F32)
            sig = jax.nn.sigmoid(a_gate)
            sigs[rows, :] = sig
            a1s[rows, :] = a_val * sig
        dlng_ref[...] += dlng
        dlnb_ref[...] += dlnb
        dcab_ref[...] += dcab
        _fill_shifted(ea, tm + A_HALO)
        for r0 in range(0, tm, CONV_ROWS):
            acc = jnp.zeros((CONV_ROWS, A_DIM), F32)
            for j in range(A_CONV_WIDTH):
                acc = acc + caw_ref[A_CONV_WIDTH - 1 - j:A_CONV_WIDTH - j, :] * _window(ea, r0 + j, CONV_ROWS)
            da1s[r0:r0 + CONV_ROWS, :] = acc
        for j0 in range(0, A_CONV_WIDTH, DW_TAPS):
            taps = range(j0, min(j0 + DW_TAPS, A_CONV_WIDTH))
            part = [jnp.zeros((CONV_ROWS, A_DIM), F32) for _ in taps]
            for r0 in range(0, tm, CONV_ROWS):
                a1c = a1s[r0:r0 + CONV_ROWS, :]
                for u, j in enumerate(taps):
                    part[u] = part[u] + _window(ea, r0 + j, CONV_ROWS) * a1c
            for u, j in enumerate(taps):
                dw_acc[A_CONV_WIDTH - 1 - j] += part[u]
        dcbw = [jnp.zeros((1, B_DIM), F32) for _ in range(B_CONV_WIDTH)]
        for r0 in range(0, tm, ELEM_ROWS):
            rows = slice(r0, r0 + ELEM_ROWS)
            da1, sig = da1s[rows, :], sigs[rows, :]
            dz_ref[rows, 0:A_DIM] = (da1 * sig).astype(BF16)
            dz_ref[rows, A_DIM:2 * A_DIM] = (da1 * z_ref[rows, 0:A_DIM].astype(F32) * (sig * (1.0 - sig))).astype(BF16)
            c_gate, b_val = z_ref[rows, 1536:2048].astype(F32), z_ref[rows, 2048:2560].astype(F32)
            dz_ref[rows, 1024:1536] = (wide[rows, A_DIM:A_DIM + B_DIM] * cv_ref[rows, :].astype(F32)).astype(BF16)
            cb = c_gate * b_val
            dcb = jnp.zeros((ELEM_ROWS, B_DIM), F32)
            for j in range(B_CONV_WIDTH):
                k = B_CONV_WIDTH - 1 - j
                sl = eb[r0 + j:r0 + j + ELEM_ROWS, :]
                dcb = dcb + cbw_ref[k:k + 1, :] * sl
                dcbw[k] = dcbw[k] + jnp.sum(sl * cb, axis=0, keepdims=True)
            dz_ref[rows, 1536:2048] = (dcb * b_val).astype(BF16)
            dz_ref[rows, 2048:2560] = (dcb * c_gate).astype(BF16)
        for k in range(B_CONV_WIDTH):
            dcbw_ref[k:k + 1, :] += dcbw[k]
        dn = jnp.zeros((tm, D_MODEL), F32)
        for j in range(N_CHIPS):
            dn = dn + _dot_nt(dz_ref[:, j * ws:(j + 1) * ws], win_v[j])
        wide[...] = dn
        g = g_ref[...]
        dg = jnp.zeros((1, D_MODEL), F32)
        for r0 in range(0, tm, ELEM_ROWS):
            rows = slice(r0, r0 + ELEM_ROWS)
            xv = x_ref[rows, :]
            _, rstd = _rms_fwd(xv, g)
            dx, dg_part = _rms_bwd(wide[rows, :], xv, rstd, g)
            dx_ref[rows, :] = dh_ref[rows, :] + dx
            dg = dg + dg_part
        dg_ref[...] += dg

        @pl.when(i == nt - 1)
        def _():
            for k in range(A_CONV_WIDTH):
                dcaw_ref[k:k + 1, :] = jnp.sum(dw_acc[k], axis=0, keepdims=True)

    row = lambda cols: jax.ShapeDtypeStruct((1, cols), F32)
    rs_ = functools.partial(_row_spec, rev_nt=nt)
    return _pallas(
        body, [dh, x, norm_g, z, a2, cv, w_in, conv_a_w, ln_g, ln_b, conv_b_w, w_out], name="bwd_even", grid=(nt,),
        in_specs=[rs_(tm, D_MODEL), rs_(tm, D_MODEL), _full_spec((1, D_MODEL)), rs_(tm, IN_EVEN), rs_(tm, A_DIM),
                  rs_(tm, B_DIM), ANY, _full_spec((A_CONV_WIDTH, A_DIM)), _full_spec((1, A_DIM)), _full_spec((1, A_DIM)),
                  _full_spec((B_CONV_WIDTH, B_DIM)), ANY],
        out_specs=[rs_(tm, D_MODEL), rs_(tm, IN_EVEN), _full_spec((1, D_MODEL)), _full_spec((A_CONV_WIDTH, A_DIM)),
                   _full_spec((1, A_DIM)), _full_spec((1, A_DIM)), _full_spec((1, A_DIM)), _full_spec((B_CONV_WIDTH, B_DIM))],
        out_shape=[jax.ShapeDtypeStruct((tokens, D_MODEL), F32), jax.ShapeDtypeStruct((tokens, IN_EVEN), BF16),
                   row(D_MODEL), jax.ShapeDtypeStruct((A_CONV_WIDTH, A_DIM), F32), row(A_DIM), row(A_DIM), row(A_DIM),
                   jax.ShapeDtypeStruct((B_CONV_WIDTH, B_DIM), F32)],
        scratch_shapes=[pltpu.VMEM((N_CHIPS, D_MODEL, ws), BF16), pltpu.VMEM((D_MODEL, D_MODEL), BF16),
                        pltpu.VMEM((SUBLANES, tm + A_HALO, A_DIM), F32), pltpu.VMEM((tm + B_HALO, B_DIM), F32),
                        pltpu.VMEM((tm, A_DIM), F32), pltpu.VMEM((tm, A_DIM), F32), pltpu.VMEM((tm, A_DIM), F32),
                        pltpu.VMEM((tm, D_MODEL), F32),
                        pltpu.VMEM((A_CONV_WIDTH, CONV_ROWS, A_DIM), F32), pltpu.SemaphoreType.DMA((N_LOADS,))],
        vmem_mib=56, riders=riders)


def _wgrad(a, b, name, *, col_shards, riders=()):
    tokens, m = a.shape
    n = b.shape[1]
    kc = 512
    if col_shards:
        bm, bn = m // 2, n // N_CHIPS
        grid = (2, N_CHIPS)
        out_spec = pl.BlockSpec((None, None, bm, bn), lambda i, j: (j, i, 0, 0))
    elif m // 8 >= MXU_ROWS:
        bm, bn = m // 8, n
        grid = (8, 1)
        out_spec = pl.BlockSpec((None, None, bm, bn), lambda i, j: (i // 2, i % 2, 0, 0))
    else:
        bm, bn = m // N_CHIPS, n
        grid = (N_CHIPS, 1)
        out_spec = pl.BlockSpec((None, 2, bm // 2, bn), lambda i, j: (i, 0, 0, 0))

    def body(a_ref, b_ref, o_ref):
        acc = jnp.zeros((bm, bn), F32)
        for k0 in range(0, tokens, kc):
            acc = acc + _dot_tn(a_ref[k0:k0 + kc, :].astype(BF16), b_ref[k0:k0 + kc, :].astype(BF16))
        if len(o_ref.shape) == 3:
            o_ref[0] = acc[0:bm // 2]
            o_ref[1] = acc[bm // 2:bm]
        else:
            o_ref[...] = acc

    out_rows = m // 2 if col_shards else m // 8
    outs, routs = _pallas(
        body, [a, b], name=name, grid=grid,
        in_specs=[pl.BlockSpec((tokens, bm), lambda i, j: (0, i)), pl.BlockSpec((tokens, bn), lambda i, j: (0, j))],
        out_specs=[out_spec], out_shape=[jax.ShapeDtypeStruct((N_CHIPS, 2, out_rows, bn), F32)],
        vmem_mib=56, riders=riders)
    return outs[0], routs


def _wgrad_pair(a, b, name, *, col_shards, riders=()):
    tokens, m = a.shape
    n = b.shape[1]
    kc = 512
    c0 = lax.axis_index("c")

    def half(ph, pre):
        return (ph + 1 + pre[0]) % 2

    if col_shards:
        bm, bn = m // 2, n // N_CHIPS
        a_spec = pl.BlockSpec((tokens, bm), lambda ph, q, pre: (0, half(ph, pre)))
        b_spec = pl.BlockSpec((tokens, bn), lambda ph, q, pre: (0, q))
    else:
        bm, bn = m // 8, n
        a_spec = pl.BlockSpec((tokens, bm), lambda ph, q, pre: (0, 2 * q + half(ph, pre)))
        b_spec = pl.BlockSpec((tokens, bn), lambda ph, q, pre: (0, 0))

    def body(pre_ref, a_ref, b_ref, o_ref, give, got, send_sems, recv_sems):
        ph, q = pl.program_id(0), pl.program_id(1)
        acc = jnp.zeros((bm, bn), F32)
        for k0 in range(0, tokens, kc):
            acc = acc + _dot_tn(a_ref[k0:k0 + kc, :].astype(BF16), b_ref[k0:k0 + kc, :].astype(BF16))
        x, y, cc = _mesh_pos()

        def tile(t):
            return _remote(give.at[t], got.at[t], send_sems.at[t], recv_sems.at[t], (x, y, 1 - cc))

        @pl.when(ph == 0)
        def _():
            give[q] = acc
            tile(q).start()

        @pl.when(ph == 1)
        def _():
            tile(q).wait_recv()
            o_ref[...] = (acc + got[q]).astype(BF16)

        @pl.when((ph == 1) & (q == N_CHIPS - 1))
        def _():
            for t in range(N_CHIPS):
                tile(t).wait_send()

    outs, routs = _pallas(
        body, [a, b], name=name, grid=(2, N_CHIPS), in_specs=[a_spec, b_spec],
        out_specs=[pl.BlockSpec((None, bm, bn), lambda ph, q, pre: (ph * q, 0, 0))],
        out_shape=[jax.ShapeDtypeStruct((N_CHIPS, bm, bn), BF16)],
        scratch_shapes=[pltpu.VMEM((N_CHIPS, bm, bn), F32), pltpu.VMEM((N_CHIPS, bm, bn), F32),
                        pltpu.SemaphoreType.DMA((N_CHIPS,)), pltpu.SemaphoreType.DMA((N_CHIPS,))],
        vmem_mib=56, riders=riders, prefetch=jnp.reshape(c0, (1,)).astype(jnp.int32))
    return outs[0], routs


class _GradReduce:
    def __init__(self, name, grad=None, chip_sum=None):
        self.name, self.grad, self.chip_sum = name, grad, chip_sum
        self.full = None

    def pair_swap(self):
        return _PairSwap([self.grad])

    def took_pair(self, outs):
        self.chip_sum = _in_hbm(_add_pair(self.grad, outs[0], f"pair_sum_{self.name}"))

    def took_chips(self, outs):
        self.full = _in_hbm(_add_chips(self.chip_sum, outs[0], f"chip_sum_{self.name}"))

    def chips_beside(self, collective_id):
        self.took_chips([_chip_swap_beside(self.chip_sum, f"chip_swap_{self.name}", collective_id)])

    def pair_share(self):
        return _PairShare([self.full])

    def took_share(self, outs):
        self.full = outs[0]

    def reduced(self):
        return jnp.reshape(self.full, (2 * self.full.shape[1], self.full.shape[2]))


def _forward_backward(x2, tgt2, w, conv_a_w, conv_b_w, od_norm, od_bias, od_lng, od_lnb,
                      ev_norm_g, ev_conv_a_b, ev_ln_a_g, ev_ln_a_b, od_w_s, od_b_s, mlp_norm_g, final_norm_g,
                      *, tm, seq, distributed=True):
    d = x2.shape[1]
    b_s_rows = jnp.broadcast_to(od_b_s[0][:, :, None], (C_GROUPS, CHUNK, CHUNK))
    (h1, n0, z, a2, cv, mix), _ = _fwd_even(
        x2, ev_norm_g, w["ev_in"], conv_a_w, ev_conv_a_b, ev_ln_a_g, ev_ln_a_b, conv_b_w, w["ev_out"], tm=tm, seq=seq)
    (h2, n1, p0, q0), _ = _fwd_mlp(h1, mlp_norm_g[0:1], w["w1_0"], w["w2_0"], 0, tm=tm)
    (h3, n2, s, cdf, sv, y), _ = _fwd_odd(h2, od_norm, w["od_in"], od_bias, od_lng, od_lnb, od_w_s[0], b_s_rows,
                                          w["od_out"], tm=tm)
    (n3, p1, q1, loss_part, dh4, dh4b, d_final_g), _ = _fwd_mlp(
        h3, mlp_norm_g[1:2], w["w1_1"], w["w2_1"], 1, tm=tm,
        head=(jnp.reshape(final_norm_g, (1, d)), tgt2))

    red = {}

    def swap(*names):
        return [red[nm].pair_swap() for nm in names] if distributed else []

    def share(*names):
        return [red[nm].pair_share() for nm in names] if distributed else []

    def took(routs, *steps):
        if distributed:
            for (nm, what), outs in zip(steps, routs):
                getattr(red[nm], what)(outs)

    swap_ids = iter(range(FIRST_SWAP_ID, FIRST_SWAP_ID + 8))

    def beside(name):
        if distributed:
            red[name].chips_beside(next(swap_ids))

    def big(lhs, rhs, name, col_shards, riders=()):
        if distributed:
            chip_sum, routs = _wgrad_pair(lhs, rhs, f"wgrad_{name}", col_shards=col_shards, riders=riders)
            red[name] = _GradReduce(name, chip_sum=_in_hbm(chip_sum))
        else:
            g, routs = _wgrad(lhs, rhs, f"wgrad_{name}", col_shards=col_shards)
            red[name] = _GradReduce(name, grad=g)
        return routs

    big(q1, dh4b, "w2_1", False)
    beside("w2_1")
    (dh3, dh3b, dp1, d_mlp_g1), _ = _bwd_mlp(dh4, h3, mlp_norm_g[1:2], p1, w["w1_1"], w["w2_1"], 1, tm=tm)
    big(n3, dp1, "w1_1", True)
    beside("w1_1")
    g, routs = _wgrad(y, dh3b, "wgrad_od_out", col_shards=False, riders=share("w2_1"))
    red["od_out"] = _GradReduce("od_out", grad=g)
    took(routs, ("w2_1", "took_share"))
    (dh2, dh2b, ds, d_od_norm, d_od_bin, d_od_lng, d_od_lnb, d_ws, d_bs), _ = _bwd_odd(
        dh3, h2, od_norm, s, cdf, sv, w["od_in"], od_lng, od_lnb, od_w_s[0], w["od_out"], tm=tm)
    routs = big(n2, ds, "od_in", True, riders=share("w1_1") + swap("od_out"))
    took(routs, ("w1_1", "took_share"), ("od_out", "took_pair"))
    beside("od_in")
    beside("od_out")
    half_groups = C_GROUPS // 2
    early = {"loss": loss_part, "od_w_s_lo": d_ws[:half_groups], "od_b_s": d_bs, "mlp_norm_g1": d_mlp_g1, "final_norm_g": d_final_g,
             "od_norm_g": d_od_norm, "od_b_in": d_od_bin, "od_ln_v_g": d_od_lng, "od_ln_v_b": d_od_lnb}
    share_early = [_ShareAll(list(early.values()))] if distributed else []
    routs = big(q0, dh2b, "w2_0", False, riders=share_early)
    landed_early = routs[0] if distributed else []
    beside("w2_0")
    (dh1, dh1b, dp0, d_mlp_g0), _ = _bwd_mlp(dh2, h1, mlp_norm_g[0:1], p0, w["w1_0"], w["w2_0"], 0, tm=tm)
    middle = {"od_w_s_hi": d_ws[half_groups:]}
    share_middle = [_ShareAll(list(middle.values()))] if distributed else []
    g, _ = _wgrad(mix, dh1b, "wgrad_ev_out", col_shards=False)
    red["ev_out"] = _GradReduce("ev_out", grad=g)
    routs = big(n1, dp0, "w1_0", True,
                riders=share("od_out") + share("od_in") + share("w2_0") + swap("ev_out") + share_middle)
    took(routs, ("od_out", "took_share"), ("od_in", "took_share"), ("w2_0", "took_share"), ("ev_out", "took_pair"))
    landed_middle = routs[4] if distributed else []
    beside("w1_0")
    beside("ev_out")

    (dx, dz, d_ev_norm, d_caw, d_cab, d_ev_lng, d_ev_lnb, d_cbw), _ = _bwd_even(
        dh1, x2, ev_norm_g, z, a2, cv, w["ev_in"], conv_a_w, ev_ln_a_g, ev_ln_a_b, conv_b_w, w["ev_out"], tm=tm, seq=seq)
    late = {"mlp_norm_g0": d_mlp_g0, "ev_norm_g": d_ev_norm, "ev_conv_a_b": d_cab, "ev_ln_a_g": d_ev_lng,
            "ev_ln_a_b": d_ev_lnb, "ev_conv_a_w": d_caw, "ev_conv_b_w": d_cbw}
    share_late = [_ShareAll(list(late.values()))] if distributed else []
    routs = big(n0, dz, "ev_in", True, riders=share("ev_out") + share("w1_0") + share_late)
    took(routs, ("ev_out", "took_share"), ("w1_0", "took_share"))
    beside("ev_in")
    own = {**early, **middle, **late}
    landed = dict(zip(own.keys(), landed_early + landed_middle + routs[2])) if distributed else None
    return dx, red, own, landed


def _rows128(a):
    rows = jnp.reshape(a, (-1, LANES))
    pad = (-rows.shape[0]) % SUBLANES
    return jnp.pad(rows, ((0, pad), (0, 0))) if pad else rows


def _pack(arrays):
    return jnp.concatenate([_rows128(a) for a in arrays], axis=0)


def _unpack(buf, shapes):
    out, r0 = [], 0
    for shp in shapes:
        size = 1
        for dim in shp:
            size *= dim
        nr = size // LANES
        out.append(jnp.reshape(buf[r0:r0 + nr], shp))
        r0 += nr + (-nr) % SUBLANES
    return out


def kernel(x, ev_norm_g, ev_w_in, ev_conv_a_w, ev_conv_a_b, ev_ln_a_g, ev_ln_a_b, ev_conv_b_w, ev_w_out, od_norm_g, od_w_in, od_b_in, od_ln_v_g, od_ln_v_b, od_w_s, od_b_s, od_w_out, mlp_norm_g, mlp_w1, mlp_w2, final_norm_g, loss_target, m_ev_norm_g, m_ev_w_in, m_ev_conv_a_w, m_ev_conv_a_b, m_ev_ln_a_g, m_ev_ln_a_b, m_ev_conv_b_w, m_ev_w_out, m_od_norm_g, m_od_w_in, m_od_b_in, m_od_ln_v_g, m_od_ln_v_b, m_od_w_s, m_od_b_s, m_od_w_out, m_mlp_norm_g, m_mlp_w1, m_mlp_w2, m_final_norm_g, v_ev_norm_g, v_ev_w_in, v_ev_conv_a_w, v_ev_conv_a_b, v_ev_ln_a_g, v_ev_ln_a_b, v_ev_conv_b_w, v_ev_w_out, v_od_norm_g, v_od_w_in, v_od_b_in, v_od_ln_v_g, v_od_ln_v_b, v_od_w_s, v_od_b_s, v_od_w_out, v_mlp_norm_g, v_mlp_w1, v_mlp_w2, v_final_norm_g):
    tm = TOKEN_TILE
    batch, seq, d = x.shape
    tokens = batch * seq
    x2 = jnp.reshape(x, (tokens, d))
    tgt2 = jnp.reshape(loss_target, (tokens, d))
    chip = 2 * lax.axis_index("x") + lax.axis_index("y")

    small_shapes = [(A_CONV_WIDTH, LANES), (B_CONV_WIDTH, LANES), (256,), (512,), (256,), (256,)]
    small_shard = _pack([ev_conv_a_w[0], ev_conv_b_w[0], od_norm_g[0], od_b_in[0], od_ln_v_g[0], od_ln_v_b[0]])
    small_shard = jnp.pad(small_shard, ((0, (-small_shard.shape[0]) % (4 * SUBLANES)), (0, 0)))
    first = [_place_shard(ev_w_in, 0, BF16, "place_ev_w_in"), _place_shard(ev_w_out, 0, BF16, "place_ev_w_out"),
             _place_shard(small_shard[None], 0, F32, "place_small")]
    staged = {
        "w1_0": _place_shard(mlp_w1, 0, BF16, "place_w1_0"), "w2_0": _place_shard(mlp_w2, 0, BF16, "place_w2_0"),
        "od_in": _place_shard(od_w_in, 0, BF16, "place_od_w_in"), "od_out": _place_shard(od_w_out, 0, BF16, "place_od_w_out"),
        "w1_1": _place_shard(mlp_w1, 1, BF16, "place_w1_1"), "w2_1": _place_shard(mlp_w2, 1, BF16, "place_w2_1"),
    }
    first = [_in_hbm(a) for a in first]
    staged = {nm: _in_hbm(a) for nm, a in staged.items()}
    g_ev_in, g_ev_out, g_small = _gather_beside(first, "gather_stage0", collective_id=1)
    gathered = {"ev_in": g_ev_in, "ev_out": g_ev_out}
    for stage, names in enumerate((("w1_0", "w2_0"), ("od_in", "od_out", "w1_1"), ("w2_1",))):
        done = _gather_beside([staged[nm] for nm in names], f"gather_stage{stage + 1}", collective_id=stage + 2)
        gathered.update(zip(names, done))
    small_all = jnp.reshape(_plain_copy(g_small, "small_weights_copy"), (N_CHIPS, -1, LANES))
    per_chip = [_unpack(small_all[q], small_shapes) for q in range(N_CHIPS)]
    conv_a_w = jnp.concatenate([pc[0] for pc in per_chip], axis=1)
    conv_b_w = jnp.concatenate([pc[1] for pc in per_chip], axis=1)
    od_norm = jnp.concatenate([pc[2] for pc in per_chip])[None, :]
    od_bias = jnp.concatenate([pc[3] for pc in per_chip])[None, :]
    od_lng = jnp.concatenate([pc[4] for pc in per_chip])[None, :]
    od_lnb = jnp.concatenate([pc[5] for pc in per_chip])[None, :]

    dx, red, own, landed = _forward_backward(
        x2, tgt2, gathered, conv_a_w, conv_b_w, od_norm, od_bias, od_lng, od_lnb,
        ev_norm_g, ev_conv_a_b, ev_ln_a_g, ev_ln_a_b, od_w_s, od_b_s, mlp_norm_g, final_norm_g, tm=tm, seq=seq)

    routs = _exchange([red["ev_in"].pair_share()], "reduce_tail")
    red["ev_in"].took_share(routs[0])

    given = {"ev_norm_g": (ev_norm_g, m_ev_norm_g, v_ev_norm_g), "ev_conv_a_b": (ev_conv_a_b, m_ev_conv_a_b, v_ev_conv_a_b),
             "ev_ln_a_g": (ev_ln_a_g, m_ev_ln_a_g, v_ev_ln_a_g), "ev_ln_a_b": (ev_ln_a_b, m_ev_ln_a_b, v_ev_ln_a_b),
             "od_w_s": (od_w_s, m_od_w_s, v_od_w_s), "od_b_s": (od_b_s, m_od_b_s, v_od_b_s),
             "mlp_norm_g": (mlp_norm_g, m_mlp_norm_g, v_mlp_norm_g), "final_norm_g": (final_norm_g, m_final_norm_g, v_final_norm_g),
             "ev_conv_a_w": (ev_conv_a_w, m_ev_conv_a_w, v_ev_conv_a_w), "ev_conv_b_w": (ev_conv_b_w, m_ev_conv_b_w, v_ev_conv_b_w),
             "od_norm_g": (od_norm_g, m_od_norm_g, v_od_norm_g), "od_b_in": (od_b_in, m_od_b_in, v_od_b_in),
             "od_ln_v_g": (od_ln_v_g, m_od_ln_v_g, v_od_ln_v_g), "od_ln_v_b": (od_ln_v_b, m_od_ln_v_b, v_od_ln_v_b)}
    shaped = {nm: tuple(jnp.reshape(a, shape) for a in given[nm]) for nm, shape, _, _ in SMALL_WEIGHTS}
    loss11, small_upd = _small_update(own, landed, shaped)
    loss = loss11[0, 0]
    upd = {nm: [jnp.reshape(o, given[nm][0].shape) for o in outs] for nm, outs in small_upd.items()}

    def big_update(wt, m, v, names, call):
        grads = [red[nm].reduced() for nm in names]
        shp3 = (len(grads),) + grads[0].shape
        outs = _adamw(jnp.reshape(wt, shp3), jnp.reshape(m, shp3), jnp.reshape(v, shp3), grads, call)
        return [jnp.reshape(o, wt.shape) for o in outs]

    wide = {"mlp_w2": (mlp_w2, m_mlp_w2, v_mlp_w2, ["w2_0", "w2_1"]), "mlp_w1": (mlp_w1, m_mlp_w1, v_mlp_w1, ["w1_0", "w1_1"]),
            "ev_w_out": (ev_w_out, m_ev_w_out, v_ev_w_out, ["ev_out"]), "od_w_out": (od_w_out, m_od_w_out, v_od_w_out, ["od_out"]),
            "od_w_in": (od_w_in, m_od_w_in, v_od_w_in, ["od_in"])}
    streamed = []
    for wt, m, v, names in wide.values():
        grads = [red[nm].reduced() for nm in names]
        shp3 = (len(grads),) + grads[0].shape
        streamed.append((jnp.reshape(wt, shp3), jnp.reshape(m, shp3), jnp.reshape(v, shp3), grads))
    for (nm, (wt, _, _, _)), outs in zip(wide.items(), _adamw_stream(streamed, "adamw_wide")):
        upd[nm] = [jnp.reshape(o, wt.shape) for o in outs]
    upd["ev_w_in"] = big_update(ev_w_in, m_ev_w_in, v_ev_w_in, ["ev_in"], "adamw_ev_w_in")

    order = ["ev_norm_g", "ev_w_in", "ev_conv_a_w", "ev_conv_a_b", "ev_ln_a_g", "ev_ln_a_b", "ev_conv_b_w", "ev_w_out",
             "od_norm_g", "od_w_in", "od_b_in", "od_ln_v_g", "od_ln_v_b", "od_w_s", "od_b_s", "od_w_out", "mlp_norm_g",
             "mlp_w1", "mlp_w2", "final_norm_g"]
    grad_x = jnp.reshape(dx, x.shape)
    return (loss, grad_x, *[upd[nm][0] for nm in order], *[upd[nm][1] for nm in order],
            *[upd[nm][2] for nm in order], *[upd[nm][3] for nm in order])
```

```python
import functools

import jax
import jax.numpy as jnp
from jax import lax
from jax.experimental import pallas as pl
from jax.experimental.pallas import tpu as pltpu
from jax.experimental.pallas import tpu_sc as plsc

F32 = jnp.float32
BF16 = jnp.bfloat16

D_MODEL = 1024
A_DIM = 512
B_DIM = 512
IN_EVEN = 2 * A_DIM + 3 * B_DIM
A_CONV_WIDTH = 31
B_CONV_WIDTH = 3
CHUNK = 128
C_GROUPS = 8
C_DIM = 1024
D_FF = 4096
RMS_EPS = 1e-6
LN_EPS = 1e-5
ADAM_LR = 0.001
ADAM_B1 = 0.9
ADAM_B2 = 0.999
ADAM_EPS = 1e-08
ADAM_WD = 0.01
ADAM_STEP = 10

N_CHIPS = 4
N_DEV = 8
TOKEN_TILE = 512
A_HALO = 32
B_HALO = 8
CONV_ROWS = 16
DW_TAPS = 4
ELEM_ROWS = 16
PAIR = 2 * CHUNK
LANES = 128
SUBLANES = 8
MXU_ROWS = 256
MIB = 1024 * 1024
MESH = pl.DeviceIdType.MESH
ANY = pl.BlockSpec(memory_space=pl.ANY)


def _dot(a, b):
    return lax.dot_general(a, b, (((1,), (0,)), ((), ())), preferred_element_type=F32)


def _dot_nt(a, b):
    return lax.dot_general(a, b, (((1,), (1,)), ((), ())), preferred_element_type=F32)


def _dot_tn(a, b):
    return lax.dot_general(a, b, (((0,), (0,)), ((), ())), preferred_element_type=F32)


def _params(vmem_mib, n_axes=1):
    return pltpu.CompilerParams(dimension_semantics=("arbitrary",) * n_axes, vmem_limit_bytes=vmem_mib * MIB)


def _row_spec(tm, cols, rev_nt=None):
    if rev_nt is None:
        return pl.BlockSpec((tm, cols), lambda i: (i, 0))
    return pl.BlockSpec((tm, cols), lambda i: (rev_nt - 1 - i, 0))


def _full_spec(shape):
    nd = len(shape)
    return pl.BlockSpec(shape, lambda i: (0,) * nd)


def _block_rows(rows, cap=512):
    best = SUBLANES
    for br in range(SUBLANES, min(rows, cap) + 1, SUBLANES):
        if rows % br == 0:
            best = br
    return best


FIRST_SWAP_ID = 5
N_LOADS = 2 * 2 * N_CHIPS


def _load_weights(loads, sems):
    @pl.when(pl.program_id(0) == 0)
    def _():
        copies = []
        for src, dst, rows_of_one in loads:
            r = src.shape[2]
            for q in range(N_CHIPS):
                for h in range(2):
                    part = dst.at[pl.ds((2 * q + h) * r, r)] if rows_of_one else dst.at[q, pl.ds(h * r, r)]
                    copies.append(pltpu.make_async_copy(src.at[q, h], part, sems.at[len(copies)]))
        for cp in copies:
            cp.start()
        for cp in copies:
            cp.wait()


def _rms_fwd(x, g):
    rstd = lax.rsqrt(jnp.mean(x * x, axis=-1, keepdims=True) + RMS_EPS)
    return x * rstd * g, rstd


def _rms_bwd(dn, x, rstd, g):
    a = dn * g
    xh = x * rstd
    dx = rstd * (a - xh * jnp.mean(a * xh, axis=-1, keepdims=True))
    dg = jnp.sum(dn * xh, axis=0, keepdims=True)
    return dx, dg


def _ln_stats(v):
    mu = jnp.mean(v, axis=-1, keepdims=True)
    xc = v - mu
    rs = lax.rsqrt(jnp.mean(xc * xc, axis=-1, keepdims=True) + LN_EPS)
    return xc * rs, rs


def _ln_bwd(dy, xhat, rs, g):
    dxh = dy * g
    dv = rs * (dxh - jnp.mean(dxh, axis=-1, keepdims=True) - xhat * jnp.mean(dxh * xhat, axis=-1, keepdims=True))
    return dv, jnp.sum(dy * xhat, axis=0, keepdims=True), jnp.sum(dy, axis=0, keepdims=True)


def _gelu_cdf(s):
    return 0.5 * (1.0 + lax.erf(s * 0.7071067811865476))


def _mesh_pos():
    return lax.axis_index("x"), lax.axis_index("y"), lax.axis_index("c")


def _other_chips(x, y):
    return [(1 - x, y), (x, 1 - y), (1 - x, 1 - y)]


def _remote(src, dst, send_sem, recv_sem, to):
    return pltpu.make_async_remote_copy(src_ref=src, dst_ref=dst, send_sem=send_sem, recv_sem=recv_sem,
                                        device_id=to, device_id_type=MESH)


def _like(arrays):
    return [jax.ShapeDtypeStruct(a.shape, a.dtype) for a in arrays]


class _PairSwap:
    def __init__(self, grads):
        self.ins = list(grads)
        self.out_shapes = [jax.ShapeDtypeStruct((g.shape[0],) + g.shape[2:], g.dtype) for g in grads]
        self.aliases = {}
        self.n_sems = len(grads)

    def _copies(self, ins, outs, send, recv):
        x, y, c = _mesh_pos()
        return [_remote(ins[t].at[:, 1 - c], outs[t], send.at[t], recv.at[t], (x, y, 1 - c)) for t in range(len(ins))]

    def start(self, ins, outs, send, recv):
        for cp in self._copies(ins, outs, send, recv):
            cp.start()

    def finish(self, ins, outs, send, recv):
        for cp in self._copies(ins, outs, send, recv):
            cp.wait()


class _ChipSwap:
    def __init__(self, parts):
        self.ins = list(parts)
        self.out_shapes = [jax.ShapeDtypeStruct((3,) + p.shape[1:], p.dtype) for p in parts]
        self.aliases = {}
        self.n_sems = 3 * len(parts)

    def _copies(self, ins, outs, send, recv):
        x, y, c = _mesh_pos()
        return [_remote(ins[t].at[2 * chip[0] + chip[1]], outs[t].at[k], send.at[3 * t + k], recv.at[3 * t + k], (*chip, c))
                for t in range(len(ins)) for k, chip in enumerate(_other_chips(x, y))]

    def start(self, ins, outs, send, recv):
        for cp in self._copies(ins, outs, send, recv):
            cp.start()

    def finish(self, ins, outs, send, recv):
        for cp in self._copies(ins, outs, send, recv):
            cp.wait()


class _PairShare:
    def __init__(self, fulls):
        self.ins = list(fulls)
        self.out_shapes = _like(fulls)
        self.aliases = {t: t for t in range(len(fulls))}
        self.n_sems = len(fulls)

    def _copies(self, ins, outs, send, recv):
        x, y, c = _mesh_pos()
        return [_remote(ins[t].at[c], outs[t].at[c], send.at[t], recv.at[t], (x, y, 1 - c)) for t in range(len(ins))]

    def start(self, ins, outs, send, recv):
        for cp in self._copies(ins, outs, send, recv):
            cp.start()

    def finish(self, ins, outs, send, recv):
        for cp in self._copies(ins, outs, send, recv):
            cp.wait()


class _ShareAll:
    def __init__(self, arrays):
        self.ins = list(arrays)
        self.out_shapes = [jax.ShapeDtypeStruct((N_DEV,) + a.shape, a.dtype) for a in arrays]
        self.aliases = {}
        self.n_sems = (N_DEV - 1) * len(arrays)

    def _peers(self):
        x, y, c = _mesh_pos()
        flips = [((r >> 2) & 1, (r >> 1) & 1, r & 1) for r in range(1, N_DEV)]
        return (x, y, c), [(x ^ fx, y ^ fy, c ^ fc) for fx, fy, fc in flips]

    def _sends(self, ins, outs, send, recv):
        (x, y, c), peers = self._peers()
        mine = 4 * x + 2 * y + c
        return [_remote(ins[a], outs[a].at[mine], send.at[7 * a + r], recv.at[7 * a + r], peer)
                for a in range(len(ins)) for r, peer in enumerate(peers)]

    def start(self, ins, outs, send, recv):
        for cp in self._sends(ins, outs, send, recv):
            cp.start()

    def finish(self, ins, outs, send, recv):
        (x, y, c), peers = self._peers()
        for a in range(len(ins)):
            for r, (px, py, pc) in enumerate(peers):
                blk = outs[a].at[4 * px + 2 * py + pc]
                _remote(blk, blk, send.at[7 * a + r], recv.at[7 * a + r], (x, y, c)).wait_recv()
        for cp in self._sends(ins, outs, send, recv):
            cp.wait_send()


def _gather_beside(bufs, name, collective_id):
    n = len(bufs)
    per = 7
    refs = [jax.new_ref(b, memory_space=pltpu.MemorySpace.HBM) for b in bufs]

    @pl.kernel(mesh=plsc.ScalarSubcoreMesh(axis_name="sequencer", num_cores=1), name=name,
               scratch_types=(pltpu.SemaphoreType.DMA((per * n,)), pltpu.SemaphoreType.DMA((per * n,))),
               compiler_params=pltpu.CompilerParams(collective_id=collective_id))
    def launch(send, recv):
        x, y, c = _mesh_pos()
        me, sibling = (x, y, c), (x, y, 1 - c)
        x_nbr, y_nbr = (1 - x, y, c), (x, 1 - y, c)
        mine, via_x, via_y, diag = 2 * x + y, 2 * (1 - x) + y, 2 * x + (1 - y), 2 * (1 - x) + (1 - y)
        barrier = pltpu.get_barrier_semaphore()
        peers = [x_nbr, y_nbr, sibling]
        for peer in peers:
            pl.semaphore_signal(barrier, inc=1, device_id=peer, device_id_type=MESH)
        pl.semaphore_wait(barrier, len(peers))

        def copy(t, k, src, dst, to):
            return _remote(src, dst, send.at[per * t + k], recv.at[per * t + k], to)

        def piece(t, chip, half, rows=None):
            blk = refs[t].at[chip, half]
            return blk if rows is None else blk.at[rows]

        started = []

        def go(cp):
            cp.start()
            started.append(cp)

        upper = [pl.ds(0, r.shape[2] // 2) for r in refs]
        lower = [pl.ds(r.shape[2] // 2, r.shape[2] // 2) for r in refs]
        for t in range(n):
            go(copy(t, 0, piece(t, mine, c), piece(t, mine, c), x_nbr))
            go(copy(t, 1, piece(t, mine, c), piece(t, mine, c), y_nbr))
        for t in range(n):
            copy(t, 0, piece(t, via_x, c), piece(t, via_x, c), me).wait_recv()
            go(copy(t, 2, piece(t, via_x, c, upper[t]), piece(t, via_x, c, upper[t]), y_nbr))
            go(copy(t, 4, piece(t, via_x, c), piece(t, via_x, c), sibling))
            copy(t, 1, piece(t, via_y, c), piece(t, via_y, c), me).wait_recv()
            go(copy(t, 3, piece(t, via_y, c, lower[t]), piece(t, via_y, c, lower[t]), x_nbr))
            go(copy(t, 5, piece(t, via_y, c), piece(t, via_y, c), sibling))
        for t in range(n):
            copy(t, 2, piece(t, diag, c, upper[t]), piece(t, diag, c, upper[t]), me).wait_recv()
            copy(t, 3, piece(t, diag, c, lower[t]), piece(t, diag, c, lower[t]), me).wait_recv()
            go(copy(t, 6, piece(t, diag, c), piece(t, diag, c), sibling))
        for t in range(n):
            for k, chip in ((4, via_x), (5, via_y), (6, diag)):
                copy(t, k, piece(t, chip, 1 - c), piece(t, chip, 1 - c), me).wait_recv()
        for cp in started:
            cp.wait_send()

    launch()
    return [r[...] for r in refs]


def _chip_swap_beside(parts, name, collective_id):
    src = jax.new_ref(parts, memory_space=pltpu.MemorySpace.HBM)
    dst = jax.empty_ref(jax.ShapeDtypeStruct((N_CHIPS - 1,) + parts.shape[1:], parts.dtype),
                        memory_space=pltpu.MemorySpace.HBM)
    swap = _ChipSwap([parts])

    @pl.kernel(mesh=plsc.ScalarSubcoreMesh(axis_name="sequencer", num_cores=1), name=name,
               scratch_types=(pltpu.SemaphoreType.DMA((N_CHIPS - 1,)), pltpu.SemaphoreType.DMA((N_CHIPS - 1,))),
               compiler_params=pltpu.CompilerParams(collective_id=collective_id))
    def launch(send, recv):
        x, y, c = _mesh_pos()
        barrier = pltpu.get_barrier_semaphore()
        peers = [(*chip, c) for chip in _other_chips(x, y)]
        for peer in peers:
            pl.semaphore_signal(barrier, inc=1, device_id=peer, device_id_type=MESH)
        pl.semaphore_wait(barrier, len(peers))
        swap.start([src], [dst], send, recv)
        swap.finish([src], [dst], send, recv)

    launch()
    return dst[...]


def _pallas(body, operands, *, name, grid, in_specs, out_specs, out_shape, scratch_shapes=(), vmem_mib=32, riders=(),
            prefetch=None):
    in_specs, out_specs, out_shape, scratch_shapes = list(in_specs), list(out_specs), list(out_shape), list(scratch_shapes)
    if not riders and prefetch is None:
        outs = pl.pallas_call(body, name=name, grid=grid, in_specs=in_specs, out_specs=out_specs, out_shape=out_shape,
                              scratch_shapes=scratch_shapes, compiler_params=_params(vmem_mib, len(grid)))(*operands)
        return list(outs), []
    n_in, n_out, n_scr = len(in_specs), len(out_specs), len(scratch_shapes)
    r_in = [len(r.ins) for r in riders]
    r_out = [len(r.out_shapes) for r in riders]
    steps = 1
    for g in grid:
        steps *= g

    n_pre = 0 if prefetch is None else 1

    def wrapped(*refs):
        refs = list(refs)
        pre, refs = refs[:n_pre], refs[n_pre:]
        ins, refs = refs[:n_in], refs[n_in:]
        rins = []
        for k in r_in:
            rins.append(refs[:k])
            refs = refs[k:]
        outs, refs = refs[:n_out], refs[n_out:]
        routs = []
        for k in r_out:
            routs.append(refs[:k])
            refs = refs[k:]
        scr, sems = refs[:n_scr], refs[n_scr:]
        step = 0
        for ax, g in enumerate(grid):
            step = step * g + pl.program_id(ax)

        def each(what):
            for j, r in enumerate(riders):
                getattr(r, what)(rins[j], routs[j], sems[2 * j], sems[2 * j + 1])

        if grid:
            pl.when(step == 0)(lambda: each("start"))
        else:
            each("start")
        body(*pre, *ins, *outs, *scr)
        if grid:
            pl.when(step == steps - 1)(lambda: each("finish"))
        else:
            each("finish")

    aliases, off_in, off_out = {}, n_pre + n_in, n_out
    for r, ki, ko in zip(riders, r_in, r_out):
        for i, o in r.aliases.items():
            aliases[off_in + i] = off_out + o
        off_in, off_out = off_in + ki, off_out + ko
    sems = []
    for r in riders:
        sems += [pltpu.SemaphoreType.DMA((r.n_sems,)), pltpu.SemaphoreType.DMA((r.n_sems,))]
    layout = dict(grid=grid, in_specs=in_specs + [ANY] * sum(r_in), out_specs=out_specs + [ANY] * sum(r_out),
                  scratch_shapes=scratch_shapes + sems)
    if prefetch is not None:
        layout = dict(grid_spec=pltpu.PrefetchScalarGridSpec(num_scalar_prefetch=1, **layout))
    res = pl.pallas_call(
        wrapped, name=name, **layout,
        out_shape=out_shape + [s for r in riders for s in r.out_shapes], input_output_aliases=aliases,
        compiler_params=pltpu.CompilerParams(dimension_semantics=("arbitrary",) * len(grid),
                                             vmem_limit_bytes=vmem_mib * MIB, has_side_effects=True),
    )(*([] if prefetch is None else [prefetch]), *operands, *[a for r in riders for a in r.ins])
    res = list(res)
    outs, res = res[:n_out], res[n_out:]
    routs = []
    for k in r_out:
        routs.append(res[:k])
        res = res[k:]
    return outs, routs


def _exchange(riders, name):
    return _pallas(lambda: None, [], name=name, grid=(), in_specs=[], out_specs=[], out_shape=[], riders=riders)[1]


def _in_hbm(a):
    return pltpu.with_memory_space_constraint(a, pltpu.HBM)


def _place_shard(w, layer, dtype, name):
    _, rows, cols = w.shape
    half = rows // 2
    br = _block_rows(half)
    nb = half // br
    mine = 2 * lax.axis_index("x") + lax.axis_index("y")

    def body(q_ref, w_ref, o_ref):
        o_ref[...] = w_ref[...].astype(dtype)

    return pl.pallas_call(
        body, name=name,
        grid_spec=pltpu.PrefetchScalarGridSpec(
            num_scalar_prefetch=1, grid=(2, nb),
            in_specs=[pl.BlockSpec((None, br, cols), lambda h, i, q: (layer, h * nb + i, 0))],
            out_specs=pl.BlockSpec((None, None, br, cols), lambda h, i, q: (q[0], h, i, 0))),
        out_shape=pltpu.HBM((N_CHIPS, 2, half, cols), dtype),
        compiler_params=_params(16, 2),
    )(jnp.reshape(mine, (1,)).astype(jnp.int32), _in_hbm(w))


def _plain_copy(a, name):
    def body(a_ref, o_ref):
        o_ref[...] = a_ref[...]

    vmem = pl.BlockSpec(memory_space=pltpu.VMEM)
    return pl.pallas_call(body, name=name, in_specs=[vmem], out_specs=vmem,
                          out_shape=jax.ShapeDtypeStruct(a.shape, a.dtype))(a)


def _add_pair(g, recv, name):
    _, _, r, cdim = g.shape
    br = _block_rows(r, 256)
    c = lax.axis_index("c")

    def body(c_ref, g_ref, r_ref, o_ref):
        o_ref[...] = (g_ref[...] + r_ref[...]).astype(BF16)

    return pl.pallas_call(
        body, name=name,
        grid_spec=pltpu.PrefetchScalarGridSpec(
            num_scalar_prefetch=1, grid=(N_CHIPS, r // br),
            in_specs=[pl.BlockSpec((None, None, br, cdim), lambda q, i, c_ref: (q, c_ref[0], i, 0)),
                      pl.BlockSpec((None, br, cdim), lambda q, i, c_ref: (q, i, 0))],
            out_specs=pl.BlockSpec((None, br, cdim), lambda q, i, c_ref: (q, i, 0))),
        out_shape=pltpu.HBM((N_CHIPS, r, cdim), BF16),
        compiler_params=_params(16, 2),
    )(jnp.reshape(c, (1,)).astype(jnp.int32), _in_hbm(g), _in_hbm(recv))


def _add_chips(own, recv, name):
    _, r, cdim = own.shape
    br = _block_rows(r, 256)
    x, y, c = _mesh_pos()

    def body(pos_ref, own_ref, r_ref, o_ref):
        acc = own_ref[...].astype(F32)
        for k in range(3):
            acc = acc + r_ref[k].astype(F32)
        o_ref[...] = acc

    return pl.pallas_call(
        body, name=name,
        grid_spec=pltpu.PrefetchScalarGridSpec(
            num_scalar_prefetch=1, grid=(r // br,),
            in_specs=[pl.BlockSpec((None, br, cdim), lambda i, pos: (pos[0], i, 0)),
                      pl.BlockSpec((3, br, cdim), lambda i, pos: (0, i, 0))],
            out_specs=pl.BlockSpec((None, br, cdim), lambda i, pos: (pos[1], i, 0))),
        out_shape=pltpu.HBM((2, r, cdim), F32),
        compiler_params=_params(16, 1),
    )(jnp.stack([2 * x + y, c]).astype(jnp.int32), _in_hbm(own), _in_hbm(recv))


def _adam_math(w, m, v, g):
    c1 = 1.0 / (1.0 - ADAM_B1 ** ADAM_STEP)
    c2 = 1.0 / (1.0 - ADAM_B2 ** ADAM_STEP)
    m_new = ADAM_B1 * m + (1.0 - ADAM_B1) * g
    v_new = ADAM_B2 * v + (1.0 - ADAM_B2) * (g * g)
    return -ADAM_LR * ((m_new * c1) / (jnp.sqrt(v_new * c2) + ADAM_EPS) + ADAM_WD * w), m_new, v_new


SMALL_WEIGHTS = [
    ("ev_norm_g", (1, D_MODEL), ["ev_norm_g"], None), ("ev_conv_a_b", (1, A_DIM), ["ev_conv_a_b"], None),
    ("ev_ln_a_g", (1, A_DIM), ["ev_ln_a_g"], None), ("ev_ln_a_b", (1, A_DIM), ["ev_ln_a_b"], None),
    ("od_w_s", (C_GROUPS, CHUNK, CHUNK), ["od_w_s_lo", "od_w_s_hi"], None), ("od_b_s", (C_GROUPS, CHUNK), ["od_b_s"], None),
    ("mlp_norm_g", (2, D_MODEL), ["mlp_norm_g0", "mlp_norm_g1"], None), ("final_norm_g", (1, D_MODEL), ["final_norm_g"], None),
    ("ev_conv_a_w", (A_CONV_WIDTH, A_DIM // N_CHIPS), ["ev_conv_a_w"], A_DIM // N_CHIPS),
    ("ev_conv_b_w", (B_CONV_WIDTH, B_DIM // N_CHIPS), ["ev_conv_b_w"], B_DIM // N_CHIPS),
    ("od_norm_g", (1, D_MODEL // N_CHIPS), ["od_norm_g"], D_MODEL // N_CHIPS),
    ("od_b_in", (1, 2 * C_DIM // N_CHIPS), ["od_b_in"], 2 * C_DIM // N_CHIPS),
    ("od_ln_v_g", (1, C_DIM // N_CHIPS), ["od_ln_v_g"], C_DIM // N_CHIPS),
    ("od_ln_v_b", (1, C_DIM // N_CHIPS), ["od_ln_v_b"], C_DIM // N_CHIPS),
]


def _small_update(own, landed, weights):
    names = list(own.keys())
    n_g, n_w = len(names), len(SMALL_WEIGHTS)

    def body(*refs):
        refs = list(refs)
        own_refs = dict(zip(names, refs[:n_g]))
        land_refs = dict(zip(names, refs[n_g:2 * n_g]))
        wmv = [refs[2 * n_g + 3 * i:2 * n_g + 3 * i + 3] for i in range(n_w)]
        o0 = 2 * n_g + 3 * n_w
        loss_ref = refs[o0]
        outs = [refs[o0 + 1 + 4 * i:o0 + 5 + 4 * i] for i in range(n_w)]
        acc = dict(zip(names, refs[o0 + 1 + 4 * n_w:]))
        x, y, c = _mesh_pos()
        mine, chip = 4 * x + 2 * y + c, 2 * x + y

        for nm in names:
            for d in range(N_DEV):
                def add(term, nm=nm, d=d):
                    acc[nm][...] = term if d == 0 else acc[nm][...] + term
                pl.when(mine == d)(lambda nm=nm, add=add: add(own_refs[nm][...]))
                pl.when(mine != d)(lambda nm=nm, d=d, add=add: add(land_refs[nm][d]))
        loss_ref[...] = acc["loss"][...]

        def update(i, rows, g):
            w_ref, m_ref, v_ref = wmv[i]
            delta, m_new, v_new = _adam_math(w_ref[rows], m_ref[rows], v_ref[rows], g)
            for ref, val in zip(outs[i], (g, delta, m_new, v_new)):
                ref[rows] = val

        for i, (_, shape, grads, per_chip) in enumerate(SMALL_WEIGHTS):
            for row, gname in enumerate(grads):
                per_grad = shape[0] // len(grads)
                rows = slice(row * per_grad, (row + 1) * per_grad)
                if per_chip is None:
                    update(i, rows, acc[gname][...])
                else:
                    for q in range(N_CHIPS):
                        pl.when(chip == q)(lambda i=i, rows=rows, gname=gname, q=q, per_chip=per_chip:
                                           update(i, rows, acc[gname][:, q * per_chip:(q + 1) * per_chip]))

    operands = [own[nm] for nm in names] + [landed[nm] for nm in names]
    for nm, _, _, _ in SMALL_WEIGHTS:
        operands += list(weights[nm])
    out_shape = [jax.ShapeDtypeStruct((1, 1), F32)]
    for _, shape, _, _ in SMALL_WEIGHTS:
        out_shape += [jax.ShapeDtypeStruct(shape, F32)] * 4
    res = pl.pallas_call(
        body, name="small_update", grid=(1,),
        in_specs=[_full_spec(a.shape) for a in operands], out_specs=[_full_spec(s.shape) for s in out_shape],
        out_shape=out_shape, scratch_shapes=[pltpu.VMEM(own[nm].shape, F32) for nm in names],
        compiler_params=_params(32, 1),
    )(*[_in_hbm(a) for a in operands])
    return res[0], {nm: res[1 + 4 * i:5 + 4 * i] for i, (nm, _, _, _) in enumerate(SMALL_WEIGHTS)}


def _adamw(w, m, v, grads, name):
    layers, r, cdim = w.shape
    br = _block_rows(r, 256 if cdim > LANES else 1024)
    blocks = r // br

    def body(*refs):
        w_ref, m_ref, v_ref = refs[:3]
        g_refs = refs[3:3 + layers]
        go_ref, d_ref, mo_ref, vo_ref = refs[3 + layers:]
        layer = pl.program_id(0)
        for l in range(layers):
            @pl.when(layer == l)
            def _(l=l):
                g = g_refs[l][...]
                go_ref[...] = g
                d_ref[...], mo_ref[...], vo_ref[...] = _adam_math(w_ref[...], m_ref[...], v_ref[...], g)

    spec3 = pl.BlockSpec((None, br, cdim), lambda l, i: (l, i, 0))
    g_specs = [pl.BlockSpec((br, cdim), lambda l, i, own=own: (jnp.clip(i + (l - own) * blocks, 0, blocks - 1), 0))
               for own in range(layers)]
    out = jax.ShapeDtypeStruct((layers, r, cdim), F32)
    outs, _ = _pallas(body, [_in_hbm(a) for a in (w, m, v, *grads)], name=name, grid=(layers, blocks),
                      in_specs=[spec3, spec3, spec3] + g_specs, out_specs=[spec3] * 4, out_shape=[out] * 4, vmem_mib=32)
    return outs


ADAMW_ROWS = 256
ADAMW_SLOTS = 3


def _adamw_stream(weights, name):
    cdim = weights[0][0].shape[2]
    flat = lambda a: jnp.reshape(a, (-1, cdim))
    operands, work = [], []
    for k, (w, m, v, grads) in enumerate(weights):
        layers, r, _ = w.shape
        assert r % ADAMW_ROWS == 0 and w.shape[2] == cdim, (name, w.shape)
        base = len(operands)
        operands += [flat(w), flat(m), flat(v), *grads]
        for l in range(layers):
            for b in range(0, r, ADAMW_ROWS):
                work.append((base, base + 3 + l, 4 * k, l * r + b, b))
    n_in = len(operands)

    def body(*refs):
        ins, outs = refs[:n_in], refs[n_in:n_in + 4 * len(weights)]
        buf_in, buf_out, sem_in, sem_out = refs[n_in + 4 * len(weights):]

        def reads(t):
            base, g_at, _, rows, g_rows = work[t]
            slot = t % ADAMW_SLOTS
            srcs = [ins[base + j].at[pl.ds(rows, ADAMW_ROWS)] for j in range(3)] + [ins[g_at].at[pl.ds(g_rows, ADAMW_ROWS)]]
            return [pltpu.make_async_copy(src, buf_in.at[slot, j], sem_in.at[slot, j]) for j, src in enumerate(srcs)]

        def writes(t):
            _, _, out_at, rows, _ = work[t]
            slot = t % ADAMW_SLOTS
            return [pltpu.make_async_copy(buf_out.at[slot, j], outs[out_at + j].at[pl.ds(rows, ADAMW_ROWS)],
                                          sem_out.at[slot, j]) for j in range(4)]

        for t in range(min(ADAMW_SLOTS - 1, len(work))):
            for cp in reads(t):
                cp.start()
        for t in range(len(work)):
            slot = t % ADAMW_SLOTS
            if t + ADAMW_SLOTS - 1 < len(work):
                for cp in reads(t + ADAMW_SLOTS - 1):
                    cp.start()
            for cp in reads(t):
                cp.wait()
            if t >= ADAMW_SLOTS:
                for cp in writes(t - ADAMW_SLOTS):
                    cp.wait()
            g = buf_in[slot, 3]
            buf_out[slot, 0] = g
            buf_out[slot, 1], buf_out[slot, 2], buf_out[slot, 3] = _adam_math(buf_in[slot, 0], buf_in[slot, 1],
                                                                             buf_in[slot, 2], g)
            for cp in writes(t):
                cp.start(priority=1)
        for t in range(max(0, len(work) - ADAMW_SLOTS), len(work)):
            for cp in writes(t):
                cp.wait()

    out_shape = [jax.ShapeDtypeStruct((w.shape[0] * w.shape[1], cdim), F32) for w, _, _, _ in weights for _ in range(4)]
    outs, _ = _pallas(body, [_in_hbm(a) for a in operands], name=name, grid=(1,), in_specs=[ANY] * n_in,
                      out_specs=[ANY] * len(out_shape), out_shape=out_shape,
                      scratch_shapes=[pltpu.VMEM((ADAMW_SLOTS, 4, ADAMW_ROWS, cdim), F32),
                                      pltpu.VMEM((ADAMW_SLOTS, 4, ADAMW_ROWS, cdim), F32),
                                      pltpu.SemaphoreType.DMA((ADAMW_SLOTS, 4)), pltpu.SemaphoreType.DMA((ADAMW_SLOTS, 4))],
                      vmem_mib=40)
    return [[jnp.reshape(o, w.shape) for o in outs[4 * k:4 * k + 4]] for k, (w, _, _, _) in enumerate(weights)]


def _fill_shifted(buf, rows):
    for b in range(1, SUBLANES):
        buf[b, 0:rows - SUBLANES, :] = buf[0, b:b + rows - SUBLANES, :]


def _window(buf, start, size):
    return buf[start % SUBLANES, start - start % SUBLANES:start - start % SUBLANES + size, :]


def _conv31(src, w_ref, r0, base, init):
    acc = init
    for k in range(A_CONV_WIDTH):
        acc = acc + w_ref[k:k + 1, :] * _window(src, base + k + r0, CONV_ROWS)
    return acc


def _fwd_even(x, norm_g, w_in, conv_a_w, conv_a_b, ln_g, ln_b, conv_b_w, w_out, *, tm, seq, riders=()):
    tokens = x.shape[0]
    nt, tps = tokens // tm, seq // tm

    def body(x_ref, g_ref, win_hbm, caw_ref, cab_ref, lng_ref, lnb_ref, cbw_ref, wout_hbm,
             h_ref, n_ref, z_ref, a2_ref, cv_ref, mix_ref, win_v, wout_v, pa, pb, sem):
        i = pl.program_id(0)

        _load_weights([(win_hbm, win_v, False), (wout_hbm, wout_v, True)], sem)

        xv = x_ref[...]
        nf, _ = _rms_fwd(xv, g_ref[...])
        n = nf.astype(BF16)
        n_ref[...] = n
        z = jnp.concatenate([_dot(n, win_v[j]) for j in range(N_CHIPS)], axis=1)
        z_ref[...] = z.astype(BF16)
        a_val, a_gate = z[:, 0:A_DIM], z[:, A_DIM:2 * A_DIM]
        b_gate, c_gate, b_val = z[:, 1024:1536], z[:, 1536:2048], z[:, 2048:2560]

        first = (i % tps) == 0

        @pl.when(first)
        def _():
            pa[0, 0:A_HALO, :] = jnp.zeros((A_HALO, A_DIM), F32)
            pb[0:B_HALO, :] = jnp.zeros((B_HALO, B_DIM), F32)

        @pl.when(jnp.logical_not(first))
        def _():
            pa[0, 0:A_HALO, :] = pa[0, tm:tm + A_HALO, :]
            pb[0:B_HALO, :] = pb[tm:tm + B_HALO, :]

        pa[0, A_HALO:A_HALO + tm, :] = a_val * jax.nn.sigmoid(a_gate)
        pb[B_HALO:B_HALO + tm, :] = c_gate * b_val
        _fill_shifted(pa, A_HALO + tm)
        bias = jnp.broadcast_to(cab_ref[...], (CONV_ROWS, A_DIM))
        for r0 in range(0, tm, CONV_ROWS):
            a2_ref[r0:r0 + CONV_ROWS, :] = _conv31(pa, caw_ref, r0, A_HALO - (A_CONV_WIDTH - 1), bias)
        xhat, _ = _ln_stats(a2_ref[...])
        a3 = xhat * lng_ref[...] + lnb_ref[...]
        a4 = a3 * jax.nn.sigmoid(a3)
        cv = cbw_ref[0:1, :] * pb[B_HALO - 2:B_HALO - 2 + tm, :]
        cv = cv + cbw_ref[1:2, :] * pb[B_HALO - 1:B_HALO - 1 + tm, :]
        cv = cv + cbw_ref[2:3, :] * pb[B_HALO:B_HALO + tm, :]
        cv_ref[...] = cv.astype(BF16)
        mix = jnp.concatenate([a4, b_gate * cv], axis=1).astype(BF16)
        mix_ref[...] = mix
        h_ref[...] = xv + _dot(mix, wout_v[...])

    shp = lambda cols, dt: jax.ShapeDtypeStruct((tokens, cols), dt)
    return _pallas(
        body, [x, norm_g, w_in, conv_a_w, conv_a_b, ln_g, ln_b, conv_b_w, w_out], name="fwd_even", grid=(nt,),
        in_specs=[_row_spec(tm, D_MODEL), _full_spec((1, D_MODEL)), ANY, _full_spec((A_CONV_WIDTH, A_DIM)),
                  _full_spec((1, A_DIM)), _full_spec((1, A_DIM)), _full_spec((1, A_DIM)),
                  _full_spec((B_CONV_WIDTH, B_DIM)), ANY],
        out_specs=[_row_spec(tm, D_MODEL), _row_spec(tm, D_MODEL), _row_spec(tm, IN_EVEN), _row_spec(tm, A_DIM),
                   _row_spec(tm, B_DIM), _row_spec(tm, D_MODEL)],
        out_shape=[shp(D_MODEL, F32), shp(D_MODEL, BF16), shp(IN_EVEN, BF16), shp(A_DIM, F32), shp(B_DIM, BF16),
                   shp(D_MODEL, BF16)],
        scratch_shapes=[pltpu.VMEM((N_CHIPS, D_MODEL, IN_EVEN // N_CHIPS), BF16), pltpu.VMEM((D_MODEL, D_MODEL), BF16),
                        pltpu.VMEM((SUBLANES, A_HALO + tm, A_DIM), F32), pltpu.VMEM((B_HALO + tm, B_DIM), F32),
                        pltpu.SemaphoreType.DMA((N_LOADS,))],
        vmem_mib=56, riders=riders)


def _loss_tail(xv, g, target, loss_ref, dh_ref, dhb_ref, dg_ref):
    @pl.when(pl.program_id(0) == 0)
    def _():
        loss_ref[...] = jnp.zeros((1, 1), F32)
        dg_ref[...] = jnp.zeros((1, D_MODEL), F32)

    out, rstd = _rms_fwd(xv, g)
    err = out - target
    per_token = jnp.sum(err * err, axis=1, keepdims=True) * (1.0 / D_MODEL)
    loss_ref[...] += 0.5 * jnp.sum(per_token, axis=0, keepdims=True)
    dx, dg = _rms_bwd(err * (1.0 / D_MODEL), xv, rstd, g)
    dh_ref[...] = dx
    dhb_ref[...] = dx.astype(BF16)
    dg_ref[...] += dg


def _fwd_mlp(h, norm_g, w1, w2, layer, *, tm, riders=(), head=None):
    tokens = h.shape[0]
    nt = tokens // tm
    fs = D_FF // N_CHIPS
    n_in = 4 if head is None else 6

    def body(*refs):
        h_ref, g_ref, w1_hbm, w2_hbm = refs[:4]
        w1_v, w2_v, sem = refs[-3:]
        outs = refs[n_in:-3]
        n_ref, p_ref, q_ref = outs[1:4] if head is None else outs[0:3]
        _load_weights([(w1_hbm, w1_v, False), (w2_hbm, w2_v, False)], sem)

        xv = h_ref[...]
        nf, _ = _rms_fwd(xv, g_ref[...])
        n = nf.astype(BF16)
        n_ref[...] = n
        acc = xv
        for j in range(N_CHIPS):
            p = _dot(n, w1_v[j])
            p_ref[:, j * fs:(j + 1) * fs] = p.astype(BF16)
            r = jnp.maximum(p, 0.0)
            q = (r * r).astype(BF16)
            q_ref[:, j * fs:(j + 1) * fs] = q
            acc = acc + _dot(q, w2_v[j])
        if head is None:
            outs[0][...] = acc
        else:
            _loss_tail(acc, refs[4][...], refs[5][...], *outs[3:7])

    shp = lambda cols, dt: jax.ShapeDtypeStruct((tokens, cols), dt)
    saved_specs = [_row_spec(tm, D_MODEL), _row_spec(tm, D_FF), _row_spec(tm, D_FF)]
    saved_shapes = [shp(D_MODEL, BF16), shp(D_FF, BF16), shp(D_FF, BF16)]
    if head is None:
        operands, in_specs = [h, norm_g, w1, w2], [_row_spec(tm, D_MODEL), _full_spec((1, D_MODEL)), ANY, ANY]
        out_specs, out_shape = [_row_spec(tm, D_MODEL)] + saved_specs, [shp(D_MODEL, F32)] + saved_shapes
    else:
        operands = [h, norm_g, w1, w2, *head]
        in_specs = [_row_spec(tm, D_MODEL), _full_spec((1, D_MODEL)), ANY, ANY, _full_spec((1, D_MODEL)), _row_spec(tm, D_MODEL)]
        out_specs = saved_specs + [_full_spec((1, 1)), _row_spec(tm, D_MODEL), _row_spec(tm, D_MODEL), _full_spec((1, D_MODEL))]
        out_shape = saved_shapes + [jax.ShapeDtypeStruct((1, 1), F32), shp(D_MODEL, F32), shp(D_MODEL, BF16),
                                    jax.ShapeDtypeStruct((1, D_MODEL), F32)]
    return _pallas(
        body, operands, name=f"fwd_mlp{layer}", grid=(nt,), in_specs=in_specs, out_specs=out_specs, out_shape=out_shape,
        scratch_shapes=[pltpu.VMEM((N_CHIPS, D_MODEL, fs), BF16), pltpu.VMEM((N_CHIPS, fs, D_MODEL), BF16),
                        pltpu.SemaphoreType.DMA((N_LOADS,))],
        vmem_mib=56, riders=riders)


def _tril_mask():
    row = lax.broadcasted_iota(jnp.int32, (CHUNK, CHUNK), 0)
    col = lax.broadcasted_iota(jnp.int32, (CHUNK, CHUNK), 1)
    return row >= col


def _triu_mask():
    row = lax.broadcasted_iota(jnp.int32, (CHUNK, CHUNK), 0)
    col = lax.broadcasted_iota(jnp.int32, (CHUNK, CHUNK), 1)
    return row <= col


def _fwd_odd(h, norm_g, w_in, b_in, ln_g, ln_b, w_s, b_s_rows, w_out, *, tm, riders=()):
    tokens = h.shape[0]
    nt = tokens // tm
    cs = 2 * C_DIM // N_CHIPS

    def body(h_ref, g_ref, win_hbm, bin_ref, lng_ref, lnb_ref, ws_ref, bs_ref, wout_hbm,
             ho_ref, n_ref, s_ref, cdf_ref, sv_ref, y_ref, win_v, wout_v, bd, sem):
        _load_weights([(win_hbm, win_v, False), (wout_hbm, wout_v, True)], sem)

        @pl.when(pl.program_id(0) == 0)
        def _():
            mask = _tril_mask()
            bd[...] = jnp.zeros(bd.shape, BF16)
            for g in range(C_GROUPS):
                w = jnp.where(mask, ws_ref[g], 0.0).astype(BF16)
                bd[g, 0:CHUNK, 0:CHUNK] = w
                bd[g, CHUNK:PAIR, CHUNK:PAIR] = w

        xv = h_ref[...]
        nf, _ = _rms_fwd(xv, g_ref[...])
        n = nf.astype(BF16)
        n_ref[...] = n
        s = jnp.concatenate([_dot(n, win_v[j]) for j in range(N_CHIPS)], axis=1) + bin_ref[...]
        s_ref[...] = s.astype(BF16)
        cdf = _gelu_cdf(s)
        cdf_ref[...] = cdf.astype(BF16)
        zz = s * cdf
        u, v = zz[:, 0:C_DIM], zz[:, C_DIM:2 * C_DIM]
        xhat, _ = _ln_stats(v)
        vn = (xhat * lng_ref[...] + lnb_ref[...]).astype(BF16)
        for g in range(C_GROUPS):
            cols = slice(g * CHUNK, (g + 1) * CHUNK)
            bias = jnp.concatenate([bs_ref[g], bs_ref[g]], axis=0)
            for r0 in range(0, tm, PAIR):
                sv = _dot(bd[g], vn[r0:r0 + PAIR, cols]) + bias
                sv_ref[r0:r0 + PAIR, cols] = sv.astype(BF16)
                y_ref[r0:r0 + PAIR, cols] = (u[r0:r0 + PAIR, cols] * sv).astype(BF16)
        ho_ref[...] = xv + _dot(y_ref[...], wout_v[...])

    shp = lambda cols, dt: jax.ShapeDtypeStruct((tokens, cols), dt)
    return _pallas(
        body, [h, norm_g, w_in, b_in, ln_g, ln_b, w_s, b_s_rows, w_out], name="fwd_odd", grid=(nt,),
        in_specs=[_row_spec(tm, D_MODEL), _full_spec((1, D_MODEL)), ANY, _full_spec((1, 2 * C_DIM)),
                  _full_spec((1, C_DIM)), _full_spec((1, C_DIM)), _full_spec((C_GROUPS, CHUNK, CHUNK)),
                  _full_spec((C_GROUPS, CHUNK, CHUNK)), ANY],
        out_specs=[_row_spec(tm, D_MODEL), _row_spec(tm, D_MODEL), _row_spec(tm, 2 * C_DIM), _row_spec(tm, 2 * C_DIM),
                   _row_spec(tm, C_DIM), _row_spec(tm, C_DIM)],
        out_shape=[shp(D_MODEL, F32), shp(D_MODEL, BF16), shp(2 * C_DIM, BF16), shp(2 * C_DIM, BF16), shp(C_DIM, BF16),
                   shp(C_DIM, BF16)],
        scratch_shapes=[pltpu.VMEM((N_CHIPS, D_MODEL, cs), BF16), pltpu.VMEM((C_DIM, D_MODEL), BF16),
                        pltpu.VMEM((C_GROUPS, PAIR, PAIR), BF16), pltpu.SemaphoreType.DMA((N_LOADS,))],
        vmem_mib=56, riders=riders)


def _bwd_mlp(dh, h, norm_g, p, w1, w2, layer, *, tm, riders=()):
    tokens = h.shape[0]
    nt = tokens // tm
    fs = D_FF // N_CHIPS

    def body(dh_ref, h_ref, g_ref, p_ref, w1_hbm, w2_hbm, dx_ref, dxb_ref, dp_ref, dg_ref, w1_v, w2_v, sem):
        @pl.when(pl.program_id(0) == 0)
        def _():
            dg_ref[...] = jnp.zeros((1, D_MODEL), F32)

        _load_weights([(w1_hbm, w1_v, False), (w2_hbm, w2_v, False)], sem)

        dhv = dh_ref[...]
        dhb = dhv.astype(BF16)
        dn = jnp.zeros((tm, D_MODEL), F32)
        for j in range(N_CHIPS):
            dq = _dot_nt(dhb, w2_v[j])
            r = jnp.maximum(p_ref[:, j * fs:(j + 1) * fs].astype(F32), 0.0)
            dp = ((2.0 * r) * dq).astype(BF16)
            dp_ref[:, j * fs:(j + 1) * fs] = dp
            dn = dn + _dot_nt(dp, w1_v[j])
        xv = h_ref[...]
        g = g_ref[...]
        _, rstd = _rms_fwd(xv, g)
        dx, dg = _rms_bwd(dn, xv, rstd, g)
        dx_ref[...] = dhv + dx
        dxb_ref[...] = (dhv + dx).astype(BF16)
        dg_ref[...] += dg

    return _pallas(
        body, [dh, h, norm_g, p, w1, w2], name=f"bwd_mlp{layer}", grid=(nt,),
        in_specs=[_row_spec(tm, D_MODEL), _row_spec(tm, D_MODEL), _full_spec((1, D_MODEL)), _row_spec(tm, D_FF), ANY, ANY],
        out_specs=[_row_spec(tm, D_MODEL), _row_spec(tm, D_MODEL), _row_spec(tm, D_FF), _full_spec((1, D_MODEL))],
        out_shape=[jax.ShapeDtypeStruct((tokens, D_MODEL), F32), jax.ShapeDtypeStruct((tokens, D_MODEL), BF16),
                   jax.ShapeDtypeStruct((tokens, D_FF), BF16), jax.ShapeDtypeStruct((1, D_MODEL), F32)],
        scratch_shapes=[pltpu.VMEM((N_CHIPS, D_MODEL, fs), BF16), pltpu.VMEM((N_CHIPS, fs, D_MODEL), BF16),
                        pltpu.SemaphoreType.DMA((N_LOADS,))],
        vmem_mib=56, riders=riders)


def _bwd_odd(dh, h, norm_g, s, cdf, sv, w_in, ln_g, ln_b, w_s, w_out, *, tm, riders=()):
    tokens = h.shape[0]
    nt = tokens // tm
    cs = 2 * C_DIM // N_CHIPS

    def body(dh_ref, h_ref, g_ref, s_ref, cdf_ref, sv_ref, win_hbm, lng_ref, lnb_ref, ws_ref, wout_hbm,
             dx_ref, dxb_ref, ds_ref, dg_ref, dbin_ref, dlng_ref, dlnb_ref, dws_ref, dbs_ref,
             win_v, wout_v, bdt, dws_acc, dbs_acc, dvn, sem):
        i = pl.program_id(0)

        _load_weights([(win_hbm, win_v, False), (wout_hbm, wout_v, True)], sem)

        @pl.when(i == 0)
        def _():
            mask_t = _triu_mask()
            bdt[...] = jnp.zeros(bdt.shape, BF16)
            for g in range(C_GROUPS):
                wt = jnp.where(mask_t, ws_ref[g].T, 0.0).astype(BF16)
                bdt[g, 0:CHUNK, 0:CHUNK] = wt
                bdt[g, CHUNK:PAIR, CHUNK:PAIR] = wt
            dws_acc[...] = jnp.zeros(dws_acc.shape, F32)
            dbs_acc[...] = jnp.zeros(dbs_acc.shape, F32)
            dg_ref[...] = jnp.zeros(dg_ref.shape, F32)
            dbin_ref[...] = jnp.zeros(dbin_ref.shape, F32)
            dlng_ref[...] = jnp.zeros(dlng_ref.shape, F32)
            dlnb_ref[...] = jnp.zeros(dlnb_ref.shape, F32)

        dhv = dh_ref[...]
        dy = _dot_nt(dhv.astype(BF16), wout_v[...])
        sf = s_ref[...].astype(F32)
        cdf = cdf_ref[...].astype(F32)
        pdf = jnp.exp(-0.5 * sf * sf) * 0.3989422804014327
        zz = sf * cdf
        dgelu = cdf + sf * pdf
        u, v = zz[:, 0:C_DIM], zz[:, C_DIM:2 * C_DIM]
        xhat, rs = _ln_stats(v)
        lng = lng_ref[...]
        vn = (xhat * lng + lnb_ref[...]).astype(BF16)
        du = dy * sv_ref[...].astype(F32)
        dsv = dy * u
        dsvb = dsv.astype(BF16)
        for g in range(C_GROUPS):
            cols = slice(g * CHUNK, (g + 1) * CHUNK)
            for r0 in range(0, tm, PAIR):
                blk = dsvb[r0:r0 + PAIR, cols]
                dvn[r0:r0 + PAIR, cols] = _dot(bdt[g], blk)
                dws_acc[g] += _dot_nt(blk, vn[r0:r0 + PAIR, cols])
                dbs_acc[g] += dsv[r0:r0 + CHUNK, cols] + dsv[r0 + CHUNK:r0 + PAIR, cols]
        dv, dlng, dlnb = _ln_bwd(dvn[...], xhat, rs, lng)
        dlng_ref[...] += dlng
        dlnb_ref[...] += dlnb
        ds = jnp.concatenate([du, dv], axis=1) * dgelu
        dbin_ref[...] += jnp.sum(ds, axis=0, keepdims=True)
        dsb = ds.astype(BF16)
        ds_ref[...] = dsb
        dn = jnp.zeros((tm, D_MODEL), F32)
        for j in range(N_CHIPS):
            dn = dn + _dot_nt(dsb[:, j * cs:(j + 1) * cs], win_v[j])
        xv = h_ref[...]
        g = g_ref[...]
        _, rstd = _rms_fwd(xv, g)
        dx, dg = _rms_bwd(dn, xv, rstd, g)
        dx_ref[...] = dhv + dx
        dxb_ref[...] = (dhv + dx).astype(BF16)
        dg_ref[...] += dg

        @pl.when(i == nt - 1)
        def _():
            mask = _tril_mask()
            for g in range(C_GROUPS):
                full = dws_acc[g]
                dws_ref[g] = jnp.where(mask, full[0:CHUNK, 0:CHUNK] + full[CHUNK:PAIR, CHUNK:PAIR], 0.0)
                dbs_ref[g:g + 1, :] = jnp.sum(dbs_acc[g].T, axis=0, keepdims=True)

    row = lambda cols: jax.ShapeDtypeStruct((1, cols), F32)
    return _pallas(
        body, [dh, h, norm_g, s, cdf, sv, w_in, ln_g, ln_b, w_s, w_out], name="bwd_odd", grid=(nt,),
        in_specs=[_row_spec(tm, D_MODEL), _row_spec(tm, D_MODEL), _full_spec((1, D_MODEL)), _row_spec(tm, 2 * C_DIM),
                  _row_spec(tm, 2 * C_DIM), _row_spec(tm, C_DIM), ANY, _full_spec((1, C_DIM)), _full_spec((1, C_DIM)),
                  _full_spec((C_GROUPS, CHUNK, CHUNK)), ANY],
        out_specs=[_row_spec(tm, D_MODEL), _row_spec(tm, D_MODEL), _row_spec(tm, 2 * C_DIM), _full_spec((1, D_MODEL)),
                   _full_spec((1, 2 * C_DIM)),
                   _full_spec((1, C_DIM)), _full_spec((1, C_DIM)), _full_spec((C_GROUPS, CHUNK, CHUNK)),
                   _full_spec((C_GROUPS, CHUNK))],
        out_shape=[jax.ShapeDtypeStruct((tokens, D_MODEL), F32), jax.ShapeDtypeStruct((tokens, D_MODEL), BF16),
                   jax.ShapeDtypeStruct((tokens, 2 * C_DIM), BF16),
                   row(D_MODEL), row(2 * C_DIM), row(C_DIM), row(C_DIM),
                   jax.ShapeDtypeStruct((C_GROUPS, CHUNK, CHUNK), F32), jax.ShapeDtypeStruct((C_GROUPS, CHUNK), F32)],
        scratch_shapes=[pltpu.VMEM((N_CHIPS, D_MODEL, cs), BF16), pltpu.VMEM((C_DIM, D_MODEL), BF16),
                        pltpu.VMEM((C_GROUPS, PAIR, PAIR), BF16), pltpu.VMEM((C_GROUPS, PAIR, PAIR), F32),
                        pltpu.VMEM((C_GROUPS, CHUNK, CHUNK), F32), pltpu.VMEM((tm, C_DIM), F32),
                        pltpu.SemaphoreType.DMA((N_LOADS,))],
        vmem_mib=56, riders=riders)


def _bwd_even(dh, x, norm_g, z, a2, cv, w_in, conv_a_w, ln_g, ln_b, conv_b_w, w_out, *, tm, seq, riders=()):
    tokens = x.shape[0]
    nt, tps = tokens // tm, seq // tm
    ws = IN_EVEN // N_CHIPS

    def body(dh_ref, x_ref, g_ref, z_ref, a2_ref, cv_ref, win_hbm, caw_ref, lng_ref, lnb_ref, cbw_ref, wout_hbm,
             dx_ref, dz_ref, dg_ref, dcaw_ref, dcab_ref, dlng_ref, dlnb_ref, dcbw_ref,
             win_v, wout_v, ea, eb, a1s, da1s, sigs, wide, dw_acc, sem):
        i = pl.program_id(0)

        _load_weights([(win_hbm, win_v, False), (wout_hbm, wout_v, True)], sem)

        @pl.when(i == 0)
        def _():
            dw_acc[...] = jnp.zeros(dw_acc.shape, F32)
            for ref in (dg_ref, dcab_ref, dlng_ref, dlnb_ref, dcbw_ref):
                ref[...] = jnp.zeros(ref.shape, F32)

        last = ((nt - 1 - i) % tps) == tps - 1

        @pl.when(last)
        def _():
            ea[0, tm:tm + A_HALO, :] = jnp.zeros((A_HALO, A_DIM), F32)
            eb[tm:tm + B_HALO, :] = jnp.zeros((B_HALO, B_DIM), F32)

        @pl.when(jnp.logical_not(last))
        def _():
            ea[0, tm:tm + A_HALO, :] = ea[0, 0:A_HALO, :]
            eb[tm:tm + B_HALO, :] = eb[0:B_HALO, :]

        wide[...] = _dot_nt(dh_ref[...].astype(BF16), wout_v[...])
        lng, lnb = lng_ref[...], lnb_ref[...]
        zero_row = jnp.zeros((1, A_DIM), F32)
        dlng, dlnb, dcab = zero_row, zero_row, zero_row
        for r0 in range(0, tm, ELEM_ROWS):
            rows = slice(r0, r0 + ELEM_ROWS)
            a_val, a_gate = z_ref[rows, 0:A_DIM].astype(F32), z_ref[rows, A_DIM:2 * A_DIM].astype(F32)
            xhat, rs = _ln_stats(a2_ref[rows, :])
            a3 = xhat * lng + lnb
            sg = jax.nn.sigmoid(a3)
            da3 = wide[rows, 0:A_DIM] * (sg * (1.0 + a3 * (1.0 - sg)))
            da2, g_part, b_part = _ln_bwd(da3, xhat, rs, lng)
            dlng, dlnb, dcab = dlng + g_part, dlnb + b_part, dcab + jnp.sum(da2, axis=0, keepdims=True)
            ea[0, rows, :] = da2
            eb[rows, :] = wide[rows, A_DIM:A_DIM + B_DIM] * z_ref[rows, 1024:1536].astype(F32)
            sig = jax.nn.sigmoid(a_gate)
            sigs[rows, :] = sig
            a1s[rows, :] = a_val * sig
        dlng_ref[...] += dlng
        dlnb_ref[...] += dlnb
        dcab_ref[...] += dcab
        _fill_shifted(ea, tm + A_HALO)
        for r0 in range(0, tm, CONV_ROWS):
            acc = jnp.zeros((CONV_ROWS, A_DIM), F32)
            for j in range(A_CONV_WIDTH):
                acc = acc + caw_ref[A_CONV_WIDTH - 1 - j:A_CONV_WIDTH - j, :] * _window(ea, r0 + j, CONV_ROWS)
            da1s[r0:r0 + CONV_ROWS, :] = acc
        for j0 in range(0, A_CONV_WIDTH, DW_TAPS):
            taps = range(j0, min(j0 + DW_TAPS, A_CONV_WIDTH))
            part = [jnp.zeros((CONV_ROWS, A_DIM), F32) for _ in taps]
            for r0 in range(0, tm, CONV_ROWS):
                a1c = a1s[r0:r0 + CONV_ROWS, :]
                for u, j in enumerate(taps):
                    part[u] = part[u] + _window(ea, r0 + j, CONV_ROWS) * a1c
            for u, j in enumerate(taps):
                dw_acc[A_CONV_WIDTH - 1 - j] += part[u]
        dcbw = [jnp.zeros((1, B_DIM), F32) for _ in range(B_CONV_WIDTH)]
        for r0 in range(0, tm, ELEM_ROWS):
            rows = slice(r0, r0 + ELEM_ROWS)
            da1, sig = da1s[rows, :], sigs[rows, :]
            dz_ref[rows, 0:A_DIM] = (da1 * sig).astype(BF16)
            dz_ref[rows, A_DIM:2 * A_DIM] = (da1 * z_ref[rows, 0:A_DIM].astype(F32) * (sig * (1.0 - sig))).astype(BF16)
            c_gate, b_val = z_ref[rows, 1536:2048].astype(F32), z_ref[rows, 2048:2560].astype(F32)
            dz_ref[rows, 1024:1536] = (wide[rows, A_DIM:A_DIM + B_DIM] * cv_ref[rows, :].astype(F32)).astype(BF16)
            cb = c_gate * b_val
            dcb = jnp.zeros((ELEM_ROWS, B_DIM), F32)
            for j in range(B_CONV_WIDTH):
                k = B_CONV_WIDTH - 1 - j
                sl = eb[r0 + j:r0 + j + ELEM_ROWS, :]
                dcb = dcb + cbw_ref[k:k + 1, :] * sl
                dcbw[k] = dcbw[k] + jnp.sum(sl * cb, axis=0, keepdims=True)
            dz_ref[rows, 1536:2048] = (dcb * b_val).astype(BF16)
            dz_ref[rows, 2048:2560] = (dcb * c_gate).astype(BF16)
        for k in range(B_CONV_WIDTH):
            dcbw_ref[k:k + 1, :] += dcbw[k]
        dn = jnp.zeros((tm, D_MODEL), F32)
        for j in range(N_CHIPS):
            dn = dn + _dot_nt(dz_ref[:, j * ws:(j + 1) * ws], win_v[j])
        wide[...] = dn
        g = g_ref[...]
        dg = jnp.zeros((1, D_MODEL), F32)
        for r0 in range(0, tm, ELEM_ROWS):
            rows = slice(r0, r0 + ELEM_ROWS)
            xv = x_ref[rows, :]
            _, rstd = _rms_fwd(xv, g)
            dx, dg_part = _rms_bwd(wide[rows, :], xv, rstd, g)
            dx_ref[rows, :] = dh_ref[rows, :] + dx
            dg = dg + dg_part
        dg_ref[...] += dg

        @pl.when(i == nt - 1)
        def _():
            for k in range(A_CONV_WIDTH):
                dcaw_ref[k:k + 1, :] = jnp.sum(dw_acc[k], axis=0, keepdims=True)

    row = lambda cols: jax.ShapeDtypeStruct((1, cols), F32)
    rs_ = functools.partial(_row_spec, rev_nt=nt)
    return _pallas(
        body, [dh, x, norm_g, z, a2, cv, w_in, conv_a_w, ln_g, ln_b, conv_b_w, w_out], name="bwd_even", grid=(nt,),
        in_specs=[rs_(tm, D_MODEL), rs_(tm, D_MODEL), _full_spec((1, D_MODEL)), rs_(tm, IN_EVEN), rs_(tm, A_DIM),
                  rs_(tm, B_DIM), ANY, _full_spec((A_CONV_WIDTH, A_DIM)), _full_spec((1, A_DIM)), _full_spec((1, A_DIM)),
                  _full_spec((B_CONV_WIDTH, B_DIM)), ANY],
        out_specs=[rs_(tm, D_MODEL), rs_(tm, IN_EVEN), _full_spec((1, D_MODEL)), _full_spec((A_CONV_WIDTH, A_DIM)),
                   _full_spec((1, A_DIM)), _full_spec((1, A_DIM)), _full_spec((1, A_DIM)), _full_spec((B_CONV_WIDTH, B_DIM))],
        out_shape=[jax.ShapeDtypeStruct((tokens, D_MODEL), F32), jax.ShapeDtypeStruct((tokens, IN_EVEN), BF16),
                   row(D_MODEL), jax.ShapeDtypeStruct((A_CONV_WIDTH, A_DIM), F32), row(A_DIM), row(A_DIM), row(A_DIM),
                   jax.ShapeDtypeStruct((B_CONV_WIDTH, B_DIM), F32)],
        scratch_shapes=[pltpu.VMEM((N_CHIPS, D_MODEL, ws), BF16), pltpu.VMEM((D_MODEL, D_MODEL), BF16),
                        pltpu.VMEM((SUBLANES, tm + A_HALO, A_DIM), F32), pltpu.VMEM((tm + B_HALO, B_DIM), F32),
                        pltpu.VMEM((tm, A_DIM), F32), pltpu.VMEM((tm, A_DIM), F32), pltpu.VMEM((tm, A_DIM), F32),
                        pltpu.VMEM((tm, D_MODEL), F32),
                        pltpu.VMEM((A_CONV_WIDTH, CONV_ROWS, A_DIM), F32), pltpu.SemaphoreType.DMA((N_LOADS,))],
        vmem_mib=56, riders=riders)


def _wgrad(a, b, name, *, col_shards, riders=()):
    tokens, m = a.shape
    n = b.shape[1]
    kc = 512
    if col_shards:
        bm, bn = m // 2, n // N_CHIPS
        grid = (2, N_CHIPS)
        out_spec = pl.BlockSpec((None, None, bm, bn), lambda i, j: (j, i, 0, 0))
    elif m // 8 >= MXU_ROWS:
        bm, bn = m // 8, n
        grid = (8, 1)
        out_spec = pl.BlockSpec((None, None, bm, bn), lambda i, j: (i // 2, i % 2, 0, 0))
    else:
        bm, bn = m // N_CHIPS, n
        grid = (N_CHIPS, 1)
        out_spec = pl.BlockSpec((None, 2, bm // 2, bn), lambda i, j: (i, 0, 0, 0))

    def body(a_ref, b_ref, o_ref):
        acc = jnp.zeros((bm, bn), F32)
        for k0 in range(0, tokens, kc):
            acc = acc + _dot_tn(a_ref[k0:k0 + kc, :].astype(BF16), b_ref[k0:k0 + kc, :].astype(BF16))
        if len(o_ref.shape) == 3:
            o_ref[0] = acc[0:bm // 2]
            o_ref[1] = acc[bm // 2:bm]
        else:
            o_ref[...] = acc

    out_rows = m // 2 if col_shards else m // 8
    outs, routs = _pallas(
        body, [a, b], name=name, grid=grid,
        in_specs=[pl.BlockSpec((tokens, bm), lambda i, j: (0, i)), pl.BlockSpec((tokens, bn), lambda i, j: (0, j))],
        out_specs=[out_spec], out_shape=[jax.ShapeDtypeStruct((N_CHIPS, 2, out_rows, bn), F32)],
        vmem_mib=56, riders=riders)
    return outs[0], routs


def _wgrad_pair(a, b, name, *, col_shards, riders=()):
    tokens, m = a.shape
    n = b.shape[1]
    kc = 512
    c0 = lax.axis_index("c")

    def half(ph, pre):
        return (ph + 1 + pre[0]) % 2

    if col_shards:
        bm, bn = m // 2, n // N_CHIPS
        a_spec = pl.BlockSpec((tokens, bm), lambda ph, q, pre: (0, half(ph, pre)))
        b_spec = pl.BlockSpec((tokens, bn), lambda ph, q, pre: (0, q))
    else:
        bm, bn = m // 8, n
        a_spec = pl.BlockSpec((tokens, bm), lambda ph, q, pre: (0, 2 * q + half(ph, pre)))
        b_spec = pl.BlockSpec((tokens, bn), lambda ph, q, pre: (0, 0))

    def body(pre_ref, a_ref, b_ref, o_ref, give, got, send_sems, recv_sems):
        ph, q = pl.program_id(0), pl.program_id(1)
        acc = jnp.zeros((bm, bn), F32)
        for k0 in range(0, tokens, kc):
            acc = acc + _dot_tn(a_ref[k0:k0 + kc, :].astype(BF16), b_ref[k0:k0 + kc, :].astype(BF16))
        x, y, cc = _mesh_pos()

        def tile(t):
            return _remote(give.at[t], got.at[t], send_sems.at[t], recv_sems.at[t], (x, y, 1 - cc))

        @pl.when(ph == 0)
        def _():
            give[q] = acc
            tile(q).start()

        @pl.when(ph == 1)
        def _():
            tile(q).wait_recv()
            o_ref[...] = (acc + got[q]).astype(BF16)

        @pl.when((ph == 1) & (q == N_CHIPS - 1))
        def _():
            for t in range(N_CHIPS):
                tile(t).wait_send()

    outs, routs = _pallas(
        body, [a, b], name=name, grid=(2, N_CHIPS), in_specs=[a_spec, b_spec],
        out_specs=[pl.BlockSpec((None, bm, bn), lambda ph, q, pre: (ph * q, 0, 0))],
        out_shape=[jax.ShapeDtypeStruct((N_CHIPS, bm, bn), BF16)],
        scratch_shapes=[pltpu.VMEM((N_CHIPS, bm, bn), F32), pltpu.VMEM((N_CHIPS, bm, bn), F32),
                        pltpu.SemaphoreType.DMA((N_CHIPS,)), pltpu.SemaphoreType.DMA((N_CHIPS,))],
        vmem_mib=56, riders=riders, prefetch=jnp.reshape(c0, (1,)).astype(jnp.int32))
    return outs[0], routs


class _GradReduce:
    def __init__(self, name, grad=None, chip_sum=None):
        self.name, self.grad, self.chip_sum = name, grad, chip_sum
        self.full = None

    def pair_swap(self):
        return _PairSwap([self.grad])

    def took_pair(self, outs):
        self.chip_sum = _in_hbm(_add_pair(self.grad, outs[0], f"pair_sum_{self.name}"))

    def took_chips(self, outs):
        self.full = _in_hbm(_add_chips(self.chip_sum, outs[0], f"chip_sum_{self.name}"))

    def chips_beside(self, collective_id):
        self.took_chips([_chip_swap_beside(self.chip_sum, f"chip_swap_{self.name}", collective_id)])

    def pair_share(self):
        return _PairShare([self.full])

    def took_share(self, outs):
        self.full = outs[0]

    def reduced(self):
        return jnp.reshape(self.full, (2 * self.full.shape[1], self.full.shape[2]))


def _forward_backward(x2, tgt2, w, conv_a_w, conv_b_w, od_norm, od_bias, od_lng, od_lnb,
                      ev_norm_g, ev_conv_a_b, ev_ln_a_g, ev_ln_a_b, od_w_s, od_b_s, mlp_norm_g, final_norm_g,
                      *, tm, seq, distributed=True):
    d = x2.shape[1]
    b_s_rows = jnp.broadcast_to(od_b_s[0][:, :, None], (C_GROUPS, CHUNK, CHUNK))
    (h1, n0, z, a2, cv, mix), _ = _fwd_even(
        x2, ev_norm_g, w["ev_in"], conv_a_w, ev_conv_a_b, ev_ln_a_g, ev_ln_a_b, conv_b_w, w["ev_out"], tm=tm, seq=seq)
    (h2, n1, p0, q0), _ = _fwd_mlp(h1, mlp_norm_g[0:1], w["w1_0"], w["w2_0"], 0, tm=tm)
    (h3, n2, s, cdf, sv, y), _ = _fwd_odd(h2, od_norm, w["od_in"], od_bias, od_lng, od_lnb, od_w_s[0], b_s_rows,
                                          w["od_out"], tm=tm)
    (n3, p1, q1, loss_part, dh4, dh4b, d_final_g), _ = _fwd_mlp(
        h3, mlp_norm_g[1:2], w["w1_1"], w["w2_1"], 1, tm=tm,
        head=(jnp.reshape(final_norm_g, (1, d)), tgt2))

    red = {}

    def swap(*names):
        return [red[nm].pair_swap() for nm in names] if distributed else []

    def share(*names):
        return [red[nm].pair_share() for nm in names] if distributed else []

    def took(routs, *steps):
        if distributed:
            for (nm, what), outs in zip(steps, routs):
                getattr(red[nm], what)(outs)

    swap_ids = iter(range(FIRST_SWAP_ID, FIRST_SWAP_ID + 8))

    def beside(name):
        if distributed:
            red[name].chips_beside(next(swap_ids))

    def big(lhs, rhs, name, col_shards, riders=()):
        if distributed:
            chip_sum, routs = _wgrad_pair(lhs, rhs, f"wgrad_{name}", col_shards=col_shards, riders=riders)
            red[name] = _GradReduce(name, chip_sum=_in_hbm(chip_sum))
        else:
            g, routs = _wgrad(lhs, rhs, f"wgrad_{name}", col_shards=col_shards)
            red[name] = _GradReduce(name, grad=g)
        return routs

    big(q1, dh4b, "w2_1", False)
    beside("w2_1")
    (dh3, dh3b, dp1, d_mlp_g1), _ = _bwd_mlp(dh4, h3, mlp_norm_g[1:2], p1, w["w1_1"], w["w2_1"], 1, tm=tm)
    big(n3, dp1, "w1_1", True)
    beside("w1_1")
    g, routs = _wgrad(y, dh3b, "wgrad_od_out", col_shards=False, riders=share("w2_1"))
    red["od_out"] = _GradReduce("od_out", grad=g)
    took(routs, ("w2_1", "took_share"))
    (dh2, dh2b, ds, d_od_norm, d_od_bin, d_od_lng, d_od_lnb, d_ws, d_bs), _ = _bwd_odd(
        dh3, h2, od_norm, s, cdf, sv, w["od_in"], od_lng, od_lnb, od_w_s[0], w["od_out"], tm=tm)
    routs = big(n2, ds, "od_in", True, riders=share("w1_1") + swap("od_out"))
    took(routs, ("w1_1", "took_share"), ("od_out", "took_pair"))
    beside("od_in")
    beside("od_out")
    half_groups = C_GROUPS // 2
    early = {"loss": loss_part, "od_w_s_lo": d_ws[:half_groups], "od_b_s": d_bs, "mlp_norm_g1": d_mlp_g1, "final_norm_g": d_final_g,
             "od_norm_g": d_od_norm, "od_b_in": d_od_bin, "od_ln_v_g": d_od_lng, "od_ln_v_b": d_od_lnb}
    share_early = [_ShareAll(list(early.values()))] if distributed else []
    routs = big(q0, dh2b, "w2_0", False, riders=share_early)
    landed_early = routs[0] if distributed else []
    beside("w2_0")
    (dh1, dh1b, dp0, d_mlp_g0), _ = _bwd_mlp(dh2, h1, mlp_norm_g[0:1], p0, w["w1_0"], w["w2_0"], 0, tm=tm)
    middle = {"od_w_s_hi": d_ws[half_groups:]}
    share_middle = [_ShareAll(list(middle.values()))] if distributed else []
    g, _ = _wgrad(mix, dh1b, "wgrad_ev_out", col_shards=False)
    red["ev_out"] = _GradReduce("ev_out", grad=g)
    routs = big(n1, dp0, "w1_0", True,
                riders=share("od_out") + share("od_in") + share("w2_0") + swap("ev_out") + share_middle)
    took(routs, ("od_out", "took_share"), ("od_in", "took_share"), ("w2_0", "took_share"), ("ev_out", "took_pair"))
    landed_middle = routs[4] if distributed else []
    beside("w1_0")
    beside("ev_out")

    (dx, dz, d_ev_norm, d_caw, d_cab, d_ev_lng, d_ev_lnb, d_cbw), _ = _bwd_even(
        dh1, x2, ev_norm_g, z, a2, cv, w["ev_in"], conv_a_w, ev_ln_a_g, ev_ln_a_b, conv_b_w, w["ev_out"], tm=tm, seq=seq)
    late = {"mlp_norm_g0": d_mlp_g0, "ev_norm_g": d_ev_norm, "ev_conv_a_b": d_cab, "ev_ln_a_g": d_ev_lng,
            "ev_ln_a_b": d_ev_lnb, "ev_conv_a_w": d_caw, "ev_conv_b_w": d_cbw}
    share_late = [_ShareAll(list(late.values()))] if distributed else []
    routs = big(n0, dz, "ev_in", True, riders=share("ev_out") + share("w1_0") + share_late)
    took(routs, ("ev_out", "took_share"), ("w1_0", "took_share"))
    beside("ev_in")
    own = {**early, **middle, **late}
    landed = dict(zip(own.keys(), landed_early + landed_middle + routs[2])) if distributed else None
    return dx, red, own, landed


def _rows128(a):
    rows = jnp.reshape(a, (-1, LANES))
    pad = (-rows.shape[0]) % SUBLANES
    return jnp.pad(rows, ((0, pad), (0, 0))) if pad else rows


def _pack(arrays):
    return jnp.concatenate([_rows128(a) for a in arrays], axis=0)


def _unpack(buf, shapes):
    out, r0 = [], 0
    for shp in shapes:
        size = 1
        for dim in shp:
            size *= dim
        nr = size // LANES
        out.append(jnp.reshape(buf[r0:r0 + nr], shp))
        r0 += nr + (-nr) % SUBLANES
    return out


def kernel(x, ev_norm_g, ev_w_in, ev_conv_a_w, ev_conv_a_b, ev_ln_a_g, ev_ln_a_b, ev_conv_b_w, ev_w_out, od_norm_g, od_w_in, od_b_in, od_ln_v_g, od_ln_v_b, od_w_s, od_b_s, od_w_out, mlp_norm_g, mlp_w1, mlp_w2, final_norm_g, loss_target, m_ev_norm_g, m_ev_w_in, m_ev_conv_a_w, m_ev_conv_a_b, m_ev_ln_a_g, m_ev_ln_a_b, m_ev_conv_b_w, m_ev_w_out, m_od_norm_g, m_od_w_in, m_od_b_in, m_od_ln_v_g, m_od_ln_v_b, m_od_w_s, m_od_b_s, m_od_w_out, m_mlp_norm_g, m_mlp_w1, m_mlp_w2, m_final_norm_g, v_ev_norm_g, v_ev_w_in, v_ev_conv_a_w, v_ev_conv_a_b, v_ev_ln_a_g, v_ev_ln_a_b, v_ev_conv_b_w, v_ev_w_out, v_od_norm_g, v_od_w_in, v_od_b_in, v_od_ln_v_g, v_od_ln_v_b, v_od_w_s, v_od_b_s, v_od_w_out, v_mlp_norm_g, v_mlp_w1, v_mlp_w2, v_final_norm_g):
    tm = TOKEN_TILE
    batch, seq, d = x.shape
    tokens = batch * seq
    x2 = jnp.reshape(x, (tokens, d))
    tgt2 = jnp.reshape(loss_target, (tokens, d))
    chip = 2 * lax.axis_index("x") + lax.axis_index("y")

    small_shapes = [(A_CONV_WIDTH, LANES), (B_CONV_WIDTH, LANES), (256,), (512,), (256,), (256,)]
    small_shard = _pack([ev_conv_a_w[0], ev_conv_b_w[0], od_norm_g[0], od_b_in[0], od_ln_v_g[0], od_ln_v_b[0]])
    small_shard = jnp.pad(small_shard, ((0, (-small_shard.shape[0]) % (4 * SUBLANES)), (0, 0)))
    first = [_place_shard(ev_w_in, 0, BF16, "place_ev_w_in"), _place_shard(ev_w_out, 0, BF16, "place_ev_w_out"),
             _place_shard(small_shard[None], 0, F32, "place_small")]
    staged = {
        "w1_0": _place_shard(mlp_w1, 0, BF16, "place_w1_0"), "w2_0": _place_shard(mlp_w2, 0, BF16, "place_w2_0"),
        "od_in": _place_shard(od_w_in, 0, BF16, "place_od_w_in"), "od_out": _place_shard(od_w_out, 0, BF16, "place_od_w_out"),
        "w1_1": _place_shard(mlp_w1, 1, BF16, "place_w1_1"), "w2_1": _place_shard(mlp_w2, 1, BF16, "place_w2_1"),
    }
    first = [_in_hbm(a) for a in first]
    staged = {nm: _in_hbm(a) for nm, a in staged.items()}
    g_ev_in, g_ev_out, g_small = _gather_beside(first, "gather_stage0", collective_id=1)
    gathered = {"ev_in": g_ev_in, "ev_out": g_ev_out}
    for stage, names in enumerate((("w1_0", "w2_0"), ("od_in", "od_out", "w1_1"), ("w2_1",))):
        done = _gather_beside([staged[nm] for nm in names], f"gather_stage{stage + 1}", collective_id=stage + 2)
        gathered.update(zip(names, done))
    small_all = jnp.reshape(_plain_copy(g_small, "small_weights_copy"), (N_CHIPS, -1, LANES))
    per_chip = [_unpack(small_all[q], small_shapes) for q in range(N_CHIPS)]
    conv_a_w = jnp.concatenate([pc[0] for pc in per_chip], axis=1)
    conv_b_w = jnp.concatenate([pc[1] for pc in per_chip], axis=1)
    od_norm = jnp.concatenate([pc[2] for pc in per_chip])[None, :]
    od_bias = jnp.concatenate([pc[3] for pc in per_chip])[None, :]
    od_lng = jnp.concatenate([pc[4] for pc in per_chip])[None, :]
    od_lnb = jnp.concatenate([pc[5] for pc in per_chip])[None, :]

    dx, red, own, landed = _forward_backward(
        x2, tgt2, gathered, conv_a_w, conv_b_w, od_norm, od_bias, od_lng, od_lnb,
        ev_norm_g, ev_conv_a_b, ev_ln_a_g, ev_ln_a_b, od_w_s, od_b_s, mlp_norm_g, final_norm_g, tm=tm, seq=seq)

    routs = _exchange([red["ev_in"].pair_share()], "reduce_tail")
    red["ev_in"].took_share(routs[0])

    given = {"ev_norm_g": (ev_norm_g, m_ev_norm_g, v_ev_norm_g), "ev_conv_a_b": (ev_conv_a_b, m_ev_conv_a_b, v_ev_conv_a_b),
             "ev_ln_a_g": (ev_ln_a_g, m_ev_ln_a_g, v_ev_ln_a_g), "ev_ln_a_b": (ev_ln_a_b, m_ev_ln_a_b, v_ev_ln_a_b),
             "od_w_s": (od_w_s, m_od_w_s, v_od_w_s), "od_b_s": (od_b_s, m_od_b_s, v_od_b_s),
             "mlp_norm_g": (mlp_norm_g, m_mlp_norm_g, v_mlp_norm_g), "final_norm_g": (final_norm_g, m_final_norm_g, v_final_norm_g),
             "ev_conv_a_w": (ev_conv_a_w, m_ev_conv_a_w, v_ev_conv_a_w), "ev_conv_b_w": (ev_conv_b_w, m_ev_conv_b_w, v_ev_conv_b_w),
             "od_norm_g": (od_norm_g, m_od_norm_g, v_od_norm_g), "od_b_in": (od_b_in, m_od_b_in, v_od_b_in),
             "od_ln_v_g": (od_ln_v_g, m_od_ln_v_g, v_od_ln_v_g), "od_ln_v_b": (od_ln_v_b, m_od_ln_v_b, v_od_ln_v_b)}
    shaped = {nm: tuple(jnp.reshape(a, shape) for a in given[nm]) for nm, shape, _, _ in SMALL_WEIGHTS}
    loss11, small_upd = _small_update(own, landed, shaped)
    loss = loss11[0, 0]
    upd = {nm: [jnp.reshape(o, given[nm][0].shape) for o in outs] for nm, outs in small_upd.items()}

    def big_update(wt, m, v, names, call):
        grads = [red[nm].reduced() for nm in names]
        shp3 = (len(grads),) + grads[0].shape
        outs = _adamw(jnp.reshape(wt, shp3), jnp.reshape(m, shp3), jnp.reshape(v, shp3), grads, call)
        return [jnp.reshape(o, wt.shape) for o in outs]

    wide = {"mlp_w2": (mlp_w2, m_mlp_w2, v_mlp_w2, ["w2_0", "w2_1"]), "mlp_w1": (mlp_w1, m_mlp_w1, v_mlp_w1, ["w1_0", "w1_1"]),
            "ev_w_out": (ev_w_out, m_ev_w_out, v_ev_w_out, ["ev_out"]), "od_w_out": (od_w_out, m_od_w_out, v_od_w_out, ["od_out"])}
    streamed = []
    for wt, m, v, names in wide.values():
        grads = [red[nm].reduced() for nm in names]
        shp3 = (len(grads),) + grads[0].shape
        streamed.append((jnp.reshape(wt, shp3), jnp.reshape(m, shp3), jnp.reshape(v, shp3), grads))
    for (nm, (wt, _, _, _)), outs in zip(wide.items(), _adamw_stream(streamed, "adamw_wide")):
        upd[nm] = [jnp.reshape(o, wt.shape) for o in outs]
    upd["ev_w_in"] = big_update(ev_w_in, m_ev_w_in, v_ev_w_in, ["ev_in"], "adamw_ev_w_in")
    upd["od_w_in"] = big_update(od_w_in, m_od_w_in, v_od_w_in, ["od_in"], "adamw_od_w_in")

    order = ["ev_norm_g", "ev_w_in", "ev_conv_a_w", "ev_conv_a_b", "ev_ln_a_g", "ev_ln_a_b", "ev_conv_b_w", "ev_w_out",
             "od_norm_g", "od_w_in", "od_b_in", "od_ln_v_g", "od_ln_v_b", "od_w_s", "od_b_s", "od_w_out", "mlp_norm_g",
             "mlp_w1", "mlp_w2", "final_norm_g"]
    grad_x = jnp.reshape(dx, x.shape)
    return (loss, grad_x, *[upd[nm][0] for nm in order], *[upd[nm][1] for nm in order],
            *[upd[nm][2] for nm in order], *[upd[nm][3] for nm in order])
```

```python
import functools

import jax
import jax.numpy as jnp
from jax import lax
from jax.experimental import pallas as pl
from jax.experimental.pallas import tpu as pltpu
from jax.experimental.pallas import tpu_sc as plsc

F32 = jnp.float32
BF16 = jnp.bfloat16

D_MODEL = 1024
A_DIM = 512
B_DIM = 512
IN_EVEN = 2 * A_DIM + 3 * B_DIM
A_CONV_WIDTH = 31
B_CONV_WIDTH = 3
CHUNK = 128
C_GROUPS = 8
C_DIM = 1024
D_FF = 4096
RMS_EPS = 1e-6
LN_EPS = 1e-5
ADAM_LR = 0.001
ADAM_B1 = 0.9
ADAM_B2 = 0.999
ADAM_EPS = 1e-08
ADAM_WD = 0.01
ADAM_STEP = 10

N_CHIPS = 4
N_DEV = 8
TOKEN_TILE = 512
A_HALO = 32
B_HALO = 8
CONV_ROWS = 16
DW_TAPS = 4
ELEM_ROWS = 16
PAIR = 2 * CHUNK
LANES = 128
SUBLANES = 8
MXU_ROWS = 256
MIB = 1024 * 1024
MESH = pl.DeviceIdType.MESH
ANY = pl.BlockSpec(memory_space=pl.ANY)


def _dot(a, b):
    return lax.dot_general(a, b, (((1,), (0,)), ((), ())), preferred_element_type=F32)


def _dot_nt(a, b):
    return lax.dot_general(a, b, (((1,), (1,)), ((), ())), preferred_element_type=F32)


def _dot_tn(a, b):
    return lax.dot_general(a, b, (((0,), (0,)), ((), ())), preferred_element_type=F32)


def _params(vmem_mib, n_axes=1):
    return pltpu.CompilerParams(dimension_semantics=("arbitrary",) * n_axes, vmem_limit_bytes=vmem_mib * MIB)


def _row_spec(tm, cols, rev_nt=None):
    if rev_nt is None:
        return pl.BlockSpec((tm, cols), lambda i: (i, 0))
    return pl.BlockSpec((tm, cols), lambda i: (rev_nt - 1 - i, 0))


def _full_spec(shape):
    nd = len(shape)
    return pl.BlockSpec(shape, lambda i: (0,) * nd)


def _block_rows(rows, cap=512):
    best = SUBLANES
    for br in range(SUBLANES, min(rows, cap) + 1, SUBLANES):
        if rows % br == 0:
            best = br
    return best


FIRST_SWAP_ID = 5
N_LOADS = 2 * 2 * N_CHIPS


def _load_weights(loads, sems):
    @pl.when(pl.program_id(0) == 0)
    def _():
        copies = []
        for src, dst, rows_of_one in loads:
            r = src.shape[2]
            for q in range(N_CHIPS):
                for h in range(2):
                    part = dst.at[pl.ds((2 * q + h) * r, r)] if rows_of_one else dst.at[q, pl.ds(h * r, r)]
                    copies.append(pltpu.make_async_copy(src.at[q, h], part, sems.at[len(copies)]))
        for k, cp in enumerate(copies):
            cp.start(priority=k % 2)
        for cp in copies:
            cp.wait()


def _rms_fwd(x, g):
    rstd = lax.rsqrt(jnp.mean(x * x, axis=-1, keepdims=True) + RMS_EPS)
    return x * rstd * g, rstd


def _rms_bwd(dn, x, rstd, g):
    a = dn * g
    xh = x * rstd
    dx = rstd * (a - xh * jnp.mean(a * xh, axis=-1, keepdims=True))
    dg = jnp.sum(dn * xh, axis=0, keepdims=True)
    return dx, dg


def _ln_stats(v):
    mu = jnp.mean(v, axis=-1, keepdims=True)
    xc = v - mu
    rs = lax.rsqrt(jnp.mean(xc * xc, axis=-1, keepdims=True) + LN_EPS)
    return xc * rs, rs


def _ln_bwd(dy, xhat, rs, g):
    dxh = dy * g
    dv = rs * (dxh - jnp.mean(dxh, axis=-1, keepdims=True) - xhat * jnp.mean(dxh * xhat, axis=-1, keepdims=True))
    return dv, jnp.sum(dy * xhat, axis=0, keepdims=True), jnp.sum(dy, axis=0, keepdims=True)


def _gelu_cdf(s):
    return 0.5 * (1.0 + lax.erf(s * 0.7071067811865476))


def _mesh_pos():
    return lax.axis_index("x"), lax.axis_index("y"), lax.axis_index("c")


def _other_chips(x, y):
    return [(1 - x, y), (x, 1 - y), (1 - x, 1 - y)]


def _remote(src, dst, send_sem, recv_sem, to):
    return pltpu.make_async_remote_copy(src_ref=src, dst_ref=dst, send_sem=send_sem, recv_sem=recv_sem,
                                        device_id=to, device_id_type=MESH)


def _like(arrays):
    return [jax.ShapeDtypeStruct(a.shape, a.dtype) for a in arrays]


class _PairSwap:
    def __init__(self, grads):
        self.ins = list(grads)
        self.out_shapes = [jax.ShapeDtypeStruct((g.shape[0],) + g.shape[2:], g.dtype) for g in grads]
        self.aliases = {}
        self.n_sems = len(grads)

    def _copies(self, ins, outs, send, recv):
        x, y, c = _mesh_pos()
        return [_remote(ins[t].at[:, 1 - c], outs[t], send.at[t], recv.at[t], (x, y, 1 - c)) for t in range(len(ins))]

    def start(self, ins, outs, send, recv):
        for cp in self._copies(ins, outs, send, recv):
            cp.start()

    def finish(self, ins, outs, send, recv):
        for cp in self._copies(ins, outs, send, recv):
            cp.wait()


class _ChipSwap:
    def __init__(self, parts):
        self.ins = list(parts)
        self.out_shapes = [jax.ShapeDtypeStruct((3,) + p.shape[1:], p.dtype) for p in parts]
        self.aliases = {}
        self.n_sems = 3 * len(parts)

    def _copies(self, ins, outs, send, recv):
        x, y, c = _mesh_pos()
        return [_remote(ins[t].at[2 * chip[0] + chip[1]], outs[t].at[k], send.at[3 * t + k], recv.at[3 * t + k], (*chip, c))
                for t in range(len(ins)) for k, chip in enumerate(_other_chips(x, y))]

    def start(self, ins, outs, send, recv):
        for cp in self._copies(ins, outs, send, recv):
            cp.start()

    def finish(self, ins, outs, send, recv):
        for cp in self._copies(ins, outs, send, recv):
            cp.wait()


class _PairShare:
    def __init__(self, fulls):
        self.ins = list(fulls)
        self.out_shapes = _like(fulls)
        self.aliases = {t: t for t in range(len(fulls))}
        self.n_sems = len(fulls)

    def _copies(self, ins, outs, send, recv):
        x, y, c = _mesh_pos()
        return [_remote(ins[t].at[c], outs[t].at[c], send.at[t], recv.at[t], (x, y, 1 - c)) for t in range(len(ins))]

    def start(self, ins, outs, send, recv):
        for cp in self._copies(ins, outs, send, recv):
            cp.start()

    def finish(self, ins, outs, send, recv):
        for cp in self._copies(ins, outs, send, recv):
            cp.wait()


class _ShareAll:
    def __init__(self, arrays):
        self.ins = list(arrays)
        self.out_shapes = [jax.ShapeDtypeStruct((N_DEV,) + a.shape, a.dtype) for a in arrays]
        self.aliases = {}
        self.n_sems = (N_DEV - 1) * len(arrays)

    def _peers(self):
        x, y, c = _mesh_pos()
        flips = [((r >> 2) & 1, (r >> 1) & 1, r & 1) for r in range(1, N_DEV)]
        return (x, y, c), [(x ^ fx, y ^ fy, c ^ fc) for fx, fy, fc in flips]

    def _sends(self, ins, outs, send, recv):
        (x, y, c), peers = self._peers()
        mine = 4 * x + 2 * y + c
        return [_remote(ins[a], outs[a].at[mine], send.at[7 * a + r], recv.at[7 * a + r], peer)
                for a in range(len(ins)) for r, peer in enumerate(peers)]

    def start(self, ins, outs, send, recv):
        for cp in self._sends(ins, outs, send, recv):
            cp.start()

    def finish(self, ins, outs, send, recv):
        (x, y, c), peers = self._peers()
        for a in range(len(ins)):
            for r, (px, py, pc) in enumerate(peers):
                blk = outs[a].at[4 * px + 2 * py + pc]
                _remote(blk, blk, send.at[7 * a + r], recv.at[7 * a + r], (x, y, c)).wait_recv()
        for cp in self._sends(ins, outs, send, recv):
            cp.wait_send()


def _gather_beside(bufs, name, collective_id):
    n = len(bufs)
    per = 7
    refs = [jax.new_ref(b, memory_space=pltpu.MemorySpace.HBM) for b in bufs]

    @pl.kernel(mesh=plsc.ScalarSubcoreMesh(axis_name="sequencer", num_cores=1), name=name,
               scratch_types=(pltpu.SemaphoreType.DMA((per * n,)), pltpu.SemaphoreType.DMA((per * n,))),
               compiler_params=pltpu.CompilerParams(collective_id=collective_id))
    def launch(send, recv):
        x, y, c = _mesh_pos()
        me, sibling = (x, y, c), (x, y, 1 - c)
        x_nbr, y_nbr = (1 - x, y, c), (x, 1 - y, c)
        mine, via_x, via_y, diag = 2 * x + y, 2 * (1 - x) + y, 2 * x + (1 - y), 2 * (1 - x) + (1 - y)
        barrier = pltpu.get_barrier_semaphore()
        peers = [x_nbr, y_nbr, sibling]
        for peer in peers:
            pl.semaphore_signal(barrier, inc=1, device_id=peer, device_id_type=MESH)
        pl.semaphore_wait(barrier, len(peers))

        def copy(t, k, src, dst, to):
            return _remote(src, dst, send.at[per * t + k], recv.at[per * t + k], to)

        def piece(t, chip, half, rows=None):
            blk = refs[t].at[chip, half]
            return blk if rows is None else blk.at[rows]

        started = []

        def go(cp):
            cp.start()
            started.append(cp)

        upper = [pl.ds(0, r.shape[2] // 2) for r in refs]
        lower = [pl.ds(r.shape[2] // 2, r.shape[2] // 2) for r in refs]
        for t in range(n):
            go(copy(t, 0, piece(t, mine, c), piece(t, mine, c), x_nbr))
            go(copy(t, 1, piece(t, mine, c), piece(t, mine, c), y_nbr))
        for t in range(n):
            copy(t, 0, piece(t, via_x, c), piece(t, via_x, c), me).wait_recv()
            go(copy(t, 2, piece(t, via_x, c, upper[t]), piece(t, via_x, c, upper[t]), y_nbr))
            go(copy(t, 4, piece(t, via_x, c), piece(t, via_x, c), sibling))
            copy(t, 1, piece(t, via_y, c), piece(t, via_y, c), me).wait_recv()
            go(copy(t, 3, piece(t, via_y, c, lower[t]), piece(t, via_y, c, lower[t]), x_nbr))
            go(copy(t, 5, piece(t, via_y, c), piece(t, via_y, c), sibling))
        for t in range(n):
            copy(t, 2, piece(t, diag, c, upper[t]), piece(t, diag, c, upper[t]), me).wait_recv()
            copy(t, 3, piece(t, diag, c, lower[t]), piece(t, diag, c, lower[t]), me).wait_recv()
            go(copy(t, 6, piece(t, diag, c), piece(t, diag, c), sibling))
        for t in range(n):
            for k, chip in ((4, via_x), (5, via_y), (6, diag)):
                copy(t, k, piece(t, chip, 1 - c), piece(t, chip, 1 - c), me).wait_recv()
        for cp in started:
            cp.wait_send()

    launch()
    return [r[...] for r in refs]


def _chip_swap_beside(parts, name, collective_id):
    src = jax.new_ref(parts, memory_space=pltpu.MemorySpace.HBM)
    dst = jax.empty_ref(jax.ShapeDtypeStruct((N_CHIPS - 1,) + parts.shape[1:], parts.dtype),
                        memory_space=pltpu.MemorySpace.HBM)
    swap = _ChipSwap([parts])

    @pl.kernel(mesh=plsc.ScalarSubcoreMesh(axis_name="sequencer", num_cores=1), name=name,
               scratch_types=(pltpu.SemaphoreType.DMA((N_CHIPS - 1,)), pltpu.SemaphoreType.DMA((N_CHIPS - 1,))),
               compiler_params=pltpu.CompilerParams(collective_id=collective_id))
    def launch(send, recv):
        x, y, c = _mesh_pos()
        barrier = pltpu.get_barrier_semaphore()
        peers = [(*chip, c) for chip in _other_chips(x, y)]
        for peer in peers:
            pl.semaphore_signal(barrier, inc=1, device_id=peer, device_id_type=MESH)
        pl.semaphore_wait(barrier, len(peers))
        swap.start([src], [dst], send, recv)
        swap.finish([src], [dst], send, recv)

    launch()
    return dst[...]


def _pallas(body, operands, *, name, grid, in_specs, out_specs, out_shape, scratch_shapes=(), vmem_mib=32, riders=(),
            prefetch=None):
    in_specs, out_specs, out_shape, scratch_shapes = list(in_specs), list(out_specs), list(out_shape), list(scratch_shapes)
    if not riders and prefetch is None:
        outs = pl.pallas_call(body, name=name, grid=grid, in_specs=in_specs, out_specs=out_specs, out_shape=out_shape,
                              scratch_shapes=scratch_shapes, compiler_params=_params(vmem_mib, len(grid)))(*operands)
        return list(outs), []
    n_in, n_out, n_scr = len(in_specs), len(out_specs), len(scratch_shapes)
    r_in = [len(r.ins) for r in riders]
    r_out = [len(r.out_shapes) for r in riders]
    steps = 1
    for g in grid:
        steps *= g

    n_pre = 0 if prefetch is None else 1

    def wrapped(*refs):
        refs = list(refs)
        pre, refs = refs[:n_pre], refs[n_pre:]
        ins, refs = refs[:n_in], refs[n_in:]
        rins = []
        for k in r_in:
            rins.append(refs[:k])
            refs = refs[k:]
        outs, refs = refs[:n_out], refs[n_out:]
        routs = []
        for k in r_out:
            routs.append(refs[:k])
            refs = refs[k:]
        scr, sems = refs[:n_scr], refs[n_scr:]
        step = 0
        for ax, g in enumerate(grid):
            step = step * g + pl.program_id(ax)

        def each(what):
            for j, r in enumerate(riders):
                getattr(r, what)(rins[j], routs[j], sems[2 * j], sems[2 * j + 1])

        if grid:
            pl.when(step == 0)(lambda: each("start"))
        else:
            each("start")
        body(*pre, *ins, *outs, *scr)
        if grid:
            pl.when(step == steps - 1)(lambda: each("finish"))
        else:
            each("finish")

    aliases, off_in, off_out = {}, n_pre + n_in, n_out
    for r, ki, ko in zip(riders, r_in, r_out):
        for i, o in r.aliases.items():
            aliases[off_in + i] = off_out + o
        off_in, off_out = off_in + ki, off_out + ko
    sems = []
    for r in riders:
        sems += [pltpu.SemaphoreType.DMA((r.n_sems,)), pltpu.SemaphoreType.DMA((r.n_sems,))]
    layout = dict(grid=grid, in_specs=in_specs + [ANY] * sum(r_in), out_specs=out_specs + [ANY] * sum(r_out),
                  scratch_shapes=scratch_shapes + sems)
    if prefetch is not None:
        layout = dict(grid_spec=pltpu.PrefetchScalarGridSpec(num_scalar_prefetch=1, **layout))
    res = pl.pallas_call(
        wrapped, name=name, **layout,
        out_shape=out_shape + [s for r in riders for s in r.out_shapes], input_output_aliases=aliases,
        compiler_params=pltpu.CompilerParams(dimension_semantics=("arbitrary",) * len(grid),
                                             vmem_limit_bytes=vmem_mib * MIB, has_side_effects=True),
    )(*([] if prefetch is None else [prefetch]), *operands, *[a for r in riders for a in r.ins])
    res = list(res)
    outs, res = res[:n_out], res[n_out:]
    routs = []
    for k in r_out:
        routs.append(res[:k])
        res = res[k:]
    return outs, routs


def _exchange(riders, name):
    return _pallas(lambda: None, [], name=name, grid=(), in_specs=[], out_specs=[], out_shape=[], riders=riders)[1]


def _in_hbm(a):
    return pltpu.with_memory_space_constraint(a, pltpu.HBM)


def _place_shard(w, layer, dtype, name):
    _, rows, cols = w.shape
    half = rows // 2
    br = _block_rows(half)
    nb = half // br
    mine = 2 * lax.axis_index("x") + lax.axis_index("y")

    def body(q_ref, w_ref, o_ref):
        o_ref[...] = w_ref[...].astype(dtype)

    return pl.pallas_call(
        body, name=name,
        grid_spec=pltpu.PrefetchScalarGridSpec(
            num_scalar_prefetch=1, grid=(2, nb),
            in_specs=[pl.BlockSpec((None, br, cols), lambda h, i, q: (layer, h * nb + i, 0))],
            out_specs=pl.BlockSpec((None, None, br, cols), lambda h, i, q: (q[0], h, i, 0))),
        out_shape=pltpu.HBM((N_CHIPS, 2, half, cols), dtype),
        compiler_params=_params(16, 2),
    )(jnp.reshape(mine, (1,)).astype(jnp.int32), _in_hbm(w))


def _plain_copy(a, name):
    def body(a_ref, o_ref):
        o_ref[...] = a_ref[...]

    vmem = pl.BlockSpec(memory_space=pltpu.VMEM)
    return pl.pallas_call(body, name=name, in_specs=[vmem], out_specs=vmem,
                          out_shape=jax.ShapeDtypeStruct(a.shape, a.dtype))(a)


def _add_pair(g, recv, name):
    _, _, r, cdim = g.shape
    br = _block_rows(r, 256)
    c = lax.axis_index("c")

    def body(c_ref, g_ref, r_ref, o_ref):
        o_ref[...] = (g_ref[...] + r_ref[...]).astype(BF16)

    return pl.pallas_call(
        body, name=name,
        grid_spec=pltpu.PrefetchScalarGridSpec(
            num_scalar_prefetch=1, grid=(N_CHIPS, r // br),
            in_specs=[pl.BlockSpec((None, None, br, cdim), lambda q, i, c_ref: (q, c_ref[0], i, 0)),
                      pl.BlockSpec((None, br, cdim), lambda q, i, c_ref: (q, i, 0))],
            out_specs=pl.BlockSpec((None, br, cdim), lambda q, i, c_ref: (q, i, 0))),
        out_shape=pltpu.HBM((N_CHIPS, r, cdim), BF16),
        compiler_params=_params(16, 2),
    )(jnp.reshape(c, (1,)).astype(jnp.int32), _in_hbm(g), _in_hbm(recv))


def _add_chips(own, recv, name):
    _, r, cdim = own.shape
    br = _block_rows(r, 256)
    x, y, c = _mesh_pos()

    def body(pos_ref, own_ref, r_ref, o_ref):
        acc = own_ref[...].astype(F32)
        for k in range(3):
            acc = acc + r_ref[k].astype(F32)
        o_ref[...] = acc

    return pl.pallas_call(
        body, name=name,
        grid_spec=pltpu.PrefetchScalarGridSpec(
            num_scalar_prefetch=1, grid=(r // br,),
            in_specs=[pl.BlockSpec((None, br, cdim), lambda i, pos: (pos[0], i, 0)),
                      pl.BlockSpec((3, br, cdim), lambda i, pos: (0, i, 0))],
            out_specs=pl.BlockSpec((None, br, cdim), lambda i, pos: (pos[1], i, 0))),
        out_shape=pltpu.HBM((2, r, cdim), F32),
        compiler_params=_params(16, 1),
    )(jnp.stack([2 * x + y, c]).astype(jnp.int32), _in_hbm(own), _in_hbm(recv))


def _adam_math(w, m, v, g):
    c1 = 1.0 / (1.0 - ADAM_B1 ** ADAM_STEP)
    c2 = 1.0 / (1.0 - ADAM_B2 ** ADAM_STEP)
    m_new = ADAM_B1 * m + (1.0 - ADAM_B1) * g
    v_new = ADAM_B2 * v + (1.0 - ADAM_B2) * (g * g)
    return -ADAM_LR * ((m_new * c1) / (jnp.sqrt(v_new * c2) + ADAM_EPS) + ADAM_WD * w), m_new, v_new


SMALL_WEIGHTS = [
    ("ev_norm_g", (1, D_MODEL), ["ev_norm_g"], None), ("ev_conv_a_b", (1, A_DIM), ["ev_conv_a_b"], None),
    ("ev_ln_a_g", (1, A_DIM), ["ev_ln_a_g"], None), ("ev_ln_a_b", (1, A_DIM), ["ev_ln_a_b"], None),
    ("od_w_s", (C_GROUPS, CHUNK, CHUNK), ["od_w_s_lo", "od_w_s_hi"], None), ("od_b_s", (C_GROUPS, CHUNK), ["od_b_s"], None),
    ("mlp_norm_g", (2, D_MODEL), ["mlp_norm_g0", "mlp_norm_g1"], None), ("final_norm_g", (1, D_MODEL), ["final_norm_g"], None),
    ("ev_conv_a_w", (A_CONV_WIDTH, A_DIM // N_CHIPS), ["ev_conv_a_w"], A_DIM // N_CHIPS),
    ("ev_conv_b_w", (B_CONV_WIDTH, B_DIM // N_CHIPS), ["ev_conv_b_w"], B_DIM // N_CHIPS),
    ("od_norm_g", (1, D_MODEL // N_CHIPS), ["od_norm_g"], D_MODEL // N_CHIPS),
    ("od_b_in", (1, 2 * C_DIM // N_CHIPS), ["od_b_in"], 2 * C_DIM // N_CHIPS),
    ("od_ln_v_g", (1, C_DIM // N_CHIPS), ["od_ln_v_g"], C_DIM // N_CHIPS),
    ("od_ln_v_b", (1, C_DIM // N_CHIPS), ["od_ln_v_b"], C_DIM // N_CHIPS),
]


def _small_update(own, landed, weights):
    names = list(own.keys())
    n_g, n_w = len(names), len(SMALL_WEIGHTS)

    def body(*refs):
        refs = list(refs)
        own_refs = dict(zip(names, refs[:n_g]))
        land_refs = dict(zip(names, refs[n_g:2 * n_g]))
        wmv = [refs[2 * n_g + 3 * i:2 * n_g + 3 * i + 3] for i in range(n_w)]
        o0 = 2 * n_g + 3 * n_w
        loss_ref = refs[o0]
        outs = [refs[o0 + 1 + 4 * i:o0 + 5 + 4 * i] for i in range(n_w)]
        acc = dict(zip(names, refs[o0 + 1 + 4 * n_w:]))
        x, y, c = _mesh_pos()
        mine, chip = 4 * x + 2 * y + c, 2 * x + y

        for nm in names:
            for d in range(N_DEV):
                def add(term, nm=nm, d=d):
                    acc[nm][...] = term if d == 0 else acc[nm][...] + term
                pl.when(mine == d)(lambda nm=nm, add=add: add(own_refs[nm][...]))
                pl.when(mine != d)(lambda nm=nm, d=d, add=add: add(land_refs[nm][d]))
        loss_ref[...] = acc["loss"][...]

        def update(i, rows, g):
            w_ref, m_ref, v_ref = wmv[i]
            delta, m_new, v_new = _adam_math(w_ref[rows], m_ref[rows], v_ref[rows], g)
            for ref, val in zip(outs[i], (g, delta, m_new, v_new)):
                ref[rows] = val

        for i, (_, shape, grads, per_chip) in enumerate(SMALL_WEIGHTS):
            for row, gname in enumerate(grads):
                per_grad = shape[0] // len(grads)
                rows = slice(row * per_grad, (row + 1) * per_grad)
                if per_chip is None:
                    update(i, rows, acc[gname][...])
                else:
                    for q in range(N_CHIPS):
                        pl.when(chip == q)(lambda i=i, rows=rows, gname=gname, q=q, per_chip=per_chip:
                                           update(i, rows, acc[gname][:, q * per_chip:(q + 1) * per_chip]))

    operands = [own[nm] for nm in names] + [landed[nm] for nm in names]
    for nm, _, _, _ in SMALL_WEIGHTS:
        operands += list(weights[nm])
    out_shape = [jax.ShapeDtypeStruct((1, 1), F32)]
    for _, shape, _, _ in SMALL_WEIGHTS:
        out_shape += [jax.ShapeDtypeStruct(shape, F32)] * 4
    res = pl.pallas_call(
        body, name="small_update", grid=(1,),
        in_specs=[_full_spec(a.shape) for a in operands], out_specs=[_full_spec(s.shape) for s in out_shape],
        out_shape=out_shape, scratch_shapes=[pltpu.VMEM(own[nm].shape, F32) for nm in names],
        compiler_params=_params(32, 1),
    )(*[_in_hbm(a) for a in operands])
    return res[0], {nm: res[1 + 4 * i:5 + 4 * i] for i, (nm, _, _, _) in enumerate(SMALL_WEIGHTS)}


def _adamw(w, m, v, grads, name):
    layers, r, cdim = w.shape
    br = _block_rows(r, 256 if cdim > LANES else 1024)
    blocks = r // br

    def body(*refs):
        w_ref, m_ref, v_ref = refs[:3]
        g_refs = refs[3:3 + layers]
        go_ref, d_ref, mo_ref, vo_ref = refs[3 + layers:]
        layer = pl.program_id(0)
        for l in range(layers):
            @pl.when(layer == l)
            def _(l=l):
                g = g_refs[l][...]
                go_ref[...] = g
                d_ref[...], mo_ref[...], vo_ref[...] = _adam_math(w_ref[...], m_ref[...], v_ref[...], g)

    spec3 = pl.BlockSpec((None, br, cdim), lambda l, i: (l, i, 0))
    g_specs = [pl.BlockSpec((br, cdim), lambda l, i, own=own: (jnp.clip(i + (l - own) * blocks, 0, blocks - 1), 0))
               for own in range(layers)]
    out = jax.ShapeDtypeStruct((layers, r, cdim), F32)
    outs, _ = _pallas(body, [_in_hbm(a) for a in (w, m, v, *grads)], name=name, grid=(layers, blocks),
                      in_specs=[spec3, spec3, spec3] + g_specs, out_specs=[spec3] * 4, out_shape=[out] * 4, vmem_mib=32)
    return outs


ADAMW_ROWS = 256
ADAMW_SLOTS = 3


def _adamw_stream(weights, name):
    cdim = weights[0][0].shape[2]
    flat = lambda a: jnp.reshape(a, (-1, cdim))
    operands, work = [], []
    for k, (w, m, v, grads) in enumerate(weights):
        layers, r, _ = w.shape
        assert r % ADAMW_ROWS == 0 and w.shape[2] == cdim, (name, w.shape)
        base = len(operands)
        operands += [flat(w), flat(m), flat(v), *grads]
        for l in range(layers):
            for b in range(0, r, ADAMW_ROWS):
                work.append((base, base + 3 + l, 4 * k, l * r + b, b))
    n_in = len(operands)

    def body(*refs):
        ins, outs = refs[:n_in], refs[n_in:n_in + 4 * len(weights)]
        buf_in, buf_out, sem_in, sem_out = refs[n_in + 4 * len(weights):]

        def reads(t):
            base, g_at, _, rows, g_rows = work[t]
            slot = t % ADAMW_SLOTS
            srcs = [ins[base + j].at[pl.ds(rows, ADAMW_ROWS)] for j in range(3)] + [ins[g_at].at[pl.ds(g_rows, ADAMW_ROWS)]]
            return [pltpu.make_async_copy(src, buf_in.at[slot, j], sem_in.at[slot, j]) for j, src in enumerate(srcs)]

        def writes(t):
            _, _, out_at, rows, _ = work[t]
            slot = t % ADAMW_SLOTS
            return [pltpu.make_async_copy(buf_out.at[slot, j], outs[out_at + j].at[pl.ds(rows, ADAMW_ROWS)],
                                          sem_out.at[slot, j]) for j in range(4)]

        for t in range(min(ADAMW_SLOTS - 1, len(work))):
            for cp in reads(t):
                cp.start()
        for t in range(len(work)):
            slot = t % ADAMW_SLOTS
            if t + ADAMW_SLOTS - 1 < len(work):
                for cp in reads(t + ADAMW_SLOTS - 1):
                    cp.start()
            for cp in reads(t):
                cp.wait()
            if t >= ADAMW_SLOTS:
                for cp in writes(t - ADAMW_SLOTS):
                    cp.wait()
            g = buf_in[slot, 3]
            buf_out[slot, 0] = g
            buf_out[slot, 1], buf_out[slot, 2], buf_out[slot, 3] = _adam_math(buf_in[slot, 0], buf_in[slot, 1],
                                                                             buf_in[slot, 2], g)
            for cp in writes(t):
                cp.start(priority=1)
        for t in range(max(0, len(work) - ADAMW_SLOTS), len(work)):
            for cp in writes(t):
                cp.wait()

    out_shape = [jax.ShapeDtypeStruct((w.shape[0] * w.shape[1], cdim), F32) for w, _, _, _ in weights for _ in range(4)]
    outs, _ = _pallas(body, [_in_hbm(a) for a in operands], name=name, grid=(1,), in_specs=[ANY] * n_in,
                      out_specs=[ANY] * len(out_shape), out_shape=out_shape,
                      scratch_shapes=[pltpu.VMEM((ADAMW_SLOTS, 4, ADAMW_ROWS, cdim), F32),
                                      pltpu.VMEM((ADAMW_SLOTS, 4, ADAMW_ROWS, cdim), F32),
                                      pltpu.SemaphoreType.DMA((ADAMW_SLOTS, 4)), pltpu.SemaphoreType.DMA((ADAMW_SLOTS, 4))],
                      vmem_mib=40)
    return [[jnp.reshape(o, w.shape) for o in outs[4 * k:4 * k + 4]] for k, (w, _, _, _) in enumerate(weights)]


def _fill_shifted(buf, rows):
    for b in range(1, SUBLANES):
        buf[b, 0:rows - SUBLANES, :] = buf[0, b:b + rows - SUBLANES, :]


def _window(buf, start, size):
    return buf[start % SUBLANES, start - start % SUBLANES:start - start % SUBLANES + size, :]


def _conv31(src, w_ref, r0, base, init):
    acc = init
    for k in range(A_CONV_WIDTH):
        acc = acc + w_ref[k:k + 1, :] * _window(src, base + k + r0, CONV_ROWS)
    return acc


def _fwd_even(x, norm_g, w_in, conv_a_w, conv_a_b, ln_g, ln_b, conv_b_w, w_out, *, tm, seq, riders=()):
    tokens = x.shape[0]
    nt, tps = tokens // tm, seq // tm

    def body(x_ref, g_ref, win_hbm, caw_ref, cab_ref, lng_ref, lnb_ref, cbw_ref, wout_hbm,
             h_ref, n_ref, z_ref, a2_ref, cv_ref, mix_ref, win_v, wout_v, pa, pb, sem):
        i = pl.program_id(0)

        _load_weights([(win_hbm, win_v, False), (wout_hbm, wout_v, True)], sem)

        xv = x_ref[...]
        nf, _ = _rms_fwd(xv, g_ref[...])
        n = nf.astype(BF16)
        n_ref[...] = n
        z = jnp.concatenate([_dot(n, win_v[j]) for j in range(N_CHIPS)], axis=1)
        z_ref[...] = z.astype(BF16)
        a_val, a_gate = z[:, 0:A_DIM], z[:, A_DIM:2 * A_DIM]
        b_gate, c_gate, b_val = z[:, 1024:1536], z[:, 1536:2048], z[:, 2048:2560]

        first = (i % tps) == 0

        @pl.when(first)
        def _():
            pa[0, 0:A_HALO, :] = jnp.zeros((A_HALO, A_DIM), F32)
            pb[0:B_HALO, :] = jnp.zeros((B_HALO, B_DIM), F32)

        @pl.when(jnp.logical_not(first))
        def _():
            pa[0, 0:A_HALO, :] = pa[0, tm:tm + A_HALO, :]
            pb[0:B_HALO, :] = pb[tm:tm + B_HALO, :]

        pa[0, A_HALO:A_HALO + tm, :] = a_val * jax.nn.sigmoid(a_gate)
        pb[B_HALO:B_HALO + tm, :] = c_gate * b_val
        _fill_shifted(pa, A_HALO + tm)
        bias = jnp.broadcast_to(cab_ref[...], (CONV_ROWS, A_DIM))
        for r0 in range(0, tm, CONV_ROWS):
            a2_ref[r0:r0 + CONV_ROWS, :] = _conv31(pa, caw_ref, r0, A_HALO - (A_CONV_WIDTH - 1), bias)
        xhat, _ = _ln_stats(a2_ref[...])
        a3 = xhat * lng_ref[...] + lnb_ref[...]
        a4 = a3 * jax.nn.sigmoid(a3)
        cv = cbw_ref[0:1, :] * pb[B_HALO - 2:B_HALO - 2 + tm, :]
        cv = cv + cbw_ref[1:2, :] * pb[B_HALO - 1:B_HALO - 1 + tm, :]
        cv = cv + cbw_ref[2:3, :] * pb[B_HALO:B_HALO + tm, :]
        cv_ref[...] = cv.astype(BF16)
        mix = jnp.concatenate([a4, b_gate * cv], axis=1).astype(BF16)
        mix_ref[...] = mix
        h_ref[...] = xv + _dot(mix, wout_v[...])

    shp = lambda cols, dt: jax.ShapeDtypeStruct((tokens, cols), dt)
    return _pallas(
        body, [x, norm_g, w_in, conv_a_w, conv_a_b, ln_g, ln_b, conv_b_w, w_out], name="fwd_even", grid=(nt,),
        in_specs=[_row_spec(tm, D_MODEL), _full_spec((1, D_MODEL)), ANY, _full_spec((A_CONV_WIDTH, A_DIM)),
                  _full_spec((1, A_DIM)), _full_spec((1, A_DIM)), _full_spec((1, A_DIM)),
                  _full_spec((B_CONV_WIDTH, B_DIM)), ANY],
        out_specs=[_row_spec(tm, D_MODEL), _row_spec(tm, D_MODEL), _row_spec(tm, IN_EVEN), _row_spec(tm, A_DIM),
                   _row_spec(tm, B_DIM), _row_spec(tm, D_MODEL)],
        out_shape=[shp(D_MODEL, F32), shp(D_MODEL, BF16), shp(IN_EVEN, BF16), shp(A_DIM, F32), shp(B_DIM, BF16),
                   shp(D_MODEL, BF16)],
        scratch_shapes=[pltpu.VMEM((N_CHIPS, D_MODEL, IN_EVEN // N_CHIPS), BF16), pltpu.VMEM((D_MODEL, D_MODEL), BF16),
                        pltpu.VMEM((SUBLANES, A_HALO + tm, A_DIM), F32), pltpu.VMEM((B_HALO + tm, B_DIM), F32),
                        pltpu.SemaphoreType.DMA((N_LOADS,))],
        vmem_mib=56, riders=riders)


def _loss_tail(xv, g, target, loss_ref, dh_ref, dhb_ref, dg_ref):
    @pl.when(pl.program_id(0) == 0)
    def _():
        loss_ref[...] = jnp.zeros((1, 1), F32)
        dg_ref[...] = jnp.zeros((1, D_MODEL), F32)

    out, rstd = _rms_fwd(xv, g)
    err = out - target
    per_token = jnp.sum(err * err, axis=1, keepdims=True) * (1.0 / D_MODEL)
    loss_ref[...] += 0.5 * jnp.sum(per_token, axis=0, keepdims=True)
    dx, dg = _rms_bwd(err * (1.0 / D_MODEL), xv, rstd, g)
    dh_ref[...] = dx
    dhb_ref[...] = dx.astype(BF16)
    dg_ref[...] += dg


def _fwd_mlp(h, norm_g, w1, w2, layer, *, tm, riders=(), head=None):
    tokens = h.shape[0]
    nt = tokens // tm
    fs = D_FF // N_CHIPS
    n_in = 4 if head is None else 6

    def body(*refs):
        h_ref, g_ref, w1_hbm, w2_hbm = refs[:4]
        w1_v, w2_v, sem = refs[-3:]
        outs = refs[n_in:-3]
        n_ref, p_ref, q_ref = outs[1:4] if head is None else outs[0:3]
        _load_weights([(w1_hbm, w1_v, False), (w2_hbm, w2_v, False)], sem)

        xv = h_ref[...]
        nf, _ = _rms_fwd(xv, g_ref[...])
        n = nf.astype(BF16)
        n_ref[...] = n
        acc = xv
        for j in range(N_CHIPS):
            p = _dot(n, w1_v[j])
            p_ref[:, j * fs:(j + 1) * fs] = p.astype(BF16)
            r = jnp.maximum(p, 0.0)
            q = (r * r).astype(BF16)
            q_ref[:, j * fs:(j + 1) * fs] = q
            acc = acc + _dot(q, w2_v[j])
        if head is None:
            outs[0][...] = acc
        else:
            _loss_tail(acc, refs[4][...], refs[5][...], *outs[3:7])

    shp = lambda cols, dt: jax.ShapeDtypeStruct((tokens, cols), dt)
    saved_specs = [_row_spec(tm, D_MODEL), _row_spec(tm, D_FF), _row_spec(tm, D_FF)]
    saved_shapes = [shp(D_MODEL, BF16), shp(D_FF, BF16), shp(D_FF, BF16)]
    if head is None:
        operands, in_specs = [h, norm_g, w1, w2], [_row_spec(tm, D_MODEL), _full_spec((1, D_MODEL)), ANY, ANY]
        out_specs, out_shape = [_row_spec(tm, D_MODEL)] + saved_specs, [shp(D_MODEL, F32)] + saved_shapes
    else:
        operands = [h, norm_g, w1, w2, *head]
        in_specs = [_row_spec(tm, D_MODEL), _full_spec((1, D_MODEL)), ANY, ANY, _full_spec((1, D_MODEL)), _row_spec(tm, D_MODEL)]
        out_specs = saved_specs + [_full_spec((1, 1)), _row_spec(tm, D_MODEL), _row_spec(tm, D_MODEL), _full_spec((1, D_MODEL))]
        out_shape = saved_shapes + [jax.ShapeDtypeStruct((1, 1), F32), shp(D_MODEL, F32), shp(D_MODEL, BF16),
                                    jax.ShapeDtypeStruct((1, D_MODEL), F32)]
    return _pallas(
        body, operands, name=f"fwd_mlp{layer}", grid=(nt,), in_specs=in_specs, out_specs=out_specs, out_shape=out_shape,
        scratch_shapes=[pltpu.VMEM((N_CHIPS, D_MODEL, fs), BF16), pltpu.VMEM((N_CHIPS, fs, D_MODEL), BF16),
                        pltpu.SemaphoreType.DMA((N_LOADS,))],
        vmem_mib=56, riders=riders)


def _tril_mask():
    row = lax.broadcasted_iota(jnp.int32, (CHUNK, CHUNK), 0)
    col = lax.broadcasted_iota(jnp.int32, (CHUNK, CHUNK), 1)
    return row >= col


def _triu_mask():
    row = lax.broadcasted_iota(jnp.int32, (CHUNK, CHUNK), 0)
    col = lax.broadcasted_iota(jnp.int32, (CHUNK, CHUNK), 1)
    return row <= col


def _fwd_odd(h, norm_g, w_in, b_in, ln_g, ln_b, w_s, b_s_rows, w_out, *, tm, riders=()):
    tokens = h.shape[0]
    nt = tokens // tm
    cs = 2 * C_DIM // N_CHIPS

    def body(h_ref, g_ref, win_hbm, bin_ref, lng_ref, lnb_ref, ws_ref, bs_ref, wout_hbm,
             ho_ref, n_ref, s_ref, cdf_ref, sv_ref, y_ref, win_v, wout_v, bd, sem):
        _load_weights([(win_hbm, win_v, False), (wout_hbm, wout_v, True)], sem)

        @pl.when(pl.program_id(0) == 0)
        def _():
            mask = _tril_mask()
            bd[...] = jnp.zeros(bd.shape, BF16)
            for g in range(C_GROUPS):
                w = jnp.where(mask, ws_ref[g], 0.0).astype(BF16)
                bd[g, 0:CHUNK, 0:CHUNK] = w
                bd[g, CHUNK:PAIR, CHUNK:PAIR] = w

        xv = h_ref[...]
        nf, _ = _rms_fwd(xv, g_ref[...])
        n = nf.astype(BF16)
        n_ref[...] = n
        s = jnp.concatenate([_dot(n, win_v[j]) for j in range(N_CHIPS)], axis=1) + bin_ref[...]
        s_ref[...] = s.astype(BF16)
        cdf = _gelu_cdf(s)
        cdf_ref[...] = cdf.astype(BF16)
        zz = s * cdf
        u, v = zz[:, 0:C_DIM], zz[:, C_DIM:2 * C_DIM]
        xhat, _ = _ln_stats(v)
        vn = (xhat * lng_ref[...] + lnb_ref[...]).astype(BF16)
        for g in range(C_GROUPS):
            cols = slice(g * CHUNK, (g + 1) * CHUNK)
            bias = jnp.concatenate([bs_ref[g], bs_ref[g]], axis=0)
            for r0 in range(0, tm, PAIR):
                sv = _dot(bd[g], vn[r0:r0 + PAIR, cols]) + bias
                sv_ref[r0:r0 + PAIR, cols] = sv.astype(BF16)
                y_ref[r0:r0 + PAIR, cols] = (u[r0:r0 + PAIR, cols] * sv).astype(BF16)
        ho_ref[...] = xv + _dot(y_ref[...], wout_v[...])

    shp = lambda cols, dt: jax.ShapeDtypeStruct((tokens, cols), dt)
    return _pallas(
        body, [h, norm_g, w_in, b_in, ln_g, ln_b, w_s, b_s_rows, w_out], name="fwd_odd", grid=(nt,),
        in_specs=[_row_spec(tm, D_MODEL), _full_spec((1, D_MODEL)), ANY, _full_spec((1, 2 * C_DIM)),
                  _full_spec((1, C_DIM)), _full_spec((1, C_DIM)), _full_spec((C_GROUPS, CHUNK, CHUNK)),
                  _full_spec((C_GROUPS, CHUNK, CHUNK)), ANY],
        out_specs=[_row_spec(tm, D_MODEL), _row_spec(tm, D_MODEL), _row_spec(tm, 2 * C_DIM), _row_spec(tm, 2 * C_DIM),
                   _row_spec(tm, C_DIM), _row_spec(tm, C_DIM)],
        out_shape=[shp(D_MODEL, F32), shp(D_MODEL, BF16), shp(2 * C_DIM, BF16), shp(2 * C_DIM, BF16), shp(C_DIM, BF16),
                   shp(C_DIM, BF16)],
        scratch_shapes=[pltpu.VMEM((N_CHIPS, D_MODEL, cs), BF16), pltpu.VMEM((C_DIM, D_MODEL), BF16),
                        pltpu.VMEM((C_GROUPS, PAIR, PAIR), BF16), pltpu.SemaphoreType.DMA((N_LOADS,))],
        vmem_mib=56, riders=riders)


def _bwd_mlp(dh, h, norm_g, p, w1, w2, layer, *, tm, riders=()):
    tokens = h.shape[0]
    nt = tokens // tm
    fs = D_FF // N_CHIPS

    def body(dh_ref, h_ref, g_ref, p_ref, w1_hbm, w2_hbm, dx_ref, dxb_ref, dp_ref, dg_ref, w1_v, w2_v, sem):
        @pl.when(pl.program_id(0) == 0)
        def _():
            dg_ref[...] = jnp.zeros((1, D_MODEL), F32)

        _load_weights([(w1_hbm, w1_v, False), (w2_hbm, w2_v, False)], sem)

        dhv = dh_ref[...]
        dhb = dhv.astype(BF16)
        dn = jnp.zeros((tm, D_MODEL), F32)
        for j in range(N_CHIPS):
            dq = _dot_nt(dhb, w2_v[j])
            r = jnp.maximum(p_ref[:, j * fs:(j + 1) * fs].astype(F32), 0.0)
            dp = ((2.0 * r) * dq).astype(BF16)
            dp_ref[:, j * fs:(j + 1) * fs] = dp
            dn = dn + _dot_nt(dp, w1_v[j])
        xv = h_ref[...]
        g = g_ref[...]
        _, rstd = _rms_fwd(xv, g)
        dx, dg = _rms_bwd(dn, xv, rstd, g)
        dx_ref[...] = dhv + dx
        dxb_ref[...] = (dhv + dx).astype(BF16)
        dg_ref[...] += dg

    return _pallas(
        body, [dh, h, norm_g, p, w1, w2], name=f"bwd_mlp{layer}", grid=(nt,),
        in_specs=[_row_spec(tm, D_MODEL), _row_spec(tm, D_MODEL), _full_spec((1, D_MODEL)), _row_spec(tm, D_FF), ANY, ANY],
        out_specs=[_row_spec(tm, D_MODEL), _row_spec(tm, D_MODEL), _row_spec(tm, D_FF), _full_spec((1, D_MODEL))],
        out_shape=[jax.ShapeDtypeStruct((tokens, D_MODEL), F32), jax.ShapeDtypeStruct((tokens, D_MODEL), BF16),
                   jax.ShapeDtypeStruct((tokens, D_FF), BF16), jax.ShapeDtypeStruct((1, D_MODEL), F32)],
        scratch_shapes=[pltpu.VMEM((N_CHIPS, D_MODEL, fs), BF16), pltpu.VMEM((N_CHIPS, fs, D_MODEL), BF16),
                        pltpu.SemaphoreType.DMA((N_LOADS,))],
        vmem_mib=56, riders=riders)


def _bwd_odd(dh, h, norm_g, s, cdf, sv, w_in, ln_g, ln_b, w_s, w_out, *, tm, riders=()):
    tokens = h.shape[0]
    nt = tokens // tm
    cs = 2 * C_DIM // N_CHIPS

    def body(dh_ref, h_ref, g_ref, s_ref, cdf_ref, sv_ref, win_hbm, lng_ref, lnb_ref, ws_ref, wout_hbm,
             dx_ref, dxb_ref, ds_ref, dg_ref, dbin_ref, dlng_ref, dlnb_ref, dws_ref, dbs_ref,
             win_v, wout_v, bdt, dws_acc, dbs_acc, dvn, sem):
        i = pl.program_id(0)

        _load_weights([(win_hbm, win_v, False), (wout_hbm, wout_v, True)], sem)

        @pl.when(i == 0)
        def _():
            mask_t = _triu_mask()
            bdt[...] = jnp.zeros(bdt.shape, BF16)
            for g in range(C_GROUPS):
                wt = jnp.where(mask_t, ws_ref[g].T, 0.0).astype(BF16)
                bdt[g, 0:CHUNK, 0:CHUNK] = wt
                bdt[g, CHUNK:PAIR, CHUNK:PAIR] = wt
            dws_acc[...] = jnp.zeros(dws_acc.shape, F32)
            dbs_acc[...] = jnp.zeros(dbs_acc.shape, F32)
            dg_ref[...] = jnp.zeros(dg_ref.shape, F32)
            dbin_ref[...] = jnp.zeros(dbin_ref.shape, F32)
            dlng_ref[...] = jnp.zeros(dlng_ref.shape, F32)
            dlnb_ref[...] = jnp.zeros(dlnb_ref.shape, F32)

        dhv = dh_ref[...]
        dy = _dot_nt(dhv.astype(BF16), wout_v[...])
        sf = s_ref[...].astype(F32)
        cdf = cdf_ref[...].astype(F32)
        pdf = jnp.exp(-0.5 * sf * sf) * 0.3989422804014327
        zz = sf * cdf
        dgelu = cdf + sf * pdf
        u, v = zz[:, 0:C_DIM], zz[:, C_DIM:2 * C_DIM]
        xhat, rs = _ln_stats(v)
        lng = lng_ref[...]
        vn = (xhat * lng + lnb_ref[...]).astype(BF16)
        du = dy * sv_ref[...].astype(F32)
        dsv = dy * u
        dsvb = dsv.astype(BF16)
        for g in range(C_GROUPS):
            cols = slice(g * CHUNK, (g + 1) * CHUNK)
            for r0 in range(0, tm, PAIR):
                blk = dsvb[r0:r0 + PAIR, cols]
                dvn[r0:r0 + PAIR, cols] = _dot(bdt[g], blk)
                dws_acc[g] += _dot_nt(blk, vn[r0:r0 + PAIR, cols])
                dbs_acc[g] += dsv[r0:r0 + CHUNK, cols] + dsv[r0 + CHUNK:r0 + PAIR, cols]
        dv, dlng, dlnb = _ln_bwd(dvn[...], xhat, rs, lng)
        dlng_ref[...] += dlng
        dlnb_ref[...] += dlnb
        ds = jnp.concatenate([du, dv], axis=1) * dgelu
        dbin_ref[...] += jnp.sum(ds, axis=0, keepdims=True)
        dsb = ds.astype(BF16)
        ds_ref[...] = dsb
        dn = jnp.zeros((tm, D_MODEL), F32)
        for j in range(N_CHIPS):
            dn = dn + _dot_nt(dsb[:, j * cs:(j + 1) * cs], win_v[j])
        xv = h_ref[...]
        g = g_ref[...]
        _, rstd = _rms_fwd(xv, g)
        dx, dg = _rms_bwd(dn, xv, rstd, g)
        dx_ref[...] = dhv + dx
        dxb_ref[...] = (dhv + dx).astype(BF16)
        dg_ref[...] += dg

        @pl.when(i == nt - 1)
        def _():
            mask = _tril_mask()
            for g in range(C_GROUPS):
                full = dws_acc[g]
                dws_ref[g] = jnp.where(mask, full[0:CHUNK, 0:CHUNK] + full[CHUNK:PAIR, CHUNK:PAIR], 0.0)
                dbs_ref[g:g + 1, :] = jnp.sum(dbs_acc[g].T, axis=0, keepdims=True)

    row = lambda cols: jax.ShapeDtypeStruct((1, cols), F32)
    return _pallas(
        body, [dh, h, norm_g, s, cdf, sv, w_in, ln_g, ln_b, w_s, w_out], name="bwd_odd", grid=(nt,),
        in_specs=[_row_spec(tm, D_MODEL), _row_spec(tm, D_MODEL), _full_spec((1, D_MODEL)), _row_spec(tm, 2 * C_DIM),
                  _row_spec(tm, 2 * C_DIM), _row_spec(tm, C_DIM), ANY, _full_spec((1, C_DIM)), _full_spec((1, C_DIM)),
                  _full_spec((C_GROUPS, CHUNK, CHUNK)), ANY],
        out_specs=[_row_spec(tm, D_MODEL), _row_spec(tm, D_MODEL), _row_spec(tm, 2 * C_DIM), _full_spec((1, D_MODEL)),
                   _full_spec((1, 2 * C_DIM)),
                   _full_spec((1, C_DIM)), _full_spec((1, C_DIM)), _full_spec((C_GROUPS, CHUNK, CHUNK)),
                   _full_spec((C_GROUPS, CHUNK))],
        out_shape=[jax.ShapeDtypeStruct((tokens, D_MODEL), F32), jax.ShapeDtypeStruct((tokens, D_MODEL), BF16),
                   jax.ShapeDtypeStruct((tokens, 2 * C_DIM), BF16),
                   row(D_MODEL), row(2 * C_DIM), row(C_DIM), row(C_DIM),
                   jax.ShapeDtypeStruct((C_GROUPS, CHUNK, CHUNK), F32), jax.ShapeDtypeStruct((C_GROUPS, CHUNK), F32)],
        scratch_shapes=[pltpu.VMEM((N_CHIPS, D_MODEL, cs), BF16), pltpu.VMEM((C_DIM, D_MODEL), BF16),
                        pltpu.VMEM((C_GROUPS, PAIR, PAIR), BF16), pltpu.VMEM((C_GROUPS, PAIR, PAIR), F32),
                        pltpu.VMEM((C_GROUPS, CHUNK, CHUNK), F32), pltpu.VMEM((tm, C_DIM), F32),
                        pltpu.SemaphoreType.DMA((N_LOADS,))],
        vmem_mib=56, riders=riders)


def _bwd_even(dh, x, norm_g, z, a2, cv, w_in, conv_a_w, ln_g, ln_b, conv_b_w, w_out, *, tm, seq, riders=()):
    tokens = x.shape[0]
    nt, tps = tokens // tm, seq // tm
    ws = IN_EVEN // N_CHIPS

    def body(dh_ref, x_ref, g_ref, z_ref, a2_ref, cv_ref, win_hbm, caw_ref, lng_ref, lnb_ref, cbw_ref, wout_hbm,
             dx_ref, dz_ref, dg_ref, dcaw_ref, dcab_ref, dlng_ref, dlnb_ref, dcbw_ref,
             win_v, wout_v, ea, eb, a1s, da1s, sigs, wide, dw_acc, sem):
        i = pl.program_id(0)

        _load_weights([(win_hbm, win_v, False), (wout_hbm, wout_v, True)], sem)

        @pl.when(i == 0)
        def _():
            dw_acc[...] = jnp.zeros(dw_acc.shape, F32)
            for ref in (dg_ref, dcab_ref, dlng_ref, dlnb_ref, dcbw_ref):
                ref[...] = jnp.zeros(ref.shape, F32)

        last = ((nt - 1 - i) % tps) == tps - 1

        @pl.when(last)
        def _():
            ea[0, tm:tm + A_HALO, :] = jnp.zeros((A_HALO, A_DIM), F32)
            eb[tm:tm + B_HALO, :] = jnp.zeros((B_HALO, B_DIM), F32)

        @pl.when(jnp.logical_not(last))
        def _():
            ea[0, tm:tm + A_HALO, :] = ea[0, 0:A_HALO, :]
            eb[tm:tm + B_HALO, :] = eb[0:B_HALO, :]

        wide[...] = _dot_nt(dh_ref[...].astype(BF16), wout_v[...])
        lng, lnb = lng_ref[...], lnb_ref[...]
        zero_row = jnp.zeros((1, A_DIM), F32)
        dlng, dlnb, dcab = zero_row, zero_row, zero_row
        for r0 in range(0, tm, ELEM_ROWS):
            rows = slice(r0, r0 + ELEM_ROWS)
            a_val, a_gate = z_ref[rows, 0:A_DIM].astype(F32), z_ref[rows, A_DIM:2 * A_DIM].astype(F32)
            xhat, rs = _ln_stats(a2_ref[rows, :])
            a3 = xhat * lng + lnb
            sg = jax.nn.sigmoid(a3)
            da3 = wide[rows, 0:A_DIM] * (sg * (1.0 + a3 * (1.0 - sg)))
            da2, g_part, b_part = _ln_bwd(da3, xhat, rs, lng)
            dlng, dlnb, dcab = dlng + g_part, dlnb + b_part, dcab + jnp.sum(da2, axis=0, keepdims=True)
            ea[0, rows, :] = da2
            eb[rows, :] = wide[rows, A_DIM:A_DIM + B_DIM] * z_ref[rows, 1024:1536].astype(F32)
            sig = jax.nn.sigmoid(a_gate)
            sigs[rows, :] = sig
            a1s[rows, :] = a_val * sig
        dlng_ref[...] += dlng
        dlnb_ref[...] += dlnb
        dcab_ref[...] += dcab
        _fill_shifted(ea, tm + A_HALO)
        for r0 in range(0, tm, CONV_ROWS):
            acc = jnp.zeros((CONV_ROWS, A_DIM), F32)
            for j in range(A_CONV_WIDTH):
                acc = acc + caw_ref[A_CONV_WIDTH - 1 - j:A_CONV_WIDTH - j, :] * _window(ea, r0 + j, CONV_ROWS)
            da1s[r0:r0 + CONV_ROWS, :] = acc
        for j0 in range(0, A_CONV_WIDTH, DW_TAPS):
            taps = range(j0, min(j0 + DW_TAPS, A_CONV_WIDTH))
            part = [jnp.zeros((CONV_ROWS, A_DIM), F32) for _ in taps]
            for r0 in range(0, tm, CONV_ROWS):
                a1c = a1s[r0:r0 + CONV_ROWS, :]
                for u, j in enumerate(taps):
                    part[u] = part[u] + _window(ea, r0 + j, CONV_ROWS) * a1c
            for u, j in enumerate(taps):
                dw_acc[A_CONV_WIDTH - 1 - j] += part[u]
        dcbw = [jnp.zeros((1, B_DIM), F32) for _ in range(B_CONV_WIDTH)]
        for r0 in range(0, tm, ELEM_ROWS):
            rows = slice(r0, r0 + ELEM_ROWS)
            da1, sig = da1s[rows, :], sigs[rows, :]
            dz_ref[rows, 0:A_DIM] = (da1 * sig).astype(BF16)
            dz_ref[rows, A_DIM:2 * A_DIM] = (da1 * z_ref[rows, 0:A_DIM].astype(F32) * (sig * (1.0 - sig))).astype(BF16)
            c_gate, b_val = z_ref[rows, 1536:2048].astype(F32), z_ref[rows, 2048:2560].astype(F32)
            dz_ref[rows, 1024:1536] = (wide[rows, A_DIM:A_DIM + B_DIM] * cv_ref[rows, :].astype(F32)).astype(BF16)
            cb = c_gate * b_val
            dcb = jnp.zeros((ELEM_ROWS, B_DIM), F32)
            for j in range(B_CONV_WIDTH):
                k = B_CONV_WIDTH - 1 - j
                sl = eb[r0 + j:r0 + j + ELEM_ROWS, :]
                dcb = dcb + cbw_ref[k:k + 1, :] * sl
                dcbw[k] = dcbw[k] + jnp.sum(sl * cb, axis=0, keepdims=True)
            dz_ref[rows, 1536:2048] = (dcb * b_val).astype(BF16)
            dz_ref[rows, 2048:2560] = (dcb * c_gate).astype(BF16)
        for k in range(B_CONV_WIDTH):
            dcbw_ref[k:k + 1, :] += dcbw[k]
        dn = jnp.zeros((tm, D_MODEL), F32)
        for j in range(N_CHIPS):
            dn = dn + _dot_nt(dz_ref[:, j * ws:(j + 1) * ws], win_v[j])
        wide[...] = dn
        g = g_ref[...]
        dg = jnp.zeros((1, D_MODEL), F32)
        for r0 in range(0, tm, ELEM_ROWS):
            rows = slice(r0, r0 + ELEM_ROWS)
            xv = x_ref[rows, :]
            _, rstd = _rms_fwd(xv, g)
            dx, dg_part = _rms_bwd(wide[rows, :], xv, rstd, g)
            dx_ref[rows, :] = dh_ref[rows, :] + dx
            dg = dg + dg_part
        dg_ref[...] += dg

        @pl.when(i == nt - 1)
        def _():
            for k in range(A_CONV_WIDTH):
                dcaw_ref[k:k + 1, :] = jnp.sum(dw_acc[k], axis=0, keepdims=True)

    row = lambda cols: jax.ShapeDtypeStruct((1, cols), F32)
    rs_ = functools.partial(_row_spec, rev_nt=nt)
    return _pallas(
        body, [dh, x, norm_g, z, a2, cv, w_in, conv_a_w, ln_g, ln_b, conv_b_w, w_out], name="bwd_even", grid=(nt,),
        in_specs=[rs_(tm, D_MODEL), rs_(tm, D_MODEL), _full_spec((1, D_MODEL)), rs_(tm, IN_EVEN), rs_(tm, A_DIM),
                  rs_(tm, B_DIM), ANY, _full_spec((A_CONV_WIDTH, A_DIM)), _full_spec((1, A_DIM)), _full_spec((1, A_DIM)),
                  _full_spec((B_CONV_WIDTH, B_DIM)), ANY],
        out_specs=[rs_(tm, D_MODEL), rs_(tm, IN_EVEN), _full_spec((1, D_MODEL)), _full_spec((A_CONV_WIDTH, A_DIM)),
                   _full_spec((1, A_DIM)), _full_spec((1, A_DIM)), _full_spec((1, A_DIM)), _full_spec((B_CONV_WIDTH, B_DIM))],
        out_shape=[jax.ShapeDtypeStruct((tokens, D_MODEL), F32), jax.ShapeDtypeStruct((tokens, IN_EVEN), BF16),
                   row(D_MODEL), jax.ShapeDtypeStruct((A_CONV_WIDTH, A_DIM), F32), row(A_DIM), row(A_DIM), row(A_DIM),
                   jax.ShapeDtypeStruct((B_CONV_WIDTH, B_DIM), F32)],
        scratch_shapes=[pltpu.VMEM((N_CHIPS, D_MODEL, ws), BF16), pltpu.VMEM((D_MODEL, D_MODEL), BF16),
                        pltpu.VMEM((SUBLANES, tm + A_HALO, A_DIM), F32), pltpu.VMEM((tm + B_HALO, B_DIM), F32),
                        pltpu.VMEM((tm, A_DIM), F32), pltpu.VMEM((tm, A_DIM), F32), pltpu.VMEM((tm, A_DIM), F32),
                        pltpu.VMEM((tm, D_MODEL), F32),
                        pltpu.VMEM((A_CONV_WIDTH, CONV_ROWS, A_DIM), F32), pltpu.SemaphoreType.DMA((N_LOADS,))],
        vmem_mib=56, riders=riders)


def _wgrad(a, b, name, *, col_shards, riders=()):
    tokens, m = a.shape
    n = b.shape[1]
    kc = 512
    if col_shards:
        bm, bn = m // 2, n // N_CHIPS
        grid = (2, N_CHIPS)
        out_spec = pl.BlockSpec((None, None, bm, bn), lambda i, j: (j, i, 0, 0))
    elif m // 8 >= MXU_ROWS:
        bm, bn = m // 8, n
        grid = (8, 1)
        out_spec = pl.BlockSpec((None, None, bm, bn), lambda i, j: (i // 2, i % 2, 0, 0))
    else:
        bm, bn = m // N_CHIPS, n
        grid = (N_CHIPS, 1)
        out_spec = pl.BlockSpec((None, 2, bm // 2, bn), lambda i, j: (i, 0, 0, 0))

    def body(a_ref, b_ref, o_ref):
        acc = jnp.zeros((bm, bn), F32)
        for k0 in range(0, tokens, kc):
            acc = acc + _dot_tn(a_ref[k0:k0 + kc, :].astype(BF16), b_ref[k0:k0 + kc, :].astype(BF16))
        if len(o_ref.shape) == 3:
            o_ref[0] = acc[0:bm // 2]
            o_ref[1] = acc[bm // 2:bm]
        else:
            o_ref[...] = acc

    out_rows = m // 2 if col_shards else m // 8
    outs, routs = _pallas(
        body, [a, b], name=name, grid=grid,
        in_specs=[pl.BlockSpec((tokens, bm), lambda i, j: (0, i)), pl.BlockSpec((tokens, bn), lambda i, j: (0, j))],
        out_specs=[out_spec], out_shape=[jax.ShapeDtypeStruct((N_CHIPS, 2, out_rows, bn), F32)],
        vmem_mib=56, riders=riders)
    return outs[0], routs


def _wgrad_pair(a, b, name, *, col_shards, riders=()):
    tokens, m = a.shape
    n = b.shape[1]
    kc = 512
    c0 = lax.axis_index("c")

    def half(ph, pre):
        return (ph + 1 + pre[0]) % 2

    if col_shards:
        bm, bn = m // 2, n // N_CHIPS
        a_spec = pl.BlockSpec((tokens, bm), lambda ph, q, pre: (0, half(ph, pre)))
        b_spec = pl.BlockSpec((tokens, bn), lambda ph, q, pre: (0, q))
    else:
        bm, bn = m // 8, n
        a_spec = pl.BlockSpec((tokens, bm), lambda ph, q, pre: (0, 2 * q + half(ph, pre)))
        b_spec = pl.BlockSpec((tokens, bn), lambda ph, q, pre: (0, 0))

    def body(pre_ref, a_ref, b_ref, o_ref, give, got, send_sems, recv_sems):
        ph, q = pl.program_id(0), pl.program_id(1)
        acc = jnp.zeros((bm, bn), F32)
        for k0 in range(0, tokens, kc):
            acc = acc + _dot_tn(a_ref[k0:k0 + kc, :].astype(BF16), b_ref[k0:k0 + kc, :].astype(BF16))
        x, y, cc = _mesh_pos()

        def tile(t):
            return _remote(give.at[t], got.at[t], send_sems.at[t], recv_sems.at[t], (x, y, 1 - cc))

        @pl.when(ph == 0)
        def _():
            give[q] = acc
            tile(q).start()

        @pl.when(ph == 1)
        def _():
            tile(q).wait_recv()
            o_ref[...] = (acc + got[q]).astype(BF16)

        @pl.when((ph == 1) & (q == N_CHIPS - 1))
        def _():
            for t in range(N_CHIPS):
                tile(t).wait_send()

    outs, routs = _pallas(
        body, [a, b], name=name, grid=(2, N_CHIPS), in_specs=[a_spec, b_spec],
        out_specs=[pl.BlockSpec((None, bm, bn), lambda ph, q, pre: (ph * q, 0, 0))],
        out_shape=[jax.ShapeDtypeStruct((N_CHIPS, bm, bn), BF16)],
        scratch_shapes=[pltpu.VMEM((N_CHIPS, bm, bn), F32), pltpu.VMEM((N_CHIPS, bm, bn), F32),
                        pltpu.SemaphoreType.DMA((N_CHIPS,)), pltpu.SemaphoreType.DMA((N_CHIPS,))],
        vmem_mib=56, riders=riders, prefetch=jnp.reshape(c0, (1,)).astype(jnp.int32))
    return outs[0], routs


class _GradReduce:
    def __init__(self, name, grad=None, chip_sum=None):
        self.name, self.grad, self.chip_sum = name, grad, chip_sum
        self.full = None

    def pair_swap(self):
        return _PairSwap([self.grad])

    def took_pair(self, outs):
        self.chip_sum = _in_hbm(_add_pair(self.grad, outs[0], f"pair_sum_{self.name}"))

    def took_chips(self, outs):
        self.full = _in_hbm(_add_chips(self.chip_sum, outs[0], f"chip_sum_{self.name}"))

    def chips_beside(self, collective_id):
        self.took_chips([_chip_swap_beside(self.chip_sum, f"chip_swap_{self.name}", collective_id)])

    def pair_share(self):
        return _PairShare([self.full])

    def took_share(self, outs):
        self.full = outs[0]

    def reduced(self):
        return jnp.reshape(self.full, (2 * self.full.shape[1], self.full.shape[2]))


def _forward_backward(x2, tgt2, w, conv_a_w, conv_b_w, od_norm, od_bias, od_lng, od_lnb,
                      ev_norm_g, ev_conv_a_b, ev_ln_a_g, ev_ln_a_b, od_w_s, od_b_s, mlp_norm_g, final_norm_g,
                      *, tm, seq, distributed=True):
    d = x2.shape[1]
    b_s_rows = jnp.broadcast_to(od_b_s[0][:, :, None], (C_GROUPS, CHUNK, CHUNK))
    (h1, n0, z, a2, cv, mix), _ = _fwd_even(
        x2, ev_norm_g, w["ev_in"], conv_a_w, ev_conv_a_b, ev_ln_a_g, ev_ln_a_b, conv_b_w, w["ev_out"], tm=tm, seq=seq)
    (h2, n1, p0, q0), _ = _fwd_mlp(h1, mlp_norm_g[0:1], w["w1_0"], w["w2_0"], 0, tm=tm)
    (h3, n2, s, cdf, sv, y), _ = _fwd_odd(h2, od_norm, w["od_in"], od_bias, od_lng, od_lnb, od_w_s[0], b_s_rows,
                                          w["od_out"], tm=tm)
    (n3, p1, q1, loss_part, dh4, dh4b, d_final_g), _ = _fwd_mlp(
        h3, mlp_norm_g[1:2], w["w1_1"], w["w2_1"], 1, tm=tm,
        head=(jnp.reshape(final_norm_g, (1, d)), tgt2))

    red = {}

    def swap(*names):
        return [red[nm].pair_swap() for nm in names] if distributed else []

    def share(*names):
        return [red[nm].pair_share() for nm in names] if distributed else []

    def took(routs, *steps):
        if distributed:
            for (nm, what), outs in zip(steps, routs):
                getattr(red[nm], what)(outs)

    swap_ids = iter(range(FIRST_SWAP_ID, FIRST_SWAP_ID + 8))

    def beside(name):
        if distributed:
            red[name].chips_beside(next(swap_ids))

    def big(lhs, rhs, name, col_shards, riders=()):
        if distributed:
            chip_sum, routs = _wgrad_pair(lhs, rhs, f"wgrad_{name}", col_shards=col_shards, riders=riders)
            red[name] = _GradReduce(name, chip_sum=_in_hbm(chip_sum))
        else:
            g, routs = _wgrad(lhs, rhs, f"wgrad_{name}", col_shards=col_shards)
            red[name] = _GradReduce(name, grad=g)
        return routs

    big(q1, dh4b, "w2_1", False)
    beside("w2_1")
    (dh3, dh3b, dp1, d_mlp_g1), _ = _bwd_mlp(dh4, h3, mlp_norm_g[1:2], p1, w["w1_1"], w["w2_1"], 1, tm=tm)
    big(n3, dp1, "w1_1", True)
    beside("w1_1")
    g, routs = _wgrad(y, dh3b, "wgrad_od_out", col_shards=False, riders=share("w2_1"))
    red["od_out"] = _GradReduce("od_out", grad=g)
    took(routs, ("w2_1", "took_share"))
    (dh2, dh2b, ds, d_od_norm, d_od_bin, d_od_lng, d_od_lnb, d_ws, d_bs), _ = _bwd_odd(
        dh3, h2, od_norm, s, cdf, sv, w["od_in"], od_lng, od_lnb, od_w_s[0], w["od_out"], tm=tm)
    routs = big(n2, ds, "od_in", True, riders=share("w1_1") + swap("od_out"))
    took(routs, ("w1_1", "took_share"), ("od_out", "took_pair"))
    beside("od_in")
    beside("od_out")
    half_groups = C_GROUPS // 2
    early = {"loss": loss_part, "od_w_s_lo": d_ws[:half_groups], "od_b_s": d_bs, "mlp_norm_g1": d_mlp_g1, "final_norm_g": d_final_g,
             "od_norm_g": d_od_norm, "od_b_in": d_od_bin, "od_ln_v_g": d_od_lng, "od_ln_v_b": d_od_lnb}
    share_early = [_ShareAll(list(early.values()))] if distributed else []
    routs = big(q0, dh2b, "w2_0", False, riders=share_early)
    landed_early = routs[0] if distributed else []
    beside("w2_0")
    (dh1, dh1b, dp0, d_mlp_g0), _ = _bwd_mlp(dh2, h1, mlp_norm_g[0:1], p0, w["w1_0"], w["w2_0"], 0, tm=tm)
    middle = {"od_w_s_hi": d_ws[half_groups:]}
    share_middle = [_ShareAll(list(middle.values()))] if distributed else []
    g, _ = _wgrad(mix, dh1b, "wgrad_ev_out", col_shards=False)
    red["ev_out"] = _GradReduce("ev_out", grad=g)
    routs = big(n1, dp0, "w1_0", True,
                riders=share("od_out") + share("od_in") + share("w2_0") + swap("ev_out") + share_middle)
    took(routs, ("od_out", "took_share"), ("od_in", "took_share"), ("w2_0", "took_share"), ("ev_out", "took_pair"))
    landed_middle = routs[4] if distributed else []
    beside("w1_0")
    beside("ev_out")

    (dx, dz, d_ev_norm, d_caw, d_cab, d_ev_lng, d_ev_lnb, d_cbw), _ = _bwd_even(
        dh1, x2, ev_norm_g, z, a2, cv, w["ev_in"], conv_a_w, ev_ln_a_g, ev_ln_a_b, conv_b_w, w["ev_out"], tm=tm, seq=seq)
    late = {"mlp_norm_g0": d_mlp_g0, "ev_norm_g": d_ev_norm, "ev_conv_a_b": d_cab, "ev_ln_a_g": d_ev_lng,
            "ev_ln_a_b": d_ev_lnb, "ev_conv_a_w": d_caw, "ev_conv_b_w": d_cbw}
    share_late = [_ShareAll(list(late.values()))] if distributed else []
    routs = big(n0, dz, "ev_in", True, riders=share("ev_out") + share("w1_0") + share_late)
    took(routs, ("ev_out", "took_share"), ("w1_0", "took_share"))
    beside("ev_in")
    own = {**early, **middle, **late}
    landed = dict(zip(own.keys(), landed_early + landed_middle + routs[2])) if distributed else None
    return dx, red, own, landed


def _rows128(a):
    rows = jnp.reshape(a, (-1, LANES))
    pad = (-rows.shape[0]) % SUBLANES
    return jnp.pad(rows, ((0, pad), (0, 0))) if pad else rows


def _pack(arrays):
    return jnp.concatenate([_rows128(a) for a in arrays], axis=0)


def _unpack(buf, shapes):
    out, r0 = [], 0
    for shp in shapes:
        size = 1
        for dim in shp:
            size *= dim
        nr = size // LANES
        out.append(jnp.reshape(buf[r0:r0 + nr], shp))
        r0 += nr + (-nr) % SUBLANES
    return out


def kernel(x, ev_norm_g, ev_w_in, ev_conv_a_w, ev_conv_a_b, ev_ln_a_g, ev_ln_a_b, ev_conv_b_w, ev_w_out, od_norm_g, od_w_in, od_b_in, od_ln_v_g, od_ln_v_b, od_w_s, od_b_s, od_w_out, mlp_norm_g, mlp_w1, mlp_w2, final_norm_g, loss_target, m_ev_norm_g, m_ev_w_in, m_ev_conv_a_w, m_ev_conv_a_b, m_ev_ln_a_g, m_ev_ln_a_b, m_ev_conv_b_w, m_ev_w_out, m_od_norm_g, m_od_w_in, m_od_b_in, m_od_ln_v_g, m_od_ln_v_b, m_od_w_s, m_od_b_s, m_od_w_out, m_mlp_norm_g, m_mlp_w1, m_mlp_w2, m_final_norm_g, v_ev_norm_g, v_ev_w_in, v_ev_conv_a_w, v_ev_conv_a_b, v_ev_ln_a_g, v_ev_ln_a_b, v_ev_conv_b_w, v_ev_w_out, v_od_norm_g, v_od_w_in, v_od_b_in, v_od_ln_v_g, v_od_ln_v_b, v_od_w_s, v_od_b_s, v_od_w_out, v_mlp_norm_g, v_mlp_w1, v_mlp_w2, v_final_norm_g):
    tm = TOKEN_TILE
    batch, seq, d = x.shape
    tokens = batch * seq
    x2 = jnp.reshape(x, (tokens, d))
    tgt2 = jnp.reshape(loss_target, (tokens, d))
    chip = 2 * lax.axis_index("x") + lax.axis_index("y")

    small_shapes = [(A_CONV_WIDTH, LANES), (B_CONV_WIDTH, LANES), (256,), (512,), (256,), (256,)]
    small_shard = _pack([ev_conv_a_w[0], ev_conv_b_w[0], od_norm_g[0], od_b_in[0], od_ln_v_g[0], od_ln_v_b[0]])
    small_shard = jnp.pad(small_shard, ((0, (-small_shard.shape[0]) % (4 * SUBLANES)), (0, 0)))
    first = [_place_shard(ev_w_in, 0, BF16, "place_ev_w_in"), _place_shard(ev_w_out, 0, BF16, "place_ev_w_out"),
             _place_shard(small_shard[None], 0, F32, "place_small")]
    staged = {
        "w1_0": _place_shard(mlp_w1, 0, BF16, "place_w1_0"), "w2_0": _place_shard(mlp_w2, 0, BF16, "place_w2_0"),
        "od_in": _place_shard(od_w_in, 0, BF16, "place_od_w_in"), "od_out": _place_shard(od_w_out, 0, BF16, "place_od_w_out"),
        "w1_1": _place_shard(mlp_w1, 1, BF16, "place_w1_1"), "w2_1": _place_shard(mlp_w2, 1, BF16, "place_w2_1"),
    }
    first = [_in_hbm(a) for a in first]
    staged = {nm: _in_hbm(a) for nm, a in staged.items()}
    g_ev_in, g_ev_out, g_small = _gather_beside(first, "gather_stage0", collective_id=1)
    gathered = {"ev_in": g_ev_in, "ev_out": g_ev_out}
    for stage, names in enumerate((("w1_0", "w2_0"), ("od_in", "od_out", "w1_1"), ("w2_1",))):
        done = _gather_beside([staged[nm] for nm in names], f"gather_stage{stage + 1}", collective_id=stage + 2)
        gathered.update(zip(names, done))
    small_all = jnp.reshape(_plain_copy(g_small, "small_weights_copy"), (N_CHIPS, -1, LANES))
    per_chip = [_unpack(small_all[q], small_shapes) for q in range(N_CHIPS)]
    conv_a_w = jnp.concatenate([pc[0] for pc in per_chip], axis=1)
    conv_b_w = jnp.concatenate([pc[1] for pc in per_chip], axis=1)
    od_norm = jnp.concatenate([pc[2] for pc in per_chip])[None, :]
    od_bias = jnp.concatenate([pc[3] for pc in per_chip])[None, :]
    od_lng = jnp.concatenate([pc[4] for pc in per_chip])[None, :]
    od_lnb = jnp.concatenate([pc[5] for pc in per_chip])[None, :]

    dx, red, own, landed = _forward_backward(
        x2, tgt2, gathered, conv_a_w, conv_b_w, od_norm, od_bias, od_lng, od_lnb,
        ev_norm_g, ev_conv_a_b, ev_ln_a_g, ev_ln_a_b, od_w_s, od_b_s, mlp_norm_g, final_norm_g, tm=tm, seq=seq)

    routs = _exchange([red["ev_in"].pair_share()], "reduce_tail")
    red["ev_in"].took_share(routs[0])

    given = {"ev_norm_g": (ev_norm_g, m_ev_norm_g, v_ev_norm_g), "ev_conv_a_b": (ev_conv_a_b, m_ev_conv_a_b, v_ev_conv_a_b),
             "ev_ln_a_g": (ev_ln_a_g, m_ev_ln_a_g, v_ev_ln_a_g), "ev_ln_a_b": (ev_ln_a_b, m_ev_ln_a_b, v_ev_ln_a_b),
             "od_w_s": (od_w_s, m_od_w_s, v_od_w_s), "od_b_s": (od_b_s, m_od_b_s, v_od_b_s),
             "mlp_norm_g": (mlp_norm_g, m_mlp_norm_g, v_mlp_norm_g), "final_norm_g": (final_norm_g, m_final_norm_g, v_final_norm_g),
             "ev_conv_a_w": (ev_conv_a_w, m_ev_conv_a_w, v_ev_conv_a_w), "ev_conv_b_w": (ev_conv_b_w, m_ev_conv_b_w, v_ev_conv_b_w),
             "od_norm_g": (od_norm_g, m_od_norm_g, v_od_norm_g), "od_b_in": (od_b_in, m_od_b_in, v_od_b_in),
             "od_ln_v_g": (od_ln_v_g, m_od_ln_v_g, v_od_ln_v_g), "od_ln_v_b": (od_ln_v_b, m_od_ln_v_b, v_od_ln_v_b)}
    shaped = {nm: tuple(jnp.reshape(a, shape) for a in given[nm]) for nm, shape, _, _ in SMALL_WEIGHTS}
    loss11, small_upd = _small_update(own, landed, shaped)
    loss = loss11[0, 0]
    upd = {nm: [jnp.reshape(o, given[nm][0].shape) for o in outs] for nm, outs in small_upd.items()}

    def big_update(wt, m, v, names, call):
        grads = [red[nm].reduced() for nm in names]
        shp3 = (len(grads),) + grads[0].shape
        outs = _adamw(jnp.reshape(wt, shp3), jnp.reshape(m, shp3), jnp.reshape(v, shp3), grads, call)
        return [jnp.reshape(o, wt.shape) for o in outs]

    wide = {"mlp_w2": (mlp_w2, m_mlp_w2, v_mlp_w2, ["w2_0", "w2_1"]), "mlp_w1": (mlp_w1, m_mlp_w1, v_mlp_w1, ["w1_0", "w1_1"]),
            "ev_w_out": (ev_w_out, m_ev_w_out, v_ev_w_out, ["ev_out"]), "od_w_out": (od_w_out, m_od_w_out, v_od_w_out, ["od_out"])}
    streamed = []
    for wt, m, v, names in wide.values():
        grads = [red[nm].reduced() for nm in names]
        shp3 = (len(grads),) + grads[0].shape
        streamed.append((jnp.reshape(wt, shp3), jnp.reshape(m, shp3), jnp.reshape(v, shp3), grads))
    for (nm, (wt, _, _, _)), outs in zip(wide.items(), _adamw_stream(streamed, "adamw_wide")):
        upd[nm] = [jnp.reshape(o, wt.shape) for o in outs]
    upd["ev_w_in"] = big_update(ev_w_in, m_ev_w_in, v_ev_w_in, ["ev_in"], "adamw_ev_w_in")
    upd["od_w_in"] = big_update(od_w_in, m_od_w_in, v_od_w_in, ["od_in"], "adamw_od_w_in")

    order = ["ev_norm_g", "ev_w_in", "ev_conv_a_w", "ev_conv_a_b", "ev_ln_a_g", "ev_ln_a_b", "ev_conv_b_w", "ev_w_out",
             "od_norm_g", "od_w_in", "od_b_in", "od_ln_v_g", "od_ln_v_b", "od_w_s", "od_b_s", "od_w_out", "mlp_norm_g",
             "mlp_w1", "mlp_w2", "final_norm_g"]
    grad_x = jnp.reshape(dx, x.shape)
    return (loss, grad_x, *[upd[nm][0] for nm in order], *[upd[nm][1] for nm in order],
            *[upd[nm][2] for nm in order], *[upd[nm][3] for nm in order])
```
